```python
import math
import jax, jax.numpy as jnp
from jax import lax
import numpy as np

D_MODEL = 1024
BATCH = 8
SEQ = 2048
DEPTH = 4

N_MIXERS = 2
N_GDN_LAYERS = (DEPTH + N_MIXERS - 1) // N_MIXERS
N_MLA_LAYERS = DEPTH // N_MIXERS

GDN_HEADS = 8
GDN_HEAD_DIM = 128
GDN_KEY_DIM = GDN_HEADS * GDN_HEAD_DIM
GDN_VALUE_DIM = GDN_HEADS * GDN_HEAD_DIM
GDN_CONV = 4
GDN_CHUNK = 64
GDN_IN = 2 * GDN_KEY_DIM + 2 * GDN_VALUE_DIM + 2 * GDN_HEADS

MLA_HEADS = 8
MLA_NOPE = 128
MLA_ROPE = 64
MLA_V = 128
MLA_Q_RANK = 384
MLA_KV_RANK = 256
MLA_IN = MLA_Q_RANK + MLA_KV_RANK + MLA_ROPE
ROPE_THETA = 10000.0
Q_BLOCK = 128

D_FF = ((8 * D_MODEL + 3 * 256 - 1) // (3 * 256)) * 256
N_MOD = 6
EPS = 1e-6

kernel_name = "hybrid_gdn_mla_adaln_trunk"


def rmsnorm(x, g):
    xf = x.astype(jnp.float32)
    y = xf * lax.rsqrt(jnp.mean(xf * xf, axis=-1, keepdims=True) + EPS)
    return (y * g.astype(jnp.float32)).astype(x.dtype)


def l2norm(x):
    return x * lax.rsqrt(jnp.sum(x * x, axis=-1, keepdims=True) + EPS)


def causal_depthwise_conv(x, w):
    K = w.shape[-1]
    kern = jnp.transpose(w)[:, None, :].astype(x.dtype)
    return lax.conv_general_dilated(
        x, kern, window_strides=(1,), padding=[(K - 1, 0)],
        dimension_numbers=("NWC", "WIO", "NWC"), feature_group_count=x.shape[-1])


def chunk_gated_delta_rule(q, k, v, g, beta):
    B, H, T, Dk = q.shape
    Dv = v.shape[-1]
    C = GDN_CHUNK
    N = T // C
    q = q.reshape(B, H, N, C, Dk)
    k = k.reshape(B, H, N, C, Dk)
    v = v.reshape(B, H, N, C, Dv)
    g_cum = jnp.cumsum(g.reshape(B, H, N, C), axis=-1)
    beta = beta.reshape(B, H, N, C)

    tril = jnp.tril(jnp.ones((C, C), dtype=bool))
    strict = jnp.tril(jnp.ones((C, C), dtype=bool), k=-1)
    diff = g_cum[..., :, None] - g_cum[..., None, :]
    decay = jnp.exp(jnp.where(tril, diff, -jnp.inf))

    kb = k * beta[..., None]
    L = jnp.where(strict, jnp.einsum("bhnid,bhnjd->bhnij", kb, k) * decay, 0.0)
    A = L + jnp.eye(C, dtype=L.dtype)
    u = lax.linalg.triangular_solve(A, v * beta[..., None], left_side=True, lower=True,
                                    unit_diagonal=True)
    w = lax.linalg.triangular_solve(A, kb * jnp.exp(g_cum)[..., None], left_side=True,
                                    lower=True, unit_diagonal=True)
    attn = jnp.einsum("bhnid,bhnjd->bhnij", q, k) * decay
    q_dec = q * jnp.exp(g_cum)[..., None]
    k_dec = k * jnp.exp(g_cum[..., -1:] - g_cum)[..., None]
    g_last = jnp.exp(g_cum[..., -1])

    def step(S, xs):
        q_i, k_i, w_i, u_i, a_i, gl_i = xs
        v_new = u_i - jnp.einsum("bhck,bhkv->bhcv", w_i, S)
        o_i = jnp.einsum("bhck,bhkv->bhcv", q_i, S) + jnp.einsum("bhij,bhjv->bhiv", a_i, v_new)
        S = S * gl_i[..., None, None] + jnp.einsum("bhck,bhcv->bhkv", k_i, v_new)
        return S, o_i

    xs = tuple(jnp.moveaxis(t, 2, 0) for t in (q_dec, k_dec, w, u, attn, g_last))
    S0 = jnp.zeros((B, H, Dk, Dv), dtype=jnp.float32)
    _, o = lax.scan(step, S0, xs)
    return jnp.moveaxis(o, 0, 2).reshape(B, H, T, Dv)


def gdn_mixer(h, w_in, conv_w, a_log, dt_bias, norm_g, w_out):
    B, T, _ = h.shape
    proj = h @ w_in
    qkv = proj[..., :2 * GDN_KEY_DIM + GDN_VALUE_DIM]
    o0 = 2 * GDN_KEY_DIM + GDN_VALUE_DIM
    gate = proj[..., o0:o0 + GDN_VALUE_DIM]
    a_in = proj[..., o0 + GDN_VALUE_DIM:o0 + GDN_VALUE_DIM + GDN_HEADS]
    b_in = proj[..., o0 + GDN_VALUE_DIM + GDN_HEADS:]

    qkv = jax.nn.silu(causal_depthwise_conv(qkv, conv_w)).astype(jnp.float32)
    q = qkv[..., :GDN_KEY_DIM].reshape(B, T, GDN_HEADS, GDN_HEAD_DIM)
    k = qkv[..., GDN_KEY_DIM:2 * GDN_KEY_DIM].reshape(B, T, GDN_HEADS, GDN_HEAD_DIM)
    v = qkv[..., 2 * GDN_KEY_DIM:].reshape(B, T, GDN_HEADS, GDN_HEAD_DIM)
    q = l2norm(q) * (GDN_HEAD_DIM ** -0.5)
    k = l2norm(k)
    beta = jax.nn.sigmoid(b_in.astype(jnp.float32))
    g = -jnp.exp(a_log.astype(jnp.float32)) * jax.nn.softplus(
        a_in.astype(jnp.float32) + dt_bias.astype(jnp.float32))

    to_bhtd = lambda t: jnp.transpose(t, (0, 2, 1, 3))
    o = chunk_gated_delta_rule(to_bhtd(q), to_bhtd(k), to_bhtd(v),
                               jnp.transpose(g, (0, 2, 1)), jnp.transpose(beta, (0, 2, 1)))
    o = jnp.transpose(o, (0, 2, 1, 3)).astype(h.dtype)
    o = rmsnorm(o, norm_g) * jax.nn.silu(gate.reshape(B, T, GDN_HEADS, GDN_HEAD_DIM))
    return o.reshape(B, T, GDN_VALUE_DIM) @ w_out


def apply_rope(x, cos, sin):
    half = x.shape[-1] // 2
    x1, x2 = x[..., :half], x[..., half:]
    return jnp.concatenate([x1 * cos - x2 * sin, x2 * cos + x1 * sin], axis=-1)


def causal_mla_attention(q_nope, q_rope, k_nope, k_rope, v):
    B, T, H, _ = q_nope.shape
    nb = T // Q_BLOCK
    scale = (MLA_NOPE + MLA_ROPE) ** -0.5
    qn = jnp.moveaxis(q_nope.reshape(B, nb, Q_BLOCK, H, MLA_NOPE), 1, 0)
    qr = jnp.moveaxis(q_rope.reshape(B, nb, Q_BLOCK, H, MLA_ROPE), 1, 0)
    kpos = jnp.arange(T)

    def block(args):
        qn_b, qr_b, start = args
        s = (jnp.einsum("bqhd,bkhd->bhqk", qn_b, k_nope)
             + jnp.einsum("bqhd,bkd->bhqk", qr_b, k_rope)).astype(jnp.float32) * scale
        qpos = start + jnp.arange(Q_BLOCK)
        s = jnp.where(kpos[None, :] <= qpos[:, None], s, -jnp.inf)
        p = jax.nn.softmax(s, axis=-1).astype(v.dtype)
        return jnp.einsum("bhqk,bkhd->bqhd", p, v)

    o = lax.map(block, (qn, qr, jnp.arange(nb) * Q_BLOCK))
    return jnp.moveaxis(o, 0, 1).reshape(B, T, H, MLA_V)


def mla_mixer(h, cos, sin, w_in, q_norm_g, kv_norm_g, w_uq, w_ukv, w_out):
    B, T, _ = h.shape
    proj = h @ w_in
    c_q = proj[..., :MLA_Q_RANK]
    c_kv = proj[..., MLA_Q_RANK:MLA_Q_RANK + MLA_KV_RANK]
    k_rope = proj[..., MLA_Q_RANK + MLA_KV_RANK:]
    q = (rmsnorm(c_q, q_norm_g) @ w_uq).reshape(B, T, MLA_HEADS, MLA_NOPE + MLA_ROPE)
    kv = (rmsnorm(c_kv, kv_norm_g) @ w_ukv).reshape(B, T, MLA_HEADS, MLA_NOPE + MLA_V)
    q_nope, q_rope = q[..., :MLA_NOPE], q[..., MLA_NOPE:]
    k_nope, v = kv[..., :MLA_NOPE], kv[..., MLA_NOPE:]
    q_rope = apply_rope(q_rope, cos[:, :, None, :], sin[:, :, None, :])
    k_rope = apply_rope(k_rope, cos, sin)
    o = causal_mla_attention(q_nope, q_rope, k_nope, k_rope, v)
    return o.reshape(B, T, MLA_HEADS * MLA_V) @ w_out


def swiglu(h, w_gate, w_up, w_down):
    return (jax.nn.silu(h @ w_gate) * (h @ w_up)) @ w_down


def _fwd_setup_inputs(seed: int = 0) -> dict:
    key = jax.random.key(seed)
    ks = jax.random.split(key, 24)
    f32 = jnp.float32
    nrm = lambda k, shape, s: jax.random.normal(k, shape, f32) * s
    x = jax.random.normal(ks[0], (BATCH, SEQ, D_MODEL), f32)
    c = jax.random.normal(ks[1], (BATCH, D_MODEL), f32)
    positions = (jnp.arange(SEQ, dtype=jnp.int32)[None, :]
                 + jax.random.randint(ks[2], (BATCH, 1), 0, 1024, dtype=jnp.int32))
    ada_w = nrm(ks[3], (DEPTH, D_MODEL, N_MOD * D_MODEL), 0.5 * D_MODEL ** -0.5)
    ada_b = nrm(ks[4], (DEPTH, N_MOD * D_MODEL), 0.02)
    norm_mix_g = 1.0 + nrm(ks[5], (DEPTH, D_MODEL), 0.05)
    norm_ffn_g = 1.0 + nrm(ks[6], (DEPTH, D_MODEL), 0.05)

    gdn_w_in = nrm(ks[7], (N_GDN_LAYERS, D_MODEL, GDN_IN), D_MODEL ** -0.5)
    gdn_conv_w = nrm(ks[8], (N_GDN_LAYERS, 2 * GDN_KEY_DIM + GDN_VALUE_DIM, GDN_CONV),
                     GDN_CONV ** -0.5)
    gdn_a_log = jnp.log(jax.random.uniform(ks[9], (N_GDN_LAYERS, GDN_HEADS), f32, 1.0, 16.0))
    dt = jnp.exp(jax.random.uniform(ks[10], (N_GDN_LAYERS, GDN_HEADS), f32,
                                    math.log(1e-3), math.log(1e-1)))
    gdn_dt_bias = dt + jnp.log(-jnp.expm1(-dt))
    gdn_norm_g = 1.0 + nrm(ks[11], (N_GDN_LAYERS, GDN_HEAD_DIM), 0.05)
    gdn_w_out = nrm(ks[12], (N_GDN_LAYERS, GDN_VALUE_DIM, D_MODEL), GDN_VALUE_DIM ** -0.5)

    mla_w_in = nrm(ks[13], (N_MLA_LAYERS, D_MODEL, MLA_IN), D_MODEL ** -0.5)
    mla_q_norm_g = 1.0 + nrm(ks[14], (N_MLA_LAYERS, MLA_Q_RANK), 0.05)
    mla_kv_norm_g = 1.0 + nrm(ks[15], (N_MLA_LAYERS, MLA_KV_RANK), 0.05)
    mla_w_uq = nrm(ks[16], (N_MLA_LAYERS, MLA_Q_RANK, MLA_HEADS * (MLA_NOPE + MLA_ROPE)),
                   MLA_Q_RANK ** -0.5)
    mla_w_ukv = nrm(ks[17], (N_MLA_LAYERS, MLA_KV_RANK, MLA_HEADS * (MLA_NOPE + MLA_V)),
                    MLA_KV_RANK ** -0.5)
    mla_w_out = nrm(ks[18], (N_MLA_LAYERS, MLA_HEADS * MLA_V, D_MODEL),
                    (MLA_HEADS * MLA_V) ** -0.5)

    ffn_w_gate = nrm(ks[19], (DEPTH, D_MODEL, D_FF), D_MODEL ** -0.5)
    ffn_w_up = nrm(ks[20], (DEPTH, D_MODEL, D_FF), D_MODEL ** -0.5)
    ffn_w_down = nrm(ks[21], (DEPTH, D_FF, D_MODEL), D_FF ** -0.5)
    final_norm_g = 1.0 + nrm(ks[22], (D_MODEL,), 0.05)
    return {
        "x": x, "c": c, "positions": positions,
        "ada_w": ada_w, "ada_b": ada_b, "norm_mix_g": norm_mix_g, "norm_ffn_g": norm_ffn_g,
        "gdn_w_in": gdn_w_in, "gdn_conv_w": gdn_conv_w, "gdn_a_log": gdn_a_log,
        "gdn_dt_bias": gdn_dt_bias, "gdn_norm_g": gdn_norm_g, "gdn_w_out": gdn_w_out,
        "mla_w_in": mla_w_in, "mla_q_norm_g": mla_q_norm_g, "mla_kv_norm_g": mla_kv_norm_g,
        "mla_w_uq": mla_w_uq, "mla_w_ukv": mla_w_ukv, "mla_w_out": mla_w_out,
        "ffn_w_gate": ffn_w_gate, "ffn_w_up": ffn_w_up, "ffn_w_down": ffn_w_down,
        "final_norm_g": final_norm_g,
    }


def _fwd_reference(x, c, positions, ada_w, ada_b, norm_mix_g, norm_ffn_g,
              gdn_w_in, gdn_conv_w, gdn_a_log, gdn_dt_bias, gdn_norm_g, gdn_w_out,
              mla_w_in, mla_q_norm_g, mla_kv_norm_g, mla_w_uq, mla_w_ukv, mla_w_out,
              ffn_w_gate, ffn_w_up, ffn_w_down, final_norm_g):
    inv_freq = ROPE_THETA ** (-jnp.arange(0, MLA_ROPE, 2, dtype=jnp.float32) / MLA_ROPE)
    ang = positions.astype(jnp.float32)[..., None] * inv_freq
    cos = jnp.cos(ang).astype(x.dtype)
    sin = jnp.sin(ang).astype(x.dtype)
    c_act = jax.nn.silu(c)

    for layer in range(DEPTH):
        mod = c_act @ ada_w[layer] + ada_b[layer]
        shift_m, scale_m, gate_m, shift_f, scale_f, gate_f = [
            m[:, None, :] for m in jnp.split(mod, N_MOD, axis=-1)]

        h = rmsnorm(x, norm_mix_g[layer]) * (1.0 + scale_m) + shift_m
        j = layer // N_MIXERS
        if layer % N_MIXERS == 0:
            y = gdn_mixer(h, gdn_w_in[j], gdn_conv_w[j], gdn_a_log[j], gdn_dt_bias[j],
                          gdn_norm_g[j], gdn_w_out[j])
        else:
            y = mla_mixer(h, cos, sin, mla_w_in[j], mla_q_norm_g[j], mla_kv_norm_g[j],
                          mla_w_uq[j], mla_w_ukv[j], mla_w_out[j])
        x = x + gate_m * y

        h = rmsnorm(x, norm_ffn_g[layer]) * (1.0 + scale_f) + shift_f
        x = x + gate_f * swiglu(h, ffn_w_gate[layer], ffn_w_up[layer], ffn_w_down[layer])

    return rmsnorm(x, final_norm_g)


import jax as _jax
import jax.numpy as _jnp

TWIN_FORMAT = 'train_step'
FWD_PARAMS = ['x', 'c', 'positions', 'ada_w', 'ada_b', 'norm_mix_g', 'norm_ffn_g', 'gdn_w_in', 'gdn_conv_w', 'gdn_a_log', 'gdn_dt_bias', 'gdn_norm_g', 'gdn_w_out', 'mla_w_in', 'mla_q_norm_g', 'mla_kv_norm_g', 'mla_w_uq', 'mla_w_ukv', 'mla_w_out', 'ffn_w_gate', 'ffn_w_up', 'ffn_w_down', 'final_norm_g']
TWIN_WEIGHTS = ['ada_w', 'ada_b', 'norm_mix_g', 'norm_ffn_g', 'gdn_w_in', 'gdn_conv_w', 'gdn_a_log', 'gdn_dt_bias', 'gdn_norm_g', 'gdn_w_out', 'mla_w_in', 'mla_q_norm_g', 'mla_kv_norm_g', 'mla_w_uq', 'mla_w_ukv', 'mla_w_out', 'ffn_w_gate', 'ffn_w_up', 'ffn_w_down', 'final_norm_g']
TWIN_DIFF_INPUT = 'x'
TWIN_INPUTS = ['x', 'c', 'positions', 'ada_w', 'ada_b', 'norm_mix_g', 'norm_ffn_g', 'gdn_w_in', 'gdn_conv_w', 'gdn_a_log', 'gdn_dt_bias', 'gdn_norm_g', 'gdn_w_out', 'mla_w_in', 'mla_q_norm_g', 'mla_kv_norm_g', 'mla_w_uq', 'mla_w_ukv', 'mla_w_out', 'ffn_w_gate', 'ffn_w_up', 'ffn_w_down', 'final_norm_g', 'loss_target', 'm_ada_w', 'm_ada_b', 'm_norm_mix_g', 'm_norm_ffn_g', 'm_gdn_w_in', 'm_gdn_conv_w', 'm_gdn_a_log', 'm_gdn_dt_bias', 'm_gdn_norm_g', 'm_gdn_w_out', 'm_mla_w_in', 'm_mla_q_norm_g', 'm_mla_kv_norm_g', 'm_mla_w_uq', 'm_mla_w_ukv', 'm_mla_w_out', 'm_ffn_w_gate', 'm_ffn_w_up', 'm_ffn_w_down', 'm_final_norm_g', 'v_ada_w', 'v_ada_b', 'v_norm_mix_g', 'v_norm_ffn_g', 'v_gdn_w_in', 'v_gdn_conv_w', 'v_gdn_a_log', 'v_gdn_dt_bias', 'v_gdn_norm_g', 'v_gdn_w_out', 'v_mla_w_in', 'v_mla_q_norm_g', 'v_mla_kv_norm_g', 'v_mla_w_uq', 'v_mla_w_ukv', 'v_mla_w_out', 'v_ffn_w_gate', 'v_ffn_w_up', 'v_ffn_w_down', 'v_final_norm_g']
TWIN_OUTPUTS = ['loss', 'grad_x', 'grad_ada_w', 'grad_ada_b', 'grad_norm_mix_g', 'grad_norm_ffn_g', 'grad_gdn_w_in', 'grad_gdn_conv_w', 'grad_gdn_a_log', 'grad_gdn_dt_bias', 'grad_gdn_norm_g', 'grad_gdn_w_out', 'grad_mla_w_in', 'grad_mla_q_norm_g', 'grad_mla_kv_norm_g', 'grad_mla_w_uq', 'grad_mla_w_ukv', 'grad_mla_w_out', 'grad_ffn_w_gate', 'grad_ffn_w_up', 'grad_ffn_w_down', 'grad_final_norm_g', 'delta_ada_w', 'delta_ada_b', 'delta_norm_mix_g', 'delta_norm_ffn_g', 'delta_gdn_w_in', 'delta_gdn_conv_w', 'delta_gdn_a_log', 'delta_gdn_dt_bias', 'delta_gdn_norm_g', 'delta_gdn_w_out', 'delta_mla_w_in', 'delta_mla_q_norm_g', 'delta_mla_kv_norm_g', 'delta_mla_w_uq', 'delta_mla_w_ukv', 'delta_mla_w_out', 'delta_ffn_w_gate', 'delta_ffn_w_up', 'delta_ffn_w_down', 'delta_final_norm_g', 'new_m_ada_w', 'new_m_ada_b', 'new_m_norm_mix_g', 'new_m_norm_ffn_g', 'new_m_gdn_w_in', 'new_m_gdn_conv_w', 'new_m_gdn_a_log', 'new_m_gdn_dt_bias', 'new_m_gdn_norm_g', 'new_m_gdn_w_out', 'new_m_mla_w_in', 'new_m_mla_q_norm_g', 'new_m_mla_kv_norm_g', 'new_m_mla_w_uq', 'new_m_mla_w_ukv', 'new_m_mla_w_out', 'new_m_ffn_w_gate', 'new_m_ffn_w_up', 'new_m_ffn_w_down', 'new_m_final_norm_g', 'new_v_ada_w', 'new_v_ada_b', 'new_v_norm_mix_g', 'new_v_norm_ffn_g', 'new_v_gdn_w_in', 'new_v_gdn_conv_w', 'new_v_gdn_a_log', 'new_v_gdn_dt_bias', 'new_v_gdn_norm_g', 'new_v_gdn_w_out', 'new_v_mla_w_in', 'new_v_mla_q_norm_g', 'new_v_mla_kv_norm_g', 'new_v_mla_w_uq', 'new_v_mla_w_ukv', 'new_v_mla_w_out', 'new_v_ffn_w_gate', 'new_v_ffn_w_up', 'new_v_ffn_w_down', 'new_v_final_norm_g']
TWIN_LEAF_KINDS = {'loss': 'loss', 'grad_x': 'grad_x', 'grad_ada_w': 'grad_w', 'grad_ada_b': 'grad_w', 'grad_norm_mix_g': 'grad_w', 'grad_norm_ffn_g': 'grad_w', 'grad_gdn_w_in': 'grad_w', 'grad_gdn_conv_w': 'grad_w', 'grad_gdn_a_log': 'grad_w', 'grad_gdn_dt_bias': 'grad_w', 'grad_gdn_norm_g': 'grad_w', 'grad_gdn_w_out': 'grad_w', 'grad_mla_w_in': 'grad_w', 'grad_mla_q_norm_g': 'grad_w', 'grad_mla_kv_norm_g': 'grad_w', 'grad_mla_w_uq': 'grad_w', 'grad_mla_w_ukv': 'grad_w', 'grad_mla_w_out': 'grad_w', 'grad_ffn_w_gate': 'grad_w', 'grad_ffn_w_up': 'grad_w', 'grad_ffn_w_down': 'grad_w', 'grad_final_norm_g': 'grad_w', 'delta_ada_w': 'delta_w', 'delta_ada_b': 'delta_w', 'delta_norm_mix_g': 'delta_w', 'delta_norm_ffn_g': 'delta_w', 'delta_gdn_w_in': 'delta_w', 'delta_gdn_conv_w': 'delta_w', 'delta_gdn_a_log': 'delta_w', 'delta_gdn_dt_bias': 'delta_w', 'delta_gdn_norm_g': 'delta_w', 'delta_gdn_w_out': 'delta_w', 'delta_mla_w_in': 'delta_w', 'delta_mla_q_norm_g': 'delta_w', 'delta_mla_kv_norm_g': 'delta_w', 'delta_mla_w_uq': 'delta_w', 'delta_mla_w_ukv': 'delta_w', 'delta_mla_w_out': 'delta_w', 'delta_ffn_w_gate': 'delta_w', 'delta_ffn_w_up': 'delta_w', 'delta_ffn_w_down': 'delta_w', 'delta_final_norm_g': 'delta_w', 'new_m_ada_w': 'new_m', 'new_m_ada_b': 'new_m', 'new_m_norm_mix_g': 'new_m', 'new_m_norm_ffn_g': 'new_m', 'new_m_gdn_w_in': 'new_m', 'new_m_gdn_conv_w': 'new_m', 'new_m_gdn_a_log': 'new_m', 'new_m_gdn_dt_bias': 'new_m', 'new_m_gdn_norm_g': 'new_m', 'new_m_gdn_w_out': 'new_m', 'new_m_mla_w_in': 'new_m', 'new_m_mla_q_norm_g': 'new_m', 'new_m_mla_kv_norm_g': 'new_m', 'new_m_mla_w_uq': 'new_m', 'new_m_mla_w_ukv': 'new_m', 'new_m_mla_w_out': 'new_m', 'new_m_ffn_w_gate': 'new_m', 'new_m_ffn_w_up': 'new_m', 'new_m_ffn_w_down': 'new_m', 'new_m_final_norm_g': 'new_m', 'new_v_ada_w': 'new_v', 'new_v_ada_b': 'new_v', 'new_v_norm_mix_g': 'new_v', 'new_v_norm_ffn_g': 'new_v', 'new_v_gdn_w_in': 'new_v', 'new_v_gdn_conv_w': 'new_v', 'new_v_gdn_a_log': 'new_v', 'new_v_gdn_dt_bias': 'new_v', 'new_v_gdn_norm_g': 'new_v', 'new_v_gdn_w_out': 'new_v', 'new_v_mla_w_in': 'new_v', 'new_v_mla_q_norm_g': 'new_v', 'new_v_mla_kv_norm_g': 'new_v', 'new_v_mla_w_uq': 'new_v', 'new_v_mla_w_ukv': 'new_v', 'new_v_mla_w_out': 'new_v', 'new_v_ffn_w_gate': 'new_v', 'new_v_ffn_w_up': 'new_v', 'new_v_ffn_w_down': 'new_v', 'new_v_final_norm_g': 'new_v'}


def _forward(args):
    return _fwd_reference(*[args[k] for k in FWD_PARAMS])


def _output_shape():
    out = _jax.eval_shape(lambda: _forward(_fwd_setup_inputs(0)))
    return out.shape, out.dtype

N_MICROBATCH = 1
ADAM_LR = 0.001
ADAM_B1 = 0.9
ADAM_B2 = 0.999
ADAM_EPS = 1e-08
ADAM_WD = 0.01
ADAM_STEP = 10
PER_EXAMPLE_BATCH_AXIS = {'x': 0, 'c': 0, 'positions': 0, 'loss_target': 0}
SHARED_INPUTS = []
_WEIGHT_DTYPES = {'ada_w': _jnp.float32, 'ada_b': _jnp.float32, 'norm_mix_g': _jnp.float32, 'norm_ffn_g': _jnp.float32, 'gdn_w_in': _jnp.float32, 'gdn_conv_w': _jnp.float32, 'gdn_a_log': _jnp.float32, 'gdn_dt_bias': _jnp.float32, 'gdn_norm_g': _jnp.float32, 'gdn_w_out': _jnp.float32, 'mla_w_in': _jnp.float32, 'mla_q_norm_g': _jnp.float32, 'mla_kv_norm_g': _jnp.float32, 'mla_w_uq': _jnp.float32, 'mla_w_ukv': _jnp.float32, 'mla_w_out': _jnp.float32, 'ffn_w_gate': _jnp.float32, 'ffn_w_up': _jnp.float32, 'ffn_w_down': _jnp.float32, 'final_norm_g': _jnp.float32}
MOMENT_SCALE = {'ada_w': 3.940970e-02, 'ada_b': 6.764546e-02, 'norm_mix_g': 3.029745e-02, 'norm_ffn_g': 3.799918e-02, 'gdn_w_in': 2.183678e-02, 'gdn_conv_w': 2.012827e-02, 'gdn_a_log': 1.012452e-01, 'gdn_dt_bias': 9.726995e-02, 'gdn_norm_g': 7.822848e-02, 'gdn_w_out': 2.635039e-02, 'mla_w_in': 2.038671e-02, 'mla_q_norm_g': 1.040344e-02, 'mla_kv_norm_g': 3.337838e-02, 'mla_w_uq': 5.321574e-03, 'mla_w_ukv': 1.057816e-02, 'mla_w_out': 1.400245e-02, 'ffn_w_gate': 1.703739e-02, 'ffn_w_up': 1.653478e-02, 'ffn_w_down': 2.745693e-02, 'final_norm_g': 1.604525e+01}


def _to_microbatches(a, axis):
    t = _jnp.moveaxis(a, axis, 0)
    t = t.reshape((N_MICROBATCH, t.shape[0] // N_MICROBATCH) + t.shape[1:])
    return _jnp.moveaxis(t, 1, axis + 1)


def setup_inputs(seed: int = 0) -> dict:
    inp = _fwd_setup_inputs(seed)
    key = _jax.random.fold_in(_jax.random.key(seed), 7919)
    shape, _ = _output_shape()
    out = dict(inp)
    out["loss_target"] = _jax.random.normal(_jax.random.fold_in(key, 0), shape, _jnp.float32)
    for i, name in enumerate(TWIN_WEIGHTS):
        w = inp[name].astype(_jnp.float32)
        if MOMENT_SCALE is None:
            s = _jnp.sqrt(_jnp.mean(_jnp.square(w)) + 1e-30)
        else:
            s = MOMENT_SCALE[name]
        km, kv = _jax.random.split(_jax.random.fold_in(key, i + 1))
        out[name] = w
        out["m_" + name] = s * _jax.random.normal(km, w.shape, _jnp.float32)
        out["v_" + name] = (s * s) * _jax.random.uniform(kv, w.shape, _jnp.float32, 0.5, 1.5)
    if N_MICROBATCH > 1:
        for name, axis in PER_EXAMPLE_BATCH_AXIS.items():
            out[name] = _to_microbatches(out[name], axis)
    return {'x': out['x'], 'c': out['c'], 'positions': out['positions'], 'ada_w': out['ada_w'], 'ada_b': out['ada_b'], 'norm_mix_g': out['norm_mix_g'], 'norm_ffn_g': out['norm_ffn_g'], 'gdn_w_in': out['gdn_w_in'], 'gdn_conv_w': out['gdn_conv_w'], 'gdn_a_log': out['gdn_a_log'], 'gdn_dt_bias': out['gdn_dt_bias'], 'gdn_norm_g': out['gdn_norm_g'], 'gdn_w_out': out['gdn_w_out'], 'mla_w_in': out['mla_w_in'], 'mla_q_norm_g': out['mla_q_norm_g'], 'mla_kv_norm_g': out['mla_kv_norm_g'], 'mla_w_uq': out['mla_w_uq'], 'mla_w_ukv': out['mla_w_ukv'], 'mla_w_out': out['mla_w_out'], 'ffn_w_gate': out['ffn_w_gate'], 'ffn_w_up': out['ffn_w_up'], 'ffn_w_down': out['ffn_w_down'], 'final_norm_g': out['final_norm_g'], 'loss_target': out['loss_target'], 'm_ada_w': out['m_ada_w'], 'm_ada_b': out['m_ada_b'], 'm_norm_mix_g': out['m_norm_mix_g'], 'm_norm_ffn_g': out['m_norm_ffn_g'], 'm_gdn_w_in': out['m_gdn_w_in'], 'm_gdn_conv_w': out['m_gdn_conv_w'], 'm_gdn_a_log': out['m_gdn_a_log'], 'm_gdn_dt_bias': out['m_gdn_dt_bias'], 'm_gdn_norm_g': out['m_gdn_norm_g'], 'm_gdn_w_out': out['m_gdn_w_out'], 'm_mla_w_in': out['m_mla_w_in'], 'm_mla_q_norm_g': out['m_mla_q_norm_g'], 'm_mla_kv_norm_g': out['m_mla_kv_norm_g'], 'm_mla_w_uq': out['m_mla_w_uq'], 'm_mla_w_ukv': out['m_mla_w_ukv'], 'm_mla_w_out': out['m_mla_w_out'], 'm_ffn_w_gate': out['m_ffn_w_gate'], 'm_ffn_w_up': out['m_ffn_w_up'], 'm_ffn_w_down': out['m_ffn_w_down'], 'm_final_norm_g': out['m_final_norm_g'], 'v_ada_w': out['v_ada_w'], 'v_ada_b': out['v_ada_b'], 'v_norm_mix_g': out['v_norm_mix_g'], 'v_norm_ffn_g': out['v_norm_ffn_g'], 'v_gdn_w_in': out['v_gdn_w_in'], 'v_gdn_conv_w': out['v_gdn_conv_w'], 'v_gdn_a_log': out['v_gdn_a_log'], 'v_gdn_dt_bias': out['v_gdn_dt_bias'], 'v_gdn_norm_g': out['v_gdn_norm_g'], 'v_gdn_w_out': out['v_gdn_w_out'], 'v_mla_w_in': out['v_mla_w_in'], 'v_mla_q_norm_g': out['v_mla_q_norm_g'], 'v_mla_kv_norm_g': out['v_mla_kv_norm_g'], 'v_mla_w_uq': out['v_mla_w_uq'], 'v_mla_w_ukv': out['v_mla_w_ukv'], 'v_mla_w_out': out['v_mla_w_out'], 'v_ffn_w_gate': out['v_ffn_w_gate'], 'v_ffn_w_up': out['v_ffn_w_up'], 'v_ffn_w_down': out['v_ffn_w_down'], 'v_final_norm_g': out['v_final_norm_g']}


def _loss(weights, diff, rest, loss_target):
    with _jax.named_scope("forward"):
        args = {**rest, TWIN_DIFF_INPUT: diff, **{k: w.astype(_WEIGHT_DTYPES[k]) for k, w in weights.items()}}
        y = _forward(args)
    with _jax.named_scope("loss_head"):
        err = _jnp.square(y.astype(_jnp.float32) - loss_target)
        return 0.5 * _jnp.sum(_jnp.mean(err, axis=-1)) if err.ndim else 0.5 * err


def _adamw(w, g, m, v):
    m = ADAM_B1 * m + (1.0 - ADAM_B1) * g
    v = ADAM_B2 * v + (1.0 - ADAM_B2) * _jnp.square(g)
    m_hat = m / (1.0 - ADAM_B1 ** ADAM_STEP)
    v_hat = v / (1.0 - ADAM_B2 ** ADAM_STEP)
    delta = -ADAM_LR * (m_hat / (_jnp.sqrt(v_hat) + ADAM_EPS) + ADAM_WD * w)
    return delta, m, v


def reference(x, c, positions, ada_w, ada_b, norm_mix_g, norm_ffn_g, gdn_w_in, gdn_conv_w, gdn_a_log, gdn_dt_bias, gdn_norm_g, gdn_w_out, mla_w_in, mla_q_norm_g, mla_kv_norm_g, mla_w_uq, mla_w_ukv, mla_w_out, ffn_w_gate, ffn_w_up, ffn_w_down, final_norm_g, loss_target, m_ada_w, m_ada_b, m_norm_mix_g, m_norm_ffn_g, m_gdn_w_in, m_gdn_conv_w, m_gdn_a_log, m_gdn_dt_bias, m_gdn_norm_g, m_gdn_w_out, m_mla_w_in, m_mla_q_norm_g, m_mla_kv_norm_g, m_mla_w_uq, m_mla_w_ukv, m_mla_w_out, m_ffn_w_gate, m_ffn_w_up, m_ffn_w_down, m_final_norm_g, v_ada_w, v_ada_b, v_norm_mix_g, v_norm_ffn_g, v_gdn_w_in, v_gdn_conv_w, v_gdn_a_log, v_gdn_dt_bias, v_gdn_norm_g, v_gdn_w_out, v_mla_w_in, v_mla_q_norm_g, v_mla_kv_norm_g, v_mla_w_uq, v_mla_w_ukv, v_mla_w_out, v_ffn_w_gate, v_ffn_w_up, v_ffn_w_down, v_final_norm_g):
    given = dict(x=x, c=c, positions=positions, ada_w=ada_w, ada_b=ada_b, norm_mix_g=norm_mix_g, norm_ffn_g=norm_ffn_g, gdn_w_in=gdn_w_in, gdn_conv_w=gdn_conv_w, gdn_a_log=gdn_a_log, gdn_dt_bias=gdn_dt_bias, gdn_norm_g=gdn_norm_g, gdn_w_out=gdn_w_out, mla_w_in=mla_w_in, mla_q_norm_g=mla_q_norm_g, mla_kv_norm_g=mla_kv_norm_g, mla_w_uq=mla_w_uq, mla_w_ukv=mla_w_ukv, mla_w_out=mla_w_out, ffn_w_gate=ffn_w_gate, ffn_w_up=ffn_w_up, ffn_w_down=ffn_w_down, final_norm_g=final_norm_g, loss_target=loss_target, m_ada_w=m_ada_w, m_ada_b=m_ada_b, m_norm_mix_g=m_norm_mix_g, m_norm_ffn_g=m_norm_ffn_g, m_gdn_w_in=m_gdn_w_in, m_gdn_conv_w=m_gdn_conv_w, m_gdn_a_log=m_gdn_a_log, m_gdn_dt_bias=m_gdn_dt_bias, m_gdn_norm_g=m_gdn_norm_g, m_gdn_w_out=m_gdn_w_out, m_mla_w_in=m_mla_w_in, m_mla_q_norm_g=m_mla_q_norm_g, m_mla_kv_norm_g=m_mla_kv_norm_g, m_mla_w_uq=m_mla_w_uq, m_mla_w_ukv=m_mla_w_ukv, m_mla_w_out=m_mla_w_out, m_ffn_w_gate=m_ffn_w_gate, m_ffn_w_up=m_ffn_w_up, m_ffn_w_down=m_ffn_w_down, m_final_norm_g=m_final_norm_g, v_ada_w=v_ada_w, v_ada_b=v_ada_b, v_norm_mix_g=v_norm_mix_g, v_norm_ffn_g=v_norm_ffn_g, v_gdn_w_in=v_gdn_w_in, v_gdn_conv_w=v_gdn_conv_w, v_gdn_a_log=v_gdn_a_log, v_gdn_dt_bias=v_gdn_dt_bias, v_gdn_norm_g=v_gdn_norm_g, v_gdn_w_out=v_gdn_w_out, v_mla_w_in=v_mla_w_in, v_mla_q_norm_g=v_mla_q_norm_g, v_mla_kv_norm_g=v_mla_kv_norm_g, v_mla_w_uq=v_mla_w_uq, v_mla_w_ukv=v_mla_w_ukv, v_mla_w_out=v_mla_w_out, v_ffn_w_gate=v_ffn_w_gate, v_ffn_w_up=v_ffn_w_up, v_ffn_w_down=v_ffn_w_down, v_final_norm_g=v_final_norm_g)
    weights = {n: given[n] for n in TWIN_WEIGHTS}
    shared = {n: given[n] for n in SHARED_INPUTS}
    per_example = {n: given[n] for n in ['x', 'c', 'positions']}
    grad_fn = _jax.value_and_grad(_loss, argnums=(0, 1))

    def one_microbatch(ex, loss_target):
        ex = dict(ex)
        diff = ex.pop(TWIN_DIFF_INPUT)
        return grad_fn(weights, diff, {**shared, **ex}, loss_target)

    if N_MICROBATCH == 1:
        loss, (grad_w, grad_x) = one_microbatch(per_example, given["loss_target"])
    else:
        def body(carry, xs):
            loss_sum, grad_sum = carry
            l_k, (gw_k, gx_k) = one_microbatch(xs[0], xs[1])
            with _jax.named_scope("update"):
                return (loss_sum + l_k, _jax.tree.map(_jnp.add, grad_sum, gw_k)), gx_k

        init = (_jnp.zeros((), _jnp.float32), _jax.tree.map(_jnp.zeros_like, weights))
        (loss, grad_w), grad_x = _jax.lax.scan(body, init, (per_example, given["loss_target"]))
    with _jax.named_scope("update"):
        delta_w, new_m, new_v = {}, {}, {}
        for n in TWIN_WEIGHTS:
            delta_w[n], new_m[n], new_v[n] = _adamw(weights[n], grad_w[n], given["m_" + n], given["v_" + n])
    return (loss, grad_x, *[grad_w[n] for n in TWIN_WEIGHTS], *[delta_w[n] for n in TWIN_WEIGHTS],
            *[new_m[n] for n in TWIN_WEIGHTS], *[new_v[n] for n in TWIN_WEIGHTS])
```

```python
import functools

import jax
import jax.numpy as jnp
from jax import lax
from jax.experimental import pallas as pl
from jax.experimental.pallas import tpu as pltpu

F32 = jnp.float32
BF16 = jnp.bfloat16
HI = lax.Precision.HIGHEST
MESH = pl.DeviceIdType.MESH

D = 1024
DEPTH = 4
N_MOD = 6
NH = 8
HD = 128
CHUNK = 64
GDN_QKV = 3 * NH * HD
GDN_INK = GDN_QKV + NH * HD + 2 * HD
Q_RANK, KV_RANK, ROPE = 384, 256, 64
MLA_INK = Q_RANK + KV_RANK + HD
DFF = 2816
EPS = 1e-6
ATT_SCALE = (HD + ROPE) ** -0.5
ROPE_THETA = 10000.0
LANES = 128
PACK_W = 1024

ADAM_LR, ADAM_B1, ADAM_B2, ADAM_EPS, ADAM_WD, ADAM_STEP = 0.001, 0.9, 0.999, 1e-08, 0.01, 10


def _dot(a, b, mode="nn", prec=None):
    dn = {"nn": (((1,), (0,)), ((), ())), "nt": (((1,), (1,)), ((), ())), "tn": (((0,), (0,)), ((), ()))}[mode]
    return lax.dot_general(a, b, dn, precision=prec, preferred_element_type=F32)


def _sig(x):
    return 1.0 / (1.0 + jnp.exp(-x))


def _pick(n, cap):
    if n <= cap:
        return n
    best = None
    for d in range(LANES, cap + 1, LANES):
        if n % d == 0:
            best = d
    assert best is not None, (n, cap)
    return best


def _params(n_grid):
    return pltpu.CompilerParams(dimension_semantics=("arbitrary",) * n_grid, vmem_limit_bytes=56 * 1024 * 1024)


def _rowwise(name, fn, rows, consts, outs, sums=(), tr=256):
    first = rows[0][0] if isinstance(rows[0], tuple) else rows[0]
    T = first.shape[-2]
    tr = min(tr, T)
    assert T % tr == 0
    nr, nc, no, ns = len(rows), len(consts), len(outs), len(sums)

    def body(*refs):
        res = fn(*[r[...] for r in refs[:nr + nc]])
        if not isinstance(res, (tuple, list)):
            res = (res,)
        o_refs = refs[nr + nc:nr + nc + no]
        s_refs = refs[nr + nc + no:]
        for r, val in zip(o_refs, res[:no]):
            r[...] = val.astype(r.dtype)
        if ns:
            @pl.when(pl.program_id(0) == 0)
            def _():
                for r in s_refs:
                    r[...] = jnp.zeros_like(r)
            for r, val in zip(s_refs, res[no:]):
                r[...] += val

    in_specs, args = [], []
    for a in rows:
        if isinstance(a, tuple):
            arr, width, cb = a
            in_specs.append(pl.BlockSpec((tr, width), lambda i, cb=cb: (i, cb)))
            args.append(arr)
        elif a.ndim == 3:
            in_specs.append(pl.BlockSpec((a.shape[0], tr, a.shape[2]), lambda i: (0, i, 0)))
            args.append(a)
        else:
            in_specs.append(pl.BlockSpec((tr, a.shape[1]), lambda i: (i, 0)))
            args.append(a)
    for a in consts:
        in_specs.append(pl.BlockSpec(a.shape, lambda i, nd=a.ndim: (0,) * nd))
        args.append(a)
    out_specs = [pl.BlockSpec((tr, w), lambda i: (i, 0)) for w, _ in outs]
    out_specs += [pl.BlockSpec((1, w), lambda i: (0, 0)) for w in sums]
    out_shape = [jax.ShapeDtypeStruct((T, w), dt) for w, dt in outs]
    out_shape += [jax.ShapeDtypeStruct((1, w), F32) for w in sums]
    res = pl.pallas_call(body, name=name, grid=(T // tr,), in_specs=in_specs, out_specs=out_specs,
                         out_shape=out_shape, compiler_params=_params(1))(*args)
    return res


def _mm(name, a, b, mode, out_dtype=F32, tm=512, tn=1024):
    if mode == "tn":
        K, M = a.shape
    else:
        M, K = a.shape
    N = b.shape[0] if mode == "nt" else b.shape[1]
    tm, tn = _pick(M, tm), _pick(N, tn)

    def body(a_ref, b_ref, o_ref):
        o_ref[...] = _dot(a_ref[...].astype(BF16), b_ref[...].astype(BF16), mode).astype(o_ref.dtype)

    a_spec = pl.BlockSpec((K, tm), lambda i, j: (0, i)) if mode == "tn" else pl.BlockSpec((tm, K), lambda i, j: (i, 0))
    b_spec = pl.BlockSpec((tn, K), lambda i, j: (j, 0)) if mode == "nt" else pl.BlockSpec((K, tn), lambda i, j: (0, j))
    return pl.pallas_call(body, name=name, grid=(M // tm, N // tn), in_specs=[a_spec, b_spec],
                          out_specs=pl.BlockSpec((tm, tn), lambda i, j: (i, j)),
                          out_shape=jax.ShapeDtypeStruct((M, N), out_dtype), compiler_params=_params(2))(a, b)


def _rms(x, eps=EPS):
    return lax.rsqrt(jnp.mean(x * x, axis=-1, keepdims=True) + eps)


def _norm_mod_fwd(name, x, g, scale, shift):
    def fn(x, g, scale, shift):
        return x * _rms(x) * g * (1.0 + scale) + shift
    return _rowwise(name, fn, [x], [g, scale, shift], [(D, BF16)])[0]


def _norm_mod_bwd(name, dh, x, dx_res, g, scale):
    def fn(dh, x, dx_res, g, scale):
        r = _rms(x)
        xh = x * r
        dxh = dh * (g * (1.0 + scale))
        dx = r * (dxh - xh * jnp.mean(dxh * xh, axis=-1, keepdims=True))
        dhx = dh * xh
        return (dx_res + dx, jnp.sum(dh, axis=0, keepdims=True), jnp.sum(dhx * g, axis=0, keepdims=True),
                jnp.sum(dhx * (1.0 + scale), axis=0, keepdims=True))
    return _rowwise(name, fn, [dh, x, dx_res], [g, scale], [(D, F32)], sums=[D, D, D])


def _residual_fwd(name, x, y, gate):
    def fn(x, y, gate):
        return x + gate * y
    return _rowwise(name, fn, [x, y], [gate], [(D, F32)])[0]


def _residual_bwd(name, dx, y, gate):
    def fn(dx, y, gate):
        return dx * gate, jnp.sum(dx * y, axis=0, keepdims=True)
    return _rowwise(name, fn, [dx, y], [gate], [(D, BF16)], sums=[D])


def _loss_head(x, target, g):
    def fn(x, t, g):
        r = _rms(x)
        xh = x * r
        err = xh * g - t
        loss = 0.5 * jnp.sum(jnp.mean(err * err, axis=-1, keepdims=True), axis=0, keepdims=True)
        dy = err * (1.0 / D)
        dxh = dy * g
        dx = r * (dxh - xh * jnp.mean(dxh * xh, axis=-1, keepdims=True))
        return dx, jnp.broadcast_to(loss, (1, LANES)), jnp.sum(dy * xh, axis=0, keepdims=True)
    return _rowwise("loss_head", fn, [x, target], [g], [(D, F32)], sums=[LANES, D])


def _swiglu_fwd(name, ab):
    def fn(a, b):
        return a * _sig(a) * b
    return _rowwise(name, fn, [(ab, DFF, 0), (ab, DFF, 1)], [], [(DFF, BF16)])[0]


def _swiglu_bwd(name, ds, ab):
    def fn(ds, a, b):
        sg = _sig(a)
        da = ds * b * (sg * (1.0 + a * (1.0 - sg)))
        db = ds * (a * sg)
        return jnp.concatenate([da, db], axis=1)
    return _rowwise(name, fn, [ds, (ab, DFF, 0), (ab, DFF, 1)], [], [(2 * DFF, BF16)])[0]


def _shift_down(x, k):
    if k == 0:
        return x
    rows = lax.broadcasted_iota(jnp.int32, x.shape, 0)
    return jnp.where(rows >= k, pltpu.roll(x, k, 0), 0.0)


def _shift_up(x, k):
    if k == 0:
        return x
    T = x.shape[0]
    rows = lax.broadcasted_iota(jnp.int32, x.shape, 0)
    return jnp.where(rows < T - k, pltpu.roll(x, T - k, 0), 0.0)


def _conv_silu(x, w):
    c = w[0:1, :] * _shift_down(x, 3) + w[1:2, :] * _shift_down(x, 2) + w[2:3, :] * _shift_down(x, 1) + w[3:4, :] * x
    sg = _sig(c)
    return c, sg, c * sg


def _gdn_conv_fwd(name, proj, cw):
    T = proj.shape[0]

    def body(x_ref, w_ref, o_ref):
        j = pl.program_id(0)
        _, _, y = _conv_silu(x_ref[...], w_ref[...])
        r = lax.rsqrt(jnp.sum(y * y, axis=1, keepdims=True) + EPS)
        mult = jnp.where(j < NH, HD ** -0.5, 1.0)
        o_ref[...] = jnp.where(j < 2 * NH, y * (r * mult), y)

    return pl.pallas_call(body, name=name, grid=(3 * NH,),
                          in_specs=[pl.BlockSpec((T, HD), lambda j: (0, j)), pl.BlockSpec((4, HD), lambda j: (0, j))],
                          out_specs=pl.BlockSpec((T, HD), lambda j: (0, j)),
                          out_shape=jax.ShapeDtypeStruct((T, GDN_QKV), F32), compiler_params=_params(1))(proj, cw)


def _gdn_conv_bwd(name, proj, cw, dz):
    T = proj.shape[0]

    def body(x_ref, w_ref, dz_ref, dx_ref, dw_ref):
        j = pl.program_id(0)
        x, w, dz = x_ref[...], w_ref[...], dz_ref[...]
        c, sg, y = _conv_silu(x, w)
        r = lax.rsqrt(jnp.sum(y * y, axis=1, keepdims=True) + EPS)
        mult = jnp.where(j < NH, HD ** -0.5, 1.0)
        dyn = mult * (r * dz - (r * r * r) * y * jnp.sum(dz * y, axis=1, keepdims=True))
        dy = jnp.where(j < 2 * NH, dyn, dz)
        dc = dy * (sg * (1.0 + c * (1.0 - sg)))
        dx = w[0:1, :] * _shift_up(dc, 3) + w[1:2, :] * _shift_up(dc, 2) + w[2:3, :] * _shift_up(dc, 1) + w[3:4, :] * dc
        dx_ref[...] = dx.astype(dx_ref.dtype)
        for k in range(4):
            dw_ref[pl.ds(k, 1), :] = jnp.sum(dc * _shift_down(x, 3 - k), axis=0, keepdims=True)

    return pl.pallas_call(body, name=name, grid=(3 * NH,),
                          in_specs=[pl.BlockSpec((T, HD), lambda j: (0, j)), pl.BlockSpec((4, HD), lambda j: (0, j)),
                                    pl.BlockSpec((T, HD), lambda j: (0, j))],
                          out_specs=[pl.BlockSpec((T, HD), lambda j: (0, j)), pl.BlockSpec((4, HD), lambda j: (0, j))],
                          out_shape=[jax.ShapeDtypeStruct((T, GDN_QKV), BF16), jax.ShapeDtypeStruct((4, GDN_QKV), F32)],
                          compiler_params=_params(1))(proj, cw, dz)


def _softplus(z):
    return jnp.maximum(z, 0.0) + jnp.log(1.0 + jnp.exp(-jnp.abs(z)))


_AB_CB = GDN_INK // (2 * HD) - 1


def _gdn_gates_fwd(name, proj, alog, dtb):
    def fn(ab, alog, dtb):
        a, b = ab[:, :HD], ab[:, HD:]
        return -jnp.exp(alog) * _softplus(a + dtb), _sig(b)
    return _rowwise(name, fn, [(proj, 2 * HD, _AB_CB)], [alog, dtb], [(HD, F32), (HD, F32)])


def _gdn_gates_bwd(name, proj, dg_h, db_h, alog, dtb):
    def fn(ab, dg_h, db_h, alog, dtb):
        lane = lax.broadcasted_iota(jnp.int32, (1, HD), 1)
        dg = jnp.zeros(dg_h.shape[1:], F32)
        dbeta = jnp.zeros(dg_h.shape[1:], F32)
        for h in range(NH):
            oh = (lane == h).astype(F32)
            dg = dg + dg_h[h] * oh
            dbeta = dbeta + db_h[h] * oh
        a, b = ab[:, :HD], ab[:, HD:]
        z = a + dtb
        ea = jnp.exp(alog)
        beta = _sig(b)
        da = dg * (-ea) * _sig(z)
        db = dbeta * beta * (1.0 - beta)
        return (jnp.concatenate([da, db], axis=1), jnp.sum(dg * (-ea * _softplus(z)), axis=0, keepdims=True),
                jnp.sum(da, axis=0, keepdims=True))
    return _rowwise(name, fn, [(proj, 2 * HD, _AB_CB), dg_h, db_h], [alog, dtb], [(2 * HD, BF16)], sums=[HD, HD])


def _chunk_common(q, k, v, gblk, bblk, h):
    C = CHUNK
    lane = lax.broadcasted_iota(jnp.int32, (1, HD), 1)
    oh = (lane == h).astype(F32)
    g_col = jnp.sum(gblk * oh, axis=1, keepdims=True)
    beta = jnp.sum(bblk * oh, axis=1, keepdims=True)
    ri = lax.broadcasted_iota(jnp.int32, (C, C), 0)
    ci = lax.broadcasted_iota(jnp.int32, (C, C), 1)
    incl = ri >= ci
    strict = ri > ci
    eye = (ri == ci).astype(F32)
    gcb = _dot(incl.astype(F32), jnp.broadcast_to(g_col, (C, HD)), "nn", HI)
    gc = gcb[:, :C]
    gc_row = _dot(jnp.ones((C, C), F32), eye * gc, "nn", HI)
    decay = jnp.where(incl, jnp.exp(jnp.where(incl, gc - gc_row, 0.0)), 0.0)
    rows = lax.broadcasted_iota(jnp.int32, (C, HD), 0)
    gclb = jnp.sum(jnp.where(rows == C - 1, gcb, 0.0), axis=0, keepdims=True)
    eg = jnp.exp(gcb)
    egl = jnp.exp(gclb - gcb)
    gl = jnp.exp(gclb)
    kb = k * beta
    m1 = _dot(kb, k, "nt", HI)
    L = jnp.where(strict, m1 * decay, 0.0)
    nl = -L
    tinv = eye + nl
    p = nl
    for _ in range(5):
        p = _dot(p, p, "nn", HI)
        tinv = tinv + _dot(tinv, p, "nn", HI)
    vb = v * beta
    kbg = kb * eg
    u = _dot(tinv, vb, "nn", HI)
    w = _dot(tinv, kbg, "nn", HI)
    qk = _dot(q, k, "nt", HI)
    attn = jnp.where(incl, qk * decay, 0.0)
    return dict(beta=beta, incl=incl, strict=strict, decay=decay, eg=eg, egl=egl, gl=gl, kb=kb, m1=m1, tinv=tinv,
                kbg=kbg, u=u, w=w, qk=qk, attn=attn, q_dec=q * eg, k_dec=k * egl, rows=rows, oh=oh)


def _gdn_chunk_fwd(name, qkv, g, beta):
    T = qkv.shape[0]
    N = T // CHUNK

    def body(q_ref, k_ref, v_ref, g_ref, b_ref, o_ref, st_ref, S):
        h, n = pl.program_id(0), pl.program_id(1)

        @pl.when(n == 0)
        def _():
            S[...] = jnp.zeros_like(S)

        c = _chunk_common(q_ref[...], k_ref[...], v_ref[...], g_ref[...], b_ref[...], h)
        s = S[...]
        st_ref[0, 0] = s
        v_new = c["u"] - _dot(c["w"], s, "nn", HI)
        o_ref[...] = _dot(c["q_dec"], s, "nn", HI) + _dot(c["attn"], v_new, "nn", HI)
        S[...] = s * c["gl"] + _dot(c["k_dec"], v_new, "tn", HI)

    blk = lambda off: pl.BlockSpec((CHUNK, HD), lambda h, n, off=off: (n, off + h))
    gspec = pl.BlockSpec((CHUNK, HD), lambda h, n: (n, 0))
    return pl.pallas_call(
        body, name=name, grid=(NH, N), in_specs=[blk(0), blk(NH), blk(2 * NH), gspec, gspec],
        out_specs=[pl.BlockSpec((CHUNK, HD), lambda h, n: (n, h)), pl.BlockSpec((1, 1, HD, HD), lambda h, n: (h, n, 0, 0))],
        out_shape=[jax.ShapeDtypeStruct((T, NH * HD), F32), jax.ShapeDtypeStruct((NH, N, HD, HD), F32)],
        scratch_shapes=[pltpu.VMEM((HD, HD), F32)], compiler_params=_params(2))(qkv, qkv, qkv, g, beta)


def _gdn_chunk_bwd(name, qkv, g, beta, states, do):
    T = qkv.shape[0]
    N = T // CHUNK
    C = CHUNK

    def body(q_ref, k_ref, v_ref, g_ref, b_ref, st_ref, do_ref, dq_ref, dk_ref, dv_ref, dg_ref, db_ref, dS):
        h, n = pl.program_id(0), pl.program_id(1)

        @pl.when(n == 0)
        def _():
            dS[...] = jnp.zeros_like(dS)

        q, k, v = q_ref[...], k_ref[...], v_ref[...]
        c = _chunk_common(q, k, v, g_ref[...], b_ref[...], h)
        s = st_ref[0, 0]
        do = do_ref[...]
        ds = dS[...]
        eg, egl, gl, beta, decay, tinv = c["eg"], c["egl"], c["gl"], c["beta"], c["decay"], c["tinv"]
        v_new = c["u"] - _dot(c["w"], s, "nn", HI)
        dv_new = _dot(c["attn"], do, "tn", HI) + _dot(c["k_dec"], ds, "nn", HI)
        dk_dec = _dot(v_new, ds, "nt", HI)
        dgl = jnp.sum(jnp.sum(s * ds, axis=1, keepdims=True), axis=0, keepdims=True)
        dq_dec = _dot(do, s, "nt", HI)
        dS[...] = ds * gl + _dot(c["q_dec"], do, "tn", HI) - _dot(c["w"], dv_new, "tn", HI)
        dattn = jnp.where(c["incl"], _dot(do, v_new, "nt", HI), 0.0)
        dw = -_dot(dv_new, s, "nt", HI)
        dvb = _dot(tinv, dv_new, "tn", HI)
        dkbg = _dot(tinv, dw, "tn", HI)
        dA = -(_dot(dvb, c["u"], "nt", HI) + _dot(dkbg, c["w"], "nt", HI))
        dL = jnp.where(c["strict"], dA, 0.0)
        dm1 = dL * decay
        dqk = dattn * decay
        xdec = (dL * c["m1"] + dattn * c["qk"]) * decay
        dkb = _dot(dm1, k, "nn", HI) + dkbg * eg
        dk = _dot(dm1, c["kb"], "tn", HI) + _dot(dqk, q, "tn", HI) + dk_dec * egl + dkb * beta
        dq = _dot(dqk, k, "nn", HI) + dq_dec * eg
        dkd_kd = jnp.sum(dk_dec * c["k_dec"], axis=1, keepdims=True)
        dgc = (jnp.sum(xdec, axis=1, keepdims=True) - _dot(xdec, jnp.ones((C, HD), F32), "tn", HI)
               + jnp.sum(dq_dec * c["q_dec"], axis=1, keepdims=True) - dkd_kd
               + jnp.sum(dkbg * c["kbg"], axis=1, keepdims=True))
        dgcl = jnp.sum(dkd_kd, axis=0, keepdims=True) + dgl * gl
        dgc = dgc + jnp.where(c["rows"] == C - 1, dgcl, 0.0)
        ri = lax.broadcasted_iota(jnp.int32, (C, C), 0)
        ci = lax.broadcasted_iota(jnp.int32, (C, C), 1)
        dg_ref[0] = _dot((ci >= ri).astype(F32), dgc, "nn", HI)
        db_ref[0] = jnp.broadcast_to(jnp.sum(dkb * k, axis=1, keepdims=True) + jnp.sum(dvb * v, axis=1, keepdims=True),
                                     (C, HD))
        dq_ref[...] = dq
        dk_ref[...] = dk
        dv_ref[...] = dvb * beta

    blk = lambda off: pl.BlockSpec((C, HD), lambda h, n, off=off: (N - 1 - n, off + h))
    gspec = pl.BlockSpec((C, HD), lambda h, n: (N - 1 - n, 0))
    ospec = pl.BlockSpec((C, HD), lambda h, n: (N - 1 - n, h))
    hspec = pl.BlockSpec((1, C, HD), lambda h, n: (h, N - 1 - n, 0))
    return pl.pallas_call(
        body, name=name, grid=(NH, N),
        in_specs=[blk(0), blk(NH), blk(2 * NH), gspec, gspec,
                  pl.BlockSpec((1, 1, HD, HD), lambda h, n: (h, N - 1 - n, 0, 0)), ospec],
        out_specs=[ospec, ospec, ospec, hspec, hspec],
        out_shape=[jax.ShapeDtypeStruct((T, NH * HD), F32)] * 3 + [jax.ShapeDtypeStruct((NH, T, HD), F32)] * 2,
        scratch_shapes=[pltpu.VMEM((HD, HD), F32)], compiler_params=_params(2))(qkv, qkv, qkv, g, beta, states, do)


_GATE_CB = GDN_QKV // (NH * HD)


def _gdn_gated_norm_fwd(name, o, proj, ng):
    def fn(o, gate, ng):
        outs = []
        for h in range(NH):
            sl = slice(h * HD, (h + 1) * HD)
            oh, gh = o[:, sl], gate[:, sl]
            outs.append(oh * _rms(oh) * ng * (gh * _sig(gh)))
        return jnp.concatenate(outs, axis=1)
    return _rowwise(name, fn, [o, (proj, NH * HD, _GATE_CB)], [ng], [(NH * HD, BF16)])[0]


def _gdn_gated_norm_bwd(name, don, o, proj, ng):
    def fn(don, o, gate, ng):
        dos, dgs = [], []
        dng = jnp.zeros((1, HD), F32)
        for h in range(NH):
            sl = slice(h * HD, (h + 1) * HD)
            oh, gh, dh = o[:, sl], gate[:, sl], don[:, sl]
            r = _rms(oh)
            xh = oh * r
            sg = _sig(gh)
            dn = dh * (gh * sg)
            dgs.append(dh * (xh * ng) * (sg * (1.0 + gh * (1.0 - sg))))
            dng = dng + jnp.sum(dn * xh, axis=0, keepdims=True)
            dxh = dn * ng
            dos.append(r * (dxh - xh * jnp.mean(dxh * xh, axis=-1, keepdims=True)))
        return jnp.concatenate(dos, axis=1), jnp.concatenate(dgs, axis=1), dng
    return _rowwise(name, fn, [don, o, (proj, NH * HD, _GATE_CB)], [ng], [(NH * HD, F32), (NH * HD, BF16)], sums=[HD])


def _rot(x):
    lane = lax.broadcasted_iota(jnp.int32, x.shape, 1)
    return jnp.where(lane < ROPE // 2, -pltpu.roll(x, HD - ROPE // 2, 1), pltpu.roll(x, ROPE // 2, 1))


def _rot_t(x):
    lane = lax.broadcasted_iota(jnp.int32, x.shape, 1)
    return jnp.where(lane < ROPE // 2, pltpu.roll(x, HD - ROPE // 2, 1), -pltpu.roll(x, ROPE // 2, 1))


def _rope_tables(pos_col):
    lane = jnp.arange(HD)
    inv_freq = ROPE_THETA ** (-(2.0 * (lane % (ROPE // 2)).astype(F32)) / ROPE)
    inv_freq = jnp.where(lane < ROPE, inv_freq, 0.0).astype(F32)[None, :]
    valid = (lane < ROPE).astype(F32)[None, :]

    def fn(pos, inv_freq, valid):
        ang = pos.astype(F32) * inv_freq
        return jnp.cos(ang) * valid, jnp.sin(ang) * valid
    return _rowwise("rope_tables", fn, [pos_col], [inv_freq, valid], [(HD, F32), (HD, F32)])


def _mla_pre_fwd(name, proj, cos, sin, qg, kvg):
    def fn(p, cos, sin, qg, kvg):
        cq, ckv, kr = p[:, :Q_RANK], p[:, Q_RANK:Q_RANK + KV_RANK], p[:, Q_RANK + KV_RANK:]
        return cq * _rms(cq) * qg, ckv * _rms(ckv) * kvg, kr * cos + _rot(kr) * sin
    return _rowwise(name, fn, [proj, cos, sin], [qg, kvg], [(Q_RANK, BF16), (KV_RANK, BF16), (HD, BF16)])


def _rms_bwd(dy, x, g):
    r = _rms(x)
    xh = x * r
    dxh = dy * g
    return r * (dxh - xh * jnp.mean(dxh * xh, axis=-1, keepdims=True)), jnp.sum(dy * xh, axis=0, keepdims=True)


def _mla_pre_bwd(name, proj, dcqn, dckvn, dkr, cos, sin, qg, kvg):
    def fn(p, dcqn, dckvn, dkr, cos, sin, qg, kvg):
        cq, ckv = p[:, :Q_RANK], p[:, Q_RANK:Q_RANK + KV_RANK]
        dcq, dqg = _rms_bwd(dcqn, cq, qg)
        dckv, dkvg = _rms_bwd(dckvn, ckv, kvg)
        dkr_pre = dkr * cos + _rot_t(dkr * sin)
        return jnp.concatenate([dcq, dckv, dkr_pre], axis=1), dqg, dkvg
    return _rowwise(name, fn, [proj, dcqn, dckvn, dkr, cos, sin], [qg, kvg], [(MLA_INK, BF16)], sums=[Q_RANK, KV_RANK])


def _mla_q_fwd(name, q, cos, sin):
    def fn(qn, qr, cos, sin):
        outs = []
        for h in range(NH):
            x = qr[:, h * HD:(h + 1) * HD]
            outs.append(x * cos + _rot(x) * sin)
        return qn, jnp.concatenate(outs, axis=1)
    return _rowwise(name, fn, [(q, NH * HD, 0), (q, NH * HD, 1), cos, sin], [], [(NH * HD, BF16), (NH * HD, BF16)])


def _mla_q_bwd(name, dqn, dqr, cos, sin):
    def fn(dqn, dqr, cos, sin):
        outs = [dqn]
        for h in range(NH):
            z = dqr[:, h * HD:(h + 1) * HD]
            outs.append(z * cos + _rot_t(z * sin))
        return jnp.concatenate(outs, axis=1)
    return _rowwise(name, fn, [dqn, dqr, cos, sin], [], [(2 * NH * HD, BF16)])[0]


def _att_probs(qn, qr, kn, kr, row0):
    s = (_dot(qn, kn, "nt") + _dot(qr, kr, "nt")) * ATT_SCALE
    qpos = row0 + lax.broadcasted_iota(jnp.int32, s.shape, 0)
    kpos = lax.broadcasted_iota(jnp.int32, s.shape, 1)
    s = jnp.where(kpos <= qpos, s, -1e30)
    p = jnp.exp(s - jnp.max(s, axis=1, keepdims=True))
    return p / jnp.sum(p, axis=1, keepdims=True)


def _mla_attn_fwd(name, qn, qr, kv, kr, tq=256):
    T = qn.shape[0]
    tq = min(tq, T)

    def body(qn_ref, qr_ref, kn_ref, v_ref, kr_ref, o_ref):
        p = _att_probs(qn_ref[...], qr_ref[...], kn_ref[...], kr_ref[...], pl.program_id(1) * tq)
        o_ref[...] = _dot(p.astype(BF16), v_ref[...], "nn").astype(o_ref.dtype)

    qspec = pl.BlockSpec((tq, HD), lambda h, i: (i, h))
    return pl.pallas_call(
        body, name=name, grid=(NH, T // tq),
        in_specs=[qspec, qspec, pl.BlockSpec((T, HD), lambda h, i: (0, h)), pl.BlockSpec((T, HD), lambda h, i: (0, NH + h)),
                  pl.BlockSpec((T, HD), lambda h, i: (0, 0))],
        out_specs=qspec, out_shape=jax.ShapeDtypeStruct((T, NH * HD), BF16), compiler_params=_params(2))(qn, qr, kv, kv, kr)


def _mla_attn_bwd(name, qn, qr, kv, kr, do, tq=256):
    T = qn.shape[0]
    tq = min(tq, T)

    def body(qn_ref, qr_ref, kn_ref, v_ref, kr_ref, do_ref, dqn_ref, dqr_ref, dkn_ref, dv_ref, dkr_ref):
        h, i = pl.program_id(0), pl.program_id(1)

        @pl.when(i == 0)
        def _():
            dkn_ref[...] = jnp.zeros_like(dkn_ref)
            dv_ref[...] = jnp.zeros_like(dv_ref)

        @pl.when((i == 0) & (h == 0))
        def _():
            dkr_ref[...] = jnp.zeros_like(dkr_ref)

        qn, qr, kn, kr, v, do = qn_ref[...], qr_ref[...], kn_ref[...], kr_ref[...], v_ref[...], do_ref[...]
        p = _att_probs(qn, qr, kn, kr, i * tq)
        dp = _dot(do, v, "nt")
        ds = (p * (dp - jnp.sum(p * dp, axis=1, keepdims=True)) * ATT_SCALE).astype(BF16)
        dqn_ref[...] = _dot(ds, kn, "nn")
        dqr_ref[...] = _dot(ds, kr, "nn")
        dkn_ref[...] += _dot(ds, qn, "tn")
        dkr_ref[...] += _dot(ds, qr, "tn")
        dv_ref[...] += _dot(p.astype(BF16), do, "tn")

    qspec = pl.BlockSpec((tq, HD), lambda h, i: (i, h))
    kspec = pl.BlockSpec((T, HD), lambda h, i: (0, h))
    return pl.pallas_call(
        body, name=name, grid=(NH, T // tq),
        in_specs=[qspec, qspec, kspec, pl.BlockSpec((T, HD), lambda h, i: (0, NH + h)),
                  pl.BlockSpec((T, HD), lambda h, i: (0, 0)), qspec],
        out_specs=[qspec, qspec, kspec, kspec, pl.BlockSpec((T, HD), lambda h, i: (0, 0))],
        out_shape=[jax.ShapeDtypeStruct((T, NH * HD), F32)] * 4 + [jax.ShapeDtypeStruct((T, HD), F32)],
        compiler_params=_params(2))(qn, qr, kv, kv, kr, do)


def _mod_rows(mod, layer):
    return [mod[layer:layer + 1, i * D:(i + 1) * D] for i in range(N_MOD)]


def _local_step(x, target, pos_col, mod, W, P):
    cos, sin = _rope_tables(pos_col)
    saved = []
    for l in range(DEPTH):
        j = l // 2
        sh_m, sc_m, ga_m, sh_f, sc_f, ga_f = _mod_rows(mod, l)
        s = dict(x0=x)
        h = _norm_mod_fwd(f"norm_mix{l}", x, P["norm_mix_g"][l:l + 1], sc_m, sh_m)
        s["h"] = h
        if l % 2 == 0:
            proj = _mm(f"gdn_in{j}", h, W["gdn_in"][j], "nn")
            qkv = _gdn_conv_fwd(f"gdn_conv{j}", proj, P["gdn_cw"][j])
            g, beta = _gdn_gates_fwd(f"gdn_gates{j}", proj, P["gdn_alog"][j], P["gdn_dtb"][j])
            o, states = _gdn_chunk_fwd(f"gdn_chunk{j}", qkv, g, beta)
            on = _gdn_gated_norm_fwd(f"gdn_gnorm{j}", o, proj, P["gdn_ng"][j])
            y = _mm(f"gdn_out{j}", on, W["gdn_out"][j], "nn")
            s.update(proj=proj, qkv=qkv, g=g, beta=beta, o=o, states=states, on=on)
        else:
            proj = _mm(f"mla_in{j}", h, W["mla_in"][j], "nn")
            cqn, ckvn, kr = _mla_pre_fwd(f"mla_pre{j}", proj, cos, sin, P["mla_qg"][j], P["mla_kvg"][j])
            q = _mm(f"mla_uq{j}", cqn, W["mla_uq"][j], "nn")
            kv = _mm(f"mla_ukv{j}", ckvn, W["mla_ukv"][j], "nn", out_dtype=BF16)
            qn, qr = _mla_q_fwd(f"mla_q{j}", q, cos, sin)
            o = _mla_attn_fwd(f"mla_attn{j}", qn, qr, kv, kr)
            y = _mm(f"mla_out{j}", o, W["mla_out"][j], "nn")
            s.update(proj=proj, cqn=cqn, ckvn=ckvn, kr=kr, kv=kv, qn=qn, qr=qr, o=o)
        s["y"] = y
        x = _residual_fwd(f"res_mix{l}", x, y, ga_m)
        s["x1"] = x
        h2 = _norm_mod_fwd(f"norm_ffn{l}", x, P["norm_ffn_g"][l:l + 1], sc_f, sh_f)
        ab = _mm(f"ffn_gu{l}", h2, W["ffn_gu"][l], "nn")
        sw = _swiglu_fwd(f"ffn_act{l}", ab)
        yf = _mm(f"ffn_down{l}", sw, W["ffn_d"][l], "nn")
        x = _residual_fwd(f"res_ffn{l}", x, yf, ga_f)
        s.update(h2=h2, ab=ab, sw=sw, yf=yf)
        saved.append(s)

    dx, loss, d_final = _loss_head(x, target, P["final_g"])
    gW = {k: [None] * len(v) for k, v in W.items()}
    gP = dict(loss=loss, final_g=d_final, norm_mix_g=[None] * DEPTH, norm_ffn_g=[None] * DEPTH,
              gdn_cw=[None] * 2, gdn_alog=[None] * 2, gdn_dtb=[None] * 2, gdn_ng=[None] * 2,
              mla_qg=[None] * 2, mla_kvg=[None] * 2)
    dmod = [None] * DEPTH
    for l in reversed(range(DEPTH)):
        j = l // 2
        s = saved[l]
        sh_m, sc_m, ga_m, sh_f, sc_f, ga_f = _mod_rows(mod, l)
        dyf, d_ga_f = _residual_bwd(f"res_ffn_b{l}", dx, s["yf"], ga_f)
        dsw = _mm(f"ffn_down_dx{l}", dyf, W["ffn_d"][l], "nt")
        gW["ffn_d"][l] = _mm(f"ffn_down_dw{l}", s["sw"], dyf, "tn", out_dtype=BF16)
        dab = _swiglu_bwd(f"ffn_act_b{l}", dsw, s["ab"])
        gW["ffn_gu"][l] = _mm(f"ffn_gu_dw{l}", s["h2"], dab, "tn", out_dtype=BF16)
        dh2 = _mm(f"ffn_gu_dx{l}", dab, W["ffn_gu"][l], "nt")
        dx, d_sh_f, d_sc_f, gP["norm_ffn_g"][l] = _norm_mod_bwd(f"norm_ffn_b{l}", dh2, s["x1"], dx,
                                                                 P["norm_ffn_g"][l:l + 1], sc_f)
        dy, d_ga_m = _residual_bwd(f"res_mix_b{l}", dx, s["y"], ga_m)
        if l % 2 == 0:
            don = _mm(f"gdn_out_dx{j}", dy, W["gdn_out"][j], "nt")
            gW["gdn_out"][j] = _mm(f"gdn_out_dw{j}", s["on"], dy, "tn", out_dtype=BF16)
            do, dgate, gP["gdn_ng"][j] = _gdn_gated_norm_bwd(f"gdn_gnorm_b{j}", don, s["o"], s["proj"], P["gdn_ng"][j])
            dq, dk, dv, dg_h, db_h = _gdn_chunk_bwd(f"gdn_chunk_b{j}", s["qkv"], s["g"], s["beta"], s["states"], do)
            dab_, gP["gdn_alog"][j], gP["gdn_dtb"][j] = _gdn_gates_bwd(f"gdn_gates_b{j}", s["proj"], dg_h, db_h,
                                                                        P["gdn_alog"][j], P["gdn_dtb"][j])
            dpre, gP["gdn_cw"][j] = _gdn_conv_bwd(f"gdn_conv_b{j}", s["proj"], P["gdn_cw"][j],
                                                  jnp.concatenate([dq, dk, dv], axis=1))
            dproj = jnp.concatenate([dpre, dgate, dab_], axis=1)
            gW["gdn_in"][j] = _mm(f"gdn_in_dw{j}", s["h"], dproj, "tn", out_dtype=BF16)
            dh = _mm(f"gdn_in_dx{j}", dproj, W["gdn_in"][j], "nt")
        else:
            do = _mm(f"mla_out_dx{j}", dy, W["mla_out"][j], "nt", out_dtype=BF16)
            gW["mla_out"][j] = _mm(f"mla_out_dw{j}", s["o"], dy, "tn", out_dtype=BF16)
            dqn, dqr, dkn, dv, dkr = _mla_attn_bwd(f"mla_attn_b{j}", s["qn"], s["qr"], s["kv"], s["kr"], do)
            dq = _mla_q_bwd(f"mla_q_b{j}", dqn, dqr, cos, sin)
            dkv = jnp.concatenate([dkn, dv], axis=1)
            gW["mla_uq"][j] = _mm(f"mla_uq_dw{j}", s["cqn"], dq, "tn", out_dtype=BF16)
            dcqn = _mm(f"mla_uq_dx{j}", dq, W["mla_uq"][j], "nt")
            gW["mla_ukv"][j] = _mm(f"mla_ukv_dw{j}", s["ckvn"], dkv, "tn", out_dtype=BF16)
            dckvn = _mm(f"mla_ukv_dx{j}", dkv, W["mla_ukv"][j], "nt")
            dproj, gP["mla_qg"][j], gP["mla_kvg"][j] = _mla_pre_bwd(f"mla_pre_b{j}", s["proj"], dcqn, dckvn, dkr, cos, sin,
                                                                     P["mla_qg"][j], P["mla_kvg"][j])
            gW["mla_in"][j] = _mm(f"mla_in_dw{j}", s["h"], dproj, "tn", out_dtype=BF16)
            dh = _mm(f"mla_in_dx{j}", dproj, W["mla_in"][j], "nt")
        dx, d_sh_m, d_sc_m, gP["norm_mix_g"][l] = _norm_mod_bwd(f"norm_mix_b{l}", dh, s["x0"], dx,
                                                                 P["norm_mix_g"][l:l + 1], sc_m)
        dmod[l] = jnp.concatenate([d_sh_m, d_sc_m, d_ga_m, d_sh_f, d_sc_f, d_ga_f], axis=1)
    return dx, jnp.concatenate(dmod, axis=0), gW, gP


def _pad_cols(a, width):
    return jnp.pad(a, ((0, 0), (0, width - a.shape[1])))


def _gdn_in_to_kernel(w):
    m = GDN_QKV + NH * HD
    return jnp.concatenate([w[:, :m], _pad_cols(w[:, m:m + NH], HD), _pad_cols(w[:, m + NH:], HD)], axis=1)


def _gdn_in_from_kernel(g):
    m = GDN_QKV + NH * HD
    return jnp.concatenate([g[:, :m], g[:, m:m + NH], g[:, m + HD:m + HD + NH]], axis=1)


def _mla_uq_to_kernel(w):
    w3 = w.reshape(Q_RANK, NH, HD + ROPE)
    rope = jnp.pad(w3[:, :, HD:], ((0, 0), (0, 0), (0, HD - ROPE)))
    return jnp.concatenate([w3[:, :, :HD].reshape(Q_RANK, NH * HD), rope.reshape(Q_RANK, NH * HD)], axis=1)


def _mla_uq_from_kernel(g):
    gn = g[:, :NH * HD].reshape(Q_RANK, NH, HD)
    gr = g[:, NH * HD:].reshape(Q_RANK, NH, HD)[:, :, :ROPE]
    return jnp.concatenate([gn, gr], axis=2).reshape(Q_RANK, NH * (HD + ROPE))


def _mla_ukv_to_kernel(w):
    w3 = w.reshape(KV_RANK, NH, 2 * HD)
    return jnp.concatenate([w3[:, :, :HD].reshape(KV_RANK, NH * HD), w3[:, :, HD:].reshape(KV_RANK, NH * HD)], axis=1)


def _mla_ukv_from_kernel(g):
    gk = g[:, :NH * HD].reshape(KV_RANK, NH, HD)
    gv = g[:, NH * HD:].reshape(KV_RANK, NH, HD)
    return jnp.concatenate([gk, gv], axis=2).reshape(KV_RANK, NH * 2 * HD)


def _weights_to_kernel(full):
    return dict(
        gdn_in=[_gdn_in_to_kernel(full["gdn_w_in"][j]) for j in range(2)],
        gdn_out=[full["gdn_w_out"][j] for j in range(2)],
        mla_in=[_pad_cols(full["mla_w_in"][j], MLA_INK) for j in range(2)],
        mla_uq=[_mla_uq_to_kernel(full["mla_w_uq"][j]) for j in range(2)],
        mla_ukv=[_mla_ukv_to_kernel(full["mla_w_ukv"][j]) for j in range(2)],
        mla_out=[full["mla_w_out"][j] for j in range(2)],
        ffn_gu=[jnp.concatenate([full["ffn_w_gate"][l], full["ffn_w_up"][l]], axis=1) for l in range(DEPTH)],
        ffn_d=[full["ffn_w_down"][l] for l in range(DEPTH)],
    )


def _grads_from_kernel(gW):
    return dict(
        gdn_w_in=jnp.stack([_gdn_in_from_kernel(g) for g in gW["gdn_in"]]),
        gdn_w_out=jnp.stack(gW["gdn_out"]),
        mla_w_in=jnp.stack([g[:, :Q_RANK + KV_RANK + ROPE] for g in gW["mla_in"]]),
        mla_w_uq=jnp.stack([_mla_uq_from_kernel(g) for g in gW["mla_uq"]]),
        mla_w_ukv=jnp.stack([_mla_ukv_from_kernel(g) for g in gW["mla_ukv"]]),
        mla_w_out=jnp.stack(gW["mla_out"]),
        ffn_w_gate=jnp.stack([g[:, :DFF] for g in gW["ffn_gu"]]),
        ffn_w_up=jnp.stack([g[:, DFF:] for g in gW["ffn_gu"]]),
        ffn_w_down=jnp.stack(gW["ffn_d"]),
    )


def _small_to_kernel(norm_mix_g, norm_ffn_g, final_norm_g, gdn_conv_w, gdn_a_log, gdn_dt_bias, gdn_norm_g, q_norm_g, kv_norm_g):
    return dict(
        norm_mix_g=norm_mix_g, norm_ffn_g=norm_ffn_g, final_g=final_norm_g.reshape(1, D),
        gdn_cw=[jnp.transpose(gdn_conv_w[j]) for j in range(2)],
        gdn_alog=[_pad_cols(gdn_a_log[j:j + 1], HD) for j in range(2)],
        gdn_dtb=[_pad_cols(gdn_dt_bias[j:j + 1], HD) for j in range(2)],
        gdn_ng=[gdn_norm_g[j:j + 1] for j in range(2)],
        mla_qg=[q_norm_g[j:j + 1] for j in range(2)],
        mla_kvg=[kv_norm_g[j:j + 1] for j in range(2)],
    )


_CHIP_FLIPS = ((1, 0), (0, 1), (1, 1))
_ANY = pl.BlockSpec(memory_space=pl.ANY)


def _me():
    return lax.axis_index("x"), lax.axis_index("y"), lax.axis_index("c")


def _chip_peer(dx, dy):
    x, y, c = _me()
    return ((1 - x) if dx else x, (1 - y) if dy else y, c)


def _rcopy(src, dst, send_sem, recv_sem, to):
    return pltpu.make_async_remote_copy(src_ref=src, dst_ref=dst, send_sem=send_sem, recv_sem=recv_sem,
                                        device_id=to, device_id_type=MESH)


def _allgather4(name, a, halves=False):
    R, C = a.shape
    rh = R // 2 if halves else R

    def body(a_ref, out_ref, send_sems, recv_sems, local_sem):
        x, y, c = _me()
        me = 2 * x + y
        src = a_ref.at[pl.ds(c * rh, rh)] if halves else a_ref
        local = pltpu.make_async_copy(src, out_ref.at[me], local_sem)
        local.start()
        sends = []
        for k, (dx, dy) in enumerate(_CHIP_FLIPS):
            cp = _rcopy(src, out_ref.at[me], send_sems.at[k], recv_sems.at[k], _chip_peer(dx, dy))
            cp.start()
            sends.append(cp)
        for k, (dx, dy) in enumerate(_CHIP_FLIPS):
            px, py, _ = _chip_peer(dx, dy)
            _rcopy(src, out_ref.at[2 * px + py], send_sems.at[k], recv_sems.at[k], _chip_peer(dx, dy)).wait_recv()
        for cp in sends:
            cp.wait_send()
        local.wait()

    return pl.pallas_call(
        body, name=name, in_specs=[_ANY], out_specs=_ANY, out_shape=jax.ShapeDtypeStruct((4, rh, C), a.dtype),
        scratch_shapes=[pltpu.SemaphoreType.DMA((3,)), pltpu.SemaphoreType.DMA((3,)), pltpu.SemaphoreType.DMA(())])(a)


def _alltoall4(name, p):
    def body(p_ref, out_ref, send_sems, recv_sems, local_sem):
        x, y, c = _me()
        me = 2 * x + y
        local = pltpu.make_async_copy(p_ref.at[me], out_ref.at[me], local_sem)
        local.start()
        sends = []
        for k, (dx, dy) in enumerate(_CHIP_FLIPS):
            px, py, _ = _chip_peer(dx, dy)
            cp = _rcopy(p_ref.at[2 * px + py], out_ref.at[me], send_sems.at[k], recv_sems.at[k], _chip_peer(dx, dy))
            cp.start()
            sends.append(cp)
        for k, (dx, dy) in enumerate(_CHIP_FLIPS):
            px, py, _ = _chip_peer(dx, dy)
            _rcopy(p_ref.at[me], out_ref.at[2 * px + py], send_sems.at[k], recv_sems.at[k], _chip_peer(dx, dy)).wait_recv()
        for cp in sends:
            cp.wait_send()
        local.wait()

    return pl.pallas_call(
        body, name=name, in_specs=[_ANY], out_specs=_ANY, out_shape=jax.ShapeDtypeStruct(p.shape, p.dtype),
        scratch_shapes=[pltpu.SemaphoreType.DMA((3,)), pltpu.SemaphoreType.DMA((3,)), pltpu.SemaphoreType.DMA(())])(p)


def _sibling_split(name, g):
    P_, R, C = g.shape
    rh = R // 2

    def body(g_ref, mine_ref, theirs_ref, send_sem, recv_sem, local_sem):
        x, y, c = _me()
        local = pltpu.make_async_copy(g_ref.at[:, pl.ds(c * rh, rh)], mine_ref, local_sem)
        local.start()
        cp = _rcopy(g_ref.at[:, pl.ds((1 - c) * rh, rh)], theirs_ref, send_sem, recv_sem, (x, y, 1 - c))
        cp.start()
        cp.wait()
        local.wait()

    half = jax.ShapeDtypeStruct((P_, rh, C), g.dtype)
    return pl.pallas_call(
        body, name=name, in_specs=[_ANY], out_specs=[_ANY, _ANY], out_shape=[half, half],
        scratch_shapes=[pltpu.SemaphoreType.DMA(()), pltpu.SemaphoreType.DMA(()), pltpu.SemaphoreType.DMA(())])(g)


def _sibling_merge(name, a):
    P_, rh, C = a.shape

    def body(a_ref, out_ref, send_sem, recv_sem, local_sem):
        x, y, c = _me()
        local = pltpu.make_async_copy(a_ref, out_ref.at[:, pl.ds(c * rh, rh)], local_sem)
        local.start()
        cp = _rcopy(a_ref, out_ref.at[:, pl.ds(c * rh, rh)], send_sem, recv_sem, (x, y, 1 - c))
        cp.start()
        cp.wait_send()
        _rcopy(a_ref, out_ref.at[:, pl.ds((1 - c) * rh, rh)], send_sem, recv_sem, (x, y, 1 - c)).wait_recv()
        local.wait()

    return pl.pallas_call(
        body, name=name, in_specs=[_ANY], out_specs=_ANY, out_shape=jax.ShapeDtypeStruct((P_, 2 * rh, C), a.dtype),
        scratch_shapes=[pltpu.SemaphoreType.DMA(()), pltpu.SemaphoreType.DMA(()), pltpu.SemaphoreType.DMA(())])(a)


def _allgather8(name, a):
    g4 = _allgather4(name + "_chips", a)
    both = _sibling_merge(name + "_cores", g4.reshape(1, 4 * a.shape[0], a.shape[1]))
    return jnp.transpose(both.reshape(2, 4, *a.shape), (1, 0, 2, 3)).reshape(8, *a.shape)


def _sum_slots(name, a, out_dtype):
    def fn(a):
        acc = a[0].astype(F32)
        for k in range(1, a.shape[0]):
            acc = acc + a[k].astype(F32)
        return acc
    return _rowwise(name, fn, [a], [], [(a.shape[2], out_dtype)])[0]


def _add2(name, a, b, out_dtype):
    def fn(a, b):
        return a.astype(F32) + b.astype(F32)
    return _rowwise(name, fn, [a, b], [], [(a.shape[1], out_dtype)])[0]


def _cast_bf16(name, a):
    return _rowwise(name, lambda a: a, [a], [], [(a.shape[1], BF16)])[0]


def _adamw(name, w, g, m, v):
    shape = w.shape
    two_d = (-1, shape[-1]) if w.ndim > 1 else (1, -1)
    w2, g2, m2, v2 = [t.reshape(two_d) for t in (w, g, m, v)]
    rows = w2.shape[0]
    tr = rows
    for cand in (256, 128, 64, 32, 16, 8):
        if rows % cand == 0:
            tr = cand
            break

    def fn(w, g, m, v):
        m = ADAM_B1 * m + (1.0 - ADAM_B1) * g
        v = ADAM_B2 * v + (1.0 - ADAM_B2) * (g * g)
        m_hat = m / (1.0 - ADAM_B1 ** ADAM_STEP)
        v_hat = v / (1.0 - ADAM_B2 ** ADAM_STEP)
        return -ADAM_LR * (m_hat / (jnp.sqrt(v_hat) + ADAM_EPS) + ADAM_WD * w), m, v

    c = w2.shape[1]
    outs = _rowwise(name, fn, [w2, g2, m2, v2], [], [(c, F32)] * 3, tr=tr)
    return [o.reshape(shape) for o in outs]


_WEIGHT_ORDER = ("ada_w", "ada_b", "norm_mix_g", "norm_ffn_g", "gdn_w_in", "gdn_conv_w", "gdn_a_log", "gdn_dt_bias",
                 "gdn_norm_g", "gdn_w_out", "mla_w_in", "mla_q_norm_g", "mla_kv_norm_g", "mla_w_uq", "mla_w_ukv",
                 "mla_w_out", "ffn_w_gate", "ffn_w_up", "ffn_w_down", "final_norm_g")
_BIG = (("gdn_w_in", 2), ("gdn_w_out", 1), ("mla_w_in", 1), ("mla_w_uq", 2), ("mla_w_ukv", 2), ("mla_w_out", 1),
        ("ffn_w_gate", 2), ("ffn_w_up", 2), ("ffn_w_down", 1))
_SMALL_SHARDED = (("gdn_conv_w", 1), ("mla_q_norm_g", 1), ("mla_kv_norm_g", 1))
_BIG_ROW_MULTIPLE = 512


def _size(shape):
    n = 1
    for s in shape:
        n *= s
    return n


def _pack_flat(tensors, row_multiple):
    flat = jnp.concatenate([t.reshape(-1) for t in tensors])
    rows = -(-flat.shape[0] // PACK_W)
    rows = -(-rows // row_multiple) * row_multiple
    return jnp.pad(flat, (0, rows * PACK_W - flat.shape[0])).reshape(rows, PACK_W)


def _pack_flat_lead(tensors, row_multiple):
    n = tensors[0].shape[0]
    flat = jnp.concatenate([t.reshape(n, -1) for t in tensors], axis=1)
    rows = -(-flat.shape[1] // PACK_W)
    rows = -(-rows // row_multiple) * row_multiple
    return jnp.pad(flat, ((0, 0), (0, rows * PACK_W - flat.shape[1]))).reshape(n, rows, PACK_W)


def _unpack_flat(pack, shapes):
    lead = pack.shape[:-2]
    flat = pack.reshape(*lead, -1)
    out, off = [], 0
    for shp in shapes:
        n = _size(shp)
        out.append(flat[..., off:off + n].reshape(*lead, *shp))
        off += n
    return out


def _pack_rows_each(tensors):
    parts, offs, off = [], [], 0
    for t in tensors:
        flat = t.reshape(-1).astype(F32)
        rows = -(-flat.shape[0] // PACK_W)
        parts.append(jnp.pad(flat, (0, rows * PACK_W - flat.shape[0])).reshape(rows, PACK_W))
        offs.append(off)
        off += rows
    pad = -(-off // 16) * 16 - off
    if pad:
        parts.append(jnp.zeros((pad, PACK_W), F32))
    return jnp.concatenate(parts, axis=0), offs


def _unpack_rows_each(pack, shapes):
    lead = pack.shape[:-2]
    out, off = [], 0
    for shp in shapes:
        n = _size(shp)
        rows = -(-n // PACK_W)
        out.append(pack[..., off:off + rows, :].reshape(*lead, -1)[..., :n].reshape(*lead, *shp))
        off += rows
    return out


def _merge_chips(stacked, axis):
    moved = jnp.moveaxis(stacked, 0, axis)
    shp = list(moved.shape)
    return moved.reshape(shp[:axis] + [shp[axis] * shp[axis + 1]] + shp[axis + 2:])


def _split_chips(full, axis):
    shp = list(full.shape)
    split = full.reshape(shp[:axis] + [4, shp[axis] // 4] + shp[axis + 1:])
    return jnp.moveaxis(split, axis, 0)


def _my_shard(full, axis, chip):
    n = full.shape[axis] // 4
    return lax.dynamic_slice_in_dim(full, chip * n, n, axis)


def kernel(x, c, positions, ada_w, ada_b, norm_mix_g, norm_ffn_g, gdn_w_in, gdn_conv_w, gdn_a_log, gdn_dt_bias, gdn_norm_g, gdn_w_out, mla_w_in, mla_q_norm_g, mla_kv_norm_g, mla_w_uq, mla_w_ukv, mla_w_out, ffn_w_gate, ffn_w_up, ffn_w_down, final_norm_g, loss_target, m_ada_w, m_ada_b, m_norm_mix_g, m_norm_ffn_g, m_gdn_w_in, m_gdn_conv_w, m_gdn_a_log, m_gdn_dt_bias, m_gdn_norm_g, m_gdn_w_out, m_mla_w_in, m_mla_q_norm_g, m_mla_kv_norm_g, m_mla_w_uq, m_mla_w_ukv, m_mla_w_out, m_ffn_w_gate, m_ffn_w_up, m_ffn_w_down, m_final_norm_g, v_ada_w, v_ada_b, v_norm_mix_g, v_norm_ffn_g, v_gdn_w_in, v_gdn_conv_w, v_gdn_a_log, v_gdn_dt_bias, v_gdn_norm_g, v_gdn_w_out, v_mla_w_in, v_mla_q_norm_g, v_mla_kv_norm_g, v_mla_w_uq, v_mla_w_ukv, v_mla_w_out, v_ffn_w_gate, v_ffn_w_up, v_ffn_w_down, v_final_norm_g):
    w = dict(ada_w=ada_w, ada_b=ada_b, norm_mix_g=norm_mix_g, norm_ffn_g=norm_ffn_g, gdn_w_in=gdn_w_in, gdn_conv_w=gdn_conv_w,
             gdn_a_log=gdn_a_log, gdn_dt_bias=gdn_dt_bias, gdn_norm_g=gdn_norm_g, gdn_w_out=gdn_w_out, mla_w_in=mla_w_in,
             mla_q_norm_g=mla_q_norm_g, mla_kv_norm_g=mla_kv_norm_g, mla_w_uq=mla_w_uq, mla_w_ukv=mla_w_ukv,
             mla_w_out=mla_w_out, ffn_w_gate=ffn_w_gate, ffn_w_up=ffn_w_up, ffn_w_down=ffn_w_down, final_norm_g=final_norm_g)
    m = dict(ada_w=m_ada_w, ada_b=m_ada_b, norm_mix_g=m_norm_mix_g, norm_ffn_g=m_norm_ffn_g, gdn_w_in=m_gdn_w_in,
             gdn_conv_w=m_gdn_conv_w, gdn_a_log=m_gdn_a_log, gdn_dt_bias=m_gdn_dt_bias, gdn_norm_g=m_gdn_norm_g,
             gdn_w_out=m_gdn_w_out, mla_w_in=m_mla_w_in, mla_q_norm_g=m_mla_q_norm_g, mla_kv_norm_g=m_mla_kv_norm_g,
             mla_w_uq=m_mla_w_uq, mla_w_ukv=m_mla_w_ukv, mla_w_out=m_mla_w_out, ffn_w_gate=m_ffn_w_gate,
             ffn_w_up=m_ffn_w_up, ffn_w_down=m_ffn_w_down, final_norm_g=m_final_norm_g)
    v = dict(ada_w=v_ada_w, ada_b=v_ada_b, norm_mix_g=v_norm_mix_g, norm_ffn_g=v_norm_ffn_g, gdn_w_in=v_gdn_w_in,
             gdn_conv_w=v_gdn_conv_w, gdn_a_log=v_gdn_a_log, gdn_dt_bias=v_gdn_dt_bias, gdn_norm_g=v_gdn_norm_g,
             gdn_w_out=v_gdn_w_out, mla_w_in=v_mla_w_in, mla_q_norm_g=v_mla_q_norm_g, mla_kv_norm_g=v_mla_kv_norm_g,
             mla_w_uq=v_mla_w_uq, mla_w_ukv=v_mla_w_ukv, mla_w_out=v_mla_w_out, ffn_w_gate=v_ffn_w_gate,
             ffn_w_up=v_ffn_w_up, ffn_w_down=v_ffn_w_down, final_norm_g=v_final_norm_g)
    T = x.shape[1]
    ix, iy, ic = _me()
    chip = 2 * ix + iy
    seq = 2 * chip + ic
    n_dev = 8

    small_shapes = [w[n].shape for n, _ in _SMALL_SHARDED] + [c.shape]
    pack0, _ = _pack_rows_each([w[n] for n, _ in _SMALL_SHARDED] + [c])
    got0 = _unpack_rows_each(_allgather8("gather_small", pack0), small_shapes)
    small_full = {n: _merge_chips(g[0::2], ax) for (n, ax), g in zip(_SMALL_SHARDED, got0)}
    c_all = got0[-1].reshape(n_dev, D)

    big_shapes = [w[n].shape for n, _ in _BIG]
    wpack = _cast_bf16("weights_to_bf16", _pack_flat([w[n] for n, _ in _BIG], _BIG_ROW_MULTIPLE))
    wgot = _sibling_merge("weights_cores", _allgather4("weights_chips", wpack, halves=True))
    full = {n: _merge_chips(t, ax) for (n, ax), t in zip(_BIG, _unpack_flat(wgot, big_shapes))}
    W = _weights_to_kernel(full)
    P = _small_to_kernel(norm_mix_g, norm_ffn_g, final_norm_g, small_full["gdn_conv_w"], gdn_a_log, gdn_dt_bias,
                         gdn_norm_g, small_full["mla_q_norm_g"], small_full["mla_kv_norm_g"])

    c16 = jnp.pad(c_all, ((0, 16 - n_dev), (0, 0)))
    ca = _rowwise("cond_silu", lambda t: t * _sig(t), [c16], [], [(D, BF16)])[0]
    n_ada = ada_w.shape[2]
    mods = jnp.concatenate([_mm(f"ada_fwd{l}", ca, ada_w[l], "nn") for l in range(DEPTH)], axis=0)
    mods_all = _allgather4("gather_mod", mods).reshape(4, DEPTH, 16, n_ada)
    mod_mm = jnp.transpose(lax.dynamic_index_in_dim(mods_all, seq, axis=2, keepdims=False), (1, 0, 2)).reshape(DEPTH, 4 * n_ada)
    mod = _rowwise("mod_bias", lambda a, b: a + b, [mod_mm, ada_b], [], [(4 * n_ada, F32)])[0]

    dx, dmod, gW, gP = _local_step(x.reshape(T, D), loss_target.reshape(T, D), positions.reshape(T, 1), mod, W, P)

    partials = [dmod, jnp.concatenate(gP["norm_mix_g"]), jnp.concatenate(gP["norm_ffn_g"]), gP["final_g"],
                jnp.stack([jnp.transpose(g) for g in gP["gdn_cw"]]), jnp.concatenate(gP["gdn_alog"])[:, :NH],
                jnp.concatenate(gP["gdn_dtb"])[:, :NH], jnp.concatenate(gP["gdn_ng"]), jnp.concatenate(gP["mla_qg"]),
                jnp.concatenate(gP["mla_kvg"]), gP["loss"][:, :1]]
    part_shapes = [p.shape for p in partials]
    ppack, _ = _pack_rows_each(partials)
    pall = _allgather8("gather_partials", ppack)
    psum = _sum_slots("sum_partials", pall, F32)
    (g_ada_b, g_norm_mix, g_norm_ffn, g_final, g_conv_full, g_alog, g_dtb, g_gdn_ng, g_qg_full, g_kvg_full,
     loss_sum) = _unpack_rows_each(psum, part_shapes)
    dmod_all = _unpack_rows_each(pall, part_shapes[:1])[0]

    grads = dict(ada_b=g_ada_b, norm_mix_g=g_norm_mix, norm_ffn_g=g_norm_ffn, final_norm_g=g_final.reshape(D),
                 gdn_conv_w=_my_shard(g_conv_full, 1, chip), gdn_a_log=g_alog, gdn_dt_bias=g_dtb, gdn_norm_g=g_gdn_ng,
                 mla_q_norm_g=_my_shard(g_qg_full, 1, chip), mla_kv_norm_g=_my_shard(g_kvg_full, 1, chip))

    ca_t = jnp.zeros((D, LANES), BF16).at[:, :16].set(jnp.transpose(ca))
    dm_mine = lax.dynamic_slice_in_dim(dmod_all, chip * n_ada, n_ada, axis=2)
    grads["ada_w"] = jnp.stack([
        _mm(f"ada_bwd{l}", ca_t, jnp.pad(dm_mine[:, l], ((0, LANES - n_dev), (0, 0))), "nn") for l in range(DEPTH)])

    gnat = _grads_from_kernel(gW)
    gpack = _pack_flat_lead([_split_chips(gnat[n], ax) for n, ax in _BIG], _BIG_ROW_MULTIPLE)
    mine, theirs = _sibling_split("grads_cores", gpack)
    rows_h = mine.shape[1]
    pair = _add2("grads_pair_sum", mine.reshape(4 * rows_h, PACK_W), theirs.reshape(4 * rows_h, PACK_W), BF16)
    swapped = _alltoall4("grads_chips", pair.reshape(4, rows_h, PACK_W))
    gshard = _sibling_merge("grads_merge", _sum_slots("grads_chip_sum", swapped, F32)[None])[0]
    for (n, _), t in zip(_BIG, _unpack_flat(gshard, big_shapes)):
        grads[n] = t

    delta, new_m, new_v = {}, {}, {}
    for n in ("ada_w",) + tuple(n for n, _ in _BIG):
        delta[n], new_m[n], new_v[n] = _adamw("adamw_" + n, w[n], grads[n], m[n], v[n])
    small_names = [n for n in _WEIGHT_ORDER if n not in delta]
    small_shapes = [w[n].shape for n in small_names]
    packs = [_pack_rows_each([d[n] for n in small_names])[0] for d in (w, grads, m, v)]
    for d, pk in zip((delta, new_m, new_v), _adamw("adamw_small", *packs)):
        for n, t in zip(small_names, _unpack_rows_each(pk, small_shapes)):
            d[n] = t

    loss = loss_sum.reshape(())
    return (loss, dx.reshape(1, T, D), *[grads[n] for n in _WEIGHT_ORDER], *[delta[n] for n in _WEIGHT_ORDER],
            *[new_m[n] for n in _WEIGHT_ORDER], *[new_v[n] for n in _WEIGHT_ORDER])
```

```python
import functools

import jax
import jax.numpy as jnp
from jax import lax
from jax.experimental import pallas as pl
from jax.experimental.pallas import tpu as pltpu

F32 = jnp.float32
BF16 = jnp.bfloat16
HI = lax.Precision.HIGHEST
MESH = pl.DeviceIdType.MESH

D = 1024
DEPTH = 4
N_MOD = 6
NH = 8
HD = 128
CHUNK = 64
GDN_QKV = 3 * NH * HD
GDN_INK = GDN_QKV + NH * HD + 2 * HD
Q_RANK, KV_RANK, ROPE = 384, 256, 64
MLA_INK = Q_RANK + KV_RANK + HD
DFF = 2816
EPS = 1e-6
ATT_SCALE = (HD + ROPE) ** -0.5
ROPE_THETA = 10000.0
LANES = 128
PACK_W = 1024

ADAM_LR, ADAM_B1, ADAM_B2, ADAM_EPS, ADAM_WD, ADAM_STEP = 0.001, 0.9, 0.999, 1e-08, 0.01, 10


def _dot(a, b, mode="nn", prec=None):
    dn = {"nn": (((1,), (0,)), ((), ())), "nt": (((1,), (1,)), ((), ())), "tn": (((0,), (0,)), ((), ()))}[mode]
    return lax.dot_general(a, b, dn, precision=prec, preferred_element_type=F32)


def _sig(x):
    return 1.0 / (1.0 + jnp.exp(-x))


def _pick(n, cap):
    if n <= cap:
        return n
    best = None
    for d in range(LANES, cap + 1, LANES):
        if n % d == 0:
            best = d
    assert best is not None, (n, cap)
    return best


def _params(n_grid):
    return pltpu.CompilerParams(dimension_semantics=("arbitrary",) * n_grid, vmem_limit_bytes=56 * 1024 * 1024)


def _rowwise(name, fn, rows, consts, outs, sums=(), tr=256):
    first = rows[0][0] if isinstance(rows[0], tuple) else rows[0]
    T = first.shape[-2]
    tr = min(tr, T)
    while T % tr:
        tr //= 2
    nr, nc, no, ns = len(rows), len(consts), len(outs), len(sums)

    def body(*refs):
        res = fn(*[r[...] for r in refs[:nr + nc]])
        if not isinstance(res, (tuple, list)):
            res = (res,)
        o_refs = refs[nr + nc:nr + nc + no]
        s_refs = refs[nr + nc + no:]
        for r, val in zip(o_refs, res[:no]):
            r[...] = val.astype(r.dtype)
        if ns:
            @pl.when(pl.program_id(0) == 0)
            def _():
                for r in s_refs:
                    r[...] = jnp.zeros_like(r)
            for r, val in zip(s_refs, res[no:]):
                r[...] += val

    in_specs, args = [], []
    for a in rows:
        if isinstance(a, tuple):
            arr, width, cb = a
            in_specs.append(pl.BlockSpec((tr, width), lambda i, cb=cb: (i, cb)))
            args.append(arr)
        elif a.ndim == 3:
            in_specs.append(pl.BlockSpec((a.shape[0], tr, a.shape[2]), lambda i: (0, i, 0)))
            args.append(a)
        else:
            in_specs.append(pl.BlockSpec((tr, a.shape[1]), lambda i: (i, 0)))
            args.append(a)
    for a in consts:
        in_specs.append(pl.BlockSpec(a.shape, lambda i, nd=a.ndim: (0,) * nd))
        args.append(a)
    out_specs = [pl.BlockSpec((tr, w), lambda i: (i, 0)) for w, _ in outs]
    out_specs += [pl.BlockSpec((1, w), lambda i: (0, 0)) for w in sums]
    out_shape = [jax.ShapeDtypeStruct((T, w), dt) for w, dt in outs]
    out_shape += [jax.ShapeDtypeStruct((1, w), F32) for w in sums]
    res = pl.pallas_call(body, name=name, grid=(T // tr,), in_specs=in_specs, out_specs=out_specs,
                         out_shape=out_shape, compiler_params=_params(1))(*args)
    return res


def _mm(name, a, b, mode, out_dtype=F32, tm=512, tn=1024):
    if mode == "tn":
        K, M = a.shape
    else:
        M, K = a.shape
    N = b.shape[0] if mode == "nt" else b.shape[1]
    tm, tn = _pick(M, tm), _pick(N, tn)

    def body(a_ref, b_ref, o_ref):
        o_ref[...] = _dot(a_ref[...].astype(BF16), b_ref[...].astype(BF16), mode).astype(o_ref.dtype)

    a_spec = pl.BlockSpec((K, tm), lambda i, j: (0, i)) if mode == "tn" else pl.BlockSpec((tm, K), lambda i, j: (i, 0))
    b_spec = pl.BlockSpec((tn, K), lambda i, j: (j, 0)) if mode == "nt" else pl.BlockSpec((K, tn), lambda i, j: (0, j))
    return pl.pallas_call(body, name=name, grid=(M // tm, N // tn), in_specs=[a_spec, b_spec],
                          out_specs=pl.BlockSpec((tm, tn), lambda i, j: (i, j)),
                          out_shape=jax.ShapeDtypeStruct((M, N), out_dtype), compiler_params=_params(2))(a, b)


def _rms(x, eps=EPS):
    return lax.rsqrt(jnp.mean(x * x, axis=-1, keepdims=True) + eps)


def _norm_mod_fwd(name, x, g, scale, shift):
    def fn(x, g, scale, shift):
        return x * _rms(x) * g * (1.0 + scale) + shift
    return _rowwise(name, fn, [x], [g, scale, shift], [(D, BF16)])[0]


def _norm_mod_bwd(name, dh, x, dx_res, g, scale):
    def fn(dh, x, dx_res, g, scale):
        r = _rms(x)
        xh = x * r
        dxh = dh * (g * (1.0 + scale))
        dx = r * (dxh - xh * jnp.mean(dxh * xh, axis=-1, keepdims=True))
        dhx = dh * xh
        return (dx_res + dx, jnp.sum(dh, axis=0, keepdims=True), jnp.sum(dhx * g, axis=0, keepdims=True),
                jnp.sum(dhx * (1.0 + scale), axis=0, keepdims=True))
    return _rowwise(name, fn, [dh, x, dx_res], [g, scale], [(D, F32)], sums=[D, D, D])


def _residual_fwd(name, x, y, gate):
    def fn(x, y, gate):
        return x + gate * y
    return _rowwise(name, fn, [x, y], [gate], [(D, F32)])[0]


def _residual_bwd(name, dx, y, gate):
    def fn(dx, y, gate):
        return dx * gate, jnp.sum(dx * y, axis=0, keepdims=True)
    return _rowwise(name, fn, [dx, y], [gate], [(D, BF16)], sums=[D])


def _loss_head(x, target, g):
    def fn(x, t, g):
        r = _rms(x)
        xh = x * r
        err = xh * g - t
        loss = 0.5 * jnp.sum(jnp.mean(err * err, axis=-1, keepdims=True), axis=0, keepdims=True)
        dy = err * (1.0 / D)
        dxh = dy * g
        dx = r * (dxh - xh * jnp.mean(dxh * xh, axis=-1, keepdims=True))
        return dx, jnp.broadcast_to(loss, (1, LANES)), jnp.sum(dy * xh, axis=0, keepdims=True)
    return _rowwise("loss_head", fn, [x, target], [g], [(D, F32)], sums=[LANES, D])


def _ffn_up(name, h, wg, wu, layer, tm=512):
    T, n = h.shape[0], wg.shape[2]
    tm = min(tm, T)

    def body(h_ref, wg_ref, wu_ref, a_ref, b_ref, s_ref):
        h = h_ref[...]
        a = _dot(h, wg_ref[0], "nn")
        b = _dot(h, wu_ref[0], "nn")
        a_ref[0] = a
        b_ref[0] = b
        s_ref[0] = (a * _sig(a) * b).astype(s_ref.dtype)

    wspec = pl.BlockSpec((1, D, n), lambda ch, i: (ch, layer, 0))
    ospec = pl.BlockSpec((1, tm, n), lambda ch, i: (ch, i, 0))
    return pl.pallas_call(
        body, name=name, grid=(4, T // tm), in_specs=[pl.BlockSpec((tm, D), lambda ch, i: (i, 0)), wspec, wspec],
        out_specs=[ospec, ospec, ospec],
        out_shape=[jax.ShapeDtypeStruct((4, T, n), F32)] * 2 + [jax.ShapeDtypeStruct((4, T, n), BF16)],
        compiler_params=_params(2))(h, wg, wu)


def _ffn_down(name, s, wd, layer, tm=512):
    _, T, n = s.shape
    tm = min(tm, T)

    def body(s_ref, w_ref, y_ref):
        @pl.when(pl.program_id(1) == 0)
        def _():
            y_ref[...] = jnp.zeros_like(y_ref)
        y_ref[...] += _dot(s_ref[0], w_ref[0], "nn")

    return pl.pallas_call(
        body, name=name, grid=(T // tm, 4),
        in_specs=[pl.BlockSpec((1, tm, n), lambda i, ch: (ch, i, 0)), pl.BlockSpec((1, n, D), lambda i, ch: (ch, layer, 0))],
        out_specs=pl.BlockSpec((tm, D), lambda i, ch: (i, 0)), out_shape=jax.ShapeDtypeStruct((T, D), F32),
        compiler_params=_params(2))(s, wd)


def _ffn_down_bwd(name, dy, wd, a, b, layer, tm=512):
    _, T, n = a.shape
    tm = min(tm, T)

    def body(dy_ref, w_ref, a_ref, b_ref, da_ref, db_ref):
        ds = _dot(dy_ref[...], w_ref[0], "nt")
        a, b = a_ref[0], b_ref[0]
        sg = _sig(a)
        da_ref[0] = (ds * b * (sg * (1.0 + a * (1.0 - sg)))).astype(da_ref.dtype)
        db_ref[0] = (ds * (a * sg)).astype(db_ref.dtype)

    bspec = pl.BlockSpec((1, tm, n), lambda ch, i: (ch, i, 0))
    return pl.pallas_call(
        body, name=name, grid=(4, T // tm),
        in_specs=[pl.BlockSpec((tm, D), lambda ch, i: (i, 0)), pl.BlockSpec((1, n, D), lambda ch, i: (ch, layer, 0)), bspec, bspec],
        out_specs=[bspec, bspec], out_shape=[jax.ShapeDtypeStruct((4, T, n), BF16)] * 2,
        compiler_params=_params(2))(dy, wd, a, b)


def _ffn_down_dw(name, s, dy):
    _, T, n = s.shape

    def body(s_ref, dy_ref, o_ref):
        o_ref[0] = _dot(s_ref[0], dy_ref[...], "tn").astype(o_ref.dtype)

    return pl.pallas_call(
        body, name=name, grid=(4,),
        in_specs=[pl.BlockSpec((1, T, n), lambda ch: (ch, 0, 0)), pl.BlockSpec((T, D), lambda ch: (0, 0))],
        out_specs=pl.BlockSpec((1, n, D), lambda ch: (ch, 0, 0)), out_shape=jax.ShapeDtypeStruct((4, n, D), BF16),
        compiler_params=_params(1))(s, dy)


def _ffn_up_dw(name, h, da, db, tm=512):
    _, T, n = da.shape

    def body(h_ref, da_ref, db_ref, dg_ref, du_ref):
        h = h_ref[...]
        dg_ref[0] = _dot(h, da_ref[0], "tn").astype(dg_ref.dtype)
        du_ref[0] = _dot(h, db_ref[0], "tn").astype(du_ref.dtype)

    dspec = pl.BlockSpec((1, T, n), lambda ch, j: (ch, 0, 0))
    ospec = pl.BlockSpec((1, tm, n), lambda ch, j: (ch, j, 0))
    return pl.pallas_call(
        body, name=name, grid=(4, D // tm), in_specs=[pl.BlockSpec((T, tm), lambda ch, j: (0, j)), dspec, dspec],
        out_specs=[ospec, ospec], out_shape=[jax.ShapeDtypeStruct((4, D, n), BF16)] * 2,
        compiler_params=_params(2))(h, da, db)


def _ffn_up_dx(name, da, db, wg, wu, layer, tm=512):
    _, T, n = da.shape
    tm = min(tm, T)

    def body(da_ref, db_ref, wg_ref, wu_ref, o_ref):
        @pl.when(pl.program_id(1) == 0)
        def _():
            o_ref[...] = jnp.zeros_like(o_ref)
        o_ref[...] += _dot(da_ref[0], wg_ref[0], "nt") + _dot(db_ref[0], wu_ref[0], "nt")

    dspec = pl.BlockSpec((1, tm, n), lambda i, ch: (ch, i, 0))
    wspec = pl.BlockSpec((1, D, n), lambda i, ch: (ch, layer, 0))
    return pl.pallas_call(
        body, name=name, grid=(T // tm, 4), in_specs=[dspec, dspec, wspec, wspec],
        out_specs=pl.BlockSpec((tm, D), lambda i, ch: (i, 0)), out_shape=jax.ShapeDtypeStruct((T, D), F32),
        compiler_params=_params(2))(da, db, wg, wu)


def _shift_down(x, k):
    if k == 0:
        return x
    rows = lax.broadcasted_iota(jnp.int32, x.shape, 0)
    return jnp.where(rows >= k, pltpu.roll(x, k, 0), 0.0)


def _shift_up(x, k):
    if k == 0:
        return x
    T = x.shape[0]
    rows = lax.broadcasted_iota(jnp.int32, x.shape, 0)
    return jnp.where(rows < T - k, pltpu.roll(x, T - k, 0), 0.0)


def _conv_silu(x, w):
    c = w[0:1, :] * _shift_down(x, 3) + w[1:2, :] * _shift_down(x, 2) + w[2:3, :] * _shift_down(x, 1) + w[3:4, :] * x
    sg = _sig(c)
    return c, sg, c * sg


def _gdn_conv_fwd(name, proj, cw):
    T = proj.shape[0]

    def body(x_ref, w_ref, o_ref):
        j = pl.program_id(0)
        _, _, y = _conv_silu(x_ref[...], w_ref[...])
        r = lax.rsqrt(jnp.sum(y * y, axis=1, keepdims=True) + EPS)
        mult = jnp.where(j < NH, HD ** -0.5, 1.0)
        o_ref[...] = jnp.where(j < 2 * NH, y * (r * mult), y)

    return pl.pallas_call(body, name=name, grid=(3 * NH,),
                          in_specs=[pl.BlockSpec((T, HD), lambda j: (0, j)), pl.BlockSpec((4, HD), lambda j: (0, j))],
                          out_specs=pl.BlockSpec((T, HD), lambda j: (0, j)),
                          out_shape=jax.ShapeDtypeStruct((T, GDN_QKV), F32), compiler_params=_params(1))(proj, cw)


def _gdn_conv_bwd(name, proj, cw, dz):
    T = proj.shape[0]

    def body(x_ref, w_ref, dz_ref, dx_ref, dw_ref):
        j = pl.program_id(0)
        x, w, dz = x_ref[...], w_ref[...], dz_ref[...]
        c, sg, y = _conv_silu(x, w)
        r = lax.rsqrt(jnp.sum(y * y, axis=1, keepdims=True) + EPS)
        mult = jnp.where(j < NH, HD ** -0.5, 1.0)
        dyn = mult * (r * dz - (r * r * r) * y * jnp.sum(dz * y, axis=1, keepdims=True))
        dy = jnp.where(j < 2 * NH, dyn, dz)
        dc = dy * (sg * (1.0 + c * (1.0 - sg)))
        dx = w[0:1, :] * _shift_up(dc, 3) + w[1:2, :] * _shift_up(dc, 2) + w[2:3, :] * _shift_up(dc, 1) + w[3:4, :] * dc
        dx_ref[...] = dx.astype(dx_ref.dtype)
        for k in range(4):
            dw_ref[pl.ds(k, 1), :] = jnp.sum(dc * _shift_down(x, 3 - k), axis=0, keepdims=True)

    return pl.pallas_call(body, name=name, grid=(3 * NH,),
                          in_specs=[pl.BlockSpec((T, HD), lambda j: (0, j)), pl.BlockSpec((4, HD), lambda j: (0, j)),
                                    pl.BlockSpec((T, HD), lambda j: (0, j))],
                          out_specs=[pl.BlockSpec((T, HD), lambda j: (0, j)), pl.BlockSpec((4, HD), lambda j: (0, j))],
                          out_shape=[jax.ShapeDtypeStruct((T, GDN_QKV), BF16), jax.ShapeDtypeStruct((4, GDN_QKV), F32)],
                          compiler_params=_params(1))(proj, cw, dz)


def _softplus(z):
    return jnp.maximum(z, 0.0) + jnp.log(1.0 + jnp.exp(-jnp.abs(z)))


_AB_CB = GDN_INK // (2 * HD) - 1


def _gdn_gates_fwd(name, proj, alog, dtb):
    def fn(ab, alog, dtb):
        a, b = ab[:, :HD], ab[:, HD:]
        return -jnp.exp(alog) * _softplus(a + dtb), _sig(b)
    return _rowwise(name, fn, [(proj, 2 * HD, _AB_CB)], [alog, dtb], [(HD, F32), (HD, F32)])


def _gdn_gates_bwd(name, proj, dg_h, db_h, alog, dtb):
    def fn(ab, dg_h, db_h, alog, dtb):
        lane = lax.broadcasted_iota(jnp.int32, (1, HD), 1)
        dg = jnp.zeros(dg_h.shape[1:], F32)
        dbeta = jnp.zeros(dg_h.shape[1:], F32)
        for h in range(NH):
            oh = (lane == h).astype(F32)
            dg = dg + dg_h[h] * oh
            dbeta = dbeta + db_h[h] * oh
        a, b = ab[:, :HD], ab[:, HD:]
        z = a + dtb
        ea = jnp.exp(alog)
        beta = _sig(b)
        da = dg * (-ea) * _sig(z)
        db = dbeta * beta * (1.0 - beta)
        return (jnp.concatenate([da, db], axis=1), jnp.sum(dg * (-ea * _softplus(z)), axis=0, keepdims=True),
                jnp.sum(da, axis=0, keepdims=True))
    return _rowwise(name, fn, [(proj, 2 * HD, _AB_CB), dg_h, db_h], [alog, dtb], [(2 * HD, BF16)], sums=[HD, HD])


def _chunk_common(q, k, v, gblk, bblk, h):
    C = CHUNK
    lane = lax.broadcasted_iota(jnp.int32, (1, HD), 1)
    oh = (lane == h).astype(F32)
    g_col = jnp.sum(gblk * oh, axis=1, keepdims=True)
    beta = jnp.sum(bblk * oh, axis=1, keepdims=True)
    ri = lax.broadcasted_iota(jnp.int32, (C, C), 0)
    ci = lax.broadcasted_iota(jnp.int32, (C, C), 1)
    incl = ri >= ci
    strict = ri > ci
    eye = (ri == ci).astype(F32)
    gcb = _dot(incl.astype(F32), jnp.broadcast_to(g_col, (C, HD)), "nn", HI)
    gc = gcb[:, :C]
    gc_row = _dot(jnp.ones((C, C), F32), eye * gc, "nn", HI)
    decay = jnp.where(incl, jnp.exp(jnp.where(incl, gc - gc_row, 0.0)), 0.0)
    rows = lax.broadcasted_iota(jnp.int32, (C, HD), 0)
    gclb = jnp.sum(jnp.where(rows == C - 1, gcb, 0.0), axis=0, keepdims=True)
    eg = jnp.exp(gcb)
    egl = jnp.exp(gclb - gcb)
    gl = jnp.exp(gclb)
    kb = k * beta
    m1 = _dot(kb, k, "nt", HI)
    L = jnp.where(strict, m1 * decay, 0.0)
    nl = -L
    tinv = eye + nl
    p = nl
    for _ in range(5):
        p = _dot(p, p, "nn", HI)
        tinv = tinv + _dot(tinv, p, "nn", HI)
    vb = v * beta
    kbg = kb * eg
    u = _dot(tinv, vb, "nn", HI)
    w = _dot(tinv, kbg, "nn", HI)
    qk = _dot(q, k, "nt", HI)
    attn = jnp.where(incl, qk * decay, 0.0)
    return dict(beta=beta, incl=incl, strict=strict, decay=decay, eg=eg, egl=egl, gl=gl, kb=kb, m1=m1, tinv=tinv,
                kbg=kbg, u=u, w=w, qk=qk, attn=attn, q_dec=q * eg, k_dec=k * egl, rows=rows, oh=oh)


def _gdn_chunk_fwd(name, qkv, g, beta):
    T = qkv.shape[0]
    N = T // CHUNK

    def body(q_ref, k_ref, v_ref, g_ref, b_ref, o_ref, st_ref, S):
        h, n = pl.program_id(0), pl.program_id(1)

        @pl.when(n == 0)
        def _():
            S[...] = jnp.zeros_like(S)

        c = _chunk_common(q_ref[...], k_ref[...], v_ref[...], g_ref[...], b_ref[...], h)
        s = S[...]
        st_ref[0, 0] = s
        v_new = c["u"] - _dot(c["w"], s, "nn", HI)
        o_ref[...] = _dot(c["q_dec"], s, "nn", HI) + _dot(c["attn"], v_new, "nn", HI)
        S[...] = s * c["gl"] + _dot(c["k_dec"], v_new, "tn", HI)

    blk = lambda off: pl.BlockSpec((CHUNK, HD), lambda h, n, off=off: (n, off + h))
    gspec = pl.BlockSpec((CHUNK, HD), lambda h, n: (n, 0))
    return pl.pallas_call(
        body, name=name, grid=(NH, N), in_specs=[blk(0), blk(NH), blk(2 * NH), gspec, gspec],
        out_specs=[pl.BlockSpec((CHUNK, HD), lambda h, n: (n, h)), pl.BlockSpec((1, 1, HD, HD), lambda h, n: (h, n, 0, 0))],
        out_shape=[jax.ShapeDtypeStruct((T, NH * HD), F32), jax.ShapeDtypeStruct((NH, N, HD, HD), F32)],
        scratch_shapes=[pltpu.VMEM((HD, HD), F32)], compiler_params=_params(2))(qkv, qkv, qkv, g, beta)


def _gdn_chunk_bwd(name, qkv, g, beta, states, do):
    T = qkv.shape[0]
    N = T // CHUNK
    C = CHUNK

    def body(q_ref, k_ref, v_ref, g_ref, b_ref, st_ref, do_ref, dq_ref, dk_ref, dv_ref, dg_ref, db_ref, dS):
        h, n = pl.program_id(0), pl.program_id(1)

        @pl.when(n == 0)
        def _():
            dS[...] = jnp.zeros_like(dS)

        q, k, v = q_ref[...], k_ref[...], v_ref[...]
        c = _chunk_common(q, k, v, g_ref[...], b_ref[...], h)
        s = st_ref[0, 0]
        do = do_ref[...]
        ds = dS[...]
        eg, egl, gl, beta, decay, tinv = c["eg"], c["egl"], c["gl"], c["beta"], c["decay"], c["tinv"]
        v_new = c["u"] - _dot(c["w"], s, "nn", HI)
        dv_new = _dot(c["attn"], do, "tn", HI) + _dot(c["k_dec"], ds, "nn", HI)
        dk_dec = _dot(v_new, ds, "nt", HI)
        dgl = jnp.sum(jnp.sum(s * ds, axis=1, keepdims=True), axis=0, keepdims=True)
        dq_dec = _dot(do, s, "nt", HI)
        dS[...] = ds * gl + _dot(c["q_dec"], do, "tn", HI) - _dot(c["w"], dv_new, "tn", HI)
        dattn = jnp.where(c["incl"], _dot(do, v_new, "nt", HI), 0.0)
        dw = -_dot(dv_new, s, "nt", HI)
        dvb = _dot(tinv, dv_new, "tn", HI)
        dkbg = _dot(tinv, dw, "tn", HI)
        dA = -(_dot(dvb, c["u"], "nt", HI) + _dot(dkbg, c["w"], "nt", HI))
        dL = jnp.where(c["strict"], dA, 0.0)
        dm1 = dL * decay
        dqk = dattn * decay
        xdec = (dL * c["m1"] + dattn * c["qk"]) * decay
        dkb = _dot(dm1, k, "nn", HI) + dkbg * eg
        dk = _dot(dm1, c["kb"], "tn", HI) + _dot(dqk, q, "tn", HI) + dk_dec * egl + dkb * beta
        dq = _dot(dqk, k, "nn", HI) + dq_dec * eg
        dkd_kd = jnp.sum(dk_dec * c["k_dec"], axis=1, keepdims=True)
        dgc = (jnp.sum(xdec, axis=1, keepdims=True) - _dot(xdec, jnp.ones((C, HD), F32), "tn", HI)
               + jnp.sum(dq_dec * c["q_dec"], axis=1, keepdims=True) - dkd_kd
               + jnp.sum(dkbg * c["kbg"], axis=1, keepdims=True))
        dgcl = jnp.sum(dkd_kd, axis=0, keepdims=True) + dgl * gl
        dgc = dgc + jnp.where(c["rows"] == C - 1, dgcl, 0.0)
        ri = lax.broadcasted_iota(jnp.int32, (C, C), 0)
        ci = lax.broadcasted_iota(jnp.int32, (C, C), 1)
        dg_ref[0] = _dot((ci >= ri).astype(F32), dgc, "nn", HI)
        db_ref[0] = jnp.broadcast_to(jnp.sum(dkb * k, axis=1, keepdims=True) + jnp.sum(dvb * v, axis=1, keepdims=True),
                                     (C, HD))
        dq_ref[...] = dq
        dk_ref[...] = dk
        dv_ref[...] = dvb * beta

    blk = lambda off: pl.BlockSpec((C, HD), lambda h, n, off=off: (N - 1 - n, off + h))
    gspec = pl.BlockSpec((C, HD), lambda h, n: (N - 1 - n, 0))
    ospec = pl.BlockSpec((C, HD), lambda h, n: (N - 1 - n, h))
    hspec = pl.BlockSpec((1, C, HD), lambda h, n: (h, N - 1 - n, 0))
    return pl.pallas_call(
        body, name=name, grid=(NH, N),
        in_specs=[blk(0), blk(NH), blk(2 * NH), gspec, gspec,
                  pl.BlockSpec((1, 1, HD, HD), lambda h, n: (h, N - 1 - n, 0, 0)), ospec],
        out_specs=[ospec, ospec, ospec, hspec, hspec],
        out_shape=[jax.ShapeDtypeStruct((T, NH * HD), F32)] * 3 + [jax.ShapeDtypeStruct((NH, T, HD), F32)] * 2,
        scratch_shapes=[pltpu.VMEM((HD, HD), F32)], compiler_params=_params(2))(qkv, qkv, qkv, g, beta, states, do)


_GATE_CB = GDN_QKV // (NH * HD)


def _gdn_gated_norm_fwd(name, o, proj, ng):
    def fn(o, gate, ng):
        outs = []
        for h in range(NH):
            sl = slice(h * HD, (h + 1) * HD)
            oh, gh = o[:, sl], gate[:, sl]
            outs.append(oh * _rms(oh) * ng * (gh * _sig(gh)))
        return jnp.concatenate(outs, axis=1)
    return _rowwise(name, fn, [o, (proj, NH * HD, _GATE_CB)], [ng], [(NH * HD, BF16)])[0]


def _gdn_gated_norm_bwd(name, don, o, proj, ng):
    def fn(don, o, gate, ng):
        dos, dgs = [], []
        dng = jnp.zeros((1, HD), F32)
        for h in range(NH):
            sl = slice(h * HD, (h + 1) * HD)
            oh, gh, dh = o[:, sl], gate[:, sl], don[:, sl]
            r = _rms(oh)
            xh = oh * r
            sg = _sig(gh)
            dn = dh * (gh * sg)
            dgs.append(dh * (xh * ng) * (sg * (1.0 + gh * (1.0 - sg))))
            dng = dng + jnp.sum(dn * xh, axis=0, keepdims=True)
            dxh = dn * ng
            dos.append(r * (dxh - xh * jnp.mean(dxh * xh, axis=-1, keepdims=True)))
        return jnp.concatenate(dos, axis=1), jnp.concatenate(dgs, axis=1), dng
    return _rowwise(name, fn, [don, o, (proj, NH * HD, _GATE_CB)], [ng], [(NH * HD, F32), (NH * HD, BF16)], sums=[HD])


def _rot(x):
    lane = lax.broadcasted_iota(jnp.int32, x.shape, 1)
    return jnp.where(lane < ROPE // 2, -pltpu.roll(x, HD - ROPE // 2, 1), pltpu.roll(x, ROPE // 2, 1))


def _rot_t(x):
    lane = lax.broadcasted_iota(jnp.int32, x.shape, 1)
    return jnp.where(lane < ROPE // 2, pltpu.roll(x, HD - ROPE // 2, 1), -pltpu.roll(x, ROPE // 2, 1))


def _rope_tables(pos_col):
    lane = jnp.arange(HD)
    inv_freq = ROPE_THETA ** (-(2.0 * (lane % (ROPE // 2)).astype(F32)) / ROPE)
    inv_freq = jnp.where(lane < ROPE, inv_freq, 0.0).astype(F32)[None, :]
    valid = (lane < ROPE).astype(F32)[None, :]

    def fn(pos, inv_freq, valid):
        ang = pos.astype(F32) * inv_freq
        return jnp.cos(ang) * valid, jnp.sin(ang) * valid
    return _rowwise("rope_tables", fn, [pos_col], [inv_freq, valid], [(HD, F32), (HD, F32)])


def _mla_pre_fwd(name, proj, cos, sin, qg, kvg):
    def fn(p, cos, sin, qg, kvg):
        cq, ckv, kr = p[:, :Q_RANK], p[:, Q_RANK:Q_RANK + KV_RANK], p[:, Q_RANK + KV_RANK:]
        return cq * _rms(cq) * qg, ckv * _rms(ckv) * kvg, kr * cos + _rot(kr) * sin
    return _rowwise(name, fn, [proj, cos, sin], [qg, kvg], [(Q_RANK, BF16), (KV_RANK, BF16), (HD, BF16)])


def _rms_bwd(dy, x, g):
    r = _rms(x)
    xh = x * r
    dxh = dy * g
    return r * (dxh - xh * jnp.mean(dxh * xh, axis=-1, keepdims=True)), jnp.sum(dy * xh, axis=0, keepdims=True)


def _mla_pre_bwd(name, proj, dcqn, dckvn, dkr, cos, sin, qg, kvg):
    def fn(p, dcqn, dckvn, dkr, cos, sin, qg, kvg):
        cq, ckv = p[:, :Q_RANK], p[:, Q_RANK:Q_RANK + KV_RANK]
        dcq, dqg = _rms_bwd(dcqn, cq, qg)
        dckv, dkvg = _rms_bwd(dckvn, ckv, kvg)
        dkr_pre = dkr * cos + _rot_t(dkr * sin)
        return jnp.concatenate([dcq, dckv, dkr_pre], axis=1), dqg, dkvg
    return _rowwise(name, fn, [proj, dcqn, dckvn, dkr, cos, sin], [qg, kvg], [(MLA_INK, BF16)], sums=[Q_RANK, KV_RANK])


def _mla_q_fwd(name, q, cos, sin):
    def fn(qn, qr, cos, sin):
        outs = []
        for h in range(NH):
            x = qr[:, h * HD:(h + 1) * HD]
            outs.append(x * cos + _rot(x) * sin)
        return qn, jnp.concatenate(outs, axis=1)
    return _rowwise(name, fn, [(q, NH * HD, 0), (q, NH * HD, 1), cos, sin], [], [(NH * HD, BF16), (NH * HD, BF16)])


def _mla_q_bwd(name, dqn, dqr, cos, sin):
    def fn(dqn, dqr, cos, sin):
        outs = [dqn]
        for h in range(NH):
            z = dqr[:, h * HD:(h + 1) * HD]
            outs.append(z * cos + _rot_t(z * sin))
        return jnp.concatenate(outs, axis=1)
    return _rowwise(name, fn, [dqn, dqr, cos, sin], [], [(2 * NH * HD, BF16)])[0]


def _att_probs(qn, qr, kn, kr, row0):
    s = (_dot(qn, kn, "nt") + _dot(qr, kr, "nt")) * ATT_SCALE
    qpos = row0 + lax.broadcasted_iota(jnp.int32, s.shape, 0)
    kpos = lax.broadcasted_iota(jnp.int32, s.shape, 1)
    s = jnp.where(kpos <= qpos, s, -1e30)
    p = jnp.exp(s - jnp.max(s, axis=1, keepdims=True))
    return p / jnp.sum(p, axis=1, keepdims=True)


def _mla_attn_fwd(name, qn, qr, kv, kr, tq=256):
    T = qn.shape[0]
    tq = min(tq, T)

    def body(qn_ref, qr_ref, kn_ref, v_ref, kr_ref, o_ref):
        p = _att_probs(qn_ref[...], qr_ref[...], kn_ref[...], kr_ref[...], pl.program_id(1) * tq)
        o_ref[...] = _dot(p.astype(BF16), v_ref[...], "nn").astype(o_ref.dtype)

    qspec = pl.BlockSpec((tq, HD), lambda h, i: (i, h))
    return pl.pallas_call(
        body, name=name, grid=(NH, T // tq),
        in_specs=[qspec, qspec, pl.BlockSpec((T, HD), lambda h, i: (0, h)), pl.BlockSpec((T, HD), lambda h, i: (0, NH + h)),
                  pl.BlockSpec((T, HD), lambda h, i: (0, 0))],
        out_specs=qspec, out_shape=jax.ShapeDtypeStruct((T, NH * HD), BF16), compiler_params=_params(2))(qn, qr, kv, kv, kr)


def _mla_attn_bwd(name, qn, qr, kv, kr, do, tq=256):
    T = qn.shape[0]
    tq = min(tq, T)

    def body(qn_ref, qr_ref, kn_ref, v_ref, kr_ref, do_ref, dqn_ref, dqr_ref, dkn_ref, dv_ref, dkr_ref):
        h, i = pl.program_id(0), pl.program_id(1)

        @pl.when(i == 0)
        def _():
            dkn_ref[...] = jnp.zeros_like(dkn_ref)
            dv_ref[...] = jnp.zeros_like(dv_ref)

        @pl.when((i == 0) & (h == 0))
        def _():
            dkr_ref[...] = jnp.zeros_like(dkr_ref)

        qn, qr, kn, kr, v, do = qn_ref[...], qr_ref[...], kn_ref[...], kr_ref[...], v_ref[...], do_ref[...]
        p = _att_probs(qn, qr, kn, kr, i * tq)
        dp = _dot(do, v, "nt")
        ds = (p * (dp - jnp.sum(p * dp, axis=1, keepdims=True)) * ATT_SCALE).astype(BF16)
        dqn_ref[...] = _dot(ds, kn, "nn")
        dqr_ref[...] = _dot(ds, kr, "nn")
        dkn_ref[...] += _dot(ds, qn, "tn")
        dkr_ref[...] += _dot(ds, qr, "tn")
        dv_ref[...] += _dot(p.astype(BF16), do, "tn")

    qspec = pl.BlockSpec((tq, HD), lambda h, i: (i, h))
    kspec = pl.BlockSpec((T, HD), lambda h, i: (0, h))
    return pl.pallas_call(
        body, name=name, grid=(NH, T // tq),
        in_specs=[qspec, qspec, kspec, pl.BlockSpec((T, HD), lambda h, i: (0, NH + h)),
                  pl.BlockSpec((T, HD), lambda h, i: (0, 0)), qspec],
        out_specs=[qspec, qspec, kspec, kspec, pl.BlockSpec((T, HD), lambda h, i: (0, 0))],
        out_shape=[jax.ShapeDtypeStruct((T, NH * HD), F32)] * 4 + [jax.ShapeDtypeStruct((T, HD), F32)],
        compiler_params=_params(2))(qn, qr, kv, kv, kr, do)


def _mod_rows(mod, layer):
    return [mod[layer:layer + 1, i * D:(i + 1) * D] for i in range(N_MOD)]


def _local_step(x, target, pos_col, mod, W, P):
    cos, sin = _rope_tables(pos_col)
    saved = []
    for l in range(DEPTH):
        j = l // 2
        sh_m, sc_m, ga_m, sh_f, sc_f, ga_f = _mod_rows(mod, l)
        s = dict(x0=x)
        h = _norm_mod_fwd(f"norm_mix{l}", x, P["norm_mix_g"][l:l + 1], sc_m, sh_m)
        s["h"] = h
        if l % 2 == 0:
            proj = _mm(f"gdn_in{j}", h, W["gdn_in"][j], "nn")
            qkv = _gdn_conv_fwd(f"gdn_conv{j}", proj, P["gdn_cw"][j])
            g, beta = _gdn_gates_fwd(f"gdn_gates{j}", proj, P["gdn_alog"][j], P["gdn_dtb"][j])
            o, states = _gdn_chunk_fwd(f"gdn_chunk{j}", qkv, g, beta)
            on = _gdn_gated_norm_fwd(f"gdn_gnorm{j}", o, proj, P["gdn_ng"][j])
            y = _mm(f"gdn_out{j}", on, W["gdn_out"][j], "nn")
            s.update(proj=proj, qkv=qkv, g=g, beta=beta, o=o, states=states, on=on)
        else:
            proj = _mm(f"mla_in{j}", h, W["mla_in"][j], "nn")
            cqn, ckvn, kr = _mla_pre_fwd(f"mla_pre{j}", proj, cos, sin, P["mla_qg"][j], P["mla_kvg"][j])
            q = _mm(f"mla_uq{j}", cqn, W["mla_uq"][j], "nn")
            kv = _mm(f"mla_ukv{j}", ckvn, W["mla_ukv"][j], "nn", out_dtype=BF16)
            qn, qr = _mla_q_fwd(f"mla_q{j}", q, cos, sin)
            o = _mla_attn_fwd(f"mla_attn{j}", qn, qr, kv, kr)
            y = _mm(f"mla_out{j}", o, W["mla_out"][j], "nn")
            s.update(proj=proj, cqn=cqn, ckvn=ckvn, kr=kr, kv=kv, qn=qn, qr=qr, o=o)
        s["y"] = y
        x = _residual_fwd(f"res_mix{l}", x, y, ga_m)
        s["x1"] = x
        h2 = _norm_mod_fwd(f"norm_ffn{l}", x, P["norm_ffn_g"][l:l + 1], sc_f, sh_f)
        fa, fb, sw = _ffn_up(f"ffn_up{l}", h2, W["ffn_g"], W["ffn_u"], l)
        yf = _ffn_down(f"ffn_down{l}", sw, W["ffn_d"], l)
        x = _residual_fwd(f"res_ffn{l}", x, yf, ga_f)
        s.update(h2=h2, fa=fa, fb=fb, sw=sw, yf=yf)
        saved.append(s)

    dx, loss, d_final = _loss_head(x, target, P["final_g"])
    gW = {k: [None] * (DEPTH if k.startswith("ffn") else 2) for k in W}
    gP = dict(loss=loss, final_g=d_final, norm_mix_g=[None] * DEPTH, norm_ffn_g=[None] * DEPTH,
              gdn_cw=[None] * 2, gdn_alog=[None] * 2, gdn_dtb=[None] * 2, gdn_ng=[None] * 2,
              mla_qg=[None] * 2, mla_kvg=[None] * 2)
    dmod = [None] * DEPTH
    for l in reversed(range(DEPTH)):
        j = l // 2
        s = saved[l]
        sh_m, sc_m, ga_m, sh_f, sc_f, ga_f = _mod_rows(mod, l)
        dyf, d_ga_f = _residual_bwd(f"res_ffn_b{l}", dx, s["yf"], ga_f)
        da, db = _ffn_down_bwd(f"ffn_down_dx{l}", dyf, W["ffn_d"], s["fa"], s["fb"], l)
        gW["ffn_d"][l] = _ffn_down_dw(f"ffn_down_dw{l}", s["sw"], dyf)
        gW["ffn_g"][l], gW["ffn_u"][l] = _ffn_up_dw(f"ffn_up_dw{l}", s["h2"], da, db)
        dh2 = _ffn_up_dx(f"ffn_up_dx{l}", da, db, W["ffn_g"], W["ffn_u"], l)
        dx, d_sh_f, d_sc_f, gP["norm_ffn_g"][l] = _norm_mod_bwd(f"norm_ffn_b{l}", dh2, s["x1"], dx,
                                                                 P["norm_ffn_g"][l:l + 1], sc_f)
        dy, d_ga_m = _residual_bwd(f"res_mix_b{l}", dx, s["y"], ga_m)
        if l % 2 == 0:
            don = _mm(f"gdn_out_dx{j}", dy, W["gdn_out"][j], "nt")
            gW["gdn_out"][j] = _mm(f"gdn_out_dw{j}", s["on"], dy, "tn", out_dtype=BF16)
            do, dgate, gP["gdn_ng"][j] = _gdn_gated_norm_bwd(f"gdn_gnorm_b{j}", don, s["o"], s["proj"], P["gdn_ng"][j])
            dq, dk, dv, dg_h, db_h = _gdn_chunk_bwd(f"gdn_chunk_b{j}", s["qkv"], s["g"], s["beta"], s["states"], do)
            dab_, gP["gdn_alog"][j], gP["gdn_dtb"][j] = _gdn_gates_bwd(f"gdn_gates_b{j}", s["proj"], dg_h, db_h,
                                                                        P["gdn_alog"][j], P["gdn_dtb"][j])
            dpre, gP["gdn_cw"][j] = _gdn_conv_bwd(f"gdn_conv_b{j}", s["proj"], P["gdn_cw"][j],
                                                  jnp.concatenate([dq, dk, dv], axis=1))
            dproj = jnp.concatenate([dpre, dgate, dab_], axis=1)
            gW["gdn_in"][j] = _mm(f"gdn_in_dw{j}", s["h"], dproj, "tn", out_dtype=BF16)
            dh = _mm(f"gdn_in_dx{j}", dproj, W["gdn_in"][j], "nt")
        else:
            do = _mm(f"mla_out_dx{j}", dy, W["mla_out"][j], "nt", out_dtype=BF16)
            gW["mla_out"][j] = _mm(f"mla_out_dw{j}", s["o"], dy, "tn", out_dtype=BF16)
            dqn, dqr, dkn, dv, dkr = _mla_attn_bwd(f"mla_attn_b{j}", s["qn"], s["qr"], s["kv"], s["kr"], do)
            dq = _mla_q_bwd(f"mla_q_b{j}", dqn, dqr, cos, sin)
            dkv = jnp.concatenate([dkn, dv], axis=1)
            gW["mla_uq"][j] = _mm(f"mla_uq_dw{j}", s["cqn"], dq, "tn", out_dtype=BF16)
            dcqn = _mm(f"mla_uq_dx{j}", dq, W["mla_uq"][j], "nt")
            gW["mla_ukv"][j] = _mm(f"mla_ukv_dw{j}", s["ckvn"], dkv, "tn", out_dtype=BF16)
            dckvn = _mm(f"mla_ukv_dx{j}", dkv, W["mla_ukv"][j], "nt")
            dproj, gP["mla_qg"][j], gP["mla_kvg"][j] = _mla_pre_bwd(f"mla_pre_b{j}", s["proj"], dcqn, dckvn, dkr, cos, sin,
                                                                     P["mla_qg"][j], P["mla_kvg"][j])
            gW["mla_in"][j] = _mm(f"mla_in_dw{j}", s["h"], dproj, "tn", out_dtype=BF16)
            dh = _mm(f"mla_in_dx{j}", dproj, W["mla_in"][j], "nt")
        dx, d_sh_m, d_sc_m, gP["norm_mix_g"][l] = _norm_mod_bwd(f"norm_mix_b{l}", dh, s["x0"], dx,
                                                                 P["norm_mix_g"][l:l + 1], sc_m)
        dmod[l] = jnp.concatenate([d_sh_m, d_sc_m, d_ga_m, d_sh_f, d_sc_f, d_ga_f], axis=1)
    return dx, jnp.concatenate(dmod, axis=0), gW, gP


def _pad_cols(a, width):
    return jnp.pad(a, ((0, 0), (0, width - a.shape[1])))


def _gdn_in_to_kernel(w):
    m = GDN_QKV + NH * HD
    return jnp.concatenate([w[:, :m], _pad_cols(w[:, m:m + NH], HD), _pad_cols(w[:, m + NH:], HD)], axis=1)


def _gdn_in_from_kernel(g):
    m = GDN_QKV + NH * HD
    return jnp.concatenate([g[:, :m], g[:, m:m + NH], g[:, m + HD:m + HD + NH]], axis=1)


def _mla_uq_to_kernel(w):
    w3 = w.reshape(Q_RANK, NH, HD + ROPE)
    rope = jnp.pad(w3[:, :, HD:], ((0, 0), (0, 0), (0, HD - ROPE)))
    return jnp.concatenate([w3[:, :, :HD].reshape(Q_RANK, NH * HD), rope.reshape(Q_RANK, NH * HD)], axis=1)


def _mla_uq_from_kernel(g):
    gn = g[:, :NH * HD].reshape(Q_RANK, NH, HD)
    gr = g[:, NH * HD:].reshape(Q_RANK, NH, HD)[:, :, :ROPE]
    return jnp.concatenate([gn, gr], axis=2).reshape(Q_RANK, NH * (HD + ROPE))


def _mla_ukv_to_kernel(w):
    w3 = w.reshape(KV_RANK, NH, 2 * HD)
    return jnp.concatenate([w3[:, :, :HD].reshape(KV_RANK, NH * HD), w3[:, :, HD:].reshape(KV_RANK, NH * HD)], axis=1)


def _mla_ukv_from_kernel(g):
    gk = g[:, :NH * HD].reshape(KV_RANK, NH, HD)
    gv = g[:, NH * HD:].reshape(KV_RANK, NH, HD)
    return jnp.concatenate([gk, gv], axis=2).reshape(KV_RANK, NH * 2 * HD)


def _layers(a, n_layers):
    r = a.shape[1] // n_layers
    return [a[:, j * r:(j + 1) * r] for j in range(n_layers)]


def _cols(t):
    return jnp.moveaxis(t, 0, 1).reshape(t.shape[1], -1)


def _uncols(g):
    return jnp.moveaxis(g.reshape(g.shape[0], 4, -1), 1, 0)


def _rows(t):
    return t.reshape(-1, t.shape[2])


def _unrows(g):
    return g.reshape(4, -1, g.shape[1])


def _weights_to_kernel(got):
    return dict(
        gdn_in=[_gdn_in_to_kernel(_cols(t)) for t in _layers(got["gdn_w_in"], 2)],
        gdn_out=[_rows(t) for t in _layers(got["gdn_w_out"], 2)],
        mla_in=[_pad_cols(_rows(t), MLA_INK) for t in _layers(got["mla_w_in"], 2)],
        mla_uq=[_mla_uq_to_kernel(_cols(t)) for t in _layers(got["mla_w_uq"], 2)],
        mla_ukv=[_mla_ukv_to_kernel(_cols(t)) for t in _layers(got["mla_w_ukv"], 2)],
        mla_out=[_rows(t) for t in _layers(got["mla_w_out"], 2)],
        ffn_g=got["ffn_w_gate"], ffn_u=got["ffn_w_up"], ffn_d=got["ffn_w_down"],
    )


def _grads_to_chips(gW):
    cat = lambda parts: jnp.concatenate(parts, axis=1)
    return dict(
        gdn_w_in=cat([_uncols(_gdn_in_from_kernel(g)) for g in gW["gdn_in"]]),
        gdn_w_out=cat([_unrows(g) for g in gW["gdn_out"]]),
        mla_w_in=cat([_unrows(g[:, :Q_RANK + KV_RANK + ROPE]) for g in gW["mla_in"]]),
        mla_w_uq=cat([_uncols(_mla_uq_from_kernel(g)) for g in gW["mla_uq"]]),
        mla_w_ukv=cat([_uncols(_mla_ukv_from_kernel(g)) for g in gW["mla_ukv"]]),
        mla_w_out=cat([_unrows(g) for g in gW["mla_out"]]),
        ffn_w_gate=cat(gW["ffn_g"]), ffn_w_up=cat(gW["ffn_u"]), ffn_w_down=cat(gW["ffn_d"]),
    )


def _small_to_kernel(norm_mix_g, norm_ffn_g, final_norm_g, gdn_conv_w, gdn_a_log, gdn_dt_bias, gdn_norm_g, q_norm_g, kv_norm_g):
    return dict(
        norm_mix_g=norm_mix_g, norm_ffn_g=norm_ffn_g, final_g=final_norm_g.reshape(1, D),
        gdn_cw=[jnp.transpose(gdn_conv_w[j]) for j in range(2)],
        gdn_alog=[_pad_cols(gdn_a_log[j:j + 1], HD) for j in range(2)],
        gdn_dtb=[_pad_cols(gdn_dt_bias[j:j + 1], HD) for j in range(2)],
        gdn_ng=[gdn_norm_g[j:j + 1] for j in range(2)],
        mla_qg=[q_norm_g[j:j + 1] for j in range(2)],
        mla_kvg=[kv_norm_g[j:j + 1] for j in range(2)],
    )


_CHIP_FLIPS = ((1, 0), (0, 1), (1, 1))
_ANY = pl.BlockSpec(memory_space=pl.ANY)


def _me():
    return lax.axis_index("x"), lax.axis_index("y"), lax.axis_index("c")


def _chip_peer(dx, dy):
    x, y, c = _me()
    return ((1 - x) if dx else x, (1 - y) if dy else y, c)


def _rcopy(src, dst, send_sem, recv_sem, to):
    return pltpu.make_async_remote_copy(src_ref=src, dst_ref=dst, send_sem=send_sem, recv_sem=recv_sem,
                                        device_id=to, device_id_type=MESH)


def _allgather4(name, a, halves=False):
    R, C = a.shape
    rh = R // 2 if halves else R

    def body(a_ref, out_ref, send_sems, recv_sems, local_sem):
        x, y, c = _me()
        me = 2 * x + y
        src = a_ref.at[pl.ds(c * rh, rh)] if halves else a_ref
        local = pltpu.make_async_copy(src, out_ref.at[me], local_sem)
        local.start()
        sends = []
        for k, (dx, dy) in enumerate(_CHIP_FLIPS):
            cp = _rcopy(src, out_ref.at[me], send_sems.at[k], recv_sems.at[k], _chip_peer(dx, dy))
            cp.start()
            sends.append(cp)
        for k, (dx, dy) in enumerate(_CHIP_FLIPS):
            px, py, _ = _chip_peer(dx, dy)
            _rcopy(src, out_ref.at[2 * px + py], send_sems.at[k], recv_sems.at[k], _chip_peer(dx, dy)).wait_recv()
        for cp in sends:
            cp.wait_send()
        local.wait()

    return pl.pallas_call(
        body, name=name, in_specs=[_ANY], out_specs=_ANY, out_shape=jax.ShapeDtypeStruct((4, rh, C), a.dtype),
        scratch_shapes=[pltpu.SemaphoreType.DMA((3,)), pltpu.SemaphoreType.DMA((3,)), pltpu.SemaphoreType.DMA(())])(a)


_NCH = 4


def _dma_sems(*counts):
    return [pltpu.SemaphoreType.DMA((n,)) for n in counts]


def _gather_weights(name, shards):
    n = len(shards)

    def body(*refs):
        a, out = refs[:n], refs[n:2 * n]
        ici_s, ici_r, d2d_s, d2d_r, loc = refs[2 * n:]
        x, y, c = _me()
        me = 2 * x + y
        sib = (x, y, 1 - c)
        peers = [_chip_peer(dx, dy) for dx, dy in _CHIP_FLIPS]
        for t in range(n):
            rows_all = a[t].shape[0] // _NCH
            for i in range(_NCH):
                rows = pl.ds(i * rows_all, rows_all)
                pltpu.make_async_copy(a[t].at[rows], out[t].at[me, rows], loc.at[t]).start()
        for t in range(n):
            h = a[t].shape[0] // 2
            ch = h // _NCH
            for k, peer in enumerate(peers):
                for i in range(_NCH):
                    rows = pl.ds(c * h + i * ch, ch)
                    _rcopy(a[t].at[rows], out[t].at[me, rows], ici_s.at[3 * t + k], ici_r.at[3 * t + k], peer).start()
        for t in range(n):
            h = a[t].shape[0] // 2
            ch = h // _NCH
            for k, peer in enumerate(peers):
                pchip = 2 * peer[0] + peer[1]
                got = out[t].at[pchip, pl.ds(c * h, h)]
                _rcopy(got, got, ici_s.at[3 * t + k], ici_r.at[3 * t + k], peer).wait_recv()
                for i in range(_NCH):
                    blk = out[t].at[pchip, pl.ds(c * h + i * ch, ch)]
                    _rcopy(blk, blk, d2d_s.at[3 * t + k], d2d_r.at[3 * t + k], sib).start()
        for t in range(n):
            h = a[t].shape[0] // 2
            for k, peer in enumerate(peers):
                pchip = 2 * peer[0] + peer[1]
                other = out[t].at[pchip, pl.ds((1 - c) * h, h)]
                _rcopy(other, other, d2d_s.at[3 * t + k], d2d_r.at[3 * t + k], sib).wait_recv()
                _rcopy(other, other, ici_s.at[3 * t + k], ici_r.at[3 * t + k], peer).wait_send()
                _rcopy(other, other, d2d_s.at[3 * t + k], d2d_r.at[3 * t + k], sib).wait_send()
        for t in range(n):
            pltpu.make_async_copy(a[t], out[t].at[me], loc.at[t]).wait()

    return pl.pallas_call(
        body, name=name, in_specs=[_ANY] * n, out_specs=[_ANY] * n,
        out_shape=[jax.ShapeDtypeStruct((4, *s.shape), s.dtype) for s in shards],
        scratch_shapes=_dma_sems(3 * n, 3 * n, 3 * n, 3 * n, n))(*shards)


def _rs_split(name, grads):
    n = len(grads)

    def body(*refs):
        g, out = refs[:n], refs[n:2 * n]
        send, recv = refs[2 * n:]
        x, y, c = _me()
        sib = (x, y, 1 - c)
        for t in range(n):
            h = g[t].shape[1] // 2
            for d in range(4):
                _rcopy(g[t].at[d, pl.ds((1 - c) * h, h)], out[t].at[d], send.at[t], recv.at[t], sib).start()
        for t in range(n):
            _rcopy(out[t], out[t], send.at[t], recv.at[t], sib).wait()

    return pl.pallas_call(
        body, name=name, in_specs=[_ANY] * n, out_specs=[_ANY] * n,
        out_shape=[jax.ShapeDtypeStruct((4, s.shape[1] // 2, s.shape[2]), s.dtype) for s in grads],
        scratch_shapes=_dma_sems(n, n))(*grads)


def _pair_add(name, g, theirs, core):
    _, R, C = g.shape
    h = R // 2
    tr = 256
    while h % tr:
        tr //= 2
    nb = h // tr

    def body(c_ref, g_ref, t_ref, o_ref):
        o_ref[...] = (g_ref[...].astype(F32) + t_ref[...].astype(F32)).astype(o_ref.dtype)

    spec = pl.BlockSpec((1, tr, C), lambda d, i, c_ref: (d, i, 0))
    grid_spec = pltpu.PrefetchScalarGridSpec(
        num_scalar_prefetch=1, grid=(4, nb),
        in_specs=[pl.BlockSpec((1, tr, C), lambda d, i, c_ref: (d, c_ref[0] * nb + i, 0)), spec], out_specs=spec)
    return pl.pallas_call(body, name=name, grid_spec=grid_spec, out_shape=jax.ShapeDtypeStruct((4, h, C), BF16),
                          compiler_params=_params(2))(core, g, theirs)


def _rs_alltoall(name, parts):
    n = len(parts)

    def body(*refs):
        p, out = refs[:n], refs[n:2 * n]
        send, recv, loc = refs[2 * n:]
        x, y, c = _me()
        me = 2 * x + y
        peers = [_chip_peer(dx, dy) for dx, dy in _CHIP_FLIPS]
        for t in range(n):
            ch = p[t].shape[1] // _NCH
            for i in range(_NCH):
                rows = pl.ds(i * ch, ch)
                pltpu.make_async_copy(p[t].at[me, rows], out[t].at[me, rows], loc.at[t]).start()
            for k, peer in enumerate(peers):
                pchip = 2 * peer[0] + peer[1]
                for i in range(_NCH):
                    rows = pl.ds(i * ch, ch)
                    _rcopy(p[t].at[pchip, rows], out[t].at[me, rows], send.at[3 * t + k], recv.at[3 * t + k], peer).start()
        for t in range(n):
            for k, peer in enumerate(peers):
                pchip = 2 * peer[0] + peer[1]
                _rcopy(out[t].at[pchip], out[t].at[pchip], send.at[3 * t + k], recv.at[3 * t + k], peer).wait()
            pltpu.make_async_copy(p[t].at[me], out[t].at[me], loc.at[t]).wait()

    return pl.pallas_call(
        body, name=name, in_specs=[_ANY] * n, out_specs=[_ANY] * n,
        out_shape=[jax.ShapeDtypeStruct(s.shape, s.dtype) for s in parts],
        scratch_shapes=_dma_sems(3 * n, 3 * n, n))(*parts)


def _rs_merge(name, halves):
    n = len(halves)

    def body(*refs):
        a, out = refs[:n], refs[n:2 * n]
        send, recv, loc = refs[2 * n:]
        x, y, c = _me()
        sib = (x, y, 1 - c)
        for t in range(n):
            h = a[t].shape[0]
            ch = h // _NCH
            for i in range(_NCH):
                dst = out[t].at[pl.ds(c * h + i * ch, ch)]
                pltpu.make_async_copy(a[t].at[pl.ds(i * ch, ch)], dst, loc.at[t]).start()
                _rcopy(a[t].at[pl.ds(i * ch, ch)], dst, send.at[t], recv.at[t], sib).start()
        for t in range(n):
            h = a[t].shape[0]
            _rcopy(a[t], out[t].at[pl.ds((1 - c) * h, h)], send.at[t], recv.at[t], sib).wait()
            pltpu.make_async_copy(a[t], out[t].at[pl.ds(c * h, h)], loc.at[t]).wait()

    return pl.pallas_call(
        body, name=name, in_specs=[_ANY] * n, out_specs=[_ANY] * n,
        out_shape=[jax.ShapeDtypeStruct((2 * s.shape[0], s.shape[1]), s.dtype) for s in halves],
        scratch_shapes=_dma_sems(n, n, n))(*halves)


def _sibling_merge(name, a):
    P_, rh, C = a.shape

    def body(a_ref, out_ref, send_sem, recv_sem, local_sem):
        x, y, c = _me()
        local = pltpu.make_async_copy(a_ref, out_ref.at[:, pl.ds(c * rh, rh)], local_sem)
        local.start()
        cp = _rcopy(a_ref, out_ref.at[:, pl.ds(c * rh, rh)], send_sem, recv_sem, (x, y, 1 - c))
        cp.start()
        cp.wait_send()
        _rcopy(a_ref, out_ref.at[:, pl.ds((1 - c) * rh, rh)], send_sem, recv_sem, (x, y, 1 - c)).wait_recv()
        local.wait()

    return pl.pallas_call(
        body, name=name, in_specs=[_ANY], out_specs=_ANY, out_shape=jax.ShapeDtypeStruct((P_, 2 * rh, C), a.dtype),
        scratch_shapes=[pltpu.SemaphoreType.DMA(()), pltpu.SemaphoreType.DMA(()), pltpu.SemaphoreType.DMA(())])(a)


def _allgather8(name, a):
    g4 = _allgather4(name + "_chips", a)
    both = _sibling_merge(name + "_cores", g4.reshape(1, 4 * a.shape[0], a.shape[1]))
    return jnp.transpose(both.reshape(2, 4, *a.shape), (1, 0, 2, 3)).reshape(8, *a.shape)


def _sum_slots(name, a, out_dtype):
    def fn(a):
        acc = a[0].astype(F32)
        for k in range(1, a.shape[0]):
            acc = acc + a[k].astype(F32)
        return acc
    return _rowwise(name, fn, [a], [], [(a.shape[2], out_dtype)])[0]


def _cast_bf16(name, a):
    return _rowwise(name, lambda a: a, [a], [], [(a.shape[1], BF16)])[0]


def _adamw(name, w, g, m, v):
    shape = w.shape
    two_d = (-1, shape[-1]) if w.ndim > 1 else (1, -1)
    w2, g2, m2, v2 = [t.reshape(two_d) for t in (w, g, m, v)]
    rows = w2.shape[0]
    tr = rows
    for cand in (256, 128, 64, 32, 16, 8):
        if rows % cand == 0:
            tr = cand
            break

    def fn(w, g, m, v):
        m = ADAM_B1 * m + (1.0 - ADAM_B1) * g
        v = ADAM_B2 * v + (1.0 - ADAM_B2) * (g * g)
        m_hat = m / (1.0 - ADAM_B1 ** ADAM_STEP)
        v_hat = v / (1.0 - ADAM_B2 ** ADAM_STEP)
        return -ADAM_LR * (m_hat / (jnp.sqrt(v_hat) + ADAM_EPS) + ADAM_WD * w), m, v

    c = w2.shape[1]
    outs = _rowwise(name, fn, [w2, g2, m2, v2], [], [(c, F32)] * 3, tr=tr)
    return [o.reshape(shape) for o in outs]


_WEIGHT_ORDER = ("ada_w", "ada_b", "norm_mix_g", "norm_ffn_g", "gdn_w_in", "gdn_conv_w", "gdn_a_log", "gdn_dt_bias",
                 "gdn_norm_g", "gdn_w_out", "mla_w_in", "mla_q_norm_g", "mla_kv_norm_g", "mla_w_uq", "mla_w_ukv",
                 "mla_w_out", "ffn_w_gate", "ffn_w_up", "ffn_w_down", "final_norm_g")
_BIG = (("gdn_w_in", 2), ("gdn_w_out", 1), ("mla_w_in", 1), ("mla_w_uq", 2), ("mla_w_ukv", 2), ("mla_w_out", 1),
        ("ffn_w_gate", 2), ("ffn_w_up", 2), ("ffn_w_down", 1))
_SMALL_SHARDED = (("gdn_conv_w", 1), ("mla_q_norm_g", 1), ("mla_kv_norm_g", 1))


def _size(shape):
    n = 1
    for s in shape:
        n *= s
    return n


def _pack_rows_each(tensors):
    parts, offs, off = [], [], 0
    for t in tensors:
        flat = t.reshape(-1).astype(F32)
        rows = -(-flat.shape[0] // PACK_W)
        parts.append(jnp.pad(flat, (0, rows * PACK_W - flat.shape[0])).reshape(rows, PACK_W))
        offs.append(off)
        off += rows
    pad = -(-off // 16) * 16 - off
    if pad:
        parts.append(jnp.zeros((pad, PACK_W), F32))
    return jnp.concatenate(parts, axis=0), offs


def _unpack_rows_each(pack, shapes):
    lead = pack.shape[:-2]
    out, off = [], 0
    for shp in shapes:
        n = _size(shp)
        rows = -(-n // PACK_W)
        out.append(pack[..., off:off + rows, :].reshape(*lead, -1)[..., :n].reshape(*lead, *shp))
        off += rows
    return out


def _merge_chips(stacked, axis):
    moved = jnp.moveaxis(stacked, 0, axis)
    shp = list(moved.shape)
    return moved.reshape(shp[:axis] + [shp[axis] * shp[axis + 1]] + shp[axis + 2:])


def _my_shard(full, axis, chip):
    n = full.shape[axis] // 4
    return lax.dynamic_slice_in_dim(full, chip * n, n, axis)


def kernel(x, c, positions, ada_w, ada_b, norm_mix_g, norm_ffn_g, gdn_w_in, gdn_conv_w, gdn_a_log, gdn_dt_bias, gdn_norm_g, gdn_w_out, mla_w_in, mla_q_norm_g, mla_kv_norm_g, mla_w_uq, mla_w_ukv, mla_w_out, ffn_w_gate, ffn_w_up, ffn_w_down, final_norm_g, loss_target, m_ada_w, m_ada_b, m_norm_mix_g, m_norm_ffn_g, m_gdn_w_in, m_gdn_conv_w, m_gdn_a_log, m_gdn_dt_bias, m_gdn_norm_g, m_gdn_w_out, m_mla_w_in, m_mla_q_norm_g, m_mla_kv_norm_g, m_mla_w_uq, m_mla_w_ukv, m_mla_w_out, m_ffn_w_gate, m_ffn_w_up, m_ffn_w_down, m_final_norm_g, v_ada_w, v_ada_b, v_norm_mix_g, v_norm_ffn_g, v_gdn_w_in, v_gdn_conv_w, v_gdn_a_log, v_gdn_dt_bias, v_gdn_norm_g, v_gdn_w_out, v_mla_w_in, v_mla_q_norm_g, v_mla_kv_norm_g, v_mla_w_uq, v_mla_w_ukv, v_mla_w_out, v_ffn_w_gate, v_ffn_w_up, v_ffn_w_down, v_final_norm_g):
    w = dict(ada_w=ada_w, ada_b=ada_b, norm_mix_g=norm_mix_g, norm_ffn_g=norm_ffn_g, gdn_w_in=gdn_w_in, gdn_conv_w=gdn_conv_w,
             gdn_a_log=gdn_a_log, gdn_dt_bias=gdn_dt_bias, gdn_norm_g=gdn_norm_g, gdn_w_out=gdn_w_out, mla_w_in=mla_w_in,
             mla_q_norm_g=mla_q_norm_g, mla_kv_norm_g=mla_kv_norm_g, mla_w_uq=mla_w_uq, mla_w_ukv=mla_w_ukv,
             mla_w_out=mla_w_out, ffn_w_gate=ffn_w_gate, ffn_w_up=ffn_w_up, ffn_w_down=ffn_w_down, final_norm_g=final_norm_g)
    m = dict(ada_w=m_ada_w, ada_b=m_ada_b, norm_mix_g=m_norm_mix_g, norm_ffn_g=m_norm_ffn_g, gdn_w_in=m_gdn_w_in,
             gdn_conv_w=m_gdn_conv_w, gdn_a_log=m_gdn_a_log, gdn_dt_bias=m_gdn_dt_bias, gdn_norm_g=m_gdn_norm_g,
             gdn_w_out=m_gdn_w_out, mla_w_in=m_mla_w_in, mla_q_norm_g=m_mla_q_norm_g, mla_kv_norm_g=m_mla_kv_norm_g,
             mla_w_uq=m_mla_w_uq, mla_w_ukv=m_mla_w_ukv, mla_w_out=m_mla_w_out, ffn_w_gate=m_ffn_w_gate,
             ffn_w_up=m_ffn_w_up, ffn_w_down=m_ffn_w_down, final_norm_g=m_final_norm_g)
    v = dict(ada_w=v_ada_w, ada_b=v_ada_b, norm_mix_g=v_norm_mix_g, norm_ffn_g=v_norm_ffn_g, gdn_w_in=v_gdn_w_in,
             gdn_conv_w=v_gdn_conv_w, gdn_a_log=v_gdn_a_log, gdn_dt_bias=v_gdn_dt_bias, gdn_norm_g=v_gdn_norm_g,
             gdn_w_out=v_gdn_w_out, mla_w_in=v_mla_w_in, mla_q_norm_g=v_mla_q_norm_g, mla_kv_norm_g=v_mla_kv_norm_g,
             mla_w_uq=v_mla_w_uq, mla_w_ukv=v_mla_w_ukv, mla_w_out=v_mla_w_out, ffn_w_gate=v_ffn_w_gate,
             ffn_w_up=v_ffn_w_up, ffn_w_down=v_ffn_w_down, final_norm_g=v_final_norm_g)
    T = x.shape[1]
    ix, iy, ic = _me()
    chip = 2 * ix + iy
    seq = 2 * chip + ic
    n_dev = 8

    small_shapes = [w[n].shape for n, _ in _SMALL_SHARDED] + [c.shape]
    pack0, _ = _pack_rows_each([w[n] for n, _ in _SMALL_SHARDED] + [c])
    got0 = _unpack_rows_each(_allgather8("gather_small", pack0), small_shapes)
    small_full = {n: _merge_chips(g[0::2], ax) for (n, ax), g in zip(_SMALL_SHARDED, got0)}
    c_all = got0[-1].reshape(n_dev, D)

    big = [n for n, _ in _BIG]
    shards = [_cast_bf16("to_bf16_" + n, w[n].reshape(-1, w[n].shape[-1])) for n in big]
    W = _weights_to_kernel(dict(zip(big, _gather_weights("gather_weights", shards))))
    P = _small_to_kernel(norm_mix_g, norm_ffn_g, final_norm_g, small_full["gdn_conv_w"], gdn_a_log, gdn_dt_bias,
                         gdn_norm_g, small_full["mla_q_norm_g"], small_full["mla_kv_norm_g"])

    c16 = jnp.pad(c_all, ((0, 16 - n_dev), (0, 0)))
    ca = _rowwise("cond_silu", lambda t: t * _sig(t), [c16], [], [(D, BF16)])[0]
    n_ada = ada_w.shape[2]
    mods = jnp.concatenate([_mm(f"ada_fwd{l}", ca, ada_w[l], "nn") for l in range(DEPTH)], axis=0)
    mods_all = _allgather4("gather_mod", mods).reshape(4, DEPTH, 16, n_ada)
    mod_mm = jnp.transpose(lax.dynamic_index_in_dim(mods_all, seq, axis=2, keepdims=False), (1, 0, 2)).reshape(DEPTH, 4 * n_ada)
    mod = _rowwise("mod_bias", lambda a, b: a + b, [mod_mm, ada_b], [], [(4 * n_ada, F32)])[0]

    dx, dmod, gW, gP = _local_step(x.reshape(T, D), loss_target.reshape(T, D), positions.reshape(T, 1), mod, W, P)

    partials = [dmod, jnp.concatenate(gP["norm_mix_g"]), jnp.concatenate(gP["norm_ffn_g"]), gP["final_g"],
                jnp.stack([jnp.transpose(g) for g in gP["gdn_cw"]]), jnp.concatenate(gP["gdn_alog"])[:, :NH],
                jnp.concatenate(gP["gdn_dtb"])[:, :NH], jnp.concatenate(gP["gdn_ng"]), jnp.concatenate(gP["mla_qg"]),
                jnp.concatenate(gP["mla_kvg"]), gP["loss"][:, :1]]
    part_shapes = [p.shape for p in partials]
    ppack, _ = _pack_rows_each(partials)
    pall = _allgather8("gather_partials", ppack)
    psum = _sum_slots("sum_partials", pall, F32)
    (g_ada_b, g_norm_mix, g_norm_ffn, g_final, g_conv_full, g_alog, g_dtb, g_gdn_ng, g_qg_full, g_kvg_full,
     loss_sum) = _unpack_rows_each(psum, part_shapes)
    dmod_all = _unpack_rows_each(pall, part_shapes[:1])[0]

    grads = dict(ada_b=g_ada_b, norm_mix_g=g_norm_mix, norm_ffn_g=g_norm_ffn, final_norm_g=g_final.reshape(D),
                 gdn_conv_w=_my_shard(g_conv_full, 1, chip), gdn_a_log=g_alog, gdn_dt_bias=g_dtb, gdn_norm_g=g_gdn_ng,
                 mla_q_norm_g=_my_shard(g_qg_full, 1, chip), mla_kv_norm_g=_my_shard(g_kvg_full, 1, chip))

    ca_t = jnp.zeros((D, LANES), BF16).at[:, :16].set(jnp.transpose(ca))
    dm_mine = lax.dynamic_slice_in_dim(dmod_all, chip * n_ada, n_ada, axis=2)
    grads["ada_w"] = jnp.stack([
        _mm(f"ada_bwd{l}", ca_t, jnp.pad(dm_mine[:, l], ((0, LANES - n_dev), (0, 0))), "nn") for l in range(DEPTH)])

    gchips = _grads_to_chips(gW)
    glist = [gchips[n] for n in big]
    theirs = _rs_split("grads_cores", glist)
    core = ic.astype(jnp.int32).reshape(1)
    pairs = [_pair_add("grads_pair_" + n, g, t, core) for n, g, t in zip(big, glist, theirs)]
    swapped = _rs_alltoall("grads_chips", pairs)
    halves = [_sum_slots("grads_sum_" + n, s, F32) for n, s in zip(big, swapped)]
    for n, t in zip(big, _rs_merge("grads_merge", halves)):
        grads[n] = t.reshape(w[n].shape)

    delta, new_m, new_v = {}, {}, {}
    for n in ("ada_w",) + tuple(n for n, _ in _BIG):
        delta[n], new_m[n], new_v[n] = _adamw("adamw_" + n, w[n], grads[n], m[n], v[n])
    small_names = [n for n in _WEIGHT_ORDER if n not in delta]
    small_shapes = [w[n].shape for n in small_names]
    packs = [_pack_rows_each([d[n] for n in small_names])[0] for d in (w, grads, m, v)]
    for d, pk in zip((delta, new_m, new_v), _adamw("adamw_small", *packs)):
        for n, t in zip(small_names, _unpack_rows_each(pk, small_shapes)):
            d[n] = t

    loss = loss_sum.reshape(())
    return (loss, dx.reshape(1, T, D), *[grads[n] for n in _WEIGHT_ORDER], *[delta[n] for n in _WEIGHT_ORDER],
            *[new_m[n] for n in _WEIGHT_ORDER], *[new_v[n] for n in _WEIGHT_ORDER])
```

```python
import functools

import jax
import jax.numpy as jnp
from jax import lax
from jax.experimental import pallas as pl
from jax.experimental.pallas import tpu as pltpu

F32 = jnp.float32
BF16 = jnp.bfloat16
HI = lax.Precision.HIGHEST
MESH = pl.DeviceIdType.MESH

D = 1024
DEPTH = 4
N_MOD = 6
NH = 8
HD = 128
CHUNK = 64
_GDN_HB = 4
GDN_QKV = 3 * NH * HD
GDN_INK = GDN_QKV + NH * HD + 2 * HD
Q_RANK, KV_RANK, ROPE = 384, 256, 64
MLA_INK = Q_RANK + KV_RANK + HD
DFF = 2816
EPS = 1e-6
ATT_SCALE = (HD + ROPE) ** -0.5
ROPE_THETA = 10000.0
LANES = 128
PACK_W = 1024

ADAM_LR, ADAM_B1, ADAM_B2, ADAM_EPS, ADAM_WD, ADAM_STEP = 0.001, 0.9, 0.999, 1e-08, 0.01, 10


def _dot(a, b, mode="nn", prec=None):
    dn = {"nn": (((1,), (0,)), ((), ())), "nt": (((1,), (1,)), ((), ())), "tn": (((0,), (0,)), ((), ()))}[mode]
    return lax.dot_general(a, b, dn, precision=prec, preferred_element_type=F32)


def _sig(x):
    return 1.0 / (1.0 + jnp.exp(-x))


def _pick(n, cap):
    if n <= cap:
        return n
    best = None
    for d in range(LANES, cap + 1, LANES):
        if n % d == 0:
            best = d
    assert best is not None, (n, cap)
    return best


def _params(n_grid):
    return pltpu.CompilerParams(dimension_semantics=("arbitrary",) * n_grid, vmem_limit_bytes=56 * 1024 * 1024)


def _rowwise(name, fn, rows, consts, outs, sums=(), tr=256):
    first = rows[0][0] if isinstance(rows[0], tuple) else rows[0]
    T = first.shape[-2]
    tr = min(tr, T)
    while T % tr:
        tr //= 2
    nr, nc, no, ns = len(rows), len(consts), len(outs), len(sums)

    def body(*refs):
        res = fn(*[r[...] for r in refs[:nr + nc]])
        if not isinstance(res, (tuple, list)):
            res = (res,)
        o_refs = refs[nr + nc:nr + nc + no]
        s_refs = refs[nr + nc + no:]
        for r, val in zip(o_refs, res[:no]):
            r[...] = val.astype(r.dtype)
        if ns:
            @pl.when(pl.program_id(0) == 0)
            def _():
                for r in s_refs:
                    r[...] = jnp.zeros_like(r)
            for r, val in zip(s_refs, res[no:]):
                r[...] += val

    in_specs, args = [], []
    for a in rows:
        if isinstance(a, tuple):
            arr, width, cb = a
            in_specs.append(pl.BlockSpec((tr, width), lambda i, cb=cb: (i, cb)))
            args.append(arr)
        elif a.ndim == 3:
            in_specs.append(pl.BlockSpec((a.shape[0], tr, a.shape[2]), lambda i: (0, i, 0)))
            args.append(a)
        else:
            in_specs.append(pl.BlockSpec((tr, a.shape[1]), lambda i: (i, 0)))
            args.append(a)
    for a in consts:
        in_specs.append(pl.BlockSpec(a.shape, lambda i, nd=a.ndim: (0,) * nd))
        args.append(a)
    out_specs = [pl.BlockSpec((tr, w), lambda i: (i, 0)) for w, _ in outs]
    out_specs += [pl.BlockSpec((1, w), lambda i: (0, 0)) for w in sums]
    out_shape = [jax.ShapeDtypeStruct((T, w), dt) for w, dt in outs]
    out_shape += [jax.ShapeDtypeStruct((1, w), F32) for w in sums]
    res = pl.pallas_call(body, name=name, grid=(T // tr,), in_specs=in_specs, out_specs=out_specs,
                         out_shape=out_shape, compiler_params=_params(1))(*args)
    return res


def _mm(name, a, b, mode, out_dtype=F32, tm=512, tn=1024):
    if mode == "tn":
        K, M = a.shape
    else:
        M, K = a.shape
    N = b.shape[0] if mode == "nt" else b.shape[1]
    tm, tn = _pick(M, tm), _pick(N, tn)

    def body(a_ref, b_ref, o_ref):
        o_ref[...] = _dot(a_ref[...].astype(BF16), b_ref[...].astype(BF16), mode).astype(o_ref.dtype)

    a_spec = pl.BlockSpec((K, tm), lambda i, j: (0, i)) if mode == "tn" else pl.BlockSpec((tm, K), lambda i, j: (i, 0))
    b_spec = pl.BlockSpec((tn, K), lambda i, j: (j, 0)) if mode == "nt" else pl.BlockSpec((K, tn), lambda i, j: (0, j))
    return pl.pallas_call(body, name=name, grid=(M // tm, N // tn), in_specs=[a_spec, b_spec],
                          out_specs=pl.BlockSpec((tm, tn), lambda i, j: (i, j)),
                          out_shape=jax.ShapeDtypeStruct((M, N), out_dtype), compiler_params=_params(2))(a, b)


def _rms(x, eps=EPS):
    return lax.rsqrt(jnp.mean(x * x, axis=-1, keepdims=True) + eps)


def _norm_mod_fwd(name, x, g, scale, shift):
    def fn(x, g, scale, shift):
        return x * _rms(x) * g * (1.0 + scale) + shift
    return _rowwise(name, fn, [x], [g, scale, shift], [(D, BF16)])[0]


def _norm_mod_bwd(name, dh, x, dx_res, g, scale):
    def fn(dh, x, dx_res, g, scale):
        r = _rms(x)
        xh = x * r
        dxh = dh * (g * (1.0 + scale))
        dx = r * (dxh - xh * jnp.mean(dxh * xh, axis=-1, keepdims=True))
        dhx = dh * xh
        return (dx_res + dx, jnp.sum(dh, axis=0, keepdims=True), jnp.sum(dhx * g, axis=0, keepdims=True),
                jnp.sum(dhx * (1.0 + scale), axis=0, keepdims=True))
    return _rowwise(name, fn, [dh, x, dx_res], [g, scale], [(D, F32)], sums=[D, D, D])


def _residual_fwd(name, x, y, gate):
    def fn(x, y, gate):
        return x + gate * y
    return _rowwise(name, fn, [x, y], [gate], [(D, F32)])[0]


def _residual_bwd(name, dx, y, gate):
    def fn(dx, y, gate):
        return dx * gate, jnp.sum(dx * y, axis=0, keepdims=True)
    return _rowwise(name, fn, [dx, y], [gate], [(D, BF16)], sums=[D])


def _loss_head(x, target, g):
    def fn(x, t, g):
        r = _rms(x)
        xh = x * r
        err = xh * g - t
        loss = 0.5 * jnp.sum(jnp.mean(err * err, axis=-1, keepdims=True), axis=0, keepdims=True)
        dy = err * (1.0 / D)
        dxh = dy * g
        dx = r * (dxh - xh * jnp.mean(dxh * xh, axis=-1, keepdims=True))
        return dx, jnp.broadcast_to(loss, (1, LANES)), jnp.sum(dy * xh, axis=0, keepdims=True)
    return _rowwise("loss_head", fn, [x, target], [g], [(D, F32)], sums=[LANES, D])


def _ffn_up(name, h, wg, wu, layer, tm=512):
    T, n = h.shape[0], wg.shape[2]
    tm = min(tm, T)

    def body(h_ref, wg_ref, wu_ref, a_ref, b_ref, s_ref):
        h = h_ref[...]
        a = _dot(h, wg_ref[0], "nn")
        b = _dot(h, wu_ref[0], "nn")
        a_ref[0] = a
        b_ref[0] = b
        s_ref[0] = (a * _sig(a) * b).astype(s_ref.dtype)

    wspec = pl.BlockSpec((1, D, n), lambda ch, i: (ch, layer, 0))
    ospec = pl.BlockSpec((1, tm, n), lambda ch, i: (ch, i, 0))
    return pl.pallas_call(
        body, name=name, grid=(4, T // tm), in_specs=[pl.BlockSpec((tm, D), lambda ch, i: (i, 0)), wspec, wspec],
        out_specs=[ospec, ospec, ospec],
        out_shape=[jax.ShapeDtypeStruct((4, T, n), F32)] * 2 + [jax.ShapeDtypeStruct((4, T, n), BF16)],
        compiler_params=_params(2))(h, wg, wu)


def _ffn_down(name, s, wd, layer, tm=512):
    _, T, n = s.shape
    tm = min(tm, T)

    def body(s_ref, w_ref, y_ref):
        @pl.when(pl.program_id(1) == 0)
        def _():
            y_ref[...] = jnp.zeros_like(y_ref)
        y_ref[...] += _dot(s_ref[0], w_ref[0], "nn")

    return pl.pallas_call(
        body, name=name, grid=(T // tm, 4),
        in_specs=[pl.BlockSpec((1, tm, n), lambda i, ch: (ch, i, 0)), pl.BlockSpec((1, n, D), lambda i, ch: (ch, layer, 0))],
        out_specs=pl.BlockSpec((tm, D), lambda i, ch: (i, 0)), out_shape=jax.ShapeDtypeStruct((T, D), F32),
        compiler_params=_params(2))(s, wd)


def _ffn_down_bwd(name, dy, wd, a, b, layer, tm=512):
    _, T, n = a.shape
    tm = min(tm, T)

    def body(dy_ref, w_ref, a_ref, b_ref, da_ref, db_ref):
        ds = _dot(dy_ref[...], w_ref[0], "nt")
        a, b = a_ref[0], b_ref[0]
        sg = _sig(a)
        da_ref[0] = (ds * b * (sg * (1.0 + a * (1.0 - sg)))).astype(da_ref.dtype)
        db_ref[0] = (ds * (a * sg)).astype(db_ref.dtype)

    bspec = pl.BlockSpec((1, tm, n), lambda ch, i: (ch, i, 0))
    return pl.pallas_call(
        body, name=name, grid=(4, T // tm),
        in_specs=[pl.BlockSpec((tm, D), lambda ch, i: (i, 0)), pl.BlockSpec((1, n, D), lambda ch, i: (ch, layer, 0)), bspec, bspec],
        out_specs=[bspec, bspec], out_shape=[jax.ShapeDtypeStruct((4, T, n), BF16)] * 2,
        compiler_params=_params(2))(dy, wd, a, b)


def _ffn_down_dw(name, s, dy):
    _, T, n = s.shape

    def body(s_ref, dy_ref, o_ref):
        o_ref[0] = _dot(s_ref[0], dy_ref[...], "tn").astype(o_ref.dtype)

    return pl.pallas_call(
        body, name=name, grid=(4,),
        in_specs=[pl.BlockSpec((1, T, n), lambda ch: (ch, 0, 0)), pl.BlockSpec((T, D), lambda ch: (0, 0))],
        out_specs=pl.BlockSpec((1, n, D), lambda ch: (ch, 0, 0)), out_shape=jax.ShapeDtypeStruct((4, n, D), BF16),
        compiler_params=_params(1))(s, dy)


def _ffn_up_dw(name, h, da, db, tm=512):
    _, T, n = da.shape

    def body(h_ref, da_ref, db_ref, dg_ref, du_ref):
        h = h_ref[...]
        dg_ref[0] = _dot(h, da_ref[0], "tn").astype(dg_ref.dtype)
        du_ref[0] = _dot(h, db_ref[0], "tn").astype(du_ref.dtype)

    dspec = pl.BlockSpec((1, T, n), lambda ch, j: (ch, 0, 0))
    ospec = pl.BlockSpec((1, tm, n), lambda ch, j: (ch, j, 0))
    return pl.pallas_call(
        body, name=name, grid=(4, D // tm), in_specs=[pl.BlockSpec((T, tm), lambda ch, j: (0, j)), dspec, dspec],
        out_specs=[ospec, ospec], out_shape=[jax.ShapeDtypeStruct((4, D, n), BF16)] * 2,
        compiler_params=_params(2))(h, da, db)


def _ffn_up_dx(name, da, db, wg, wu, layer, tm=512):
    _, T, n = da.shape
    tm = min(tm, T)

    def body(da_ref, db_ref, wg_ref, wu_ref, o_ref):
        @pl.when(pl.program_id(1) == 0)
        def _():
            o_ref[...] = jnp.zeros_like(o_ref)
        o_ref[...] += _dot(da_ref[0], wg_ref[0], "nt") + _dot(db_ref[0], wu_ref[0], "nt")

    dspec = pl.BlockSpec((1, tm, n), lambda i, ch: (ch, i, 0))
    wspec = pl.BlockSpec((1, D, n), lambda i, ch: (ch, layer, 0))
    return pl.pallas_call(
        body, name=name, grid=(T // tm, 4), in_specs=[dspec, dspec, wspec, wspec],
        out_specs=pl.BlockSpec((tm, D), lambda i, ch: (i, 0)), out_shape=jax.ShapeDtypeStruct((T, D), F32),
        compiler_params=_params(2))(da, db, wg, wu)


def _shift_down(x, k):
    if k == 0:
        return x
    rows = lax.broadcasted_iota(jnp.int32, x.shape, 0)
    return jnp.where(rows >= k, pltpu.roll(x, k, 0), 0.0)


def _shift_up(x, k):
    if k == 0:
        return x
    T = x.shape[0]
    rows = lax.broadcasted_iota(jnp.int32, x.shape, 0)
    return jnp.where(rows < T - k, pltpu.roll(x, T - k, 0), 0.0)


def _conv_silu(x, w):
    c = w[0:1, :] * _shift_down(x, 3) + w[1:2, :] * _shift_down(x, 2) + w[2:3, :] * _shift_down(x, 1) + w[3:4, :] * x
    sg = _sig(c)
    return c, sg, c * sg


def _gdn_conv_fwd(name, proj, cw):
    T = proj.shape[0]

    def body(x_ref, w_ref, o_ref):
        j = pl.program_id(0)
        _, _, y = _conv_silu(x_ref[...], w_ref[...])
        r = lax.rsqrt(jnp.sum(y * y, axis=1, keepdims=True) + EPS)
        mult = jnp.where(j < NH, HD ** -0.5, 1.0)
        o_ref[...] = jnp.where(j < 2 * NH, y * (r * mult), y)

    return pl.pallas_call(body, name=name, grid=(3 * NH,),
                          in_specs=[pl.BlockSpec((T, HD), lambda j: (0, j)), pl.BlockSpec((4, HD), lambda j: (0, j))],
                          out_specs=pl.BlockSpec((T, HD), lambda j: (0, j)),
                          out_shape=jax.ShapeDtypeStruct((T, GDN_QKV), F32), compiler_params=_params(1))(proj, cw)


def _gdn_conv_bwd(name, proj, cw, dz):
    T = proj.shape[0]

    def body(x_ref, w_ref, dz_ref, dx_ref, dw_ref):
        j = pl.program_id(0)
        x, w, dz = x_ref[...], w_ref[...], dz_ref[...]
        c, sg, y = _conv_silu(x, w)
        r = lax.rsqrt(jnp.sum(y * y, axis=1, keepdims=True) + EPS)
        mult = jnp.where(j < NH, HD ** -0.5, 1.0)
        dyn = mult * (r * dz - (r * r * r) * y * jnp.sum(dz * y, axis=1, keepdims=True))
        dy = jnp.where(j < 2 * NH, dyn, dz)
        dc = dy * (sg * (1.0 + c * (1.0 - sg)))
        dx = w[0:1, :] * _shift_up(dc, 3) + w[1:2, :] * _shift_up(dc, 2) + w[2:3, :] * _shift_up(dc, 1) + w[3:4, :] * dc
        dx_ref[...] = dx.astype(dx_ref.dtype)
        for k in range(4):
            dw_ref[pl.ds(k, 1), :] = jnp.sum(dc * _shift_down(x, 3 - k), axis=0, keepdims=True)

    return pl.pallas_call(body, name=name, grid=(3 * NH,),
                          in_specs=[pl.BlockSpec((T, HD), lambda j: (0, j)), pl.BlockSpec((4, HD), lambda j: (0, j)),
                                    pl.BlockSpec((T, HD), lambda j: (0, j))],
                          out_specs=[pl.BlockSpec((T, HD), lambda j: (0, j)), pl.BlockSpec((4, HD), lambda j: (0, j))],
                          out_shape=[jax.ShapeDtypeStruct((T, GDN_QKV), BF16), jax.ShapeDtypeStruct((4, GDN_QKV), F32)],
                          compiler_params=_params(1))(proj, cw, dz)


def _softplus(z):
    return jnp.maximum(z, 0.0) + jnp.log(1.0 + jnp.exp(-jnp.abs(z)))


_AB_CB = GDN_INK // (2 * HD) - 1


def _gdn_gates_fwd(name, proj, alog, dtb):
    def fn(ab, alog, dtb):
        a, b = ab[:, :HD], ab[:, HD:]
        return -jnp.exp(alog) * _softplus(a + dtb), _sig(b)
    return _rowwise(name, fn, [(proj, 2 * HD, _AB_CB)], [alog, dtb], [(HD, F32), (HD, F32)])


def _gdn_gates_bwd(name, proj, dg_h, db_h, alog, dtb):
    def fn(ab, dg_h, db_h, alog, dtb):
        lane = lax.broadcasted_iota(jnp.int32, (1, HD), 1)
        dg = jnp.zeros(dg_h.shape[1:], F32)
        dbeta = jnp.zeros(dg_h.shape[1:], F32)
        for h in range(NH):
            oh = (lane == h).astype(F32)
            dg = dg + dg_h[h] * oh
            dbeta = dbeta + db_h[h] * oh
        a, b = ab[:, :HD], ab[:, HD:]
        z = a + dtb
        ea = jnp.exp(alog)
        beta = _sig(b)
        da = dg * (-ea) * _sig(z)
        db = dbeta * beta * (1.0 - beta)
        return (jnp.concatenate([da, db], axis=1), jnp.sum(dg * (-ea * _softplus(z)), axis=0, keepdims=True),
                jnp.sum(da, axis=0, keepdims=True))
    return _rowwise(name, fn, [(proj, 2 * HD, _AB_CB), dg_h, db_h], [alog, dtb], [(2 * HD, BF16)], sums=[HD, HD])


def _interleave(gens):
    gens = list(gens)
    results = [None] * len(gens)
    active = list(range(len(gens)))
    while active:
        for i in list(active):
            try:
                next(gens[i])
            except StopIteration as stop:
                results[i] = stop.value
                active.remove(i)
    return results


def _chunk_common(q, k, v, gblk, bblk, h):
    C = CHUNK
    lane = lax.broadcasted_iota(jnp.int32, (1, HD), 1)
    oh = (lane == h).astype(F32)
    g_col = jnp.sum(gblk * oh, axis=1, keepdims=True)
    beta = jnp.sum(bblk * oh, axis=1, keepdims=True)
    ri = lax.broadcasted_iota(jnp.int32, (C, C), 0)
    ci = lax.broadcasted_iota(jnp.int32, (C, C), 1)
    incl = ri >= ci
    strict = ri > ci
    eye = (ri == ci).astype(F32)
    gcb = _dot(incl.astype(F32), jnp.broadcast_to(g_col, (C, HD)), "nn", HI)
    yield
    gc = gcb[:, :C]
    gc_row = _dot(jnp.ones((C, C), F32), eye * gc, "nn", HI)
    yield
    decay = jnp.where(incl, jnp.exp(jnp.where(incl, gc - gc_row, 0.0)), 0.0)
    rows = lax.broadcasted_iota(jnp.int32, (C, HD), 0)
    gclb = jnp.sum(jnp.where(rows == C - 1, gcb, 0.0), axis=0, keepdims=True)
    eg = jnp.exp(gcb)
    egl = jnp.exp(gclb - gcb)
    gl = jnp.exp(gclb)
    kb = k * beta
    m1 = _dot(kb, k, "nt", HI)
    qk = _dot(q, k, "nt", HI)
    yield
    L = jnp.where(strict, m1 * decay, 0.0)
    nl = -L
    tinv = eye + nl
    p = nl
    for _ in range(5):
        p = _dot(p, p, "nn", HI)
        yield
        tinv = tinv + _dot(tinv, p, "nn", HI)
    vb = v * beta
    kbg = kb * eg
    yield
    u = _dot(tinv, vb, "nn", HI)
    w = _dot(tinv, kbg, "nn", HI)
    yield
    attn = jnp.where(incl, qk * decay, 0.0)
    return dict(beta=beta, incl=incl, strict=strict, decay=decay, eg=eg, egl=egl, gl=gl, kb=kb, m1=m1, tinv=tinv,
                kbg=kbg, u=u, w=w, qk=qk, attn=attn, q_dec=q * eg, k_dec=k * egl, rows=rows, oh=oh)


def _gdn_chunk_fwd(name, qkv, g, beta):
    T = qkv.shape[0]
    N = T // CHUNK

    hb = _GDN_HB
    w = hb * HD

    def body(q_ref, k_ref, v_ref, g_ref, b_ref, o_ref, st_ref, S):
        hg, n = pl.program_id(0), pl.program_id(1)

        @pl.when(n == 0)
        def _():
            S[...] = jnp.zeros_like(S)

        gblk, bblk = g_ref[...], b_ref[...]

        def one_head(i, q, k, v, s):
            c = yield from _chunk_common(q, k, v, gblk, bblk, hg * hb + i)
            v_new = c["u"] - _dot(c["w"], s, "nn", HI)
            qs = _dot(c["q_dec"], s, "nn", HI)
            yield
            o = qs + _dot(c["attn"], v_new, "nn", HI)
            return o, s * c["gl"] + _dot(c["k_dec"], v_new, "tn", HI)

        sls = [slice(i * HD, (i + 1) * HD) for i in range(hb)]
        states = [S[i] for i in range(hb)]
        res = _interleave(one_head(i, q_ref[:, sls[i]], k_ref[:, sls[i]], v_ref[:, sls[i]], states[i]) for i in range(hb))
        for i, (o, s_new) in enumerate(res):
            st_ref[i, 0] = states[i]
            o_ref[:, sls[i]] = o
            S[i] = s_new

    blk = lambda off: pl.BlockSpec((CHUNK, w), lambda h, n, off=off: (n, off + h))
    gspec = pl.BlockSpec((CHUNK, HD), lambda h, n: (n, 0))
    return pl.pallas_call(
        body, name=name, grid=(NH // hb, N), in_specs=[blk(0), blk(NH // hb), blk(2 * NH // hb), gspec, gspec],
        out_specs=[pl.BlockSpec((CHUNK, w), lambda h, n: (n, h)), pl.BlockSpec((hb, 1, HD, HD), lambda h, n: (h, n, 0, 0))],
        out_shape=[jax.ShapeDtypeStruct((T, NH * HD), F32), jax.ShapeDtypeStruct((NH, N, HD, HD), F32)],
        scratch_shapes=[pltpu.VMEM((hb, HD, HD), F32)], compiler_params=_params(2))(qkv, qkv, qkv, g, beta)


def _gdn_chunk_bwd(name, qkv, g, beta, states, do):
    T = qkv.shape[0]
    N = T // CHUNK
    C = CHUNK

    hb = _GDN_HB
    w = hb * HD

    def body(q_ref, k_ref, v_ref, g_ref, b_ref, st_ref, do_ref, dq_ref, dk_ref, dv_ref, dg_ref, db_ref, dS):
        hg, n = pl.program_id(0), pl.program_id(1)

        @pl.when(n == 0)
        def _():
            dS[...] = jnp.zeros_like(dS)

        gblk, bblk = g_ref[...], b_ref[...]
        sls = [slice(i * HD, (i + 1) * HD) for i in range(hb)]
        res = _interleave(one_head(hg * hb + i, gblk, bblk, q_ref[:, sls[i]], k_ref[:, sls[i]], v_ref[:, sls[i]],
                                   st_ref[i, 0], do_ref[:, sls[i]], dS[i]) for i in range(hb))
        for i, (dq, dk, dv, dg, db, ds_new) in enumerate(res):
            dq_ref[:, sls[i]] = dq
            dk_ref[:, sls[i]] = dk
            dv_ref[:, sls[i]] = dv
            dg_ref[i] = dg
            db_ref[i] = db
            dS[i] = ds_new

    def one_head(h, gblk, bblk, q, k, v, s, do, ds):
        c = yield from _chunk_common(q, k, v, gblk, bblk, h)
        eg, egl, gl, beta, decay, tinv = c["eg"], c["egl"], c["gl"], c["beta"], c["decay"], c["tinv"]
        v_new = c["u"] - _dot(c["w"], s, "nn", HI)
        dq_dec = _dot(do, s, "nt", HI)
        yield
        dv_new = _dot(c["attn"], do, "tn", HI) + _dot(c["k_dec"], ds, "nn", HI)
        dk_dec = _dot(v_new, ds, "nt", HI)
        dgl = jnp.sum(jnp.sum(s * ds, axis=1, keepdims=True), axis=0, keepdims=True)
        yield
        ds_new = ds * gl + _dot(c["q_dec"], do, "tn", HI) - _dot(c["w"], dv_new, "tn", HI)
        dattn = jnp.where(c["incl"], _dot(do, v_new, "nt", HI), 0.0)
        dw = -_dot(dv_new, s, "nt", HI)
        yield
        dvb = _dot(tinv, dv_new, "tn", HI)
        dkbg = _dot(tinv, dw, "tn", HI)
        yield
        dA = -(_dot(dvb, c["u"], "nt", HI) + _dot(dkbg, c["w"], "nt", HI))
        yield
        dL = jnp.where(c["strict"], dA, 0.0)
        dm1 = dL * decay
        dqk = dattn * decay
        xdec = (dL * c["m1"] + dattn * c["qk"]) * decay
        dkb = _dot(dm1, k, "nn", HI) + dkbg * eg
        dk = _dot(dm1, c["kb"], "tn", HI) + _dot(dqk, q, "tn", HI) + dk_dec * egl + dkb * beta
        dq = _dot(dqk, k, "nn", HI) + dq_dec * eg
        yield
        dkd_kd = jnp.sum(dk_dec * c["k_dec"], axis=1, keepdims=True)
        dgc = (jnp.sum(xdec, axis=1, keepdims=True) - _dot(xdec, jnp.ones((C, HD), F32), "tn", HI)
               + jnp.sum(dq_dec * c["q_dec"], axis=1, keepdims=True) - dkd_kd
               + jnp.sum(dkbg * c["kbg"], axis=1, keepdims=True))
        dgcl = jnp.sum(dkd_kd, axis=0, keepdims=True) + dgl * gl
        dgc = dgc + jnp.where(c["rows"] == C - 1, dgcl, 0.0)
        ri = lax.broadcasted_iota(jnp.int32, (C, C), 0)
        ci = lax.broadcasted_iota(jnp.int32, (C, C), 1)
        dg = _dot((ci >= ri).astype(F32), dgc, "nn", HI)
        db = jnp.broadcast_to(jnp.sum(dkb * k, axis=1, keepdims=True) + jnp.sum(dvb * v, axis=1, keepdims=True), (C, HD))
        return dq, dk, dvb * beta, dg, db, ds_new

    blk = lambda off: pl.BlockSpec((C, w), lambda h, n, off=off: (N - 1 - n, off + h))
    gspec = pl.BlockSpec((C, HD), lambda h, n: (N - 1 - n, 0))
    ospec = pl.BlockSpec((C, w), lambda h, n: (N - 1 - n, h))
    hspec = pl.BlockSpec((hb, C, HD), lambda h, n: (h, N - 1 - n, 0))
    return pl.pallas_call(
        body, name=name, grid=(NH // hb, N),
        in_specs=[blk(0), blk(NH // hb), blk(2 * NH // hb), gspec, gspec,
                  pl.BlockSpec((hb, 1, HD, HD), lambda h, n: (h, N - 1 - n, 0, 0)), ospec],
        out_specs=[ospec, ospec, ospec, hspec, hspec],
        out_shape=[jax.ShapeDtypeStruct((T, NH * HD), F32)] * 3 + [jax.ShapeDtypeStruct((NH, T, HD), F32)] * 2,
        scratch_shapes=[pltpu.VMEM((hb, HD, HD), F32)], compiler_params=_params(2))(qkv, qkv, qkv, g, beta, states, do)


_GATE_CB = GDN_QKV // (NH * HD)


def _gdn_gated_norm_fwd(name, o, proj, ng):
    def fn(o, gate, ng):
        outs = []
        for h in range(NH):
            sl = slice(h * HD, (h + 1) * HD)
            oh, gh = o[:, sl], gate[:, sl]
            outs.append(oh * _rms(oh) * ng * (gh * _sig(gh)))
        return jnp.concatenate(outs, axis=1)
    return _rowwise(name, fn, [o, (proj, NH * HD, _GATE_CB)], [ng], [(NH * HD, BF16)])[0]


def _gdn_gated_norm_bwd(name, don, o, proj, ng):
    def fn(don, o, gate, ng):
        dos, dgs = [], []
        dng = jnp.zeros((1, HD), F32)
        for h in range(NH):
            sl = slice(h * HD, (h + 1) * HD)
            oh, gh, dh = o[:, sl], gate[:, sl], don[:, sl]
            r = _rms(oh)
            xh = oh * r
            sg = _sig(gh)
            dn = dh * (gh * sg)
            dgs.append(dh * (xh * ng) * (sg * (1.0 + gh * (1.0 - sg))))
            dng = dng + jnp.sum(dn * xh, axis=0, keepdims=True)
            dxh = dn * ng
            dos.append(r * (dxh - xh * jnp.mean(dxh * xh, axis=-1, keepdims=True)))
        return jnp.concatenate(dos, axis=1), jnp.concatenate(dgs, axis=1), dng
    return _rowwise(name, fn, [don, o, (proj, NH * HD, _GATE_CB)], [ng], [(NH * HD, F32), (NH * HD, BF16)], sums=[HD])


def _rot(x):
    lane = lax.broadcasted_iota(jnp.int32, x.shape, 1)
    return jnp.where(lane < ROPE // 2, -pltpu.roll(x, HD - ROPE // 2, 1), pltpu.roll(x, ROPE // 2, 1))


def _rot_t(x):
    lane = lax.broadcasted_iota(jnp.int32, x.shape, 1)
    return jnp.where(lane < ROPE // 2, pltpu.roll(x, HD - ROPE // 2, 1), -pltpu.roll(x, ROPE // 2, 1))


def _rope_tables(pos_col):
    lane = jnp.arange(HD)
    inv_freq = ROPE_THETA ** (-(2.0 * (lane % (ROPE // 2)).astype(F32)) / ROPE)
    inv_freq = jnp.where(lane < ROPE, inv_freq, 0.0).astype(F32)[None, :]
    valid = (lane < ROPE).astype(F32)[None, :]

    def fn(pos, inv_freq, valid):
        ang = pos.astype(F32) * inv_freq
        return jnp.cos(ang) * valid, jnp.sin(ang) * valid
    return _rowwise("rope_tables", fn, [pos_col], [inv_freq, valid], [(HD, F32), (HD, F32)])


def _mla_pre_fwd(name, proj, cos, sin, qg, kvg):
    def fn(p, cos, sin, qg, kvg):
        cq, ckv, kr = p[:, :Q_RANK], p[:, Q_RANK:Q_RANK + KV_RANK], p[:, Q_RANK + KV_RANK:]
        return cq * _rms(cq) * qg, ckv * _rms(ckv) * kvg, kr * cos + _rot(kr) * sin
    return _rowwise(name, fn, [proj, cos, sin], [qg, kvg], [(Q_RANK, BF16), (KV_RANK, BF16), (HD, BF16)])


def _rms_bwd(dy, x, g):
    r = _rms(x)
    xh = x * r
    dxh = dy * g
    return r * (dxh - xh * jnp.mean(dxh * xh, axis=-1, keepdims=True)), jnp.sum(dy * xh, axis=0, keepdims=True)


def _mla_pre_bwd(name, proj, dcqn, dckvn, dkr, cos, sin, qg, kvg):
    def fn(p, dcqn, dckvn, dkr, cos, sin, qg, kvg):
        cq, ckv = p[:, :Q_RANK], p[:, Q_RANK:Q_RANK + KV_RANK]
        dcq, dqg = _rms_bwd(dcqn, cq, qg)
        dckv, dkvg = _rms_bwd(dckvn, ckv, kvg)
        dkr_pre = dkr * cos + _rot_t(dkr * sin)
        return jnp.concatenate([dcq, dckv, dkr_pre], axis=1), dqg, dkvg
    return _rowwise(name, fn, [proj, dcqn, dckvn, dkr, cos, sin], [qg, kvg], [(MLA_INK, BF16)], sums=[Q_RANK, KV_RANK])


def _mla_q_fwd(name, q, cos, sin):
    def fn(qn, qr, cos, sin):
        outs = []
        for h in range(NH):
            x = qr[:, h * HD:(h + 1) * HD]
            outs.append(x * cos + _rot(x) * sin)
        return qn, jnp.concatenate(outs, axis=1)
    return _rowwise(name, fn, [(q, NH * HD, 0), (q, NH * HD, 1), cos, sin], [], [(NH * HD, BF16), (NH * HD, BF16)])


def _mla_q_bwd(name, dqn, dqr, cos, sin):
    def fn(dqn, dqr, cos, sin):
        outs = [dqn]
        for h in range(NH):
            z = dqr[:, h * HD:(h + 1) * HD]
            outs.append(z * cos + _rot_t(z * sin))
        return jnp.concatenate(outs, axis=1)
    return _rowwise(name, fn, [dqn, dqr, cos, sin], [], [(2 * NH * HD, BF16)])[0]


def _att_probs(qn, qr, kn, kr, row0):
    s = (_dot(qn, kn, "nt") + _dot(qr, kr, "nt")) * ATT_SCALE
    qpos = row0 + lax.broadcasted_iota(jnp.int32, s.shape, 0)
    kpos = lax.broadcasted_iota(jnp.int32, s.shape, 1)
    s = jnp.where(kpos <= qpos, s, -1e30)
    p = jnp.exp(s - jnp.max(s, axis=1, keepdims=True))
    return p / jnp.sum(p, axis=1, keepdims=True)


def _mla_attn_fwd(name, qn, qr, kv, kr, tq=256):
    T = qn.shape[0]
    tq = min(tq, T)

    def body(qn_ref, qr_ref, kn_ref, v_ref, kr_ref, o_ref):
        p = _att_probs(qn_ref[...], qr_ref[...], kn_ref[...], kr_ref[...], pl.program_id(1) * tq)
        o_ref[...] = _dot(p.astype(BF16), v_ref[...], "nn").astype(o_ref.dtype)

    qspec = pl.BlockSpec((tq, HD), lambda h, i: (i, h))
    return pl.pallas_call(
        body, name=name, grid=(NH, T // tq),
        in_specs=[qspec, qspec, pl.BlockSpec((T, HD), lambda h, i: (0, h)), pl.BlockSpec((T, HD), lambda h, i: (0, NH + h)),
                  pl.BlockSpec((T, HD), lambda h, i: (0, 0))],
        out_specs=qspec, out_shape=jax.ShapeDtypeStruct((T, NH * HD), BF16), compiler_params=_params(2))(qn, qr, kv, kv, kr)


def _mla_attn_bwd(name, qn, qr, kv, kr, do, tq=256):
    T = qn.shape[0]
    tq = min(tq, T)

    def body(qn_ref, qr_ref, kn_ref, v_ref, kr_ref, do_ref, dqn_ref, dqr_ref, dkn_ref, dv_ref, dkr_ref):
        h, i = pl.program_id(0), pl.program_id(1)

        @pl.when(i == 0)
        def _():
            dkn_ref[...] = jnp.zeros_like(dkn_ref)
            dv_ref[...] = jnp.zeros_like(dv_ref)

        @pl.when((i == 0) & (h == 0))
        def _():
            dkr_ref[...] = jnp.zeros_like(dkr_ref)

        qn, qr, kn, kr, v, do = qn_ref[...], qr_ref[...], kn_ref[...], kr_ref[...], v_ref[...], do_ref[...]
        p = _att_probs(qn, qr, kn, kr, i * tq)
        dp = _dot(do, v, "nt")
        ds = (p * (dp - jnp.sum(p * dp, axis=1, keepdims=True)) * ATT_SCALE).astype(BF16)
        dqn_ref[...] = _dot(ds, kn, "nn")
        dqr_ref[...] = _dot(ds, kr, "nn")
        dkn_ref[...] += _dot(ds, qn, "tn")
        dkr_ref[...] += _dot(ds, qr, "tn")
        dv_ref[...] += _dot(p.astype(BF16), do, "tn")

    qspec = pl.BlockSpec((tq, HD), lambda h, i: (i, h))
    kspec = pl.BlockSpec((T, HD), lambda h, i: (0, h))
    return pl.pallas_call(
        body, name=name, grid=(NH, T // tq),
        in_specs=[qspec, qspec, kspec, pl.BlockSpec((T, HD), lambda h, i: (0, NH + h)),
                  pl.BlockSpec((T, HD), lambda h, i: (0, 0)), qspec],
        out_specs=[qspec, qspec, kspec, kspec, pl.BlockSpec((T, HD), lambda h, i: (0, 0))],
        out_shape=[jax.ShapeDtypeStruct((T, NH * HD), F32)] * 4 + [jax.ShapeDtypeStruct((T, HD), F32)],
        compiler_params=_params(2))(qn, qr, kv, kv, kr, do)


def _mod_rows(mod, layer):
    return [mod[layer:layer + 1, i * D:(i + 1) * D] for i in range(N_MOD)]


def _local_step(x, target, pos_col, mod, W, P):
    cos, sin = _rope_tables(pos_col)
    saved = []
    for l in range(DEPTH):
        j = l // 2
        sh_m, sc_m, ga_m, sh_f, sc_f, ga_f = _mod_rows(mod, l)
        s = dict(x0=x)
        h = _norm_mod_fwd(f"norm_mix{l}", x, P["norm_mix_g"][l:l + 1], sc_m, sh_m)
        s["h"] = h
        if l % 2 == 0:
            proj = _mm(f"gdn_in{j}", h, W["gdn_in"][j], "nn")
            qkv = _gdn_conv_fwd(f"gdn_conv{j}", proj, P["gdn_cw"][j])
            g, beta = _gdn_gates_fwd(f"gdn_gates{j}", proj, P["gdn_alog"][j], P["gdn_dtb"][j])
            o, states = _gdn_chunk_fwd(f"gdn_chunk{j}", qkv, g, beta)
            on = _gdn_gated_norm_fwd(f"gdn_gnorm{j}", o, proj, P["gdn_ng"][j])
            y = _mm(f"gdn_out{j}", on, W["gdn_out"][j], "nn")
            s.update(proj=proj, qkv=qkv, g=g, beta=beta, o=o, states=states, on=on)
        else:
            proj = _mm(f"mla_in{j}", h, W["mla_in"][j], "nn")
            cqn, ckvn, kr = _mla_pre_fwd(f"mla_pre{j}", proj, cos, sin, P["mla_qg"][j], P["mla_kvg"][j])
            q = _mm(f"mla_uq{j}", cqn, W["mla_uq"][j], "nn")
            kv = _mm(f"mla_ukv{j}", ckvn, W["mla_ukv"][j], "nn", out_dtype=BF16)
            qn, qr = _mla_q_fwd(f"mla_q{j}", q, cos, sin)
            o = _mla_attn_fwd(f"mla_attn{j}", qn, qr, kv, kr)
            y = _mm(f"mla_out{j}", o, W["mla_out"][j], "nn")
            s.update(proj=proj, cqn=cqn, ckvn=ckvn, kr=kr, kv=kv, qn=qn, qr=qr, o=o)
        s["y"] = y
        x = _residual_fwd(f"res_mix{l}", x, y, ga_m)
        s["x1"] = x
        h2 = _norm_mod_fwd(f"norm_ffn{l}", x, P["norm_ffn_g"][l:l + 1], sc_f, sh_f)
        fa, fb, sw = _ffn_up(f"ffn_up{l}", h2, W["ffn_g"], W["ffn_u"], l)
        yf = _ffn_down(f"ffn_down{l}", sw, W["ffn_d"], l)
        x = _residual_fwd(f"res_ffn{l}", x, yf, ga_f)
        s.update(h2=h2, fa=fa, fb=fb, sw=sw, yf=yf)
        saved.append(s)

    dx, loss, d_final = _loss_head(x, target, P["final_g"])
    gW = {k: [None] * (DEPTH if k.startswith("ffn") else 2) for k in W}
    gP = dict(loss=loss, final_g=d_final, norm_mix_g=[None] * DEPTH, norm_ffn_g=[None] * DEPTH,
              gdn_cw=[None] * 2, gdn_alog=[None] * 2, gdn_dtb=[None] * 2, gdn_ng=[None] * 2,
              mla_qg=[None] * 2, mla_kvg=[None] * 2)
    dmod = [None] * DEPTH
    for l in reversed(range(DEPTH)):
        j = l // 2
        s = saved[l]
        sh_m, sc_m, ga_m, sh_f, sc_f, ga_f = _mod_rows(mod, l)
        dyf, d_ga_f = _residual_bwd(f"res_ffn_b{l}", dx, s["yf"], ga_f)
        da, db = _ffn_down_bwd(f"ffn_down_dx{l}", dyf, W["ffn_d"], s["fa"], s["fb"], l)
        gW["ffn_d"][l] = _ffn_down_dw(f"ffn_down_dw{l}", s["sw"], dyf)
        gW["ffn_g"][l], gW["ffn_u"][l] = _ffn_up_dw(f"ffn_up_dw{l}", s["h2"], da, db)
        dh2 = _ffn_up_dx(f"ffn_up_dx{l}", da, db, W["ffn_g"], W["ffn_u"], l)
        dx, d_sh_f, d_sc_f, gP["norm_ffn_g"][l] = _norm_mod_bwd(f"norm_ffn_b{l}", dh2, s["x1"], dx,
                                                                 P["norm_ffn_g"][l:l + 1], sc_f)
        dy, d_ga_m = _residual_bwd(f"res_mix_b{l}", dx, s["y"], ga_m)
        if l % 2 == 0:
            don = _mm(f"gdn_out_dx{j}", dy, W["gdn_out"][j], "nt")
            gW["gdn_out"][j] = _mm(f"gdn_out_dw{j}", s["on"], dy, "tn", out_dtype=BF16)
            do, dgate, gP["gdn_ng"][j] = _gdn_gated_norm_bwd(f"gdn_gnorm_b{j}", don, s["o"], s["proj"], P["gdn_ng"][j])
            dq, dk, dv, dg_h, db_h = _gdn_chunk_bwd(f"gdn_chunk_b{j}", s["qkv"], s["g"], s["beta"], s["states"], do)
            dab_, gP["gdn_alog"][j], gP["gdn_dtb"][j] = _gdn_gates_bwd(f"gdn_gates_b{j}", s["proj"], dg_h, db_h,
                                                                        P["gdn_alog"][j], P["gdn_dtb"][j])
            dpre, gP["gdn_cw"][j] = _gdn_conv_bwd(f"gdn_conv_b{j}", s["proj"], P["gdn_cw"][j],
                                                  jnp.concatenate([dq, dk, dv], axis=1))
            dproj = jnp.concatenate([dpre, dgate, dab_], axis=1)
            gW["gdn_in"][j] = _mm(f"gdn_in_dw{j}", s["h"], dproj, "tn", out_dtype=BF16)
            dh = _mm(f"gdn_in_dx{j}", dproj, W["gdn_in"][j], "nt")
        else:
            do = _mm(f"mla_out_dx{j}", dy, W["mla_out"][j], "nt", out_dtype=BF16)
            gW["mla_out"][j] = _mm(f"mla_out_dw{j}", s["o"], dy, "tn", out_dtype=BF16)
            dqn, dqr, dkn, dv, dkr = _mla_attn_bwd(f"mla_attn_b{j}", s["qn"], s["qr"], s["kv"], s["kr"], do)
            dq = _mla_q_bwd(f"mla_q_b{j}", dqn, dqr, cos, sin)
            dkv = jnp.concatenate([dkn, dv], axis=1)
            gW["mla_uq"][j] = _mm(f"mla_uq_dw{j}", s["cqn"], dq, "tn", out_dtype=BF16)
            dcqn = _mm(f"mla_uq_dx{j}", dq, W["mla_uq"][j], "nt")
            gW["mla_ukv"][j] = _mm(f"mla_ukv_dw{j}", s["ckvn"], dkv, "tn", out_dtype=BF16)
            dckvn = _mm(f"mla_ukv_dx{j}", dkv, W["mla_ukv"][j], "nt")
            dproj, gP["mla_qg"][j], gP["mla_kvg"][j] = _mla_pre_bwd(f"mla_pre_b{j}", s["proj"], dcqn, dckvn, dkr, cos, sin,
                                                                     P["mla_qg"][j], P["mla_kvg"][j])
            gW["mla_in"][j] = _mm(f"mla_in_dw{j}", s["h"], dproj, "tn", out_dtype=BF16)
            dh = _mm(f"mla_in_dx{j}", dproj, W["mla_in"][j], "nt")
        dx, d_sh_m, d_sc_m, gP["norm_mix_g"][l] = _norm_mod_bwd(f"norm_mix_b{l}", dh, s["x0"], dx,
                                                                 P["norm_mix_g"][l:l + 1], sc_m)
        dmod[l] = jnp.concatenate([d_sh_m, d_sc_m, d_ga_m, d_sh_f, d_sc_f, d_ga_f], axis=1)
    return dx, jnp.concatenate(dmod, axis=0), gW, gP


def _pad_cols(a, width):
    return jnp.pad(a, ((0, 0), (0, width - a.shape[1])))


def _gdn_in_to_kernel(w):
    m = GDN_QKV + NH * HD
    return jnp.concatenate([w[:, :m], _pad_cols(w[:, m:m + NH], HD), _pad_cols(w[:, m + NH:], HD)], axis=1)


def _gdn_in_from_kernel(g):
    m = GDN_QKV + NH * HD
    return jnp.concatenate([g[:, :m], g[:, m:m + NH], g[:, m + HD:m + HD + NH]], axis=1)


def _mla_uq_to_kernel(w):
    w3 = w.reshape(Q_RANK, NH, HD + ROPE)
    rope = jnp.pad(w3[:, :, HD:], ((0, 0), (0, 0), (0, HD - ROPE)))
    return jnp.concatenate([w3[:, :, :HD].reshape(Q_RANK, NH * HD), rope.reshape(Q_RANK, NH * HD)], axis=1)


def _mla_uq_from_kernel(g):
    gn = g[:, :NH * HD].reshape(Q_RANK, NH, HD)
    gr = g[:, NH * HD:].reshape(Q_RANK, NH, HD)[:, :, :ROPE]
    return jnp.concatenate([gn, gr], axis=2).reshape(Q_RANK, NH * (HD + ROPE))


def _mla_ukv_to_kernel(w):
    w3 = w.reshape(KV_RANK, NH, 2 * HD)
    return jnp.concatenate([w3[:, :, :HD].reshape(KV_RANK, NH * HD), w3[:, :, HD:].reshape(KV_RANK, NH * HD)], axis=1)


def _mla_ukv_from_kernel(g):
    gk = g[:, :NH * HD].reshape(KV_RANK, NH, HD)
    gv = g[:, NH * HD:].reshape(KV_RANK, NH, HD)
    return jnp.concatenate([gk, gv], axis=2).reshape(KV_RANK, NH * 2 * HD)


def _layers(a, n_layers):
    r = a.shape[1] // n_layers
    return [a[:, j * r:(j + 1) * r] for j in range(n_layers)]


def _cols(t):
    return jnp.moveaxis(t, 0, 1).reshape(t.shape[1], -1)


def _uncols(g):
    return jnp.moveaxis(g.reshape(g.shape[0], 4, -1), 1, 0)


def _rows(t):
    return t.reshape(-1, t.shape[2])


def _unrows(g):
    return g.reshape(4, -1, g.shape[1])


def _weights_to_kernel(got):
    return dict(
        gdn_in=[_gdn_in_to_kernel(_cols(t)) for t in _layers(got["gdn_w_in"], 2)],
        gdn_out=[_rows(t) for t in _layers(got["gdn_w_out"], 2)],
        mla_in=[_pad_cols(_rows(t), MLA_INK) for t in _layers(got["mla_w_in"], 2)],
        mla_uq=[_mla_uq_to_kernel(_cols(t)) for t in _layers(got["mla_w_uq"], 2)],
        mla_ukv=[_mla_ukv_to_kernel(_cols(t)) for t in _layers(got["mla_w_ukv"], 2)],
        mla_out=[_rows(t) for t in _layers(got["mla_w_out"], 2)],
        ffn_g=got["ffn_w_gate"], ffn_u=got["ffn_w_up"], ffn_d=got["ffn_w_down"],
    )


def _grads_to_chips(gW):
    cat = lambda parts: jnp.concatenate(parts, axis=1)
    return dict(
        gdn_w_in=cat([_uncols(_gdn_in_from_kernel(g)) for g in gW["gdn_in"]]),
        gdn_w_out=cat([_unrows(g) for g in gW["gdn_out"]]),
        mla_w_in=cat([_unrows(g[:, :Q_RANK + KV_RANK + ROPE]) for g in gW["mla_in"]]),
        mla_w_uq=cat([_uncols(_mla_uq_from_kernel(g)) for g in gW["mla_uq"]]),
        mla_w_ukv=cat([_uncols(_mla_ukv_from_kernel(g)) for g in gW["mla_ukv"]]),
        mla_w_out=cat([_unrows(g) for g in gW["mla_out"]]),
        ffn_w_gate=cat(gW["ffn_g"]), ffn_w_up=cat(gW["ffn_u"]), ffn_w_down=cat(gW["ffn_d"]),
    )


def _small_to_kernel(norm_mix_g, norm_ffn_g, final_norm_g, gdn_conv_w, gdn_a_log, gdn_dt_bias, gdn_norm_g, q_norm_g, kv_norm_g):
    return dict(
        norm_mix_g=norm_mix_g, norm_ffn_g=norm_ffn_g, final_g=final_norm_g.reshape(1, D),
        gdn_cw=[jnp.transpose(gdn_conv_w[j]) for j in range(2)],
        gdn_alog=[_pad_cols(gdn_a_log[j:j + 1], HD) for j in range(2)],
        gdn_dtb=[_pad_cols(gdn_dt_bias[j:j + 1], HD) for j in range(2)],
        gdn_ng=[gdn_norm_g[j:j + 1] for j in range(2)],
        mla_qg=[q_norm_g[j:j + 1] for j in range(2)],
        mla_kvg=[kv_norm_g[j:j + 1] for j in range(2)],
    )


_CHIP_FLIPS = ((1, 0), (0, 1), (1, 1))
_ANY = pl.BlockSpec(memory_space=pl.ANY)


def _me():
    return lax.axis_index("x"), lax.axis_index("y"), lax.axis_index("c")


def _chip_peer(dx, dy):
    x, y, c = _me()
    return ((1 - x) if dx else x, (1 - y) if dy else y, c)


def _rcopy(src, dst, send_sem, recv_sem, to):
    return pltpu.make_async_remote_copy(src_ref=src, dst_ref=dst, send_sem=send_sem, recv_sem=recv_sem,
                                        device_id=to, device_id_type=MESH)


def _allgather4(name, a, halves=False):
    R, C = a.shape
    rh = R // 2 if halves else R

    def body(a_ref, out_ref, send_sems, recv_sems, local_sem):
        x, y, c = _me()
        me = 2 * x + y
        src = a_ref.at[pl.ds(c * rh, rh)] if halves else a_ref
        local = pltpu.make_async_copy(src, out_ref.at[me], local_sem)
        local.start()
        sends = []
        for k, (dx, dy) in enumerate(_CHIP_FLIPS):
            cp = _rcopy(src, out_ref.at[me], send_sems.at[k], recv_sems.at[k], _chip_peer(dx, dy))
            cp.start()
            sends.append(cp)
        for k, (dx, dy) in enumerate(_CHIP_FLIPS):
            px, py, _ = _chip_peer(dx, dy)
            _rcopy(src, out_ref.at[2 * px + py], send_sems.at[k], recv_sems.at[k], _chip_peer(dx, dy)).wait_recv()
        for cp in sends:
            cp.wait_send()
        local.wait()

    return pl.pallas_call(
        body, name=name, in_specs=[_ANY], out_specs=_ANY, out_shape=jax.ShapeDtypeStruct((4, rh, C), a.dtype),
        scratch_shapes=[pltpu.SemaphoreType.DMA((3,)), pltpu.SemaphoreType.DMA((3,)), pltpu.SemaphoreType.DMA(())])(a)


_NCH = 4


def _dma_sems(*counts):
    return [pltpu.SemaphoreType.DMA((n,)) for n in counts]


def _slot_tile(rows):
    tr = 256
    while rows % tr:
        tr //= 2
    return tr


def _cast_into_slot(name, a, chip):
    R, C = a.shape
    tr = _slot_tile(R)

    def body(c_ref, a_ref, o_ref):
        o_ref[0] = a_ref[...].astype(o_ref.dtype)

    grid_spec = pltpu.PrefetchScalarGridSpec(
        num_scalar_prefetch=1, grid=(R // tr,), in_specs=[pl.BlockSpec((tr, C), lambda i, c_ref: (i, 0))],
        out_specs=pl.BlockSpec((1, tr, C), lambda i, c_ref: (c_ref[0], i, 0)))
    return pl.pallas_call(body, name=name, grid_spec=grid_spec, out_shape=jax.ShapeDtypeStruct((4, R, C), BF16),
                          compiler_params=_params(1))(chip, a)


def _own_slot(name, p, chip):
    _, h, C = p.shape
    tr = _slot_tile(h)

    def body(c_ref, p_ref, o_ref):
        o_ref[...] = p_ref[...]

    spec = pl.BlockSpec((1, tr, C), lambda i, c_ref: (c_ref[0], i, 0))
    grid_spec = pltpu.PrefetchScalarGridSpec(num_scalar_prefetch=1, grid=(h // tr,), in_specs=[spec], out_specs=spec)
    return pl.pallas_call(body, name=name, grid_spec=grid_spec, out_shape=jax.ShapeDtypeStruct(p.shape, p.dtype),
                          compiler_params=_params(1))(chip, p)


def _gather_weights(name, bufs):
    n = len(bufs)

    def body(*refs):
        out = refs[n:2 * n]
        ici_s, ici_r, d2d_s, d2d_r = refs[2 * n:]
        x, y, c = _me()
        me = 2 * x + y
        sib = (x, y, 1 - c)
        peers = [_chip_peer(dx, dy) for dx, dy in _CHIP_FLIPS]
        for t in range(n):
            h = out[t].shape[1] // 2
            ch = h // _NCH
            for k, peer in enumerate(peers):
                for i in range(_NCH):
                    blk = out[t].at[me, pl.ds(c * h + i * ch, ch)]
                    _rcopy(blk, blk, ici_s.at[3 * t + k], ici_r.at[3 * t + k], peer).start()
        for t in range(n):
            h = out[t].shape[1] // 2
            ch = h // _NCH
            for k, peer in enumerate(peers):
                pchip = 2 * peer[0] + peer[1]
                got = out[t].at[pchip, pl.ds(c * h, h)]
                _rcopy(got, got, ici_s.at[3 * t + k], ici_r.at[3 * t + k], peer).wait_recv()
                for i in range(_NCH):
                    blk = out[t].at[pchip, pl.ds(c * h + i * ch, ch)]
                    _rcopy(blk, blk, d2d_s.at[3 * t + k], d2d_r.at[3 * t + k], sib).start()
        for t in range(n):
            h = out[t].shape[1] // 2
            for k, peer in enumerate(peers):
                pchip = 2 * peer[0] + peer[1]
                other = out[t].at[pchip, pl.ds((1 - c) * h, h)]
                _rcopy(other, other, d2d_s.at[3 * t + k], d2d_r.at[3 * t + k], sib).wait_recv()
                _rcopy(other, other, ici_s.at[3 * t + k], ici_r.at[3 * t + k], peer).wait_send()
                _rcopy(other, other, d2d_s.at[3 * t + k], d2d_r.at[3 * t + k], sib).wait_send()

    return pl.pallas_call(
        body, name=name, in_specs=[_ANY] * n, out_specs=[_ANY] * n,
        out_shape=[jax.ShapeDtypeStruct(s.shape, s.dtype) for s in bufs],
        input_output_aliases={t: t for t in range(n)},
        scratch_shapes=_dma_sems(3 * n, 3 * n, 3 * n, 3 * n))(*bufs)


def _rs_split(name, grads):
    n = len(grads)

    def body(*refs):
        g, out = refs[:n], refs[n:2 * n]
        send, recv = refs[2 * n:]
        x, y, c = _me()
        sib = (x, y, 1 - c)
        for t in range(n):
            h = g[t].shape[1] // 2
            for d in range(4):
                _rcopy(g[t].at[d, pl.ds((1 - c) * h, h)], out[t].at[d], send.at[t], recv.at[t], sib).start()
        for t in range(n):
            _rcopy(out[t], out[t], send.at[t], recv.at[t], sib).wait()

    return pl.pallas_call(
        body, name=name, in_specs=[_ANY] * n, out_specs=[_ANY] * n,
        out_shape=[jax.ShapeDtypeStruct((4, s.shape[1] // 2, s.shape[2]), s.dtype) for s in grads],
        scratch_shapes=_dma_sems(n, n))(*grads)


def _pair_add(name, g, theirs, core):
    _, R, C = g.shape
    h = R // 2
    tr = 256
    while h % tr:
        tr //= 2
    nb = h // tr

    def body(c_ref, g_ref, t_ref, o_ref):
        o_ref[...] = (g_ref[...].astype(F32) + t_ref[...].astype(F32)).astype(o_ref.dtype)

    spec = pl.BlockSpec((1, tr, C), lambda d, i, c_ref: (d, i, 0))
    grid_spec = pltpu.PrefetchScalarGridSpec(
        num_scalar_prefetch=1, grid=(4, nb),
        in_specs=[pl.BlockSpec((1, tr, C), lambda d, i, c_ref: (d, c_ref[0] * nb + i, 0)), spec], out_specs=spec)
    return pl.pallas_call(body, name=name, grid_spec=grid_spec, out_shape=jax.ShapeDtypeStruct((4, h, C), BF16),
                          compiler_params=_params(2))(core, g, theirs)


def _rs_alltoall(name, parts, bufs):
    n = len(parts)

    def body(*refs):
        p, out = refs[:n], refs[2 * n:3 * n]
        send, recv = refs[3 * n:]
        x, y, c = _me()
        me = 2 * x + y
        peers = [_chip_peer(dx, dy) for dx, dy in _CHIP_FLIPS]
        for t in range(n):
            ch = p[t].shape[1] // _NCH
            for k, peer in enumerate(peers):
                pchip = 2 * peer[0] + peer[1]
                for i in range(_NCH):
                    rows = pl.ds(i * ch, ch)
                    _rcopy(p[t].at[pchip, rows], out[t].at[me, rows], send.at[3 * t + k], recv.at[3 * t + k], peer).start()
        for t in range(n):
            for k, peer in enumerate(peers):
                pchip = 2 * peer[0] + peer[1]
                _rcopy(out[t].at[pchip], out[t].at[pchip], send.at[3 * t + k], recv.at[3 * t + k], peer).wait()

    return pl.pallas_call(
        body, name=name, in_specs=[_ANY] * (2 * n), out_specs=[_ANY] * n,
        out_shape=[jax.ShapeDtypeStruct(s.shape, s.dtype) for s in bufs],
        input_output_aliases={n + t: t for t in range(n)},
        scratch_shapes=_dma_sems(3 * n, 3 * n))(*parts, *bufs)


def _rs_swap(name, halves):
    n = len(halves)

    def body(*refs):
        a, out = refs[:n], refs[n:2 * n]
        send, recv = refs[2 * n:]
        x, y, c = _me()
        sib = (x, y, 1 - c)
        for t in range(n):
            ch = a[t].shape[0] // _NCH
            for i in range(_NCH):
                rows = pl.ds(i * ch, ch)
                _rcopy(a[t].at[rows], out[t].at[rows], send.at[t], recv.at[t], sib).start()
        for t in range(n):
            _rcopy(a[t], out[t], send.at[t], recv.at[t], sib).wait()

    return pl.pallas_call(
        body, name=name, in_specs=[_ANY] * n, out_specs=[_ANY] * n,
        out_shape=[jax.ShapeDtypeStruct(s.shape, s.dtype) for s in halves],
        scratch_shapes=_dma_sems(n, n))(*halves)


def _sibling_merge(name, a):
    P_, rh, C = a.shape

    def body(a_ref, out_ref, send_sem, recv_sem, local_sem):
        x, y, c = _me()
        local = pltpu.make_async_copy(a_ref, out_ref.at[:, pl.ds(c * rh, rh)], local_sem)
        local.start()
        cp = _rcopy(a_ref, out_ref.at[:, pl.ds(c * rh, rh)], send_sem, recv_sem, (x, y, 1 - c))
        cp.start()
        cp.wait_send()
        _rcopy(a_ref, out_ref.at[:, pl.ds((1 - c) * rh, rh)], send_sem, recv_sem, (x, y, 1 - c)).wait_recv()
        local.wait()

    return pl.pallas_call(
        body, name=name, in_specs=[_ANY], out_specs=_ANY, out_shape=jax.ShapeDtypeStruct((P_, 2 * rh, C), a.dtype),
        scratch_shapes=[pltpu.SemaphoreType.DMA(()), pltpu.SemaphoreType.DMA(()), pltpu.SemaphoreType.DMA(())])(a)


def _allgather8(name, a):
    g4 = _allgather4(name + "_chips", a)
    both = _sibling_merge(name + "_cores", g4.reshape(1, 4 * a.shape[0], a.shape[1]))
    return jnp.transpose(both.reshape(2, 4, *a.shape), (1, 0, 2, 3)).reshape(8, *a.shape)


def _sum_slots(name, a, out_dtype):
    def fn(a):
        acc = a[0].astype(F32)
        for k in range(1, a.shape[0]):
            acc = acc + a[k].astype(F32)
        return acc
    return _rowwise(name, fn, [a], [], [(a.shape[2], out_dtype)])[0]


def _adamw_math(w, g, m, v):
    m = ADAM_B1 * m + (1.0 - ADAM_B1) * g
    v = ADAM_B2 * v + (1.0 - ADAM_B2) * (g * g)
    m_hat = m / (1.0 - ADAM_B1 ** ADAM_STEP)
    v_hat = v / (1.0 - ADAM_B2 ** ADAM_STEP)
    return -ADAM_LR * (m_hat / (jnp.sqrt(v_hat) + ADAM_EPS) + ADAM_WD * w), m, v


def _adamw_halves(name, w, m, v, mine, theirs):
    shape = w.shape
    w2, m2, v2 = [t.reshape(-1, shape[-1]) for t in (w, m, v)]
    R, C = w2.shape
    h = R // 2
    tr = _slot_tile(h)
    nb = h // tr

    def body(w_ref, m_ref, v_ref, a_ref, b_ref, g_ref, d_ref, nm_ref, nv_ref):
        g = jnp.where(pl.program_id(0) == lax.axis_index("c"), a_ref[...], b_ref[...])
        g_ref[...] = g
        d_ref[...], nm_ref[...], nv_ref[...] = _adamw_math(w_ref[...], g, m_ref[...], v_ref[...])

    full = pl.BlockSpec((tr, C), lambda s, i: (s * nb + i, 0))
    half = pl.BlockSpec((tr, C), lambda s, i: (i, 0))
    outs = pl.pallas_call(body, name=name, grid=(2, nb), in_specs=[full, full, full, half, half], out_specs=[full] * 4,
                          out_shape=[jax.ShapeDtypeStruct((R, C), F32)] * 4, compiler_params=_params(2))(w2, m2, v2, mine, theirs)
    return [o.reshape(shape) for o in outs]


def _adamw(name, w, g, m, v):
    shape = w.shape
    two_d = (-1, shape[-1]) if w.ndim > 1 else (1, -1)
    w2, g2, m2, v2 = [t.reshape(two_d) for t in (w, g, m, v)]
    rows = w2.shape[0]
    tr = rows
    for cand in (256, 128, 64, 32, 16, 8):
        if rows % cand == 0:
            tr = cand
            break

    c = w2.shape[1]
    outs = _rowwise(name, _adamw_math, [w2, g2, m2, v2], [], [(c, F32)] * 3, tr=tr)
    return [o.reshape(shape) for o in outs]


_WEIGHT_ORDER = ("ada_w", "ada_b", "norm_mix_g", "norm_ffn_g", "gdn_w_in", "gdn_conv_w", "gdn_a_log", "gdn_dt_bias",
                 "gdn_norm_g", "gdn_w_out", "mla_w_in", "mla_q_norm_g", "mla_kv_norm_g", "mla_w_uq", "mla_w_ukv",
                 "mla_w_out", "ffn_w_gate", "ffn_w_up", "ffn_w_down", "final_norm_g")
_BIG = (("gdn_w_in", 2), ("gdn_w_out", 1), ("mla_w_in", 1), ("mla_w_uq", 2), ("mla_w_ukv", 2), ("mla_w_out", 1),
        ("ffn_w_gate", 2), ("ffn_w_up", 2), ("ffn_w_down", 1))
_SMALL_SHARDED = (("gdn_conv_w", 1), ("mla_q_norm_g", 1), ("mla_kv_norm_g", 1))


def _size(shape):
    n = 1
    for s in shape:
        n *= s
    return n


def _pack_rows_each(tensors):
    parts, offs, off = [], [], 0
    for t in tensors:
        flat = t.reshape(-1).astype(F32)
        rows = -(-flat.shape[0] // PACK_W)
        parts.append(jnp.pad(flat, (0, rows * PACK_W - flat.shape[0])).reshape(rows, PACK_W))
        offs.append(off)
        off += rows
    pad = -(-off // 16) * 16 - off
    if pad:
        parts.append(jnp.zeros((pad, PACK_W), F32))
    return jnp.concatenate(parts, axis=0), offs


def _unpack_rows_each(pack, shapes):
    lead = pack.shape[:-2]
    out, off = [], 0
    for shp in shapes:
        n = _size(shp)
        rows = -(-n // PACK_W)
        out.append(pack[..., off:off + rows, :].reshape(*lead, -1)[..., :n].reshape(*lead, *shp))
        off += rows
    return out


def _merge_chips(stacked, axis):
    moved = jnp.moveaxis(stacked, 0, axis)
    shp = list(moved.shape)
    return moved.reshape(shp[:axis] + [shp[axis] * shp[axis + 1]] + shp[axis + 2:])


def _my_shard(full, axis, chip):
    n = full.shape[axis] // 4
    return lax.dynamic_slice_in_dim(full, chip * n, n, axis)


def kernel(x, c, positions, ada_w, ada_b, norm_mix_g, norm_ffn_g, gdn_w_in, gdn_conv_w, gdn_a_log, gdn_dt_bias, gdn_norm_g, gdn_w_out, mla_w_in, mla_q_norm_g, mla_kv_norm_g, mla_w_uq, mla_w_ukv, mla_w_out, ffn_w_gate, ffn_w_up, ffn_w_down, final_norm_g, loss_target, m_ada_w, m_ada_b, m_norm_mix_g, m_norm_ffn_g, m_gdn_w_in, m_gdn_conv_w, m_gdn_a_log, m_gdn_dt_bias, m_gdn_norm_g, m_gdn_w_out, m_mla_w_in, m_mla_q_norm_g, m_mla_kv_norm_g, m_mla_w_uq, m_mla_w_ukv, m_mla_w_out, m_ffn_w_gate, m_ffn_w_up, m_ffn_w_down, m_final_norm_g, v_ada_w, v_ada_b, v_norm_mix_g, v_norm_ffn_g, v_gdn_w_in, v_gdn_conv_w, v_gdn_a_log, v_gdn_dt_bias, v_gdn_norm_g, v_gdn_w_out, v_mla_w_in, v_mla_q_norm_g, v_mla_kv_norm_g, v_mla_w_uq, v_mla_w_ukv, v_mla_w_out, v_ffn_w_gate, v_ffn_w_up, v_ffn_w_down, v_final_norm_g):
    w = dict(ada_w=ada_w, ada_b=ada_b, norm_mix_g=norm_mix_g, norm_ffn_g=norm_ffn_g, gdn_w_in=gdn_w_in, gdn_conv_w=gdn_conv_w,
             gdn_a_log=gdn_a_log, gdn_dt_bias=gdn_dt_bias, gdn_norm_g=gdn_norm_g, gdn_w_out=gdn_w_out, mla_w_in=mla_w_in,
             mla_q_norm_g=mla_q_norm_g, mla_kv_norm_g=mla_kv_norm_g, mla_w_uq=mla_w_uq, mla_w_ukv=mla_w_ukv,
             mla_w_out=mla_w_out, ffn_w_gate=ffn_w_gate, ffn_w_up=ffn_w_up, ffn_w_down=ffn_w_down, final_norm_g=final_norm_g)
    m = dict(ada_w=m_ada_w, ada_b=m_ada_b, norm_mix_g=m_norm_mix_g, norm_ffn_g=m_norm_ffn_g, gdn_w_in=m_gdn_w_in,
             gdn_conv_w=m_gdn_conv_w, gdn_a_log=m_gdn_a_log, gdn_dt_bias=m_gdn_dt_bias, gdn_norm_g=m_gdn_norm_g,
             gdn_w_out=m_gdn_w_out, mla_w_in=m_mla_w_in, mla_q_norm_g=m_mla_q_norm_g, mla_kv_norm_g=m_mla_kv_norm_g,
             mla_w_uq=m_mla_w_uq, mla_w_ukv=m_mla_w_ukv, mla_w_out=m_mla_w_out, ffn_w_gate=m_ffn_w_gate,
             ffn_w_up=m_ffn_w_up, ffn_w_down=m_ffn_w_down, final_norm_g=m_final_norm_g)
    v = dict(ada_w=v_ada_w, ada_b=v_ada_b, norm_mix_g=v_norm_mix_g, norm_ffn_g=v_norm_ffn_g, gdn_w_in=v_gdn_w_in,
             gdn_conv_w=v_gdn_conv_w, gdn_a_log=v_gdn_a_log, gdn_dt_bias=v_gdn_dt_bias, gdn_norm_g=v_gdn_norm_g,
             gdn_w_out=v_gdn_w_out, mla_w_in=v_mla_w_in, mla_q_norm_g=v_mla_q_norm_g, mla_kv_norm_g=v_mla_kv_norm_g,
             mla_w_uq=v_mla_w_uq, mla_w_ukv=v_mla_w_ukv, mla_w_out=v_mla_w_out, ffn_w_gate=v_ffn_w_gate,
             ffn_w_up=v_ffn_w_up, ffn_w_down=v_ffn_w_down, final_norm_g=v_final_norm_g)
    T = x.shape[1]
    ix, iy, ic = _me()
    chip = 2 * ix + iy
    seq = 2 * chip + ic
    n_dev = 8

    small_shapes = [w[n].shape for n, _ in _SMALL_SHARDED] + [c.shape]
    pack0, _ = _pack_rows_each([w[n] for n, _ in _SMALL_SHARDED] + [c])
    got0 = _unpack_rows_each(_allgather8("gather_small", pack0), small_shapes)
    small_full = {n: _merge_chips(g[0::2], ax) for (n, ax), g in zip(_SMALL_SHARDED, got0)}
    c_all = got0[-1].reshape(n_dev, D)

    big = [n for n, _ in _BIG]
    chip_arr = chip.astype(jnp.int32).reshape(1)
    bufs = [_cast_into_slot("to_bf16_" + n, w[n].reshape(-1, w[n].shape[-1]), chip_arr) for n in big]
    W = _weights_to_kernel(dict(zip(big, _gather_weights("gather_weights", bufs))))
    P = _small_to_kernel(norm_mix_g, norm_ffn_g, final_norm_g, small_full["gdn_conv_w"], gdn_a_log, gdn_dt_bias,
                         gdn_norm_g, small_full["mla_q_norm_g"], small_full["mla_kv_norm_g"])

    c16 = jnp.pad(c_all, ((0, 16 - n_dev), (0, 0)))
    ca = _rowwise("cond_silu", lambda t: t * _sig(t), [c16], [], [(D, BF16)])[0]
    n_ada = ada_w.shape[2]
    mods = jnp.concatenate([_mm(f"ada_fwd{l}", ca, ada_w[l], "nn") for l in range(DEPTH)], axis=0)
    mods_all = _allgather4("gather_mod", mods).reshape(4, DEPTH, 16, n_ada)
    mod_mm = jnp.transpose(lax.dynamic_index_in_dim(mods_all, seq, axis=2, keepdims=False), (1, 0, 2)).reshape(DEPTH, 4 * n_ada)
    mod = _rowwise("mod_bias", lambda a, b: a + b, [mod_mm, ada_b], [], [(4 * n_ada, F32)])[0]

    dx, dmod, gW, gP = _local_step(x.reshape(T, D), loss_target.reshape(T, D), positions.reshape(T, 1), mod, W, P)

    partials = [dmod, jnp.concatenate(gP["norm_mix_g"]), jnp.concatenate(gP["norm_ffn_g"]), gP["final_g"],
                jnp.stack([jnp.transpose(g) for g in gP["gdn_cw"]]), jnp.concatenate(gP["gdn_alog"])[:, :NH],
                jnp.concatenate(gP["gdn_dtb"])[:, :NH], jnp.concatenate(gP["gdn_ng"]), jnp.concatenate(gP["mla_qg"]),
                jnp.concatenate(gP["mla_kvg"]), gP["loss"][:, :1]]
    part_shapes = [p.shape for p in partials]
    ppack, _ = _pack_rows_each(partials)
    pall = _allgather8("gather_partials", ppack)
    psum = _sum_slots("sum_partials", pall, F32)
    (g_ada_b, g_norm_mix, g_norm_ffn, g_final, g_conv_full, g_alog, g_dtb, g_gdn_ng, g_qg_full, g_kvg_full,
     loss_sum) = _unpack_rows_each(psum, part_shapes)
    dmod_all = _unpack_rows_each(pall, part_shapes[:1])[0]

    grads = dict(ada_b=g_ada_b, norm_mix_g=g_norm_mix, norm_ffn_g=g_norm_ffn, final_norm_g=g_final.reshape(D),
                 gdn_conv_w=_my_shard(g_conv_full, 1, chip), gdn_a_log=g_alog, gdn_dt_bias=g_dtb, gdn_norm_g=g_gdn_ng,
                 mla_q_norm_g=_my_shard(g_qg_full, 1, chip), mla_kv_norm_g=_my_shard(g_kvg_full, 1, chip))

    ca_t = jnp.zeros((D, LANES), BF16).at[:, :16].set(jnp.transpose(ca))
    dm_mine = lax.dynamic_slice_in_dim(dmod_all, chip * n_ada, n_ada, axis=2)
    grads["ada_w"] = jnp.stack([
        _mm(f"ada_bwd{l}", ca_t, jnp.pad(dm_mine[:, l], ((0, LANES - n_dev), (0, 0))), "nn") for l in range(DEPTH)])

    gchips = _grads_to_chips(gW)
    glist = [gchips[n] for n in big]
    theirs = _rs_split("grads_cores", glist)
    core = ic.astype(jnp.int32).reshape(1)
    pairs = [_pair_add("grads_pair_" + n, g, t, core) for n, g, t in zip(big, glist, theirs)]
    own = [_own_slot("grads_own_" + n, p, chip_arr) for n, p in zip(big, pairs)]
    swapped = _rs_alltoall("grads_chips", pairs, own)
    halves = [_sum_slots("grads_sum_" + n, s, F32) for n, s in zip(big, swapped)]
    other_halves = _rs_swap("grads_swap", halves)

    delta, new_m, new_v = {}, {}, {}
    for n, mine, theirs in zip(big, halves, other_halves):
        grads[n], delta[n], new_m[n], new_v[n] = _adamw_halves("adamw_" + n, w[n], m[n], v[n], mine, theirs)
    delta["ada_w"], new_m["ada_w"], new_v["ada_w"] = _adamw("adamw_ada_w", ada_w, grads["ada_w"], m_ada_w, v_ada_w)
    small_names = [n for n in _WEIGHT_ORDER if n not in delta]
    small_shapes = [w[n].shape for n in small_names]
    packs = [_pack_rows_each([d[n] for n in small_names])[0] for d in (w, grads, m, v)]
    for d, pk in zip((delta, new_m, new_v), _adamw("adamw_small", *packs)):
        for n, t in zip(small_names, _unpack_rows_each(pk, small_shapes)):
            d[n] = t

    loss = loss_sum.reshape(())
    return (loss, dx.reshape(1, T, D), *[grads[n] for n in _WEIGHT_ORDER], *[delta[n] for n in _WEIGHT_ORDER],
            *[new_m[n] for n in _WEIGHT_ORDER], *[new_v[n] for n in _WEIGHT_ORDER])
```

```python
import functools

import jax
import jax.numpy as jnp
from jax import lax
from jax.experimental import pallas as pl
from jax.experimental.pallas import tpu as pltpu

F32 = jnp.float32
BF16 = jnp.bfloat16
HI = lax.Precision.HIGHEST
MESH = pl.DeviceIdType.MESH

D = 1024
DEPTH = 4
N_MOD = 6
NH = 8
HD = 128
CHUNK = 64
_GDN_HB = 4
GDN_QKV = 3 * NH * HD
GDN_INK = GDN_QKV + NH * HD + 2 * HD
Q_RANK, KV_RANK, ROPE = 384, 256, 64
MLA_INK = Q_RANK + KV_RANK + HD
DFF = 2816
EPS = 1e-6
ATT_SCALE = (HD + ROPE) ** -0.5
ROPE_THETA = 10000.0
LANES = 128
PACK_W = 1024

ADAM_LR, ADAM_B1, ADAM_B2, ADAM_EPS, ADAM_WD, ADAM_STEP = 0.001, 0.9, 0.999, 1e-08, 0.01, 10


H3 = "bf16x3"
B1 = "bf16"
HS = H3


def _dot(a, b, mode="nn", prec=None):
    dn = {"nn": (((1,), (0,)), ((), ())), "nt": (((1,), (1,)), ((), ())), "tn": (((0,), (0,)), ((), ()))}[mode]
    if prec == B1:
        return _dot(a.astype(BF16), b.astype(BF16), mode)
    if prec == H3:
        ah, bh = a.astype(BF16), b.astype(BF16)
        al, bl = (a - ah.astype(F32)).astype(BF16), (b - bh.astype(F32)).astype(BF16)
        return _dot(ah, bh, mode) + (_dot(ah, bl, mode) + _dot(al, bh, mode))
    return lax.dot_general(a, b, dn, precision=prec, preferred_element_type=F32)


def _sig(x):
    return 1.0 / (1.0 + jnp.exp(-x))


def _pick(n, cap):
    if n <= cap:
        return n
    best = None
    for d in range(LANES, cap + 1, LANES):
        if n % d == 0:
            best = d
    assert best is not None, (n, cap)
    return best


def _params(n_grid):
    return pltpu.CompilerParams(dimension_semantics=("arbitrary",) * n_grid, vmem_limit_bytes=56 * 1024 * 1024)


def _rowwise(name, fn, rows, consts, outs, sums=(), tr=256):
    first = rows[0][0] if isinstance(rows[0], tuple) else rows[0]
    T = first.shape[-2]
    tr = min(tr, T)
    while T % tr:
        tr //= 2
    nr, nc, no, ns = len(rows), len(consts), len(outs), len(sums)

    def body(*refs):
        res = fn(*[r[...] for r in refs[:nr + nc]])
        if not isinstance(res, (tuple, list)):
            res = (res,)
        o_refs = refs[nr + nc:nr + nc + no]
        s_refs = refs[nr + nc + no:]
        for r, val in zip(o_refs, res[:no]):
            r[...] = val.astype(r.dtype)
        if ns:
            @pl.when(pl.program_id(0) == 0)
            def _():
                for r in s_refs:
                    r[...] = jnp.zeros_like(r)
            for r, val in zip(s_refs, res[no:]):
                r[...] += val

    in_specs, args = [], []
    for a in rows:
        if isinstance(a, tuple):
            arr, width, cb = a
            in_specs.append(pl.BlockSpec((tr, width), lambda i, cb=cb: (i, cb)))
            args.append(arr)
        elif a.ndim == 3:
            in_specs.append(pl.BlockSpec((a.shape[0], tr, a.shape[2]), lambda i: (0, i, 0)))
            args.append(a)
        else:
            in_specs.append(pl.BlockSpec((tr, a.shape[1]), lambda i: (i, 0)))
            args.append(a)
    for a in consts:
        in_specs.append(pl.BlockSpec(a.shape, lambda i, nd=a.ndim: (0,) * nd))
        args.append(a)
    out_specs = [pl.BlockSpec((tr, w), lambda i: (i, 0)) for w, _ in outs]
    out_specs += [pl.BlockSpec((1, w), lambda i: (0, 0)) for w in sums]
    out_shape = [jax.ShapeDtypeStruct((T, w), dt) for w, dt in outs]
    out_shape += [jax.ShapeDtypeStruct((1, w), F32) for w in sums]
    res = pl.pallas_call(body, name=name, grid=(T // tr,), in_specs=in_specs, out_specs=out_specs,
                         out_shape=out_shape, compiler_params=_params(1))(*args)
    return res


def _mm(name, a, b, mode, out_dtype=F32, tm=512, tn=1024):
    if mode == "tn":
        K, M = a.shape
    else:
        M, K = a.shape
    N = b.shape[0] if mode == "nt" else b.shape[1]
    tm, tn = _pick(M, tm), _pick(N, tn)

    def body(a_ref, b_ref, o_ref):
        o_ref[...] = _dot(a_ref[...].astype(BF16), b_ref[...].astype(BF16), mode).astype(o_ref.dtype)

    a_spec = pl.BlockSpec((K, tm), lambda i, j: (0, i)) if mode == "tn" else pl.BlockSpec((tm, K), lambda i, j: (i, 0))
    b_spec = pl.BlockSpec((tn, K), lambda i, j: (j, 0)) if mode == "nt" else pl.BlockSpec((K, tn), lambda i, j: (0, j))
    return pl.pallas_call(body, name=name, grid=(M // tm, N // tn), in_specs=[a_spec, b_spec],
                          out_specs=pl.BlockSpec((tm, tn), lambda i, j: (i, j)),
                          out_shape=jax.ShapeDtypeStruct((M, N), out_dtype), compiler_params=_params(2))(a, b)


def _rms(x, eps=EPS):
    return lax.rsqrt(jnp.mean(x * x, axis=-1, keepdims=True) + eps)


def _norm_mod_fwd(name, x, g, scale, shift):
    def fn(x, g, scale, shift):
        return x * _rms(x) * g * (1.0 + scale) + shift
    return _rowwise(name, fn, [x], [g, scale, shift], [(D, BF16)])[0]


def _norm_mod_bwd(name, dh, x, dx_res, g, scale):
    def fn(dh, x, dx_res, g, scale):
        r = _rms(x)
        xh = x * r
        dxh = dh * (g * (1.0 + scale))
        dx = r * (dxh - xh * jnp.mean(dxh * xh, axis=-1, keepdims=True))
        dhx = dh * xh
        return (dx_res + dx, jnp.sum(dh, axis=0, keepdims=True), jnp.sum(dhx * g, axis=0, keepdims=True),
                jnp.sum(dhx * (1.0 + scale), axis=0, keepdims=True))
    return _rowwise(name, fn, [dh, x, dx_res], [g, scale], [(D, F32)], sums=[D, D, D])


def _residual_fwd(name, x, y, gate):
    def fn(x, y, gate):
        return x + gate * y
    return _rowwise(name, fn, [x, y], [gate], [(D, F32)])[0]


def _residual_bwd(name, dx, y, gate):
    def fn(dx, y, gate):
        return dx * gate, jnp.sum(dx * y, axis=0, keepdims=True)
    return _rowwise(name, fn, [dx, y], [gate], [(D, BF16)], sums=[D])


def _loss_head(x, target, g):
    def fn(x, t, g):
        r = _rms(x)
        xh = x * r
        err = xh * g - t
        loss = 0.5 * jnp.sum(jnp.mean(err * err, axis=-1, keepdims=True), axis=0, keepdims=True)
        dy = err * (1.0 / D)
        dxh = dy * g
        dx = r * (dxh - xh * jnp.mean(dxh * xh, axis=-1, keepdims=True))
        return dx, jnp.broadcast_to(loss, (1, LANES)), jnp.sum(dy * xh, axis=0, keepdims=True)
    return _rowwise("loss_head", fn, [x, target], [g], [(D, F32)], sums=[LANES, D])


def _ffn_up(name, h, wg, wu, layer, tm=512):
    T, n = h.shape[0], wg.shape[2]
    tm = min(tm, T)

    def body(h_ref, wg_ref, wu_ref, a_ref, b_ref, s_ref):
        h = h_ref[...]
        a = _dot(h, wg_ref[0], "nn")
        b = _dot(h, wu_ref[0], "nn")
        a_ref[0] = a
        b_ref[0] = b
        s_ref[0] = (a * _sig(a) * b).astype(s_ref.dtype)

    wspec = pl.BlockSpec((1, D, n), lambda ch, i: (ch, layer, 0))
    ospec = pl.BlockSpec((1, tm, n), lambda ch, i: (ch, i, 0))
    return pl.pallas_call(
        body, name=name, grid=(4, T // tm), in_specs=[pl.BlockSpec((tm, D), lambda ch, i: (i, 0)), wspec, wspec],
        out_specs=[ospec, ospec, ospec],
        out_shape=[jax.ShapeDtypeStruct((4, T, n), F32)] * 2 + [jax.ShapeDtypeStruct((4, T, n), BF16)],
        compiler_params=_params(2))(h, wg, wu)


def _ffn_down(name, s, wd, layer, tm=512):
    _, T, n = s.shape
    tm = min(tm, T)

    def body(s_ref, w_ref, y_ref):
        @pl.when(pl.program_id(1) == 0)
        def _():
            y_ref[...] = jnp.zeros_like(y_ref)
        y_ref[...] += _dot(s_ref[0], w_ref[0], "nn")

    return pl.pallas_call(
        body, name=name, grid=(T // tm, 4),
        in_specs=[pl.BlockSpec((1, tm, n), lambda i, ch: (ch, i, 0)), pl.BlockSpec((1, n, D), lambda i, ch: (ch, layer, 0))],
        out_specs=pl.BlockSpec((tm, D), lambda i, ch: (i, 0)), out_shape=jax.ShapeDtypeStruct((T, D), F32),
        compiler_params=_params(2))(s, wd)


def _ffn_down_bwd(name, dy, wd, a, b, layer, tm=512):
    _, T, n = a.shape
    tm = min(tm, T)

    def body(dy_ref, w_ref, a_ref, b_ref, da_ref, db_ref):
        ds = _dot(dy_ref[...], w_ref[0], "nt")
        a, b = a_ref[0], b_ref[0]
        sg = _sig(a)
        da_ref[0] = (ds * b * (sg * (1.0 + a * (1.0 - sg)))).astype(da_ref.dtype)
        db_ref[0] = (ds * (a * sg)).astype(db_ref.dtype)

    bspec = pl.BlockSpec((1, tm, n), lambda ch, i: (ch, i, 0))
    return pl.pallas_call(
        body, name=name, grid=(4, T // tm),
        in_specs=[pl.BlockSpec((tm, D), lambda ch, i: (i, 0)), pl.BlockSpec((1, n, D), lambda ch, i: (ch, layer, 0)), bspec, bspec],
        out_specs=[bspec, bspec], out_shape=[jax.ShapeDtypeStruct((4, T, n), BF16)] * 2,
        compiler_params=_params(2))(dy, wd, a, b)


def _ffn_down_dw(name, s, dy):
    _, T, n = s.shape

    def body(s_ref, dy_ref, o_ref):
        o_ref[0] = _dot(s_ref[0], dy_ref[...], "tn").astype(o_ref.dtype)

    return pl.pallas_call(
        body, name=name, grid=(4,),
        in_specs=[pl.BlockSpec((1, T, n), lambda ch: (ch, 0, 0)), pl.BlockSpec((T, D), lambda ch: (0, 0))],
        out_specs=pl.BlockSpec((1, n, D), lambda ch: (ch, 0, 0)), out_shape=jax.ShapeDtypeStruct((4, n, D), BF16),
        compiler_params=_params(1))(s, dy)


def _ffn_up_dw(name, h, da, db, tm=512):
    _, T, n = da.shape

    def body(h_ref, da_ref, db_ref, dg_ref, du_ref):
        h = h_ref[...]
        dg_ref[0] = _dot(h, da_ref[0], "tn").astype(dg_ref.dtype)
        du_ref[0] = _dot(h, db_ref[0], "tn").astype(du_ref.dtype)

    dspec = pl.BlockSpec((1, T, n), lambda ch, j: (ch, 0, 0))
    ospec = pl.BlockSpec((1, tm, n), lambda ch, j: (ch, j, 0))
    return pl.pallas_call(
        body, name=name, grid=(4, D // tm), in_specs=[pl.BlockSpec((T, tm), lambda ch, j: (0, j)), dspec, dspec],
        out_specs=[ospec, ospec], out_shape=[jax.ShapeDtypeStruct((4, D, n), BF16)] * 2,
        compiler_params=_params(2))(h, da, db)


def _ffn_up_dx(name, da, db, wg, wu, layer, tm=512):
    _, T, n = da.shape
    tm = min(tm, T)

    def body(da_ref, db_ref, wg_ref, wu_ref, o_ref):
        @pl.when(pl.program_id(1) == 0)
        def _():
            o_ref[...] = jnp.zeros_like(o_ref)
        o_ref[...] += _dot(da_ref[0], wg_ref[0], "nt") + _dot(db_ref[0], wu_ref[0], "nt")

    dspec = pl.BlockSpec((1, tm, n), lambda i, ch: (ch, i, 0))
    wspec = pl.BlockSpec((1, D, n), lambda i, ch: (ch, layer, 0))
    return pl.pallas_call(
        body, name=name, grid=(T // tm, 4), in_specs=[dspec, dspec, wspec, wspec],
        out_specs=pl.BlockSpec((tm, D), lambda i, ch: (i, 0)), out_shape=jax.ShapeDtypeStruct((T, D), F32),
        compiler_params=_params(2))(da, db, wg, wu)


def _shift_down(x, k):
    if k == 0:
        return x
    rows = lax.broadcasted_iota(jnp.int32, x.shape, 0)
    return jnp.where(rows >= k, pltpu.roll(x, k, 0), 0.0)


def _shift_up(x, k):
    if k == 0:
        return x
    T = x.shape[0]
    rows = lax.broadcasted_iota(jnp.int32, x.shape, 0)
    return jnp.where(rows < T - k, pltpu.roll(x, T - k, 0), 0.0)


def _conv_silu(x, w):
    c = w[0:1, :] * _shift_down(x, 3) + w[1:2, :] * _shift_down(x, 2) + w[2:3, :] * _shift_down(x, 1) + w[3:4, :] * x
    sg = _sig(c)
    return c, sg, c * sg


def _gdn_conv_fwd(name, proj, cw):
    T = proj.shape[0]

    def body(x_ref, w_ref, o_ref):
        j = pl.program_id(0)
        _, _, y = _conv_silu(x_ref[...], w_ref[...])
        r = lax.rsqrt(jnp.sum(y * y, axis=1, keepdims=True) + EPS)
        mult = jnp.where(j < NH, HD ** -0.5, 1.0)
        o_ref[...] = jnp.where(j < 2 * NH, y * (r * mult), y)

    return pl.pallas_call(body, name=name, grid=(3 * NH,),
                          in_specs=[pl.BlockSpec((T, HD), lambda j: (0, j)), pl.BlockSpec((4, HD), lambda j: (0, j))],
                          out_specs=pl.BlockSpec((T, HD), lambda j: (0, j)),
                          out_shape=jax.ShapeDtypeStruct((T, GDN_QKV), F32), compiler_params=_params(1))(proj, cw)


def _gdn_conv_bwd(name, proj, cw, dz):
    T = proj.shape[0]

    def body(x_ref, w_ref, dz_ref, dx_ref, dw_ref):
        j = pl.program_id(0)
        x, w, dz = x_ref[...], w_ref[...], dz_ref[...]
        c, sg, y = _conv_silu(x, w)
        r = lax.rsqrt(jnp.sum(y * y, axis=1, keepdims=True) + EPS)
        mult = jnp.where(j < NH, HD ** -0.5, 1.0)
        dyn = mult * (r * dz - (r * r * r) * y * jnp.sum(dz * y, axis=1, keepdims=True))
        dy = jnp.where(j < 2 * NH, dyn, dz)
        dc = dy * (sg * (1.0 + c * (1.0 - sg)))
        dx = w[0:1, :] * _shift_up(dc, 3) + w[1:2, :] * _shift_up(dc, 2) + w[2:3, :] * _shift_up(dc, 1) + w[3:4, :] * dc
        dx_ref[...] = dx.astype(dx_ref.dtype)
        for k in range(4):
            dw_ref[pl.ds(k, 1), :] = jnp.sum(dc * _shift_down(x, 3 - k), axis=0, keepdims=True)

    return pl.pallas_call(body, name=name, grid=(3 * NH,),
                          in_specs=[pl.BlockSpec((T, HD), lambda j: (0, j)), pl.BlockSpec((4, HD), lambda j: (0, j)),
                                    pl.BlockSpec((T, HD), lambda j: (0, j))],
                          out_specs=[pl.BlockSpec((T, HD), lambda j: (0, j)), pl.BlockSpec((4, HD), lambda j: (0, j))],
                          out_shape=[jax.ShapeDtypeStruct((T, GDN_QKV), BF16), jax.ShapeDtypeStruct((4, GDN_QKV), F32)],
                          compiler_params=_params(1))(proj, cw, dz)


def _softplus(z):
    return jnp.maximum(z, 0.0) + jnp.log(1.0 + jnp.exp(-jnp.abs(z)))


_AB_CB = GDN_INK // (2 * HD) - 1


def _gdn_gates_fwd(name, proj, alog, dtb):
    def fn(ab, alog, dtb):
        a, b = ab[:, :HD], ab[:, HD:]
        return -jnp.exp(alog) * _softplus(a + dtb), _sig(b)
    return _rowwise(name, fn, [(proj, 2 * HD, _AB_CB)], [alog, dtb], [(HD, F32), (HD, F32)])


def _gdn_gates_bwd(name, proj, dg_h, db_h, alog, dtb):
    def fn(ab, dg_h, db_h, alog, dtb):
        lane = lax.broadcasted_iota(jnp.int32, (1, HD), 1)
        dg = jnp.zeros(dg_h.shape[1:], F32)
        dbeta = jnp.zeros(dg_h.shape[1:], F32)
        for h in range(NH):
            oh = (lane == h).astype(F32)
            dg = dg + dg_h[h] * oh
            dbeta = dbeta + db_h[h] * oh
        a, b = ab[:, :HD], ab[:, HD:]
        z = a + dtb
        ea = jnp.exp(alog)
        beta = _sig(b)
        da = dg * (-ea) * _sig(z)
        db = dbeta * beta * (1.0 - beta)
        return (jnp.concatenate([da, db], axis=1), jnp.sum(dg * (-ea * _softplus(z)), axis=0, keepdims=True),
                jnp.sum(da, axis=0, keepdims=True))
    return _rowwise(name, fn, [(proj, 2 * HD, _AB_CB), dg_h, db_h], [alog, dtb], [(2 * HD, BF16)], sums=[HD, HD])


def _interleave(gens):
    gens = list(gens)
    results = [None] * len(gens)
    active = list(range(len(gens)))
    while active:
        for i in list(active):
            try:
                next(gens[i])
            except StopIteration as stop:
                results[i] = stop.value
                active.remove(i)
    return results


def _chunk_common(q, k, v, gblk, bblk, h):
    C = CHUNK
    lane = lax.broadcasted_iota(jnp.int32, (1, HD), 1)
    oh = (lane == h).astype(F32)
    g_col = jnp.sum(gblk * oh, axis=1, keepdims=True)
    beta = jnp.sum(bblk * oh, axis=1, keepdims=True)
    ri = lax.broadcasted_iota(jnp.int32, (C, C), 0)
    ci = lax.broadcasted_iota(jnp.int32, (C, C), 1)
    incl = ri >= ci
    strict = ri > ci
    eye = (ri == ci).astype(F32)
    gcb = _dot(incl.astype(F32), jnp.broadcast_to(g_col, (C, HD)), "nn", HI)
    yield
    gc = gcb[:, :C]
    gc_row = _dot(jnp.ones((C, C), F32), eye * gc, "nn", HI)
    yield
    decay = jnp.where(incl, jnp.exp(jnp.where(incl, gc - gc_row, 0.0)), 0.0)
    rows = lax.broadcasted_iota(jnp.int32, (C, HD), 0)
    gclb = jnp.sum(jnp.where(rows == C - 1, gcb, 0.0), axis=0, keepdims=True)
    eg = jnp.exp(gcb)
    egl = jnp.exp(gclb - gcb)
    gl = jnp.exp(gclb)
    kb = k * beta
    m1 = _dot(kb, k, "nt", HS)
    qk = _dot(q, k, "nt", HS)
    yield
    L = jnp.where(strict, m1 * decay, 0.0)
    nl = -L
    tinv = eye + nl
    p = nl
    for _ in range(5):
        p = _dot(p, p, "nn", H3)
        yield
        tinv = tinv + _dot(tinv, p, "nn", H3)
    vb = v * beta
    kbg = kb * eg
    yield
    u = _dot(tinv, vb, "nn", HS)
    w = _dot(tinv, kbg, "nn", HS)
    yield
    attn = jnp.where(incl, qk * decay, 0.0)
    return dict(beta=beta, incl=incl, strict=strict, decay=decay, eg=eg, egl=egl, gl=gl, kb=kb, m1=m1, tinv=tinv,
                kbg=kbg, u=u, w=w, qk=qk, attn=attn, q_dec=q * eg, k_dec=k * egl, rows=rows, oh=oh)


def _gdn_chunk_fwd(name, qkv, g, beta):
    T = qkv.shape[0]
    N = T // CHUNK

    hb = _GDN_HB
    w = hb * HD

    def body(q_ref, k_ref, v_ref, g_ref, b_ref, o_ref, st_ref, S):
        hg, n = pl.program_id(0), pl.program_id(1)

        @pl.when(n == 0)
        def _():
            S[...] = jnp.zeros_like(S)

        gblk, bblk = g_ref[...], b_ref[...]

        def one_head(i, q, k, v, s):
            c = yield from _chunk_common(q, k, v, gblk, bblk, hg * hb + i)
            v_new = c["u"] - _dot(c["w"], s, "nn", HS)
            qs = _dot(c["q_dec"], s, "nn", HS)
            yield
            o = qs + _dot(c["attn"], v_new, "nn", HS)
            return o, s * c["gl"] + _dot(c["k_dec"], v_new, "tn", HS)

        sls = [slice(i * HD, (i + 1) * HD) for i in range(hb)]
        states = [S[i] for i in range(hb)]
        res = _interleave(one_head(i, q_ref[:, sls[i]], k_ref[:, sls[i]], v_ref[:, sls[i]], states[i]) for i in range(hb))
        for i, (o, s_new) in enumerate(res):
            st_ref[i, 0] = states[i]
            o_ref[:, sls[i]] = o
            S[i] = s_new

    blk = lambda off: pl.BlockSpec((CHUNK, w), lambda h, n, off=off: (n, off + h))
    gspec = pl.BlockSpec((CHUNK, HD), lambda h, n: (n, 0))
    return pl.pallas_call(
        body, name=name, grid=(NH // hb, N), in_specs=[blk(0), blk(NH // hb), blk(2 * NH // hb), gspec, gspec],
        out_specs=[pl.BlockSpec((CHUNK, w), lambda h, n: (n, h)), pl.BlockSpec((hb, 1, HD, HD), lambda h, n: (h, n, 0, 0))],
        out_shape=[jax.ShapeDtypeStruct((T, NH * HD), F32), jax.ShapeDtypeStruct((NH, N, HD, HD), F32)],
        scratch_shapes=[pltpu.VMEM((hb, HD, HD), F32)], compiler_params=_params(2))(qkv, qkv, qkv, g, beta)


def _gdn_chunk_bwd(name, qkv, g, beta, states, do):
    T = qkv.shape[0]
    N = T // CHUNK
    C = CHUNK

    hb = _GDN_HB
    w = hb * HD

    def body(q_ref, k_ref, v_ref, g_ref, b_ref, st_ref, do_ref, dq_ref, dk_ref, dv_ref, dg_ref, db_ref, dS):
        hg, n = pl.program_id(0), pl.program_id(1)

        @pl.when(n == 0)
        def _():
            dS[...] = jnp.zeros_like(dS)

        gblk, bblk = g_ref[...], b_ref[...]
        sls = [slice(i * HD, (i + 1) * HD) for i in range(hb)]
        res = _interleave(one_head(hg * hb + i, gblk, bblk, q_ref[:, sls[i]], k_ref[:, sls[i]], v_ref[:, sls[i]],
                                   st_ref[i, 0], do_ref[:, sls[i]], dS[i]) for i in range(hb))
        for i, (dq, dk, dv, dg, db, ds_new) in enumerate(res):
            dq_ref[:, sls[i]] = dq
            dk_ref[:, sls[i]] = dk
            dv_ref[:, sls[i]] = dv
            dg_ref[i] = dg
            db_ref[i] = db
            dS[i] = ds_new

    def one_head(h, gblk, bblk, q, k, v, s, do, ds):
        c = yield from _chunk_common(q, k, v, gblk, bblk, h)
        eg, egl, gl, beta, decay, tinv = c["eg"], c["egl"], c["gl"], c["beta"], c["decay"], c["tinv"]
        v_new = c["u"] - _dot(c["w"], s, "nn", HS)
        dq_dec = _dot(do, s, "nt", HS)
        yield
        dv_new = _dot(c["attn"], do, "tn", HS) + _dot(c["k_dec"], ds, "nn", HS)
        dk_dec = _dot(v_new, ds, "nt", HS)
        dgl = jnp.sum(jnp.sum(s * ds, axis=1, keepdims=True), axis=0, keepdims=True)
        yield
        ds_new = ds * gl + _dot(c["q_dec"], do, "tn", HS) - _dot(c["w"], dv_new, "tn", HS)
        dattn = jnp.where(c["incl"], _dot(do, v_new, "nt", HS), 0.0)
        dw = -_dot(dv_new, s, "nt", HS)
        yield
        dvb = _dot(tinv, dv_new, "tn", HS)
        dkbg = _dot(tinv, dw, "tn", HS)
        yield
        dA = -(_dot(dvb, c["u"], "nt", HS) + _dot(dkbg, c["w"], "nt", HS))
        yield
        dL = jnp.where(c["strict"], dA, 0.0)
        dm1 = dL * decay
        dqk = dattn * decay
        xdec = (dL * c["m1"] + dattn * c["qk"]) * decay
        dkb = _dot(dm1, k, "nn", HS) + dkbg * eg
        dk = _dot(dm1, c["kb"], "tn", HS) + _dot(dqk, q, "tn", HS) + dk_dec * egl + dkb * beta
        dq = _dot(dqk, k, "nn", HS) + dq_dec * eg
        yield
        dkd_kd = jnp.sum(dk_dec * c["k_dec"], axis=1, keepdims=True)
        dgc = (jnp.sum(xdec, axis=1, keepdims=True) - _dot(xdec, jnp.ones((C, HD), F32), "tn", HS)
               + jnp.sum(dq_dec * c["q_dec"], axis=1, keepdims=True) - dkd_kd
               + jnp.sum(dkbg * c["kbg"], axis=1, keepdims=True))
        dgcl = jnp.sum(dkd_kd, axis=0, keepdims=True) + dgl * gl
        dgc = dgc + jnp.where(c["rows"] == C - 1, dgcl, 0.0)
        ri = lax.broadcasted_iota(jnp.int32, (C, C), 0)
        ci = lax.broadcasted_iota(jnp.int32, (C, C), 1)
        dg = _dot((ci >= ri).astype(F32), dgc, "nn", HI)
        db = jnp.broadcast_to(jnp.sum(dkb * k, axis=1, keepdims=True) + jnp.sum(dvb * v, axis=1, keepdims=True), (C, HD))
        return dq, dk, dvb * beta, dg, db, ds_new

    blk = lambda off: pl.BlockSpec((C, w), lambda h, n, off=off: (N - 1 - n, off + h))
    gspec = pl.BlockSpec((C, HD), lambda h, n: (N - 1 - n, 0))
    ospec = pl.BlockSpec((C, w), lambda h, n: (N - 1 - n, h))
    hspec = pl.BlockSpec((hb, C, HD), lambda h, n: (h, N - 1 - n, 0))
    return pl.pallas_call(
        body, name=name, grid=(NH // hb, N),
        in_specs=[blk(0), blk(NH // hb), blk(2 * NH // hb), gspec, gspec,
                  pl.BlockSpec((hb, 1, HD, HD), lambda h, n: (h, N - 1 - n, 0, 0)), ospec],
        out_specs=[ospec, ospec, ospec, hspec, hspec],
        out_shape=[jax.ShapeDtypeStruct((T, NH * HD), F32)] * 3 + [jax.ShapeDtypeStruct((NH, T, HD), F32)] * 2,
        scratch_shapes=[pltpu.VMEM((hb, HD, HD), F32)], compiler_params=_params(2))(qkv, qkv, qkv, g, beta, states, do)


_GATE_CB = GDN_QKV // (NH * HD)


def _gdn_gated_norm_fwd(name, o, proj, ng):
    def fn(o, gate, ng):
        outs = []
        for h in range(NH):
            sl = slice(h * HD, (h + 1) * HD)
            oh, gh = o[:, sl], gate[:, sl]
            outs.append(oh * _rms(oh) * ng * (gh * _sig(gh)))
        return jnp.concatenate(outs, axis=1)
    return _rowwise(name, fn, [o, (proj, NH * HD, _GATE_CB)], [ng], [(NH * HD, BF16)])[0]


def _gdn_gated_norm_bwd(name, don, o, proj, ng):
    def fn(don, o, gate, ng):
        dos, dgs = [], []
        dng = jnp.zeros((1, HD), F32)
        for h in range(NH):
            sl = slice(h * HD, (h + 1) * HD)
            oh, gh, dh = o[:, sl], gate[:, sl], don[:, sl]
            r = _rms(oh)
            xh = oh * r
            sg = _sig(gh)
            dn = dh * (gh * sg)
            dgs.append(dh * (xh * ng) * (sg * (1.0 + gh * (1.0 - sg))))
            dng = dng + jnp.sum(dn * xh, axis=0, keepdims=True)
            dxh = dn * ng
            dos.append(r * (dxh - xh * jnp.mean(dxh * xh, axis=-1, keepdims=True)))
        return jnp.concatenate(dos, axis=1), jnp.concatenate(dgs, axis=1), dng
    return _rowwise(name, fn, [don, o, (proj, NH * HD, _GATE_CB)], [ng], [(NH * HD, F32), (NH * HD, BF16)], sums=[HD])


def _rot(x):
    lane = lax.broadcasted_iota(jnp.int32, x.shape, 1)
    return jnp.where(lane < ROPE // 2, -pltpu.roll(x, HD - ROPE // 2, 1), pltpu.roll(x, ROPE // 2, 1))


def _rot_t(x):
    lane = lax.broadcasted_iota(jnp.int32, x.shape, 1)
    return jnp.where(lane < ROPE // 2, pltpu.roll(x, HD - ROPE // 2, 1), -pltpu.roll(x, ROPE // 2, 1))


def _rope_tables(pos_col):
    lane = jnp.arange(HD)
    inv_freq = ROPE_THETA ** (-(2.0 * (lane % (ROPE // 2)).astype(F32)) / ROPE)
    inv_freq = jnp.where(lane < ROPE, inv_freq, 0.0).astype(F32)[None, :]
    valid = (lane < ROPE).astype(F32)[None, :]

    def fn(pos, inv_freq, valid):
        ang = pos.astype(F32) * inv_freq
        return jnp.cos(ang) * valid, jnp.sin(ang) * valid
    return _rowwise("rope_tables", fn, [pos_col], [inv_freq, valid], [(HD, F32), (HD, F32)])


def _mla_pre_fwd(name, proj, cos, sin, qg, kvg):
    def fn(p, cos, sin, qg, kvg):
        cq, ckv, kr = p[:, :Q_RANK], p[:, Q_RANK:Q_RANK + KV_RANK], p[:, Q_RANK + KV_RANK:]
        return cq * _rms(cq) * qg, ckv * _rms(ckv) * kvg, kr * cos + _rot(kr) * sin
    return _rowwise(name, fn, [proj, cos, sin], [qg, kvg], [(Q_RANK, BF16), (KV_RANK, BF16), (HD, BF16)])


def _rms_bwd(dy, x, g):
    r = _rms(x)
    xh = x * r
    dxh = dy * g
    return r * (dxh - xh * jnp.mean(dxh * xh, axis=-1, keepdims=True)), jnp.sum(dy * xh, axis=0, keepdims=True)


def _mla_pre_bwd(name, proj, dcqn, dckvn, dkr, cos, sin, qg, kvg):
    def fn(p, dcqn, dckvn, dkr, cos, sin, qg, kvg):
        cq, ckv = p[:, :Q_RANK], p[:, Q_RANK:Q_RANK + KV_RANK]
        dcq, dqg = _rms_bwd(dcqn, cq, qg)
        dckv, dkvg = _rms_bwd(dckvn, ckv, kvg)
        dkr_pre = dkr * cos + _rot_t(dkr * sin)
        return jnp.concatenate([dcq, dckv, dkr_pre], axis=1), dqg, dkvg
    return _rowwise(name, fn, [proj, dcqn, dckvn, dkr, cos, sin], [qg, kvg], [(MLA_INK, BF16)], sums=[Q_RANK, KV_RANK])


def _mla_q_fwd(name, q, cos, sin):
    def fn(qn, qr, cos, sin):
        outs = []
        for h in range(NH):
            x = qr[:, h * HD:(h + 1) * HD]
            outs.append(x * cos + _rot(x) * sin)
        return qn, jnp.concatenate(outs, axis=1)
    return _rowwise(name, fn, [(q, NH * HD, 0), (q, NH * HD, 1), cos, sin], [], [(NH * HD, BF16), (NH * HD, BF16)])


def _mla_q_bwd(name, dqn, dqr, cos, sin):
    def fn(dqn, dqr, cos, sin):
        outs = [dqn]
        for h in range(NH):
            z = dqr[:, h * HD:(h + 1) * HD]
            outs.append(z * cos + _rot_t(z * sin))
        return jnp.concatenate(outs, axis=1)
    return _rowwise(name, fn, [dqn, dqr, cos, sin], [], [(2 * NH * HD, BF16)])[0]


def _att_probs(qn, qr, kn, kr, row0):
    s = (_dot(qn, kn, "nt") + _dot(qr, kr, "nt")) * ATT_SCALE
    qpos = row0 + lax.broadcasted_iota(jnp.int32, s.shape, 0)
    kpos = lax.broadcasted_iota(jnp.int32, s.shape, 1)
    s = jnp.where(kpos <= qpos, s, -1e30)
    p = jnp.exp(s - jnp.max(s, axis=1, keepdims=True))
    return p / jnp.sum(p, axis=1, keepdims=True)


def _mla_attn_fwd(name, qn, qr, kv, kr, tq=256):
    T = qn.shape[0]
    tq = min(tq, T)

    def body(qn_ref, qr_ref, kn_ref, v_ref, kr_ref, o_ref):
        i = pl.program_id(1)
        for blk in range(T // tq):
            @pl.when(i == blk)
            def _(blk=blk):
                keys = pl.ds(0, (blk + 1) * tq)
                p = _att_probs(qn_ref[...], qr_ref[...], kn_ref[keys, :], kr_ref[keys, :], blk * tq)
                o_ref[...] = _dot(p.astype(BF16), v_ref[keys, :], "nn").astype(o_ref.dtype)

    qspec = pl.BlockSpec((tq, HD), lambda h, i: (i, h))
    return pl.pallas_call(
        body, name=name, grid=(NH, T // tq),
        in_specs=[qspec, qspec, pl.BlockSpec((T, HD), lambda h, i: (0, h)), pl.BlockSpec((T, HD), lambda h, i: (0, NH + h)),
                  pl.BlockSpec((T, HD), lambda h, i: (0, 0))],
        out_specs=qspec, out_shape=jax.ShapeDtypeStruct((T, NH * HD), BF16), compiler_params=_params(2))(qn, qr, kv, kv, kr)


def _mla_attn_bwd(name, qn, qr, kv, kr, do, tq=256):
    T = qn.shape[0]
    tq = min(tq, T)

    def body(qn_ref, qr_ref, kn_ref, v_ref, kr_ref, do_ref, dqn_ref, dqr_ref, dkn_ref, dv_ref, dkr_ref):
        h, i = pl.program_id(0), pl.program_id(1)

        @pl.when(i == 0)
        def _():
            dkn_ref[...] = jnp.zeros_like(dkn_ref)
            dv_ref[...] = jnp.zeros_like(dv_ref)

        @pl.when((i == 0) & (h == 0))
        def _():
            dkr_ref[...] = jnp.zeros_like(dkr_ref)

        for blk in range(T // tq):
            @pl.when(i == blk)
            def _(blk=blk):
                keys = pl.ds(0, (blk + 1) * tq)
                qn, qr, do = qn_ref[...], qr_ref[...], do_ref[...]
                kn, kr, v = kn_ref[keys, :], kr_ref[keys, :], v_ref[keys, :]
                p = _att_probs(qn, qr, kn, kr, blk * tq)
                dp = _dot(do, v, "nt")
                ds = (p * (dp - jnp.sum(p * dp, axis=1, keepdims=True)) * ATT_SCALE).astype(BF16)
                dqn_ref[...] = _dot(ds, kn, "nn")
                dqr_ref[...] = _dot(ds, kr, "nn")
                dkn_ref[keys, :] += _dot(ds, qn, "tn")
                dkr_ref[keys, :] += _dot(ds, qr, "tn")
                dv_ref[keys, :] += _dot(p.astype(BF16), do, "tn")

    qspec = pl.BlockSpec((tq, HD), lambda h, i: (i, h))
    kspec = pl.BlockSpec((T, HD), lambda h, i: (0, h))
    return pl.pallas_call(
        body, name=name, grid=(NH, T // tq),
        in_specs=[qspec, qspec, kspec, pl.BlockSpec((T, HD), lambda h, i: (0, NH + h)),
                  pl.BlockSpec((T, HD), lambda h, i: (0, 0)), qspec],
        out_specs=[qspec, qspec, kspec, kspec, pl.BlockSpec((T, HD), lambda h, i: (0, 0))],
        out_shape=[jax.ShapeDtypeStruct((T, NH * HD), F32)] * 4 + [jax.ShapeDtypeStruct((T, HD), F32)],
        compiler_params=_params(2))(qn, qr, kv, kv, kr, do)


def _mod_rows(mod, layer):
    return [mod[layer:layer + 1, i * D:(i + 1) * D] for i in range(N_MOD)]


def _local_step(x, target, pos_col, mod, W, P):
    cos, sin = _rope_tables(pos_col)
    saved = []
    for l in range(DEPTH):
        j = l // 2
        sh_m, sc_m, ga_m, sh_f, sc_f, ga_f = _mod_rows(mod, l)
        s = dict(x0=x)
        h = _norm_mod_fwd(f"norm_mix{l}", x, P["norm_mix_g"][l:l + 1], sc_m, sh_m)
        s["h"] = h
        if l % 2 == 0:
            proj = _mm(f"gdn_in{j}", h, W["gdn_in"][j], "nn")
            qkv = _gdn_conv_fwd(f"gdn_conv{j}", proj, P["gdn_cw"][j])
            g, beta = _gdn_gates_fwd(f"gdn_gates{j}", proj, P["gdn_alog"][j], P["gdn_dtb"][j])
            o, states = _gdn_chunk_fwd(f"gdn_chunk{j}", qkv, g, beta)
            on = _gdn_gated_norm_fwd(f"gdn_gnorm{j}", o, proj, P["gdn_ng"][j])
            y = _mm(f"gdn_out{j}", on, W["gdn_out"][j], "nn")
            s.update(proj=proj, qkv=qkv, g=g, beta=beta, o=o, states=states, on=on)
        else:
            proj = _mm(f"mla_in{j}", h, W["mla_in"][j], "nn")
            cqn, ckvn, kr = _mla_pre_fwd(f"mla_pre{j}", proj, cos, sin, P["mla_qg"][j], P["mla_kvg"][j])
            q = _mm(f"mla_uq{j}", cqn, W["mla_uq"][j], "nn")
            kv = _mm(f"mla_ukv{j}", ckvn, W["mla_ukv"][j], "nn", out_dtype=BF16)
            qn, qr = _mla_q_fwd(f"mla_q{j}", q, cos, sin)
            o = _mla_attn_fwd(f"mla_attn{j}", qn, qr, kv, kr)
            y = _mm(f"mla_out{j}", o, W["mla_out"][j], "nn")
            s.update(proj=proj, cqn=cqn, ckvn=ckvn, kr=kr, kv=kv, qn=qn, qr=qr, o=o)
        s["y"] = y
        x = _residual_fwd(f"res_mix{l}", x, y, ga_m)
        s["x1"] = x
        h2 = _norm_mod_fwd(f"norm_ffn{l}", x, P["norm_ffn_g"][l:l + 1], sc_f, sh_f)
        fa, fb, sw = _ffn_up(f"ffn_up{l}", h2, W["ffn_g"], W["ffn_u"], l)
        yf = _ffn_down(f"ffn_down{l}", sw, W["ffn_d"], l)
        x = _residual_fwd(f"res_ffn{l}", x, yf, ga_f)
        s.update(h2=h2, fa=fa, fb=fb, sw=sw, yf=yf)
        saved.append(s)

    dx, loss, d_final = _loss_head(x, target, P["final_g"])
    gW = {k: [None] * (DEPTH if k.startswith("ffn") else 2) for k in W}
    gP = dict(loss=loss, final_g=d_final, norm_mix_g=[None] * DEPTH, norm_ffn_g=[None] * DEPTH,
              gdn_cw=[None] * 2, gdn_alog=[None] * 2, gdn_dtb=[None] * 2, gdn_ng=[None] * 2,
              mla_qg=[None] * 2, mla_kvg=[None] * 2)
    dmod = [None] * DEPTH
    for l in reversed(range(DEPTH)):
        j = l // 2
        s = saved[l]
        sh_m, sc_m, ga_m, sh_f, sc_f, ga_f = _mod_rows(mod, l)
        dyf, d_ga_f = _residual_bwd(f"res_ffn_b{l}", dx, s["yf"], ga_f)
        da, db = _ffn_down_bwd(f"ffn_down_dx{l}", dyf, W["ffn_d"], s["fa"], s["fb"], l)
        gW["ffn_d"][l] = _ffn_down_dw(f"ffn_down_dw{l}", s["sw"], dyf)
        gW["ffn_g"][l], gW["ffn_u"][l] = _ffn_up_dw(f"ffn_up_dw{l}", s["h2"], da, db)
        dh2 = _ffn_up_dx(f"ffn_up_dx{l}", da, db, W["ffn_g"], W["ffn_u"], l)
        dx, d_sh_f, d_sc_f, gP["norm_ffn_g"][l] = _norm_mod_bwd(f"norm_ffn_b{l}", dh2, s["x1"], dx,
                                                                 P["norm_ffn_g"][l:l + 1], sc_f)
        dy, d_ga_m = _residual_bwd(f"res_mix_b{l}", dx, s["y"], ga_m)
        if l % 2 == 0:
            don = _mm(f"gdn_out_dx{j}", dy, W["gdn_out"][j], "nt")
            gW["gdn_out"][j] = _mm(f"gdn_out_dw{j}", s["on"], dy, "tn", out_dtype=BF16)
            do, dgate, gP["gdn_ng"][j] = _gdn_gated_norm_bwd(f"gdn_gnorm_b{j}", don, s["o"], s["proj"], P["gdn_ng"][j])
            dq, dk, dv, dg_h, db_h = _gdn_chunk_bwd(f"gdn_chunk_b{j}", s["qkv"], s["g"], s["beta"], s["states"], do)
            dab_, gP["gdn_alog"][j], gP["gdn_dtb"][j] = _gdn_gates_bwd(f"gdn_gates_b{j}", s["proj"], dg_h, db_h,
                                                                        P["gdn_alog"][j], P["gdn_dtb"][j])
            dpre, gP["gdn_cw"][j] = _gdn_conv_bwd(f"gdn_conv_b{j}", s["proj"], P["gdn_cw"][j],
                                                  jnp.concatenate([dq, dk, dv], axis=1))
            dproj = jnp.concatenate([dpre, dgate, dab_], axis=1)
            gW["gdn_in"][j] = _mm(f"gdn_in_dw{j}", s["h"], dproj, "tn", out_dtype=BF16)
            dh = _mm(f"gdn_in_dx{j}", dproj, W["gdn_in"][j], "nt")
        else:
            do = _mm(f"mla_out_dx{j}", dy, W["mla_out"][j], "nt", out_dtype=BF16)
            gW["mla_out"][j] = _mm(f"mla_out_dw{j}", s["o"], dy, "tn", out_dtype=BF16)
            dqn, dqr, dkn, dv, dkr = _mla_attn_bwd(f"mla_attn_b{j}", s["qn"], s["qr"], s["kv"], s["kr"], do)
            dq = _mla_q_bwd(f"mla_q_b{j}", dqn, dqr, cos, sin)
            dkv = jnp.concatenate([dkn, dv], axis=1)
            gW["mla_uq"][j] = _mm(f"mla_uq_dw{j}", s["cqn"], dq, "tn", out_dtype=BF16)
            dcqn = _mm(f"mla_uq_dx{j}", dq, W["mla_uq"][j], "nt")
            gW["mla_ukv"][j] = _mm(f"mla_ukv_dw{j}", s["ckvn"], dkv, "tn", out_dtype=BF16)
            dckvn = _mm(f"mla_ukv_dx{j}", dkv, W["mla_ukv"][j], "nt")
            dproj, gP["mla_qg"][j], gP["mla_kvg"][j] = _mla_pre_bwd(f"mla_pre_b{j}", s["proj"], dcqn, dckvn, dkr, cos, sin,
                                                                     P["mla_qg"][j], P["mla_kvg"][j])
            gW["mla_in"][j] = _mm(f"mla_in_dw{j}", s["h"], dproj, "tn", out_dtype=BF16)
            dh = _mm(f"mla_in_dx{j}", dproj, W["mla_in"][j], "nt")
        dx, d_sh_m, d_sc_m, gP["norm_mix_g"][l] = _norm_mod_bwd(f"norm_mix_b{l}", dh, s["x0"], dx,
                                                                 P["norm_mix_g"][l:l + 1], sc_m)
        dmod[l] = jnp.concatenate([d_sh_m, d_sc_m, d_ga_m, d_sh_f, d_sc_f, d_ga_f], axis=1)
    return dx, jnp.concatenate(dmod, axis=0), gW, gP


def _pad_cols(a, width):
    return jnp.pad(a, ((0, 0), (0, width - a.shape[1])))


def _gdn_in_to_kernel(w):
    m = GDN_QKV + NH * HD
    return jnp.concatenate([w[:, :m], _pad_cols(w[:, m:m + NH], HD), _pad_cols(w[:, m + NH:], HD)], axis=1)


def _gdn_in_from_kernel(g):
    m = GDN_QKV + NH * HD
    return jnp.concatenate([g[:, :m], g[:, m:m + NH], g[:, m + HD:m + HD + NH]], axis=1)


def _mla_uq_to_kernel(w):
    w3 = w.reshape(Q_RANK, NH, HD + ROPE)
    rope = jnp.pad(w3[:, :, HD:], ((0, 0), (0, 0), (0, HD - ROPE)))
    return jnp.concatenate([w3[:, :, :HD].reshape(Q_RANK, NH * HD), rope.reshape(Q_RANK, NH * HD)], axis=1)


def _mla_uq_from_kernel(g):
    gn = g[:, :NH * HD].reshape(Q_RANK, NH, HD)
    gr = g[:, NH * HD:].reshape(Q_RANK, NH, HD)[:, :, :ROPE]
    return jnp.concatenate([gn, gr], axis=2).reshape(Q_RANK, NH * (HD + ROPE))


def _mla_ukv_to_kernel(w):
    w3 = w.reshape(KV_RANK, NH, 2 * HD)
    return jnp.concatenate([w3[:, :, :HD].reshape(KV_RANK, NH * HD), w3[:, :, HD:].reshape(KV_RANK, NH * HD)], axis=1)


def _mla_ukv_from_kernel(g):
    gk = g[:, :NH * HD].reshape(KV_RANK, NH, HD)
    gv = g[:, NH * HD:].reshape(KV_RANK, NH, HD)
    return jnp.concatenate([gk, gv], axis=2).reshape(KV_RANK, NH * 2 * HD)


def _layers(a, n_layers):
    r = a.shape[1] // n_layers
    return [a[:, j * r:(j + 1) * r] for j in range(n_layers)]


def _cols(t):
    return jnp.moveaxis(t, 0, 1).reshape(t.shape[1], -1)


def _uncols(g):
    return jnp.moveaxis(g.reshape(g.shape[0], 4, -1), 1, 0)


def _rows(t):
    return t.reshape(-1, t.shape[2])


def _unrows(g):
    return g.reshape(4, -1, g.shape[1])


def _weights_to_kernel(got):
    return dict(
        gdn_in=[_gdn_in_to_kernel(_cols(t)) for t in _layers(got["gdn_w_in"], 2)],
        gdn_out=[_rows(t) for t in _layers(got["gdn_w_out"], 2)],
        mla_in=[_pad_cols(_rows(t), MLA_INK) for t in _layers(got["mla_w_in"], 2)],
        mla_uq=[_mla_uq_to_kernel(_cols(t)) for t in _layers(got["mla_w_uq"], 2)],
        mla_ukv=[_mla_ukv_to_kernel(_cols(t)) for t in _layers(got["mla_w_ukv"], 2)],
        mla_out=[_rows(t) for t in _layers(got["mla_w_out"], 2)],
        ffn_g=got["ffn_w_gate"], ffn_u=got["ffn_w_up"], ffn_d=got["ffn_w_down"],
    )


def _grads_to_chips(gW):
    cat = lambda parts: jnp.concatenate(parts, axis=1)
    return dict(
        gdn_w_in=cat([_uncols(_gdn_in_from_kernel(g)) for g in gW["gdn_in"]]),
        gdn_w_out=cat([_unrows(g) for g in gW["gdn_out"]]),
        mla_w_in=cat([_unrows(g[:, :Q_RANK + KV_RANK + ROPE]) for g in gW["mla_in"]]),
        mla_w_uq=cat([_uncols(_mla_uq_from_kernel(g)) for g in gW["mla_uq"]]),
        mla_w_ukv=cat([_uncols(_mla_ukv_from_kernel(g)) for g in gW["mla_ukv"]]),
        mla_w_out=cat([_unrows(g) for g in gW["mla_out"]]),
        ffn_w_gate=cat(gW["ffn_g"]), ffn_w_up=cat(gW["ffn_u"]), ffn_w_down=cat(gW["ffn_d"]),
    )


def _small_to_kernel(norm_mix_g, norm_ffn_g, final_norm_g, gdn_conv_w, gdn_a_log, gdn_dt_bias, gdn_norm_g, q_norm_g, kv_norm_g):
    return dict(
        norm_mix_g=norm_mix_g, norm_ffn_g=norm_ffn_g, final_g=final_norm_g.reshape(1, D),
        gdn_cw=[jnp.transpose(gdn_conv_w[j]) for j in range(2)],
        gdn_alog=[_pad_cols(gdn_a_log[j:j + 1], HD) for j in range(2)],
        gdn_dtb=[_pad_cols(gdn_dt_bias[j:j + 1], HD) for j in range(2)],
        gdn_ng=[gdn_norm_g[j:j + 1] for j in range(2)],
        mla_qg=[q_norm_g[j:j + 1] for j in range(2)],
        mla_kvg=[kv_norm_g[j:j + 1] for j in range(2)],
    )


_CHIP_FLIPS = ((1, 0), (0, 1), (1, 1))
_ANY = pl.BlockSpec(memory_space=pl.ANY)


def _me():
    return lax.axis_index("x"), lax.axis_index("y"), lax.axis_index("c")


def _chip_peer(dx, dy):
    x, y, c = _me()
    return ((1 - x) if dx else x, (1 - y) if dy else y, c)


def _rcopy(src, dst, send_sem, recv_sem, to):
    return pltpu.make_async_remote_copy(src_ref=src, dst_ref=dst, send_sem=send_sem, recv_sem=recv_sem,
                                        device_id=to, device_id_type=MESH)


def _allgather4(name, a, halves=False):
    R, C = a.shape
    rh = R // 2 if halves else R

    def body(a_ref, out_ref, send_sems, recv_sems, local_sem):
        x, y, c = _me()
        me = 2 * x + y
        src = a_ref.at[pl.ds(c * rh, rh)] if halves else a_ref
        local = pltpu.make_async_copy(src, out_ref.at[me], local_sem)
        local.start()
        sends = []
        for k, (dx, dy) in enumerate(_CHIP_FLIPS):
            cp = _rcopy(src, out_ref.at[me], send_sems.at[k], recv_sems.at[k], _chip_peer(dx, dy))
            cp.start()
            sends.append(cp)
        for k, (dx, dy) in enumerate(_CHIP_FLIPS):
            px, py, _ = _chip_peer(dx, dy)
            _rcopy(src, out_ref.at[2 * px + py], send_sems.at[k], recv_sems.at[k], _chip_peer(dx, dy)).wait_recv()
        for cp in sends:
            cp.wait_send()
        local.wait()

    return pl.pallas_call(
        body, name=name, in_specs=[_ANY], out_specs=_ANY, out_shape=jax.ShapeDtypeStruct((4, rh, C), a.dtype),
        scratch_shapes=[pltpu.SemaphoreType.DMA((3,)), pltpu.SemaphoreType.DMA((3,)), pltpu.SemaphoreType.DMA(())])(a)


_NCH = 4


def _dma_sems(*counts):
    return [pltpu.SemaphoreType.DMA((n,)) for n in counts]


def _slot_tile(rows):
    tr = 256
    while rows % tr:
        tr //= 2
    return tr


def _cast_into_slot(name, a, chip):
    R, C = a.shape
    tr = _slot_tile(R)

    def body(c_ref, a_ref, o_ref):
        o_ref[0] = a_ref[...].astype(o_ref.dtype)

    grid_spec = pltpu.PrefetchScalarGridSpec(
        num_scalar_prefetch=1, grid=(R // tr,), in_specs=[pl.BlockSpec((tr, C), lambda i, c_ref: (i, 0))],
        out_specs=pl.BlockSpec((1, tr, C), lambda i, c_ref: (c_ref[0], i, 0)))
    return pl.pallas_call(body, name=name, grid_spec=grid_spec, out_shape=jax.ShapeDtypeStruct((4, R, C), BF16),
                          compiler_params=_params(1))(chip, a)


def _own_slot(name, p, chip):
    _, h, C = p.shape
    tr = _slot_tile(h)

    def body(c_ref, p_ref, o_ref):
        o_ref[...] = p_ref[...]

    spec = pl.BlockSpec((1, tr, C), lambda i, c_ref: (c_ref[0], i, 0))
    grid_spec = pltpu.PrefetchScalarGridSpec(num_scalar_prefetch=1, grid=(h // tr,), in_specs=[spec], out_specs=spec)
    return pl.pallas_call(body, name=name, grid_spec=grid_spec, out_shape=jax.ShapeDtypeStruct(p.shape, p.dtype),
                          compiler_params=_params(1))(chip, p)


def _gather_weights(name, bufs):
    n = len(bufs)

    def body(*refs):
        out = refs[n:2 * n]
        ici_s, ici_r, d2d_s, d2d_r = refs[2 * n:]
        x, y, c = _me()
        me = 2 * x + y
        sib = (x, y, 1 - c)
        peers = [_chip_peer(dx, dy) for dx, dy in _CHIP_FLIPS]
        for t in range(n):
            h = out[t].shape[1] // 2
            ch = h // _NCH
            for k, peer in enumerate(peers):
                for i in range(_NCH):
                    blk = out[t].at[me, pl.ds(c * h + i * ch, ch)]
                    _rcopy(blk, blk, ici_s.at[3 * t + k], ici_r.at[3 * t + k], peer).start()
        for t in range(n):
            h = out[t].shape[1] // 2
            ch = h // _NCH
            for k, peer in enumerate(peers):
                pchip = 2 * peer[0] + peer[1]
                got = out[t].at[pchip, pl.ds(c * h, h)]
                _rcopy(got, got, ici_s.at[3 * t + k], ici_r.at[3 * t + k], peer).wait_recv()
                for i in range(_NCH):
                    blk = out[t].at[pchip, pl.ds(c * h + i * ch, ch)]
                    _rcopy(blk, blk, d2d_s.at[3 * t + k], d2d_r.at[3 * t + k], sib).start()
        for t in range(n):
            h = out[t].shape[1] // 2
            for k, peer in enumerate(peers):
                pchip = 2 * peer[0] + peer[1]
                other = out[t].at[pchip, pl.ds((1 - c) * h, h)]
                _rcopy(other, other, d2d_s.at[3 * t + k], d2d_r.at[3 * t + k], sib).wait_recv()
                _rcopy(other, other, ici_s.at[3 * t + k], ici_r.at[3 * t + k], peer).wait_send()
                _rcopy(other, other, d2d_s.at[3 * t + k], d2d_r.at[3 * t + k], sib).wait_send()

    return pl.pallas_call(
        body, name=name, in_specs=[_ANY] * n, out_specs=[_ANY] * n,
        out_shape=[jax.ShapeDtypeStruct(s.shape, s.dtype) for s in bufs],
        input_output_aliases={t: t for t in range(n)},
        scratch_shapes=_dma_sems(3 * n, 3 * n, 3 * n, 3 * n))(*bufs)


def _rs_split(name, grads):
    n = len(grads)

    def body(*refs):
        g, out = refs[:n], refs[n:2 * n]
        send, recv = refs[2 * n:]
        x, y, c = _me()
        sib = (x, y, 1 - c)
        for t in range(n):
            h = g[t].shape[1] // 2
            for d in range(4):
                _rcopy(g[t].at[d, pl.ds((1 - c) * h, h)], out[t].at[d], send.at[t], recv.at[t], sib).start()
        for t in range(n):
            _rcopy(out[t], out[t], send.at[t], recv.at[t], sib).wait()

    return pl.pallas_call(
        body, name=name, in_specs=[_ANY] * n, out_specs=[_ANY] * n,
        out_shape=[jax.ShapeDtypeStruct((4, s.shape[1] // 2, s.shape[2]), s.dtype) for s in grads],
        scratch_shapes=_dma_sems(n, n))(*grads)


def _pair_add(name, g, theirs, core):
    _, R, C = g.shape
    h = R // 2
    tr = 256
    while h % tr:
        tr //= 2
    nb = h // tr

    def body(c_ref, g_ref, t_ref, o_ref):
        o_ref[...] = (g_ref[...].astype(F32) + t_ref[...].astype(F32)).astype(o_ref.dtype)

    spec = pl.BlockSpec((1, tr, C), lambda d, i, c_ref: (d, i, 0))
    grid_spec = pltpu.PrefetchScalarGridSpec(
        num_scalar_prefetch=1, grid=(4, nb),
        in_specs=[pl.BlockSpec((1, tr, C), lambda d, i, c_ref: (d, c_ref[0] * nb + i, 0)), spec], out_specs=spec)
    return pl.pallas_call(body, name=name, grid_spec=grid_spec, out_shape=jax.ShapeDtypeStruct((4, h, C), BF16),
                          compiler_params=_params(2))(core, g, theirs)


def _rs_alltoall(name, parts, bufs):
    n = len(parts)

    def body(*refs):
        p, out = refs[:n], refs[2 * n:3 * n]
        send, recv = refs[3 * n:]
        x, y, c = _me()
        me = 2 * x + y
        peers = [_chip_peer(dx, dy) for dx, dy in _CHIP_FLIPS]
        for t in range(n):
            ch = p[t].shape[1] // _NCH
            for k, peer in enumerate(peers):
                pchip = 2 * peer[0] + peer[1]
                for i in range(_NCH):
                    rows = pl.ds(i * ch, ch)
                    _rcopy(p[t].at[pchip, rows], out[t].at[me, rows], send.at[3 * t + k], recv.at[3 * t + k], peer).start()
        for t in range(n):
            for k, peer in enumerate(peers):
                pchip = 2 * peer[0] + peer[1]
                _rcopy(out[t].at[pchip], out[t].at[pchip], send.at[3 * t + k], recv.at[3 * t + k], peer).wait()

    return pl.pallas_call(
        body, name=name, in_specs=[_ANY] * (2 * n), out_specs=[_ANY] * n,
        out_shape=[jax.ShapeDtypeStruct(s.shape, s.dtype) for s in bufs],
        input_output_aliases={n + t: t for t in range(n)},
        scratch_shapes=_dma_sems(3 * n, 3 * n))(*parts, *bufs)


def _rs_swap(name, halves):
    n = len(halves)

    def body(*refs):
        a, out = refs[:n], refs[n:2 * n]
        send, recv = refs[2 * n:]
        x, y, c = _me()
        sib = (x, y, 1 - c)
        for t in range(n):
            ch = a[t].shape[0] // _NCH
            for i in range(_NCH):
                rows = pl.ds(i * ch, ch)
                _rcopy(a[t].at[rows], out[t].at[rows], send.at[t], recv.at[t], sib).start()
        for t in range(n):
            _rcopy(a[t], out[t], send.at[t], recv.at[t], sib).wait()

    return pl.pallas_call(
        body, name=name, in_specs=[_ANY] * n, out_specs=[_ANY] * n,
        out_shape=[jax.ShapeDtypeStruct(s.shape, s.dtype) for s in halves],
        scratch_shapes=_dma_sems(n, n))(*halves)


def _sibling_merge(name, a):
    P_, rh, C = a.shape

    def body(a_ref, out_ref, send_sem, recv_sem, local_sem):
        x, y, c = _me()
        local = pltpu.make_async_copy(a_ref, out_ref.at[:, pl.ds(c * rh, rh)], local_sem)
        local.start()
        cp = _rcopy(a_ref, out_ref.at[:, pl.ds(c * rh, rh)], send_sem, recv_sem, (x, y, 1 - c))
        cp.start()
        cp.wait_send()
        _rcopy(a_ref, out_ref.at[:, pl.ds((1 - c) * rh, rh)], send_sem, recv_sem, (x, y, 1 - c)).wait_recv()
        local.wait()

    return pl.pallas_call(
        body, name=name, in_specs=[_ANY], out_specs=_ANY, out_shape=jax.ShapeDtypeStruct((P_, 2 * rh, C), a.dtype),
        scratch_shapes=[pltpu.SemaphoreType.DMA(()), pltpu.SemaphoreType.DMA(()), pltpu.SemaphoreType.DMA(())])(a)


def _allgather8(name, a):
    g4 = _allgather4(name + "_chips", a)
    both = _sibling_merge(name + "_cores", g4.reshape(1, 4 * a.shape[0], a.shape[1]))
    return jnp.transpose(both.reshape(2, 4, *a.shape), (1, 0, 2, 3)).reshape(8, *a.shape)


def _sum_slots(name, a, out_dtype):
    def fn(a):
        acc = a[0].astype(F32)
        for k in range(1, a.shape[0]):
            acc = acc + a[k].astype(F32)
        return acc
    return _rowwise(name, fn, [a], [], [(a.shape[2], out_dtype)])[0]


def _adamw_math(w, g, m, v):
    m = ADAM_B1 * m + (1.0 - ADAM_B1) * g
    v = ADAM_B2 * v + (1.0 - ADAM_B2) * (g * g)
    m_hat = m / (1.0 - ADAM_B1 ** ADAM_STEP)
    v_hat = v / (1.0 - ADAM_B2 ** ADAM_STEP)
    return -ADAM_LR * (m_hat / (jnp.sqrt(v_hat) + ADAM_EPS) + ADAM_WD * w), m, v


def _adamw_halves(name, w, m, v, mine, theirs):
    shape = w.shape
    w2, m2, v2 = [t.reshape(-1, shape[-1]) for t in (w, m, v)]
    R, C = w2.shape
    h = R // 2
    tr = _slot_tile(h)
    nb = h // tr

    def body(w_ref, m_ref, v_ref, a_ref, b_ref, g_ref, d_ref, nm_ref, nv_ref):
        g = jnp.where(pl.program_id(0) == lax.axis_index("c"), a_ref[...], b_ref[...])
        g_ref[...] = g
        d_ref[...], nm_ref[...], nv_ref[...] = _adamw_math(w_ref[...], g, m_ref[...], v_ref[...])

    full = pl.BlockSpec((tr, C), lambda s, i: (s * nb + i, 0))
    half = pl.BlockSpec((tr, C), lambda s, i: (i, 0))
    outs = pl.pallas_call(body, name=name, grid=(2, nb), in_specs=[full, full, full, half, half], out_specs=[full] * 4,
                          out_shape=[jax.ShapeDtypeStruct((R, C), F32)] * 4, compiler_params=_params(2))(w2, m2, v2, mine, theirs)
    return [o.reshape(shape) for o in outs]


def _adamw(name, w, g, m, v):
    shape = w.shape
    two_d = (-1, shape[-1]) if w.ndim > 1 else (1, -1)
    w2, g2, m2, v2 = [t.reshape(two_d) for t in (w, g, m, v)]
    rows = w2.shape[0]
    tr = rows
    for cand in (256, 128, 64, 32, 16, 8):
        if rows % cand == 0:
            tr = cand
            break

    c = w2.shape[1]
    outs = _rowwise(name, _adamw_math, [w2, g2, m2, v2], [], [(c, F32)] * 3, tr=tr)
    return [o.reshape(shape) for o in outs]


_WEIGHT_ORDER = ("ada_w", "ada_b", "norm_mix_g", "norm_ffn_g", "gdn_w_in", "gdn_conv_w", "gdn_a_log", "gdn_dt_bias",
                 "gdn_norm_g", "gdn_w_out", "mla_w_in", "mla_q_norm_g", "mla_kv_norm_g", "mla_w_uq", "mla_w_ukv",
                 "mla_w_out", "ffn_w_gate", "ffn_w_up", "ffn_w_down", "final_norm_g")
_BIG = (("gdn_w_in", 2), ("gdn_w_out", 1), ("mla_w_in", 1), ("mla_w_uq", 2), ("mla_w_ukv", 2), ("mla_w_out", 1),
        ("ffn_w_gate", 2), ("ffn_w_up", 2), ("ffn_w_down", 1))
_SMALL_SHARDED = (("gdn_conv_w", 1), ("mla_q_norm_g", 1), ("mla_kv_norm_g", 1))


def _size(shape):
    n = 1
    for s in shape:
        n *= s
    return n


def _pack_rows_each(tensors):
    parts, offs, off = [], [], 0
    for t in tensors:
        flat = t.reshape(-1).astype(F32)
        rows = -(-flat.shape[0] // PACK_W)
        parts.append(jnp.pad(flat, (0, rows * PACK_W - flat.shape[0])).reshape(rows, PACK_W))
        offs.append(off)
        off += rows
    pad = -(-off // 16) * 16 - off
    if pad:
        parts.append(jnp.zeros((pad, PACK_W), F32))
    return jnp.concatenate(parts, axis=0), offs


def _unpack_rows_each(pack, shapes):
    lead = pack.shape[:-2]
    out, off = [], 0
    for shp in shapes:
        n = _size(shp)
        rows = -(-n // PACK_W)
        out.append(pack[..., off:off + rows, :].reshape(*lead, -1)[..., :n].reshape(*lead, *shp))
        off += rows
    return out


def _merge_chips(stacked, axis):
    moved = jnp.moveaxis(stacked, 0, axis)
    shp = list(moved.shape)
    return moved.reshape(shp[:axis] + [shp[axis] * shp[axis + 1]] + shp[axis + 2:])


def _my_shard(full, axis, chip):
    n = full.shape[axis] // 4
    return lax.dynamic_slice_in_dim(full, chip * n, n, axis)


def kernel(x, c, positions, ada_w, ada_b, norm_mix_g, norm_ffn_g, gdn_w_in, gdn_conv_w, gdn_a_log, gdn_dt_bias, gdn_norm_g, gdn_w_out, mla_w_in, mla_q_norm_g, mla_kv_norm_g, mla_w_uq, mla_w_ukv, mla_w_out, ffn_w_gate, ffn_w_up, ffn_w_down, final_norm_g, loss_target, m_ada_w, m_ada_b, m_norm_mix_g, m_norm_ffn_g, m_gdn_w_in, m_gdn_conv_w, m_gdn_a_log, m_gdn_dt_bias, m_gdn_norm_g, m_gdn_w_out, m_mla_w_in, m_mla_q_norm_g, m_mla_kv_norm_g, m_mla_w_uq, m_mla_w_ukv, m_mla_w_out, m_ffn_w_gate, m_ffn_w_up, m_ffn_w_down, m_final_norm_g, v_ada_w, v_ada_b, v_norm_mix_g, v_norm_ffn_g, v_gdn_w_in, v_gdn_conv_w, v_gdn_a_log, v_gdn_dt_bias, v_gdn_norm_g, v_gdn_w_out, v_mla_w_in, v_mla_q_norm_g, v_mla_kv_norm_g, v_mla_w_uq, v_mla_w_ukv, v_mla_w_out, v_ffn_w_gate, v_ffn_w_up, v_ffn_w_down, v_final_norm_g):
    w = dict(ada_w=ada_w, ada_b=ada_b, norm_mix_g=norm_mix_g, norm_ffn_g=norm_ffn_g, gdn_w_in=gdn_w_in, gdn_conv_w=gdn_conv_w,
             gdn_a_log=gdn_a_log, gdn_dt_bias=gdn_dt_bias, gdn_norm_g=gdn_norm_g, gdn_w_out=gdn_w_out, mla_w_in=mla_w_in,
             mla_q_norm_g=mla_q_norm_g, mla_kv_norm_g=mla_kv_norm_g, mla_w_uq=mla_w_uq, mla_w_ukv=mla_w_ukv,
             mla_w_out=mla_w_out, ffn_w_gate=ffn_w_gate, ffn_w_up=ffn_w_up, ffn_w_down=ffn_w_down, final_norm_g=final_norm_g)
    m = dict(ada_w=m_ada_w, ada_b=m_ada_b, norm_mix_g=m_norm_mix_g, norm_ffn_g=m_norm_ffn_g, gdn_w_in=m_gdn_w_in,
             gdn_conv_w=m_gdn_conv_w, gdn_a_log=m_gdn_a_log, gdn_dt_bias=m_gdn_dt_bias, gdn_norm_g=m_gdn_norm_g,
             gdn_w_out=m_gdn_w_out, mla_w_in=m_mla_w_in, mla_q_norm_g=m_mla_q_norm_g, mla_kv_norm_g=m_mla_kv_norm_g,
             mla_w_uq=m_mla_w_uq, mla_w_ukv=m_mla_w_ukv, mla_w_out=m_mla_w_out, ffn_w_gate=m_ffn_w_gate,
             ffn_w_up=m_ffn_w_up, ffn_w_down=m_ffn_w_down, final_norm_g=m_final_norm_g)
    v = dict(ada_w=v_ada_w, ada_b=v_ada_b, norm_mix_g=v_norm_mix_g, norm_ffn_g=v_norm_ffn_g, gdn_w_in=v_gdn_w_in,
             gdn_conv_w=v_gdn_conv_w, gdn_a_log=v_gdn_a_log, gdn_dt_bias=v_gdn_dt_bias, gdn_norm_g=v_gdn_norm_g,
             gdn_w_out=v_gdn_w_out, mla_w_in=v_mla_w_in, mla_q_norm_g=v_mla_q_norm_g, mla_kv_norm_g=v_mla_kv_norm_g,
             mla_w_uq=v_mla_w_uq, mla_w_ukv=v_mla_w_ukv, mla_w_out=v_mla_w_out, ffn_w_gate=v_ffn_w_gate,
             ffn_w_up=v_ffn_w_up, ffn_w_down=v_ffn_w_down, final_norm_g=v_final_norm_g)
    T = x.shape[1]
    ix, iy, ic = _me()
    chip = 2 * ix + iy
    seq = 2 * chip + ic
    n_dev = 8

    small_shapes = [w[n].shape for n, _ in _SMALL_SHARDED] + [c.shape]
    pack0, _ = _pack_rows_each([w[n] for n, _ in _SMALL_SHARDED] + [c])
    got0 = _unpack_rows_each(_allgather8("gather_small", pack0), small_shapes)
    small_full = {n: _merge_chips(g[0::2], ax) for (n, ax), g in zip(_SMALL_SHARDED, got0)}
    c_all = got0[-1].reshape(n_dev, D)

    big = [n for n, _ in _BIG]
    chip_arr = chip.astype(jnp.int32).reshape(1)
    bufs = [_cast_into_slot("to_bf16_" + n, w[n].reshape(-1, w[n].shape[-1]), chip_arr) for n in big]
    W = _weights_to_kernel(dict(zip(big, _gather_weights("gather_weights", bufs))))
    P = _small_to_kernel(norm_mix_g, norm_ffn_g, final_norm_g, small_full["gdn_conv_w"], gdn_a_log, gdn_dt_bias,
                         gdn_norm_g, small_full["mla_q_norm_g"], small_full["mla_kv_norm_g"])

    c16 = jnp.pad(c_all, ((0, 16 - n_dev), (0, 0)))
    ca = _rowwise("cond_silu", lambda t: t * _sig(t), [c16], [], [(D, BF16)])[0]
    n_ada = ada_w.shape[2]
    mods = jnp.concatenate([_mm(f"ada_fwd{l}", ca, ada_w[l], "nn") for l in range(DEPTH)], axis=0)
    mods_all = _allgather4("gather_mod", mods).reshape(4, DEPTH, 16, n_ada)
    mod_mm = jnp.transpose(lax.dynamic_index_in_dim(mods_all, seq, axis=2, keepdims=False), (1, 0, 2)).reshape(DEPTH, 4 * n_ada)
    mod = _rowwise("mod_bias", lambda a, b: a + b, [mod_mm, ada_b], [], [(4 * n_ada, F32)])[0]

    dx, dmod, gW, gP = _local_step(x.reshape(T, D), loss_target.reshape(T, D), positions.reshape(T, 1), mod, W, P)

    partials = [dmod, jnp.concatenate(gP["norm_mix_g"]), jnp.concatenate(gP["norm_ffn_g"]), gP["final_g"],
                jnp.stack([jnp.transpose(g) for g in gP["gdn_cw"]]), jnp.concatenate(gP["gdn_alog"])[:, :NH],
                jnp.concatenate(gP["gdn_dtb"])[:, :NH], jnp.concatenate(gP["gdn_ng"]), jnp.concatenate(gP["mla_qg"]),
                jnp.concatenate(gP["mla_kvg"]), gP["loss"][:, :1]]
    part_shapes = [p.shape for p in partials]
    ppack, _ = _pack_rows_each(partials)
    pall = _allgather8("gather_partials", ppack)
    psum = _sum_slots("sum_partials", pall, F32)
    (g_ada_b, g_norm_mix, g_norm_ffn, g_final, g_conv_full, g_alog, g_dtb, g_gdn_ng, g_qg_full, g_kvg_full,
     loss_sum) = _unpack_rows_each(psum, part_shapes)
    dmod_all = _unpack_rows_each(pall, part_shapes[:1])[0]

    grads = dict(ada_b=g_ada_b, norm_mix_g=g_norm_mix, norm_ffn_g=g_norm_ffn, final_norm_g=g_final.reshape(D),
                 gdn_conv_w=_my_shard(g_conv_full, 1, chip), gdn_a_log=g_alog, gdn_dt_bias=g_dtb, gdn_norm_g=g_gdn_ng,
                 mla_q_norm_g=_my_shard(g_qg_full, 1, chip), mla_kv_norm_g=_my_shard(g_kvg_full, 1, chip))

    ca_t = jnp.zeros((D, LANES), BF16).at[:, :16].set(jnp.transpose(ca))
    dm_mine = lax.dynamic_slice_in_dim(dmod_all, chip * n_ada, n_ada, axis=2)
    grads["ada_w"] = jnp.stack([
        _mm(f"ada_bwd{l}", ca_t, jnp.pad(dm_mine[:, l], ((0, LANES - n_dev), (0, 0))), "nn") for l in range(DEPTH)])

    gchips = _grads_to_chips(gW)
    glist = [gchips[n] for n in big]
    theirs = _rs_split("grads_cores", glist)
    core = ic.astype(jnp.int32).reshape(1)
    pairs = [_pair_add("grads_pair_" + n, g, t, core) for n, g, t in zip(big, glist, theirs)]
    own = [_own_slot("grads_own_" + n, p, chip_arr) for n, p in zip(big, pairs)]
    swapped = _rs_alltoall("grads_chips", pairs, own)
    halves = [_sum_slots("grads_sum_" + n, s, F32) for n, s in zip(big, swapped)]
    other_halves = _rs_swap("grads_swap", halves)

    delta, new_m, new_v = {}, {}, {}
    for n, mine, theirs in zip(big, halves, other_halves):
        grads[n], delta[n], new_m[n], new_v[n] = _adamw_halves("adamw_" + n, w[n], m[n], v[n], mine, theirs)
    delta["ada_w"], new_m["ada_w"], new_v["ada_w"] = _adamw("adamw_ada_w", ada_w, grads["ada_w"], m_ada_w, v_ada_w)
    small_names = [n for n in _WEIGHT_ORDER if n not in delta]
    small_shapes = [w[n].shape for n in small_names]
    packs = [_pack_rows_each([d[n] for n in small_names])[0] for d in (w, grads, m, v)]
    for d, pk in zip((delta, new_m, new_v), _adamw("adamw_small", *packs)):
        for n, t in zip(small_names, _unpack_rows_each(pk, small_shapes)):
            d[n] = t

    loss = loss_sum.reshape(())
    return (loss, dx.reshape(1, T, D), *[grads[n] for n in _WEIGHT_ORDER], *[delta[n] for n in _WEIGHT_ORDER],
            *[new_m[n] for n in _WEIGHT_ORDER], *[new_v[n] for n in _WEIGHT_ORDER])
```

```python
import functools

import jax
import jax.numpy as jnp
from jax import lax
from jax.experimental import pallas as pl
from jax.experimental.pallas import tpu as pltpu
from jax.experimental.pallas import tpu_sc as plsc

F32 = jnp.float32
BF16 = jnp.bfloat16
HI = lax.Precision.HIGHEST
MESH = pl.DeviceIdType.MESH

D = 1024
DEPTH = 4
N_MOD = 6
NH = 8
HD = 128
CHUNK = 64
_GDN_HB = 4
GDN_QKV = 3 * NH * HD
GDN_INK = GDN_QKV + NH * HD + 2 * HD
Q_RANK, KV_RANK, ROPE = 384, 256, 64
MLA_INK = Q_RANK + KV_RANK + HD
DFF = 2816
EPS = 1e-6
ATT_SCALE = (HD + ROPE) ** -0.5
ROPE_THETA = 10000.0
LANES = 128
PACK_W = 1024

ADAM_LR, ADAM_B1, ADAM_B2, ADAM_EPS, ADAM_WD, ADAM_STEP = 0.001, 0.9, 0.999, 1e-08, 0.01, 10


H3 = "bf16x3"
B1 = "bf16"
HS = H3


def _dot(a, b, mode="nn", prec=None):
    dn = {"nn": (((1,), (0,)), ((), ())), "nt": (((1,), (1,)), ((), ())), "tn": (((0,), (0,)), ((), ()))}[mode]
    if prec == B1:
        return _dot(a.astype(BF16), b.astype(BF16), mode)
    if prec == H3:
        ah, bh = a.astype(BF16), b.astype(BF16)
        al, bl = (a - ah.astype(F32)).astype(BF16), (b - bh.astype(F32)).astype(BF16)
        return _dot(ah, bh, mode) + (_dot(ah, bl, mode) + _dot(al, bh, mode))
    return lax.dot_general(a, b, dn, precision=prec, preferred_element_type=F32)


def _sig(x):
    return 1.0 / (1.0 + jnp.exp(-x))


def _pick(n, cap):
    if n <= cap:
        return n
    best = None
    for d in range(LANES, cap + 1, LANES):
        if n % d == 0:
            best = d
    assert best is not None, (n, cap)
    return best


def _params(n_grid):
    return pltpu.CompilerParams(dimension_semantics=("arbitrary",) * n_grid, vmem_limit_bytes=56 * 1024 * 1024)


def _rowwise(name, fn, rows, consts, outs, sums=(), tr=256):
    first = rows[0][0] if isinstance(rows[0], tuple) else rows[0]
    T = first.shape[-2]
    tr = min(tr, T)
    while T % tr:
        tr //= 2
    nr, nc, no, ns = len(rows), len(consts), len(outs), len(sums)

    def body(*refs):
        res = fn(*[r[...] for r in refs[:nr + nc]])
        if not isinstance(res, (tuple, list)):
            res = (res,)
        o_refs = refs[nr + nc:nr + nc + no]
        s_refs = refs[nr + nc + no:]
        for r, val in zip(o_refs, res[:no]):
            r[...] = val.astype(r.dtype)
        if ns:
            @pl.when(pl.program_id(0) == 0)
            def _():
                for r in s_refs:
                    r[...] = jnp.zeros_like(r)
            for r, val in zip(s_refs, res[no:]):
                r[...] += val

    in_specs, args = [], []
    for a in rows:
        if isinstance(a, tuple):
            arr, width, cb = a
            in_specs.append(pl.BlockSpec((tr, width), lambda i, cb=cb: (i, cb)))
            args.append(arr)
        elif a.ndim == 3:
            in_specs.append(pl.BlockSpec((a.shape[0], tr, a.shape[2]), lambda i: (0, i, 0)))
            args.append(a)
        else:
            in_specs.append(pl.BlockSpec((tr, a.shape[1]), lambda i: (i, 0)))
            args.append(a)
    for a in consts:
        in_specs.append(pl.BlockSpec(a.shape, lambda i, nd=a.ndim: (0,) * nd))
        args.append(a)
    out_specs = [pl.BlockSpec((tr, w), lambda i: (i, 0)) for w, _ in outs]
    out_specs += [pl.BlockSpec((1, w), lambda i: (0, 0)) for w in sums]
    out_shape = [jax.ShapeDtypeStruct((T, w), dt) for w, dt in outs]
    out_shape += [jax.ShapeDtypeStruct((1, w), F32) for w in sums]
    res = pl.pallas_call(body, name=name, grid=(T // tr,), in_specs=in_specs, out_specs=out_specs,
                         out_shape=out_shape, compiler_params=_params(1))(*args)
    return res


def _mm(name, a, b, mode, out_dtype=F32, tm=512, tn=1024):
    if mode == "tn":
        K, M = a.shape
    else:
        M, K = a.shape
    N = b.shape[0] if mode == "nt" else b.shape[1]
    tm, tn = _pick(M, tm), _pick(N, tn)

    def body(a_ref, b_ref, o_ref):
        o_ref[...] = _dot(a_ref[...].astype(BF16), b_ref[...].astype(BF16), mode).astype(o_ref.dtype)

    a_spec = pl.BlockSpec((K, tm), lambda i, j: (0, i)) if mode == "tn" else pl.BlockSpec((tm, K), lambda i, j: (i, 0))
    b_spec = pl.BlockSpec((tn, K), lambda i, j: (j, 0)) if mode == "nt" else pl.BlockSpec((K, tn), lambda i, j: (0, j))
    return pl.pallas_call(body, name=name, grid=(M // tm, N // tn), in_specs=[a_spec, b_spec],
                          out_specs=pl.BlockSpec((tm, tn), lambda i, j: (i, j)),
                          out_shape=jax.ShapeDtypeStruct((M, N), out_dtype), compiler_params=_params(2))(a, b)


def _rms(x, eps=EPS):
    return lax.rsqrt(jnp.mean(x * x, axis=-1, keepdims=True) + eps)


def _norm_mod_fwd(name, x, g, scale, shift):
    def fn(x, g, scale, shift):
        return x * _rms(x) * g * (1.0 + scale) + shift
    return _rowwise(name, fn, [x], [g, scale, shift], [(D, BF16)])[0]


def _norm_mod_bwd(name, dh, x, dx_res, g, scale):
    def fn(dh, x, dx_res, g, scale):
        r = _rms(x)
        xh = x * r
        dxh = dh * (g * (1.0 + scale))
        dx = r * (dxh - xh * jnp.mean(dxh * xh, axis=-1, keepdims=True))
        dhx = dh * xh
        return (dx_res + dx, jnp.sum(dh, axis=0, keepdims=True), jnp.sum(dhx * g, axis=0, keepdims=True),
                jnp.sum(dhx * (1.0 + scale), axis=0, keepdims=True))
    return _rowwise(name, fn, [dh, x, dx_res], [g, scale], [(D, F32)], sums=[D, D, D])


def _residual_fwd(name, x, y, gate):
    def fn(x, y, gate):
        return x + gate * y
    return _rowwise(name, fn, [x, y], [gate], [(D, F32)])[0]


def _residual_bwd(name, dx, y, gate):
    def fn(dx, y, gate):
        return dx * gate, jnp.sum(dx * y, axis=0, keepdims=True)
    return _rowwise(name, fn, [dx, y], [gate], [(D, BF16)], sums=[D])


def _loss_head(x, target, g):
    def fn(x, t, g):
        r = _rms(x)
        xh = x * r
        err = xh * g - t
        loss = 0.5 * jnp.sum(jnp.mean(err * err, axis=-1, keepdims=True), axis=0, keepdims=True)
        dy = err * (1.0 / D)
        dxh = dy * g
        dx = r * (dxh - xh * jnp.mean(dxh * xh, axis=-1, keepdims=True))
        return dx, jnp.broadcast_to(loss, (1, LANES)), jnp.sum(dy * xh, axis=0, keepdims=True)
    return _rowwise("loss_head", fn, [x, target], [g], [(D, F32)], sums=[LANES, D])


def _ffn_up(name, h, wg, wu, layer, tm=512):
    T, n = h.shape[0], wg.shape[2]
    tm = min(tm, T)

    def body(h_ref, wg_ref, wu_ref, a_ref, b_ref, s_ref):
        h = h_ref[...]
        a = _dot(h, wg_ref[0], "nn")
        b = _dot(h, wu_ref[0], "nn")
        a_ref[0] = a
        b_ref[0] = b
        s_ref[0] = (a * _sig(a) * b).astype(s_ref.dtype)

    wspec = pl.BlockSpec((1, D, n), lambda ch, i: (ch, layer, 0))
    ospec = pl.BlockSpec((1, tm, n), lambda ch, i: (ch, i, 0))
    return pl.pallas_call(
        body, name=name, grid=(4, T // tm), in_specs=[pl.BlockSpec((tm, D), lambda ch, i: (i, 0)), wspec, wspec],
        out_specs=[ospec, ospec, ospec],
        out_shape=[jax.ShapeDtypeStruct((4, T, n), F32)] * 2 + [jax.ShapeDtypeStruct((4, T, n), BF16)],
        compiler_params=_params(2))(h, wg, wu)


def _ffn_down(name, s, wd, layer, tm=512):
    _, T, n = s.shape
    tm = min(tm, T)

    def body(s_ref, w_ref, y_ref):
        @pl.when(pl.program_id(1) == 0)
        def _():
            y_ref[...] = jnp.zeros_like(y_ref)
        y_ref[...] += _dot(s_ref[0], w_ref[0], "nn")

    return pl.pallas_call(
        body, name=name, grid=(T // tm, 4),
        in_specs=[pl.BlockSpec((1, tm, n), lambda i, ch: (ch, i, 0)), pl.BlockSpec((1, n, D), lambda i, ch: (ch, layer, 0))],
        out_specs=pl.BlockSpec((tm, D), lambda i, ch: (i, 0)), out_shape=jax.ShapeDtypeStruct((T, D), F32),
        compiler_params=_params(2))(s, wd)


def _ffn_down_bwd(name, dy, wd, a, b, layer, tm=512):
    _, T, n = a.shape
    tm = min(tm, T)

    def body(dy_ref, w_ref, a_ref, b_ref, da_ref, db_ref):
        ds = _dot(dy_ref[...], w_ref[0], "nt")
        a, b = a_ref[0], b_ref[0]
        sg = _sig(a)
        da_ref[0] = (ds * b * (sg * (1.0 + a * (1.0 - sg)))).astype(da_ref.dtype)
        db_ref[0] = (ds * (a * sg)).astype(db_ref.dtype)

    bspec = pl.BlockSpec((1, tm, n), lambda ch, i: (ch, i, 0))
    return pl.pallas_call(
        body, name=name, grid=(4, T // tm),
        in_specs=[pl.BlockSpec((tm, D), lambda ch, i: (i, 0)), pl.BlockSpec((1, n, D), lambda ch, i: (ch, layer, 0)), bspec, bspec],
        out_specs=[bspec, bspec], out_shape=[jax.ShapeDtypeStruct((4, T, n), BF16)] * 2,
        compiler_params=_params(2))(dy, wd, a, b)


def _ffn_down_dw(name, s, dy):
    _, T, n = s.shape

    def body(s_ref, dy_ref, o_ref):
        o_ref[0] = _dot(s_ref[0], dy_ref[...], "tn").astype(o_ref.dtype)

    return pl.pallas_call(
        body, name=name, grid=(4,),
        in_specs=[pl.BlockSpec((1, T, n), lambda ch: (ch, 0, 0)), pl.BlockSpec((T, D), lambda ch: (0, 0))],
        out_specs=pl.BlockSpec((1, n, D), lambda ch: (ch, 0, 0)), out_shape=jax.ShapeDtypeStruct((4, n, D), BF16),
        compiler_params=_params(1))(s, dy)


def _ffn_up_dw(name, h, da, db, tm=512):
    _, T, n = da.shape

    def body(h_ref, da_ref, db_ref, dg_ref, du_ref):
        h = h_ref[...]
        dg_ref[0] = _dot(h, da_ref[0], "tn").astype(dg_ref.dtype)
        du_ref[0] = _dot(h, db_ref[0], "tn").astype(du_ref.dtype)

    dspec = pl.BlockSpec((1, T, n), lambda ch, j: (ch, 0, 0))
    ospec = pl.BlockSpec((1, tm, n), lambda ch, j: (ch, j, 0))
    return pl.pallas_call(
        body, name=name, grid=(4, D // tm), in_specs=[pl.BlockSpec((T, tm), lambda ch, j: (0, j)), dspec, dspec],
        out_specs=[ospec, ospec], out_shape=[jax.ShapeDtypeStruct((4, D, n), BF16)] * 2,
        compiler_params=_params(2))(h, da, db)


def _ffn_up_dx(name, da, db, wg, wu, layer, tm=512):
    _, T, n = da.shape
    tm = min(tm, T)

    def body(da_ref, db_ref, wg_ref, wu_ref, o_ref):
        @pl.when(pl.program_id(1) == 0)
        def _():
            o_ref[...] = jnp.zeros_like(o_ref)
        o_ref[...] += _dot(da_ref[0], wg_ref[0], "nt") + _dot(db_ref[0], wu_ref[0], "nt")

    dspec = pl.BlockSpec((1, tm, n), lambda i, ch: (ch, i, 0))
    wspec = pl.BlockSpec((1, D, n), lambda i, ch: (ch, layer, 0))
    return pl.pallas_call(
        body, name=name, grid=(T // tm, 4), in_specs=[dspec, dspec, wspec, wspec],
        out_specs=pl.BlockSpec((tm, D), lambda i, ch: (i, 0)), out_shape=jax.ShapeDtypeStruct((T, D), F32),
        compiler_params=_params(2))(da, db, wg, wu)


def _shift_down(x, k):
    if k == 0:
        return x
    rows = lax.broadcasted_iota(jnp.int32, x.shape, 0)
    return jnp.where(rows >= k, pltpu.roll(x, k, 0), 0.0)


def _shift_up(x, k):
    if k == 0:
        return x
    T = x.shape[0]
    rows = lax.broadcasted_iota(jnp.int32, x.shape, 0)
    return jnp.where(rows < T - k, pltpu.roll(x, T - k, 0), 0.0)


def _conv_silu(x, w):
    c = w[0:1, :] * _shift_down(x, 3) + w[1:2, :] * _shift_down(x, 2) + w[2:3, :] * _shift_down(x, 1) + w[3:4, :] * x
    sg = _sig(c)
    return c, sg, c * sg


def _gdn_conv_fwd(name, proj, cw):
    T = proj.shape[0]

    def body(x_ref, w_ref, o_ref):
        j = pl.program_id(0)
        _, _, y = _conv_silu(x_ref[...], w_ref[...])
        r = lax.rsqrt(jnp.sum(y * y, axis=1, keepdims=True) + EPS)
        mult = jnp.where(j < NH, HD ** -0.5, 1.0)
        o_ref[...] = jnp.where(j < 2 * NH, y * (r * mult), y)

    return pl.pallas_call(body, name=name, grid=(3 * NH,),
                          in_specs=[pl.BlockSpec((T, HD), lambda j: (0, j)), pl.BlockSpec((4, HD), lambda j: (0, j))],
                          out_specs=pl.BlockSpec((T, HD), lambda j: (0, j)),
                          out_shape=jax.ShapeDtypeStruct((T, GDN_QKV), F32), compiler_params=_params(1))(proj, cw)


def _gdn_conv_bwd(name, proj, cw, dz):
    T = proj.shape[0]

    def body(x_ref, w_ref, dz_ref, dx_ref, dw_ref):
        j = pl.program_id(0)
        x, w, dz = x_ref[...], w_ref[...], dz_ref[...]
        c, sg, y = _conv_silu(x, w)
        r = lax.rsqrt(jnp.sum(y * y, axis=1, keepdims=True) + EPS)
        mult = jnp.where(j < NH, HD ** -0.5, 1.0)
        dyn = mult * (r * dz - (r * r * r) * y * jnp.sum(dz * y, axis=1, keepdims=True))
        dy = jnp.where(j < 2 * NH, dyn, dz)
        dc = dy * (sg * (1.0 + c * (1.0 - sg)))
        dx = w[0:1, :] * _shift_up(dc, 3) + w[1:2, :] * _shift_up(dc, 2) + w[2:3, :] * _shift_up(dc, 1) + w[3:4, :] * dc
        dx_ref[...] = dx.astype(dx_ref.dtype)
        for k in range(4):
            dw_ref[pl.ds(k, 1), :] = jnp.sum(dc * _shift_down(x, 3 - k), axis=0, keepdims=True)

    return pl.pallas_call(body, name=name, grid=(3 * NH,),
                          in_specs=[pl.BlockSpec((T, HD), lambda j: (0, j)), pl.BlockSpec((4, HD), lambda j: (0, j)),
                                    pl.BlockSpec((T, HD), lambda j: (0, j))],
                          out_specs=[pl.BlockSpec((T, HD), lambda j: (0, j)), pl.BlockSpec((4, HD), lambda j: (0, j))],
                          out_shape=[jax.ShapeDtypeStruct((T, GDN_QKV), BF16), jax.ShapeDtypeStruct((4, GDN_QKV), F32)],
                          compiler_params=_params(1))(proj, cw, dz)


def _softplus(z):
    return jnp.maximum(z, 0.0) + jnp.log(1.0 + jnp.exp(-jnp.abs(z)))


_AB_CB = GDN_INK // (2 * HD) - 1


def _gdn_gates_fwd(name, proj, alog, dtb):
    def fn(ab, alog, dtb):
        a, b = ab[:, :HD], ab[:, HD:]
        return -jnp.exp(alog) * _softplus(a + dtb), _sig(b)
    return _rowwise(name, fn, [(proj, 2 * HD, _AB_CB)], [alog, dtb], [(HD, F32), (HD, F32)])


def _gdn_gates_bwd(name, proj, dg_h, db_h, alog, dtb):
    def fn(ab, dg_h, db_h, alog, dtb):
        lane = lax.broadcasted_iota(jnp.int32, (1, HD), 1)
        dg = jnp.zeros(dg_h.shape[1:], F32)
        dbeta = jnp.zeros(dg_h.shape[1:], F32)
        for h in range(NH):
            oh = (lane == h).astype(F32)
            dg = dg + dg_h[h] * oh
            dbeta = dbeta + db_h[h] * oh
        a, b = ab[:, :HD], ab[:, HD:]
        z = a + dtb
        ea = jnp.exp(alog)
        beta = _sig(b)
        da = dg * (-ea) * _sig(z)
        db = dbeta * beta * (1.0 - beta)
        return (jnp.concatenate([da, db], axis=1), jnp.sum(dg * (-ea * _softplus(z)), axis=0, keepdims=True),
                jnp.sum(da, axis=0, keepdims=True))
    return _rowwise(name, fn, [(proj, 2 * HD, _AB_CB), dg_h, db_h], [alog, dtb], [(2 * HD, BF16)], sums=[HD, HD])


def _interleave(gens):
    gens = list(gens)
    results = [None] * len(gens)
    active = list(range(len(gens)))
    while active:
        for i in list(active):
            try:
                next(gens[i])
            except StopIteration as stop:
                results[i] = stop.value
                active.remove(i)
    return results


def _chunk_common(q, k, v, gblk, bblk, h):
    C = CHUNK
    lane = lax.broadcasted_iota(jnp.int32, (1, HD), 1)
    oh = (lane == h).astype(F32)
    g_col = jnp.sum(gblk * oh, axis=1, keepdims=True)
    beta = jnp.sum(bblk * oh, axis=1, keepdims=True)
    ri = lax.broadcasted_iota(jnp.int32, (C, C), 0)
    ci = lax.broadcasted_iota(jnp.int32, (C, C), 1)
    incl = ri >= ci
    strict = ri > ci
    eye = (ri == ci).astype(F32)
    gcb = _dot(incl.astype(F32), jnp.broadcast_to(g_col, (C, HD)), "nn", HI)
    yield
    gc = gcb[:, :C]
    gc_row = _dot(jnp.ones((C, C), F32), eye * gc, "nn", HI)
    yield
    decay = jnp.where(incl, jnp.exp(jnp.where(incl, gc - gc_row, 0.0)), 0.0)
    rows = lax.broadcasted_iota(jnp.int32, (C, HD), 0)
    gclb = jnp.sum(jnp.where(rows == C - 1, gcb, 0.0), axis=0, keepdims=True)
    eg = jnp.exp(gcb)
    egl = jnp.exp(gclb - gcb)
    gl = jnp.exp(gclb)
    kb = k * beta
    m1 = _dot(kb, k, "nt", HS)
    qk = _dot(q, k, "nt", HS)
    yield
    L = jnp.where(strict, m1 * decay, 0.0)
    nl = -L
    tinv = eye + nl
    p = nl
    for _ in range(5):
        p = _dot(p, p, "nn", H3)
        yield
        tinv = tinv + _dot(tinv, p, "nn", H3)
    vb = v * beta
    kbg = kb * eg
    yield
    u = _dot(tinv, vb, "nn", HS)
    w = _dot(tinv, kbg, "nn", HS)
    yield
    attn = jnp.where(incl, qk * decay, 0.0)
    return dict(beta=beta, incl=incl, strict=strict, decay=decay, eg=eg, egl=egl, gl=gl, kb=kb, m1=m1, tinv=tinv,
                kbg=kbg, u=u, w=w, qk=qk, attn=attn, q_dec=q * eg, k_dec=k * egl, rows=rows, oh=oh)


def _gdn_chunk_fwd(name, qkv, g, beta):
    T = qkv.shape[0]
    N = T // CHUNK

    hb = _GDN_HB
    w = hb * HD

    def body(q_ref, k_ref, v_ref, g_ref, b_ref, o_ref, st_ref, S):
        hg, n = pl.program_id(0), pl.program_id(1)

        @pl.when(n == 0)
        def _():
            S[...] = jnp.zeros_like(S)

        gblk, bblk = g_ref[...], b_ref[...]

        def one_head(i, q, k, v, s):
            c = yield from _chunk_common(q, k, v, gblk, bblk, hg * hb + i)
            v_new = c["u"] - _dot(c["w"], s, "nn", HS)
            qs = _dot(c["q_dec"], s, "nn", HS)
            yield
            o = qs + _dot(c["attn"], v_new, "nn", HS)
            return o, s * c["gl"] + _dot(c["k_dec"], v_new, "tn", HS)

        sls = [slice(i * HD, (i + 1) * HD) for i in range(hb)]
        states = [S[i] for i in range(hb)]
        res = _interleave(one_head(i, q_ref[:, sls[i]], k_ref[:, sls[i]], v_ref[:, sls[i]], states[i]) for i in range(hb))
        for i, (o, s_new) in enumerate(res):
            st_ref[i, 0] = states[i]
            o_ref[:, sls[i]] = o
            S[i] = s_new

    blk = lambda off: pl.BlockSpec((CHUNK, w), lambda h, n, off=off: (n, off + h))
    gspec = pl.BlockSpec((CHUNK, HD), lambda h, n: (n, 0))
    return pl.pallas_call(
        body, name=name, grid=(NH // hb, N), in_specs=[blk(0), blk(NH // hb), blk(2 * NH // hb), gspec, gspec],
        out_specs=[pl.BlockSpec((CHUNK, w), lambda h, n: (n, h)), pl.BlockSpec((hb, 1, HD, HD), lambda h, n: (h, n, 0, 0))],
        out_shape=[jax.ShapeDtypeStruct((T, NH * HD), F32), jax.ShapeDtypeStruct((NH, N, HD, HD), F32)],
        scratch_shapes=[pltpu.VMEM((hb, HD, HD), F32)], compiler_params=_params(2))(qkv, qkv, qkv, g, beta)


def _gdn_chunk_bwd(name, qkv, g, beta, states, do):
    T = qkv.shape[0]
    N = T // CHUNK
    C = CHUNK

    hb = _GDN_HB
    w = hb * HD

    def body(q_ref, k_ref, v_ref, g_ref, b_ref, st_ref, do_ref, dq_ref, dk_ref, dv_ref, dg_ref, db_ref, dS):
        hg, n = pl.program_id(0), pl.program_id(1)

        @pl.when(n == 0)
        def _():
            dS[...] = jnp.zeros_like(dS)

        gblk, bblk = g_ref[...], b_ref[...]
        sls = [slice(i * HD, (i + 1) * HD) for i in range(hb)]
        res = _interleave(one_head(hg * hb + i, gblk, bblk, q_ref[:, sls[i]], k_ref[:, sls[i]], v_ref[:, sls[i]],
                                   st_ref[i, 0], do_ref[:, sls[i]], dS[i]) for i in range(hb))
        for i, (dq, dk, dv, dg, db, ds_new) in enumerate(res):
            dq_ref[:, sls[i]] = dq
            dk_ref[:, sls[i]] = dk
            dv_ref[:, sls[i]] = dv
            dg_ref[i] = dg
            db_ref[i] = db
            dS[i] = ds_new

    def one_head(h, gblk, bblk, q, k, v, s, do, ds):
        c = yield from _chunk_common(q, k, v, gblk, bblk, h)
        eg, egl, gl, beta, decay, tinv = c["eg"], c["egl"], c["gl"], c["beta"], c["decay"], c["tinv"]
        v_new = c["u"] - _dot(c["w"], s, "nn", HS)
        dq_dec = _dot(do, s, "nt", HS)
        yield
        dv_new = _dot(c["attn"], do, "tn", HS) + _dot(c["k_dec"], ds, "nn", HS)
        dk_dec = _dot(v_new, ds, "nt", HS)
        dgl = jnp.sum(jnp.sum(s * ds, axis=1, keepdims=True), axis=0, keepdims=True)
        yield
        ds_new = ds * gl + _dot(c["q_dec"], do, "tn", HS) - _dot(c["w"], dv_new, "tn", HS)
        dattn = jnp.where(c["incl"], _dot(do, v_new, "nt", HS), 0.0)
        dw = -_dot(dv_new, s, "nt", HS)
        yield
        dvb = _dot(tinv, dv_new, "tn", HS)
        dkbg = _dot(tinv, dw, "tn", HS)
        yield
        dA = -(_dot(dvb, c["u"], "nt", HS) + _dot(dkbg, c["w"], "nt", HS))
        yield
        dL = jnp.where(c["strict"], dA, 0.0)
        dm1 = dL * decay
        dqk = dattn * decay
        xdec = (dL * c["m1"] + dattn * c["qk"]) * decay
        dkb = _dot(dm1, k, "nn", HS) + dkbg * eg
        dk = _dot(dm1, c["kb"], "tn", HS) + _dot(dqk, q, "tn", HS) + dk_dec * egl + dkb * beta
        dq = _dot(dqk, k, "nn", HS) + dq_dec * eg
        yield
        dkd_kd = jnp.sum(dk_dec * c["k_dec"], axis=1, keepdims=True)
        dgc = (jnp.sum(xdec, axis=1, keepdims=True) - _dot(xdec, jnp.ones((C, HD), F32), "tn", HS)
               + jnp.sum(dq_dec * c["q_dec"], axis=1, keepdims=True) - dkd_kd
               + jnp.sum(dkbg * c["kbg"], axis=1, keepdims=True))
        dgcl = jnp.sum(dkd_kd, axis=0, keepdims=True) + dgl * gl
        dgc = dgc + jnp.where(c["rows"] == C - 1, dgcl, 0.0)
        ri = lax.broadcasted_iota(jnp.int32, (C, C), 0)
        ci = lax.broadcasted_iota(jnp.int32, (C, C), 1)
        dg = _dot((ci >= ri).astype(F32), dgc, "nn", HI)
        db = jnp.broadcast_to(jnp.sum(dkb * k, axis=1, keepdims=True) + jnp.sum(dvb * v, axis=1, keepdims=True), (C, HD))
        return dq, dk, dvb * beta, dg, db, ds_new

    blk = lambda off: pl.BlockSpec((C, w), lambda h, n, off=off: (N - 1 - n, off + h))
    gspec = pl.BlockSpec((C, HD), lambda h, n: (N - 1 - n, 0))
    ospec = pl.BlockSpec((C, w), lambda h, n: (N - 1 - n, h))
    hspec = pl.BlockSpec((hb, C, HD), lambda h, n: (h, N - 1 - n, 0))
    return pl.pallas_call(
        body, name=name, grid=(NH // hb, N),
        in_specs=[blk(0), blk(NH // hb), blk(2 * NH // hb), gspec, gspec,
                  pl.BlockSpec((hb, 1, HD, HD), lambda h, n: (h, N - 1 - n, 0, 0)), ospec],
        out_specs=[ospec, ospec, ospec, hspec, hspec],
        out_shape=[jax.ShapeDtypeStruct((T, NH * HD), F32)] * 3 + [jax.ShapeDtypeStruct((NH, T, HD), F32)] * 2,
        scratch_shapes=[pltpu.VMEM((hb, HD, HD), F32)], compiler_params=_params(2))(qkv, qkv, qkv, g, beta, states, do)


_GATE_CB = GDN_QKV // (NH * HD)


def _gdn_gated_norm_fwd(name, o, proj, ng):
    def fn(o, gate, ng):
        outs = []
        for h in range(NH):
            sl = slice(h * HD, (h + 1) * HD)
            oh, gh = o[:, sl], gate[:, sl]
            outs.append(oh * _rms(oh) * ng * (gh * _sig(gh)))
        return jnp.concatenate(outs, axis=1)
    return _rowwise(name, fn, [o, (proj, NH * HD, _GATE_CB)], [ng], [(NH * HD, BF16)])[0]


def _gdn_gated_norm_bwd(name, don, o, proj, ng):
    def fn(don, o, gate, ng):
        dos, dgs = [], []
        dng = jnp.zeros((1, HD), F32)
        for h in range(NH):
            sl = slice(h * HD, (h + 1) * HD)
            oh, gh, dh = o[:, sl], gate[:, sl], don[:, sl]
            r = _rms(oh)
            xh = oh * r
            sg = _sig(gh)
            dn = dh * (gh * sg)
            dgs.append(dh * (xh * ng) * (sg * (1.0 + gh * (1.0 - sg))))
            dng = dng + jnp.sum(dn * xh, axis=0, keepdims=True)
            dxh = dn * ng
            dos.append(r * (dxh - xh * jnp.mean(dxh * xh, axis=-1, keepdims=True)))
        return jnp.concatenate(dos, axis=1), jnp.concatenate(dgs, axis=1), dng
    return _rowwise(name, fn, [don, o, (proj, NH * HD, _GATE_CB)], [ng], [(NH * HD, F32), (NH * HD, BF16)], sums=[HD])


def _rot(x):
    lane = lax.broadcasted_iota(jnp.int32, x.shape, 1)
    return jnp.where(lane < ROPE // 2, -pltpu.roll(x, HD - ROPE // 2, 1), pltpu.roll(x, ROPE // 2, 1))


def _rot_t(x):
    lane = lax.broadcasted_iota(jnp.int32, x.shape, 1)
    return jnp.where(lane < ROPE // 2, pltpu.roll(x, HD - ROPE // 2, 1), -pltpu.roll(x, ROPE // 2, 1))


def _rope_tables(pos_col):
    lane = jnp.arange(HD)
    inv_freq = ROPE_THETA ** (-(2.0 * (lane % (ROPE // 2)).astype(F32)) / ROPE)
    inv_freq = jnp.where(lane < ROPE, inv_freq, 0.0).astype(F32)[None, :]
    valid = (lane < ROPE).astype(F32)[None, :]

    def fn(pos, inv_freq, valid):
        ang = pos.astype(F32) * inv_freq
        return jnp.cos(ang) * valid, jnp.sin(ang) * valid
    return _rowwise("rope_tables", fn, [pos_col], [inv_freq, valid], [(HD, F32), (HD, F32)])


def _mla_pre_fwd(name, proj, cos, sin, qg, kvg):
    def fn(p, cos, sin, qg, kvg):
        cq, ckv, kr = p[:, :Q_RANK], p[:, Q_RANK:Q_RANK + KV_RANK], p[:, Q_RANK + KV_RANK:]
        return cq * _rms(cq) * qg, ckv * _rms(ckv) * kvg, kr * cos + _rot(kr) * sin
    return _rowwise(name, fn, [proj, cos, sin], [qg, kvg], [(Q_RANK, BF16), (KV_RANK, BF16), (HD, BF16)])


def _rms_bwd(dy, x, g):
    r = _rms(x)
    xh = x * r
    dxh = dy * g
    return r * (dxh - xh * jnp.mean(dxh * xh, axis=-1, keepdims=True)), jnp.sum(dy * xh, axis=0, keepdims=True)


def _mla_pre_bwd(name, proj, dcqn, dckvn, dkr, cos, sin, qg, kvg):
    def fn(p, dcqn, dckvn, dkr, cos, sin, qg, kvg):
        cq, ckv = p[:, :Q_RANK], p[:, Q_RANK:Q_RANK + KV_RANK]
        dcq, dqg = _rms_bwd(dcqn, cq, qg)
        dckv, dkvg = _rms_bwd(dckvn, ckv, kvg)
        dkr_pre = dkr * cos + _rot_t(dkr * sin)
        return jnp.concatenate([dcq, dckv, dkr_pre], axis=1), dqg, dkvg
    return _rowwise(name, fn, [proj, dcqn, dckvn, dkr, cos, sin], [qg, kvg], [(MLA_INK, BF16)], sums=[Q_RANK, KV_RANK])


def _mla_q_fwd(name, q, cos, sin):
    def fn(qn, qr, cos, sin):
        outs = []
        for h in range(NH):
            x = qr[:, h * HD:(h + 1) * HD]
            outs.append(x * cos + _rot(x) * sin)
        return qn, jnp.concatenate(outs, axis=1)
    return _rowwise(name, fn, [(q, NH * HD, 0), (q, NH * HD, 1), cos, sin], [], [(NH * HD, BF16), (NH * HD, BF16)])


def _mla_q_bwd(name, dqn, dqr, cos, sin):
    def fn(dqn, dqr, cos, sin):
        outs = [dqn]
        for h in range(NH):
            z = dqr[:, h * HD:(h + 1) * HD]
            outs.append(z * cos + _rot_t(z * sin))
        return jnp.concatenate(outs, axis=1)
    return _rowwise(name, fn, [dqn, dqr, cos, sin], [], [(2 * NH * HD, BF16)])[0]


def _att_probs(qn, qr, kn, kr, row0):
    s = (_dot(qn, kn, "nt") + _dot(qr, kr, "nt")) * ATT_SCALE
    qpos = row0 + lax.broadcasted_iota(jnp.int32, s.shape, 0)
    kpos = lax.broadcasted_iota(jnp.int32, s.shape, 1)
    s = jnp.where(kpos <= qpos, s, -1e30)
    p = jnp.exp(s - jnp.max(s, axis=1, keepdims=True))
    return p / jnp.sum(p, axis=1, keepdims=True)


def _mla_attn_fwd(name, qn, qr, kv, kr, tq=256):
    T = qn.shape[0]
    tq = min(tq, T)

    def body(qn_ref, qr_ref, kn_ref, v_ref, kr_ref, o_ref):
        i = pl.program_id(1)
        for blk in range(T // tq):
            @pl.when(i == blk)
            def _(blk=blk):
                keys = pl.ds(0, (blk + 1) * tq)
                p = _att_probs(qn_ref[...], qr_ref[...], kn_ref[keys, :], kr_ref[keys, :], blk * tq)
                o_ref[...] = _dot(p.astype(BF16), v_ref[keys, :], "nn").astype(o_ref.dtype)

    qspec = pl.BlockSpec((tq, HD), lambda h, i: (i, h))
    return pl.pallas_call(
        body, name=name, grid=(NH, T // tq),
        in_specs=[qspec, qspec, pl.BlockSpec((T, HD), lambda h, i: (0, h)), pl.BlockSpec((T, HD), lambda h, i: (0, NH + h)),
                  pl.BlockSpec((T, HD), lambda h, i: (0, 0))],
        out_specs=qspec, out_shape=jax.ShapeDtypeStruct((T, NH * HD), BF16), compiler_params=_params(2))(qn, qr, kv, kv, kr)


def _mla_attn_bwd(name, qn, qr, kv, kr, do, tq=256):
    T = qn.shape[0]
    tq = min(tq, T)

    def body(qn_ref, qr_ref, kn_ref, v_ref, kr_ref, do_ref, dqn_ref, dqr_ref, dkn_ref, dv_ref, dkr_ref):
        h, i = pl.program_id(0), pl.program_id(1)

        @pl.when(i == 0)
        def _():
            dkn_ref[...] = jnp.zeros_like(dkn_ref)
            dv_ref[...] = jnp.zeros_like(dv_ref)

        @pl.when((i == 0) & (h == 0))
        def _():
            dkr_ref[...] = jnp.zeros_like(dkr_ref)

        for blk in range(T // tq):
            @pl.when(i == blk)
            def _(blk=blk):
                keys = pl.ds(0, (blk + 1) * tq)
                qn, qr, do = qn_ref[...], qr_ref[...], do_ref[...]
                kn, kr, v = kn_ref[keys, :], kr_ref[keys, :], v_ref[keys, :]
                p = _att_probs(qn, qr, kn, kr, blk * tq)
                dp = _dot(do, v, "nt")
                ds = (p * (dp - jnp.sum(p * dp, axis=1, keepdims=True)) * ATT_SCALE).astype(BF16)
                dqn_ref[...] = _dot(ds, kn, "nn")
                dqr_ref[...] = _dot(ds, kr, "nn")
                dkn_ref[keys, :] += _dot(ds, qn, "tn")
                dkr_ref[keys, :] += _dot(ds, qr, "tn")
                dv_ref[keys, :] += _dot(p.astype(BF16), do, "tn")

    qspec = pl.BlockSpec((tq, HD), lambda h, i: (i, h))
    kspec = pl.BlockSpec((T, HD), lambda h, i: (0, h))
    return pl.pallas_call(
        body, name=name, grid=(NH, T // tq),
        in_specs=[qspec, qspec, kspec, pl.BlockSpec((T, HD), lambda h, i: (0, NH + h)),
                  pl.BlockSpec((T, HD), lambda h, i: (0, 0)), qspec],
        out_specs=[qspec, qspec, kspec, kspec, pl.BlockSpec((T, HD), lambda h, i: (0, 0))],
        out_shape=[jax.ShapeDtypeStruct((T, NH * HD), F32)] * 4 + [jax.ShapeDtypeStruct((T, HD), F32)],
        compiler_params=_params(2))(qn, qr, kv, kv, kr, do)


def _mod_rows(mod, layer):
    return [mod[layer:layer + 1, i * D:(i + 1) * D] for i in range(N_MOD)]


def _local_step(x, target, pos_col, mod, W, P):
    cos, sin = _rope_tables(pos_col)
    saved = []
    for l in range(DEPTH):
        j = l // 2
        sh_m, sc_m, ga_m, sh_f, sc_f, ga_f = _mod_rows(mod, l)
        s = dict(x0=x)
        h = _norm_mod_fwd(f"norm_mix{l}", x, P["norm_mix_g"][l:l + 1], sc_m, sh_m)
        s["h"] = h
        if l % 2 == 0:
            proj = _mm(f"gdn_in{j}", h, W["gdn_in"][j], "nn")
            qkv = _gdn_conv_fwd(f"gdn_conv{j}", proj, P["gdn_cw"][j])
            g, beta = _gdn_gates_fwd(f"gdn_gates{j}", proj, P["gdn_alog"][j], P["gdn_dtb"][j])
            o, states = _gdn_chunk_fwd(f"gdn_chunk{j}", qkv, g, beta)
            on = _gdn_gated_norm_fwd(f"gdn_gnorm{j}", o, proj, P["gdn_ng"][j])
            y = _mm(f"gdn_out{j}", on, W["gdn_out"][j], "nn")
            s.update(proj=proj, qkv=qkv, g=g, beta=beta, o=o, states=states, on=on)
        else:
            proj = _mm(f"mla_in{j}", h, W["mla_in"][j], "nn")
            cqn, ckvn, kr = _mla_pre_fwd(f"mla_pre{j}", proj, cos, sin, P["mla_qg"][j], P["mla_kvg"][j])
            q = _mm(f"mla_uq{j}", cqn, W["mla_uq"][j], "nn")
            kv = _mm(f"mla_ukv{j}", ckvn, W["mla_ukv"][j], "nn", out_dtype=BF16)
            qn, qr = _mla_q_fwd(f"mla_q{j}", q, cos, sin)
            o = _mla_attn_fwd(f"mla_attn{j}", qn, qr, kv, kr)
            y = _mm(f"mla_out{j}", o, W["mla_out"][j], "nn")
            s.update(proj=proj, cqn=cqn, ckvn=ckvn, kr=kr, kv=kv, qn=qn, qr=qr, o=o)
        s["y"] = y
        x = _residual_fwd(f"res_mix{l}", x, y, ga_m)
        s["x1"] = x
        h2 = _norm_mod_fwd(f"norm_ffn{l}", x, P["norm_ffn_g"][l:l + 1], sc_f, sh_f)
        fa, fb, sw = _ffn_up(f"ffn_up{l}", h2, W["ffn_g"][l], W["ffn_u"][l], 0)
        yf = _ffn_down(f"ffn_down{l}", sw, W["ffn_d"][l], 0)
        x = _residual_fwd(f"res_ffn{l}", x, yf, ga_f)
        s.update(h2=h2, fa=fa, fb=fb, sw=sw, yf=yf)
        saved.append(s)

    dx, loss, d_final = _loss_head(x, target, P["final_g"])
    gW = {k: [None] * (DEPTH if k.startswith("ffn") else 2) for k in W}
    gP = dict(loss=loss, final_g=d_final, norm_mix_g=[None] * DEPTH, norm_ffn_g=[None] * DEPTH,
              gdn_cw=[None] * 2, gdn_alog=[None] * 2, gdn_dtb=[None] * 2, gdn_ng=[None] * 2,
              mla_qg=[None] * 2, mla_kvg=[None] * 2)
    dmod = [None] * DEPTH
    for l in reversed(range(DEPTH)):
        j = l // 2
        s = saved[l]
        sh_m, sc_m, ga_m, sh_f, sc_f, ga_f = _mod_rows(mod, l)
        dyf, d_ga_f = _residual_bwd(f"res_ffn_b{l}", dx, s["yf"], ga_f)
        da, db = _ffn_down_bwd(f"ffn_down_dx{l}", dyf, W["ffn_d"][l], s["fa"], s["fb"], 0)
        gW["ffn_d"][l] = _ffn_down_dw(f"ffn_down_dw{l}", s["sw"], dyf)
        gW["ffn_g"][l], gW["ffn_u"][l] = _ffn_up_dw(f"ffn_up_dw{l}", s["h2"], da, db)
        dh2 = _ffn_up_dx(f"ffn_up_dx{l}", da, db, W["ffn_g"][l], W["ffn_u"][l], 0)
        dx, d_sh_f, d_sc_f, gP["norm_ffn_g"][l] = _norm_mod_bwd(f"norm_ffn_b{l}", dh2, s["x1"], dx,
                                                                 P["norm_ffn_g"][l:l + 1], sc_f)
        dy, d_ga_m = _residual_bwd(f"res_mix_b{l}", dx, s["y"], ga_m)
        if l % 2 == 0:
            don = _mm(f"gdn_out_dx{j}", dy, W["gdn_out"][j], "nt")
            gW["gdn_out"][j] = _mm(f"gdn_out_dw{j}", s["on"], dy, "tn", out_dtype=BF16)
            do, dgate, gP["gdn_ng"][j] = _gdn_gated_norm_bwd(f"gdn_gnorm_b{j}", don, s["o"], s["proj"], P["gdn_ng"][j])
            dq, dk, dv, dg_h, db_h = _gdn_chunk_bwd(f"gdn_chunk_b{j}", s["qkv"], s["g"], s["beta"], s["states"], do)
            dab_, gP["gdn_alog"][j], gP["gdn_dtb"][j] = _gdn_gates_bwd(f"gdn_gates_b{j}", s["proj"], dg_h, db_h,
                                                                        P["gdn_alog"][j], P["gdn_dtb"][j])
            dpre, gP["gdn_cw"][j] = _gdn_conv_bwd(f"gdn_conv_b{j}", s["proj"], P["gdn_cw"][j],
                                                  jnp.concatenate([dq, dk, dv], axis=1))
            dproj = jnp.concatenate([dpre, dgate, dab_], axis=1)
            gW["gdn_in"][j] = _mm(f"gdn_in_dw{j}", s["h"], dproj, "tn", out_dtype=BF16)
            dh = _mm(f"gdn_in_dx{j}", dproj, W["gdn_in"][j], "nt")
        else:
            do = _mm(f"mla_out_dx{j}", dy, W["mla_out"][j], "nt", out_dtype=BF16)
            gW["mla_out"][j] = _mm(f"mla_out_dw{j}", s["o"], dy, "tn", out_dtype=BF16)
            dqn, dqr, dkn, dv, dkr = _mla_attn_bwd(f"mla_attn_b{j}", s["qn"], s["qr"], s["kv"], s["kr"], do)
            dq = _mla_q_bwd(f"mla_q_b{j}", dqn, dqr, cos, sin)
            dkv = jnp.concatenate([dkn, dv], axis=1)
            gW["mla_uq"][j] = _mm(f"mla_uq_dw{j}", s["cqn"], dq, "tn", out_dtype=BF16)
            dcqn = _mm(f"mla_uq_dx{j}", dq, W["mla_uq"][j], "nt")
            gW["mla_ukv"][j] = _mm(f"mla_ukv_dw{j}", s["ckvn"], dkv, "tn", out_dtype=BF16)
            dckvn = _mm(f"mla_ukv_dx{j}", dkv, W["mla_ukv"][j], "nt")
            dproj, gP["mla_qg"][j], gP["mla_kvg"][j] = _mla_pre_bwd(f"mla_pre_b{j}", s["proj"], dcqn, dckvn, dkr, cos, sin,
                                                                     P["mla_qg"][j], P["mla_kvg"][j])
            gW["mla_in"][j] = _mm(f"mla_in_dw{j}", s["h"], dproj, "tn", out_dtype=BF16)
            dh = _mm(f"mla_in_dx{j}", dproj, W["mla_in"][j], "nt")
        dx, d_sh_m, d_sc_m, gP["norm_mix_g"][l] = _norm_mod_bwd(f"norm_mix_b{l}", dh, s["x0"], dx,
                                                                 P["norm_mix_g"][l:l + 1], sc_m)
        dmod[l] = jnp.concatenate([d_sh_m, d_sc_m, d_ga_m, d_sh_f, d_sc_f, d_ga_f], axis=1)
    return dx, jnp.concatenate(dmod, axis=0), gW, gP


def _pad_cols(a, width):
    return jnp.pad(a, ((0, 0), (0, width - a.shape[1])))


def _gdn_in_to_kernel(w):
    m = GDN_QKV + NH * HD
    return jnp.concatenate([w[:, :m], _pad_cols(w[:, m:m + NH], HD), _pad_cols(w[:, m + NH:], HD)], axis=1)


def _gdn_in_from_kernel(g):
    m = GDN_QKV + NH * HD
    return jnp.concatenate([g[:, :m], g[:, m:m + NH], g[:, m + HD:m + HD + NH]], axis=1)


def _mla_uq_to_kernel(w):
    w3 = w.reshape(Q_RANK, NH, HD + ROPE)
    rope = jnp.pad(w3[:, :, HD:], ((0, 0), (0, 0), (0, HD - ROPE)))
    return jnp.concatenate([w3[:, :, :HD].reshape(Q_RANK, NH * HD), rope.reshape(Q_RANK, NH * HD)], axis=1)


def _mla_uq_from_kernel(g):
    gn = g[:, :NH * HD].reshape(Q_RANK, NH, HD)
    gr = g[:, NH * HD:].reshape(Q_RANK, NH, HD)[:, :, :ROPE]
    return jnp.concatenate([gn, gr], axis=2).reshape(Q_RANK, NH * (HD + ROPE))


def _mla_ukv_to_kernel(w):
    w3 = w.reshape(KV_RANK, NH, 2 * HD)
    return jnp.concatenate([w3[:, :, :HD].reshape(KV_RANK, NH * HD), w3[:, :, HD:].reshape(KV_RANK, NH * HD)], axis=1)


def _mla_ukv_from_kernel(g):
    gk = g[:, :NH * HD].reshape(KV_RANK, NH, HD)
    gv = g[:, NH * HD:].reshape(KV_RANK, NH, HD)
    return jnp.concatenate([gk, gv], axis=2).reshape(KV_RANK, NH * 2 * HD)


def _cols(t):
    return jnp.moveaxis(t, 0, 1).reshape(t.shape[1], -1)


def _uncols(g):
    return jnp.moveaxis(g.reshape(g.shape[0], 4, -1), 1, 0)


def _rows(t):
    return t.reshape(-1, t.shape[2])


def _unrows(g):
    return g.reshape(4, -1, g.shape[1])


def _layer_weights(layer):
    mixer = ("gdn_w_in", "gdn_w_out") if layer % 2 == 0 else ("mla_w_in", "mla_w_uq", "mla_w_ukv", "mla_w_out")
    return [(n, layer // 2) for n in mixer] + [(n, layer) for n in ("ffn_w_gate", "ffn_w_up", "ffn_w_down")]


def _weights_to_kernel(got):
    gdn, mla = [got[0], got[2]], [got[1], got[3]]
    return dict(
        gdn_in=[_gdn_in_to_kernel(_cols(g["gdn_w_in"])) for g in gdn],
        gdn_out=[_rows(g["gdn_w_out"]) for g in gdn],
        mla_in=[_pad_cols(_rows(g["mla_w_in"]), MLA_INK) for g in mla],
        mla_uq=[_mla_uq_to_kernel(_cols(g["mla_w_uq"])) for g in mla],
        mla_ukv=[_mla_ukv_to_kernel(_cols(g["mla_w_ukv"])) for g in mla],
        mla_out=[_rows(g["mla_w_out"]) for g in mla],
        ffn_g=[g["ffn_w_gate"] for g in got], ffn_u=[g["ffn_w_up"] for g in got], ffn_d=[g["ffn_w_down"] for g in got],
    )


def _grads_to_chips(gW):
    cat = lambda parts: jnp.concatenate(parts, axis=1)
    return dict(
        gdn_w_in=cat([_uncols(_gdn_in_from_kernel(g)) for g in gW["gdn_in"]]),
        gdn_w_out=cat([_unrows(g) for g in gW["gdn_out"]]),
        mla_w_in=cat([_unrows(g[:, :Q_RANK + KV_RANK + ROPE]) for g in gW["mla_in"]]),
        mla_w_uq=cat([_uncols(_mla_uq_from_kernel(g)) for g in gW["mla_uq"]]),
        mla_w_ukv=cat([_uncols(_mla_ukv_from_kernel(g)) for g in gW["mla_ukv"]]),
        mla_w_out=cat([_unrows(g) for g in gW["mla_out"]]),
        ffn_w_gate=cat(gW["ffn_g"]), ffn_w_up=cat(gW["ffn_u"]), ffn_w_down=cat(gW["ffn_d"]),
    )


def _small_to_kernel(norm_mix_g, norm_ffn_g, final_norm_g, gdn_conv_w, gdn_a_log, gdn_dt_bias, gdn_norm_g, q_norm_g, kv_norm_g):
    return dict(
        norm_mix_g=norm_mix_g, norm_ffn_g=norm_ffn_g, final_g=final_norm_g.reshape(1, D),
        gdn_cw=[jnp.transpose(gdn_conv_w[j]) for j in range(2)],
        gdn_alog=[_pad_cols(gdn_a_log[j:j + 1], HD) for j in range(2)],
        gdn_dtb=[_pad_cols(gdn_dt_bias[j:j + 1], HD) for j in range(2)],
        gdn_ng=[gdn_norm_g[j:j + 1] for j in range(2)],
        mla_qg=[q_norm_g[j:j + 1] for j in range(2)],
        mla_kvg=[kv_norm_g[j:j + 1] for j in range(2)],
    )


_CHIP_FLIPS = ((1, 0), (0, 1), (1, 1))
_ANY = pl.BlockSpec(memory_space=pl.ANY)


def _me():
    return lax.axis_index("x"), lax.axis_index("y"), lax.axis_index("c")


def _chip_peer(dx, dy):
    x, y, c = _me()
    return ((1 - x) if dx else x, (1 - y) if dy else y, c)


def _rcopy(src, dst, send_sem, recv_sem, to):
    return pltpu.make_async_remote_copy(src_ref=src, dst_ref=dst, send_sem=send_sem, recv_sem=recv_sem,
                                        device_id=to, device_id_type=MESH)


def _allgather4(name, a, halves=False):
    R, C = a.shape
    rh = R // 2 if halves else R

    def body(a_ref, out_ref, send_sems, recv_sems, local_sem):
        x, y, c = _me()
        me = 2 * x + y
        src = a_ref.at[pl.ds(c * rh, rh)] if halves else a_ref
        local = pltpu.make_async_copy(src, out_ref.at[me], local_sem)
        local.start()
        sends = []
        for k, (dx, dy) in enumerate(_CHIP_FLIPS):
            cp = _rcopy(src, out_ref.at[me], send_sems.at[k], recv_sems.at[k], _chip_peer(dx, dy))
            cp.start()
            sends.append(cp)
        for k, (dx, dy) in enumerate(_CHIP_FLIPS):
            px, py, _ = _chip_peer(dx, dy)
            _rcopy(src, out_ref.at[2 * px + py], send_sems.at[k], recv_sems.at[k], _chip_peer(dx, dy)).wait_recv()
        for cp in sends:
            cp.wait_send()
        local.wait()

    return pl.pallas_call(
        body, name=name, in_specs=[_ANY], out_specs=_ANY, out_shape=jax.ShapeDtypeStruct((4, rh, C), a.dtype),
        scratch_shapes=[pltpu.SemaphoreType.DMA((3,)), pltpu.SemaphoreType.DMA((3,)), pltpu.SemaphoreType.DMA(())])(a)


_NCH = 4


def _dma_sems(*counts):
    return [pltpu.SemaphoreType.DMA((n,)) for n in counts]


def _slot_tile(rows):
    tr = 256
    while rows % tr:
        tr //= 2
    return tr


def _cast_into_slot(name, a, chip, row0, rows):
    C = a.shape[1]
    tr = _slot_tile(rows)
    assert row0 % tr == 0
    first = row0 // tr

    def body(c_ref, a_ref, o_ref):
        o_ref[0] = a_ref[...].astype(o_ref.dtype)

    grid_spec = pltpu.PrefetchScalarGridSpec(
        num_scalar_prefetch=1, grid=(rows // tr,), in_specs=[pl.BlockSpec((tr, C), lambda i, c_ref: (first + i, 0))],
        out_specs=pl.BlockSpec((1, tr, C), lambda i, c_ref: (c_ref[0], i, 0)))
    return pl.pallas_call(body, name=name, grid_spec=grid_spec, out_shape=jax.ShapeDtypeStruct((4, rows, C), BF16),
                          compiler_params=_params(1))(chip, a)


def _own_slot(name, p, chip):
    _, h, C = p.shape
    tr = _slot_tile(h)

    def body(c_ref, p_ref, o_ref):
        o_ref[...] = p_ref[...]

    spec = pl.BlockSpec((1, tr, C), lambda i, c_ref: (c_ref[0], i, 0))
    grid_spec = pltpu.PrefetchScalarGridSpec(num_scalar_prefetch=1, grid=(h // tr,), in_specs=[spec], out_specs=spec)
    return pl.pallas_call(body, name=name, grid_spec=grid_spec, out_shape=jax.ShapeDtypeStruct(p.shape, p.dtype),
                          compiler_params=_params(1))(chip, p)


def _gather_weights(name, bufs):
    n = len(bufs)

    def body(*refs):
        _gather_exchange(refs[n:2 * n], *refs[2 * n:])

    return pl.pallas_call(
        body, name=name, in_specs=[_ANY] * n, out_specs=[_ANY] * n,
        out_shape=[jax.ShapeDtypeStruct(s.shape, s.dtype) for s in bufs],
        input_output_aliases={t: t for t in range(n)},
        scratch_shapes=_dma_sems(3 * n, 3 * n, 3 * n, 3 * n))(*bufs)


def _chunks(rows, align):
    for nch in (_NCH, 2):
        if rows % (nch * align) == 0:
            return nch
    return 1


def _gather_exchange(out, ici_s, ici_r, d2d_s, d2d_r):
    n = len(out)
    x, y, c = _me()
    me = 2 * x + y
    sib = (x, y, 1 - c)
    peers = [_chip_peer(dx, dy) for dx, dy in _CHIP_FLIPS]
    for t in range(n):
        h = out[t].shape[1] // 2
        nch = _chunks(h, 16)
        ch = h // nch
        for k, peer in enumerate(peers):
            for i in range(nch):
                blk = out[t].at[me, pl.ds(c * h + i * ch, ch)]
                _rcopy(blk, blk, ici_s.at[3 * t + k], ici_r.at[3 * t + k], peer).start()
    for t in range(n):
        h = out[t].shape[1] // 2
        nch = _chunks(h, 16)
        ch = h // nch
        for k, peer in enumerate(peers):
            pchip = 2 * peer[0] + peer[1]
            got = out[t].at[pchip, pl.ds(c * h, h)]
            _rcopy(got, got, ici_s.at[3 * t + k], ici_r.at[3 * t + k], peer).wait_recv()
            for i in range(nch):
                blk = out[t].at[pchip, pl.ds(c * h + i * ch, ch)]
                _rcopy(blk, blk, d2d_s.at[3 * t + k], d2d_r.at[3 * t + k], sib).start()
    for t in range(n):
        h = out[t].shape[1] // 2
        for k, peer in enumerate(peers):
            pchip = 2 * peer[0] + peer[1]
            other = out[t].at[pchip, pl.ds((1 - c) * h, h)]
            _rcopy(other, other, d2d_s.at[3 * t + k], d2d_r.at[3 * t + k], sib).wait_recv()
            _rcopy(other, other, ici_s.at[3 * t + k], ici_r.at[3 * t + k], peer).wait_send()
            _rcopy(other, other, d2d_s.at[3 * t + k], d2d_r.at[3 * t + k], sib).wait_send()


def _gather_weights_async(name, collective_id, bufs):
    n = len(bufs)
    refs = [jax.new_ref(b, memory_space=pltpu.MemorySpace.HBM) for b in bufs]

    @pl.kernel(mesh=plsc.ScalarSubcoreMesh(axis_name="sequencer", num_cores=1), name=name,
               scratch_types=tuple(_dma_sems(3 * n, 3 * n, 3 * n, 3 * n)),
               compiler_params=pltpu.CompilerParams(collective_id=collective_id))
    def launch(ici_s, ici_r, d2d_s, d2d_r):
        x, y, c = _me()
        barrier = pltpu.get_barrier_semaphore()
        for peer in [_chip_peer(dx, dy) for dx, dy in _CHIP_FLIPS] + [(x, y, 1 - c)]:
            pl.semaphore_signal(barrier, inc=1, device_id=peer, device_id_type=MESH)
        pl.semaphore_wait(barrier, 4)
        _gather_exchange(refs, ici_s, ici_r, d2d_s, d2d_r)

    launch()
    return [r[...] for r in refs]


def _rs_split(name, grads):
    n = len(grads)

    def body(*refs):
        g, out = refs[:n], refs[n:2 * n]
        send, recv = refs[2 * n:]
        x, y, c = _me()
        sib = (x, y, 1 - c)
        for t in range(n):
            h = g[t].shape[1] // 2
            for d in range(4):
                _rcopy(g[t].at[d, pl.ds((1 - c) * h, h)], out[t].at[d], send.at[t], recv.at[t], sib).start()
        for t in range(n):
            _rcopy(out[t], out[t], send.at[t], recv.at[t], sib).wait()

    return pl.pallas_call(
        body, name=name, in_specs=[_ANY] * n, out_specs=[_ANY] * n,
        out_shape=[jax.ShapeDtypeStruct((4, s.shape[1] // 2, s.shape[2]), s.dtype) for s in grads],
        scratch_shapes=_dma_sems(n, n))(*grads)


def _pair_add(name, g, theirs, core):
    _, R, C = g.shape
    h = R // 2
    tr = 256
    while h % tr:
        tr //= 2
    nb = h // tr

    def body(c_ref, g_ref, t_ref, o_ref):
        o_ref[...] = (g_ref[...].astype(F32) + t_ref[...].astype(F32)).astype(o_ref.dtype)

    spec = pl.BlockSpec((1, tr, C), lambda d, i, c_ref: (d, i, 0))
    grid_spec = pltpu.PrefetchScalarGridSpec(
        num_scalar_prefetch=1, grid=(4, nb),
        in_specs=[pl.BlockSpec((1, tr, C), lambda d, i, c_ref: (d, c_ref[0] * nb + i, 0)), spec], out_specs=spec)
    return pl.pallas_call(body, name=name, grid_spec=grid_spec, out_shape=jax.ShapeDtypeStruct((4, h, C), BF16),
                          compiler_params=_params(2))(core, g, theirs)


def _rs_alltoall(name, parts, bufs):
    n = len(parts)

    def body(*refs):
        p, out = refs[:n], refs[2 * n:3 * n]
        send, recv = refs[3 * n:]
        x, y, c = _me()
        me = 2 * x + y
        peers = [_chip_peer(dx, dy) for dx, dy in _CHIP_FLIPS]
        for t in range(n):
            ch = p[t].shape[1] // _NCH
            for k, peer in enumerate(peers):
                pchip = 2 * peer[0] + peer[1]
                for i in range(_NCH):
                    rows = pl.ds(i * ch, ch)
                    _rcopy(p[t].at[pchip, rows], out[t].at[me, rows], send.at[3 * t + k], recv.at[3 * t + k], peer).start()
        for t in range(n):
            for k, peer in enumerate(peers):
                pchip = 2 * peer[0] + peer[1]
                _rcopy(out[t].at[pchip], out[t].at[pchip], send.at[3 * t + k], recv.at[3 * t + k], peer).wait()

    return pl.pallas_call(
        body, name=name, in_specs=[_ANY] * (2 * n), out_specs=[_ANY] * n,
        out_shape=[jax.ShapeDtypeStruct(s.shape, s.dtype) for s in bufs],
        input_output_aliases={n + t: t for t in range(n)},
        scratch_shapes=_dma_sems(3 * n, 3 * n))(*parts, *bufs)


def _rs_swap(name, halves):
    n = len(halves)

    def body(*refs):
        a, out = refs[:n], refs[n:2 * n]
        send, recv = refs[2 * n:]
        x, y, c = _me()
        sib = (x, y, 1 - c)
        for t in range(n):
            ch = a[t].shape[0] // _NCH
            for i in range(_NCH):
                rows = pl.ds(i * ch, ch)
                _rcopy(a[t].at[rows], out[t].at[rows], send.at[t], recv.at[t], sib).start()
        for t in range(n):
            _rcopy(a[t], out[t], send.at[t], recv.at[t], sib).wait()

    return pl.pallas_call(
        body, name=name, in_specs=[_ANY] * n, out_specs=[_ANY] * n,
        out_shape=[jax.ShapeDtypeStruct(s.shape, s.dtype) for s in halves],
        scratch_shapes=_dma_sems(n, n))(*halves)


def _sibling_merge(name, a):
    P_, rh, C = a.shape

    def body(a_ref, out_ref, send_sem, recv_sem, local_sem):
        x, y, c = _me()
        local = pltpu.make_async_copy(a_ref, out_ref.at[:, pl.ds(c * rh, rh)], local_sem)
        local.start()
        cp = _rcopy(a_ref, out_ref.at[:, pl.ds(c * rh, rh)], send_sem, recv_sem, (x, y, 1 - c))
        cp.start()
        cp.wait_send()
        _rcopy(a_ref, out_ref.at[:, pl.ds((1 - c) * rh, rh)], send_sem, recv_sem, (x, y, 1 - c)).wait_recv()
        local.wait()

    return pl.pallas_call(
        body, name=name, in_specs=[_ANY], out_specs=_ANY, out_shape=jax.ShapeDtypeStruct((P_, 2 * rh, C), a.dtype),
        scratch_shapes=[pltpu.SemaphoreType.DMA(()), pltpu.SemaphoreType.DMA(()), pltpu.SemaphoreType.DMA(())])(a)


def _allgather8(name, a):
    g4 = _allgather4(name + "_chips", a)
    both = _sibling_merge(name + "_cores", g4.reshape(1, 4 * a.shape[0], a.shape[1]))
    return jnp.transpose(both.reshape(2, 4, *a.shape), (1, 0, 2, 3)).reshape(8, *a.shape)


def _sum_slots(name, a, out_dtype):
    def fn(a):
        acc = a[0].astype(F32)
        for k in range(1, a.shape[0]):
            acc = acc + a[k].astype(F32)
        return acc
    return _rowwise(name, fn, [a], [], [(a.shape[2], out_dtype)])[0]


def _adamw_math(w, g, m, v):
    m = ADAM_B1 * m + (1.0 - ADAM_B1) * g
    v = ADAM_B2 * v + (1.0 - ADAM_B2) * (g * g)
    m_hat = m / (1.0 - ADAM_B1 ** ADAM_STEP)
    v_hat = v / (1.0 - ADAM_B2 ** ADAM_STEP)
    return -ADAM_LR * (m_hat / (jnp.sqrt(v_hat) + ADAM_EPS) + ADAM_WD * w), m, v


def _adamw_halves(name, w, m, v, mine, theirs):
    shape = w.shape
    w2, m2, v2 = [t.reshape(-1, shape[-1]) for t in (w, m, v)]
    R, C = w2.shape
    h = R // 2
    tr = _slot_tile(h)
    nb = h // tr

    def body(w_ref, m_ref, v_ref, a_ref, b_ref, g_ref, d_ref, nm_ref, nv_ref):
        g = jnp.where(pl.program_id(0) == lax.axis_index("c"), a_ref[...], b_ref[...])
        g_ref[...] = g
        d_ref[...], nm_ref[...], nv_ref[...] = _adamw_math(w_ref[...], g, m_ref[...], v_ref[...])

    full = pl.BlockSpec((tr, C), lambda s, i: (s * nb + i, 0))
    half = pl.BlockSpec((tr, C), lambda s, i: (i, 0))
    outs = pl.pallas_call(body, name=name, grid=(2, nb), in_specs=[full, full, full, half, half], out_specs=[full] * 4,
                          out_shape=[jax.ShapeDtypeStruct((R, C), F32)] * 4, compiler_params=_params(2))(w2, m2, v2, mine, theirs)
    return [o.reshape(shape) for o in outs]


def _adamw(name, w, g, m, v):
    shape = w.shape
    two_d = (-1, shape[-1]) if w.ndim > 1 else (1, -1)
    w2, g2, m2, v2 = [t.reshape(two_d) for t in (w, g, m, v)]
    rows = w2.shape[0]
    tr = rows
    for cand in (256, 128, 64, 32, 16, 8):
        if rows % cand == 0:
            tr = cand
            break

    c = w2.shape[1]
    outs = _rowwise(name, _adamw_math, [w2, g2, m2, v2], [], [(c, F32)] * 3, tr=tr)
    return [o.reshape(shape) for o in outs]


_WEIGHT_ORDER = ("ada_w", "ada_b", "norm_mix_g", "norm_ffn_g", "gdn_w_in", "gdn_conv_w", "gdn_a_log", "gdn_dt_bias",
                 "gdn_norm_g", "gdn_w_out", "mla_w_in", "mla_q_norm_g", "mla_kv_norm_g", "mla_w_uq", "mla_w_ukv",
                 "mla_w_out", "ffn_w_gate", "ffn_w_up", "ffn_w_down", "final_norm_g")
_BIG = (("gdn_w_in", 2), ("gdn_w_out", 1), ("mla_w_in", 1), ("mla_w_uq", 2), ("mla_w_ukv", 2), ("mla_w_out", 1),
        ("ffn_w_gate", 2), ("ffn_w_up", 2), ("ffn_w_down", 1))
_SMALL_SHARDED = (("gdn_conv_w", 1), ("mla_q_norm_g", 1), ("mla_kv_norm_g", 1))


def _size(shape):
    n = 1
    for s in shape:
        n *= s
    return n


def _pack_rows_each(tensors):
    parts, offs, off = [], [], 0
    for t in tensors:
        flat = t.reshape(-1).astype(F32)
        rows = -(-flat.shape[0] // PACK_W)
        parts.append(jnp.pad(flat, (0, rows * PACK_W - flat.shape[0])).reshape(rows, PACK_W))
        offs.append(off)
        off += rows
    total = -(-off // 16) * 16
    pack = jnp.pad(parts[0], ((offs[0], total - offs[0] - parts[0].shape[0]), (0, 0)))
    for p, o in zip(parts[1:], offs[1:]):
        pack = pack + jnp.pad(p, ((o, total - o - p.shape[0]), (0, 0)))
    return pack, offs


def _unpack_rows_each(pack, shapes):
    lead = pack.shape[:-2]
    out, off = [], 0
    for shp in shapes:
        n = _size(shp)
        rows = -(-n // PACK_W)
        out.append(pack[..., off:off + rows, :].reshape(*lead, -1)[..., :n].reshape(*lead, *shp))
        off += rows
    return out


def _merge_chips(stacked, axis):
    moved = jnp.moveaxis(stacked, 0, axis)
    shp = list(moved.shape)
    return moved.reshape(shp[:axis] + [shp[axis] * shp[axis + 1]] + shp[axis + 2:])


def _my_shard(full, axis, chip):
    n = full.shape[axis] // 4
    return lax.dynamic_slice_in_dim(full, chip * n, n, axis)


def kernel(x, c, positions, ada_w, ada_b, norm_mix_g, norm_ffn_g, gdn_w_in, gdn_conv_w, gdn_a_log, gdn_dt_bias, gdn_norm_g, gdn_w_out, mla_w_in, mla_q_norm_g, mla_kv_norm_g, mla_w_uq, mla_w_ukv, mla_w_out, ffn_w_gate, ffn_w_up, ffn_w_down, final_norm_g, loss_target, m_ada_w, m_ada_b, m_norm_mix_g, m_norm_ffn_g, m_gdn_w_in, m_gdn_conv_w, m_gdn_a_log, m_gdn_dt_bias, m_gdn_norm_g, m_gdn_w_out, m_mla_w_in, m_mla_q_norm_g, m_mla_kv_norm_g, m_mla_w_uq, m_mla_w_ukv, m_mla_w_out, m_ffn_w_gate, m_ffn_w_up, m_ffn_w_down, m_final_norm_g, v_ada_w, v_ada_b, v_norm_mix_g, v_norm_ffn_g, v_gdn_w_in, v_gdn_conv_w, v_gdn_a_log, v_gdn_dt_bias, v_gdn_norm_g, v_gdn_w_out, v_mla_w_in, v_mla_q_norm_g, v_mla_kv_norm_g, v_mla_w_uq, v_mla_w_ukv, v_mla_w_out, v_ffn_w_gate, v_ffn_w_up, v_ffn_w_down, v_final_norm_g):
    w = dict(ada_w=ada_w, ada_b=ada_b, norm_mix_g=norm_mix_g, norm_ffn_g=norm_ffn_g, gdn_w_in=gdn_w_in, gdn_conv_w=gdn_conv_w,
             gdn_a_log=gdn_a_log, gdn_dt_bias=gdn_dt_bias, gdn_norm_g=gdn_norm_g, gdn_w_out=gdn_w_out, mla_w_in=mla_w_in,
             mla_q_norm_g=mla_q_norm_g, mla_kv_norm_g=mla_kv_norm_g, mla_w_uq=mla_w_uq, mla_w_ukv=mla_w_ukv,
             mla_w_out=mla_w_out, ffn_w_gate=ffn_w_gate, ffn_w_up=ffn_w_up, ffn_w_down=ffn_w_down, final_norm_g=final_norm_g)
    m = dict(ada_w=m_ada_w, ada_b=m_ada_b, norm_mix_g=m_norm_mix_g, norm_ffn_g=m_norm_ffn_g, gdn_w_in=m_gdn_w_in,
             gdn_conv_w=m_gdn_conv_w, gdn_a_log=m_gdn_a_log, gdn_dt_bias=m_gdn_dt_bias, gdn_norm_g=m_gdn_norm_g,
             gdn_w_out=m_gdn_w_out, mla_w_in=m_mla_w_in, mla_q_norm_g=m_mla_q_norm_g, mla_kv_norm_g=m_mla_kv_norm_g,
             mla_w_uq=m_mla_w_uq, mla_w_ukv=m_mla_w_ukv, mla_w_out=m_mla_w_out, ffn_w_gate=m_ffn_w_gate,
             ffn_w_up=m_ffn_w_up, ffn_w_down=m_ffn_w_down, final_norm_g=m_final_norm_g)
    v = dict(ada_w=v_ada_w, ada_b=v_ada_b, norm_mix_g=v_norm_mix_g, norm_ffn_g=v_norm_ffn_g, gdn_w_in=v_gdn_w_in,
             gdn_conv_w=v_gdn_conv_w, gdn_a_log=v_gdn_a_log, gdn_dt_bias=v_gdn_dt_bias, gdn_norm_g=v_gdn_norm_g,
             gdn_w_out=v_gdn_w_out, mla_w_in=v_mla_w_in, mla_q_norm_g=v_mla_q_norm_g, mla_kv_norm_g=v_mla_kv_norm_g,
             mla_w_uq=v_mla_w_uq, mla_w_ukv=v_mla_w_ukv, mla_w_out=v_mla_w_out, ffn_w_gate=v_ffn_w_gate,
             ffn_w_up=v_ffn_w_up, ffn_w_down=v_ffn_w_down, final_norm_g=v_final_norm_g)
    T = x.shape[1]
    ix, iy, ic = _me()
    chip = 2 * ix + iy
    seq = 2 * chip + ic
    n_dev = 8

    small_shapes = [w[n].shape for n, _ in _SMALL_SHARDED] + [c.shape]
    pack0, _ = _pack_rows_each([w[n] for n, _ in _SMALL_SHARDED] + [c])
    got0 = _unpack_rows_each(_allgather8("gather_small", pack0), small_shapes)
    small_full = {n: _merge_chips(g[0::2], ax) for (n, ax), g in zip(_SMALL_SHARDED, got0)}
    c_all = got0[-1].reshape(n_dev, D)

    big = [n for n, _ in _BIG]
    chip_arr = chip.astype(jnp.int32).reshape(1)
    got = []
    for l in range(DEPTH):
        names = _layer_weights(l)
        bufs = [_cast_into_slot(f"to_bf16_{n}{l}", w[n].reshape(-1, w[n].shape[-1]), chip_arr, j * w[n].shape[1], w[n].shape[1])
                for n, j in names]
        filled = _gather_weights("gather_weights0", bufs) if l == 0 else _gather_weights_async(f"gather_weights{l}", l, bufs)
        got.append({n: b for (n, _), b in zip(names, filled)})
    W = _weights_to_kernel(got)
    P = _small_to_kernel(norm_mix_g, norm_ffn_g, final_norm_g, small_full["gdn_conv_w"], gdn_a_log, gdn_dt_bias,
                         gdn_norm_g, small_full["mla_q_norm_g"], small_full["mla_kv_norm_g"])

    c16 = jnp.pad(c_all, ((0, 16 - n_dev), (0, 0)))
    ca = _rowwise("cond_silu", lambda t: t * _sig(t), [c16], [], [(D, BF16)])[0]
    n_ada = ada_w.shape[2]
    mods = jnp.concatenate([_mm(f"ada_fwd{l}", ca, ada_w[l], "nn") for l in range(DEPTH)], axis=0)
    mods_all = _allgather4("gather_mod", mods).reshape(4, DEPTH, 16, n_ada)
    mod_mm = jnp.transpose(lax.dynamic_index_in_dim(mods_all, seq, axis=2, keepdims=False), (1, 0, 2)).reshape(DEPTH, 4 * n_ada)
    mod = _rowwise("mod_bias", lambda a, b: a + b, [mod_mm, ada_b], [], [(4 * n_ada, F32)])[0]

    dx, dmod, gW, gP = _local_step(x.reshape(T, D), loss_target.reshape(T, D), positions.reshape(T, 1), mod, W, P)

    partials = [dmod, jnp.concatenate(gP["norm_mix_g"]), jnp.concatenate(gP["norm_ffn_g"]), gP["final_g"],
                jnp.stack([jnp.transpose(g) for g in gP["gdn_cw"]]), jnp.concatenate(gP["gdn_alog"])[:, :NH],
                jnp.concatenate(gP["gdn_dtb"])[:, :NH], jnp.concatenate(gP["gdn_ng"]), jnp.concatenate(gP["mla_qg"]),
                jnp.concatenate(gP["mla_kvg"]), gP["loss"][:, :1]]
    part_shapes = [p.shape for p in partials]
    ppack, _ = _pack_rows_each(partials)
    pall = _allgather8("gather_partials", ppack)
    psum = _sum_slots("sum_partials", pall, F32)
    (g_ada_b, g_norm_mix, g_norm_ffn, g_final, g_conv_full, g_alog, g_dtb, g_gdn_ng, g_qg_full, g_kvg_full,
     loss_sum) = _unpack_rows_each(psum, part_shapes)
    dmod_all = _unpack_rows_each(pall, part_shapes[:1])[0]

    grads = dict(ada_b=g_ada_b, norm_mix_g=g_norm_mix, norm_ffn_g=g_norm_ffn, final_norm_g=g_final.reshape(D),
                 gdn_conv_w=_my_shard(g_conv_full, 1, chip), gdn_a_log=g_alog, gdn_dt_bias=g_dtb, gdn_norm_g=g_gdn_ng,
                 mla_q_norm_g=_my_shard(g_qg_full, 1, chip), mla_kv_norm_g=_my_shard(g_kvg_full, 1, chip))

    ca_t = jnp.zeros((D, LANES), BF16).at[:, :16].set(jnp.transpose(ca))
    dm_mine = lax.dynamic_slice_in_dim(dmod_all, chip * n_ada, n_ada, axis=2)
    grads["ada_w"] = jnp.stack([
        _mm(f"ada_bwd{l}", ca_t, jnp.pad(dm_mine[:, l], ((0, LANES - n_dev), (0, 0))), "nn") for l in range(DEPTH)])

    gchips = _grads_to_chips(gW)
    glist = [gchips[n] for n in big]
    theirs = _rs_split("grads_cores", glist)
    core = ic.astype(jnp.int32).reshape(1)
    pairs = [_pair_add("grads_pair_" + n, g, t, core) for n, g, t in zip(big, glist, theirs)]
    own = [_own_slot("grads_own_" + n, p, chip_arr) for n, p in zip(big, pairs)]
    swapped = _rs_alltoall("grads_chips", pairs, own)
    halves = [_sum_slots("grads_sum_" + n, s, F32) for n, s in zip(big, swapped)]
    other_halves = _rs_swap("grads_swap", halves)

    delta, new_m, new_v = {}, {}, {}
    for n, mine, theirs in zip(big, halves, other_halves):
        grads[n], delta[n], new_m[n], new_v[n] = _adamw_halves("adamw_" + n, w[n], m[n], v[n], mine, theirs)
    delta["ada_w"], new_m["ada_w"], new_v["ada_w"] = _adamw("adamw_ada_w", ada_w, grads["ada_w"], m_ada_w, v_ada_w)
    small_names = [n for n in _WEIGHT_ORDER if n not in delta]
    small_shapes = [w[n].shape for n in small_names]
    packs = [_pack_rows_each([d[n] for n in small_names])[0] for d in (w, grads, m, v)]
    for d, pk in zip((delta, new_m, new_v), _adamw("adamw_small", *packs)):
        for n, t in zip(small_names, _unpack_rows_each(pk, small_shapes)):
            d[n] = t

    loss = loss_sum.reshape(())
    return (loss, dx.reshape(1, T, D), *[grads[n] for n in _WEIGHT_ORDER], *[delta[n] for n in _WEIGHT_ORDER],
            *[new_m[n] for n in _WEIGHT_ORDER], *[new_v[n] for n in _WEIGHT_ORDER])
```

```python
import functools

import jax
import jax.numpy as jnp
from jax import lax
from jax.experimental import pallas as pl
from jax.experimental.pallas import tpu as pltpu
from jax.experimental.pallas import tpu_sc as plsc

F32 = jnp.float32
BF16 = jnp.bfloat16
HI = lax.Precision.HIGHEST
MESH = pl.DeviceIdType.MESH

D = 1024
DEPTH = 4
N_MOD = 6
NH = 8
HD = 128
CHUNK = 64
_GDN_HB = 4
GDN_QKV = 3 * NH * HD
GDN_INK = GDN_QKV + NH * HD + 2 * HD
Q_RANK, KV_RANK, ROPE = 384, 256, 64
MLA_INK = Q_RANK + KV_RANK + HD
DFF = 2816
EPS = 1e-6
ATT_SCALE = (HD + ROPE) ** -0.5
ROPE_THETA = 10000.0
LANES = 128
PACK_W = 1024

ADAM_LR, ADAM_B1, ADAM_B2, ADAM_EPS, ADAM_WD, ADAM_STEP = 0.001, 0.9, 0.999, 1e-08, 0.01, 10


H3 = "bf16x3"
B1 = "bf16"
HS = H3


def _dot(a, b, mode="nn", prec=None):
    dn = {"nn": (((1,), (0,)), ((), ())), "nt": (((1,), (1,)), ((), ())), "tn": (((0,), (0,)), ((), ()))}[mode]
    if prec == B1:
        return _dot(a.astype(BF16), b.astype(BF16), mode)
    if prec == H3:
        ah, bh = a.astype(BF16), b.astype(BF16)
        al, bl = (a - ah.astype(F32)).astype(BF16), (b - bh.astype(F32)).astype(BF16)
        return _dot(ah, bh, mode) + (_dot(ah, bl, mode) + _dot(al, bh, mode))
    return lax.dot_general(a, b, dn, precision=prec, preferred_element_type=F32)


def _sig(x):
    return 1.0 / (1.0 + jnp.exp(-x))


def _pick(n, cap):
    if n <= cap:
        return n
    best = None
    for d in range(LANES, cap + 1, LANES):
        if n % d == 0:
            best = d
    assert best is not None, (n, cap)
    return best


def _params(n_grid):
    return pltpu.CompilerParams(dimension_semantics=("arbitrary",) * n_grid, vmem_limit_bytes=56 * 1024 * 1024)


def _rowwise(name, fn, rows, consts, outs, sums=(), tr=256):
    first = rows[0][0] if isinstance(rows[0], tuple) else rows[0]
    T = first.shape[-2]
    tr = min(tr, T)
    while T % tr:
        tr //= 2
    nr, nc, no, ns = len(rows), len(consts), len(outs), len(sums)

    def body(*refs):
        res = fn(*[r[...] for r in refs[:nr + nc]])
        if not isinstance(res, (tuple, list)):
            res = (res,)
        o_refs = refs[nr + nc:nr + nc + no]
        s_refs = refs[nr + nc + no:]
        for r, val in zip(o_refs, res[:no]):
            r[...] = val.astype(r.dtype)
        if ns:
            @pl.when(pl.program_id(0) == 0)
            def _():
                for r in s_refs:
                    r[...] = jnp.zeros_like(r)
            for r, val in zip(s_refs, res[no:]):
                r[...] += val

    in_specs, args = [], []
    for a in rows:
        if isinstance(a, tuple):
            arr, width, cb = a
            in_specs.append(pl.BlockSpec((tr, width), lambda i, cb=cb: (i, cb)))
            args.append(arr)
        elif a.ndim == 3:
            in_specs.append(pl.BlockSpec((a.shape[0], tr, a.shape[2]), lambda i: (0, i, 0)))
            args.append(a)
        else:
            in_specs.append(pl.BlockSpec((tr, a.shape[1]), lambda i: (i, 0)))
            args.append(a)
    for a in consts:
        in_specs.append(pl.BlockSpec(a.shape, lambda i, nd=a.ndim: (0,) * nd))
        args.append(a)
    out_specs = [pl.BlockSpec((tr, w), lambda i: (i, 0)) for w, _ in outs]
    out_specs += [pl.BlockSpec((1, w), lambda i: (0, 0)) for w in sums]
    out_shape = [jax.ShapeDtypeStruct((T, w), dt) for w, dt in outs]
    out_shape += [jax.ShapeDtypeStruct((1, w), F32) for w in sums]
    res = pl.pallas_call(body, name=name, grid=(T // tr,), in_specs=in_specs, out_specs=out_specs,
                         out_shape=out_shape, compiler_params=_params(1))(*args)
    return res


def _mm(name, a, b, mode, out_dtype=F32, tm=512, tn=1024):
    if mode == "tn":
        K, M = a.shape
    else:
        M, K = a.shape
    N = b.shape[0] if mode == "nt" else b.shape[1]
    tm, tn = _pick(M, tm), _pick(N, tn)

    def body(a_ref, b_ref, o_ref):
        o_ref[...] = _dot(a_ref[...].astype(BF16), b_ref[...].astype(BF16), mode).astype(o_ref.dtype)

    a_spec = pl.BlockSpec((K, tm), lambda i, j: (0, i)) if mode == "tn" else pl.BlockSpec((tm, K), lambda i, j: (i, 0))
    b_spec = pl.BlockSpec((tn, K), lambda i, j: (j, 0)) if mode == "nt" else pl.BlockSpec((K, tn), lambda i, j: (0, j))
    return pl.pallas_call(body, name=name, grid=(M // tm, N // tn), in_specs=[a_spec, b_spec],
                          out_specs=pl.BlockSpec((tm, tn), lambda i, j: (i, j)),
                          out_shape=jax.ShapeDtypeStruct((M, N), out_dtype), compiler_params=_params(2))(a, b)


def _rms(x, eps=EPS):
    return lax.rsqrt(jnp.mean(x * x, axis=-1, keepdims=True) + eps)


def _norm_mod_fwd(name, x, g, scale, shift):
    def fn(x, g, scale, shift):
        return x * _rms(x) * g * (1.0 + scale) + shift
    return _rowwise(name, fn, [x], [g, scale, shift], [(D, BF16)])[0]


def _norm_mod_bwd(name, dh, x, dx_res, g, scale):
    def fn(dh, x, dx_res, g, scale):
        r = _rms(x)
        xh = x * r
        dxh = dh * (g * (1.0 + scale))
        dx = r * (dxh - xh * jnp.mean(dxh * xh, axis=-1, keepdims=True))
        dhx = dh * xh
        return (dx_res + dx, jnp.sum(dh, axis=0, keepdims=True), jnp.sum(dhx * g, axis=0, keepdims=True),
                jnp.sum(dhx * (1.0 + scale), axis=0, keepdims=True))
    return _rowwise(name, fn, [dh, x, dx_res], [g, scale], [(D, F32)], sums=[D, D, D])


def _residual_fwd(name, x, y, gate):
    def fn(x, y, gate):
        return x + gate * y
    return _rowwise(name, fn, [x, y], [gate], [(D, F32)])[0]


def _residual_bwd(name, dx, y, gate):
    def fn(dx, y, gate):
        return dx * gate, jnp.sum(dx * y, axis=0, keepdims=True)
    return _rowwise(name, fn, [dx, y], [gate], [(D, BF16)], sums=[D])


def _loss_head(x, target, g):
    def fn(x, t, g):
        r = _rms(x)
        xh = x * r
        err = xh * g - t
        loss = 0.5 * jnp.sum(jnp.mean(err * err, axis=-1, keepdims=True), axis=0, keepdims=True)
        dy = err * (1.0 / D)
        dxh = dy * g
        dx = r * (dxh - xh * jnp.mean(dxh * xh, axis=-1, keepdims=True))
        return dx, jnp.broadcast_to(loss, (1, LANES)), jnp.sum(dy * xh, axis=0, keepdims=True)
    return _rowwise("loss_head", fn, [x, target], [g], [(D, F32)], sums=[LANES, D])


def _ffn_up(name, h, wg, wu, layer, tm=512):
    T, n = h.shape[0], wg.shape[2]
    tm = min(tm, T)

    def body(h_ref, wg_ref, wu_ref, a_ref, b_ref, s_ref):
        h = h_ref[...]
        a = _dot(h, wg_ref[0], "nn")
        b = _dot(h, wu_ref[0], "nn")
        a_ref[0] = a
        b_ref[0] = b
        s_ref[0] = (a * _sig(a) * b).astype(s_ref.dtype)

    wspec = pl.BlockSpec((1, D, n), lambda ch, i: (ch, layer, 0))
    ospec = pl.BlockSpec((1, tm, n), lambda ch, i: (ch, i, 0))
    return pl.pallas_call(
        body, name=name, grid=(4, T // tm), in_specs=[pl.BlockSpec((tm, D), lambda ch, i: (i, 0)), wspec, wspec],
        out_specs=[ospec, ospec, ospec],
        out_shape=[jax.ShapeDtypeStruct((4, T, n), F32)] * 2 + [jax.ShapeDtypeStruct((4, T, n), BF16)],
        compiler_params=_params(2))(h, wg, wu)


def _ffn_down(name, s, wd, layer, tm=512):
    _, T, n = s.shape
    tm = min(tm, T)

    def body(s_ref, w_ref, y_ref):
        @pl.when(pl.program_id(1) == 0)
        def _():
            y_ref[...] = jnp.zeros_like(y_ref)
        y_ref[...] += _dot(s_ref[0], w_ref[0], "nn")

    return pl.pallas_call(
        body, name=name, grid=(T // tm, 4),
        in_specs=[pl.BlockSpec((1, tm, n), lambda i, ch: (ch, i, 0)), pl.BlockSpec((1, n, D), lambda i, ch: (ch, layer, 0))],
        out_specs=pl.BlockSpec((tm, D), lambda i, ch: (i, 0)), out_shape=jax.ShapeDtypeStruct((T, D), F32),
        compiler_params=_params(2))(s, wd)


def _ffn_down_bwd(name, dy, wd, a, b, layer, tm=512):
    _, T, n = a.shape
    tm = min(tm, T)

    def body(dy_ref, w_ref, a_ref, b_ref, da_ref, db_ref):
        ds = _dot(dy_ref[...], w_ref[0], "nt")
        a, b = a_ref[0], b_ref[0]
        sg = _sig(a)
        da_ref[0] = (ds * b * (sg * (1.0 + a * (1.0 - sg)))).astype(da_ref.dtype)
        db_ref[0] = (ds * (a * sg)).astype(db_ref.dtype)

    bspec = pl.BlockSpec((1, tm, n), lambda ch, i: (ch, i, 0))
    return pl.pallas_call(
        body, name=name, grid=(4, T // tm),
        in_specs=[pl.BlockSpec((tm, D), lambda ch, i: (i, 0)), pl.BlockSpec((1, n, D), lambda ch, i: (ch, layer, 0)), bspec, bspec],
        out_specs=[bspec, bspec], out_shape=[jax.ShapeDtypeStruct((4, T, n), BF16)] * 2,
        compiler_params=_params(2))(dy, wd, a, b)


def _ffn_down_dw(name, s, dy):
    _, T, n = s.shape

    def body(s_ref, dy_ref, o_ref):
        o_ref[0] = _dot(s_ref[0], dy_ref[...], "tn").astype(o_ref.dtype)

    return pl.pallas_call(
        body, name=name, grid=(4,),
        in_specs=[pl.BlockSpec((1, T, n), lambda ch: (ch, 0, 0)), pl.BlockSpec((T, D), lambda ch: (0, 0))],
        out_specs=pl.BlockSpec((1, n, D), lambda ch: (ch, 0, 0)), out_shape=jax.ShapeDtypeStruct((4, n, D), BF16),
        compiler_params=_params(1))(s, dy)


def _ffn_up_dw(name, h, da, db, tm=512):
    _, T, n = da.shape

    def body(h_ref, da_ref, db_ref, dg_ref, du_ref):
        h = h_ref[...]
        dg_ref[0] = _dot(h, da_ref[0], "tn").astype(dg_ref.dtype)
        du_ref[0] = _dot(h, db_ref[0], "tn").astype(du_ref.dtype)

    dspec = pl.BlockSpec((1, T, n), lambda ch, j: (ch, 0, 0))
    ospec = pl.BlockSpec((1, tm, n), lambda ch, j: (ch, j, 0))
    return pl.pallas_call(
        body, name=name, grid=(4, D // tm), in_specs=[pl.BlockSpec((T, tm), lambda ch, j: (0, j)), dspec, dspec],
        out_specs=[ospec, ospec], out_shape=[jax.ShapeDtypeStruct((4, D, n), BF16)] * 2,
        compiler_params=_params(2))(h, da, db)


def _ffn_up_dx(name, da, db, wg, wu, layer, tm=512):
    _, T, n = da.shape
    tm = min(tm, T)

    def body(da_ref, db_ref, wg_ref, wu_ref, o_ref):
        @pl.when(pl.program_id(1) == 0)
        def _():
            o_ref[...] = jnp.zeros_like(o_ref)
        o_ref[...] += _dot(da_ref[0], wg_ref[0], "nt") + _dot(db_ref[0], wu_ref[0], "nt")

    dspec = pl.BlockSpec((1, tm, n), lambda i, ch: (ch, i, 0))
    wspec = pl.BlockSpec((1, D, n), lambda i, ch: (ch, layer, 0))
    return pl.pallas_call(
        body, name=name, grid=(T // tm, 4), in_specs=[dspec, dspec, wspec, wspec],
        out_specs=pl.BlockSpec((tm, D), lambda i, ch: (i, 0)), out_shape=jax.ShapeDtypeStruct((T, D), F32),
        compiler_params=_params(2))(da, db, wg, wu)


def _shift_down(x, k):
    if k == 0:
        return x
    rows = lax.broadcasted_iota(jnp.int32, x.shape, 0)
    return jnp.where(rows >= k, pltpu.roll(x, k, 0), 0.0)


def _shift_up(x, k):
    if k == 0:
        return x
    T = x.shape[0]
    rows = lax.broadcasted_iota(jnp.int32, x.shape, 0)
    return jnp.where(rows < T - k, pltpu.roll(x, T - k, 0), 0.0)


def _conv_silu(x, w):
    c = w[0:1, :] * _shift_down(x, 3) + w[1:2, :] * _shift_down(x, 2) + w[2:3, :] * _shift_down(x, 1) + w[3:4, :] * x
    sg = _sig(c)
    return c, sg, c * sg


def _gdn_conv_fwd(name, proj, cw):
    T = proj.shape[0]

    def body(x_ref, w_ref, o_ref):
        j = pl.program_id(0)
        _, _, y = _conv_silu(x_ref[...], w_ref[...])
        r = lax.rsqrt(jnp.sum(y * y, axis=1, keepdims=True) + EPS)
        mult = jnp.where(j < NH, HD ** -0.5, 1.0)
        o_ref[...] = jnp.where(j < 2 * NH, y * (r * mult), y)

    return pl.pallas_call(body, name=name, grid=(3 * NH,),
                          in_specs=[pl.BlockSpec((T, HD), lambda j: (0, j)), pl.BlockSpec((4, HD), lambda j: (0, j))],
                          out_specs=pl.BlockSpec((T, HD), lambda j: (0, j)),
                          out_shape=jax.ShapeDtypeStruct((T, GDN_QKV), F32), compiler_params=_params(1))(proj, cw)


def _gdn_conv_bwd(name, proj, cw, dz):
    T = proj.shape[0]

    def body(x_ref, w_ref, dz_ref, dx_ref, dw_ref):
        j = pl.program_id(0)
        x, w, dz = x_ref[...], w_ref[...], dz_ref[...]
        c, sg, y = _conv_silu(x, w)
        r = lax.rsqrt(jnp.sum(y * y, axis=1, keepdims=True) + EPS)
        mult = jnp.where(j < NH, HD ** -0.5, 1.0)
        dyn = mult * (r * dz - (r * r * r) * y * jnp.sum(dz * y, axis=1, keepdims=True))
        dy = jnp.where(j < 2 * NH, dyn, dz)
        dc = dy * (sg * (1.0 + c * (1.0 - sg)))
        dx = w[0:1, :] * _shift_up(dc, 3) + w[1:2, :] * _shift_up(dc, 2) + w[2:3, :] * _shift_up(dc, 1) + w[3:4, :] * dc
        dx_ref[...] = dx.astype(dx_ref.dtype)
        for k in range(4):
            dw_ref[pl.ds(k, 1), :] = jnp.sum(dc * _shift_down(x, 3 - k), axis=0, keepdims=True)

    return pl.pallas_call(body, name=name, grid=(3 * NH,),
                          in_specs=[pl.BlockSpec((T, HD), lambda j: (0, j)), pl.BlockSpec((4, HD), lambda j: (0, j)),
                                    pl.BlockSpec((T, HD), lambda j: (0, j))],
                          out_specs=[pl.BlockSpec((T, HD), lambda j: (0, j)), pl.BlockSpec((4, HD), lambda j: (0, j))],
                          out_shape=[jax.ShapeDtypeStruct((T, GDN_QKV), BF16), jax.ShapeDtypeStruct((4, GDN_QKV), F32)],
                          compiler_params=_params(1))(proj, cw, dz)


def _softplus(z):
    return jnp.maximum(z, 0.0) + jnp.log(1.0 + jnp.exp(-jnp.abs(z)))


_AB_CB = GDN_INK // (2 * HD) - 1


def _gdn_gates_fwd(name, proj, alog, dtb):
    def fn(ab, alog, dtb):
        a, b = ab[:, :HD], ab[:, HD:]
        return -jnp.exp(alog) * _softplus(a + dtb), _sig(b)
    return _rowwise(name, fn, [(proj, 2 * HD, _AB_CB)], [alog, dtb], [(HD, F32), (HD, F32)])


def _gdn_gates_bwd(name, proj, dg_h, db_h, alog, dtb):
    def fn(ab, dg_h, db_h, alog, dtb):
        lane = lax.broadcasted_iota(jnp.int32, (1, HD), 1)
        dg = jnp.zeros(dg_h.shape[1:], F32)
        dbeta = jnp.zeros(dg_h.shape[1:], F32)
        for h in range(NH):
            oh = (lane == h).astype(F32)
            dg = dg + dg_h[h] * oh
            dbeta = dbeta + db_h[h] * oh
        a, b = ab[:, :HD], ab[:, HD:]
        z = a + dtb
        ea = jnp.exp(alog)
        beta = _sig(b)
        da = dg * (-ea) * _sig(z)
        db = dbeta * beta * (1.0 - beta)
        return (jnp.concatenate([da, db], axis=1), jnp.sum(dg * (-ea * _softplus(z)), axis=0, keepdims=True),
                jnp.sum(da, axis=0, keepdims=True))
    return _rowwise(name, fn, [(proj, 2 * HD, _AB_CB), dg_h, db_h], [alog, dtb], [(2 * HD, BF16)], sums=[HD, HD])


def _interleave(gens):
    gens = list(gens)
    results = [None] * len(gens)
    active = list(range(len(gens)))
    while active:
        for i in list(active):
            try:
                next(gens[i])
            except StopIteration as stop:
                results[i] = stop.value
                active.remove(i)
    return results


def _chunk_common(q, k, v, gblk, bblk, h):
    C = CHUNK
    lane = lax.broadcasted_iota(jnp.int32, (1, HD), 1)
    oh = (lane == h).astype(F32)
    g_col = jnp.sum(gblk * oh, axis=1, keepdims=True)
    beta = jnp.sum(bblk * oh, axis=1, keepdims=True)
    ri = lax.broadcasted_iota(jnp.int32, (C, C), 0)
    ci = lax.broadcasted_iota(jnp.int32, (C, C), 1)
    incl = ri >= ci
    strict = ri > ci
    eye = (ri == ci).astype(F32)
    gcb = _dot(incl.astype(F32), jnp.broadcast_to(g_col, (C, HD)), "nn", HI)
    yield
    gc = gcb[:, :C]
    gc_row = _dot(jnp.ones((C, C), F32), eye * gc, "nn", HI)
    yield
    decay = jnp.where(incl, jnp.exp(jnp.where(incl, gc - gc_row, 0.0)), 0.0)
    rows = lax.broadcasted_iota(jnp.int32, (C, HD), 0)
    gclb = jnp.sum(jnp.where(rows == C - 1, gcb, 0.0), axis=0, keepdims=True)
    eg = jnp.exp(gcb)
    egl = jnp.exp(gclb - gcb)
    gl = jnp.exp(gclb)
    kb = k * beta
    m1 = _dot(kb, k, "nt", HS)
    qk = _dot(q, k, "nt", HS)
    yield
    L = jnp.where(strict, m1 * decay, 0.0)
    nl = -L
    tinv = eye + nl
    p = nl
    for _ in range(5):
        p = _dot(p, p, "nn", H3)
        yield
        tinv = tinv + _dot(tinv, p, "nn", H3)
    vb = v * beta
    kbg = kb * eg
    yield
    u = _dot(tinv, vb, "nn", HS)
    w = _dot(tinv, kbg, "nn", HS)
    yield
    attn = jnp.where(incl, qk * decay, 0.0)
    return dict(beta=beta, incl=incl, strict=strict, decay=decay, eg=eg, egl=egl, gl=gl, kb=kb, m1=m1, tinv=tinv,
                kbg=kbg, u=u, w=w, qk=qk, attn=attn, q_dec=q * eg, k_dec=k * egl, rows=rows, oh=oh)


def _gdn_chunk_fwd(name, qkv, g, beta):
    T = qkv.shape[0]
    N = T // CHUNK

    hb = _GDN_HB
    w = hb * HD

    def body(q_ref, k_ref, v_ref, g_ref, b_ref, o_ref, st_ref, S):
        hg, n = pl.program_id(0), pl.program_id(1)

        @pl.when(n == 0)
        def _():
            S[...] = jnp.zeros_like(S)

        gblk, bblk = g_ref[...], b_ref[...]

        def one_head(i, q, k, v, s):
            c = yield from _chunk_common(q, k, v, gblk, bblk, hg * hb + i)
            v_new = c["u"] - _dot(c["w"], s, "nn", HS)
            qs = _dot(c["q_dec"], s, "nn", HS)
            yield
            o = qs + _dot(c["attn"], v_new, "nn", HS)
            return o, s * c["gl"] + _dot(c["k_dec"], v_new, "tn", HS)

        sls = [slice(i * HD, (i + 1) * HD) for i in range(hb)]
        states = [S[i] for i in range(hb)]
        res = _interleave(one_head(i, q_ref[:, sls[i]], k_ref[:, sls[i]], v_ref[:, sls[i]], states[i]) for i in range(hb))
        for i, (o, s_new) in enumerate(res):
            st_ref[i, 0] = states[i]
            o_ref[:, sls[i]] = o
            S[i] = s_new

    blk = lambda off: pl.BlockSpec((CHUNK, w), lambda h, n, off=off: (n, off + h))
    gspec = pl.BlockSpec((CHUNK, HD), lambda h, n: (n, 0))
    return pl.pallas_call(
        body, name=name, grid=(NH // hb, N), in_specs=[blk(0), blk(NH // hb), blk(2 * NH // hb), gspec, gspec],
        out_specs=[pl.BlockSpec((CHUNK, w), lambda h, n: (n, h)), pl.BlockSpec((hb, 1, HD, HD), lambda h, n: (h, n, 0, 0))],
        out_shape=[jax.ShapeDtypeStruct((T, NH * HD), F32), jax.ShapeDtypeStruct((NH, N, HD, HD), F32)],
        scratch_shapes=[pltpu.VMEM((hb, HD, HD), F32)], compiler_params=_params(2))(qkv, qkv, qkv, g, beta)


def _gdn_chunk_bwd(name, qkv, g, beta, states, do):
    T = qkv.shape[0]
    N = T // CHUNK
    C = CHUNK

    hb = _GDN_HB
    w = hb * HD

    def body(q_ref, k_ref, v_ref, g_ref, b_ref, st_ref, do_ref, dq_ref, dk_ref, dv_ref, dg_ref, db_ref, dS):
        hg, n = pl.program_id(0), pl.program_id(1)

        @pl.when(n == 0)
        def _():
            dS[...] = jnp.zeros_like(dS)

        gblk, bblk = g_ref[...], b_ref[...]
        sls = [slice(i * HD, (i + 1) * HD) for i in range(hb)]
        res = _interleave(one_head(hg * hb + i, gblk, bblk, q_ref[:, sls[i]], k_ref[:, sls[i]], v_ref[:, sls[i]],
                                   st_ref[i, 0], do_ref[:, sls[i]], dS[i]) for i in range(hb))
        for i, (dq, dk, dv, dg, db, ds_new) in enumerate(res):
            dq_ref[:, sls[i]] = dq
            dk_ref[:, sls[i]] = dk
            dv_ref[:, sls[i]] = dv
            dg_ref[i] = dg
            db_ref[i] = db
            dS[i] = ds_new

    def one_head(h, gblk, bblk, q, k, v, s, do, ds):
        c = yield from _chunk_common(q, k, v, gblk, bblk, h)
        eg, egl, gl, beta, decay, tinv = c["eg"], c["egl"], c["gl"], c["beta"], c["decay"], c["tinv"]
        v_new = c["u"] - _dot(c["w"], s, "nn", HS)
        dq_dec = _dot(do, s, "nt", HS)
        yield
        dv_new = _dot(c["attn"], do, "tn", HS) + _dot(c["k_dec"], ds, "nn", HS)
        dk_dec = _dot(v_new, ds, "nt", HS)
        dgl = jnp.sum(jnp.sum(s * ds, axis=1, keepdims=True), axis=0, keepdims=True)
        yield
        ds_new = ds * gl + _dot(c["q_dec"], do, "tn", HS) - _dot(c["w"], dv_new, "tn", HS)
        dattn = jnp.where(c["incl"], _dot(do, v_new, "nt", HS), 0.0)
        dw = -_dot(dv_new, s, "nt", HS)
        yield
        dvb = _dot(tinv, dv_new, "tn", HS)
        dkbg = _dot(tinv, dw, "tn", HS)
        yield
        dA = -(_dot(dvb, c["u"], "nt", HS) + _dot(dkbg, c["w"], "nt", HS))
        yield
        dL = jnp.where(c["strict"], dA, 0.0)
        dm1 = dL * decay
        dqk = dattn * decay
        xdec = (dL * c["m1"] + dattn * c["qk"]) * decay
        dkb = _dot(dm1, k, "nn", HS) + dkbg * eg
        dk = _dot(dm1, c["kb"], "tn", HS) + _dot(dqk, q, "tn", HS) + dk_dec * egl + dkb * beta
        dq = _dot(dqk, k, "nn", HS) + dq_dec * eg
        yield
        dkd_kd = jnp.sum(dk_dec * c["k_dec"], axis=1, keepdims=True)
        dgc = (jnp.sum(xdec, axis=1, keepdims=True) - _dot(xdec, jnp.ones((C, HD), F32), "tn", HS)
               + jnp.sum(dq_dec * c["q_dec"], axis=1, keepdims=True) - dkd_kd
               + jnp.sum(dkbg * c["kbg"], axis=1, keepdims=True))
        dgcl = jnp.sum(dkd_kd, axis=0, keepdims=True) + dgl * gl
        dgc = dgc + jnp.where(c["rows"] == C - 1, dgcl, 0.0)
        ri = lax.broadcasted_iota(jnp.int32, (C, C), 0)
        ci = lax.broadcasted_iota(jnp.int32, (C, C), 1)
        dg = _dot((ci >= ri).astype(F32), dgc, "nn", HI)
        db = jnp.broadcast_to(jnp.sum(dkb * k, axis=1, keepdims=True) + jnp.sum(dvb * v, axis=1, keepdims=True), (C, HD))
        return dq, dk, dvb * beta, dg, db, ds_new

    blk = lambda off: pl.BlockSpec((C, w), lambda h, n, off=off: (N - 1 - n, off + h))
    gspec = pl.BlockSpec((C, HD), lambda h, n: (N - 1 - n, 0))
    ospec = pl.BlockSpec((C, w), lambda h, n: (N - 1 - n, h))
    hspec = pl.BlockSpec((hb, C, HD), lambda h, n: (h, N - 1 - n, 0))
    return pl.pallas_call(
        body, name=name, grid=(NH // hb, N),
        in_specs=[blk(0), blk(NH // hb), blk(2 * NH // hb), gspec, gspec,
                  pl.BlockSpec((hb, 1, HD, HD), lambda h, n: (h, N - 1 - n, 0, 0)), ospec],
        out_specs=[ospec, ospec, ospec, hspec, hspec],
        out_shape=[jax.ShapeDtypeStruct((T, NH * HD), F32)] * 3 + [jax.ShapeDtypeStruct((NH, T, HD), F32)] * 2,
        scratch_shapes=[pltpu.VMEM((hb, HD, HD), F32)], compiler_params=_params(2))(qkv, qkv, qkv, g, beta, states, do)


_GATE_CB = GDN_QKV // (NH * HD)


def _gdn_gated_norm_fwd(name, o, proj, ng):
    def fn(o, gate, ng):
        outs = []
        for h in range(NH):
            sl = slice(h * HD, (h + 1) * HD)
            oh, gh = o[:, sl], gate[:, sl]
            outs.append(oh * _rms(oh) * ng * (gh * _sig(gh)))
        return jnp.concatenate(outs, axis=1)
    return _rowwise(name, fn, [o, (proj, NH * HD, _GATE_CB)], [ng], [(NH * HD, BF16)])[0]


def _gdn_gated_norm_bwd(name, don, o, proj, ng):
    def fn(don, o, gate, ng):
        dos, dgs = [], []
        dng = jnp.zeros((1, HD), F32)
        for h in range(NH):
            sl = slice(h * HD, (h + 1) * HD)
            oh, gh, dh = o[:, sl], gate[:, sl], don[:, sl]
            r = _rms(oh)
            xh = oh * r
            sg = _sig(gh)
            dn = dh * (gh * sg)
            dgs.append(dh * (xh * ng) * (sg * (1.0 + gh * (1.0 - sg))))
            dng = dng + jnp.sum(dn * xh, axis=0, keepdims=True)
            dxh = dn * ng
            dos.append(r * (dxh - xh * jnp.mean(dxh * xh, axis=-1, keepdims=True)))
        return jnp.concatenate(dos, axis=1), jnp.concatenate(dgs, axis=1), dng
    return _rowwise(name, fn, [don, o, (proj, NH * HD, _GATE_CB)], [ng], [(NH * HD, F32), (NH * HD, BF16)], sums=[HD])


def _rot(x):
    lane = lax.broadcasted_iota(jnp.int32, x.shape, 1)
    return jnp.where(lane < ROPE // 2, -pltpu.roll(x, HD - ROPE // 2, 1), pltpu.roll(x, ROPE // 2, 1))


def _rot_t(x):
    lane = lax.broadcasted_iota(jnp.int32, x.shape, 1)
    return jnp.where(lane < ROPE // 2, pltpu.roll(x, HD - ROPE // 2, 1), -pltpu.roll(x, ROPE // 2, 1))


def _rope_tables(pos_col):
    lane = jnp.arange(HD)
    inv_freq = ROPE_THETA ** (-(2.0 * (lane % (ROPE // 2)).astype(F32)) / ROPE)
    inv_freq = jnp.where(lane < ROPE, inv_freq, 0.0).astype(F32)[None, :]
    valid = (lane < ROPE).astype(F32)[None, :]

    def fn(pos, inv_freq, valid):
        ang = pos.astype(F32) * inv_freq
        return jnp.cos(ang) * valid, jnp.sin(ang) * valid
    return _rowwise("rope_tables", fn, [pos_col], [inv_freq, valid], [(HD, F32), (HD, F32)])


def _mla_pre_fwd(name, proj, cos, sin, qg, kvg):
    def fn(p, cos, sin, qg, kvg):
        cq, ckv, kr = p[:, :Q_RANK], p[:, Q_RANK:Q_RANK + KV_RANK], p[:, Q_RANK + KV_RANK:]
        return cq * _rms(cq) * qg, ckv * _rms(ckv) * kvg, kr * cos + _rot(kr) * sin
    return _rowwise(name, fn, [proj, cos, sin], [qg, kvg], [(Q_RANK, BF16), (KV_RANK, BF16), (HD, BF16)])


def _rms_bwd(dy, x, g):
    r = _rms(x)
    xh = x * r
    dxh = dy * g
    return r * (dxh - xh * jnp.mean(dxh * xh, axis=-1, keepdims=True)), jnp.sum(dy * xh, axis=0, keepdims=True)


def _mla_pre_bwd(name, proj, dcqn, dckvn, dkr, cos, sin, qg, kvg):
    def fn(p, dcqn, dckvn, dkr, cos, sin, qg, kvg):
        cq, ckv = p[:, :Q_RANK], p[:, Q_RANK:Q_RANK + KV_RANK]
        dcq, dqg = _rms_bwd(dcqn, cq, qg)
        dckv, dkvg = _rms_bwd(dckvn, ckv, kvg)
        dkr_pre = dkr * cos + _rot_t(dkr * sin)
        return jnp.concatenate([dcq, dckv, dkr_pre], axis=1), dqg, dkvg
    return _rowwise(name, fn, [proj, dcqn, dckvn, dkr, cos, sin], [qg, kvg], [(MLA_INK, BF16)], sums=[Q_RANK, KV_RANK])


def _mla_q_fwd(name, q, cos, sin):
    def fn(qn, qr, cos, sin):
        outs = []
        for h in range(NH):
            x = qr[:, h * HD:(h + 1) * HD]
            outs.append(x * cos + _rot(x) * sin)
        return qn, jnp.concatenate(outs, axis=1)
    return _rowwise(name, fn, [(q, NH * HD, 0), (q, NH * HD, 1), cos, sin], [], [(NH * HD, BF16), (NH * HD, BF16)])


def _mla_q_bwd(name, dqn, dqr, cos, sin):
    def fn(dqn, dqr, cos, sin):
        outs = [dqn]
        for h in range(NH):
            z = dqr[:, h * HD:(h + 1) * HD]
            outs.append(z * cos + _rot_t(z * sin))
        return jnp.concatenate(outs, axis=1)
    return _rowwise(name, fn, [dqn, dqr, cos, sin], [], [(2 * NH * HD, BF16)])[0]


def _att_probs(qn, qr, kn, kr, row0):
    s = (_dot(qn, kn, "nt") + _dot(qr, kr, "nt")) * ATT_SCALE
    qpos = row0 + lax.broadcasted_iota(jnp.int32, s.shape, 0)
    kpos = lax.broadcasted_iota(jnp.int32, s.shape, 1)
    s = jnp.where(kpos <= qpos, s, -1e30)
    p = jnp.exp(s - jnp.max(s, axis=1, keepdims=True))
    return p / jnp.sum(p, axis=1, keepdims=True)


def _mla_attn_fwd(name, qn, qr, kv, kr, tq=256):
    T = qn.shape[0]
    tq = min(tq, T)

    def body(qn_ref, qr_ref, kn_ref, v_ref, kr_ref, o_ref):
        i = pl.program_id(1)
        for blk in range(T // tq):
            @pl.when(i == blk)
            def _(blk=blk):
                keys = pl.ds(0, (blk + 1) * tq)
                p = _att_probs(qn_ref[...], qr_ref[...], kn_ref[keys, :], kr_ref[keys, :], blk * tq)
                o_ref[...] = _dot(p.astype(BF16), v_ref[keys, :], "nn").astype(o_ref.dtype)

    qspec = pl.BlockSpec((tq, HD), lambda h, i: (i, h))
    return pl.pallas_call(
        body, name=name, grid=(NH, T // tq),
        in_specs=[qspec, qspec, pl.BlockSpec((T, HD), lambda h, i: (0, h)), pl.BlockSpec((T, HD), lambda h, i: (0, NH + h)),
                  pl.BlockSpec((T, HD), lambda h, i: (0, 0))],
        out_specs=qspec, out_shape=jax.ShapeDtypeStruct((T, NH * HD), BF16), compiler_params=_params(2))(qn, qr, kv, kv, kr)


def _mla_attn_bwd(name, qn, qr, kv, kr, do, tq=256):
    T = qn.shape[0]
    tq = min(tq, T)

    def body(qn_ref, qr_ref, kn_ref, v_ref, kr_ref, do_ref, dqn_ref, dqr_ref, dkn_ref, dv_ref, dkr_ref):
        h, i = pl.program_id(0), pl.program_id(1)

        @pl.when(i == 0)
        def _():
            dkn_ref[...] = jnp.zeros_like(dkn_ref)
            dv_ref[...] = jnp.zeros_like(dv_ref)

        @pl.when((i == 0) & (h == 0))
        def _():
            dkr_ref[...] = jnp.zeros_like(dkr_ref)

        for blk in range(T // tq):
            @pl.when(i == blk)
            def _(blk=blk):
                keys = pl.ds(0, (blk + 1) * tq)
                qn, qr, do = qn_ref[...], qr_ref[...], do_ref[...]
                kn, kr, v = kn_ref[keys, :], kr_ref[keys, :], v_ref[keys, :]
                p = _att_probs(qn, qr, kn, kr, blk * tq)
                dp = _dot(do, v, "nt")
                ds = (p * (dp - jnp.sum(p * dp, axis=1, keepdims=True)) * ATT_SCALE).astype(BF16)
                dqn_ref[...] = _dot(ds, kn, "nn")
                dqr_ref[...] = _dot(ds, kr, "nn")
                dkn_ref[keys, :] += _dot(ds, qn, "tn")
                dkr_ref[keys, :] += _dot(ds, qr, "tn")
                dv_ref[keys, :] += _dot(p.astype(BF16), do, "tn")

    qspec = pl.BlockSpec((tq, HD), lambda h, i: (i, h))
    kspec = pl.BlockSpec((T, HD), lambda h, i: (0, h))
    return pl.pallas_call(
        body, name=name, grid=(NH, T // tq),
        in_specs=[qspec, qspec, kspec, pl.BlockSpec((T, HD), lambda h, i: (0, NH + h)),
                  pl.BlockSpec((T, HD), lambda h, i: (0, 0)), qspec],
        out_specs=[qspec, qspec, kspec, kspec, pl.BlockSpec((T, HD), lambda h, i: (0, 0))],
        out_shape=[jax.ShapeDtypeStruct((T, NH * HD), F32)] * 4 + [jax.ShapeDtypeStruct((T, HD), F32)],
        compiler_params=_params(2))(qn, qr, kv, kv, kr, do)


def _mod_rows(mod, layer):
    return [mod[layer:layer + 1, i * D:(i + 1) * D] for i in range(N_MOD)]


def _local_step(x, target, pos_col, mod, W, P, on_grads):
    cos, sin = _rope_tables(pos_col)
    saved = []
    for l in range(DEPTH):
        j = l // 2
        sh_m, sc_m, ga_m, sh_f, sc_f, ga_f = _mod_rows(mod, l)
        s = dict(x0=x)
        h = _norm_mod_fwd(f"norm_mix{l}", x, P["norm_mix_g"][l:l + 1], sc_m, sh_m)
        s["h"] = h
        if l % 2 == 0:
            proj = _mm(f"gdn_in{j}", h, W["gdn_in"][j], "nn")
            qkv = _gdn_conv_fwd(f"gdn_conv{j}", proj, P["gdn_cw"][j])
            g, beta = _gdn_gates_fwd(f"gdn_gates{j}", proj, P["gdn_alog"][j], P["gdn_dtb"][j])
            o, states = _gdn_chunk_fwd(f"gdn_chunk{j}", qkv, g, beta)
            on = _gdn_gated_norm_fwd(f"gdn_gnorm{j}", o, proj, P["gdn_ng"][j])
            y = _mm(f"gdn_out{j}", on, W["gdn_out"][j], "nn")
            s.update(proj=proj, qkv=qkv, g=g, beta=beta, o=o, states=states, on=on)
        else:
            proj = _mm(f"mla_in{j}", h, W["mla_in"][j], "nn")
            cqn, ckvn, kr = _mla_pre_fwd(f"mla_pre{j}", proj, cos, sin, P["mla_qg"][j], P["mla_kvg"][j])
            q = _mm(f"mla_uq{j}", cqn, W["mla_uq"][j], "nn")
            kv = _mm(f"mla_ukv{j}", ckvn, W["mla_ukv"][j], "nn", out_dtype=BF16)
            qn, qr = _mla_q_fwd(f"mla_q{j}", q, cos, sin)
            o = _mla_attn_fwd(f"mla_attn{j}", qn, qr, kv, kr)
            y = _mm(f"mla_out{j}", o, W["mla_out"][j], "nn")
            s.update(proj=proj, cqn=cqn, ckvn=ckvn, kr=kr, kv=kv, qn=qn, qr=qr, o=o)
        s["y"] = y
        x = _residual_fwd(f"res_mix{l}", x, y, ga_m)
        s["x1"] = x
        h2 = _norm_mod_fwd(f"norm_ffn{l}", x, P["norm_ffn_g"][l:l + 1], sc_f, sh_f)
        fa, fb, sw = _ffn_up(f"ffn_up{l}", h2, W["ffn_g"][l], W["ffn_u"][l], 0)
        yf = _ffn_down(f"ffn_down{l}", sw, W["ffn_d"][l], 0)
        x = _residual_fwd(f"res_ffn{l}", x, yf, ga_f)
        s.update(h2=h2, fa=fa, fb=fb, sw=sw, yf=yf)
        saved.append(s)

    dx, loss, d_final = _loss_head(x, target, P["final_g"])
    gP = dict(loss=loss, final_g=d_final, norm_mix_g=[None] * DEPTH, norm_ffn_g=[None] * DEPTH,
              gdn_cw=[None] * 2, gdn_alog=[None] * 2, gdn_dtb=[None] * 2, gdn_ng=[None] * 2,
              mla_qg=[None] * 2, mla_kvg=[None] * 2)
    dmod = [None] * DEPTH
    for l in reversed(range(DEPTH)):
        j = l // 2
        s = saved[l]
        sh_m, sc_m, ga_m, sh_f, sc_f, ga_f = _mod_rows(mod, l)
        dyf, d_ga_f = _residual_bwd(f"res_ffn_b{l}", dx, s["yf"], ga_f)
        da, db = _ffn_down_bwd(f"ffn_down_dx{l}", dyf, W["ffn_d"][l], s["fa"], s["fb"], 0)
        g_down = _ffn_down_dw(f"ffn_down_dw{l}", s["sw"], dyf)
        g_gate, g_up = _ffn_up_dw(f"ffn_up_dw{l}", s["h2"], da, db)
        on_grads(l, "ffn", dict(ffn_w_gate=g_gate, ffn_w_up=g_up, ffn_w_down=g_down))
        dh2 = _ffn_up_dx(f"ffn_up_dx{l}", da, db, W["ffn_g"][l], W["ffn_u"][l], 0)
        dx, d_sh_f, d_sc_f, gP["norm_ffn_g"][l] = _norm_mod_bwd(f"norm_ffn_b{l}", dh2, s["x1"], dx,
                                                                 P["norm_ffn_g"][l:l + 1], sc_f)
        dy, d_ga_m = _residual_bwd(f"res_mix_b{l}", dx, s["y"], ga_m)
        if l % 2 == 0:
            don = _mm(f"gdn_out_dx{j}", dy, W["gdn_out"][j], "nt")
            g_out = _mm(f"gdn_out_dw{j}", s["on"], dy, "tn", out_dtype=BF16)
            do, dgate, gP["gdn_ng"][j] = _gdn_gated_norm_bwd(f"gdn_gnorm_b{j}", don, s["o"], s["proj"], P["gdn_ng"][j])
            dq, dk, dv, dg_h, db_h = _gdn_chunk_bwd(f"gdn_chunk_b{j}", s["qkv"], s["g"], s["beta"], s["states"], do)
            dab_, gP["gdn_alog"][j], gP["gdn_dtb"][j] = _gdn_gates_bwd(f"gdn_gates_b{j}", s["proj"], dg_h, db_h,
                                                                        P["gdn_alog"][j], P["gdn_dtb"][j])
            dpre, gP["gdn_cw"][j] = _gdn_conv_bwd(f"gdn_conv_b{j}", s["proj"], P["gdn_cw"][j],
                                                  jnp.concatenate([dq, dk, dv], axis=1))
            dproj = jnp.concatenate([dpre, dgate, dab_], axis=1)
            g_in = _mm(f"gdn_in_dw{j}", s["h"], dproj, "tn", out_dtype=BF16)
            on_grads(l, "mix", dict(gdn_w_in=_uncols(_gdn_in_from_kernel(g_in)), gdn_w_out=_unrows(g_out)))
            dh = _mm(f"gdn_in_dx{j}", dproj, W["gdn_in"][j], "nt")
        else:
            do = _mm(f"mla_out_dx{j}", dy, W["mla_out"][j], "nt", out_dtype=BF16)
            g_out = _mm(f"mla_out_dw{j}", s["o"], dy, "tn", out_dtype=BF16)
            dqn, dqr, dkn, dv, dkr = _mla_attn_bwd(f"mla_attn_b{j}", s["qn"], s["qr"], s["kv"], s["kr"], do)
            dq = _mla_q_bwd(f"mla_q_b{j}", dqn, dqr, cos, sin)
            dkv = jnp.concatenate([dkn, dv], axis=1)
            g_uq = _mm(f"mla_uq_dw{j}", s["cqn"], dq, "tn", out_dtype=BF16)
            dcqn = _mm(f"mla_uq_dx{j}", dq, W["mla_uq"][j], "nt")
            g_ukv = _mm(f"mla_ukv_dw{j}", s["ckvn"], dkv, "tn", out_dtype=BF16)
            dckvn = _mm(f"mla_ukv_dx{j}", dkv, W["mla_ukv"][j], "nt")
            dproj, gP["mla_qg"][j], gP["mla_kvg"][j] = _mla_pre_bwd(f"mla_pre_b{j}", s["proj"], dcqn, dckvn, dkr, cos, sin,
                                                                     P["mla_qg"][j], P["mla_kvg"][j])
            g_in = _mm(f"mla_in_dw{j}", s["h"], dproj, "tn", out_dtype=BF16)
            on_grads(l, "mix", dict(mla_w_in=_unrows(g_in[:, :Q_RANK + KV_RANK + ROPE]), mla_w_uq=_uncols(_mla_uq_from_kernel(g_uq)),
                                    mla_w_ukv=_uncols(_mla_ukv_from_kernel(g_ukv)), mla_w_out=_unrows(g_out)))
            dh = _mm(f"mla_in_dx{j}", dproj, W["mla_in"][j], "nt")
        dx, d_sh_m, d_sc_m, gP["norm_mix_g"][l] = _norm_mod_bwd(f"norm_mix_b{l}", dh, s["x0"], dx,
                                                                 P["norm_mix_g"][l:l + 1], sc_m)
        dmod[l] = jnp.concatenate([d_sh_m, d_sc_m, d_ga_m, d_sh_f, d_sc_f, d_ga_f], axis=1)
    return dx, jnp.concatenate(dmod, axis=0), gP


def _pad_cols(a, width):
    return jnp.pad(a, ((0, 0), (0, width - a.shape[1])))


def _gdn_in_to_kernel(w):
    m = GDN_QKV + NH * HD
    return jnp.concatenate([w[:, :m], _pad_cols(w[:, m:m + NH], HD), _pad_cols(w[:, m + NH:], HD)], axis=1)


def _gdn_in_from_kernel(g):
    m = GDN_QKV + NH * HD
    return jnp.concatenate([g[:, :m], g[:, m:m + NH], g[:, m + HD:m + HD + NH]], axis=1)


def _mla_uq_to_kernel(w):
    w3 = w.reshape(Q_RANK, NH, HD + ROPE)
    rope = jnp.pad(w3[:, :, HD:], ((0, 0), (0, 0), (0, HD - ROPE)))
    return jnp.concatenate([w3[:, :, :HD].reshape(Q_RANK, NH * HD), rope.reshape(Q_RANK, NH * HD)], axis=1)


def _mla_uq_from_kernel(g):
    gn = g[:, :NH * HD].reshape(Q_RANK, NH, HD)
    gr = g[:, NH * HD:].reshape(Q_RANK, NH, HD)[:, :, :ROPE]
    return jnp.concatenate([gn, gr], axis=2).reshape(Q_RANK, NH * (HD + ROPE))


def _mla_ukv_to_kernel(w):
    w3 = w.reshape(KV_RANK, NH, 2 * HD)
    return jnp.concatenate([w3[:, :, :HD].reshape(KV_RANK, NH * HD), w3[:, :, HD:].reshape(KV_RANK, NH * HD)], axis=1)


def _mla_ukv_from_kernel(g):
    gk = g[:, :NH * HD].reshape(KV_RANK, NH, HD)
    gv = g[:, NH * HD:].reshape(KV_RANK, NH, HD)
    return jnp.concatenate([gk, gv], axis=2).reshape(KV_RANK, NH * 2 * HD)


def _cols(t):
    return jnp.moveaxis(t, 0, 1).reshape(t.shape[1], -1)


def _uncols(g):
    return jnp.moveaxis(g.reshape(g.shape[0], 4, -1), 1, 0)


def _rows(t):
    return t.reshape(-1, t.shape[2])


def _unrows(g):
    return g.reshape(4, -1, g.shape[1])


def _layer_weights(layer):
    mixer = ("gdn_w_in", "gdn_w_out") if layer % 2 == 0 else ("mla_w_in", "mla_w_uq", "mla_w_ukv", "mla_w_out")
    return [(n, layer // 2) for n in mixer] + [(n, layer) for n in ("ffn_w_gate", "ffn_w_up", "ffn_w_down")]


def _weights_to_kernel(got):
    gdn, mla = [got[0], got[2]], [got[1], got[3]]
    return dict(
        gdn_in=[_gdn_in_to_kernel(_cols(g["gdn_w_in"])) for g in gdn],
        gdn_out=[_rows(g["gdn_w_out"]) for g in gdn],
        mla_in=[_pad_cols(_rows(g["mla_w_in"]), MLA_INK) for g in mla],
        mla_uq=[_mla_uq_to_kernel(_cols(g["mla_w_uq"])) for g in mla],
        mla_ukv=[_mla_ukv_to_kernel(_cols(g["mla_w_ukv"])) for g in mla],
        mla_out=[_rows(g["mla_w_out"]) for g in mla],
        ffn_g=[g["ffn_w_gate"] for g in got], ffn_u=[g["ffn_w_up"] for g in got], ffn_d=[g["ffn_w_down"] for g in got],
    )


def _small_to_kernel(norm_mix_g, norm_ffn_g, final_norm_g, gdn_conv_w, gdn_a_log, gdn_dt_bias, gdn_norm_g, q_norm_g, kv_norm_g):
    return dict(
        norm_mix_g=norm_mix_g, norm_ffn_g=norm_ffn_g, final_g=final_norm_g.reshape(1, D),
        gdn_cw=[jnp.transpose(gdn_conv_w[j]) for j in range(2)],
        gdn_alog=[_pad_cols(gdn_a_log[j:j + 1], HD) for j in range(2)],
        gdn_dtb=[_pad_cols(gdn_dt_bias[j:j + 1], HD) for j in range(2)],
        gdn_ng=[gdn_norm_g[j:j + 1] for j in range(2)],
        mla_qg=[q_norm_g[j:j + 1] for j in range(2)],
        mla_kvg=[kv_norm_g[j:j + 1] for j in range(2)],
    )


_CHIP_FLIPS = ((1, 0), (0, 1), (1, 1))
_ANY = pl.BlockSpec(memory_space=pl.ANY)


def _me():
    return lax.axis_index("x"), lax.axis_index("y"), lax.axis_index("c")


def _chip_peer(dx, dy):
    x, y, c = _me()
    return ((1 - x) if dx else x, (1 - y) if dy else y, c)


def _rcopy(src, dst, send_sem, recv_sem, to):
    return pltpu.make_async_remote_copy(src_ref=src, dst_ref=dst, send_sem=send_sem, recv_sem=recv_sem,
                                        device_id=to, device_id_type=MESH)


def _allgather4(name, a, halves=False):
    R, C = a.shape
    rh = R // 2 if halves else R

    def body(a_ref, out_ref, send_sems, recv_sems, local_sem):
        x, y, c = _me()
        me = 2 * x + y
        src = a_ref.at[pl.ds(c * rh, rh)] if halves else a_ref
        local = pltpu.make_async_copy(src, out_ref.at[me], local_sem)
        local.start()
        sends = []
        for k, (dx, dy) in enumerate(_CHIP_FLIPS):
            cp = _rcopy(src, out_ref.at[me], send_sems.at[k], recv_sems.at[k], _chip_peer(dx, dy))
            cp.start()
            sends.append(cp)
        for k, (dx, dy) in enumerate(_CHIP_FLIPS):
            px, py, _ = _chip_peer(dx, dy)
            _rcopy(src, out_ref.at[2 * px + py], send_sems.at[k], recv_sems.at[k], _chip_peer(dx, dy)).wait_recv()
        for cp in sends:
            cp.wait_send()
        local.wait()

    return pl.pallas_call(
        body, name=name, in_specs=[_ANY], out_specs=_ANY, out_shape=jax.ShapeDtypeStruct((4, rh, C), a.dtype),
        scratch_shapes=[pltpu.SemaphoreType.DMA((3,)), pltpu.SemaphoreType.DMA((3,)), pltpu.SemaphoreType.DMA(())])(a)


_NCH = 4


def _dma_sems(*counts):
    return [pltpu.SemaphoreType.DMA((n,)) for n in counts]


def _slot_tile(rows):
    tr = 256
    while rows % tr:
        tr //= 2
    return tr


def _cast_into_slot(name, a, chip, row0, rows):
    C = a.shape[1]
    tr = _slot_tile(rows)
    assert row0 % tr == 0
    first = row0 // tr

    def body(c_ref, a_ref, o_ref):
        o_ref[0] = a_ref[...].astype(o_ref.dtype)

    grid_spec = pltpu.PrefetchScalarGridSpec(
        num_scalar_prefetch=1, grid=(rows // tr,), in_specs=[pl.BlockSpec((tr, C), lambda i, c_ref: (first + i, 0))],
        out_specs=pl.BlockSpec((1, tr, C), lambda i, c_ref: (c_ref[0], i, 0)))
    return pl.pallas_call(body, name=name, grid_spec=grid_spec, out_shape=jax.ShapeDtypeStruct((4, rows, C), BF16),
                          compiler_params=_params(1))(chip, a)


def _chunks(rows, align):
    for nch in (_NCH, 2):
        if rows % (nch * align) == 0:
            return nch
    return 1


def _gather_exchange(out, ici_s, ici_r, d2d_s, d2d_r):
    n = len(out)
    x, y, c = _me()
    me = 2 * x + y
    sib = (x, y, 1 - c)
    peers = [_chip_peer(dx, dy) for dx, dy in _CHIP_FLIPS]
    for t in range(n):
        h = out[t].shape[1] // 2
        nch = _chunks(h, 16)
        ch = h // nch
        for k, peer in enumerate(peers):
            for i in range(nch):
                blk = out[t].at[me, pl.ds(c * h + i * ch, ch)]
                _rcopy(blk, blk, ici_s.at[3 * t + k], ici_r.at[3 * t + k], peer).start()
    for t in range(n):
        h = out[t].shape[1] // 2
        nch = _chunks(h, 16)
        ch = h // nch
        for k, peer in enumerate(peers):
            pchip = 2 * peer[0] + peer[1]
            got = out[t].at[pchip, pl.ds(c * h, h)]
            _rcopy(got, got, ici_s.at[3 * t + k], ici_r.at[3 * t + k], peer).wait_recv()
            for i in range(nch):
                blk = out[t].at[pchip, pl.ds(c * h + i * ch, ch)]
                _rcopy(blk, blk, d2d_s.at[3 * t + k], d2d_r.at[3 * t + k], sib).start()
    for t in range(n):
        h = out[t].shape[1] // 2
        for k, peer in enumerate(peers):
            pchip = 2 * peer[0] + peer[1]
            other = out[t].at[pchip, pl.ds((1 - c) * h, h)]
            _rcopy(other, other, d2d_s.at[3 * t + k], d2d_r.at[3 * t + k], sib).wait_recv()
            _rcopy(other, other, ici_s.at[3 * t + k], ici_r.at[3 * t + k], peer).wait_send()
            _rcopy(other, other, d2d_s.at[3 * t + k], d2d_r.at[3 * t + k], sib).wait_send()


def _gather_weights_async(name, collective_id, bufs):
    n = len(bufs)
    refs = [jax.new_ref(b, memory_space=pltpu.MemorySpace.HBM) for b in bufs]

    @pl.kernel(mesh=plsc.ScalarSubcoreMesh(axis_name="sequencer", num_cores=1), name=name,
               scratch_types=tuple(_dma_sems(3 * n, 3 * n, 3 * n, 3 * n)),
               compiler_params=pltpu.CompilerParams(collective_id=collective_id))
    def launch(ici_s, ici_r, d2d_s, d2d_r):
        x, y, c = _me()
        barrier = pltpu.get_barrier_semaphore()
        for peer in [_chip_peer(dx, dy) for dx, dy in _CHIP_FLIPS] + [(x, y, 1 - c)]:
            pl.semaphore_signal(barrier, inc=1, device_id=peer, device_id_type=MESH)
        pl.semaphore_wait(barrier, 4)
        _gather_exchange(refs, ici_s, ici_r, d2d_s, d2d_r)

    launch()
    return [r[...] for r in refs]


def _rs_split(name, grads):
    n = len(grads)

    def body(*refs):
        g, out = refs[:n], refs[n:2 * n]
        send, recv = refs[2 * n:]
        x, y, c = _me()
        sib = (x, y, 1 - c)
        for t in range(n):
            h = g[t].shape[1] // 2
            for d in range(4):
                _rcopy(g[t].at[d, pl.ds((1 - c) * h, h)], out[t].at[d], send.at[t], recv.at[t], sib).start()
        for t in range(n):
            _rcopy(out[t], out[t], send.at[t], recv.at[t], sib).wait()

    return pl.pallas_call(
        body, name=name, in_specs=[_ANY] * n, out_specs=[_ANY] * n,
        out_shape=[jax.ShapeDtypeStruct((4, s.shape[1] // 2, s.shape[2]), s.dtype) for s in grads],
        scratch_shapes=_dma_sems(n, n))(*grads)


def _pair_add(name, g, theirs, core_chip):
    _, R, C = g.shape
    h = R // 2
    tr = _slot_tile(h)
    nb = h // tr

    def body(s_ref, g_ref, t_ref, p_ref, o_ref):
        val = (g_ref[...].astype(F32) + t_ref[...].astype(F32)).astype(p_ref.dtype)
        p_ref[...] = val

        @pl.when(pl.program_id(1) == s_ref[1])
        def _():
            o_ref[...] = val

    spec = pl.BlockSpec((1, tr, C), lambda i, d, s_ref: (d, i, 0))
    grid_spec = pltpu.PrefetchScalarGridSpec(
        num_scalar_prefetch=1, grid=(nb, 4),
        in_specs=[pl.BlockSpec((1, tr, C), lambda i, d, s_ref: (d, s_ref[0] * nb + i, 0)), spec],
        out_specs=[spec, pl.BlockSpec((1, tr, C), lambda i, d, s_ref: (s_ref[1], i, 0))])
    half = jax.ShapeDtypeStruct((4, h, C), BF16)
    return pl.pallas_call(body, name=name, grid_spec=grid_spec, out_shape=[half, half],
                          compiler_params=_params(2))(core_chip, g, theirs)


def _rs_alltoall_async(name, collective_id, parts, bufs):
    n = len(parts)
    p = [jax.new_ref(a, memory_space=pltpu.MemorySpace.HBM) for a in parts]
    out = [jax.new_ref(b, memory_space=pltpu.MemorySpace.HBM) for b in bufs]

    @pl.kernel(mesh=plsc.ScalarSubcoreMesh(axis_name="sequencer", num_cores=1), name=name,
               scratch_types=tuple(_dma_sems(3 * n, 3 * n)),
               compiler_params=pltpu.CompilerParams(collective_id=collective_id))
    def launch(send, recv):
        barrier = pltpu.get_barrier_semaphore()
        for peer in [_chip_peer(dx, dy) for dx, dy in _CHIP_FLIPS]:
            pl.semaphore_signal(barrier, inc=1, device_id=peer, device_id_type=MESH)
        pl.semaphore_wait(barrier, 3)
        _alltoall_exchange(p, out, send, recv)

    launch()
    return [r[...] for r in out]


def _alltoall_exchange(p, out, send, recv):
    x, y, c = _me()
    me = 2 * x + y
    peers = [_chip_peer(dx, dy) for dx, dy in _CHIP_FLIPS]
    for t in range(len(p)):
        h = p[t].shape[1]
        nch = _chunks(h, 16)
        ch = h // nch
        for k, peer in enumerate(peers):
            pchip = 2 * peer[0] + peer[1]
            for i in range(nch):
                rows = pl.ds(i * ch, ch)
                _rcopy(p[t].at[pchip, rows], out[t].at[me, rows], send.at[3 * t + k], recv.at[3 * t + k], peer).start()
    for t in range(len(p)):
        for k, peer in enumerate(peers):
            pchip = 2 * peer[0] + peer[1]
            _rcopy(out[t].at[pchip], out[t].at[pchip], send.at[3 * t + k], recv.at[3 * t + k], peer).wait()


def _rs_swap(name, halves):
    n = len(halves)

    def body(*refs):
        a, out = refs[:n], refs[n:2 * n]
        send, recv = refs[2 * n:]
        x, y, c = _me()
        sib = (x, y, 1 - c)
        for t in range(n):
            ch = a[t].shape[0] // _NCH
            for i in range(_NCH):
                rows = pl.ds(i * ch, ch)
                _rcopy(a[t].at[rows], out[t].at[rows], send.at[t], recv.at[t], sib).start()
        for t in range(n):
            _rcopy(a[t], out[t], send.at[t], recv.at[t], sib).wait()

    return pl.pallas_call(
        body, name=name, in_specs=[_ANY] * n, out_specs=[_ANY] * n,
        out_shape=[jax.ShapeDtypeStruct(s.shape, s.dtype) for s in halves],
        scratch_shapes=_dma_sems(n, n))(*halves)


def _sibling_merge(name, a):
    P_, rh, C = a.shape

    def body(a_ref, out_ref, send_sem, recv_sem, local_sem):
        x, y, c = _me()
        local = pltpu.make_async_copy(a_ref, out_ref.at[:, pl.ds(c * rh, rh)], local_sem)
        local.start()
        cp = _rcopy(a_ref, out_ref.at[:, pl.ds(c * rh, rh)], send_sem, recv_sem, (x, y, 1 - c))
        cp.start()
        cp.wait_send()
        _rcopy(a_ref, out_ref.at[:, pl.ds((1 - c) * rh, rh)], send_sem, recv_sem, (x, y, 1 - c)).wait_recv()
        local.wait()

    return pl.pallas_call(
        body, name=name, in_specs=[_ANY], out_specs=_ANY, out_shape=jax.ShapeDtypeStruct((P_, 2 * rh, C), a.dtype),
        scratch_shapes=[pltpu.SemaphoreType.DMA(()), pltpu.SemaphoreType.DMA(()), pltpu.SemaphoreType.DMA(())])(a)


def _allgather8(name, a):
    g4 = _allgather4(name + "_chips", a)
    both = _sibling_merge(name + "_cores", g4.reshape(1, 4 * a.shape[0], a.shape[1]))
    return jnp.transpose(both.reshape(2, 4, *a.shape), (1, 0, 2, 3)).reshape(8, *a.shape)


def _sum_slots(name, a, out_dtype):
    def fn(a):
        acc = a[0].astype(F32)
        for k in range(1, a.shape[0]):
            acc = acc + a[k].astype(F32)
        return acc
    return _rowwise(name, fn, [a], [], [(a.shape[2], out_dtype)])[0]


def _adamw_math(w, g, m, v):
    m = ADAM_B1 * m + (1.0 - ADAM_B1) * g
    v = ADAM_B2 * v + (1.0 - ADAM_B2) * (g * g)
    m_hat = m / (1.0 - ADAM_B1 ** ADAM_STEP)
    v_hat = v / (1.0 - ADAM_B2 ** ADAM_STEP)
    return -ADAM_LR * (m_hat / (jnp.sqrt(v_hat) + ADAM_EPS) + ADAM_WD * w), m, v


def _adamw_piece(name, w2, m2, v2, mine, theirs, row0, prev):
    R, C = w2.shape
    h = mine.shape[0]
    tr = _slot_tile(h)
    nb = h // tr
    assert row0 % tr == 0
    first = row0 // tr

    def body(w_ref, m_ref, v_ref, a_ref, b_ref, *rest):
        g_ref, d_ref, nm_ref, nv_ref = rest[-4:]
        g = jnp.where(pl.program_id(0) == lax.axis_index("c"), a_ref[...], b_ref[...])
        g_ref[...] = g
        d_ref[...], nm_ref[...], nv_ref[...] = _adamw_math(w_ref[...], g, m_ref[...], v_ref[...])

    full = pl.BlockSpec((tr, C), lambda s, i: (first + s * nb + i, 0))
    half = pl.BlockSpec((tr, C), lambda s, i: (i, 0))
    extra = [] if prev is None else list(prev)
    return pl.pallas_call(
        body, name=name, grid=(2, nb), in_specs=[full, full, full, half, half] + [_ANY] * len(extra), out_specs=[full] * 4,
        out_shape=[jax.ShapeDtypeStruct((R, C), F32)] * 4, input_output_aliases={5 + k: k for k in range(len(extra))},
        compiler_params=_params(2))(w2, m2, v2, mine, theirs, *extra)


def _adamw(name, w, g, m, v):
    shape = w.shape
    two_d = (-1, shape[-1]) if w.ndim > 1 else (1, -1)
    w2, g2, m2, v2 = [t.reshape(two_d) for t in (w, g, m, v)]
    rows = w2.shape[0]
    tr = rows
    for cand in (256, 128, 64, 32, 16, 8):
        if rows % cand == 0:
            tr = cand
            break

    c = w2.shape[1]
    outs = _rowwise(name, _adamw_math, [w2, g2, m2, v2], [], [(c, F32)] * 3, tr=tr)
    return [o.reshape(shape) for o in outs]


_WEIGHT_ORDER = ("ada_w", "ada_b", "norm_mix_g", "norm_ffn_g", "gdn_w_in", "gdn_conv_w", "gdn_a_log", "gdn_dt_bias",
                 "gdn_norm_g", "gdn_w_out", "mla_w_in", "mla_q_norm_g", "mla_kv_norm_g", "mla_w_uq", "mla_w_ukv",
                 "mla_w_out", "ffn_w_gate", "ffn_w_up", "ffn_w_down", "final_norm_g")
_BIG = (("gdn_w_in", 2), ("gdn_w_out", 1), ("mla_w_in", 1), ("mla_w_uq", 2), ("mla_w_ukv", 2), ("mla_w_out", 1),
        ("ffn_w_gate", 2), ("ffn_w_up", 2), ("ffn_w_down", 1))
_SMALL_SHARDED = (("gdn_conv_w", 1), ("mla_q_norm_g", 1), ("mla_kv_norm_g", 1))


def _size(shape):
    n = 1
    for s in shape:
        n *= s
    return n


def _pack_rows_each(tensors):
    parts, offs, off = [], [], 0
    for t in tensors:
        flat = t.reshape(-1).astype(F32)
        rows = -(-flat.shape[0] // PACK_W)
        parts.append(jnp.pad(flat, (0, rows * PACK_W - flat.shape[0])).reshape(rows, PACK_W))
        offs.append(off)
        off += rows
    total = -(-off // 16) * 16
    pack = jnp.pad(parts[0], ((offs[0], total - offs[0] - parts[0].shape[0]), (0, 0)))
    for p, o in zip(parts[1:], offs[1:]):
        pack = pack + jnp.pad(p, ((o, total - o - p.shape[0]), (0, 0)))
    return pack, offs


def _unpack_rows_each(pack, shapes):
    lead = pack.shape[:-2]
    out, off = [], 0
    for shp in shapes:
        n = _size(shp)
        rows = -(-n // PACK_W)
        out.append(pack[..., off:off + rows, :].reshape(*lead, -1)[..., :n].reshape(*lead, *shp))
        off += rows
    return out


def _merge_chips(stacked, axis):
    moved = jnp.moveaxis(stacked, 0, axis)
    shp = list(moved.shape)
    return moved.reshape(shp[:axis] + [shp[axis] * shp[axis + 1]] + shp[axis + 2:])


def _my_shard(full, axis, chip):
    n = full.shape[axis] // 4
    return lax.dynamic_slice_in_dim(full, chip * n, n, axis)


def kernel(x, c, positions, ada_w, ada_b, norm_mix_g, norm_ffn_g, gdn_w_in, gdn_conv_w, gdn_a_log, gdn_dt_bias, gdn_norm_g, gdn_w_out, mla_w_in, mla_q_norm_g, mla_kv_norm_g, mla_w_uq, mla_w_ukv, mla_w_out, ffn_w_gate, ffn_w_up, ffn_w_down, final_norm_g, loss_target, m_ada_w, m_ada_b, m_norm_mix_g, m_norm_ffn_g, m_gdn_w_in, m_gdn_conv_w, m_gdn_a_log, m_gdn_dt_bias, m_gdn_norm_g, m_gdn_w_out, m_mla_w_in, m_mla_q_norm_g, m_mla_kv_norm_g, m_mla_w_uq, m_mla_w_ukv, m_mla_w_out, m_ffn_w_gate, m_ffn_w_up, m_ffn_w_down, m_final_norm_g, v_ada_w, v_ada_b, v_norm_mix_g, v_norm_ffn_g, v_gdn_w_in, v_gdn_conv_w, v_gdn_a_log, v_gdn_dt_bias, v_gdn_norm_g, v_gdn_w_out, v_mla_w_in, v_mla_q_norm_g, v_mla_kv_norm_g, v_mla_w_uq, v_mla_w_ukv, v_mla_w_out, v_ffn_w_gate, v_ffn_w_up, v_ffn_w_down, v_final_norm_g):
    w = dict(ada_w=ada_w, ada_b=ada_b, norm_mix_g=norm_mix_g, norm_ffn_g=norm_ffn_g, gdn_w_in=gdn_w_in, gdn_conv_w=gdn_conv_w,
             gdn_a_log=gdn_a_log, gdn_dt_bias=gdn_dt_bias, gdn_norm_g=gdn_norm_g, gdn_w_out=gdn_w_out, mla_w_in=mla_w_in,
             mla_q_norm_g=mla_q_norm_g, mla_kv_norm_g=mla_kv_norm_g, mla_w_uq=mla_w_uq, mla_w_ukv=mla_w_ukv,
             mla_w_out=mla_w_out, ffn_w_gate=ffn_w_gate, ffn_w_up=ffn_w_up, ffn_w_down=ffn_w_down, final_norm_g=final_norm_g)
    m = dict(ada_w=m_ada_w, ada_b=m_ada_b, norm_mix_g=m_norm_mix_g, norm_ffn_g=m_norm_ffn_g, gdn_w_in=m_gdn_w_in,
             gdn_conv_w=m_gdn_conv_w, gdn_a_log=m_gdn_a_log, gdn_dt_bias=m_gdn_dt_bias, gdn_norm_g=m_gdn_norm_g,
             gdn_w_out=m_gdn_w_out, mla_w_in=m_mla_w_in, mla_q_norm_g=m_mla_q_norm_g, mla_kv_norm_g=m_mla_kv_norm_g,
             mla_w_uq=m_mla_w_uq, mla_w_ukv=m_mla_w_ukv, mla_w_out=m_mla_w_out, ffn_w_gate=m_ffn_w_gate,
             ffn_w_up=m_ffn_w_up, ffn_w_down=m_ffn_w_down, final_norm_g=m_final_norm_g)
    v = dict(ada_w=v_ada_w, ada_b=v_ada_b, norm_mix_g=v_norm_mix_g, norm_ffn_g=v_norm_ffn_g, gdn_w_in=v_gdn_w_in,
             gdn_conv_w=v_gdn_conv_w, gdn_a_log=v_gdn_a_log, gdn_dt_bias=v_gdn_dt_bias, gdn_norm_g=v_gdn_norm_g,
             gdn_w_out=v_gdn_w_out, mla_w_in=v_mla_w_in, mla_q_norm_g=v_mla_q_norm_g, mla_kv_norm_g=v_mla_kv_norm_g,
             mla_w_uq=v_mla_w_uq, mla_w_ukv=v_mla_w_ukv, mla_w_out=v_mla_w_out, ffn_w_gate=v_ffn_w_gate,
             ffn_w_up=v_ffn_w_up, ffn_w_down=v_ffn_w_down, final_norm_g=v_final_norm_g)
    T = x.shape[1]
    ix, iy, ic = _me()
    chip = 2 * ix + iy
    seq = 2 * chip + ic
    n_dev = 8

    small_shapes = [w[n].shape for n, _ in _SMALL_SHARDED] + [c.shape]
    pack0, _ = _pack_rows_each([w[n] for n, _ in _SMALL_SHARDED] + [c])
    got0 = _unpack_rows_each(_allgather8("gather_small", pack0), small_shapes)
    small_full = {n: _merge_chips(g[0::2], ax) for (n, ax), g in zip(_SMALL_SHARDED, got0)}
    c_all = got0[-1].reshape(n_dev, D)

    big = [n for n, _ in _BIG]
    chip_arr = chip.astype(jnp.int32).reshape(1)
    got = []
    for l in range(DEPTH):
        names = _layer_weights(l)
        bufs = [_cast_into_slot(f"to_bf16_{n}{l}", w[n].reshape(-1, w[n].shape[-1]), chip_arr, j * w[n].shape[1], w[n].shape[1])
                for n, j in names]
        filled = _gather_weights_async(f"gather_weights{l}", 1 + l, bufs)
        got.append({n: b for (n, _), b in zip(names, filled)})
    W = _weights_to_kernel(got)
    P = _small_to_kernel(norm_mix_g, norm_ffn_g, final_norm_g, small_full["gdn_conv_w"], gdn_a_log, gdn_dt_bias,
                         gdn_norm_g, small_full["mla_q_norm_g"], small_full["mla_kv_norm_g"])

    c16 = jnp.pad(c_all, ((0, 16 - n_dev), (0, 0)))
    ca = _rowwise("cond_silu", lambda t: t * _sig(t), [c16], [], [(D, BF16)])[0]
    n_ada = ada_w.shape[2]
    mods = jnp.concatenate([_mm(f"ada_fwd{l}", ca, ada_w[l], "nn") for l in range(DEPTH)], axis=0)
    mods_all = _allgather4("gather_mod", mods).reshape(4, DEPTH, 16, n_ada)
    mod_mm = jnp.transpose(lax.dynamic_index_in_dim(mods_all, seq, axis=2, keepdims=False), (1, 0, 2)).reshape(DEPTH, 4 * n_ada)
    mod = _rowwise("mod_bias", lambda a, b: a + b, [mod_mm, ada_b], [], [(4 * n_ada, F32)])[0]

    core_chip = jnp.stack([ic, chip]).astype(jnp.int32)
    pending, in_flight = {}, []

    def reduce_group(layer, part, pieces):
        pending.update({(n, layer if n.startswith("ffn_") else layer // 2): g for n, g in pieces.items()})
        if part == "ffn" and layer > 0:
            return
        keys = list(pending)
        glist = [pending.pop(k) for k in keys]
        tag = f"{layer}{part}"
        theirs = _rs_split("grads_cores_" + tag, glist)
        both = [_pair_add(f"grads_pair_{n}{l}", g, t, core_chip) for (n, l), g, t in zip(keys, glist, theirs)]
        swapped = _rs_alltoall_async("grads_chips_" + tag, DEPTH + 1 + len(in_flight), [p for p, _ in both], [o for _, o in both])
        in_flight.append((tag, keys, swapped))

    dx, dmod, gP = _local_step(x.reshape(T, D), loss_target.reshape(T, D), positions.reshape(T, 1), mod, W, P, reduce_group)

    partials = [dmod, jnp.concatenate(gP["norm_mix_g"]), jnp.concatenate(gP["norm_ffn_g"]), gP["final_g"],
                jnp.stack([jnp.transpose(g) for g in gP["gdn_cw"]]), jnp.concatenate(gP["gdn_alog"])[:, :NH],
                jnp.concatenate(gP["gdn_dtb"])[:, :NH], jnp.concatenate(gP["gdn_ng"]), jnp.concatenate(gP["mla_qg"]),
                jnp.concatenate(gP["mla_kvg"]), gP["loss"][:, :1]]
    part_shapes = [p.shape for p in partials]
    ppack, _ = _pack_rows_each(partials)
    pall = _allgather8("gather_partials", ppack)
    psum = _sum_slots("sum_partials", pall, F32)
    (g_ada_b, g_norm_mix, g_norm_ffn, g_final, g_conv_full, g_alog, g_dtb, g_gdn_ng, g_qg_full, g_kvg_full,
     loss_sum) = _unpack_rows_each(psum, part_shapes)
    dmod_all = _unpack_rows_each(pall, part_shapes[:1])[0]

    grads = dict(ada_b=g_ada_b, norm_mix_g=g_norm_mix, norm_ffn_g=g_norm_ffn, final_norm_g=g_final.reshape(D),
                 gdn_conv_w=_my_shard(g_conv_full, 1, chip), gdn_a_log=g_alog, gdn_dt_bias=g_dtb, gdn_norm_g=g_gdn_ng,
                 mla_q_norm_g=_my_shard(g_qg_full, 1, chip), mla_kv_norm_g=_my_shard(g_kvg_full, 1, chip))

    ca_t = jnp.zeros((D, LANES), BF16).at[:, :16].set(jnp.transpose(ca))
    dm_mine = lax.dynamic_slice_in_dim(dmod_all, chip * n_ada, n_ada, axis=2)
    grads["ada_w"] = jnp.stack([
        _mm(f"ada_bwd{l}", ca_t, jnp.pad(dm_mine[:, l], ((0, LANES - n_dev), (0, 0))), "nn") for l in range(DEPTH)])

    delta, new_m, new_v = {}, {}, {}
    two_d = lambda t: t.reshape(-1, t.shape[-1])
    results = {}
    for tag, keys, swapped in in_flight:
        halves = [_sum_slots(f"grads_sum_{n}{l}", s, F32) for (n, l), s in zip(keys, swapped)]
        others = _rs_swap("grads_swap_" + tag, halves)
        for (n, l), mine, theirs in zip(keys, halves, others):
            results[n] = _adamw_piece(f"adamw_{n}{l}", two_d(w[n]), two_d(m[n]), two_d(v[n]), mine, theirs,
                                      l * w[n].shape[1], results.get(n))
    for n in big:
        grads[n], delta[n], new_m[n], new_v[n] = [t.reshape(w[n].shape) for t in results[n]]
    delta["ada_w"], new_m["ada_w"], new_v["ada_w"] = _adamw("adamw_ada_w", ada_w, grads["ada_w"], m_ada_w, v_ada_w)
    small_names = [n for n in _WEIGHT_ORDER if n not in delta]
    small_shapes = [w[n].shape for n in small_names]
    packs = [_pack_rows_each([d[n] for n in small_names])[0] for d in (w, grads, m, v)]
    for d, pk in zip((delta, new_m, new_v), _adamw("adamw_small", *packs)):
        for n, t in zip(small_names, _unpack_rows_each(pk, small_shapes)):
            d[n] = t

    loss = loss_sum.reshape(())
    return (loss, dx.reshape(1, T, D), *[grads[n] for n in _WEIGHT_ORDER], *[delta[n] for n in _WEIGHT_ORDER],
            *[new_m[n] for n in _WEIGHT_ORDER], *[new_v[n] for n in _WEIGHT_ORDER])
```

```python
import functools

import jax
import jax.numpy as jnp
from jax import lax
from jax.experimental import pallas as pl
from jax.experimental.pallas import tpu as pltpu
from jax.experimental.pallas import tpu_sc as plsc

F32 = jnp.float32
BF16 = jnp.bfloat16
HI = lax.Precision.HIGHEST
MESH = pl.DeviceIdType.MESH

D = 1024
DEPTH = 4
N_MOD = 6
NH = 8
HD = 128
CHUNK = 64
_GDN_HB = 4
GDN_QKV = 3 * NH * HD
GDN_INK = GDN_QKV + NH * HD + 2 * HD
Q_RANK, KV_RANK, ROPE = 384, 256, 64
MLA_INK = Q_RANK + KV_RANK + HD
DFF = 2816
EPS = 1e-6
ATT_SCALE = (HD + ROPE) ** -0.5
ROPE_THETA = 10000.0
LANES = 128
PACK_W = 1024

ADAM_LR, ADAM_B1, ADAM_B2, ADAM_EPS, ADAM_WD, ADAM_STEP = 0.001, 0.9, 0.999, 1e-08, 0.01, 10


H3 = "bf16x3"
B1 = "bf16"
HS = H3


def _dot(a, b, mode="nn", prec=None):
    dn = {"nn": (((1,), (0,)), ((), ())), "nt": (((1,), (1,)), ((), ())), "tn": (((0,), (0,)), ((), ()))}[mode]
    if prec == B1:
        return _dot(a.astype(BF16), b.astype(BF16), mode)
    if prec == H3:
        ah, bh = a.astype(BF16), b.astype(BF16)
        al, bl = (a - ah.astype(F32)).astype(BF16), (b - bh.astype(F32)).astype(BF16)
        return _dot(ah, bh, mode) + (_dot(ah, bl, mode) + _dot(al, bh, mode))
    return lax.dot_general(a, b, dn, precision=prec, preferred_element_type=F32)


def _sig(x):
    return 1.0 / (1.0 + jnp.exp(-x))


def _pick(n, cap):
    if n <= cap:
        return n
    best = None
    for d in range(LANES, cap + 1, LANES):
        if n % d == 0:
            best = d
    assert best is not None, (n, cap)
    return best


def _params(n_grid):
    return pltpu.CompilerParams(dimension_semantics=("arbitrary",) * n_grid, vmem_limit_bytes=56 * 1024 * 1024)


def _rowwise(name, fn, rows, consts, outs, sums=(), tr=256):
    first = rows[0][0] if isinstance(rows[0], tuple) else rows[0]
    T = first.shape[-2]
    tr = min(tr, T)
    while T % tr:
        tr //= 2
    nr, nc, no, ns = len(rows), len(consts), len(outs), len(sums)

    def body(*refs):
        res = fn(*[r[...] for r in refs[:nr + nc]])
        if not isinstance(res, (tuple, list)):
            res = (res,)
        o_refs = refs[nr + nc:nr + nc + no]
        s_refs = refs[nr + nc + no:]
        for r, val in zip(o_refs, res[:no]):
            r[...] = val.astype(r.dtype)
        if ns:
            @pl.when(pl.program_id(0) == 0)
            def _():
                for r in s_refs:
                    r[...] = jnp.zeros_like(r)
            for r, val in zip(s_refs, res[no:]):
                r[...] += val

    in_specs, args = [], []
    for a in rows:
        if isinstance(a, tuple):
            arr, width, cb = a
            in_specs.append(pl.BlockSpec((tr, width), lambda i, cb=cb: (i, cb)))
            args.append(arr)
        elif a.ndim == 3:
            in_specs.append(pl.BlockSpec((a.shape[0], tr, a.shape[2]), lambda i: (0, i, 0)))
            args.append(a)
        else:
            in_specs.append(pl.BlockSpec((tr, a.shape[1]), lambda i: (i, 0)))
            args.append(a)
    for a in consts:
        in_specs.append(pl.BlockSpec(a.shape, lambda i, nd=a.ndim: (0,) * nd))
        args.append(a)
    out_specs = [pl.BlockSpec((tr, w), lambda i: (i, 0)) for w, _ in outs]
    out_specs += [pl.BlockSpec((1, w), lambda i: (0, 0)) for w in sums]
    out_shape = [jax.ShapeDtypeStruct((T, w), dt) for w, dt in outs]
    out_shape += [jax.ShapeDtypeStruct((1, w), F32) for w in sums]
    res = pl.pallas_call(body, name=name, grid=(T // tr,), in_specs=in_specs, out_specs=out_specs,
                         out_shape=out_shape, compiler_params=_params(1))(*args)
    return res


def _mm(name, a, b, mode, out_dtype=F32, tm=512, tn=1024):
    if mode == "tn":
        K, M = a.shape
    else:
        M, K = a.shape
    N = b.shape[0] if mode == "nt" else b.shape[1]
    tm, tn = _pick(M, tm), _pick(N, tn)

    def body(a_ref, b_ref, o_ref):
        o_ref[...] = _dot(a_ref[...].astype(BF16), b_ref[...].astype(BF16), mode).astype(o_ref.dtype)

    a_spec = pl.BlockSpec((K, tm), lambda i, j: (0, i)) if mode == "tn" else pl.BlockSpec((tm, K), lambda i, j: (i, 0))
    b_spec = pl.BlockSpec((tn, K), lambda i, j: (j, 0)) if mode == "nt" else pl.BlockSpec((K, tn), lambda i, j: (0, j))
    return pl.pallas_call(body, name=name, grid=(M // tm, N // tn), in_specs=[a_spec, b_spec],
                          out_specs=pl.BlockSpec((tm, tn), lambda i, j: (i, j)),
                          out_shape=jax.ShapeDtypeStruct((M, N), out_dtype), compiler_params=_params(2))(a, b)


def _rms(x, eps=EPS):
    return lax.rsqrt(jnp.mean(x * x, axis=-1, keepdims=True) + eps)


def _norm_mod_fwd(name, x, g, scale, shift):
    def fn(x, g, scale, shift):
        return x * _rms(x) * g * (1.0 + scale) + shift
    return _rowwise(name, fn, [x], [g, scale, shift], [(D, BF16)])[0]


def _norm_mod_bwd(name, dh, x, dx_res, g, scale):
    def fn(dh, x, dx_res, g, scale):
        r = _rms(x)
        xh = x * r
        dxh = dh * (g * (1.0 + scale))
        dx = r * (dxh - xh * jnp.mean(dxh * xh, axis=-1, keepdims=True))
        dhx = dh * xh
        return (dx_res + dx, jnp.sum(dh, axis=0, keepdims=True), jnp.sum(dhx * g, axis=0, keepdims=True),
                jnp.sum(dhx * (1.0 + scale), axis=0, keepdims=True))
    return _rowwise(name, fn, [dh, x, dx_res], [g, scale], [(D, F32)], sums=[D, D, D])


def _residual_fwd(name, x, y, gate):
    def fn(x, y, gate):
        return x + gate * y
    return _rowwise(name, fn, [x, y], [gate], [(D, F32)])[0]


def _residual_bwd(name, dx, y, gate):
    def fn(dx, y, gate):
        return dx * gate, jnp.sum(dx * y, axis=0, keepdims=True)
    return _rowwise(name, fn, [dx, y], [gate], [(D, BF16)], sums=[D])


def _loss_head(x, target, g):
    def fn(x, t, g):
        r = _rms(x)
        xh = x * r
        err = xh * g - t
        loss = 0.5 * jnp.sum(jnp.mean(err * err, axis=-1, keepdims=True), axis=0, keepdims=True)
        dy = err * (1.0 / D)
        dxh = dy * g
        dx = r * (dxh - xh * jnp.mean(dxh * xh, axis=-1, keepdims=True))
        return dx, jnp.broadcast_to(loss, (1, LANES)), jnp.sum(dy * xh, axis=0, keepdims=True)
    return _rowwise("loss_head", fn, [x, target], [g], [(D, F32)], sums=[LANES, D])


def _ffn_up(name, h, wg, wu, layer, tm=512):
    T, n = h.shape[0], wg.shape[2]
    tm = min(tm, T)

    def body(h_ref, wg_ref, wu_ref, a_ref, b_ref, s_ref):
        h = h_ref[...]
        a = _dot(h, wg_ref[0], "nn")
        b = _dot(h, wu_ref[0], "nn")
        a_ref[0] = a
        b_ref[0] = b
        s_ref[0] = (a * _sig(a) * b).astype(s_ref.dtype)

    wspec = pl.BlockSpec((1, D, n), lambda ch, i: (ch, layer, 0))
    ospec = pl.BlockSpec((1, tm, n), lambda ch, i: (ch, i, 0))
    return pl.pallas_call(
        body, name=name, grid=(4, T // tm), in_specs=[pl.BlockSpec((tm, D), lambda ch, i: (i, 0)), wspec, wspec],
        out_specs=[ospec, ospec, ospec],
        out_shape=[jax.ShapeDtypeStruct((4, T, n), F32)] * 2 + [jax.ShapeDtypeStruct((4, T, n), BF16)],
        compiler_params=_params(2))(h, wg, wu)


def _ffn_down(name, s, wd, layer, tm=512):
    _, T, n = s.shape
    tm = min(tm, T)

    def body(s_ref, w_ref, y_ref):
        @pl.when(pl.program_id(1) == 0)
        def _():
            y_ref[...] = jnp.zeros_like(y_ref)
        y_ref[...] += _dot(s_ref[0], w_ref[0], "nn")

    return pl.pallas_call(
        body, name=name, grid=(T // tm, 4),
        in_specs=[pl.BlockSpec((1, tm, n), lambda i, ch: (ch, i, 0)), pl.BlockSpec((1, n, D), lambda i, ch: (ch, layer, 0))],
        out_specs=pl.BlockSpec((tm, D), lambda i, ch: (i, 0)), out_shape=jax.ShapeDtypeStruct((T, D), F32),
        compiler_params=_params(2))(s, wd)


def _ffn_down_bwd(name, dy, wd, a, b, layer, tm=512):
    _, T, n = a.shape
    tm = min(tm, T)

    def body(dy_ref, w_ref, a_ref, b_ref, da_ref, db_ref):
        ds = _dot(dy_ref[...], w_ref[0], "nt")
        a, b = a_ref[0], b_ref[0]
        sg = _sig(a)
        da_ref[0] = (ds * b * (sg * (1.0 + a * (1.0 - sg)))).astype(da_ref.dtype)
        db_ref[0] = (ds * (a * sg)).astype(db_ref.dtype)

    bspec = pl.BlockSpec((1, tm, n), lambda ch, i: (ch, i, 0))
    return pl.pallas_call(
        body, name=name, grid=(4, T // tm),
        in_specs=[pl.BlockSpec((tm, D), lambda ch, i: (i, 0)), pl.BlockSpec((1, n, D), lambda ch, i: (ch, layer, 0)), bspec, bspec],
        out_specs=[bspec, bspec], out_shape=[jax.ShapeDtypeStruct((4, T, n), BF16)] * 2,
        compiler_params=_params(2))(dy, wd, a, b)


def _ffn_down_dw(name, s, dy):
    _, T, n = s.shape

    def body(s_ref, dy_ref, o_ref):
        o_ref[0] = _dot(s_ref[0], dy_ref[...], "tn").astype(o_ref.dtype)

    return pl.pallas_call(
        body, name=name, grid=(4,),
        in_specs=[pl.BlockSpec((1, T, n), lambda ch: (ch, 0, 0)), pl.BlockSpec((T, D), lambda ch: (0, 0))],
        out_specs=pl.BlockSpec((1, n, D), lambda ch: (ch, 0, 0)), out_shape=jax.ShapeDtypeStruct((4, n, D), BF16),
        compiler_params=_params(1))(s, dy)


def _ffn_up_dw(name, h, da, db, tm=512):
    _, T, n = da.shape

    def body(h_ref, da_ref, db_ref, dg_ref, du_ref):
        h = h_ref[...]
        dg_ref[0] = _dot(h, da_ref[0], "tn").astype(dg_ref.dtype)
        du_ref[0] = _dot(h, db_ref[0], "tn").astype(du_ref.dtype)

    dspec = pl.BlockSpec((1, T, n), lambda ch, j: (ch, 0, 0))
    ospec = pl.BlockSpec((1, tm, n), lambda ch, j: (ch, j, 0))
    return pl.pallas_call(
        body, name=name, grid=(4, D // tm), in_specs=[pl.BlockSpec((T, tm), lambda ch, j: (0, j)), dspec, dspec],
        out_specs=[ospec, ospec], out_shape=[jax.ShapeDtypeStruct((4, D, n), BF16)] * 2,
        compiler_params=_params(2))(h, da, db)


def _ffn_up_dx(name, da, db, wg, wu, layer, tm=512):
    _, T, n = da.shape
    tm = min(tm, T)

    def body(da_ref, db_ref, wg_ref, wu_ref, o_ref):
        @pl.when(pl.program_id(1) == 0)
        def _():
            o_ref[...] = jnp.zeros_like(o_ref)
        o_ref[...] += _dot(da_ref[0], wg_ref[0], "nt") + _dot(db_ref[0], wu_ref[0], "nt")

    dspec = pl.BlockSpec((1, tm, n), lambda i, ch: (ch, i, 0))
    wspec = pl.BlockSpec((1, D, n), lambda i, ch: (ch, layer, 0))
    return pl.pallas_call(
        body, name=name, grid=(T // tm, 4), in_specs=[dspec, dspec, wspec, wspec],
        out_specs=pl.BlockSpec((tm, D), lambda i, ch: (i, 0)), out_shape=jax.ShapeDtypeStruct((T, D), F32),
        compiler_params=_params(2))(da, db, wg, wu)


def _shift_down(x, k):
    if k == 0:
        return x
    rows = lax.broadcasted_iota(jnp.int32, x.shape, 0)
    return jnp.where(rows >= k, pltpu.roll(x, k, 0), 0.0)


def _shift_up(x, k):
    if k == 0:
        return x
    T = x.shape[0]
    rows = lax.broadcasted_iota(jnp.int32, x.shape, 0)
    return jnp.where(rows < T - k, pltpu.roll(x, T - k, 0), 0.0)


def _conv_silu(x, w):
    c = w[0:1, :] * _shift_down(x, 3) + w[1:2, :] * _shift_down(x, 2) + w[2:3, :] * _shift_down(x, 1) + w[3:4, :] * x
    sg = _sig(c)
    return c, sg, c * sg


def _gdn_conv_fwd(name, proj, cw):
    T = proj.shape[0]

    def body(x_ref, w_ref, o_ref):
        j = pl.program_id(0)
        _, _, y = _conv_silu(x_ref[...], w_ref[...])
        r = lax.rsqrt(jnp.sum(y * y, axis=1, keepdims=True) + EPS)
        mult = jnp.where(j < NH, HD ** -0.5, 1.0)
        o_ref[...] = jnp.where(j < 2 * NH, y * (r * mult), y)

    return pl.pallas_call(body, name=name, grid=(3 * NH,),
                          in_specs=[pl.BlockSpec((T, HD), lambda j: (0, j)), pl.BlockSpec((4, HD), lambda j: (0, j))],
                          out_specs=pl.BlockSpec((T, HD), lambda j: (0, j)),
                          out_shape=jax.ShapeDtypeStruct((T, GDN_QKV), F32), compiler_params=_params(1))(proj, cw)


def _gdn_conv_bwd(name, proj, cw, dz):
    T = proj.shape[0]

    def body(x_ref, w_ref, dz_ref, dx_ref, dw_ref):
        j = pl.program_id(0)
        x, w, dz = x_ref[...], w_ref[...], dz_ref[...]
        c, sg, y = _conv_silu(x, w)
        r = lax.rsqrt(jnp.sum(y * y, axis=1, keepdims=True) + EPS)
        mult = jnp.where(j < NH, HD ** -0.5, 1.0)
        dyn = mult * (r * dz - (r * r * r) * y * jnp.sum(dz * y, axis=1, keepdims=True))
        dy = jnp.where(j < 2 * NH, dyn, dz)
        dc = dy * (sg * (1.0 + c * (1.0 - sg)))
        dx = w[0:1, :] * _shift_up(dc, 3) + w[1:2, :] * _shift_up(dc, 2) + w[2:3, :] * _shift_up(dc, 1) + w[3:4, :] * dc
        dx_ref[...] = dx.astype(dx_ref.dtype)
        for k in range(4):
            dw_ref[pl.ds(k, 1), :] = jnp.sum(dc * _shift_down(x, 3 - k), axis=0, keepdims=True)

    return pl.pallas_call(body, name=name, grid=(3 * NH,),
                          in_specs=[pl.BlockSpec((T, HD), lambda j: (0, j)), pl.BlockSpec((4, HD), lambda j: (0, j)),
                                    pl.BlockSpec((T, HD), lambda j: (0, j))],
                          out_specs=[pl.BlockSpec((T, HD), lambda j: (0, j)), pl.BlockSpec((4, HD), lambda j: (0, j))],
                          out_shape=[jax.ShapeDtypeStruct((T, GDN_QKV), BF16), jax.ShapeDtypeStruct((4, GDN_QKV), F32)],
                          compiler_params=_params(1))(proj, cw, dz)


def _softplus(z):
    return jnp.maximum(z, 0.0) + jnp.log(1.0 + jnp.exp(-jnp.abs(z)))


_AB_CB = GDN_INK // (2 * HD) - 1


def _gdn_gates_fwd(name, proj, alog, dtb):
    def fn(ab, alog, dtb):
        a, b = ab[:, :HD], ab[:, HD:]
        return -jnp.exp(alog) * _softplus(a + dtb), _sig(b)
    return _rowwise(name, fn, [(proj, 2 * HD, _AB_CB)], [alog, dtb], [(HD, F32), (HD, F32)])


def _gdn_gates_bwd(name, proj, dg_h, db_h, alog, dtb):
    def fn(ab, dg_h, db_h, alog, dtb):
        lane = lax.broadcasted_iota(jnp.int32, (1, HD), 1)
        dg = jnp.zeros(dg_h.shape[1:], F32)
        dbeta = jnp.zeros(dg_h.shape[1:], F32)
        for h in range(NH):
            oh = (lane == h).astype(F32)
            dg = dg + dg_h[h] * oh
            dbeta = dbeta + db_h[h] * oh
        a, b = ab[:, :HD], ab[:, HD:]
        z = a + dtb
        ea = jnp.exp(alog)
        beta = _sig(b)
        da = dg * (-ea) * _sig(z)
        db = dbeta * beta * (1.0 - beta)
        return (jnp.concatenate([da, db], axis=1), jnp.sum(dg * (-ea * _softplus(z)), axis=0, keepdims=True),
                jnp.sum(da, axis=0, keepdims=True))
    return _rowwise(name, fn, [(proj, 2 * HD, _AB_CB), dg_h, db_h], [alog, dtb], [(2 * HD, BF16)], sums=[HD, HD])


def _interleave(gens):
    gens = list(gens)
    results = [None] * len(gens)
    active = list(range(len(gens)))
    while active:
        for i in list(active):
            try:
                next(gens[i])
            except StopIteration as stop:
                results[i] = stop.value
                active.remove(i)
    return results


def _chunk_common(q, k, v, gblk, bblk, h):
    C = CHUNK
    lane = lax.broadcasted_iota(jnp.int32, (1, HD), 1)
    oh = (lane == h).astype(F32)
    g_col = jnp.sum(gblk * oh, axis=1, keepdims=True)
    beta = jnp.sum(bblk * oh, axis=1, keepdims=True)
    ri = lax.broadcasted_iota(jnp.int32, (C, C), 0)
    ci = lax.broadcasted_iota(jnp.int32, (C, C), 1)
    incl = ri >= ci
    strict = ri > ci
    eye = (ri == ci).astype(F32)
    gcb = _dot(incl.astype(F32), jnp.broadcast_to(g_col, (C, HD)), "nn", HI)
    yield
    gc = gcb[:, :C]
    gc_row = _dot(jnp.ones((C, C), F32), eye * gc, "nn", HI)
    yield
    decay = jnp.where(incl, jnp.exp(jnp.where(incl, gc - gc_row, 0.0)), 0.0)
    rows = lax.broadcasted_iota(jnp.int32, (C, HD), 0)
    gclb = jnp.sum(jnp.where(rows == C - 1, gcb, 0.0), axis=0, keepdims=True)
    eg = jnp.exp(gcb)
    egl = jnp.exp(gclb - gcb)
    gl = jnp.exp(gclb)
    kb = k * beta
    m1 = _dot(kb, k, "nt", HS)
    qk = _dot(q, k, "nt", HS)
    yield
    L = jnp.where(strict, m1 * decay, 0.0)
    nl = -L
    tinv = eye + nl
    p = nl
    for _ in range(5):
        p = _dot(p, p, "nn", H3)
        yield
        tinv = tinv + _dot(tinv, p, "nn", H3)
    vb = v * beta
    kbg = kb * eg
    yield
    u = _dot(tinv, vb, "nn", HS)
    w = _dot(tinv, kbg, "nn", HS)
    yield
    attn = jnp.where(incl, qk * decay, 0.0)
    return dict(beta=beta, incl=incl, strict=strict, decay=decay, eg=eg, egl=egl, gl=gl, kb=kb, m1=m1, tinv=tinv,
                kbg=kbg, u=u, w=w, qk=qk, attn=attn, q_dec=q * eg, k_dec=k * egl, rows=rows, oh=oh)


def _gdn_chunk_fwd(name, qkv, g, beta):
    T = qkv.shape[0]
    N = T // CHUNK

    hb = _GDN_HB
    w = hb * HD

    def body(q_ref, k_ref, v_ref, g_ref, b_ref, o_ref, st_ref, S):
        hg, n = pl.program_id(0), pl.program_id(1)

        @pl.when(n == 0)
        def _():
            S[...] = jnp.zeros_like(S)

        gblk, bblk = g_ref[...], b_ref[...]

        def one_head(i, q, k, v, s):
            c = yield from _chunk_common(q, k, v, gblk, bblk, hg * hb + i)
            v_new = c["u"] - _dot(c["w"], s, "nn", HS)
            qs = _dot(c["q_dec"], s, "nn", HS)
            yield
            o = qs + _dot(c["attn"], v_new, "nn", HS)
            return o, s * c["gl"] + _dot(c["k_dec"], v_new, "tn", HS)

        sls = [slice(i * HD, (i + 1) * HD) for i in range(hb)]
        states = [S[i] for i in range(hb)]
        res = _interleave(one_head(i, q_ref[:, sls[i]], k_ref[:, sls[i]], v_ref[:, sls[i]], states[i]) for i in range(hb))
        for i, (o, s_new) in enumerate(res):
            st_ref[i, 0] = states[i]
            o_ref[:, sls[i]] = o
            S[i] = s_new

    blk = lambda off: pl.BlockSpec((CHUNK, w), lambda h, n, off=off: (n, off + h))
    gspec = pl.BlockSpec((CHUNK, HD), lambda h, n: (n, 0))
    return pl.pallas_call(
        body, name=name, grid=(NH // hb, N), in_specs=[blk(0), blk(NH // hb), blk(2 * NH // hb), gspec, gspec],
        out_specs=[pl.BlockSpec((CHUNK, w), lambda h, n: (n, h)), pl.BlockSpec((hb, 1, HD, HD), lambda h, n: (h, n, 0, 0))],
        out_shape=[jax.ShapeDtypeStruct((T, NH * HD), F32), jax.ShapeDtypeStruct((NH, N, HD, HD), F32)],
        scratch_shapes=[pltpu.VMEM((hb, HD, HD), F32)], compiler_params=_params(2))(qkv, qkv, qkv, g, beta)


def _gdn_chunk_bwd(name, qkv, g, beta, states, do):
    T = qkv.shape[0]
    N = T // CHUNK
    C = CHUNK

    hb = _GDN_HB
    w = hb * HD

    def body(q_ref, k_ref, v_ref, g_ref, b_ref, st_ref, do_ref, dq_ref, dk_ref, dv_ref, dg_ref, db_ref, dS):
        hg, n = pl.program_id(0), pl.program_id(1)

        @pl.when(n == 0)
        def _():
            dS[...] = jnp.zeros_like(dS)

        gblk, bblk = g_ref[...], b_ref[...]
        sls = [slice(i * HD, (i + 1) * HD) for i in range(hb)]
        res = _interleave(one_head(hg * hb + i, gblk, bblk, q_ref[:, sls[i]], k_ref[:, sls[i]], v_ref[:, sls[i]],
                                   st_ref[i, 0], do_ref[:, sls[i]], dS[i]) for i in range(hb))
        for i, (dq, dk, dv, dg, db, ds_new) in enumerate(res):
            dq_ref[:, sls[i]] = dq
            dk_ref[:, sls[i]] = dk
            dv_ref[:, sls[i]] = dv
            dg_ref[i] = dg
            db_ref[i] = db
            dS[i] = ds_new

    def one_head(h, gblk, bblk, q, k, v, s, do, ds):
        c = yield from _chunk_common(q, k, v, gblk, bblk, h)
        eg, egl, gl, beta, decay, tinv = c["eg"], c["egl"], c["gl"], c["beta"], c["decay"], c["tinv"]
        v_new = c["u"] - _dot(c["w"], s, "nn", HS)
        dq_dec = _dot(do, s, "nt", HS)
        yield
        dv_new = _dot(c["attn"], do, "tn", HS) + _dot(c["k_dec"], ds, "nn", HS)
        dk_dec = _dot(v_new, ds, "nt", HS)
        dgl = jnp.sum(jnp.sum(s * ds, axis=1, keepdims=True), axis=0, keepdims=True)
        yield
        ds_new = ds * gl + _dot(c["q_dec"], do, "tn", HS) - _dot(c["w"], dv_new, "tn", HS)
        dattn = jnp.where(c["incl"], _dot(do, v_new, "nt", HS), 0.0)
        dw = -_dot(dv_new, s, "nt", HS)
        yield
        dvb = _dot(tinv, dv_new, "tn", HS)
        dkbg = _dot(tinv, dw, "tn", HS)
        yield
        dA = -(_dot(dvb, c["u"], "nt", HS) + _dot(dkbg, c["w"], "nt", HS))
        yield
        dL = jnp.where(c["strict"], dA, 0.0)
        dm1 = dL * decay
        dqk = dattn * decay
        xdec = (dL * c["m1"] + dattn * c["qk"]) * decay
        dkb = _dot(dm1, k, "nn", HS) + dkbg * eg
        dk = _dot(dm1, c["kb"], "tn", HS) + _dot(dqk, q, "tn", HS) + dk_dec * egl + dkb * beta
        dq = _dot(dqk, k, "nn", HS) + dq_dec * eg
        yield
        dkd_kd = jnp.sum(dk_dec * c["k_dec"], axis=1, keepdims=True)
        dgc = (jnp.sum(xdec, axis=1, keepdims=True) - _dot(xdec, jnp.ones((C, HD), F32), "tn", HS)
               + jnp.sum(dq_dec * c["q_dec"], axis=1, keepdims=True) - dkd_kd
               + jnp.sum(dkbg * c["kbg"], axis=1, keepdims=True))
        dgcl = jnp.sum(dkd_kd, axis=0, keepdims=True) + dgl * gl
        dgc = dgc + jnp.where(c["rows"] == C - 1, dgcl, 0.0)
        ri = lax.broadcasted_iota(jnp.int32, (C, C), 0)
        ci = lax.broadcasted_iota(jnp.int32, (C, C), 1)
        dg = _dot((ci >= ri).astype(F32), dgc, "nn", HI)
        db = jnp.broadcast_to(jnp.sum(dkb * k, axis=1, keepdims=True) + jnp.sum(dvb * v, axis=1, keepdims=True), (C, HD))
        return dq, dk, dvb * beta, dg, db, ds_new

    blk = lambda off: pl.BlockSpec((C, w), lambda h, n, off=off: (N - 1 - n, off + h))
    gspec = pl.BlockSpec((C, HD), lambda h, n: (N - 1 - n, 0))
    ospec = pl.BlockSpec((C, w), lambda h, n: (N - 1 - n, h))
    hspec = pl.BlockSpec((hb, C, HD), lambda h, n: (h, N - 1 - n, 0))
    return pl.pallas_call(
        body, name=name, grid=(NH // hb, N),
        in_specs=[blk(0), blk(NH // hb), blk(2 * NH // hb), gspec, gspec,
                  pl.BlockSpec((hb, 1, HD, HD), lambda h, n: (h, N - 1 - n, 0, 0)), ospec],
        out_specs=[ospec, ospec, ospec, hspec, hspec],
        out_shape=[jax.ShapeDtypeStruct((T, NH * HD), F32)] * 3 + [jax.ShapeDtypeStruct((NH, T, HD), F32)] * 2,
        scratch_shapes=[pltpu.VMEM((hb, HD, HD), F32)], compiler_params=_params(2))(qkv, qkv, qkv, g, beta, states, do)


_GATE_CB = GDN_QKV // (NH * HD)


def _gdn_gated_norm_fwd(name, o, proj, ng):
    def fn(o, gate, ng):
        outs = []
        for h in range(NH):
            sl = slice(h * HD, (h + 1) * HD)
            oh, gh = o[:, sl], gate[:, sl]
            outs.append(oh * _rms(oh) * ng * (gh * _sig(gh)))
        return jnp.concatenate(outs, axis=1)
    return _rowwise(name, fn, [o, (proj, NH * HD, _GATE_CB)], [ng], [(NH * HD, BF16)])[0]


def _gdn_gated_norm_bwd(name, don, o, proj, ng):
    def fn(don, o, gate, ng):
        dos, dgs = [], []
        dng = jnp.zeros((1, HD), F32)
        for h in range(NH):
            sl = slice(h * HD, (h + 1) * HD)
            oh, gh, dh = o[:, sl], gate[:, sl], don[:, sl]
            r = _rms(oh)
            xh = oh * r
            sg = _sig(gh)
            dn = dh * (gh * sg)
            dgs.append(dh * (xh * ng) * (sg * (1.0 + gh * (1.0 - sg))))
            dng = dng + jnp.sum(dn * xh, axis=0, keepdims=True)
            dxh = dn * ng
            dos.append(r * (dxh - xh * jnp.mean(dxh * xh, axis=-1, keepdims=True)))
        return jnp.concatenate(dos, axis=1), jnp.concatenate(dgs, axis=1), dng
    return _rowwise(name, fn, [don, o, (proj, NH * HD, _GATE_CB)], [ng], [(NH * HD, F32), (NH * HD, BF16)], sums=[HD])


def _rot(x):
    lane = lax.broadcasted_iota(jnp.int32, x.shape, 1)
    return jnp.where(lane < ROPE // 2, -pltpu.roll(x, HD - ROPE // 2, 1), pltpu.roll(x, ROPE // 2, 1))


def _rot_t(x):
    lane = lax.broadcasted_iota(jnp.int32, x.shape, 1)
    return jnp.where(lane < ROPE // 2, pltpu.roll(x, HD - ROPE // 2, 1), -pltpu.roll(x, ROPE // 2, 1))


def _rope_tables(pos_col):
    lane = jnp.arange(HD)
    inv_freq = ROPE_THETA ** (-(2.0 * (lane % (ROPE // 2)).astype(F32)) / ROPE)
    inv_freq = jnp.where(lane < ROPE, inv_freq, 0.0).astype(F32)[None, :]
    valid = (lane < ROPE).astype(F32)[None, :]

    def fn(pos, inv_freq, valid):
        ang = pos.astype(F32) * inv_freq
        return jnp.cos(ang) * valid, jnp.sin(ang) * valid
    return _rowwise("rope_tables", fn, [pos_col], [inv_freq, valid], [(HD, F32), (HD, F32)])


def _mla_pre_fwd(name, proj, cos, sin, qg, kvg):
    def fn(p, cos, sin, qg, kvg):
        cq, ckv, kr = p[:, :Q_RANK], p[:, Q_RANK:Q_RANK + KV_RANK], p[:, Q_RANK + KV_RANK:]
        return cq * _rms(cq) * qg, ckv * _rms(ckv) * kvg, kr * cos + _rot(kr) * sin
    return _rowwise(name, fn, [proj, cos, sin], [qg, kvg], [(Q_RANK, BF16), (KV_RANK, BF16), (HD, BF16)])


def _rms_bwd(dy, x, g):
    r = _rms(x)
    xh = x * r
    dxh = dy * g
    return r * (dxh - xh * jnp.mean(dxh * xh, axis=-1, keepdims=True)), jnp.sum(dy * xh, axis=0, keepdims=True)


def _mla_pre_bwd(name, proj, dcqn, dckvn, dkr, cos, sin, qg, kvg):
    def fn(p, dcqn, dckvn, dkr, cos, sin, qg, kvg):
        cq, ckv = p[:, :Q_RANK], p[:, Q_RANK:Q_RANK + KV_RANK]
        dcq, dqg = _rms_bwd(dcqn, cq, qg)
        dckv, dkvg = _rms_bwd(dckvn, ckv, kvg)
        dkr_pre = dkr * cos + _rot_t(dkr * sin)
        return jnp.concatenate([dcq, dckv, dkr_pre], axis=1), dqg, dkvg
    return _rowwise(name, fn, [proj, dcqn, dckvn, dkr, cos, sin], [qg, kvg], [(MLA_INK, BF16)], sums=[Q_RANK, KV_RANK])


def _mla_q_fwd(name, q, cos, sin):
    def fn(qn, qr, cos, sin):
        outs = []
        for h in range(NH):
            x = qr[:, h * HD:(h + 1) * HD]
            outs.append(x * cos + _rot(x) * sin)
        return qn, jnp.concatenate(outs, axis=1)
    return _rowwise(name, fn, [(q, NH * HD, 0), (q, NH * HD, 1), cos, sin], [], [(NH * HD, BF16), (NH * HD, BF16)])


def _mla_q_bwd(name, dqn, dqr, cos, sin):
    def fn(dqn, dqr, cos, sin):
        outs = [dqn]
        for h in range(NH):
            z = dqr[:, h * HD:(h + 1) * HD]
            outs.append(z * cos + _rot_t(z * sin))
        return jnp.concatenate(outs, axis=1)
    return _rowwise(name, fn, [dqn, dqr, cos, sin], [], [(2 * NH * HD, BF16)])[0]


def _att_probs(qn, qr, kn, kr, row0):
    s = (_dot(qn, kn, "nt") + _dot(qr, kr, "nt")) * ATT_SCALE
    qpos = row0 + lax.broadcasted_iota(jnp.int32, s.shape, 0)
    kpos = lax.broadcasted_iota(jnp.int32, s.shape, 1)
    s = jnp.where(kpos <= qpos, s, -1e30)
    p = jnp.exp(s - jnp.max(s, axis=1, keepdims=True))
    return p / jnp.sum(p, axis=1, keepdims=True)


def _mla_attn_fwd(name, qn, qr, kv, kr, tq=256):
    T = qn.shape[0]
    tq = min(tq, T)

    def body(qn_ref, qr_ref, kn_ref, v_ref, kr_ref, o_ref):
        i = pl.program_id(1)
        for blk in range(T // tq):
            @pl.when(i == blk)
            def _(blk=blk):
                keys = pl.ds(0, (blk + 1) * tq)
                p = _att_probs(qn_ref[...], qr_ref[...], kn_ref[keys, :], kr_ref[keys, :], blk * tq)
                o_ref[...] = _dot(p.astype(BF16), v_ref[keys, :], "nn").astype(o_ref.dtype)

    qspec = pl.BlockSpec((tq, HD), lambda h, i: (i, h))
    return pl.pallas_call(
        body, name=name, grid=(NH, T // tq),
        in_specs=[qspec, qspec, pl.BlockSpec((T, HD), lambda h, i: (0, h)), pl.BlockSpec((T, HD), lambda h, i: (0, NH + h)),
                  pl.BlockSpec((T, HD), lambda h, i: (0, 0))],
        out_specs=qspec, out_shape=jax.ShapeDtypeStruct((T, NH * HD), BF16), compiler_params=_params(2))(qn, qr, kv, kv, kr)


def _mla_attn_bwd(name, qn, qr, kv, kr, do, tq=256):
    T = qn.shape[0]
    tq = min(tq, T)

    def body(qn_ref, qr_ref, kn_ref, v_ref, kr_ref, do_ref, dqn_ref, dqr_ref, dkn_ref, dv_ref, dkr_ref):
        h, i = pl.program_id(0), pl.program_id(1)

        @pl.when(i == 0)
        def _():
            dkn_ref[...] = jnp.zeros_like(dkn_ref)
            dv_ref[...] = jnp.zeros_like(dv_ref)

        @pl.when((i == 0) & (h == 0))
        def _():
            dkr_ref[...] = jnp.zeros_like(dkr_ref)

        for blk in range(T // tq):
            @pl.when(i == blk)
            def _(blk=blk):
                keys = pl.ds(0, (blk + 1) * tq)
                qn, qr, do = qn_ref[...], qr_ref[...], do_ref[...]
                kn, kr, v = kn_ref[keys, :], kr_ref[keys, :], v_ref[keys, :]
                p = _att_probs(qn, qr, kn, kr, blk * tq)
                dp = _dot(do, v, "nt")
                ds = (p * (dp - jnp.sum(p * dp, axis=1, keepdims=True)) * ATT_SCALE).astype(BF16)
                dqn_ref[...] = _dot(ds, kn, "nn")
                dqr_ref[...] = _dot(ds, kr, "nn")
                dkn_ref[keys, :] += _dot(ds, qn, "tn")
                dkr_ref[keys, :] += _dot(ds, qr, "tn")
                dv_ref[keys, :] += _dot(p.astype(BF16), do, "tn")

    qspec = pl.BlockSpec((tq, HD), lambda h, i: (i, h))
    kspec = pl.BlockSpec((T, HD), lambda h, i: (0, h))
    return pl.pallas_call(
        body, name=name, grid=(NH, T // tq),
        in_specs=[qspec, qspec, kspec, pl.BlockSpec((T, HD), lambda h, i: (0, NH + h)),
                  pl.BlockSpec((T, HD), lambda h, i: (0, 0)), qspec],
        out_specs=[qspec, qspec, kspec, kspec, pl.BlockSpec((T, HD), lambda h, i: (0, 0))],
        out_shape=[jax.ShapeDtypeStruct((T, NH * HD), F32)] * 4 + [jax.ShapeDtypeStruct((T, HD), F32)],
        compiler_params=_params(2))(qn, qr, kv, kv, kr, do)


def _mod_rows(mod, layer):
    return [mod[layer:layer + 1, i * D:(i + 1) * D] for i in range(N_MOD)]


def _local_step(x, target, pos_col, mod, weights_of, P, on_grads):
    cos, sin = _rope_tables(pos_col)
    saved = []
    for l in range(DEPTH):
        j = l // 2
        sh_m, sc_m, ga_m, sh_f, sc_f, ga_f = _mod_rows(mod, l)
        s = dict(x0=x)
        h = _norm_mod_fwd(f"norm_mix{l}", x, P["norm_mix_g"][l:l + 1], sc_m, sh_m)
        W = weights_of(l, h)
        s.update(h=h, W=W)
        if l % 2 == 0:
            proj = _mm(f"gdn_in{j}", h, W["gdn_in"], "nn")
            qkv = _gdn_conv_fwd(f"gdn_conv{j}", proj, P["gdn_cw"][j])
            g, beta = _gdn_gates_fwd(f"gdn_gates{j}", proj, P["gdn_alog"][j], P["gdn_dtb"][j])
            o, states = _gdn_chunk_fwd(f"gdn_chunk{j}", qkv, g, beta)
            on = _gdn_gated_norm_fwd(f"gdn_gnorm{j}", o, proj, P["gdn_ng"][j])
            y = _mm(f"gdn_out{j}", on, W["gdn_out"], "nn")
            s.update(proj=proj, qkv=qkv, g=g, beta=beta, o=o, states=states, on=on)
        else:
            proj = _mm(f"mla_in{j}", h, W["mla_in"], "nn")
            cqn, ckvn, kr = _mla_pre_fwd(f"mla_pre{j}", proj, cos, sin, P["mla_qg"][j], P["mla_kvg"][j])
            q = _mm(f"mla_uq{j}", cqn, W["mla_uq"], "nn")
            kv = _mm(f"mla_ukv{j}", ckvn, W["mla_ukv"], "nn", out_dtype=BF16)
            qn, qr = _mla_q_fwd(f"mla_q{j}", q, cos, sin)
            o = _mla_attn_fwd(f"mla_attn{j}", qn, qr, kv, kr)
            y = _mm(f"mla_out{j}", o, W["mla_out"], "nn")
            s.update(proj=proj, cqn=cqn, ckvn=ckvn, kr=kr, kv=kv, qn=qn, qr=qr, o=o)
        s["y"] = y
        x = _residual_fwd(f"res_mix{l}", x, y, ga_m)
        s["x1"] = x
        h2 = _norm_mod_fwd(f"norm_ffn{l}", x, P["norm_ffn_g"][l:l + 1], sc_f, sh_f)
        fa, fb, sw = _ffn_up(f"ffn_up{l}", h2, W["ffn_g"], W["ffn_u"], 0)
        yf = _ffn_down(f"ffn_down{l}", sw, W["ffn_d"], 0)
        x = _residual_fwd(f"res_ffn{l}", x, yf, ga_f)
        s.update(h2=h2, fa=fa, fb=fb, sw=sw, yf=yf)
        saved.append(s)

    dx, loss, d_final = _loss_head(x, target, P["final_g"])
    gP = dict(loss=loss, final_g=d_final, norm_mix_g=[None] * DEPTH, norm_ffn_g=[None] * DEPTH,
              gdn_cw=[None] * 2, gdn_alog=[None] * 2, gdn_dtb=[None] * 2, gdn_ng=[None] * 2,
              mla_qg=[None] * 2, mla_kvg=[None] * 2)
    dmod = [None] * DEPTH
    for l in reversed(range(DEPTH)):
        j = l // 2
        s = saved[l]
        W = s["W"]
        sh_m, sc_m, ga_m, sh_f, sc_f, ga_f = _mod_rows(mod, l)
        dyf, d_ga_f = _residual_bwd(f"res_ffn_b{l}", dx, s["yf"], ga_f)
        da, db = _ffn_down_bwd(f"ffn_down_dx{l}", dyf, W["ffn_d"], s["fa"], s["fb"], 0)
        g_down = _ffn_down_dw(f"ffn_down_dw{l}", s["sw"], dyf)
        g_gate, g_up = _ffn_up_dw(f"ffn_up_dw{l}", s["h2"], da, db)
        on_grads(l, "ffn", dict(ffn_w_gate=g_gate, ffn_w_up=g_up, ffn_w_down=g_down))
        dh2 = _ffn_up_dx(f"ffn_up_dx{l}", da, db, W["ffn_g"], W["ffn_u"], 0)
        dx, d_sh_f, d_sc_f, gP["norm_ffn_g"][l] = _norm_mod_bwd(f"norm_ffn_b{l}", dh2, s["x1"], dx,
                                                                 P["norm_ffn_g"][l:l + 1], sc_f)
        dy, d_ga_m = _residual_bwd(f"res_mix_b{l}", dx, s["y"], ga_m)
        if l % 2 == 0:
            don = _mm(f"gdn_out_dx{j}", dy, W["gdn_out"], "nt")
            g_out = _mm(f"gdn_out_dw{j}", s["on"], dy, "tn", out_dtype=BF16)
            do, dgate, gP["gdn_ng"][j] = _gdn_gated_norm_bwd(f"gdn_gnorm_b{j}", don, s["o"], s["proj"], P["gdn_ng"][j])
            dq, dk, dv, dg_h, db_h = _gdn_chunk_bwd(f"gdn_chunk_b{j}", s["qkv"], s["g"], s["beta"], s["states"], do)
            dab_, gP["gdn_alog"][j], gP["gdn_dtb"][j] = _gdn_gates_bwd(f"gdn_gates_b{j}", s["proj"], dg_h, db_h,
                                                                        P["gdn_alog"][j], P["gdn_dtb"][j])
            dpre, gP["gdn_cw"][j] = _gdn_conv_bwd(f"gdn_conv_b{j}", s["proj"], P["gdn_cw"][j],
                                                  jnp.concatenate([dq, dk, dv], axis=1))
            dproj = jnp.concatenate([dpre, dgate, dab_], axis=1)
            g_in = _mm(f"gdn_in_dw{j}", s["h"], dproj, "tn", out_dtype=BF16)
            on_grads(l, "mix", dict(gdn_w_in=_uncols(_gdn_in_from_kernel(g_in)), gdn_w_out=_unrows(g_out)))
            dh = _mm(f"gdn_in_dx{j}", dproj, W["gdn_in"], "nt")
        else:
            do = _mm(f"mla_out_dx{j}", dy, W["mla_out"], "nt", out_dtype=BF16)
            g_out = _mm(f"mla_out_dw{j}", s["o"], dy, "tn", out_dtype=BF16)
            dqn, dqr, dkn, dv, dkr = _mla_attn_bwd(f"mla_attn_b{j}", s["qn"], s["qr"], s["kv"], s["kr"], do)
            dq = _mla_q_bwd(f"mla_q_b{j}", dqn, dqr, cos, sin)
            dkv = jnp.concatenate([dkn, dv], axis=1)
            g_uq = _mm(f"mla_uq_dw{j}", s["cqn"], dq, "tn", out_dtype=BF16)
            dcqn = _mm(f"mla_uq_dx{j}", dq, W["mla_uq"], "nt")
            g_ukv = _mm(f"mla_ukv_dw{j}", s["ckvn"], dkv, "tn", out_dtype=BF16)
            dckvn = _mm(f"mla_ukv_dx{j}", dkv, W["mla_ukv"], "nt")
            dproj, gP["mla_qg"][j], gP["mla_kvg"][j] = _mla_pre_bwd(f"mla_pre_b{j}", s["proj"], dcqn, dckvn, dkr, cos, sin,
                                                                     P["mla_qg"][j], P["mla_kvg"][j])
            g_in = _mm(f"mla_in_dw{j}", s["h"], dproj, "tn", out_dtype=BF16)
            on_grads(l, "mix", dict(mla_w_in=_unrows(g_in[:, :Q_RANK + KV_RANK + ROPE]), mla_w_uq=_uncols(_mla_uq_from_kernel(g_uq)),
                                    mla_w_ukv=_uncols(_mla_ukv_from_kernel(g_ukv)), mla_w_out=_unrows(g_out)))
            dh = _mm(f"mla_in_dx{j}", dproj, W["mla_in"], "nt")
        dx, d_sh_m, d_sc_m, gP["norm_mix_g"][l] = _norm_mod_bwd(f"norm_mix_b{l}", dh, s["x0"], dx,
                                                                 P["norm_mix_g"][l:l + 1], sc_m)
        dmod[l] = jnp.concatenate([d_sh_m, d_sc_m, d_ga_m, d_sh_f, d_sc_f, d_ga_f], axis=1)
    return dx, jnp.concatenate(dmod, axis=0), gP


def _pad_cols(a, width):
    return jnp.pad(a, ((0, 0), (0, width - a.shape[1])))


def _gdn_in_to_kernel(w):
    m = GDN_QKV + NH * HD
    return jnp.concatenate([w[:, :m], _pad_cols(w[:, m:m + NH], HD), _pad_cols(w[:, m + NH:], HD)], axis=1)


def _gdn_in_from_kernel(g):
    m = GDN_QKV + NH * HD
    return jnp.concatenate([g[:, :m], g[:, m:m + NH], g[:, m + HD:m + HD + NH]], axis=1)


def _mla_uq_to_kernel(w):
    w3 = w.reshape(Q_RANK, NH, HD + ROPE)
    rope = jnp.pad(w3[:, :, HD:], ((0, 0), (0, 0), (0, HD - ROPE)))
    return jnp.concatenate([w3[:, :, :HD].reshape(Q_RANK, NH * HD), rope.reshape(Q_RANK, NH * HD)], axis=1)


def _mla_uq_from_kernel(g):
    gn = g[:, :NH * HD].reshape(Q_RANK, NH, HD)
    gr = g[:, NH * HD:].reshape(Q_RANK, NH, HD)[:, :, :ROPE]
    return jnp.concatenate([gn, gr], axis=2).reshape(Q_RANK, NH * (HD + ROPE))


def _mla_ukv_to_kernel(w):
    w3 = w.reshape(KV_RANK, NH, 2 * HD)
    return jnp.concatenate([w3[:, :, :HD].reshape(KV_RANK, NH * HD), w3[:, :, HD:].reshape(KV_RANK, NH * HD)], axis=1)


def _mla_ukv_from_kernel(g):
    gk = g[:, :NH * HD].reshape(KV_RANK, NH, HD)
    gv = g[:, NH * HD:].reshape(KV_RANK, NH, HD)
    return jnp.concatenate([gk, gv], axis=2).reshape(KV_RANK, NH * 2 * HD)


def _cols(t):
    return jnp.moveaxis(t, 0, 1).reshape(t.shape[1], -1)


def _uncols(g):
    return jnp.moveaxis(g.reshape(g.shape[0], 4, -1), 1, 0)


def _rows(t):
    return t.reshape(-1, t.shape[2])


def _unrows(g):
    return g.reshape(4, -1, g.shape[1])


def _layer_weights(layer):
    mixer = ("gdn_w_in", "gdn_w_out") if layer % 2 == 0 else ("mla_w_in", "mla_w_uq", "mla_w_ukv", "mla_w_out")
    return [(n, layer // 2) for n in mixer] + [(n, layer) for n in ("ffn_w_gate", "ffn_w_up", "ffn_w_down")]


def _weights_to_kernel(layer, g):
    out = dict(ffn_g=g["ffn_w_gate"], ffn_u=g["ffn_w_up"], ffn_d=g["ffn_w_down"])
    if layer % 2 == 0:
        out.update(gdn_in=_gdn_in_to_kernel(_cols(g["gdn_w_in"])), gdn_out=_rows(g["gdn_w_out"]))
    else:
        out.update(mla_in=_pad_cols(_rows(g["mla_w_in"]), MLA_INK), mla_uq=_mla_uq_to_kernel(_cols(g["mla_w_uq"])),
                   mla_ukv=_mla_ukv_to_kernel(_cols(g["mla_w_ukv"])), mla_out=_rows(g["mla_w_out"]))
    return out


def _small_to_kernel(norm_mix_g, norm_ffn_g, final_norm_g, gdn_conv_w, gdn_a_log, gdn_dt_bias, gdn_norm_g, q_norm_g, kv_norm_g):
    return dict(
        norm_mix_g=norm_mix_g, norm_ffn_g=norm_ffn_g, final_g=final_norm_g.reshape(1, D),
        gdn_cw=[jnp.transpose(gdn_conv_w[j]) for j in range(2)],
        gdn_alog=[_pad_cols(gdn_a_log[j:j + 1], HD) for j in range(2)],
        gdn_dtb=[_pad_cols(gdn_dt_bias[j:j + 1], HD) for j in range(2)],
        gdn_ng=[gdn_norm_g[j:j + 1] for j in range(2)],
        mla_qg=[q_norm_g[j:j + 1] for j in range(2)],
        mla_kvg=[kv_norm_g[j:j + 1] for j in range(2)],
    )


_CHIP_FLIPS = ((1, 0), (0, 1), (1, 1))
_ANY = pl.BlockSpec(memory_space=pl.ANY)


def _me():
    return lax.axis_index("x"), lax.axis_index("y"), lax.axis_index("c")


def _chip_peer(dx, dy):
    x, y, c = _me()
    return ((1 - x) if dx else x, (1 - y) if dy else y, c)


def _rcopy(src, dst, send_sem, recv_sem, to):
    return pltpu.make_async_remote_copy(src_ref=src, dst_ref=dst, send_sem=send_sem, recv_sem=recv_sem,
                                        device_id=to, device_id_type=MESH)


def _allgather4(name, a, halves=False):
    R, C = a.shape
    rh = R // 2 if halves else R

    def body(a_ref, out_ref, send_sems, recv_sems, local_sem):
        x, y, c = _me()
        me = 2 * x + y
        src = a_ref.at[pl.ds(c * rh, rh)] if halves else a_ref
        local = pltpu.make_async_copy(src, out_ref.at[me], local_sem)
        local.start()
        sends = []
        for k, (dx, dy) in enumerate(_CHIP_FLIPS):
            cp = _rcopy(src, out_ref.at[me], send_sems.at[k], recv_sems.at[k], _chip_peer(dx, dy))
            cp.start()
            sends.append(cp)
        for k, (dx, dy) in enumerate(_CHIP_FLIPS):
            px, py, _ = _chip_peer(dx, dy)
            _rcopy(src, out_ref.at[2 * px + py], send_sems.at[k], recv_sems.at[k], _chip_peer(dx, dy)).wait_recv()
        for cp in sends:
            cp.wait_send()
        local.wait()

    return pl.pallas_call(
        body, name=name, in_specs=[_ANY], out_specs=_ANY, out_shape=jax.ShapeDtypeStruct((4, rh, C), a.dtype),
        scratch_shapes=[pltpu.SemaphoreType.DMA((3,)), pltpu.SemaphoreType.DMA((3,)), pltpu.SemaphoreType.DMA(())])(a)


_NCH = 4


def _dma_sems(*counts):
    return [pltpu.SemaphoreType.DMA((n,)) for n in counts]


def _slot_tile(rows):
    tr = 256
    while rows % tr:
        tr //= 2
    return tr


def _cast_into_slot(name, a, chip, row0, rows):
    C = a.shape[1]
    tr = _slot_tile(rows)
    assert row0 % tr == 0
    first = row0 // tr

    def body(c_ref, a_ref, o_ref):
        o_ref[0] = a_ref[...].astype(o_ref.dtype)

    grid_spec = pltpu.PrefetchScalarGridSpec(
        num_scalar_prefetch=1, grid=(rows // tr,), in_specs=[pl.BlockSpec((tr, C), lambda i, c_ref: (first + i, 0))],
        out_specs=pl.BlockSpec((1, tr, C), lambda i, c_ref: (c_ref[0], i, 0)))
    return pl.pallas_call(body, name=name, grid_spec=grid_spec, out_shape=jax.ShapeDtypeStruct((4, rows, C), BF16),
                          compiler_params=_params(1))(chip, a)


def _chunks(rows, align):
    for nch in (_NCH, 2):
        if rows % (nch * align) == 0:
            return nch
    return 1


def _gather_exchange(out, ici_s, ici_r, d2d_s, d2d_r):
    n = len(out)
    x, y, c = _me()
    me = 2 * x + y
    sib = (x, y, 1 - c)
    peers = [_chip_peer(dx, dy) for dx, dy in _CHIP_FLIPS]
    for t in range(n):
        h = out[t].shape[1] // 2
        nch = _chunks(h, 16)
        ch = h // nch
        for k, peer in enumerate(peers):
            for i in range(nch):
                blk = out[t].at[me, pl.ds(c * h + i * ch, ch)]
                _rcopy(blk, blk, ici_s.at[3 * t + k], ici_r.at[3 * t + k], peer).start()
    for t in range(n):
        h = out[t].shape[1] // 2
        nch = _chunks(h, 16)
        ch = h // nch
        for k, peer in enumerate(peers):
            pchip = 2 * peer[0] + peer[1]
            got = out[t].at[pchip, pl.ds(c * h, h)]
            _rcopy(got, got, ici_s.at[3 * t + k], ici_r.at[3 * t + k], peer).wait_recv()
            for i in range(nch):
                blk = out[t].at[pchip, pl.ds(c * h + i * ch, ch)]
                _rcopy(blk, blk, d2d_s.at[3 * t + k], d2d_r.at[3 * t + k], sib).start()
    for t in range(n):
        h = out[t].shape[1] // 2
        for k, peer in enumerate(peers):
            pchip = 2 * peer[0] + peer[1]
            other = out[t].at[pchip, pl.ds((1 - c) * h, h)]
            _rcopy(other, other, d2d_s.at[3 * t + k], d2d_r.at[3 * t + k], sib).wait_recv()
            _rcopy(other, other, ici_s.at[3 * t + k], ici_r.at[3 * t + k], peer).wait_send()
            _rcopy(other, other, d2d_s.at[3 * t + k], d2d_r.at[3 * t + k], sib).wait_send()


def _gather_weights(name, bufs):
    n = len(bufs)

    def body(*refs):
        _gather_exchange(refs[n:2 * n], *refs[2 * n:])

    return pl.pallas_call(
        body, name=name, in_specs=[_ANY] * n, out_specs=[_ANY] * n,
        out_shape=[jax.ShapeDtypeStruct(s.shape, s.dtype) for s in bufs],
        input_output_aliases={t: t for t in range(n)},
        scratch_shapes=_dma_sems(3 * n, 3 * n, 3 * n, 3 * n))(*bufs)


def _gather_weights_async(name, collective_id, bufs):
    n = len(bufs)
    refs = [jax.new_ref(b, memory_space=pltpu.MemorySpace.HBM) for b in bufs]

    @pl.kernel(mesh=plsc.ScalarSubcoreMesh(axis_name="sequencer", num_cores=1), name=name,
               scratch_types=tuple(_dma_sems(3 * n, 3 * n, 3 * n, 3 * n)),
               compiler_params=pltpu.CompilerParams(collective_id=collective_id))
    def launch(ici_s, ici_r, d2d_s, d2d_r):
        x, y, c = _me()
        barrier = pltpu.get_barrier_semaphore()
        for peer in [_chip_peer(dx, dy) for dx, dy in _CHIP_FLIPS] + [(x, y, 1 - c)]:
            pl.semaphore_signal(barrier, inc=1, device_id=peer, device_id_type=MESH)
        pl.semaphore_wait(barrier, 4)
        _gather_exchange(refs, ici_s, ici_r, d2d_s, d2d_r)

    launch()
    return [r[...] for r in refs]


def _rs_split(name, grads):
    n = len(grads)

    def body(*refs):
        g, out = refs[:n], refs[n:2 * n]
        send, recv = refs[2 * n:]
        x, y, c = _me()
        sib = (x, y, 1 - c)
        for t in range(n):
            h = g[t].shape[1] // 2
            for d in range(4):
                _rcopy(g[t].at[d, pl.ds((1 - c) * h, h)], out[t].at[d], send.at[t], recv.at[t], sib).start()
        for t in range(n):
            _rcopy(out[t], out[t], send.at[t], recv.at[t], sib).wait()

    return pl.pallas_call(
        body, name=name, in_specs=[_ANY] * n, out_specs=[_ANY] * n,
        out_shape=[jax.ShapeDtypeStruct((4, s.shape[1] // 2, s.shape[2]), s.dtype) for s in grads],
        scratch_shapes=_dma_sems(n, n))(*grads)


def _pair_add(name, g, theirs, core_chip):
    _, R, C = g.shape
    h = R // 2
    tr = _slot_tile(h)
    nb = h // tr

    def body(s_ref, g_ref, t_ref, p_ref, o_ref):
        val = (g_ref[...].astype(F32) + t_ref[...].astype(F32)).astype(p_ref.dtype)
        p_ref[...] = val

        @pl.when(pl.program_id(1) == s_ref[1])
        def _():
            o_ref[...] = val

    spec = pl.BlockSpec((1, tr, C), lambda i, d, s_ref: (d, i, 0))
    grid_spec = pltpu.PrefetchScalarGridSpec(
        num_scalar_prefetch=1, grid=(nb, 4),
        in_specs=[pl.BlockSpec((1, tr, C), lambda i, d, s_ref: (d, s_ref[0] * nb + i, 0)), spec],
        out_specs=[spec, pl.BlockSpec((1, tr, C), lambda i, d, s_ref: (s_ref[1], i, 0))])
    half = jax.ShapeDtypeStruct((4, h, C), BF16)
    return pl.pallas_call(body, name=name, grid_spec=grid_spec, out_shape=[half, half],
                          compiler_params=_params(2))(core_chip, g, theirs)


def _rs_alltoall_async(name, collective_id, parts, bufs):
    n = len(parts)
    p = [jax.new_ref(a, memory_space=pltpu.MemorySpace.HBM) for a in parts]
    out = [jax.new_ref(b, memory_space=pltpu.MemorySpace.HBM) for b in bufs]

    @pl.kernel(mesh=plsc.ScalarSubcoreMesh(axis_name="sequencer", num_cores=1), name=name,
               scratch_types=tuple(_dma_sems(3 * n, 3 * n)),
               compiler_params=pltpu.CompilerParams(collective_id=collective_id))
    def launch(send, recv):
        barrier = pltpu.get_barrier_semaphore()
        for peer in [_chip_peer(dx, dy) for dx, dy in _CHIP_FLIPS]:
            pl.semaphore_signal(barrier, inc=1, device_id=peer, device_id_type=MESH)
        pl.semaphore_wait(barrier, 3)
        _alltoall_exchange(p, out, send, recv)

    launch()
    return [r[...] for r in out]


def _alltoall_exchange(p, out, send, recv):
    x, y, c = _me()
    me = 2 * x + y
    peers = [_chip_peer(dx, dy) for dx, dy in _CHIP_FLIPS]
    for t in range(len(p)):
        h = p[t].shape[1]
        nch = _chunks(h, 16)
        ch = h // nch
        for k, peer in enumerate(peers):
            pchip = 2 * peer[0] + peer[1]
            for i in range(nch):
                rows = pl.ds(i * ch, ch)
                _rcopy(p[t].at[pchip, rows], out[t].at[me, rows], send.at[3 * t + k], recv.at[3 * t + k], peer).start()
    for t in range(len(p)):
        for k, peer in enumerate(peers):
            pchip = 2 * peer[0] + peer[1]
            _rcopy(out[t].at[pchip], out[t].at[pchip], send.at[3 * t + k], recv.at[3 * t + k], peer).wait()


def _rs_swap(name, halves):
    n = len(halves)

    def body(*refs):
        a, out = refs[:n], refs[n:2 * n]
        send, recv = refs[2 * n:]
        x, y, c = _me()
        sib = (x, y, 1 - c)
        for t in range(n):
            ch = a[t].shape[0] // _NCH
            for i in range(_NCH):
                rows = pl.ds(i * ch, ch)
                _rcopy(a[t].at[rows], out[t].at[rows], send.at[t], recv.at[t], sib).start()
        for t in range(n):
            _rcopy(a[t], out[t], send.at[t], recv.at[t], sib).wait()

    return pl.pallas_call(
        body, name=name, in_specs=[_ANY] * n, out_specs=[_ANY] * n,
        out_shape=[jax.ShapeDtypeStruct(s.shape, s.dtype) for s in halves],
        scratch_shapes=_dma_sems(n, n))(*halves)


def _sibling_merge(name, a):
    P_, rh, C = a.shape

    def body(a_ref, out_ref, send_sem, recv_sem, local_sem):
        x, y, c = _me()
        local = pltpu.make_async_copy(a_ref, out_ref.at[:, pl.ds(c * rh, rh)], local_sem)
        local.start()
        cp = _rcopy(a_ref, out_ref.at[:, pl.ds(c * rh, rh)], send_sem, recv_sem, (x, y, 1 - c))
        cp.start()
        cp.wait_send()
        _rcopy(a_ref, out_ref.at[:, pl.ds((1 - c) * rh, rh)], send_sem, recv_sem, (x, y, 1 - c)).wait_recv()
        local.wait()

    return pl.pallas_call(
        body, name=name, in_specs=[_ANY], out_specs=_ANY, out_shape=jax.ShapeDtypeStruct((P_, 2 * rh, C), a.dtype),
        scratch_shapes=[pltpu.SemaphoreType.DMA(()), pltpu.SemaphoreType.DMA(()), pltpu.SemaphoreType.DMA(())])(a)


def _allgather8(name, a):
    g4 = _allgather4(name + "_chips", a)
    both = _sibling_merge(name + "_cores", g4.reshape(1, 4 * a.shape[0], a.shape[1]))
    return jnp.transpose(both.reshape(2, 4, *a.shape), (1, 0, 2, 3)).reshape(8, *a.shape)


def _sum_slots(name, a, out_dtype):
    def fn(a):
        acc = a[0].astype(F32)
        for k in range(1, a.shape[0]):
            acc = acc + a[k].astype(F32)
        return acc
    return _rowwise(name, fn, [a], [], [(a.shape[2], out_dtype)])[0]


def _adamw_math(w, g, m, v):
    m = ADAM_B1 * m + (1.0 - ADAM_B1) * g
    v = ADAM_B2 * v + (1.0 - ADAM_B2) * (g * g)
    m_hat = m / (1.0 - ADAM_B1 ** ADAM_STEP)
    v_hat = v / (1.0 - ADAM_B2 ** ADAM_STEP)
    return -ADAM_LR * (m_hat / (jnp.sqrt(v_hat) + ADAM_EPS) + ADAM_WD * w), m, v


def _adamw_piece(name, w2, m2, v2, mine, theirs, row0, prev):
    R, C = w2.shape
    h = mine.shape[0]
    tr = _slot_tile(h)
    nb = h // tr
    assert row0 % tr == 0
    first = row0 // tr

    def body(w_ref, m_ref, v_ref, a_ref, b_ref, *rest):
        g_ref, d_ref, nm_ref, nv_ref = rest[-4:]
        g = jnp.where(pl.program_id(0) == lax.axis_index("c"), a_ref[...], b_ref[...])
        g_ref[...] = g
        d_ref[...], nm_ref[...], nv_ref[...] = _adamw_math(w_ref[...], g, m_ref[...], v_ref[...])

    full = pl.BlockSpec((tr, C), lambda s, i: (first + s * nb + i, 0))
    half = pl.BlockSpec((tr, C), lambda s, i: (i, 0))
    extra = [] if prev is None else list(prev)
    return pl.pallas_call(
        body, name=name, grid=(2, nb), in_specs=[full, full, full, half, half] + [_ANY] * len(extra), out_specs=[full] * 4,
        out_shape=[jax.ShapeDtypeStruct((R, C), F32)] * 4, input_output_aliases={5 + k: k for k in range(len(extra))},
        compiler_params=_params(2))(w2, m2, v2, mine, theirs, *extra)


def _adamw(name, w, g, m, v):
    shape = w.shape
    two_d = (-1, shape[-1]) if w.ndim > 1 else (1, -1)
    w2, g2, m2, v2 = [t.reshape(two_d) for t in (w, g, m, v)]
    rows = w2.shape[0]
    tr = rows
    for cand in (256, 128, 64, 32, 16, 8):
        if rows % cand == 0:
            tr = cand
            break

    c = w2.shape[1]
    outs = _rowwise(name, _adamw_math, [w2, g2, m2, v2], [], [(c, F32)] * 3, tr=tr)
    return [o.reshape(shape) for o in outs]


_WEIGHT_ORDER = ("ada_w", "ada_b", "norm_mix_g", "norm_ffn_g", "gdn_w_in", "gdn_conv_w", "gdn_a_log", "gdn_dt_bias",
                 "gdn_norm_g", "gdn_w_out", "mla_w_in", "mla_q_norm_g", "mla_kv_norm_g", "mla_w_uq", "mla_w_ukv",
                 "mla_w_out", "ffn_w_gate", "ffn_w_up", "ffn_w_down", "final_norm_g")
_BIG = (("gdn_w_in", 2), ("gdn_w_out", 1), ("mla_w_in", 1), ("mla_w_uq", 2), ("mla_w_ukv", 2), ("mla_w_out", 1),
        ("ffn_w_gate", 2), ("ffn_w_up", 2), ("ffn_w_down", 1))
_SMALL_SHARDED = (("gdn_conv_w", 1), ("mla_q_norm_g", 1), ("mla_kv_norm_g", 1))


def _size(shape):
    n = 1
    for s in shape:
        n *= s
    return n


def _pack_rows_each(tensors):
    parts, offs, off = [], [], 0
    for t in tensors:
        flat = t.reshape(-1).astype(F32)
        rows = -(-flat.shape[0] // PACK_W)
        parts.append(jnp.pad(flat, (0, rows * PACK_W - flat.shape[0])).reshape(rows, PACK_W))
        offs.append(off)
        off += rows
    total = -(-off // 16) * 16
    pack = jnp.pad(parts[0], ((offs[0], total - offs[0] - parts[0].shape[0]), (0, 0)))
    for p, o in zip(parts[1:], offs[1:]):
        pack = pack + jnp.pad(p, ((o, total - o - p.shape[0]), (0, 0)))
    return pack, offs


def _unpack_rows_each(pack, shapes):
    lead = pack.shape[:-2]
    out, off = [], 0
    for shp in shapes:
        n = _size(shp)
        rows = -(-n // PACK_W)
        out.append(pack[..., off:off + rows, :].reshape(*lead, -1)[..., :n].reshape(*lead, *shp))
        off += rows
    return out


def _merge_chips(stacked, axis):
    moved = jnp.moveaxis(stacked, 0, axis)
    shp = list(moved.shape)
    return moved.reshape(shp[:axis] + [shp[axis] * shp[axis + 1]] + shp[axis + 2:])


def _my_shard(full, axis, chip):
    n = full.shape[axis] // 4
    return lax.dynamic_slice_in_dim(full, chip * n, n, axis)


def kernel(x, c, positions, ada_w, ada_b, norm_mix_g, norm_ffn_g, gdn_w_in, gdn_conv_w, gdn_a_log, gdn_dt_bias, gdn_norm_g, gdn_w_out, mla_w_in, mla_q_norm_g, mla_kv_norm_g, mla_w_uq, mla_w_ukv, mla_w_out, ffn_w_gate, ffn_w_up, ffn_w_down, final_norm_g, loss_target, m_ada_w, m_ada_b, m_norm_mix_g, m_norm_ffn_g, m_gdn_w_in, m_gdn_conv_w, m_gdn_a_log, m_gdn_dt_bias, m_gdn_norm_g, m_gdn_w_out, m_mla_w_in, m_mla_q_norm_g, m_mla_kv_norm_g, m_mla_w_uq, m_mla_w_ukv, m_mla_w_out, m_ffn_w_gate, m_ffn_w_up, m_ffn_w_down, m_final_norm_g, v_ada_w, v_ada_b, v_norm_mix_g, v_norm_ffn_g, v_gdn_w_in, v_gdn_conv_w, v_gdn_a_log, v_gdn_dt_bias, v_gdn_norm_g, v_gdn_w_out, v_mla_w_in, v_mla_q_norm_g, v_mla_kv_norm_g, v_mla_w_uq, v_mla_w_ukv, v_mla_w_out, v_ffn_w_gate, v_ffn_w_up, v_ffn_w_down, v_final_norm_g):
    w = dict(ada_w=ada_w, ada_b=ada_b, norm_mix_g=norm_mix_g, norm_ffn_g=norm_ffn_g, gdn_w_in=gdn_w_in, gdn_conv_w=gdn_conv_w,
             gdn_a_log=gdn_a_log, gdn_dt_bias=gdn_dt_bias, gdn_norm_g=gdn_norm_g, gdn_w_out=gdn_w_out, mla_w_in=mla_w_in,
             mla_q_norm_g=mla_q_norm_g, mla_kv_norm_g=mla_kv_norm_g, mla_w_uq=mla_w_uq, mla_w_ukv=mla_w_ukv,
             mla_w_out=mla_w_out, ffn_w_gate=ffn_w_gate, ffn_w_up=ffn_w_up, ffn_w_down=ffn_w_down, final_norm_g=final_norm_g)
    m = dict(ada_w=m_ada_w, ada_b=m_ada_b, norm_mix_g=m_norm_mix_g, norm_ffn_g=m_norm_ffn_g, gdn_w_in=m_gdn_w_in,
             gdn_conv_w=m_gdn_conv_w, gdn_a_log=m_gdn_a_log, gdn_dt_bias=m_gdn_dt_bias, gdn_norm_g=m_gdn_norm_g,
             gdn_w_out=m_gdn_w_out, mla_w_in=m_mla_w_in, mla_q_norm_g=m_mla_q_norm_g, mla_kv_norm_g=m_mla_kv_norm_g,
             mla_w_uq=m_mla_w_uq, mla_w_ukv=m_mla_w_ukv, mla_w_out=m_mla_w_out, ffn_w_gate=m_ffn_w_gate,
             ffn_w_up=m_ffn_w_up, ffn_w_down=m_ffn_w_down, final_norm_g=m_final_norm_g)
    v = dict(ada_w=v_ada_w, ada_b=v_ada_b, norm_mix_g=v_norm_mix_g, norm_ffn_g=v_norm_ffn_g, gdn_w_in=v_gdn_w_in,
             gdn_conv_w=v_gdn_conv_w, gdn_a_log=v_gdn_a_log, gdn_dt_bias=v_gdn_dt_bias, gdn_norm_g=v_gdn_norm_g,
             gdn_w_out=v_gdn_w_out, mla_w_in=v_mla_w_in, mla_q_norm_g=v_mla_q_norm_g, mla_kv_norm_g=v_mla_kv_norm_g,
             mla_w_uq=v_mla_w_uq, mla_w_ukv=v_mla_w_ukv, mla_w_out=v_mla_w_out, ffn_w_gate=v_ffn_w_gate,
             ffn_w_up=v_ffn_w_up, ffn_w_down=v_ffn_w_down, final_norm_g=v_final_norm_g)
    T = x.shape[1]
    ix, iy, ic = _me()
    chip = 2 * ix + iy
    seq = 2 * chip + ic
    n_dev = 8

    small_shapes = [w[n].shape for n, _ in _SMALL_SHARDED] + [c.shape]
    pack0, _ = _pack_rows_each([w[n] for n, _ in _SMALL_SHARDED] + [c])
    got0 = _unpack_rows_each(_allgather8("gather_small", pack0), small_shapes)
    small_full = {n: _merge_chips(g[0::2], ax) for (n, ax), g in zip(_SMALL_SHARDED, got0)}
    c_all = got0[-1].reshape(n_dev, D)

    big = [n for n, _ in _BIG]
    chip_arr = chip.astype(jnp.int32).reshape(1)
    gathered = {}

    gathered = []
    for l in range(DEPTH):
        names = _layer_weights(l)
        bufs = [_cast_into_slot(f"to_bf16_{n}{l}", w[n].reshape(-1, w[n].shape[-1]), chip_arr, j * w[n].shape[1], w[n].shape[1])
                for n, j in names]
        filled = _gather_weights("gather_weights0", bufs) if l == 0 else _gather_weights_async(f"gather_weights{l}", l, bufs)
        gathered.append({n: b for (n, _), b in zip(names, filled)})

    def weights_of(l, h):
        return _weights_to_kernel(l, gathered[l])

    P = _small_to_kernel(norm_mix_g, norm_ffn_g, final_norm_g, small_full["gdn_conv_w"], gdn_a_log, gdn_dt_bias,
                         gdn_norm_g, small_full["mla_q_norm_g"], small_full["mla_kv_norm_g"])

    c16 = jnp.pad(c_all, ((0, 16 - n_dev), (0, 0)))
    ca = _rowwise("cond_silu", lambda t: t * _sig(t), [c16], [], [(D, BF16)])[0]
    n_ada = ada_w.shape[2]
    mods = jnp.concatenate([_mm(f"ada_fwd{l}", ca, ada_w[l], "nn") for l in range(DEPTH)], axis=0)
    mods_all = _allgather4("gather_mod", mods).reshape(4, DEPTH, 16, n_ada)
    mod_mm = jnp.transpose(lax.dynamic_index_in_dim(mods_all, seq, axis=2, keepdims=False), (1, 0, 2)).reshape(DEPTH, 4 * n_ada)
    mod = _rowwise("mod_bias", lambda a, b: a + b, [mod_mm, ada_b], [], [(4 * n_ada, F32)])[0]

    core_chip = jnp.stack([ic, chip]).astype(jnp.int32)
    pending, in_flight = {}, []

    def reduce_group(layer, part, pieces):
        pending.update({(n, layer if n.startswith("ffn_") else layer // 2): g for n, g in pieces.items()})
        if part == "ffn" and layer > 0:
            return
        keys = list(pending)
        glist = [pending.pop(k) for k in keys]
        tag = f"{layer}{part}"
        theirs = _rs_split("grads_cores_" + tag, glist)
        both = [_pair_add(f"grads_pair_{n}{l}", g, t, core_chip) for (n, l), g, t in zip(keys, glist, theirs)]
        swapped = _rs_alltoall_async("grads_chips_" + tag, DEPTH + 1 + len(in_flight), [p for p, _ in both], [o for _, o in both])
        in_flight.append((tag, keys, swapped))

    dx, dmod, gP = _local_step(x.reshape(T, D), loss_target.reshape(T, D), positions.reshape(T, 1), mod, weights_of, P, reduce_group)

    partials = [dmod, jnp.concatenate(gP["norm_mix_g"]), jnp.concatenate(gP["norm_ffn_g"]), gP["final_g"],
                jnp.stack([jnp.transpose(g) for g in gP["gdn_cw"]]), jnp.concatenate(gP["gdn_alog"])[:, :NH],
                jnp.concatenate(gP["gdn_dtb"])[:, :NH], jnp.concatenate(gP["gdn_ng"]), jnp.concatenate(gP["mla_qg"]),
                jnp.concatenate(gP["mla_kvg"]), gP["loss"][:, :1]]
    part_shapes = [p.shape for p in partials]
    ppack, _ = _pack_rows_each(partials)
    pall = _allgather8("gather_partials", ppack)
    psum = _sum_slots("sum_partials", pall, F32)
    (g_ada_b, g_norm_mix, g_norm_ffn, g_final, g_conv_full, g_alog, g_dtb, g_gdn_ng, g_qg_full, g_kvg_full,
     loss_sum) = _unpack_rows_each(psum, part_shapes)
    dmod_all = _unpack_rows_each(pall, part_shapes[:1])[0]

    grads = dict(ada_b=g_ada_b, norm_mix_g=g_norm_mix, norm_ffn_g=g_norm_ffn, final_norm_g=g_final.reshape(D),
                 gdn_conv_w=_my_shard(g_conv_full, 1, chip), gdn_a_log=g_alog, gdn_dt_bias=g_dtb, gdn_norm_g=g_gdn_ng,
                 mla_q_norm_g=_my_shard(g_qg_full, 1, chip), mla_kv_norm_g=_my_shard(g_kvg_full, 1, chip))

    ca_t = jnp.zeros((D, LANES), BF16).at[:, :16].set(jnp.transpose(ca))
    dm_mine = lax.dynamic_slice_in_dim(dmod_all, chip * n_ada, n_ada, axis=2)
    grads["ada_w"] = jnp.stack([
        _mm(f"ada_bwd{l}", ca_t, jnp.pad(dm_mine[:, l], ((0, LANES - n_dev), (0, 0))), "nn") for l in range(DEPTH)])

    delta, new_m, new_v = {}, {}, {}
    two_d = lambda t: t.reshape(-1, t.shape[-1])
    results = {}
    for tag, keys, swapped in in_flight:
        halves = [_sum_slots(f"grads_sum_{n}{l}", s, F32) for (n, l), s in zip(keys, swapped)]
        others = _rs_swap("grads_swap_" + tag, halves)
        for (n, l), mine, theirs in zip(keys, halves, others):
            results[n] = _adamw_piece(f"adamw_{n}{l}", two_d(w[n]), two_d(m[n]), two_d(v[n]), mine, theirs,
                                      l * w[n].shape[1], results.get(n))
    for n in big:
        grads[n], delta[n], new_m[n], new_v[n] = [t.reshape(w[n].shape) for t in results[n]]
    delta["ada_w"], new_m["ada_w"], new_v["ada_w"] = _adamw("adamw_ada_w", ada_w, grads["ada_w"], m_ada_w, v_ada_w)
    small_names = [n for n in _WEIGHT_ORDER if n not in delta]
    small_shapes = [w[n].shape for n in small_names]
    packs = [_pack_rows_each([d[n] for n in small_names])[0] for d in (w, grads, m, v)]
    for d, pk in zip((delta, new_m, new_v), _adamw("adamw_small", *packs)):
        for n, t in zip(small_names, _unpack_rows_each(pk, small_shapes)):
            d[n] = t

    loss = loss_sum.reshape(())
    return (loss, dx.reshape(1, T, D), *[grads[n] for n in _WEIGHT_ORDER], *[delta[n] for n in _WEIGHT_ORDER],
            *[new_m[n] for n in _WEIGHT_ORDER], *[new_v[n] for n in _WEIGHT_ORDER])
```

```python
import functools

import jax
import jax.numpy as jnp
from jax import lax
from jax.experimental import pallas as pl
from jax.experimental.pallas import tpu as pltpu
from jax.experimental.pallas import tpu_sc as plsc

F32 = jnp.float32
BF16 = jnp.bfloat16
HI = lax.Precision.HIGHEST
MESH = pl.DeviceIdType.MESH

D = 1024
DEPTH = 4
N_MOD = 6
NH = 8
HD = 128
CHUNK = 64
_GDN_HB = 4
GDN_QKV = 3 * NH * HD
GDN_INK = GDN_QKV + NH * HD + 2 * HD
Q_RANK, KV_RANK, ROPE = 384, 256, 64
MLA_INK = Q_RANK + KV_RANK + HD
DFF = 2816
EPS = 1e-6
ATT_SCALE = (HD + ROPE) ** -0.5
ROPE_THETA = 10000.0
LANES = 128
PACK_W = 1024

ADAM_LR, ADAM_B1, ADAM_B2, ADAM_EPS, ADAM_WD, ADAM_STEP = 0.001, 0.9, 0.999, 1e-08, 0.01, 10


H3 = "bf16x3"
B1 = "bf16"
HS = H3


def _dot(a, b, mode="nn", prec=None):
    dn = {"nn": (((1,), (0,)), ((), ())), "nt": (((1,), (1,)), ((), ())), "tn": (((0,), (0,)), ((), ()))}[mode]
    if prec == B1:
        return _dot(a.astype(BF16), b.astype(BF16), mode)
    if prec == H3:
        ah, bh = a.astype(BF16), b.astype(BF16)
        al, bl = (a - ah.astype(F32)).astype(BF16), (b - bh.astype(F32)).astype(BF16)
        return _dot(ah, bh, mode) + (_dot(ah, bl, mode) + _dot(al, bh, mode))
    return lax.dot_general(a, b, dn, precision=prec, preferred_element_type=F32)


def _sig(x):
    return 1.0 / (1.0 + jnp.exp(-x))


def _pick(n, cap):
    if n <= cap:
        return n
    best = None
    for d in range(LANES, cap + 1, LANES):
        if n % d == 0:
            best = d
    assert best is not None, (n, cap)
    return best


def _params(n_grid):
    return pltpu.CompilerParams(dimension_semantics=("arbitrary",) * n_grid, vmem_limit_bytes=56 * 1024 * 1024)


def _rowwise(name, fn, rows, consts, outs, sums=(), tr=256):
    first = rows[0][0] if isinstance(rows[0], tuple) else rows[0]
    T = first.shape[-2]
    tr = _slot_tile(T, tr)
    nr, nc, no, ns = len(rows), len(consts), len(outs), len(sums)

    def body(*refs):
        res = fn(*[r[...] for r in refs[:nr + nc]])
        if not isinstance(res, (tuple, list)):
            res = (res,)
        o_refs = refs[nr + nc:nr + nc + no]
        s_refs = refs[nr + nc + no:]
        for r, val in zip(o_refs, res[:no]):
            r[...] = val.astype(r.dtype)
        if ns:
            @pl.when(pl.program_id(0) == 0)
            def _():
                for r in s_refs:
                    r[...] = jnp.zeros_like(r)
            for r, val in zip(s_refs, res[no:]):
                r[...] += val

    in_specs, args = [], []
    for a in rows:
        if isinstance(a, tuple):
            arr, width, cb = a
            in_specs.append(pl.BlockSpec((tr, width), lambda i, cb=cb: (i, cb)))
            args.append(arr)
        elif a.ndim == 3:
            in_specs.append(pl.BlockSpec((a.shape[0], tr, a.shape[2]), lambda i: (0, i, 0)))
            args.append(a)
        else:
            in_specs.append(pl.BlockSpec((tr, a.shape[1]), lambda i: (i, 0)))
            args.append(a)
    for a in consts:
        in_specs.append(pl.BlockSpec(a.shape, lambda i, nd=a.ndim: (0,) * nd))
        args.append(a)
    out_specs = [pl.BlockSpec((tr, w), lambda i: (i, 0)) for w, _ in outs]
    out_specs += [pl.BlockSpec((1, w), lambda i: (0, 0)) for w in sums]
    out_shape = [jax.ShapeDtypeStruct((T, w), dt) for w, dt in outs]
    out_shape += [jax.ShapeDtypeStruct((1, w), F32) for w in sums]
    res = pl.pallas_call(body, name=name, grid=(T // tr,), in_specs=in_specs, out_specs=out_specs,
                         out_shape=out_shape, compiler_params=_params(1))(*args)
    return res


def _mm(name, a, b, mode, out_dtype=F32, tm=512, tn=1024):
    if mode == "tn":
        K, M = a.shape
    else:
        M, K = a.shape
    N = b.shape[0] if mode == "nt" else b.shape[1]
    tm, tn = _pick(M, tm), _pick(N, tn)

    def body(a_ref, b_ref, o_ref):
        o_ref[...] = _dot(a_ref[...].astype(BF16), b_ref[...].astype(BF16), mode).astype(o_ref.dtype)

    a_spec = pl.BlockSpec((K, tm), lambda i, j: (0, i)) if mode == "tn" else pl.BlockSpec((tm, K), lambda i, j: (i, 0))
    b_spec = pl.BlockSpec((tn, K), lambda i, j: (j, 0)) if mode == "nt" else pl.BlockSpec((K, tn), lambda i, j: (0, j))
    return pl.pallas_call(body, name=name, grid=(M // tm, N // tn), in_specs=[a_spec, b_spec],
                          out_specs=pl.BlockSpec((tm, tn), lambda i, j: (i, j)),
                          out_shape=jax.ShapeDtypeStruct((M, N), out_dtype), compiler_params=_params(2))(a, b)


def _rms(x, eps=EPS):
    return lax.rsqrt(jnp.mean(x * x, axis=-1, keepdims=True) + eps)


def _norm_mod_fwd(name, x, g, scale, shift):
    def fn(x, g, scale, shift):
        return x * _rms(x) * g * (1.0 + scale) + shift
    return _rowwise(name, fn, [x], [g, scale, shift], [(D, BF16)])[0]


def _norm_mod_bwd(name, dh, x, dx_res, g, scale):
    def fn(dh, x, dx_res, g, scale):
        r = _rms(x)
        xh = x * r
        dxh = dh * (g * (1.0 + scale))
        dx = r * (dxh - xh * jnp.mean(dxh * xh, axis=-1, keepdims=True))
        dhx = dh * xh
        return (dx_res + dx, jnp.sum(dh, axis=0, keepdims=True), jnp.sum(dhx * g, axis=0, keepdims=True),
                jnp.sum(dhx * (1.0 + scale), axis=0, keepdims=True))
    return _rowwise(name, fn, [dh, x, dx_res], [g, scale], [(D, F32)], sums=[D, D, D])


def _residual_fwd(name, x, y, gate):
    def fn(x, y, gate):
        return x + gate * y
    return _rowwise(name, fn, [x, y], [gate], [(D, F32)])[0]


def _residual_bwd(name, dx, y, gate):
    def fn(dx, y, gate):
        return dx * gate, jnp.sum(dx * y, axis=0, keepdims=True)
    return _rowwise(name, fn, [dx, y], [gate], [(D, BF16)], sums=[D])


def _loss_head(x, target, g):
    def fn(x, t, g):
        r = _rms(x)
        xh = x * r
        err = xh * g - t
        loss = 0.5 * jnp.sum(jnp.mean(err * err, axis=-1, keepdims=True), axis=0, keepdims=True)
        dy = err * (1.0 / D)
        dxh = dy * g
        dx = r * (dxh - xh * jnp.mean(dxh * xh, axis=-1, keepdims=True))
        return dx, jnp.broadcast_to(loss, (1, LANES)), jnp.sum(dy * xh, axis=0, keepdims=True)
    return _rowwise("loss_head", fn, [x, target], [g], [(D, F32)], sums=[LANES, D])


def _ffn_up(name, h, wg, wu, layer, tm=1024):
    T, n = h.shape[0], wg.shape[2]
    tm = min(tm, T)

    def body(h_ref, wg_ref, wu_ref, a_ref, b_ref, s_ref):
        h = h_ref[...]
        a = _dot(h, wg_ref[0], "nn")
        b = _dot(h, wu_ref[0], "nn")
        a_ref[0] = a
        b_ref[0] = b
        s_ref[0] = (a * _sig(a) * b).astype(s_ref.dtype)

    wspec = pl.BlockSpec((1, D, n), lambda ch, i: (ch, layer, 0))
    ospec = pl.BlockSpec((1, tm, n), lambda ch, i: (ch, i, 0))
    return pl.pallas_call(
        body, name=name, grid=(4, T // tm), in_specs=[pl.BlockSpec((tm, D), lambda ch, i: (i, 0)), wspec, wspec],
        out_specs=[ospec, ospec, ospec],
        out_shape=[jax.ShapeDtypeStruct((4, T, n), F32)] * 2 + [jax.ShapeDtypeStruct((4, T, n), BF16)],
        compiler_params=_params(2))(h, wg, wu)


def _ffn_down(name, s, wd, layer, tm=1024):
    _, T, n = s.shape
    tm = min(tm, T)

    def body(s_ref, w_ref, y_ref):
        @pl.when(pl.program_id(1) == 0)
        def _():
            y_ref[...] = jnp.zeros_like(y_ref)
        y_ref[...] += _dot(s_ref[0], w_ref[0], "nn")

    return pl.pallas_call(
        body, name=name, grid=(T // tm, 4),
        in_specs=[pl.BlockSpec((1, tm, n), lambda i, ch: (ch, i, 0)), pl.BlockSpec((1, n, D), lambda i, ch: (ch, layer, 0))],
        out_specs=pl.BlockSpec((tm, D), lambda i, ch: (i, 0)), out_shape=jax.ShapeDtypeStruct((T, D), F32),
        compiler_params=_params(2))(s, wd)


def _ffn_down_bwd(name, dy, wd, a, b, layer, tm=1024):
    _, T, n = a.shape
    tm = min(tm, T)

    def body(dy_ref, w_ref, a_ref, b_ref, da_ref, db_ref):
        ds = _dot(dy_ref[...], w_ref[0], "nt")
        a, b = a_ref[0], b_ref[0]
        sg = _sig(a)
        da_ref[0] = (ds * b * (sg * (1.0 + a * (1.0 - sg)))).astype(da_ref.dtype)
        db_ref[0] = (ds * (a * sg)).astype(db_ref.dtype)

    bspec = pl.BlockSpec((1, tm, n), lambda ch, i: (ch, i, 0))
    return pl.pallas_call(
        body, name=name, grid=(4, T // tm),
        in_specs=[pl.BlockSpec((tm, D), lambda ch, i: (i, 0)), pl.BlockSpec((1, n, D), lambda ch, i: (ch, layer, 0)), bspec, bspec],
        out_specs=[bspec, bspec], out_shape=[jax.ShapeDtypeStruct((4, T, n), BF16)] * 2,
        compiler_params=_params(2))(dy, wd, a, b)


def _ffn_down_dw(name, s, dy):
    _, T, n = s.shape

    def body(s_ref, dy_ref, o_ref):
        o_ref[0] = _dot(s_ref[0], dy_ref[...], "tn").astype(o_ref.dtype)

    return pl.pallas_call(
        body, name=name, grid=(4,),
        in_specs=[pl.BlockSpec((1, T, n), lambda ch: (ch, 0, 0)), pl.BlockSpec((T, D), lambda ch: (0, 0))],
        out_specs=pl.BlockSpec((1, n, D), lambda ch: (ch, 0, 0)), out_shape=jax.ShapeDtypeStruct((4, n, D), BF16),
        compiler_params=_params(1))(s, dy)


def _ffn_up_dw(name, h, da, db, tm=512):
    _, T, n = da.shape

    def body(h_ref, da_ref, db_ref, dg_ref, du_ref):
        h = h_ref[...]
        dg_ref[0] = _dot(h, da_ref[0], "tn").astype(dg_ref.dtype)
        du_ref[0] = _dot(h, db_ref[0], "tn").astype(du_ref.dtype)

    dspec = pl.BlockSpec((1, T, n), lambda ch, j: (ch, 0, 0))
    ospec = pl.BlockSpec((1, tm, n), lambda ch, j: (ch, j, 0))
    return pl.pallas_call(
        body, name=name, grid=(4, D // tm), in_specs=[pl.BlockSpec((T, tm), lambda ch, j: (0, j)), dspec, dspec],
        out_specs=[ospec, ospec], out_shape=[jax.ShapeDtypeStruct((4, D, n), BF16)] * 2,
        compiler_params=_params(2))(h, da, db)


def _ffn_up_dx(name, da, db, wg, wu, layer, tm=1024):
    _, T, n = da.shape
    tm = min(tm, T)

    def body(da_ref, db_ref, wg_ref, wu_ref, o_ref):
        @pl.when(pl.program_id(1) == 0)
        def _():
            o_ref[...] = jnp.zeros_like(o_ref)
        o_ref[...] += _dot(da_ref[0], wg_ref[0], "nt") + _dot(db_ref[0], wu_ref[0], "nt")

    dspec = pl.BlockSpec((1, tm, n), lambda i, ch: (ch, i, 0))
    wspec = pl.BlockSpec((1, D, n), lambda i, ch: (ch, layer, 0))
    return pl.pallas_call(
        body, name=name, grid=(T // tm, 4), in_specs=[dspec, dspec, wspec, wspec],
        out_specs=pl.BlockSpec((tm, D), lambda i, ch: (i, 0)), out_shape=jax.ShapeDtypeStruct((T, D), F32),
        compiler_params=_params(2))(da, db, wg, wu)


def _shift_down(x, k):
    if k == 0:
        return x
    rows = lax.broadcasted_iota(jnp.int32, x.shape, 0)
    return jnp.where(rows >= k, pltpu.roll(x, k, 0), 0.0)


def _shift_up(x, k):
    if k == 0:
        return x
    T = x.shape[0]
    rows = lax.broadcasted_iota(jnp.int32, x.shape, 0)
    return jnp.where(rows < T - k, pltpu.roll(x, T - k, 0), 0.0)


def _conv_silu(x, w):
    c = w[0:1, :] * _shift_down(x, 3) + w[1:2, :] * _shift_down(x, 2) + w[2:3, :] * _shift_down(x, 1) + w[3:4, :] * x
    sg = _sig(c)
    return c, sg, c * sg


def _gdn_conv_fwd(name, proj, cw):
    T = proj.shape[0]

    def body(x_ref, w_ref, o_ref):
        j = pl.program_id(0)
        _, _, y = _conv_silu(x_ref[...], w_ref[...])
        r = lax.rsqrt(jnp.sum(y * y, axis=1, keepdims=True) + EPS)
        mult = jnp.where(j < NH, HD ** -0.5, 1.0)
        o_ref[...] = jnp.where(j < 2 * NH, y * (r * mult), y)

    return pl.pallas_call(body, name=name, grid=(3 * NH,),
                          in_specs=[pl.BlockSpec((T, HD), lambda j: (0, j)), pl.BlockSpec((4, HD), lambda j: (0, j))],
                          out_specs=pl.BlockSpec((T, HD), lambda j: (0, j)),
                          out_shape=jax.ShapeDtypeStruct((T, GDN_QKV), F32), compiler_params=_params(1))(proj, cw)


def _gdn_conv_bwd(name, proj, cw, dz):
    T = proj.shape[0]

    def body(x_ref, w_ref, dz_ref, dx_ref, dw_ref):
        j = pl.program_id(0)
        x, w, dz = x_ref[...], w_ref[...], dz_ref[...]
        c, sg, y = _conv_silu(x, w)
        r = lax.rsqrt(jnp.sum(y * y, axis=1, keepdims=True) + EPS)
        mult = jnp.where(j < NH, HD ** -0.5, 1.0)
        dyn = mult * (r * dz - (r * r * r) * y * jnp.sum(dz * y, axis=1, keepdims=True))
        dy = jnp.where(j < 2 * NH, dyn, dz)
        dc = dy * (sg * (1.0 + c * (1.0 - sg)))
        dx = w[0:1, :] * _shift_up(dc, 3) + w[1:2, :] * _shift_up(dc, 2) + w[2:3, :] * _shift_up(dc, 1) + w[3:4, :] * dc
        dx_ref[...] = dx.astype(dx_ref.dtype)
        for k in range(4):
            dw_ref[pl.ds(k, 1), :] = jnp.sum(dc * _shift_down(x, 3 - k), axis=0, keepdims=True)

    return pl.pallas_call(body, name=name, grid=(3 * NH,),
                          in_specs=[pl.BlockSpec((T, HD), lambda j: (0, j)), pl.BlockSpec((4, HD), lambda j: (0, j)),
                                    pl.BlockSpec((T, HD), lambda j: (0, j))],
                          out_specs=[pl.BlockSpec((T, HD), lambda j: (0, j)), pl.BlockSpec((4, HD), lambda j: (0, j))],
                          out_shape=[jax.ShapeDtypeStruct((T, GDN_QKV), BF16), jax.ShapeDtypeStruct((4, GDN_QKV), F32)],
                          compiler_params=_params(1))(proj, cw, dz)


def _softplus(z):
    return jnp.maximum(z, 0.0) + jnp.log(1.0 + jnp.exp(-jnp.abs(z)))


_AB_CB = GDN_INK // (2 * HD) - 1


def _gdn_gates_fwd(name, proj, alog, dtb):
    def fn(ab, alog, dtb):
        a, b = ab[:, :HD], ab[:, HD:]
        return -jnp.exp(alog) * _softplus(a + dtb), _sig(b)
    return _rowwise(name, fn, [(proj, 2 * HD, _AB_CB)], [alog, dtb], [(HD, F32), (HD, F32)])


def _gdn_gates_bwd(name, proj, dg_h, db_h, alog, dtb):
    def fn(ab, dg_h, db_h, alog, dtb):
        lane = lax.broadcasted_iota(jnp.int32, (1, HD), 1)
        dg = jnp.zeros(dg_h.shape[1:], F32)
        dbeta = jnp.zeros(dg_h.shape[1:], F32)
        for h in range(NH):
            oh = (lane == h).astype(F32)
            dg = dg + dg_h[h] * oh
            dbeta = dbeta + db_h[h] * oh
        a, b = ab[:, :HD], ab[:, HD:]
        z = a + dtb
        ea = jnp.exp(alog)
        beta = _sig(b)
        da = dg * (-ea) * _sig(z)
        db = dbeta * beta * (1.0 - beta)
        return (jnp.concatenate([da, db], axis=1), jnp.sum(dg * (-ea * _softplus(z)), axis=0, keepdims=True),
                jnp.sum(da, axis=0, keepdims=True))
    return _rowwise(name, fn, [(proj, 2 * HD, _AB_CB), dg_h, db_h], [alog, dtb], [(2 * HD, BF16)], sums=[HD, HD])


def _interleave(gens):
    gens = list(gens)
    results = [None] * len(gens)
    active = list(range(len(gens)))
    while active:
        for i in list(active):
            try:
                next(gens[i])
            except StopIteration as stop:
                results[i] = stop.value
                active.remove(i)
    return results


def _chunk_common(q, k, v, gblk, bblk, h):
    C = CHUNK
    lane = lax.broadcasted_iota(jnp.int32, (1, HD), 1)
    oh = (lane == h).astype(F32)
    g_col = jnp.sum(gblk * oh, axis=1, keepdims=True)
    beta = jnp.sum(bblk * oh, axis=1, keepdims=True)
    ri = lax.broadcasted_iota(jnp.int32, (C, C), 0)
    ci = lax.broadcasted_iota(jnp.int32, (C, C), 1)
    incl = ri >= ci
    strict = ri > ci
    eye = (ri == ci).astype(F32)
    gcb = _dot(incl.astype(F32), jnp.broadcast_to(g_col, (C, HD)), "nn", HI)
    yield
    gc = gcb[:, :C]
    gc_row = _dot(jnp.ones((C, C), F32), eye * gc, "nn", HI)
    yield
    decay = jnp.where(incl, jnp.exp(jnp.where(incl, gc - gc_row, 0.0)), 0.0)
    rows = lax.broadcasted_iota(jnp.int32, (C, HD), 0)
    gclb = jnp.sum(jnp.where(rows == C - 1, gcb, 0.0), axis=0, keepdims=True)
    eg = jnp.exp(gcb)
    egl = jnp.exp(gclb - gcb)
    gl = jnp.exp(gclb)
    kb = k * beta
    m1 = _dot(kb, k, "nt", HS)
    qk = _dot(q, k, "nt", HS)
    yield
    L = jnp.where(strict, m1 * decay, 0.0)
    nl = -L
    tinv = eye + nl
    p = nl
    for _ in range(5):
        p = _dot(p, p, "nn", H3)
        yield
        tinv = tinv + _dot(tinv, p, "nn", H3)
    vb = v * beta
    kbg = kb * eg
    yield
    u = _dot(tinv, vb, "nn", HS)
    w = _dot(tinv, kbg, "nn", HS)
    yield
    attn = jnp.where(incl, qk * decay, 0.0)
    return dict(beta=beta, incl=incl, strict=strict, decay=decay, eg=eg, egl=egl, gl=gl, kb=kb, m1=m1, tinv=tinv,
                kbg=kbg, u=u, w=w, qk=qk, attn=attn, q_dec=q * eg, k_dec=k * egl, rows=rows, oh=oh)


def _gdn_chunk_fwd(name, qkv, g, beta):
    T = qkv.shape[0]
    N = T // CHUNK

    hb = _GDN_HB
    w = hb * HD

    def body(q_ref, k_ref, v_ref, g_ref, b_ref, o_ref, st_ref, S):
        hg, n = pl.program_id(0), pl.program_id(1)

        @pl.when(n == 0)
        def _():
            S[...] = jnp.zeros_like(S)

        gblk, bblk = g_ref[...], b_ref[...]

        def one_head(i, q, k, v, s):
            c = yield from _chunk_common(q, k, v, gblk, bblk, hg * hb + i)
            v_new = c["u"] - _dot(c["w"], s, "nn", HS)
            qs = _dot(c["q_dec"], s, "nn", HS)
            yield
            o = qs + _dot(c["attn"], v_new, "nn", HS)
            return o, s * c["gl"] + _dot(c["k_dec"], v_new, "tn", HS)

        sls = [slice(i * HD, (i + 1) * HD) for i in range(hb)]
        states = [S[i] for i in range(hb)]
        res = _interleave(one_head(i, q_ref[:, sls[i]], k_ref[:, sls[i]], v_ref[:, sls[i]], states[i]) for i in range(hb))
        for i, (o, s_new) in enumerate(res):
            st_ref[i, 0] = states[i]
            o_ref[:, sls[i]] = o
            S[i] = s_new

    blk = lambda off: pl.BlockSpec((CHUNK, w), lambda h, n, off=off: (n, off + h))
    gspec = pl.BlockSpec((CHUNK, HD), lambda h, n: (n, 0))
    return pl.pallas_call(
        body, name=name, grid=(NH // hb, N), in_specs=[blk(0), blk(NH // hb), blk(2 * NH // hb), gspec, gspec],
        out_specs=[pl.BlockSpec((CHUNK, w), lambda h, n: (n, h)), pl.BlockSpec((hb, 1, HD, HD), lambda h, n: (h, n, 0, 0))],
        out_shape=[jax.ShapeDtypeStruct((T, NH * HD), F32), jax.ShapeDtypeStruct((NH, N, HD, HD), F32)],
        scratch_shapes=[pltpu.VMEM((hb, HD, HD), F32)], compiler_params=_params(2))(qkv, qkv, qkv, g, beta)


def _gdn_chunk_bwd(name, qkv, g, beta, states, do):
    T = qkv.shape[0]
    N = T // CHUNK
    C = CHUNK

    hb = _GDN_HB
    w = hb * HD

    def body(q_ref, k_ref, v_ref, g_ref, b_ref, st_ref, do_ref, dq_ref, dk_ref, dv_ref, dg_ref, db_ref, dS):
        hg, n = pl.program_id(0), pl.program_id(1)

        @pl.when(n == 0)
        def _():
            dS[...] = jnp.zeros_like(dS)

        gblk, bblk = g_ref[...], b_ref[...]
        sls = [slice(i * HD, (i + 1) * HD) for i in range(hb)]
        res = _interleave(one_head(hg * hb + i, gblk, bblk, q_ref[:, sls[i]], k_ref[:, sls[i]], v_ref[:, sls[i]],
                                   st_ref[i, 0], do_ref[:, sls[i]], dS[i]) for i in range(hb))
        for i, (dq, dk, dv, dg, db, ds_new) in enumerate(res):
            dq_ref[:, sls[i]] = dq
            dk_ref[:, sls[i]] = dk
            dv_ref[:, sls[i]] = dv
            dg_ref[i] = dg
            db_ref[i] = db
            dS[i] = ds_new

    def one_head(h, gblk, bblk, q, k, v, s, do, ds):
        c = yield from _chunk_common(q, k, v, gblk, bblk, h)
        eg, egl, gl, beta, decay, tinv = c["eg"], c["egl"], c["gl"], c["beta"], c["decay"], c["tinv"]
        v_new = c["u"] - _dot(c["w"], s, "nn", HS)
        dq_dec = _dot(do, s, "nt", HS)
        yield
        dv_new = _dot(c["attn"], do, "tn", HS) + _dot(c["k_dec"], ds, "nn", HS)
        dk_dec = _dot(v_new, ds, "nt", HS)
        dgl = jnp.sum(jnp.sum(s * ds, axis=1, keepdims=True), axis=0, keepdims=True)
        yield
        ds_new = ds * gl + _dot(c["q_dec"], do, "tn", HS) - _dot(c["w"], dv_new, "tn", HS)
        dattn = jnp.where(c["incl"], _dot(do, v_new, "nt", HS), 0.0)
        dw = -_dot(dv_new, s, "nt", HS)
        yield
        dvb = _dot(tinv, dv_new, "tn", HS)
        dkbg = _dot(tinv, dw, "tn", HS)
        yield
        dA = -(_dot(dvb, c["u"], "nt", HS) + _dot(dkbg, c["w"], "nt", HS))
        yield
        dL = jnp.where(c["strict"], dA, 0.0)
        dm1 = dL * decay
        dqk = dattn * decay
        xdec = (dL * c["m1"] + dattn * c["qk"]) * decay
        dkb = _dot(dm1, k, "nn", HS) + dkbg * eg
        dk = _dot(dm1, c["kb"], "tn", HS) + _dot(dqk, q, "tn", HS) + dk_dec * egl + dkb * beta
        dq = _dot(dqk, k, "nn", HS) + dq_dec * eg
        yield
        dkd_kd = jnp.sum(dk_dec * c["k_dec"], axis=1, keepdims=True)
        dgc = (jnp.sum(xdec, axis=1, keepdims=True) - _dot(xdec, jnp.ones((C, HD), F32), "tn", HS)
               + jnp.sum(dq_dec * c["q_dec"], axis=1, keepdims=True) - dkd_kd
               + jnp.sum(dkbg * c["kbg"], axis=1, keepdims=True))
        dgcl = jnp.sum(dkd_kd, axis=0, keepdims=True) + dgl * gl
        dgc = dgc + jnp.where(c["rows"] == C - 1, dgcl, 0.0)
        ri = lax.broadcasted_iota(jnp.int32, (C, C), 0)
        ci = lax.broadcasted_iota(jnp.int32, (C, C), 1)
        dg = _dot((ci >= ri).astype(F32), dgc, "nn", HI)
        db = jnp.broadcast_to(jnp.sum(dkb * k, axis=1, keepdims=True) + jnp.sum(dvb * v, axis=1, keepdims=True), (C, HD))
        return dq, dk, dvb * beta, dg, db, ds_new

    blk = lambda off: pl.BlockSpec((C, w), lambda h, n, off=off: (N - 1 - n, off + h))
    gspec = pl.BlockSpec((C, HD), lambda h, n: (N - 1 - n, 0))
    ospec = pl.BlockSpec((C, w), lambda h, n: (N - 1 - n, h))
    hspec = pl.BlockSpec((hb, C, HD), lambda h, n: (h, N - 1 - n, 0))
    return pl.pallas_call(
        body, name=name, grid=(NH // hb, N),
        in_specs=[blk(0), blk(NH // hb), blk(2 * NH // hb), gspec, gspec,
                  pl.BlockSpec((hb, 1, HD, HD), lambda h, n: (h, N - 1 - n, 0, 0)), ospec],
        out_specs=[ospec, ospec, ospec, hspec, hspec],
        out_shape=[jax.ShapeDtypeStruct((T, NH * HD), F32)] * 3 + [jax.ShapeDtypeStruct((NH, T, HD), F32)] * 2,
        scratch_shapes=[pltpu.VMEM((hb, HD, HD), F32)], compiler_params=_params(2))(qkv, qkv, qkv, g, beta, states, do)


_GATE_CB = GDN_QKV // (NH * HD)


def _gdn_gated_norm_fwd(name, o, proj, ng):
    def fn(o, gate, ng):
        outs = []
        for h in range(NH):
            sl = slice(h * HD, (h + 1) * HD)
            oh, gh = o[:, sl], gate[:, sl]
            outs.append(oh * _rms(oh) * ng * (gh * _sig(gh)))
        return jnp.concatenate(outs, axis=1)
    return _rowwise(name, fn, [o, (proj, NH * HD, _GATE_CB)], [ng], [(NH * HD, BF16)])[0]


def _gdn_gated_norm_bwd(name, don, o, proj, ng):
    def fn(don, o, gate, ng):
        dos, dgs = [], []
        dng = jnp.zeros((1, HD), F32)
        for h in range(NH):
            sl = slice(h * HD, (h + 1) * HD)
            oh, gh, dh = o[:, sl], gate[:, sl], don[:, sl]
            r = _rms(oh)
            xh = oh * r
            sg = _sig(gh)
            dn = dh * (gh * sg)
            dgs.append(dh * (xh * ng) * (sg * (1.0 + gh * (1.0 - sg))))
            dng = dng + jnp.sum(dn * xh, axis=0, keepdims=True)
            dxh = dn * ng
            dos.append(r * (dxh - xh * jnp.mean(dxh * xh, axis=-1, keepdims=True)))
        return jnp.concatenate(dos, axis=1), jnp.concatenate(dgs, axis=1), dng
    return _rowwise(name, fn, [don, o, (proj, NH * HD, _GATE_CB)], [ng], [(NH * HD, F32), (NH * HD, BF16)], sums=[HD])


def _rot(x):
    lane = lax.broadcasted_iota(jnp.int32, x.shape, 1)
    return jnp.where(lane < ROPE // 2, -pltpu.roll(x, HD - ROPE // 2, 1), pltpu.roll(x, ROPE // 2, 1))


def _rot_t(x):
    lane = lax.broadcasted_iota(jnp.int32, x.shape, 1)
    return jnp.where(lane < ROPE // 2, pltpu.roll(x, HD - ROPE // 2, 1), -pltpu.roll(x, ROPE // 2, 1))


def _rope_tables(pos_col):
    lane = jnp.arange(HD)
    inv_freq = ROPE_THETA ** (-(2.0 * (lane % (ROPE // 2)).astype(F32)) / ROPE)
    inv_freq = jnp.where(lane < ROPE, inv_freq, 0.0).astype(F32)[None, :]
    valid = (lane < ROPE).astype(F32)[None, :]

    def fn(pos, inv_freq, valid):
        ang = pos.astype(F32) * inv_freq
        return jnp.cos(ang) * valid, jnp.sin(ang) * valid
    return _rowwise("rope_tables", fn, [pos_col], [inv_freq, valid], [(HD, F32), (HD, F32)])


def _mla_pre_fwd(name, proj, cos, sin, qg, kvg):
    def fn(p, cos, sin, qg, kvg):
        cq, ckv, kr = p[:, :Q_RANK], p[:, Q_RANK:Q_RANK + KV_RANK], p[:, Q_RANK + KV_RANK:]
        return cq * _rms(cq) * qg, ckv * _rms(ckv) * kvg, kr * cos + _rot(kr) * sin
    return _rowwise(name, fn, [proj, cos, sin], [qg, kvg], [(Q_RANK, BF16), (KV_RANK, BF16), (HD, BF16)])


def _rms_bwd(dy, x, g):
    r = _rms(x)
    xh = x * r
    dxh = dy * g
    return r * (dxh - xh * jnp.mean(dxh * xh, axis=-1, keepdims=True)), jnp.sum(dy * xh, axis=0, keepdims=True)


def _mla_pre_bwd(name, proj, dcqn, dckvn, dkr, cos, sin, qg, kvg):
    def fn(p, dcqn, dckvn, dkr, cos, sin, qg, kvg):
        cq, ckv = p[:, :Q_RANK], p[:, Q_RANK:Q_RANK + KV_RANK]
        dcq, dqg = _rms_bwd(dcqn, cq, qg)
        dckv, dkvg = _rms_bwd(dckvn, ckv, kvg)
        dkr_pre = dkr * cos + _rot_t(dkr * sin)
        return jnp.concatenate([dcq, dckv, dkr_pre], axis=1), dqg, dkvg
    return _rowwise(name, fn, [proj, dcqn, dckvn, dkr, cos, sin], [qg, kvg], [(MLA_INK, BF16)], sums=[Q_RANK, KV_RANK])


def _mla_q_fwd(name, q, cos, sin):
    def fn(qn, qr, cos, sin):
        outs = []
        for h in range(NH):
            x = qr[:, h * HD:(h + 1) * HD]
            outs.append(x * cos + _rot(x) * sin)
        return qn, jnp.concatenate(outs, axis=1)
    return _rowwise(name, fn, [(q, NH * HD, 0), (q, NH * HD, 1), cos, sin], [], [(NH * HD, BF16), (NH * HD, BF16)])


def _mla_q_bwd(name, dqn, dqr, cos, sin):
    def fn(dqn, dqr, cos, sin):
        outs = [dqn]
        for h in range(NH):
            z = dqr[:, h * HD:(h + 1) * HD]
            outs.append(z * cos + _rot_t(z * sin))
        return jnp.concatenate(outs, axis=1)
    return _rowwise(name, fn, [dqn, dqr, cos, sin], [], [(2 * NH * HD, BF16)])[0]


def _att_probs(qn, qr, kn, kr, row0):
    s = (_dot(qn, kn, "nt") + _dot(qr, kr, "nt")) * ATT_SCALE
    qpos = row0 + lax.broadcasted_iota(jnp.int32, s.shape, 0)
    kpos = lax.broadcasted_iota(jnp.int32, s.shape, 1)
    s = jnp.where(kpos <= qpos, s, -1e30)
    p = jnp.exp(s - jnp.max(s, axis=1, keepdims=True))
    return p / jnp.sum(p, axis=1, keepdims=True)


def _mla_attn_fwd(name, qn, qr, kv, kr, tq=256):
    T = qn.shape[0]
    tq = min(tq, T)

    def body(qn_ref, qr_ref, kn_ref, v_ref, kr_ref, o_ref):
        i = pl.program_id(1)
        for blk in range(T // tq):
            @pl.when(i == blk)
            def _(blk=blk):
                keys = pl.ds(0, (blk + 1) * tq)
                p = _att_probs(qn_ref[...], qr_ref[...], kn_ref[keys, :], kr_ref[keys, :], blk * tq)
                o_ref[...] = _dot(p.astype(BF16), v_ref[keys, :], "nn").astype(o_ref.dtype)

    qspec = pl.BlockSpec((tq, HD), lambda h, i: (i, h))
    return pl.pallas_call(
        body, name=name, grid=(NH, T // tq),
        in_specs=[qspec, qspec, pl.BlockSpec((T, HD), lambda h, i: (0, h)), pl.BlockSpec((T, HD), lambda h, i: (0, NH + h)),
                  pl.BlockSpec((T, HD), lambda h, i: (0, 0))],
        out_specs=qspec, out_shape=jax.ShapeDtypeStruct((T, NH * HD), BF16), compiler_params=_params(2))(qn, qr, kv, kv, kr)


def _mla_attn_bwd(name, qn, qr, kv, kr, do, tq=256):
    T = qn.shape[0]
    tq = min(tq, T)

    def body(qn_ref, qr_ref, kn_ref, v_ref, kr_ref, do_ref, dqn_ref, dqr_ref, dkn_ref, dv_ref, dkr_ref):
        h, i = pl.program_id(0), pl.program_id(1)

        @pl.when(i == 0)
        def _():
            dkn_ref[...] = jnp.zeros_like(dkn_ref)
            dv_ref[...] = jnp.zeros_like(dv_ref)

        @pl.when((i == 0) & (h == 0))
        def _():
            dkr_ref[...] = jnp.zeros_like(dkr_ref)

        for blk in range(T // tq):
            @pl.when(i == blk)
            def _(blk=blk):
                keys = pl.ds(0, (blk + 1) * tq)
                qn, qr, do = qn_ref[...], qr_ref[...], do_ref[...]
                kn, kr, v = kn_ref[keys, :], kr_ref[keys, :], v_ref[keys, :]
                p = _att_probs(qn, qr, kn, kr, blk * tq)
                dp = _dot(do, v, "nt")
                ds = (p * (dp - jnp.sum(p * dp, axis=1, keepdims=True)) * ATT_SCALE).astype(BF16)
                dqn_ref[...] = _dot(ds, kn, "nn")
                dqr_ref[...] = _dot(ds, kr, "nn")
                dkn_ref[keys, :] += _dot(ds, qn, "tn")
                dkr_ref[keys, :] += _dot(ds, qr, "tn")
                dv_ref[keys, :] += _dot(p.astype(BF16), do, "tn")

    qspec = pl.BlockSpec((tq, HD), lambda h, i: (i, h))
    kspec = pl.BlockSpec((T, HD), lambda h, i: (0, h))
    return pl.pallas_call(
        body, name=name, grid=(NH, T // tq),
        in_specs=[qspec, qspec, kspec, pl.BlockSpec((T, HD), lambda h, i: (0, NH + h)),
                  pl.BlockSpec((T, HD), lambda h, i: (0, 0)), qspec],
        out_specs=[qspec, qspec, kspec, kspec, pl.BlockSpec((T, HD), lambda h, i: (0, 0))],
        out_shape=[jax.ShapeDtypeStruct((T, NH * HD), F32)] * 4 + [jax.ShapeDtypeStruct((T, HD), F32)],
        compiler_params=_params(2))(qn, qr, kv, kv, kr, do)


def _mod_rows(mod, layer):
    return [mod[layer:layer + 1, i * D:(i + 1) * D] for i in range(N_MOD)]


def _local_step(x, target, pos_col, mod, weights_of, P, on_grads):
    cos, sin = _rope_tables(pos_col)
    saved = []
    for l in range(DEPTH):
        j = l // 2
        sh_m, sc_m, ga_m, sh_f, sc_f, ga_f = _mod_rows(mod, l)
        s = dict(x0=x)
        h = _norm_mod_fwd(f"norm_mix{l}", x, P["norm_mix_g"][l:l + 1], sc_m, sh_m)
        W = weights_of(l, h)
        s.update(h=h, W=W)
        if l % 2 == 0:
            proj = _mm(f"gdn_in{j}", h, W["gdn_in"], "nn", tn=GDN_INK // 2)
            qkv = _gdn_conv_fwd(f"gdn_conv{j}", proj, P["gdn_cw"][j])
            g, beta = _gdn_gates_fwd(f"gdn_gates{j}", proj, P["gdn_alog"][j], P["gdn_dtb"][j])
            o, states = _gdn_chunk_fwd(f"gdn_chunk{j}", qkv, g, beta)
            on = _gdn_gated_norm_fwd(f"gdn_gnorm{j}", o, proj, P["gdn_ng"][j])
            y = _mm(f"gdn_out{j}", on, W["gdn_out"], "nn")
            s.update(proj=proj, qkv=qkv, g=g, beta=beta, o=o, states=states, on=on)
        else:
            proj = _mm(f"mla_in{j}", h, W["mla_in"], "nn")
            cqn, ckvn, kr = _mla_pre_fwd(f"mla_pre{j}", proj, cos, sin, P["mla_qg"][j], P["mla_kvg"][j])
            q = _mm(f"mla_uq{j}", cqn, W["mla_uq"], "nn")
            kv = _mm(f"mla_ukv{j}", ckvn, W["mla_ukv"], "nn", out_dtype=BF16)
            qn, qr = _mla_q_fwd(f"mla_q{j}", q, cos, sin)
            o = _mla_attn_fwd(f"mla_attn{j}", qn, qr, kv, kr)
            y = _mm(f"mla_out{j}", o, W["mla_out"], "nn")
            s.update(proj=proj, cqn=cqn, ckvn=ckvn, kr=kr, kv=kv, qn=qn, qr=qr, o=o)
        s["y"] = y
        x = _residual_fwd(f"res_mix{l}", x, y, ga_m)
        s["x1"] = x
        h2 = _norm_mod_fwd(f"norm_ffn{l}", x, P["norm_ffn_g"][l:l + 1], sc_f, sh_f)
        fa, fb, sw = _ffn_up(f"ffn_up{l}", h2, W["ffn_g"], W["ffn_u"], 0)
        yf = _ffn_down(f"ffn_down{l}", sw, W["ffn_d"], 0)
        x = _residual_fwd(f"res_ffn{l}", x, yf, ga_f)
        s.update(h2=h2, fa=fa, fb=fb, sw=sw, yf=yf)
        saved.append(s)

    dx, loss, d_final = _loss_head(x, target, P["final_g"])
    gP = dict(loss=loss, final_g=d_final, norm_mix_g=[None] * DEPTH, norm_ffn_g=[None] * DEPTH,
              gdn_cw=[None] * 2, gdn_alog=[None] * 2, gdn_dtb=[None] * 2, gdn_ng=[None] * 2,
              mla_qg=[None] * 2, mla_kvg=[None] * 2)
    dmod = [None] * DEPTH
    for l in reversed(range(DEPTH)):
        j = l // 2
        s = saved[l]
        W = s["W"]
        sh_m, sc_m, ga_m, sh_f, sc_f, ga_f = _mod_rows(mod, l)
        dyf, d_ga_f = _residual_bwd(f"res_ffn_b{l}", dx, s["yf"], ga_f)
        da, db = _ffn_down_bwd(f"ffn_down_dx{l}", dyf, W["ffn_d"], s["fa"], s["fb"], 0)
        g_down = _ffn_down_dw(f"ffn_down_dw{l}", s["sw"], dyf)
        g_gate, g_up = _ffn_up_dw(f"ffn_up_dw{l}", s["h2"], da, db)
        on_grads(l, "ffn", dict(ffn_w_gate=g_gate, ffn_w_up=g_up, ffn_w_down=g_down))
        dh2 = _ffn_up_dx(f"ffn_up_dx{l}", da, db, W["ffn_g"], W["ffn_u"], 0)
        dx, d_sh_f, d_sc_f, gP["norm_ffn_g"][l] = _norm_mod_bwd(f"norm_ffn_b{l}", dh2, s["x1"], dx,
                                                                 P["norm_ffn_g"][l:l + 1], sc_f)
        dy, d_ga_m = _residual_bwd(f"res_mix_b{l}", dx, s["y"], ga_m)
        if l % 2 == 0:
            don = _mm(f"gdn_out_dx{j}", dy, W["gdn_out"], "nt")
            g_out = _mm(f"gdn_out_dw{j}", s["on"], dy, "tn", out_dtype=BF16)
            do, dgate, gP["gdn_ng"][j] = _gdn_gated_norm_bwd(f"gdn_gnorm_b{j}", don, s["o"], s["proj"], P["gdn_ng"][j])
            dq, dk, dv, dg_h, db_h = _gdn_chunk_bwd(f"gdn_chunk_b{j}", s["qkv"], s["g"], s["beta"], s["states"], do)
            dab_, gP["gdn_alog"][j], gP["gdn_dtb"][j] = _gdn_gates_bwd(f"gdn_gates_b{j}", s["proj"], dg_h, db_h,
                                                                        P["gdn_alog"][j], P["gdn_dtb"][j])
            dpre, gP["gdn_cw"][j] = _gdn_conv_bwd(f"gdn_conv_b{j}", s["proj"], P["gdn_cw"][j],
                                                  jnp.concatenate([dq, dk, dv], axis=1))
            dproj = jnp.concatenate([dpre, dgate, dab_], axis=1)
            g_in = _mm(f"gdn_in_dw{j}", s["h"], dproj, "tn", out_dtype=BF16, tn=GDN_INK // 2)
            on_grads(l, "mix", dict(gdn_w_in=_uncols(_gdn_in_from_kernel(g_in)), gdn_w_out=_unrows(g_out)))
            dh = _mm(f"gdn_in_dx{j}", dproj, W["gdn_in"], "nt")
        else:
            do = _mm(f"mla_out_dx{j}", dy, W["mla_out"], "nt", out_dtype=BF16)
            g_out = _mm(f"mla_out_dw{j}", s["o"], dy, "tn", out_dtype=BF16)
            dqn, dqr, dkn, dv, dkr = _mla_attn_bwd(f"mla_attn_b{j}", s["qn"], s["qr"], s["kv"], s["kr"], do)
            dq = _mla_q_bwd(f"mla_q_b{j}", dqn, dqr, cos, sin)
            dkv = jnp.concatenate([dkn, dv], axis=1)
            g_uq = _mm(f"mla_uq_dw{j}", s["cqn"], dq, "tn", out_dtype=BF16)
            dcqn = _mm(f"mla_uq_dx{j}", dq, W["mla_uq"], "nt")
            g_ukv = _mm(f"mla_ukv_dw{j}", s["ckvn"], dkv, "tn", out_dtype=BF16)
            dckvn = _mm(f"mla_ukv_dx{j}", dkv, W["mla_ukv"], "nt")
            dproj, gP["mla_qg"][j], gP["mla_kvg"][j] = _mla_pre_bwd(f"mla_pre_b{j}", s["proj"], dcqn, dckvn, dkr, cos, sin,
                                                                     P["mla_qg"][j], P["mla_kvg"][j])
            g_in = _mm(f"mla_in_dw{j}", s["h"], dproj, "tn", out_dtype=BF16)
            on_grads(l, "mix", dict(mla_w_in=_unrows(g_in[:, :Q_RANK + KV_RANK + ROPE]), mla_w_uq=_uncols(_mla_uq_from_kernel(g_uq)),
                                    mla_w_ukv=_uncols(_mla_ukv_from_kernel(g_ukv)), mla_w_out=_unrows(g_out)))
            dh = _mm(f"mla_in_dx{j}", dproj, W["mla_in"], "nt")
        dx, d_sh_m, d_sc_m, gP["norm_mix_g"][l] = _norm_mod_bwd(f"norm_mix_b{l}", dh, s["x0"], dx,
                                                                 P["norm_mix_g"][l:l + 1], sc_m)
        dmod[l] = jnp.concatenate([d_sh_m, d_sc_m, d_ga_m, d_sh_f, d_sc_f, d_ga_f], axis=1)
    return dx, jnp.concatenate(dmod, axis=0), gP


def _pad_cols(a, width):
    return jnp.pad(a, ((0, 0), (0, width - a.shape[1])))


def _gdn_in_to_kernel(w):
    m = GDN_QKV + NH * HD
    return jnp.concatenate([w[:, :m], _pad_cols(w[:, m:m + NH], HD), _pad_cols(w[:, m + NH:], HD)], axis=1)


def _gdn_in_from_kernel(g):
    m = GDN_QKV + NH * HD
    return jnp.concatenate([g[:, :m], g[:, m:m + NH], g[:, m + HD:m + HD + NH]], axis=1)


def _mla_uq_to_kernel(w):
    w3 = w.reshape(Q_RANK, NH, HD + ROPE)
    rope = jnp.pad(w3[:, :, HD:], ((0, 0), (0, 0), (0, HD - ROPE)))
    return jnp.concatenate([w3[:, :, :HD].reshape(Q_RANK, NH * HD), rope.reshape(Q_RANK, NH * HD)], axis=1)


def _mla_uq_from_kernel(g):
    gn = g[:, :NH * HD].reshape(Q_RANK, NH, HD)
    gr = g[:, NH * HD:].reshape(Q_RANK, NH, HD)[:, :, :ROPE]
    return jnp.concatenate([gn, gr], axis=2).reshape(Q_RANK, NH * (HD + ROPE))


def _mla_ukv_to_kernel(w):
    w3 = w.reshape(KV_RANK, NH, 2 * HD)
    return jnp.concatenate([w3[:, :, :HD].reshape(KV_RANK, NH * HD), w3[:, :, HD:].reshape(KV_RANK, NH * HD)], axis=1)


def _mla_ukv_from_kernel(g):
    gk = g[:, :NH * HD].reshape(KV_RANK, NH, HD)
    gv = g[:, NH * HD:].reshape(KV_RANK, NH, HD)
    return jnp.concatenate([gk, gv], axis=2).reshape(KV_RANK, NH * 2 * HD)


def _cols(t):
    return jnp.moveaxis(t, 0, 1).reshape(t.shape[1], -1)


def _uncols(g):
    return jnp.moveaxis(g.reshape(g.shape[0], 4, -1), 1, 0)


def _rows(t):
    return t.reshape(-1, t.shape[2])


def _unrows(g):
    return g.reshape(4, -1, g.shape[1])


def _layer_weights(layer):
    mixer = ("gdn_w_in", "gdn_w_out") if layer % 2 == 0 else ("mla_w_in", "mla_w_uq", "mla_w_ukv", "mla_w_out")
    return [(n, layer // 2) for n in mixer] + [(n, layer) for n in ("ffn_w_gate", "ffn_w_up", "ffn_w_down")]


def _weights_to_kernel(layer, g):
    out = dict(ffn_g=g["ffn_w_gate"], ffn_u=g["ffn_w_up"], ffn_d=g["ffn_w_down"])
    if layer % 2 == 0:
        out.update(gdn_in=_gdn_in_to_kernel(_cols(g["gdn_w_in"])), gdn_out=_rows(g["gdn_w_out"]))
    else:
        out.update(mla_in=_pad_cols(_rows(g["mla_w_in"]), MLA_INK), mla_uq=_mla_uq_to_kernel(_cols(g["mla_w_uq"])),
                   mla_ukv=_mla_ukv_to_kernel(_cols(g["mla_w_ukv"])), mla_out=_rows(g["mla_w_out"]))
    return out


def _small_to_kernel(norm_mix_g, norm_ffn_g, final_norm_g, gdn_conv_w, gdn_a_log, gdn_dt_bias, gdn_norm_g, q_norm_g, kv_norm_g):
    return dict(
        norm_mix_g=norm_mix_g, norm_ffn_g=norm_ffn_g, final_g=final_norm_g.reshape(1, D),
        gdn_cw=[jnp.transpose(gdn_conv_w[j]) for j in range(2)],
        gdn_alog=[_pad_cols(gdn_a_log[j:j + 1], HD) for j in range(2)],
        gdn_dtb=[_pad_cols(gdn_dt_bias[j:j + 1], HD) for j in range(2)],
        gdn_ng=[gdn_norm_g[j:j + 1] for j in range(2)],
        mla_qg=[q_norm_g[j:j + 1] for j in range(2)],
        mla_kvg=[kv_norm_g[j:j + 1] for j in range(2)],
    )


_CHIP_FLIPS = ((1, 0), (0, 1), (1, 1))
_ANY = pl.BlockSpec(memory_space=pl.ANY)


def _me():
    return lax.axis_index("x"), lax.axis_index("y"), lax.axis_index("c")


def _chip_peer(dx, dy):
    x, y, c = _me()
    return ((1 - x) if dx else x, (1 - y) if dy else y, c)


def _rcopy(src, dst, send_sem, recv_sem, to):
    return pltpu.make_async_remote_copy(src_ref=src, dst_ref=dst, send_sem=send_sem, recv_sem=recv_sem,
                                        device_id=to, device_id_type=MESH)


def _allgather4(name, a, halves=False):
    R, C = a.shape
    rh = R // 2 if halves else R

    def body(a_ref, out_ref, send_sems, recv_sems, local_sem):
        x, y, c = _me()
        me = 2 * x + y
        src = a_ref.at[pl.ds(c * rh, rh)] if halves else a_ref
        local = pltpu.make_async_copy(src, out_ref.at[me], local_sem)
        local.start()
        sends = []
        for k, (dx, dy) in enumerate(_CHIP_FLIPS):
            cp = _rcopy(src, out_ref.at[me], send_sems.at[k], recv_sems.at[k], _chip_peer(dx, dy))
            cp.start()
            sends.append(cp)
        for k, (dx, dy) in enumerate(_CHIP_FLIPS):
            px, py, _ = _chip_peer(dx, dy)
            _rcopy(src, out_ref.at[2 * px + py], send_sems.at[k], recv_sems.at[k], _chip_peer(dx, dy)).wait_recv()
        for cp in sends:
            cp.wait_send()
        local.wait()

    return pl.pallas_call(
        body, name=name, in_specs=[_ANY], out_specs=_ANY, out_shape=jax.ShapeDtypeStruct((4, rh, C), a.dtype),
        scratch_shapes=[pltpu.SemaphoreType.DMA((3,)), pltpu.SemaphoreType.DMA((3,)), pltpu.SemaphoreType.DMA(())])(a)


_NCH = 4


def _dma_sems(*counts):
    return [pltpu.SemaphoreType.DMA((n,)) for n in counts]


def _slot_tile(rows, cap=512):
    best = rows
    for tr in range(16, min(rows, cap) + 1, 16):
        if rows % tr == 0:
            best = tr
    return best


def _cast_into_slot(name, a, chip, row0, rows):
    C = a.shape[1]
    tr = _slot_tile(rows)
    assert row0 % tr == 0
    first = row0 // tr

    def body(c_ref, a_ref, o_ref):
        o_ref[0] = a_ref[...].astype(o_ref.dtype)

    grid_spec = pltpu.PrefetchScalarGridSpec(
        num_scalar_prefetch=1, grid=(rows // tr,), in_specs=[pl.BlockSpec((tr, C), lambda i, c_ref: (first + i, 0))],
        out_specs=pl.BlockSpec((1, tr, C), lambda i, c_ref: (c_ref[0], i, 0)))
    return pl.pallas_call(body, name=name, grid_spec=grid_spec, out_shape=jax.ShapeDtypeStruct((4, rows, C), BF16),
                          compiler_params=_params(1))(chip, a)


def _chunks(rows, align):
    for nch in (_NCH, 2):
        if rows % (nch * align) == 0:
            return nch
    return 1


def _gather_exchange(out, ici_s, ici_r, d2d_s, d2d_r):
    n = len(out)
    x, y, c = _me()
    me = 2 * x + y
    sib = (x, y, 1 - c)
    peers = [_chip_peer(dx, dy) for dx, dy in _CHIP_FLIPS]
    for t in range(n):
        h = out[t].shape[1] // 2
        nch = _chunks(h, 16)
        ch = h // nch
        for k, peer in enumerate(peers):
            for i in range(nch):
                blk = out[t].at[me, pl.ds(c * h + i * ch, ch)]
                _rcopy(blk, blk, ici_s.at[3 * t + k], ici_r.at[3 * t + k], peer).start()
    for t in range(n):
        h = out[t].shape[1] // 2
        nch = _chunks(h, 16)
        ch = h // nch
        for k, peer in enumerate(peers):
            pchip = 2 * peer[0] + peer[1]
            got = out[t].at[pchip, pl.ds(c * h, h)]
            _rcopy(got, got, ici_s.at[3 * t + k], ici_r.at[3 * t + k], peer).wait_recv()
            for i in range(nch):
                blk = out[t].at[pchip, pl.ds(c * h + i * ch, ch)]
                _rcopy(blk, blk, d2d_s.at[3 * t + k], d2d_r.at[3 * t + k], sib).start()
    for t in range(n):
        h = out[t].shape[1] // 2
        for k, peer in enumerate(peers):
            pchip = 2 * peer[0] + peer[1]
            other = out[t].at[pchip, pl.ds((1 - c) * h, h)]
            _rcopy(other, other, d2d_s.at[3 * t + k], d2d_r.at[3 * t + k], sib).wait_recv()
            _rcopy(other, other, ici_s.at[3 * t + k], ici_r.at[3 * t + k], peer).wait_send()
            _rcopy(other, other, d2d_s.at[3 * t + k], d2d_r.at[3 * t + k], sib).wait_send()


def _gather_weights(name, bufs):
    n = len(bufs)

    def body(*refs):
        _gather_exchange(refs[n:2 * n], *refs[2 * n:])

    return pl.pallas_call(
        body, name=name, in_specs=[_ANY] * n, out_specs=[_ANY] * n,
        out_shape=[jax.ShapeDtypeStruct(s.shape, s.dtype) for s in bufs],
        input_output_aliases={t: t for t in range(n)},
        scratch_shapes=_dma_sems(3 * n, 3 * n, 3 * n, 3 * n))(*bufs)


def _gather_weights_async(name, collective_id, bufs):
    n = len(bufs)
    refs = [jax.new_ref(b, memory_space=pltpu.MemorySpace.HBM) for b in bufs]

    @pl.kernel(mesh=plsc.ScalarSubcoreMesh(axis_name="sequencer", num_cores=1), name=name,
               scratch_types=tuple(_dma_sems(3 * n, 3 * n, 3 * n, 3 * n)),
               compiler_params=pltpu.CompilerParams(collective_id=collective_id))
    def launch(ici_s, ici_r, d2d_s, d2d_r):
        x, y, c = _me()
        barrier = pltpu.get_barrier_semaphore()
        for peer in [_chip_peer(dx, dy) for dx, dy in _CHIP_FLIPS] + [(x, y, 1 - c)]:
            pl.semaphore_signal(barrier, inc=1, device_id=peer, device_id_type=MESH)
        pl.semaphore_wait(barrier, 4)
        _gather_exchange(refs, ici_s, ici_r, d2d_s, d2d_r)

    launch()
    return [r[...] for r in refs]


def _rs_split(name, grads):
    n = len(grads)

    def body(*refs):
        g, out = refs[:n], refs[n:2 * n]
        send, recv = refs[2 * n:]
        x, y, c = _me()
        sib = (x, y, 1 - c)
        for t in range(n):
            h = g[t].shape[1] // 2
            for d in range(4):
                _rcopy(g[t].at[d, pl.ds((1 - c) * h, h)], out[t].at[d], send.at[t], recv.at[t], sib).start()
        for t in range(n):
            _rcopy(out[t], out[t], send.at[t], recv.at[t], sib).wait()

    return pl.pallas_call(
        body, name=name, in_specs=[_ANY] * n, out_specs=[_ANY] * n,
        out_shape=[jax.ShapeDtypeStruct((4, s.shape[1] // 2, s.shape[2]), s.dtype) for s in grads],
        scratch_shapes=_dma_sems(n, n))(*grads)


def _pair_add(name, g, theirs, core_chip):
    _, R, C = g.shape
    h = R // 2
    tr = _slot_tile(h)
    nb = h // tr

    def body(s_ref, g_ref, t_ref, p_ref, o_ref):
        val = (g_ref[...].astype(F32) + t_ref[...].astype(F32)).astype(p_ref.dtype)
        p_ref[...] = val

        @pl.when(pl.program_id(1) == s_ref[1])
        def _():
            o_ref[...] = val

    spec = pl.BlockSpec((1, tr, C), lambda i, d, s_ref: (d, i, 0))
    grid_spec = pltpu.PrefetchScalarGridSpec(
        num_scalar_prefetch=1, grid=(nb, 4),
        in_specs=[pl.BlockSpec((1, tr, C), lambda i, d, s_ref: (d, s_ref[0] * nb + i, 0)), spec],
        out_specs=[spec, pl.BlockSpec((1, tr, C), lambda i, d, s_ref: (s_ref[1], i, 0))])
    half = jax.ShapeDtypeStruct((4, h, C), BF16)
    return pl.pallas_call(body, name=name, grid_spec=grid_spec, out_shape=[half, half],
                          compiler_params=_params(2))(core_chip, g, theirs)


def _rs_alltoall_async(name, collective_id, parts, bufs):
    n = len(parts)
    p = [jax.new_ref(a, memory_space=pltpu.MemorySpace.HBM) for a in parts]
    out = [jax.new_ref(b, memory_space=pltpu.MemorySpace.HBM) for b in bufs]

    @pl.kernel(mesh=plsc.ScalarSubcoreMesh(axis_name="sequencer", num_cores=1), name=name,
               scratch_types=tuple(_dma_sems(3 * n, 3 * n)),
               compiler_params=pltpu.CompilerParams(collective_id=collective_id))
    def launch(send, recv):
        barrier = pltpu.get_barrier_semaphore()
        for peer in [_chip_peer(dx, dy) for dx, dy in _CHIP_FLIPS]:
            pl.semaphore_signal(barrier, inc=1, device_id=peer, device_id_type=MESH)
        pl.semaphore_wait(barrier, 3)
        _alltoall_exchange(p, out, send, recv)

    launch()
    return [r[...] for r in out]


def _alltoall_exchange(p, out, send, recv):
    x, y, c = _me()
    me = 2 * x + y
    peers = [_chip_peer(dx, dy) for dx, dy in _CHIP_FLIPS]
    for t in range(len(p)):
        h = p[t].shape[1]
        nch = _chunks(h, 16)
        ch = h // nch
        for k, peer in enumerate(peers):
            pchip = 2 * peer[0] + peer[1]
            for i in range(nch):
                rows = pl.ds(i * ch, ch)
                _rcopy(p[t].at[pchip, rows], out[t].at[me, rows], send.at[3 * t + k], recv.at[3 * t + k], peer).start()
    for t in range(len(p)):
        for k, peer in enumerate(peers):
            pchip = 2 * peer[0] + peer[1]
            _rcopy(out[t].at[pchip], out[t].at[pchip], send.at[3 * t + k], recv.at[3 * t + k], peer).wait()


def _rs_swap(name, halves):
    n = len(halves)

    def body(*refs):
        a, out = refs[:n], refs[n:2 * n]
        send, recv = refs[2 * n:]
        x, y, c = _me()
        sib = (x, y, 1 - c)
        for t in range(n):
            ch = a[t].shape[0] // _NCH
            for i in range(_NCH):
                rows = pl.ds(i * ch, ch)
                _rcopy(a[t].at[rows], out[t].at[rows], send.at[t], recv.at[t], sib).start()
        for t in range(n):
            _rcopy(a[t], out[t], send.at[t], recv.at[t], sib).wait()

    return pl.pallas_call(
        body, name=name, in_specs=[_ANY] * n, out_specs=[_ANY] * n,
        out_shape=[jax.ShapeDtypeStruct(s.shape, s.dtype) for s in halves],
        scratch_shapes=_dma_sems(n, n))(*halves)


def _sibling_merge(name, a):
    P_, rh, C = a.shape

    def body(a_ref, out_ref, send_sem, recv_sem, local_sem):
        x, y, c = _me()
        local = pltpu.make_async_copy(a_ref, out_ref.at[:, pl.ds(c * rh, rh)], local_sem)
        local.start()
        cp = _rcopy(a_ref, out_ref.at[:, pl.ds(c * rh, rh)], send_sem, recv_sem, (x, y, 1 - c))
        cp.start()
        cp.wait_send()
        _rcopy(a_ref, out_ref.at[:, pl.ds((1 - c) * rh, rh)], send_sem, recv_sem, (x, y, 1 - c)).wait_recv()
        local.wait()

    return pl.pallas_call(
        body, name=name, in_specs=[_ANY], out_specs=_ANY, out_shape=jax.ShapeDtypeStruct((P_, 2 * rh, C), a.dtype),
        scratch_shapes=[pltpu.SemaphoreType.DMA(()), pltpu.SemaphoreType.DMA(()), pltpu.SemaphoreType.DMA(())])(a)


def _allgather8(name, a):
    g4 = _allgather4(name + "_chips", a)
    both = _sibling_merge(name + "_cores", g4.reshape(1, 4 * a.shape[0], a.shape[1]))
    return jnp.transpose(both.reshape(2, 4, *a.shape), (1, 0, 2, 3)).reshape(8, *a.shape)


def _sum_slots(name, a, out_dtype):
    def fn(a):
        acc = a[0].astype(F32)
        for k in range(1, a.shape[0]):
            acc = acc + a[k].astype(F32)
        return acc
    return _rowwise(name, fn, [a], [], [(a.shape[2], out_dtype)])[0]


def _adamw_math(w, g, m, v):
    m = ADAM_B1 * m + (1.0 - ADAM_B1) * g
    v = ADAM_B2 * v + (1.0 - ADAM_B2) * (g * g)
    m_hat = m / (1.0 - ADAM_B1 ** ADAM_STEP)
    v_hat = v / (1.0 - ADAM_B2 ** ADAM_STEP)
    return -ADAM_LR * (m_hat / (jnp.sqrt(v_hat) + ADAM_EPS) + ADAM_WD * w), m, v


def _adamw_piece(name, w2, m2, v2, mine, theirs, row0, prev):
    R, C = w2.shape
    h = mine.shape[0]
    tr = _slot_tile(h, 256)
    nb = h // tr
    assert row0 % tr == 0
    first = row0 // tr

    def body(w_ref, m_ref, v_ref, a_ref, b_ref, *rest):
        g_ref, d_ref, nm_ref, nv_ref = rest[-4:]
        g = jnp.where(pl.program_id(0) == lax.axis_index("c"), a_ref[...], b_ref[...])
        g_ref[...] = g
        d_ref[...], nm_ref[...], nv_ref[...] = _adamw_math(w_ref[...], g, m_ref[...], v_ref[...])

    full = pl.BlockSpec((tr, C), lambda s, i: (first + s * nb + i, 0))
    half = pl.BlockSpec((tr, C), lambda s, i: (i, 0))
    extra = [] if prev is None else list(prev)
    return pl.pallas_call(
        body, name=name, grid=(2, nb), in_specs=[full, full, full, half, half] + [_ANY] * len(extra), out_specs=[full] * 4,
        out_shape=[jax.ShapeDtypeStruct((R, C), F32)] * 4, input_output_aliases={5 + k: k for k in range(len(extra))},
        compiler_params=_params(2))(w2, m2, v2, mine, theirs, *extra)


def _adamw(name, w, g, m, v):
    shape = w.shape
    two_d = (-1, shape[-1]) if w.ndim > 1 else (1, -1)
    w2, g2, m2, v2 = [t.reshape(two_d) for t in (w, g, m, v)]
    rows = w2.shape[0]
    tr = rows
    for cand in (256, 128, 64, 32, 16, 8):
        if rows % cand == 0:
            tr = cand
            break

    c = w2.shape[1]
    outs = _rowwise(name, _adamw_math, [w2, g2, m2, v2], [], [(c, F32)] * 3, tr=tr)
    return [o.reshape(shape) for o in outs]


_WEIGHT_ORDER = ("ada_w", "ada_b", "norm_mix_g", "norm_ffn_g", "gdn_w_in", "gdn_conv_w", "gdn_a_log", "gdn_dt_bias",
                 "gdn_norm_g", "gdn_w_out", "mla_w_in", "mla_q_norm_g", "mla_kv_norm_g", "mla_w_uq", "mla_w_ukv",
                 "mla_w_out", "ffn_w_gate", "ffn_w_up", "ffn_w_down", "final_norm_g")
_BIG = (("gdn_w_in", 2), ("gdn_w_out", 1), ("mla_w_in", 1), ("mla_w_uq", 2), ("mla_w_ukv", 2), ("mla_w_out", 1),
        ("ffn_w_gate", 2), ("ffn_w_up", 2), ("ffn_w_down", 1))
_SMALL_SHARDED = (("gdn_conv_w", 1), ("mla_q_norm_g", 1), ("mla_kv_norm_g", 1))


def _size(shape):
    n = 1
    for s in shape:
        n *= s
    return n


def _pack_rows_each(tensors):
    parts, offs, off = [], [], 0
    for t in tensors:
        flat = t.reshape(-1).astype(F32)
        rows = -(-flat.shape[0] // PACK_W)
        parts.append(jnp.pad(flat, (0, rows * PACK_W - flat.shape[0])).reshape(rows, PACK_W))
        offs.append(off)
        off += rows
    total = -(-off // 16) * 16
    pack = jnp.pad(parts[0], ((offs[0], total - offs[0] - parts[0].shape[0]), (0, 0)))
    for p, o in zip(parts[1:], offs[1:]):
        pack = pack + jnp.pad(p, ((o, total - o - p.shape[0]), (0, 0)))
    return pack, offs


def _unpack_rows_each(pack, shapes):
    lead = pack.shape[:-2]
    out, off = [], 0
    for shp in shapes:
        n = _size(shp)
        rows = -(-n // PACK_W)
        out.append(pack[..., off:off + rows, :].reshape(*lead, -1)[..., :n].reshape(*lead, *shp))
        off += rows
    return out


def _merge_chips(stacked, axis):
    moved = jnp.moveaxis(stacked, 0, axis)
    shp = list(moved.shape)
    return moved.reshape(shp[:axis] + [shp[axis] * shp[axis + 1]] + shp[axis + 2:])


def _my_shard(full, axis, chip):
    n = full.shape[axis] // 4
    return lax.dynamic_slice_in_dim(full, chip * n, n, axis)


def kernel(x, c, positions, ada_w, ada_b, norm_mix_g, norm_ffn_g, gdn_w_in, gdn_conv_w, gdn_a_log, gdn_dt_bias, gdn_norm_g, gdn_w_out, mla_w_in, mla_q_norm_g, mla_kv_norm_g, mla_w_uq, mla_w_ukv, mla_w_out, ffn_w_gate, ffn_w_up, ffn_w_down, final_norm_g, loss_target, m_ada_w, m_ada_b, m_norm_mix_g, m_norm_ffn_g, m_gdn_w_in, m_gdn_conv_w, m_gdn_a_log, m_gdn_dt_bias, m_gdn_norm_g, m_gdn_w_out, m_mla_w_in, m_mla_q_norm_g, m_mla_kv_norm_g, m_mla_w_uq, m_mla_w_ukv, m_mla_w_out, m_ffn_w_gate, m_ffn_w_up, m_ffn_w_down, m_final_norm_g, v_ada_w, v_ada_b, v_norm_mix_g, v_norm_ffn_g, v_gdn_w_in, v_gdn_conv_w, v_gdn_a_log, v_gdn_dt_bias, v_gdn_norm_g, v_gdn_w_out, v_mla_w_in, v_mla_q_norm_g, v_mla_kv_norm_g, v_mla_w_uq, v_mla_w_ukv, v_mla_w_out, v_ffn_w_gate, v_ffn_w_up, v_ffn_w_down, v_final_norm_g):
    w = dict(ada_w=ada_w, ada_b=ada_b, norm_mix_g=norm_mix_g, norm_ffn_g=norm_ffn_g, gdn_w_in=gdn_w_in, gdn_conv_w=gdn_conv_w,
             gdn_a_log=gdn_a_log, gdn_dt_bias=gdn_dt_bias, gdn_norm_g=gdn_norm_g, gdn_w_out=gdn_w_out, mla_w_in=mla_w_in,
             mla_q_norm_g=mla_q_norm_g, mla_kv_norm_g=mla_kv_norm_g, mla_w_uq=mla_w_uq, mla_w_ukv=mla_w_ukv,
             mla_w_out=mla_w_out, ffn_w_gate=ffn_w_gate, ffn_w_up=ffn_w_up, ffn_w_down=ffn_w_down, final_norm_g=final_norm_g)
    m = dict(ada_w=m_ada_w, ada_b=m_ada_b, norm_mix_g=m_norm_mix_g, norm_ffn_g=m_norm_ffn_g, gdn_w_in=m_gdn_w_in,
             gdn_conv_w=m_gdn_conv_w, gdn_a_log=m_gdn_a_log, gdn_dt_bias=m_gdn_dt_bias, gdn_norm_g=m_gdn_norm_g,
             gdn_w_out=m_gdn_w_out, mla_w_in=m_mla_w_in, mla_q_norm_g=m_mla_q_norm_g, mla_kv_norm_g=m_mla_kv_norm_g,
             mla_w_uq=m_mla_w_uq, mla_w_ukv=m_mla_w_ukv, mla_w_out=m_mla_w_out, ffn_w_gate=m_ffn_w_gate,
             ffn_w_up=m_ffn_w_up, ffn_w_down=m_ffn_w_down, final_norm_g=m_final_norm_g)
    v = dict(ada_w=v_ada_w, ada_b=v_ada_b, norm_mix_g=v_norm_mix_g, norm_ffn_g=v_norm_ffn_g, gdn_w_in=v_gdn_w_in,
             gdn_conv_w=v_gdn_conv_w, gdn_a_log=v_gdn_a_log, gdn_dt_bias=v_gdn_dt_bias, gdn_norm_g=v_gdn_norm_g,
             gdn_w_out=v_gdn_w_out, mla_w_in=v_mla_w_in, mla_q_norm_g=v_mla_q_norm_g, mla_kv_norm_g=v_mla_kv_norm_g,
             mla_w_uq=v_mla_w_uq, mla_w_ukv=v_mla_w_ukv, mla_w_out=v_mla_w_out, ffn_w_gate=v_ffn_w_gate,
             ffn_w_up=v_ffn_w_up, ffn_w_down=v_ffn_w_down, final_norm_g=v_final_norm_g)
    T = x.shape[1]
    ix, iy, ic = _me()
    chip = 2 * ix + iy
    seq = 2 * chip + ic
    n_dev = 8

    small_shapes = [w[n].shape for n, _ in _SMALL_SHARDED] + [c.shape]
    pack0, _ = _pack_rows_each([w[n] for n, _ in _SMALL_SHARDED] + [c])
    got0 = _unpack_rows_each(_allgather8("gather_small", pack0), small_shapes)
    small_full = {n: _merge_chips(g[0::2], ax) for (n, ax), g in zip(_SMALL_SHARDED, got0)}
    c_all = got0[-1].reshape(n_dev, D)

    big = [n for n, _ in _BIG]
    chip_arr = chip.astype(jnp.int32).reshape(1)
    gathered = {}

    gathered = []
    for l in range(DEPTH):
        names = _layer_weights(l)
        bufs = [_cast_into_slot(f"to_bf16_{n}{l}", w[n].reshape(-1, w[n].shape[-1]), chip_arr, j * w[n].shape[1], w[n].shape[1])
                for n, j in names]
        filled = _gather_weights("gather_weights0", bufs) if l == 0 else _gather_weights_async(f"gather_weights{l}", l, bufs)
        gathered.append({n: b for (n, _), b in zip(names, filled)})

    def weights_of(l, h):
        return _weights_to_kernel(l, gathered[l])

    P = _small_to_kernel(norm_mix_g, norm_ffn_g, final_norm_g, small_full["gdn_conv_w"], gdn_a_log, gdn_dt_bias,
                         gdn_norm_g, small_full["mla_q_norm_g"], small_full["mla_kv_norm_g"])

    c16 = jnp.pad(c_all, ((0, 16 - n_dev), (0, 0)))
    ca = _rowwise("cond_silu", lambda t: t * _sig(t), [c16], [], [(D, BF16)])[0]
    n_ada = ada_w.shape[2]
    mods = jnp.concatenate([_mm(f"ada_fwd{l}", ca, ada_w[l], "nn") for l in range(DEPTH)], axis=0)
    mods_all = _allgather4("gather_mod", mods).reshape(4, DEPTH, 16, n_ada)
    mod_mm = jnp.transpose(lax.dynamic_index_in_dim(mods_all, seq, axis=2, keepdims=False), (1, 0, 2)).reshape(DEPTH, 4 * n_ada)
    mod = _rowwise("mod_bias", lambda a, b: a + b, [mod_mm, ada_b], [], [(4 * n_ada, F32)])[0]

    core_chip = jnp.stack([ic, chip]).astype(jnp.int32)
    pending, in_flight = {}, []

    def reduce_group(layer, part, pieces):
        pending.update({(n, layer if n.startswith("ffn_") else layer // 2): g for n, g in pieces.items()})
        if part == "ffn" and layer > 0:
            return
        keys = list(pending)
        glist = [pending.pop(k) for k in keys]
        tag = f"{layer}{part}"
        theirs = _rs_split("grads_cores_" + tag, glist)
        both = [_pair_add(f"grads_pair_{n}{l}", g, t, core_chip) for (n, l), g, t in zip(keys, glist, theirs)]
        swapped = _rs_alltoall_async("grads_chips_" + tag, DEPTH + 1 + len(in_flight), [p for p, _ in both], [o for _, o in both])
        in_flight.append((tag, keys, swapped))

    dx, dmod, gP = _local_step(x.reshape(T, D), loss_target.reshape(T, D), positions.reshape(T, 1), mod, weights_of, P, reduce_group)

    partials = [dmod, jnp.concatenate(gP["norm_mix_g"]), jnp.concatenate(gP["norm_ffn_g"]), gP["final_g"],
                jnp.stack([jnp.transpose(g) for g in gP["gdn_cw"]]), jnp.concatenate(gP["gdn_alog"])[:, :NH],
                jnp.concatenate(gP["gdn_dtb"])[:, :NH], jnp.concatenate(gP["gdn_ng"]), jnp.concatenate(gP["mla_qg"]),
                jnp.concatenate(gP["mla_kvg"]), gP["loss"][:, :1]]
    part_shapes = [p.shape for p in partials]
    ppack, _ = _pack_rows_each(partials)
    pall = _allgather8("gather_partials", ppack)
    psum = _sum_slots("sum_partials", pall, F32)
    (g_ada_b, g_norm_mix, g_norm_ffn, g_final, g_conv_full, g_alog, g_dtb, g_gdn_ng, g_qg_full, g_kvg_full,
     loss_sum) = _unpack_rows_each(psum, part_shapes)
    dmod_all = _unpack_rows_each(pall, part_shapes[:1])[0]

    grads = dict(ada_b=g_ada_b, norm_mix_g=g_norm_mix, norm_ffn_g=g_norm_ffn, final_norm_g=g_final.reshape(D),
                 gdn_conv_w=_my_shard(g_conv_full, 1, chip), gdn_a_log=g_alog, gdn_dt_bias=g_dtb, gdn_norm_g=g_gdn_ng,
                 mla_q_norm_g=_my_shard(g_qg_full, 1, chip), mla_kv_norm_g=_my_shard(g_kvg_full, 1, chip))

    ca_t = jnp.zeros((D, LANES), BF16).at[:, :16].set(jnp.transpose(ca))
    dm_mine = lax.dynamic_slice_in_dim(dmod_all, chip * n_ada, n_ada, axis=2)
    grads["ada_w"] = jnp.stack([
        _mm(f"ada_bwd{l}", ca_t, jnp.pad(dm_mine[:, l], ((0, LANES - n_dev), (0, 0))), "nn") for l in range(DEPTH)])

    delta, new_m, new_v = {}, {}, {}
    two_d = lambda t: t.reshape(-1, t.shape[-1])
    results = {}
    for tag, keys, swapped in in_flight:
        halves = [_sum_slots(f"grads_sum_{n}{l}", s, F32) for (n, l), s in zip(keys, swapped)]
        others = _rs_swap("grads_swap_" + tag, halves)
        for (n, l), mine, theirs in zip(keys, halves, others):
            results[n] = _adamw_piece(f"adamw_{n}{l}", two_d(w[n]), two_d(m[n]), two_d(v[n]), mine, theirs,
                                      l * w[n].shape[1], results.get(n))
    for n in big:
        grads[n], delta[n], new_m[n], new_v[n] = [t.reshape(w[n].shape) for t in results[n]]
    delta["ada_w"], new_m["ada_w"], new_v["ada_w"] = _adamw("adamw_ada_w", ada_w, grads["ada_w"], m_ada_w, v_ada_w)
    small_names = [n for n in _WEIGHT_ORDER if n not in delta]
    small_shapes = [w[n].shape for n in small_names]
    packs = [_pack_rows_each([d[n] for n in small_names])[0] for d in (w, grads, m, v)]
    for d, pk in zip((delta, new_m, new_v), _adamw("adamw_small", *packs)):
        for n, t in zip(small_names, _unpack_rows_each(pk, small_shapes)):
            d[n] = t

    loss = loss_sum.reshape(())
    return (loss, dx.reshape(1, T, D), *[grads[n] for n in _WEIGHT_ORDER], *[delta[n] for n in _WEIGHT_ORDER],
            *[new_m[n] for n in _WEIGHT_ORDER], *[new_v[n] for n in _WEIGHT_ORDER])
```

```python
import functools

import jax
import jax.numpy as jnp
from jax import lax
from jax.experimental import pallas as pl
from jax.experimental.pallas import tpu as pltpu
from jax.experimental.pallas import tpu_sc as plsc

F32 = jnp.float32
BF16 = jnp.bfloat16
HI = lax.Precision.HIGHEST
MESH = pl.DeviceIdType.MESH

D = 1024
DEPTH = 4
N_MOD = 6
NH = 8
HD = 128
CHUNK = 64
_GDN_HB = 4
GDN_QKV = 3 * NH * HD
GDN_INK = GDN_QKV + NH * HD + 2 * HD
Q_RANK, KV_RANK, ROPE = 384, 256, 64
MLA_INK = Q_RANK + KV_RANK + HD
DFF = 2816
EPS = 1e-6
ATT_SCALE = (HD + ROPE) ** -0.5
ROPE_THETA = 10000.0
LANES = 128
PACK_W = 1024

ADAM_LR, ADAM_B1, ADAM_B2, ADAM_EPS, ADAM_WD, ADAM_STEP = 0.001, 0.9, 0.999, 1e-08, 0.01, 10


H3 = "bf16x3"
B1 = "bf16"
HS = H3
HF = B1


def _dot(a, b, mode="nn", prec=None):
    dn = {"nn": (((1,), (0,)), ((), ())), "nt": (((1,), (1,)), ((), ())), "tn": (((0,), (0,)), ((), ()))}[mode]
    if prec == B1:
        return _dot(a.astype(BF16), b.astype(BF16), mode)
    if prec == H3:
        ah, bh = a.astype(BF16), b.astype(BF16)
        al, bl = (a - ah.astype(F32)).astype(BF16), (b - bh.astype(F32)).astype(BF16)
        return _dot(ah, bh, mode) + (_dot(ah, bl, mode) + _dot(al, bh, mode))
    return lax.dot_general(a, b, dn, precision=prec, preferred_element_type=F32)


def _sig(x):
    return 1.0 / (1.0 + jnp.exp(-x))


def _pick(n, cap):
    if n <= cap:
        return n
    best = None
    for d in range(LANES, cap + 1, LANES):
        if n % d == 0:
            best = d
    assert best is not None, (n, cap)
    return best


def _params(n_grid):
    return pltpu.CompilerParams(dimension_semantics=("arbitrary",) * n_grid, vmem_limit_bytes=56 * 1024 * 1024)


def _rowwise(name, fn, rows, consts, outs, sums=(), tr=256):
    first = rows[0][0] if isinstance(rows[0], tuple) else rows[0]
    T = first.shape[-2]
    tr = _slot_tile(T, tr)
    nr, nc, no, ns = len(rows), len(consts), len(outs), len(sums)

    def body(*refs):
        res = fn(*[r[...] for r in refs[:nr + nc]])
        if not isinstance(res, (tuple, list)):
            res = (res,)
        o_refs = refs[nr + nc:nr + nc + no]
        s_refs = refs[nr + nc + no:]
        for r, val in zip(o_refs, res[:no]):
            r[...] = val.astype(r.dtype)
        if ns:
            @pl.when(pl.program_id(0) == 0)
            def _():
                for r in s_refs:
                    r[...] = jnp.zeros_like(r)
            for r, val in zip(s_refs, res[no:]):
                r[...] += val

    in_specs, args = [], []
    for a in rows:
        if isinstance(a, tuple):
            arr, width, cb = a
            in_specs.append(pl.BlockSpec((tr, width), lambda i, cb=cb: (i, cb)))
            args.append(arr)
        elif a.ndim == 3:
            in_specs.append(pl.BlockSpec((a.shape[0], tr, a.shape[2]), lambda i: (0, i, 0)))
            args.append(a)
        else:
            in_specs.append(pl.BlockSpec((tr, a.shape[1]), lambda i: (i, 0)))
            args.append(a)
    for a in consts:
        in_specs.append(pl.BlockSpec(a.shape, lambda i, nd=a.ndim: (0,) * nd))
        args.append(a)
    out_specs = [pl.BlockSpec((tr, w), lambda i: (i, 0)) for w, _ in outs]
    out_specs += [pl.BlockSpec((1, w), lambda i: (0, 0)) for w in sums]
    out_shape = [jax.ShapeDtypeStruct((T, w), dt) for w, dt in outs]
    out_shape += [jax.ShapeDtypeStruct((1, w), F32) for w in sums]
    res = pl.pallas_call(body, name=name, grid=(T // tr,), in_specs=in_specs, out_specs=out_specs,
                         out_shape=out_shape, compiler_params=_params(1))(*args)
    return res


def _mm(name, a, b, mode, out_dtype=F32, tm=512, tn=1024):
    if mode == "tn":
        K, M = a.shape
    else:
        M, K = a.shape
    N = b.shape[0] if mode == "nt" else b.shape[1]
    tm, tn = _pick(M, tm), _pick(N, tn)

    def body(a_ref, b_ref, o_ref):
        o_ref[...] = _dot(a_ref[...].astype(BF16), b_ref[...].astype(BF16), mode).astype(o_ref.dtype)

    a_spec = pl.BlockSpec((K, tm), lambda i, j: (0, i)) if mode == "tn" else pl.BlockSpec((tm, K), lambda i, j: (i, 0))
    b_spec = pl.BlockSpec((tn, K), lambda i, j: (j, 0)) if mode == "nt" else pl.BlockSpec((K, tn), lambda i, j: (0, j))
    return pl.pallas_call(body, name=name, grid=(M // tm, N // tn), in_specs=[a_spec, b_spec],
                          out_specs=pl.BlockSpec((tm, tn), lambda i, j: (i, j)),
                          out_shape=jax.ShapeDtypeStruct((M, N), out_dtype), compiler_params=_params(2))(a, b)


def _rms(x, eps=EPS):
    return lax.rsqrt(jnp.mean(x * x, axis=-1, keepdims=True) + eps)


def _norm_mod_fwd(name, x, g, scale, shift):
    def fn(x, g, scale, shift):
        return x * _rms(x) * g * (1.0 + scale) + shift
    return _rowwise(name, fn, [x], [g, scale, shift], [(D, BF16)])[0]


def _norm_mod_bwd(name, dh, x, dx_res, g, scale):
    def fn(dh, x, dx_res, g, scale):
        r = _rms(x)
        xh = x * r
        dxh = dh * (g * (1.0 + scale))
        dx = r * (dxh - xh * jnp.mean(dxh * xh, axis=-1, keepdims=True))
        dhx = dh * xh
        return (dx_res + dx, jnp.sum(dh, axis=0, keepdims=True), jnp.sum(dhx * g, axis=0, keepdims=True),
                jnp.sum(dhx * (1.0 + scale), axis=0, keepdims=True))
    return _rowwise(name, fn, [dh, x, dx_res], [g, scale], [(D, F32)], sums=[D, D, D])


def _residual_fwd(name, x, y, gate):
    def fn(x, y, gate):
        return x + gate * y
    return _rowwise(name, fn, [x, y], [gate], [(D, F32)])[0]


def _residual_bwd(name, dx, y, gate):
    def fn(dx, y, gate):
        return dx * gate, jnp.sum(dx * y, axis=0, keepdims=True)
    return _rowwise(name, fn, [dx, y], [gate], [(D, BF16)], sums=[D])


def _loss_head(x, target, g):
    def fn(x, t, g):
        r = _rms(x)
        xh = x * r
        err = xh * g - t
        loss = 0.5 * jnp.sum(jnp.mean(err * err, axis=-1, keepdims=True), axis=0, keepdims=True)
        dy = err * (1.0 / D)
        dxh = dy * g
        dx = r * (dxh - xh * jnp.mean(dxh * xh, axis=-1, keepdims=True))
        return dx, jnp.broadcast_to(loss, (1, LANES)), jnp.sum(dy * xh, axis=0, keepdims=True)
    return _rowwise("loss_head", fn, [x, target], [g], [(D, F32)], sums=[LANES, D])


def _ffn_up(name, h, wg, wu, layer, tm=1024):
    T, n = h.shape[0], wg.shape[2]
    tm = min(tm, T)

    def body(h_ref, wg_ref, wu_ref, a_ref, b_ref, s_ref):
        h = h_ref[...]
        a = _dot(h, wg_ref[0], "nn")
        b = _dot(h, wu_ref[0], "nn")
        a_ref[0] = a
        b_ref[0] = b
        s_ref[0] = (a * _sig(a) * b).astype(s_ref.dtype)

    wspec = pl.BlockSpec((1, D, n), lambda ch, i: (ch, layer, 0))
    ospec = pl.BlockSpec((1, tm, n), lambda ch, i: (ch, i, 0))
    return pl.pallas_call(
        body, name=name, grid=(4, T // tm), in_specs=[pl.BlockSpec((tm, D), lambda ch, i: (i, 0)), wspec, wspec],
        out_specs=[ospec, ospec, ospec],
        out_shape=[jax.ShapeDtypeStruct((4, T, n), F32)] * 2 + [jax.ShapeDtypeStruct((4, T, n), BF16)],
        compiler_params=_params(2))(h, wg, wu)


def _ffn_down(name, s, wd, layer, tm=1024):
    _, T, n = s.shape
    tm = min(tm, T)

    def body(s_ref, w_ref, y_ref):
        @pl.when(pl.program_id(1) == 0)
        def _():
            y_ref[...] = jnp.zeros_like(y_ref)
        y_ref[...] += _dot(s_ref[0], w_ref[0], "nn")

    return pl.pallas_call(
        body, name=name, grid=(T // tm, 4),
        in_specs=[pl.BlockSpec((1, tm, n), lambda i, ch: (ch, i, 0)), pl.BlockSpec((1, n, D), lambda i, ch: (ch, layer, 0))],
        out_specs=pl.BlockSpec((tm, D), lambda i, ch: (i, 0)), out_shape=jax.ShapeDtypeStruct((T, D), F32),
        compiler_params=_params(2))(s, wd)


def _ffn_down_bwd(name, dy, wd, a, b, layer, tm=1024):
    _, T, n = a.shape
    tm = min(tm, T)

    def body(dy_ref, w_ref, a_ref, b_ref, da_ref, db_ref):
        ds = _dot(dy_ref[...], w_ref[0], "nt")
        a, b = a_ref[0], b_ref[0]
        sg = _sig(a)
        da_ref[0] = (ds * b * (sg * (1.0 + a * (1.0 - sg)))).astype(da_ref.dtype)
        db_ref[0] = (ds * (a * sg)).astype(db_ref.dtype)

    bspec = pl.BlockSpec((1, tm, n), lambda ch, i: (ch, i, 0))
    return pl.pallas_call(
        body, name=name, grid=(4, T // tm),
        in_specs=[pl.BlockSpec((tm, D), lambda ch, i: (i, 0)), pl.BlockSpec((1, n, D), lambda ch, i: (ch, layer, 0)), bspec, bspec],
        out_specs=[bspec, bspec], out_shape=[jax.ShapeDtypeStruct((4, T, n), BF16)] * 2,
        compiler_params=_params(2))(dy, wd, a, b)


def _ffn_down_dw(name, s, dy):
    _, T, n = s.shape

    def body(s_ref, dy_ref, o_ref):
        o_ref[0] = _dot(s_ref[0], dy_ref[...], "tn").astype(o_ref.dtype)

    return pl.pallas_call(
        body, name=name, grid=(4,),
        in_specs=[pl.BlockSpec((1, T, n), lambda ch: (ch, 0, 0)), pl.BlockSpec((T, D), lambda ch: (0, 0))],
        out_specs=pl.BlockSpec((1, n, D), lambda ch: (ch, 0, 0)), out_shape=jax.ShapeDtypeStruct((4, n, D), BF16),
        compiler_params=_params(1))(s, dy)


def _ffn_up_dw(name, h, da, db, tm=512):
    _, T, n = da.shape

    def body(h_ref, da_ref, db_ref, dg_ref, du_ref):
        h = h_ref[...]
        dg_ref[0] = _dot(h, da_ref[0], "tn").astype(dg_ref.dtype)
        du_ref[0] = _dot(h, db_ref[0], "tn").astype(du_ref.dtype)

    dspec = pl.BlockSpec((1, T, n), lambda ch, j: (ch, 0, 0))
    ospec = pl.BlockSpec((1, tm, n), lambda ch, j: (ch, j, 0))
    return pl.pallas_call(
        body, name=name, grid=(4, D // tm), in_specs=[pl.BlockSpec((T, tm), lambda ch, j: (0, j)), dspec, dspec],
        out_specs=[ospec, ospec], out_shape=[jax.ShapeDtypeStruct((4, D, n), BF16)] * 2,
        compiler_params=_params(2))(h, da, db)


def _ffn_up_dx(name, da, db, wg, wu, layer, tm=1024):
    _, T, n = da.shape
    tm = min(tm, T)

    def body(da_ref, db_ref, wg_ref, wu_ref, o_ref):
        @pl.when(pl.program_id(1) == 0)
        def _():
            o_ref[...] = jnp.zeros_like(o_ref)
        o_ref[...] += _dot(da_ref[0], wg_ref[0], "nt") + _dot(db_ref[0], wu_ref[0], "nt")

    dspec = pl.BlockSpec((1, tm, n), lambda i, ch: (ch, i, 0))
    wspec = pl.BlockSpec((1, D, n), lambda i, ch: (ch, layer, 0))
    return pl.pallas_call(
        body, name=name, grid=(T // tm, 4), in_specs=[dspec, dspec, wspec, wspec],
        out_specs=pl.BlockSpec((tm, D), lambda i, ch: (i, 0)), out_shape=jax.ShapeDtypeStruct((T, D), F32),
        compiler_params=_params(2))(da, db, wg, wu)


def _shift_down(x, k):
    if k == 0:
        return x
    rows = lax.broadcasted_iota(jnp.int32, x.shape, 0)
    return jnp.where(rows >= k, pltpu.roll(x, k, 0), 0.0)


def _shift_up(x, k):
    if k == 0:
        return x
    T = x.shape[0]
    rows = lax.broadcasted_iota(jnp.int32, x.shape, 0)
    return jnp.where(rows < T - k, pltpu.roll(x, T - k, 0), 0.0)


def _conv_silu(x, w):
    c = w[0:1, :] * _shift_down(x, 3) + w[1:2, :] * _shift_down(x, 2) + w[2:3, :] * _shift_down(x, 1) + w[3:4, :] * x
    sg = _sig(c)
    return c, sg, c * sg


def _gdn_conv_fwd(name, proj, cw):
    T = proj.shape[0]

    def body(x_ref, w_ref, o_ref):
        j = pl.program_id(0)
        _, _, y = _conv_silu(x_ref[...], w_ref[...])
        r = lax.rsqrt(jnp.sum(y * y, axis=1, keepdims=True) + EPS)
        mult = jnp.where(j < NH, HD ** -0.5, 1.0)
        o_ref[...] = jnp.where(j < 2 * NH, y * (r * mult), y)

    return pl.pallas_call(body, name=name, grid=(3 * NH,),
                          in_specs=[pl.BlockSpec((T, HD), lambda j: (0, j)), pl.BlockSpec((4, HD), lambda j: (0, j))],
                          out_specs=pl.BlockSpec((T, HD), lambda j: (0, j)),
                          out_shape=jax.ShapeDtypeStruct((T, GDN_QKV), F32), compiler_params=_params(1))(proj, cw)


def _gdn_conv_bwd(name, proj, cw, dz):
    T = proj.shape[0]

    def body(x_ref, w_ref, dz_ref, dx_ref, dw_ref):
        j = pl.program_id(0)
        x, w, dz = x_ref[...], w_ref[...], dz_ref[...]
        c, sg, y = _conv_silu(x, w)
        r = lax.rsqrt(jnp.sum(y * y, axis=1, keepdims=True) + EPS)
        mult = jnp.where(j < NH, HD ** -0.5, 1.0)
        dyn = mult * (r * dz - (r * r * r) * y * jnp.sum(dz * y, axis=1, keepdims=True))
        dy = jnp.where(j < 2 * NH, dyn, dz)
        dc = dy * (sg * (1.0 + c * (1.0 - sg)))
        dx = w[0:1, :] * _shift_up(dc, 3) + w[1:2, :] * _shift_up(dc, 2) + w[2:3, :] * _shift_up(dc, 1) + w[3:4, :] * dc
        dx_ref[...] = dx.astype(dx_ref.dtype)
        for k in range(4):
            dw_ref[pl.ds(k, 1), :] = jnp.sum(dc * _shift_down(x, 3 - k), axis=0, keepdims=True)

    return pl.pallas_call(body, name=name, grid=(3 * NH,),
                          in_specs=[pl.BlockSpec((T, HD), lambda j: (0, j)), pl.BlockSpec((4, HD), lambda j: (0, j)),
                                    pl.BlockSpec((T, HD), lambda j: (0, j))],
                          out_specs=[pl.BlockSpec((T, HD), lambda j: (0, j)), pl.BlockSpec((4, HD), lambda j: (0, j))],
                          out_shape=[jax.ShapeDtypeStruct((T, GDN_QKV), BF16), jax.ShapeDtypeStruct((4, GDN_QKV), F32)],
                          compiler_params=_params(1))(proj, cw, dz)


def _softplus(z):
    return jnp.maximum(z, 0.0) + jnp.log(1.0 + jnp.exp(-jnp.abs(z)))


_AB_CB = GDN_INK // (2 * HD) - 1


def _gdn_gates_fwd(name, proj, alog, dtb):
    def fn(ab, alog, dtb):
        a, b = ab[:, :HD], ab[:, HD:]
        return -jnp.exp(alog) * _softplus(a + dtb), _sig(b)
    return _rowwise(name, fn, [(proj, 2 * HD, _AB_CB)], [alog, dtb], [(HD, F32), (HD, F32)])


def _gdn_gates_bwd(name, proj, dg_h, db_h, alog, dtb):
    def fn(ab, dg_h, db_h, alog, dtb):
        lane = lax.broadcasted_iota(jnp.int32, (1, HD), 1)
        dg = jnp.zeros(dg_h.shape[1:], F32)
        dbeta = jnp.zeros(dg_h.shape[1:], F32)
        for h in range(NH):
            oh = (lane == h).astype(F32)
            dg = dg + dg_h[h] * oh
            dbeta = dbeta + db_h[h] * oh
        a, b = ab[:, :HD], ab[:, HD:]
        z = a + dtb
        ea = jnp.exp(alog)
        beta = _sig(b)
        da = dg * (-ea) * _sig(z)
        db = dbeta * beta * (1.0 - beta)
        return (jnp.concatenate([da, db], axis=1), jnp.sum(dg * (-ea * _softplus(z)), axis=0, keepdims=True),
                jnp.sum(da, axis=0, keepdims=True))
    return _rowwise(name, fn, [(proj, 2 * HD, _AB_CB), dg_h, db_h], [alog, dtb], [(2 * HD, BF16)], sums=[HD, HD])


def _interleave(gens):
    gens = list(gens)
    results = [None] * len(gens)
    active = list(range(len(gens)))
    while active:
        for i in list(active):
            try:
                next(gens[i])
            except StopIteration as stop:
                results[i] = stop.value
                active.remove(i)
    return results


def _chunk_common(q, k, v, gblk, bblk, h, prec):
    C = CHUNK
    lane = lax.broadcasted_iota(jnp.int32, (1, HD), 1)
    oh = (lane == h).astype(F32)
    g_col = jnp.sum(gblk * oh, axis=1, keepdims=True)
    beta = jnp.sum(bblk * oh, axis=1, keepdims=True)
    ri = lax.broadcasted_iota(jnp.int32, (C, C), 0)
    ci = lax.broadcasted_iota(jnp.int32, (C, C), 1)
    incl = ri >= ci
    strict = ri > ci
    eye = (ri == ci).astype(F32)
    gcb = _dot(incl.astype(F32), jnp.broadcast_to(g_col, (C, HD)), "nn", HI)
    yield
    gc = gcb[:, :C]
    gc_row = _dot(jnp.ones((C, C), F32), eye * gc, "nn", HI)
    yield
    decay = jnp.where(incl, jnp.exp(jnp.where(incl, gc - gc_row, 0.0)), 0.0)
    rows = lax.broadcasted_iota(jnp.int32, (C, HD), 0)
    gclb = jnp.sum(jnp.where(rows == C - 1, gcb, 0.0), axis=0, keepdims=True)
    eg = jnp.exp(gcb)
    egl = jnp.exp(gclb - gcb)
    gl = jnp.exp(gclb)
    kb = k * beta
    m1 = _dot(kb, k, "nt", prec)
    qk = _dot(q, k, "nt", prec)
    yield
    L = jnp.where(strict, m1 * decay, 0.0)
    nl = -L
    tinv = eye + nl
    p = nl
    for _ in range(5):
        p = _dot(p, p, "nn", H3)
        yield
        tinv = tinv + _dot(tinv, p, "nn", H3)
    vb = v * beta
    kbg = kb * eg
    yield
    u = _dot(tinv, vb, "nn", prec)
    w = _dot(tinv, kbg, "nn", prec)
    yield
    attn = jnp.where(incl, qk * decay, 0.0)
    return dict(beta=beta, incl=incl, strict=strict, decay=decay, eg=eg, egl=egl, gl=gl, kb=kb, m1=m1, tinv=tinv,
                kbg=kbg, u=u, w=w, qk=qk, attn=attn, q_dec=q * eg, k_dec=k * egl, rows=rows, oh=oh)


def _gdn_chunk_fwd(name, qkv, g, beta):
    T = qkv.shape[0]
    N = T // CHUNK

    hb = _GDN_HB
    w = hb * HD

    def body(q_ref, k_ref, v_ref, g_ref, b_ref, o_ref, st_ref, S):
        hg, n = pl.program_id(0), pl.program_id(1)

        @pl.when(n == 0)
        def _():
            S[...] = jnp.zeros_like(S)

        gblk, bblk = g_ref[...], b_ref[...]

        def one_head(i, q, k, v, s):
            c = yield from _chunk_common(q, k, v, gblk, bblk, hg * hb + i, HF)
            v_new = c["u"] - _dot(c["w"], s, "nn", HF)
            qs = _dot(c["q_dec"], s, "nn", HF)
            yield
            o = qs + _dot(c["attn"], v_new, "nn", HF)
            return o, s * c["gl"] + _dot(c["k_dec"], v_new, "tn", HF)

        sls = [slice(i * HD, (i + 1) * HD) for i in range(hb)]
        states = [S[i] for i in range(hb)]
        res = _interleave(one_head(i, q_ref[:, sls[i]], k_ref[:, sls[i]], v_ref[:, sls[i]], states[i]) for i in range(hb))
        for i, (o, s_new) in enumerate(res):
            st_ref[i, 0] = states[i]
            o_ref[:, sls[i]] = o
            S[i] = s_new

    blk = lambda off: pl.BlockSpec((CHUNK, w), lambda h, n, off=off: (n, off + h))
    gspec = pl.BlockSpec((CHUNK, HD), lambda h, n: (n, 0))
    return pl.pallas_call(
        body, name=name, grid=(NH // hb, N), in_specs=[blk(0), blk(NH // hb), blk(2 * NH // hb), gspec, gspec],
        out_specs=[pl.BlockSpec((CHUNK, w), lambda h, n: (n, h)), pl.BlockSpec((hb, 1, HD, HD), lambda h, n: (h, n, 0, 0))],
        out_shape=[jax.ShapeDtypeStruct((T, NH * HD), F32), jax.ShapeDtypeStruct((NH, N, HD, HD), F32)],
        scratch_shapes=[pltpu.VMEM((hb, HD, HD), F32)], compiler_params=_params(2))(qkv, qkv, qkv, g, beta)


def _gdn_chunk_bwd(name, qkv, g, beta, states, do):
    T = qkv.shape[0]
    N = T // CHUNK
    C = CHUNK

    hb = _GDN_HB
    w = hb * HD

    def body(q_ref, k_ref, v_ref, g_ref, b_ref, st_ref, do_ref, dq_ref, dk_ref, dv_ref, dg_ref, db_ref, dS):
        hg, n = pl.program_id(0), pl.program_id(1)

        @pl.when(n == 0)
        def _():
            dS[...] = jnp.zeros_like(dS)

        gblk, bblk = g_ref[...], b_ref[...]
        sls = [slice(i * HD, (i + 1) * HD) for i in range(hb)]
        res = _interleave(one_head(hg * hb + i, gblk, bblk, q_ref[:, sls[i]], k_ref[:, sls[i]], v_ref[:, sls[i]],
                                   st_ref[i, 0], do_ref[:, sls[i]], dS[i]) for i in range(hb))
        for i, (dq, dk, dv, dg, db, ds_new) in enumerate(res):
            dq_ref[:, sls[i]] = dq
            dk_ref[:, sls[i]] = dk
            dv_ref[:, sls[i]] = dv
            dg_ref[i] = dg
            db_ref[i] = db
            dS[i] = ds_new

    def one_head(h, gblk, bblk, q, k, v, s, do, ds):
        c = yield from _chunk_common(q, k, v, gblk, bblk, h, HF)
        eg, egl, gl, beta, decay, tinv = c["eg"], c["egl"], c["gl"], c["beta"], c["decay"], c["tinv"]
        v_new = c["u"] - _dot(c["w"], s, "nn", HF)
        dq_dec = _dot(do, s, "nt", HF)
        yield
        dv_new = _dot(c["attn"], do, "tn", HF) + _dot(c["k_dec"], ds, "nn", HF)
        dk_dec = _dot(v_new, ds, "nt", HF)
        dgl = jnp.sum(jnp.sum(s * ds, axis=1, keepdims=True), axis=0, keepdims=True)
        yield
        ds_new = ds * gl + _dot(c["q_dec"], do, "tn", HF) - _dot(c["w"], dv_new, "tn", HF)
        dattn = jnp.where(c["incl"], _dot(do, v_new, "nt", HF), 0.0)
        dw = -_dot(dv_new, s, "nt", HF)
        yield
        dvb = _dot(tinv, dv_new, "tn", HS)
        dkbg = _dot(tinv, dw, "tn", HS)
        yield
        dA = -(_dot(dvb, c["u"], "nt", HS) + _dot(dkbg, c["w"], "nt", HS))
        yield
        dL = jnp.where(c["strict"], dA, 0.0)
        dm1 = dL * decay
        dqk = dattn * decay
        xdec = (dL * c["m1"] + dattn * c["qk"]) * decay
        dkb = _dot(dm1, k, "nn", HS) + dkbg * eg
        dk = _dot(dm1, c["kb"], "tn", HS) + _dot(dqk, q, "tn", HS) + dk_dec * egl + dkb * beta
        dq = _dot(dqk, k, "nn", HS) + dq_dec * eg
        yield
        dkd_kd = jnp.sum(dk_dec * c["k_dec"], axis=1, keepdims=True)
        dgc = (jnp.sum(xdec, axis=1, keepdims=True) - _dot(xdec, jnp.ones((C, HD), F32), "tn", HS)
               + jnp.sum(dq_dec * c["q_dec"], axis=1, keepdims=True) - dkd_kd
               + jnp.sum(dkbg * c["kbg"], axis=1, keepdims=True))
        dgcl = jnp.sum(dkd_kd, axis=0, keepdims=True) + dgl * gl
        dgc = dgc + jnp.where(c["rows"] == C - 1, dgcl, 0.0)
        ri = lax.broadcasted_iota(jnp.int32, (C, C), 0)
        ci = lax.broadcasted_iota(jnp.int32, (C, C), 1)
        dg = _dot((ci >= ri).astype(F32), dgc, "nn", HI)
        db = jnp.broadcast_to(jnp.sum(dkb * k, axis=1, keepdims=True) + jnp.sum(dvb * v, axis=1, keepdims=True), (C, HD))
        return dq, dk, dvb * beta, dg, db, ds_new

    blk = lambda off: pl.BlockSpec((C, w), lambda h, n, off=off: (N - 1 - n, off + h))
    gspec = pl.BlockSpec((C, HD), lambda h, n: (N - 1 - n, 0))
    ospec = pl.BlockSpec((C, w), lambda h, n: (N - 1 - n, h))
    hspec = pl.BlockSpec((hb, C, HD), lambda h, n: (h, N - 1 - n, 0))
    return pl.pallas_call(
        body, name=name, grid=(NH // hb, N),
        in_specs=[blk(0), blk(NH // hb), blk(2 * NH // hb), gspec, gspec,
                  pl.BlockSpec((hb, 1, HD, HD), lambda h, n: (h, N - 1 - n, 0, 0)), ospec],
        out_specs=[ospec, ospec, ospec, hspec, hspec],
        out_shape=[jax.ShapeDtypeStruct((T, NH * HD), F32)] * 3 + [jax.ShapeDtypeStruct((NH, T, HD), F32)] * 2,
        scratch_shapes=[pltpu.VMEM((hb, HD, HD), F32)], compiler_params=_params(2))(qkv, qkv, qkv, g, beta, states, do)


_GATE_CB = GDN_QKV // (NH * HD)


def _gdn_gated_norm_fwd(name, o, proj, ng):
    def fn(o, gate, ng):
        outs = []
        for h in range(NH):
            sl = slice(h * HD, (h + 1) * HD)
            oh, gh = o[:, sl], gate[:, sl]
            outs.append(oh * _rms(oh) * ng * (gh * _sig(gh)))
        return jnp.concatenate(outs, axis=1)
    return _rowwise(name, fn, [o, (proj, NH * HD, _GATE_CB)], [ng], [(NH * HD, BF16)])[0]


def _gdn_gated_norm_bwd(name, don, o, proj, ng):
    def fn(don, o, gate, ng):
        dos, dgs = [], []
        dng = jnp.zeros((1, HD), F32)
        for h in range(NH):
            sl = slice(h * HD, (h + 1) * HD)
            oh, gh, dh = o[:, sl], gate[:, sl], don[:, sl]
            r = _rms(oh)
            xh = oh * r
            sg = _sig(gh)
            dn = dh * (gh * sg)
            dgs.append(dh * (xh * ng) * (sg * (1.0 + gh * (1.0 - sg))))
            dng = dng + jnp.sum(dn * xh, axis=0, keepdims=True)
            dxh = dn * ng
            dos.append(r * (dxh - xh * jnp.mean(dxh * xh, axis=-1, keepdims=True)))
        return jnp.concatenate(dos, axis=1), jnp.concatenate(dgs, axis=1), dng
    return _rowwise(name, fn, [don, o, (proj, NH * HD, _GATE_CB)], [ng], [(NH * HD, F32), (NH * HD, BF16)], sums=[HD])


def _rot(x):
    lane = lax.broadcasted_iota(jnp.int32, x.shape, 1)
    return jnp.where(lane < ROPE // 2, -pltpu.roll(x, HD - ROPE // 2, 1), pltpu.roll(x, ROPE // 2, 1))


def _rot_t(x):
    lane = lax.broadcasted_iota(jnp.int32, x.shape, 1)
    return jnp.where(lane < ROPE // 2, pltpu.roll(x, HD - ROPE // 2, 1), -pltpu.roll(x, ROPE // 2, 1))


def _rope_tables(pos_col):
    lane = jnp.arange(HD)
    inv_freq = ROPE_THETA ** (-(2.0 * (lane % (ROPE // 2)).astype(F32)) / ROPE)
    inv_freq = jnp.where(lane < ROPE, inv_freq, 0.0).astype(F32)[None, :]
    valid = (lane < ROPE).astype(F32)[None, :]

    def fn(pos, inv_freq, valid):
        ang = pos.astype(F32) * inv_freq
        return jnp.cos(ang) * valid, jnp.sin(ang) * valid
    return _rowwise("rope_tables", fn, [pos_col], [inv_freq, valid], [(HD, F32), (HD, F32)])


def _mla_pre_fwd(name, proj, cos, sin, qg, kvg):
    def fn(p, cos, sin, qg, kvg):
        cq, ckv, kr = p[:, :Q_RANK], p[:, Q_RANK:Q_RANK + KV_RANK], p[:, Q_RANK + KV_RANK:]
        return cq * _rms(cq) * qg, ckv * _rms(ckv) * kvg, kr * cos + _rot(kr) * sin
    return _rowwise(name, fn, [proj, cos, sin], [qg, kvg], [(Q_RANK, BF16), (KV_RANK, BF16), (HD, BF16)])


def _rms_bwd(dy, x, g):
    r = _rms(x)
    xh = x * r
    dxh = dy * g
    return r * (dxh - xh * jnp.mean(dxh * xh, axis=-1, keepdims=True)), jnp.sum(dy * xh, axis=0, keepdims=True)


def _mla_pre_bwd(name, proj, dcqn, dckvn, dkr, cos, sin, qg, kvg):
    def fn(p, dcqn, dckvn, dkr, cos, sin, qg, kvg):
        cq, ckv = p[:, :Q_RANK], p[:, Q_RANK:Q_RANK + KV_RANK]
        dcq, dqg = _rms_bwd(dcqn, cq, qg)
        dckv, dkvg = _rms_bwd(dckvn, ckv, kvg)
        dkr_pre = dkr * cos + _rot_t(dkr * sin)
        return jnp.concatenate([dcq, dckv, dkr_pre], axis=1), dqg, dkvg
    return _rowwise(name, fn, [proj, dcqn, dckvn, dkr, cos, sin], [qg, kvg], [(MLA_INK, BF16)], sums=[Q_RANK, KV_RANK])


def _mla_q_fwd(name, q, cos, sin):
    def fn(qn, qr, cos, sin):
        outs = []
        for h in range(NH):
            x = qr[:, h * HD:(h + 1) * HD]
            outs.append(x * cos + _rot(x) * sin)
        return qn, jnp.concatenate(outs, axis=1)
    return _rowwise(name, fn, [(q, NH * HD, 0), (q, NH * HD, 1), cos, sin], [], [(NH * HD, BF16), (NH * HD, BF16)])


def _mla_q_bwd(name, dqn, dqr, cos, sin):
    def fn(dqn, dqr, cos, sin):
        outs = [dqn]
        for h in range(NH):
            z = dqr[:, h * HD:(h + 1) * HD]
            outs.append(z * cos + _rot_t(z * sin))
        return jnp.concatenate(outs, axis=1)
    return _rowwise(name, fn, [dqn, dqr, cos, sin], [], [(2 * NH * HD, BF16)])[0]


def _att_probs(qn, qr, kn, kr, row0):
    s = (_dot(qn, kn, "nt") + _dot(qr, kr, "nt")) * ATT_SCALE
    qpos = row0 + lax.broadcasted_iota(jnp.int32, s.shape, 0)
    kpos = lax.broadcasted_iota(jnp.int32, s.shape, 1)
    s = jnp.where(kpos <= qpos, s, -1e30)
    p = jnp.exp(s - jnp.max(s, axis=1, keepdims=True))
    return p / jnp.sum(p, axis=1, keepdims=True)


def _mla_attn_fwd(name, qn, qr, kv, kr, tq=256):
    T = qn.shape[0]
    tq = min(tq, T)

    def body(qn_ref, qr_ref, kn_ref, v_ref, kr_ref, o_ref):
        i = pl.program_id(1)
        for blk in range(T // tq):
            @pl.when(i == blk)
            def _(blk=blk):
                keys = pl.ds(0, (blk + 1) * tq)
                p = _att_probs(qn_ref[...], qr_ref[...], kn_ref[keys, :], kr_ref[keys, :], blk * tq)
                o_ref[...] = _dot(p.astype(BF16), v_ref[keys, :], "nn").astype(o_ref.dtype)

    qspec = pl.BlockSpec((tq, HD), lambda h, i: (i, h))
    return pl.pallas_call(
        body, name=name, grid=(NH, T // tq),
        in_specs=[qspec, qspec, pl.BlockSpec((T, HD), lambda h, i: (0, h)), pl.BlockSpec((T, HD), lambda h, i: (0, NH + h)),
                  pl.BlockSpec((T, HD), lambda h, i: (0, 0))],
        out_specs=qspec, out_shape=jax.ShapeDtypeStruct((T, NH * HD), BF16), compiler_params=_params(2))(qn, qr, kv, kv, kr)


def _mla_attn_bwd(name, qn, qr, kv, kr, do, tq=256):
    T = qn.shape[0]
    tq = min(tq, T)

    def body(qn_ref, qr_ref, kn_ref, v_ref, kr_ref, do_ref, dqn_ref, dqr_ref, dkn_ref, dv_ref, dkr_ref):
        h, i = pl.program_id(0), pl.program_id(1)

        @pl.when(i == 0)
        def _():
            dkn_ref[...] = jnp.zeros_like(dkn_ref)
            dv_ref[...] = jnp.zeros_like(dv_ref)

        @pl.when((i == 0) & (h == 0))
        def _():
            dkr_ref[...] = jnp.zeros_like(dkr_ref)

        for blk in range(T // tq):
            @pl.when(i == blk)
            def _(blk=blk):
                keys = pl.ds(0, (blk + 1) * tq)
                qn, qr, do = qn_ref[...], qr_ref[...], do_ref[...]
                kn, kr, v = kn_ref[keys, :], kr_ref[keys, :], v_ref[keys, :]
                p = _att_probs(qn, qr, kn, kr, blk * tq)
                dp = _dot(do, v, "nt")
                ds = (p * (dp - jnp.sum(p * dp, axis=1, keepdims=True)) * ATT_SCALE).astype(BF16)
                dqn_ref[...] = _dot(ds, kn, "nn")
                dqr_ref[...] = _dot(ds, kr, "nn")
                dkn_ref[keys, :] += _dot(ds, qn, "tn")
                dkr_ref[keys, :] += _dot(ds, qr, "tn")
                dv_ref[keys, :] += _dot(p.astype(BF16), do, "tn")

    qspec = pl.BlockSpec((tq, HD), lambda h, i: (i, h))
    kspec = pl.BlockSpec((T, HD), lambda h, i: (0, h))
    return pl.pallas_call(
        body, name=name, grid=(NH, T // tq),
        in_specs=[qspec, qspec, kspec, pl.BlockSpec((T, HD), lambda h, i: (0, NH + h)),
                  pl.BlockSpec((T, HD), lambda h, i: (0, 0)), qspec],
        out_specs=[qspec, qspec, kspec, kspec, pl.BlockSpec((T, HD), lambda h, i: (0, 0))],
        out_shape=[jax.ShapeDtypeStruct((T, NH * HD), F32)] * 4 + [jax.ShapeDtypeStruct((T, HD), F32)],
        compiler_params=_params(2))(qn, qr, kv, kv, kr, do)


def _mod_rows(mod, layer):
    return [mod[layer:layer + 1, i * D:(i + 1) * D] for i in range(N_MOD)]


def _local_step(x, target, pos_col, mod, weights_of, P, on_grads):
    cos, sin = _rope_tables(pos_col)
    saved = []
    for l in range(DEPTH):
        j = l // 2
        sh_m, sc_m, ga_m, sh_f, sc_f, ga_f = _mod_rows(mod, l)
        s = dict(x0=x)
        h = _norm_mod_fwd(f"norm_mix{l}", x, P["norm_mix_g"][l:l + 1], sc_m, sh_m)
        W = weights_of(l, h)
        s.update(h=h, W=W)
        if l % 2 == 0:
            proj = _mm(f"gdn_in{j}", h, W["gdn_in"], "nn", tn=GDN_INK // 2)
            qkv = _gdn_conv_fwd(f"gdn_conv{j}", proj, P["gdn_cw"][j])
            g, beta = _gdn_gates_fwd(f"gdn_gates{j}", proj, P["gdn_alog"][j], P["gdn_dtb"][j])
            o, states = _gdn_chunk_fwd(f"gdn_chunk{j}", qkv, g, beta)
            on = _gdn_gated_norm_fwd(f"gdn_gnorm{j}", o, proj, P["gdn_ng"][j])
            y = _mm(f"gdn_out{j}", on, W["gdn_out"], "nn")
            s.update(proj=proj, qkv=qkv, g=g, beta=beta, o=o, states=states, on=on)
        else:
            proj = _mm(f"mla_in{j}", h, W["mla_in"], "nn")
            cqn, ckvn, kr = _mla_pre_fwd(f"mla_pre{j}", proj, cos, sin, P["mla_qg"][j], P["mla_kvg"][j])
            q = _mm(f"mla_uq{j}", cqn, W["mla_uq"], "nn")
            kv = _mm(f"mla_ukv{j}", ckvn, W["mla_ukv"], "nn", out_dtype=BF16)
            qn, qr = _mla_q_fwd(f"mla_q{j}", q, cos, sin)
            o = _mla_attn_fwd(f"mla_attn{j}", qn, qr, kv, kr)
            y = _mm(f"mla_out{j}", o, W["mla_out"], "nn")
            s.update(proj=proj, cqn=cqn, ckvn=ckvn, kr=kr, kv=kv, qn=qn, qr=qr, o=o)
        s["y"] = y
        x = _residual_fwd(f"res_mix{l}", x, y, ga_m)
        s["x1"] = x
        h2 = _norm_mod_fwd(f"norm_ffn{l}", x, P["norm_ffn_g"][l:l + 1], sc_f, sh_f)
        fa, fb, sw = _ffn_up(f"ffn_up{l}", h2, W["ffn_g"], W["ffn_u"], 0)
        yf = _ffn_down(f"ffn_down{l}", sw, W["ffn_d"], 0)
        x = _residual_fwd(f"res_ffn{l}", x, yf, ga_f)
        s.update(h2=h2, fa=fa, fb=fb, sw=sw, yf=yf)
        saved.append(s)

    dx, loss, d_final = _loss_head(x, target, P["final_g"])
    gP = dict(loss=loss, final_g=d_final, norm_mix_g=[None] * DEPTH, norm_ffn_g=[None] * DEPTH,
              gdn_cw=[None] * 2, gdn_alog=[None] * 2, gdn_dtb=[None] * 2, gdn_ng=[None] * 2,
              mla_qg=[None] * 2, mla_kvg=[None] * 2)
    dmod = [None] * DEPTH
    for l in reversed(range(DEPTH)):
        j = l // 2
        s = saved[l]
        W = s["W"]
        sh_m, sc_m, ga_m, sh_f, sc_f, ga_f = _mod_rows(mod, l)
        dyf, d_ga_f = _residual_bwd(f"res_ffn_b{l}", dx, s["yf"], ga_f)
        da, db = _ffn_down_bwd(f"ffn_down_dx{l}", dyf, W["ffn_d"], s["fa"], s["fb"], 0)
        g_down = _ffn_down_dw(f"ffn_down_dw{l}", s["sw"], dyf)
        g_gate, g_up = _ffn_up_dw(f"ffn_up_dw{l}", s["h2"], da, db)
        on_grads(l, "ffn", dict(ffn_w_gate=g_gate, ffn_w_up=g_up, ffn_w_down=g_down))
        dh2 = _ffn_up_dx(f"ffn_up_dx{l}", da, db, W["ffn_g"], W["ffn_u"], 0)
        dx, d_sh_f, d_sc_f, gP["norm_ffn_g"][l] = _norm_mod_bwd(f"norm_ffn_b{l}", dh2, s["x1"], dx,
                                                                 P["norm_ffn_g"][l:l + 1], sc_f)
        dy, d_ga_m = _residual_bwd(f"res_mix_b{l}", dx, s["y"], ga_m)
        if l % 2 == 0:
            don = _mm(f"gdn_out_dx{j}", dy, W["gdn_out"], "nt")
            g_out = _mm(f"gdn_out_dw{j}", s["on"], dy, "tn", out_dtype=BF16)
            do, dgate, gP["gdn_ng"][j] = _gdn_gated_norm_bwd(f"gdn_gnorm_b{j}", don, s["o"], s["proj"], P["gdn_ng"][j])
            dq, dk, dv, dg_h, db_h = _gdn_chunk_bwd(f"gdn_chunk_b{j}", s["qkv"], s["g"], s["beta"], s["states"], do)
            dab_, gP["gdn_alog"][j], gP["gdn_dtb"][j] = _gdn_gates_bwd(f"gdn_gates_b{j}", s["proj"], dg_h, db_h,
                                                                        P["gdn_alog"][j], P["gdn_dtb"][j])
            dpre, gP["gdn_cw"][j] = _gdn_conv_bwd(f"gdn_conv_b{j}", s["proj"], P["gdn_cw"][j],
                                                  jnp.concatenate([dq, dk, dv], axis=1))
            dproj = jnp.concatenate([dpre, dgate, dab_], axis=1)
            g_in = _mm(f"gdn_in_dw{j}", s["h"], dproj, "tn", out_dtype=BF16, tn=GDN_INK // 2)
            on_grads(l, "mix", dict(gdn_w_in=_uncols(_gdn_in_from_kernel(g_in)), gdn_w_out=_unrows(g_out)))
            dh = _mm(f"gdn_in_dx{j}", dproj, W["gdn_in"], "nt")
        else:
            do = _mm(f"mla_out_dx{j}", dy, W["mla_out"], "nt", out_dtype=BF16)
            g_out = _mm(f"mla_out_dw{j}", s["o"], dy, "tn", out_dtype=BF16)
            dqn, dqr, dkn, dv, dkr = _mla_attn_bwd(f"mla_attn_b{j}", s["qn"], s["qr"], s["kv"], s["kr"], do)
            dq = _mla_q_bwd(f"mla_q_b{j}", dqn, dqr, cos, sin)
            dkv = jnp.concatenate([dkn, dv], axis=1)
            g_uq = _mm(f"mla_uq_dw{j}", s["cqn"], dq, "tn", out_dtype=BF16)
            dcqn = _mm(f"mla_uq_dx{j}", dq, W["mla_uq"], "nt")
            g_ukv = _mm(f"mla_ukv_dw{j}", s["ckvn"], dkv, "tn", out_dtype=BF16)
            dckvn = _mm(f"mla_ukv_dx{j}", dkv, W["mla_ukv"], "nt")
            dproj, gP["mla_qg"][j], gP["mla_kvg"][j] = _mla_pre_bwd(f"mla_pre_b{j}", s["proj"], dcqn, dckvn, dkr, cos, sin,
                                                                     P["mla_qg"][j], P["mla_kvg"][j])
            g_in = _mm(f"mla_in_dw{j}", s["h"], dproj, "tn", out_dtype=BF16)
            on_grads(l, "mix", dict(mla_w_in=_unrows(g_in[:, :Q_RANK + KV_RANK + ROPE]), mla_w_uq=_uncols(_mla_uq_from_kernel(g_uq)),
                                    mla_w_ukv=_uncols(_mla_ukv_from_kernel(g_ukv)), mla_w_out=_unrows(g_out)))
            dh = _mm(f"mla_in_dx{j}", dproj, W["mla_in"], "nt")
        dx, d_sh_m, d_sc_m, gP["norm_mix_g"][l] = _norm_mod_bwd(f"norm_mix_b{l}", dh, s["x0"], dx,
                                                                 P["norm_mix_g"][l:l + 1], sc_m)
        dmod[l] = jnp.concatenate([d_sh_m, d_sc_m, d_ga_m, d_sh_f, d_sc_f, d_ga_f], axis=1)
    return dx, jnp.concatenate(dmod, axis=0), gP


def _pad_cols(a, width):
    return jnp.pad(a, ((0, 0), (0, width - a.shape[1])))


def _gdn_in_to_kernel(w):
    m = GDN_QKV + NH * HD
    return jnp.concatenate([w[:, :m], _pad_cols(w[:, m:m + NH], HD), _pad_cols(w[:, m + NH:], HD)], axis=1)


def _gdn_in_from_kernel(g):
    m = GDN_QKV + NH * HD
    return jnp.concatenate([g[:, :m], g[:, m:m + NH], g[:, m + HD:m + HD + NH]], axis=1)


def _mla_uq_to_kernel(w):
    w3 = w.reshape(Q_RANK, NH, HD + ROPE)
    rope = jnp.pad(w3[:, :, HD:], ((0, 0), (0, 0), (0, HD - ROPE)))
    return jnp.concatenate([w3[:, :, :HD].reshape(Q_RANK, NH * HD), rope.reshape(Q_RANK, NH * HD)], axis=1)


def _mla_uq_from_kernel(g):
    gn = g[:, :NH * HD].reshape(Q_RANK, NH, HD)
    gr = g[:, NH * HD:].reshape(Q_RANK, NH, HD)[:, :, :ROPE]
    return jnp.concatenate([gn, gr], axis=2).reshape(Q_RANK, NH * (HD + ROPE))


def _mla_ukv_to_kernel(w):
    w3 = w.reshape(KV_RANK, NH, 2 * HD)
    return jnp.concatenate([w3[:, :, :HD].reshape(KV_RANK, NH * HD), w3[:, :, HD:].reshape(KV_RANK, NH * HD)], axis=1)


def _mla_ukv_from_kernel(g):
    gk = g[:, :NH * HD].reshape(KV_RANK, NH, HD)
    gv = g[:, NH * HD:].reshape(KV_RANK, NH, HD)
    return jnp.concatenate([gk, gv], axis=2).reshape(KV_RANK, NH * 2 * HD)


def _cols(t):
    return jnp.moveaxis(t, 0, 1).reshape(t.shape[1], -1)


def _uncols(g):
    return jnp.moveaxis(g.reshape(g.shape[0], 4, -1), 1, 0)


def _rows(t):
    return t.reshape(-1, t.shape[2])


def _unrows(g):
    return g.reshape(4, -1, g.shape[1])


def _layer_weights(layer):
    mixer = ("gdn_w_in", "gdn_w_out") if layer % 2 == 0 else ("mla_w_in", "mla_w_uq", "mla_w_ukv", "mla_w_out")
    return [(n, layer // 2) for n in mixer] + [(n, layer) for n in ("ffn_w_gate", "ffn_w_up", "ffn_w_down")]


def _weights_to_kernel(layer, g):
    out = dict(ffn_g=g["ffn_w_gate"], ffn_u=g["ffn_w_up"], ffn_d=g["ffn_w_down"])
    if layer % 2 == 0:
        out.update(gdn_in=_gdn_in_to_kernel(_cols(g["gdn_w_in"])), gdn_out=_rows(g["gdn_w_out"]))
    else:
        out.update(mla_in=_pad_cols(_rows(g["mla_w_in"]), MLA_INK), mla_uq=_mla_uq_to_kernel(_cols(g["mla_w_uq"])),
                   mla_ukv=_mla_ukv_to_kernel(_cols(g["mla_w_ukv"])), mla_out=_rows(g["mla_w_out"]))
    return out


def _small_to_kernel(norm_mix_g, norm_ffn_g, final_norm_g, gdn_conv_w, gdn_a_log, gdn_dt_bias, gdn_norm_g, q_norm_g, kv_norm_g):
    return dict(
        norm_mix_g=norm_mix_g, norm_ffn_g=norm_ffn_g, final_g=final_norm_g.reshape(1, D),
        gdn_cw=[jnp.transpose(gdn_conv_w[j]) for j in range(2)],
        gdn_alog=[_pad_cols(gdn_a_log[j:j + 1], HD) for j in range(2)],
        gdn_dtb=[_pad_cols(gdn_dt_bias[j:j + 1], HD) for j in range(2)],
        gdn_ng=[gdn_norm_g[j:j + 1] for j in range(2)],
        mla_qg=[q_norm_g[j:j + 1] for j in range(2)],
        mla_kvg=[kv_norm_g[j:j + 1] for j in range(2)],
    )


_CHIP_FLIPS = ((1, 0), (0, 1), (1, 1))
_ANY = pl.BlockSpec(memory_space=pl.ANY)


def _me():
    return lax.axis_index("x"), lax.axis_index("y"), lax.axis_index("c")


def _chip_peer(dx, dy):
    x, y, c = _me()
    return ((1 - x) if dx else x, (1 - y) if dy else y, c)


def _rcopy(src, dst, send_sem, recv_sem, to):
    return pltpu.make_async_remote_copy(src_ref=src, dst_ref=dst, send_sem=send_sem, recv_sem=recv_sem,
                                        device_id=to, device_id_type=MESH)


def _allgather4(name, a, halves=False):
    R, C = a.shape
    rh = R // 2 if halves else R

    def body(a_ref, out_ref, send_sems, recv_sems, local_sem):
        x, y, c = _me()
        me = 2 * x + y
        src = a_ref.at[pl.ds(c * rh, rh)] if halves else a_ref
        local = pltpu.make_async_copy(src, out_ref.at[me], local_sem)
        local.start()
        sends = []
        for k, (dx, dy) in enumerate(_CHIP_FLIPS):
            cp = _rcopy(src, out_ref.at[me], send_sems.at[k], recv_sems.at[k], _chip_peer(dx, dy))
            cp.start()
            sends.append(cp)
        for k, (dx, dy) in enumerate(_CHIP_FLIPS):
            px, py, _ = _chip_peer(dx, dy)
            _rcopy(src, out_ref.at[2 * px + py], send_sems.at[k], recv_sems.at[k], _chip_peer(dx, dy)).wait_recv()
        for cp in sends:
            cp.wait_send()
        local.wait()

    return pl.pallas_call(
        body, name=name, in_specs=[_ANY], out_specs=_ANY, out_shape=jax.ShapeDtypeStruct((4, rh, C), a.dtype),
        scratch_shapes=[pltpu.SemaphoreType.DMA((3,)), pltpu.SemaphoreType.DMA((3,)), pltpu.SemaphoreType.DMA(())])(a)


_NCH = 4


def _dma_sems(*counts):
    return [pltpu.SemaphoreType.DMA((n,)) for n in counts]


def _slot_tile(rows, cap=512):
    best = rows
    for tr in range(16, min(rows, cap) + 1, 16):
        if rows % tr == 0:
            best = tr
    return best


def _cast_into_slot(name, a, chip, row0, rows):
    C = a.shape[1]
    tr = _slot_tile(rows)
    assert row0 % tr == 0
    first = row0 // tr

    def body(c_ref, a_ref, o_ref):
        o_ref[0] = a_ref[...].astype(o_ref.dtype)

    grid_spec = pltpu.PrefetchScalarGridSpec(
        num_scalar_prefetch=1, grid=(rows // tr,), in_specs=[pl.BlockSpec((tr, C), lambda i, c_ref: (first + i, 0))],
        out_specs=pl.BlockSpec((1, tr, C), lambda i, c_ref: (c_ref[0], i, 0)))
    return pl.pallas_call(body, name=name, grid_spec=grid_spec, out_shape=jax.ShapeDtypeStruct((4, rows, C), BF16),
                          compiler_params=_params(1))(chip, a)


def _chunks(rows, align):
    for nch in (_NCH, 2):
        if rows % (nch * align) == 0:
            return nch
    return 1


def _gather_exchange(out, ici_s, ici_r, d2d_s, d2d_r):
    n = len(out)
    x, y, c = _me()
    me = 2 * x + y
    sib = (x, y, 1 - c)
    peers = [_chip_peer(dx, dy) for dx, dy in _CHIP_FLIPS]
    for t in range(n):
        h = out[t].shape[1] // 2
        nch = _chunks(h, 16)
        ch = h // nch
        for k, peer in enumerate(peers):
            for i in range(nch):
                blk = out[t].at[me, pl.ds(c * h + i * ch, ch)]
                _rcopy(blk, blk, ici_s.at[3 * t + k], ici_r.at[3 * t + k], peer).start()
    for t in range(n):
        h = out[t].shape[1] // 2
        nch = _chunks(h, 16)
        ch = h // nch
        for k, peer in enumerate(peers):
            pchip = 2 * peer[0] + peer[1]
            got = out[t].at[pchip, pl.ds(c * h, h)]
            _rcopy(got, got, ici_s.at[3 * t + k], ici_r.at[3 * t + k], peer).wait_recv()
            for i in range(nch):
                blk = out[t].at[pchip, pl.ds(c * h + i * ch, ch)]
                _rcopy(blk, blk, d2d_s.at[3 * t + k], d2d_r.at[3 * t + k], sib).start()
    for t in range(n):
        h = out[t].shape[1] // 2
        for k, peer in enumerate(peers):
            pchip = 2 * peer[0] + peer[1]
            other = out[t].at[pchip, pl.ds((1 - c) * h, h)]
            _rcopy(other, other, d2d_s.at[3 * t + k], d2d_r.at[3 * t + k], sib).wait_recv()
            _rcopy(other, other, ici_s.at[3 * t + k], ici_r.at[3 * t + k], peer).wait_send()
            _rcopy(other, other, d2d_s.at[3 * t + k], d2d_r.at[3 * t + k], sib).wait_send()


def _gather_weights(name, bufs):
    n = len(bufs)

    def body(*refs):
        _gather_exchange(refs[n:2 * n], *refs[2 * n:])

    return pl.pallas_call(
        body, name=name, in_specs=[_ANY] * n, out_specs=[_ANY] * n,
        out_shape=[jax.ShapeDtypeStruct(s.shape, s.dtype) for s in bufs],
        input_output_aliases={t: t for t in range(n)},
        scratch_shapes=_dma_sems(3 * n, 3 * n, 3 * n, 3 * n))(*bufs)


def _gather_weights_async(name, collective_id, bufs):
    n = len(bufs)
    refs = [jax.new_ref(b, memory_space=pltpu.MemorySpace.HBM) for b in bufs]

    @pl.kernel(mesh=plsc.ScalarSubcoreMesh(axis_name="sequencer", num_cores=1), name=name,
               scratch_types=tuple(_dma_sems(3 * n, 3 * n, 3 * n, 3 * n)),
               compiler_params=pltpu.CompilerParams(collective_id=collective_id))
    def launch(ici_s, ici_r, d2d_s, d2d_r):
        x, y, c = _me()
        barrier = pltpu.get_barrier_semaphore()
        for peer in [_chip_peer(dx, dy) for dx, dy in _CHIP_FLIPS] + [(x, y, 1 - c)]:
            pl.semaphore_signal(barrier, inc=1, device_id=peer, device_id_type=MESH)
        pl.semaphore_wait(barrier, 4)
        _gather_exchange(refs, ici_s, ici_r, d2d_s, d2d_r)

    launch()
    return [r[...] for r in refs]


def _rs_split(name, grads):
    n = len(grads)

    def body(*refs):
        g, out = refs[:n], refs[n:2 * n]
        send, recv = refs[2 * n:]
        x, y, c = _me()
        sib = (x, y, 1 - c)
        for t in range(n):
            h = g[t].shape[1] // 2
            for d in range(4):
                _rcopy(g[t].at[d, pl.ds((1 - c) * h, h)], out[t].at[d], send.at[t], recv.at[t], sib).start()
        for t in range(n):
            _rcopy(out[t], out[t], send.at[t], recv.at[t], sib).wait()

    return pl.pallas_call(
        body, name=name, in_specs=[_ANY] * n, out_specs=[_ANY] * n,
        out_shape=[jax.ShapeDtypeStruct((4, s.shape[1] // 2, s.shape[2]), s.dtype) for s in grads],
        scratch_shapes=_dma_sems(n, n))(*grads)


def _pair_add(name, g, theirs, core_chip):
    _, R, C = g.shape
    h = R // 2
    tr = _slot_tile(h)
    nb = h // tr

    def body(s_ref, g_ref, t_ref, p_ref, o_ref):
        val = (g_ref[...].astype(F32) + t_ref[...].astype(F32)).astype(p_ref.dtype)
        p_ref[...] = val

        @pl.when(pl.program_id(1) == s_ref[1])
        def _():
            o_ref[...] = val

    spec = pl.BlockSpec((1, tr, C), lambda i, d, s_ref: (d, i, 0))
    grid_spec = pltpu.PrefetchScalarGridSpec(
        num_scalar_prefetch=1, grid=(nb, 4),
        in_specs=[pl.BlockSpec((1, tr, C), lambda i, d, s_ref: (d, s_ref[0] * nb + i, 0)), spec],
        out_specs=[spec, pl.BlockSpec((1, tr, C), lambda i, d, s_ref: (s_ref[1], i, 0))])
    half = jax.ShapeDtypeStruct((4, h, C), BF16)
    return pl.pallas_call(body, name=name, grid_spec=grid_spec, out_shape=[half, half],
                          compiler_params=_params(2))(core_chip, g, theirs)


def _rs_alltoall_async(name, collective_id, parts, bufs):
    n = len(parts)
    p = [jax.new_ref(a, memory_space=pltpu.MemorySpace.HBM) for a in parts]
    out = [jax.new_ref(b, memory_space=pltpu.MemorySpace.HBM) for b in bufs]

    @pl.kernel(mesh=plsc.ScalarSubcoreMesh(axis_name="sequencer", num_cores=1), name=name,
               scratch_types=tuple(_dma_sems(3 * n, 3 * n)),
               compiler_params=pltpu.CompilerParams(collective_id=collective_id))
    def launch(send, recv):
        barrier = pltpu.get_barrier_semaphore()
        for peer in [_chip_peer(dx, dy) for dx, dy in _CHIP_FLIPS]:
            pl.semaphore_signal(barrier, inc=1, device_id=peer, device_id_type=MESH)
        pl.semaphore_wait(barrier, 3)
        _alltoall_exchange(p, out, send, recv)

    launch()
    return [r[...] for r in out]


def _alltoall_exchange(p, out, send, recv):
    x, y, c = _me()
    me = 2 * x + y
    peers = [_chip_peer(dx, dy) for dx, dy in _CHIP_FLIPS]
    for t in range(len(p)):
        h = p[t].shape[1]
        nch = _chunks(h, 16)
        ch = h // nch
        for k, peer in enumerate(peers):
            pchip = 2 * peer[0] + peer[1]
            for i in range(nch):
                rows = pl.ds(i * ch, ch)
                _rcopy(p[t].at[pchip, rows], out[t].at[me, rows], send.at[3 * t + k], recv.at[3 * t + k], peer).start()
    for t in range(len(p)):
        for k, peer in enumerate(peers):
            pchip = 2 * peer[0] + peer[1]
            _rcopy(out[t].at[pchip], out[t].at[pchip], send.at[3 * t + k], recv.at[3 * t + k], peer).wait()


def _rs_swap(name, halves):
    n = len(halves)

    def body(*refs):
        a, out = refs[:n], refs[n:2 * n]
        send, recv = refs[2 * n:]
        x, y, c = _me()
        sib = (x, y, 1 - c)
        for t in range(n):
            ch = a[t].shape[0] // _NCH
            for i in range(_NCH):
                rows = pl.ds(i * ch, ch)
                _rcopy(a[t].at[rows], out[t].at[rows], send.at[t], recv.at[t], sib).start()
        for t in range(n):
            _rcopy(a[t], out[t], send.at[t], recv.at[t], sib).wait()

    return pl.pallas_call(
        body, name=name, in_specs=[_ANY] * n, out_specs=[_ANY] * n,
        out_shape=[jax.ShapeDtypeStruct(s.shape, s.dtype) for s in halves],
        scratch_shapes=_dma_sems(n, n))(*halves)


def _sibling_merge(name, a):
    P_, rh, C = a.shape

    def body(a_ref, out_ref, send_sem, recv_sem, local_sem):
        x, y, c = _me()
        local = pltpu.make_async_copy(a_ref, out_ref.at[:, pl.ds(c * rh, rh)], local_sem)
        local.start()
        cp = _rcopy(a_ref, out_ref.at[:, pl.ds(c * rh, rh)], send_sem, recv_sem, (x, y, 1 - c))
        cp.start()
        cp.wait_send()
        _rcopy(a_ref, out_ref.at[:, pl.ds((1 - c) * rh, rh)], send_sem, recv_sem, (x, y, 1 - c)).wait_recv()
        local.wait()

    return pl.pallas_call(
        body, name=name, in_specs=[_ANY], out_specs=_ANY, out_shape=jax.ShapeDtypeStruct((P_, 2 * rh, C), a.dtype),
        scratch_shapes=[pltpu.SemaphoreType.DMA(()), pltpu.SemaphoreType.DMA(()), pltpu.SemaphoreType.DMA(())])(a)


def _allgather8(name, a):
    g4 = _allgather4(name + "_chips", a)
    both = _sibling_merge(name + "_cores", g4.reshape(1, 4 * a.shape[0], a.shape[1]))
    return jnp.transpose(both.reshape(2, 4, *a.shape), (1, 0, 2, 3)).reshape(8, *a.shape)


def _sum_slots(name, a, out_dtype):
    def fn(a):
        acc = a[0].astype(F32)
        for k in range(1, a.shape[0]):
            acc = acc + a[k].astype(F32)
        return acc
    return _rowwise(name, fn, [a], [], [(a.shape[2], out_dtype)])[0]


def _adamw_math(w, g, m, v):
    m = ADAM_B1 * m + (1.0 - ADAM_B1) * g
    v = ADAM_B2 * v + (1.0 - ADAM_B2) * (g * g)
    m_hat = m / (1.0 - ADAM_B1 ** ADAM_STEP)
    v_hat = v / (1.0 - ADAM_B2 ** ADAM_STEP)
    return -ADAM_LR * (m_hat / (jnp.sqrt(v_hat) + ADAM_EPS) + ADAM_WD * w), m, v


def _adamw_piece(name, w2, m2, v2, mine, theirs, row0, prev):
    R, C = w2.shape
    h = mine.shape[0]
    tr = _slot_tile(h, 256)
    nb = h // tr
    assert row0 % tr == 0
    first = row0 // tr

    def body(w_ref, m_ref, v_ref, a_ref, b_ref, *rest):
        g_ref, d_ref, nm_ref, nv_ref = rest[-4:]
        g = jnp.where(pl.program_id(0) == lax.axis_index("c"), a_ref[...], b_ref[...])
        g_ref[...] = g
        d_ref[...], nm_ref[...], nv_ref[...] = _adamw_math(w_ref[...], g, m_ref[...], v_ref[...])

    full = pl.BlockSpec((tr, C), lambda s, i: (first + s * nb + i, 0))
    half = pl.BlockSpec((tr, C), lambda s, i: (i, 0))
    extra = [] if prev is None else list(prev)
    return pl.pallas_call(
        body, name=name, grid=(2, nb), in_specs=[full, full, full, half, half] + [_ANY] * len(extra), out_specs=[full] * 4,
        out_shape=[jax.ShapeDtypeStruct((R, C), F32)] * 4, input_output_aliases={5 + k: k for k in range(len(extra))},
        compiler_params=_params(2))(w2, m2, v2, mine, theirs, *extra)


def _adamw(name, w, g, m, v):
    shape = w.shape
    two_d = (-1, shape[-1]) if w.ndim > 1 else (1, -1)
    w2, g2, m2, v2 = [t.reshape(two_d) for t in (w, g, m, v)]
    rows = w2.shape[0]
    tr = rows
    for cand in (256, 128, 64, 32, 16, 8):
        if rows % cand == 0:
            tr = cand
            break

    c = w2.shape[1]
    outs = _rowwise(name, _adamw_math, [w2, g2, m2, v2], [], [(c, F32)] * 3, tr=tr)
    return [o.reshape(shape) for o in outs]


_WEIGHT_ORDER = ("ada_w", "ada_b", "norm_mix_g", "norm_ffn_g", "gdn_w_in", "gdn_conv_w", "gdn_a_log", "gdn_dt_bias",
                 "gdn_norm_g", "gdn_w_out", "mla_w_in", "mla_q_norm_g", "mla_kv_norm_g", "mla_w_uq", "mla_w_ukv",
                 "mla_w_out", "ffn_w_gate", "ffn_w_up", "ffn_w_down", "final_norm_g")
_BIG = (("gdn_w_in", 2), ("gdn_w_out", 1), ("mla_w_in", 1), ("mla_w_uq", 2), ("mla_w_ukv", 2), ("mla_w_out", 1),
        ("ffn_w_gate", 2), ("ffn_w_up", 2), ("ffn_w_down", 1))
_SMALL_SHARDED = (("gdn_conv_w", 1), ("mla_q_norm_g", 1), ("mla_kv_norm_g", 1))


def _size(shape):
    n = 1
    for s in shape:
        n *= s
    return n


def _pack_rows_each(tensors):
    parts, offs, off = [], [], 0
    for t in tensors:
        flat = t.reshape(-1).astype(F32)
        rows = -(-flat.shape[0] // PACK_W)
        parts.append(jnp.pad(flat, (0, rows * PACK_W - flat.shape[0])).reshape(rows, PACK_W))
        offs.append(off)
        off += rows
    total = -(-off // 16) * 16
    pack = jnp.pad(parts[0], ((offs[0], total - offs[0] - parts[0].shape[0]), (0, 0)))
    for p, o in zip(parts[1:], offs[1:]):
        pack = pack + jnp.pad(p, ((o, total - o - p.shape[0]), (0, 0)))
    return pack, offs


def _unpack_rows_each(pack, shapes):
    lead = pack.shape[:-2]
    out, off = [], 0
    for shp in shapes:
        n = _size(shp)
        rows = -(-n // PACK_W)
        out.append(pack[..., off:off + rows, :].reshape(*lead, -1)[..., :n].reshape(*lead, *shp))
        off += rows
    return out


def _merge_chips(stacked, axis):
    moved = jnp.moveaxis(stacked, 0, axis)
    shp = list(moved.shape)
    return moved.reshape(shp[:axis] + [shp[axis] * shp[axis + 1]] + shp[axis + 2:])


def _my_shard(full, axis, chip):
    n = full.shape[axis] // 4
    return lax.dynamic_slice_in_dim(full, chip * n, n, axis)


def kernel(x, c, positions, ada_w, ada_b, norm_mix_g, norm_ffn_g, gdn_w_in, gdn_conv_w, gdn_a_log, gdn_dt_bias, gdn_norm_g, gdn_w_out, mla_w_in, mla_q_norm_g, mla_kv_norm_g, mla_w_uq, mla_w_ukv, mla_w_out, ffn_w_gate, ffn_w_up, ffn_w_down, final_norm_g, loss_target, m_ada_w, m_ada_b, m_norm_mix_g, m_norm_ffn_g, m_gdn_w_in, m_gdn_conv_w, m_gdn_a_log, m_gdn_dt_bias, m_gdn_norm_g, m_gdn_w_out, m_mla_w_in, m_mla_q_norm_g, m_mla_kv_norm_g, m_mla_w_uq, m_mla_w_ukv, m_mla_w_out, m_ffn_w_gate, m_ffn_w_up, m_ffn_w_down, m_final_norm_g, v_ada_w, v_ada_b, v_norm_mix_g, v_norm_ffn_g, v_gdn_w_in, v_gdn_conv_w, v_gdn_a_log, v_gdn_dt_bias, v_gdn_norm_g, v_gdn_w_out, v_mla_w_in, v_mla_q_norm_g, v_mla_kv_norm_g, v_mla_w_uq, v_mla_w_ukv, v_mla_w_out, v_ffn_w_gate, v_ffn_w_up, v_ffn_w_down, v_final_norm_g):
    w = dict(ada_w=ada_w, ada_b=ada_b, norm_mix_g=norm_mix_g, norm_ffn_g=norm_ffn_g, gdn_w_in=gdn_w_in, gdn_conv_w=gdn_conv_w,
             gdn_a_log=gdn_a_log, gdn_dt_bias=gdn_dt_bias, gdn_norm_g=gdn_norm_g, gdn_w_out=gdn_w_out, mla_w_in=mla_w_in,
             mla_q_norm_g=mla_q_norm_g, mla_kv_norm_g=mla_kv_norm_g, mla_w_uq=mla_w_uq, mla_w_ukv=mla_w_ukv,
             mla_w_out=mla_w_out, ffn_w_gate=ffn_w_gate, ffn_w_up=ffn_w_up, ffn_w_down=ffn_w_down, final_norm_g=final_norm_g)
    m = dict(ada_w=m_ada_w, ada_b=m_ada_b, norm_mix_g=m_norm_mix_g, norm_ffn_g=m_norm_ffn_g, gdn_w_in=m_gdn_w_in,
             gdn_conv_w=m_gdn_conv_w, gdn_a_log=m_gdn_a_log, gdn_dt_bias=m_gdn_dt_bias, gdn_norm_g=m_gdn_norm_g,
             gdn_w_out=m_gdn_w_out, mla_w_in=m_mla_w_in, mla_q_norm_g=m_mla_q_norm_g, mla_kv_norm_g=m_mla_kv_norm_g,
             mla_w_uq=m_mla_w_uq, mla_w_ukv=m_mla_w_ukv, mla_w_out=m_mla_w_out, ffn_w_gate=m_ffn_w_gate,
             ffn_w_up=m_ffn_w_up, ffn_w_down=m_ffn_w_down, final_norm_g=m_final_norm_g)
    v = dict(ada_w=v_ada_w, ada_b=v_ada_b, norm_mix_g=v_norm_mix_g, norm_ffn_g=v_norm_ffn_g, gdn_w_in=v_gdn_w_in,
             gdn_conv_w=v_gdn_conv_w, gdn_a_log=v_gdn_a_log, gdn_dt_bias=v_gdn_dt_bias, gdn_norm_g=v_gdn_norm_g,
             gdn_w_out=v_gdn_w_out, mla_w_in=v_mla_w_in, mla_q_norm_g=v_mla_q_norm_g, mla_kv_norm_g=v_mla_kv_norm_g,
             mla_w_uq=v_mla_w_uq, mla_w_ukv=v_mla_w_ukv, mla_w_out=v_mla_w_out, ffn_w_gate=v_ffn_w_gate,
             ffn_w_up=v_ffn_w_up, ffn_w_down=v_ffn_w_down, final_norm_g=v_final_norm_g)
    T = x.shape[1]
    ix, iy, ic = _me()
    chip = 2 * ix + iy
    seq = 2 * chip + ic
    n_dev = 8

    small_shapes = [w[n].shape for n, _ in _SMALL_SHARDED] + [c.shape]
    pack0, _ = _pack_rows_each([w[n] for n, _ in _SMALL_SHARDED] + [c])
    got0 = _unpack_rows_each(_allgather8("gather_small", pack0), small_shapes)
    small_full = {n: _merge_chips(g[0::2], ax) for (n, ax), g in zip(_SMALL_SHARDED, got0)}
    c_all = got0[-1].reshape(n_dev, D)

    big = [n for n, _ in _BIG]
    chip_arr = chip.astype(jnp.int32).reshape(1)
    gathered = {}

    gathered = []
    for l in range(DEPTH):
        names = _layer_weights(l)
        bufs = [_cast_into_slot(f"to_bf16_{n}{l}", w[n].reshape(-1, w[n].shape[-1]), chip_arr, j * w[n].shape[1], w[n].shape[1])
                for n, j in names]
        filled = _gather_weights("gather_weights0", bufs) if l == 0 else _gather_weights_async(f"gather_weights{l}", l, bufs)
        gathered.append({n: b for (n, _), b in zip(names, filled)})

    def weights_of(l, h):
        return _weights_to_kernel(l, gathered[l])

    P = _small_to_kernel(norm_mix_g, norm_ffn_g, final_norm_g, small_full["gdn_conv_w"], gdn_a_log, gdn_dt_bias,
                         gdn_norm_g, small_full["mla_q_norm_g"], small_full["mla_kv_norm_g"])

    c16 = jnp.pad(c_all, ((0, 16 - n_dev), (0, 0)))
    ca = _rowwise("cond_silu", lambda t: t * _sig(t), [c16], [], [(D, BF16)])[0]
    n_ada = ada_w.shape[2]
    mods = jnp.concatenate([_mm(f"ada_fwd{l}", ca, ada_w[l], "nn") for l in range(DEPTH)], axis=0)
    mods_all = _allgather4("gather_mod", mods).reshape(4, DEPTH, 16, n_ada)
    mod_mm = jnp.transpose(lax.dynamic_index_in_dim(mods_all, seq, axis=2, keepdims=False), (1, 0, 2)).reshape(DEPTH, 4 * n_ada)
    mod = _rowwise("mod_bias", lambda a, b: a + b, [mod_mm, ada_b], [], [(4 * n_ada, F32)])[0]

    core_chip = jnp.stack([ic, chip]).astype(jnp.int32)
    pending, in_flight = {}, []

    def reduce_group(layer, part, pieces):
        pending.update({(n, layer if n.startswith("ffn_") else layer // 2): g for n, g in pieces.items()})
        if part == "ffn" and layer > 0:
            return
        keys = list(pending)
        glist = [pending.pop(k) for k in keys]
        tag = f"{layer}{part}"
        theirs = _rs_split("grads_cores_" + tag, glist)
        both = [_pair_add(f"grads_pair_{n}{l}", g, t, core_chip) for (n, l), g, t in zip(keys, glist, theirs)]
        swapped = _rs_alltoall_async("grads_chips_" + tag, DEPTH + 1 + len(in_flight), [p for p, _ in both], [o for _, o in both])
        in_flight.append((tag, keys, swapped))

    dx, dmod, gP = _local_step(x.reshape(T, D), loss_target.reshape(T, D), positions.reshape(T, 1), mod, weights_of, P, reduce_group)

    partials = [dmod, jnp.concatenate(gP["norm_mix_g"]), jnp.concatenate(gP["norm_ffn_g"]), gP["final_g"],
                jnp.stack([jnp.transpose(g) for g in gP["gdn_cw"]]), jnp.concatenate(gP["gdn_alog"])[:, :NH],
                jnp.concatenate(gP["gdn_dtb"])[:, :NH], jnp.concatenate(gP["gdn_ng"]), jnp.concatenate(gP["mla_qg"]),
                jnp.concatenate(gP["mla_kvg"]), gP["loss"][:, :1]]
    part_shapes = [p.shape for p in partials]
    ppack, _ = _pack_rows_each(partials)
    pall = _allgather8("gather_partials", ppack)
    psum = _sum_slots("sum_partials", pall, F32)
    (g_ada_b, g_norm_mix, g_norm_ffn, g_final, g_conv_full, g_alog, g_dtb, g_gdn_ng, g_qg_full, g_kvg_full,
     loss_sum) = _unpack_rows_each(psum, part_shapes)
    dmod_all = _unpack_rows_each(pall, part_shapes[:1])[0]

    grads = dict(ada_b=g_ada_b, norm_mix_g=g_norm_mix, norm_ffn_g=g_norm_ffn, final_norm_g=g_final.reshape(D),
                 gdn_conv_w=_my_shard(g_conv_full, 1, chip), gdn_a_log=g_alog, gdn_dt_bias=g_dtb, gdn_norm_g=g_gdn_ng,
                 mla_q_norm_g=_my_shard(g_qg_full, 1, chip), mla_kv_norm_g=_my_shard(g_kvg_full, 1, chip))

    ca_t = jnp.zeros((D, LANES), BF16).at[:, :16].set(jnp.transpose(ca))
    dm_mine = lax.dynamic_slice_in_dim(dmod_all, chip * n_ada, n_ada, axis=2)
    grads["ada_w"] = jnp.stack([
        _mm(f"ada_bwd{l}", ca_t, jnp.pad(dm_mine[:, l], ((0, LANES - n_dev), (0, 0))), "nn") for l in range(DEPTH)])

    delta, new_m, new_v = {}, {}, {}
    two_d = lambda t: t.reshape(-1, t.shape[-1])
    results = {}
    keys = [k for _, ks, _ in in_flight for k in ks]
    halves = [_sum_slots(f"grads_sum_{n}{l}", s, F32) for _, ks, sw in in_flight for (n, l), s in zip(ks, sw)]
    others = _rs_swap("grads_swap", halves)
    for (n, l), mine, theirs in zip(keys, halves, others):
        results[n] = _adamw_piece(f"adamw_{n}{l}", two_d(w[n]), two_d(m[n]), two_d(v[n]), mine, theirs,
                                  l * w[n].shape[1], results.get(n))
    for n in big:
        grads[n], delta[n], new_m[n], new_v[n] = [t.reshape(w[n].shape) for t in results[n]]
    delta["ada_w"], new_m["ada_w"], new_v["ada_w"] = _adamw("adamw_ada_w", ada_w, grads["ada_w"], m_ada_w, v_ada_w)
    small_names = [n for n in _WEIGHT_ORDER if n not in delta]
    small_shapes = [w[n].shape for n in small_names]
    packs = [_pack_rows_each([d[n] for n in small_names])[0] for d in (w, grads, m, v)]
    for d, pk in zip((delta, new_m, new_v), _adamw("adamw_small", *packs)):
        for n, t in zip(small_names, _unpack_rows_each(pk, small_shapes)):
            d[n] = t

    loss = loss_sum.reshape(())
    return (loss, dx.reshape(1, T, D), *[grads[n] for n in _WEIGHT_ORDER], *[delta[n] for n in _WEIGHT_ORDER],
            *[new_m[n] for n in _WEIGHT_ORDER], *[new_v[n] for n in _WEIGHT_ORDER])
```

```python
import functools

import jax
import jax.numpy as jnp
from jax import lax
from jax.experimental import pallas as pl
from jax.experimental.pallas import tpu as pltpu
from jax.experimental.pallas import tpu_sc as plsc

F32 = jnp.float32
BF16 = jnp.bfloat16
HI = lax.Precision.HIGHEST
MESH = pl.DeviceIdType.MESH

D = 1024
DEPTH = 4
N_MOD = 6
NH = 8
HD = 128
CHUNK = 64
_GDN_HB = 8
GDN_QKV = 3 * NH * HD
GDN_INK = GDN_QKV + NH * HD + 2 * HD
Q_RANK, KV_RANK, ROPE = 384, 256, 64
MLA_INK = Q_RANK + KV_RANK + HD
DFF = 2816
EPS = 1e-6
ATT_SCALE = (HD + ROPE) ** -0.5
ROPE_THETA = 10000.0
LANES = 128
PACK_W = 1024

ADAM_LR, ADAM_B1, ADAM_B2, ADAM_EPS, ADAM_WD, ADAM_STEP = 0.001, 0.9, 0.999, 1e-08, 0.01, 10


H3 = "bf16x3"
B1 = "bf16"
HS = H3
HF = B1


def _dot(a, b, mode="nn", prec=None):
    dn = {"nn": (((1,), (0,)), ((), ())), "nt": (((1,), (1,)), ((), ())), "tn": (((0,), (0,)), ((), ()))}[mode]
    if prec == B1:
        return _dot(a.astype(BF16), b.astype(BF16), mode)
    if prec == H3:
        ah, bh = a.astype(BF16), b.astype(BF16)
        al, bl = (a - ah.astype(F32)).astype(BF16), (b - bh.astype(F32)).astype(BF16)
        return _dot(ah, bh, mode) + (_dot(ah, bl, mode) + _dot(al, bh, mode))
    return lax.dot_general(a, b, dn, precision=prec, preferred_element_type=F32)


def _sig(x):
    return 1.0 / (1.0 + jnp.exp(-x))


def _pick(n, cap):
    if n <= cap:
        return n
    best = None
    for d in range(LANES, cap + 1, LANES):
        if n % d == 0:
            best = d
    assert best is not None, (n, cap)
    return best


def _params(n_grid):
    return pltpu.CompilerParams(dimension_semantics=("arbitrary",) * n_grid, vmem_limit_bytes=56 * 1024 * 1024)


def _rowwise(name, fn, rows, consts, outs, sums=(), tr=256):
    first = rows[0][0] if isinstance(rows[0], tuple) else rows[0]
    T = first.shape[-2]
    tr = _slot_tile(T, tr)
    nr, nc, no, ns = len(rows), len(consts), len(outs), len(sums)

    def body(*refs):
        res = fn(*[r[...] for r in refs[:nr + nc]])
        if not isinstance(res, (tuple, list)):
            res = (res,)
        o_refs = refs[nr + nc:nr + nc + no]
        s_refs = refs[nr + nc + no:]
        for r, val in zip(o_refs, res[:no]):
            r[...] = val.astype(r.dtype)
        if ns:
            @pl.when(pl.program_id(0) == 0)
            def _():
                for r in s_refs:
                    r[...] = jnp.zeros_like(r)
            for r, val in zip(s_refs, res[no:]):
                r[...] += val

    in_specs, args = [], []
    for a in rows:
        if isinstance(a, tuple):
            arr, width, cb = a
            in_specs.append(pl.BlockSpec((tr, width), lambda i, cb=cb: (i, cb)))
            args.append(arr)
        elif a.ndim == 3:
            in_specs.append(pl.BlockSpec((a.shape[0], tr, a.shape[2]), lambda i: (0, i, 0)))
            args.append(a)
        else:
            in_specs.append(pl.BlockSpec((tr, a.shape[1]), lambda i: (i, 0)))
            args.append(a)
    for a in consts:
        in_specs.append(pl.BlockSpec(a.shape, lambda i, nd=a.ndim: (0,) * nd))
        args.append(a)
    out_specs = [pl.BlockSpec((tr, w), lambda i: (i, 0)) for w, _ in outs]
    out_specs += [pl.BlockSpec((1, w), lambda i: (0, 0)) for w in sums]
    out_shape = [jax.ShapeDtypeStruct((T, w), dt) for w, dt in outs]
    out_shape += [jax.ShapeDtypeStruct((1, w), F32) for w in sums]
    res = pl.pallas_call(body, name=name, grid=(T // tr,), in_specs=in_specs, out_specs=out_specs,
                         out_shape=out_shape, compiler_params=_params(1))(*args)
    return res


def _mm(name, a, b, mode, out_dtype=F32, tm=512, tn=1024):
    if mode == "tn":
        K, M = a.shape
    else:
        M, K = a.shape
    N = b.shape[0] if mode == "nt" else b.shape[1]
    tm, tn = _pick(M, tm), _pick(N, tn)

    def body(a_ref, b_ref, o_ref):
        o_ref[...] = _dot(a_ref[...].astype(BF16), b_ref[...].astype(BF16), mode).astype(o_ref.dtype)

    a_spec = pl.BlockSpec((K, tm), lambda i, j: (0, i)) if mode == "tn" else pl.BlockSpec((tm, K), lambda i, j: (i, 0))
    b_spec = pl.BlockSpec((tn, K), lambda i, j: (j, 0)) if mode == "nt" else pl.BlockSpec((K, tn), lambda i, j: (0, j))
    return pl.pallas_call(body, name=name, grid=(M // tm, N // tn), in_specs=[a_spec, b_spec],
                          out_specs=pl.BlockSpec((tm, tn), lambda i, j: (i, j)),
                          out_shape=jax.ShapeDtypeStruct((M, N), out_dtype), compiler_params=_params(2))(a, b)


def _rms(x, eps=EPS):
    return lax.rsqrt(jnp.mean(x * x, axis=-1, keepdims=True) + eps)


def _norm_mod_fwd(name, x, g, scale, shift):
    def fn(x, g, scale, shift):
        return x * _rms(x) * g * (1.0 + scale) + shift
    return _rowwise(name, fn, [x], [g, scale, shift], [(D, BF16)])[0]


def _norm_mod_bwd(name, dh, x, dx_res, g, scale):
    def fn(dh, x, dx_res, g, scale):
        r = _rms(x)
        xh = x * r
        dxh = dh * (g * (1.0 + scale))
        dx = r * (dxh - xh * jnp.mean(dxh * xh, axis=-1, keepdims=True))
        dhx = dh * xh
        return (dx_res + dx, jnp.sum(dh, axis=0, keepdims=True), jnp.sum(dhx * g, axis=0, keepdims=True),
                jnp.sum(dhx * (1.0 + scale), axis=0, keepdims=True))
    return _rowwise(name, fn, [dh, x, dx_res], [g, scale], [(D, F32)], sums=[D, D, D])


def _residual_fwd(name, x, y, gate):
    def fn(x, y, gate):
        return x + gate * y
    return _rowwise(name, fn, [x, y], [gate], [(D, F32)])[0]


def _residual_bwd(name, dx, y, gate):
    def fn(dx, y, gate):
        return dx * gate, jnp.sum(dx * y, axis=0, keepdims=True)
    return _rowwise(name, fn, [dx, y], [gate], [(D, BF16)], sums=[D])


def _loss_head(x, target, g):
    def fn(x, t, g):
        r = _rms(x)
        xh = x * r
        err = xh * g - t
        loss = 0.5 * jnp.sum(jnp.mean(err * err, axis=-1, keepdims=True), axis=0, keepdims=True)
        dy = err * (1.0 / D)
        dxh = dy * g
        dx = r * (dxh - xh * jnp.mean(dxh * xh, axis=-1, keepdims=True))
        return dx, jnp.broadcast_to(loss, (1, LANES)), jnp.sum(dy * xh, axis=0, keepdims=True)
    return _rowwise("loss_head", fn, [x, target], [g], [(D, F32)], sums=[LANES, D])


def _ffn_up(name, h, wg, wu, layer, tm=1024):
    T, n = h.shape[0], wg.shape[1]
    tm = min(tm, T)

    def body(h_ref, wg_ref, wu_ref, a_ref, b_ref, s_ref):
        h = h_ref[...]
        a = _dot(h, wg_ref[0], "nt")
        b = _dot(h, wu_ref[0], "nt")
        a_ref[0] = a
        b_ref[0] = b
        s_ref[0] = (a * _sig(a) * b).astype(s_ref.dtype)

    wspec = pl.BlockSpec((1, n, D), lambda ch, i: (ch, layer, 0))
    ospec = pl.BlockSpec((1, tm, n), lambda ch, i: (ch, i, 0))
    return pl.pallas_call(
        body, name=name, grid=(4, T // tm), in_specs=[pl.BlockSpec((tm, D), lambda ch, i: (i, 0)), wspec, wspec],
        out_specs=[ospec, ospec, ospec],
        out_shape=[jax.ShapeDtypeStruct((4, T, n), F32)] * 2 + [jax.ShapeDtypeStruct((4, T, n), BF16)],
        compiler_params=_params(2))(h, wg, wu)


def _ffn_down(name, s, wd, layer, tm=1024):
    _, T, n = s.shape
    tm = min(tm, T)

    def body(s_ref, w_ref, y_ref):
        @pl.when(pl.program_id(1) == 0)
        def _():
            y_ref[...] = jnp.zeros_like(y_ref)
        y_ref[...] += _dot(s_ref[0], w_ref[0], "nn")

    return pl.pallas_call(
        body, name=name, grid=(T // tm, 4),
        in_specs=[pl.BlockSpec((1, tm, n), lambda i, ch: (ch, i, 0)), pl.BlockSpec((1, n, D), lambda i, ch: (ch, layer, 0))],
        out_specs=pl.BlockSpec((tm, D), lambda i, ch: (i, 0)), out_shape=jax.ShapeDtypeStruct((T, D), F32),
        compiler_params=_params(2))(s, wd)


def _ffn_down_bwd(name, dy, wd, a, b, layer, tm=1024):
    _, T, n = a.shape
    tm = min(tm, T)

    def body(dy_ref, w_ref, a_ref, b_ref, da_ref, db_ref):
        ds = _dot(dy_ref[...], w_ref[0], "nt")
        a, b = a_ref[0], b_ref[0]
        sg = _sig(a)
        da_ref[0] = (ds * b * (sg * (1.0 + a * (1.0 - sg)))).astype(da_ref.dtype)
        db_ref[0] = (ds * (a * sg)).astype(db_ref.dtype)

    bspec = pl.BlockSpec((1, tm, n), lambda ch, i: (ch, i, 0))
    return pl.pallas_call(
        body, name=name, grid=(4, T // tm),
        in_specs=[pl.BlockSpec((tm, D), lambda ch, i: (i, 0)), pl.BlockSpec((1, n, D), lambda ch, i: (ch, layer, 0)), bspec, bspec],
        out_specs=[bspec, bspec], out_shape=[jax.ShapeDtypeStruct((4, T, n), BF16)] * 2,
        compiler_params=_params(2))(dy, wd, a, b)


def _ffn_down_dw(name, s, dy):
    _, T, n = s.shape

    def body(s_ref, dy_ref, o_ref):
        o_ref[0] = _dot(s_ref[0], dy_ref[...], "tn").astype(o_ref.dtype)

    return pl.pallas_call(
        body, name=name, grid=(4,),
        in_specs=[pl.BlockSpec((1, T, n), lambda ch: (ch, 0, 0)), pl.BlockSpec((T, D), lambda ch: (0, 0))],
        out_specs=pl.BlockSpec((1, n, D), lambda ch: (ch, 0, 0)), out_shape=jax.ShapeDtypeStruct((4, n, D), BF16),
        compiler_params=_params(1))(s, dy)


def _ffn_up_dw(name, h, da, db, tm=512):
    _, T, n = da.shape

    def body(h_ref, da_ref, db_ref, dg_ref, du_ref):
        h = h_ref[...]
        dg_ref[0] = _dot(da_ref[0], h, "tn").astype(dg_ref.dtype)
        du_ref[0] = _dot(db_ref[0], h, "tn").astype(du_ref.dtype)

    dspec = pl.BlockSpec((1, T, n), lambda ch, j: (ch, 0, 0))
    ospec = pl.BlockSpec((1, n, tm), lambda ch, j: (ch, 0, j))
    return pl.pallas_call(
        body, name=name, grid=(4, D // tm), in_specs=[pl.BlockSpec((T, tm), lambda ch, j: (0, j)), dspec, dspec],
        out_specs=[ospec, ospec], out_shape=[jax.ShapeDtypeStruct((4, n, D), BF16)] * 2,
        compiler_params=_params(2))(h, da, db)


def _ffn_up_dx(name, da, db, wg, wu, layer, tm=1024):
    _, T, n = da.shape
    tm = min(tm, T)

    def body(da_ref, db_ref, wg_ref, wu_ref, o_ref):
        @pl.when(pl.program_id(1) == 0)
        def _():
            o_ref[...] = jnp.zeros_like(o_ref)
        o_ref[...] += _dot(da_ref[0], wg_ref[0], "nn") + _dot(db_ref[0], wu_ref[0], "nn")

    dspec = pl.BlockSpec((1, tm, n), lambda i, ch: (ch, i, 0))
    wspec = pl.BlockSpec((1, n, D), lambda i, ch: (ch, layer, 0))
    return pl.pallas_call(
        body, name=name, grid=(T // tm, 4), in_specs=[dspec, dspec, wspec, wspec],
        out_specs=pl.BlockSpec((tm, D), lambda i, ch: (i, 0)), out_shape=jax.ShapeDtypeStruct((T, D), F32),
        compiler_params=_params(2))(da, db, wg, wu)


def _shift_down(x, k):
    if k == 0:
        return x
    rows = lax.broadcasted_iota(jnp.int32, x.shape, 0)
    return jnp.where(rows >= k, pltpu.roll(x, k, 0), 0.0)


def _shift_up(x, k):
    if k == 0:
        return x
    T = x.shape[0]
    rows = lax.broadcasted_iota(jnp.int32, x.shape, 0)
    return jnp.where(rows < T - k, pltpu.roll(x, T - k, 0), 0.0)


def _conv_silu(x, w):
    c = w[0:1, :] * _shift_down(x, 3) + w[1:2, :] * _shift_down(x, 2) + w[2:3, :] * _shift_down(x, 1) + w[3:4, :] * x
    sg = _sig(c)
    return c, sg, c * sg


def _gdn_conv_fwd(name, proj, cw):
    T = proj.shape[0]

    def body(x_ref, w_ref, o_ref):
        j = pl.program_id(0)
        _, _, y = _conv_silu(x_ref[...], w_ref[...])
        r = lax.rsqrt(jnp.sum(y * y, axis=1, keepdims=True) + EPS)
        mult = jnp.where(j < NH, HD ** -0.5, 1.0)
        o_ref[...] = jnp.where(j < 2 * NH, y * (r * mult), y)

    return pl.pallas_call(body, name=name, grid=(3 * NH,),
                          in_specs=[pl.BlockSpec((T, HD), lambda j: (0, j)), pl.BlockSpec((4, HD), lambda j: (0, j))],
                          out_specs=pl.BlockSpec((T, HD), lambda j: (0, j)),
                          out_shape=jax.ShapeDtypeStruct((T, GDN_QKV), F32), compiler_params=_params(1))(proj, cw)


def _gdn_conv_bwd(name, proj, cw, dz):
    T = proj.shape[0]

    def body(x_ref, w_ref, dz_ref, dx_ref, dw_ref):
        j = pl.program_id(0)
        x, w, dz = x_ref[...], w_ref[...], dz_ref[...]
        c, sg, y = _conv_silu(x, w)
        r = lax.rsqrt(jnp.sum(y * y, axis=1, keepdims=True) + EPS)
        mult = jnp.where(j < NH, HD ** -0.5, 1.0)
        dyn = mult * (r * dz - (r * r * r) * y * jnp.sum(dz * y, axis=1, keepdims=True))
        dy = jnp.where(j < 2 * NH, dyn, dz)
        dc = dy * (sg * (1.0 + c * (1.0 - sg)))
        dx = w[0:1, :] * _shift_up(dc, 3) + w[1:2, :] * _shift_up(dc, 2) + w[2:3, :] * _shift_up(dc, 1) + w[3:4, :] * dc
        dx_ref[...] = dx.astype(dx_ref.dtype)
        for k in range(4):
            dw_ref[pl.ds(k, 1), :] = jnp.sum(dc * _shift_down(x, 3 - k), axis=0, keepdims=True)

    return pl.pallas_call(body, name=name, grid=(3 * NH,),
                          in_specs=[pl.BlockSpec((T, HD), lambda j: (0, j)), pl.BlockSpec((4, HD), lambda j: (0, j)),
                                    pl.BlockSpec((T, HD), lambda j: (0, j))],
                          out_specs=[pl.BlockSpec((T, HD), lambda j: (0, j)), pl.BlockSpec((4, HD), lambda j: (0, j))],
                          out_shape=[jax.ShapeDtypeStruct((T, GDN_QKV), BF16), jax.ShapeDtypeStruct((4, GDN_QKV), F32)],
                          compiler_params=_params(1))(proj, cw, dz)


def _softplus(z):
    return jnp.maximum(z, 0.0) + jnp.log(1.0 + jnp.exp(-jnp.abs(z)))


_AB_CB = GDN_INK // (2 * HD) - 1


def _gdn_gates_fwd(name, proj, alog, dtb):
    def fn(ab, alog, dtb):
        a, b = ab[:, :HD], ab[:, HD:]
        return -jnp.exp(alog) * _softplus(a + dtb), _sig(b)
    return _rowwise(name, fn, [(proj, 2 * HD, _AB_CB)], [alog, dtb], [(HD, F32), (HD, F32)])


def _gdn_gates_bwd(name, proj, dg_h, db_h, alog, dtb):
    def fn(ab, dg_h, db_h, alog, dtb):
        lane = lax.broadcasted_iota(jnp.int32, (1, HD), 1)
        dg = jnp.zeros(dg_h.shape[1:], F32)
        dbeta = jnp.zeros(dg_h.shape[1:], F32)
        for h in range(NH):
            oh = (lane == h).astype(F32)
            dg = dg + dg_h[h] * oh
            dbeta = dbeta + db_h[h] * oh
        a, b = ab[:, :HD], ab[:, HD:]
        z = a + dtb
        ea = jnp.exp(alog)
        beta = _sig(b)
        da = dg * (-ea) * _sig(z)
        db = dbeta * beta * (1.0 - beta)
        return (jnp.concatenate([da, db], axis=1), jnp.sum(dg * (-ea * _softplus(z)), axis=0, keepdims=True),
                jnp.sum(da, axis=0, keepdims=True))
    return _rowwise(name, fn, [(proj, 2 * HD, _AB_CB), dg_h, db_h], [alog, dtb], [(2 * HD, BF16)], sums=[HD, HD])


def _interleave(gens):
    gens = list(gens)
    results = [None] * len(gens)
    active = list(range(len(gens)))
    while active:
        for i in list(active):
            try:
                next(gens[i])
            except StopIteration as stop:
                results[i] = stop.value
                active.remove(i)
    return results


def _chunk_common(q, k, v, gblk, bblk, h, prec):
    C = CHUNK
    lane = lax.broadcasted_iota(jnp.int32, (1, HD), 1)
    oh = (lane == h).astype(F32)
    g_col = jnp.sum(gblk * oh, axis=1, keepdims=True)
    beta = jnp.sum(bblk * oh, axis=1, keepdims=True)
    ri = lax.broadcasted_iota(jnp.int32, (C, C), 0)
    ci = lax.broadcasted_iota(jnp.int32, (C, C), 1)
    incl = ri >= ci
    strict = ri > ci
    eye = (ri == ci).astype(F32)
    gcb = _dot(incl.astype(F32), jnp.broadcast_to(g_col, (C, HD)), "nn", HI)
    yield
    gc = gcb[:, :C]
    gc_row = _dot(jnp.ones((C, C), F32), eye * gc, "nn", HI)
    yield
    decay = jnp.where(incl, jnp.exp(jnp.where(incl, gc - gc_row, 0.0)), 0.0)
    rows = lax.broadcasted_iota(jnp.int32, (C, HD), 0)
    gclb = jnp.sum(jnp.where(rows == C - 1, gcb, 0.0), axis=0, keepdims=True)
    eg = jnp.exp(gcb)
    egl = jnp.exp(gclb - gcb)
    gl = jnp.exp(gclb)
    kb = k * beta
    m1 = _dot(kb, k, "nt", prec)
    qk = _dot(q, k, "nt", prec)
    yield
    L = jnp.where(strict, m1 * decay, 0.0)
    nl = -L
    tinv = eye + nl
    p = nl
    for _ in range(5):
        p = _dot(p, p, "nn", H3)
        yield
        tinv = tinv + _dot(tinv, p, "nn", H3)
    vb = v * beta
    kbg = kb * eg
    yield
    u = _dot(tinv, vb, "nn", prec)
    w = _dot(tinv, kbg, "nn", prec)
    yield
    attn = jnp.where(incl, qk * decay, 0.0)
    return dict(beta=beta, incl=incl, strict=strict, decay=decay, eg=eg, egl=egl, gl=gl, kb=kb, m1=m1, tinv=tinv,
                kbg=kbg, u=u, w=w, qk=qk, attn=attn, q_dec=q * eg, k_dec=k * egl, rows=rows, oh=oh)


def _gdn_chunk_fwd(name, qkv, g, beta):
    T = qkv.shape[0]
    N = T // CHUNK

    hb = _GDN_HB
    w = hb * HD

    def body(q_ref, k_ref, v_ref, g_ref, b_ref, o_ref, st_ref, S):
        hg, n = pl.program_id(0), pl.program_id(1)

        @pl.when(n == 0)
        def _():
            S[...] = jnp.zeros_like(S)

        gblk, bblk = g_ref[...], b_ref[...]

        def one_head(i, q, k, v, s):
            c = yield from _chunk_common(q, k, v, gblk, bblk, hg * hb + i, HF)
            v_new = c["u"] - _dot(c["w"], s, "nn", HF)
            qs = _dot(c["q_dec"], s, "nn", HF)
            yield
            o = qs + _dot(c["attn"], v_new, "nn", HF)
            return o, s * c["gl"] + _dot(c["k_dec"], v_new, "tn", HF)

        sls = [slice(i * HD, (i + 1) * HD) for i in range(hb)]
        states = [S[i] for i in range(hb)]
        res = _interleave(one_head(i, q_ref[:, sls[i]], k_ref[:, sls[i]], v_ref[:, sls[i]], states[i]) for i in range(hb))
        for i, (o, s_new) in enumerate(res):
            st_ref[i, 0] = states[i]
            o_ref[:, sls[i]] = o
            S[i] = s_new

    blk = lambda off: pl.BlockSpec((CHUNK, w), lambda h, n, off=off: (n, off + h))
    gspec = pl.BlockSpec((CHUNK, HD), lambda h, n: (n, 0))
    return pl.pallas_call(
        body, name=name, grid=(NH // hb, N), in_specs=[blk(0), blk(NH // hb), blk(2 * NH // hb), gspec, gspec],
        out_specs=[pl.BlockSpec((CHUNK, w), lambda h, n: (n, h)), pl.BlockSpec((hb, 1, HD, HD), lambda h, n: (h, n, 0, 0))],
        out_shape=[jax.ShapeDtypeStruct((T, NH * HD), F32), jax.ShapeDtypeStruct((NH, N, HD, HD), F32)],
        scratch_shapes=[pltpu.VMEM((hb, HD, HD), F32)], compiler_params=_params(2))(qkv, qkv, qkv, g, beta)


def _gdn_chunk_bwd(name, qkv, g, beta, states, do):
    T = qkv.shape[0]
    N = T // CHUNK
    C = CHUNK

    hb = _GDN_HB
    w = hb * HD

    def body(q_ref, k_ref, v_ref, g_ref, b_ref, st_ref, do_ref, dq_ref, dk_ref, dv_ref, dg_ref, db_ref, dS):
        hg, n = pl.program_id(0), pl.program_id(1)

        @pl.when(n == 0)
        def _():
            dS[...] = jnp.zeros_like(dS)

        gblk, bblk = g_ref[...], b_ref[...]
        sls = [slice(i * HD, (i + 1) * HD) for i in range(hb)]
        res = _interleave(one_head(hg * hb + i, gblk, bblk, q_ref[:, sls[i]], k_ref[:, sls[i]], v_ref[:, sls[i]],
                                   st_ref[i, 0], do_ref[:, sls[i]], dS[i]) for i in range(hb))
        for i, (dq, dk, dv, dg, db, ds_new) in enumerate(res):
            dq_ref[:, sls[i]] = dq
            dk_ref[:, sls[i]] = dk
            dv_ref[:, sls[i]] = dv
            dg_ref[i] = dg
            db_ref[i] = db
            dS[i] = ds_new

    def one_head(h, gblk, bblk, q, k, v, s, do, ds):
        c = yield from _chunk_common(q, k, v, gblk, bblk, h, HF)
        eg, egl, gl, beta, decay, tinv = c["eg"], c["egl"], c["gl"], c["beta"], c["decay"], c["tinv"]
        v_new = c["u"] - _dot(c["w"], s, "nn", HF)
        dq_dec = _dot(do, s, "nt", HF)
        yield
        dv_new = _dot(c["attn"], do, "tn", HF) + _dot(c["k_dec"], ds, "nn", HF)
        dk_dec = _dot(v_new, ds, "nt", HF)
        dgl = jnp.sum(jnp.sum(s * ds, axis=1, keepdims=True), axis=0, keepdims=True)
        yield
        ds_new = ds * gl + _dot(c["q_dec"], do, "tn", HF) - _dot(c["w"], dv_new, "tn", HF)
        dattn = jnp.where(c["incl"], _dot(do, v_new, "nt", HF), 0.0)
        dw = -_dot(dv_new, s, "nt", HF)
        yield
        dvb = _dot(tinv, dv_new, "tn", HS)
        dkbg = _dot(tinv, dw, "tn", HS)
        yield
        dA = -(_dot(dvb, c["u"], "nt", HS) + _dot(dkbg, c["w"], "nt", HS))
        yield
        dL = jnp.where(c["strict"], dA, 0.0)
        dm1 = dL * decay
        dqk = dattn * decay
        xdec = (dL * c["m1"] + dattn * c["qk"]) * decay
        dkb = _dot(dm1, k, "nn", HS) + dkbg * eg
        dk = _dot(dm1, c["kb"], "tn", HS) + _dot(dqk, q, "tn", HS) + dk_dec * egl + dkb * beta
        dq = _dot(dqk, k, "nn", HS) + dq_dec * eg
        yield
        dkd_kd = jnp.sum(dk_dec * c["k_dec"], axis=1, keepdims=True)
        dgc = (jnp.sum(xdec, axis=1, keepdims=True) - _dot(xdec, jnp.ones((C, HD), F32), "tn", HS)
               + jnp.sum(dq_dec * c["q_dec"], axis=1, keepdims=True) - dkd_kd
               + jnp.sum(dkbg * c["kbg"], axis=1, keepdims=True))
        dgcl = jnp.sum(dkd_kd, axis=0, keepdims=True) + dgl * gl
        dgc = dgc + jnp.where(c["rows"] == C - 1, dgcl, 0.0)
        ri = lax.broadcasted_iota(jnp.int32, (C, C), 0)
        ci = lax.broadcasted_iota(jnp.int32, (C, C), 1)
        dg = _dot((ci >= ri).astype(F32), dgc, "nn", HI)
        db = jnp.broadcast_to(jnp.sum(dkb * k, axis=1, keepdims=True) + jnp.sum(dvb * v, axis=1, keepdims=True), (C, HD))
        return dq, dk, dvb * beta, dg, db, ds_new

    blk = lambda off: pl.BlockSpec((C, w), lambda h, n, off=off: (N - 1 - n, off + h))
    gspec = pl.BlockSpec((C, HD), lambda h, n: (N - 1 - n, 0))
    ospec = pl.BlockSpec((C, w), lambda h, n: (N - 1 - n, h))
    hspec = pl.BlockSpec((hb, C, HD), lambda h, n: (h, N - 1 - n, 0))
    return pl.pallas_call(
        body, name=name, grid=(NH // hb, N),
        in_specs=[blk(0), blk(NH // hb), blk(2 * NH // hb), gspec, gspec,
                  pl.BlockSpec((hb, 1, HD, HD), lambda h, n: (h, N - 1 - n, 0, 0)), ospec],
        out_specs=[ospec, ospec, ospec, hspec, hspec],
        out_shape=[jax.ShapeDtypeStruct((T, NH * HD), F32)] * 3 + [jax.ShapeDtypeStruct((NH, T, HD), F32)] * 2,
        scratch_shapes=[pltpu.VMEM((hb, HD, HD), F32)], compiler_params=_params(2))(qkv, qkv, qkv, g, beta, states, do)


_GATE_CB = GDN_QKV // (NH * HD)


def _gdn_gated_norm_fwd(name, o, proj, ng):
    def fn(o, gate, ng):
        outs = []
        for h in range(NH):
            sl = slice(h * HD, (h + 1) * HD)
            oh, gh = o[:, sl], gate[:, sl]
            outs.append(oh * _rms(oh) * ng * (gh * _sig(gh)))
        return jnp.concatenate(outs, axis=1)
    return _rowwise(name, fn, [o, (proj, NH * HD, _GATE_CB)], [ng], [(NH * HD, BF16)])[0]


def _gdn_gated_norm_bwd(name, don, o, proj, ng):
    def fn(don, o, gate, ng):
        dos, dgs = [], []
        dng = jnp.zeros((1, HD), F32)
        for h in range(NH):
            sl = slice(h * HD, (h + 1) * HD)
            oh, gh, dh = o[:, sl], gate[:, sl], don[:, sl]
            r = _rms(oh)
            xh = oh * r
            sg = _sig(gh)
            dn = dh * (gh * sg)
            dgs.append(dh * (xh * ng) * (sg * (1.0 + gh * (1.0 - sg))))
            dng = dng + jnp.sum(dn * xh, axis=0, keepdims=True)
            dxh = dn * ng
            dos.append(r * (dxh - xh * jnp.mean(dxh * xh, axis=-1, keepdims=True)))
        return jnp.concatenate(dos, axis=1), jnp.concatenate(dgs, axis=1), dng
    return _rowwise(name, fn, [don, o, (proj, NH * HD, _GATE_CB)], [ng], [(NH * HD, F32), (NH * HD, BF16)], sums=[HD])


def _rot(x):
    lane = lax.broadcasted_iota(jnp.int32, x.shape, 1)
    return jnp.where(lane < ROPE // 2, -pltpu.roll(x, HD - ROPE // 2, 1), pltpu.roll(x, ROPE // 2, 1))


def _rot_t(x):
    lane = lax.broadcasted_iota(jnp.int32, x.shape, 1)
    return jnp.where(lane < ROPE // 2, pltpu.roll(x, HD - ROPE // 2, 1), -pltpu.roll(x, ROPE // 2, 1))


def _rope_tables(pos_col):
    lane = jnp.arange(HD)
    inv_freq = ROPE_THETA ** (-(2.0 * (lane % (ROPE // 2)).astype(F32)) / ROPE)
    inv_freq = jnp.where(lane < ROPE, inv_freq, 0.0).astype(F32)[None, :]
    valid = (lane < ROPE).astype(F32)[None, :]

    def fn(pos, inv_freq, valid):
        ang = pos.astype(F32) * inv_freq
        return jnp.cos(ang) * valid, jnp.sin(ang) * valid
    return _rowwise("rope_tables", fn, [pos_col], [inv_freq, valid], [(HD, F32), (HD, F32)])


def _mla_pre_fwd(name, proj, cos, sin, qg, kvg):
    def fn(p, cos, sin, qg, kvg):
        cq, ckv, kr = p[:, :Q_RANK], p[:, Q_RANK:Q_RANK + KV_RANK], p[:, Q_RANK + KV_RANK:]
        return cq * _rms(cq) * qg, ckv * _rms(ckv) * kvg, kr * cos + _rot(kr) * sin
    return _rowwise(name, fn, [proj, cos, sin], [qg, kvg], [(Q_RANK, BF16), (KV_RANK, BF16), (HD, BF16)])


def _rms_bwd(dy, x, g):
    r = _rms(x)
    xh = x * r
    dxh = dy * g
    return r * (dxh - xh * jnp.mean(dxh * xh, axis=-1, keepdims=True)), jnp.sum(dy * xh, axis=0, keepdims=True)


def _mla_pre_bwd(name, proj, dcqn, dckvn, dkr, cos, sin, qg, kvg):
    def fn(p, dcqn, dckvn, dkr, cos, sin, qg, kvg):
        cq, ckv = p[:, :Q_RANK], p[:, Q_RANK:Q_RANK + KV_RANK]
        dcq, dqg = _rms_bwd(dcqn, cq, qg)
        dckv, dkvg = _rms_bwd(dckvn, ckv, kvg)
        dkr_pre = dkr * cos + _rot_t(dkr * sin)
        return jnp.concatenate([dcq, dckv, dkr_pre], axis=1), dqg, dkvg
    return _rowwise(name, fn, [proj, dcqn, dckvn, dkr, cos, sin], [qg, kvg], [(MLA_INK, BF16)], sums=[Q_RANK, KV_RANK])


def _mla_q_fwd(name, q, cos, sin):
    def fn(qn, qr, cos, sin):
        outs = []
        for h in range(NH):
            x = qr[:, h * HD:(h + 1) * HD]
            outs.append(x * cos + _rot(x) * sin)
        return qn, jnp.concatenate(outs, axis=1)
    return _rowwise(name, fn, [(q, NH * HD, 0), (q, NH * HD, 1), cos, sin], [], [(NH * HD, BF16), (NH * HD, BF16)])


def _mla_q_bwd(name, dqn, dqr, cos, sin):
    def fn(dqn, dqr, cos, sin):
        outs = [dqn]
        for h in range(NH):
            z = dqr[:, h * HD:(h + 1) * HD]
            outs.append(z * cos + _rot_t(z * sin))
        return jnp.concatenate(outs, axis=1)
    return _rowwise(name, fn, [dqn, dqr, cos, sin], [], [(2 * NH * HD, BF16)])[0]


def _att_probs(qn, qr, kn, kr, row0):
    s = (_dot(qn, kn, "nt") + _dot(qr, kr, "nt")) * ATT_SCALE
    qpos = row0 + lax.broadcasted_iota(jnp.int32, s.shape, 0)
    kpos = lax.broadcasted_iota(jnp.int32, s.shape, 1)
    s = jnp.where(kpos <= qpos, s, -1e30)
    p = jnp.exp(s - jnp.max(s, axis=1, keepdims=True))
    return p / jnp.sum(p, axis=1, keepdims=True)


def _mla_attn_fwd(name, qn, qr, kv, kr, tq=256):
    T = qn.shape[0]
    tq = min(tq, T)

    def body(qn_ref, qr_ref, kn_ref, v_ref, kr_ref, o_ref):
        i = pl.program_id(1)
        for blk in range(T // tq):
            @pl.when(i == blk)
            def _(blk=blk):
                keys = pl.ds(0, (blk + 1) * tq)
                p = _att_probs(qn_ref[...], qr_ref[...], kn_ref[keys, :], kr_ref[keys, :], blk * tq)
                o_ref[...] = _dot(p.astype(BF16), v_ref[keys, :], "nn").astype(o_ref.dtype)

    qspec = pl.BlockSpec((tq, HD), lambda h, i: (i, h))
    return pl.pallas_call(
        body, name=name, grid=(NH, T // tq),
        in_specs=[qspec, qspec, pl.BlockSpec((T, HD), lambda h, i: (0, h)), pl.BlockSpec((T, HD), lambda h, i: (0, NH + h)),
                  pl.BlockSpec((T, HD), lambda h, i: (0, 0))],
        out_specs=qspec, out_shape=jax.ShapeDtypeStruct((T, NH * HD), BF16), compiler_params=_params(2))(qn, qr, kv, kv, kr)


def _mla_attn_bwd(name, qn, qr, kv, kr, do, tq=256):
    T = qn.shape[0]
    tq = min(tq, T)

    def body(qn_ref, qr_ref, kn_ref, v_ref, kr_ref, do_ref, dqn_ref, dqr_ref, dkn_ref, dv_ref, dkr_ref):
        h, i = pl.program_id(0), pl.program_id(1)

        @pl.when(i == 0)
        def _():
            dkn_ref[...] = jnp.zeros_like(dkn_ref)
            dv_ref[...] = jnp.zeros_like(dv_ref)

        @pl.when((i == 0) & (h == 0))
        def _():
            dkr_ref[...] = jnp.zeros_like(dkr_ref)

        for blk in range(T // tq):
            @pl.when(i == blk)
            def _(blk=blk):
                keys = pl.ds(0, (blk + 1) * tq)
                qn, qr, do = qn_ref[...], qr_ref[...], do_ref[...]
                kn, kr, v = kn_ref[keys, :], kr_ref[keys, :], v_ref[keys, :]
                p = _att_probs(qn, qr, kn, kr, blk * tq)
                dp = _dot(do, v, "nt")
                ds = (p * (dp - jnp.sum(p * dp, axis=1, keepdims=True)) * ATT_SCALE).astype(BF16)
                dqn_ref[...] = _dot(ds, kn, "nn")
                dqr_ref[...] = _dot(ds, kr, "nn")
                dkn_ref[keys, :] += _dot(ds, qn, "tn")
                dkr_ref[keys, :] += _dot(ds, qr, "tn")
                dv_ref[keys, :] += _dot(p.astype(BF16), do, "tn")

    qspec = pl.BlockSpec((tq, HD), lambda h, i: (i, h))
    kspec = pl.BlockSpec((T, HD), lambda h, i: (0, h))
    return pl.pallas_call(
        body, name=name, grid=(NH, T // tq),
        in_specs=[qspec, qspec, kspec, pl.BlockSpec((T, HD), lambda h, i: (0, NH + h)),
                  pl.BlockSpec((T, HD), lambda h, i: (0, 0)), qspec],
        out_specs=[qspec, qspec, kspec, kspec, pl.BlockSpec((T, HD), lambda h, i: (0, 0))],
        out_shape=[jax.ShapeDtypeStruct((T, NH * HD), F32)] * 4 + [jax.ShapeDtypeStruct((T, HD), F32)],
        compiler_params=_params(2))(qn, qr, kv, kv, kr, do)


def _mod_rows(mod, layer):
    return [mod[layer:layer + 1, i * D:(i + 1) * D] for i in range(N_MOD)]


def _local_step(x, target, pos_col, mod, weights_of, P, on_grads):
    cos, sin = _rope_tables(pos_col)
    saved = []
    for l in range(DEPTH):
        j = l // 2
        sh_m, sc_m, ga_m, sh_f, sc_f, ga_f = _mod_rows(mod, l)
        s = dict(x0=x)
        h = _norm_mod_fwd(f"norm_mix{l}", x, P["norm_mix_g"][l:l + 1], sc_m, sh_m)
        W = weights_of(l, h)
        s.update(h=h, W=W)
        if l % 2 == 0:
            proj = _mm(f"gdn_in{j}", h, W["gdn_in"], "nn", tn=GDN_INK // 2)
            qkv = _gdn_conv_fwd(f"gdn_conv{j}", proj, P["gdn_cw"][j])
            g, beta = _gdn_gates_fwd(f"gdn_gates{j}", proj, P["gdn_alog"][j], P["gdn_dtb"][j])
            o, states = _gdn_chunk_fwd(f"gdn_chunk{j}", qkv, g, beta)
            on = _gdn_gated_norm_fwd(f"gdn_gnorm{j}", o, proj, P["gdn_ng"][j])
            y = _mm(f"gdn_out{j}", on, W["gdn_out"], "nn")
            s.update(proj=proj, qkv=qkv, g=g, beta=beta, o=o, states=states, on=on)
        else:
            proj = _mm(f"mla_in{j}", h, W["mla_in"], "nn")
            cqn, ckvn, kr = _mla_pre_fwd(f"mla_pre{j}", proj, cos, sin, P["mla_qg"][j], P["mla_kvg"][j])
            q = _mm(f"mla_uq{j}", cqn, W["mla_uq"], "nn")
            kv = _mm(f"mla_ukv{j}", ckvn, W["mla_ukv"], "nn", out_dtype=BF16)
            qn, qr = _mla_q_fwd(f"mla_q{j}", q, cos, sin)
            o = _mla_attn_fwd(f"mla_attn{j}", qn, qr, kv, kr)
            y = _mm(f"mla_out{j}", o, W["mla_out"], "nn")
            s.update(proj=proj, cqn=cqn, ckvn=ckvn, kr=kr, kv=kv, qn=qn, qr=qr, o=o)
        s["y"] = y
        x = _residual_fwd(f"res_mix{l}", x, y, ga_m)
        s["x1"] = x
        h2 = _norm_mod_fwd(f"norm_ffn{l}", x, P["norm_ffn_g"][l:l + 1], sc_f, sh_f)
        fa, fb, sw = _ffn_up(f"ffn_up{l}", h2, W["ffn_g"], W["ffn_u"], 0)
        yf = _ffn_down(f"ffn_down{l}", sw, W["ffn_d"], 0)
        x = _residual_fwd(f"res_ffn{l}", x, yf, ga_f)
        s.update(h2=h2, fa=fa, fb=fb, sw=sw, yf=yf)
        saved.append(s)

    dx, loss, d_final = _loss_head(x, target, P["final_g"])
    gP = dict(loss=loss, final_g=d_final, norm_mix_g=[None] * DEPTH, norm_ffn_g=[None] * DEPTH,
              gdn_cw=[None] * 2, gdn_alog=[None] * 2, gdn_dtb=[None] * 2, gdn_ng=[None] * 2,
              mla_qg=[None] * 2, mla_kvg=[None] * 2)
    dmod = [None] * DEPTH
    for l in reversed(range(DEPTH)):
        j = l // 2
        s = saved[l]
        W = s["W"]
        sh_m, sc_m, ga_m, sh_f, sc_f, ga_f = _mod_rows(mod, l)
        dyf, d_ga_f = _residual_bwd(f"res_ffn_b{l}", dx, s["yf"], ga_f)
        da, db = _ffn_down_bwd(f"ffn_down_dx{l}", dyf, W["ffn_d"], s["fa"], s["fb"], 0)
        g_down = _ffn_down_dw(f"ffn_down_dw{l}", s["sw"], dyf)
        g_gate, g_up = _ffn_up_dw(f"ffn_up_dw{l}", s["h2"], da, db)
        on_grads(l, "ffn", dict(ffn_w_gate=g_gate, ffn_w_up=g_up, ffn_w_down=g_down))
        dh2 = _ffn_up_dx(f"ffn_up_dx{l}", da, db, W["ffn_g"], W["ffn_u"], 0)
        dx, d_sh_f, d_sc_f, gP["norm_ffn_g"][l] = _norm_mod_bwd(f"norm_ffn_b{l}", dh2, s["x1"], dx,
                                                                 P["norm_ffn_g"][l:l + 1], sc_f)
        dy, d_ga_m = _residual_bwd(f"res_mix_b{l}", dx, s["y"], ga_m)
        if l % 2 == 0:
            don = _mm(f"gdn_out_dx{j}", dy, W["gdn_out"], "nt")
            g_out = _mm(f"gdn_out_dw{j}", s["on"], dy, "tn", out_dtype=BF16)
            do, dgate, gP["gdn_ng"][j] = _gdn_gated_norm_bwd(f"gdn_gnorm_b{j}", don, s["o"], s["proj"], P["gdn_ng"][j])
            dq, dk, dv, dg_h, db_h = _gdn_chunk_bwd(f"gdn_chunk_b{j}", s["qkv"], s["g"], s["beta"], s["states"], do)
            dab_, gP["gdn_alog"][j], gP["gdn_dtb"][j] = _gdn_gates_bwd(f"gdn_gates_b{j}", s["proj"], dg_h, db_h,
                                                                        P["gdn_alog"][j], P["gdn_dtb"][j])
            dpre, gP["gdn_cw"][j] = _gdn_conv_bwd(f"gdn_conv_b{j}", s["proj"], P["gdn_cw"][j],
                                                  jnp.concatenate([dq, dk, dv], axis=1))
            dproj = jnp.concatenate([dpre, dgate, dab_], axis=1)
            g_in = _mm(f"gdn_in_dw{j}", s["h"], dproj, "tn", out_dtype=BF16, tn=GDN_INK // 2)
            on_grads(l, "mix", dict(gdn_w_in=_uncols(_gdn_in_from_kernel(g_in)), gdn_w_out=_unrows(g_out)))
            dh = _mm(f"gdn_in_dx{j}", dproj, W["gdn_in"], "nt")
        else:
            do = _mm(f"mla_out_dx{j}", dy, W["mla_out"], "nt", out_dtype=BF16)
            g_out = _mm(f"mla_out_dw{j}", s["o"], dy, "tn", out_dtype=BF16)
            dqn, dqr, dkn, dv, dkr = _mla_attn_bwd(f"mla_attn_b{j}", s["qn"], s["qr"], s["kv"], s["kr"], do)
            dq = _mla_q_bwd(f"mla_q_b{j}", dqn, dqr, cos, sin)
            dkv = jnp.concatenate([dkn, dv], axis=1)
            g_uq = _mm(f"mla_uq_dw{j}", s["cqn"], dq, "tn", out_dtype=BF16)
            dcqn = _mm(f"mla_uq_dx{j}", dq, W["mla_uq"], "nt")
            g_ukv = _mm(f"mla_ukv_dw{j}", s["ckvn"], dkv, "tn", out_dtype=BF16)
            dckvn = _mm(f"mla_ukv_dx{j}", dkv, W["mla_ukv"], "nt")
            dproj, gP["mla_qg"][j], gP["mla_kvg"][j] = _mla_pre_bwd(f"mla_pre_b{j}", s["proj"], dcqn, dckvn, dkr, cos, sin,
                                                                     P["mla_qg"][j], P["mla_kvg"][j])
            g_in = _mm(f"mla_in_dw{j}", s["h"], dproj, "tn", out_dtype=BF16)
            on_grads(l, "mix", dict(mla_w_in=_unrows(g_in[:, :Q_RANK + KV_RANK + ROPE]), mla_w_uq=_uncols(_mla_uq_from_kernel(g_uq)),
                                    mla_w_ukv=_uncols(_mla_ukv_from_kernel(g_ukv)), mla_w_out=_unrows(g_out)))
            dh = _mm(f"mla_in_dx{j}", dproj, W["mla_in"], "nt")
        dx, d_sh_m, d_sc_m, gP["norm_mix_g"][l] = _norm_mod_bwd(f"norm_mix_b{l}", dh, s["x0"], dx,
                                                                 P["norm_mix_g"][l:l + 1], sc_m)
        dmod[l] = jnp.concatenate([d_sh_m, d_sc_m, d_ga_m, d_sh_f, d_sc_f, d_ga_f], axis=1)
    return dx, jnp.concatenate(dmod, axis=0), gP


def _pad_cols(a, width):
    return jnp.pad(a, ((0, 0), (0, width - a.shape[1])))


def _gdn_in_to_kernel(w):
    m = GDN_QKV + NH * HD
    return jnp.concatenate([w[:, :m], _pad_cols(w[:, m:m + NH], HD), _pad_cols(w[:, m + NH:], HD)], axis=1)


def _gdn_in_from_kernel(g):
    m = GDN_QKV + NH * HD
    return jnp.concatenate([g[:, :m], g[:, m:m + NH], g[:, m + HD:m + HD + NH]], axis=1)


def _mla_uq_to_kernel(w):
    w3 = w.reshape(Q_RANK, NH, HD + ROPE)
    rope = jnp.pad(w3[:, :, HD:], ((0, 0), (0, 0), (0, HD - ROPE)))
    return jnp.concatenate([w3[:, :, :HD].reshape(Q_RANK, NH * HD), rope.reshape(Q_RANK, NH * HD)], axis=1)


def _mla_uq_from_kernel(g):
    gn = g[:, :NH * HD].reshape(Q_RANK, NH, HD)
    gr = g[:, NH * HD:].reshape(Q_RANK, NH, HD)[:, :, :ROPE]
    return jnp.concatenate([gn, gr], axis=2).reshape(Q_RANK, NH * (HD + ROPE))


def _mla_ukv_to_kernel(w):
    w3 = w.reshape(KV_RANK, NH, 2 * HD)
    return jnp.concatenate([w3[:, :, :HD].reshape(KV_RANK, NH * HD), w3[:, :, HD:].reshape(KV_RANK, NH * HD)], axis=1)


def _mla_ukv_from_kernel(g):
    gk = g[:, :NH * HD].reshape(KV_RANK, NH, HD)
    gv = g[:, NH * HD:].reshape(KV_RANK, NH, HD)
    return jnp.concatenate([gk, gv], axis=2).reshape(KV_RANK, NH * 2 * HD)


def _cols(t):
    return jnp.moveaxis(t, 0, 1).reshape(t.shape[1], -1)


def _uncols(g):
    return jnp.moveaxis(g.reshape(g.shape[0], 4, -1), 1, 0)


def _rows(t):
    return t.reshape(-1, t.shape[2])


def _unrows(g):
    return g.reshape(4, -1, g.shape[1])


def _layer_weights(layer):
    mixer = ("gdn_w_in", "gdn_w_out") if layer % 2 == 0 else ("mla_w_in", "mla_w_uq", "mla_w_ukv", "mla_w_out")
    return [(n, layer // 2) for n in mixer] + [(n, layer) for n in ("ffn_w_gate", "ffn_w_up", "ffn_w_down")]


def _weights_to_kernel(layer, g):
    out = dict(ffn_g=g["ffn_w_gate"], ffn_u=g["ffn_w_up"], ffn_d=g["ffn_w_down"])
    if layer % 2 == 0:
        out.update(gdn_in=_gdn_in_to_kernel(_cols(g["gdn_w_in"])), gdn_out=_rows(g["gdn_w_out"]))
    else:
        out.update(mla_in=_pad_cols(_rows(g["mla_w_in"]), MLA_INK), mla_uq=_mla_uq_to_kernel(_cols(g["mla_w_uq"])),
                   mla_ukv=_mla_ukv_to_kernel(_cols(g["mla_w_ukv"])), mla_out=_rows(g["mla_w_out"]))
    return out


def _small_to_kernel(norm_mix_g, norm_ffn_g, final_norm_g, gdn_conv_w, gdn_a_log, gdn_dt_bias, gdn_norm_g, q_norm_g, kv_norm_g):
    return dict(
        norm_mix_g=norm_mix_g, norm_ffn_g=norm_ffn_g, final_g=final_norm_g.reshape(1, D),
        gdn_cw=[jnp.transpose(gdn_conv_w[j]) for j in range(2)],
        gdn_alog=[_pad_cols(gdn_a_log[j:j + 1], HD) for j in range(2)],
        gdn_dtb=[_pad_cols(gdn_dt_bias[j:j + 1], HD) for j in range(2)],
        gdn_ng=[gdn_norm_g[j:j + 1] for j in range(2)],
        mla_qg=[q_norm_g[j:j + 1] for j in range(2)],
        mla_kvg=[kv_norm_g[j:j + 1] for j in range(2)],
    )


_CHIP_FLIPS = ((1, 0), (0, 1), (1, 1))
_ANY = pl.BlockSpec(memory_space=pl.ANY)


def _me():
    return lax.axis_index("x"), lax.axis_index("y"), lax.axis_index("c")


def _chip_peer(dx, dy):
    x, y, c = _me()
    return ((1 - x) if dx else x, (1 - y) if dy else y, c)


def _rcopy(src, dst, send_sem, recv_sem, to):
    return pltpu.make_async_remote_copy(src_ref=src, dst_ref=dst, send_sem=send_sem, recv_sem=recv_sem,
                                        device_id=to, device_id_type=MESH)


def _allgather4(name, a, halves=False):
    R, C = a.shape
    rh = R // 2 if halves else R

    def body(a_ref, out_ref, send_sems, recv_sems, local_sem):
        x, y, c = _me()
        me = 2 * x + y
        src = a_ref.at[pl.ds(c * rh, rh)] if halves else a_ref
        local = pltpu.make_async_copy(src, out_ref.at[me], local_sem)
        local.start()
        sends = []
        for k, (dx, dy) in enumerate(_CHIP_FLIPS):
            cp = _rcopy(src, out_ref.at[me], send_sems.at[k], recv_sems.at[k], _chip_peer(dx, dy))
            cp.start()
            sends.append(cp)
        for k, (dx, dy) in enumerate(_CHIP_FLIPS):
            px, py, _ = _chip_peer(dx, dy)
            _rcopy(src, out_ref.at[2 * px + py], send_sems.at[k], recv_sems.at[k], _chip_peer(dx, dy)).wait_recv()
        for cp in sends:
            cp.wait_send()
        local.wait()

    return pl.pallas_call(
        body, name=name, in_specs=[_ANY], out_specs=_ANY, out_shape=jax.ShapeDtypeStruct((4, rh, C), a.dtype),
        scratch_shapes=[pltpu.SemaphoreType.DMA((3,)), pltpu.SemaphoreType.DMA((3,)), pltpu.SemaphoreType.DMA(())])(a)


_NCH = 4


def _dma_sems(*counts):
    return [pltpu.SemaphoreType.DMA((n,)) for n in counts]


def _slot_tile(rows, cap=512):
    best = rows
    for tr in range(16, min(rows, cap) + 1, 16):
        if rows % tr == 0:
            best = tr
    return best


def _cast_into_slot(name, a, chip, row0, rows):
    C = a.shape[1]
    tr = _slot_tile(rows)
    assert row0 % tr == 0
    first = row0 // tr

    def body(c_ref, a_ref, o_ref):
        o_ref[0] = a_ref[...].astype(o_ref.dtype)

    grid_spec = pltpu.PrefetchScalarGridSpec(
        num_scalar_prefetch=1, grid=(rows // tr,), in_specs=[pl.BlockSpec((tr, C), lambda i, c_ref: (first + i, 0))],
        out_specs=pl.BlockSpec((1, tr, C), lambda i, c_ref: (c_ref[0], i, 0)))
    return pl.pallas_call(body, name=name, grid_spec=grid_spec, out_shape=jax.ShapeDtypeStruct((4, rows, C), BF16),
                          compiler_params=_params(1))(chip, a)


def _chunks(rows, align):
    for nch in (_NCH, 2):
        if rows % (nch * align) == 0:
            return nch
    return 1


def _gather_exchange(out, ici_s, ici_r, d2d_s, d2d_r):
    n = len(out)
    x, y, c = _me()
    me = 2 * x + y
    sib = (x, y, 1 - c)
    peers = [_chip_peer(dx, dy) for dx, dy in _CHIP_FLIPS]
    for t in range(n):
        h = out[t].shape[1] // 2
        nch = _chunks(h, 16)
        ch = h // nch
        for k, peer in enumerate(peers):
            for i in range(nch):
                blk = out[t].at[me, pl.ds(c * h + i * ch, ch)]
                _rcopy(blk, blk, ici_s.at[3 * t + k], ici_r.at[3 * t + k], peer).start()
    for t in range(n):
        h = out[t].shape[1] // 2
        nch = _chunks(h, 16)
        ch = h // nch
        for k, peer in enumerate(peers):
            pchip = 2 * peer[0] + peer[1]
            got = out[t].at[pchip, pl.ds(c * h, h)]
            _rcopy(got, got, ici_s.at[3 * t + k], ici_r.at[3 * t + k], peer).wait_recv()
            for i in range(nch):
                blk = out[t].at[pchip, pl.ds(c * h + i * ch, ch)]
                _rcopy(blk, blk, d2d_s.at[3 * t + k], d2d_r.at[3 * t + k], sib).start()
    for t in range(n):
        h = out[t].shape[1] // 2
        for k, peer in enumerate(peers):
            pchip = 2 * peer[0] + peer[1]
            other = out[t].at[pchip, pl.ds((1 - c) * h, h)]
            _rcopy(other, other, d2d_s.at[3 * t + k], d2d_r.at[3 * t + k], sib).wait_recv()
            _rcopy(other, other, ici_s.at[3 * t + k], ici_r.at[3 * t + k], peer).wait_send()
            _rcopy(other, other, d2d_s.at[3 * t + k], d2d_r.at[3 * t + k], sib).wait_send()


def _gather_weights(name, bufs):
    n = len(bufs)

    def body(*refs):
        _gather_exchange(refs[n:2 * n], *refs[2 * n:])

    return pl.pallas_call(
        body, name=name, in_specs=[_ANY] * n, out_specs=[_ANY] * n,
        out_shape=[jax.ShapeDtypeStruct(s.shape, s.dtype) for s in bufs],
        input_output_aliases={t: t for t in range(n)},
        scratch_shapes=_dma_sems(3 * n, 3 * n, 3 * n, 3 * n))(*bufs)


def _gather_weights_async(name, collective_id, bufs):
    n = len(bufs)
    refs = [jax.new_ref(b, memory_space=pltpu.MemorySpace.HBM) for b in bufs]

    @pl.kernel(mesh=plsc.ScalarSubcoreMesh(axis_name="sequencer", num_cores=1), name=name,
               scratch_types=tuple(_dma_sems(3 * n, 3 * n, 3 * n, 3 * n)),
               compiler_params=pltpu.CompilerParams(collective_id=collective_id))
    def launch(ici_s, ici_r, d2d_s, d2d_r):
        x, y, c = _me()
        barrier = pltpu.get_barrier_semaphore()
        for peer in [_chip_peer(dx, dy) for dx, dy in _CHIP_FLIPS] + [(x, y, 1 - c)]:
            pl.semaphore_signal(barrier, inc=1, device_id=peer, device_id_type=MESH)
        pl.semaphore_wait(barrier, 4)
        _gather_exchange(refs, ici_s, ici_r, d2d_s, d2d_r)

    launch()
    return [r[...] for r in refs]


def _rs_split(name, grads):
    n = len(grads)

    def body(*refs):
        g, out = refs[:n], refs[n:2 * n]
        send, recv = refs[2 * n:]
        x, y, c = _me()
        sib = (x, y, 1 - c)
        for t in range(n):
            h = g[t].shape[1] // 2
            for d in range(4):
                _rcopy(g[t].at[d, pl.ds((1 - c) * h, h)], out[t].at[d], send.at[t], recv.at[t], sib).start()
        for t in range(n):
            _rcopy(out[t], out[t], send.at[t], recv.at[t], sib).wait()

    return pl.pallas_call(
        body, name=name, in_specs=[_ANY] * n, out_specs=[_ANY] * n,
        out_shape=[jax.ShapeDtypeStruct((4, s.shape[1] // 2, s.shape[2]), s.dtype) for s in grads],
        scratch_shapes=_dma_sems(n, n))(*grads)


def _pair_add(name, g, theirs, core_chip):
    _, R, C = g.shape
    h = R // 2
    tr = _slot_tile(h)
    nb = h // tr

    def body(s_ref, g_ref, t_ref, p_ref, o_ref):
        val = (g_ref[...].astype(F32) + t_ref[...].astype(F32)).astype(p_ref.dtype)
        p_ref[...] = val

        @pl.when(pl.program_id(1) == s_ref[1])
        def _():
            o_ref[...] = val

    spec = pl.BlockSpec((1, tr, C), lambda i, d, s_ref: (d, i, 0))
    grid_spec = pltpu.PrefetchScalarGridSpec(
        num_scalar_prefetch=1, grid=(nb, 4),
        in_specs=[pl.BlockSpec((1, tr, C), lambda i, d, s_ref: (d, s_ref[0] * nb + i, 0)), spec],
        out_specs=[spec, pl.BlockSpec((1, tr, C), lambda i, d, s_ref: (s_ref[1], i, 0))])
    half = jax.ShapeDtypeStruct((4, h, C), BF16)
    return pl.pallas_call(body, name=name, grid_spec=grid_spec, out_shape=[half, half],
                          compiler_params=_params(2))(core_chip, g, theirs)


def _rs_alltoall_async(name, collective_id, parts, bufs):
    n = len(parts)
    p = [jax.new_ref(a, memory_space=pltpu.MemorySpace.HBM) for a in parts]
    out = [jax.new_ref(b, memory_space=pltpu.MemorySpace.HBM) for b in bufs]

    @pl.kernel(mesh=plsc.ScalarSubcoreMesh(axis_name="sequencer", num_cores=1), name=name,
               scratch_types=tuple(_dma_sems(3 * n, 3 * n)),
               compiler_params=pltpu.CompilerParams(collective_id=collective_id))
    def launch(send, recv):
        barrier = pltpu.get_barrier_semaphore()
        for peer in [_chip_peer(dx, dy) for dx, dy in _CHIP_FLIPS]:
            pl.semaphore_signal(barrier, inc=1, device_id=peer, device_id_type=MESH)
        pl.semaphore_wait(barrier, 3)
        _alltoall_exchange(p, out, send, recv)

    launch()
    return [r[...] for r in out]


def _alltoall_exchange(p, out, send, recv):
    x, y, c = _me()
    me = 2 * x + y
    peers = [_chip_peer(dx, dy) for dx, dy in _CHIP_FLIPS]
    for t in range(len(p)):
        h = p[t].shape[1]
        nch = _chunks(h, 16)
        ch = h // nch
        for k, peer in enumerate(peers):
            pchip = 2 * peer[0] + peer[1]
            for i in range(nch):
                rows = pl.ds(i * ch, ch)
                _rcopy(p[t].at[pchip, rows], out[t].at[me, rows], send.at[3 * t + k], recv.at[3 * t + k], peer).start()
    for t in range(len(p)):
        for k, peer in enumerate(peers):
            pchip = 2 * peer[0] + peer[1]
            _rcopy(out[t].at[pchip], out[t].at[pchip], send.at[3 * t + k], recv.at[3 * t + k], peer).wait()


def _rs_swap(name, halves):
    n = len(halves)

    def body(*refs):
        a, out = refs[:n], refs[n:2 * n]
        send, recv = refs[2 * n:]
        x, y, c = _me()
        sib = (x, y, 1 - c)
        for t in range(n):
            ch = a[t].shape[0] // _NCH
            for i in range(_NCH):
                rows = pl.ds(i * ch, ch)
                _rcopy(a[t].at[rows], out[t].at[rows], send.at[t], recv.at[t], sib).start()
        for t in range(n):
            _rcopy(a[t], out[t], send.at[t], recv.at[t], sib).wait()

    return pl.pallas_call(
        body, name=name, in_specs=[_ANY] * n, out_specs=[_ANY] * n,
        out_shape=[jax.ShapeDtypeStruct(s.shape, s.dtype) for s in halves],
        scratch_shapes=_dma_sems(n, n))(*halves)


def _sibling_merge(name, a):
    P_, rh, C = a.shape

    def body(a_ref, out_ref, send_sem, recv_sem, local_sem):
        x, y, c = _me()
        local = pltpu.make_async_copy(a_ref, out_ref.at[:, pl.ds(c * rh, rh)], local_sem)
        local.start()
        cp = _rcopy(a_ref, out_ref.at[:, pl.ds(c * rh, rh)], send_sem, recv_sem, (x, y, 1 - c))
        cp.start()
        cp.wait_send()
        _rcopy(a_ref, out_ref.at[:, pl.ds((1 - c) * rh, rh)], send_sem, recv_sem, (x, y, 1 - c)).wait_recv()
        local.wait()

    return pl.pallas_call(
        body, name=name, in_specs=[_ANY], out_specs=_ANY, out_shape=jax.ShapeDtypeStruct((P_, 2 * rh, C), a.dtype),
        scratch_shapes=[pltpu.SemaphoreType.DMA(()), pltpu.SemaphoreType.DMA(()), pltpu.SemaphoreType.DMA(())])(a)


def _allgather8(name, a):
    g4 = _allgather4(name + "_chips", a)
    both = _sibling_merge(name + "_cores", g4.reshape(1, 4 * a.shape[0], a.shape[1]))
    return jnp.transpose(both.reshape(2, 4, *a.shape), (1, 0, 2, 3)).reshape(8, *a.shape)


def _sum_slots(name, a, out_dtype):
    def fn(a):
        acc = a[0].astype(F32)
        for k in range(1, a.shape[0]):
            acc = acc + a[k].astype(F32)
        return acc
    return _rowwise(name, fn, [a], [], [(a.shape[2], out_dtype)])[0]


def _adamw_math(w, g, m, v):
    m = ADAM_B1 * m + (1.0 - ADAM_B1) * g
    v = ADAM_B2 * v + (1.0 - ADAM_B2) * (g * g)
    m_hat = m / (1.0 - ADAM_B1 ** ADAM_STEP)
    v_hat = v / (1.0 - ADAM_B2 ** ADAM_STEP)
    return -ADAM_LR * (m_hat / (jnp.sqrt(v_hat) + ADAM_EPS) + ADAM_WD * w), m, v


def _adamw_piece(name, w2, m2, v2, mine, theirs, row0, prev):
    R, C = w2.shape
    h = mine.shape[0]
    tr = _slot_tile(h, 256)
    nb = h // tr
    assert row0 % tr == 0
    first = row0 // tr

    def body(w_ref, m_ref, v_ref, a_ref, b_ref, *rest):
        g_ref, d_ref, nm_ref, nv_ref = rest[-4:]
        g = jnp.where(pl.program_id(0) == lax.axis_index("c"), a_ref[...], b_ref[...])
        g_ref[...] = g
        d_ref[...], nm_ref[...], nv_ref[...] = _adamw_math(w_ref[...], g, m_ref[...], v_ref[...])

    full = pl.BlockSpec((tr, C), lambda s, i: (first + s * nb + i, 0))
    half = pl.BlockSpec((tr, C), lambda s, i: (i, 0))
    extra = [] if prev is None else list(prev)
    return pl.pallas_call(
        body, name=name, grid=(2, nb), in_specs=[full, full, full, half, half] + [_ANY] * len(extra), out_specs=[full] * 4,
        out_shape=[jax.ShapeDtypeStruct((R, C), F32)] * 4, input_output_aliases={5 + k: k for k in range(len(extra))},
        compiler_params=_params(2))(w2, m2, v2, mine, theirs, *extra)


def _adamw(name, w, g, m, v):
    shape = w.shape
    two_d = (-1, shape[-1]) if w.ndim > 1 else (1, -1)
    w2, g2, m2, v2 = [t.reshape(two_d) for t in (w, g, m, v)]
    rows = w2.shape[0]
    tr = rows
    for cand in (256, 128, 64, 32, 16, 8):
        if rows % cand == 0:
            tr = cand
            break

    c = w2.shape[1]
    outs = _rowwise(name, _adamw_math, [w2, g2, m2, v2], [], [(c, F32)] * 3, tr=tr)
    return [o.reshape(shape) for o in outs]


_WEIGHT_ORDER = ("ada_w", "ada_b", "norm_mix_g", "norm_ffn_g", "gdn_w_in", "gdn_conv_w", "gdn_a_log", "gdn_dt_bias",
                 "gdn_norm_g", "gdn_w_out", "mla_w_in", "mla_q_norm_g", "mla_kv_norm_g", "mla_w_uq", "mla_w_ukv",
                 "mla_w_out", "ffn_w_gate", "ffn_w_up", "ffn_w_down", "final_norm_g")
_BIG = (("gdn_w_in", 2), ("gdn_w_out", 1), ("mla_w_in", 1), ("mla_w_uq", 2), ("mla_w_ukv", 2), ("mla_w_out", 1),
        ("ffn_w_gate", 2), ("ffn_w_up", 2), ("ffn_w_down", 1))
_SMALL_SHARDED = (("gdn_conv_w", 1), ("mla_q_norm_g", 1), ("mla_kv_norm_g", 1))
_STORED_TRANSPOSED = ("ffn_w_gate", "ffn_w_up")


def _size(shape):
    n = 1
    for s in shape:
        n *= s
    return n


def _pack_rows_each(tensors):
    parts, offs, off = [], [], 0
    for t in tensors:
        flat = t.reshape(-1).astype(F32)
        rows = -(-flat.shape[0] // PACK_W)
        parts.append(jnp.pad(flat, (0, rows * PACK_W - flat.shape[0])).reshape(rows, PACK_W))
        offs.append(off)
        off += rows
    total = -(-off // 16) * 16
    pack = jnp.pad(parts[0], ((offs[0], total - offs[0] - parts[0].shape[0]), (0, 0)))
    for p, o in zip(parts[1:], offs[1:]):
        pack = pack + jnp.pad(p, ((o, total - o - p.shape[0]), (0, 0)))
    return pack, offs


def _unpack_rows_each(pack, shapes):
    lead = pack.shape[:-2]
    out, off = [], 0
    for shp in shapes:
        n = _size(shp)
        rows = -(-n // PACK_W)
        out.append(pack[..., off:off + rows, :].reshape(*lead, -1)[..., :n].reshape(*lead, *shp))
        off += rows
    return out


def _merge_chips(stacked, axis):
    moved = jnp.moveaxis(stacked, 0, axis)
    shp = list(moved.shape)
    return moved.reshape(shp[:axis] + [shp[axis] * shp[axis + 1]] + shp[axis + 2:])


def _my_shard(full, axis, chip):
    n = full.shape[axis] // 4
    return lax.dynamic_slice_in_dim(full, chip * n, n, axis)


def kernel(x, c, positions, ada_w, ada_b, norm_mix_g, norm_ffn_g, gdn_w_in, gdn_conv_w, gdn_a_log, gdn_dt_bias, gdn_norm_g, gdn_w_out, mla_w_in, mla_q_norm_g, mla_kv_norm_g, mla_w_uq, mla_w_ukv, mla_w_out, ffn_w_gate, ffn_w_up, ffn_w_down, final_norm_g, loss_target, m_ada_w, m_ada_b, m_norm_mix_g, m_norm_ffn_g, m_gdn_w_in, m_gdn_conv_w, m_gdn_a_log, m_gdn_dt_bias, m_gdn_norm_g, m_gdn_w_out, m_mla_w_in, m_mla_q_norm_g, m_mla_kv_norm_g, m_mla_w_uq, m_mla_w_ukv, m_mla_w_out, m_ffn_w_gate, m_ffn_w_up, m_ffn_w_down, m_final_norm_g, v_ada_w, v_ada_b, v_norm_mix_g, v_norm_ffn_g, v_gdn_w_in, v_gdn_conv_w, v_gdn_a_log, v_gdn_dt_bias, v_gdn_norm_g, v_gdn_w_out, v_mla_w_in, v_mla_q_norm_g, v_mla_kv_norm_g, v_mla_w_uq, v_mla_w_ukv, v_mla_w_out, v_ffn_w_gate, v_ffn_w_up, v_ffn_w_down, v_final_norm_g):
    w = dict(ada_w=ada_w, ada_b=ada_b, norm_mix_g=norm_mix_g, norm_ffn_g=norm_ffn_g, gdn_w_in=gdn_w_in, gdn_conv_w=gdn_conv_w,
             gdn_a_log=gdn_a_log, gdn_dt_bias=gdn_dt_bias, gdn_norm_g=gdn_norm_g, gdn_w_out=gdn_w_out, mla_w_in=mla_w_in,
             mla_q_norm_g=mla_q_norm_g, mla_kv_norm_g=mla_kv_norm_g, mla_w_uq=mla_w_uq, mla_w_ukv=mla_w_ukv,
             mla_w_out=mla_w_out, ffn_w_gate=ffn_w_gate, ffn_w_up=ffn_w_up, ffn_w_down=ffn_w_down, final_norm_g=final_norm_g)
    m = dict(ada_w=m_ada_w, ada_b=m_ada_b, norm_mix_g=m_norm_mix_g, norm_ffn_g=m_norm_ffn_g, gdn_w_in=m_gdn_w_in,
             gdn_conv_w=m_gdn_conv_w, gdn_a_log=m_gdn_a_log, gdn_dt_bias=m_gdn_dt_bias, gdn_norm_g=m_gdn_norm_g,
             gdn_w_out=m_gdn_w_out, mla_w_in=m_mla_w_in, mla_q_norm_g=m_mla_q_norm_g, mla_kv_norm_g=m_mla_kv_norm_g,
             mla_w_uq=m_mla_w_uq, mla_w_ukv=m_mla_w_ukv, mla_w_out=m_mla_w_out, ffn_w_gate=m_ffn_w_gate,
             ffn_w_up=m_ffn_w_up, ffn_w_down=m_ffn_w_down, final_norm_g=m_final_norm_g)
    v = dict(ada_w=v_ada_w, ada_b=v_ada_b, norm_mix_g=v_norm_mix_g, norm_ffn_g=v_norm_ffn_g, gdn_w_in=v_gdn_w_in,
             gdn_conv_w=v_gdn_conv_w, gdn_a_log=v_gdn_a_log, gdn_dt_bias=v_gdn_dt_bias, gdn_norm_g=v_gdn_norm_g,
             gdn_w_out=v_gdn_w_out, mla_w_in=v_mla_w_in, mla_q_norm_g=v_mla_q_norm_g, mla_kv_norm_g=v_mla_kv_norm_g,
             mla_w_uq=v_mla_w_uq, mla_w_ukv=v_mla_w_ukv, mla_w_out=v_mla_w_out, ffn_w_gate=v_ffn_w_gate,
             ffn_w_up=v_ffn_w_up, ffn_w_down=v_ffn_w_down, final_norm_g=v_final_norm_g)
    T = x.shape[1]
    ix, iy, ic = _me()
    chip = 2 * ix + iy
    seq = 2 * chip + ic
    n_dev = 8

    small_shapes = [w[n].shape for n, _ in _SMALL_SHARDED] + [c.shape]
    pack0, _ = _pack_rows_each([w[n] for n, _ in _SMALL_SHARDED] + [c])
    got0 = _unpack_rows_each(_allgather8("gather_small", pack0), small_shapes)
    small_full = {n: _merge_chips(g[0::2], ax) for (n, ax), g in zip(_SMALL_SHARDED, got0)}
    c_all = got0[-1].reshape(n_dev, D)

    big = [n for n, _ in _BIG]
    chip_arr = chip.astype(jnp.int32).reshape(1)

    def stored(n, t):
        return jnp.swapaxes(t, 1, 2) if n in _STORED_TRANSPOSED else t

    ws, ms, vs = [{n: stored(n, d[n]) for n in big} for d in (w, m, v)]
    two_d = lambda t: t.reshape(-1, t.shape[-1])

    gathered = []
    for l in range(DEPTH):
        names = _layer_weights(l)
        bufs = [_cast_into_slot(f"to_bf16_{n}{l}", two_d(ws[n]), chip_arr, j * ws[n].shape[1], ws[n].shape[1]) for n, j in names]
        filled = _gather_weights("gather_weights0", bufs) if l == 0 else _gather_weights_async(f"gather_weights{l}", l, bufs)
        gathered.append({n: b for (n, _), b in zip(names, filled)})

    def weights_of(l, h):
        return _weights_to_kernel(l, gathered[l])

    P = _small_to_kernel(norm_mix_g, norm_ffn_g, final_norm_g, small_full["gdn_conv_w"], gdn_a_log, gdn_dt_bias,
                         gdn_norm_g, small_full["mla_q_norm_g"], small_full["mla_kv_norm_g"])

    c16 = jnp.pad(c_all, ((0, 16 - n_dev), (0, 0)))
    ca = _rowwise("cond_silu", lambda t: t * _sig(t), [c16], [], [(D, BF16)])[0]
    n_ada = ada_w.shape[2]
    mods = jnp.concatenate([_mm(f"ada_fwd{l}", ca, ada_w[l], "nn") for l in range(DEPTH)], axis=0)
    mods_all = _allgather4("gather_mod", mods).reshape(4, DEPTH, 16, n_ada)
    mod_mm = jnp.transpose(lax.dynamic_index_in_dim(mods_all, seq, axis=2, keepdims=False), (1, 0, 2)).reshape(DEPTH, 4 * n_ada)
    mod = _rowwise("mod_bias", lambda a, b: a + b, [mod_mm, ada_b], [], [(4 * n_ada, F32)])[0]

    core_chip = jnp.stack([ic, chip]).astype(jnp.int32)
    pending, in_flight = {}, []

    def reduce_group(layer, part, pieces):
        pending.update({(n, layer if n.startswith("ffn_") else layer // 2): g for n, g in pieces.items()})
        if part == "ffn" and layer > 0:
            return
        keys = list(pending)
        glist = [pending.pop(k) for k in keys]
        tag = f"{layer}{part}"
        theirs = _rs_split("grads_cores_" + tag, glist)
        both = [_pair_add(f"grads_pair_{n}{l}", g, t, core_chip) for (n, l), g, t in zip(keys, glist, theirs)]
        swapped = _rs_alltoall_async("grads_chips_" + tag, DEPTH + 1 + len(in_flight), [p for p, _ in both], [o for _, o in both])
        in_flight.append((tag, keys, swapped))

    dx, dmod, gP = _local_step(x.reshape(T, D), loss_target.reshape(T, D), positions.reshape(T, 1), mod, weights_of, P, reduce_group)

    partials = [dmod, jnp.concatenate(gP["norm_mix_g"]), jnp.concatenate(gP["norm_ffn_g"]), gP["final_g"],
                jnp.stack([jnp.transpose(g) for g in gP["gdn_cw"]]), jnp.concatenate(gP["gdn_alog"])[:, :NH],
                jnp.concatenate(gP["gdn_dtb"])[:, :NH], jnp.concatenate(gP["gdn_ng"]), jnp.concatenate(gP["mla_qg"]),
                jnp.concatenate(gP["mla_kvg"]), gP["loss"][:, :1]]
    part_shapes = [p.shape for p in partials]
    ppack, _ = _pack_rows_each(partials)
    pall = _allgather8("gather_partials", ppack)
    psum = _sum_slots("sum_partials", pall, F32)
    (g_ada_b, g_norm_mix, g_norm_ffn, g_final, g_conv_full, g_alog, g_dtb, g_gdn_ng, g_qg_full, g_kvg_full,
     loss_sum) = _unpack_rows_each(psum, part_shapes)
    dmod_all = _unpack_rows_each(pall, part_shapes[:1])[0]

    grads = dict(ada_b=g_ada_b, norm_mix_g=g_norm_mix, norm_ffn_g=g_norm_ffn, final_norm_g=g_final.reshape(D),
                 gdn_conv_w=_my_shard(g_conv_full, 1, chip), gdn_a_log=g_alog, gdn_dt_bias=g_dtb, gdn_norm_g=g_gdn_ng,
                 mla_q_norm_g=_my_shard(g_qg_full, 1, chip), mla_kv_norm_g=_my_shard(g_kvg_full, 1, chip))

    ca_t = jnp.zeros((D, LANES), BF16).at[:, :16].set(jnp.transpose(ca))
    dm_mine = lax.dynamic_slice_in_dim(dmod_all, chip * n_ada, n_ada, axis=2)
    grads["ada_w"] = jnp.stack([
        _mm(f"ada_bwd{l}", ca_t, jnp.pad(dm_mine[:, l], ((0, LANES - n_dev), (0, 0))), "nn") for l in range(DEPTH)])

    delta, new_m, new_v = {}, {}, {}
    results = {}
    keys = [k for _, ks, _ in in_flight for k in ks]
    halves = [_sum_slots(f"grads_sum_{n}{l}", s, F32) for _, ks, sw in in_flight for (n, l), s in zip(ks, sw)]
    others = _rs_swap("grads_swap", halves)
    for (n, l), mine, theirs in zip(keys, halves, others):
        results[n] = _adamw_piece(f"adamw_{n}{l}", two_d(ws[n]), two_d(ms[n]), two_d(vs[n]), mine, theirs,
                                  l * ws[n].shape[1], results.get(n))
    for n in big:
        grads[n], delta[n], new_m[n], new_v[n] = [stored(n, t.reshape(ws[n].shape)) for t in results[n]]
    delta["ada_w"], new_m["ada_w"], new_v["ada_w"] = _adamw("adamw_ada_w", ada_w, grads["ada_w"], m_ada_w, v_ada_w)
    small_names = [n for n in _WEIGHT_ORDER if n not in delta]
    small_shapes = [w[n].shape for n in small_names]
    packs = [_pack_rows_each([d[n] for n in small_names])[0] for d in (w, grads, m, v)]
    for d, pk in zip((delta, new_m, new_v), _adamw("adamw_small", *packs)):
        for n, t in zip(small_names, _unpack_rows_each(pk, small_shapes)):
            d[n] = t

    loss = loss_sum.reshape(())
    return (loss, dx.reshape(1, T, D), *[grads[n] for n in _WEIGHT_ORDER], *[delta[n] for n in _WEIGHT_ORDER],
            *[new_m[n] for n in _WEIGHT_ORDER], *[new_v[n] for n in _WEIGHT_ORDER])
```

```python
import functools

import jax
import jax.numpy as jnp
from jax import lax
from jax.experimental import pallas as pl
from jax.experimental.pallas import tpu as pltpu
from jax.experimental.pallas import tpu_sc as plsc

F32 = jnp.float32
BF16 = jnp.bfloat16
HI = lax.Precision.HIGHEST
MESH = pl.DeviceIdType.MESH

D = 1024
DEPTH = 4
N_MOD = 6
NH = 8
HD = 128
CHUNK = 64
_GDN_HB = 8
GDN_QKV = 3 * NH * HD
GDN_INK = GDN_QKV + NH * HD + 2 * HD
Q_RANK, KV_RANK, ROPE = 384, 256, 64
MLA_INK = Q_RANK + KV_RANK + HD
DFF = 2816
EPS = 1e-6
ATT_SCALE = (HD + ROPE) ** -0.5
ROPE_THETA = 10000.0
LANES = 128
PACK_W = 1024

ADAM_LR, ADAM_B1, ADAM_B2, ADAM_EPS, ADAM_WD, ADAM_STEP = 0.001, 0.9, 0.999, 1e-08, 0.01, 10


H3 = "bf16x3"
B1 = "bf16"
HS = H3
HF = B1


def _dot(a, b, mode="nn", prec=None):
    dn = {"nn": (((1,), (0,)), ((), ())), "nt": (((1,), (1,)), ((), ())), "tn": (((0,), (0,)), ((), ()))}[mode]
    if prec == B1:
        return _dot(a.astype(BF16), b.astype(BF16), mode)
    if prec == H3:
        ah, bh = a.astype(BF16), b.astype(BF16)
        al, bl = (a - ah.astype(F32)).astype(BF16), (b - bh.astype(F32)).astype(BF16)
        return _dot(ah, bh, mode) + (_dot(ah, bl, mode) + _dot(al, bh, mode))
    return lax.dot_general(a, b, dn, precision=prec, preferred_element_type=F32)


def _sig(x):
    return 1.0 / (1.0 + jnp.exp(-x))


def _pick(n, cap):
    if n <= cap:
        return n
    best = None
    for d in range(LANES, cap + 1, LANES):
        if n % d == 0:
            best = d
    assert best is not None, (n, cap)
    return best


def _params(n_grid):
    return pltpu.CompilerParams(dimension_semantics=("arbitrary",) * n_grid, vmem_limit_bytes=56 * 1024 * 1024)


def _rowwise(name, fn, rows, consts, outs, sums=(), tr=256):
    first = rows[0][0] if isinstance(rows[0], tuple) else rows[0]
    T = first.shape[-2]
    tr = _slot_tile(T, tr)
    nr, nc, no, ns = len(rows), len(consts), len(outs), len(sums)

    windows = [c[1:] if isinstance(c, tuple) else None for c in consts]
    consts = [c[0] if isinstance(c, tuple) else c for c in consts]

    def body(*refs):
        vals = [r[...] for r in refs[:nr]]
        for r, win in zip(refs[nr:nr + nc], windows):
            vals.append(r[...] if win is None else r[win[0]:win[0] + 1, win[1] * win[2]:(win[1] + 1) * win[2]])
        res = fn(*vals)
        if not isinstance(res, (tuple, list)):
            res = (res,)
        o_refs = refs[nr + nc:nr + nc + no]
        s_refs = refs[nr + nc + no:]
        for r, val in zip(o_refs, res[:no]):
            r[...] = val.astype(r.dtype)
        if ns:
            @pl.when(pl.program_id(0) == 0)
            def _():
                for r in s_refs:
                    r[...] = jnp.zeros_like(r)
            for r, val in zip(s_refs, res[no:]):
                r[...] += val

    in_specs, args = [], []
    for a in rows:
        if isinstance(a, tuple):
            arr, width, cb = a
            in_specs.append(pl.BlockSpec((tr, width), lambda i, cb=cb: (i, cb)))
            args.append(arr)
        elif a.ndim == 3:
            in_specs.append(pl.BlockSpec((a.shape[0], tr, a.shape[2]), lambda i: (0, i, 0)))
            args.append(a)
        else:
            in_specs.append(pl.BlockSpec((tr, a.shape[1]), lambda i: (i, 0)))
            args.append(a)
    for a in consts:
        in_specs.append(pl.BlockSpec(a.shape, lambda i, nd=a.ndim: (0,) * nd))
        args.append(a)
    out_specs = [pl.BlockSpec((tr, w), lambda i: (i, 0)) for w, _ in outs]
    out_specs += [pl.BlockSpec((1, w), lambda i: (0, 0)) for w in sums]
    out_shape = [jax.ShapeDtypeStruct((T, w), dt) for w, dt in outs]
    out_shape += [jax.ShapeDtypeStruct((1, w), F32) for w in sums]
    res = pl.pallas_call(body, name=name, grid=(T // tr,), in_specs=in_specs, out_specs=out_specs,
                         out_shape=out_shape, compiler_params=_params(1))(*args)
    return res


def _mm(name, a, b, mode, out_dtype=F32, tm=512, tn=1024):
    if mode == "tn":
        K, M = a.shape
    else:
        M, K = a.shape
    N = b.shape[0] if mode == "nt" else b.shape[1]
    tm, tn = _pick(M, tm), _pick(N, tn)

    def body(a_ref, b_ref, o_ref):
        o_ref[...] = _dot(a_ref[...].astype(BF16), b_ref[...].astype(BF16), mode).astype(o_ref.dtype)

    a_spec = pl.BlockSpec((K, tm), lambda i, j: (0, i)) if mode == "tn" else pl.BlockSpec((tm, K), lambda i, j: (i, 0))
    b_spec = pl.BlockSpec((tn, K), lambda i, j: (j, 0)) if mode == "nt" else pl.BlockSpec((K, tn), lambda i, j: (0, j))
    return pl.pallas_call(body, name=name, grid=(M // tm, N // tn), in_specs=[a_spec, b_spec],
                          out_specs=pl.BlockSpec((tm, tn), lambda i, j: (i, j)),
                          out_shape=jax.ShapeDtypeStruct((M, N), out_dtype), compiler_params=_params(2))(a, b)


def _rms(x, eps=EPS):
    return lax.rsqrt(jnp.mean(x * x, axis=-1, keepdims=True) + eps)


def _norm_mod_fwd(name, x, g, scale, shift):
    def fn(x, g, scale, shift):
        return x * _rms(x) * g * (1.0 + scale) + shift
    return _rowwise(name, fn, [x], [g, scale, shift], [(D, BF16)])[0]


def _norm_mod_bwd(name, dh, x, dx_res, g, scale):
    def fn(dh, x, dx_res, g, scale):
        r = _rms(x)
        xh = x * r
        dxh = dh * (g * (1.0 + scale))
        dx = r * (dxh - xh * jnp.mean(dxh * xh, axis=-1, keepdims=True))
        dhx = dh * xh
        return (dx_res + dx, jnp.sum(dh, axis=0, keepdims=True), jnp.sum(dhx * g, axis=0, keepdims=True),
                jnp.sum(dhx * (1.0 + scale), axis=0, keepdims=True))
    return _rowwise(name, fn, [dh, x, dx_res], [g, scale], [(D, F32)], sums=[D, D, D])


def _residual_fwd(name, x, y, gate):
    def fn(x, y, gate):
        return x + gate * y
    return _rowwise(name, fn, [x, y], [gate], [(D, F32)])[0]


def _residual_bwd(name, dx, y, gate):
    def fn(dx, y, gate):
        return dx * gate, jnp.sum(dx * y, axis=0, keepdims=True)
    return _rowwise(name, fn, [dx, y], [gate], [(D, BF16)], sums=[D])


def _loss_head(x, target, g):
    def fn(x, t, g):
        r = _rms(x)
        xh = x * r
        err = xh * g - t
        loss = 0.5 * jnp.sum(jnp.mean(err * err, axis=-1, keepdims=True), axis=0, keepdims=True)
        dy = err * (1.0 / D)
        dxh = dy * g
        dx = r * (dxh - xh * jnp.mean(dxh * xh, axis=-1, keepdims=True))
        return dx, jnp.broadcast_to(loss, (1, LANES)), jnp.sum(dy * xh, axis=0, keepdims=True)
    return _rowwise("loss_head", fn, [x, target], [g], [(D, F32)], sums=[LANES, D])


def _ffn_up(name, h, wg, wu, layer, tm=1024):
    T, n = h.shape[0], wg.shape[1]
    tm = min(tm, T)

    def body(h_ref, wg_ref, wu_ref, a_ref, b_ref, s_ref):
        h = h_ref[...]
        a = _dot(h, wg_ref[0], "nt")
        b = _dot(h, wu_ref[0], "nt")
        a_ref[0] = a.astype(a_ref.dtype)
        b_ref[0] = b.astype(b_ref.dtype)
        s_ref[0] = (a * _sig(a) * b).astype(s_ref.dtype)

    wspec = pl.BlockSpec((1, n, D), lambda ch, i: (ch, layer, 0))
    ospec = pl.BlockSpec((1, tm, n), lambda ch, i: (ch, i, 0))
    return pl.pallas_call(
        body, name=name, grid=(4, T // tm), in_specs=[pl.BlockSpec((tm, D), lambda ch, i: (i, 0)), wspec, wspec],
        out_specs=[ospec, ospec, ospec],
        out_shape=[jax.ShapeDtypeStruct((4, T, n), BF16)] * 3, compiler_params=_params(2))(h, wg, wu)


def _ffn_down(name, s, wd, layer, tm=1024):
    _, T, n = s.shape
    tm = min(tm, T)

    def body(s_ref, w_ref, y_ref):
        @pl.when(pl.program_id(1) == 0)
        def _():
            y_ref[...] = jnp.zeros_like(y_ref)
        y_ref[...] += _dot(s_ref[0], w_ref[0], "nn")

    return pl.pallas_call(
        body, name=name, grid=(T // tm, 4),
        in_specs=[pl.BlockSpec((1, tm, n), lambda i, ch: (ch, i, 0)), pl.BlockSpec((1, n, D), lambda i, ch: (ch, layer, 0))],
        out_specs=pl.BlockSpec((tm, D), lambda i, ch: (i, 0)), out_shape=jax.ShapeDtypeStruct((T, D), F32),
        compiler_params=_params(2))(s, wd)


def _ffn_down_bwd(name, dy, wd, a, b, layer, tm=1024):
    _, T, n = a.shape
    tm = min(tm, T)

    def body(dy_ref, w_ref, a_ref, b_ref, da_ref, db_ref):
        ds = _dot(dy_ref[...], w_ref[0], "nt")
        a, b = a_ref[0].astype(F32), b_ref[0].astype(F32)
        sg = _sig(a)
        da_ref[0] = (ds * b * (sg * (1.0 + a * (1.0 - sg)))).astype(da_ref.dtype)
        db_ref[0] = (ds * (a * sg)).astype(db_ref.dtype)

    bspec = pl.BlockSpec((1, tm, n), lambda ch, i: (ch, i, 0))
    return pl.pallas_call(
        body, name=name, grid=(4, T // tm),
        in_specs=[pl.BlockSpec((tm, D), lambda ch, i: (i, 0)), pl.BlockSpec((1, n, D), lambda ch, i: (ch, layer, 0)), bspec, bspec],
        out_specs=[bspec, bspec], out_shape=[jax.ShapeDtypeStruct((4, T, n), BF16)] * 2,
        compiler_params=_params(2))(dy, wd, a, b)


def _ffn_down_dw(name, s, dy):
    _, T, n = s.shape

    def body(s_ref, dy_ref, o_ref):
        o_ref[0] = _dot(s_ref[0], dy_ref[...], "tn").astype(o_ref.dtype)

    return pl.pallas_call(
        body, name=name, grid=(4,),
        in_specs=[pl.BlockSpec((1, T, n), lambda ch: (ch, 0, 0)), pl.BlockSpec((T, D), lambda ch: (0, 0))],
        out_specs=pl.BlockSpec((1, n, D), lambda ch: (ch, 0, 0)), out_shape=jax.ShapeDtypeStruct((4, n, D), BF16),
        compiler_params=_params(1))(s, dy)


def _ffn_up_dw(name, h, da, db, tm=512):
    _, T, n = da.shape

    def body(h_ref, da_ref, db_ref, dg_ref, du_ref):
        h = h_ref[...]
        dg_ref[0] = _dot(da_ref[0], h, "tn").astype(dg_ref.dtype)
        du_ref[0] = _dot(db_ref[0], h, "tn").astype(du_ref.dtype)

    dspec = pl.BlockSpec((1, T, n), lambda ch, j: (ch, 0, 0))
    ospec = pl.BlockSpec((1, n, tm), lambda ch, j: (ch, 0, j))
    return pl.pallas_call(
        body, name=name, grid=(4, D // tm), in_specs=[pl.BlockSpec((T, tm), lambda ch, j: (0, j)), dspec, dspec],
        out_specs=[ospec, ospec], out_shape=[jax.ShapeDtypeStruct((4, n, D), BF16)] * 2,
        compiler_params=_params(2))(h, da, db)


def _ffn_up_dx(name, da, db, wg, wu, layer, tm=1024):
    _, T, n = da.shape
    tm = min(tm, T)

    def body(da_ref, db_ref, wg_ref, wu_ref, o_ref):
        @pl.when(pl.program_id(1) == 0)
        def _():
            o_ref[...] = jnp.zeros_like(o_ref)
        o_ref[...] += _dot(da_ref[0], wg_ref[0], "nn") + _dot(db_ref[0], wu_ref[0], "nn")

    dspec = pl.BlockSpec((1, tm, n), lambda i, ch: (ch, i, 0))
    wspec = pl.BlockSpec((1, n, D), lambda i, ch: (ch, layer, 0))
    return pl.pallas_call(
        body, name=name, grid=(T // tm, 4), in_specs=[dspec, dspec, wspec, wspec],
        out_specs=pl.BlockSpec((tm, D), lambda i, ch: (i, 0)), out_shape=jax.ShapeDtypeStruct((T, D), F32),
        compiler_params=_params(2))(da, db, wg, wu)


def _shift_down(x, k):
    if k == 0:
        return x
    rows = lax.broadcasted_iota(jnp.int32, x.shape, 0)
    return jnp.where(rows >= k, pltpu.roll(x, k, 0), 0.0)


def _shift_up(x, k):
    if k == 0:
        return x
    T = x.shape[0]
    rows = lax.broadcasted_iota(jnp.int32, x.shape, 0)
    return jnp.where(rows < T - k, pltpu.roll(x, T - k, 0), 0.0)


def _conv_silu(x, w):
    c = w[0:1, :] * _shift_down(x, 3) + w[1:2, :] * _shift_down(x, 2) + w[2:3, :] * _shift_down(x, 1) + w[3:4, :] * x
    sg = _sig(c)
    return c, sg, c * sg


def _gdn_conv_fwd(name, proj, cw):
    T = proj.shape[0]

    def body(x_ref, w_ref, o_ref):
        j = pl.program_id(0)
        _, _, y = _conv_silu(x_ref[...], w_ref[...])
        r = lax.rsqrt(jnp.sum(y * y, axis=1, keepdims=True) + EPS)
        mult = jnp.where(j < NH, HD ** -0.5, 1.0)
        o_ref[...] = jnp.where(j < 2 * NH, y * (r * mult), y)

    return pl.pallas_call(body, name=name, grid=(3 * NH,),
                          in_specs=[pl.BlockSpec((T, HD), lambda j: (0, j)), pl.BlockSpec((4, HD), lambda j: (0, j))],
                          out_specs=pl.BlockSpec((T, HD), lambda j: (0, j)),
                          out_shape=jax.ShapeDtypeStruct((T, GDN_QKV), F32), compiler_params=_params(1))(proj, cw)


def _gdn_conv_bwd(name, proj, cw, dz):
    T = proj.shape[0]

    def body(x_ref, w_ref, dz_ref, dx_ref, dw_ref):
        j = pl.program_id(0)
        x, w, dz = x_ref[...], w_ref[...], dz_ref[...]
        c, sg, y = _conv_silu(x, w)
        r = lax.rsqrt(jnp.sum(y * y, axis=1, keepdims=True) + EPS)
        mult = jnp.where(j < NH, HD ** -0.5, 1.0)
        dyn = mult * (r * dz - (r * r * r) * y * jnp.sum(dz * y, axis=1, keepdims=True))
        dy = jnp.where(j < 2 * NH, dyn, dz)
        dc = dy * (sg * (1.0 + c * (1.0 - sg)))
        dx = w[0:1, :] * _shift_up(dc, 3) + w[1:2, :] * _shift_up(dc, 2) + w[2:3, :] * _shift_up(dc, 1) + w[3:4, :] * dc
        dx_ref[...] = dx.astype(dx_ref.dtype)
        for k in range(4):
            dw_ref[pl.ds(k, 1), :] = jnp.sum(dc * _shift_down(x, 3 - k), axis=0, keepdims=True)

    return pl.pallas_call(body, name=name, grid=(3 * NH,),
                          in_specs=[pl.BlockSpec((T, HD), lambda j: (0, j)), pl.BlockSpec((4, HD), lambda j: (0, j)),
                                    pl.BlockSpec((T, HD), lambda j: (0, j))],
                          out_specs=[pl.BlockSpec((T, HD), lambda j: (0, j)), pl.BlockSpec((4, HD), lambda j: (0, j))],
                          out_shape=[jax.ShapeDtypeStruct((T, GDN_QKV), BF16), jax.ShapeDtypeStruct((4, GDN_QKV), F32)],
                          compiler_params=_params(1))(proj, cw, dz)


def _softplus(z):
    return jnp.maximum(z, 0.0) + jnp.log(1.0 + jnp.exp(-jnp.abs(z)))


_AB_CB = GDN_INK // (2 * HD) - 1


def _gdn_gates_fwd(name, proj, alog, dtb):
    def fn(ab, alog, dtb):
        a, b = ab[:, :HD], ab[:, HD:]
        return -jnp.exp(alog) * _softplus(a + dtb), _sig(b)
    return _rowwise(name, fn, [(proj, 2 * HD, _AB_CB)], [alog, dtb], [(HD, F32), (HD, F32)])


def _gdn_gates_bwd(name, proj, dg_h, db_h, alog, dtb):
    def fn(ab, dg_h, db_h, alog, dtb):
        lane = lax.broadcasted_iota(jnp.int32, (1, HD), 1)
        dg = jnp.zeros(dg_h.shape[1:], F32)
        dbeta = jnp.zeros(dg_h.shape[1:], F32)
        for h in range(NH):
            oh = (lane == h).astype(F32)
            dg = dg + dg_h[h] * oh
            dbeta = dbeta + db_h[h] * oh
        a, b = ab[:, :HD], ab[:, HD:]
        z = a + dtb
        ea = jnp.exp(alog)
        beta = _sig(b)
        da = dg * (-ea) * _sig(z)
        db = dbeta * beta * (1.0 - beta)
        return (jnp.concatenate([da, db], axis=1), jnp.sum(dg * (-ea * _softplus(z)), axis=0, keepdims=True),
                jnp.sum(da, axis=0, keepdims=True))
    return _rowwise(name, fn, [(proj, 2 * HD, _AB_CB), dg_h, db_h], [alog, dtb], [(2 * HD, BF16)], sums=[HD, HD])


def _interleave(gens):
    gens = list(gens)
    results = [None] * len(gens)
    active = list(range(len(gens)))
    while active:
        for i in list(active):
            try:
                next(gens[i])
            except StopIteration as stop:
                results[i] = stop.value
                active.remove(i)
    return results


def _chunk_common(q, k, v, gblk, bblk, h, prec):
    C = CHUNK
    lane = lax.broadcasted_iota(jnp.int32, (1, HD), 1)
    oh = (lane == h).astype(F32)
    g_col = jnp.sum(gblk * oh, axis=1, keepdims=True)
    beta = jnp.sum(bblk * oh, axis=1, keepdims=True)
    ri = lax.broadcasted_iota(jnp.int32, (C, C), 0)
    ci = lax.broadcasted_iota(jnp.int32, (C, C), 1)
    incl = ri >= ci
    strict = ri > ci
    eye = (ri == ci).astype(F32)
    gcb = _dot(incl.astype(F32), jnp.broadcast_to(g_col, (C, HD)), "nn", HI)
    yield
    gc = gcb[:, :C]
    gc_row = _dot(jnp.ones((C, C), F32), eye * gc, "nn", HI)
    yield
    decay = jnp.where(incl, jnp.exp(jnp.where(incl, gc - gc_row, 0.0)), 0.0)
    rows = lax.broadcasted_iota(jnp.int32, (C, HD), 0)
    gclb = jnp.sum(jnp.where(rows == C - 1, gcb, 0.0), axis=0, keepdims=True)
    eg = jnp.exp(gcb)
    egl = jnp.exp(gclb - gcb)
    gl = jnp.exp(gclb)
    kb = k * beta
    m1 = _dot(kb, k, "nt", prec)
    qk = _dot(q, k, "nt", prec)
    yield
    L = jnp.where(strict, m1 * decay, 0.0)
    nl = -L
    tinv = eye + nl
    p = nl
    for _ in range(5):
        p = _dot(p, p, "nn", H3)
        yield
        tinv = tinv + _dot(tinv, p, "nn", H3)
    vb = v * beta
    kbg = kb * eg
    yield
    u = _dot(tinv, vb, "nn", prec)
    w = _dot(tinv, kbg, "nn", prec)
    yield
    attn = jnp.where(incl, qk * decay, 0.0)
    return dict(beta=beta, incl=incl, strict=strict, decay=decay, eg=eg, egl=egl, gl=gl, kb=kb, m1=m1, tinv=tinv,
                kbg=kbg, u=u, w=w, qk=qk, attn=attn, q_dec=q * eg, k_dec=k * egl, rows=rows, oh=oh)


def _gdn_chunk_fwd(name, qkv, g, beta):
    T = qkv.shape[0]
    N = T // CHUNK

    hb = _GDN_HB
    w = hb * HD

    def body(q_ref, k_ref, v_ref, g_ref, b_ref, o_ref, st_ref, S):
        hg, n = pl.program_id(0), pl.program_id(1)

        @pl.when(n == 0)
        def _():
            S[...] = jnp.zeros_like(S)

        gblk, bblk = g_ref[...], b_ref[...]

        def one_head(i, q, k, v, s):
            c = yield from _chunk_common(q, k, v, gblk, bblk, hg * hb + i, HF)
            v_new = c["u"] - _dot(c["w"], s, "nn", HF)
            qs = _dot(c["q_dec"], s, "nn", HF)
            yield
            o = qs + _dot(c["attn"], v_new, "nn", HF)
            return o, s * c["gl"] + _dot(c["k_dec"], v_new, "tn", HF)

        sls = [slice(i * HD, (i + 1) * HD) for i in range(hb)]
        states = [S[i] for i in range(hb)]
        res = _interleave(one_head(i, q_ref[:, sls[i]], k_ref[:, sls[i]], v_ref[:, sls[i]], states[i]) for i in range(hb))
        for i, (o, s_new) in enumerate(res):
            st_ref[i, 0] = states[i]
            o_ref[:, sls[i]] = o
            S[i] = s_new

    blk = lambda off: pl.BlockSpec((CHUNK, w), lambda h, n, off=off: (n, off + h))
    gspec = pl.BlockSpec((CHUNK, HD), lambda h, n: (n, 0))
    return pl.pallas_call(
        body, name=name, grid=(NH // hb, N), in_specs=[blk(0), blk(NH // hb), blk(2 * NH // hb), gspec, gspec],
        out_specs=[pl.BlockSpec((CHUNK, w), lambda h, n: (n, h)), pl.BlockSpec((hb, 1, HD, HD), lambda h, n: (h, n, 0, 0))],
        out_shape=[jax.ShapeDtypeStruct((T, NH * HD), F32), jax.ShapeDtypeStruct((NH, N, HD, HD), F32)],
        scratch_shapes=[pltpu.VMEM((hb, HD, HD), F32)], compiler_params=_params(2))(qkv, qkv, qkv, g, beta)


def _gdn_chunk_bwd(name, qkv, g, beta, states, do):
    T = qkv.shape[0]
    N = T // CHUNK
    C = CHUNK

    hb = _GDN_HB
    w = hb * HD

    def body(q_ref, k_ref, v_ref, g_ref, b_ref, st_ref, do_ref, dq_ref, dk_ref, dv_ref, dg_ref, db_ref, dS):
        hg, n = pl.program_id(0), pl.program_id(1)

        @pl.when(n == 0)
        def _():
            dS[...] = jnp.zeros_like(dS)

        gblk, bblk = g_ref[...], b_ref[...]
        sls = [slice(i * HD, (i + 1) * HD) for i in range(hb)]
        res = _interleave(one_head(hg * hb + i, gblk, bblk, q_ref[:, sls[i]], k_ref[:, sls[i]], v_ref[:, sls[i]],
                                   st_ref[i, 0], do_ref[:, sls[i]], dS[i]) for i in range(hb))
        for i, (dq, dk, dv, dg, db, ds_new) in enumerate(res):
            dq_ref[:, sls[i]] = dq
            dk_ref[:, sls[i]] = dk
            dv_ref[:, sls[i]] = dv
            dg_ref[i] = dg
            db_ref[i] = db
            dS[i] = ds_new

    def one_head(h, gblk, bblk, q, k, v, s, do, ds):
        c = yield from _chunk_common(q, k, v, gblk, bblk, h, HF)
        eg, egl, gl, beta, decay, tinv = c["eg"], c["egl"], c["gl"], c["beta"], c["decay"], c["tinv"]
        v_new = c["u"] - _dot(c["w"], s, "nn", HF)
        dq_dec = _dot(do, s, "nt", HF)
        yield
        dv_new = _dot(c["attn"], do, "tn", HF) + _dot(c["k_dec"], ds, "nn", HF)
        dk_dec = _dot(v_new, ds, "nt", HF)
        dgl = jnp.sum(jnp.sum(s * ds, axis=1, keepdims=True), axis=0, keepdims=True)
        yield
        ds_new = ds * gl + _dot(c["q_dec"], do, "tn", HF) - _dot(c["w"], dv_new, "tn", HF)
        dattn = jnp.where(c["incl"], _dot(do, v_new, "nt", HF), 0.0)
        dw = -_dot(dv_new, s, "nt", HF)
        yield
        dvb = _dot(tinv, dv_new, "tn", HS)
        dkbg = _dot(tinv, dw, "tn", HS)
        yield
        dA = -(_dot(dvb, c["u"], "nt", HS) + _dot(dkbg, c["w"], "nt", HS))
        yield
        dL = jnp.where(c["strict"], dA, 0.0)
        dm1 = dL * decay
        dqk = dattn * decay
        xdec = (dL * c["m1"] + dattn * c["qk"]) * decay
        dkb = _dot(dm1, k, "nn", HS) + dkbg * eg
        dk = _dot(dm1, c["kb"], "tn", HS) + _dot(dqk, q, "tn", HS) + dk_dec * egl + dkb * beta
        dq = _dot(dqk, k, "nn", HS) + dq_dec * eg
        yield
        dkd_kd = jnp.sum(dk_dec * c["k_dec"], axis=1, keepdims=True)
        dgc = (jnp.sum(xdec, axis=1, keepdims=True) - _dot(xdec, jnp.ones((C, HD), F32), "tn", HS)
               + jnp.sum(dq_dec * c["q_dec"], axis=1, keepdims=True) - dkd_kd
               + jnp.sum(dkbg * c["kbg"], axis=1, keepdims=True))
        dgcl = jnp.sum(dkd_kd, axis=0, keepdims=True) + dgl * gl
        dgc = dgc + jnp.where(c["rows"] == C - 1, dgcl, 0.0)
        ri = lax.broadcasted_iota(jnp.int32, (C, C), 0)
        ci = lax.broadcasted_iota(jnp.int32, (C, C), 1)
        dg = _dot((ci >= ri).astype(F32), dgc, "nn", HI)
        db = jnp.broadcast_to(jnp.sum(dkb * k, axis=1, keepdims=True) + jnp.sum(dvb * v, axis=1, keepdims=True), (C, HD))
        return dq, dk, dvb * beta, dg, db, ds_new

    blk = lambda off: pl.BlockSpec((C, w), lambda h, n, off=off: (N - 1 - n, off + h))
    gspec = pl.BlockSpec((C, HD), lambda h, n: (N - 1 - n, 0))
    ospec = pl.BlockSpec((C, w), lambda h, n: (N - 1 - n, h))
    hspec = pl.BlockSpec((hb, C, HD), lambda h, n: (h, N - 1 - n, 0))
    return pl.pallas_call(
        body, name=name, grid=(NH // hb, N),
        in_specs=[blk(0), blk(NH // hb), blk(2 * NH // hb), gspec, gspec,
                  pl.BlockSpec((hb, 1, HD, HD), lambda h, n: (h, N - 1 - n, 0, 0)), ospec],
        out_specs=[ospec, ospec, ospec, hspec, hspec],
        out_shape=[jax.ShapeDtypeStruct((T, NH * HD), F32)] * 3 + [jax.ShapeDtypeStruct((NH, T, HD), F32)] * 2,
        scratch_shapes=[pltpu.VMEM((hb, HD, HD), F32)], compiler_params=_params(2))(qkv, qkv, qkv, g, beta, states, do)


_GATE_CB = GDN_QKV // (NH * HD)


def _gdn_gated_norm_fwd(name, o, proj, ng):
    def fn(o, gate, ng):
        outs = []
        for h in range(NH):
            sl = slice(h * HD, (h + 1) * HD)
            oh, gh = o[:, sl], gate[:, sl]
            outs.append(oh * _rms(oh) * ng * (gh * _sig(gh)))
        return jnp.concatenate(outs, axis=1)
    return _rowwise(name, fn, [o, (proj, NH * HD, _GATE_CB)], [ng], [(NH * HD, BF16)])[0]


def _gdn_gated_norm_bwd(name, don, o, proj, ng):
    def fn(don, o, gate, ng):
        dos, dgs = [], []
        dng = jnp.zeros((1, HD), F32)
        for h in range(NH):
            sl = slice(h * HD, (h + 1) * HD)
            oh, gh, dh = o[:, sl], gate[:, sl], don[:, sl]
            r = _rms(oh)
            xh = oh * r
            sg = _sig(gh)
            dn = dh * (gh * sg)
            dgs.append(dh * (xh * ng) * (sg * (1.0 + gh * (1.0 - sg))))
            dng = dng + jnp.sum(dn * xh, axis=0, keepdims=True)
            dxh = dn * ng
            dos.append(r * (dxh - xh * jnp.mean(dxh * xh, axis=-1, keepdims=True)))
        return jnp.concatenate(dos, axis=1), jnp.concatenate(dgs, axis=1), dng
    return _rowwise(name, fn, [don, o, (proj, NH * HD, _GATE_CB)], [ng], [(NH * HD, F32), (NH * HD, BF16)], sums=[HD])


def _rot(x):
    lane = lax.broadcasted_iota(jnp.int32, x.shape, 1)
    return jnp.where(lane < ROPE // 2, -pltpu.roll(x, HD - ROPE // 2, 1), pltpu.roll(x, ROPE // 2, 1))


def _rot_t(x):
    lane = lax.broadcasted_iota(jnp.int32, x.shape, 1)
    return jnp.where(lane < ROPE // 2, pltpu.roll(x, HD - ROPE // 2, 1), -pltpu.roll(x, ROPE // 2, 1))


def _rope_tables(pos_col):
    lane = jnp.arange(HD)
    inv_freq = ROPE_THETA ** (-(2.0 * (lane % (ROPE // 2)).astype(F32)) / ROPE)
    inv_freq = jnp.where(lane < ROPE, inv_freq, 0.0).astype(F32)[None, :]
    valid = (lane < ROPE).astype(F32)[None, :]

    def fn(pos, inv_freq, valid):
        ang = pos.astype(F32) * inv_freq
        return jnp.cos(ang) * valid, jnp.sin(ang) * valid
    return _rowwise("rope_tables", fn, [pos_col], [inv_freq, valid], [(HD, F32), (HD, F32)])


def _mla_pre_fwd(name, proj, cos, sin, qg, kvg):
    def fn(p, cos, sin, qg, kvg):
        cq, ckv, kr = p[:, :Q_RANK], p[:, Q_RANK:Q_RANK + KV_RANK], p[:, Q_RANK + KV_RANK:]
        return cq * _rms(cq) * qg, ckv * _rms(ckv) * kvg, kr * cos + _rot(kr) * sin
    return _rowwise(name, fn, [proj, cos, sin], [qg, kvg], [(Q_RANK, BF16), (KV_RANK, BF16), (HD, BF16)])


def _rms_bwd(dy, x, g):
    r = _rms(x)
    xh = x * r
    dxh = dy * g
    return r * (dxh - xh * jnp.mean(dxh * xh, axis=-1, keepdims=True)), jnp.sum(dy * xh, axis=0, keepdims=True)


def _mla_pre_bwd(name, proj, dcqn, dckvn, dkr, cos, sin, qg, kvg):
    def fn(p, dcqn, dckvn, dkr, cos, sin, qg, kvg):
        cq, ckv = p[:, :Q_RANK], p[:, Q_RANK:Q_RANK + KV_RANK]
        dcq, dqg = _rms_bwd(dcqn, cq, qg)
        dckv, dkvg = _rms_bwd(dckvn, ckv, kvg)
        dkr_pre = dkr * cos + _rot_t(dkr * sin)
        return jnp.concatenate([dcq, dckv, dkr_pre], axis=1), dqg, dkvg
    return _rowwise(name, fn, [proj, dcqn, dckvn, dkr, cos, sin], [qg, kvg], [(MLA_INK, BF16)], sums=[Q_RANK, KV_RANK])


def _mla_q_fwd(name, q, cos, sin):
    def fn(qn, qr, cos, sin):
        outs = []
        for h in range(NH):
            x = qr[:, h * HD:(h + 1) * HD]
            outs.append(x * cos + _rot(x) * sin)
        return qn, jnp.concatenate(outs, axis=1)
    return _rowwise(name, fn, [(q, NH * HD, 0), (q, NH * HD, 1), cos, sin], [], [(NH * HD, BF16), (NH * HD, BF16)])


def _mla_q_bwd(name, dqn, dqr, cos, sin):
    def fn(dqn, dqr, cos, sin):
        outs = [dqn]
        for h in range(NH):
            z = dqr[:, h * HD:(h + 1) * HD]
            outs.append(z * cos + _rot_t(z * sin))
        return jnp.concatenate(outs, axis=1)
    return _rowwise(name, fn, [dqn, dqr, cos, sin], [], [(2 * NH * HD, BF16)])[0]


def _att_probs(qn, qr, kn, kr, row0):
    s = (_dot(qn, kn, "nt") + _dot(qr, kr, "nt")) * ATT_SCALE
    qpos = row0 + lax.broadcasted_iota(jnp.int32, s.shape, 0)
    kpos = lax.broadcasted_iota(jnp.int32, s.shape, 1)
    s = jnp.where(kpos <= qpos, s, -1e30)
    p = jnp.exp(s - jnp.max(s, axis=1, keepdims=True))
    return p / jnp.sum(p, axis=1, keepdims=True)


def _mla_attn_fwd(name, qn, qr, kv, kr, tq=256):
    T = qn.shape[0]
    tq = min(tq, T)

    def body(qn_ref, qr_ref, kn_ref, v_ref, kr_ref, o_ref):
        i = pl.program_id(1)
        for blk in range(T // tq):
            @pl.when(i == blk)
            def _(blk=blk):
                keys = pl.ds(0, (blk + 1) * tq)
                p = _att_probs(qn_ref[...], qr_ref[...], kn_ref[keys, :], kr_ref[keys, :], blk * tq)
                o_ref[...] = _dot(p.astype(BF16), v_ref[keys, :], "nn").astype(o_ref.dtype)

    qspec = pl.BlockSpec((tq, HD), lambda h, i: (i, h))
    return pl.pallas_call(
        body, name=name, grid=(NH, T // tq),
        in_specs=[qspec, qspec, pl.BlockSpec((T, HD), lambda h, i: (0, h)), pl.BlockSpec((T, HD), lambda h, i: (0, NH + h)),
                  pl.BlockSpec((T, HD), lambda h, i: (0, 0))],
        out_specs=qspec, out_shape=jax.ShapeDtypeStruct((T, NH * HD), BF16), compiler_params=_params(2))(qn, qr, kv, kv, kr)


def _mla_attn_bwd(name, qn, qr, kv, kr, do, tq=256):
    T = qn.shape[0]
    tq = min(tq, T)

    def body(qn_ref, qr_ref, kn_ref, v_ref, kr_ref, do_ref, dqn_ref, dqr_ref, dkn_ref, dv_ref, dkr_ref):
        h, i = pl.program_id(0), pl.program_id(1)

        @pl.when(i == 0)
        def _():
            dkn_ref[...] = jnp.zeros_like(dkn_ref)
            dv_ref[...] = jnp.zeros_like(dv_ref)

        @pl.when((i == 0) & (h == 0))
        def _():
            dkr_ref[...] = jnp.zeros_like(dkr_ref)

        for blk in range(T // tq):
            @pl.when(i == blk)
            def _(blk=blk):
                keys = pl.ds(0, (blk + 1) * tq)
                qn, qr, do = qn_ref[...], qr_ref[...], do_ref[...]
                kn, kr, v = kn_ref[keys, :], kr_ref[keys, :], v_ref[keys, :]
                p = _att_probs(qn, qr, kn, kr, blk * tq)
                dp = _dot(do, v, "nt")
                ds = (p * (dp - jnp.sum(p * dp, axis=1, keepdims=True)) * ATT_SCALE).astype(BF16)
                dqn_ref[...] = _dot(ds, kn, "nn")
                dqr_ref[...] = _dot(ds, kr, "nn")
                dkn_ref[keys, :] += _dot(ds, qn, "tn")
                dkr_ref[keys, :] += _dot(ds, qr, "tn")
                dv_ref[keys, :] += _dot(p.astype(BF16), do, "tn")

    qspec = pl.BlockSpec((tq, HD), lambda h, i: (i, h))
    kspec = pl.BlockSpec((T, HD), lambda h, i: (0, h))
    return pl.pallas_call(
        body, name=name, grid=(NH, T // tq),
        in_specs=[qspec, qspec, kspec, pl.BlockSpec((T, HD), lambda h, i: (0, NH + h)),
                  pl.BlockSpec((T, HD), lambda h, i: (0, 0)), qspec],
        out_specs=[qspec, qspec, kspec, kspec, pl.BlockSpec((T, HD), lambda h, i: (0, 0))],
        out_shape=[jax.ShapeDtypeStruct((T, NH * HD), F32)] * 4 + [jax.ShapeDtypeStruct((T, HD), F32)],
        compiler_params=_params(2))(qn, qr, kv, kv, kr, do)


def _mod_rows(mod, layer):
    return [(mod, layer, i, D) for i in range(N_MOD)]


def _local_step(x, target, pos_col, mod, weights_of, P, on_grads):
    cos, sin = _rope_tables(pos_col)
    saved = []
    for l in range(DEPTH):
        j = l // 2
        sh_m, sc_m, ga_m, sh_f, sc_f, ga_f = _mod_rows(mod, l)
        s = dict(x0=x)
        h = _norm_mod_fwd(f"norm_mix{l}", x, (P["norm_mix_g"], l, 0, D), sc_m, sh_m)
        W = weights_of(l, h)
        s.update(h=h, W=W)
        if l % 2 == 0:
            proj = _mm(f"gdn_in{j}", h, W["gdn_in"], "nn", tn=GDN_INK // 2)
            qkv = _gdn_conv_fwd(f"gdn_conv{j}", proj, P["gdn_cw"][j])
            g, beta = _gdn_gates_fwd(f"gdn_gates{j}", proj, P["gdn_alog"][j], P["gdn_dtb"][j])
            o, states = _gdn_chunk_fwd(f"gdn_chunk{j}", qkv, g, beta)
            on = _gdn_gated_norm_fwd(f"gdn_gnorm{j}", o, proj, P["gdn_ng"][j])
            y = _mm(f"gdn_out{j}", on, W["gdn_out"], "nn")
            s.update(proj=proj, qkv=qkv, g=g, beta=beta, o=o, states=states, on=on)
        else:
            proj = _mm(f"mla_in{j}", h, W["mla_in"], "nn")
            cqn, ckvn, kr = _mla_pre_fwd(f"mla_pre{j}", proj, cos, sin, P["mla_qg"][j], P["mla_kvg"][j])
            q = _mm(f"mla_uq{j}", cqn, W["mla_uq"], "nn")
            kv = _mm(f"mla_ukv{j}", ckvn, W["mla_ukv"], "nn", out_dtype=BF16)
            qn, qr = _mla_q_fwd(f"mla_q{j}", q, cos, sin)
            o = _mla_attn_fwd(f"mla_attn{j}", qn, qr, kv, kr)
            y = _mm(f"mla_out{j}", o, W["mla_out"], "nn")
            s.update(proj=proj, cqn=cqn, ckvn=ckvn, kr=kr, kv=kv, qn=qn, qr=qr, o=o)
        s["y"] = y
        x = _residual_fwd(f"res_mix{l}", x, y, ga_m)
        s["x1"] = x
        h2 = _norm_mod_fwd(f"norm_ffn{l}", x, (P["norm_ffn_g"], l, 0, D), sc_f, sh_f)
        fa, fb, sw = _ffn_up(f"ffn_up{l}", h2, W["ffn_g"], W["ffn_u"], 0)
        yf = _ffn_down(f"ffn_down{l}", sw, W["ffn_d"], 0)
        x = _residual_fwd(f"res_ffn{l}", x, yf, ga_f)
        s.update(h2=h2, fa=fa, fb=fb, sw=sw, yf=yf)
        saved.append(s)

    dx, loss, d_final = _loss_head(x, target, P["final_g"])
    gP = dict(loss=loss, final_g=d_final, norm_mix_g=[None] * DEPTH, norm_ffn_g=[None] * DEPTH,
              gdn_cw=[None] * 2, gdn_alog=[None] * 2, gdn_dtb=[None] * 2, gdn_ng=[None] * 2,
              mla_qg=[None] * 2, mla_kvg=[None] * 2)
    dmod = [None] * DEPTH
    for l in reversed(range(DEPTH)):
        j = l // 2
        s = saved[l]
        W = s["W"]
        sh_m, sc_m, ga_m, sh_f, sc_f, ga_f = _mod_rows(mod, l)
        dyf, d_ga_f = _residual_bwd(f"res_ffn_b{l}", dx, s["yf"], ga_f)
        da, db = _ffn_down_bwd(f"ffn_down_dx{l}", dyf, W["ffn_d"], s["fa"], s["fb"], 0)
        g_down = _ffn_down_dw(f"ffn_down_dw{l}", s["sw"], dyf)
        g_gate, g_up = _ffn_up_dw(f"ffn_up_dw{l}", s["h2"], da, db)
        on_grads(l, "ffn", dict(ffn_w_gate=g_gate, ffn_w_up=g_up, ffn_w_down=g_down))
        dh2 = _ffn_up_dx(f"ffn_up_dx{l}", da, db, W["ffn_g"], W["ffn_u"], 0)
        dx, d_sh_f, d_sc_f, gP["norm_ffn_g"][l] = _norm_mod_bwd(f"norm_ffn_b{l}", dh2, s["x1"], dx,
                                                                 (P["norm_ffn_g"], l, 0, D), sc_f)
        dy, d_ga_m = _residual_bwd(f"res_mix_b{l}", dx, s["y"], ga_m)
        if l % 2 == 0:
            don = _mm(f"gdn_out_dx{j}", dy, W["gdn_out"], "nt")
            g_out = _mm(f"gdn_out_dw{j}", s["on"], dy, "tn", out_dtype=BF16)
            do, dgate, gP["gdn_ng"][j] = _gdn_gated_norm_bwd(f"gdn_gnorm_b{j}", don, s["o"], s["proj"], P["gdn_ng"][j])
            dq, dk, dv, dg_h, db_h = _gdn_chunk_bwd(f"gdn_chunk_b{j}", s["qkv"], s["g"], s["beta"], s["states"], do)
            dab_, gP["gdn_alog"][j], gP["gdn_dtb"][j] = _gdn_gates_bwd(f"gdn_gates_b{j}", s["proj"], dg_h, db_h,
                                                                        P["gdn_alog"][j], P["gdn_dtb"][j])
            dpre, gP["gdn_cw"][j] = _gdn_conv_bwd(f"gdn_conv_b{j}", s["proj"], P["gdn_cw"][j],
                                                  jnp.concatenate([dq, dk, dv], axis=1))
            dproj = jnp.concatenate([dpre, dgate, dab_], axis=1)
            g_in = _mm(f"gdn_in_dw{j}", s["h"], dproj, "tn", out_dtype=BF16, tn=GDN_INK // 2)
            on_grads(l, "mix", dict(gdn_w_in=_uncols(_gdn_in_from_kernel(g_in)), gdn_w_out=_unrows(g_out)))
            dh = _mm(f"gdn_in_dx{j}", dproj, W["gdn_in"], "nt")
        else:
            do = _mm(f"mla_out_dx{j}", dy, W["mla_out"], "nt", out_dtype=BF16)
            g_out = _mm(f"mla_out_dw{j}", s["o"], dy, "tn", out_dtype=BF16)
            dqn, dqr, dkn, dv, dkr = _mla_attn_bwd(f"mla_attn_b{j}", s["qn"], s["qr"], s["kv"], s["kr"], do)
            dq = _mla_q_bwd(f"mla_q_b{j}", dqn, dqr, cos, sin)
            dkv = jnp.concatenate([dkn, dv], axis=1)
            g_uq = _mm(f"mla_uq_dw{j}", s["cqn"], dq, "tn", out_dtype=BF16)
            dcqn = _mm(f"mla_uq_dx{j}", dq, W["mla_uq"], "nt")
            g_ukv = _mm(f"mla_ukv_dw{j}", s["ckvn"], dkv, "tn", out_dtype=BF16)
            dckvn = _mm(f"mla_ukv_dx{j}", dkv, W["mla_ukv"], "nt")
            dproj, gP["mla_qg"][j], gP["mla_kvg"][j] = _mla_pre_bwd(f"mla_pre_b{j}", s["proj"], dcqn, dckvn, dkr, cos, sin,
                                                                     P["mla_qg"][j], P["mla_kvg"][j])
            g_in = _mm(f"mla_in_dw{j}", s["h"], dproj, "tn", out_dtype=BF16)
            on_grads(l, "mix", dict(mla_w_in=_unrows(g_in[:, :Q_RANK + KV_RANK + ROPE]), mla_w_uq=_uncols(_mla_uq_from_kernel(g_uq)),
                                    mla_w_ukv=_uncols(_mla_ukv_from_kernel(g_ukv)), mla_w_out=_unrows(g_out)))
            dh = _mm(f"mla_in_dx{j}", dproj, W["mla_in"], "nt")
        dx, d_sh_m, d_sc_m, gP["norm_mix_g"][l] = _norm_mod_bwd(f"norm_mix_b{l}", dh, s["x0"], dx,
                                                                 (P["norm_mix_g"], l, 0, D), sc_m)
        dmod[l] = jnp.concatenate([d_sh_m, d_sc_m, d_ga_m, d_sh_f, d_sc_f, d_ga_f], axis=1)
    return dx, jnp.concatenate(dmod, axis=0), gP


def _pad_cols(a, width):
    return jnp.pad(a, ((0, 0), (0, width - a.shape[1])))


def _gdn_in_to_kernel(w):
    m = GDN_QKV + NH * HD
    return jnp.concatenate([w[:, :m], _pad_cols(w[:, m:m + NH], HD), _pad_cols(w[:, m + NH:], HD)], axis=1)


def _gdn_in_from_kernel(g):
    m = GDN_QKV + NH * HD
    return jnp.concatenate([g[:, :m], g[:, m:m + NH], g[:, m + HD:m + HD + NH]], axis=1)


def _mla_uq_to_kernel(w):
    w3 = w.reshape(Q_RANK, NH, HD + ROPE)
    rope = jnp.pad(w3[:, :, HD:], ((0, 0), (0, 0), (0, HD - ROPE)))
    return jnp.concatenate([w3[:, :, :HD].reshape(Q_RANK, NH * HD), rope.reshape(Q_RANK, NH * HD)], axis=1)


def _mla_uq_from_kernel(g):
    gn = g[:, :NH * HD].reshape(Q_RANK, NH, HD)
    gr = g[:, NH * HD:].reshape(Q_RANK, NH, HD)[:, :, :ROPE]
    return jnp.concatenate([gn, gr], axis=2).reshape(Q_RANK, NH * (HD + ROPE))


def _mla_ukv_to_kernel(w):
    w3 = w.reshape(KV_RANK, NH, 2 * HD)
    return jnp.concatenate([w3[:, :, :HD].reshape(KV_RANK, NH * HD), w3[:, :, HD:].reshape(KV_RANK, NH * HD)], axis=1)


def _mla_ukv_from_kernel(g):
    gk = g[:, :NH * HD].reshape(KV_RANK, NH, HD)
    gv = g[:, NH * HD:].reshape(KV_RANK, NH, HD)
    return jnp.concatenate([gk, gv], axis=2).reshape(KV_RANK, NH * 2 * HD)


def _cols(t):
    return jnp.moveaxis(t, 0, 1).reshape(t.shape[1], -1)


def _uncols(g):
    return jnp.moveaxis(g.reshape(g.shape[0], 4, -1), 1, 0)


def _rows(t):
    return t.reshape(-1, t.shape[2])


def _unrows(g):
    return g.reshape(4, -1, g.shape[1])


def _layer_weights(layer):
    mixer = ("gdn_w_in", "gdn_w_out") if layer % 2 == 0 else ("mla_w_in", "mla_w_uq", "mla_w_ukv", "mla_w_out")
    return [(n, layer // 2) for n in mixer] + [(n, layer) for n in ("ffn_w_gate", "ffn_w_up", "ffn_w_down")]


def _weights_to_kernel(layer, g):
    out = dict(ffn_g=g["ffn_w_gate"], ffn_u=g["ffn_w_up"], ffn_d=g["ffn_w_down"])
    if layer % 2 == 0:
        out.update(gdn_in=_gdn_in_to_kernel(_cols(g["gdn_w_in"])), gdn_out=_rows(g["gdn_w_out"]))
    else:
        out.update(mla_in=_pad_cols(_rows(g["mla_w_in"]), MLA_INK), mla_uq=_mla_uq_to_kernel(_cols(g["mla_w_uq"])),
                   mla_ukv=_mla_ukv_to_kernel(_cols(g["mla_w_ukv"])), mla_out=_rows(g["mla_w_out"]))
    return out


def _small_to_kernel(norm_mix_g, norm_ffn_g, final_norm_g, gdn_conv_w, gdn_a_log, gdn_dt_bias, gdn_norm_g, q_norm_g, kv_norm_g):
    return dict(
        norm_mix_g=norm_mix_g, norm_ffn_g=norm_ffn_g, final_g=final_norm_g.reshape(1, D),
        gdn_cw=[jnp.transpose(gdn_conv_w[j]) for j in range(2)],
        gdn_alog=[_pad_cols(gdn_a_log[j:j + 1], HD) for j in range(2)],
        gdn_dtb=[_pad_cols(gdn_dt_bias[j:j + 1], HD) for j in range(2)],
        gdn_ng=[gdn_norm_g[j:j + 1] for j in range(2)],
        mla_qg=[q_norm_g[j:j + 1] for j in range(2)],
        mla_kvg=[kv_norm_g[j:j + 1] for j in range(2)],
    )


_CHIP_FLIPS = ((1, 0), (0, 1), (1, 1))
_ANY = pl.BlockSpec(memory_space=pl.ANY)


def _me():
    return lax.axis_index("x"), lax.axis_index("y"), lax.axis_index("c")


def _chip_peer(dx, dy):
    x, y, c = _me()
    return ((1 - x) if dx else x, (1 - y) if dy else y, c)


def _rcopy(src, dst, send_sem, recv_sem, to):
    return pltpu.make_async_remote_copy(src_ref=src, dst_ref=dst, send_sem=send_sem, recv_sem=recv_sem,
                                        device_id=to, device_id_type=MESH)


def _allgather4(name, a, halves=False):
    R, C = a.shape
    rh = R // 2 if halves else R

    def body(a_ref, out_ref, send_sems, recv_sems, local_sem):
        x, y, c = _me()
        me = 2 * x + y
        src = a_ref.at[pl.ds(c * rh, rh)] if halves else a_ref
        local = pltpu.make_async_copy(src, out_ref.at[me], local_sem)
        local.start()
        sends = []
        for k, (dx, dy) in enumerate(_CHIP_FLIPS):
            cp = _rcopy(src, out_ref.at[me], send_sems.at[k], recv_sems.at[k], _chip_peer(dx, dy))
            cp.start()
            sends.append(cp)
        for k, (dx, dy) in enumerate(_CHIP_FLIPS):
            px, py, _ = _chip_peer(dx, dy)
            _rcopy(src, out_ref.at[2 * px + py], send_sems.at[k], recv_sems.at[k], _chip_peer(dx, dy)).wait_recv()
        for cp in sends:
            cp.wait_send()
        local.wait()

    return pl.pallas_call(
        body, name=name, in_specs=[_ANY], out_specs=_ANY, out_shape=jax.ShapeDtypeStruct((4, rh, C), a.dtype),
        scratch_shapes=[pltpu.SemaphoreType.DMA((3,)), pltpu.SemaphoreType.DMA((3,)), pltpu.SemaphoreType.DMA(())])(a)


_NCH = 4


def _dma_sems(*counts):
    return [pltpu.SemaphoreType.DMA((n,)) for n in counts]


def _slot_tile(rows, cap=512):
    best = rows
    for tr in range(16, min(rows, cap) + 1, 16):
        if rows % tr == 0:
            best = tr
    return best


def _cast_into_slot(name, a, chip, row0, rows):
    C = a.shape[1]
    tr = _slot_tile(rows)
    assert row0 % tr == 0
    first = row0 // tr

    def body(c_ref, a_ref, o_ref):
        o_ref[0] = a_ref[...].astype(o_ref.dtype)

    grid_spec = pltpu.PrefetchScalarGridSpec(
        num_scalar_prefetch=1, grid=(rows // tr,), in_specs=[pl.BlockSpec((tr, C), lambda i, c_ref: (first + i, 0))],
        out_specs=pl.BlockSpec((1, tr, C), lambda i, c_ref: (c_ref[0], i, 0)))
    return pl.pallas_call(body, name=name, grid_spec=grid_spec, out_shape=jax.ShapeDtypeStruct((4, rows, C), BF16),
                          compiler_params=_params(1))(chip, a)


def _chunks(rows, align):
    for nch in (_NCH, 2):
        if rows % (nch * align) == 0:
            return nch
    return 1


def _gather_exchange(out, ici_s, ici_r, d2d_s, d2d_r):
    n = len(out)
    x, y, c = _me()
    me = 2 * x + y
    sib = (x, y, 1 - c)
    peers = [_chip_peer(dx, dy) for dx, dy in _CHIP_FLIPS]
    for t in range(n):
        h = out[t].shape[1] // 2
        nch = _chunks(h, 16)
        ch = h // nch
        for k, peer in enumerate(peers):
            for i in range(nch):
                blk = out[t].at[me, pl.ds(c * h + i * ch, ch)]
                _rcopy(blk, blk, ici_s.at[3 * t + k], ici_r.at[3 * t + k], peer).start()
    for t in range(n):
        h = out[t].shape[1] // 2
        nch = _chunks(h, 16)
        ch = h // nch
        for k, peer in enumerate(peers):
            pchip = 2 * peer[0] + peer[1]
            got = out[t].at[pchip, pl.ds(c * h, h)]
            _rcopy(got, got, ici_s.at[3 * t + k], ici_r.at[3 * t + k], peer).wait_recv()
            for i in range(nch):
                blk = out[t].at[pchip, pl.ds(c * h + i * ch, ch)]
                _rcopy(blk, blk, d2d_s.at[3 * t + k], d2d_r.at[3 * t + k], sib).start()
    for t in range(n):
        h = out[t].shape[1] // 2
        for k, peer in enumerate(peers):
            pchip = 2 * peer[0] + peer[1]
            other = out[t].at[pchip, pl.ds((1 - c) * h, h)]
            _rcopy(other, other, d2d_s.at[3 * t + k], d2d_r.at[3 * t + k], sib).wait_recv()
            _rcopy(other, other, ici_s.at[3 * t + k], ici_r.at[3 * t + k], peer).wait_send()
            _rcopy(other, other, d2d_s.at[3 * t + k], d2d_r.at[3 * t + k], sib).wait_send()


def _gather_weights(name, bufs):
    n = len(bufs)

    def body(*refs):
        _gather_exchange(refs[n:2 * n], *refs[2 * n:])

    return pl.pallas_call(
        body, name=name, in_specs=[_ANY] * n, out_specs=[_ANY] * n,
        out_shape=[jax.ShapeDtypeStruct(s.shape, s.dtype) for s in bufs],
        input_output_aliases={t: t for t in range(n)},
        scratch_shapes=_dma_sems(3 * n, 3 * n, 3 * n, 3 * n))(*bufs)


def _gather_weights_async(name, collective_id, bufs):
    n = len(bufs)
    refs = [jax.new_ref(b, memory_space=pltpu.MemorySpace.HBM) for b in bufs]

    @pl.kernel(mesh=plsc.ScalarSubcoreMesh(axis_name="sequencer", num_cores=1), name=name,
               scratch_types=tuple(_dma_sems(3 * n, 3 * n, 3 * n, 3 * n)),
               compiler_params=pltpu.CompilerParams(collective_id=collective_id))
    def launch(ici_s, ici_r, d2d_s, d2d_r):
        x, y, c = _me()
        barrier = pltpu.get_barrier_semaphore()
        for peer in [_chip_peer(dx, dy) for dx, dy in _CHIP_FLIPS] + [(x, y, 1 - c)]:
            pl.semaphore_signal(barrier, inc=1, device_id=peer, device_id_type=MESH)
        pl.semaphore_wait(barrier, 4)
        _gather_exchange(refs, ici_s, ici_r, d2d_s, d2d_r)

    launch()
    return [r[...] for r in refs]


def _rs_split(name, grads):
    n = len(grads)

    def body(*refs):
        g, out = refs[:n], refs[n:2 * n]
        send, recv = refs[2 * n:]
        x, y, c = _me()
        sib = (x, y, 1 - c)
        for t in range(n):
            h = g[t].shape[1] // 2
            for d in range(4):
                _rcopy(g[t].at[d, pl.ds((1 - c) * h, h)], out[t].at[d], send.at[t], recv.at[t], sib).start()
        for t in range(n):
            _rcopy(out[t], out[t], send.at[t], recv.at[t], sib).wait()

    return pl.pallas_call(
        body, name=name, in_specs=[_ANY] * n, out_specs=[_ANY] * n,
        out_shape=[jax.ShapeDtypeStruct((4, s.shape[1] // 2, s.shape[2]), s.dtype) for s in grads],
        scratch_shapes=_dma_sems(n, n))(*grads)


def _pair_add(name, g, theirs, core_chip):
    _, R, C = g.shape
    h = R // 2
    tr = _slot_tile(h)
    nb = h // tr

    def body(s_ref, g_ref, t_ref, p_ref, o_ref):
        val = (g_ref[...].astype(F32) + t_ref[...].astype(F32)).astype(p_ref.dtype)
        p_ref[...] = val

        @pl.when(pl.program_id(1) == s_ref[1])
        def _():
            o_ref[...] = val

    spec = pl.BlockSpec((1, tr, C), lambda i, d, s_ref: (d, i, 0))
    grid_spec = pltpu.PrefetchScalarGridSpec(
        num_scalar_prefetch=1, grid=(nb, 4),
        in_specs=[pl.BlockSpec((1, tr, C), lambda i, d, s_ref: (d, s_ref[0] * nb + i, 0)), spec],
        out_specs=[spec, pl.BlockSpec((1, tr, C), lambda i, d, s_ref: (s_ref[1], i, 0))])
    half = jax.ShapeDtypeStruct((4, h, C), BF16)
    return pl.pallas_call(body, name=name, grid_spec=grid_spec, out_shape=[half, half],
                          compiler_params=_params(2))(core_chip, g, theirs)


def _rs_alltoall_async(name, collective_id, parts, bufs):
    n = len(parts)
    p = [jax.new_ref(a, memory_space=pltpu.MemorySpace.HBM) for a in parts]
    out = [jax.new_ref(b, memory_space=pltpu.MemorySpace.HBM) for b in bufs]

    @pl.kernel(mesh=plsc.ScalarSubcoreMesh(axis_name="sequencer", num_cores=1), name=name,
               scratch_types=tuple(_dma_sems(3 * n, 3 * n)),
               compiler_params=pltpu.CompilerParams(collective_id=collective_id))
    def launch(send, recv):
        barrier = pltpu.get_barrier_semaphore()
        for peer in [_chip_peer(dx, dy) for dx, dy in _CHIP_FLIPS]:
            pl.semaphore_signal(barrier, inc=1, device_id=peer, device_id_type=MESH)
        pl.semaphore_wait(barrier, 3)
        _alltoall_exchange(p, out, send, recv)

    launch()
    return [r[...] for r in out]


def _alltoall_exchange(p, out, send, recv):
    x, y, c = _me()
    me = 2 * x + y
    peers = [_chip_peer(dx, dy) for dx, dy in _CHIP_FLIPS]
    for t in range(len(p)):
        h = p[t].shape[1]
        nch = _chunks(h, 16)
        ch = h // nch
        for k, peer in enumerate(peers):
            pchip = 2 * peer[0] + peer[1]
            for i in range(nch):
                rows = pl.ds(i * ch, ch)
                _rcopy(p[t].at[pchip, rows], out[t].at[me, rows], send.at[3 * t + k], recv.at[3 * t + k], peer).start()
    for t in range(len(p)):
        for k, peer in enumerate(peers):
            pchip = 2 * peer[0] + peer[1]
            _rcopy(out[t].at[pchip], out[t].at[pchip], send.at[3 * t + k], recv.at[3 * t + k], peer).wait()


def _rs_swap(name, halves):
    n = len(halves)

    def body(*refs):
        a, out = refs[:n], refs[n:2 * n]
        send, recv = refs[2 * n:]
        x, y, c = _me()
        sib = (x, y, 1 - c)
        for t in range(n):
            ch = a[t].shape[0] // _NCH
            for i in range(_NCH):
                rows = pl.ds(i * ch, ch)
                _rcopy(a[t].at[rows], out[t].at[rows], send.at[t], recv.at[t], sib).start()
        for t in range(n):
            _rcopy(a[t], out[t], send.at[t], recv.at[t], sib).wait()

    return pl.pallas_call(
        body, name=name, in_specs=[_ANY] * n, out_specs=[_ANY] * n,
        out_shape=[jax.ShapeDtypeStruct(s.shape, s.dtype) for s in halves],
        scratch_shapes=_dma_sems(n, n))(*halves)


def _sibling_merge(name, a):
    P_, rh, C = a.shape

    def body(a_ref, out_ref, send_sem, recv_sem, local_sem):
        x, y, c = _me()
        local = pltpu.make_async_copy(a_ref, out_ref.at[:, pl.ds(c * rh, rh)], local_sem)
        local.start()
        cp = _rcopy(a_ref, out_ref.at[:, pl.ds(c * rh, rh)], send_sem, recv_sem, (x, y, 1 - c))
        cp.start()
        cp.wait_send()
        _rcopy(a_ref, out_ref.at[:, pl.ds((1 - c) * rh, rh)], send_sem, recv_sem, (x, y, 1 - c)).wait_recv()
        local.wait()

    return pl.pallas_call(
        body, name=name, in_specs=[_ANY], out_specs=_ANY, out_shape=jax.ShapeDtypeStruct((P_, 2 * rh, C), a.dtype),
        scratch_shapes=[pltpu.SemaphoreType.DMA(()), pltpu.SemaphoreType.DMA(()), pltpu.SemaphoreType.DMA(())])(a)


def _allgather8(name, a):
    g4 = _allgather4(name + "_chips", a)
    both = _sibling_merge(name + "_cores", g4.reshape(1, 4 * a.shape[0], a.shape[1]))
    return jnp.transpose(both.reshape(2, 4, *a.shape), (1, 0, 2, 3)).reshape(8, *a.shape)


def _sum_slots(name, a, out_dtype):
    def fn(a):
        acc = a[0].astype(F32)
        for k in range(1, a.shape[0]):
            acc = acc + a[k].astype(F32)
        return acc
    return _rowwise(name, fn, [a], [], [(a.shape[2], out_dtype)])[0]


def _adamw_math(w, g, m, v):
    m = ADAM_B1 * m + (1.0 - ADAM_B1) * g
    v = ADAM_B2 * v + (1.0 - ADAM_B2) * (g * g)
    m_hat = m / (1.0 - ADAM_B1 ** ADAM_STEP)
    v_hat = v / (1.0 - ADAM_B2 ** ADAM_STEP)
    return -ADAM_LR * (m_hat / (jnp.sqrt(v_hat) + ADAM_EPS) + ADAM_WD * w), m, v


def _adamw_piece(name, w2, m2, v2, mine, theirs, row0, prev):
    R, C = w2.shape
    h = mine.shape[0]
    tr = _slot_tile(h, 256)
    nb = h // tr
    assert row0 % tr == 0
    first = row0 // tr

    def body(w_ref, m_ref, v_ref, a_ref, b_ref, *rest):
        g_ref, d_ref, nm_ref, nv_ref = rest[-4:]
        g = jnp.where(pl.program_id(0) == lax.axis_index("c"), a_ref[...], b_ref[...])
        g_ref[...] = g
        d_ref[...], nm_ref[...], nv_ref[...] = _adamw_math(w_ref[...], g, m_ref[...], v_ref[...])

    full = pl.BlockSpec((tr, C), lambda s, i: (first + s * nb + i, 0))
    half = pl.BlockSpec((tr, C), lambda s, i: (i, 0))
    extra = [] if prev is None else list(prev)
    return pl.pallas_call(
        body, name=name, grid=(2, nb), in_specs=[full, full, full, half, half] + [_ANY] * len(extra), out_specs=[full] * 4,
        out_shape=[jax.ShapeDtypeStruct((R, C), F32)] * 4, input_output_aliases={5 + k: k for k in range(len(extra))},
        compiler_params=_params(2))(w2, m2, v2, mine, theirs, *extra)


def _adamw(name, w, g, m, v):
    shape = w.shape
    two_d = (-1, shape[-1]) if w.ndim > 1 else (1, -1)
    w2, g2, m2, v2 = [t.reshape(two_d) for t in (w, g, m, v)]
    rows = w2.shape[0]
    tr = rows
    for cand in (256, 128, 64, 32, 16, 8):
        if rows % cand == 0:
            tr = cand
            break

    c = w2.shape[1]
    outs = _rowwise(name, _adamw_math, [w2, g2, m2, v2], [], [(c, F32)] * 3, tr=tr)
    return [o.reshape(shape) for o in outs]


_WEIGHT_ORDER = ("ada_w", "ada_b", "norm_mix_g", "norm_ffn_g", "gdn_w_in", "gdn_conv_w", "gdn_a_log", "gdn_dt_bias",
                 "gdn_norm_g", "gdn_w_out", "mla_w_in", "mla_q_norm_g", "mla_kv_norm_g", "mla_w_uq", "mla_w_ukv",
                 "mla_w_out", "ffn_w_gate", "ffn_w_up", "ffn_w_down", "final_norm_g")
_BIG = (("gdn_w_in", 2), ("gdn_w_out", 1), ("mla_w_in", 1), ("mla_w_uq", 2), ("mla_w_ukv", 2), ("mla_w_out", 1),
        ("ffn_w_gate", 2), ("ffn_w_up", 2), ("ffn_w_down", 1))
_SMALL_SHARDED = (("gdn_conv_w", 1), ("mla_q_norm_g", 1), ("mla_kv_norm_g", 1))
_STORED_TRANSPOSED = ("ffn_w_gate", "ffn_w_up")


def _size(shape):
    n = 1
    for s in shape:
        n *= s
    return n


def _pack_rows_each(tensors):
    parts, offs, off = [], [], 0
    for t in tensors:
        flat = t.reshape(-1).astype(F32)
        rows = -(-flat.shape[0] // PACK_W)
        parts.append(jnp.pad(flat, (0, rows * PACK_W - flat.shape[0])).reshape(rows, PACK_W))
        offs.append(off)
        off += rows
    total = -(-off // 16) * 16
    pack = jnp.pad(parts[0], ((offs[0], total - offs[0] - parts[0].shape[0]), (0, 0)))
    for p, o in zip(parts[1:], offs[1:]):
        pack = pack + jnp.pad(p, ((o, total - o - p.shape[0]), (0, 0)))
    return pack, offs


def _unpack_rows_each(pack, shapes):
    lead = pack.shape[:-2]
    out, off = [], 0
    for shp in shapes:
        n = _size(shp)
        rows = -(-n // PACK_W)
        out.append(pack[..., off:off + rows, :].reshape(*lead, -1)[..., :n].reshape(*lead, *shp))
        off += rows
    return out


def _merge_chips(stacked, axis):
    moved = jnp.moveaxis(stacked, 0, axis)
    shp = list(moved.shape)
    return moved.reshape(shp[:axis] + [shp[axis] * shp[axis + 1]] + shp[axis + 2:])


def _my_shard(full, axis, chip):
    n = full.shape[axis] // 4
    return lax.dynamic_slice_in_dim(full, chip * n, n, axis)


def kernel(x, c, positions, ada_w, ada_b, norm_mix_g, norm_ffn_g, gdn_w_in, gdn_conv_w, gdn_a_log, gdn_dt_bias, gdn_norm_g, gdn_w_out, mla_w_in, mla_q_norm_g, mla_kv_norm_g, mla_w_uq, mla_w_ukv, mla_w_out, ffn_w_gate, ffn_w_up, ffn_w_down, final_norm_g, loss_target, m_ada_w, m_ada_b, m_norm_mix_g, m_norm_ffn_g, m_gdn_w_in, m_gdn_conv_w, m_gdn_a_log, m_gdn_dt_bias, m_gdn_norm_g, m_gdn_w_out, m_mla_w_in, m_mla_q_norm_g, m_mla_kv_norm_g, m_mla_w_uq, m_mla_w_ukv, m_mla_w_out, m_ffn_w_gate, m_ffn_w_up, m_ffn_w_down, m_final_norm_g, v_ada_w, v_ada_b, v_norm_mix_g, v_norm_ffn_g, v_gdn_w_in, v_gdn_conv_w, v_gdn_a_log, v_gdn_dt_bias, v_gdn_norm_g, v_gdn_w_out, v_mla_w_in, v_mla_q_norm_g, v_mla_kv_norm_g, v_mla_w_uq, v_mla_w_ukv, v_mla_w_out, v_ffn_w_gate, v_ffn_w_up, v_ffn_w_down, v_final_norm_g):
    w = dict(ada_w=ada_w, ada_b=ada_b, norm_mix_g=norm_mix_g, norm_ffn_g=norm_ffn_g, gdn_w_in=gdn_w_in, gdn_conv_w=gdn_conv_w,
             gdn_a_log=gdn_a_log, gdn_dt_bias=gdn_dt_bias, gdn_norm_g=gdn_norm_g, gdn_w_out=gdn_w_out, mla_w_in=mla_w_in,
             mla_q_norm_g=mla_q_norm_g, mla_kv_norm_g=mla_kv_norm_g, mla_w_uq=mla_w_uq, mla_w_ukv=mla_w_ukv,
             mla_w_out=mla_w_out, ffn_w_gate=ffn_w_gate, ffn_w_up=ffn_w_up, ffn_w_down=ffn_w_down, final_norm_g=final_norm_g)
    m = dict(ada_w=m_ada_w, ada_b=m_ada_b, norm_mix_g=m_norm_mix_g, norm_ffn_g=m_norm_ffn_g, gdn_w_in=m_gdn_w_in,
             gdn_conv_w=m_gdn_conv_w, gdn_a_log=m_gdn_a_log, gdn_dt_bias=m_gdn_dt_bias, gdn_norm_g=m_gdn_norm_g,
             gdn_w_out=m_gdn_w_out, mla_w_in=m_mla_w_in, mla_q_norm_g=m_mla_q_norm_g, mla_kv_norm_g=m_mla_kv_norm_g,
             mla_w_uq=m_mla_w_uq, mla_w_ukv=m_mla_w_ukv, mla_w_out=m_mla_w_out, ffn_w_gate=m_ffn_w_gate,
             ffn_w_up=m_ffn_w_up, ffn_w_down=m_ffn_w_down, final_norm_g=m_final_norm_g)
    v = dict(ada_w=v_ada_w, ada_b=v_ada_b, norm_mix_g=v_norm_mix_g, norm_ffn_g=v_norm_ffn_g, gdn_w_in=v_gdn_w_in,
             gdn_conv_w=v_gdn_conv_w, gdn_a_log=v_gdn_a_log, gdn_dt_bias=v_gdn_dt_bias, gdn_norm_g=v_gdn_norm_g,
             gdn_w_out=v_gdn_w_out, mla_w_in=v_mla_w_in, mla_q_norm_g=v_mla_q_norm_g, mla_kv_norm_g=v_mla_kv_norm_g,
             mla_w_uq=v_mla_w_uq, mla_w_ukv=v_mla_w_ukv, mla_w_out=v_mla_w_out, ffn_w_gate=v_ffn_w_gate,
             ffn_w_up=v_ffn_w_up, ffn_w_down=v_ffn_w_down, final_norm_g=v_final_norm_g)
    T = x.shape[1]
    ix, iy, ic = _me()
    chip = 2 * ix + iy
    seq = 2 * chip + ic
    n_dev = 8

    small_shapes = [w[n].shape for n, _ in _SMALL_SHARDED] + [c.shape]
    pack0, _ = _pack_rows_each([w[n] for n, _ in _SMALL_SHARDED] + [c])
    got0 = _unpack_rows_each(_allgather8("gather_small", pack0), small_shapes)
    small_full = {n: _merge_chips(g[0::2], ax) for (n, ax), g in zip(_SMALL_SHARDED, got0)}
    c_all = got0[-1].reshape(n_dev, D)

    big = [n for n, _ in _BIG]
    chip_arr = chip.astype(jnp.int32).reshape(1)

    def stored(n, t):
        return jnp.swapaxes(t, 1, 2) if n in _STORED_TRANSPOSED else t

    ws, ms, vs = [{n: stored(n, d[n]) for n in big} for d in (w, m, v)]
    two_d = lambda t: t.reshape(-1, t.shape[-1])

    gathered = []
    for l in range(DEPTH):
        names = _layer_weights(l)
        bufs = [_cast_into_slot(f"to_bf16_{n}{l}", two_d(ws[n]), chip_arr, j * ws[n].shape[1], ws[n].shape[1]) for n, j in names]
        filled = _gather_weights("gather_weights0", bufs) if l == 0 else _gather_weights_async(f"gather_weights{l}", l, bufs)
        gathered.append({n: b for (n, _), b in zip(names, filled)})

    def weights_of(l, h):
        return _weights_to_kernel(l, gathered[l])

    P = _small_to_kernel(norm_mix_g, norm_ffn_g, final_norm_g, small_full["gdn_conv_w"], gdn_a_log, gdn_dt_bias,
                         gdn_norm_g, small_full["mla_q_norm_g"], small_full["mla_kv_norm_g"])

    c16 = jnp.pad(c_all, ((0, 16 - n_dev), (0, 0)))
    ca = _rowwise("cond_silu", lambda t: t * _sig(t), [c16], [], [(D, BF16)])[0]
    n_ada = ada_w.shape[2]
    mods = jnp.concatenate([_mm(f"ada_fwd{l}", ca, ada_w[l], "nn") for l in range(DEPTH)], axis=0)
    mods_all = _allgather4("gather_mod", mods).reshape(4, DEPTH, 16, n_ada)
    mod_mm = jnp.transpose(lax.dynamic_index_in_dim(mods_all, seq, axis=2, keepdims=False), (1, 0, 2)).reshape(DEPTH, 4 * n_ada)
    mod = _rowwise("mod_bias", lambda a, b: a + b, [mod_mm, ada_b], [], [(4 * n_ada, F32)])[0]

    core_chip = jnp.stack([ic, chip]).astype(jnp.int32)
    pending, in_flight = {}, []

    def reduce_group(layer, part, pieces):
        pending.update({(n, layer if n.startswith("ffn_") else layer // 2): g for n, g in pieces.items()})
        if part == "ffn" and layer > 0:
            return
        keys = list(pending)
        glist = [pending.pop(k) for k in keys]
        tag = f"{layer}{part}"
        theirs = _rs_split("grads_cores_" + tag, glist)
        both = [_pair_add(f"grads_pair_{n}{l}", g, t, core_chip) for (n, l), g, t in zip(keys, glist, theirs)]
        swapped = _rs_alltoall_async("grads_chips_" + tag, DEPTH + 1 + len(in_flight), [p for p, _ in both], [o for _, o in both])
        in_flight.append((tag, keys, swapped))

    dx, dmod, gP = _local_step(x.reshape(T, D), loss_target.reshape(T, D), positions.reshape(T, 1), mod, weights_of, P, reduce_group)

    partials = [dmod, jnp.concatenate(gP["norm_mix_g"]), jnp.concatenate(gP["norm_ffn_g"]), gP["final_g"],
                jnp.stack([jnp.transpose(g) for g in gP["gdn_cw"]]), jnp.concatenate(gP["gdn_alog"])[:, :NH],
                jnp.concatenate(gP["gdn_dtb"])[:, :NH], jnp.concatenate(gP["gdn_ng"]), jnp.concatenate(gP["mla_qg"]),
                jnp.concatenate(gP["mla_kvg"]), gP["loss"][:, :1]]
    part_shapes = [p.shape for p in partials]
    ppack, _ = _pack_rows_each(partials)
    pall = _allgather8("gather_partials", ppack)
    psum = _sum_slots("sum_partials", pall, F32)
    (g_ada_b, g_norm_mix, g_norm_ffn, g_final, g_conv_full, g_alog, g_dtb, g_gdn_ng, g_qg_full, g_kvg_full,
     loss_sum) = _unpack_rows_each(psum, part_shapes)
    dmod_all = _unpack_rows_each(pall, part_shapes[:1])[0]

    grads = dict(ada_b=g_ada_b, norm_mix_g=g_norm_mix, norm_ffn_g=g_norm_ffn, final_norm_g=g_final.reshape(D),
                 gdn_conv_w=_my_shard(g_conv_full, 1, chip), gdn_a_log=g_alog, gdn_dt_bias=g_dtb, gdn_norm_g=g_gdn_ng,
                 mla_q_norm_g=_my_shard(g_qg_full, 1, chip), mla_kv_norm_g=_my_shard(g_kvg_full, 1, chip))

    ca_t = jnp.zeros((D, LANES), BF16).at[:, :16].set(jnp.transpose(ca))
    dm_mine = lax.dynamic_slice_in_dim(dmod_all, chip * n_ada, n_ada, axis=2)
    grads["ada_w"] = jnp.stack([
        _mm(f"ada_bwd{l}", ca_t, jnp.pad(dm_mine[:, l], ((0, LANES - n_dev), (0, 0))), "nn") for l in range(DEPTH)])

    delta, new_m, new_v = {}, {}, {}
    results = {}
    keys = [k for _, ks, _ in in_flight for k in ks]
    halves = [_sum_slots(f"grads_sum_{n}{l}", s, F32) for _, ks, sw in in_flight for (n, l), s in zip(ks, sw)]
    others = _rs_swap("grads_swap", halves)
    for (n, l), mine, theirs in zip(keys, halves, others):
        results[n] = _adamw_piece(f"adamw_{n}{l}", two_d(ws[n]), two_d(ms[n]), two_d(vs[n]), mine, theirs,
                                  l * ws[n].shape[1], results.get(n))
    for n in big:
        grads[n], delta[n], new_m[n], new_v[n] = [stored(n, t.reshape(ws[n].shape)) for t in results[n]]
    delta["ada_w"], new_m["ada_w"], new_v["ada_w"] = _adamw("adamw_ada_w", ada_w, grads["ada_w"], m_ada_w, v_ada_w)
    small_names = [n for n in _WEIGHT_ORDER if n not in delta]
    small_shapes = [w[n].shape for n in small_names]
    packs = [_pack_rows_each([d[n] for n in small_names])[0] for d in (w, grads, m, v)]
    for d, pk in zip((delta, new_m, new_v), _adamw("adamw_small", *packs)):
        for n, t in zip(small_names, _unpack_rows_each(pk, small_shapes)):
            d[n] = t

    loss = loss_sum.reshape(())
    return (loss, dx.reshape(1, T, D), *[grads[n] for n in _WEIGHT_ORDER], *[delta[n] for n in _WEIGHT_ORDER],
            *[new_m[n] for n in _WEIGHT_ORDER], *[new_v[n] for n in _WEIGHT_ORDER])
```

```python
import functools

import jax
import jax.numpy as jnp
from jax import lax
from jax.experimental import pallas as pl
from jax.experimental.pallas import tpu as pltpu
from jax.experimental.pallas import tpu_sc as plsc

F32 = jnp.float32
BF16 = jnp.bfloat16
HI = lax.Precision.HIGHEST
MESH = pl.DeviceIdType.MESH

D = 1024
DEPTH = 4
N_MOD = 6
NH = 8
HD = 128
CHUNK = 64
_GDN_HB = 8
GDN_QKV = 3 * NH * HD
GDN_INK = GDN_QKV + NH * HD + 2 * HD
Q_RANK, KV_RANK, ROPE = 384, 256, 64
MLA_INK = Q_RANK + KV_RANK + HD
DFF = 2816
EPS = 1e-6
ATT_SCALE = (HD + ROPE) ** -0.5
ROPE_THETA = 10000.0
LANES = 128
PACK_W = 1024

ADAM_LR, ADAM_B1, ADAM_B2, ADAM_EPS, ADAM_WD, ADAM_STEP = 0.001, 0.9, 0.999, 1e-08, 0.01, 10


H3 = "bf16x3"
B1 = "bf16"
HS = H3
HF = B1


def _dot(a, b, mode="nn", prec=None):
    dn = {"nn": (((1,), (0,)), ((), ())), "nt": (((1,), (1,)), ((), ())), "tn": (((0,), (0,)), ((), ()))}[mode]
    if prec == B1:
        return _dot(a.astype(BF16), b.astype(BF16), mode)
    if prec == H3:
        ah, bh = a.astype(BF16), b.astype(BF16)
        al, bl = (a - ah.astype(F32)).astype(BF16), (b - bh.astype(F32)).astype(BF16)
        return _dot(ah, bh, mode) + (_dot(ah, bl, mode) + _dot(al, bh, mode))
    return lax.dot_general(a, b, dn, precision=prec, preferred_element_type=F32)


def _sig(x):
    return 1.0 / (1.0 + jnp.exp(-x))


def _pick(n, cap):
    if n <= cap:
        return n
    best = None
    for d in range(LANES, cap + 1, LANES):
        if n % d == 0:
            best = d
    assert best is not None, (n, cap)
    return best


def _params(n_grid):
    return pltpu.CompilerParams(dimension_semantics=("arbitrary",) * n_grid, vmem_limit_bytes=56 * 1024 * 1024)


def _rowwise(name, fn, rows, consts, outs, sums=(), tr=256):
    first = rows[0][0] if isinstance(rows[0], tuple) else rows[0]
    T = first.shape[-2]
    tr = _slot_tile(T, tr)
    nr, nc, no, ns = len(rows), len(consts), len(outs), len(sums)

    windows = [c[1:] if isinstance(c, tuple) else None for c in consts]
    consts = [c[0] if isinstance(c, tuple) else c for c in consts]

    def body(*refs):
        vals = [r[...] for r in refs[:nr]]
        for r, win in zip(refs[nr:nr + nc], windows):
            vals.append(r[...] if win is None else r[win[0]:win[0] + 1, win[1] * win[2]:(win[1] + 1) * win[2]])
        res = fn(*vals)
        if not isinstance(res, (tuple, list)):
            res = (res,)
        o_refs = refs[nr + nc:nr + nc + no]
        s_refs = refs[nr + nc + no:]
        for r, val in zip(o_refs, res[:no]):
            r[...] = val.astype(r.dtype)
        if ns:
            @pl.when(pl.program_id(0) == 0)
            def _():
                for r in s_refs:
                    r[...] = jnp.zeros_like(r)
            for r, val in zip(s_refs, res[no:]):
                r[...] += val

    in_specs, args = [], []
    for a in rows:
        if isinstance(a, tuple):
            arr, width, cb = a
            in_specs.append(pl.BlockSpec((tr, width), lambda i, cb=cb: (i, cb)))
            args.append(arr)
        elif a.ndim == 3:
            in_specs.append(pl.BlockSpec((a.shape[0], tr, a.shape[2]), lambda i: (0, i, 0)))
            args.append(a)
        else:
            in_specs.append(pl.BlockSpec((tr, a.shape[1]), lambda i: (i, 0)))
            args.append(a)
    for a in consts:
        in_specs.append(pl.BlockSpec(a.shape, lambda i, nd=a.ndim: (0,) * nd))
        args.append(a)
    out_specs = [pl.BlockSpec((tr, w), lambda i: (i, 0)) for w, _ in outs]
    out_specs += [pl.BlockSpec((1, w), lambda i: (0, 0)) for w in sums]
    out_shape = [jax.ShapeDtypeStruct((T, w), dt) for w, dt in outs]
    out_shape += [jax.ShapeDtypeStruct((1, w), F32) for w in sums]
    res = pl.pallas_call(body, name=name, grid=(T // tr,), in_specs=in_specs, out_specs=out_specs,
                         out_shape=out_shape, compiler_params=_params(1))(*args)
    return res


def _mm(name, a, b, mode, out_dtype=F32, tm=512, tn=1024):
    if mode == "tn":
        K, M = a.shape
    else:
        M, K = a.shape
    N = b.shape[0] if mode == "nt" else b.shape[1]
    tm, tn = _pick(M, tm), _pick(N, tn)

    def body(a_ref, b_ref, o_ref):
        o_ref[...] = _dot(a_ref[...].astype(BF16), b_ref[...].astype(BF16), mode).astype(o_ref.dtype)

    a_spec = pl.BlockSpec((K, tm), lambda i, j: (0, i)) if mode == "tn" else pl.BlockSpec((tm, K), lambda i, j: (i, 0))
    b_spec = pl.BlockSpec((tn, K), lambda i, j: (j, 0)) if mode == "nt" else pl.BlockSpec((K, tn), lambda i, j: (0, j))
    return pl.pallas_call(body, name=name, grid=(M // tm, N // tn), in_specs=[a_spec, b_spec],
                          out_specs=pl.BlockSpec((tm, tn), lambda i, j: (i, j)),
                          out_shape=jax.ShapeDtypeStruct((M, N), out_dtype), compiler_params=_params(2))(a, b)


def _rms(x, eps=EPS):
    return lax.rsqrt(jnp.mean(x * x, axis=-1, keepdims=True) + eps)


def _norm_mod_fwd(name, x, g, scale, shift):
    def fn(x, g, scale, shift):
        return x * _rms(x) * g * (1.0 + scale) + shift
    return _rowwise(name, fn, [x], [g, scale, shift], [(D, BF16)])[0]


def _norm_mod_bwd(name, dh, x, dx_res, g, scale):
    def fn(dh, x, dx_res, g, scale):
        r = _rms(x)
        xh = x * r
        dxh = dh * (g * (1.0 + scale))
        dx = r * (dxh - xh * jnp.mean(dxh * xh, axis=-1, keepdims=True))
        dhx = dh * xh
        return (dx_res + dx, jnp.sum(dh, axis=0, keepdims=True), jnp.sum(dhx * g, axis=0, keepdims=True),
                jnp.sum(dhx * (1.0 + scale), axis=0, keepdims=True))
    return _rowwise(name, fn, [dh, x, dx_res], [g, scale], [(D, F32)], sums=[D, D, D])


def _residual_fwd(name, x, y, gate):
    def fn(x, y, gate):
        return x + gate * y
    return _rowwise(name, fn, [x, y], [gate], [(D, F32)])[0]


def _residual_bwd(name, dx, y, gate):
    def fn(dx, y, gate):
        return dx * gate, jnp.sum(dx * y, axis=0, keepdims=True)
    return _rowwise(name, fn, [dx, y], [gate], [(D, BF16)], sums=[D])


def _loss_head(x, target, g):
    def fn(x, t, g):
        r = _rms(x)
        xh = x * r
        err = xh * g - t
        loss = 0.5 * jnp.sum(jnp.mean(err * err, axis=-1, keepdims=True), axis=0, keepdims=True)
        dy = err * (1.0 / D)
        dxh = dy * g
        dx = r * (dxh - xh * jnp.mean(dxh * xh, axis=-1, keepdims=True))
        return dx, jnp.broadcast_to(loss, (1, LANES)), jnp.sum(dy * xh, axis=0, keepdims=True)
    return _rowwise("loss_head", fn, [x, target], [g], [(D, F32)], sums=[LANES, D])


def _ffn_up(name, h, wg, wu, layer, tm=1024):
    T, n = h.shape[0], wg.shape[1]
    tm = min(tm, T)

    def body(h_ref, wg_ref, wu_ref, a_ref, b_ref, s_ref):
        h = h_ref[...]
        a = _dot(h, wg_ref[0], "nt")
        b = _dot(h, wu_ref[0], "nt")
        a_ref[0] = a.astype(a_ref.dtype)
        b_ref[0] = b.astype(b_ref.dtype)
        s_ref[0] = (a * _sig(a) * b).astype(s_ref.dtype)

    wspec = pl.BlockSpec((1, n, D), lambda ch, i: (ch, layer, 0))
    ospec = pl.BlockSpec((1, tm, n), lambda ch, i: (ch, i, 0))
    return pl.pallas_call(
        body, name=name, grid=(4, T // tm), in_specs=[pl.BlockSpec((tm, D), lambda ch, i: (i, 0)), wspec, wspec],
        out_specs=[ospec, ospec, ospec],
        out_shape=[jax.ShapeDtypeStruct((4, T, n), BF16)] * 3, compiler_params=_params(2))(h, wg, wu)


def _ffn_down(name, s, wd, layer, tm=1024):
    _, T, n = s.shape
    tm = min(tm, T)

    def body(s_ref, w_ref, y_ref):
        @pl.when(pl.program_id(1) == 0)
        def _():
            y_ref[...] = jnp.zeros_like(y_ref)
        y_ref[...] += _dot(s_ref[0], w_ref[0], "nn")

    return pl.pallas_call(
        body, name=name, grid=(T // tm, 4),
        in_specs=[pl.BlockSpec((1, tm, n), lambda i, ch: (ch, i, 0)), pl.BlockSpec((1, n, D), lambda i, ch: (ch, layer, 0))],
        out_specs=pl.BlockSpec((tm, D), lambda i, ch: (i, 0)), out_shape=jax.ShapeDtypeStruct((T, D), F32),
        compiler_params=_params(2))(s, wd)


def _ffn_down_bwd(name, dy, wd, a, b, layer, tm=1024):
    _, T, n = a.shape
    tm = min(tm, T)

    def body(dy_ref, w_ref, a_ref, b_ref, da_ref, db_ref):
        ds = _dot(dy_ref[...], w_ref[0], "nt")
        a, b = a_ref[0].astype(F32), b_ref[0].astype(F32)
        sg = _sig(a)
        da_ref[0] = (ds * b * (sg * (1.0 + a * (1.0 - sg)))).astype(da_ref.dtype)
        db_ref[0] = (ds * (a * sg)).astype(db_ref.dtype)

    bspec = pl.BlockSpec((1, tm, n), lambda ch, i: (ch, i, 0))
    return pl.pallas_call(
        body, name=name, grid=(4, T // tm),
        in_specs=[pl.BlockSpec((tm, D), lambda ch, i: (i, 0)), pl.BlockSpec((1, n, D), lambda ch, i: (ch, layer, 0)), bspec, bspec],
        out_specs=[bspec, bspec], out_shape=[jax.ShapeDtypeStruct((4, T, n), BF16)] * 2,
        compiler_params=_params(2))(dy, wd, a, b)


def _ffn_down_dw(name, s, dy):
    _, T, n = s.shape

    def body(s_ref, dy_ref, o_ref):
        o_ref[0] = _dot(s_ref[0], dy_ref[...], "tn").astype(o_ref.dtype)

    return pl.pallas_call(
        body, name=name, grid=(4,),
        in_specs=[pl.BlockSpec((1, T, n), lambda ch: (ch, 0, 0)), pl.BlockSpec((T, D), lambda ch: (0, 0))],
        out_specs=pl.BlockSpec((1, n, D), lambda ch: (ch, 0, 0)), out_shape=jax.ShapeDtypeStruct((4, n, D), BF16),
        compiler_params=_params(1))(s, dy)


def _ffn_up_dw(name, h, da, db, tm=512):
    _, T, n = da.shape

    def body(h_ref, da_ref, db_ref, dg_ref, du_ref):
        h = h_ref[...]
        dg_ref[0] = _dot(da_ref[0], h, "tn").astype(dg_ref.dtype)
        du_ref[0] = _dot(db_ref[0], h, "tn").astype(du_ref.dtype)

    dspec = pl.BlockSpec((1, T, n), lambda ch, j: (ch, 0, 0))
    ospec = pl.BlockSpec((1, n, tm), lambda ch, j: (ch, 0, j))
    return pl.pallas_call(
        body, name=name, grid=(4, D // tm), in_specs=[pl.BlockSpec((T, tm), lambda ch, j: (0, j)), dspec, dspec],
        out_specs=[ospec, ospec], out_shape=[jax.ShapeDtypeStruct((4, n, D), BF16)] * 2,
        compiler_params=_params(2))(h, da, db)


def _ffn_up_dx(name, da, db, wg, wu, layer, tm=1024):
    _, T, n = da.shape
    tm = min(tm, T)

    def body(da_ref, db_ref, wg_ref, wu_ref, o_ref):
        @pl.when(pl.program_id(1) == 0)
        def _():
            o_ref[...] = jnp.zeros_like(o_ref)
        o_ref[...] += _dot(da_ref[0], wg_ref[0], "nn") + _dot(db_ref[0], wu_ref[0], "nn")

    dspec = pl.BlockSpec((1, tm, n), lambda i, ch: (ch, i, 0))
    wspec = pl.BlockSpec((1, n, D), lambda i, ch: (ch, layer, 0))
    return pl.pallas_call(
        body, name=name, grid=(T // tm, 4), in_specs=[dspec, dspec, wspec, wspec],
        out_specs=pl.BlockSpec((tm, D), lambda i, ch: (i, 0)), out_shape=jax.ShapeDtypeStruct((T, D), F32),
        compiler_params=_params(2))(da, db, wg, wu)


def _shift_down(x, k):
    if k == 0:
        return x
    rows = lax.broadcasted_iota(jnp.int32, x.shape, 0)
    return jnp.where(rows >= k, pltpu.roll(x, k, 0), 0.0)


def _shift_up(x, k):
    if k == 0:
        return x
    T = x.shape[0]
    rows = lax.broadcasted_iota(jnp.int32, x.shape, 0)
    return jnp.where(rows < T - k, pltpu.roll(x, T - k, 0), 0.0)


def _conv_silu(x, w):
    c = w[0:1, :] * _shift_down(x, 3) + w[1:2, :] * _shift_down(x, 2) + w[2:3, :] * _shift_down(x, 1) + w[3:4, :] * x
    sg = _sig(c)
    return c, sg, c * sg


def _gdn_conv_fwd(name, proj, cw):
    T = proj.shape[0]

    def body(x_ref, w_ref, o_ref):
        j = pl.program_id(0)
        _, _, y = _conv_silu(x_ref[...], w_ref[...])
        r = lax.rsqrt(jnp.sum(y * y, axis=1, keepdims=True) + EPS)
        mult = jnp.where(j < NH, HD ** -0.5, 1.0)
        o_ref[...] = jnp.where(j < 2 * NH, y * (r * mult), y)

    return pl.pallas_call(body, name=name, grid=(3 * NH,),
                          in_specs=[pl.BlockSpec((T, HD), lambda j: (0, j)), pl.BlockSpec((4, HD), lambda j: (0, j))],
                          out_specs=pl.BlockSpec((T, HD), lambda j: (0, j)),
                          out_shape=jax.ShapeDtypeStruct((T, GDN_QKV), F32), compiler_params=_params(1))(proj, cw)


def _gdn_conv_bwd(name, proj, cw, dz):
    T = proj.shape[0]

    def body(x_ref, w_ref, dz_ref, dx_ref, dw_ref):
        j = pl.program_id(0)
        x, w, dz = x_ref[...], w_ref[...], dz_ref[...]
        c, sg, y = _conv_silu(x, w)
        r = lax.rsqrt(jnp.sum(y * y, axis=1, keepdims=True) + EPS)
        mult = jnp.where(j < NH, HD ** -0.5, 1.0)
        dyn = mult * (r * dz - (r * r * r) * y * jnp.sum(dz * y, axis=1, keepdims=True))
        dy = jnp.where(j < 2 * NH, dyn, dz)
        dc = dy * (sg * (1.0 + c * (1.0 - sg)))
        dx = w[0:1, :] * _shift_up(dc, 3) + w[1:2, :] * _shift_up(dc, 2) + w[2:3, :] * _shift_up(dc, 1) + w[3:4, :] * dc
        dx_ref[...] = dx.astype(dx_ref.dtype)
        for k in range(4):
            dw_ref[pl.ds(k, 1), :] = jnp.sum(dc * _shift_down(x, 3 - k), axis=0, keepdims=True)

    return pl.pallas_call(body, name=name, grid=(3 * NH,),
                          in_specs=[pl.BlockSpec((T, HD), lambda j: (0, j)), pl.BlockSpec((4, HD), lambda j: (0, j)),
                                    pl.BlockSpec((T, HD), lambda j: (0, j))],
                          out_specs=[pl.BlockSpec((T, HD), lambda j: (0, j)), pl.BlockSpec((4, HD), lambda j: (0, j))],
                          out_shape=[jax.ShapeDtypeStruct((T, GDN_QKV), BF16), jax.ShapeDtypeStruct((4, GDN_QKV), F32)],
                          compiler_params=_params(1))(proj, cw, dz)


def _softplus(z):
    return jnp.maximum(z, 0.0) + jnp.log(1.0 + jnp.exp(-jnp.abs(z)))


_AB_CB = GDN_INK // (2 * HD) - 1


def _gdn_gates_fwd(name, proj, alog, dtb):
    def fn(ab, alog, dtb):
        a, b = ab[:, :HD], ab[:, HD:]
        return -jnp.exp(alog) * _softplus(a + dtb), _sig(b)
    return _rowwise(name, fn, [(proj, 2 * HD, _AB_CB)], [alog, dtb], [(HD, F32), (HD, F32)])


def _gdn_gates_bwd(name, proj, dg_h, db_h, alog, dtb):
    def fn(ab, dg_h, db_h, alog, dtb):
        lane = lax.broadcasted_iota(jnp.int32, (1, HD), 1)
        dg = jnp.zeros(dg_h.shape[1:], F32)
        dbeta = jnp.zeros(dg_h.shape[1:], F32)
        for h in range(NH):
            oh = (lane == h).astype(F32)
            dg = dg + dg_h[h] * oh
            dbeta = dbeta + db_h[h] * oh
        a, b = ab[:, :HD], ab[:, HD:]
        z = a + dtb
        ea = jnp.exp(alog)
        beta = _sig(b)
        da = dg * (-ea) * _sig(z)
        db = dbeta * beta * (1.0 - beta)
        return (jnp.concatenate([da, db], axis=1), jnp.sum(dg * (-ea * _softplus(z)), axis=0, keepdims=True),
                jnp.sum(da, axis=0, keepdims=True))
    return _rowwise(name, fn, [(proj, 2 * HD, _AB_CB), dg_h, db_h], [alog, dtb], [(2 * HD, BF16)], sums=[HD, HD])


def _interleave(gens):
    gens = list(gens)
    results = [None] * len(gens)
    active = list(range(len(gens)))
    while active:
        for i in list(active):
            try:
                next(gens[i])
            except StopIteration as stop:
                results[i] = stop.value
                active.remove(i)
    return results


def _chunk_common(q, k, v, gblk, bblk, h, prec):
    C = CHUNK
    lane = lax.broadcasted_iota(jnp.int32, (1, HD), 1)
    oh = (lane == h).astype(F32)
    g_col = jnp.sum(gblk * oh, axis=1, keepdims=True)
    beta = jnp.sum(bblk * oh, axis=1, keepdims=True)
    ri = lax.broadcasted_iota(jnp.int32, (C, C), 0)
    ci = lax.broadcasted_iota(jnp.int32, (C, C), 1)
    incl = ri >= ci
    strict = ri > ci
    eye = (ri == ci).astype(F32)
    gcb = _dot(incl.astype(F32), jnp.broadcast_to(g_col, (C, HD)), "nn", HI)
    yield
    gc = gcb[:, :C]
    gc_row = _dot(jnp.ones((C, C), F32), eye * gc, "nn", HI)
    yield
    decay = jnp.where(incl, jnp.exp(jnp.where(incl, gc - gc_row, 0.0)), 0.0)
    rows = lax.broadcasted_iota(jnp.int32, (C, HD), 0)
    gclb = jnp.sum(jnp.where(rows == C - 1, gcb, 0.0), axis=0, keepdims=True)
    eg = jnp.exp(gcb)
    egl = jnp.exp(gclb - gcb)
    gl = jnp.exp(gclb)
    kb = k * beta
    m1 = _dot(kb, k, "nt", prec)
    qk = _dot(q, k, "nt", prec)
    yield
    L = jnp.where(strict, m1 * decay, 0.0)
    nl = -L
    tinv = eye + nl
    p = nl
    for _ in range(5):
        p = _dot(p, p, "nn", H3)
        yield
        tinv = tinv + _dot(tinv, p, "nn", H3)
    vb = v * beta
    kbg = kb * eg
    yield
    u = _dot(tinv, vb, "nn", prec)
    w = _dot(tinv, kbg, "nn", prec)
    yield
    attn = jnp.where(incl, qk * decay, 0.0)
    return dict(beta=beta, incl=incl, strict=strict, decay=decay, eg=eg, egl=egl, gl=gl, kb=kb, m1=m1, tinv=tinv,
                kbg=kbg, u=u, w=w, qk=qk, attn=attn, q_dec=q * eg, k_dec=k * egl, rows=rows, oh=oh)


def _gdn_chunk_fwd(name, qkv, g, beta):
    T = qkv.shape[0]
    N = T // CHUNK

    hb = _GDN_HB
    w = hb * HD

    def body(q_ref, k_ref, v_ref, g_ref, b_ref, o_ref, st_ref, S):
        hg, n = pl.program_id(0), pl.program_id(1)

        @pl.when(n == 0)
        def _():
            S[...] = jnp.zeros_like(S)

        gblk, bblk = g_ref[...], b_ref[...]

        def one_head(i, q, k, v, s):
            c = yield from _chunk_common(q, k, v, gblk, bblk, hg * hb + i, HF)
            v_new = c["u"] - _dot(c["w"], s, "nn", HF)
            qs = _dot(c["q_dec"], s, "nn", HF)
            yield
            o = qs + _dot(c["attn"], v_new, "nn", HF)
            return o, s * c["gl"] + _dot(c["k_dec"], v_new, "tn", HF)

        sls = [slice(i * HD, (i + 1) * HD) for i in range(hb)]
        states = [S[i] for i in range(hb)]
        res = _interleave(one_head(i, q_ref[:, sls[i]], k_ref[:, sls[i]], v_ref[:, sls[i]], states[i]) for i in range(hb))
        for i, (o, s_new) in enumerate(res):
            st_ref[i, 0] = states[i]
            o_ref[:, sls[i]] = o
            S[i] = s_new

    blk = lambda off: pl.BlockSpec((CHUNK, w), lambda h, n, off=off: (n, off + h))
    gspec = pl.BlockSpec((CHUNK, HD), lambda h, n: (n, 0))
    return pl.pallas_call(
        body, name=name, grid=(NH // hb, N), in_specs=[blk(0), blk(NH // hb), blk(2 * NH // hb), gspec, gspec],
        out_specs=[pl.BlockSpec((CHUNK, w), lambda h, n: (n, h)), pl.BlockSpec((hb, 1, HD, HD), lambda h, n: (h, n, 0, 0))],
        out_shape=[jax.ShapeDtypeStruct((T, NH * HD), F32), jax.ShapeDtypeStruct((NH, N, HD, HD), F32)],
        scratch_shapes=[pltpu.VMEM((hb, HD, HD), F32)], compiler_params=_params(2))(qkv, qkv, qkv, g, beta)


def _gdn_chunk_bwd(name, qkv, g, beta, states, do):
    T = qkv.shape[0]
    N = T // CHUNK
    C = CHUNK

    hb = _GDN_HB
    w = hb * HD
    assert hb == NH

    def body(q_ref, k_ref, v_ref, g_ref, b_ref, st_ref, do_ref, dqkv_ref, dg_ref, db_ref, dS):
        hg, n = pl.program_id(0), pl.program_id(1)

        @pl.when(n == 0)
        def _():
            dS[...] = jnp.zeros_like(dS)

        gblk, bblk = g_ref[...], b_ref[...]
        sls = [slice(i * HD, (i + 1) * HD) for i in range(hb)]
        res = _interleave(one_head(hg * hb + i, gblk, bblk, q_ref[:, sls[i]], k_ref[:, sls[i]], v_ref[:, sls[i]],
                                   st_ref[i, 0], do_ref[:, sls[i]], dS[i]) for i in range(hb))
        for i, (dq, dk, dv, dg, db, ds_new) in enumerate(res):
            dqkv_ref[:, sls[i]] = dq
            dqkv_ref[:, slice(w + i * HD, w + (i + 1) * HD)] = dk
            dqkv_ref[:, slice(2 * w + i * HD, 2 * w + (i + 1) * HD)] = dv
            dg_ref[i] = dg
            db_ref[i] = db
            dS[i] = ds_new

    def one_head(h, gblk, bblk, q, k, v, s, do, ds):
        c = yield from _chunk_common(q, k, v, gblk, bblk, h, HF)
        eg, egl, gl, beta, decay, tinv = c["eg"], c["egl"], c["gl"], c["beta"], c["decay"], c["tinv"]
        v_new = c["u"] - _dot(c["w"], s, "nn", HF)
        dq_dec = _dot(do, s, "nt", HF)
        yield
        dv_new = _dot(c["attn"], do, "tn", HF) + _dot(c["k_dec"], ds, "nn", HF)
        dk_dec = _dot(v_new, ds, "nt", HF)
        dgl = jnp.sum(jnp.sum(s * ds, axis=1, keepdims=True), axis=0, keepdims=True)
        yield
        ds_new = ds * gl + _dot(c["q_dec"], do, "tn", HF) - _dot(c["w"], dv_new, "tn", HF)
        dattn = jnp.where(c["incl"], _dot(do, v_new, "nt", HF), 0.0)
        dw = -_dot(dv_new, s, "nt", HF)
        yield
        dvb = _dot(tinv, dv_new, "tn", HS)
        dkbg = _dot(tinv, dw, "tn", HS)
        yield
        dA = -(_dot(dvb, c["u"], "nt", HS) + _dot(dkbg, c["w"], "nt", HS))
        yield
        dL = jnp.where(c["strict"], dA, 0.0)
        dm1 = dL * decay
        dqk = dattn * decay
        xdec = (dL * c["m1"] + dattn * c["qk"]) * decay
        dkb = _dot(dm1, k, "nn", HS) + dkbg * eg
        dk = _dot(dm1, c["kb"], "tn", HS) + _dot(dqk, q, "tn", HS) + dk_dec * egl + dkb * beta
        dq = _dot(dqk, k, "nn", HS) + dq_dec * eg
        yield
        dkd_kd = jnp.sum(dk_dec * c["k_dec"], axis=1, keepdims=True)
        dgc = (jnp.sum(xdec, axis=1, keepdims=True) - _dot(xdec, jnp.ones((C, HD), F32), "tn", HS)
               + jnp.sum(dq_dec * c["q_dec"], axis=1, keepdims=True) - dkd_kd
               + jnp.sum(dkbg * c["kbg"], axis=1, keepdims=True))
        dgcl = jnp.sum(dkd_kd, axis=0, keepdims=True) + dgl * gl
        dgc = dgc + jnp.where(c["rows"] == C - 1, dgcl, 0.0)
        ri = lax.broadcasted_iota(jnp.int32, (C, C), 0)
        ci = lax.broadcasted_iota(jnp.int32, (C, C), 1)
        dg = _dot((ci >= ri).astype(F32), dgc, "nn", HI)
        db = jnp.broadcast_to(jnp.sum(dkb * k, axis=1, keepdims=True) + jnp.sum(dvb * v, axis=1, keepdims=True), (C, HD))
        return dq, dk, dvb * beta, dg, db, ds_new

    blk = lambda off: pl.BlockSpec((C, w), lambda h, n, off=off: (N - 1 - n, off + h))
    gspec = pl.BlockSpec((C, HD), lambda h, n: (N - 1 - n, 0))
    ospec = pl.BlockSpec((C, w), lambda h, n: (N - 1 - n, h))
    hspec = pl.BlockSpec((hb, C, HD), lambda h, n: (h, N - 1 - n, 0))
    return pl.pallas_call(
        body, name=name, grid=(NH // hb, N),
        in_specs=[blk(0), blk(NH // hb), blk(2 * NH // hb), gspec, gspec,
                  pl.BlockSpec((hb, 1, HD, HD), lambda h, n: (h, N - 1 - n, 0, 0)), ospec],
        out_specs=[pl.BlockSpec((C, 3 * w), lambda h, n: (N - 1 - n, 0)), hspec, hspec],
        out_shape=[jax.ShapeDtypeStruct((T, 3 * NH * HD), F32)] + [jax.ShapeDtypeStruct((NH, T, HD), F32)] * 2,
        scratch_shapes=[pltpu.VMEM((hb, HD, HD), F32)], compiler_params=_params(2))(qkv, qkv, qkv, g, beta, states, do)


_GATE_CB = GDN_QKV // (NH * HD)


def _gdn_gated_norm_fwd(name, o, proj, ng):
    def fn(o, gate, ng):
        outs = []
        for h in range(NH):
            sl = slice(h * HD, (h + 1) * HD)
            oh, gh = o[:, sl], gate[:, sl]
            outs.append(oh * _rms(oh) * ng * (gh * _sig(gh)))
        return jnp.concatenate(outs, axis=1)
    return _rowwise(name, fn, [o, (proj, NH * HD, _GATE_CB)], [ng], [(NH * HD, BF16)])[0]


def _gdn_gated_norm_bwd(name, don, o, proj, ng):
    def fn(don, o, gate, ng):
        dos, dgs = [], []
        dng = jnp.zeros((1, HD), F32)
        for h in range(NH):
            sl = slice(h * HD, (h + 1) * HD)
            oh, gh, dh = o[:, sl], gate[:, sl], don[:, sl]
            r = _rms(oh)
            xh = oh * r
            sg = _sig(gh)
            dn = dh * (gh * sg)
            dgs.append(dh * (xh * ng) * (sg * (1.0 + gh * (1.0 - sg))))
            dng = dng + jnp.sum(dn * xh, axis=0, keepdims=True)
            dxh = dn * ng
            dos.append(r * (dxh - xh * jnp.mean(dxh * xh, axis=-1, keepdims=True)))
        return jnp.concatenate(dos, axis=1), jnp.concatenate(dgs, axis=1), dng
    return _rowwise(name, fn, [don, o, (proj, NH * HD, _GATE_CB)], [ng], [(NH * HD, F32), (NH * HD, BF16)], sums=[HD])


def _rot(x):
    lane = lax.broadcasted_iota(jnp.int32, x.shape, 1)
    return jnp.where(lane < ROPE // 2, -pltpu.roll(x, HD - ROPE // 2, 1), pltpu.roll(x, ROPE // 2, 1))


def _rot_t(x):
    lane = lax.broadcasted_iota(jnp.int32, x.shape, 1)
    return jnp.where(lane < ROPE // 2, pltpu.roll(x, HD - ROPE // 2, 1), -pltpu.roll(x, ROPE // 2, 1))


def _rope_tables(pos_col):
    lane = jnp.arange(HD)
    inv_freq = ROPE_THETA ** (-(2.0 * (lane % (ROPE // 2)).astype(F32)) / ROPE)
    inv_freq = jnp.where(lane < ROPE, inv_freq, 0.0).astype(F32)[None, :]
    valid = (lane < ROPE).astype(F32)[None, :]

    def fn(pos, inv_freq, valid):
        ang = pos.astype(F32) * inv_freq
        return jnp.cos(ang) * valid, jnp.sin(ang) * valid
    return _rowwise("rope_tables", fn, [pos_col], [inv_freq, valid], [(HD, F32), (HD, F32)])


def _mla_pre_fwd(name, proj, cos, sin, qg, kvg):
    def fn(p, cos, sin, qg, kvg):
        cq, ckv, kr = p[:, :Q_RANK], p[:, Q_RANK:Q_RANK + KV_RANK], p[:, Q_RANK + KV_RANK:]
        return cq * _rms(cq) * qg, ckv * _rms(ckv) * kvg, kr * cos + _rot(kr) * sin
    return _rowwise(name, fn, [proj, cos, sin], [qg, kvg], [(Q_RANK, BF16), (KV_RANK, BF16), (HD, BF16)])


def _rms_bwd(dy, x, g):
    r = _rms(x)
    xh = x * r
    dxh = dy * g
    return r * (dxh - xh * jnp.mean(dxh * xh, axis=-1, keepdims=True)), jnp.sum(dy * xh, axis=0, keepdims=True)


def _mla_pre_bwd(name, proj, dcqn, dckvn, dkr, cos, sin, qg, kvg):
    def fn(p, dcqn, dckvn, dkr, cos, sin, qg, kvg):
        cq, ckv = p[:, :Q_RANK], p[:, Q_RANK:Q_RANK + KV_RANK]
        dcq, dqg = _rms_bwd(dcqn, cq, qg)
        dckv, dkvg = _rms_bwd(dckvn, ckv, kvg)
        dkr_pre = dkr * cos + _rot_t(dkr * sin)
        return jnp.concatenate([dcq, dckv, dkr_pre], axis=1), dqg, dkvg
    return _rowwise(name, fn, [proj, dcqn, dckvn, dkr, cos, sin], [qg, kvg], [(MLA_INK, BF16)], sums=[Q_RANK, KV_RANK])


def _mla_q_fwd(name, q, cos, sin):
    def fn(qn, qr, cos, sin):
        outs = []
        for h in range(NH):
            x = qr[:, h * HD:(h + 1) * HD]
            outs.append(x * cos + _rot(x) * sin)
        return qn, jnp.concatenate(outs, axis=1)
    return _rowwise(name, fn, [(q, NH * HD, 0), (q, NH * HD, 1), cos, sin], [], [(NH * HD, BF16), (NH * HD, BF16)])


def _mla_q_bwd(name, dqn, dqr, cos, sin):
    def fn(dqn, dqr, cos, sin):
        outs = [dqn]
        for h in range(NH):
            z = dqr[:, h * HD:(h + 1) * HD]
            outs.append(z * cos + _rot_t(z * sin))
        return jnp.concatenate(outs, axis=1)
    return _rowwise(name, fn, [dqn, dqr, cos, sin], [], [(2 * NH * HD, BF16)])[0]


def _att_probs(qn, qr, kn, kr, row0):
    s = (_dot(qn, kn, "nt") + _dot(qr, kr, "nt")) * ATT_SCALE
    qpos = row0 + lax.broadcasted_iota(jnp.int32, s.shape, 0)
    kpos = lax.broadcasted_iota(jnp.int32, s.shape, 1)
    s = jnp.where(kpos <= qpos, s, -1e30)
    p = jnp.exp(s - jnp.max(s, axis=1, keepdims=True))
    return p / jnp.sum(p, axis=1, keepdims=True)


def _mla_attn_fwd(name, qn, qr, kv, kr, tq=256):
    T = qn.shape[0]
    tq = min(tq, T)

    def body(qn_ref, qr_ref, kn_ref, v_ref, kr_ref, o_ref):
        i = pl.program_id(1)
        for blk in range(T // tq):
            @pl.when(i == blk)
            def _(blk=blk):
                keys = pl.ds(0, (blk + 1) * tq)
                p = _att_probs(qn_ref[...], qr_ref[...], kn_ref[keys, :], kr_ref[keys, :], blk * tq)
                o_ref[...] = _dot(p.astype(BF16), v_ref[keys, :], "nn").astype(o_ref.dtype)

    qspec = pl.BlockSpec((tq, HD), lambda h, i: (i, h))
    return pl.pallas_call(
        body, name=name, grid=(NH, T // tq),
        in_specs=[qspec, qspec, pl.BlockSpec((T, HD), lambda h, i: (0, h)), pl.BlockSpec((T, HD), lambda h, i: (0, NH + h)),
                  pl.BlockSpec((T, HD), lambda h, i: (0, 0))],
        out_specs=qspec, out_shape=jax.ShapeDtypeStruct((T, NH * HD), BF16), compiler_params=_params(2))(qn, qr, kv, kv, kr)


def _mla_attn_bwd(name, qn, qr, kv, kr, do, tq=256):
    T = qn.shape[0]
    tq = min(tq, T)

    def body(qn_ref, qr_ref, kn_ref, v_ref, kr_ref, do_ref, dqn_ref, dqr_ref, dkn_ref, dv_ref, dkr_ref):
        h, i = pl.program_id(0), pl.program_id(1)

        @pl.when(i == 0)
        def _():
            dkn_ref[...] = jnp.zeros_like(dkn_ref)
            dv_ref[...] = jnp.zeros_like(dv_ref)

        @pl.when((i == 0) & (h == 0))
        def _():
            dkr_ref[...] = jnp.zeros_like(dkr_ref)

        for blk in range(T // tq):
            @pl.when(i == blk)
            def _(blk=blk):
                keys = pl.ds(0, (blk + 1) * tq)
                qn, qr, do = qn_ref[...], qr_ref[...], do_ref[...]
                kn, kr, v = kn_ref[keys, :], kr_ref[keys, :], v_ref[keys, :]
                p = _att_probs(qn, qr, kn, kr, blk * tq)
                dp = _dot(do, v, "nt")
                ds = (p * (dp - jnp.sum(p * dp, axis=1, keepdims=True)) * ATT_SCALE).astype(BF16)
                dqn_ref[...] = _dot(ds, kn, "nn")
                dqr_ref[...] = _dot(ds, kr, "nn")
                dkn_ref[keys, :] += _dot(ds, qn, "tn")
                dkr_ref[keys, :] += _dot(ds, qr, "tn")
                dv_ref[keys, :] += _dot(p.astype(BF16), do, "tn")

    qspec = pl.BlockSpec((tq, HD), lambda h, i: (i, h))
    kspec = pl.BlockSpec((T, HD), lambda h, i: (0, h))
    return pl.pallas_call(
        body, name=name, grid=(NH, T // tq),
        in_specs=[qspec, qspec, kspec, pl.BlockSpec((T, HD), lambda h, i: (0, NH + h)),
                  pl.BlockSpec((T, HD), lambda h, i: (0, 0)), qspec],
        out_specs=[qspec, qspec, kspec, kspec, pl.BlockSpec((T, HD), lambda h, i: (0, 0))],
        out_shape=[jax.ShapeDtypeStruct((T, NH * HD), F32)] * 4 + [jax.ShapeDtypeStruct((T, HD), F32)],
        compiler_params=_params(2))(qn, qr, kv, kv, kr, do)


def _mod_rows(mod, layer):
    return [(mod, layer, i, D) for i in range(N_MOD)]


def _local_step(x, target, pos_col, mod, weights_of, P, on_grads):
    cos, sin = _rope_tables(pos_col)
    saved = []
    for l in range(DEPTH):
        j = l // 2
        sh_m, sc_m, ga_m, sh_f, sc_f, ga_f = _mod_rows(mod, l)
        s = dict(x0=x)
        h = _norm_mod_fwd(f"norm_mix{l}", x, (P["norm_mix_g"], l, 0, D), sc_m, sh_m)
        W = weights_of(l, h)
        s.update(h=h, W=W)
        if l % 2 == 0:
            proj = _mm(f"gdn_in{j}", h, W["gdn_in"], "nn", tn=GDN_INK // 2)
            qkv = _gdn_conv_fwd(f"gdn_conv{j}", proj, P["gdn_cw"][j])
            g, beta = _gdn_gates_fwd(f"gdn_gates{j}", proj, P["gdn_alog"][j], P["gdn_dtb"][j])
            o, states = _gdn_chunk_fwd(f"gdn_chunk{j}", qkv, g, beta)
            on = _gdn_gated_norm_fwd(f"gdn_gnorm{j}", o, proj, P["gdn_ng"][j])
            y = _mm(f"gdn_out{j}", on, W["gdn_out"], "nn")
            s.update(proj=proj, qkv=qkv, g=g, beta=beta, o=o, states=states, on=on)
        else:
            proj = _mm(f"mla_in{j}", h, W["mla_in"], "nn")
            cqn, ckvn, kr = _mla_pre_fwd(f"mla_pre{j}", proj, cos, sin, P["mla_qg"][j], P["mla_kvg"][j])
            q = _mm(f"mla_uq{j}", cqn, W["mla_uq"], "nn")
            kv = _mm(f"mla_ukv{j}", ckvn, W["mla_ukv"], "nn", out_dtype=BF16)
            qn, qr = _mla_q_fwd(f"mla_q{j}", q, cos, sin)
            o = _mla_attn_fwd(f"mla_attn{j}", qn, qr, kv, kr)
            y = _mm(f"mla_out{j}", o, W["mla_out"], "nn")
            s.update(proj=proj, cqn=cqn, ckvn=ckvn, kr=kr, kv=kv, qn=qn, qr=qr, o=o)
        s["y"] = y
        x = _residual_fwd(f"res_mix{l}", x, y, ga_m)
        s["x1"] = x
        h2 = _norm_mod_fwd(f"norm_ffn{l}", x, (P["norm_ffn_g"], l, 0, D), sc_f, sh_f)
        fa, fb, sw = _ffn_up(f"ffn_up{l}", h2, W["ffn_g"], W["ffn_u"], 0)
        yf = _ffn_down(f"ffn_down{l}", sw, W["ffn_d"], 0)
        x = _residual_fwd(f"res_ffn{l}", x, yf, ga_f)
        s.update(h2=h2, fa=fa, fb=fb, sw=sw, yf=yf)
        saved.append(s)

    dx, loss, d_final = _loss_head(x, target, P["final_g"])
    gP = dict(loss=loss, final_g=d_final, norm_mix_g=[None] * DEPTH, norm_ffn_g=[None] * DEPTH,
              gdn_cw=[None] * 2, gdn_alog=[None] * 2, gdn_dtb=[None] * 2, gdn_ng=[None] * 2,
              mla_qg=[None] * 2, mla_kvg=[None] * 2)
    dmod = [None] * DEPTH
    for l in reversed(range(DEPTH)):
        j = l // 2
        s = saved[l]
        W = s["W"]
        sh_m, sc_m, ga_m, sh_f, sc_f, ga_f = _mod_rows(mod, l)
        dyf, d_ga_f = _residual_bwd(f"res_ffn_b{l}", dx, s["yf"], ga_f)
        da, db = _ffn_down_bwd(f"ffn_down_dx{l}", dyf, W["ffn_d"], s["fa"], s["fb"], 0)
        g_down = _ffn_down_dw(f"ffn_down_dw{l}", s["sw"], dyf)
        g_gate, g_up = _ffn_up_dw(f"ffn_up_dw{l}", s["h2"], da, db)
        on_grads(l, "ffn", dict(ffn_w_gate=g_gate, ffn_w_up=g_up, ffn_w_down=g_down))
        dh2 = _ffn_up_dx(f"ffn_up_dx{l}", da, db, W["ffn_g"], W["ffn_u"], 0)
        dx, d_sh_f, d_sc_f, gP["norm_ffn_g"][l] = _norm_mod_bwd(f"norm_ffn_b{l}", dh2, s["x1"], dx,
                                                                 (P["norm_ffn_g"], l, 0, D), sc_f)
        dy, d_ga_m = _residual_bwd(f"res_mix_b{l}", dx, s["y"], ga_m)
        if l % 2 == 0:
            don = _mm(f"gdn_out_dx{j}", dy, W["gdn_out"], "nt")
            g_out = _mm(f"gdn_out_dw{j}", s["on"], dy, "tn", out_dtype=BF16)
            do, dgate, gP["gdn_ng"][j] = _gdn_gated_norm_bwd(f"gdn_gnorm_b{j}", don, s["o"], s["proj"], P["gdn_ng"][j])
            dqkv, dg_h, db_h = _gdn_chunk_bwd(f"gdn_chunk_b{j}", s["qkv"], s["g"], s["beta"], s["states"], do)
            dab_, gP["gdn_alog"][j], gP["gdn_dtb"][j] = _gdn_gates_bwd(f"gdn_gates_b{j}", s["proj"], dg_h, db_h,
                                                                        P["gdn_alog"][j], P["gdn_dtb"][j])
            dpre, gP["gdn_cw"][j] = _gdn_conv_bwd(f"gdn_conv_b{j}", s["proj"], P["gdn_cw"][j], dqkv)
            dproj = jnp.concatenate([dpre, dgate, dab_], axis=1)
            g_in = _mm(f"gdn_in_dw{j}", s["h"], dproj, "tn", out_dtype=BF16, tn=GDN_INK // 2)
            on_grads(l, "mix", dict(gdn_w_in=_uncols(_gdn_in_from_kernel(g_in)), gdn_w_out=_unrows(g_out)))
            dh = _mm(f"gdn_in_dx{j}", dproj, W["gdn_in"], "nt")
        else:
            do = _mm(f"mla_out_dx{j}", dy, W["mla_out"], "nt", out_dtype=BF16)
            g_out = _mm(f"mla_out_dw{j}", s["o"], dy, "tn", out_dtype=BF16)
            dqn, dqr, dkn, dv, dkr = _mla_attn_bwd(f"mla_attn_b{j}", s["qn"], s["qr"], s["kv"], s["kr"], do)
            dq = _mla_q_bwd(f"mla_q_b{j}", dqn, dqr, cos, sin)
            dkv = jnp.concatenate([dkn, dv], axis=1)
            g_uq = _mm(f"mla_uq_dw{j}", s["cqn"], dq, "tn", out_dtype=BF16)
            dcqn = _mm(f"mla_uq_dx{j}", dq, W["mla_uq"], "nt")
            g_ukv = _mm(f"mla_ukv_dw{j}", s["ckvn"], dkv, "tn", out_dtype=BF16)
            dckvn = _mm(f"mla_ukv_dx{j}", dkv, W["mla_ukv"], "nt")
            dproj, gP["mla_qg"][j], gP["mla_kvg"][j] = _mla_pre_bwd(f"mla_pre_b{j}", s["proj"], dcqn, dckvn, dkr, cos, sin,
                                                                     P["mla_qg"][j], P["mla_kvg"][j])
            g_in = _mm(f"mla_in_dw{j}", s["h"], dproj, "tn", out_dtype=BF16)
            on_grads(l, "mix", dict(mla_w_in=_unrows(g_in[:, :Q_RANK + KV_RANK + ROPE]), mla_w_uq=_uncols(_mla_uq_from_kernel(g_uq)),
                                    mla_w_ukv=_uncols(_mla_ukv_from_kernel(g_ukv)), mla_w_out=_unrows(g_out)))
            dh = _mm(f"mla_in_dx{j}", dproj, W["mla_in"], "nt")
        dx, d_sh_m, d_sc_m, gP["norm_mix_g"][l] = _norm_mod_bwd(f"norm_mix_b{l}", dh, s["x0"], dx,
                                                                 (P["norm_mix_g"], l, 0, D), sc_m)
        dmod[l] = jnp.concatenate([d_sh_m, d_sc_m, d_ga_m, d_sh_f, d_sc_f, d_ga_f], axis=1)
    return dx, jnp.concatenate(dmod, axis=0), gP


def _pad_cols(a, width):
    return jnp.pad(a, ((0, 0), (0, width - a.shape[1])))


def _gdn_in_to_kernel(w):
    m = GDN_QKV + NH * HD
    return jnp.concatenate([w[:, :m], _pad_cols(w[:, m:m + NH], HD), _pad_cols(w[:, m + NH:], HD)], axis=1)


def _gdn_in_from_kernel(g):
    m = GDN_QKV + NH * HD
    return jnp.concatenate([g[:, :m], g[:, m:m + NH], g[:, m + HD:m + HD + NH]], axis=1)


def _mla_uq_to_kernel(w):
    w3 = w.reshape(Q_RANK, NH, HD + ROPE)
    rope = jnp.pad(w3[:, :, HD:], ((0, 0), (0, 0), (0, HD - ROPE)))
    return jnp.concatenate([w3[:, :, :HD].reshape(Q_RANK, NH * HD), rope.reshape(Q_RANK, NH * HD)], axis=1)


def _mla_uq_from_kernel(g):
    gn = g[:, :NH * HD].reshape(Q_RANK, NH, HD)
    gr = g[:, NH * HD:].reshape(Q_RANK, NH, HD)[:, :, :ROPE]
    return jnp.concatenate([gn, gr], axis=2).reshape(Q_RANK, NH * (HD + ROPE))


def _mla_ukv_to_kernel(w):
    w3 = w.reshape(KV_RANK, NH, 2 * HD)
    return jnp.concatenate([w3[:, :, :HD].reshape(KV_RANK, NH * HD), w3[:, :, HD:].reshape(KV_RANK, NH * HD)], axis=1)


def _mla_ukv_from_kernel(g):
    gk = g[:, :NH * HD].reshape(KV_RANK, NH, HD)
    gv = g[:, NH * HD:].reshape(KV_RANK, NH, HD)
    return jnp.concatenate([gk, gv], axis=2).reshape(KV_RANK, NH * 2 * HD)


def _cols(t):
    return jnp.moveaxis(t, 0, 1).reshape(t.shape[1], -1)


def _uncols(g):
    return jnp.moveaxis(g.reshape(g.shape[0], 4, -1), 1, 0)


def _rows(t):
    return t.reshape(-1, t.shape[2])


def _unrows(g):
    return g.reshape(4, -1, g.shape[1])


def _layer_weights(layer):
    mixer = ("gdn_w_in", "gdn_w_out") if layer % 2 == 0 else ("mla_w_in", "mla_w_uq", "mla_w_ukv", "mla_w_out")
    return [(n, layer // 2) for n in mixer] + [(n, layer) for n in ("ffn_w_gate", "ffn_w_up", "ffn_w_down")]


def _weights_to_kernel(layer, g):
    out = dict(ffn_g=g["ffn_w_gate"], ffn_u=g["ffn_w_up"], ffn_d=g["ffn_w_down"])
    if layer % 2 == 0:
        out.update(gdn_in=_gdn_in_to_kernel(_cols(g["gdn_w_in"])), gdn_out=_rows(g["gdn_w_out"]))
    else:
        out.update(mla_in=_pad_cols(_rows(g["mla_w_in"]), MLA_INK), mla_uq=_mla_uq_to_kernel(_cols(g["mla_w_uq"])),
                   mla_ukv=_mla_ukv_to_kernel(_cols(g["mla_w_ukv"])), mla_out=_rows(g["mla_w_out"]))
    return out


def _small_to_kernel(norm_mix_g, norm_ffn_g, final_norm_g, gdn_conv_w, gdn_a_log, gdn_dt_bias, gdn_norm_g, q_norm_g, kv_norm_g):
    return dict(
        norm_mix_g=norm_mix_g, norm_ffn_g=norm_ffn_g, final_g=final_norm_g.reshape(1, D),
        gdn_cw=[jnp.transpose(gdn_conv_w[j]) for j in range(2)],
        gdn_alog=[_pad_cols(gdn_a_log[j:j + 1], HD) for j in range(2)],
        gdn_dtb=[_pad_cols(gdn_dt_bias[j:j + 1], HD) for j in range(2)],
        gdn_ng=[gdn_norm_g[j:j + 1] for j in range(2)],
        mla_qg=[q_norm_g[j:j + 1] for j in range(2)],
        mla_kvg=[kv_norm_g[j:j + 1] for j in range(2)],
    )


_CHIP_FLIPS = ((1, 0), (0, 1), (1, 1))
_ANY = pl.BlockSpec(memory_space=pl.ANY)


def _me():
    return lax.axis_index("x"), lax.axis_index("y"), lax.axis_index("c")


def _chip_peer(dx, dy):
    x, y, c = _me()
    return ((1 - x) if dx else x, (1 - y) if dy else y, c)


def _rcopy(src, dst, send_sem, recv_sem, to):
    return pltpu.make_async_remote_copy(src_ref=src, dst_ref=dst, send_sem=send_sem, recv_sem=recv_sem,
                                        device_id=to, device_id_type=MESH)


def _allgather4(name, a, halves=False):
    R, C = a.shape
    rh = R // 2 if halves else R

    def body(a_ref, out_ref, send_sems, recv_sems, local_sem):
        x, y, c = _me()
        me = 2 * x + y
        src = a_ref.at[pl.ds(c * rh, rh)] if halves else a_ref
        local = pltpu.make_async_copy(src, out_ref.at[me], local_sem)
        local.start()
        sends = []
        for k, (dx, dy) in enumerate(_CHIP_FLIPS):
            cp = _rcopy(src, out_ref.at[me], send_sems.at[k], recv_sems.at[k], _chip_peer(dx, dy))
            cp.start()
            sends.append(cp)
        for k, (dx, dy) in enumerate(_CHIP_FLIPS):
            px, py, _ = _chip_peer(dx, dy)
            _rcopy(src, out_ref.at[2 * px + py], send_sems.at[k], recv_sems.at[k], _chip_peer(dx, dy)).wait_recv()
        for cp in sends:
            cp.wait_send()
        local.wait()

    return pl.pallas_call(
        body, name=name, in_specs=[_ANY], out_specs=_ANY, out_shape=jax.ShapeDtypeStruct((4, rh, C), a.dtype),
        scratch_shapes=[pltpu.SemaphoreType.DMA((3,)), pltpu.SemaphoreType.DMA((3,)), pltpu.SemaphoreType.DMA(())])(a)


_NCH = 4


def _dma_sems(*counts):
    return [pltpu.SemaphoreType.DMA((n,)) for n in counts]


def _slot_tile(rows, cap=512):
    best = rows
    for tr in range(16, min(rows, cap) + 1, 16):
        if rows % tr == 0:
            best = tr
    return best


def _cast_into_slot(name, a, chip, row0, rows):
    C = a.shape[1]
    tr = _slot_tile(rows)
    assert row0 % tr == 0
    first = row0 // tr

    def body(c_ref, a_ref, o_ref):
        o_ref[0] = a_ref[...].astype(o_ref.dtype)

    grid_spec = pltpu.PrefetchScalarGridSpec(
        num_scalar_prefetch=1, grid=(rows // tr,), in_specs=[pl.BlockSpec((tr, C), lambda i, c_ref: (first + i, 0))],
        out_specs=pl.BlockSpec((1, tr, C), lambda i, c_ref: (c_ref[0], i, 0)))
    return pl.pallas_call(body, name=name, grid_spec=grid_spec, out_shape=jax.ShapeDtypeStruct((4, rows, C), BF16),
                          compiler_params=_params(1))(chip, a)


def _chunks(rows, align):
    for nch in (_NCH, 2):
        if rows % (nch * align) == 0:
            return nch
    return 1


def _gather_exchange(out, ici_s, ici_r, d2d_s, d2d_r):
    n = len(out)
    x, y, c = _me()
    me = 2 * x + y
    sib = (x, y, 1 - c)
    peers = [_chip_peer(dx, dy) for dx, dy in _CHIP_FLIPS]
    for t in range(n):
        h = out[t].shape[1] // 2
        nch = _chunks(h, 16)
        ch = h // nch
        for k, peer in enumerate(peers):
            for i in range(nch):
                blk = out[t].at[me, pl.ds(c * h + i * ch, ch)]
                _rcopy(blk, blk, ici_s.at[3 * t + k], ici_r.at[3 * t + k], peer).start()
    for t in range(n):
        h = out[t].shape[1] // 2
        nch = _chunks(h, 16)
        ch = h // nch
        for k, peer in enumerate(peers):
            pchip = 2 * peer[0] + peer[1]
            got = out[t].at[pchip, pl.ds(c * h, h)]
            _rcopy(got, got, ici_s.at[3 * t + k], ici_r.at[3 * t + k], peer).wait_recv()
            for i in range(nch):
                blk = out[t].at[pchip, pl.ds(c * h + i * ch, ch)]
                _rcopy(blk, blk, d2d_s.at[3 * t + k], d2d_r.at[3 * t + k], sib).start()
    for t in range(n):
        h = out[t].shape[1] // 2
        for k, peer in enumerate(peers):
            pchip = 2 * peer[0] + peer[1]
            other = out[t].at[pchip, pl.ds((1 - c) * h, h)]
            _rcopy(other, other, d2d_s.at[3 * t + k], d2d_r.at[3 * t + k], sib).wait_recv()
            _rcopy(other, other, ici_s.at[3 * t + k], ici_r.at[3 * t + k], peer).wait_send()
            _rcopy(other, other, d2d_s.at[3 * t + k], d2d_r.at[3 * t + k], sib).wait_send()


def _gather_weights(name, bufs):
    n = len(bufs)

    def body(*refs):
        _gather_exchange(refs[n:2 * n], *refs[2 * n:])

    return pl.pallas_call(
        body, name=name, in_specs=[_ANY] * n, out_specs=[_ANY] * n,
        out_shape=[jax.ShapeDtypeStruct(s.shape, s.dtype) for s in bufs],
        input_output_aliases={t: t for t in range(n)},
        scratch_shapes=_dma_sems(3 * n, 3 * n, 3 * n, 3 * n))(*bufs)


def _gather_weights_async(name, collective_id, bufs):
    n = len(bufs)
    refs = [jax.new_ref(b, memory_space=pltpu.MemorySpace.HBM) for b in bufs]

    @pl.kernel(mesh=plsc.ScalarSubcoreMesh(axis_name="sequencer", num_cores=1), name=name,
               scratch_types=tuple(_dma_sems(3 * n, 3 * n, 3 * n, 3 * n)),
               compiler_params=pltpu.CompilerParams(collective_id=collective_id))
    def launch(ici_s, ici_r, d2d_s, d2d_r):
        x, y, c = _me()
        barrier = pltpu.get_barrier_semaphore()
        for peer in [_chip_peer(dx, dy) for dx, dy in _CHIP_FLIPS] + [(x, y, 1 - c)]:
            pl.semaphore_signal(barrier, inc=1, device_id=peer, device_id_type=MESH)
        pl.semaphore_wait(barrier, 4)
        _gather_exchange(refs, ici_s, ici_r, d2d_s, d2d_r)

    launch()
    return [r[...] for r in refs]


def _rs_split(name, grads):
    n = len(grads)

    def body(*refs):
        g, out = refs[:n], refs[n:2 * n]
        send, recv = refs[2 * n:]
        x, y, c = _me()
        sib = (x, y, 1 - c)
        for t in range(n):
            h = g[t].shape[1] // 2
            for d in range(4):
                _rcopy(g[t].at[d, pl.ds((1 - c) * h, h)], out[t].at[d], send.at[t], recv.at[t], sib).start()
        for t in range(n):
            _rcopy(out[t], out[t], send.at[t], recv.at[t], sib).wait()

    return pl.pallas_call(
        body, name=name, in_specs=[_ANY] * n, out_specs=[_ANY] * n,
        out_shape=[jax.ShapeDtypeStruct((4, s.shape[1] // 2, s.shape[2]), s.dtype) for s in grads],
        scratch_shapes=_dma_sems(n, n))(*grads)


def _pair_add(name, g, theirs, core_chip):
    _, R, C = g.shape
    h = R // 2
    tr = _slot_tile(h)
    nb = h // tr

    def body(s_ref, g_ref, t_ref, p_ref, o_ref):
        val = (g_ref[...].astype(F32) + t_ref[...].astype(F32)).astype(p_ref.dtype)
        p_ref[...] = val

        @pl.when(pl.program_id(1) == s_ref[1])
        def _():
            o_ref[...] = val

    spec = pl.BlockSpec((1, tr, C), lambda i, d, s_ref: (d, i, 0))
    grid_spec = pltpu.PrefetchScalarGridSpec(
        num_scalar_prefetch=1, grid=(nb, 4),
        in_specs=[pl.BlockSpec((1, tr, C), lambda i, d, s_ref: (d, s_ref[0] * nb + i, 0)), spec],
        out_specs=[spec, pl.BlockSpec((1, tr, C), lambda i, d, s_ref: (s_ref[1], i, 0))])
    half = jax.ShapeDtypeStruct((4, h, C), BF16)
    return pl.pallas_call(body, name=name, grid_spec=grid_spec, out_shape=[half, half],
                          compiler_params=_params(2))(core_chip, g, theirs)


def _rs_alltoall_async(name, collective_id, parts, bufs):
    n = len(parts)
    p = [jax.new_ref(a, memory_space=pltpu.MemorySpace.HBM) for a in parts]
    out = [jax.new_ref(b, memory_space=pltpu.MemorySpace.HBM) for b in bufs]

    @pl.kernel(mesh=plsc.ScalarSubcoreMesh(axis_name="sequencer", num_cores=1), name=name,
               scratch_types=tuple(_dma_sems(3 * n, 3 * n)),
               compiler_params=pltpu.CompilerParams(collective_id=collective_id))
    def launch(send, recv):
        barrier = pltpu.get_barrier_semaphore()
        for peer in [_chip_peer(dx, dy) for dx, dy in _CHIP_FLIPS]:
            pl.semaphore_signal(barrier, inc=1, device_id=peer, device_id_type=MESH)
        pl.semaphore_wait(barrier, 3)
        _alltoall_exchange(p, out, send, recv)

    launch()
    return [r[...] for r in out]


def _alltoall_exchange(p, out, send, recv):
    x, y, c = _me()
    me = 2 * x + y
    peers = [_chip_peer(dx, dy) for dx, dy in _CHIP_FLIPS]
    for t in range(len(p)):
        h = p[t].shape[1]
        nch = _chunks(h, 16)
        ch = h // nch
        for k, peer in enumerate(peers):
            pchip = 2 * peer[0] + peer[1]
            for i in range(nch):
                rows = pl.ds(i * ch, ch)
                _rcopy(p[t].at[pchip, rows], out[t].at[me, rows], send.at[3 * t + k], recv.at[3 * t + k], peer).start()
    for t in range(len(p)):
        for k, peer in enumerate(peers):
            pchip = 2 * peer[0] + peer[1]
            _rcopy(out[t].at[pchip], out[t].at[pchip], send.at[3 * t + k], recv.at[3 * t + k], peer).wait()


def _rs_swap(name, halves):
    n = len(halves)

    def body(*refs):
        a, out = refs[:n], refs[n:2 * n]
        send, recv = refs[2 * n:]
        x, y, c = _me()
        sib = (x, y, 1 - c)
        for t in range(n):
            ch = a[t].shape[0] // _NCH
            for i in range(_NCH):
                rows = pl.ds(i * ch, ch)
                _rcopy(a[t].at[rows], out[t].at[rows], send.at[t], recv.at[t], sib).start()
        for t in range(n):
            _rcopy(a[t], out[t], send.at[t], recv.at[t], sib).wait()

    return pl.pallas_call(
        body, name=name, in_specs=[_ANY] * n, out_specs=[_ANY] * n,
        out_shape=[jax.ShapeDtypeStruct(s.shape, s.dtype) for s in halves],
        scratch_shapes=_dma_sems(n, n))(*halves)


def _sibling_merge(name, a):
    P_, rh, C = a.shape

    def body(a_ref, out_ref, send_sem, recv_sem, local_sem):
        x, y, c = _me()
        local = pltpu.make_async_copy(a_ref, out_ref.at[:, pl.ds(c * rh, rh)], local_sem)
        local.start()
        cp = _rcopy(a_ref, out_ref.at[:, pl.ds(c * rh, rh)], send_sem, recv_sem, (x, y, 1 - c))
        cp.start()
        cp.wait_send()
        _rcopy(a_ref, out_ref.at[:, pl.ds((1 - c) * rh, rh)], send_sem, recv_sem, (x, y, 1 - c)).wait_recv()
        local.wait()

    return pl.pallas_call(
        body, name=name, in_specs=[_ANY], out_specs=_ANY, out_shape=jax.ShapeDtypeStruct((P_, 2 * rh, C), a.dtype),
        scratch_shapes=[pltpu.SemaphoreType.DMA(()), pltpu.SemaphoreType.DMA(()), pltpu.SemaphoreType.DMA(())])(a)


def _allgather8(name, a):
    g4 = _allgather4(name + "_chips", a)
    both = _sibling_merge(name + "_cores", g4.reshape(1, 4 * a.shape[0], a.shape[1]))
    return jnp.transpose(both.reshape(2, 4, *a.shape), (1, 0, 2, 3)).reshape(8, *a.shape)


def _sum_slots(name, a, out_dtype):
    def fn(a):
        acc = a[0].astype(F32)
        for k in range(1, a.shape[0]):
            acc = acc + a[k].astype(F32)
        return acc
    return _rowwise(name, fn, [a], [], [(a.shape[2], out_dtype)])[0]


def _adamw_math(w, g, m, v):
    m = ADAM_B1 * m + (1.0 - ADAM_B1) * g
    v = ADAM_B2 * v + (1.0 - ADAM_B2) * (g * g)
    m_hat = m / (1.0 - ADAM_B1 ** ADAM_STEP)
    v_hat = v / (1.0 - ADAM_B2 ** ADAM_STEP)
    return -ADAM_LR * (m_hat / (jnp.sqrt(v_hat) + ADAM_EPS) + ADAM_WD * w), m, v


def _adamw_piece(name, w2, m2, v2, mine, theirs, row0, prev, core):
    R, C = w2.shape
    h = mine.shape[0]
    tr = _slot_tile(h, 256)
    nb = h // tr
    assert row0 % tr == 0
    first = row0 // tr

    def body(c_ref, w_ref, m_ref, v_ref, a_ref, b_ref, *rest):
        g_ref, d_ref, nm_ref, nv_ref = rest[-4:]
        g = jnp.where(pl.program_id(0) == c_ref[0], a_ref[...], b_ref[...])
        g_ref[...] = g
        d_ref[...], nm_ref[...], nv_ref[...] = _adamw_math(w_ref[...], g, m_ref[...], v_ref[...])

    full = pl.BlockSpec((tr, C), lambda s, i, c_ref: (first + s * nb + i, 0))
    mine_spec = pl.BlockSpec((tr, C), lambda s, i, c_ref: (jnp.where(s == c_ref[0], i, 0), 0))
    theirs_spec = pl.BlockSpec((tr, C), lambda s, i, c_ref: (jnp.where(s == c_ref[0], 0, i), 0))
    extra = [] if prev is None else list(prev)
    grid_spec = pltpu.PrefetchScalarGridSpec(
        num_scalar_prefetch=1, grid=(2, nb), in_specs=[full, full, full, mine_spec, theirs_spec] + [_ANY] * len(extra),
        out_specs=[full] * 4)
    return pl.pallas_call(
        body, name=name, grid_spec=grid_spec, out_shape=[jax.ShapeDtypeStruct((R, C), F32)] * 4,
        input_output_aliases={6 + k: k for k in range(len(extra))}, compiler_params=_params(2))(core, w2, m2, v2, mine, theirs, *extra)


def _adamw(name, w, g, m, v):
    shape = w.shape
    two_d = (-1, shape[-1]) if w.ndim > 1 else (1, -1)
    w2, g2, m2, v2 = [t.reshape(two_d) for t in (w, g, m, v)]
    rows = w2.shape[0]
    tr = rows
    for cand in (256, 128, 64, 32, 16, 8):
        if rows % cand == 0:
            tr = cand
            break

    c = w2.shape[1]
    outs = _rowwise(name, _adamw_math, [w2, g2, m2, v2], [], [(c, F32)] * 3, tr=tr)
    return [o.reshape(shape) for o in outs]


_WEIGHT_ORDER = ("ada_w", "ada_b", "norm_mix_g", "norm_ffn_g", "gdn_w_in", "gdn_conv_w", "gdn_a_log", "gdn_dt_bias",
                 "gdn_norm_g", "gdn_w_out", "mla_w_in", "mla_q_norm_g", "mla_kv_norm_g", "mla_w_uq", "mla_w_ukv",
                 "mla_w_out", "ffn_w_gate", "ffn_w_up", "ffn_w_down", "final_norm_g")
_BIG = (("gdn_w_in", 2), ("gdn_w_out", 1), ("mla_w_in", 1), ("mla_w_uq", 2), ("mla_w_ukv", 2), ("mla_w_out", 1),
        ("ffn_w_gate", 2), ("ffn_w_up", 2), ("ffn_w_down", 1))
_SMALL_SHARDED = (("gdn_conv_w", 1), ("mla_q_norm_g", 1), ("mla_kv_norm_g", 1))
_STORED_TRANSPOSED = ("ffn_w_gate", "ffn_w_up")


def _size(shape):
    n = 1
    for s in shape:
        n *= s
    return n


def _pack_rows_each(tensors):
    parts, offs, off = [], [], 0
    for t in tensors:
        flat = t.reshape(-1).astype(F32)
        rows = -(-flat.shape[0] // PACK_W)
        parts.append(jnp.pad(flat, (0, rows * PACK_W - flat.shape[0])).reshape(rows, PACK_W))
        offs.append(off)
        off += rows
    total = -(-off // 16) * 16
    pack = jnp.pad(parts[0], ((offs[0], total - offs[0] - parts[0].shape[0]), (0, 0)))
    for p, o in zip(parts[1:], offs[1:]):
        pack = pack + jnp.pad(p, ((o, total - o - p.shape[0]), (0, 0)))
    return pack, offs


def _unpack_rows_each(pack, shapes):
    lead = pack.shape[:-2]
    out, off = [], 0
    for shp in shapes:
        n = _size(shp)
        rows = -(-n // PACK_W)
        out.append(pack[..., off:off + rows, :].reshape(*lead, -1)[..., :n].reshape(*lead, *shp))
        off += rows
    return out


def _merge_chips(stacked, axis):
    moved = jnp.moveaxis(stacked, 0, axis)
    shp = list(moved.shape)
    return moved.reshape(shp[:axis] + [shp[axis] * shp[axis + 1]] + shp[axis + 2:])


def _my_shard(full, axis, chip):
    n = full.shape[axis] // 4
    return lax.dynamic_slice_in_dim(full, chip * n, n, axis)


def kernel(x, c, positions, ada_w, ada_b, norm_mix_g, norm_ffn_g, gdn_w_in, gdn_conv_w, gdn_a_log, gdn_dt_bias, gdn_norm_g, gdn_w_out, mla_w_in, mla_q_norm_g, mla_kv_norm_g, mla_w_uq, mla_w_ukv, mla_w_out, ffn_w_gate, ffn_w_up, ffn_w_down, final_norm_g, loss_target, m_ada_w, m_ada_b, m_norm_mix_g, m_norm_ffn_g, m_gdn_w_in, m_gdn_conv_w, m_gdn_a_log, m_gdn_dt_bias, m_gdn_norm_g, m_gdn_w_out, m_mla_w_in, m_mla_q_norm_g, m_mla_kv_norm_g, m_mla_w_uq, m_mla_w_ukv, m_mla_w_out, m_ffn_w_gate, m_ffn_w_up, m_ffn_w_down, m_final_norm_g, v_ada_w, v_ada_b, v_norm_mix_g, v_norm_ffn_g, v_gdn_w_in, v_gdn_conv_w, v_gdn_a_log, v_gdn_dt_bias, v_gdn_norm_g, v_gdn_w_out, v_mla_w_in, v_mla_q_norm_g, v_mla_kv_norm_g, v_mla_w_uq, v_mla_w_ukv, v_mla_w_out, v_ffn_w_gate, v_ffn_w_up, v_ffn_w_down, v_final_norm_g):
    w = dict(ada_w=ada_w, ada_b=ada_b, norm_mix_g=norm_mix_g, norm_ffn_g=norm_ffn_g, gdn_w_in=gdn_w_in, gdn_conv_w=gdn_conv_w,
             gdn_a_log=gdn_a_log, gdn_dt_bias=gdn_dt_bias, gdn_norm_g=gdn_norm_g, gdn_w_out=gdn_w_out, mla_w_in=mla_w_in,
             mla_q_norm_g=mla_q_norm_g, mla_kv_norm_g=mla_kv_norm_g, mla_w_uq=mla_w_uq, mla_w_ukv=mla_w_ukv,
             mla_w_out=mla_w_out, ffn_w_gate=ffn_w_gate, ffn_w_up=ffn_w_up, ffn_w_down=ffn_w_down, final_norm_g=final_norm_g)
    m = dict(ada_w=m_ada_w, ada_b=m_ada_b, norm_mix_g=m_norm_mix_g, norm_ffn_g=m_norm_ffn_g, gdn_w_in=m_gdn_w_in,
             gdn_conv_w=m_gdn_conv_w, gdn_a_log=m_gdn_a_log, gdn_dt_bias=m_gdn_dt_bias, gdn_norm_g=m_gdn_norm_g,
             gdn_w_out=m_gdn_w_out, mla_w_in=m_mla_w_in, mla_q_norm_g=m_mla_q_norm_g, mla_kv_norm_g=m_mla_kv_norm_g,
             mla_w_uq=m_mla_w_uq, mla_w_ukv=m_mla_w_ukv, mla_w_out=m_mla_w_out, ffn_w_gate=m_ffn_w_gate,
             ffn_w_up=m_ffn_w_up, ffn_w_down=m_ffn_w_down, final_norm_g=m_final_norm_g)
    v = dict(ada_w=v_ada_w, ada_b=v_ada_b, norm_mix_g=v_norm_mix_g, norm_ffn_g=v_norm_ffn_g, gdn_w_in=v_gdn_w_in,
             gdn_conv_w=v_gdn_conv_w, gdn_a_log=v_gdn_a_log, gdn_dt_bias=v_gdn_dt_bias, gdn_norm_g=v_gdn_norm_g,
             gdn_w_out=v_gdn_w_out, mla_w_in=v_mla_w_in, mla_q_norm_g=v_mla_q_norm_g, mla_kv_norm_g=v_mla_kv_norm_g,
             mla_w_uq=v_mla_w_uq, mla_w_ukv=v_mla_w_ukv, mla_w_out=v_mla_w_out, ffn_w_gate=v_ffn_w_gate,
             ffn_w_up=v_ffn_w_up, ffn_w_down=v_ffn_w_down, final_norm_g=v_final_norm_g)
    T = x.shape[1]
    ix, iy, ic = _me()
    chip = 2 * ix + iy
    seq = 2 * chip + ic
    n_dev = 8

    small_shapes = [w[n].shape for n, _ in _SMALL_SHARDED] + [c.shape]
    pack0, _ = _pack_rows_each([w[n] for n, _ in _SMALL_SHARDED] + [c])
    got0 = _unpack_rows_each(_allgather8("gather_small", pack0), small_shapes)
    small_full = {n: _merge_chips(g[0::2], ax) for (n, ax), g in zip(_SMALL_SHARDED, got0)}
    c_all = got0[-1].reshape(n_dev, D)

    big = [n for n, _ in _BIG]
    chip_arr = chip.astype(jnp.int32).reshape(1)

    def stored(n, t):
        return jnp.swapaxes(t, 1, 2) if n in _STORED_TRANSPOSED else t

    ws, ms, vs = [{n: stored(n, d[n]) for n in big} for d in (w, m, v)]
    two_d = lambda t: t.reshape(-1, t.shape[-1])

    gathered = []
    for l in range(DEPTH):
        names = _layer_weights(l)
        bufs = [_cast_into_slot(f"to_bf16_{n}{l}", two_d(ws[n]), chip_arr, j * ws[n].shape[1], ws[n].shape[1]) for n, j in names]
        filled = _gather_weights("gather_weights0", bufs) if l == 0 else _gather_weights_async(f"gather_weights{l}", l, bufs)
        gathered.append({n: b for (n, _), b in zip(names, filled)})

    def weights_of(l, h):
        return _weights_to_kernel(l, gathered[l])

    P = _small_to_kernel(norm_mix_g, norm_ffn_g, final_norm_g, small_full["gdn_conv_w"], gdn_a_log, gdn_dt_bias,
                         gdn_norm_g, small_full["mla_q_norm_g"], small_full["mla_kv_norm_g"])

    c16 = jnp.pad(c_all, ((0, 16 - n_dev), (0, 0)))
    ca = _rowwise("cond_silu", lambda t: t * _sig(t), [c16], [], [(D, BF16)])[0]
    n_ada = ada_w.shape[2]
    mods = jnp.concatenate([_mm(f"ada_fwd{l}", ca, ada_w[l], "nn") for l in range(DEPTH)], axis=0)
    mods_all = _allgather4("gather_mod", mods).reshape(4, DEPTH, 16, n_ada)
    mod_mm = jnp.transpose(lax.dynamic_index_in_dim(mods_all, seq, axis=2, keepdims=False), (1, 0, 2)).reshape(DEPTH, 4 * n_ada)
    mod = _rowwise("mod_bias", lambda a, b: a + b, [mod_mm, ada_b], [], [(4 * n_ada, F32)])[0]

    core_chip = jnp.stack([ic, chip]).astype(jnp.int32)
    pending, in_flight = {}, []

    def reduce_group(layer, part, pieces):
        pending.update({(n, layer if n.startswith("ffn_") else layer // 2): g for n, g in pieces.items()})
        if part == "ffn" and layer > 0:
            return
        keys = list(pending)
        glist = [pending.pop(k) for k in keys]
        tag = f"{layer}{part}"
        theirs = _rs_split("grads_cores_" + tag, glist)
        both = [_pair_add(f"grads_pair_{n}{l}", g, t, core_chip) for (n, l), g, t in zip(keys, glist, theirs)]
        swapped = _rs_alltoall_async("grads_chips_" + tag, DEPTH + 1 + len(in_flight), [p for p, _ in both], [o for _, o in both])
        in_flight.append((tag, keys, swapped))

    dx, dmod, gP = _local_step(x.reshape(T, D), loss_target.reshape(T, D), positions.reshape(T, 1), mod, weights_of, P, reduce_group)

    partials = [dmod, jnp.concatenate(gP["norm_mix_g"]), jnp.concatenate(gP["norm_ffn_g"]), gP["final_g"],
                jnp.stack([jnp.transpose(g) for g in gP["gdn_cw"]]), jnp.concatenate(gP["gdn_alog"])[:, :NH],
                jnp.concatenate(gP["gdn_dtb"])[:, :NH], jnp.concatenate(gP["gdn_ng"]), jnp.concatenate(gP["mla_qg"]),
                jnp.concatenate(gP["mla_kvg"]), gP["loss"][:, :1]]
    part_shapes = [p.shape for p in partials]
    ppack, _ = _pack_rows_each(partials)
    pall = _allgather8("gather_partials", ppack)
    psum = _sum_slots("sum_partials", pall, F32)
    (g_ada_b, g_norm_mix, g_norm_ffn, g_final, g_conv_full, g_alog, g_dtb, g_gdn_ng, g_qg_full, g_kvg_full,
     loss_sum) = _unpack_rows_each(psum, part_shapes)
    dmod_all = _unpack_rows_each(pall, part_shapes[:1])[0]

    grads = dict(ada_b=g_ada_b, norm_mix_g=g_norm_mix, norm_ffn_g=g_norm_ffn, final_norm_g=g_final.reshape(D),
                 gdn_conv_w=_my_shard(g_conv_full, 1, chip), gdn_a_log=g_alog, gdn_dt_bias=g_dtb, gdn_norm_g=g_gdn_ng,
                 mla_q_norm_g=_my_shard(g_qg_full, 1, chip), mla_kv_norm_g=_my_shard(g_kvg_full, 1, chip))

    ca_t = jnp.zeros((D, LANES), BF16).at[:, :16].set(jnp.transpose(ca))
    dm_mine = lax.dynamic_slice_in_dim(dmod_all, chip * n_ada, n_ada, axis=2)
    grads["ada_w"] = jnp.stack([
        _mm(f"ada_bwd{l}", ca_t, jnp.pad(dm_mine[:, l], ((0, LANES - n_dev), (0, 0))), "nn") for l in range(DEPTH)])

    delta, new_m, new_v = {}, {}, {}
    results = {}
    keys = [k for _, ks, _ in in_flight for k in ks]
    halves = [_sum_slots(f"grads_sum_{n}{l}", s, F32) for _, ks, sw in in_flight for (n, l), s in zip(ks, sw)]
    others = _rs_swap("grads_swap", halves)
    for (n, l), mine, theirs in zip(keys, halves, others):
        results[n] = _adamw_piece(f"adamw_{n}{l}", two_d(ws[n]), two_d(ms[n]), two_d(vs[n]), mine, theirs,
                                  l * ws[n].shape[1], results.get(n), core_chip[:1])
    for n in big:
        grads[n], delta[n], new_m[n], new_v[n] = [stored(n, t.reshape(ws[n].shape)) for t in results[n]]
    delta["ada_w"], new_m["ada_w"], new_v["ada_w"] = _adamw("adamw_ada_w", ada_w, grads["ada_w"], m_ada_w, v_ada_w)
    small_names = [n for n in _WEIGHT_ORDER if n not in delta]
    small_shapes = [w[n].shape for n in small_names]
    packs = [_pack_rows_each([d[n] for n in small_names])[0] for d in (w, grads, m, v)]
    for d, pk in zip((delta, new_m, new_v), _adamw("adamw_small", *packs)):
        for n, t in zip(small_names, _unpack_rows_each(pk, small_shapes)):
            d[n] = t

    loss = loss_sum.reshape(())
    return (loss, dx.reshape(1, T, D), *[grads[n] for n in _WEIGHT_ORDER], *[delta[n] for n in _WEIGHT_ORDER],
            *[new_m[n] for n in _WEIGHT_ORDER], *[new_v[n] for n in _WEIGHT_ORDER])
```

```python
import functools

import jax
import jax.numpy as jnp
from jax import lax
from jax.experimental import pallas as pl
from jax.experimental.pallas import tpu as pltpu
from jax.experimental.pallas import tpu_sc as plsc

F32 = jnp.float32
BF16 = jnp.bfloat16
HI = lax.Precision.HIGHEST
MESH = pl.DeviceIdType.MESH

D = 1024
DEPTH = 4
N_MOD = 6
NH = 8
HD = 128
CHUNK = 64
_GDN_HB = 8
GDN_QKV = 3 * NH * HD
GDN_INK = GDN_QKV + NH * HD + 2 * HD
Q_RANK, KV_RANK, ROPE = 384, 256, 64
MLA_INK = Q_RANK + KV_RANK + HD
DFF = 2816
EPS = 1e-6
ATT_SCALE = (HD + ROPE) ** -0.5
ROPE_THETA = 10000.0
LANES = 128
PACK_W = 1024

ADAM_LR, ADAM_B1, ADAM_B2, ADAM_EPS, ADAM_WD, ADAM_STEP = 0.001, 0.9, 0.999, 1e-08, 0.01, 10


H3 = "bf16x3"
B1 = "bf16"
HS = H3
HF = B1


def _dot(a, b, mode="nn", prec=None):
    dn = {"nn": (((1,), (0,)), ((), ())), "nt": (((1,), (1,)), ((), ())), "tn": (((0,), (0,)), ((), ()))}[mode]
    if prec == B1:
        return _dot(a.astype(BF16), b.astype(BF16), mode)
    if prec == H3:
        ah, bh = a.astype(BF16), b.astype(BF16)
        al, bl = (a - ah.astype(F32)).astype(BF16), (b - bh.astype(F32)).astype(BF16)
        return _dot(ah, bh, mode) + (_dot(ah, bl, mode) + _dot(al, bh, mode))
    return lax.dot_general(a, b, dn, precision=prec, preferred_element_type=F32)


def _sig(x):
    return 1.0 / (1.0 + jnp.exp(-x))


def _pick(n, cap):
    if n <= cap:
        return n
    best = None
    for d in range(LANES, cap + 1, LANES):
        if n % d == 0:
            best = d
    assert best is not None, (n, cap)
    return best


def _params(n_grid):
    return pltpu.CompilerParams(dimension_semantics=("arbitrary",) * n_grid, vmem_limit_bytes=56 * 1024 * 1024)


def _rowwise(name, fn, rows, consts, outs, sums=(), tr=256):
    first = rows[0][0] if isinstance(rows[0], tuple) else rows[0]
    T = first.shape[-2]
    tr = _slot_tile(T, tr)
    nr, nc, no, ns = len(rows), len(consts), len(outs), len(sums)

    windows = [c[1:] if isinstance(c, tuple) else None for c in consts]
    consts = [c[0] if isinstance(c, tuple) else c for c in consts]

    def body(*refs):
        vals = [r[...] for r in refs[:nr]]
        for r, win in zip(refs[nr:nr + nc], windows):
            vals.append(r[...] if win is None else r[win[0]:win[0] + 1, win[1] * win[2]:(win[1] + 1) * win[2]])
        res = fn(*vals)
        if not isinstance(res, (tuple, list)):
            res = (res,)
        o_refs = refs[nr + nc:nr + nc + no]
        s_refs = refs[nr + nc + no:]
        for r, val in zip(o_refs, res[:no]):
            r[...] = val.astype(r.dtype)
        if ns:
            @pl.when(pl.program_id(0) == 0)
            def _():
                for r in s_refs:
                    r[...] = jnp.zeros_like(r)
            for r, val in zip(s_refs, res[no:]):
                r[...] += val

    in_specs, args = [], []
    for a in rows:
        if isinstance(a, tuple):
            arr, width, cb = a
            in_specs.append(pl.BlockSpec((tr, width), lambda i, cb=cb: (i, cb)))
            args.append(arr)
        elif a.ndim == 3:
            in_specs.append(pl.BlockSpec((a.shape[0], tr, a.shape[2]), lambda i: (0, i, 0)))
            args.append(a)
        else:
            in_specs.append(pl.BlockSpec((tr, a.shape[1]), lambda i: (i, 0)))
            args.append(a)
    for a in consts:
        in_specs.append(pl.BlockSpec(a.shape, lambda i, nd=a.ndim: (0,) * nd))
        args.append(a)
    out_specs = [pl.BlockSpec((tr, w), lambda i: (i, 0)) for w, _ in outs]
    out_specs += [pl.BlockSpec((1, w), lambda i: (0, 0)) for w in sums]
    out_shape = [jax.ShapeDtypeStruct((T, w), dt) for w, dt in outs]
    out_shape += [jax.ShapeDtypeStruct((1, w), F32) for w in sums]
    res = pl.pallas_call(body, name=name, grid=(T // tr,), in_specs=in_specs, out_specs=out_specs,
                         out_shape=out_shape, compiler_params=_params(1))(*args)
    return res


def _mm(name, a, b, mode, out_dtype=F32, tm=512, tn=1024):
    if mode == "tn":
        K, M = a.shape
    else:
        M, K = a.shape
    N = b.shape[0] if mode == "nt" else b.shape[1]
    tm, tn = _pick(M, tm), _pick(N, tn)

    def body(a_ref, b_ref, o_ref):
        o_ref[...] = _dot(a_ref[...].astype(BF16), b_ref[...].astype(BF16), mode).astype(o_ref.dtype)

    a_spec = pl.BlockSpec((K, tm), lambda i, j: (0, i)) if mode == "tn" else pl.BlockSpec((tm, K), lambda i, j: (i, 0))
    b_spec = pl.BlockSpec((tn, K), lambda i, j: (j, 0)) if mode == "nt" else pl.BlockSpec((K, tn), lambda i, j: (0, j))
    return pl.pallas_call(body, name=name, grid=(M // tm, N // tn), in_specs=[a_spec, b_spec],
                          out_specs=pl.BlockSpec((tm, tn), lambda i, j: (i, j)),
                          out_shape=jax.ShapeDtypeStruct((M, N), out_dtype), compiler_params=_params(2))(a, b)


def _rms(x, eps=EPS):
    return lax.rsqrt(jnp.mean(x * x, axis=-1, keepdims=True) + eps)


def _norm_mod_fwd(name, x, g, scale, shift):
    def fn(x, g, scale, shift):
        return x * _rms(x) * g * (1.0 + scale) + shift
    return _rowwise(name, fn, [x], [g, scale, shift], [(D, BF16)])[0]


def _norm_mod_bwd(name, dh, x, dx_res, g, scale):
    def fn(dh, x, dx_res, g, scale):
        r = _rms(x)
        xh = x * r
        dxh = dh * (g * (1.0 + scale))
        dx = r * (dxh - xh * jnp.mean(dxh * xh, axis=-1, keepdims=True))
        dhx = dh * xh
        return (dx_res + dx, jnp.sum(dh, axis=0, keepdims=True), jnp.sum(dhx * g, axis=0, keepdims=True),
                jnp.sum(dhx * (1.0 + scale), axis=0, keepdims=True))
    return _rowwise(name, fn, [dh, x, dx_res], [g, scale], [(D, F32)], sums=[D, D, D])


def _residual_fwd(name, x, y, gate):
    def fn(x, y, gate):
        return x + gate * y
    return _rowwise(name, fn, [x, y], [gate], [(D, F32)])[0]


def _residual_norm_fwd(name, x, y, gate, g, scale, shift):
    def fn(x, y, gate, g, scale, shift):
        x = x + gate * y
        return x, x * _rms(x) * g * (1.0 + scale) + shift
    return _rowwise(name, fn, [x, y], [gate, g, scale, shift], [(D, F32), (D, BF16)])


def _norm_residual_bwd(name, dh, x, dx_res, g, scale, y, gate):
    def fn(dh, x, dx_res, y, g, scale, gate):
        r = _rms(x)
        xh = x * r
        dxh = dh * (g * (1.0 + scale))
        dx = dx_res + r * (dxh - xh * jnp.mean(dxh * xh, axis=-1, keepdims=True))
        dhx = dh * xh
        return (dx, dx * gate, jnp.sum(dh, axis=0, keepdims=True), jnp.sum(dhx * g, axis=0, keepdims=True),
                jnp.sum(dhx * (1.0 + scale), axis=0, keepdims=True), jnp.sum(dx * y, axis=0, keepdims=True))
    return _rowwise(name, fn, [dh, x, dx_res, y], [g, scale, gate], [(D, F32), (D, BF16)], sums=[D, D, D, D])


def _residual_bwd(name, dx, y, gate):
    def fn(dx, y, gate):
        return dx * gate, jnp.sum(dx * y, axis=0, keepdims=True)
    return _rowwise(name, fn, [dx, y], [gate], [(D, BF16)], sums=[D])


def _loss_head(x, target, g):
    def fn(x, t, g):
        r = _rms(x)
        xh = x * r
        err = xh * g - t
        loss = 0.5 * jnp.sum(jnp.mean(err * err, axis=-1, keepdims=True), axis=0, keepdims=True)
        dy = err * (1.0 / D)
        dxh = dy * g
        dx = r * (dxh - xh * jnp.mean(dxh * xh, axis=-1, keepdims=True))
        return dx, jnp.broadcast_to(loss, (1, LANES)), jnp.sum(dy * xh, axis=0, keepdims=True)
    return _rowwise("loss_head", fn, [x, target], [g], [(D, F32)], sums=[LANES, D])


def _ffn_up(name, h, wg, wu, layer, tm=1024):
    T, n = h.shape[0], wg.shape[1]
    tm = min(tm, T)

    def body(h_ref, wg_ref, wu_ref, a_ref, b_ref, s_ref):
        h = h_ref[...]
        a = _dot(h, wg_ref[0], "nt")
        b = _dot(h, wu_ref[0], "nt")
        a_ref[0] = a.astype(a_ref.dtype)
        b_ref[0] = b.astype(b_ref.dtype)
        s_ref[0] = (a * _sig(a) * b).astype(s_ref.dtype)

    wspec = pl.BlockSpec((1, n, D), lambda ch, i: (ch, layer, 0))
    ospec = pl.BlockSpec((1, tm, n), lambda ch, i: (ch, i, 0))
    return pl.pallas_call(
        body, name=name, grid=(4, T // tm), in_specs=[pl.BlockSpec((tm, D), lambda ch, i: (i, 0)), wspec, wspec],
        out_specs=[ospec, ospec, ospec],
        out_shape=[jax.ShapeDtypeStruct((4, T, n), BF16)] * 3, compiler_params=_params(2))(h, wg, wu)


def _ffn_down(name, s, wd, layer, tm=1024):
    _, T, n = s.shape
    tm = min(tm, T)

    def body(s_ref, w_ref, y_ref):
        @pl.when(pl.program_id(1) == 0)
        def _():
            y_ref[...] = jnp.zeros_like(y_ref)
        y_ref[...] += _dot(s_ref[0], w_ref[0], "nn")

    return pl.pallas_call(
        body, name=name, grid=(T // tm, 4),
        in_specs=[pl.BlockSpec((1, tm, n), lambda i, ch: (ch, i, 0)), pl.BlockSpec((1, n, D), lambda i, ch: (ch, layer, 0))],
        out_specs=pl.BlockSpec((tm, D), lambda i, ch: (i, 0)), out_shape=jax.ShapeDtypeStruct((T, D), F32),
        compiler_params=_params(2))(s, wd)


def _ffn_down_bwd(name, dy, wd, a, b, layer, tm=1024):
    _, T, n = a.shape
    tm = min(tm, T)

    def body(dy_ref, w_ref, a_ref, b_ref, da_ref, db_ref):
        ds = _dot(dy_ref[...], w_ref[0], "nt")
        a, b = a_ref[0].astype(F32), b_ref[0].astype(F32)
        sg = _sig(a)
        da_ref[0] = (ds * b * (sg * (1.0 + a * (1.0 - sg)))).astype(da_ref.dtype)
        db_ref[0] = (ds * (a * sg)).astype(db_ref.dtype)

    bspec = pl.BlockSpec((1, tm, n), lambda ch, i: (ch, i, 0))
    return pl.pallas_call(
        body, name=name, grid=(4, T // tm),
        in_specs=[pl.BlockSpec((tm, D), lambda ch, i: (i, 0)), pl.BlockSpec((1, n, D), lambda ch, i: (ch, layer, 0)), bspec, bspec],
        out_specs=[bspec, bspec], out_shape=[jax.ShapeDtypeStruct((4, T, n), BF16)] * 2,
        compiler_params=_params(2))(dy, wd, a, b)


def _ffn_down_dw(name, s, dy):
    _, T, n = s.shape

    def body(s_ref, dy_ref, o_ref):
        o_ref[0] = _dot(s_ref[0], dy_ref[...], "tn").astype(o_ref.dtype)

    return pl.pallas_call(
        body, name=name, grid=(4,),
        in_specs=[pl.BlockSpec((1, T, n), lambda ch: (ch, 0, 0)), pl.BlockSpec((T, D), lambda ch: (0, 0))],
        out_specs=pl.BlockSpec((1, n, D), lambda ch: (ch, 0, 0)), out_shape=jax.ShapeDtypeStruct((4, n, D), BF16),
        compiler_params=_params(1))(s, dy)


def _ffn_up_dw(name, h, da, db, tm=512):
    _, T, n = da.shape

    def body(h_ref, da_ref, db_ref, dg_ref, du_ref):
        h = h_ref[...]
        dg_ref[0] = _dot(da_ref[0], h, "tn").astype(dg_ref.dtype)
        du_ref[0] = _dot(db_ref[0], h, "tn").astype(du_ref.dtype)

    dspec = pl.BlockSpec((1, T, n), lambda ch, j: (ch, 0, 0))
    ospec = pl.BlockSpec((1, n, tm), lambda ch, j: (ch, 0, j))
    return pl.pallas_call(
        body, name=name, grid=(4, D // tm), in_specs=[pl.BlockSpec((T, tm), lambda ch, j: (0, j)), dspec, dspec],
        out_specs=[ospec, ospec], out_shape=[jax.ShapeDtypeStruct((4, n, D), BF16)] * 2,
        compiler_params=_params(2))(h, da, db)


def _ffn_up_dx(name, da, db, wg, wu, layer, tm=1024):
    _, T, n = da.shape
    tm = min(tm, T)

    def body(da_ref, db_ref, wg_ref, wu_ref, o_ref):
        @pl.when(pl.program_id(1) == 0)
        def _():
            o_ref[...] = jnp.zeros_like(o_ref)
        o_ref[...] += _dot(da_ref[0], wg_ref[0], "nn") + _dot(db_ref[0], wu_ref[0], "nn")

    dspec = pl.BlockSpec((1, tm, n), lambda i, ch: (ch, i, 0))
    wspec = pl.BlockSpec((1, n, D), lambda i, ch: (ch, layer, 0))
    return pl.pallas_call(
        body, name=name, grid=(T // tm, 4), in_specs=[dspec, dspec, wspec, wspec],
        out_specs=pl.BlockSpec((tm, D), lambda i, ch: (i, 0)), out_shape=jax.ShapeDtypeStruct((T, D), F32),
        compiler_params=_params(2))(da, db, wg, wu)


def _shift_down(x, k):
    if k == 0:
        return x
    rows = lax.broadcasted_iota(jnp.int32, x.shape, 0)
    return jnp.where(rows >= k, pltpu.roll(x, k, 0), 0.0)


def _shift_up(x, k):
    if k == 0:
        return x
    T = x.shape[0]
    rows = lax.broadcasted_iota(jnp.int32, x.shape, 0)
    return jnp.where(rows < T - k, pltpu.roll(x, T - k, 0), 0.0)


def _conv_silu(x, w):
    c = w[0:1, :] * _shift_down(x, 3) + w[1:2, :] * _shift_down(x, 2) + w[2:3, :] * _shift_down(x, 1) + w[3:4, :] * x
    sg = _sig(c)
    return c, sg, c * sg


def _gdn_conv_fwd(name, proj, cw):
    T = proj.shape[0]

    def body(x_ref, w_ref, o_ref):
        j = pl.program_id(0)
        _, _, y = _conv_silu(x_ref[...], w_ref[...])
        r = lax.rsqrt(jnp.sum(y * y, axis=1, keepdims=True) + EPS)
        mult = jnp.where(j < NH, HD ** -0.5, 1.0)
        o_ref[...] = jnp.where(j < 2 * NH, y * (r * mult), y)

    return pl.pallas_call(body, name=name, grid=(3 * NH,),
                          in_specs=[pl.BlockSpec((T, HD), lambda j: (0, j)), pl.BlockSpec((4, HD), lambda j: (0, j))],
                          out_specs=pl.BlockSpec((T, HD), lambda j: (0, j)),
                          out_shape=jax.ShapeDtypeStruct((T, GDN_QKV), F32), compiler_params=_params(1))(proj, cw)


def _gdn_conv_bwd(name, proj, cw, dz):
    T = proj.shape[0]

    def body(x_ref, w_ref, dz_ref, dx_ref, dw_ref):
        j = pl.program_id(0)
        x, w, dz = x_ref[...], w_ref[...], dz_ref[...]
        c, sg, y = _conv_silu(x, w)
        r = lax.rsqrt(jnp.sum(y * y, axis=1, keepdims=True) + EPS)
        mult = jnp.where(j < NH, HD ** -0.5, 1.0)
        dyn = mult * (r * dz - (r * r * r) * y * jnp.sum(dz * y, axis=1, keepdims=True))
        dy = jnp.where(j < 2 * NH, dyn, dz)
        dc = dy * (sg * (1.0 + c * (1.0 - sg)))
        dx = w[0:1, :] * _shift_up(dc, 3) + w[1:2, :] * _shift_up(dc, 2) + w[2:3, :] * _shift_up(dc, 1) + w[3:4, :] * dc
        dx_ref[...] = dx.astype(dx_ref.dtype)
        for k in range(4):
            dw_ref[pl.ds(k, 1), :] = jnp.sum(dc * _shift_down(x, 3 - k), axis=0, keepdims=True)

    return pl.pallas_call(body, name=name, grid=(3 * NH,),
                          in_specs=[pl.BlockSpec((T, HD), lambda j: (0, j)), pl.BlockSpec((4, HD), lambda j: (0, j)),
                                    pl.BlockSpec((T, HD), lambda j: (0, j))],
                          out_specs=[pl.BlockSpec((T, HD), lambda j: (0, j)), pl.BlockSpec((4, HD), lambda j: (0, j))],
                          out_shape=[jax.ShapeDtypeStruct((T, GDN_QKV), BF16), jax.ShapeDtypeStruct((4, GDN_QKV), F32)],
                          compiler_params=_params(1))(proj, cw, dz)


def _softplus(z):
    return jnp.maximum(z, 0.0) + jnp.log(1.0 + jnp.exp(-jnp.abs(z)))


_AB_CB = GDN_INK // (2 * HD) - 1


def _gdn_gates_fwd(name, proj, alog, dtb):
    def fn(ab, alog, dtb):
        a, b = ab[:, :HD], ab[:, HD:]
        return -jnp.exp(alog) * _softplus(a + dtb), _sig(b)
    return _rowwise(name, fn, [(proj, 2 * HD, _AB_CB)], [alog, dtb], [(HD, F32), (HD, F32)])


def _gdn_gates_bwd(name, proj, dg_h, db_h, alog, dtb):
    def fn(ab, dg_h, db_h, alog, dtb):
        lane = lax.broadcasted_iota(jnp.int32, (1, HD), 1)
        dg = jnp.zeros(dg_h.shape[1:], F32)
        dbeta = jnp.zeros(dg_h.shape[1:], F32)
        for h in range(NH):
            oh = (lane == h).astype(F32)
            dg = dg + dg_h[h] * oh
            dbeta = dbeta + db_h[h] * oh
        a, b = ab[:, :HD], ab[:, HD:]
        z = a + dtb
        ea = jnp.exp(alog)
        beta = _sig(b)
        da = dg * (-ea) * _sig(z)
        db = dbeta * beta * (1.0 - beta)
        return (jnp.concatenate([da, db], axis=1), jnp.sum(dg * (-ea * _softplus(z)), axis=0, keepdims=True),
                jnp.sum(da, axis=0, keepdims=True))
    return _rowwise(name, fn, [(proj, 2 * HD, _AB_CB), dg_h, db_h], [alog, dtb], [(2 * HD, BF16)], sums=[HD, HD])


def _interleave(gens):
    gens = list(gens)
    results = [None] * len(gens)
    active = list(range(len(gens)))
    while active:
        for i in list(active):
            try:
                next(gens[i])
            except StopIteration as stop:
                results[i] = stop.value
                active.remove(i)
    return results


def _chunk_common(q, k, v, gblk, bblk, h, prec):
    C = CHUNK
    lane = lax.broadcasted_iota(jnp.int32, (1, HD), 1)
    oh = (lane == h).astype(F32)
    g_col = jnp.sum(gblk * oh, axis=1, keepdims=True)
    beta = jnp.sum(bblk * oh, axis=1, keepdims=True)
    ri = lax.broadcasted_iota(jnp.int32, (C, C), 0)
    ci = lax.broadcasted_iota(jnp.int32, (C, C), 1)
    incl = ri >= ci
    strict = ri > ci
    eye = (ri == ci).astype(F32)
    gcb = _dot(incl.astype(F32), jnp.broadcast_to(g_col, (C, HD)), "nn", HI)
    yield
    gc = gcb[:, :C]
    gc_row = _dot(jnp.ones((C, C), F32), eye * gc, "nn", HI)
    yield
    decay = jnp.where(incl, jnp.exp(jnp.where(incl, gc - gc_row, 0.0)), 0.0)
    rows = lax.broadcasted_iota(jnp.int32, (C, HD), 0)
    gclb = jnp.sum(jnp.where(rows == C - 1, gcb, 0.0), axis=0, keepdims=True)
    eg = jnp.exp(gcb)
    egl = jnp.exp(gclb - gcb)
    gl = jnp.exp(gclb)
    kb = k * beta
    m1 = _dot(kb, k, "nt", prec)
    qk = _dot(q, k, "nt", prec)
    yield
    L = jnp.where(strict, m1 * decay, 0.0)
    nl = -L
    tinv = eye + nl
    p = nl
    for _ in range(5):
        p = _dot(p, p, "nn", H3)
        yield
        tinv = tinv + _dot(tinv, p, "nn", H3)
    vb = v * beta
    kbg = kb * eg
    yield
    u = _dot(tinv, vb, "nn", prec)
    w = _dot(tinv, kbg, "nn", prec)
    yield
    attn = jnp.where(incl, qk * decay, 0.0)
    return dict(beta=beta, incl=incl, strict=strict, decay=decay, eg=eg, egl=egl, gl=gl, kb=kb, m1=m1, tinv=tinv,
                kbg=kbg, u=u, w=w, qk=qk, attn=attn, q_dec=q * eg, k_dec=k * egl, rows=rows, oh=oh)


def _gdn_chunk_fwd(name, qkv, g, beta):
    T = qkv.shape[0]
    N = T // CHUNK

    hb = _GDN_HB
    w = hb * HD

    def body(q_ref, k_ref, v_ref, g_ref, b_ref, o_ref, st_ref, S):
        hg, n = pl.program_id(0), pl.program_id(1)

        @pl.when(n == 0)
        def _():
            S[...] = jnp.zeros_like(S)

        gblk, bblk = g_ref[...], b_ref[...]

        def one_head(i, q, k, v, s):
            c = yield from _chunk_common(q, k, v, gblk, bblk, hg * hb + i, HF)
            v_new = c["u"] - _dot(c["w"], s, "nn", HF)
            qs = _dot(c["q_dec"], s, "nn", HF)
            yield
            o = qs + _dot(c["attn"], v_new, "nn", HF)
            return o, s * c["gl"] + _dot(c["k_dec"], v_new, "tn", HF)

        sls = [slice(i * HD, (i + 1) * HD) for i in range(hb)]
        states = [S[i] for i in range(hb)]
        res = _interleave(one_head(i, q_ref[:, sls[i]], k_ref[:, sls[i]], v_ref[:, sls[i]], states[i]) for i in range(hb))
        for i, (o, s_new) in enumerate(res):
            st_ref[i, 0] = states[i]
            o_ref[:, sls[i]] = o
            S[i] = s_new

    blk = lambda off: pl.BlockSpec((CHUNK, w), lambda h, n, off=off: (n, off + h))
    gspec = pl.BlockSpec((CHUNK, HD), lambda h, n: (n, 0))
    return pl.pallas_call(
        body, name=name, grid=(NH // hb, N), in_specs=[blk(0), blk(NH // hb), blk(2 * NH // hb), gspec, gspec],
        out_specs=[pl.BlockSpec((CHUNK, w), lambda h, n: (n, h)), pl.BlockSpec((hb, 1, HD, HD), lambda h, n: (h, n, 0, 0))],
        out_shape=[jax.ShapeDtypeStruct((T, NH * HD), F32), jax.ShapeDtypeStruct((NH, N, HD, HD), F32)],
        scratch_shapes=[pltpu.VMEM((hb, HD, HD), F32)], compiler_params=_params(2))(qkv, qkv, qkv, g, beta)


def _gdn_chunk_bwd(name, qkv, g, beta, states, do):
    T = qkv.shape[0]
    N = T // CHUNK
    C = CHUNK

    hb = _GDN_HB
    w = hb * HD
    assert hb == NH

    def body(q_ref, k_ref, v_ref, g_ref, b_ref, st_ref, do_ref, dqkv_ref, dg_ref, db_ref, dS):
        hg, n = pl.program_id(0), pl.program_id(1)

        @pl.when(n == 0)
        def _():
            dS[...] = jnp.zeros_like(dS)

        gblk, bblk = g_ref[...], b_ref[...]
        sls = [slice(i * HD, (i + 1) * HD) for i in range(hb)]
        res = _interleave(one_head(hg * hb + i, gblk, bblk, q_ref[:, sls[i]], k_ref[:, sls[i]], v_ref[:, sls[i]],
                                   st_ref[i, 0], do_ref[:, sls[i]], dS[i]) for i in range(hb))
        for i, (dq, dk, dv, dg, db, ds_new) in enumerate(res):
            dqkv_ref[:, sls[i]] = dq
            dqkv_ref[:, slice(w + i * HD, w + (i + 1) * HD)] = dk
            dqkv_ref[:, slice(2 * w + i * HD, 2 * w + (i + 1) * HD)] = dv
            dg_ref[i] = dg
            db_ref[i] = db
            dS[i] = ds_new

    def one_head(h, gblk, bblk, q, k, v, s, do, ds):
        c = yield from _chunk_common(q, k, v, gblk, bblk, h, HF)
        eg, egl, gl, beta, decay, tinv = c["eg"], c["egl"], c["gl"], c["beta"], c["decay"], c["tinv"]
        v_new = c["u"] - _dot(c["w"], s, "nn", HF)
        dq_dec = _dot(do, s, "nt", HF)
        yield
        dv_new = _dot(c["attn"], do, "tn", HF) + _dot(c["k_dec"], ds, "nn", HF)
        dk_dec = _dot(v_new, ds, "nt", HF)
        dgl = jnp.sum(jnp.sum(s * ds, axis=1, keepdims=True), axis=0, keepdims=True)
        yield
        ds_new = ds * gl + _dot(c["q_dec"], do, "tn", HF) - _dot(c["w"], dv_new, "tn", HF)
        dattn = jnp.where(c["incl"], _dot(do, v_new, "nt", HF), 0.0)
        dw = -_dot(dv_new, s, "nt", HF)
        yield
        dvb = _dot(tinv, dv_new, "tn", HS)
        dkbg = _dot(tinv, dw, "tn", HS)
        yield
        dA = -(_dot(dvb, c["u"], "nt", HS) + _dot(dkbg, c["w"], "nt", HS))
        yield
        dL = jnp.where(c["strict"], dA, 0.0)
        dm1 = dL * decay
        dqk = dattn * decay
        xdec = (dL * c["m1"] + dattn * c["qk"]) * decay
        dkb = _dot(dm1, k, "nn", HS) + dkbg * eg
        dk = _dot(dm1, c["kb"], "tn", HS) + _dot(dqk, q, "tn", HS) + dk_dec * egl + dkb * beta
        dq = _dot(dqk, k, "nn", HS) + dq_dec * eg
        yield
        dkd_kd = jnp.sum(dk_dec * c["k_dec"], axis=1, keepdims=True)
        dgc = (jnp.sum(xdec, axis=1, keepdims=True) - _dot(xdec, jnp.ones((C, HD), F32), "tn", HS)
               + jnp.sum(dq_dec * c["q_dec"], axis=1, keepdims=True) - dkd_kd
               + jnp.sum(dkbg * c["kbg"], axis=1, keepdims=True))
        dgcl = jnp.sum(dkd_kd, axis=0, keepdims=True) + dgl * gl
        dgc = dgc + jnp.where(c["rows"] == C - 1, dgcl, 0.0)
        ri = lax.broadcasted_iota(jnp.int32, (C, C), 0)
        ci = lax.broadcasted_iota(jnp.int32, (C, C), 1)
        dg = _dot((ci >= ri).astype(F32), dgc, "nn", HI)
        db = jnp.broadcast_to(jnp.sum(dkb * k, axis=1, keepdims=True) + jnp.sum(dvb * v, axis=1, keepdims=True), (C, HD))
        return dq, dk, dvb * beta, dg, db, ds_new

    blk = lambda off: pl.BlockSpec((C, w), lambda h, n, off=off: (N - 1 - n, off + h))
    gspec = pl.BlockSpec((C, HD), lambda h, n: (N - 1 - n, 0))
    ospec = pl.BlockSpec((C, w), lambda h, n: (N - 1 - n, h))
    hspec = pl.BlockSpec((hb, C, HD), lambda h, n: (h, N - 1 - n, 0))
    return pl.pallas_call(
        body, name=name, grid=(NH // hb, N),
        in_specs=[blk(0), blk(NH // hb), blk(2 * NH // hb), gspec, gspec,
                  pl.BlockSpec((hb, 1, HD, HD), lambda h, n: (h, N - 1 - n, 0, 0)), ospec],
        out_specs=[pl.BlockSpec((C, 3 * w), lambda h, n: (N - 1 - n, 0)), hspec, hspec],
        out_shape=[jax.ShapeDtypeStruct((T, 3 * NH * HD), F32)] + [jax.ShapeDtypeStruct((NH, T, HD), F32)] * 2,
        scratch_shapes=[pltpu.VMEM((hb, HD, HD), F32)], compiler_params=_params(2))(qkv, qkv, qkv, g, beta, states, do)


_GATE_CB = GDN_QKV // (NH * HD)


def _gdn_gated_norm_fwd(name, o, proj, ng):
    def fn(o, gate, ng):
        outs = []
        for h in range(NH):
            sl = slice(h * HD, (h + 1) * HD)
            oh, gh = o[:, sl], gate[:, sl]
            outs.append(oh * _rms(oh) * ng * (gh * _sig(gh)))
        return jnp.concatenate(outs, axis=1)
    return _rowwise(name, fn, [o, (proj, NH * HD, _GATE_CB)], [ng], [(NH * HD, BF16)])[0]


def _gdn_gated_norm_bwd(name, don, o, proj, ng):
    def fn(don, o, gate, ng):
        dos, dgs = [], []
        dng = jnp.zeros((1, HD), F32)
        for h in range(NH):
            sl = slice(h * HD, (h + 1) * HD)
            oh, gh, dh = o[:, sl], gate[:, sl], don[:, sl]
            r = _rms(oh)
            xh = oh * r
            sg = _sig(gh)
            dn = dh * (gh * sg)
            dgs.append(dh * (xh * ng) * (sg * (1.0 + gh * (1.0 - sg))))
            dng = dng + jnp.sum(dn * xh, axis=0, keepdims=True)
            dxh = dn * ng
            dos.append(r * (dxh - xh * jnp.mean(dxh * xh, axis=-1, keepdims=True)))
        return jnp.concatenate(dos, axis=1), jnp.concatenate(dgs, axis=1), dng
    return _rowwise(name, fn, [don, o, (proj, NH * HD, _GATE_CB)], [ng], [(NH * HD, F32), (NH * HD, BF16)], sums=[HD])


def _rot(x):
    lane = lax.broadcasted_iota(jnp.int32, x.shape, 1)
    return jnp.where(lane < ROPE // 2, -pltpu.roll(x, HD - ROPE // 2, 1), pltpu.roll(x, ROPE // 2, 1))


def _rot_t(x):
    lane = lax.broadcasted_iota(jnp.int32, x.shape, 1)
    return jnp.where(lane < ROPE // 2, pltpu.roll(x, HD - ROPE // 2, 1), -pltpu.roll(x, ROPE // 2, 1))


def _rope_tables(pos_col):
    lane = jnp.arange(HD)
    inv_freq = ROPE_THETA ** (-(2.0 * (lane % (ROPE // 2)).astype(F32)) / ROPE)
    inv_freq = jnp.where(lane < ROPE, inv_freq, 0.0).astype(F32)[None, :]
    valid = (lane < ROPE).astype(F32)[None, :]

    def fn(pos, inv_freq, valid):
        ang = pos.astype(F32) * inv_freq
        return jnp.cos(ang) * valid, jnp.sin(ang) * valid
    return _rowwise("rope_tables", fn, [pos_col], [inv_freq, valid], [(HD, F32), (HD, F32)])


def _mla_pre_fwd(name, proj, cos, sin, qg, kvg):
    def fn(p, cos, sin, qg, kvg):
        cq, ckv, kr = p[:, :Q_RANK], p[:, Q_RANK:Q_RANK + KV_RANK], p[:, Q_RANK + KV_RANK:]
        return cq * _rms(cq) * qg, ckv * _rms(ckv) * kvg, kr * cos + _rot(kr) * sin
    return _rowwise(name, fn, [proj, cos, sin], [qg, kvg], [(Q_RANK, BF16), (KV_RANK, BF16), (HD, BF16)])


def _rms_bwd(dy, x, g):
    r = _rms(x)
    xh = x * r
    dxh = dy * g
    return r * (dxh - xh * jnp.mean(dxh * xh, axis=-1, keepdims=True)), jnp.sum(dy * xh, axis=0, keepdims=True)


def _mla_pre_bwd(name, proj, dcqn, dckvn, dkr, cos, sin, qg, kvg):
    def fn(p, dcqn, dckvn, dkr, cos, sin, qg, kvg):
        cq, ckv = p[:, :Q_RANK], p[:, Q_RANK:Q_RANK + KV_RANK]
        dcq, dqg = _rms_bwd(dcqn, cq, qg)
        dckv, dkvg = _rms_bwd(dckvn, ckv, kvg)
        dkr_pre = dkr * cos + _rot_t(dkr * sin)
        return jnp.concatenate([dcq, dckv, dkr_pre], axis=1), dqg, dkvg
    return _rowwise(name, fn, [proj, dcqn, dckvn, dkr, cos, sin], [qg, kvg], [(MLA_INK, BF16)], sums=[Q_RANK, KV_RANK])


def _mla_q_fwd(name, q, cos, sin):
    def fn(qn, qr, cos, sin):
        outs = []
        for h in range(NH):
            x = qr[:, h * HD:(h + 1) * HD]
            outs.append(x * cos + _rot(x) * sin)
        return qn, jnp.concatenate(outs, axis=1)
    return _rowwise(name, fn, [(q, NH * HD, 0), (q, NH * HD, 1), cos, sin], [], [(NH * HD, BF16), (NH * HD, BF16)])


def _mla_q_bwd(name, dqn, dqr, cos, sin):
    def fn(dqn, dqr, cos, sin):
        outs = [dqn]
        for h in range(NH):
            z = dqr[:, h * HD:(h + 1) * HD]
            outs.append(z * cos + _rot_t(z * sin))
        return jnp.concatenate(outs, axis=1)
    return _rowwise(name, fn, [dqn, dqr, cos, sin], [], [(2 * NH * HD, BF16)])[0]


def _att_probs(qn, qr, kn, kr, row0):
    s = (_dot(qn, kn, "nt") + _dot(qr, kr, "nt")) * ATT_SCALE
    qpos = row0 + lax.broadcasted_iota(jnp.int32, s.shape, 0)
    kpos = lax.broadcasted_iota(jnp.int32, s.shape, 1)
    s = jnp.where(kpos <= qpos, s, -1e30)
    p = jnp.exp(s - jnp.max(s, axis=1, keepdims=True))
    return p / jnp.sum(p, axis=1, keepdims=True)


def _mla_attn_fwd(name, qn, qr, kv, kr, tq=256):
    T = qn.shape[0]
    tq = min(tq, T)

    def body(qn_ref, qr_ref, kn_ref, v_ref, kr_ref, o_ref):
        i = pl.program_id(1)
        for blk in range(T // tq):
            @pl.when(i == blk)
            def _(blk=blk):
                keys = pl.ds(0, (blk + 1) * tq)
                p = _att_probs(qn_ref[...], qr_ref[...], kn_ref[keys, :], kr_ref[keys, :], blk * tq)
                o_ref[...] = _dot(p.astype(BF16), v_ref[keys, :], "nn").astype(o_ref.dtype)

    qspec = pl.BlockSpec((tq, HD), lambda h, i: (i, h))
    return pl.pallas_call(
        body, name=name, grid=(NH, T // tq),
        in_specs=[qspec, qspec, pl.BlockSpec((T, HD), lambda h, i: (0, h)), pl.BlockSpec((T, HD), lambda h, i: (0, NH + h)),
                  pl.BlockSpec((T, HD), lambda h, i: (0, 0))],
        out_specs=qspec, out_shape=jax.ShapeDtypeStruct((T, NH * HD), BF16), compiler_params=_params(2))(qn, qr, kv, kv, kr)


def _mla_attn_bwd(name, qn, qr, kv, kr, do, tq=256):
    T = qn.shape[0]
    tq = min(tq, T)

    def body(qn_ref, qr_ref, kn_ref, v_ref, kr_ref, do_ref, dqn_ref, dqr_ref, dkn_ref, dv_ref, dkr_ref):
        h, i = pl.program_id(0), pl.program_id(1)

        @pl.when(i == 0)
        def _():
            dkn_ref[...] = jnp.zeros_like(dkn_ref)
            dv_ref[...] = jnp.zeros_like(dv_ref)

        @pl.when((i == 0) & (h == 0))
        def _():
            dkr_ref[...] = jnp.zeros_like(dkr_ref)

        for blk in range(T // tq):
            @pl.when(i == blk)
            def _(blk=blk):
                keys = pl.ds(0, (blk + 1) * tq)
                qn, qr, do = qn_ref[...], qr_ref[...], do_ref[...]
                kn, kr, v = kn_ref[keys, :], kr_ref[keys, :], v_ref[keys, :]
                p = _att_probs(qn, qr, kn, kr, blk * tq)
                dp = _dot(do, v, "nt")
                ds = (p * (dp - jnp.sum(p * dp, axis=1, keepdims=True)) * ATT_SCALE).astype(BF16)
                dqn_ref[...] = _dot(ds, kn, "nn")
                dqr_ref[...] = _dot(ds, kr, "nn")
                dkn_ref[keys, :] += _dot(ds, qn, "tn")
                dkr_ref[keys, :] += _dot(ds, qr, "tn")
                dv_ref[keys, :] += _dot(p.astype(BF16), do, "tn")

    qspec = pl.BlockSpec((tq, HD), lambda h, i: (i, h))
    kspec = pl.BlockSpec((T, HD), lambda h, i: (0, h))
    return pl.pallas_call(
        body, name=name, grid=(NH, T // tq),
        in_specs=[qspec, qspec, kspec, pl.BlockSpec((T, HD), lambda h, i: (0, NH + h)),
                  pl.BlockSpec((T, HD), lambda h, i: (0, 0)), qspec],
        out_specs=[qspec, qspec, kspec, kspec, pl.BlockSpec((T, HD), lambda h, i: (0, 0))],
        out_shape=[jax.ShapeDtypeStruct((T, NH * HD), F32)] * 4 + [jax.ShapeDtypeStruct((T, HD), F32)],
        compiler_params=_params(2))(qn, qr, kv, kv, kr, do)


def _mod_rows(mod, layer):
    return [(mod, layer, i, D) for i in range(N_MOD)]


def _local_step(x, target, pos_col, mod, weights_of, P, on_grads):
    cos, sin = _rope_tables(pos_col)
    saved = []
    sh_m, sc_m = _mod_rows(mod, 0)[:2]
    h = _norm_mod_fwd("norm_mix0", x, (P["norm_mix_g"], 0, 0, D), sc_m, sh_m)
    for l in range(DEPTH):
        j = l // 2
        sh_m, sc_m, ga_m, sh_f, sc_f, ga_f = _mod_rows(mod, l)
        s = dict(x0=x)
        W = weights_of(l, h)
        s.update(h=h, W=W)
        if l % 2 == 0:
            proj = _mm(f"gdn_in{j}", h, W["gdn_in"], "nn", tn=GDN_INK // 2)
            qkv = _gdn_conv_fwd(f"gdn_conv{j}", proj, P["gdn_cw"][j])
            g, beta = _gdn_gates_fwd(f"gdn_gates{j}", proj, P["gdn_alog"][j], P["gdn_dtb"][j])
            o, states = _gdn_chunk_fwd(f"gdn_chunk{j}", qkv, g, beta)
            on = _gdn_gated_norm_fwd(f"gdn_gnorm{j}", o, proj, P["gdn_ng"][j])
            y = _mm(f"gdn_out{j}", on, W["gdn_out"], "nn")
            s.update(proj=proj, qkv=qkv, g=g, beta=beta, o=o, states=states, on=on)
        else:
            proj = _mm(f"mla_in{j}", h, W["mla_in"], "nn")
            cqn, ckvn, kr = _mla_pre_fwd(f"mla_pre{j}", proj, cos, sin, P["mla_qg"][j], P["mla_kvg"][j])
            q = _mm(f"mla_uq{j}", cqn, W["mla_uq"], "nn")
            kv = _mm(f"mla_ukv{j}", ckvn, W["mla_ukv"], "nn", out_dtype=BF16)
            qn, qr = _mla_q_fwd(f"mla_q{j}", q, cos, sin)
            o = _mla_attn_fwd(f"mla_attn{j}", qn, qr, kv, kr)
            y = _mm(f"mla_out{j}", o, W["mla_out"], "nn")
            s.update(proj=proj, cqn=cqn, ckvn=ckvn, kr=kr, kv=kv, qn=qn, qr=qr, o=o)
        s["y"] = y
        x, h2 = _residual_norm_fwd(f"res_mix{l}", x, y, ga_m, (P["norm_ffn_g"], l, 0, D), sc_f, sh_f)
        s["x1"] = x
        fa, fb, sw = _ffn_up(f"ffn_up{l}", h2, W["ffn_g"], W["ffn_u"], 0)
        yf = _ffn_down(f"ffn_down{l}", sw, W["ffn_d"], 0)
        if l + 1 < DEPTH:
            sh_n, sc_n = _mod_rows(mod, l + 1)[:2]
            x, h = _residual_norm_fwd(f"res_ffn{l}", x, yf, ga_f, (P["norm_mix_g"], l + 1, 0, D), sc_n, sh_n)
        else:
            x = _residual_fwd(f"res_ffn{l}", x, yf, ga_f)
        s.update(h2=h2, fa=fa, fb=fb, sw=sw, yf=yf)
        saved.append(s)

    dx, loss, d_final = _loss_head(x, target, P["final_g"])
    gP = dict(loss=loss, final_g=d_final, norm_mix_g=[None] * DEPTH, norm_ffn_g=[None] * DEPTH,
              gdn_cw=[None] * 2, gdn_alog=[None] * 2, gdn_dtb=[None] * 2, gdn_ng=[None] * 2,
              mla_qg=[None] * 2, mla_kvg=[None] * 2)
    dmod = [None] * DEPTH
    dyf, d_ga_f = _residual_bwd(f"res_ffn_b{DEPTH - 1}", dx, saved[-1]["yf"], _mod_rows(mod, DEPTH - 1)[5])
    for l in reversed(range(DEPTH)):
        j = l // 2
        s = saved[l]
        W = s["W"]
        sh_m, sc_m, ga_m, sh_f, sc_f, ga_f = _mod_rows(mod, l)
        da, db = _ffn_down_bwd(f"ffn_down_dx{l}", dyf, W["ffn_d"], s["fa"], s["fb"], 0)
        g_down = _ffn_down_dw(f"ffn_down_dw{l}", s["sw"], dyf)
        g_gate, g_up = _ffn_up_dw(f"ffn_up_dw{l}", s["h2"], da, db)
        on_grads(l, "ffn", dict(ffn_w_gate=g_gate, ffn_w_up=g_up, ffn_w_down=g_down))
        dh2 = _ffn_up_dx(f"ffn_up_dx{l}", da, db, W["ffn_g"], W["ffn_u"], 0)
        dx, dy, d_sh_f, d_sc_f, gP["norm_ffn_g"][l], d_ga_m = _norm_residual_bwd(
            f"norm_ffn_b{l}", dh2, s["x1"], dx, (P["norm_ffn_g"], l, 0, D), sc_f, s["y"], ga_m)
        if l % 2 == 0:
            don = _mm(f"gdn_out_dx{j}", dy, W["gdn_out"], "nt")
            g_out = _mm(f"gdn_out_dw{j}", s["on"], dy, "tn", out_dtype=BF16)
            do, dgate, gP["gdn_ng"][j] = _gdn_gated_norm_bwd(f"gdn_gnorm_b{j}", don, s["o"], s["proj"], P["gdn_ng"][j])
            dqkv, dg_h, db_h = _gdn_chunk_bwd(f"gdn_chunk_b{j}", s["qkv"], s["g"], s["beta"], s["states"], do)
            dab_, gP["gdn_alog"][j], gP["gdn_dtb"][j] = _gdn_gates_bwd(f"gdn_gates_b{j}", s["proj"], dg_h, db_h,
                                                                        P["gdn_alog"][j], P["gdn_dtb"][j])
            dpre, gP["gdn_cw"][j] = _gdn_conv_bwd(f"gdn_conv_b{j}", s["proj"], P["gdn_cw"][j], dqkv)
            dproj = jnp.concatenate([dpre, dgate, dab_], axis=1)
            g_in = _mm(f"gdn_in_dw{j}", s["h"], dproj, "tn", out_dtype=BF16, tn=GDN_INK // 2)
            on_grads(l, "mix", dict(gdn_w_in=_uncols(_gdn_in_from_kernel(g_in)), gdn_w_out=_unrows(g_out)))
            dh = _mm(f"gdn_in_dx{j}", dproj, W["gdn_in"], "nt")
        else:
            do = _mm(f"mla_out_dx{j}", dy, W["mla_out"], "nt", out_dtype=BF16)
            g_out = _mm(f"mla_out_dw{j}", s["o"], dy, "tn", out_dtype=BF16)
            dqn, dqr, dkn, dv, dkr = _mla_attn_bwd(f"mla_attn_b{j}", s["qn"], s["qr"], s["kv"], s["kr"], do)
            dq = _mla_q_bwd(f"mla_q_b{j}", dqn, dqr, cos, sin)
            dkv = jnp.concatenate([dkn, dv], axis=1)
            g_uq = _mm(f"mla_uq_dw{j}", s["cqn"], dq, "tn", out_dtype=BF16)
            dcqn = _mm(f"mla_uq_dx{j}", dq, W["mla_uq"], "nt")
            g_ukv = _mm(f"mla_ukv_dw{j}", s["ckvn"], dkv, "tn", out_dtype=BF16)
            dckvn = _mm(f"mla_ukv_dx{j}", dkv, W["mla_ukv"], "nt")
            dproj, gP["mla_qg"][j], gP["mla_kvg"][j] = _mla_pre_bwd(f"mla_pre_b{j}", s["proj"], dcqn, dckvn, dkr, cos, sin,
                                                                     P["mla_qg"][j], P["mla_kvg"][j])
            g_in = _mm(f"mla_in_dw{j}", s["h"], dproj, "tn", out_dtype=BF16)
            on_grads(l, "mix", dict(mla_w_in=_unrows(g_in[:, :Q_RANK + KV_RANK + ROPE]), mla_w_uq=_uncols(_mla_uq_from_kernel(g_uq)),
                                    mla_w_ukv=_uncols(_mla_ukv_from_kernel(g_ukv)), mla_w_out=_unrows(g_out)))
            dh = _mm(f"mla_in_dx{j}", dproj, W["mla_in"], "nt")
        if l > 0:
            dx, dyf_prev, d_sh_m, d_sc_m, gP["norm_mix_g"][l], d_ga_f_prev = _norm_residual_bwd(
                f"norm_mix_b{l}", dh, s["x0"], dx, (P["norm_mix_g"], l, 0, D), sc_m, saved[l - 1]["yf"], _mod_rows(mod, l - 1)[5])
        else:
            dx, d_sh_m, d_sc_m, gP["norm_mix_g"][l] = _norm_mod_bwd(f"norm_mix_b{l}", dh, s["x0"], dx,
                                                                     (P["norm_mix_g"], l, 0, D), sc_m)
        dmod[l] = jnp.concatenate([d_sh_m, d_sc_m, d_ga_m, d_sh_f, d_sc_f, d_ga_f], axis=1)
        if l > 0:
            dyf, d_ga_f = dyf_prev, d_ga_f_prev
    return dx, jnp.concatenate(dmod, axis=0), gP


def _pad_cols(a, width):
    return jnp.pad(a, ((0, 0), (0, width - a.shape[1])))


def _gdn_in_to_kernel(w):
    m = GDN_QKV + NH * HD
    return jnp.concatenate([w[:, :m], _pad_cols(w[:, m:m + NH], HD), _pad_cols(w[:, m + NH:], HD)], axis=1)


def _gdn_in_from_kernel(g):
    m = GDN_QKV + NH * HD
    return jnp.concatenate([g[:, :m], g[:, m:m + NH], g[:, m + HD:m + HD + NH]], axis=1)


def _mla_uq_to_kernel(w):
    w3 = w.reshape(Q_RANK, NH, HD + ROPE)
    rope = jnp.pad(w3[:, :, HD:], ((0, 0), (0, 0), (0, HD - ROPE)))
    return jnp.concatenate([w3[:, :, :HD].reshape(Q_RANK, NH * HD), rope.reshape(Q_RANK, NH * HD)], axis=1)


def _mla_uq_from_kernel(g):
    gn = g[:, :NH * HD].reshape(Q_RANK, NH, HD)
    gr = g[:, NH * HD:].reshape(Q_RANK, NH, HD)[:, :, :ROPE]
    return jnp.concatenate([gn, gr], axis=2).reshape(Q_RANK, NH * (HD + ROPE))


def _mla_ukv_to_kernel(w):
    w3 = w.reshape(KV_RANK, NH, 2 * HD)
    return jnp.concatenate([w3[:, :, :HD].reshape(KV_RANK, NH * HD), w3[:, :, HD:].reshape(KV_RANK, NH * HD)], axis=1)


def _mla_ukv_from_kernel(g):
    gk = g[:, :NH * HD].reshape(KV_RANK, NH, HD)
    gv = g[:, NH * HD:].reshape(KV_RANK, NH, HD)
    return jnp.concatenate([gk, gv], axis=2).reshape(KV_RANK, NH * 2 * HD)


def _cols(t):
    return jnp.moveaxis(t, 0, 1).reshape(t.shape[1], -1)


def _uncols(g):
    return jnp.moveaxis(g.reshape(g.shape[0], 4, -1), 1, 0)


def _rows(t):
    return t.reshape(-1, t.shape[2])


def _unrows(g):
    return g.reshape(4, -1, g.shape[1])


def _layer_weights(layer):
    mixer = ("gdn_w_in", "gdn_w_out") if layer % 2 == 0 else ("mla_w_in", "mla_w_uq", "mla_w_ukv", "mla_w_out")
    return [(n, layer // 2) for n in mixer] + [(n, layer) for n in ("ffn_w_gate", "ffn_w_up", "ffn_w_down")]


def _weights_to_kernel(layer, g):
    out = dict(ffn_g=g["ffn_w_gate"], ffn_u=g["ffn_w_up"], ffn_d=g["ffn_w_down"])
    if layer % 2 == 0:
        out.update(gdn_in=_gdn_in_to_kernel(_cols(g["gdn_w_in"])), gdn_out=_rows(g["gdn_w_out"]))
    else:
        out.update(mla_in=_pad_cols(_rows(g["mla_w_in"]), MLA_INK), mla_uq=_mla_uq_to_kernel(_cols(g["mla_w_uq"])),
                   mla_ukv=_mla_ukv_to_kernel(_cols(g["mla_w_ukv"])), mla_out=_rows(g["mla_w_out"]))
    return out


def _small_to_kernel(norm_mix_g, norm_ffn_g, final_norm_g, gdn_conv_w, gdn_a_log, gdn_dt_bias, gdn_norm_g, q_norm_g, kv_norm_g):
    return dict(
        norm_mix_g=norm_mix_g, norm_ffn_g=norm_ffn_g, final_g=final_norm_g.reshape(1, D),
        gdn_cw=[jnp.transpose(gdn_conv_w[j]) for j in range(2)],
        gdn_alog=[_pad_cols(gdn_a_log[j:j + 1], HD) for j in range(2)],
        gdn_dtb=[_pad_cols(gdn_dt_bias[j:j + 1], HD) for j in range(2)],
        gdn_ng=[gdn_norm_g[j:j + 1] for j in range(2)],
        mla_qg=[q_norm_g[j:j + 1] for j in range(2)],
        mla_kvg=[kv_norm_g[j:j + 1] for j in range(2)],
    )


_CHIP_FLIPS = ((1, 0), (0, 1), (1, 1))
_ANY = pl.BlockSpec(memory_space=pl.ANY)


def _me():
    return lax.axis_index("x"), lax.axis_index("y"), lax.axis_index("c")


def _chip_peer(dx, dy):
    x, y, c = _me()
    return ((1 - x) if dx else x, (1 - y) if dy else y, c)


def _rcopy(src, dst, send_sem, recv_sem, to):
    return pltpu.make_async_remote_copy(src_ref=src, dst_ref=dst, send_sem=send_sem, recv_sem=recv_sem,
                                        device_id=to, device_id_type=MESH)


def _allgather4(name, a, halves=False):
    R, C = a.shape
    rh = R // 2 if halves else R

    def body(a_ref, out_ref, send_sems, recv_sems, local_sem):
        x, y, c = _me()
        me = 2 * x + y
        src = a_ref.at[pl.ds(c * rh, rh)] if halves else a_ref
        local = pltpu.make_async_copy(src, out_ref.at[me], local_sem)
        local.start()
        sends = []
        for k, (dx, dy) in enumerate(_CHIP_FLIPS):
            cp = _rcopy(src, out_ref.at[me], send_sems.at[k], recv_sems.at[k], _chip_peer(dx, dy))
            cp.start()
            sends.append(cp)
        for k, (dx, dy) in enumerate(_CHIP_FLIPS):
            px, py, _ = _chip_peer(dx, dy)
            _rcopy(src, out_ref.at[2 * px + py], send_sems.at[k], recv_sems.at[k], _chip_peer(dx, dy)).wait_recv()
        for cp in sends:
            cp.wait_send()
        local.wait()

    return pl.pallas_call(
        body, name=name, in_specs=[_ANY], out_specs=_ANY, out_shape=jax.ShapeDtypeStruct((4, rh, C), a.dtype),
        scratch_shapes=[pltpu.SemaphoreType.DMA((3,)), pltpu.SemaphoreType.DMA((3,)), pltpu.SemaphoreType.DMA(())])(a)


_NCH = 4


def _dma_sems(*counts):
    return [pltpu.SemaphoreType.DMA((n,)) for n in counts]


def _slot_tile(rows, cap=512):
    best = rows
    for tr in range(16, min(rows, cap) + 1, 16):
        if rows % tr == 0:
            best = tr
    return best


def _cast_into_slot(name, a, chip, row0, rows):
    C = a.shape[1]
    tr = _slot_tile(rows)
    assert row0 % tr == 0
    first = row0 // tr

    def body(c_ref, a_ref, o_ref):
        o_ref[0] = a_ref[...].astype(o_ref.dtype)

    grid_spec = pltpu.PrefetchScalarGridSpec(
        num_scalar_prefetch=1, grid=(rows // tr,), in_specs=[pl.BlockSpec((tr, C), lambda i, c_ref: (first + i, 0))],
        out_specs=pl.BlockSpec((1, tr, C), lambda i, c_ref: (c_ref[0], i, 0)))
    return pl.pallas_call(body, name=name, grid_spec=grid_spec, out_shape=jax.ShapeDtypeStruct((4, rows, C), BF16),
                          compiler_params=_params(1))(chip, a)


def _chunks(rows, align):
    for nch in (_NCH, 2):
        if rows % (nch * align) == 0:
            return nch
    return 1


def _gather_exchange(out, ici_s, ici_r, d2d_s, d2d_r):
    n = len(out)
    x, y, c = _me()
    me = 2 * x + y
    sib = (x, y, 1 - c)
    peers = [_chip_peer(dx, dy) for dx, dy in _CHIP_FLIPS]
    for t in range(n):
        h = out[t].shape[1] // 2
        nch = _chunks(h, 16)
        ch = h // nch
        for k, peer in enumerate(peers):
            for i in range(nch):
                blk = out[t].at[me, pl.ds(c * h + i * ch, ch)]
                _rcopy(blk, blk, ici_s.at[3 * t + k], ici_r.at[3 * t + k], peer).start()
    for t in range(n):
        h = out[t].shape[1] // 2
        nch = _chunks(h, 16)
        ch = h // nch
        for k, peer in enumerate(peers):
            pchip = 2 * peer[0] + peer[1]
            got = out[t].at[pchip, pl.ds(c * h, h)]
            _rcopy(got, got, ici_s.at[3 * t + k], ici_r.at[3 * t + k], peer).wait_recv()
            for i in range(nch):
                blk = out[t].at[pchip, pl.ds(c * h + i * ch, ch)]
                _rcopy(blk, blk, d2d_s.at[3 * t + k], d2d_r.at[3 * t + k], sib).start()
    for t in range(n):
        h = out[t].shape[1] // 2
        for k, peer in enumerate(peers):
            pchip = 2 * peer[0] + peer[1]
            other = out[t].at[pchip, pl.ds((1 - c) * h, h)]
            _rcopy(other, other, d2d_s.at[3 * t + k], d2d_r.at[3 * t + k], sib).wait_recv()
            _rcopy(other, other, ici_s.at[3 * t + k], ici_r.at[3 * t + k], peer).wait_send()
            _rcopy(other, other, d2d_s.at[3 * t + k], d2d_r.at[3 * t + k], sib).wait_send()


def _gather_weights(name, bufs):
    n = len(bufs)

    def body(*refs):
        _gather_exchange(refs[n:2 * n], *refs[2 * n:])

    return pl.pallas_call(
        body, name=name, in_specs=[_ANY] * n, out_specs=[_ANY] * n,
        out_shape=[jax.ShapeDtypeStruct(s.shape, s.dtype) for s in bufs],
        input_output_aliases={t: t for t in range(n)},
        scratch_shapes=_dma_sems(3 * n, 3 * n, 3 * n, 3 * n))(*bufs)


def _gather_weights_async(name, collective_id, bufs):
    n = len(bufs)
    refs = [jax.new_ref(b, memory_space=pltpu.MemorySpace.HBM) for b in bufs]

    @pl.kernel(mesh=plsc.ScalarSubcoreMesh(axis_name="sequencer", num_cores=1), name=name,
               scratch_types=tuple(_dma_sems(3 * n, 3 * n, 3 * n, 3 * n)),
               compiler_params=pltpu.CompilerParams(collective_id=collective_id))
    def launch(ici_s, ici_r, d2d_s, d2d_r):
        x, y, c = _me()
        barrier = pltpu.get_barrier_semaphore()
        for peer in [_chip_peer(dx, dy) for dx, dy in _CHIP_FLIPS] + [(x, y, 1 - c)]:
            pl.semaphore_signal(barrier, inc=1, device_id=peer, device_id_type=MESH)
        pl.semaphore_wait(barrier, 4)
        _gather_exchange(refs, ici_s, ici_r, d2d_s, d2d_r)

    launch()
    return [r[...] for r in refs]


def _rs_split(name, grads):
    n = len(grads)

    def body(*refs):
        g, out = refs[:n], refs[n:2 * n]
        send, recv = refs[2 * n:]
        x, y, c = _me()
        sib = (x, y, 1 - c)
        for t in range(n):
            h = g[t].shape[1] // 2
            for d in range(4):
                _rcopy(g[t].at[d, pl.ds((1 - c) * h, h)], out[t].at[d], send.at[t], recv.at[t], sib).start()
        for t in range(n):
            _rcopy(out[t], out[t], send.at[t], recv.at[t], sib).wait()

    return pl.pallas_call(
        body, name=name, in_specs=[_ANY] * n, out_specs=[_ANY] * n,
        out_shape=[jax.ShapeDtypeStruct((4, s.shape[1] // 2, s.shape[2]), s.dtype) for s in grads],
        scratch_shapes=_dma_sems(n, n))(*grads)


def _pair_add(name, g, theirs, core_chip):
    _, R, C = g.shape
    h = R // 2
    tr = _slot_tile(h)
    nb = h // tr

    def body(s_ref, g_ref, t_ref, p_ref, o_ref):
        val = (g_ref[...].astype(F32) + t_ref[...].astype(F32)).astype(p_ref.dtype)
        p_ref[...] = val

        @pl.when(pl.program_id(1) == s_ref[1])
        def _():
            o_ref[...] = val

    spec = pl.BlockSpec((1, tr, C), lambda i, d, s_ref: (d, i, 0))
    grid_spec = pltpu.PrefetchScalarGridSpec(
        num_scalar_prefetch=1, grid=(nb, 4),
        in_specs=[pl.BlockSpec((1, tr, C), lambda i, d, s_ref: (d, s_ref[0] * nb + i, 0)), spec],
        out_specs=[spec, pl.BlockSpec((1, tr, C), lambda i, d, s_ref: (s_ref[1], i, 0))])
    half = jax.ShapeDtypeStruct((4, h, C), BF16)
    return pl.pallas_call(body, name=name, grid_spec=grid_spec, out_shape=[half, half],
                          compiler_params=_params(2))(core_chip, g, theirs)


def _rs_alltoall_async(name, collective_id, parts, bufs):
    n = len(parts)
    p = [jax.new_ref(a, memory_space=pltpu.MemorySpace.HBM) for a in parts]
    out = [jax.new_ref(b, memory_space=pltpu.MemorySpace.HBM) for b in bufs]

    @pl.kernel(mesh=plsc.ScalarSubcoreMesh(axis_name="sequencer", num_cores=1), name=name,
               scratch_types=tuple(_dma_sems(3 * n, 3 * n)),
               compiler_params=pltpu.CompilerParams(collective_id=collective_id))
    def launch(send, recv):
        barrier = pltpu.get_barrier_semaphore()
        for peer in [_chip_peer(dx, dy) for dx, dy in _CHIP_FLIPS]:
            pl.semaphore_signal(barrier, inc=1, device_id=peer, device_id_type=MESH)
        pl.semaphore_wait(barrier, 3)
        _alltoall_exchange(p, out, send, recv)

    launch()
    return [r[...] for r in out]


def _alltoall_exchange(p, out, send, recv):
    x, y, c = _me()
    me = 2 * x + y
    peers = [_chip_peer(dx, dy) for dx, dy in _CHIP_FLIPS]
    for t in range(len(p)):
        h = p[t].shape[1]
        nch = _chunks(h, 16)
        ch = h // nch
        for k, peer in enumerate(peers):
            pchip = 2 * peer[0] + peer[1]
            for i in range(nch):
                rows = pl.ds(i * ch, ch)
                _rcopy(p[t].at[pchip, rows], out[t].at[me, rows], send.at[3 * t + k], recv.at[3 * t + k], peer).start()
    for t in range(len(p)):
        for k, peer in enumerate(peers):
            pchip = 2 * peer[0] + peer[1]
            _rcopy(out[t].at[pchip], out[t].at[pchip], send.at[3 * t + k], recv.at[3 * t + k], peer).wait()


def _rs_swap(name, halves):
    n = len(halves)

    def body(*refs):
        a, out = refs[:n], refs[n:2 * n]
        send, recv = refs[2 * n:]
        x, y, c = _me()
        sib = (x, y, 1 - c)
        for t in range(n):
            ch = a[t].shape[0] // _NCH
            for i in range(_NCH):
                rows = pl.ds(i * ch, ch)
                _rcopy(a[t].at[rows], out[t].at[rows], send.at[t], recv.at[t], sib).start()
        for t in range(n):
            _rcopy(a[t], out[t], send.at[t], recv.at[t], sib).wait()

    return pl.pallas_call(
        body, name=name, in_specs=[_ANY] * n, out_specs=[_ANY] * n,
        out_shape=[jax.ShapeDtypeStruct(s.shape, s.dtype) for s in halves],
        scratch_shapes=_dma_sems(n, n))(*halves)


def _sibling_merge(name, a):
    P_, rh, C = a.shape

    def body(a_ref, out_ref, send_sem, recv_sem, local_sem):
        x, y, c = _me()
        local = pltpu.make_async_copy(a_ref, out_ref.at[:, pl.ds(c * rh, rh)], local_sem)
        local.start()
        cp = _rcopy(a_ref, out_ref.at[:, pl.ds(c * rh, rh)], send_sem, recv_sem, (x, y, 1 - c))
        cp.start()
        cp.wait_send()
        _rcopy(a_ref, out_ref.at[:, pl.ds((1 - c) * rh, rh)], send_sem, recv_sem, (x, y, 1 - c)).wait_recv()
        local.wait()

    return pl.pallas_call(
        body, name=name, in_specs=[_ANY], out_specs=_ANY, out_shape=jax.ShapeDtypeStruct((P_, 2 * rh, C), a.dtype),
        scratch_shapes=[pltpu.SemaphoreType.DMA(()), pltpu.SemaphoreType.DMA(()), pltpu.SemaphoreType.DMA(())])(a)


def _allgather8(name, a):
    g4 = _allgather4(name + "_chips", a)
    both = _sibling_merge(name + "_cores", g4.reshape(1, 4 * a.shape[0], a.shape[1]))
    return jnp.transpose(both.reshape(2, 4, *a.shape), (1, 0, 2, 3)).reshape(8, *a.shape)


def _sum_slots(name, a, out_dtype):
    def fn(a):
        acc = a[0].astype(F32)
        for k in range(1, a.shape[0]):
            acc = acc + a[k].astype(F32)
        return acc
    return _rowwise(name, fn, [a], [], [(a.shape[2], out_dtype)])[0]


def _adamw_math(w, g, m, v):
    m = ADAM_B1 * m + (1.0 - ADAM_B1) * g
    v = ADAM_B2 * v + (1.0 - ADAM_B2) * (g * g)
    m_hat = m / (1.0 - ADAM_B1 ** ADAM_STEP)
    v_hat = v / (1.0 - ADAM_B2 ** ADAM_STEP)
    return -ADAM_LR * (m_hat / (jnp.sqrt(v_hat) + ADAM_EPS) + ADAM_WD * w), m, v


def _adamw_piece(name, w2, m2, v2, mine, theirs, row0, prev, core):
    R, C = w2.shape
    h = mine.shape[0]
    tr = _slot_tile(h, 256)
    nb = h // tr
    assert row0 % tr == 0
    first = row0 // tr

    def body(c_ref, w_ref, m_ref, v_ref, a_ref, b_ref, *rest):
        g_ref, d_ref, nm_ref, nv_ref = rest[-4:]
        g = jnp.where(pl.program_id(0) == c_ref[0], a_ref[...], b_ref[...])
        g_ref[...] = g
        d_ref[...], nm_ref[...], nv_ref[...] = _adamw_math(w_ref[...], g, m_ref[...], v_ref[...])

    full = pl.BlockSpec((tr, C), lambda s, i, c_ref: (first + s * nb + i, 0))
    mine_spec = pl.BlockSpec((tr, C), lambda s, i, c_ref: (jnp.where(s == c_ref[0], i, 0), 0))
    theirs_spec = pl.BlockSpec((tr, C), lambda s, i, c_ref: (jnp.where(s == c_ref[0], 0, i), 0))
    extra = [] if prev is None else list(prev)
    grid_spec = pltpu.PrefetchScalarGridSpec(
        num_scalar_prefetch=1, grid=(2, nb), in_specs=[full, full, full, mine_spec, theirs_spec] + [_ANY] * len(extra),
        out_specs=[full] * 4)
    return pl.pallas_call(
        body, name=name, grid_spec=grid_spec, out_shape=[jax.ShapeDtypeStruct((R, C), F32)] * 4,
        input_output_aliases={6 + k: k for k in range(len(extra))}, compiler_params=_params(2))(core, w2, m2, v2, mine, theirs, *extra)


def _adamw(name, w, g, m, v):
    shape = w.shape
    two_d = (-1, shape[-1]) if w.ndim > 1 else (1, -1)
    w2, g2, m2, v2 = [t.reshape(two_d) for t in (w, g, m, v)]
    rows = w2.shape[0]
    tr = rows
    for cand in (256, 128, 64, 32, 16, 8):
        if rows % cand == 0:
            tr = cand
            break

    c = w2.shape[1]
    outs = _rowwise(name, _adamw_math, [w2, g2, m2, v2], [], [(c, F32)] * 3, tr=tr)
    return [o.reshape(shape) for o in outs]


_WEIGHT_ORDER = ("ada_w", "ada_b", "norm_mix_g", "norm_ffn_g", "gdn_w_in", "gdn_conv_w", "gdn_a_log", "gdn_dt_bias",
                 "gdn_norm_g", "gdn_w_out", "mla_w_in", "mla_q_norm_g", "mla_kv_norm_g", "mla_w_uq", "mla_w_ukv",
                 "mla_w_out", "ffn_w_gate", "ffn_w_up", "ffn_w_down", "final_norm_g")
_BIG = (("gdn_w_in", 2), ("gdn_w_out", 1), ("mla_w_in", 1), ("mla_w_uq", 2), ("mla_w_ukv", 2), ("mla_w_out", 1),
        ("ffn_w_gate", 2), ("ffn_w_up", 2), ("ffn_w_down", 1))
_SMALL_SHARDED = (("gdn_conv_w", 1), ("mla_q_norm_g", 1), ("mla_kv_norm_g", 1))
_STORED_TRANSPOSED = ("ffn_w_gate", "ffn_w_up")


def _size(shape):
    n = 1
    for s in shape:
        n *= s
    return n


def _pack_rows_each(tensors):
    parts, offs, off = [], [], 0
    for t in tensors:
        flat = t.reshape(-1).astype(F32)
        rows = -(-flat.shape[0] // PACK_W)
        parts.append(jnp.pad(flat, (0, rows * PACK_W - flat.shape[0])).reshape(rows, PACK_W))
        offs.append(off)
        off += rows
    total = -(-off // 16) * 16
    pack = jnp.pad(parts[0], ((offs[0], total - offs[0] - parts[0].shape[0]), (0, 0)))
    for p, o in zip(parts[1:], offs[1:]):
        pack = pack + jnp.pad(p, ((o, total - o - p.shape[0]), (0, 0)))
    return pack, offs


def _unpack_rows_each(pack, shapes):
    lead = pack.shape[:-2]
    out, off = [], 0
    for shp in shapes:
        n = _size(shp)
        rows = -(-n // PACK_W)
        out.append(pack[..., off:off + rows, :].reshape(*lead, -1)[..., :n].reshape(*lead, *shp))
        off += rows
    return out


def _merge_chips(stacked, axis):
    moved = jnp.moveaxis(stacked, 0, axis)
    shp = list(moved.shape)
    return moved.reshape(shp[:axis] + [shp[axis] * shp[axis + 1]] + shp[axis + 2:])


def _my_shard(full, axis, chip):
    n = full.shape[axis] // 4
    return lax.dynamic_slice_in_dim(full, chip * n, n, axis)


def kernel(x, c, positions, ada_w, ada_b, norm_mix_g, norm_ffn_g, gdn_w_in, gdn_conv_w, gdn_a_log, gdn_dt_bias, gdn_norm_g, gdn_w_out, mla_w_in, mla_q_norm_g, mla_kv_norm_g, mla_w_uq, mla_w_ukv, mla_w_out, ffn_w_gate, ffn_w_up, ffn_w_down, final_norm_g, loss_target, m_ada_w, m_ada_b, m_norm_mix_g, m_norm_ffn_g, m_gdn_w_in, m_gdn_conv_w, m_gdn_a_log, m_gdn_dt_bias, m_gdn_norm_g, m_gdn_w_out, m_mla_w_in, m_mla_q_norm_g, m_mla_kv_norm_g, m_mla_w_uq, m_mla_w_ukv, m_mla_w_out, m_ffn_w_gate, m_ffn_w_up, m_ffn_w_down, m_final_norm_g, v_ada_w, v_ada_b, v_norm_mix_g, v_norm_ffn_g, v_gdn_w_in, v_gdn_conv_w, v_gdn_a_log, v_gdn_dt_bias, v_gdn_norm_g, v_gdn_w_out, v_mla_w_in, v_mla_q_norm_g, v_mla_kv_norm_g, v_mla_w_uq, v_mla_w_ukv, v_mla_w_out, v_ffn_w_gate, v_ffn_w_up, v_ffn_w_down, v_final_norm_g):
    w = dict(ada_w=ada_w, ada_b=ada_b, norm_mix_g=norm_mix_g, norm_ffn_g=norm_ffn_g, gdn_w_in=gdn_w_in, gdn_conv_w=gdn_conv_w,
             gdn_a_log=gdn_a_log, gdn_dt_bias=gdn_dt_bias, gdn_norm_g=gdn_norm_g, gdn_w_out=gdn_w_out, mla_w_in=mla_w_in,
             mla_q_norm_g=mla_q_norm_g, mla_kv_norm_g=mla_kv_norm_g, mla_w_uq=mla_w_uq, mla_w_ukv=mla_w_ukv,
             mla_w_out=mla_w_out, ffn_w_gate=ffn_w_gate, ffn_w_up=ffn_w_up, ffn_w_down=ffn_w_down, final_norm_g=final_norm_g)
    m = dict(ada_w=m_ada_w, ada_b=m_ada_b, norm_mix_g=m_norm_mix_g, norm_ffn_g=m_norm_ffn_g, gdn_w_in=m_gdn_w_in,
             gdn_conv_w=m_gdn_conv_w, gdn_a_log=m_gdn_a_log, gdn_dt_bias=m_gdn_dt_bias, gdn_norm_g=m_gdn_norm_g,
             gdn_w_out=m_gdn_w_out, mla_w_in=m_mla_w_in, mla_q_norm_g=m_mla_q_norm_g, mla_kv_norm_g=m_mla_kv_norm_g,
             mla_w_uq=m_mla_w_uq, mla_w_ukv=m_mla_w_ukv, mla_w_out=m_mla_w_out, ffn_w_gate=m_ffn_w_gate,
             ffn_w_up=m_ffn_w_up, ffn_w_down=m_ffn_w_down, final_norm_g=m_final_norm_g)
    v = dict(ada_w=v_ada_w, ada_b=v_ada_b, norm_mix_g=v_norm_mix_g, norm_ffn_g=v_norm_ffn_g, gdn_w_in=v_gdn_w_in,
             gdn_conv_w=v_gdn_conv_w, gdn_a_log=v_gdn_a_log, gdn_dt_bias=v_gdn_dt_bias, gdn_norm_g=v_gdn_norm_g,
             gdn_w_out=v_gdn_w_out, mla_w_in=v_mla_w_in, mla_q_norm_g=v_mla_q_norm_g, mla_kv_norm_g=v_mla_kv_norm_g,
             mla_w_uq=v_mla_w_uq, mla_w_ukv=v_mla_w_ukv, mla_w_out=v_mla_w_out, ffn_w_gate=v_ffn_w_gate,
             ffn_w_up=v_ffn_w_up, ffn_w_down=v_ffn_w_down, final_norm_g=v_final_norm_g)
    T = x.shape[1]
    ix, iy, ic = _me()
    chip = 2 * ix + iy
    seq = 2 * chip + ic
    n_dev = 8

    small_shapes = [w[n].shape for n, _ in _SMALL_SHARDED] + [c.shape]
    pack0, _ = _pack_rows_each([w[n] for n, _ in _SMALL_SHARDED] + [c])
    got0 = _unpack_rows_each(_allgather8("gather_small", pack0), small_shapes)
    small_full = {n: _merge_chips(g[0::2], ax) for (n, ax), g in zip(_SMALL_SHARDED, got0)}
    c_all = got0[-1].reshape(n_dev, D)

    big = [n for n, _ in _BIG]
    chip_arr = chip.astype(jnp.int32).reshape(1)

    def stored(n, t):
        return jnp.swapaxes(t, 1, 2) if n in _STORED_TRANSPOSED else t

    ws, ms, vs = [{n: stored(n, d[n]) for n in big} for d in (w, m, v)]
    two_d = lambda t: t.reshape(-1, t.shape[-1])

    gathered = []
    for l in range(DEPTH):
        names = _layer_weights(l)
        bufs = [_cast_into_slot(f"to_bf16_{n}{l}", two_d(ws[n]), chip_arr, j * ws[n].shape[1], ws[n].shape[1]) for n, j in names]
        filled = _gather_weights("gather_weights0", bufs) if l == 0 else _gather_weights_async(f"gather_weights{l}", l, bufs)
        gathered.append({n: b for (n, _), b in zip(names, filled)})

    def weights_of(l, h):
        return _weights_to_kernel(l, gathered[l])

    P = _small_to_kernel(norm_mix_g, norm_ffn_g, final_norm_g, small_full["gdn_conv_w"], gdn_a_log, gdn_dt_bias,
                         gdn_norm_g, small_full["mla_q_norm_g"], small_full["mla_kv_norm_g"])

    c16 = jnp.pad(c_all, ((0, 16 - n_dev), (0, 0)))
    ca = _rowwise("cond_silu", lambda t: t * _sig(t), [c16], [], [(D, BF16)])[0]
    n_ada = ada_w.shape[2]
    mods = jnp.concatenate([_mm(f"ada_fwd{l}", ca, ada_w[l], "nn") for l in range(DEPTH)], axis=0)
    mods_all = _allgather4("gather_mod", mods).reshape(4, DEPTH, 16, n_ada)
    mod_mm = jnp.transpose(lax.dynamic_index_in_dim(mods_all, seq, axis=2, keepdims=False), (1, 0, 2)).reshape(DEPTH, 4 * n_ada)
    mod = _rowwise("mod_bias", lambda a, b: a + b, [mod_mm, ada_b], [], [(4 * n_ada, F32)])[0]

    core_chip = jnp.stack([ic, chip]).astype(jnp.int32)
    pending, in_flight = {}, []

    def reduce_group(layer, part, pieces):
        pending.update({(n, layer if n.startswith("ffn_") else layer // 2): g for n, g in pieces.items()})
        if part == "ffn" and layer > 0:
            return
        keys = list(pending)
        glist = [pending.pop(k) for k in keys]
        tag = f"{layer}{part}"
        theirs = _rs_split("grads_cores_" + tag, glist)
        both = [_pair_add(f"grads_pair_{n}{l}", g, t, core_chip) for (n, l), g, t in zip(keys, glist, theirs)]
        swapped = _rs_alltoall_async("grads_chips_" + tag, DEPTH + 1 + len(in_flight), [p for p, _ in both], [o for _, o in both])
        in_flight.append((tag, keys, swapped))

    dx, dmod, gP = _local_step(x.reshape(T, D), loss_target.reshape(T, D), positions.reshape(T, 1), mod, weights_of, P, reduce_group)

    partials = [dmod, jnp.concatenate(gP["norm_mix_g"]), jnp.concatenate(gP["norm_ffn_g"]), gP["final_g"],
                jnp.stack([jnp.transpose(g) for g in gP["gdn_cw"]]), jnp.concatenate(gP["gdn_alog"])[:, :NH],
                jnp.concatenate(gP["gdn_dtb"])[:, :NH], jnp.concatenate(gP["gdn_ng"]), jnp.concatenate(gP["mla_qg"]),
                jnp.concatenate(gP["mla_kvg"]), gP["loss"][:, :1]]
    part_shapes = [p.shape for p in partials]
    ppack, _ = _pack_rows_each(partials)
    pall = _allgather8("gather_partials", ppack)
    psum = _sum_slots("sum_partials", pall, F32)
    (g_ada_b, g_norm_mix, g_norm_ffn, g_final, g_conv_full, g_alog, g_dtb, g_gdn_ng, g_qg_full, g_kvg_full,
     loss_sum) = _unpack_rows_each(psum, part_shapes)
    dmod_all = _unpack_rows_each(pall, part_shapes[:1])[0]

    grads = dict(ada_b=g_ada_b, norm_mix_g=g_norm_mix, norm_ffn_g=g_norm_ffn, final_norm_g=g_final.reshape(D),
                 gdn_conv_w=_my_shard(g_conv_full, 1, chip), gdn_a_log=g_alog, gdn_dt_bias=g_dtb, gdn_norm_g=g_gdn_ng,
                 mla_q_norm_g=_my_shard(g_qg_full, 1, chip), mla_kv_norm_g=_my_shard(g_kvg_full, 1, chip))

    ca_t = jnp.zeros((D, LANES), BF16).at[:, :16].set(jnp.transpose(ca))
    dm_mine = lax.dynamic_slice_in_dim(dmod_all, chip * n_ada, n_ada, axis=2)
    grads["ada_w"] = jnp.stack([
        _mm(f"ada_bwd{l}", ca_t, jnp.pad(dm_mine[:, l], ((0, LANES - n_dev), (0, 0))), "nn") for l in range(DEPTH)])

    delta, new_m, new_v = {}, {}, {}
    results = {}
    keys = [k for _, ks, _ in in_flight for k in ks]
    halves = [_sum_slots(f"grads_sum_{n}{l}", s, F32) for _, ks, sw in in_flight for (n, l), s in zip(ks, sw)]
    others = _rs_swap("grads_swap", halves)
    for (n, l), mine, theirs in zip(keys, halves, others):
        results[n] = _adamw_piece(f"adamw_{n}{l}", two_d(ws[n]), two_d(ms[n]), two_d(vs[n]), mine, theirs,
                                  l * ws[n].shape[1], results.get(n), core_chip[:1])
    for n in big:
        grads[n], delta[n], new_m[n], new_v[n] = [stored(n, t.reshape(ws[n].shape)) for t in results[n]]
    delta["ada_w"], new_m["ada_w"], new_v["ada_w"] = _adamw("adamw_ada_w", ada_w, grads["ada_w"], m_ada_w, v_ada_w)
    small_names = [n for n in _WEIGHT_ORDER if n not in delta]
    small_shapes = [w[n].shape for n in small_names]
    packs = [_pack_rows_each([d[n] for n in small_names])[0] for d in (w, grads, m, v)]
    for d, pk in zip((delta, new_m, new_v), _adamw("adamw_small", *packs)):
        for n, t in zip(small_names, _unpack_rows_each(pk, small_shapes)):
            d[n] = t

    loss = loss_sum.reshape(())
    return (loss, dx.reshape(1, T, D), *[grads[n] for n in _WEIGHT_ORDER], *[delta[n] for n in _WEIGHT_ORDER],
            *[new_m[n] for n in _WEIGHT_ORDER], *[new_v[n] for n in _WEIGHT_ORDER])
```

```python
import functools

import jax
import jax.numpy as jnp
from jax import lax
from jax.experimental import pallas as pl
from jax.experimental.pallas import tpu as pltpu
from jax.experimental.pallas import tpu_sc as plsc

F32 = jnp.float32
BF16 = jnp.bfloat16
HI = lax.Precision.HIGHEST
MESH = pl.DeviceIdType.MESH

D = 1024
DEPTH = 4
N_MOD = 6
NH = 8
HD = 128
CHUNK = 64
_GDN_HB = 8
GDN_QKV = 3 * NH * HD
GDN_INK = GDN_QKV + NH * HD + 2 * HD
Q_RANK, KV_RANK, ROPE = 384, 256, 64
MLA_INK = Q_RANK + KV_RANK + HD
DFF = 2816
EPS = 1e-6
ATT_SCALE = (HD + ROPE) ** -0.5
ROPE_THETA = 10000.0
LANES = 128
PACK_W = 1024

ADAM_LR, ADAM_B1, ADAM_B2, ADAM_EPS, ADAM_WD, ADAM_STEP = 0.001, 0.9, 0.999, 1e-08, 0.01, 10


H3 = "bf16x3"
B1 = "bf16"
HS = H3
HF = B1


def _dot(a, b, mode="nn", prec=None):
    dn = {"nn": (((1,), (0,)), ((), ())), "nt": (((1,), (1,)), ((), ())), "tn": (((0,), (0,)), ((), ()))}[mode]
    if prec == B1:
        return _dot(a.astype(BF16), b.astype(BF16), mode)
    if prec == H3:
        ah, bh = a.astype(BF16), b.astype(BF16)
        al, bl = (a - ah.astype(F32)).astype(BF16), (b - bh.astype(F32)).astype(BF16)
        return _dot(ah, bh, mode) + (_dot(ah, bl, mode) + _dot(al, bh, mode))
    return lax.dot_general(a, b, dn, precision=prec, preferred_element_type=F32)


def _sig(x):
    return 1.0 / (1.0 + jnp.exp(-x))


def _pick(n, cap):
    if n <= cap:
        return n
    best = None
    for d in range(LANES, cap + 1, LANES):
        if n % d == 0:
            best = d
    assert best is not None, (n, cap)
    return best


def _params(n_grid):
    return pltpu.CompilerParams(dimension_semantics=("arbitrary",) * n_grid, vmem_limit_bytes=56 * 1024 * 1024)


def _rowwise(name, fn, rows, consts, outs, sums=(), tr=256):
    first = rows[0][0] if isinstance(rows[0], tuple) else rows[0]
    T = first.shape[-2]
    tr = _slot_tile(T, tr)
    nr, nc, no, ns = len(rows), len(consts), len(outs), len(sums)

    windows = [c[1:] if isinstance(c, tuple) else None for c in consts]
    consts = [c[0] if isinstance(c, tuple) else c for c in consts]

    def body(*refs):
        vals = [r[...] for r in refs[:nr]]
        for r, win in zip(refs[nr:nr + nc], windows):
            vals.append(r[...] if win is None else r[win[0]:win[0] + 1, win[1] * win[2]:(win[1] + 1) * win[2]])
        res = fn(*vals)
        if not isinstance(res, (tuple, list)):
            res = (res,)
        o_refs = refs[nr + nc:nr + nc + no]
        s_refs = refs[nr + nc + no:]
        for r, val in zip(o_refs, res[:no]):
            r[...] = val.astype(r.dtype)
        if ns:
            @pl.when(pl.program_id(0) == 0)
            def _():
                for r in s_refs:
                    r[...] = jnp.zeros_like(r)
            for r, val in zip(s_refs, res[no:]):
                r[...] += val

    in_specs, args = [], []
    for a in rows:
        if isinstance(a, tuple):
            arr, width, cb = a
            in_specs.append(pl.BlockSpec((tr, width), lambda i, cb=cb: (i, cb)))
            args.append(arr)
        elif a.ndim == 3:
            in_specs.append(pl.BlockSpec((a.shape[0], tr, a.shape[2]), lambda i: (0, i, 0)))
            args.append(a)
        else:
            in_specs.append(pl.BlockSpec((tr, a.shape[1]), lambda i: (i, 0)))
            args.append(a)
    for a in consts:
        in_specs.append(pl.BlockSpec(a.shape, lambda i, nd=a.ndim: (0,) * nd))
        args.append(a)
    out_specs = [pl.BlockSpec((tr, w), lambda i: (i, 0)) for w, _ in outs]
    out_specs += [pl.BlockSpec((1, w), lambda i: (0, 0)) for w in sums]
    out_shape = [jax.ShapeDtypeStruct((T, w), dt) for w, dt in outs]
    out_shape += [jax.ShapeDtypeStruct((1, w), F32) for w in sums]
    res = pl.pallas_call(body, name=name, grid=(T // tr,), in_specs=in_specs, out_specs=out_specs,
                         out_shape=out_shape, compiler_params=_params(1))(*args)
    return res


def _mm(name, a, b, mode, out_dtype=F32, tm=512, tn=1024):
    if mode == "tn":
        K, M = a.shape
    else:
        M, K = a.shape
    N = b.shape[0] if mode == "nt" else b.shape[1]
    tm, tn = _pick(M, tm), _pick(N, tn)

    def body(a_ref, b_ref, o_ref):
        o_ref[...] = _dot(a_ref[...].astype(BF16), b_ref[...].astype(BF16), mode).astype(o_ref.dtype)

    a_spec = pl.BlockSpec((K, tm), lambda i, j: (0, i)) if mode == "tn" else pl.BlockSpec((tm, K), lambda i, j: (i, 0))
    b_spec = pl.BlockSpec((tn, K), lambda i, j: (j, 0)) if mode == "nt" else pl.BlockSpec((K, tn), lambda i, j: (0, j))
    return pl.pallas_call(body, name=name, grid=(M // tm, N // tn), in_specs=[a_spec, b_spec],
                          out_specs=pl.BlockSpec((tm, tn), lambda i, j: (i, j)),
                          out_shape=jax.ShapeDtypeStruct((M, N), out_dtype), compiler_params=_params(2))(a, b)


def _rms(x, eps=EPS):
    return lax.rsqrt(jnp.mean(x * x, axis=-1, keepdims=True) + eps)


def _norm_mod_fwd(name, x, g, scale, shift):
    def fn(x, g, scale, shift):
        return x * _rms(x) * g * (1.0 + scale) + shift
    return _rowwise(name, fn, [x], [g, scale, shift], [(D, BF16)])[0]


def _norm_mod_bwd(name, dh, x, dx_res, g, scale):
    def fn(dh, x, dx_res, g, scale):
        r = _rms(x)
        xh = x * r
        dxh = dh * (g * (1.0 + scale))
        dx = r * (dxh - xh * jnp.mean(dxh * xh, axis=-1, keepdims=True))
        dhx = dh * xh
        return (dx_res + dx, jnp.sum(dh, axis=0, keepdims=True), jnp.sum(dhx * g, axis=0, keepdims=True),
                jnp.sum(dhx * (1.0 + scale), axis=0, keepdims=True))
    return _rowwise(name, fn, [dh, x, dx_res], [g, scale], [(D, F32)], sums=[D, D, D])


def _residual_fwd(name, x, y, gate):
    def fn(x, y, gate):
        return x + gate * y
    return _rowwise(name, fn, [x, y], [gate], [(D, F32)])[0]


def _residual_norm_fwd(name, x, y, gate, g, scale, shift):
    def fn(x, y, gate, g, scale, shift):
        x = x + gate * y
        return x, x * _rms(x) * g * (1.0 + scale) + shift
    return _rowwise(name, fn, [x, y], [gate, g, scale, shift], [(D, F32), (D, BF16)])


def _norm_residual_bwd(name, dh, x, dx_res, g, scale, y, gate):
    def fn(dh, x, dx_res, y, g, scale, gate):
        r = _rms(x)
        xh = x * r
        dxh = dh * (g * (1.0 + scale))
        dx = dx_res + r * (dxh - xh * jnp.mean(dxh * xh, axis=-1, keepdims=True))
        dhx = dh * xh
        return (dx, dx * gate, jnp.sum(dh, axis=0, keepdims=True), jnp.sum(dhx * g, axis=0, keepdims=True),
                jnp.sum(dhx * (1.0 + scale), axis=0, keepdims=True), jnp.sum(dx * y, axis=0, keepdims=True))
    return _rowwise(name, fn, [dh, x, dx_res, y], [g, scale, gate], [(D, F32), (D, BF16)], sums=[D, D, D, D])


def _residual_bwd(name, dx, y, gate):
    def fn(dx, y, gate):
        return dx * gate, jnp.sum(dx * y, axis=0, keepdims=True)
    return _rowwise(name, fn, [dx, y], [gate], [(D, BF16)], sums=[D])


def _loss_head(x, target, g):
    def fn(x, t, g):
        r = _rms(x)
        xh = x * r
        err = xh * g - t
        loss = 0.5 * jnp.sum(jnp.mean(err * err, axis=-1, keepdims=True), axis=0, keepdims=True)
        dy = err * (1.0 / D)
        dxh = dy * g
        dx = r * (dxh - xh * jnp.mean(dxh * xh, axis=-1, keepdims=True))
        return dx, jnp.broadcast_to(loss, (1, LANES)), jnp.sum(dy * xh, axis=0, keepdims=True)
    return _rowwise("loss_head", fn, [x, target], [g], [(D, F32)], sums=[LANES, D])


def _ffn_up(name, h, wg, wu, layer, tm=1024):
    T, n = h.shape[0], wg.shape[1]
    tm = min(tm, T)

    def body(h_ref, wg_ref, wu_ref, a_ref, b_ref, s_ref):
        h = h_ref[...]
        a = _dot(h, wg_ref[0], "nt")
        b = _dot(h, wu_ref[0], "nt")
        a_ref[0] = a.astype(a_ref.dtype)
        b_ref[0] = b.astype(b_ref.dtype)
        s_ref[0] = (a * _sig(a) * b).astype(s_ref.dtype)

    wspec = pl.BlockSpec((1, n, D), lambda ch, i: (ch, layer, 0))
    ospec = pl.BlockSpec((1, tm, n), lambda ch, i: (ch, i, 0))
    return pl.pallas_call(
        body, name=name, grid=(4, T // tm), in_specs=[pl.BlockSpec((tm, D), lambda ch, i: (i, 0)), wspec, wspec],
        out_specs=[ospec, ospec, ospec],
        out_shape=[jax.ShapeDtypeStruct((4, T, n), BF16)] * 3, compiler_params=_params(2))(h, wg, wu)


def _ffn_down(name, s, wd, layer, tm=1024):
    _, T, n = s.shape
    tm = min(tm, T)

    def body(s_ref, w_ref, y_ref):
        @pl.when(pl.program_id(1) == 0)
        def _():
            y_ref[...] = jnp.zeros_like(y_ref)
        y_ref[...] += _dot(s_ref[0], w_ref[0], "nn")

    return pl.pallas_call(
        body, name=name, grid=(T // tm, 4),
        in_specs=[pl.BlockSpec((1, tm, n), lambda i, ch: (ch, i, 0)), pl.BlockSpec((1, n, D), lambda i, ch: (ch, layer, 0))],
        out_specs=pl.BlockSpec((tm, D), lambda i, ch: (i, 0)), out_shape=jax.ShapeDtypeStruct((T, D), F32),
        compiler_params=_params(2))(s, wd)


def _ffn_down_bwd(name, dy, wd, a, b, layer, tm=1024):
    _, T, n = a.shape
    tm = min(tm, T)

    def body(dy_ref, w_ref, a_ref, b_ref, da_ref, db_ref):
        ds = _dot(dy_ref[...], w_ref[0], "nt")
        a, b = a_ref[0].astype(F32), b_ref[0].astype(F32)
        sg = _sig(a)
        da_ref[0] = (ds * b * (sg * (1.0 + a * (1.0 - sg)))).astype(da_ref.dtype)
        db_ref[0] = (ds * (a * sg)).astype(db_ref.dtype)

    bspec = pl.BlockSpec((1, tm, n), lambda ch, i: (ch, i, 0))
    return pl.pallas_call(
        body, name=name, grid=(4, T // tm),
        in_specs=[pl.BlockSpec((tm, D), lambda ch, i: (i, 0)), pl.BlockSpec((1, n, D), lambda ch, i: (ch, layer, 0)), bspec, bspec],
        out_specs=[bspec, bspec], out_shape=[jax.ShapeDtypeStruct((4, T, n), BF16)] * 2,
        compiler_params=_params(2))(dy, wd, a, b)


def _ffn_down_dw(name, s, dy):
    _, T, n = s.shape

    def body(s_ref, dy_ref, o_ref):
        o_ref[0] = _dot(s_ref[0], dy_ref[...], "tn").astype(o_ref.dtype)

    return pl.pallas_call(
        body, name=name, grid=(4,),
        in_specs=[pl.BlockSpec((1, T, n), lambda ch: (ch, 0, 0)), pl.BlockSpec((T, D), lambda ch: (0, 0))],
        out_specs=pl.BlockSpec((1, n, D), lambda ch: (ch, 0, 0)), out_shape=jax.ShapeDtypeStruct((4, n, D), BF16),
        compiler_params=_params(1))(s, dy)


def _ffn_up_dw(name, h, da, db, tm=512):
    _, T, n = da.shape

    def body(h_ref, da_ref, db_ref, dg_ref, du_ref):
        h = h_ref[...]
        dg_ref[0] = _dot(da_ref[0], h, "tn").astype(dg_ref.dtype)
        du_ref[0] = _dot(db_ref[0], h, "tn").astype(du_ref.dtype)

    dspec = pl.BlockSpec((1, T, n), lambda ch, j: (ch, 0, 0))
    ospec = pl.BlockSpec((1, n, tm), lambda ch, j: (ch, 0, j))
    return pl.pallas_call(
        body, name=name, grid=(4, D // tm), in_specs=[pl.BlockSpec((T, tm), lambda ch, j: (0, j)), dspec, dspec],
        out_specs=[ospec, ospec], out_shape=[jax.ShapeDtypeStruct((4, n, D), BF16)] * 2,
        compiler_params=_params(2))(h, da, db)


def _ffn_up_dx(name, da, db, wg, wu, layer, tm=1024):
    _, T, n = da.shape
    tm = min(tm, T)

    def body(da_ref, db_ref, wg_ref, wu_ref, o_ref):
        @pl.when(pl.program_id(1) == 0)
        def _():
            o_ref[...] = jnp.zeros_like(o_ref)
        o_ref[...] += _dot(da_ref[0], wg_ref[0], "nn") + _dot(db_ref[0], wu_ref[0], "nn")

    dspec = pl.BlockSpec((1, tm, n), lambda i, ch: (ch, i, 0))
    wspec = pl.BlockSpec((1, n, D), lambda i, ch: (ch, layer, 0))
    return pl.pallas_call(
        body, name=name, grid=(T // tm, 4), in_specs=[dspec, dspec, wspec, wspec],
        out_specs=pl.BlockSpec((tm, D), lambda i, ch: (i, 0)), out_shape=jax.ShapeDtypeStruct((T, D), F32),
        compiler_params=_params(2))(da, db, wg, wu)


def _shift_down(x, k):
    if k == 0:
        return x
    rows = lax.broadcasted_iota(jnp.int32, x.shape, 0)
    return jnp.where(rows >= k, pltpu.roll(x, k, 0), 0.0)


def _shift_up(x, k):
    if k == 0:
        return x
    T = x.shape[0]
    rows = lax.broadcasted_iota(jnp.int32, x.shape, 0)
    return jnp.where(rows < T - k, pltpu.roll(x, T - k, 0), 0.0)


def _conv_silu(x, w):
    c = w[0:1, :] * _shift_down(x, 3) + w[1:2, :] * _shift_down(x, 2) + w[2:3, :] * _shift_down(x, 1) + w[3:4, :] * x
    sg = _sig(c)
    return c, sg, c * sg


def _gdn_conv_fwd(name, proj, cw):
    T = proj.shape[0]

    def body(x_ref, w_ref, o_ref):
        j = pl.program_id(0)
        _, _, y = _conv_silu(x_ref[...], w_ref[...])
        r = lax.rsqrt(jnp.sum(y * y, axis=1, keepdims=True) + EPS)
        mult = jnp.where(j < NH, HD ** -0.5, 1.0)
        o_ref[...] = jnp.where(j < 2 * NH, y * (r * mult), y)

    return pl.pallas_call(body, name=name, grid=(3 * NH,),
                          in_specs=[pl.BlockSpec((T, HD), lambda j: (0, j)), pl.BlockSpec((4, HD), lambda j: (0, j))],
                          out_specs=pl.BlockSpec((T, HD), lambda j: (0, j)),
                          out_shape=jax.ShapeDtypeStruct((T, GDN_QKV), F32), compiler_params=_params(1))(proj, cw)


def _gdn_conv_bwd(name, proj, cw, dz):
    T = proj.shape[0]

    def body(x_ref, w_ref, dz_ref, dx_ref, dw_ref):
        j = pl.program_id(0)
        x, w, dz = x_ref[...], w_ref[...], dz_ref[...]
        c, sg, y = _conv_silu(x, w)
        r = lax.rsqrt(jnp.sum(y * y, axis=1, keepdims=True) + EPS)
        mult = jnp.where(j < NH, HD ** -0.5, 1.0)
        dyn = mult * (r * dz - (r * r * r) * y * jnp.sum(dz * y, axis=1, keepdims=True))
        dy = jnp.where(j < 2 * NH, dyn, dz)
        dc = dy * (sg * (1.0 + c * (1.0 - sg)))
        dx = w[0:1, :] * _shift_up(dc, 3) + w[1:2, :] * _shift_up(dc, 2) + w[2:3, :] * _shift_up(dc, 1) + w[3:4, :] * dc
        dx_ref[...] = dx.astype(dx_ref.dtype)
        for k in range(4):
            dw_ref[pl.ds(k, 1), :] = jnp.sum(dc * _shift_down(x, 3 - k), axis=0, keepdims=True)

    return pl.pallas_call(body, name=name, grid=(3 * NH,),
                          in_specs=[pl.BlockSpec((T, HD), lambda j: (0, j)), pl.BlockSpec((4, HD), lambda j: (0, j)),
                                    pl.BlockSpec((T, HD), lambda j: (0, j))],
                          out_specs=[pl.BlockSpec((T, HD), lambda j: (0, j)), pl.BlockSpec((4, HD), lambda j: (0, j))],
                          out_shape=[jax.ShapeDtypeStruct((T, GDN_QKV), BF16), jax.ShapeDtypeStruct((4, GDN_QKV), F32)],
                          compiler_params=_params(1))(proj, cw, dz)


def _softplus(z):
    return jnp.maximum(z, 0.0) + jnp.log(1.0 + jnp.exp(-jnp.abs(z)))


_AB_CB = GDN_INK // (2 * HD) - 1


def _gdn_gates_fwd(name, proj, alog, dtb):
    def fn(ab, alog, dtb):
        a, b = ab[:, :HD], ab[:, HD:]
        return -jnp.exp(alog) * _softplus(a + dtb), _sig(b)
    return _rowwise(name, fn, [(proj, 2 * HD, _AB_CB)], [alog, dtb], [(HD, F32), (HD, F32)])


def _gdn_gates_bwd(name, proj, dg_h, db_h, alog, dtb):
    def fn(ab, dg_h, db_h, alog, dtb):
        lane = lax.broadcasted_iota(jnp.int32, (1, HD), 1)
        dg = jnp.zeros(dg_h.shape[1:], F32)
        dbeta = jnp.zeros(dg_h.shape[1:], F32)
        for h in range(NH):
            oh = (lane == h).astype(F32)
            dg = dg + dg_h[h] * oh
            dbeta = dbeta + db_h[h] * oh
        a, b = ab[:, :HD], ab[:, HD:]
        z = a + dtb
        ea = jnp.exp(alog)
        beta = _sig(b)
        da = dg * (-ea) * _sig(z)
        db = dbeta * beta * (1.0 - beta)
        return (jnp.concatenate([da, db], axis=1), jnp.sum(dg * (-ea * _softplus(z)), axis=0, keepdims=True),
                jnp.sum(da, axis=0, keepdims=True))
    return _rowwise(name, fn, [(proj, 2 * HD, _AB_CB), dg_h, db_h], [alog, dtb], [(2 * HD, BF16)], sums=[HD, HD])


def _interleave(gens):
    gens = list(gens)
    results = [None] * len(gens)
    active = list(range(len(gens)))
    while active:
        for i in list(active):
            try:
                next(gens[i])
            except StopIteration as stop:
                results[i] = stop.value
                active.remove(i)
    return results


def _chunk_common(q, k, v, gblk, bblk, h, prec):
    C = CHUNK
    lane = lax.broadcasted_iota(jnp.int32, (1, HD), 1)
    oh = (lane == h).astype(F32)
    g_col = jnp.sum(gblk * oh, axis=1, keepdims=True)
    beta = jnp.sum(bblk * oh, axis=1, keepdims=True)
    ri = lax.broadcasted_iota(jnp.int32, (C, C), 0)
    ci = lax.broadcasted_iota(jnp.int32, (C, C), 1)
    incl = ri >= ci
    strict = ri > ci
    eye = (ri == ci).astype(F32)
    gcb = _dot(incl.astype(F32), jnp.broadcast_to(g_col, (C, HD)), "nn", HI)
    yield
    gc = gcb[:, :C]
    gc_row = _dot(jnp.ones((C, C), F32), eye * gc, "nn", HI)
    yield
    decay = jnp.where(incl, jnp.exp(jnp.where(incl, gc - gc_row, 0.0)), 0.0)
    rows = lax.broadcasted_iota(jnp.int32, (C, HD), 0)
    gclb = jnp.sum(jnp.where(rows == C - 1, gcb, 0.0), axis=0, keepdims=True)
    eg = jnp.exp(gcb)
    egl = jnp.exp(gclb - gcb)
    gl = jnp.exp(gclb)
    kb = k * beta
    m1 = _dot(kb, k, "nt", prec)
    qk = _dot(q, k, "nt", prec)
    yield
    L = jnp.where(strict, m1 * decay, 0.0)
    nl = -L
    tinv = eye + nl
    p = nl
    for _ in range(5):
        p = _dot(p, p, "nn", H3)
        yield
        tinv = tinv + _dot(tinv, p, "nn", H3)
    vb = v * beta
    kbg = kb * eg
    yield
    u = _dot(tinv, vb, "nn", prec)
    w = _dot(tinv, kbg, "nn", prec)
    yield
    attn = jnp.where(incl, qk * decay, 0.0)
    return dict(beta=beta, incl=incl, strict=strict, decay=decay, eg=eg, egl=egl, gl=gl, kb=kb, m1=m1, tinv=tinv,
                kbg=kbg, u=u, w=w, qk=qk, attn=attn, q_dec=q * eg, k_dec=k * egl, rows=rows, oh=oh)


def _gdn_chunk_fwd(name, qkv, g, beta):
    T = qkv.shape[0]
    N = T // CHUNK

    hb = _GDN_HB
    w = hb * HD

    def body(q_ref, k_ref, v_ref, g_ref, b_ref, o_ref, st_ref, S):
        hg, n = pl.program_id(0), pl.program_id(1)

        @pl.when(n == 0)
        def _():
            S[...] = jnp.zeros_like(S)

        gblk, bblk = g_ref[...], b_ref[...]

        def one_head(i, q, k, v, s):
            c = yield from _chunk_common(q, k, v, gblk, bblk, hg * hb + i, HF)
            v_new = c["u"] - _dot(c["w"], s, "nn", HF)
            qs = _dot(c["q_dec"], s, "nn", HF)
            yield
            o = qs + _dot(c["attn"], v_new, "nn", HF)
            return o, s * c["gl"] + _dot(c["k_dec"], v_new, "tn", HF)

        sls = [slice(i * HD, (i + 1) * HD) for i in range(hb)]
        states = [S[i] for i in range(hb)]
        res = _interleave(one_head(i, q_ref[:, sls[i]], k_ref[:, sls[i]], v_ref[:, sls[i]], states[i]) for i in range(hb))
        for i, (o, s_new) in enumerate(res):
            st_ref[i, 0] = states[i]
            o_ref[:, sls[i]] = o
            S[i] = s_new

    blk = lambda off: pl.BlockSpec((CHUNK, w), lambda h, n, off=off: (n, off + h))
    gspec = pl.BlockSpec((CHUNK, HD), lambda h, n: (n, 0))
    return pl.pallas_call(
        body, name=name, grid=(NH // hb, N), in_specs=[blk(0), blk(NH // hb), blk(2 * NH // hb), gspec, gspec],
        out_specs=[pl.BlockSpec((CHUNK, w), lambda h, n: (n, h)), pl.BlockSpec((hb, 1, HD, HD), lambda h, n: (h, n, 0, 0))],
        out_shape=[jax.ShapeDtypeStruct((T, NH * HD), F32), jax.ShapeDtypeStruct((NH, N, HD, HD), F32)],
        scratch_shapes=[pltpu.VMEM((hb, HD, HD), F32)], compiler_params=_params(2))(qkv, qkv, qkv, g, beta)


def _gdn_chunk_bwd(name, qkv, g, beta, states, do):
    T = qkv.shape[0]
    N = T // CHUNK
    C = CHUNK

    hb = _GDN_HB
    w = hb * HD
    assert hb == NH

    def body(q_ref, k_ref, v_ref, g_ref, b_ref, st_ref, do_ref, dqkv_ref, dg_ref, db_ref, dS):
        hg, n = pl.program_id(0), pl.program_id(1)

        @pl.when(n == 0)
        def _():
            dS[...] = jnp.zeros_like(dS)

        gblk, bblk = g_ref[...], b_ref[...]
        sls = [slice(i * HD, (i + 1) * HD) for i in range(hb)]
        res = _interleave(one_head(hg * hb + i, gblk, bblk, q_ref[:, sls[i]], k_ref[:, sls[i]], v_ref[:, sls[i]],
                                   st_ref[i, 0], do_ref[:, sls[i]], dS[i]) for i in range(hb))
        for i, (dq, dk, dv, dg, db, ds_new) in enumerate(res):
            dqkv_ref[:, sls[i]] = dq
            dqkv_ref[:, slice(w + i * HD, w + (i + 1) * HD)] = dk
            dqkv_ref[:, slice(2 * w + i * HD, 2 * w + (i + 1) * HD)] = dv
            dg_ref[i] = dg
            db_ref[i] = db
            dS[i] = ds_new

    def one_head(h, gblk, bblk, q, k, v, s, do, ds):
        c = yield from _chunk_common(q, k, v, gblk, bblk, h, HF)
        eg, egl, gl, beta, decay, tinv = c["eg"], c["egl"], c["gl"], c["beta"], c["decay"], c["tinv"]
        v_new = c["u"] - _dot(c["w"], s, "nn", HF)
        dq_dec = _dot(do, s, "nt", HF)
        yield
        dv_new = _dot(c["attn"], do, "tn", HF) + _dot(c["k_dec"], ds, "nn", HF)
        dk_dec = _dot(v_new, ds, "nt", HF)
        dgl = jnp.sum(jnp.sum(s * ds, axis=1, keepdims=True), axis=0, keepdims=True)
        yield
        ds_new = ds * gl + _dot(c["q_dec"], do, "tn", HF) - _dot(c["w"], dv_new, "tn", HF)
        dattn = jnp.where(c["incl"], _dot(do, v_new, "nt", HF), 0.0)
        dw = -_dot(dv_new, s, "nt", HF)
        yield
        dvb = _dot(tinv, dv_new, "tn", HS)
        dkbg = _dot(tinv, dw, "tn", HS)
        yield
        dA = -(_dot(dvb, c["u"], "nt", HS) + _dot(dkbg, c["w"], "nt", HS))
        yield
        dL = jnp.where(c["strict"], dA, 0.0)
        dm1 = dL * decay
        dqk = dattn * decay
        xdec = (dL * c["m1"] + dattn * c["qk"]) * decay
        dkb = _dot(dm1, k, "nn", HS) + dkbg * eg
        dk = _dot(dm1, c["kb"], "tn", HS) + _dot(dqk, q, "tn", HS) + dk_dec * egl + dkb * beta
        dq = _dot(dqk, k, "nn", HS) + dq_dec * eg
        yield
        dkd_kd = jnp.sum(dk_dec * c["k_dec"], axis=1, keepdims=True)
        dgc = (jnp.sum(xdec, axis=1, keepdims=True) - _dot(xdec, jnp.ones((C, HD), F32), "tn", HS)
               + jnp.sum(dq_dec * c["q_dec"], axis=1, keepdims=True) - dkd_kd
               + jnp.sum(dkbg * c["kbg"], axis=1, keepdims=True))
        dgcl = jnp.sum(dkd_kd, axis=0, keepdims=True) + dgl * gl
        dgc = dgc + jnp.where(c["rows"] == C - 1, dgcl, 0.0)
        ri = lax.broadcasted_iota(jnp.int32, (C, C), 0)
        ci = lax.broadcasted_iota(jnp.int32, (C, C), 1)
        dg = _dot((ci >= ri).astype(F32), dgc, "nn", HI)
        db = jnp.broadcast_to(jnp.sum(dkb * k, axis=1, keepdims=True) + jnp.sum(dvb * v, axis=1, keepdims=True), (C, HD))
        return dq, dk, dvb * beta, dg, db, ds_new

    blk = lambda off: pl.BlockSpec((C, w), lambda h, n, off=off: (N - 1 - n, off + h))
    gspec = pl.BlockSpec((C, HD), lambda h, n: (N - 1 - n, 0))
    ospec = pl.BlockSpec((C, w), lambda h, n: (N - 1 - n, h))
    hspec = pl.BlockSpec((hb, C, HD), lambda h, n: (h, N - 1 - n, 0))
    return pl.pallas_call(
        body, name=name, grid=(NH // hb, N),
        in_specs=[blk(0), blk(NH // hb), blk(2 * NH // hb), gspec, gspec,
                  pl.BlockSpec((hb, 1, HD, HD), lambda h, n: (h, N - 1 - n, 0, 0)), ospec],
        out_specs=[pl.BlockSpec((C, 3 * w), lambda h, n: (N - 1 - n, 0)), hspec, hspec],
        out_shape=[jax.ShapeDtypeStruct((T, 3 * NH * HD), F32)] + [jax.ShapeDtypeStruct((NH, T, HD), F32)] * 2,
        scratch_shapes=[pltpu.VMEM((hb, HD, HD), F32)], compiler_params=_params(2))(qkv, qkv, qkv, g, beta, states, do)


_GATE_CB = GDN_QKV // (NH * HD)


def _gdn_gated_norm_fwd(name, o, proj, ng):
    def fn(o, gate, ng):
        outs = []
        for h in range(NH):
            sl = slice(h * HD, (h + 1) * HD)
            oh, gh = o[:, sl], gate[:, sl]
            outs.append(oh * _rms(oh) * ng * (gh * _sig(gh)))
        return jnp.concatenate(outs, axis=1)
    return _rowwise(name, fn, [o, (proj, NH * HD, _GATE_CB)], [ng], [(NH * HD, BF16)])[0]


def _gdn_gated_norm_bwd(name, don, o, proj, ng):
    def fn(don, o, gate, ng):
        dos, dgs = [], []
        dng = jnp.zeros((1, HD), F32)
        for h in range(NH):
            sl = slice(h * HD, (h + 1) * HD)
            oh, gh, dh = o[:, sl], gate[:, sl], don[:, sl]
            r = _rms(oh)
            xh = oh * r
            sg = _sig(gh)
            dn = dh * (gh * sg)
            dgs.append(dh * (xh * ng) * (sg * (1.0 + gh * (1.0 - sg))))
            dng = dng + jnp.sum(dn * xh, axis=0, keepdims=True)
            dxh = dn * ng
            dos.append(r * (dxh - xh * jnp.mean(dxh * xh, axis=-1, keepdims=True)))
        return jnp.concatenate(dos, axis=1), jnp.concatenate(dgs, axis=1), dng
    return _rowwise(name, fn, [don, o, (proj, NH * HD, _GATE_CB)], [ng], [(NH * HD, F32), (NH * HD, BF16)], sums=[HD])


def _rot(x):
    lane = lax.broadcasted_iota(jnp.int32, x.shape, 1)
    return jnp.where(lane < ROPE // 2, -pltpu.roll(x, HD - ROPE // 2, 1), pltpu.roll(x, ROPE // 2, 1))


def _rot_t(x):
    lane = lax.broadcasted_iota(jnp.int32, x.shape, 1)
    return jnp.where(lane < ROPE // 2, pltpu.roll(x, HD - ROPE // 2, 1), -pltpu.roll(x, ROPE // 2, 1))


def _rope_tables(pos_col):
    lane = jnp.arange(HD)
    inv_freq = ROPE_THETA ** (-(2.0 * (lane % (ROPE // 2)).astype(F32)) / ROPE)
    inv_freq = jnp.where(lane < ROPE, inv_freq, 0.0).astype(F32)[None, :]
    valid = (lane < ROPE).astype(F32)[None, :]

    def fn(pos, inv_freq, valid):
        ang = pos.astype(F32) * inv_freq
        return jnp.cos(ang) * valid, jnp.sin(ang) * valid
    return _rowwise("rope_tables", fn, [pos_col], [inv_freq, valid], [(HD, F32), (HD, F32)])


def _mla_pre_fwd(name, proj, cos, sin, qg, kvg):
    def fn(p, cos, sin, qg, kvg):
        cq, ckv, kr = p[:, :Q_RANK], p[:, Q_RANK:Q_RANK + KV_RANK], p[:, Q_RANK + KV_RANK:]
        return cq * _rms(cq) * qg, ckv * _rms(ckv) * kvg, kr * cos + _rot(kr) * sin
    return _rowwise(name, fn, [proj, cos, sin], [qg, kvg], [(Q_RANK, BF16), (KV_RANK, BF16), (HD, BF16)])


def _rms_bwd(dy, x, g):
    r = _rms(x)
    xh = x * r
    dxh = dy * g
    return r * (dxh - xh * jnp.mean(dxh * xh, axis=-1, keepdims=True)), jnp.sum(dy * xh, axis=0, keepdims=True)


def _mla_pre_bwd(name, proj, dcqn, dckvn, dkr, cos, sin, qg, kvg):
    def fn(p, dcqn, dckvn, dkr, cos, sin, qg, kvg):
        cq, ckv = p[:, :Q_RANK], p[:, Q_RANK:Q_RANK + KV_RANK]
        dcq, dqg = _rms_bwd(dcqn, cq, qg)
        dckv, dkvg = _rms_bwd(dckvn, ckv, kvg)
        dkr_pre = dkr * cos + _rot_t(dkr * sin)
        return jnp.concatenate([dcq, dckv, dkr_pre], axis=1), dqg, dkvg
    return _rowwise(name, fn, [proj, dcqn, dckvn, dkr, cos, sin], [qg, kvg], [(MLA_INK, BF16)], sums=[Q_RANK, KV_RANK])


def _mla_q_fwd(name, q, cos, sin):
    def fn(qn, qr, cos, sin):
        outs = []
        for h in range(NH):
            x = qr[:, h * HD:(h + 1) * HD]
            outs.append(x * cos + _rot(x) * sin)
        return qn, jnp.concatenate(outs, axis=1)
    return _rowwise(name, fn, [(q, NH * HD, 0), (q, NH * HD, 1), cos, sin], [], [(NH * HD, BF16), (NH * HD, BF16)])


def _mla_q_bwd(name, dqn, dqr, cos, sin):
    def fn(dqn, dqr, cos, sin):
        outs = [dqn]
        for h in range(NH):
            z = dqr[:, h * HD:(h + 1) * HD]
            outs.append(z * cos + _rot_t(z * sin))
        return jnp.concatenate(outs, axis=1)
    return _rowwise(name, fn, [dqn, dqr, cos, sin], [], [(2 * NH * HD, BF16)])[0]


def _att_probs(qn, qr, kn, kr, row0):
    s = (_dot(qn, kn, "nt") + _dot(qr, kr, "nt")) * ATT_SCALE
    qpos = row0 + lax.broadcasted_iota(jnp.int32, s.shape, 0)
    kpos = lax.broadcasted_iota(jnp.int32, s.shape, 1)
    s = jnp.where(kpos <= qpos, s, -1e30)
    p = jnp.exp(s - jnp.max(s, axis=1, keepdims=True))
    return p * (1.0 / jnp.sum(p, axis=1, keepdims=True))


def _mla_attn_fwd(name, qn, qr, kv, kr, tq=256):
    T = qn.shape[0]
    tq = min(tq, T)

    def body(qn_ref, qr_ref, kn_ref, v_ref, kr_ref, o_ref):
        i = pl.program_id(1)
        for blk in range(T // tq):
            @pl.when(i == blk)
            def _(blk=blk):
                keys = pl.ds(0, (blk + 1) * tq)
                p = _att_probs(qn_ref[...], qr_ref[...], kn_ref[keys, :], kr_ref[keys, :], blk * tq)
                o_ref[...] = _dot(p.astype(BF16), v_ref[keys, :], "nn").astype(o_ref.dtype)

    qspec = pl.BlockSpec((tq, HD), lambda h, i: (i, h))
    return pl.pallas_call(
        body, name=name, grid=(NH, T // tq),
        in_specs=[qspec, qspec, pl.BlockSpec((T, HD), lambda h, i: (0, h)), pl.BlockSpec((T, HD), lambda h, i: (0, NH + h)),
                  pl.BlockSpec((T, HD), lambda h, i: (0, 0))],
        out_specs=qspec, out_shape=jax.ShapeDtypeStruct((T, NH * HD), BF16), compiler_params=_params(2))(qn, qr, kv, kv, kr)


def _mla_attn_bwd(name, qn, qr, kv, kr, do, tq=256):
    T = qn.shape[0]
    tq = min(tq, T)

    def body(qn_ref, qr_ref, kn_ref, v_ref, kr_ref, do_ref, dqn_ref, dqr_ref, dkn_ref, dv_ref, dkr_ref):
        h, i = pl.program_id(0), pl.program_id(1)

        @pl.when(i == 0)
        def _():
            dkn_ref[...] = jnp.zeros_like(dkn_ref)
            dv_ref[...] = jnp.zeros_like(dv_ref)

        @pl.when((i == 0) & (h == 0))
        def _():
            dkr_ref[...] = jnp.zeros_like(dkr_ref)

        for blk in range(T // tq):
            @pl.when(i == blk)
            def _(blk=blk):
                keys = pl.ds(0, (blk + 1) * tq)
                qn, qr, do = qn_ref[...], qr_ref[...], do_ref[...]
                kn, kr, v = kn_ref[keys, :], kr_ref[keys, :], v_ref[keys, :]
                p = _att_probs(qn, qr, kn, kr, blk * tq)
                dp = _dot(do, v, "nt")
                ds = (p * (dp - jnp.sum(p * dp, axis=1, keepdims=True)) * ATT_SCALE).astype(BF16)
                dqn_ref[...] = _dot(ds, kn, "nn")
                dqr_ref[...] = _dot(ds, kr, "nn")
                dkn_ref[keys, :] += _dot(ds, qn, "tn")
                dkr_ref[keys, :] += _dot(ds, qr, "tn")
                dv_ref[keys, :] += _dot(p.astype(BF16), do, "tn")

    qspec = pl.BlockSpec((tq, HD), lambda h, i: (i, h))
    kspec = pl.BlockSpec((T, HD), lambda h, i: (0, h))
    return pl.pallas_call(
        body, name=name, grid=(NH, T // tq),
        in_specs=[qspec, qspec, kspec, pl.BlockSpec((T, HD), lambda h, i: (0, NH + h)),
                  pl.BlockSpec((T, HD), lambda h, i: (0, 0)), qspec],
        out_specs=[qspec, qspec, kspec, kspec, pl.BlockSpec((T, HD), lambda h, i: (0, 0))],
        out_shape=[jax.ShapeDtypeStruct((T, NH * HD), F32)] * 4 + [jax.ShapeDtypeStruct((T, HD), F32)],
        compiler_params=_params(2))(qn, qr, kv, kv, kr, do)


def _mod_rows(mod, layer):
    return [(mod, layer, i, D) for i in range(N_MOD)]


def _local_step(x, target, pos_col, mod, weights_of, P, on_grads):
    cos, sin = _rope_tables(pos_col)
    saved = []
    sh_m, sc_m = _mod_rows(mod, 0)[:2]
    h = _norm_mod_fwd("norm_mix0", x, (P["norm_mix_g"], 0, 0, D), sc_m, sh_m)
    for l in range(DEPTH):
        j = l // 2
        sh_m, sc_m, ga_m, sh_f, sc_f, ga_f = _mod_rows(mod, l)
        s = dict(x0=x)
        W = weights_of(l, h)
        s.update(h=h, W=W)
        if l % 2 == 0:
            proj = _mm(f"gdn_in{j}", h, W["gdn_in"], "nn", tn=GDN_INK // 2)
            qkv = _gdn_conv_fwd(f"gdn_conv{j}", proj, P["gdn_cw"][j])
            g, beta = _gdn_gates_fwd(f"gdn_gates{j}", proj, P["gdn_alog"][j], P["gdn_dtb"][j])
            o, states = _gdn_chunk_fwd(f"gdn_chunk{j}", qkv, g, beta)
            on = _gdn_gated_norm_fwd(f"gdn_gnorm{j}", o, proj, P["gdn_ng"][j])
            y = _mm(f"gdn_out{j}", on, W["gdn_out"], "nn")
            s.update(proj=proj, qkv=qkv, g=g, beta=beta, o=o, states=states, on=on)
        else:
            proj = _mm(f"mla_in{j}", h, W["mla_in"], "nn")
            cqn, ckvn, kr = _mla_pre_fwd(f"mla_pre{j}", proj, cos, sin, P["mla_qg"][j], P["mla_kvg"][j])
            q = _mm(f"mla_uq{j}", cqn, W["mla_uq"], "nn")
            kv = _mm(f"mla_ukv{j}", ckvn, W["mla_ukv"], "nn", out_dtype=BF16)
            qn, qr = _mla_q_fwd(f"mla_q{j}", q, cos, sin)
            o = _mla_attn_fwd(f"mla_attn{j}", qn, qr, kv, kr)
            y = _mm(f"mla_out{j}", o, W["mla_out"], "nn")
            s.update(proj=proj, cqn=cqn, ckvn=ckvn, kr=kr, kv=kv, qn=qn, qr=qr, o=o)
        s["y"] = y
        x, h2 = _residual_norm_fwd(f"res_mix{l}", x, y, ga_m, (P["norm_ffn_g"], l, 0, D), sc_f, sh_f)
        s["x1"] = x
        fa, fb, sw = _ffn_up(f"ffn_up{l}", h2, W["ffn_g"], W["ffn_u"], 0)
        yf = _ffn_down(f"ffn_down{l}", sw, W["ffn_d"], 0)
        if l + 1 < DEPTH:
            sh_n, sc_n = _mod_rows(mod, l + 1)[:2]
            x, h = _residual_norm_fwd(f"res_ffn{l}", x, yf, ga_f, (P["norm_mix_g"], l + 1, 0, D), sc_n, sh_n)
        else:
            x = _residual_fwd(f"res_ffn{l}", x, yf, ga_f)
        s.update(h2=h2, fa=fa, fb=fb, sw=sw, yf=yf)
        saved.append(s)

    dx, loss, d_final = _loss_head(x, target, P["final_g"])
    gP = dict(loss=loss, final_g=d_final, norm_mix_g=[None] * DEPTH, norm_ffn_g=[None] * DEPTH,
              gdn_cw=[None] * 2, gdn_alog=[None] * 2, gdn_dtb=[None] * 2, gdn_ng=[None] * 2,
              mla_qg=[None] * 2, mla_kvg=[None] * 2)
    dmod = [None] * DEPTH
    dyf, d_ga_f = _residual_bwd(f"res_ffn_b{DEPTH - 1}", dx, saved[-1]["yf"], _mod_rows(mod, DEPTH - 1)[5])
    for l in reversed(range(DEPTH)):
        j = l // 2
        s = saved[l]
        W = s["W"]
        sh_m, sc_m, ga_m, sh_f, sc_f, ga_f = _mod_rows(mod, l)
        da, db = _ffn_down_bwd(f"ffn_down_dx{l}", dyf, W["ffn_d"], s["fa"], s["fb"], 0)
        g_down = _ffn_down_dw(f"ffn_down_dw{l}", s["sw"], dyf)
        g_gate, g_up = _ffn_up_dw(f"ffn_up_dw{l}", s["h2"], da, db)
        on_grads(l, "ffn", dict(ffn_w_gate=g_gate, ffn_w_up=g_up, ffn_w_down=g_down))
        dh2 = _ffn_up_dx(f"ffn_up_dx{l}", da, db, W["ffn_g"], W["ffn_u"], 0)
        dx, dy, d_sh_f, d_sc_f, gP["norm_ffn_g"][l], d_ga_m = _norm_residual_bwd(
            f"norm_ffn_b{l}", dh2, s["x1"], dx, (P["norm_ffn_g"], l, 0, D), sc_f, s["y"], ga_m)
        if l % 2 == 0:
            don = _mm(f"gdn_out_dx{j}", dy, W["gdn_out"], "nt")
            g_out = _mm(f"gdn_out_dw{j}", s["on"], dy, "tn", out_dtype=BF16)
            do, dgate, gP["gdn_ng"][j] = _gdn_gated_norm_bwd(f"gdn_gnorm_b{j}", don, s["o"], s["proj"], P["gdn_ng"][j])
            dqkv, dg_h, db_h = _gdn_chunk_bwd(f"gdn_chunk_b{j}", s["qkv"], s["g"], s["beta"], s["states"], do)
            dab_, gP["gdn_alog"][j], gP["gdn_dtb"][j] = _gdn_gates_bwd(f"gdn_gates_b{j}", s["proj"], dg_h, db_h,
                                                                        P["gdn_alog"][j], P["gdn_dtb"][j])
            dpre, gP["gdn_cw"][j] = _gdn_conv_bwd(f"gdn_conv_b{j}", s["proj"], P["gdn_cw"][j], dqkv)
            dproj = jnp.concatenate([dpre, dgate, dab_], axis=1)
            g_in = _mm(f"gdn_in_dw{j}", s["h"], dproj, "tn", out_dtype=BF16, tn=GDN_INK // 2)
            on_grads(l, "mix", dict(gdn_w_in=_uncols(_gdn_in_from_kernel(g_in)), gdn_w_out=_unrows(g_out)))
            dh = _mm(f"gdn_in_dx{j}", dproj, W["gdn_in"], "nt")
        else:
            do = _mm(f"mla_out_dx{j}", dy, W["mla_out"], "nt", out_dtype=BF16)
            g_out = _mm(f"mla_out_dw{j}", s["o"], dy, "tn", out_dtype=BF16)
            dqn, dqr, dkn, dv, dkr = _mla_attn_bwd(f"mla_attn_b{j}", s["qn"], s["qr"], s["kv"], s["kr"], do)
            dq = _mla_q_bwd(f"mla_q_b{j}", dqn, dqr, cos, sin)
            dkv = jnp.concatenate([dkn, dv], axis=1)
            g_uq = _mm(f"mla_uq_dw{j}", s["cqn"], dq, "tn", out_dtype=BF16)
            dcqn = _mm(f"mla_uq_dx{j}", dq, W["mla_uq"], "nt")
            g_ukv = _mm(f"mla_ukv_dw{j}", s["ckvn"], dkv, "tn", out_dtype=BF16)
            dckvn = _mm(f"mla_ukv_dx{j}", dkv, W["mla_ukv"], "nt")
            dproj, gP["mla_qg"][j], gP["mla_kvg"][j] = _mla_pre_bwd(f"mla_pre_b{j}", s["proj"], dcqn, dckvn, dkr, cos, sin,
                                                                     P["mla_qg"][j], P["mla_kvg"][j])
            g_in = _mm(f"mla_in_dw{j}", s["h"], dproj, "tn", out_dtype=BF16)
            on_grads(l, "mix", dict(mla_w_in=_unrows(g_in[:, :Q_RANK + KV_RANK + ROPE]), mla_w_uq=_uncols(_mla_uq_from_kernel(g_uq)),
                                    mla_w_ukv=_uncols(_mla_ukv_from_kernel(g_ukv)), mla_w_out=_unrows(g_out)))
            dh = _mm(f"mla_in_dx{j}", dproj, W["mla_in"], "nt")
        if l > 0:
            dx, dyf_prev, d_sh_m, d_sc_m, gP["norm_mix_g"][l], d_ga_f_prev = _norm_residual_bwd(
                f"norm_mix_b{l}", dh, s["x0"], dx, (P["norm_mix_g"], l, 0, D), sc_m, saved[l - 1]["yf"], _mod_rows(mod, l - 1)[5])
        else:
            dx, d_sh_m, d_sc_m, gP["norm_mix_g"][l] = _norm_mod_bwd(f"norm_mix_b{l}", dh, s["x0"], dx,
                                                                     (P["norm_mix_g"], l, 0, D), sc_m)
        dmod[l] = jnp.concatenate([d_sh_m, d_sc_m, d_ga_m, d_sh_f, d_sc_f, d_ga_f], axis=1)
        if l > 0:
            dyf, d_ga_f = dyf_prev, d_ga_f_prev
    return dx, jnp.concatenate(dmod, axis=0), gP


def _pad_cols(a, width):
    return jnp.pad(a, ((0, 0), (0, width - a.shape[1])))


def _gdn_in_to_kernel(w):
    m = GDN_QKV + NH * HD
    return jnp.concatenate([w[:, :m], _pad_cols(w[:, m:m + NH], HD), _pad_cols(w[:, m + NH:], HD)], axis=1)


def _gdn_in_from_kernel(g):
    m = GDN_QKV + NH * HD
    return jnp.concatenate([g[:, :m], g[:, m:m + NH], g[:, m + HD:m + HD + NH]], axis=1)


def _mla_uq_to_kernel(w):
    w3 = w.reshape(Q_RANK, NH, HD + ROPE)
    rope = jnp.pad(w3[:, :, HD:], ((0, 0), (0, 0), (0, HD - ROPE)))
    return jnp.concatenate([w3[:, :, :HD].reshape(Q_RANK, NH * HD), rope.reshape(Q_RANK, NH * HD)], axis=1)


def _mla_uq_from_kernel(g):
    gn = g[:, :NH * HD].reshape(Q_RANK, NH, HD)
    gr = g[:, NH * HD:].reshape(Q_RANK, NH, HD)[:, :, :ROPE]
    return jnp.concatenate([gn, gr], axis=2).reshape(Q_RANK, NH * (HD + ROPE))


def _mla_ukv_to_kernel(w):
    w3 = w.reshape(KV_RANK, NH, 2 * HD)
    return jnp.concatenate([w3[:, :, :HD].reshape(KV_RANK, NH * HD), w3[:, :, HD:].reshape(KV_RANK, NH * HD)], axis=1)


def _mla_ukv_from_kernel(g):
    gk = g[:, :NH * HD].reshape(KV_RANK, NH, HD)
    gv = g[:, NH * HD:].reshape(KV_RANK, NH, HD)
    return jnp.concatenate([gk, gv], axis=2).reshape(KV_RANK, NH * 2 * HD)


def _cols(t):
    return jnp.moveaxis(t, 0, 1).reshape(t.shape[1], -1)


def _uncols(g):
    return jnp.moveaxis(g.reshape(g.shape[0], 4, -1), 1, 0)


def _rows(t):
    return t.reshape(-1, t.shape[2])


def _unrows(g):
    return g.reshape(4, -1, g.shape[1])


def _layer_weights(layer):
    mixer = ("gdn_w_in", "gdn_w_out") if layer % 2 == 0 else ("mla_w_in", "mla_w_uq", "mla_w_ukv", "mla_w_out")
    return [(n, layer // 2) for n in mixer] + [(n, layer) for n in ("ffn_w_gate", "ffn_w_up", "ffn_w_down")]


def _weights_to_kernel(layer, g):
    out = dict(ffn_g=g["ffn_w_gate"], ffn_u=g["ffn_w_up"], ffn_d=g["ffn_w_down"])
    if layer % 2 == 0:
        out.update(gdn_in=_gdn_in_to_kernel(_cols(g["gdn_w_in"])), gdn_out=_rows(g["gdn_w_out"]))
    else:
        out.update(mla_in=_pad_cols(_rows(g["mla_w_in"]), MLA_INK), mla_uq=_mla_uq_to_kernel(_cols(g["mla_w_uq"])),
                   mla_ukv=_mla_ukv_to_kernel(_cols(g["mla_w_ukv"])), mla_out=_rows(g["mla_w_out"]))
    return out


def _small_to_kernel(norm_mix_g, norm_ffn_g, final_norm_g, gdn_conv_w, gdn_a_log, gdn_dt_bias, gdn_norm_g, q_norm_g, kv_norm_g):
    return dict(
        norm_mix_g=norm_mix_g, norm_ffn_g=norm_ffn_g, final_g=final_norm_g.reshape(1, D),
        gdn_cw=[jnp.transpose(gdn_conv_w[j]) for j in range(2)],
        gdn_alog=[_pad_cols(gdn_a_log[j:j + 1], HD) for j in range(2)],
        gdn_dtb=[_pad_cols(gdn_dt_bias[j:j + 1], HD) for j in range(2)],
        gdn_ng=[gdn_norm_g[j:j + 1] for j in range(2)],
        mla_qg=[q_norm_g[j:j + 1] for j in range(2)],
        mla_kvg=[kv_norm_g[j:j + 1] for j in range(2)],
    )


_CHIP_FLIPS = ((1, 0), (0, 1), (1, 1))
_ANY = pl.BlockSpec(memory_space=pl.ANY)


def _me():
    return lax.axis_index("x"), lax.axis_index("y"), lax.axis_index("c")


def _chip_peer(dx, dy):
    x, y, c = _me()
    return ((1 - x) if dx else x, (1 - y) if dy else y, c)


def _rcopy(src, dst, send_sem, recv_sem, to):
    return pltpu.make_async_remote_copy(src_ref=src, dst_ref=dst, send_sem=send_sem, recv_sem=recv_sem,
                                        device_id=to, device_id_type=MESH)


def _allgather4(name, a, halves=False):
    R, C = a.shape
    rh = R // 2 if halves else R

    def body(a_ref, out_ref, send_sems, recv_sems, local_sem):
        x, y, c = _me()
        me = 2 * x + y
        src = a_ref.at[pl.ds(c * rh, rh)] if halves else a_ref
        local = pltpu.make_async_copy(src, out_ref.at[me], local_sem)
        local.start()
        sends = []
        for k, (dx, dy) in enumerate(_CHIP_FLIPS):
            cp = _rcopy(src, out_ref.at[me], send_sems.at[k], recv_sems.at[k], _chip_peer(dx, dy))
            cp.start()
            sends.append(cp)
        for k, (dx, dy) in enumerate(_CHIP_FLIPS):
            px, py, _ = _chip_peer(dx, dy)
            _rcopy(src, out_ref.at[2 * px + py], send_sems.at[k], recv_sems.at[k], _chip_peer(dx, dy)).wait_recv()
        for cp in sends:
            cp.wait_send()
        local.wait()

    return pl.pallas_call(
        body, name=name, in_specs=[_ANY], out_specs=_ANY, out_shape=jax.ShapeDtypeStruct((4, rh, C), a.dtype),
        scratch_shapes=[pltpu.SemaphoreType.DMA((3,)), pltpu.SemaphoreType.DMA((3,)), pltpu.SemaphoreType.DMA(())])(a)


_NCH = 4


def _dma_sems(*counts):
    return [pltpu.SemaphoreType.DMA((n,)) for n in counts]


def _slot_tile(rows, cap=512):
    best = rows
    for tr in range(16, min(rows, cap) + 1, 16):
        if rows % tr == 0:
            best = tr
    return best


def _cast_into_slot(name, a, chip, row0, rows):
    C = a.shape[1]
    tr = _slot_tile(rows)
    assert row0 % tr == 0
    first = row0 // tr

    def body(c_ref, a_ref, o_ref):
        o_ref[0] = a_ref[...].astype(o_ref.dtype)

    grid_spec = pltpu.PrefetchScalarGridSpec(
        num_scalar_prefetch=1, grid=(rows // tr,), in_specs=[pl.BlockSpec((tr, C), lambda i, c_ref: (first + i, 0))],
        out_specs=pl.BlockSpec((1, tr, C), lambda i, c_ref: (c_ref[0], i, 0)))
    return pl.pallas_call(body, name=name, grid_spec=grid_spec, out_shape=jax.ShapeDtypeStruct((4, rows, C), BF16),
                          compiler_params=_params(1))(chip, a)


def _chunks(rows, align):
    for nch in (_NCH, 2):
        if rows % (nch * align) == 0:
            return nch
    return 1


def _gather_exchange(out, ici_s, ici_r, d2d_s, d2d_r):
    n = len(out)
    x, y, c = _me()
    me = 2 * x + y
    sib = (x, y, 1 - c)
    peers = [_chip_peer(dx, dy) for dx, dy in _CHIP_FLIPS]
    for t in range(n):
        h = out[t].shape[1] // 2
        nch = _chunks(h, 16)
        ch = h // nch
        for k, peer in enumerate(peers):
            for i in range(nch):
                blk = out[t].at[me, pl.ds(c * h + i * ch, ch)]
                _rcopy(blk, blk, ici_s.at[3 * t + k], ici_r.at[3 * t + k], peer).start()
    for t in range(n):
        h = out[t].shape[1] // 2
        nch = _chunks(h, 16)
        ch = h // nch
        for k, peer in enumerate(peers):
            pchip = 2 * peer[0] + peer[1]
            got = out[t].at[pchip, pl.ds(c * h, h)]
            _rcopy(got, got, ici_s.at[3 * t + k], ici_r.at[3 * t + k], peer).wait_recv()
            for i in range(nch):
                blk = out[t].at[pchip, pl.ds(c * h + i * ch, ch)]
                _rcopy(blk, blk, d2d_s.at[3 * t + k], d2d_r.at[3 * t + k], sib).start()
    for t in range(n):
        h = out[t].shape[1] // 2
        for k, peer in enumerate(peers):
            pchip = 2 * peer[0] + peer[1]
            other = out[t].at[pchip, pl.ds((1 - c) * h, h)]
            _rcopy(other, other, d2d_s.at[3 * t + k], d2d_r.at[3 * t + k], sib).wait_recv()
            _rcopy(other, other, ici_s.at[3 * t + k], ici_r.at[3 * t + k], peer).wait_send()
            _rcopy(other, other, d2d_s.at[3 * t + k], d2d_r.at[3 * t + k], sib).wait_send()


def _gather_weights(name, bufs):
    n = len(bufs)

    def body(*refs):
        _gather_exchange(refs[n:2 * n], *refs[2 * n:])

    return pl.pallas_call(
        body, name=name, in_specs=[_ANY] * n, out_specs=[_ANY] * n,
        out_shape=[jax.ShapeDtypeStruct(s.shape, s.dtype) for s in bufs],
        input_output_aliases={t: t for t in range(n)},
        scratch_shapes=_dma_sems(3 * n, 3 * n, 3 * n, 3 * n))(*bufs)


def _gather_weights_async(name, collective_id, bufs):
    n = len(bufs)
    refs = [jax.new_ref(b, memory_space=pltpu.MemorySpace.HBM) for b in bufs]

    @pl.kernel(mesh=plsc.ScalarSubcoreMesh(axis_name="sequencer", num_cores=1), name=name,
               scratch_types=tuple(_dma_sems(3 * n, 3 * n, 3 * n, 3 * n)),
               compiler_params=pltpu.CompilerParams(collective_id=collective_id))
    def launch(ici_s, ici_r, d2d_s, d2d_r):
        x, y, c = _me()
        barrier = pltpu.get_barrier_semaphore()
        for peer in [_chip_peer(dx, dy) for dx, dy in _CHIP_FLIPS] + [(x, y, 1 - c)]:
            pl.semaphore_signal(barrier, inc=1, device_id=peer, device_id_type=MESH)
        pl.semaphore_wait(barrier, 4)
        _gather_exchange(refs, ici_s, ici_r, d2d_s, d2d_r)

    launch()
    return [r[...] for r in refs]


def _rs_split(name, grads):
    n = len(grads)

    def body(*refs):
        g, out = refs[:n], refs[n:2 * n]
        send, recv = refs[2 * n:]
        x, y, c = _me()
        sib = (x, y, 1 - c)
        for t in range(n):
            h = g[t].shape[1] // 2
            for d in range(4):
                _rcopy(g[t].at[d, pl.ds((1 - c) * h, h)], out[t].at[d], send.at[t], recv.at[t], sib).start()
        for t in range(n):
            _rcopy(out[t], out[t], send.at[t], recv.at[t], sib).wait()

    return pl.pallas_call(
        body, name=name, in_specs=[_ANY] * n, out_specs=[_ANY] * n,
        out_shape=[jax.ShapeDtypeStruct((4, s.shape[1] // 2, s.shape[2]), s.dtype) for s in grads],
        scratch_shapes=_dma_sems(n, n))(*grads)


def _pair_add(name, g, theirs, core_chip):
    _, R, C = g.shape
    h = R // 2
    tr = _slot_tile(h)
    nb = h // tr

    def body(s_ref, g_ref, t_ref, p_ref, o_ref):
        val = (g_ref[...].astype(F32) + t_ref[...].astype(F32)).astype(p_ref.dtype)
        p_ref[...] = val

        @pl.when(pl.program_id(1) == s_ref[1])
        def _():
            o_ref[...] = val

    spec = pl.BlockSpec((1, tr, C), lambda i, d, s_ref: (d, i, 0))
    grid_spec = pltpu.PrefetchScalarGridSpec(
        num_scalar_prefetch=1, grid=(nb, 4),
        in_specs=[pl.BlockSpec((1, tr, C), lambda i, d, s_ref: (d, s_ref[0] * nb + i, 0)), spec],
        out_specs=[spec, pl.BlockSpec((1, tr, C), lambda i, d, s_ref: (s_ref[1], i, 0))])
    half = jax.ShapeDtypeStruct((4, h, C), BF16)
    return pl.pallas_call(body, name=name, grid_spec=grid_spec, out_shape=[half, half],
                          compiler_params=_params(2))(core_chip, g, theirs)


def _rs_alltoall_async(name, collective_id, parts, bufs):
    n = len(parts)
    p = [jax.new_ref(a, memory_space=pltpu.MemorySpace.HBM) for a in parts]
    out = [jax.new_ref(b, memory_space=pltpu.MemorySpace.HBM) for b in bufs]

    @pl.kernel(mesh=plsc.ScalarSubcoreMesh(axis_name="sequencer", num_cores=1), name=name,
               scratch_types=tuple(_dma_sems(3 * n, 3 * n)),
               compiler_params=pltpu.CompilerParams(collective_id=collective_id))
    def launch(send, recv):
        barrier = pltpu.get_barrier_semaphore()
        for peer in [_chip_peer(dx, dy) for dx, dy in _CHIP_FLIPS]:
            pl.semaphore_signal(barrier, inc=1, device_id=peer, device_id_type=MESH)
        pl.semaphore_wait(barrier, 3)
        _alltoall_exchange(p, out, send, recv)

    launch()
    return [r[...] for r in out]


def _alltoall_exchange(p, out, send, recv):
    x, y, c = _me()
    me = 2 * x + y
    peers = [_chip_peer(dx, dy) for dx, dy in _CHIP_FLIPS]
    for t in range(len(p)):
        h = p[t].shape[1]
        nch = _chunks(h, 16)
        ch = h // nch
        for k, peer in enumerate(peers):
            pchip = 2 * peer[0] + peer[1]
            for i in range(nch):
                rows = pl.ds(i * ch, ch)
                _rcopy(p[t].at[pchip, rows], out[t].at[me, rows], send.at[3 * t + k], recv.at[3 * t + k], peer).start()
    for t in range(len(p)):
        for k, peer in enumerate(peers):
            pchip = 2 * peer[0] + peer[1]
            _rcopy(out[t].at[pchip], out[t].at[pchip], send.at[3 * t + k], recv.at[3 * t + k], peer).wait()


def _rs_swap(name, halves):
    n = len(halves)

    def body(*refs):
        a, out = refs[:n], refs[n:2 * n]
        send, recv = refs[2 * n:]
        x, y, c = _me()
        sib = (x, y, 1 - c)
        for t in range(n):
            ch = a[t].shape[0] // _NCH
            for i in range(_NCH):
                rows = pl.ds(i * ch, ch)
                _rcopy(a[t].at[rows], out[t].at[rows], send.at[t], recv.at[t], sib).start()
        for t in range(n):
            _rcopy(a[t], out[t], send.at[t], recv.at[t], sib).wait()

    return pl.pallas_call(
        body, name=name, in_specs=[_ANY] * n, out_specs=[_ANY] * n,
        out_shape=[jax.ShapeDtypeStruct(s.shape, s.dtype) for s in halves],
        scratch_shapes=_dma_sems(n, n))(*halves)


def _sibling_merge(name, a):
    P_, rh, C = a.shape

    def body(a_ref, out_ref, send_sem, recv_sem, local_sem):
        x, y, c = _me()
        local = pltpu.make_async_copy(a_ref, out_ref.at[:, pl.ds(c * rh, rh)], local_sem)
        local.start()
        cp = _rcopy(a_ref, out_ref.at[:, pl.ds(c * rh, rh)], send_sem, recv_sem, (x, y, 1 - c))
        cp.start()
        cp.wait_send()
        _rcopy(a_ref, out_ref.at[:, pl.ds((1 - c) * rh, rh)], send_sem, recv_sem, (x, y, 1 - c)).wait_recv()
        local.wait()

    return pl.pallas_call(
        body, name=name, in_specs=[_ANY], out_specs=_ANY, out_shape=jax.ShapeDtypeStruct((P_, 2 * rh, C), a.dtype),
        scratch_shapes=[pltpu.SemaphoreType.DMA(()), pltpu.SemaphoreType.DMA(()), pltpu.SemaphoreType.DMA(())])(a)


def _allgather8(name, a):
    g4 = _allgather4(name + "_chips", a)
    both = _sibling_merge(name + "_cores", g4.reshape(1, 4 * a.shape[0], a.shape[1]))
    return jnp.transpose(both.reshape(2, 4, *a.shape), (1, 0, 2, 3)).reshape(8, *a.shape)


def _sum_slots(name, a, out_dtype):
    def fn(a):
        acc = a[0].astype(F32)
        for k in range(1, a.shape[0]):
            acc = acc + a[k].astype(F32)
        return acc
    return _rowwise(name, fn, [a], [], [(a.shape[2], out_dtype)])[0]


def _adamw_math(w, g, m, v):
    m = ADAM_B1 * m + (1.0 - ADAM_B1) * g
    v = ADAM_B2 * v + (1.0 - ADAM_B2) * (g * g)
    m_hat = m / (1.0 - ADAM_B1 ** ADAM_STEP)
    v_hat = v / (1.0 - ADAM_B2 ** ADAM_STEP)
    return -ADAM_LR * (m_hat / (jnp.sqrt(v_hat) + ADAM_EPS) + ADAM_WD * w), m, v


def _adamw_piece(name, w2, m2, v2, mine, theirs, row0, prev, core):
    R, C = w2.shape
    h = mine.shape[0]
    tr = _slot_tile(h, 256)
    nb = h // tr
    assert row0 % tr == 0
    first = row0 // tr

    def body(c_ref, w_ref, m_ref, v_ref, a_ref, b_ref, *rest):
        g_ref, d_ref, nm_ref, nv_ref = rest[-4:]
        g = jnp.where(pl.program_id(0) == c_ref[0], a_ref[...], b_ref[...])
        g_ref[...] = g
        d_ref[...], nm_ref[...], nv_ref[...] = _adamw_math(w_ref[...], g, m_ref[...], v_ref[...])

    full = pl.BlockSpec((tr, C), lambda s, i, c_ref: (first + s * nb + i, 0))
    mine_spec = pl.BlockSpec((tr, C), lambda s, i, c_ref: (jnp.where(s == c_ref[0], i, 0), 0))
    theirs_spec = pl.BlockSpec((tr, C), lambda s, i, c_ref: (jnp.where(s == c_ref[0], 0, i), 0))
    extra = [] if prev is None else list(prev)
    grid_spec = pltpu.PrefetchScalarGridSpec(
        num_scalar_prefetch=1, grid=(2, nb), in_specs=[full, full, full, mine_spec, theirs_spec] + [_ANY] * len(extra),
        out_specs=[full] * 4)
    return pl.pallas_call(
        body, name=name, grid_spec=grid_spec, out_shape=[jax.ShapeDtypeStruct((R, C), F32)] * 4,
        input_output_aliases={6 + k: k for k in range(len(extra))}, compiler_params=_params(2))(core, w2, m2, v2, mine, theirs, *extra)


def _adamw(name, w, g, m, v):
    shape = w.shape
    two_d = (-1, shape[-1]) if w.ndim > 1 else (1, -1)
    w2, g2, m2, v2 = [t.reshape(two_d) for t in (w, g, m, v)]
    rows = w2.shape[0]
    tr = rows
    for cand in (256, 128, 64, 32, 16, 8):
        if rows % cand == 0:
            tr = cand
            break

    c = w2.shape[1]
    outs = _rowwise(name, _adamw_math, [w2, g2, m2, v2], [], [(c, F32)] * 3, tr=tr)
    return [o.reshape(shape) for o in outs]


_WEIGHT_ORDER = ("ada_w", "ada_b", "norm_mix_g", "norm_ffn_g", "gdn_w_in", "gdn_conv_w", "gdn_a_log", "gdn_dt_bias",
                 "gdn_norm_g", "gdn_w_out", "mla_w_in", "mla_q_norm_g", "mla_kv_norm_g", "mla_w_uq", "mla_w_ukv",
                 "mla_w_out", "ffn_w_gate", "ffn_w_up", "ffn_w_down", "final_norm_g")
_BIG = (("gdn_w_in", 2), ("gdn_w_out", 1), ("mla_w_in", 1), ("mla_w_uq", 2), ("mla_w_ukv", 2), ("mla_w_out", 1),
        ("ffn_w_gate", 2), ("ffn_w_up", 2), ("ffn_w_down", 1))
_SMALL_SHARDED = (("gdn_conv_w", 1), ("mla_q_norm_g", 1), ("mla_kv_norm_g", 1))
_STORED_TRANSPOSED = ("ffn_w_gate", "ffn_w_up")


def _size(shape):
    n = 1
    for s in shape:
        n *= s
    return n


def _pack_rows_each(tensors):
    parts, offs, off = [], [], 0
    for t in tensors:
        flat = t.reshape(-1).astype(F32)
        rows = -(-flat.shape[0] // PACK_W)
        parts.append(jnp.pad(flat, (0, rows * PACK_W - flat.shape[0])).reshape(rows, PACK_W))
        offs.append(off)
        off += rows
    total = -(-off // 16) * 16
    pack = jnp.pad(parts[0], ((offs[0], total - offs[0] - parts[0].shape[0]), (0, 0)))
    for p, o in zip(parts[1:], offs[1:]):
        pack = pack + jnp.pad(p, ((o, total - o - p.shape[0]), (0, 0)))
    return pack, offs


def _unpack_rows_each(pack, shapes):
    lead = pack.shape[:-2]
    out, off = [], 0
    for shp in shapes:
        n = _size(shp)
        rows = -(-n // PACK_W)
        out.append(pack[..., off:off + rows, :].reshape(*lead, -1)[..., :n].reshape(*lead, *shp))
        off += rows
    return out


def _merge_chips(stacked, axis):
    moved = jnp.moveaxis(stacked, 0, axis)
    shp = list(moved.shape)
    return moved.reshape(shp[:axis] + [shp[axis] * shp[axis + 1]] + shp[axis + 2:])


def _my_shard(full, axis, chip):
    n = full.shape[axis] // 4
    return lax.dynamic_slice_in_dim(full, chip * n, n, axis)


def kernel(x, c, positions, ada_w, ada_b, norm_mix_g, norm_ffn_g, gdn_w_in, gdn_conv_w, gdn_a_log, gdn_dt_bias, gdn_norm_g, gdn_w_out, mla_w_in, mla_q_norm_g, mla_kv_norm_g, mla_w_uq, mla_w_ukv, mla_w_out, ffn_w_gate, ffn_w_up, ffn_w_down, final_norm_g, loss_target, m_ada_w, m_ada_b, m_norm_mix_g, m_norm_ffn_g, m_gdn_w_in, m_gdn_conv_w, m_gdn_a_log, m_gdn_dt_bias, m_gdn_norm_g, m_gdn_w_out, m_mla_w_in, m_mla_q_norm_g, m_mla_kv_norm_g, m_mla_w_uq, m_mla_w_ukv, m_mla_w_out, m_ffn_w_gate, m_ffn_w_up, m_ffn_w_down, m_final_norm_g, v_ada_w, v_ada_b, v_norm_mix_g, v_norm_ffn_g, v_gdn_w_in, v_gdn_conv_w, v_gdn_a_log, v_gdn_dt_bias, v_gdn_norm_g, v_gdn_w_out, v_mla_w_in, v_mla_q_norm_g, v_mla_kv_norm_g, v_mla_w_uq, v_mla_w_ukv, v_mla_w_out, v_ffn_w_gate, v_ffn_w_up, v_ffn_w_down, v_final_norm_g):
    w = dict(ada_w=ada_w, ada_b=ada_b, norm_mix_g=norm_mix_g, norm_ffn_g=norm_ffn_g, gdn_w_in=gdn_w_in, gdn_conv_w=gdn_conv_w,
             gdn_a_log=gdn_a_log, gdn_dt_bias=gdn_dt_bias, gdn_norm_g=gdn_norm_g, gdn_w_out=gdn_w_out, mla_w_in=mla_w_in,
             mla_q_norm_g=mla_q_norm_g, mla_kv_norm_g=mla_kv_norm_g, mla_w_uq=mla_w_uq, mla_w_ukv=mla_w_ukv,
             mla_w_out=mla_w_out, ffn_w_gate=ffn_w_gate, ffn_w_up=ffn_w_up, ffn_w_down=ffn_w_down, final_norm_g=final_norm_g)
    m = dict(ada_w=m_ada_w, ada_b=m_ada_b, norm_mix_g=m_norm_mix_g, norm_ffn_g=m_norm_ffn_g, gdn_w_in=m_gdn_w_in,
             gdn_conv_w=m_gdn_conv_w, gdn_a_log=m_gdn_a_log, gdn_dt_bias=m_gdn_dt_bias, gdn_norm_g=m_gdn_norm_g,
             gdn_w_out=m_gdn_w_out, mla_w_in=m_mla_w_in, mla_q_norm_g=m_mla_q_norm_g, mla_kv_norm_g=m_mla_kv_norm_g,
             mla_w_uq=m_mla_w_uq, mla_w_ukv=m_mla_w_ukv, mla_w_out=m_mla_w_out, ffn_w_gate=m_ffn_w_gate,
             ffn_w_up=m_ffn_w_up, ffn_w_down=m_ffn_w_down, final_norm_g=m_final_norm_g)
    v = dict(ada_w=v_ada_w, ada_b=v_ada_b, norm_mix_g=v_norm_mix_g, norm_ffn_g=v_norm_ffn_g, gdn_w_in=v_gdn_w_in,
             gdn_conv_w=v_gdn_conv_w, gdn_a_log=v_gdn_a_log, gdn_dt_bias=v_gdn_dt_bias, gdn_norm_g=v_gdn_norm_g,
             gdn_w_out=v_gdn_w_out, mla_w_in=v_mla_w_in, mla_q_norm_g=v_mla_q_norm_g, mla_kv_norm_g=v_mla_kv_norm_g,
             mla_w_uq=v_mla_w_uq, mla_w_ukv=v_mla_w_ukv, mla_w_out=v_mla_w_out, ffn_w_gate=v_ffn_w_gate,
             ffn_w_up=v_ffn_w_up, ffn_w_down=v_ffn_w_down, final_norm_g=v_final_norm_g)
    T = x.shape[1]
    ix, iy, ic = _me()
    chip = 2 * ix + iy
    seq = 2 * chip + ic
    n_dev = 8

    small_shapes = [w[n].shape for n, _ in _SMALL_SHARDED] + [c.shape]
    pack0, _ = _pack_rows_each([w[n] for n, _ in _SMALL_SHARDED] + [c])
    got0 = _unpack_rows_each(_allgather8("gather_small", pack0), small_shapes)
    small_full = {n: _merge_chips(g[0::2], ax) for (n, ax), g in zip(_SMALL_SHARDED, got0)}
    c_all = got0[-1].reshape(n_dev, D)

    big = [n for n, _ in _BIG]
    chip_arr = chip.astype(jnp.int32).reshape(1)

    def stored(n, t):
        return jnp.swapaxes(t, 1, 2) if n in _STORED_TRANSPOSED else t

    ws, ms, vs = [{n: stored(n, d[n]) for n in big} for d in (w, m, v)]
    two_d = lambda t: t.reshape(-1, t.shape[-1])

    gathered = []
    for l in range(DEPTH):
        names = _layer_weights(l)
        bufs = [_cast_into_slot(f"to_bf16_{n}{l}", two_d(ws[n]), chip_arr, j * ws[n].shape[1], ws[n].shape[1]) for n, j in names]
        filled = _gather_weights("gather_weights0", bufs) if l == 0 else _gather_weights_async(f"gather_weights{l}", l, bufs)
        gathered.append({n: b for (n, _), b in zip(names, filled)})

    def weights_of(l, h):
        return _weights_to_kernel(l, gathered[l])

    P = _small_to_kernel(norm_mix_g, norm_ffn_g, final_norm_g, small_full["gdn_conv_w"], gdn_a_log, gdn_dt_bias,
                         gdn_norm_g, small_full["mla_q_norm_g"], small_full["mla_kv_norm_g"])

    c16 = jnp.pad(c_all, ((0, 16 - n_dev), (0, 0)))
    ca = _rowwise("cond_silu", lambda t: t * _sig(t), [c16], [], [(D, BF16)])[0]
    n_ada = ada_w.shape[2]
    mods = jnp.concatenate([_mm(f"ada_fwd{l}", ca, ada_w[l], "nn") for l in range(DEPTH)], axis=0)
    mods_all = _allgather4("gather_mod", mods).reshape(4, DEPTH, 16, n_ada)
    mod_mm = jnp.transpose(lax.dynamic_index_in_dim(mods_all, seq, axis=2, keepdims=False), (1, 0, 2)).reshape(DEPTH, 4 * n_ada)
    mod = _rowwise("mod_bias", lambda a, b: a + b, [mod_mm, ada_b], [], [(4 * n_ada, F32)])[0]

    core_chip = jnp.stack([ic, chip]).astype(jnp.int32)
    pending, in_flight = {}, []

    def reduce_group(layer, part, pieces):
        pending.update({(n, layer if n.startswith("ffn_") else layer // 2): g for n, g in pieces.items()})
        if part == "ffn" and layer > 0:
            return
        keys = list(pending)
        glist = [pending.pop(k) for k in keys]
        tag = f"{layer}{part}"
        theirs = _rs_split("grads_cores_" + tag, glist)
        both = [_pair_add(f"grads_pair_{n}{l}", g, t, core_chip) for (n, l), g, t in zip(keys, glist, theirs)]
        swapped = _rs_alltoall_async("grads_chips_" + tag, DEPTH + 1 + len(in_flight), [p for p, _ in both], [o for _, o in both])
        in_flight.append((tag, keys, swapped))

    dx, dmod, gP = _local_step(x.reshape(T, D), loss_target.reshape(T, D), positions.reshape(T, 1), mod, weights_of, P, reduce_group)

    partials = [dmod, jnp.concatenate(gP["norm_mix_g"]), jnp.concatenate(gP["norm_ffn_g"]), gP["final_g"],
                jnp.stack([jnp.transpose(g) for g in gP["gdn_cw"]]), jnp.concatenate(gP["gdn_alog"])[:, :NH],
                jnp.concatenate(gP["gdn_dtb"])[:, :NH], jnp.concatenate(gP["gdn_ng"]), jnp.concatenate(gP["mla_qg"]),
                jnp.concatenate(gP["mla_kvg"]), gP["loss"][:, :1]]
    part_shapes = [p.shape for p in partials]
    ppack, _ = _pack_rows_each(partials)
    pall = _allgather8("gather_partials", ppack)
    psum = _sum_slots("sum_partials", pall, F32)
    (g_ada_b, g_norm_mix, g_norm_ffn, g_final, g_conv_full, g_alog, g_dtb, g_gdn_ng, g_qg_full, g_kvg_full,
     loss_sum) = _unpack_rows_each(psum, part_shapes)
    dmod_all = _unpack_rows_each(pall, part_shapes[:1])[0]

    grads = dict(ada_b=g_ada_b, norm_mix_g=g_norm_mix, norm_ffn_g=g_norm_ffn, final_norm_g=g_final.reshape(D),
                 gdn_conv_w=_my_shard(g_conv_full, 1, chip), gdn_a_log=g_alog, gdn_dt_bias=g_dtb, gdn_norm_g=g_gdn_ng,
                 mla_q_norm_g=_my_shard(g_qg_full, 1, chip), mla_kv_norm_g=_my_shard(g_kvg_full, 1, chip))

    ca_t = jnp.zeros((D, LANES), BF16).at[:, :16].set(jnp.transpose(ca))
    dm_mine = lax.dynamic_slice_in_dim(dmod_all, chip * n_ada, n_ada, axis=2)
    grads["ada_w"] = jnp.stack([
        _mm(f"ada_bwd{l}", ca_t, jnp.pad(dm_mine[:, l], ((0, LANES - n_dev), (0, 0))), "nn") for l in range(DEPTH)])

    delta, new_m, new_v = {}, {}, {}
    results = {}
    keys = [k for _, ks, _ in in_flight for k in ks]
    halves = [_sum_slots(f"grads_sum_{n}{l}", s, F32) for _, ks, sw in in_flight for (n, l), s in zip(ks, sw)]
    others = _rs_swap("grads_swap", halves)
    for (n, l), mine, theirs in zip(keys, halves, others):
        results[n] = _adamw_piece(f"adamw_{n}{l}", two_d(ws[n]), two_d(ms[n]), two_d(vs[n]), mine, theirs,
                                  l * ws[n].shape[1], results.get(n), core_chip[:1])
    for n in big:
        grads[n], delta[n], new_m[n], new_v[n] = [stored(n, t.reshape(ws[n].shape)) for t in results[n]]
    delta["ada_w"], new_m["ada_w"], new_v["ada_w"] = _adamw("adamw_ada_w", ada_w, grads["ada_w"], m_ada_w, v_ada_w)
    for n in [n for n in _WEIGHT_ORDER if n not in delta]:
        delta[n], new_m[n], new_v[n] = _adamw("adamw_" + n, w[n], grads[n], m[n], v[n])

    loss = loss_sum.reshape(())
    return (loss, dx.reshape(1, T, D), *[grads[n] for n in _WEIGHT_ORDER], *[delta[n] for n in _WEIGHT_ORDER],
            *[new_m[n] for n in _WEIGHT_ORDER], *[new_v[n] for n in _WEIGHT_ORDER])
```

```python
import functools

import jax
import jax.numpy as jnp
from jax import lax
from jax.experimental import pallas as pl
from jax.experimental.pallas import tpu as pltpu
from jax.experimental.pallas import tpu_sc as plsc

F32 = jnp.float32
BF16 = jnp.bfloat16
HI = lax.Precision.HIGHEST
MESH = pl.DeviceIdType.MESH

D = 1024
DEPTH = 4
N_MOD = 6
NH = 8
HD = 128
CHUNK = 64
_GDN_HB = 8
GDN_QKV = 3 * NH * HD
GDN_INK = GDN_QKV + NH * HD + 2 * HD
Q_RANK, KV_RANK, ROPE = 384, 256, 64
MLA_INK = Q_RANK + KV_RANK + HD
DFF = 2816
EPS = 1e-6
ATT_SCALE = (HD + ROPE) ** -0.5
ROPE_THETA = 10000.0
LANES = 128
PACK_W = 1024

ADAM_LR, ADAM_B1, ADAM_B2, ADAM_EPS, ADAM_WD, ADAM_STEP = 0.001, 0.9, 0.999, 1e-08, 0.01, 10


H3 = "bf16x3"
B1 = "bf16"
HS = H3
HF = B1


def _dot(a, b, mode="nn", prec=None):
    dn = {"nn": (((1,), (0,)), ((), ())), "nt": (((1,), (1,)), ((), ())), "tn": (((0,), (0,)), ((), ()))}[mode]
    if prec == B1:
        return _dot(a.astype(BF16), b.astype(BF16), mode)
    if prec == H3:
        ah, bh = a.astype(BF16), b.astype(BF16)
        al, bl = (a - ah.astype(F32)).astype(BF16), (b - bh.astype(F32)).astype(BF16)
        return _dot(ah, bh, mode) + (_dot(ah, bl, mode) + _dot(al, bh, mode))
    return lax.dot_general(a, b, dn, precision=prec, preferred_element_type=F32)


def _sig(x):
    return 0.5 * jnp.tanh(0.5 * x) + 0.5


def _pick(n, cap):
    if n <= cap:
        return n
    best = None
    for d in range(LANES, cap + 1, LANES):
        if n % d == 0:
            best = d
    assert best is not None, (n, cap)
    return best


def _params(n_grid):
    return pltpu.CompilerParams(dimension_semantics=("arbitrary",) * n_grid, vmem_limit_bytes=56 * 1024 * 1024)


def _rowwise(name, fn, rows, consts, outs, sums=(), tr=256):
    first = rows[0][0] if isinstance(rows[0], tuple) else rows[0]
    T = first.shape[-2]
    tr = _slot_tile(T, tr)
    nr, nc, no, ns = len(rows), len(consts), len(outs), len(sums)

    windows = [c[1:] if isinstance(c, tuple) else None for c in consts]
    consts = [c[0] if isinstance(c, tuple) else c for c in consts]

    def body(*refs):
        vals = [r[...] for r in refs[:nr]]
        for r, win in zip(refs[nr:nr + nc], windows):
            vals.append(r[...] if win is None else r[win[0]:win[0] + 1, win[1] * win[2]:(win[1] + 1) * win[2]])
        res = fn(*vals)
        if not isinstance(res, (tuple, list)):
            res = (res,)
        o_refs = refs[nr + nc:nr + nc + no]
        s_refs = refs[nr + nc + no:]
        for r, val in zip(o_refs, res[:no]):
            r[...] = val.astype(r.dtype)
        if ns:
            @pl.when(pl.program_id(0) == 0)
            def _():
                for r in s_refs:
                    r[...] = jnp.zeros_like(r)
            for r, val in zip(s_refs, res[no:]):
                r[...] += val

    in_specs, args = [], []
    for a in rows:
        if isinstance(a, tuple):
            arr, width, cb = a
            in_specs.append(pl.BlockSpec((tr, width), lambda i, cb=cb: (i, cb)))
            args.append(arr)
        elif a.ndim == 3:
            in_specs.append(pl.BlockSpec((a.shape[0], tr, a.shape[2]), lambda i: (0, i, 0)))
            args.append(a)
        else:
            in_specs.append(pl.BlockSpec((tr, a.shape[1]), lambda i: (i, 0)))
            args.append(a)
    for a in consts:
        in_specs.append(pl.BlockSpec(a.shape, lambda i, nd=a.ndim: (0,) * nd))
        args.append(a)
    out_specs = [pl.BlockSpec((tr, w), lambda i: (i, 0)) for w, _ in outs]
    out_specs += [pl.BlockSpec((1, w), lambda i: (0, 0)) for w in sums]
    out_shape = [jax.ShapeDtypeStruct((T, w), dt) for w, dt in outs]
    out_shape += [jax.ShapeDtypeStruct((1, w), F32) for w in sums]
    res = pl.pallas_call(body, name=name, grid=(T // tr,), in_specs=in_specs, out_specs=out_specs,
                         out_shape=out_shape, compiler_params=_params(1))(*args)
    return res


def _mm(name, a, b, mode, out_dtype=F32, tm=512, tn=1024):
    if mode == "tn":
        K, M = a.shape
    else:
        M, K = a.shape
    N = b.shape[0] if mode == "nt" else b.shape[1]
    tm, tn = _pick(M, tm), _pick(N, tn)

    def body(a_ref, b_ref, o_ref):
        o_ref[...] = _dot(a_ref[...].astype(BF16), b_ref[...].astype(BF16), mode).astype(o_ref.dtype)

    a_spec = pl.BlockSpec((K, tm), lambda i, j: (0, i)) if mode == "tn" else pl.BlockSpec((tm, K), lambda i, j: (i, 0))
    b_spec = pl.BlockSpec((tn, K), lambda i, j: (j, 0)) if mode == "nt" else pl.BlockSpec((K, tn), lambda i, j: (0, j))
    return pl.pallas_call(body, name=name, grid=(M // tm, N // tn), in_specs=[a_spec, b_spec],
                          out_specs=pl.BlockSpec((tm, tn), lambda i, j: (i, j)),
                          out_shape=jax.ShapeDtypeStruct((M, N), out_dtype), compiler_params=_params(2))(a, b)


def _rms(x, eps=EPS):
    return lax.rsqrt(jnp.mean(x * x, axis=-1, keepdims=True) + eps)


def _norm_mod_fwd(name, x, g, scale, shift):
    def fn(x, g, scale, shift):
        return x * _rms(x) * g * (1.0 + scale) + shift
    return _rowwise(name, fn, [x], [g, scale, shift], [(D, BF16)])[0]


def _norm_mod_bwd(name, dh, x, dx_res, g, scale):
    def fn(dh, x, dx_res, g, scale):
        r = _rms(x)
        xh = x * r
        dxh = dh * (g * (1.0 + scale))
        dx = r * (dxh - xh * jnp.mean(dxh * xh, axis=-1, keepdims=True))
        dhx = dh * xh
        return (dx_res + dx, jnp.sum(dh, axis=0, keepdims=True), jnp.sum(dhx * g, axis=0, keepdims=True),
                jnp.sum(dhx * (1.0 + scale), axis=0, keepdims=True))
    return _rowwise(name, fn, [dh, x, dx_res], [g, scale], [(D, F32)], sums=[D, D, D])


def _residual_fwd(name, x, y, gate):
    def fn(x, y, gate):
        return x + gate * y
    return _rowwise(name, fn, [x, y], [gate], [(D, F32)])[0]


def _residual_norm_fwd(name, x, y, gate, g, scale, shift):
    def fn(x, y, gate, g, scale, shift):
        x = x + gate * y
        return x, x * _rms(x) * g * (1.0 + scale) + shift
    return _rowwise(name, fn, [x, y], [gate, g, scale, shift], [(D, F32), (D, BF16)])


def _norm_residual_bwd(name, dh, x, dx_res, g, scale, y, gate):
    def fn(dh, x, dx_res, y, g, scale, gate):
        r = _rms(x)
        xh = x * r
        dxh = dh * (g * (1.0 + scale))
        dx = dx_res + r * (dxh - xh * jnp.mean(dxh * xh, axis=-1, keepdims=True))
        dhx = dh * xh
        return (dx, dx * gate, jnp.sum(dh, axis=0, keepdims=True), jnp.sum(dhx * g, axis=0, keepdims=True),
                jnp.sum(dhx * (1.0 + scale), axis=0, keepdims=True), jnp.sum(dx * y, axis=0, keepdims=True))
    return _rowwise(name, fn, [dh, x, dx_res, y], [g, scale, gate], [(D, F32), (D, BF16)], sums=[D, D, D, D])


def _residual_bwd(name, dx, y, gate):
    def fn(dx, y, gate):
        return dx * gate, jnp.sum(dx * y, axis=0, keepdims=True)
    return _rowwise(name, fn, [dx, y], [gate], [(D, BF16)], sums=[D])


def _loss_head(x, target, g):
    def fn(x, t, g):
        r = _rms(x)
        xh = x * r
        err = xh * g - t
        loss = 0.5 * jnp.sum(jnp.mean(err * err, axis=-1, keepdims=True), axis=0, keepdims=True)
        dy = err * (1.0 / D)
        dxh = dy * g
        dx = r * (dxh - xh * jnp.mean(dxh * xh, axis=-1, keepdims=True))
        return dx, jnp.broadcast_to(loss, (1, LANES)), jnp.sum(dy * xh, axis=0, keepdims=True)
    return _rowwise("loss_head", fn, [x, target], [g], [(D, F32)], sums=[LANES, D])


def _ffn_up(name, h, wg, wu, layer, tm=2048):
    T, n = h.shape[0], wg.shape[1]
    tm = min(tm, T)

    def body(h_ref, wg_ref, wu_ref, a_ref, b_ref, s_ref):
        h = h_ref[...]
        a = _dot(h, wg_ref[0], "nt")
        b = _dot(h, wu_ref[0], "nt")
        a_ref[0] = a.astype(a_ref.dtype)
        b_ref[0] = b.astype(b_ref.dtype)
        s_ref[0] = (a * _sig(a) * b).astype(s_ref.dtype)

    wspec = pl.BlockSpec((1, n, D), lambda ch, i: (ch, layer, 0))
    ospec = pl.BlockSpec((1, tm, n), lambda ch, i: (ch, i, 0))
    return pl.pallas_call(
        body, name=name, grid=(4, T // tm), in_specs=[pl.BlockSpec((tm, D), lambda ch, i: (i, 0)), wspec, wspec],
        out_specs=[ospec, ospec, ospec],
        out_shape=[jax.ShapeDtypeStruct((4, T, n), BF16)] * 3, compiler_params=_params(2))(h, wg, wu)


def _ffn_down(name, s, wd, layer, tm=1024):
    _, T, n = s.shape
    tm = min(tm, T)

    def body(s_ref, w_ref, y_ref):
        @pl.when(pl.program_id(1) == 0)
        def _():
            y_ref[...] = jnp.zeros_like(y_ref)
        y_ref[...] += _dot(s_ref[0], w_ref[0], "nn")

    return pl.pallas_call(
        body, name=name, grid=(T // tm, 4),
        in_specs=[pl.BlockSpec((1, tm, n), lambda i, ch: (ch, i, 0)), pl.BlockSpec((1, n, D), lambda i, ch: (ch, layer, 0))],
        out_specs=pl.BlockSpec((tm, D), lambda i, ch: (i, 0)), out_shape=jax.ShapeDtypeStruct((T, D), F32),
        compiler_params=_params(2))(s, wd)


def _ffn_down_bwd(name, dy, wd, a, b, layer, tm=2048):
    _, T, n = a.shape
    tm = min(tm, T)

    def body(dy_ref, w_ref, a_ref, b_ref, da_ref, db_ref):
        ds = _dot(dy_ref[...], w_ref[0], "nt")
        a, b = a_ref[0].astype(F32), b_ref[0].astype(F32)
        sg = _sig(a)
        da_ref[0] = (ds * b * (sg * (1.0 + a * (1.0 - sg)))).astype(da_ref.dtype)
        db_ref[0] = (ds * (a * sg)).astype(db_ref.dtype)

    bspec = pl.BlockSpec((1, tm, n), lambda ch, i: (ch, i, 0))
    return pl.pallas_call(
        body, name=name, grid=(4, T // tm),
        in_specs=[pl.BlockSpec((tm, D), lambda ch, i: (i, 0)), pl.BlockSpec((1, n, D), lambda ch, i: (ch, layer, 0)), bspec, bspec],
        out_specs=[bspec, bspec], out_shape=[jax.ShapeDtypeStruct((4, T, n), BF16)] * 2,
        compiler_params=_params(2))(dy, wd, a, b)


def _ffn_down_dw(name, s, dy):
    _, T, n = s.shape

    def body(s_ref, dy_ref, o_ref):
        o_ref[0] = _dot(s_ref[0], dy_ref[...], "tn").astype(o_ref.dtype)

    return pl.pallas_call(
        body, name=name, grid=(4,),
        in_specs=[pl.BlockSpec((1, T, n), lambda ch: (ch, 0, 0)), pl.BlockSpec((T, D), lambda ch: (0, 0))],
        out_specs=pl.BlockSpec((1, n, D), lambda ch: (ch, 0, 0)), out_shape=jax.ShapeDtypeStruct((4, n, D), BF16),
        compiler_params=_params(1))(s, dy)


def _ffn_up_dw(name, h, da, db, tm=512):
    _, T, n = da.shape

    def body(h_ref, da_ref, db_ref, dg_ref, du_ref):
        h = h_ref[...]
        dg_ref[0] = _dot(da_ref[0], h, "tn").astype(dg_ref.dtype)
        du_ref[0] = _dot(db_ref[0], h, "tn").astype(du_ref.dtype)

    dspec = pl.BlockSpec((1, T, n), lambda ch, j: (ch, 0, 0))
    ospec = pl.BlockSpec((1, n, tm), lambda ch, j: (ch, 0, j))
    return pl.pallas_call(
        body, name=name, grid=(4, D // tm), in_specs=[pl.BlockSpec((T, tm), lambda ch, j: (0, j)), dspec, dspec],
        out_specs=[ospec, ospec], out_shape=[jax.ShapeDtypeStruct((4, n, D), BF16)] * 2,
        compiler_params=_params(2))(h, da, db)


def _ffn_up_dx(name, da, db, wg, wu, layer, tm=1024):
    _, T, n = da.shape
    tm = min(tm, T)

    def body(da_ref, db_ref, wg_ref, wu_ref, o_ref):
        @pl.when(pl.program_id(1) == 0)
        def _():
            o_ref[...] = jnp.zeros_like(o_ref)
        o_ref[...] += _dot(da_ref[0], wg_ref[0], "nn") + _dot(db_ref[0], wu_ref[0], "nn")

    dspec = pl.BlockSpec((1, tm, n), lambda i, ch: (ch, i, 0))
    wspec = pl.BlockSpec((1, n, D), lambda i, ch: (ch, layer, 0))
    return pl.pallas_call(
        body, name=name, grid=(T // tm, 4), in_specs=[dspec, dspec, wspec, wspec],
        out_specs=pl.BlockSpec((tm, D), lambda i, ch: (i, 0)), out_shape=jax.ShapeDtypeStruct((T, D), F32),
        compiler_params=_params(2))(da, db, wg, wu)


def _shift_down(x, k):
    if k == 0:
        return x
    rows = lax.broadcasted_iota(jnp.int32, x.shape, 0)
    return jnp.where(rows >= k, pltpu.roll(x, k, 0), 0.0)


def _shift_up(x, k):
    if k == 0:
        return x
    T = x.shape[0]
    rows = lax.broadcasted_iota(jnp.int32, x.shape, 0)
    return jnp.where(rows < T - k, pltpu.roll(x, T - k, 0), 0.0)


def _conv_silu(x, w):
    c = w[0:1, :] * _shift_down(x, 3) + w[1:2, :] * _shift_down(x, 2) + w[2:3, :] * _shift_down(x, 1) + w[3:4, :] * x
    sg = _sig(c)
    return c, sg, c * sg


def _gdn_conv_fwd(name, proj, cw):
    T = proj.shape[0]

    def body(x_ref, w_ref, o_ref):
        j = pl.program_id(0)
        _, _, y = _conv_silu(x_ref[...], w_ref[...])
        r = lax.rsqrt(jnp.sum(y * y, axis=1, keepdims=True) + EPS)
        mult = jnp.where(j < NH, HD ** -0.5, 1.0)
        o_ref[...] = jnp.where(j < 2 * NH, y * (r * mult), y)

    return pl.pallas_call(body, name=name, grid=(3 * NH,),
                          in_specs=[pl.BlockSpec((T, HD), lambda j: (0, j)), pl.BlockSpec((4, HD), lambda j: (0, j))],
                          out_specs=pl.BlockSpec((T, HD), lambda j: (0, j)),
                          out_shape=jax.ShapeDtypeStruct((T, GDN_QKV), F32), compiler_params=_params(1))(proj, cw)


def _gdn_conv_bwd(name, proj, cw, dz):
    T = proj.shape[0]

    def body(x_ref, w_ref, dz_ref, dx_ref, dw_ref):
        j = pl.program_id(0)
        x, w, dz = x_ref[...], w_ref[...], dz_ref[...]
        c, sg, y = _conv_silu(x, w)
        r = lax.rsqrt(jnp.sum(y * y, axis=1, keepdims=True) + EPS)
        mult = jnp.where(j < NH, HD ** -0.5, 1.0)
        dyn = mult * (r * dz - (r * r * r) * y * jnp.sum(dz * y, axis=1, keepdims=True))
        dy = jnp.where(j < 2 * NH, dyn, dz)
        dc = dy * (sg * (1.0 + c * (1.0 - sg)))
        dx = w[0:1, :] * _shift_up(dc, 3) + w[1:2, :] * _shift_up(dc, 2) + w[2:3, :] * _shift_up(dc, 1) + w[3:4, :] * dc
        dx_ref[...] = dx.astype(dx_ref.dtype)
        for k in range(4):
            dw_ref[pl.ds(k, 1), :] = jnp.sum(dc * _shift_down(x, 3 - k), axis=0, keepdims=True)

    return pl.pallas_call(body, name=name, grid=(3 * NH,),
                          in_specs=[pl.BlockSpec((T, HD), lambda j: (0, j)), pl.BlockSpec((4, HD), lambda j: (0, j)),
                                    pl.BlockSpec((T, HD), lambda j: (0, j))],
                          out_specs=[pl.BlockSpec((T, HD), lambda j: (0, j)), pl.BlockSpec((4, HD), lambda j: (0, j))],
                          out_shape=[jax.ShapeDtypeStruct((T, GDN_QKV), BF16), jax.ShapeDtypeStruct((4, GDN_QKV), F32)],
                          compiler_params=_params(1))(proj, cw, dz)


def _softplus(z):
    return jnp.maximum(z, 0.0) + jnp.log(1.0 + jnp.exp(-jnp.abs(z)))


_AB_CB = GDN_INK // (2 * HD) - 1


def _gdn_gates_fwd(name, proj, alog, dtb):
    def fn(ab, alog, dtb):
        a, b = ab[:, :HD], ab[:, HD:]
        return -jnp.exp(alog) * _softplus(a + dtb), _sig(b)
    return _rowwise(name, fn, [(proj, 2 * HD, _AB_CB)], [alog, dtb], [(HD, F32), (HD, F32)])


def _gdn_gates_bwd(name, proj, dg_h, db_h, alog, dtb):
    def fn(ab, dg_h, db_h, alog, dtb):
        lane = lax.broadcasted_iota(jnp.int32, (1, HD), 1)
        dg = jnp.zeros(dg_h.shape[1:], F32)
        dbeta = jnp.zeros(dg_h.shape[1:], F32)
        for h in range(NH):
            oh = (lane == h).astype(F32)
            dg = dg + dg_h[h] * oh
            dbeta = dbeta + db_h[h] * oh
        a, b = ab[:, :HD], ab[:, HD:]
        z = a + dtb
        ea = jnp.exp(alog)
        beta = _sig(b)
        da = dg * (-ea) * _sig(z)
        db = dbeta * beta * (1.0 - beta)
        return (jnp.concatenate([da, db], axis=1), jnp.sum(dg * (-ea * _softplus(z)), axis=0, keepdims=True),
                jnp.sum(da, axis=0, keepdims=True))
    return _rowwise(name, fn, [(proj, 2 * HD, _AB_CB), dg_h, db_h], [alog, dtb], [(2 * HD, BF16)], sums=[HD, HD])


def _interleave(gens):
    gens = list(gens)
    results = [None] * len(gens)
    active = list(range(len(gens)))
    while active:
        for i in list(active):
            try:
                next(gens[i])
            except StopIteration as stop:
                results[i] = stop.value
                active.remove(i)
    return results


def _chunk_common(q, k, v, gblk, bblk, h, prec):
    C = CHUNK
    lane = lax.broadcasted_iota(jnp.int32, (1, HD), 1)
    oh = (lane == h).astype(F32)
    g_col = jnp.sum(gblk * oh, axis=1, keepdims=True)
    beta = jnp.sum(bblk * oh, axis=1, keepdims=True)
    ri = lax.broadcasted_iota(jnp.int32, (C, C), 0)
    ci = lax.broadcasted_iota(jnp.int32, (C, C), 1)
    incl = ri >= ci
    strict = ri > ci
    eye = (ri == ci).astype(F32)
    gcb = _dot(incl.astype(F32), jnp.broadcast_to(g_col, (C, HD)), "nn", HI)
    yield
    gc = gcb[:, :C]
    gc_row = _dot(jnp.ones((C, C), F32), eye * gc, "nn", HI)
    yield
    decay = jnp.where(incl, jnp.exp(jnp.where(incl, gc - gc_row, 0.0)), 0.0)
    rows = lax.broadcasted_iota(jnp.int32, (C, HD), 0)
    gclb = jnp.sum(jnp.where(rows == C - 1, gcb, 0.0), axis=0, keepdims=True)
    eg = jnp.exp(gcb)
    egl = jnp.exp(gclb - gcb)
    gl = jnp.exp(gclb)
    kb = k * beta
    m1 = _dot(kb, k, "nt", prec)
    qk = _dot(q, k, "nt", prec)
    yield
    L = jnp.where(strict, m1 * decay, 0.0)
    nl = -L
    tinv = eye + nl
    p = nl
    for _ in range(5):
        p = _dot(p, p, "nn", H3)
        yield
        tinv = tinv + _dot(tinv, p, "nn", H3)
    vb = v * beta
    kbg = kb * eg
    yield
    u = _dot(tinv, vb, "nn", prec)
    w = _dot(tinv, kbg, "nn", prec)
    yield
    attn = jnp.where(incl, qk * decay, 0.0)
    return dict(beta=beta, incl=incl, strict=strict, decay=decay, eg=eg, egl=egl, gl=gl, kb=kb, m1=m1, tinv=tinv,
                kbg=kbg, u=u, w=w, qk=qk, attn=attn, q_dec=q * eg, k_dec=k * egl, rows=rows, oh=oh)


def _gdn_chunk_fwd(name, qkv, g, beta):
    T = qkv.shape[0]
    N = T // CHUNK

    hb = _GDN_HB
    w = hb * HD

    def body(q_ref, k_ref, v_ref, g_ref, b_ref, o_ref, st_ref, S):
        hg, n = pl.program_id(0), pl.program_id(1)

        @pl.when(n == 0)
        def _():
            S[...] = jnp.zeros_like(S)

        gblk, bblk = g_ref[...], b_ref[...]

        def one_head(i, q, k, v, s):
            c = yield from _chunk_common(q, k, v, gblk, bblk, hg * hb + i, HF)
            v_new = c["u"] - _dot(c["w"], s, "nn", HF)
            qs = _dot(c["q_dec"], s, "nn", HF)
            yield
            o = qs + _dot(c["attn"], v_new, "nn", HF)
            return o, s * c["gl"] + _dot(c["k_dec"], v_new, "tn", HF)

        sls = [slice(i * HD, (i + 1) * HD) for i in range(hb)]
        states = [S[i] for i in range(hb)]
        res = _interleave(one_head(i, q_ref[:, sls[i]], k_ref[:, sls[i]], v_ref[:, sls[i]], states[i]) for i in range(hb))
        for i, (o, s_new) in enumerate(res):
            st_ref[i, 0] = states[i]
            o_ref[:, sls[i]] = o
            S[i] = s_new

    blk = lambda off: pl.BlockSpec((CHUNK, w), lambda h, n, off=off: (n, off + h))
    gspec = pl.BlockSpec((CHUNK, HD), lambda h, n: (n, 0))
    return pl.pallas_call(
        body, name=name, grid=(NH // hb, N), in_specs=[blk(0), blk(NH // hb), blk(2 * NH // hb), gspec, gspec],
        out_specs=[pl.BlockSpec((CHUNK, w), lambda h, n: (n, h)), pl.BlockSpec((hb, 1, HD, HD), lambda h, n: (h, n, 0, 0))],
        out_shape=[jax.ShapeDtypeStruct((T, NH * HD), F32), jax.ShapeDtypeStruct((NH, N, HD, HD), F32)],
        scratch_shapes=[pltpu.VMEM((hb, HD, HD), F32)], compiler_params=_params(2))(qkv, qkv, qkv, g, beta)


def _gdn_chunk_bwd(name, qkv, g, beta, states, do):
    T = qkv.shape[0]
    N = T // CHUNK
    C = CHUNK

    hb = _GDN_HB
    w = hb * HD
    assert hb == NH

    def body(q_ref, k_ref, v_ref, g_ref, b_ref, st_ref, do_ref, dqkv_ref, dg_ref, db_ref, dS):
        hg, n = pl.program_id(0), pl.program_id(1)

        @pl.when(n == 0)
        def _():
            dS[...] = jnp.zeros_like(dS)

        gblk, bblk = g_ref[...], b_ref[...]
        sls = [slice(i * HD, (i + 1) * HD) for i in range(hb)]
        res = _interleave(one_head(hg * hb + i, gblk, bblk, q_ref[:, sls[i]], k_ref[:, sls[i]], v_ref[:, sls[i]],
                                   st_ref[i, 0], do_ref[:, sls[i]], dS[i]) for i in range(hb))
        for i, (dq, dk, dv, dg, db, ds_new) in enumerate(res):
            dqkv_ref[:, sls[i]] = dq
            dqkv_ref[:, slice(w + i * HD, w + (i + 1) * HD)] = dk
            dqkv_ref[:, slice(2 * w + i * HD, 2 * w + (i + 1) * HD)] = dv
            dg_ref[i] = dg
            db_ref[i] = db
            dS[i] = ds_new

    def one_head(h, gblk, bblk, q, k, v, s, do, ds):
        c = yield from _chunk_common(q, k, v, gblk, bblk, h, HF)
        eg, egl, gl, beta, decay, tinv = c["eg"], c["egl"], c["gl"], c["beta"], c["decay"], c["tinv"]
        v_new = c["u"] - _dot(c["w"], s, "nn", HF)
        dq_dec = _dot(do, s, "nt", HF)
        yield
        dv_new = _dot(c["attn"], do, "tn", HF) + _dot(c["k_dec"], ds, "nn", HF)
        dk_dec = _dot(v_new, ds, "nt", HF)
        dgl = jnp.sum(jnp.sum(s * ds, axis=1, keepdims=True), axis=0, keepdims=True)
        yield
        ds_new = ds * gl + _dot(c["q_dec"], do, "tn", HF) - _dot(c["w"], dv_new, "tn", HF)
        dattn = jnp.where(c["incl"], _dot(do, v_new, "nt", HF), 0.0)
        dw = -_dot(dv_new, s, "nt", HF)
        yield
        dvb = _dot(tinv, dv_new, "tn", HS)
        dkbg = _dot(tinv, dw, "tn", HS)
        yield
        dA = -(_dot(dvb, c["u"], "nt", HS) + _dot(dkbg, c["w"], "nt", HS))
        yield
        dL = jnp.where(c["strict"], dA, 0.0)
        dm1 = dL * decay
        dqk = dattn * decay
        xdec = (dL * c["m1"] + dattn * c["qk"]) * decay
        dkb = _dot(dm1, k, "nn", HS) + dkbg * eg
        dk = _dot(dm1, c["kb"], "tn", HS) + _dot(dqk, q, "tn", HS) + dk_dec * egl + dkb * beta
        dq = _dot(dqk, k, "nn", HS) + dq_dec * eg
        yield
        dkd_kd = jnp.sum(dk_dec * c["k_dec"], axis=1, keepdims=True)
        dgc = (jnp.sum(xdec, axis=1, keepdims=True) - _dot(xdec, jnp.ones((C, HD), F32), "tn", HS)
               + jnp.sum(dq_dec * c["q_dec"], axis=1, keepdims=True) - dkd_kd
               + jnp.sum(dkbg * c["kbg"], axis=1, keepdims=True))
        dgcl = jnp.sum(dkd_kd, axis=0, keepdims=True) + dgl * gl
        dgc = dgc + jnp.where(c["rows"] == C - 1, dgcl, 0.0)
        ri = lax.broadcasted_iota(jnp.int32, (C, C), 0)
        ci = lax.broadcasted_iota(jnp.int32, (C, C), 1)
        dg = _dot((ci >= ri).astype(F32), dgc, "nn", HI)
        db = jnp.broadcast_to(jnp.sum(dkb * k, axis=1, keepdims=True) + jnp.sum(dvb * v, axis=1, keepdims=True), (C, HD))
        return dq, dk, dvb * beta, dg, db, ds_new

    blk = lambda off: pl.BlockSpec((C, w), lambda h, n, off=off: (N - 1 - n, off + h))
    gspec = pl.BlockSpec((C, HD), lambda h, n: (N - 1 - n, 0))
    ospec = pl.BlockSpec((C, w), lambda h, n: (N - 1 - n, h))
    hspec = pl.BlockSpec((hb, C, HD), lambda h, n: (h, N - 1 - n, 0))
    return pl.pallas_call(
        body, name=name, grid=(NH // hb, N),
        in_specs=[blk(0), blk(NH // hb), blk(2 * NH // hb), gspec, gspec,
                  pl.BlockSpec((hb, 1, HD, HD), lambda h, n: (h, N - 1 - n, 0, 0)), ospec],
        out_specs=[pl.BlockSpec((C, 3 * w), lambda h, n: (N - 1 - n, 0)), hspec, hspec],
        out_shape=[jax.ShapeDtypeStruct((T, 3 * NH * HD), F32)] + [jax.ShapeDtypeStruct((NH, T, HD), F32)] * 2,
        scratch_shapes=[pltpu.VMEM((hb, HD, HD), F32)], compiler_params=_params(2))(qkv, qkv, qkv, g, beta, states, do)


_GATE_CB = GDN_QKV // (NH * HD)


def _gdn_gated_norm_fwd(name, o, proj, ng):
    def fn(o, gate, ng):
        outs = []
        for h in range(NH):
            sl = slice(h * HD, (h + 1) * HD)
            oh, gh = o[:, sl], gate[:, sl]
            outs.append(oh * _rms(oh) * ng * (gh * _sig(gh)))
        return jnp.concatenate(outs, axis=1)
    return _rowwise(name, fn, [o, (proj, NH * HD, _GATE_CB)], [ng], [(NH * HD, BF16)])[0]


def _gdn_gated_norm_bwd(name, don, o, proj, ng):
    def fn(don, o, gate, ng):
        dos, dgs = [], []
        dng = jnp.zeros((1, HD), F32)
        for h in range(NH):
            sl = slice(h * HD, (h + 1) * HD)
            oh, gh, dh = o[:, sl], gate[:, sl], don[:, sl]
            r = _rms(oh)
            xh = oh * r
            sg = _sig(gh)
            dn = dh * (gh * sg)
            dgs.append(dh * (xh * ng) * (sg * (1.0 + gh * (1.0 - sg))))
            dng = dng + jnp.sum(dn * xh, axis=0, keepdims=True)
            dxh = dn * ng
            dos.append(r * (dxh - xh * jnp.mean(dxh * xh, axis=-1, keepdims=True)))
        return jnp.concatenate(dos, axis=1), jnp.concatenate(dgs, axis=1), dng
    return _rowwise(name, fn, [don, o, (proj, NH * HD, _GATE_CB)], [ng], [(NH * HD, F32), (NH * HD, BF16)], sums=[HD])


def _rot(x):
    lane = lax.broadcasted_iota(jnp.int32, x.shape, 1)
    return jnp.where(lane < ROPE // 2, -pltpu.roll(x, HD - ROPE // 2, 1), pltpu.roll(x, ROPE // 2, 1))


def _rot_t(x):
    lane = lax.broadcasted_iota(jnp.int32, x.shape, 1)
    return jnp.where(lane < ROPE // 2, pltpu.roll(x, HD - ROPE // 2, 1), -pltpu.roll(x, ROPE // 2, 1))


def _rope_tables(pos_col):
    lane = jnp.arange(HD)
    inv_freq = ROPE_THETA ** (-(2.0 * (lane % (ROPE // 2)).astype(F32)) / ROPE)
    inv_freq = jnp.where(lane < ROPE, inv_freq, 0.0).astype(F32)[None, :]
    valid = (lane < ROPE).astype(F32)[None, :]

    def fn(pos, inv_freq, valid):
        ang = pos.astype(F32) * inv_freq
        return jnp.cos(ang) * valid, jnp.sin(ang) * valid
    return _rowwise("rope_tables", fn, [pos_col], [inv_freq, valid], [(HD, F32), (HD, F32)])


def _mla_pre_fwd(name, proj, cos, sin, qg, kvg):
    def fn(p, cos, sin, qg, kvg):
        cq, ckv, kr = p[:, :Q_RANK], p[:, Q_RANK:Q_RANK + KV_RANK], p[:, Q_RANK + KV_RANK:]
        return cq * _rms(cq) * qg, ckv * _rms(ckv) * kvg, kr * cos + _rot(kr) * sin
    return _rowwise(name, fn, [proj, cos, sin], [qg, kvg], [(Q_RANK, BF16), (KV_RANK, BF16), (HD, BF16)])


def _rms_bwd(dy, x, g):
    r = _rms(x)
    xh = x * r
    dxh = dy * g
    return r * (dxh - xh * jnp.mean(dxh * xh, axis=-1, keepdims=True)), jnp.sum(dy * xh, axis=0, keepdims=True)


def _mla_pre_bwd(name, proj, dcqn, dckvn, dkr, cos, sin, qg, kvg):
    def fn(p, dcqn, dckvn, dkr, cos, sin, qg, kvg):
        cq, ckv = p[:, :Q_RANK], p[:, Q_RANK:Q_RANK + KV_RANK]
        dcq, dqg = _rms_bwd(dcqn, cq, qg)
        dckv, dkvg = _rms_bwd(dckvn, ckv, kvg)
        dkr_pre = dkr * cos + _rot_t(dkr * sin)
        return jnp.concatenate([dcq, dckv, dkr_pre], axis=1), dqg, dkvg
    return _rowwise(name, fn, [proj, dcqn, dckvn, dkr, cos, sin], [qg, kvg], [(MLA_INK, BF16)], sums=[Q_RANK, KV_RANK])


def _mla_q_fwd(name, q, cos, sin):
    def fn(qn, qr, cos, sin):
        outs = []
        for h in range(NH):
            x = qr[:, h * HD:(h + 1) * HD]
            outs.append(x * cos + _rot(x) * sin)
        return qn, jnp.concatenate(outs, axis=1)
    return _rowwise(name, fn, [(q, NH * HD, 0), (q, NH * HD, 1), cos, sin], [], [(NH * HD, BF16), (NH * HD, BF16)])


def _mla_q_bwd(name, dqn, dqr, cos, sin):
    def fn(dqn, dqr, cos, sin):
        outs = [dqn]
        for h in range(NH):
            z = dqr[:, h * HD:(h + 1) * HD]
            outs.append(z * cos + _rot_t(z * sin))
        return jnp.concatenate(outs, axis=1)
    return _rowwise(name, fn, [dqn, dqr, cos, sin], [], [(2 * NH * HD, BF16)])[0]


def _att_probs(qn, qr, kn, kr, row0):
    s = (_dot(qn, kn, "nt") + _dot(qr, kr, "nt")) * ATT_SCALE
    qpos = row0 + lax.broadcasted_iota(jnp.int32, s.shape, 0)
    kpos = lax.broadcasted_iota(jnp.int32, s.shape, 1)
    s = jnp.where(kpos <= qpos, s, -1e30)
    p = jnp.exp(s - jnp.max(s, axis=1, keepdims=True))
    return p * (1.0 / jnp.sum(p, axis=1, keepdims=True))


def _mla_attn_fwd(name, qn, qr, kv, kr, tq=256):
    T = qn.shape[0]
    tq = min(tq, T)

    def body(qn_ref, qr_ref, kn_ref, v_ref, kr_ref, o_ref):
        i = pl.program_id(1)
        for blk in range(T // tq):
            @pl.when(i == blk)
            def _(blk=blk):
                keys = pl.ds(0, (blk + 1) * tq)
                p = _att_probs(qn_ref[...], qr_ref[...], kn_ref[keys, :], kr_ref[keys, :], blk * tq)
                o_ref[...] = _dot(p.astype(BF16), v_ref[keys, :], "nn").astype(o_ref.dtype)

    qspec = pl.BlockSpec((tq, HD), lambda h, i: (i, h))
    return pl.pallas_call(
        body, name=name, grid=(NH, T // tq),
        in_specs=[qspec, qspec, pl.BlockSpec((T, HD), lambda h, i: (0, h)), pl.BlockSpec((T, HD), lambda h, i: (0, NH + h)),
                  pl.BlockSpec((T, HD), lambda h, i: (0, 0))],
        out_specs=qspec, out_shape=jax.ShapeDtypeStruct((T, NH * HD), BF16), compiler_params=_params(2))(qn, qr, kv, kv, kr)


def _mla_attn_bwd(name, qn, qr, kv, kr, do, tq=256):
    T = qn.shape[0]
    tq = min(tq, T)

    def body(qn_ref, qr_ref, kn_ref, v_ref, kr_ref, do_ref, dqn_ref, dqr_ref, dkn_ref, dv_ref, dkr_ref):
        h, i = pl.program_id(0), pl.program_id(1)

        @pl.when(i == 0)
        def _():
            dkn_ref[...] = jnp.zeros_like(dkn_ref)
            dv_ref[...] = jnp.zeros_like(dv_ref)

        @pl.when((i == 0) & (h == 0))
        def _():
            dkr_ref[...] = jnp.zeros_like(dkr_ref)

        for blk in range(T // tq):
            @pl.when(i == blk)
            def _(blk=blk):
                keys = pl.ds(0, (blk + 1) * tq)
                qn, qr, do = qn_ref[...], qr_ref[...], do_ref[...]
                kn, kr, v = kn_ref[keys, :], kr_ref[keys, :], v_ref[keys, :]
                p = _att_probs(qn, qr, kn, kr, blk * tq)
                dp = _dot(do, v, "nt")
                ds = (p * (dp - jnp.sum(p * dp, axis=1, keepdims=True)) * ATT_SCALE).astype(BF16)
                dqn_ref[...] = _dot(ds, kn, "nn")
                dqr_ref[...] = _dot(ds, kr, "nn")
                dkn_ref[keys, :] += _dot(ds, qn, "tn")
                dkr_ref[keys, :] += _dot(ds, qr, "tn")
                dv_ref[keys, :] += _dot(p.astype(BF16), do, "tn")

    qspec = pl.BlockSpec((tq, HD), lambda h, i: (i, h))
    kspec = pl.BlockSpec((T, HD), lambda h, i: (0, h))
    return pl.pallas_call(
        body, name=name, grid=(NH, T // tq),
        in_specs=[qspec, qspec, kspec, pl.BlockSpec((T, HD), lambda h, i: (0, NH + h)),
                  pl.BlockSpec((T, HD), lambda h, i: (0, 0)), qspec],
        out_specs=[qspec, qspec, kspec, kspec, pl.BlockSpec((T, HD), lambda h, i: (0, 0))],
        out_shape=[jax.ShapeDtypeStruct((T, NH * HD), F32)] * 4 + [jax.ShapeDtypeStruct((T, HD), F32)],
        compiler_params=_params(2))(qn, qr, kv, kv, kr, do)


def _mod_rows(mod, layer):
    return [(mod, layer, i, D) for i in range(N_MOD)]


def _local_step(x, target, pos_col, mod, weights_of, P, on_grads):
    cos, sin = _rope_tables(pos_col)
    saved = []
    sh_m, sc_m = _mod_rows(mod, 0)[:2]
    h = _norm_mod_fwd("norm_mix0", x, (P["norm_mix_g"], 0, 0, D), sc_m, sh_m)
    for l in range(DEPTH):
        j = l // 2
        sh_m, sc_m, ga_m, sh_f, sc_f, ga_f = _mod_rows(mod, l)
        s = dict(x0=x)
        W = weights_of(l, h)
        s.update(h=h, W=W)
        if l % 2 == 0:
            proj = _mm(f"gdn_in{j}", h, W["gdn_in"], "nn", tn=GDN_INK // 2)
            qkv = _gdn_conv_fwd(f"gdn_conv{j}", proj, P["gdn_cw"][j])
            g, beta = _gdn_gates_fwd(f"gdn_gates{j}", proj, P["gdn_alog"][j], P["gdn_dtb"][j])
            o, states = _gdn_chunk_fwd(f"gdn_chunk{j}", qkv, g, beta)
            on = _gdn_gated_norm_fwd(f"gdn_gnorm{j}", o, proj, P["gdn_ng"][j])
            y = _mm(f"gdn_out{j}", on, W["gdn_out"], "nn")
            s.update(proj=proj, qkv=qkv, g=g, beta=beta, o=o, states=states, on=on)
        else:
            proj = _mm(f"mla_in{j}", h, W["mla_in"], "nn")
            cqn, ckvn, kr = _mla_pre_fwd(f"mla_pre{j}", proj, cos, sin, P["mla_qg"][j], P["mla_kvg"][j])
            q = _mm(f"mla_uq{j}", cqn, W["mla_uq"], "nn")
            kv = _mm(f"mla_ukv{j}", ckvn, W["mla_ukv"], "nn", out_dtype=BF16)
            qn, qr = _mla_q_fwd(f"mla_q{j}", q, cos, sin)
            o = _mla_attn_fwd(f"mla_attn{j}", qn, qr, kv, kr)
            y = _mm(f"mla_out{j}", o, W["mla_out"], "nn")
            s.update(proj=proj, cqn=cqn, ckvn=ckvn, kr=kr, kv=kv, qn=qn, qr=qr, o=o)
        s["y"] = y
        x, h2 = _residual_norm_fwd(f"res_mix{l}", x, y, ga_m, (P["norm_ffn_g"], l, 0, D), sc_f, sh_f)
        s["x1"] = x
        fa, fb, sw = _ffn_up(f"ffn_up{l}", h2, W["ffn_g"], W["ffn_u"], 0)
        yf = _ffn_down(f"ffn_down{l}", sw, W["ffn_d"], 0)
        if l + 1 < DEPTH:
            sh_n, sc_n = _mod_rows(mod, l + 1)[:2]
            x, h = _residual_norm_fwd(f"res_ffn{l}", x, yf, ga_f, (P["norm_mix_g"], l + 1, 0, D), sc_n, sh_n)
        else:
            x = _residual_fwd(f"res_ffn{l}", x, yf, ga_f)
        s.update(h2=h2, fa=fa, fb=fb, sw=sw, yf=yf)
        saved.append(s)

    dx, loss, d_final = _loss_head(x, target, P["final_g"])
    gP = dict(loss=loss, final_g=d_final, norm_mix_g=[None] * DEPTH, norm_ffn_g=[None] * DEPTH,
              gdn_cw=[None] * 2, gdn_alog=[None] * 2, gdn_dtb=[None] * 2, gdn_ng=[None] * 2,
              mla_qg=[None] * 2, mla_kvg=[None] * 2)
    dmod = [None] * DEPTH
    dyf, d_ga_f = _residual_bwd(f"res_ffn_b{DEPTH - 1}", dx, saved[-1]["yf"], _mod_rows(mod, DEPTH - 1)[5])
    for l in reversed(range(DEPTH)):
        j = l // 2
        s = saved[l]
        W = s["W"]
        sh_m, sc_m, ga_m, sh_f, sc_f, ga_f = _mod_rows(mod, l)
        da, db = _ffn_down_bwd(f"ffn_down_dx{l}", dyf, W["ffn_d"], s["fa"], s["fb"], 0)
        g_down = _ffn_down_dw(f"ffn_down_dw{l}", s["sw"], dyf)
        g_gate, g_up = _ffn_up_dw(f"ffn_up_dw{l}", s["h2"], da, db)
        on_grads(l, "ffn", dict(ffn_w_gate=g_gate, ffn_w_up=g_up, ffn_w_down=g_down))
        dh2 = _ffn_up_dx(f"ffn_up_dx{l}", da, db, W["ffn_g"], W["ffn_u"], 0)
        dx, dy, d_sh_f, d_sc_f, gP["norm_ffn_g"][l], d_ga_m = _norm_residual_bwd(
            f"norm_ffn_b{l}", dh2, s["x1"], dx, (P["norm_ffn_g"], l, 0, D), sc_f, s["y"], ga_m)
        if l % 2 == 0:
            don = _mm(f"gdn_out_dx{j}", dy, W["gdn_out"], "nt")
            g_out = _mm(f"gdn_out_dw{j}", s["on"], dy, "tn", out_dtype=BF16)
            do, dgate, gP["gdn_ng"][j] = _gdn_gated_norm_bwd(f"gdn_gnorm_b{j}", don, s["o"], s["proj"], P["gdn_ng"][j])
            dqkv, dg_h, db_h = _gdn_chunk_bwd(f"gdn_chunk_b{j}", s["qkv"], s["g"], s["beta"], s["states"], do)
            dab_, gP["gdn_alog"][j], gP["gdn_dtb"][j] = _gdn_gates_bwd(f"gdn_gates_b{j}", s["proj"], dg_h, db_h,
                                                                        P["gdn_alog"][j], P["gdn_dtb"][j])
            dpre, gP["gdn_cw"][j] = _gdn_conv_bwd(f"gdn_conv_b{j}", s["proj"], P["gdn_cw"][j], dqkv)
            dproj = jnp.concatenate([dpre, dgate, dab_], axis=1)
            g_in = _mm(f"gdn_in_dw{j}", s["h"], dproj, "tn", out_dtype=BF16, tn=GDN_INK // 2)
            on_grads(l, "mix", dict(gdn_w_in=_uncols(_gdn_in_from_kernel(g_in)), gdn_w_out=_unrows(g_out)))
            dh = _mm(f"gdn_in_dx{j}", dproj, W["gdn_in"], "nt")
        else:
            do = _mm(f"mla_out_dx{j}", dy, W["mla_out"], "nt", out_dtype=BF16)
            g_out = _mm(f"mla_out_dw{j}", s["o"], dy, "tn", out_dtype=BF16)
            dqn, dqr, dkn, dv, dkr = _mla_attn_bwd(f"mla_attn_b{j}", s["qn"], s["qr"], s["kv"], s["kr"], do)
            dq = _mla_q_bwd(f"mla_q_b{j}", dqn, dqr, cos, sin)
            dkv = jnp.concatenate([dkn, dv], axis=1)
            g_uq = _mm(f"mla_uq_dw{j}", s["cqn"], dq, "tn", out_dtype=BF16)
            dcqn = _mm(f"mla_uq_dx{j}", dq, W["mla_uq"], "nt")
            g_ukv = _mm(f"mla_ukv_dw{j}", s["ckvn"], dkv, "tn", out_dtype=BF16)
            dckvn = _mm(f"mla_ukv_dx{j}", dkv, W["mla_ukv"], "nt")
            dproj, gP["mla_qg"][j], gP["mla_kvg"][j] = _mla_pre_bwd(f"mla_pre_b{j}", s["proj"], dcqn, dckvn, dkr, cos, sin,
                                                                     P["mla_qg"][j], P["mla_kvg"][j])
            g_in = _mm(f"mla_in_dw{j}", s["h"], dproj, "tn", out_dtype=BF16)
            on_grads(l, "mix", dict(mla_w_in=_unrows(g_in[:, :Q_RANK + KV_RANK + ROPE]), mla_w_uq=_uncols(_mla_uq_from_kernel(g_uq)),
                                    mla_w_ukv=_uncols(_mla_ukv_from_kernel(g_ukv)), mla_w_out=_unrows(g_out)))
            dh = _mm(f"mla_in_dx{j}", dproj, W["mla_in"], "nt")
        if l > 0:
            dx, dyf_prev, d_sh_m, d_sc_m, gP["norm_mix_g"][l], d_ga_f_prev = _norm_residual_bwd(
                f"norm_mix_b{l}", dh, s["x0"], dx, (P["norm_mix_g"], l, 0, D), sc_m, saved[l - 1]["yf"], _mod_rows(mod, l - 1)[5])
        else:
            dx, d_sh_m, d_sc_m, gP["norm_mix_g"][l] = _norm_mod_bwd(f"norm_mix_b{l}", dh, s["x0"], dx,
                                                                     (P["norm_mix_g"], l, 0, D), sc_m)
        dmod[l] = jnp.concatenate([d_sh_m, d_sc_m, d_ga_m, d_sh_f, d_sc_f, d_ga_f], axis=1)
        if l > 0:
            dyf, d_ga_f = dyf_prev, d_ga_f_prev
    return dx, jnp.concatenate(dmod, axis=0), gP


def _pad_cols(a, width):
    return jnp.pad(a, ((0, 0), (0, width - a.shape[1])))


def _gdn_in_to_kernel(w):
    m = GDN_QKV + NH * HD
    return jnp.concatenate([w[:, :m], _pad_cols(w[:, m:m + NH], HD), _pad_cols(w[:, m + NH:], HD)], axis=1)


def _gdn_in_from_kernel(g):
    m = GDN_QKV + NH * HD
    return jnp.concatenate([g[:, :m], g[:, m:m + NH], g[:, m + HD:m + HD + NH]], axis=1)


def _mla_uq_to_kernel(w):
    w3 = w.reshape(Q_RANK, NH, HD + ROPE)
    rope = jnp.pad(w3[:, :, HD:], ((0, 0), (0, 0), (0, HD - ROPE)))
    return jnp.concatenate([w3[:, :, :HD].reshape(Q_RANK, NH * HD), rope.reshape(Q_RANK, NH * HD)], axis=1)


def _mla_uq_from_kernel(g):
    gn = g[:, :NH * HD].reshape(Q_RANK, NH, HD)
    gr = g[:, NH * HD:].reshape(Q_RANK, NH, HD)[:, :, :ROPE]
    return jnp.concatenate([gn, gr], axis=2).reshape(Q_RANK, NH * (HD + ROPE))


def _mla_ukv_to_kernel(w):
    w3 = w.reshape(KV_RANK, NH, 2 * HD)
    return jnp.concatenate([w3[:, :, :HD].reshape(KV_RANK, NH * HD), w3[:, :, HD:].reshape(KV_RANK, NH * HD)], axis=1)


def _mla_ukv_from_kernel(g):
    gk = g[:, :NH * HD].reshape(KV_RANK, NH, HD)
    gv = g[:, NH * HD:].reshape(KV_RANK, NH, HD)
    return jnp.concatenate([gk, gv], axis=2).reshape(KV_RANK, NH * 2 * HD)


def _cols(t):
    return jnp.moveaxis(t, 0, 1).reshape(t.shape[1], -1)


def _uncols(g):
    return jnp.moveaxis(g.reshape(g.shape[0], 4, -1), 1, 0)


def _rows(t):
    return t.reshape(-1, t.shape[2])


def _unrows(g):
    return g.reshape(4, -1, g.shape[1])


def _layer_weights(layer):
    mixer = ("gdn_w_in", "gdn_w_out") if layer % 2 == 0 else ("mla_w_in", "mla_w_uq", "mla_w_ukv", "mla_w_out")
    return [(n, layer // 2) for n in mixer] + [(n, layer) for n in ("ffn_w_gate", "ffn_w_up", "ffn_w_down")]


def _weights_to_kernel(layer, g):
    out = dict(ffn_g=g["ffn_w_gate"], ffn_u=g["ffn_w_up"], ffn_d=g["ffn_w_down"])
    if layer % 2 == 0:
        out.update(gdn_in=_gdn_in_to_kernel(_cols(g["gdn_w_in"])), gdn_out=_rows(g["gdn_w_out"]))
    else:
        out.update(mla_in=_pad_cols(_rows(g["mla_w_in"]), MLA_INK), mla_uq=_mla_uq_to_kernel(_cols(g["mla_w_uq"])),
                   mla_ukv=_mla_ukv_to_kernel(_cols(g["mla_w_ukv"])), mla_out=_rows(g["mla_w_out"]))
    return out


def _small_to_kernel(norm_mix_g, norm_ffn_g, final_norm_g, gdn_conv_w, gdn_a_log, gdn_dt_bias, gdn_norm_g, q_norm_g, kv_norm_g):
    return dict(
        norm_mix_g=norm_mix_g, norm_ffn_g=norm_ffn_g, final_g=final_norm_g.reshape(1, D),
        gdn_cw=[jnp.transpose(gdn_conv_w[j]) for j in range(2)],
        gdn_alog=[_pad_cols(gdn_a_log[j:j + 1], HD) for j in range(2)],
        gdn_dtb=[_pad_cols(gdn_dt_bias[j:j + 1], HD) for j in range(2)],
        gdn_ng=[gdn_norm_g[j:j + 1] for j in range(2)],
        mla_qg=[q_norm_g[j:j + 1] for j in range(2)],
        mla_kvg=[kv_norm_g[j:j + 1] for j in range(2)],
    )


_CHIP_FLIPS = ((1, 0), (0, 1), (1, 1))
_ANY = pl.BlockSpec(memory_space=pl.ANY)


def _me():
    return lax.axis_index("x"), lax.axis_index("y"), lax.axis_index("c")


def _chip_peer(dx, dy):
    x, y, c = _me()
    return ((1 - x) if dx else x, (1 - y) if dy else y, c)


def _rcopy(src, dst, send_sem, recv_sem, to):
    return pltpu.make_async_remote_copy(src_ref=src, dst_ref=dst, send_sem=send_sem, recv_sem=recv_sem,
                                        device_id=to, device_id_type=MESH)


def _allgather4(name, a, halves=False):
    R, C = a.shape
    rh = R // 2 if halves else R

    def body(a_ref, out_ref, send_sems, recv_sems, local_sem):
        x, y, c = _me()
        me = 2 * x + y
        src = a_ref.at[pl.ds(c * rh, rh)] if halves else a_ref
        local = pltpu.make_async_copy(src, out_ref.at[me], local_sem)
        local.start()
        sends = []
        for k, (dx, dy) in enumerate(_CHIP_FLIPS):
            cp = _rcopy(src, out_ref.at[me], send_sems.at[k], recv_sems.at[k], _chip_peer(dx, dy))
            cp.start()
            sends.append(cp)
        for k, (dx, dy) in enumerate(_CHIP_FLIPS):
            px, py, _ = _chip_peer(dx, dy)
            _rcopy(src, out_ref.at[2 * px + py], send_sems.at[k], recv_sems.at[k], _chip_peer(dx, dy)).wait_recv()
        for cp in sends:
            cp.wait_send()
        local.wait()

    return pl.pallas_call(
        body, name=name, in_specs=[_ANY], out_specs=_ANY, out_shape=jax.ShapeDtypeStruct((4, rh, C), a.dtype),
        scratch_shapes=[pltpu.SemaphoreType.DMA((3,)), pltpu.SemaphoreType.DMA((3,)), pltpu.SemaphoreType.DMA(())])(a)


_NCH = 4


def _dma_sems(*counts):
    return [pltpu.SemaphoreType.DMA((n,)) for n in counts]


def _slot_tile(rows, cap=512):
    best = rows
    for tr in range(16, min(rows, cap) + 1, 16):
        if rows % tr == 0:
            best = tr
    return best


def _cast_into_slot(name, a, chip, row0, rows):
    C = a.shape[1]
    tr = _slot_tile(rows)
    assert row0 % tr == 0
    first = row0 // tr

    def body(c_ref, a_ref, o_ref):
        o_ref[0] = a_ref[...].astype(o_ref.dtype)

    grid_spec = pltpu.PrefetchScalarGridSpec(
        num_scalar_prefetch=1, grid=(rows // tr,), in_specs=[pl.BlockSpec((tr, C), lambda i, c_ref: (first + i, 0))],
        out_specs=pl.BlockSpec((1, tr, C), lambda i, c_ref: (c_ref[0], i, 0)))
    return pl.pallas_call(body, name=name, grid_spec=grid_spec, out_shape=jax.ShapeDtypeStruct((4, rows, C), BF16),
                          compiler_params=_params(1))(chip, a)


def _chunks(rows, align):
    for nch in (_NCH, 2):
        if rows % (nch * align) == 0:
            return nch
    return 1


def _gather_exchange(out, ici_s, ici_r, d2d_s, d2d_r):
    n = len(out)
    x, y, c = _me()
    me = 2 * x + y
    sib = (x, y, 1 - c)
    peers = [_chip_peer(dx, dy) for dx, dy in _CHIP_FLIPS]
    for t in range(n):
        h = out[t].shape[1] // 2
        nch = _chunks(h, 16)
        ch = h // nch
        for k, peer in enumerate(peers):
            for i in range(nch):
                blk = out[t].at[me, pl.ds(c * h + i * ch, ch)]
                _rcopy(blk, blk, ici_s.at[3 * t + k], ici_r.at[3 * t + k], peer).start()
    for t in range(n):
        h = out[t].shape[1] // 2
        nch = _chunks(h, 16)
        ch = h // nch
        for k, peer in enumerate(peers):
            pchip = 2 * peer[0] + peer[1]
            got = out[t].at[pchip, pl.ds(c * h, h)]
            _rcopy(got, got, ici_s.at[3 * t + k], ici_r.at[3 * t + k], peer).wait_recv()
            for i in range(nch):
                blk = out[t].at[pchip, pl.ds(c * h + i * ch, ch)]
                _rcopy(blk, blk, d2d_s.at[3 * t + k], d2d_r.at[3 * t + k], sib).start()
    for t in range(n):
        h = out[t].shape[1] // 2
        for k, peer in enumerate(peers):
            pchip = 2 * peer[0] + peer[1]
            other = out[t].at[pchip, pl.ds((1 - c) * h, h)]
            _rcopy(other, other, d2d_s.at[3 * t + k], d2d_r.at[3 * t + k], sib).wait_recv()
            _rcopy(other, other, ici_s.at[3 * t + k], ici_r.at[3 * t + k], peer).wait_send()
            _rcopy(other, other, d2d_s.at[3 * t + k], d2d_r.at[3 * t + k], sib).wait_send()


def _gather_weights(name, bufs):
    n = len(bufs)

    def body(*refs):
        _gather_exchange(refs[n:2 * n], *refs[2 * n:])

    return pl.pallas_call(
        body, name=name, in_specs=[_ANY] * n, out_specs=[_ANY] * n,
        out_shape=[jax.ShapeDtypeStruct(s.shape, s.dtype) for s in bufs],
        input_output_aliases={t: t for t in range(n)},
        scratch_shapes=_dma_sems(3 * n, 3 * n, 3 * n, 3 * n))(*bufs)


def _gather_weights_async(name, collective_id, bufs):
    n = len(bufs)
    refs = [jax.new_ref(b, memory_space=pltpu.MemorySpace.HBM) for b in bufs]

    @pl.kernel(mesh=plsc.ScalarSubcoreMesh(axis_name="sequencer", num_cores=1), name=name,
               scratch_types=tuple(_dma_sems(3 * n, 3 * n, 3 * n, 3 * n)),
               compiler_params=pltpu.CompilerParams(collective_id=collective_id))
    def launch(ici_s, ici_r, d2d_s, d2d_r):
        x, y, c = _me()
        barrier = pltpu.get_barrier_semaphore()
        for peer in [_chip_peer(dx, dy) for dx, dy in _CHIP_FLIPS] + [(x, y, 1 - c)]:
            pl.semaphore_signal(barrier, inc=1, device_id=peer, device_id_type=MESH)
        pl.semaphore_wait(barrier, 4)
        _gather_exchange(refs, ici_s, ici_r, d2d_s, d2d_r)

    launch()
    return [r[...] for r in refs]


def _rs_split(name, grads):
    n = len(grads)

    def body(*refs):
        g, out = refs[:n], refs[n:2 * n]
        send, recv = refs[2 * n:]
        x, y, c = _me()
        sib = (x, y, 1 - c)
        for t in range(n):
            h = g[t].shape[1] // 2
            for d in range(4):
                _rcopy(g[t].at[d, pl.ds((1 - c) * h, h)], out[t].at[d], send.at[t], recv.at[t], sib).start()
        for t in range(n):
            _rcopy(out[t], out[t], send.at[t], recv.at[t], sib).wait()

    return pl.pallas_call(
        body, name=name, in_specs=[_ANY] * n, out_specs=[_ANY] * n,
        out_shape=[jax.ShapeDtypeStruct((4, s.shape[1] // 2, s.shape[2]), s.dtype) for s in grads],
        scratch_shapes=_dma_sems(n, n))(*grads)


def _pair_add(name, g, theirs, core_chip):
    _, R, C = g.shape
    h = R // 2
    tr = _slot_tile(h)
    nb = h // tr

    def body(s_ref, g_ref, t_ref, p_ref, o_ref):
        val = (g_ref[...].astype(F32) + t_ref[...].astype(F32)).astype(p_ref.dtype)
        p_ref[...] = val

        @pl.when(pl.program_id(1) == s_ref[1])
        def _():
            o_ref[...] = val

    spec = pl.BlockSpec((1, tr, C), lambda i, d, s_ref: (d, i, 0))
    grid_spec = pltpu.PrefetchScalarGridSpec(
        num_scalar_prefetch=1, grid=(nb, 4),
        in_specs=[pl.BlockSpec((1, tr, C), lambda i, d, s_ref: (d, s_ref[0] * nb + i, 0)), spec],
        out_specs=[spec, pl.BlockSpec((1, tr, C), lambda i, d, s_ref: (s_ref[1], i, 0))])
    half = jax.ShapeDtypeStruct((4, h, C), BF16)
    return pl.pallas_call(body, name=name, grid_spec=grid_spec, out_shape=[half, half],
                          compiler_params=_params(2))(core_chip, g, theirs)


def _rs_alltoall_async(name, collective_id, parts, bufs):
    n = len(parts)
    p = [jax.new_ref(a, memory_space=pltpu.MemorySpace.HBM) for a in parts]
    out = [jax.new_ref(b, memory_space=pltpu.MemorySpace.HBM) for b in bufs]

    @pl.kernel(mesh=plsc.ScalarSubcoreMesh(axis_name="sequencer", num_cores=1), name=name,
               scratch_types=tuple(_dma_sems(3 * n, 3 * n)),
               compiler_params=pltpu.CompilerParams(collective_id=collective_id))
    def launch(send, recv):
        barrier = pltpu.get_barrier_semaphore()
        for peer in [_chip_peer(dx, dy) for dx, dy in _CHIP_FLIPS]:
            pl.semaphore_signal(barrier, inc=1, device_id=peer, device_id_type=MESH)
        pl.semaphore_wait(barrier, 3)
        _alltoall_exchange(p, out, send, recv)

    launch()
    return [r[...] for r in out]


def _alltoall_exchange(p, out, send, recv):
    x, y, c = _me()
    me = 2 * x + y
    peers = [_chip_peer(dx, dy) for dx, dy in _CHIP_FLIPS]
    for t in range(len(p)):
        h = p[t].shape[1]
        nch = _chunks(h, 16)
        ch = h // nch
        for k, peer in enumerate(peers):
            pchip = 2 * peer[0] + peer[1]
            for i in range(nch):
                rows = pl.ds(i * ch, ch)
                _rcopy(p[t].at[pchip, rows], out[t].at[me, rows], send.at[3 * t + k], recv.at[3 * t + k], peer).start()
    for t in range(len(p)):
        for k, peer in enumerate(peers):
            pchip = 2 * peer[0] + peer[1]
            _rcopy(out[t].at[pchip], out[t].at[pchip], send.at[3 * t + k], recv.at[3 * t + k], peer).wait()


def _rs_swap(name, halves):
    n = len(halves)

    def body(*refs):
        a, out = refs[:n], refs[n:2 * n]
        send, recv = refs[2 * n:]
        x, y, c = _me()
        sib = (x, y, 1 - c)
        for t in range(n):
            ch = a[t].shape[0] // _NCH
            for i in range(_NCH):
                rows = pl.ds(i * ch, ch)
                _rcopy(a[t].at[rows], out[t].at[rows], send.at[t], recv.at[t], sib).start()
        for t in range(n):
            _rcopy(a[t], out[t], send.at[t], recv.at[t], sib).wait()

    return pl.pallas_call(
        body, name=name, in_specs=[_ANY] * n, out_specs=[_ANY] * n,
        out_shape=[jax.ShapeDtypeStruct(s.shape, s.dtype) for s in halves],
        scratch_shapes=_dma_sems(n, n))(*halves)


def _sibling_merge(name, a):
    P_, rh, C = a.shape

    def body(a_ref, out_ref, send_sem, recv_sem, local_sem):
        x, y, c = _me()
        local = pltpu.make_async_copy(a_ref, out_ref.at[:, pl.ds(c * rh, rh)], local_sem)
        local.start()
        cp = _rcopy(a_ref, out_ref.at[:, pl.ds(c * rh, rh)], send_sem, recv_sem, (x, y, 1 - c))
        cp.start()
        cp.wait_send()
        _rcopy(a_ref, out_ref.at[:, pl.ds((1 - c) * rh, rh)], send_sem, recv_sem, (x, y, 1 - c)).wait_recv()
        local.wait()

    return pl.pallas_call(
        body, name=name, in_specs=[_ANY], out_specs=_ANY, out_shape=jax.ShapeDtypeStruct((P_, 2 * rh, C), a.dtype),
        scratch_shapes=[pltpu.SemaphoreType.DMA(()), pltpu.SemaphoreType.DMA(()), pltpu.SemaphoreType.DMA(())])(a)


def _allgather8(name, a):
    g4 = _allgather4(name + "_chips", a)
    both = _sibling_merge(name + "_cores", g4.reshape(1, 4 * a.shape[0], a.shape[1]))
    return jnp.transpose(both.reshape(2, 4, *a.shape), (1, 0, 2, 3)).reshape(8, *a.shape)


def _sum_slots(name, a, out_dtype):
    def fn(a):
        acc = a[0].astype(F32)
        for k in range(1, a.shape[0]):
            acc = acc + a[k].astype(F32)
        return acc
    return _rowwise(name, fn, [a], [], [(a.shape[2], out_dtype)])[0]


def _adamw_math(w, g, m, v):
    m = ADAM_B1 * m + (1.0 - ADAM_B1) * g
    v = ADAM_B2 * v + (1.0 - ADAM_B2) * (g * g)
    m_hat = m / (1.0 - ADAM_B1 ** ADAM_STEP)
    v_hat = v / (1.0 - ADAM_B2 ** ADAM_STEP)
    return -ADAM_LR * (m_hat / (jnp.sqrt(v_hat) + ADAM_EPS) + ADAM_WD * w), m, v


def _adamw_piece(name, w2, m2, v2, mine, theirs, row0, prev, core):
    R, C = w2.shape
    h = mine.shape[0]
    tr = _slot_tile(h, 256)
    nb = h // tr
    assert row0 % tr == 0
    first = row0 // tr

    def body(c_ref, w_ref, m_ref, v_ref, a_ref, b_ref, *rest):
        g_ref, d_ref, nm_ref, nv_ref = rest[-4:]
        g = jnp.where(pl.program_id(0) == c_ref[0], a_ref[...], b_ref[...])
        g_ref[...] = g
        d_ref[...], nm_ref[...], nv_ref[...] = _adamw_math(w_ref[...], g, m_ref[...], v_ref[...])

    full = pl.BlockSpec((tr, C), lambda s, i, c_ref: (first + s * nb + i, 0))
    mine_spec = pl.BlockSpec((tr, C), lambda s, i, c_ref: (jnp.where(s == c_ref[0], i, 0), 0))
    theirs_spec = pl.BlockSpec((tr, C), lambda s, i, c_ref: (jnp.where(s == c_ref[0], 0, i), 0))
    extra = [] if prev is None else list(prev)
    grid_spec = pltpu.PrefetchScalarGridSpec(
        num_scalar_prefetch=1, grid=(2, nb), in_specs=[full, full, full, mine_spec, theirs_spec] + [_ANY] * len(extra),
        out_specs=[full] * 4)
    return pl.pallas_call(
        body, name=name, grid_spec=grid_spec, out_shape=[jax.ShapeDtypeStruct((R, C), F32)] * 4,
        input_output_aliases={6 + k: k for k in range(len(extra))}, compiler_params=_params(2))(core, w2, m2, v2, mine, theirs, *extra)


def _adamw(name, w, g, m, v):
    shape = w.shape
    two_d = (-1, shape[-1]) if w.ndim > 1 else (1, -1)
    w2, g2, m2, v2 = [t.reshape(two_d) for t in (w, g, m, v)]
    rows = w2.shape[0]
    tr = rows
    for cand in (256, 128, 64, 32, 16, 8):
        if rows % cand == 0:
            tr = cand
            break

    c = w2.shape[1]
    outs = _rowwise(name, _adamw_math, [w2, g2, m2, v2], [], [(c, F32)] * 3, tr=tr)
    return [o.reshape(shape) for o in outs]


_WEIGHT_ORDER = ("ada_w", "ada_b", "norm_mix_g", "norm_ffn_g", "gdn_w_in", "gdn_conv_w", "gdn_a_log", "gdn_dt_bias",
                 "gdn_norm_g", "gdn_w_out", "mla_w_in", "mla_q_norm_g", "mla_kv_norm_g", "mla_w_uq", "mla_w_ukv",
                 "mla_w_out", "ffn_w_gate", "ffn_w_up", "ffn_w_down", "final_norm_g")
_BIG = (("gdn_w_in", 2), ("gdn_w_out", 1), ("mla_w_in", 1), ("mla_w_uq", 2), ("mla_w_ukv", 2), ("mla_w_out", 1),
        ("ffn_w_gate", 2), ("ffn_w_up", 2), ("ffn_w_down", 1))
_SMALL_SHARDED = (("gdn_conv_w", 1), ("mla_q_norm_g", 1), ("mla_kv_norm_g", 1))
_STORED_TRANSPOSED = ("ffn_w_gate", "ffn_w_up")


def _size(shape):
    n = 1
    for s in shape:
        n *= s
    return n


def _pack_rows_each(tensors):
    parts, offs, off = [], [], 0
    for t in tensors:
        flat = t.reshape(-1).astype(F32)
        rows = -(-flat.shape[0] // PACK_W)
        parts.append(jnp.pad(flat, (0, rows * PACK_W - flat.shape[0])).reshape(rows, PACK_W))
        offs.append(off)
        off += rows
    total = -(-off // 16) * 16
    pack = jnp.pad(parts[0], ((offs[0], total - offs[0] - parts[0].shape[0]), (0, 0)))
    for p, o in zip(parts[1:], offs[1:]):
        pack = pack + jnp.pad(p, ((o, total - o - p.shape[0]), (0, 0)))
    return pack, offs


def _unpack_rows_each(pack, shapes):
    lead = pack.shape[:-2]
    out, off = [], 0
    for shp in shapes:
        n = _size(shp)
        rows = -(-n // PACK_W)
        out.append(pack[..., off:off + rows, :].reshape(*lead, -1)[..., :n].reshape(*lead, *shp))
        off += rows
    return out


def _merge_chips(stacked, axis):
    moved = jnp.moveaxis(stacked, 0, axis)
    shp = list(moved.shape)
    return moved.reshape(shp[:axis] + [shp[axis] * shp[axis + 1]] + shp[axis + 2:])


def _my_shard(full, axis, chip):
    n = full.shape[axis] // 4
    return lax.dynamic_slice_in_dim(full, chip * n, n, axis)


def kernel(x, c, positions, ada_w, ada_b, norm_mix_g, norm_ffn_g, gdn_w_in, gdn_conv_w, gdn_a_log, gdn_dt_bias, gdn_norm_g, gdn_w_out, mla_w_in, mla_q_norm_g, mla_kv_norm_g, mla_w_uq, mla_w_ukv, mla_w_out, ffn_w_gate, ffn_w_up, ffn_w_down, final_norm_g, loss_target, m_ada_w, m_ada_b, m_norm_mix_g, m_norm_ffn_g, m_gdn_w_in, m_gdn_conv_w, m_gdn_a_log, m_gdn_dt_bias, m_gdn_norm_g, m_gdn_w_out, m_mla_w_in, m_mla_q_norm_g, m_mla_kv_norm_g, m_mla_w_uq, m_mla_w_ukv, m_mla_w_out, m_ffn_w_gate, m_ffn_w_up, m_ffn_w_down, m_final_norm_g, v_ada_w, v_ada_b, v_norm_mix_g, v_norm_ffn_g, v_gdn_w_in, v_gdn_conv_w, v_gdn_a_log, v_gdn_dt_bias, v_gdn_norm_g, v_gdn_w_out, v_mla_w_in, v_mla_q_norm_g, v_mla_kv_norm_g, v_mla_w_uq, v_mla_w_ukv, v_mla_w_out, v_ffn_w_gate, v_ffn_w_up, v_ffn_w_down, v_final_norm_g):
    w = dict(ada_w=ada_w, ada_b=ada_b, norm_mix_g=norm_mix_g, norm_ffn_g=norm_ffn_g, gdn_w_in=gdn_w_in, gdn_conv_w=gdn_conv_w,
             gdn_a_log=gdn_a_log, gdn_dt_bias=gdn_dt_bias, gdn_norm_g=gdn_norm_g, gdn_w_out=gdn_w_out, mla_w_in=mla_w_in,
             mla_q_norm_g=mla_q_norm_g, mla_kv_norm_g=mla_kv_norm_g, mla_w_uq=mla_w_uq, mla_w_ukv=mla_w_ukv,
             mla_w_out=mla_w_out, ffn_w_gate=ffn_w_gate, ffn_w_up=ffn_w_up, ffn_w_down=ffn_w_down, final_norm_g=final_norm_g)
    m = dict(ada_w=m_ada_w, ada_b=m_ada_b, norm_mix_g=m_norm_mix_g, norm_ffn_g=m_norm_ffn_g, gdn_w_in=m_gdn_w_in,
             gdn_conv_w=m_gdn_conv_w, gdn_a_log=m_gdn_a_log, gdn_dt_bias=m_gdn_dt_bias, gdn_norm_g=m_gdn_norm_g,
             gdn_w_out=m_gdn_w_out, mla_w_in=m_mla_w_in, mla_q_norm_g=m_mla_q_norm_g, mla_kv_norm_g=m_mla_kv_norm_g,
             mla_w_uq=m_mla_w_uq, mla_w_ukv=m_mla_w_ukv, mla_w_out=m_mla_w_out, ffn_w_gate=m_ffn_w_gate,
             ffn_w_up=m_ffn_w_up, ffn_w_down=m_ffn_w_down, final_norm_g=m_final_norm_g)
    v = dict(ada_w=v_ada_w, ada_b=v_ada_b, norm_mix_g=v_norm_mix_g, norm_ffn_g=v_norm_ffn_g, gdn_w_in=v_gdn_w_in,
             gdn_conv_w=v_gdn_conv_w, gdn_a_log=v_gdn_a_log, gdn_dt_bias=v_gdn_dt_bias, gdn_norm_g=v_gdn_norm_g,
             gdn_w_out=v_gdn_w_out, mla_w_in=v_mla_w_in, mla_q_norm_g=v_mla_q_norm_g, mla_kv_norm_g=v_mla_kv_norm_g,
             mla_w_uq=v_mla_w_uq, mla_w_ukv=v_mla_w_ukv, mla_w_out=v_mla_w_out, ffn_w_gate=v_ffn_w_gate,
             ffn_w_up=v_ffn_w_up, ffn_w_down=v_ffn_w_down, final_norm_g=v_final_norm_g)
    T = x.shape[1]
    ix, iy, ic = _me()
    chip = 2 * ix + iy
    seq = 2 * chip + ic
    n_dev = 8

    small_shapes = [w[n].shape for n, _ in _SMALL_SHARDED] + [c.shape]
    pack0, _ = _pack_rows_each([w[n] for n, _ in _SMALL_SHARDED] + [c])
    got0 = _unpack_rows_each(_allgather8("gather_small", pack0), small_shapes)
    small_full = {n: _merge_chips(g[0::2], ax) for (n, ax), g in zip(_SMALL_SHARDED, got0)}
    c_all = got0[-1].reshape(n_dev, D)

    big = [n for n, _ in _BIG]
    chip_arr = chip.astype(jnp.int32).reshape(1)

    def stored(n, t):
        return jnp.swapaxes(t, 1, 2) if n in _STORED_TRANSPOSED else t

    ws, ms, vs = [{n: stored(n, d[n]) for n in big} for d in (w, m, v)]
    two_d = lambda t: t.reshape(-1, t.shape[-1])

    gathered = []
    for l in range(DEPTH):
        names = _layer_weights(l)
        bufs = [_cast_into_slot(f"to_bf16_{n}{l}", two_d(ws[n]), chip_arr, j * ws[n].shape[1], ws[n].shape[1]) for n, j in names]
        filled = _gather_weights("gather_weights0", bufs) if l == 0 else _gather_weights_async(f"gather_weights{l}", l, bufs)
        gathered.append({n: b for (n, _), b in zip(names, filled)})

    def weights_of(l, h):
        return _weights_to_kernel(l, gathered[l])

    P = _small_to_kernel(norm_mix_g, norm_ffn_g, final_norm_g, small_full["gdn_conv_w"], gdn_a_log, gdn_dt_bias,
                         gdn_norm_g, small_full["mla_q_norm_g"], small_full["mla_kv_norm_g"])

    c16 = jnp.pad(c_all, ((0, 16 - n_dev), (0, 0)))
    ca = _rowwise("cond_silu", lambda t: t * _sig(t), [c16], [], [(D, BF16)])[0]
    n_ada = ada_w.shape[2]
    mods = jnp.concatenate([_mm(f"ada_fwd{l}", ca, ada_w[l], "nn") for l in range(DEPTH)], axis=0)
    mods_all = _allgather4("gather_mod", mods).reshape(4, DEPTH, 16, n_ada)
    mod_mm = jnp.transpose(lax.dynamic_index_in_dim(mods_all, seq, axis=2, keepdims=False), (1, 0, 2)).reshape(DEPTH, 4 * n_ada)
    mod = _rowwise("mod_bias", lambda a, b: a + b, [mod_mm, ada_b], [], [(4 * n_ada, F32)])[0]

    core_chip = jnp.stack([ic, chip]).astype(jnp.int32)
    pending, in_flight = {}, []

    def reduce_group(layer, part, pieces):
        pending.update({(n, layer if n.startswith("ffn_") else layer // 2): g for n, g in pieces.items()})
        if part == "ffn" or layer == 3:
            return
        keys = list(pending)
        glist = [pending.pop(k) for k in keys]
        tag = f"{layer}{part}"
        theirs = _rs_split("grads_cores_" + tag, glist)
        both = [_pair_add(f"grads_pair_{n}{l}", g, t, core_chip) for (n, l), g, t in zip(keys, glist, theirs)]
        swapped = _rs_alltoall_async("grads_chips_" + tag, DEPTH + 1 + len(in_flight), [p for p, _ in both], [o for _, o in both])
        in_flight.append((tag, keys, swapped))

    dx, dmod, gP = _local_step(x.reshape(T, D), loss_target.reshape(T, D), positions.reshape(T, 1), mod, weights_of, P, reduce_group)

    partials = [dmod, jnp.concatenate(gP["norm_mix_g"]), jnp.concatenate(gP["norm_ffn_g"]), gP["final_g"],
                jnp.stack([jnp.transpose(g) for g in gP["gdn_cw"]]), jnp.concatenate(gP["gdn_alog"])[:, :NH],
                jnp.concatenate(gP["gdn_dtb"])[:, :NH], jnp.concatenate(gP["gdn_ng"]), jnp.concatenate(gP["mla_qg"]),
                jnp.concatenate(gP["mla_kvg"]), gP["loss"][:, :1]]
    part_shapes = [p.shape for p in partials]
    ppack, _ = _pack_rows_each(partials)
    pall = _allgather8("gather_partials", ppack)
    psum = _sum_slots("sum_partials", pall, F32)
    (g_ada_b, g_norm_mix, g_norm_ffn, g_final, g_conv_full, g_alog, g_dtb, g_gdn_ng, g_qg_full, g_kvg_full,
     loss_sum) = _unpack_rows_each(psum, part_shapes)
    dmod_all = _unpack_rows_each(pall, part_shapes[:1])[0]

    grads = dict(ada_b=g_ada_b, norm_mix_g=g_norm_mix, norm_ffn_g=g_norm_ffn, final_norm_g=g_final.reshape(D),
                 gdn_conv_w=_my_shard(g_conv_full, 1, chip), gdn_a_log=g_alog, gdn_dt_bias=g_dtb, gdn_norm_g=g_gdn_ng,
                 mla_q_norm_g=_my_shard(g_qg_full, 1, chip), mla_kv_norm_g=_my_shard(g_kvg_full, 1, chip))

    ca_t = jnp.zeros((D, LANES), BF16).at[:, :16].set(jnp.transpose(ca))
    dm_mine = lax.dynamic_slice_in_dim(dmod_all, chip * n_ada, n_ada, axis=2)
    grads["ada_w"] = jnp.stack([
        _mm(f"ada_bwd{l}", ca_t, jnp.pad(dm_mine[:, l], ((0, LANES - n_dev), (0, 0))), "nn") for l in range(DEPTH)])

    delta, new_m, new_v = {}, {}, {}
    results = {}
    keys = [k for _, ks, _ in in_flight for k in ks]
    halves = [_sum_slots(f"grads_sum_{n}{l}", s, F32) for _, ks, sw in in_flight for (n, l), s in zip(ks, sw)]
    others = _rs_swap("grads_swap", halves)
    for (n, l), mine, theirs in zip(keys, halves, others):
        results[n] = _adamw_piece(f"adamw_{n}{l}", two_d(ws[n]), two_d(ms[n]), two_d(vs[n]), mine, theirs,
                                  l * ws[n].shape[1], results.get(n), core_chip[:1])
    for n in big:
        grads[n], delta[n], new_m[n], new_v[n] = [stored(n, t.reshape(ws[n].shape)) for t in results[n]]
    delta["ada_w"], new_m["ada_w"], new_v["ada_w"] = _adamw("adamw_ada_w", ada_w, grads["ada_w"], m_ada_w, v_ada_w)
    for n in [n for n in _WEIGHT_ORDER if n not in delta]:
        delta[n], new_m[n], new_v[n] = _adamw("adamw_" + n, w[n], grads[n], m[n], v[n])

    loss = loss_sum.reshape(())
    return (loss, dx.reshape(1, T, D), *[grads[n] for n in _WEIGHT_ORDER], *[delta[n] for n in _WEIGHT_ORDER],
            *[new_m[n] for n in _WEIGHT_ORDER], *[new_v[n] for n in _WEIGHT_ORDER])
```

```python
import functools

import jax
import jax.numpy as jnp
from jax import lax
from jax.experimental import pallas as pl
from jax.experimental.pallas import tpu as pltpu
from jax.experimental.pallas import tpu_sc as plsc

F32 = jnp.float32
BF16 = jnp.bfloat16
HI = lax.Precision.HIGHEST
MESH = pl.DeviceIdType.MESH

D = 1024
DEPTH = 4
N_MOD = 6
NH = 8
HD = 128
CHUNK = 64
_GDN_HB = 8
GDN_QKV = 3 * NH * HD
GDN_INK = GDN_QKV + NH * HD + 2 * HD
Q_RANK, KV_RANK, ROPE = 384, 256, 64
MLA_INK = Q_RANK + KV_RANK + HD
DFF = 2816
EPS = 1e-6
ATT_SCALE = (HD + ROPE) ** -0.5
ROPE_THETA = 10000.0
LANES = 128
PACK_W = 1024

ADAM_LR, ADAM_B1, ADAM_B2, ADAM_EPS, ADAM_WD, ADAM_STEP = 0.001, 0.9, 0.999, 1e-08, 0.01, 10


H3 = "bf16x3"
B1 = "bf16"
HS = H3
HF = B1


def _dot(a, b, mode="nn", prec=None):
    dn = {"nn": (((1,), (0,)), ((), ())), "nt": (((1,), (1,)), ((), ())), "tn": (((0,), (0,)), ((), ()))}[mode]
    if prec == B1:
        return _dot(a.astype(BF16), b.astype(BF16), mode)
    if prec == H3:
        ah, bh = a.astype(BF16), b.astype(BF16)
        al, bl = (a - ah.astype(F32)).astype(BF16), (b - bh.astype(F32)).astype(BF16)
        return _dot(ah, bh, mode) + (_dot(ah, bl, mode) + _dot(al, bh, mode))
    return lax.dot_general(a, b, dn, precision=prec, preferred_element_type=F32)


def _sig(x):
    return 1.0 / (1.0 + jnp.exp(-x))


def _pick(n, cap):
    if n <= cap:
        return n
    best = None
    for d in range(LANES, cap + 1, LANES):
        if n % d == 0:
            best = d
    assert best is not None, (n, cap)
    return best


def _params(n_grid):
    return pltpu.CompilerParams(dimension_semantics=("arbitrary",) * n_grid, vmem_limit_bytes=56 * 1024 * 1024)


def _rowwise(name, fn, rows, consts, outs, sums=(), tr=256):
    first = rows[0][0] if isinstance(rows[0], tuple) else rows[0]
    T = first.shape[-2]
    tr = _slot_tile(T, tr)
    nr, nc, no, ns = len(rows), len(consts), len(outs), len(sums)

    windows = [c[1:] if isinstance(c, tuple) else None for c in consts]
    consts = [c[0] if isinstance(c, tuple) else c for c in consts]

    def body(*refs):
        vals = [r[...] for r in refs[:nr]]
        for r, win in zip(refs[nr:nr + nc], windows):
            vals.append(r[...] if win is None else r[win[0]:win[0] + 1, win[1] * win[2]:(win[1] + 1) * win[2]])
        res = fn(*vals)
        if not isinstance(res, (tuple, list)):
            res = (res,)
        o_refs = refs[nr + nc:nr + nc + no]
        s_refs = refs[nr + nc + no:]
        for r, val in zip(o_refs, res[:no]):
            r[...] = val.astype(r.dtype)
        if ns:
            @pl.when(pl.program_id(0) == 0)
            def _():
                for r in s_refs:
                    r[...] = jnp.zeros_like(r)
            for r, val in zip(s_refs, res[no:]):
                r[...] += val

    in_specs, args = [], []
    for a in rows:
        if isinstance(a, tuple):
            arr, width, cb = a
            in_specs.append(pl.BlockSpec((tr, width), lambda i, cb=cb: (i, cb)))
            args.append(arr)
        elif a.ndim == 3:
            in_specs.append(pl.BlockSpec((a.shape[0], tr, a.shape[2]), lambda i: (0, i, 0)))
            args.append(a)
        else:
            in_specs.append(pl.BlockSpec((tr, a.shape[1]), lambda i: (i, 0)))
            args.append(a)
    for a in consts:
        in_specs.append(pl.BlockSpec(a.shape, lambda i, nd=a.ndim: (0,) * nd))
        args.append(a)
    out_specs = [pl.BlockSpec((tr, w), lambda i: (i, 0)) for w, _ in outs]
    out_specs += [pl.BlockSpec((1, w), lambda i: (0, 0)) for w in sums]
    out_shape = [jax.ShapeDtypeStruct((T, w), dt) for w, dt in outs]
    out_shape += [jax.ShapeDtypeStruct((1, w), F32) for w in sums]
    res = pl.pallas_call(body, name=name, grid=(T // tr,), in_specs=in_specs, out_specs=out_specs,
                         out_shape=out_shape, compiler_params=_params(1))(*args)
    return res


def _mm(name, a, b, mode, out_dtype=F32, tm=512, tn=1024):
    if mode == "tn":
        K, M = a.shape
    else:
        M, K = a.shape
    N = b.shape[0] if mode == "nt" else b.shape[1]
    tm, tn = _pick(M, tm), _pick(N, tn)

    def body(a_ref, b_ref, o_ref):
        o_ref[...] = _dot(a_ref[...].astype(BF16), b_ref[...].astype(BF16), mode).astype(o_ref.dtype)

    a_spec = pl.BlockSpec((K, tm), lambda i, j: (0, i)) if mode == "tn" else pl.BlockSpec((tm, K), lambda i, j: (i, 0))
    b_spec = pl.BlockSpec((tn, K), lambda i, j: (j, 0)) if mode == "nt" else pl.BlockSpec((K, tn), lambda i, j: (0, j))
    return pl.pallas_call(body, name=name, grid=(M // tm, N // tn), in_specs=[a_spec, b_spec],
                          out_specs=pl.BlockSpec((tm, tn), lambda i, j: (i, j)),
                          out_shape=jax.ShapeDtypeStruct((M, N), out_dtype), compiler_params=_params(2))(a, b)


def _rms(x, eps=EPS):
    return lax.rsqrt(jnp.mean(x * x, axis=-1, keepdims=True) + eps)


def _norm_mod_fwd(name, x, g, scale, shift):
    def fn(x, g, scale, shift):
        return x * _rms(x) * g * (1.0 + scale) + shift
    return _rowwise(name, fn, [x], [g, scale, shift], [(D, BF16)])[0]


def _norm_mod_bwd(name, dh, x, dx_res, g, scale):
    def fn(dh, x, dx_res, g, scale):
        r = _rms(x)
        xh = x * r
        dxh = dh * (g * (1.0 + scale))
        dx = r * (dxh - xh * jnp.mean(dxh * xh, axis=-1, keepdims=True))
        dhx = dh * xh
        return (dx_res + dx, jnp.sum(dh, axis=0, keepdims=True), jnp.sum(dhx * g, axis=0, keepdims=True),
                jnp.sum(dhx * (1.0 + scale), axis=0, keepdims=True))
    return _rowwise(name, fn, [dh, x, dx_res], [g, scale], [(D, F32)], sums=[D, D, D])


def _residual_fwd(name, x, y, gate):
    def fn(x, y, gate):
        return x + gate * y
    return _rowwise(name, fn, [x, y], [gate], [(D, F32)])[0]


def _residual_norm_fwd(name, x, y, gate, g, scale, shift):
    def fn(x, y, gate, g, scale, shift):
        x = x + gate * y
        return x, x * _rms(x) * g * (1.0 + scale) + shift
    return _rowwise(name, fn, [x, y], [gate, g, scale, shift], [(D, F32), (D, BF16)])


def _norm_residual_bwd(name, dh, x, dx_res, g, scale, y, gate):
    def fn(dh, x, dx_res, y, g, scale, gate):
        r = _rms(x)
        xh = x * r
        dxh = dh * (g * (1.0 + scale))
        dx = dx_res + r * (dxh - xh * jnp.mean(dxh * xh, axis=-1, keepdims=True))
        dhx = dh * xh
        return (dx, dx * gate, jnp.sum(dh, axis=0, keepdims=True), jnp.sum(dhx * g, axis=0, keepdims=True),
                jnp.sum(dhx * (1.0 + scale), axis=0, keepdims=True), jnp.sum(dx * y, axis=0, keepdims=True))
    return _rowwise(name, fn, [dh, x, dx_res, y], [g, scale, gate], [(D, F32), (D, BF16)], sums=[D, D, D, D])


def _residual_bwd(name, dx, y, gate):
    def fn(dx, y, gate):
        return dx * gate, jnp.sum(dx * y, axis=0, keepdims=True)
    return _rowwise(name, fn, [dx, y], [gate], [(D, BF16)], sums=[D])


def _loss_head(x, target, g):
    def fn(x, t, g):
        r = _rms(x)
        xh = x * r
        err = xh * g - t
        loss = 0.5 * jnp.sum(jnp.mean(err * err, axis=-1, keepdims=True), axis=0, keepdims=True)
        dy = err * (1.0 / D)
        dxh = dy * g
        dx = r * (dxh - xh * jnp.mean(dxh * xh, axis=-1, keepdims=True))
        return dx, jnp.broadcast_to(loss, (1, LANES)), jnp.sum(dy * xh, axis=0, keepdims=True)
    return _rowwise("loss_head", fn, [x, target], [g], [(D, F32)], sums=[LANES, D])


def _ffn_up(name, h, wg, wu, layer, tm=1024):
    T, n = h.shape[0], wg.shape[1]
    tm = min(tm, T)

    def body(h_ref, wg_ref, wu_ref, a_ref, b_ref, s_ref):
        h = h_ref[...]
        a = _dot(h, wg_ref[0], "nt")
        b = _dot(h, wu_ref[0], "nt")
        a_ref[0] = a.astype(a_ref.dtype)
        b_ref[0] = b.astype(b_ref.dtype)
        s_ref[0] = (a * _sig(a) * b).astype(s_ref.dtype)

    wspec = pl.BlockSpec((1, n, D), lambda ch, i: (ch, layer, 0))
    ospec = pl.BlockSpec((1, tm, n), lambda ch, i: (ch, i, 0))
    return pl.pallas_call(
        body, name=name, grid=(4, T // tm), in_specs=[pl.BlockSpec((tm, D), lambda ch, i: (i, 0)), wspec, wspec],
        out_specs=[ospec, ospec, ospec],
        out_shape=[jax.ShapeDtypeStruct((4, T, n), BF16)] * 3, compiler_params=_params(2))(h, wg, wu)


def _ffn_down(name, s, wd, layer, tm=1024):
    _, T, n = s.shape
    tm = min(tm, T)

    def body(s_ref, w_ref, y_ref):
        @pl.when(pl.program_id(1) == 0)
        def _():
            y_ref[...] = jnp.zeros_like(y_ref)
        y_ref[...] += _dot(s_ref[0], w_ref[0], "nn")

    return pl.pallas_call(
        body, name=name, grid=(T // tm, 4),
        in_specs=[pl.BlockSpec((1, tm, n), lambda i, ch: (ch, i, 0)), pl.BlockSpec((1, n, D), lambda i, ch: (ch, layer, 0))],
        out_specs=pl.BlockSpec((tm, D), lambda i, ch: (i, 0)), out_shape=jax.ShapeDtypeStruct((T, D), F32),
        compiler_params=_params(2))(s, wd)


def _ffn_down_bwd(name, dy, wd, a, b, layer, tm=1024):
    _, T, n = a.shape
    tm = min(tm, T)

    def body(dy_ref, w_ref, a_ref, b_ref, da_ref, db_ref):
        ds = _dot(dy_ref[...], w_ref[0], "nt")
        a, b = a_ref[0].astype(F32), b_ref[0].astype(F32)
        sg = _sig(a)
        da_ref[0] = (ds * b * (sg * (1.0 + a * (1.0 - sg)))).astype(da_ref.dtype)
        db_ref[0] = (ds * (a * sg)).astype(db_ref.dtype)

    bspec = pl.BlockSpec((1, tm, n), lambda ch, i: (ch, i, 0))
    return pl.pallas_call(
        body, name=name, grid=(4, T // tm),
        in_specs=[pl.BlockSpec((tm, D), lambda ch, i: (i, 0)), pl.BlockSpec((1, n, D), lambda ch, i: (ch, layer, 0)), bspec, bspec],
        out_specs=[bspec, bspec], out_shape=[jax.ShapeDtypeStruct((4, T, n), BF16)] * 2,
        compiler_params=_params(2))(dy, wd, a, b)


def _ffn_down_dw(name, s, dy):
    _, T, n = s.shape

    def body(s_ref, dy_ref, o_ref):
        o_ref[0] = _dot(s_ref[0], dy_ref[...], "tn").astype(o_ref.dtype)

    return pl.pallas_call(
        body, name=name, grid=(4,),
        in_specs=[pl.BlockSpec((1, T, n), lambda ch: (ch, 0, 0)), pl.BlockSpec((T, D), lambda ch: (0, 0))],
        out_specs=pl.BlockSpec((1, n, D), lambda ch: (ch, 0, 0)), out_shape=jax.ShapeDtypeStruct((4, n, D), BF16),
        compiler_params=_params(1))(s, dy)


def _ffn_up_dw(name, h, da, db, tm=512):
    _, T, n = da.shape

    def body(h_ref, da_ref, db_ref, dg_ref, du_ref):
        h = h_ref[...]
        dg_ref[0] = _dot(da_ref[0], h, "tn").astype(dg_ref.dtype)
        du_ref[0] = _dot(db_ref[0], h, "tn").astype(du_ref.dtype)

    dspec = pl.BlockSpec((1, T, n), lambda ch, j: (ch, 0, 0))
    ospec = pl.BlockSpec((1, n, tm), lambda ch, j: (ch, 0, j))
    return pl.pallas_call(
        body, name=name, grid=(4, D // tm), in_specs=[pl.BlockSpec((T, tm), lambda ch, j: (0, j)), dspec, dspec],
        out_specs=[ospec, ospec], out_shape=[jax.ShapeDtypeStruct((4, n, D), BF16)] * 2,
        compiler_params=_params(2))(h, da, db)


def _ffn_up_dx(name, da, db, wg, wu, layer, tm=1024):
    _, T, n = da.shape
    tm = min(tm, T)

    def body(da_ref, db_ref, wg_ref, wu_ref, o_ref):
        @pl.when(pl.program_id(1) == 0)
        def _():
            o_ref[...] = jnp.zeros_like(o_ref)
        o_ref[...] += _dot(da_ref[0], wg_ref[0], "nn") + _dot(db_ref[0], wu_ref[0], "nn")

    dspec = pl.BlockSpec((1, tm, n), lambda i, ch: (ch, i, 0))
    wspec = pl.BlockSpec((1, n, D), lambda i, ch: (ch, layer, 0))
    return pl.pallas_call(
        body, name=name, grid=(T // tm, 4), in_specs=[dspec, dspec, wspec, wspec],
        out_specs=pl.BlockSpec((tm, D), lambda i, ch: (i, 0)), out_shape=jax.ShapeDtypeStruct((T, D), F32),
        compiler_params=_params(2))(da, db, wg, wu)


def _shift_down(x, k):
    if k == 0:
        return x
    rows = lax.broadcasted_iota(jnp.int32, x.shape, 0)
    return jnp.where(rows >= k, pltpu.roll(x, k, 0), 0.0)


def _shift_up(x, k):
    if k == 0:
        return x
    T = x.shape[0]
    rows = lax.broadcasted_iota(jnp.int32, x.shape, 0)
    return jnp.where(rows < T - k, pltpu.roll(x, T - k, 0), 0.0)


def _conv_silu(x, w):
    c = w[0:1, :] * _shift_down(x, 3) + w[1:2, :] * _shift_down(x, 2) + w[2:3, :] * _shift_down(x, 1) + w[3:4, :] * x
    sg = _sig(c)
    return c, sg, c * sg


def _gdn_conv_fwd(name, proj, cw):
    T = proj.shape[0]

    def body(x_ref, w_ref, o_ref):
        j = pl.program_id(0)
        _, _, y = _conv_silu(x_ref[...], w_ref[...])
        r = lax.rsqrt(jnp.sum(y * y, axis=1, keepdims=True) + EPS)
        mult = jnp.where(j < NH, HD ** -0.5, 1.0)
        o_ref[...] = jnp.where(j < 2 * NH, y * (r * mult), y)

    return pl.pallas_call(body, name=name, grid=(3 * NH,),
                          in_specs=[pl.BlockSpec((T, HD), lambda j: (0, j)), pl.BlockSpec((4, HD), lambda j: (0, j))],
                          out_specs=pl.BlockSpec((T, HD), lambda j: (0, j)),
                          out_shape=jax.ShapeDtypeStruct((T, GDN_QKV), F32), compiler_params=_params(1))(proj, cw)


def _gdn_conv_bwd(name, proj, cw, dz):
    T = proj.shape[0]

    def body(x_ref, w_ref, dz_ref, dx_ref, dw_ref):
        j = pl.program_id(0)
        x, w, dz = x_ref[...], w_ref[...], dz_ref[...]
        c, sg, y = _conv_silu(x, w)
        r = lax.rsqrt(jnp.sum(y * y, axis=1, keepdims=True) + EPS)
        mult = jnp.where(j < NH, HD ** -0.5, 1.0)
        dyn = mult * (r * dz - (r * r * r) * y * jnp.sum(dz * y, axis=1, keepdims=True))
        dy = jnp.where(j < 2 * NH, dyn, dz)
        dc = dy * (sg * (1.0 + c * (1.0 - sg)))
        dx = w[0:1, :] * _shift_up(dc, 3) + w[1:2, :] * _shift_up(dc, 2) + w[2:3, :] * _shift_up(dc, 1) + w[3:4, :] * dc
        dx_ref[...] = dx.astype(dx_ref.dtype)
        for k in range(4):
            dw_ref[pl.ds(k, 1), :] = jnp.sum(dc * _shift_down(x, 3 - k), axis=0, keepdims=True)

    return pl.pallas_call(body, name=name, grid=(3 * NH,),
                          in_specs=[pl.BlockSpec((T, HD), lambda j: (0, j)), pl.BlockSpec((4, HD), lambda j: (0, j)),
                                    pl.BlockSpec((T, HD), lambda j: (0, j))],
                          out_specs=[pl.BlockSpec((T, HD), lambda j: (0, j)), pl.BlockSpec((4, HD), lambda j: (0, j))],
                          out_shape=[jax.ShapeDtypeStruct((T, GDN_QKV), BF16), jax.ShapeDtypeStruct((4, GDN_QKV), F32)],
                          compiler_params=_params(1))(proj, cw, dz)


def _softplus(z):
    return jnp.maximum(z, 0.0) + jnp.log(1.0 + jnp.exp(-jnp.abs(z)))


_AB_CB = GDN_INK // (2 * HD) - 1


def _gdn_gates_fwd(name, proj, alog, dtb):
    def fn(ab, alog, dtb):
        a, b = ab[:, :HD], ab[:, HD:]
        return -jnp.exp(alog) * _softplus(a + dtb), _sig(b)
    return _rowwise(name, fn, [(proj, 2 * HD, _AB_CB)], [alog, dtb], [(HD, F32), (HD, F32)])


def _gdn_gates_bwd(name, proj, dg_h, db_h, alog, dtb):
    def fn(ab, dg_h, db_h, alog, dtb):
        lane = lax.broadcasted_iota(jnp.int32, (1, HD), 1)
        dg = jnp.zeros(dg_h.shape[1:], F32)
        dbeta = jnp.zeros(dg_h.shape[1:], F32)
        for h in range(NH):
            oh = (lane == h).astype(F32)
            dg = dg + dg_h[h] * oh
            dbeta = dbeta + db_h[h] * oh
        a, b = ab[:, :HD], ab[:, HD:]
        z = a + dtb
        ea = jnp.exp(alog)
        beta = _sig(b)
        da = dg * (-ea) * _sig(z)
        db = dbeta * beta * (1.0 - beta)
        return (jnp.concatenate([da, db], axis=1), jnp.sum(dg * (-ea * _softplus(z)), axis=0, keepdims=True),
                jnp.sum(da, axis=0, keepdims=True))
    return _rowwise(name, fn, [(proj, 2 * HD, _AB_CB), dg_h, db_h], [alog, dtb], [(2 * HD, BF16)], sums=[HD, HD])


def _interleave(gens):
    gens = list(gens)
    results = [None] * len(gens)
    active = list(range(len(gens)))
    while active:
        for i in list(active):
            try:
                next(gens[i])
            except StopIteration as stop:
                results[i] = stop.value
                active.remove(i)
    return results


def _chunk_common(q, k, v, gblk, bblk, h, prec):
    C = CHUNK
    lane = lax.broadcasted_iota(jnp.int32, (1, HD), 1)
    oh = (lane == h).astype(F32)
    g_col = jnp.sum(gblk * oh, axis=1, keepdims=True)
    beta = jnp.sum(bblk * oh, axis=1, keepdims=True)
    ri = lax.broadcasted_iota(jnp.int32, (C, C), 0)
    ci = lax.broadcasted_iota(jnp.int32, (C, C), 1)
    incl = ri >= ci
    strict = ri > ci
    eye = (ri == ci).astype(F32)
    gcb = _dot(incl.astype(F32), jnp.broadcast_to(g_col, (C, HD)), "nn", HI)
    yield
    gc = gcb[:, :C]
    gc_row = _dot(jnp.ones((C, C), F32), eye * gc, "nn", HI)
    yield
    decay = jnp.where(incl, jnp.exp(jnp.where(incl, gc - gc_row, 0.0)), 0.0)
    rows = lax.broadcasted_iota(jnp.int32, (C, HD), 0)
    gclb = jnp.sum(jnp.where(rows == C - 1, gcb, 0.0), axis=0, keepdims=True)
    eg = jnp.exp(gcb)
    egl = jnp.exp(gclb - gcb)
    gl = jnp.exp(gclb)
    kb = k * beta
    m1 = _dot(kb, k, "nt", prec)
    qk = _dot(q, k, "nt", prec)
    yield
    L = jnp.where(strict, m1 * decay, 0.0)
    nl = -L
    tinv = eye + nl
    p = nl
    for _ in range(5):
        p = _dot(p, p, "nn", H3)
        yield
        tinv = tinv + _dot(tinv, p, "nn", H3)
    vb = v * beta
    kbg = kb * eg
    yield
    u = _dot(tinv, vb, "nn", prec)
    w = _dot(tinv, kbg, "nn", prec)
    yield
    attn = jnp.where(incl, qk * decay, 0.0)
    return dict(beta=beta, incl=incl, strict=strict, decay=decay, eg=eg, egl=egl, gl=gl, kb=kb, m1=m1, tinv=tinv,
                kbg=kbg, u=u, w=w, qk=qk, attn=attn, q_dec=q * eg, k_dec=k * egl, rows=rows, oh=oh)


def _gdn_chunk_fwd(name, qkv, g, beta):
    T = qkv.shape[0]
    N = T // CHUNK

    hb = _GDN_HB
    w = hb * HD

    def body(q_ref, k_ref, v_ref, g_ref, b_ref, o_ref, st_ref, S):
        hg, n = pl.program_id(0), pl.program_id(1)

        @pl.when(n == 0)
        def _():
            S[...] = jnp.zeros_like(S)

        gblk, bblk = g_ref[...], b_ref[...]

        def one_head(i, q, k, v, s):
            c = yield from _chunk_common(q, k, v, gblk, bblk, hg * hb + i, HF)
            v_new = c["u"] - _dot(c["w"], s, "nn", HF)
            qs = _dot(c["q_dec"], s, "nn", HF)
            yield
            o = qs + _dot(c["attn"], v_new, "nn", HF)
            return o, s * c["gl"] + _dot(c["k_dec"], v_new, "tn", HF)

        sls = [slice(i * HD, (i + 1) * HD) for i in range(hb)]
        states = [S[i] for i in range(hb)]
        res = _interleave(one_head(i, q_ref[:, sls[i]], k_ref[:, sls[i]], v_ref[:, sls[i]], states[i]) for i in range(hb))
        for i, (o, s_new) in enumerate(res):
            st_ref[i, 0] = states[i]
            o_ref[:, sls[i]] = o
            S[i] = s_new

    blk = lambda off: pl.BlockSpec((CHUNK, w), lambda h, n, off=off: (n, off + h))
    gspec = pl.BlockSpec((CHUNK, HD), lambda h, n: (n, 0))
    return pl.pallas_call(
        body, name=name, grid=(NH // hb, N), in_specs=[blk(0), blk(NH // hb), blk(2 * NH // hb), gspec, gspec],
        out_specs=[pl.BlockSpec((CHUNK, w), lambda h, n: (n, h)), pl.BlockSpec((hb, 1, HD, HD), lambda h, n: (h, n, 0, 0))],
        out_shape=[jax.ShapeDtypeStruct((T, NH * HD), F32), jax.ShapeDtypeStruct((NH, N, HD, HD), F32)],
        scratch_shapes=[pltpu.VMEM((hb, HD, HD), F32)], compiler_params=_params(2))(qkv, qkv, qkv, g, beta)


def _gdn_chunk_bwd(name, qkv, g, beta, states, do):
    T = qkv.shape[0]
    N = T // CHUNK
    C = CHUNK

    hb = _GDN_HB
    w = hb * HD
    assert hb == NH

    def body(q_ref, k_ref, v_ref, g_ref, b_ref, st_ref, do_ref, dqkv_ref, dg_ref, db_ref, dS):
        hg, n = pl.program_id(0), pl.program_id(1)

        @pl.when(n == 0)
        def _():
            dS[...] = jnp.zeros_like(dS)

        gblk, bblk = g_ref[...], b_ref[...]
        sls = [slice(i * HD, (i + 1) * HD) for i in range(hb)]
        res = _interleave(one_head(hg * hb + i, gblk, bblk, q_ref[:, sls[i]], k_ref[:, sls[i]], v_ref[:, sls[i]],
                                   st_ref[i, 0], do_ref[:, sls[i]], dS[i]) for i in range(hb))
        for i, (dq, dk, dv, dg, db, ds_new) in enumerate(res):
            dqkv_ref[:, sls[i]] = dq
            dqkv_ref[:, slice(w + i * HD, w + (i + 1) * HD)] = dk
            dqkv_ref[:, slice(2 * w + i * HD, 2 * w + (i + 1) * HD)] = dv
            dg_ref[i] = dg
            db_ref[i] = db
            dS[i] = ds_new

    def one_head(h, gblk, bblk, q, k, v, s, do, ds):
        c = yield from _chunk_common(q, k, v, gblk, bblk, h, HF)
        eg, egl, gl, beta, decay, tinv = c["eg"], c["egl"], c["gl"], c["beta"], c["decay"], c["tinv"]
        v_new = c["u"] - _dot(c["w"], s, "nn", HF)
        dq_dec = _dot(do, s, "nt", HF)
        yield
        dv_new = _dot(c["attn"], do, "tn", HF) + _dot(c["k_dec"], ds, "nn", HF)
        dk_dec = _dot(v_new, ds, "nt", HF)
        dgl = jnp.sum(jnp.sum(s * ds, axis=1, keepdims=True), axis=0, keepdims=True)
        yield
        ds_new = ds * gl + _dot(c["q_dec"], do, "tn", HF) - _dot(c["w"], dv_new, "tn", HF)
        dattn = jnp.where(c["incl"], _dot(do, v_new, "nt", HF), 0.0)
        dw = -_dot(dv_new, s, "nt", HF)
        yield
        dvb = _dot(tinv, dv_new, "tn", HS)
        dkbg = _dot(tinv, dw, "tn", HS)
        yield
        dA = -(_dot(dvb, c["u"], "nt", HS) + _dot(dkbg, c["w"], "nt", HS))
        yield
        dL = jnp.where(c["strict"], dA, 0.0)
        dm1 = dL * decay
        dqk = dattn * decay
        xdec = (dL * c["m1"] + dattn * c["qk"]) * decay
        dkb = _dot(dm1, k, "nn", HS) + dkbg * eg
        dk = _dot(dm1, c["kb"], "tn", HF) + _dot(dqk, q, "tn", HF) + dk_dec * egl + dkb * beta
        dq = _dot(dqk, k, "nn", HF) + dq_dec * eg
        yield
        dkd_kd = jnp.sum(dk_dec * c["k_dec"], axis=1, keepdims=True)
        dgc = (jnp.sum(xdec, axis=1, keepdims=True) - _dot(xdec, jnp.ones((C, HD), F32), "tn", HS)
               + jnp.sum(dq_dec * c["q_dec"], axis=1, keepdims=True) - dkd_kd
               + jnp.sum(dkbg * c["kbg"], axis=1, keepdims=True))
        dgcl = jnp.sum(dkd_kd, axis=0, keepdims=True) + dgl * gl
        dgc = dgc + jnp.where(c["rows"] == C - 1, dgcl, 0.0)
        ri = lax.broadcasted_iota(jnp.int32, (C, C), 0)
        ci = lax.broadcasted_iota(jnp.int32, (C, C), 1)
        dg = _dot((ci >= ri).astype(F32), dgc, "nn", HI)
        db = jnp.broadcast_to(jnp.sum(dkb * k, axis=1, keepdims=True) + jnp.sum(dvb * v, axis=1, keepdims=True), (C, HD))
        return dq, dk, dvb * beta, dg, db, ds_new

    blk = lambda off: pl.BlockSpec((C, w), lambda h, n, off=off: (N - 1 - n, off + h))
    gspec = pl.BlockSpec((C, HD), lambda h, n: (N - 1 - n, 0))
    ospec = pl.BlockSpec((C, w), lambda h, n: (N - 1 - n, h))
    hspec = pl.BlockSpec((hb, C, HD), lambda h, n: (h, N - 1 - n, 0))
    return pl.pallas_call(
        body, name=name, grid=(NH // hb, N),
        in_specs=[blk(0), blk(NH // hb), blk(2 * NH // hb), gspec, gspec,
                  pl.BlockSpec((hb, 1, HD, HD), lambda h, n: (h, N - 1 - n, 0, 0)), ospec],
        out_specs=[pl.BlockSpec((C, 3 * w), lambda h, n: (N - 1 - n, 0)), hspec, hspec],
        out_shape=[jax.ShapeDtypeStruct((T, 3 * NH * HD), F32)] + [jax.ShapeDtypeStruct((NH, T, HD), F32)] * 2,
        scratch_shapes=[pltpu.VMEM((hb, HD, HD), F32)], compiler_params=_params(2))(qkv, qkv, qkv, g, beta, states, do)


_GATE_CB = GDN_QKV // (NH * HD)


def _gdn_gated_norm_fwd(name, o, proj, ng):
    def fn(o, gate, ng):
        outs = []
        for h in range(NH):
            sl = slice(h * HD, (h + 1) * HD)
            oh, gh = o[:, sl], gate[:, sl]
            outs.append(oh * _rms(oh) * ng * (gh * _sig(gh)))
        return jnp.concatenate(outs, axis=1)
    return _rowwise(name, fn, [o, (proj, NH * HD, _GATE_CB)], [ng], [(NH * HD, BF16)])[0]


def _gdn_gated_norm_bwd(name, don, o, proj, ng):
    def fn(don, o, gate, ng):
        dos, dgs = [], []
        dng = jnp.zeros((1, HD), F32)
        for h in range(NH):
            sl = slice(h * HD, (h + 1) * HD)
            oh, gh, dh = o[:, sl], gate[:, sl], don[:, sl]
            r = _rms(oh)
            xh = oh * r
            sg = _sig(gh)
            dn = dh * (gh * sg)
            dgs.append(dh * (xh * ng) * (sg * (1.0 + gh * (1.0 - sg))))
            dng = dng + jnp.sum(dn * xh, axis=0, keepdims=True)
            dxh = dn * ng
            dos.append(r * (dxh - xh * jnp.mean(dxh * xh, axis=-1, keepdims=True)))
        return jnp.concatenate(dos, axis=1), jnp.concatenate(dgs, axis=1), dng
    return _rowwise(name, fn, [don, o, (proj, NH * HD, _GATE_CB)], [ng], [(NH * HD, F32), (NH * HD, BF16)], sums=[HD])


def _rot(x):
    lane = lax.broadcasted_iota(jnp.int32, x.shape, 1)
    return jnp.where(lane < ROPE // 2, -pltpu.roll(x, HD - ROPE // 2, 1), pltpu.roll(x, ROPE // 2, 1))


def _rot_t(x):
    lane = lax.broadcasted_iota(jnp.int32, x.shape, 1)
    return jnp.where(lane < ROPE // 2, pltpu.roll(x, HD - ROPE // 2, 1), -pltpu.roll(x, ROPE // 2, 1))


def _rope_tables(pos_col):
    lane = jnp.arange(HD)
    inv_freq = ROPE_THETA ** (-(2.0 * (lane % (ROPE // 2)).astype(F32)) / ROPE)
    inv_freq = jnp.where(lane < ROPE, inv_freq, 0.0).astype(F32)[None, :]
    valid = (lane < ROPE).astype(F32)[None, :]

    def fn(pos, inv_freq, valid):
        ang = pos.astype(F32) * inv_freq
        return jnp.cos(ang) * valid, jnp.sin(ang) * valid
    return _rowwise("rope_tables", fn, [pos_col], [inv_freq, valid], [(HD, F32), (HD, F32)])


def _mla_pre_fwd(name, proj, cos, sin, qg, kvg):
    def fn(p, cos, sin, qg, kvg):
        cq, ckv, kr = p[:, :Q_RANK], p[:, Q_RANK:Q_RANK + KV_RANK], p[:, Q_RANK + KV_RANK:]
        return cq * _rms(cq) * qg, ckv * _rms(ckv) * kvg, kr * cos + _rot(kr) * sin
    return _rowwise(name, fn, [proj, cos, sin], [qg, kvg], [(Q_RANK, BF16), (KV_RANK, BF16), (HD, BF16)])


def _rms_bwd(dy, x, g):
    r = _rms(x)
    xh = x * r
    dxh = dy * g
    return r * (dxh - xh * jnp.mean(dxh * xh, axis=-1, keepdims=True)), jnp.sum(dy * xh, axis=0, keepdims=True)


def _mla_pre_bwd(name, proj, dcqn, dckvn, dkr, cos, sin, qg, kvg):
    def fn(p, dcqn, dckvn, dkr, cos, sin, qg, kvg):
        cq, ckv = p[:, :Q_RANK], p[:, Q_RANK:Q_RANK + KV_RANK]
        dcq, dqg = _rms_bwd(dcqn, cq, qg)
        dckv, dkvg = _rms_bwd(dckvn, ckv, kvg)
        dkr_pre = dkr * cos + _rot_t(dkr * sin)
        return jnp.concatenate([dcq, dckv, dkr_pre], axis=1), dqg, dkvg
    return _rowwise(name, fn, [proj, dcqn, dckvn, dkr, cos, sin], [qg, kvg], [(MLA_INK, BF16)], sums=[Q_RANK, KV_RANK])


def _mla_q_fwd(name, q, cos, sin):
    def fn(qn, qr, cos, sin):
        outs = []
        for h in range(NH):
            x = qr[:, h * HD:(h + 1) * HD]
            outs.append(x * cos + _rot(x) * sin)
        return qn, jnp.concatenate(outs, axis=1)
    return _rowwise(name, fn, [(q, NH * HD, 0), (q, NH * HD, 1), cos, sin], [], [(NH * HD, BF16), (NH * HD, BF16)])


def _mla_q_bwd(name, dqn, dqr, cos, sin):
    def fn(dqn, dqr, cos, sin):
        outs = [dqn]
        for h in range(NH):
            z = dqr[:, h * HD:(h + 1) * HD]
            outs.append(z * cos + _rot_t(z * sin))
        return jnp.concatenate(outs, axis=1)
    return _rowwise(name, fn, [dqn, dqr, cos, sin], [], [(2 * NH * HD, BF16)])[0]


def _att_probs(qn, qr, kn, kr, row0):
    s = (_dot(qn, kn, "nt") + _dot(qr, kr, "nt")) * ATT_SCALE
    qpos = row0 + lax.broadcasted_iota(jnp.int32, s.shape, 0)
    kpos = lax.broadcasted_iota(jnp.int32, s.shape, 1)
    s = jnp.where(kpos <= qpos, s, -1e30)
    p = jnp.exp(s - jnp.max(s, axis=1, keepdims=True))
    return p * (1.0 / jnp.sum(p, axis=1, keepdims=True))


def _mla_attn_fwd(name, qn, qr, kv, kr, tq=256):
    T = qn.shape[0]
    tq = min(tq, T)

    def body(qn_ref, qr_ref, kn_ref, v_ref, kr_ref, o_ref):
        i = pl.program_id(1)
        for blk in range(T // tq):
            @pl.when(i == blk)
            def _(blk=blk):
                keys = pl.ds(0, (blk + 1) * tq)
                p = _att_probs(qn_ref[...], qr_ref[...], kn_ref[keys, :], kr_ref[keys, :], blk * tq)
                o_ref[...] = _dot(p.astype(BF16), v_ref[keys, :], "nn").astype(o_ref.dtype)

    qspec = pl.BlockSpec((tq, HD), lambda h, i: (i, h))
    return pl.pallas_call(
        body, name=name, grid=(NH, T // tq),
        in_specs=[qspec, qspec, pl.BlockSpec((T, HD), lambda h, i: (0, h)), pl.BlockSpec((T, HD), lambda h, i: (0, NH + h)),
                  pl.BlockSpec((T, HD), lambda h, i: (0, 0))],
        out_specs=qspec, out_shape=jax.ShapeDtypeStruct((T, NH * HD), BF16), compiler_params=_params(2))(qn, qr, kv, kv, kr)


def _mla_attn_bwd(name, qn, qr, kv, kr, do, tq=256):
    T = qn.shape[0]
    tq = min(tq, T)

    def body(qn_ref, qr_ref, kn_ref, v_ref, kr_ref, do_ref, dqn_ref, dqr_ref, dkn_ref, dv_ref, dkr_ref):
        h, i = pl.program_id(0), pl.program_id(1)

        @pl.when(i == 0)
        def _():
            dkn_ref[...] = jnp.zeros_like(dkn_ref)
            dv_ref[...] = jnp.zeros_like(dv_ref)

        @pl.when((i == 0) & (h == 0))
        def _():
            dkr_ref[...] = jnp.zeros_like(dkr_ref)

        for blk in range(T // tq):
            @pl.when(i == blk)
            def _(blk=blk):
                keys = pl.ds(0, (blk + 1) * tq)
                qn, qr, do = qn_ref[...], qr_ref[...], do_ref[...]
                kn, kr, v = kn_ref[keys, :], kr_ref[keys, :], v_ref[keys, :]
                p = _att_probs(qn, qr, kn, kr, blk * tq)
                dp = _dot(do, v, "nt")
                ds = (p * (dp - jnp.sum(p * dp, axis=1, keepdims=True)) * ATT_SCALE).astype(BF16)
                dqn_ref[...] = _dot(ds, kn, "nn")
                dqr_ref[...] = _dot(ds, kr, "nn")
                dkn_ref[keys, :] += _dot(ds, qn, "tn")
                dkr_ref[keys, :] += _dot(ds, qr, "tn")
                dv_ref[keys, :] += _dot(p.astype(BF16), do, "tn")

    qspec = pl.BlockSpec((tq, HD), lambda h, i: (i, h))
    kspec = pl.BlockSpec((T, HD), lambda h, i: (0, h))
    return pl.pallas_call(
        body, name=name, grid=(NH, T // tq),
        in_specs=[qspec, qspec, kspec, pl.BlockSpec((T, HD), lambda h, i: (0, NH + h)),
                  pl.BlockSpec((T, HD), lambda h, i: (0, 0)), qspec],
        out_specs=[qspec, qspec, kspec, kspec, pl.BlockSpec((T, HD), lambda h, i: (0, 0))],
        out_shape=[jax.ShapeDtypeStruct((T, NH * HD), F32)] * 4 + [jax.ShapeDtypeStruct((T, HD), F32)],
        compiler_params=_params(2))(qn, qr, kv, kv, kr, do)


def _mod_rows(mod, layer):
    return [(mod, layer, i, D) for i in range(N_MOD)]


def _local_step(x, target, pos_col, mod, weights_of, P, on_grads):
    cos, sin = _rope_tables(pos_col)
    saved = []
    sh_m, sc_m = _mod_rows(mod, 0)[:2]
    h = _norm_mod_fwd("norm_mix0", x, (P["norm_mix_g"], 0, 0, D), sc_m, sh_m)
    for l in range(DEPTH):
        j = l // 2
        sh_m, sc_m, ga_m, sh_f, sc_f, ga_f = _mod_rows(mod, l)
        s = dict(x0=x)
        W = weights_of(l, h)
        s.update(h=h, W=W)
        if l % 2 == 0:
            proj = _mm(f"gdn_in{j}", h, W["gdn_in"], "nn", tn=GDN_INK // 2)
            qkv = _gdn_conv_fwd(f"gdn_conv{j}", proj, P["gdn_cw"][j])
            g, beta = _gdn_gates_fwd(f"gdn_gates{j}", proj, P["gdn_alog"][j], P["gdn_dtb"][j])
            o, states = _gdn_chunk_fwd(f"gdn_chunk{j}", qkv, g, beta)
            on = _gdn_gated_norm_fwd(f"gdn_gnorm{j}", o, proj, P["gdn_ng"][j])
            y = _mm(f"gdn_out{j}", on, W["gdn_out"], "nn")
            s.update(proj=proj, qkv=qkv, g=g, beta=beta, o=o, states=states, on=on)
        else:
            proj = _mm(f"mla_in{j}", h, W["mla_in"], "nn")
            cqn, ckvn, kr = _mla_pre_fwd(f"mla_pre{j}", proj, cos, sin, P["mla_qg"][j], P["mla_kvg"][j])
            q = _mm(f"mla_uq{j}", cqn, W["mla_uq"], "nn")
            kv = _mm(f"mla_ukv{j}", ckvn, W["mla_ukv"], "nn", out_dtype=BF16)
            qn, qr = _mla_q_fwd(f"mla_q{j}", q, cos, sin)
            o = _mla_attn_fwd(f"mla_attn{j}", qn, qr, kv, kr)
            y = _mm(f"mla_out{j}", o, W["mla_out"], "nn")
            s.update(proj=proj, cqn=cqn, ckvn=ckvn, kr=kr, kv=kv, qn=qn, qr=qr, o=o)
        s["y"] = y
        x, h2 = _residual_norm_fwd(f"res_mix{l}", x, y, ga_m, (P["norm_ffn_g"], l, 0, D), sc_f, sh_f)
        s["x1"] = x
        fa, fb, sw = _ffn_up(f"ffn_up{l}", h2, W["ffn_g"], W["ffn_u"], 0)
        yf = _ffn_down(f"ffn_down{l}", sw, W["ffn_d"], 0)
        if l + 1 < DEPTH:
            sh_n, sc_n = _mod_rows(mod, l + 1)[:2]
            x, h = _residual_norm_fwd(f"res_ffn{l}", x, yf, ga_f, (P["norm_mix_g"], l + 1, 0, D), sc_n, sh_n)
        else:
            x = _residual_fwd(f"res_ffn{l}", x, yf, ga_f)
        s.update(h2=h2, fa=fa, fb=fb, sw=sw, yf=yf)
        saved.append(s)

    dx, loss, d_final = _loss_head(x, target, P["final_g"])
    gP = dict(loss=loss, final_g=d_final, norm_mix_g=[None] * DEPTH, norm_ffn_g=[None] * DEPTH,
              gdn_cw=[None] * 2, gdn_alog=[None] * 2, gdn_dtb=[None] * 2, gdn_ng=[None] * 2,
              mla_qg=[None] * 2, mla_kvg=[None] * 2)
    dmod = [None] * DEPTH
    dyf, d_ga_f = _residual_bwd(f"res_ffn_b{DEPTH - 1}", dx, saved[-1]["yf"], _mod_rows(mod, DEPTH - 1)[5])
    for l in reversed(range(DEPTH)):
        j = l // 2
        s = saved[l]
        W = s["W"]
        sh_m, sc_m, ga_m, sh_f, sc_f, ga_f = _mod_rows(mod, l)
        da, db = _ffn_down_bwd(f"ffn_down_dx{l}", dyf, W["ffn_d"], s["fa"], s["fb"], 0)
        g_down = _ffn_down_dw(f"ffn_down_dw{l}", s["sw"], dyf)
        g_gate, g_up = _ffn_up_dw(f"ffn_up_dw{l}", s["h2"], da, db)
        on_grads(l, "ffn", dict(ffn_w_gate=g_gate, ffn_w_up=g_up, ffn_w_down=g_down))
        dh2 = _ffn_up_dx(f"ffn_up_dx{l}", da, db, W["ffn_g"], W["ffn_u"], 0)
        dx, dy, d_sh_f, d_sc_f, gP["norm_ffn_g"][l], d_ga_m = _norm_residual_bwd(
            f"norm_ffn_b{l}", dh2, s["x1"], dx, (P["norm_ffn_g"], l, 0, D), sc_f, s["y"], ga_m)
        if l % 2 == 0:
            don = _mm(f"gdn_out_dx{j}", dy, W["gdn_out"], "nt")
            g_out = _mm(f"gdn_out_dw{j}", s["on"], dy, "tn", out_dtype=BF16)
            do, dgate, gP["gdn_ng"][j] = _gdn_gated_norm_bwd(f"gdn_gnorm_b{j}", don, s["o"], s["proj"], P["gdn_ng"][j])
            dqkv, dg_h, db_h = _gdn_chunk_bwd(f"gdn_chunk_b{j}", s["qkv"], s["g"], s["beta"], s["states"], do)
            dab_, gP["gdn_alog"][j], gP["gdn_dtb"][j] = _gdn_gates_bwd(f"gdn_gates_b{j}", s["proj"], dg_h, db_h,
                                                                        P["gdn_alog"][j], P["gdn_dtb"][j])
            dpre, gP["gdn_cw"][j] = _gdn_conv_bwd(f"gdn_conv_b{j}", s["proj"], P["gdn_cw"][j], dqkv)
            dproj = jnp.concatenate([dpre, dgate, dab_], axis=1)
            g_in = _mm(f"gdn_in_dw{j}", s["h"], dproj, "tn", out_dtype=BF16, tn=GDN_INK // 2)
            on_grads(l, "mix", dict(gdn_w_in=_uncols(_gdn_in_from_kernel(g_in)), gdn_w_out=_unrows(g_out)))
            dh = _mm(f"gdn_in_dx{j}", dproj, W["gdn_in"], "nt")
        else:
            do = _mm(f"mla_out_dx{j}", dy, W["mla_out"], "nt", out_dtype=BF16)
            g_out = _mm(f"mla_out_dw{j}", s["o"], dy, "tn", out_dtype=BF16)
            dqn, dqr, dkn, dv, dkr = _mla_attn_bwd(f"mla_attn_b{j}", s["qn"], s["qr"], s["kv"], s["kr"], do)
            dq = _mla_q_bwd(f"mla_q_b{j}", dqn, dqr, cos, sin)
            dkv = jnp.concatenate([dkn, dv], axis=1)
            g_uq = _mm(f"mla_uq_dw{j}", s["cqn"], dq, "tn", out_dtype=BF16)
            dcqn = _mm(f"mla_uq_dx{j}", dq, W["mla_uq"], "nt")
            g_ukv = _mm(f"mla_ukv_dw{j}", s["ckvn"], dkv, "tn", out_dtype=BF16)
            dckvn = _mm(f"mla_ukv_dx{j}", dkv, W["mla_ukv"], "nt")
            dproj, gP["mla_qg"][j], gP["mla_kvg"][j] = _mla_pre_bwd(f"mla_pre_b{j}", s["proj"], dcqn, dckvn, dkr, cos, sin,
                                                                     P["mla_qg"][j], P["mla_kvg"][j])
            g_in = _mm(f"mla_in_dw{j}", s["h"], dproj, "tn", out_dtype=BF16)
            on_grads(l, "mix", dict(mla_w_in=_unrows(g_in[:, :Q_RANK + KV_RANK + ROPE]), mla_w_uq=_uncols(_mla_uq_from_kernel(g_uq)),
                                    mla_w_ukv=_uncols(_mla_ukv_from_kernel(g_ukv)), mla_w_out=_unrows(g_out)))
            dh = _mm(f"mla_in_dx{j}", dproj, W["mla_in"], "nt")
        if l > 0:
            dx, dyf_prev, d_sh_m, d_sc_m, gP["norm_mix_g"][l], d_ga_f_prev = _norm_residual_bwd(
                f"norm_mix_b{l}", dh, s["x0"], dx, (P["norm_mix_g"], l, 0, D), sc_m, saved[l - 1]["yf"], _mod_rows(mod, l - 1)[5])
        else:
            dx, d_sh_m, d_sc_m, gP["norm_mix_g"][l] = _norm_mod_bwd(f"norm_mix_b{l}", dh, s["x0"], dx,
                                                                     (P["norm_mix_g"], l, 0, D), sc_m)
        dmod[l] = jnp.concatenate([d_sh_m, d_sc_m, d_ga_m, d_sh_f, d_sc_f, d_ga_f], axis=1)
        if l > 0:
            dyf, d_ga_f = dyf_prev, d_ga_f_prev
    return dx, jnp.concatenate(dmod, axis=0), gP


def _pad_cols(a, width):
    return jnp.pad(a, ((0, 0), (0, width - a.shape[1])))


def _gdn_in_to_kernel(w):
    m = GDN_QKV + NH * HD
    return jnp.concatenate([w[:, :m], _pad_cols(w[:, m:m + NH], HD), _pad_cols(w[:, m + NH:], HD)], axis=1)


def _gdn_in_from_kernel(g):
    m = GDN_QKV + NH * HD
    return jnp.concatenate([g[:, :m], g[:, m:m + NH], g[:, m + HD:m + HD + NH]], axis=1)


def _mla_uq_to_kernel(w):
    w3 = w.reshape(Q_RANK, NH, HD + ROPE)
    rope = jnp.pad(w3[:, :, HD:], ((0, 0), (0, 0), (0, HD - ROPE)))
    return jnp.concatenate([w3[:, :, :HD].reshape(Q_RANK, NH * HD), rope.reshape(Q_RANK, NH * HD)], axis=1)


def _mla_uq_from_kernel(g):
    gn = g[:, :NH * HD].reshape(Q_RANK, NH, HD)
    gr = g[:, NH * HD:].reshape(Q_RANK, NH, HD)[:, :, :ROPE]
    return jnp.concatenate([gn, gr], axis=2).reshape(Q_RANK, NH * (HD + ROPE))


def _mla_ukv_to_kernel(w):
    w3 = w.reshape(KV_RANK, NH, 2 * HD)
    return jnp.concatenate([w3[:, :, :HD].reshape(KV_RANK, NH * HD), w3[:, :, HD:].reshape(KV_RANK, NH * HD)], axis=1)


def _mla_ukv_from_kernel(g):
    gk = g[:, :NH * HD].reshape(KV_RANK, NH, HD)
    gv = g[:, NH * HD:].reshape(KV_RANK, NH, HD)
    return jnp.concatenate([gk, gv], axis=2).reshape(KV_RANK, NH * 2 * HD)


def _cols(t):
    return jnp.moveaxis(t, 0, 1).reshape(t.shape[1], -1)


def _uncols(g):
    return jnp.moveaxis(g.reshape(g.shape[0], 4, -1), 1, 0)


def _rows(t):
    return t.reshape(-1, t.shape[2])


def _unrows(g):
    return g.reshape(4, -1, g.shape[1])


def _layer_weights(layer):
    mixer = ("gdn_w_in", "gdn_w_out") if layer % 2 == 0 else ("mla_w_in", "mla_w_uq", "mla_w_ukv", "mla_w_out")
    return [(n, layer // 2) for n in mixer] + [(n, layer) for n in ("ffn_w_gate", "ffn_w_up", "ffn_w_down")]


def _weights_to_kernel(layer, g):
    out = dict(ffn_g=g["ffn_w_gate"], ffn_u=g["ffn_w_up"], ffn_d=g["ffn_w_down"])
    if layer % 2 == 0:
        out.update(gdn_in=_gdn_in_to_kernel(_cols(g["gdn_w_in"])), gdn_out=_rows(g["gdn_w_out"]))
    else:
        out.update(mla_in=_pad_cols(_rows(g["mla_w_in"]), MLA_INK), mla_uq=_mla_uq_to_kernel(_cols(g["mla_w_uq"])),
                   mla_ukv=_mla_ukv_to_kernel(_cols(g["mla_w_ukv"])), mla_out=_rows(g["mla_w_out"]))
    return out


def _small_to_kernel(norm_mix_g, norm_ffn_g, final_norm_g, gdn_conv_w, gdn_a_log, gdn_dt_bias, gdn_norm_g, q_norm_g, kv_norm_g):
    return dict(
        norm_mix_g=norm_mix_g, norm_ffn_g=norm_ffn_g, final_g=final_norm_g.reshape(1, D),
        gdn_cw=[jnp.transpose(gdn_conv_w[j]) for j in range(2)],
        gdn_alog=[_pad_cols(gdn_a_log[j:j + 1], HD) for j in range(2)],
        gdn_dtb=[_pad_cols(gdn_dt_bias[j:j + 1], HD) for j in range(2)],
        gdn_ng=[gdn_norm_g[j:j + 1] for j in range(2)],
        mla_qg=[q_norm_g[j:j + 1] for j in range(2)],
        mla_kvg=[kv_norm_g[j:j + 1] for j in range(2)],
    )


_CHIP_FLIPS = ((1, 0), (0, 1), (1, 1))
_ANY = pl.BlockSpec(memory_space=pl.ANY)


def _me():
    return lax.axis_index("x"), lax.axis_index("y"), lax.axis_index("c")


def _chip_peer(dx, dy):
    x, y, c = _me()
    return ((1 - x) if dx else x, (1 - y) if dy else y, c)


def _rcopy(src, dst, send_sem, recv_sem, to):
    return pltpu.make_async_remote_copy(src_ref=src, dst_ref=dst, send_sem=send_sem, recv_sem=recv_sem,
                                        device_id=to, device_id_type=MESH)


def _allgather4(name, a, halves=False):
    R, C = a.shape
    rh = R // 2 if halves else R

    def body(a_ref, out_ref, send_sems, recv_sems, local_sem):
        x, y, c = _me()
        me = 2 * x + y
        src = a_ref.at[pl.ds(c * rh, rh)] if halves else a_ref
        local = pltpu.make_async_copy(src, out_ref.at[me], local_sem)
        local.start()
        sends = []
        for k, (dx, dy) in enumerate(_CHIP_FLIPS):
            cp = _rcopy(src, out_ref.at[me], send_sems.at[k], recv_sems.at[k], _chip_peer(dx, dy))
            cp.start()
            sends.append(cp)
        for k, (dx, dy) in enumerate(_CHIP_FLIPS):
            px, py, _ = _chip_peer(dx, dy)
            _rcopy(src, out_ref.at[2 * px + py], send_sems.at[k], recv_sems.at[k], _chip_peer(dx, dy)).wait_recv()
        for cp in sends:
            cp.wait_send()
        local.wait()

    return pl.pallas_call(
        body, name=name, in_specs=[_ANY], out_specs=_ANY, out_shape=jax.ShapeDtypeStruct((4, rh, C), a.dtype),
        scratch_shapes=[pltpu.SemaphoreType.DMA((3,)), pltpu.SemaphoreType.DMA((3,)), pltpu.SemaphoreType.DMA(())])(a)


_NCH = 4


def _dma_sems(*counts):
    return [pltpu.SemaphoreType.DMA((n,)) for n in counts]


def _slot_tile(rows, cap=512):
    best = rows
    for tr in range(16, min(rows, cap) + 1, 16):
        if rows % tr == 0:
            best = tr
    return best


def _cast_into_slot(name, a, chip, row0, rows):
    C = a.shape[1]
    tr = _slot_tile(rows)
    assert row0 % tr == 0
    first = row0 // tr

    def body(c_ref, a_ref, o_ref):
        o_ref[0] = a_ref[...].astype(o_ref.dtype)

    grid_spec = pltpu.PrefetchScalarGridSpec(
        num_scalar_prefetch=1, grid=(rows // tr,), in_specs=[pl.BlockSpec((tr, C), lambda i, c_ref: (first + i, 0))],
        out_specs=pl.BlockSpec((1, tr, C), lambda i, c_ref: (c_ref[0], i, 0)))
    return pl.pallas_call(body, name=name, grid_spec=grid_spec, out_shape=jax.ShapeDtypeStruct((4, rows, C), BF16),
                          compiler_params=_params(1))(chip, a)


def _chunks(rows, align):
    for nch in (_NCH, 2):
        if rows % (nch * align) == 0:
            return nch
    return 1


def _gather_exchange(out, ici_s, ici_r, d2d_s, d2d_r):
    n = len(out)
    x, y, c = _me()
    me = 2 * x + y
    sib = (x, y, 1 - c)
    peers = [_chip_peer(dx, dy) for dx, dy in _CHIP_FLIPS]
    for t in range(n):
        h = out[t].shape[1] // 2
        nch = _chunks(h, 16)
        ch = h // nch
        for k, peer in enumerate(peers):
            for i in range(nch):
                blk = out[t].at[me, pl.ds(c * h + i * ch, ch)]
                _rcopy(blk, blk, ici_s.at[3 * t + k], ici_r.at[3 * t + k], peer).start()
    for t in range(n):
        h = out[t].shape[1] // 2
        nch = _chunks(h, 16)
        ch = h // nch
        for k, peer in enumerate(peers):
            pchip = 2 * peer[0] + peer[1]
            got = out[t].at[pchip, pl.ds(c * h, h)]
            _rcopy(got, got, ici_s.at[3 * t + k], ici_r.at[3 * t + k], peer).wait_recv()
            for i in range(nch):
                blk = out[t].at[pchip, pl.ds(c * h + i * ch, ch)]
                _rcopy(blk, blk, d2d_s.at[3 * t + k], d2d_r.at[3 * t + k], sib).start()
    for t in range(n):
        h = out[t].shape[1] // 2
        for k, peer in enumerate(peers):
            pchip = 2 * peer[0] + peer[1]
            other = out[t].at[pchip, pl.ds((1 - c) * h, h)]
            _rcopy(other, other, d2d_s.at[3 * t + k], d2d_r.at[3 * t + k], sib).wait_recv()
            _rcopy(other, other, ici_s.at[3 * t + k], ici_r.at[3 * t + k], peer).wait_send()
            _rcopy(other, other, d2d_s.at[3 * t + k], d2d_r.at[3 * t + k], sib).wait_send()


def _gather_weights(name, bufs):
    n = len(bufs)

    def body(*refs):
        _gather_exchange(refs[n:2 * n], *refs[2 * n:])

    return pl.pallas_call(
        body, name=name, in_specs=[_ANY] * n, out_specs=[_ANY] * n,
        out_shape=[jax.ShapeDtypeStruct(s.shape, s.dtype) for s in bufs],
        input_output_aliases={t: t for t in range(n)},
        scratch_shapes=_dma_sems(3 * n, 3 * n, 3 * n, 3 * n))(*bufs)


def _gather_weights_async(name, collective_id, bufs):
    n = len(bufs)
    refs = [jax.new_ref(b, memory_space=pltpu.MemorySpace.HBM) for b in bufs]

    @pl.kernel(mesh=plsc.ScalarSubcoreMesh(axis_name="sequencer", num_cores=1), name=name,
               scratch_types=tuple(_dma_sems(3 * n, 3 * n, 3 * n, 3 * n)),
               compiler_params=pltpu.CompilerParams(collective_id=collective_id))
    def launch(ici_s, ici_r, d2d_s, d2d_r):
        x, y, c = _me()
        barrier = pltpu.get_barrier_semaphore()
        for peer in [_chip_peer(dx, dy) for dx, dy in _CHIP_FLIPS] + [(x, y, 1 - c)]:
            pl.semaphore_signal(barrier, inc=1, device_id=peer, device_id_type=MESH)
        pl.semaphore_wait(barrier, 4)
        _gather_exchange(refs, ici_s, ici_r, d2d_s, d2d_r)

    launch()
    return [r[...] for r in refs]


def _rs_split(name, grads):
    n = len(grads)

    def body(*refs):
        g, out = refs[:n], refs[n:2 * n]
        send, recv = refs[2 * n:]
        x, y, c = _me()
        sib = (x, y, 1 - c)
        for t in range(n):
            h = g[t].shape[1] // 2
            for d in range(4):
                _rcopy(g[t].at[d, pl.ds((1 - c) * h, h)], out[t].at[d], send.at[t], recv.at[t], sib).start()
        for t in range(n):
            _rcopy(out[t], out[t], send.at[t], recv.at[t], sib).wait()

    return pl.pallas_call(
        body, name=name, in_specs=[_ANY] * n, out_specs=[_ANY] * n,
        out_shape=[jax.ShapeDtypeStruct((4, s.shape[1] // 2, s.shape[2]), s.dtype) for s in grads],
        scratch_shapes=_dma_sems(n, n))(*grads)


def _pair_add(name, g, theirs, core_chip):
    _, R, C = g.shape
    h = R // 2
    tr = _slot_tile(h)
    nb = h // tr

    def body(s_ref, g_ref, t_ref, p_ref, o_ref):
        val = (g_ref[...].astype(F32) + t_ref[...].astype(F32)).astype(p_ref.dtype)
        p_ref[...] = val

        @pl.when(pl.program_id(1) == s_ref[1])
        def _():
            o_ref[...] = val

    spec = pl.BlockSpec((1, tr, C), lambda i, d, s_ref: (d, i, 0))
    grid_spec = pltpu.PrefetchScalarGridSpec(
        num_scalar_prefetch=1, grid=(nb, 4),
        in_specs=[pl.BlockSpec((1, tr, C), lambda i, d, s_ref: (d, s_ref[0] * nb + i, 0)), spec],
        out_specs=[spec, pl.BlockSpec((1, tr, C), lambda i, d, s_ref: (s_ref[1], i, 0))])
    half = jax.ShapeDtypeStruct((4, h, C), BF16)
    return pl.pallas_call(body, name=name, grid_spec=grid_spec, out_shape=[half, half],
                          compiler_params=_params(2))(core_chip, g, theirs)


def _rs_alltoall_async(name, collective_id, parts, bufs):
    n = len(parts)
    p = [jax.new_ref(a, memory_space=pltpu.MemorySpace.HBM) for a in parts]
    out = [jax.new_ref(b, memory_space=pltpu.MemorySpace.HBM) for b in bufs]

    @pl.kernel(mesh=plsc.ScalarSubcoreMesh(axis_name="sequencer", num_cores=1), name=name,
               scratch_types=tuple(_dma_sems(3 * n, 3 * n)),
               compiler_params=pltpu.CompilerParams(collective_id=collective_id))
    def launch(send, recv):
        barrier = pltpu.get_barrier_semaphore()
        for peer in [_chip_peer(dx, dy) for dx, dy in _CHIP_FLIPS]:
            pl.semaphore_signal(barrier, inc=1, device_id=peer, device_id_type=MESH)
        pl.semaphore_wait(barrier, 3)
        _alltoall_exchange(p, out, send, recv)

    launch()
    return [r[...] for r in out]


def _alltoall_exchange(p, out, send, recv):
    x, y, c = _me()
    me = 2 * x + y
    peers = [_chip_peer(dx, dy) for dx, dy in _CHIP_FLIPS]
    for t in range(len(p)):
        h = p[t].shape[1]
        nch = _chunks(h, 16)
        ch = h // nch
        for k, peer in enumerate(peers):
            pchip = 2 * peer[0] + peer[1]
            for i in range(nch):
                rows = pl.ds(i * ch, ch)
                _rcopy(p[t].at[pchip, rows], out[t].at[me, rows], send.at[3 * t + k], recv.at[3 * t + k], peer).start()
    for t in range(len(p)):
        for k, peer in enumerate(peers):
            pchip = 2 * peer[0] + peer[1]
            _rcopy(out[t].at[pchip], out[t].at[pchip], send.at[3 * t + k], recv.at[3 * t + k], peer).wait()


def _rs_swap(name, halves):
    n = len(halves)

    def body(*refs):
        a, out = refs[:n], refs[n:2 * n]
        send, recv = refs[2 * n:]
        x, y, c = _me()
        sib = (x, y, 1 - c)
        for t in range(n):
            ch = a[t].shape[0] // _NCH
            for i in range(_NCH):
                rows = pl.ds(i * ch, ch)
                _rcopy(a[t].at[rows], out[t].at[rows], send.at[t], recv.at[t], sib).start()
        for t in range(n):
            _rcopy(a[t], out[t], send.at[t], recv.at[t], sib).wait()

    return pl.pallas_call(
        body, name=name, in_specs=[_ANY] * n, out_specs=[_ANY] * n,
        out_shape=[jax.ShapeDtypeStruct(s.shape, s.dtype) for s in halves],
        scratch_shapes=_dma_sems(n, n))(*halves)


def _sibling_merge(name, a):
    P_, rh, C = a.shape

    def body(a_ref, out_ref, send_sem, recv_sem, local_sem):
        x, y, c = _me()
        local = pltpu.make_async_copy(a_ref, out_ref.at[:, pl.ds(c * rh, rh)], local_sem)
        local.start()
        cp = _rcopy(a_ref, out_ref.at[:, pl.ds(c * rh, rh)], send_sem, recv_sem, (x, y, 1 - c))
        cp.start()
        cp.wait_send()
        _rcopy(a_ref, out_ref.at[:, pl.ds((1 - c) * rh, rh)], send_sem, recv_sem, (x, y, 1 - c)).wait_recv()
        local.wait()

    return pl.pallas_call(
        body, name=name, in_specs=[_ANY], out_specs=_ANY, out_shape=jax.ShapeDtypeStruct((P_, 2 * rh, C), a.dtype),
        scratch_shapes=[pltpu.SemaphoreType.DMA(()), pltpu.SemaphoreType.DMA(()), pltpu.SemaphoreType.DMA(())])(a)


def _allgather8(name, a):
    g4 = _allgather4(name + "_chips", a)
    both = _sibling_merge(name + "_cores", g4.reshape(1, 4 * a.shape[0], a.shape[1]))
    return jnp.transpose(both.reshape(2, 4, *a.shape), (1, 0, 2, 3)).reshape(8, *a.shape)


def _sum_slots(name, a, out_dtype):
    def fn(a):
        acc = a[0].astype(F32)
        for k in range(1, a.shape[0]):
            acc = acc + a[k].astype(F32)
        return acc
    return _rowwise(name, fn, [a], [], [(a.shape[2], out_dtype)])[0]


def _adamw_math(w, g, m, v):
    m = ADAM_B1 * m + (1.0 - ADAM_B1) * g
    v = ADAM_B2 * v + (1.0 - ADAM_B2) * (g * g)
    m_hat = m / (1.0 - ADAM_B1 ** ADAM_STEP)
    v_hat = v / (1.0 - ADAM_B2 ** ADAM_STEP)
    return -ADAM_LR * (m_hat / (jnp.sqrt(v_hat) + ADAM_EPS) + ADAM_WD * w), m, v


def _adamw_piece(name, w2, m2, v2, mine, theirs, row0, prev, core):
    R, C = w2.shape
    h = mine.shape[0]
    tr = _slot_tile(h, 256)
    nb = h // tr
    assert row0 % tr == 0
    first = row0 // tr

    def body(c_ref, w_ref, m_ref, v_ref, a_ref, b_ref, *rest):
        g_ref, d_ref, nm_ref, nv_ref = rest[-4:]
        g = jnp.where(pl.program_id(0) == c_ref[0], a_ref[...], b_ref[...])
        g_ref[...] = g
        d_ref[...], nm_ref[...], nv_ref[...] = _adamw_math(w_ref[...], g, m_ref[...], v_ref[...])

    full = pl.BlockSpec((tr, C), lambda s, i, c_ref: (first + s * nb + i, 0))
    mine_spec = pl.BlockSpec((tr, C), lambda s, i, c_ref: (jnp.where(s == c_ref[0], i, 0), 0))
    theirs_spec = pl.BlockSpec((tr, C), lambda s, i, c_ref: (jnp.where(s == c_ref[0], 0, i), 0))
    extra = [] if prev is None else list(prev)
    grid_spec = pltpu.PrefetchScalarGridSpec(
        num_scalar_prefetch=1, grid=(2, nb), in_specs=[full, full, full, mine_spec, theirs_spec] + [_ANY] * len(extra),
        out_specs=[full] * 4)
    return pl.pallas_call(
        body, name=name, grid_spec=grid_spec, out_shape=[jax.ShapeDtypeStruct((R, C), F32)] * 4,
        input_output_aliases={6 + k: k for k in range(len(extra))}, compiler_params=_params(2))(core, w2, m2, v2, mine, theirs, *extra)


def _adamw(name, w, g, m, v):
    shape = w.shape
    two_d = (-1, shape[-1]) if w.ndim > 1 else (1, -1)
    w2, g2, m2, v2 = [t.reshape(two_d) for t in (w, g, m, v)]
    rows = w2.shape[0]
    tr = rows
    for cand in (256, 128, 64, 32, 16, 8):
        if rows % cand == 0:
            tr = cand
            break

    c = w2.shape[1]
    outs = _rowwise(name, _adamw_math, [w2, g2, m2, v2], [], [(c, F32)] * 3, tr=tr)
    return [o.reshape(shape) for o in outs]


_WEIGHT_ORDER = ("ada_w", "ada_b", "norm_mix_g", "norm_ffn_g", "gdn_w_in", "gdn_conv_w", "gdn_a_log", "gdn_dt_bias",
                 "gdn_norm_g", "gdn_w_out", "mla_w_in", "mla_q_norm_g", "mla_kv_norm_g", "mla_w_uq", "mla_w_ukv",
                 "mla_w_out", "ffn_w_gate", "ffn_w_up", "ffn_w_down", "final_norm_g")
_BIG = (("gdn_w_in", 2), ("gdn_w_out", 1), ("mla_w_in", 1), ("mla_w_uq", 2), ("mla_w_ukv", 2), ("mla_w_out", 1),
        ("ffn_w_gate", 2), ("ffn_w_up", 2), ("ffn_w_down", 1))
_SMALL_SHARDED = (("gdn_conv_w", 1), ("mla_q_norm_g", 1), ("mla_kv_norm_g", 1))
_STORED_TRANSPOSED = ("ffn_w_gate", "ffn_w_up")


def _size(shape):
    n = 1
    for s in shape:
        n *= s
    return n


def _pack_rows_each(tensors):
    parts, offs, off = [], [], 0
    for t in tensors:
        flat = t.reshape(-1).astype(F32)
        rows = -(-flat.shape[0] // PACK_W)
        parts.append(jnp.pad(flat, (0, rows * PACK_W - flat.shape[0])).reshape(rows, PACK_W))
        offs.append(off)
        off += rows
    total = -(-off // 16) * 16
    pack = jnp.pad(parts[0], ((offs[0], total - offs[0] - parts[0].shape[0]), (0, 0)))
    for p, o in zip(parts[1:], offs[1:]):
        pack = pack + jnp.pad(p, ((o, total - o - p.shape[0]), (0, 0)))
    return pack, offs


def _unpack_rows_each(pack, shapes):
    lead = pack.shape[:-2]
    out, off = [], 0
    for shp in shapes:
        n = _size(shp)
        rows = -(-n // PACK_W)
        out.append(pack[..., off:off + rows, :].reshape(*lead, -1)[..., :n].reshape(*lead, *shp))
        off += rows
    return out


def _merge_chips(stacked, axis):
    moved = jnp.moveaxis(stacked, 0, axis)
    shp = list(moved.shape)
    return moved.reshape(shp[:axis] + [shp[axis] * shp[axis + 1]] + shp[axis + 2:])


def _my_shard(full, axis, chip):
    n = full.shape[axis] // 4
    return lax.dynamic_slice_in_dim(full, chip * n, n, axis)


def kernel(x, c, positions, ada_w, ada_b, norm_mix_g, norm_ffn_g, gdn_w_in, gdn_conv_w, gdn_a_log, gdn_dt_bias, gdn_norm_g, gdn_w_out, mla_w_in, mla_q_norm_g, mla_kv_norm_g, mla_w_uq, mla_w_ukv, mla_w_out, ffn_w_gate, ffn_w_up, ffn_w_down, final_norm_g, loss_target, m_ada_w, m_ada_b, m_norm_mix_g, m_norm_ffn_g, m_gdn_w_in, m_gdn_conv_w, m_gdn_a_log, m_gdn_dt_bias, m_gdn_norm_g, m_gdn_w_out, m_mla_w_in, m_mla_q_norm_g, m_mla_kv_norm_g, m_mla_w_uq, m_mla_w_ukv, m_mla_w_out, m_ffn_w_gate, m_ffn_w_up, m_ffn_w_down, m_final_norm_g, v_ada_w, v_ada_b, v_norm_mix_g, v_norm_ffn_g, v_gdn_w_in, v_gdn_conv_w, v_gdn_a_log, v_gdn_dt_bias, v_gdn_norm_g, v_gdn_w_out, v_mla_w_in, v_mla_q_norm_g, v_mla_kv_norm_g, v_mla_w_uq, v_mla_w_ukv, v_mla_w_out, v_ffn_w_gate, v_ffn_w_up, v_ffn_w_down, v_final_norm_g):
    w = dict(ada_w=ada_w, ada_b=ada_b, norm_mix_g=norm_mix_g, norm_ffn_g=norm_ffn_g, gdn_w_in=gdn_w_in, gdn_conv_w=gdn_conv_w,
             gdn_a_log=gdn_a_log, gdn_dt_bias=gdn_dt_bias, gdn_norm_g=gdn_norm_g, gdn_w_out=gdn_w_out, mla_w_in=mla_w_in,
             mla_q_norm_g=mla_q_norm_g, mla_kv_norm_g=mla_kv_norm_g, mla_w_uq=mla_w_uq, mla_w_ukv=mla_w_ukv,
             mla_w_out=mla_w_out, ffn_w_gate=ffn_w_gate, ffn_w_up=ffn_w_up, ffn_w_down=ffn_w_down, final_norm_g=final_norm_g)
    m = dict(ada_w=m_ada_w, ada_b=m_ada_b, norm_mix_g=m_norm_mix_g, norm_ffn_g=m_norm_ffn_g, gdn_w_in=m_gdn_w_in,
             gdn_conv_w=m_gdn_conv_w, gdn_a_log=m_gdn_a_log, gdn_dt_bias=m_gdn_dt_bias, gdn_norm_g=m_gdn_norm_g,
             gdn_w_out=m_gdn_w_out, mla_w_in=m_mla_w_in, mla_q_norm_g=m_mla_q_norm_g, mla_kv_norm_g=m_mla_kv_norm_g,
             mla_w_uq=m_mla_w_uq, mla_w_ukv=m_mla_w_ukv, mla_w_out=m_mla_w_out, ffn_w_gate=m_ffn_w_gate,
             ffn_w_up=m_ffn_w_up, ffn_w_down=m_ffn_w_down, final_norm_g=m_final_norm_g)
    v = dict(ada_w=v_ada_w, ada_b=v_ada_b, norm_mix_g=v_norm_mix_g, norm_ffn_g=v_norm_ffn_g, gdn_w_in=v_gdn_w_in,
             gdn_conv_w=v_gdn_conv_w, gdn_a_log=v_gdn_a_log, gdn_dt_bias=v_gdn_dt_bias, gdn_norm_g=v_gdn_norm_g,
             gdn_w_out=v_gdn_w_out, mla_w_in=v_mla_w_in, mla_q_norm_g=v_mla_q_norm_g, mla_kv_norm_g=v_mla_kv_norm_g,
             mla_w_uq=v_mla_w_uq, mla_w_ukv=v_mla_w_ukv, mla_w_out=v_mla_w_out, ffn_w_gate=v_ffn_w_gate,
             ffn_w_up=v_ffn_w_up, ffn_w_down=v_ffn_w_down, final_norm_g=v_final_norm_g)
    T = x.shape[1]
    ix, iy, ic = _me()
    chip = 2 * ix + iy
    seq = 2 * chip + ic
    n_dev = 8

    small_shapes = [w[n].shape for n, _ in _SMALL_SHARDED] + [c.shape]
    pack0, _ = _pack_rows_each([w[n] for n, _ in _SMALL_SHARDED] + [c])
    got0 = _unpack_rows_each(_allgather8("gather_small", pack0), small_shapes)
    small_full = {n: _merge_chips(g[0::2], ax) for (n, ax), g in zip(_SMALL_SHARDED, got0)}
    c_all = got0[-1].reshape(n_dev, D)

    big = [n for n, _ in _BIG]
    chip_arr = chip.astype(jnp.int32).reshape(1)

    def stored(n, t):
        return jnp.swapaxes(t, 1, 2) if n in _STORED_TRANSPOSED else t

    ws, ms, vs = [{n: stored(n, d[n]) for n in big} for d in (w, m, v)]
    two_d = lambda t: t.reshape(-1, t.shape[-1])

    gathered = []
    for l in range(DEPTH):
        names = _layer_weights(l)
        bufs = [_cast_into_slot(f"to_bf16_{n}{l}", two_d(ws[n]), chip_arr, j * ws[n].shape[1], ws[n].shape[1]) for n, j in names]
        filled = _gather_weights("gather_weights0", bufs) if l == 0 else _gather_weights_async(f"gather_weights{l}", l, bufs)
        gathered.append({n: b for (n, _), b in zip(names, filled)})

    def weights_of(l, h):
        return _weights_to_kernel(l, gathered[l])

    P = _small_to_kernel(norm_mix_g, norm_ffn_g, final_norm_g, small_full["gdn_conv_w"], gdn_a_log, gdn_dt_bias,
                         gdn_norm_g, small_full["mla_q_norm_g"], small_full["mla_kv_norm_g"])

    c16 = jnp.pad(c_all, ((0, 16 - n_dev), (0, 0)))
    ca = _rowwise("cond_silu", lambda t: t * _sig(t), [c16], [], [(D, BF16)])[0]
    n_ada = ada_w.shape[2]
    mods = jnp.concatenate([_mm(f"ada_fwd{l}", ca, ada_w[l], "nn") for l in range(DEPTH)], axis=0)
    mods_all = _allgather4("gather_mod", mods).reshape(4, DEPTH, 16, n_ada)
    mod_mm = jnp.transpose(lax.dynamic_index_in_dim(mods_all, seq, axis=2, keepdims=False), (1, 0, 2)).reshape(DEPTH, 4 * n_ada)
    mod = _rowwise("mod_bias", lambda a, b: a + b, [mod_mm, ada_b], [], [(4 * n_ada, F32)])[0]

    core_chip = jnp.stack([ic, chip]).astype(jnp.int32)
    pending, in_flight = {}, []

    def reduce_group(layer, part, pieces):
        pending.update({(n, layer if n.startswith("ffn_") else layer // 2): g for n, g in pieces.items()})
        if part == "ffn" and layer > 0:
            return
        keys = list(pending)
        glist = [pending.pop(k) for k in keys]
        tag = f"{layer}{part}"
        theirs = _rs_split("grads_cores_" + tag, glist)
        both = [_pair_add(f"grads_pair_{n}{l}", g, t, core_chip) for (n, l), g, t in zip(keys, glist, theirs)]
        swapped = _rs_alltoall_async("grads_chips_" + tag, DEPTH + 1 + len(in_flight), [p for p, _ in both], [o for _, o in both])
        in_flight.append((tag, keys, swapped))

    dx, dmod, gP = _local_step(x.reshape(T, D), loss_target.reshape(T, D), positions.reshape(T, 1), mod, weights_of, P, reduce_group)

    partials = [dmod, jnp.concatenate(gP["norm_mix_g"]), jnp.concatenate(gP["norm_ffn_g"]), gP["final_g"],
                jnp.stack([jnp.transpose(g) for g in gP["gdn_cw"]]), jnp.concatenate(gP["gdn_alog"])[:, :NH],
                jnp.concatenate(gP["gdn_dtb"])[:, :NH], jnp.concatenate(gP["gdn_ng"]), jnp.concatenate(gP["mla_qg"]),
                jnp.concatenate(gP["mla_kvg"]), gP["loss"][:, :1]]
    part_shapes = [p.shape for p in partials]
    ppack, _ = _pack_rows_each(partials)
    pall = _allgather8("gather_partials", ppack)
    psum = _sum_slots("sum_partials", pall, F32)
    (g_ada_b, g_norm_mix, g_norm_ffn, g_final, g_conv_full, g_alog, g_dtb, g_gdn_ng, g_qg_full, g_kvg_full,
     loss_sum) = _unpack_rows_each(psum, part_shapes)
    dmod_all = _unpack_rows_each(pall, part_shapes[:1])[0]

    grads = dict(ada_b=g_ada_b, norm_mix_g=g_norm_mix, norm_ffn_g=g_norm_ffn, final_norm_g=g_final.reshape(D),
                 gdn_conv_w=_my_shard(g_conv_full, 1, chip), gdn_a_log=g_alog, gdn_dt_bias=g_dtb, gdn_norm_g=g_gdn_ng,
                 mla_q_norm_g=_my_shard(g_qg_full, 1, chip), mla_kv_norm_g=_my_shard(g_kvg_full, 1, chip))

    ca_t = jnp.zeros((D, LANES), BF16).at[:, :16].set(jnp.transpose(ca))
    dm_mine = lax.dynamic_slice_in_dim(dmod_all, chip * n_ada, n_ada, axis=2)
    grads["ada_w"] = jnp.stack([
        _mm(f"ada_bwd{l}", ca_t, jnp.pad(dm_mine[:, l], ((0, LANES - n_dev), (0, 0))), "nn") for l in range(DEPTH)])

    delta, new_m, new_v = {}, {}, {}
    results = {}
    keys = [k for _, ks, _ in in_flight for k in ks]
    halves = [_sum_slots(f"grads_sum_{n}{l}", s, F32) for _, ks, sw in in_flight for (n, l), s in zip(ks, sw)]
    others = _rs_swap("grads_swap", halves)
    for (n, l), mine, theirs in zip(keys, halves, others):
        results[n] = _adamw_piece(f"adamw_{n}{l}", two_d(ws[n]), two_d(ms[n]), two_d(vs[n]), mine, theirs,
                                  l * ws[n].shape[1], results.get(n), core_chip[:1])
    for n in big:
        grads[n], delta[n], new_m[n], new_v[n] = [stored(n, t.reshape(ws[n].shape)) for t in results[n]]
    delta["ada_w"], new_m["ada_w"], new_v["ada_w"] = _adamw("adamw_ada_w", ada_w, grads["ada_w"], m_ada_w, v_ada_w)
    for n in [n for n in _WEIGHT_ORDER if n not in delta]:
        delta[n], new_m[n], new_v[n] = _adamw("adamw_" + n, w[n], grads[n], m[n], v[n])

    loss = loss_sum.reshape(())
    return (loss, dx.reshape(1, T, D), *[grads[n] for n in _WEIGHT_ORDER], *[delta[n] for n in _WEIGHT_ORDER],
            *[new_m[n] for n in _WEIGHT_ORDER], *[new_v[n] for n in _WEIGHT_ORDER])
```

```python
import functools

import jax
import jax.numpy as jnp
from jax import lax
from jax.experimental import pallas as pl
from jax.experimental.pallas import tpu as pltpu
from jax.experimental.pallas import tpu_sc as plsc

F32 = jnp.float32
BF16 = jnp.bfloat16
HI = lax.Precision.HIGHEST
MESH = pl.DeviceIdType.MESH

D = 1024
DEPTH = 4
N_MOD = 6
NH = 8
HD = 128
CHUNK = 64
_GDN_HB = 8
GDN_QKV = 3 * NH * HD
GDN_INK = GDN_QKV + NH * HD + 2 * HD
Q_RANK, KV_RANK, ROPE = 384, 256, 64
MLA_INK = Q_RANK + KV_RANK + HD
DFF = 2816
EPS = 1e-6
ATT_SCALE = (HD + ROPE) ** -0.5
ROPE_THETA = 10000.0
LANES = 128
PACK_W = 1024

ADAM_LR, ADAM_B1, ADAM_B2, ADAM_EPS, ADAM_WD, ADAM_STEP = 0.001, 0.9, 0.999, 1e-08, 0.01, 10


H3 = "bf16x3"
B1 = "bf16"
HS = H3
HF = B1


def _dot(a, b, mode="nn", prec=None):
    dn = {"nn": (((1,), (0,)), ((), ())), "nt": (((1,), (1,)), ((), ())), "tn": (((0,), (0,)), ((), ()))}[mode]
    if prec == B1:
        return _dot(a.astype(BF16), b.astype(BF16), mode)
    if prec == H3:
        ah, bh = a.astype(BF16), b.astype(BF16)
        al, bl = (a - ah.astype(F32)).astype(BF16), (b - bh.astype(F32)).astype(BF16)
        return _dot(ah, bh, mode) + (_dot(ah, bl, mode) + _dot(al, bh, mode))
    return lax.dot_general(a, b, dn, precision=prec, preferred_element_type=F32)


def _sig(x):
    return 1.0 / (1.0 + jnp.exp(-x))


def _pick(n, cap):
    if n <= cap:
        return n
    best = None
    for d in range(LANES, cap + 1, LANES):
        if n % d == 0:
            best = d
    assert best is not None, (n, cap)
    return best


def _params(n_grid):
    return pltpu.CompilerParams(dimension_semantics=("arbitrary",) * n_grid, vmem_limit_bytes=56 * 1024 * 1024)


def _rowwise(name, fn, rows, consts, outs, sums=(), tr=256):
    first = rows[0][0] if isinstance(rows[0], tuple) else rows[0]
    T = first.shape[-2]
    tr = _slot_tile(T, tr)
    nr, nc, no, ns = len(rows), len(consts), len(outs), len(sums)

    windows = [c[1:] if isinstance(c, tuple) else None for c in consts]
    consts = [c[0] if isinstance(c, tuple) else c for c in consts]

    def body(*refs):
        vals = [r[...] for r in refs[:nr]]
        for r, win in zip(refs[nr:nr + nc], windows):
            vals.append(r[...] if win is None else r[win[0]:win[0] + 1, win[1] * win[2]:(win[1] + 1) * win[2]])
        res = fn(*vals)
        if not isinstance(res, (tuple, list)):
            res = (res,)
        o_refs = refs[nr + nc:nr + nc + no]
        s_refs = refs[nr + nc + no:]
        for r, val in zip(o_refs, res[:no]):
            r[...] = val.astype(r.dtype)
        if ns:
            @pl.when(pl.program_id(0) == 0)
            def _():
                for r in s_refs:
                    r[...] = jnp.zeros_like(r)
            for r, val in zip(s_refs, res[no:]):
                r[...] += val

    in_specs, args = [], []
    for a in rows:
        if isinstance(a, tuple):
            arr, width, cb = a
            in_specs.append(pl.BlockSpec((tr, width), lambda i, cb=cb: (i, cb)))
            args.append(arr)
        elif a.ndim == 3:
            in_specs.append(pl.BlockSpec((a.shape[0], tr, a.shape[2]), lambda i: (0, i, 0)))
            args.append(a)
        else:
            in_specs.append(pl.BlockSpec((tr, a.shape[1]), lambda i: (i, 0)))
            args.append(a)
    for a in consts:
        in_specs.append(pl.BlockSpec(a.shape, lambda i, nd=a.ndim: (0,) * nd))
        args.append(a)
    out_specs = [pl.BlockSpec((tr, w), lambda i: (i, 0)) for w, _ in outs]
    out_specs += [pl.BlockSpec((1, w), lambda i: (0, 0)) for w in sums]
    out_shape = [jax.ShapeDtypeStruct((T, w), dt) for w, dt in outs]
    out_shape += [jax.ShapeDtypeStruct((1, w), F32) for w in sums]
    res = pl.pallas_call(body, name=name, grid=(T // tr,), in_specs=in_specs, out_specs=out_specs,
                         out_shape=out_shape, compiler_params=_params(1))(*args)
    return res


def _mm(name, a, b, mode, out_dtype=F32, tm=512, tn=1024):
    if mode == "tn":
        K, M = a.shape
    else:
        M, K = a.shape
    N = b.shape[0] if mode == "nt" else b.shape[1]
    tm, tn = _pick(M, tm), _pick(N, tn)

    def body(a_ref, b_ref, o_ref):
        o_ref[...] = _dot(a_ref[...].astype(BF16), b_ref[...].astype(BF16), mode).astype(o_ref.dtype)

    a_spec = pl.BlockSpec((K, tm), lambda i, j: (0, i)) if mode == "tn" else pl.BlockSpec((tm, K), lambda i, j: (i, 0))
    b_spec = pl.BlockSpec((tn, K), lambda i, j: (j, 0)) if mode == "nt" else pl.BlockSpec((K, tn), lambda i, j: (0, j))
    return pl.pallas_call(body, name=name, grid=(M // tm, N // tn), in_specs=[a_spec, b_spec],
                          out_specs=pl.BlockSpec((tm, tn), lambda i, j: (i, j)),
                          out_shape=jax.ShapeDtypeStruct((M, N), out_dtype), compiler_params=_params(2))(a, b)


def _rms(x, eps=EPS):
    return lax.rsqrt(jnp.mean(x * x, axis=-1, keepdims=True) + eps)


def _norm_mod_fwd(name, x, g, scale, shift):
    def fn(x, g, scale, shift):
        return x * _rms(x) * g * (1.0 + scale) + shift
    return _rowwise(name, fn, [x], [g, scale, shift], [(D, BF16)])[0]


def _norm_mod_bwd(name, dh, x, dx_res, g, scale):
    def fn(dh, x, dx_res, g, scale):
        r = _rms(x)
        xh = x * r
        dxh = dh * (g * (1.0 + scale))
        dx = r * (dxh - xh * jnp.mean(dxh * xh, axis=-1, keepdims=True))
        dhx = dh * xh
        return (dx_res + dx, jnp.sum(dh, axis=0, keepdims=True), jnp.sum(dhx * g, axis=0, keepdims=True),
                jnp.sum(dhx * (1.0 + scale), axis=0, keepdims=True))
    return _rowwise(name, fn, [dh, x, dx_res], [g, scale], [(D, F32)], sums=[D, D, D])


def _residual_fwd(name, x, y, gate):
    def fn(x, y, gate):
        return x + gate * y
    return _rowwise(name, fn, [x, y], [gate], [(D, F32)])[0]


def _residual_norm_fwd(name, x, y, gate, g, scale, shift):
    def fn(x, y, gate, g, scale, shift):
        x = x + gate * y
        return x, x * _rms(x) * g * (1.0 + scale) + shift
    return _rowwise(name, fn, [x, y], [gate, g, scale, shift], [(D, F32), (D, BF16)])


def _norm_residual_bwd(name, dh, x, dx_res, g, scale, y, gate):
    def fn(dh, x, dx_res, y, g, scale, gate):
        r = _rms(x)
        xh = x * r
        dxh = dh * (g * (1.0 + scale))
        dx = dx_res + r * (dxh - xh * jnp.mean(dxh * xh, axis=-1, keepdims=True))
        dhx = dh * xh
        return (dx, dx * gate, jnp.sum(dh, axis=0, keepdims=True), jnp.sum(dhx * g, axis=0, keepdims=True),
                jnp.sum(dhx * (1.0 + scale), axis=0, keepdims=True), jnp.sum(dx * y, axis=0, keepdims=True))
    return _rowwise(name, fn, [dh, x, dx_res, y], [g, scale, gate], [(D, F32), (D, BF16)], sums=[D, D, D, D])


def _residual_bwd(name, dx, y, gate):
    def fn(dx, y, gate):
        return dx * gate, jnp.sum(dx * y, axis=0, keepdims=True)
    return _rowwise(name, fn, [dx, y], [gate], [(D, BF16)], sums=[D])


def _loss_head(x, target, g):
    def fn(x, t, g):
        r = _rms(x)
        xh = x * r
        err = xh * g - t
        loss = 0.5 * jnp.sum(jnp.mean(err * err, axis=-1, keepdims=True), axis=0, keepdims=True)
        dy = err * (1.0 / D)
        dxh = dy * g
        dx = r * (dxh - xh * jnp.mean(dxh * xh, axis=-1, keepdims=True))
        return dx, jnp.broadcast_to(loss, (1, LANES)), jnp.sum(dy * xh, axis=0, keepdims=True)
    return _rowwise("loss_head", fn, [x, target], [g], [(D, F32)], sums=[LANES, D])


def _ffn_up(name, h, wg, wu, layer, tm=1024):
    T, n = h.shape[0], wg.shape[1]
    tm = min(tm, T)

    def body(h_ref, wg_ref, wu_ref, a_ref, b_ref, s_ref):
        h = h_ref[...]
        a = _dot(h, wg_ref[0], "nt")
        b = _dot(h, wu_ref[0], "nt")
        a_ref[0] = a.astype(a_ref.dtype)
        b_ref[0] = b.astype(b_ref.dtype)
        s_ref[0] = (a * _sig(a) * b).astype(s_ref.dtype)

    wspec = pl.BlockSpec((1, n, D), lambda ch, i: (ch, layer, 0))
    ospec = pl.BlockSpec((1, tm, n), lambda ch, i: (ch, i, 0))
    return pl.pallas_call(
        body, name=name, grid=(4, T // tm), in_specs=[pl.BlockSpec((tm, D), lambda ch, i: (i, 0)), wspec, wspec],
        out_specs=[ospec, ospec, ospec],
        out_shape=[jax.ShapeDtypeStruct((4, T, n), BF16)] * 3, compiler_params=_params(2))(h, wg, wu)


def _ffn_down(name, s, wd, layer, tm=1024):
    _, T, n = s.shape
    tm = min(tm, T)

    def body(s_ref, w_ref, y_ref):
        @pl.when(pl.program_id(1) == 0)
        def _():
            y_ref[...] = jnp.zeros_like(y_ref)
        y_ref[...] += _dot(s_ref[0], w_ref[0], "nn")

    return pl.pallas_call(
        body, name=name, grid=(T // tm, 4),
        in_specs=[pl.BlockSpec((1, tm, n), lambda i, ch: (ch, i, 0)), pl.BlockSpec((1, n, D), lambda i, ch: (ch, layer, 0))],
        out_specs=pl.BlockSpec((tm, D), lambda i, ch: (i, 0)), out_shape=jax.ShapeDtypeStruct((T, D), F32),
        compiler_params=_params(2))(s, wd)


def _ffn_down_bwd(name, dy, wd, a, b, layer, tm=1024):
    _, T, n = a.shape
    tm = min(tm, T)

    def body(dy_ref, w_ref, a_ref, b_ref, da_ref, db_ref):
        ds = _dot(dy_ref[...], w_ref[0], "nt")
        a, b = a_ref[0].astype(F32), b_ref[0].astype(F32)
        sg = _sig(a)
        da_ref[0] = (ds * b * (sg * (1.0 + a * (1.0 - sg)))).astype(da_ref.dtype)
        db_ref[0] = (ds * (a * sg)).astype(db_ref.dtype)

    bspec = pl.BlockSpec((1, tm, n), lambda ch, i: (ch, i, 0))
    return pl.pallas_call(
        body, name=name, grid=(4, T // tm),
        in_specs=[pl.BlockSpec((tm, D), lambda ch, i: (i, 0)), pl.BlockSpec((1, n, D), lambda ch, i: (ch, layer, 0)), bspec, bspec],
        out_specs=[bspec, bspec], out_shape=[jax.ShapeDtypeStruct((4, T, n), BF16)] * 2,
        compiler_params=_params(2))(dy, wd, a, b)


def _ffn_down_dw(name, s, dy):
    _, T, n = s.shape

    def body(s_ref, dy_ref, o_ref):
        o_ref[0] = _dot(s_ref[0], dy_ref[...], "tn").astype(o_ref.dtype)

    return pl.pallas_call(
        body, name=name, grid=(4,),
        in_specs=[pl.BlockSpec((1, T, n), lambda ch: (ch, 0, 0)), pl.BlockSpec((T, D), lambda ch: (0, 0))],
        out_specs=pl.BlockSpec((1, n, D), lambda ch: (ch, 0, 0)), out_shape=jax.ShapeDtypeStruct((4, n, D), BF16),
        compiler_params=_params(1))(s, dy)


def _ffn_up_dw(name, h, da, db, tm=512):
    _, T, n = da.shape

    def body(h_ref, da_ref, db_ref, dg_ref, du_ref):
        h = h_ref[...]
        dg_ref[0] = _dot(da_ref[0], h, "tn").astype(dg_ref.dtype)
        du_ref[0] = _dot(db_ref[0], h, "tn").astype(du_ref.dtype)

    dspec = pl.BlockSpec((1, T, n), lambda ch, j: (ch, 0, 0))
    ospec = pl.BlockSpec((1, n, tm), lambda ch, j: (ch, 0, j))
    return pl.pallas_call(
        body, name=name, grid=(4, D // tm), in_specs=[pl.BlockSpec((T, tm), lambda ch, j: (0, j)), dspec, dspec],
        out_specs=[ospec, ospec], out_shape=[jax.ShapeDtypeStruct((4, n, D), BF16)] * 2,
        compiler_params=_params(2))(h, da, db)


def _ffn_up_dx(name, da, db, wg, wu, layer, tm=1024):
    _, T, n = da.shape
    tm = min(tm, T)

    def body(da_ref, db_ref, wg_ref, wu_ref, o_ref):
        @pl.when(pl.program_id(1) == 0)
        def _():
            o_ref[...] = jnp.zeros_like(o_ref)
        o_ref[...] += _dot(da_ref[0], wg_ref[0], "nn") + _dot(db_ref[0], wu_ref[0], "nn")

    dspec = pl.BlockSpec((1, tm, n), lambda i, ch: (ch, i, 0))
    wspec = pl.BlockSpec((1, n, D), lambda i, ch: (ch, layer, 0))
    return pl.pallas_call(
        body, name=name, grid=(T // tm, 4), in_specs=[dspec, dspec, wspec, wspec],
        out_specs=pl.BlockSpec((tm, D), lambda i, ch: (i, 0)), out_shape=jax.ShapeDtypeStruct((T, D), F32),
        compiler_params=_params(2))(da, db, wg, wu)


def _shift_down(x, k):
    if k == 0:
        return x
    rows = lax.broadcasted_iota(jnp.int32, x.shape, 0)
    return jnp.where(rows >= k, pltpu.roll(x, k, 0), 0.0)


def _shift_up(x, k):
    if k == 0:
        return x
    T = x.shape[0]
    rows = lax.broadcasted_iota(jnp.int32, x.shape, 0)
    return jnp.where(rows < T - k, pltpu.roll(x, T - k, 0), 0.0)


def _conv_silu(x, w):
    c = w[0:1, :] * _shift_down(x, 3) + w[1:2, :] * _shift_down(x, 2) + w[2:3, :] * _shift_down(x, 1) + w[3:4, :] * x
    sg = _sig(c)
    return c, sg, c * sg


def _gdn_conv_fwd(name, proj, cw):
    T = proj.shape[0]

    def body(x_ref, w_ref, o_ref):
        j = pl.program_id(0)
        _, _, y = _conv_silu(x_ref[...], w_ref[...])
        r = lax.rsqrt(jnp.sum(y * y, axis=1, keepdims=True) + EPS)
        mult = jnp.where(j < NH, HD ** -0.5, 1.0)
        o_ref[...] = jnp.where(j < 2 * NH, y * (r * mult), y)

    return pl.pallas_call(body, name=name, grid=(3 * NH,),
                          in_specs=[pl.BlockSpec((T, HD), lambda j: (0, j)), pl.BlockSpec((4, HD), lambda j: (0, j))],
                          out_specs=pl.BlockSpec((T, HD), lambda j: (0, j)),
                          out_shape=jax.ShapeDtypeStruct((T, GDN_QKV), F32), compiler_params=_params(1))(proj, cw)


def _gdn_conv_bwd(name, proj, cw, dz):
    T = proj.shape[0]

    def body(x_ref, w_ref, dz_ref, dx_ref, dw_ref):
        j = pl.program_id(0)
        x, w, dz = x_ref[...], w_ref[...], dz_ref[...]
        c, sg, y = _conv_silu(x, w)
        r = lax.rsqrt(jnp.sum(y * y, axis=1, keepdims=True) + EPS)
        mult = jnp.where(j < NH, HD ** -0.5, 1.0)
        dyn = mult * (r * dz - (r * r * r) * y * jnp.sum(dz * y, axis=1, keepdims=True))
        dy = jnp.where(j < 2 * NH, dyn, dz)
        dc = dy * (sg * (1.0 + c * (1.0 - sg)))
        dx = w[0:1, :] * _shift_up(dc, 3) + w[1:2, :] * _shift_up(dc, 2) + w[2:3, :] * _shift_up(dc, 1) + w[3:4, :] * dc
        dx_ref[...] = dx.astype(dx_ref.dtype)
        for k in range(4):
            dw_ref[pl.ds(k, 1), :] = jnp.sum(dc * _shift_down(x, 3 - k), axis=0, keepdims=True)

    return pl.pallas_call(body, name=name, grid=(3 * NH,),
                          in_specs=[pl.BlockSpec((T, HD), lambda j: (0, j)), pl.BlockSpec((4, HD), lambda j: (0, j)),
                                    pl.BlockSpec((T, HD), lambda j: (0, j))],
                          out_specs=[pl.BlockSpec((T, HD), lambda j: (0, j)), pl.BlockSpec((4, HD), lambda j: (0, j))],
                          out_shape=[jax.ShapeDtypeStruct((T, GDN_QKV), BF16), jax.ShapeDtypeStruct((4, GDN_QKV), F32)],
                          compiler_params=_params(1))(proj, cw, dz)


def _softplus(z):
    return jnp.maximum(z, 0.0) + jnp.log(1.0 + jnp.exp(-jnp.abs(z)))


_AB_CB = GDN_INK // (2 * HD) - 1


def _gdn_gates_fwd(name, proj, alog, dtb):
    def fn(ab, alog, dtb):
        a, b = ab[:, :HD], ab[:, HD:]
        return -jnp.exp(alog) * _softplus(a + dtb), _sig(b)
    return _rowwise(name, fn, [(proj, 2 * HD, _AB_CB)], [alog, dtb], [(HD, F32), (HD, F32)])


def _gdn_gates_bwd(name, proj, dg_h, db_h, alog, dtb):
    def fn(ab, dg_h, db_h, alog, dtb):
        lane = lax.broadcasted_iota(jnp.int32, (1, HD), 1)
        dg = jnp.zeros(dg_h.shape[1:], F32)
        dbeta = jnp.zeros(dg_h.shape[1:], F32)
        for h in range(NH):
            oh = (lane == h).astype(F32)
            dg = dg + dg_h[h] * oh
            dbeta = dbeta + db_h[h] * oh
        a, b = ab[:, :HD], ab[:, HD:]
        z = a + dtb
        ea = jnp.exp(alog)
        beta = _sig(b)
        da = dg * (-ea) * _sig(z)
        db = dbeta * beta * (1.0 - beta)
        return (jnp.concatenate([da, db], axis=1), jnp.sum(dg * (-ea * _softplus(z)), axis=0, keepdims=True),
                jnp.sum(da, axis=0, keepdims=True))
    return _rowwise(name, fn, [(proj, 2 * HD, _AB_CB), dg_h, db_h], [alog, dtb], [(2 * HD, BF16)], sums=[HD, HD])


def _interleave(gens):
    gens = list(gens)
    results = [None] * len(gens)
    active = list(range(len(gens)))
    while active:
        for i in list(active):
            try:
                next(gens[i])
            except StopIteration as stop:
                results[i] = stop.value
                active.remove(i)
    return results


def _chunk_common(q, k, v, gblk, bblk, h, prec):
    C = CHUNK
    lane = lax.broadcasted_iota(jnp.int32, (1, HD), 1)
    oh = (lane == h).astype(F32)
    g_col = jnp.sum(gblk * oh, axis=1, keepdims=True)
    beta = jnp.sum(bblk * oh, axis=1, keepdims=True)
    ri = lax.broadcasted_iota(jnp.int32, (C, C), 0)
    ci = lax.broadcasted_iota(jnp.int32, (C, C), 1)
    incl = ri >= ci
    strict = ri > ci
    eye = (ri == ci).astype(F32)
    gcb = _dot(incl.astype(F32), jnp.broadcast_to(g_col, (C, HD)), "nn", HI)
    yield
    gc = gcb[:, :C]
    gc_row = _dot(jnp.ones((C, C), F32), eye * gc, "nn", HI)
    yield
    decay = jnp.where(incl, jnp.exp(jnp.where(incl, gc - gc_row, 0.0)), 0.0)
    rows = lax.broadcasted_iota(jnp.int32, (C, HD), 0)
    gclb = jnp.sum(jnp.where(rows == C - 1, gcb, 0.0), axis=0, keepdims=True)
    eg = jnp.exp(gcb)
    egl = jnp.exp(gclb - gcb)
    gl = jnp.exp(gclb)
    kb = k * beta
    m1 = _dot(kb, k, "nt", prec)
    qk = _dot(q, k, "nt", prec)
    yield
    L = jnp.where(strict, m1 * decay, 0.0)
    nl = -L
    tinv = eye + nl
    p = nl
    for _ in range(5):
        p = _dot(p, p, "nn", H3)
        yield
        tinv = tinv + _dot(tinv, p, "nn", H3)
    vb = v * beta
    kbg = kb * eg
    yield
    u = _dot(tinv, vb, "nn", prec)
    w = _dot(tinv, kbg, "nn", prec)
    yield
    attn = jnp.where(incl, qk * decay, 0.0)
    return dict(beta=beta, incl=incl, strict=strict, decay=decay, eg=eg, egl=egl, gl=gl, kb=kb, m1=m1, tinv=tinv,
                kbg=kbg, u=u, w=w, qk=qk, attn=attn, q_dec=q * eg, k_dec=k * egl, rows=rows, oh=oh)


def _gdn_chunk_fwd(name, qkv, g, beta):
    T = qkv.shape[0]
    N = T // CHUNK

    hb = _GDN_HB
    w = hb * HD

    def body(q_ref, k_ref, v_ref, g_ref, b_ref, o_ref, st_ref, S):
        hg, n = pl.program_id(0), pl.program_id(1)

        @pl.when(n == 0)
        def _():
            S[...] = jnp.zeros_like(S)

        gblk, bblk = g_ref[...], b_ref[...]

        def one_head(i, q, k, v, s):
            c = yield from _chunk_common(q, k, v, gblk, bblk, hg * hb + i, HF)
            v_new = c["u"] - _dot(c["w"], s, "nn", HF)
            qs = _dot(c["q_dec"], s, "nn", HF)
            yield
            o = qs + _dot(c["attn"], v_new, "nn", HF)
            return o, s * c["gl"] + _dot(c["k_dec"], v_new, "tn", HF)

        sls = [slice(i * HD, (i + 1) * HD) for i in range(hb)]
        states = [S[i] for i in range(hb)]
        res = _interleave(one_head(i, q_ref[:, sls[i]], k_ref[:, sls[i]], v_ref[:, sls[i]], states[i]) for i in range(hb))
        for i, (o, s_new) in enumerate(res):
            st_ref[i, 0] = states[i]
            o_ref[:, sls[i]] = o
            S[i] = s_new

    blk = lambda off: pl.BlockSpec((CHUNK, w), lambda h, n, off=off: (n, off + h))
    gspec = pl.BlockSpec((CHUNK, HD), lambda h, n: (n, 0))
    return pl.pallas_call(
        body, name=name, grid=(NH // hb, N), in_specs=[blk(0), blk(NH // hb), blk(2 * NH // hb), gspec, gspec],
        out_specs=[pl.BlockSpec((CHUNK, w), lambda h, n: (n, h)), pl.BlockSpec((hb, 1, HD, HD), lambda h, n: (h, n, 0, 0))],
        out_shape=[jax.ShapeDtypeStruct((T, NH * HD), F32), jax.ShapeDtypeStruct((NH, N, HD, HD), F32)],
        scratch_shapes=[pltpu.VMEM((hb, HD, HD), F32)], compiler_params=_params(2))(qkv, qkv, qkv, g, beta)


def _gdn_chunk_bwd(name, qkv, g, beta, states, do):
    T = qkv.shape[0]
    N = T // CHUNK
    C = CHUNK

    hb = _GDN_HB
    w = hb * HD
    assert hb == NH

    def body(q_ref, k_ref, v_ref, g_ref, b_ref, st_ref, do_ref, dqkv_ref, dg_ref, db_ref, dS):
        hg, n = pl.program_id(0), pl.program_id(1)

        @pl.when(n == 0)
        def _():
            dS[...] = jnp.zeros_like(dS)

        gblk, bblk = g_ref[...], b_ref[...]
        sls = [slice(i * HD, (i + 1) * HD) for i in range(hb)]
        res = _interleave(one_head(hg * hb + i, gblk, bblk, q_ref[:, sls[i]], k_ref[:, sls[i]], v_ref[:, sls[i]],
                                   st_ref[i, 0], do_ref[:, sls[i]], dS[i]) for i in range(hb))
        for i, (dq, dk, dv, dg, db, ds_new) in enumerate(res):
            dqkv_ref[:, sls[i]] = dq
            dqkv_ref[:, slice(w + i * HD, w + (i + 1) * HD)] = dk
            dqkv_ref[:, slice(2 * w + i * HD, 2 * w + (i + 1) * HD)] = dv
            dg_ref[i] = dg
            db_ref[i] = db
            dS[i] = ds_new

    def one_head(h, gblk, bblk, q, k, v, s, do, ds):
        c = yield from _chunk_common(q, k, v, gblk, bblk, h, HF)
        eg, egl, gl, beta, decay, tinv = c["eg"], c["egl"], c["gl"], c["beta"], c["decay"], c["tinv"]
        v_new = c["u"] - _dot(c["w"], s, "nn", HF)
        dq_dec = _dot(do, s, "nt", HF)
        yield
        dv_new = _dot(c["attn"], do, "tn", HF) + _dot(c["k_dec"], ds, "nn", HF)
        dk_dec = _dot(v_new, ds, "nt", HF)
        dgl = jnp.sum(jnp.sum(s * ds, axis=1, keepdims=True), axis=0, keepdims=True)
        yield
        ds_new = ds * gl + _dot(c["q_dec"], do, "tn", HF) - _dot(c["w"], dv_new, "tn", HF)
        dattn = jnp.where(c["incl"], _dot(do, v_new, "nt", HF), 0.0)
        dw = -_dot(dv_new, s, "nt", HF)
        yield
        dvb = _dot(tinv, dv_new, "tn", HS)
        dkbg = _dot(tinv, dw, "tn", HS)
        yield
        dA = -(_dot(dvb, c["u"], "nt", HS) + _dot(dkbg, c["w"], "nt", HS))
        yield
        dL = jnp.where(c["strict"], dA, 0.0)
        dm1 = dL * decay
        dqk = dattn * decay
        xdec = (dL * c["m1"] + dattn * c["qk"]) * decay
        dkb = _dot(dm1, k, "nn", HS) + dkbg * eg
        dk = _dot(dm1, c["kb"], "tn", HF) + _dot(dqk, q, "tn", HF) + dk_dec * egl + dkb * beta
        dq = _dot(dqk, k, "nn", HF) + dq_dec * eg
        yield
        dkd_kd = jnp.sum(dk_dec * c["k_dec"], axis=1, keepdims=True)
        dgc = (jnp.sum(xdec, axis=1, keepdims=True) - _dot(xdec, jnp.ones((C, HD), F32), "tn", HS)
               + jnp.sum(dq_dec * c["q_dec"], axis=1, keepdims=True) - dkd_kd
               + jnp.sum(dkbg * c["kbg"], axis=1, keepdims=True))
        dgcl = jnp.sum(dkd_kd, axis=0, keepdims=True) + dgl * gl
        dgc = dgc + jnp.where(c["rows"] == C - 1, dgcl, 0.0)
        ri = lax.broadcasted_iota(jnp.int32, (C, C), 0)
        ci = lax.broadcasted_iota(jnp.int32, (C, C), 1)
        dg = _dot((ci >= ri).astype(F32), dgc, "nn", HI)
        db = jnp.broadcast_to(jnp.sum(dkb * k, axis=1, keepdims=True) + jnp.sum(dvb * v, axis=1, keepdims=True), (C, HD))
        return dq, dk, dvb * beta, dg, db, ds_new

    blk = lambda off: pl.BlockSpec((C, w), lambda h, n, off=off: (N - 1 - n, off + h))
    gspec = pl.BlockSpec((C, HD), lambda h, n: (N - 1 - n, 0))
    ospec = pl.BlockSpec((C, w), lambda h, n: (N - 1 - n, h))
    hspec = pl.BlockSpec((hb, C, HD), lambda h, n: (h, N - 1 - n, 0))
    return pl.pallas_call(
        body, name=name, grid=(NH // hb, N),
        in_specs=[blk(0), blk(NH // hb), blk(2 * NH // hb), gspec, gspec,
                  pl.BlockSpec((hb, 1, HD, HD), lambda h, n: (h, N - 1 - n, 0, 0)), ospec],
        out_specs=[pl.BlockSpec((C, 3 * w), lambda h, n: (N - 1 - n, 0)), hspec, hspec],
        out_shape=[jax.ShapeDtypeStruct((T, 3 * NH * HD), F32)] + [jax.ShapeDtypeStruct((NH, T, HD), F32)] * 2,
        scratch_shapes=[pltpu.VMEM((hb, HD, HD), F32)], compiler_params=_params(2))(qkv, qkv, qkv, g, beta, states, do)


_GATE_CB = GDN_QKV // (NH * HD)


def _gdn_gated_norm_fwd(name, o, proj, ng):
    def fn(o, gate, ng):
        outs = []
        for h in range(NH):
            sl = slice(h * HD, (h + 1) * HD)
            oh, gh = o[:, sl], gate[:, sl]
            outs.append(oh * _rms(oh) * ng * (gh * _sig(gh)))
        return jnp.concatenate(outs, axis=1)
    return _rowwise(name, fn, [o, (proj, NH * HD, _GATE_CB)], [ng], [(NH * HD, BF16)])[0]


def _gdn_gated_norm_bwd(name, don, o, proj, ng):
    def fn(don, o, gate, ng):
        dos, dgs = [], []
        dng = jnp.zeros((1, HD), F32)
        for h in range(NH):
            sl = slice(h * HD, (h + 1) * HD)
            oh, gh, dh = o[:, sl], gate[:, sl], don[:, sl]
            r = _rms(oh)
            xh = oh * r
            sg = _sig(gh)
            dn = dh * (gh * sg)
            dgs.append(dh * (xh * ng) * (sg * (1.0 + gh * (1.0 - sg))))
            dng = dng + jnp.sum(dn * xh, axis=0, keepdims=True)
            dxh = dn * ng
            dos.append(r * (dxh - xh * jnp.mean(dxh * xh, axis=-1, keepdims=True)))
        return jnp.concatenate(dos, axis=1), jnp.concatenate(dgs, axis=1), dng
    return _rowwise(name, fn, [don, o, (proj, NH * HD, _GATE_CB)], [ng], [(NH * HD, F32), (NH * HD, BF16)], sums=[HD])


def _rot(x):
    lane = lax.broadcasted_iota(jnp.int32, x.shape, 1)
    return jnp.where(lane < ROPE // 2, -pltpu.roll(x, HD - ROPE // 2, 1), pltpu.roll(x, ROPE // 2, 1))


def _rot_t(x):
    lane = lax.broadcasted_iota(jnp.int32, x.shape, 1)
    return jnp.where(lane < ROPE // 2, pltpu.roll(x, HD - ROPE // 2, 1), -pltpu.roll(x, ROPE // 2, 1))


def _rope_tables(pos_col):
    lane = jnp.arange(HD)
    inv_freq = ROPE_THETA ** (-(2.0 * (lane % (ROPE // 2)).astype(F32)) / ROPE)
    inv_freq = jnp.where(lane < ROPE, inv_freq, 0.0).astype(F32)[None, :]
    valid = (lane < ROPE).astype(F32)[None, :]

    def fn(pos, inv_freq, valid):
        ang = pos.astype(F32) * inv_freq
        return jnp.cos(ang) * valid, jnp.sin(ang) * valid
    return _rowwise("rope_tables", fn, [pos_col], [inv_freq, valid], [(HD, F32), (HD, F32)])


def _mla_pre_fwd(name, proj, cos, sin, qg, kvg):
    def fn(p, cos, sin, qg, kvg):
        cq, ckv, kr = p[:, :Q_RANK], p[:, Q_RANK:Q_RANK + KV_RANK], p[:, Q_RANK + KV_RANK:]
        return cq * _rms(cq) * qg, ckv * _rms(ckv) * kvg, kr * cos + _rot(kr) * sin
    return _rowwise(name, fn, [proj, cos, sin], [qg, kvg], [(Q_RANK, BF16), (KV_RANK, BF16), (HD, BF16)])


def _rms_bwd(dy, x, g):
    r = _rms(x)
    xh = x * r
    dxh = dy * g
    return r * (dxh - xh * jnp.mean(dxh * xh, axis=-1, keepdims=True)), jnp.sum(dy * xh, axis=0, keepdims=True)


def _mla_pre_bwd(name, proj, dcqn, dckvn, dkr, cos, sin, qg, kvg):
    def fn(p, dcqn, dckvn, dkr, cos, sin, qg, kvg):
        cq, ckv = p[:, :Q_RANK], p[:, Q_RANK:Q_RANK + KV_RANK]
        dcq, dqg = _rms_bwd(dcqn, cq, qg)
        dckv, dkvg = _rms_bwd(dckvn, ckv, kvg)
        dkr_pre = dkr * cos + _rot_t(dkr * sin)
        return jnp.concatenate([dcq, dckv, dkr_pre], axis=1), dqg, dkvg
    return _rowwise(name, fn, [proj, dcqn, dckvn, dkr, cos, sin], [qg, kvg], [(MLA_INK, BF16)], sums=[Q_RANK, KV_RANK])


def _mla_q_fwd(name, q, cos, sin):
    def fn(qn, qr, cos, sin):
        outs = []
        for h in range(NH):
            x = qr[:, h * HD:(h + 1) * HD]
            outs.append(x * cos + _rot(x) * sin)
        return qn, jnp.concatenate(outs, axis=1)
    return _rowwise(name, fn, [(q, NH * HD, 0), (q, NH * HD, 1), cos, sin], [], [(NH * HD, BF16), (NH * HD, BF16)])


def _mla_q_bwd(name, dqn, dqr, cos, sin):
    def fn(dqn, dqr, cos, sin):
        outs = [dqn]
        for h in range(NH):
            z = dqr[:, h * HD:(h + 1) * HD]
            outs.append(z * cos + _rot_t(z * sin))
        return jnp.concatenate(outs, axis=1)
    return _rowwise(name, fn, [dqn, dqr, cos, sin], [], [(2 * NH * HD, BF16)])[0]


def _att_probs(qn, qr, kn, kr, row0):
    s = (_dot(qn, kn, "nt") + _dot(qr, kr, "nt")) * ATT_SCALE
    qpos = row0 + lax.broadcasted_iota(jnp.int32, s.shape, 0)
    kpos = lax.broadcasted_iota(jnp.int32, s.shape, 1)
    s = jnp.where(kpos <= qpos, s, -1e30)
    p = jnp.exp(s - jnp.max(s, axis=1, keepdims=True))
    return p * (1.0 / jnp.sum(p, axis=1, keepdims=True))


def _mla_attn_fwd(name, qn, qr, kv, kr, tq=256):
    T = qn.shape[0]
    tq = min(tq, T)

    def body(qn_ref, qr_ref, kn_ref, v_ref, kr_ref, o_ref):
        i = pl.program_id(1)
        for blk in range(T // tq):
            @pl.when(i == blk)
            def _(blk=blk):
                keys = pl.ds(0, (blk + 1) * tq)
                p = _att_probs(qn_ref[...], qr_ref[...], kn_ref[keys, :], kr_ref[keys, :], blk * tq)
                o_ref[...] = _dot(p.astype(BF16), v_ref[keys, :], "nn").astype(o_ref.dtype)

    qspec = pl.BlockSpec((tq, HD), lambda h, i: (i, h))
    return pl.pallas_call(
        body, name=name, grid=(NH, T // tq),
        in_specs=[qspec, qspec, pl.BlockSpec((T, HD), lambda h, i: (0, h)), pl.BlockSpec((T, HD), lambda h, i: (0, NH + h)),
                  pl.BlockSpec((T, HD), lambda h, i: (0, 0))],
        out_specs=qspec, out_shape=jax.ShapeDtypeStruct((T, NH * HD), BF16), compiler_params=_params(2))(qn, qr, kv, kv, kr)


def _mla_attn_bwd(name, qn, qr, kv, kr, do, o, tq=256):
    T = qn.shape[0]
    tq = min(tq, T)

    def body(qn_ref, qr_ref, kn_ref, v_ref, kr_ref, do_ref, o_ref, dqn_ref, dqr_ref, dkn_ref, dv_ref, dkr_ref):
        h, i = pl.program_id(0), pl.program_id(1)

        @pl.when(i == 0)
        def _():
            dkn_ref[...] = jnp.zeros_like(dkn_ref)
            dv_ref[...] = jnp.zeros_like(dv_ref)

        @pl.when((i == 0) & (h == 0))
        def _():
            dkr_ref[...] = jnp.zeros_like(dkr_ref)

        for blk in range(T // tq):
            @pl.when(i == blk)
            def _(blk=blk):
                keys = pl.ds(0, (blk + 1) * tq)
                qn, qr, do = qn_ref[...], qr_ref[...], do_ref[...]
                kn, kr, v = kn_ref[keys, :], kr_ref[keys, :], v_ref[keys, :]
                p = _att_probs(qn, qr, kn, kr, blk * tq)
                dp = _dot(do, v, "nt")
                delta = jnp.sum(do.astype(F32) * o_ref[...].astype(F32), axis=1, keepdims=True)
                ds = (p * (dp - delta) * ATT_SCALE).astype(BF16)
                dqn_ref[...] = _dot(ds, kn, "nn")
                dqr_ref[...] = _dot(ds, kr, "nn")
                dkn_ref[keys, :] += _dot(ds, qn, "tn")
                dkr_ref[keys, :] += _dot(ds, qr, "tn")
                dv_ref[keys, :] += _dot(p.astype(BF16), do, "tn")

    qspec = pl.BlockSpec((tq, HD), lambda h, i: (i, h))
    kspec = pl.BlockSpec((T, HD), lambda h, i: (0, h))
    return pl.pallas_call(
        body, name=name, grid=(NH, T // tq),
        in_specs=[qspec, qspec, kspec, pl.BlockSpec((T, HD), lambda h, i: (0, NH + h)),
                  pl.BlockSpec((T, HD), lambda h, i: (0, 0)), qspec, qspec],
        out_specs=[qspec, qspec, kspec, kspec, pl.BlockSpec((T, HD), lambda h, i: (0, 0))],
        out_shape=[jax.ShapeDtypeStruct((T, NH * HD), F32)] * 4 + [jax.ShapeDtypeStruct((T, HD), F32)],
        compiler_params=_params(2))(qn, qr, kv, kv, kr, do, o)


def _mod_rows(mod, layer):
    return [(mod, layer, i, D) for i in range(N_MOD)]


def _local_step(x, target, pos_col, mod, weights_of, P, on_grads):
    cos, sin = _rope_tables(pos_col)
    saved = []
    sh_m, sc_m = _mod_rows(mod, 0)[:2]
    h = _norm_mod_fwd("norm_mix0", x, (P["norm_mix_g"], 0, 0, D), sc_m, sh_m)
    for l in range(DEPTH):
        j = l // 2
        sh_m, sc_m, ga_m, sh_f, sc_f, ga_f = _mod_rows(mod, l)
        s = dict(x0=x)
        W = weights_of(l, h)
        s.update(h=h, W=W)
        if l % 2 == 0:
            proj = _mm(f"gdn_in{j}", h, W["gdn_in"], "nn", tn=GDN_INK // 2)
            qkv = _gdn_conv_fwd(f"gdn_conv{j}", proj, P["gdn_cw"][j])
            g, beta = _gdn_gates_fwd(f"gdn_gates{j}", proj, P["gdn_alog"][j], P["gdn_dtb"][j])
            o, states = _gdn_chunk_fwd(f"gdn_chunk{j}", qkv, g, beta)
            on = _gdn_gated_norm_fwd(f"gdn_gnorm{j}", o, proj, P["gdn_ng"][j])
            y = _mm(f"gdn_out{j}", on, W["gdn_out"], "nn")
            s.update(proj=proj, qkv=qkv, g=g, beta=beta, o=o, states=states, on=on)
        else:
            proj = _mm(f"mla_in{j}", h, W["mla_in"], "nn")
            cqn, ckvn, kr = _mla_pre_fwd(f"mla_pre{j}", proj, cos, sin, P["mla_qg"][j], P["mla_kvg"][j])
            q = _mm(f"mla_uq{j}", cqn, W["mla_uq"], "nn")
            kv = _mm(f"mla_ukv{j}", ckvn, W["mla_ukv"], "nn", out_dtype=BF16)
            qn, qr = _mla_q_fwd(f"mla_q{j}", q, cos, sin)
            o = _mla_attn_fwd(f"mla_attn{j}", qn, qr, kv, kr)
            y = _mm(f"mla_out{j}", o, W["mla_out"], "nn")
            s.update(proj=proj, cqn=cqn, ckvn=ckvn, kr=kr, kv=kv, qn=qn, qr=qr, o=o)
        s["y"] = y
        x, h2 = _residual_norm_fwd(f"res_mix{l}", x, y, ga_m, (P["norm_ffn_g"], l, 0, D), sc_f, sh_f)
        s["x1"] = x
        fa, fb, sw = _ffn_up(f"ffn_up{l}", h2, W["ffn_g"], W["ffn_u"], 0)
        yf = _ffn_down(f"ffn_down{l}", sw, W["ffn_d"], 0)
        if l + 1 < DEPTH:
            sh_n, sc_n = _mod_rows(mod, l + 1)[:2]
            x, h = _residual_norm_fwd(f"res_ffn{l}", x, yf, ga_f, (P["norm_mix_g"], l + 1, 0, D), sc_n, sh_n)
        else:
            x = _residual_fwd(f"res_ffn{l}", x, yf, ga_f)
        s.update(h2=h2, fa=fa, fb=fb, sw=sw, yf=yf)
        saved.append(s)

    dx, loss, d_final = _loss_head(x, target, P["final_g"])
    gP = dict(loss=loss, final_g=d_final, norm_mix_g=[None] * DEPTH, norm_ffn_g=[None] * DEPTH,
              gdn_cw=[None] * 2, gdn_alog=[None] * 2, gdn_dtb=[None] * 2, gdn_ng=[None] * 2,
              mla_qg=[None] * 2, mla_kvg=[None] * 2)
    dmod = [None] * DEPTH
    dyf, d_ga_f = _residual_bwd(f"res_ffn_b{DEPTH - 1}", dx, saved[-1]["yf"], _mod_rows(mod, DEPTH - 1)[5])
    for l in reversed(range(DEPTH)):
        j = l // 2
        s = saved[l]
        W = s["W"]
        sh_m, sc_m, ga_m, sh_f, sc_f, ga_f = _mod_rows(mod, l)
        da, db = _ffn_down_bwd(f"ffn_down_dx{l}", dyf, W["ffn_d"], s["fa"], s["fb"], 0)
        g_down = _ffn_down_dw(f"ffn_down_dw{l}", s["sw"], dyf)
        g_gate, g_up = _ffn_up_dw(f"ffn_up_dw{l}", s["h2"], da, db)
        on_grads(l, "ffn", dict(ffn_w_gate=g_gate, ffn_w_up=g_up, ffn_w_down=g_down))
        dh2 = _ffn_up_dx(f"ffn_up_dx{l}", da, db, W["ffn_g"], W["ffn_u"], 0)
        dx, dy, d_sh_f, d_sc_f, gP["norm_ffn_g"][l], d_ga_m = _norm_residual_bwd(
            f"norm_ffn_b{l}", dh2, s["x1"], dx, (P["norm_ffn_g"], l, 0, D), sc_f, s["y"], ga_m)
        if l % 2 == 0:
            don = _mm(f"gdn_out_dx{j}", dy, W["gdn_out"], "nt")
            g_out = _mm(f"gdn_out_dw{j}", s["on"], dy, "tn", out_dtype=BF16)
            do, dgate, gP["gdn_ng"][j] = _gdn_gated_norm_bwd(f"gdn_gnorm_b{j}", don, s["o"], s["proj"], P["gdn_ng"][j])
            dqkv, dg_h, db_h = _gdn_chunk_bwd(f"gdn_chunk_b{j}", s["qkv"], s["g"], s["beta"], s["states"], do)
            dab_, gP["gdn_alog"][j], gP["gdn_dtb"][j] = _gdn_gates_bwd(f"gdn_gates_b{j}", s["proj"], dg_h, db_h,
                                                                        P["gdn_alog"][j], P["gdn_dtb"][j])
            dpre, gP["gdn_cw"][j] = _gdn_conv_bwd(f"gdn_conv_b{j}", s["proj"], P["gdn_cw"][j], dqkv)
            dproj = jnp.concatenate([dpre, dgate, dab_], axis=1)
            g_in = _mm(f"gdn_in_dw{j}", s["h"], dproj, "tn", out_dtype=BF16, tn=GDN_INK // 2)
            on_grads(l, "mix", dict(gdn_w_in=_uncols(_gdn_in_from_kernel(g_in)), gdn_w_out=_unrows(g_out)))
            dh = _mm(f"gdn_in_dx{j}", dproj, W["gdn_in"], "nt")
        else:
            do = _mm(f"mla_out_dx{j}", dy, W["mla_out"], "nt", out_dtype=BF16)
            g_out = _mm(f"mla_out_dw{j}", s["o"], dy, "tn", out_dtype=BF16)
            dqn, dqr, dkn, dv, dkr = _mla_attn_bwd(f"mla_attn_b{j}", s["qn"], s["qr"], s["kv"], s["kr"], do, s["o"])
            dq = _mla_q_bwd(f"mla_q_b{j}", dqn, dqr, cos, sin)
            dkv = jnp.concatenate([dkn, dv], axis=1)
            g_uq = _mm(f"mla_uq_dw{j}", s["cqn"], dq, "tn", out_dtype=BF16)
            dcqn = _mm(f"mla_uq_dx{j}", dq, W["mla_uq"], "nt")
            g_ukv = _mm(f"mla_ukv_dw{j}", s["ckvn"], dkv, "tn", out_dtype=BF16)
            dckvn = _mm(f"mla_ukv_dx{j}", dkv, W["mla_ukv"], "nt")
            dproj, gP["mla_qg"][j], gP["mla_kvg"][j] = _mla_pre_bwd(f"mla_pre_b{j}", s["proj"], dcqn, dckvn, dkr, cos, sin,
                                                                     P["mla_qg"][j], P["mla_kvg"][j])
            g_in = _mm(f"mla_in_dw{j}", s["h"], dproj, "tn", out_dtype=BF16)
            on_grads(l, "mix", dict(mla_w_in=_unrows(g_in[:, :Q_RANK + KV_RANK + ROPE]), mla_w_uq=_uncols(_mla_uq_from_kernel(g_uq)),
                                    mla_w_ukv=_uncols(_mla_ukv_from_kernel(g_ukv)), mla_w_out=_unrows(g_out)))
            dh = _mm(f"mla_in_dx{j}", dproj, W["mla_in"], "nt")
        if l > 0:
            dx, dyf_prev, d_sh_m, d_sc_m, gP["norm_mix_g"][l], d_ga_f_prev = _norm_residual_bwd(
                f"norm_mix_b{l}", dh, s["x0"], dx, (P["norm_mix_g"], l, 0, D), sc_m, saved[l - 1]["yf"], _mod_rows(mod, l - 1)[5])
        else:
            dx, d_sh_m, d_sc_m, gP["norm_mix_g"][l] = _norm_mod_bwd(f"norm_mix_b{l}", dh, s["x0"], dx,
                                                                     (P["norm_mix_g"], l, 0, D), sc_m)
        dmod[l] = jnp.concatenate([d_sh_m, d_sc_m, d_ga_m, d_sh_f, d_sc_f, d_ga_f], axis=1)
        if l > 0:
            dyf, d_ga_f = dyf_prev, d_ga_f_prev
    return dx, jnp.concatenate(dmod, axis=0), gP


def _pad_cols(a, width):
    return jnp.pad(a, ((0, 0), (0, width - a.shape[1])))


def _gdn_in_to_kernel(w):
    m = GDN_QKV + NH * HD
    return jnp.concatenate([w[:, :m], _pad_cols(w[:, m:m + NH], HD), _pad_cols(w[:, m + NH:], HD)], axis=1)


def _gdn_in_from_kernel(g):
    m = GDN_QKV + NH * HD
    return jnp.concatenate([g[:, :m], g[:, m:m + NH], g[:, m + HD:m + HD + NH]], axis=1)


def _mla_uq_to_kernel(w):
    w3 = w.reshape(Q_RANK, NH, HD + ROPE)
    rope = jnp.pad(w3[:, :, HD:], ((0, 0), (0, 0), (0, HD - ROPE)))
    return jnp.concatenate([w3[:, :, :HD].reshape(Q_RANK, NH * HD), rope.reshape(Q_RANK, NH * HD)], axis=1)


def _mla_uq_from_kernel(g):
    gn = g[:, :NH * HD].reshape(Q_RANK, NH, HD)
    gr = g[:, NH * HD:].reshape(Q_RANK, NH, HD)[:, :, :ROPE]
    return jnp.concatenate([gn, gr], axis=2).reshape(Q_RANK, NH * (HD + ROPE))


def _mla_ukv_to_kernel(w):
    w3 = w.reshape(KV_RANK, NH, 2 * HD)
    return jnp.concatenate([w3[:, :, :HD].reshape(KV_RANK, NH * HD), w3[:, :, HD:].reshape(KV_RANK, NH * HD)], axis=1)


def _mla_ukv_from_kernel(g):
    gk = g[:, :NH * HD].reshape(KV_RANK, NH, HD)
    gv = g[:, NH * HD:].reshape(KV_RANK, NH, HD)
    return jnp.concatenate([gk, gv], axis=2).reshape(KV_RANK, NH * 2 * HD)


def _cols(t):
    return jnp.moveaxis(t, 0, 1).reshape(t.shape[1], -1)


def _uncols(g):
    return jnp.moveaxis(g.reshape(g.shape[0], 4, -1), 1, 0)


def _rows(t):
    return t.reshape(-1, t.shape[2])


def _unrows(g):
    return g.reshape(4, -1, g.shape[1])


def _layer_weights(layer):
    mixer = ("gdn_w_in", "gdn_w_out") if layer % 2 == 0 else ("mla_w_in", "mla_w_uq", "mla_w_ukv", "mla_w_out")
    return [(n, layer // 2) for n in mixer] + [(n, layer) for n in ("ffn_w_gate", "ffn_w_up", "ffn_w_down")]


def _weights_to_kernel(layer, g):
    out = dict(ffn_g=g["ffn_w_gate"], ffn_u=g["ffn_w_up"], ffn_d=g["ffn_w_down"])
    if layer % 2 == 0:
        out.update(gdn_in=_gdn_in_to_kernel(_cols(g["gdn_w_in"])), gdn_out=_rows(g["gdn_w_out"]))
    else:
        out.update(mla_in=_pad_cols(_rows(g["mla_w_in"]), MLA_INK), mla_uq=_mla_uq_to_kernel(_cols(g["mla_w_uq"])),
                   mla_ukv=_mla_ukv_to_kernel(_cols(g["mla_w_ukv"])), mla_out=_rows(g["mla_w_out"]))
    return out


def _small_to_kernel(norm_mix_g, norm_ffn_g, final_norm_g, gdn_conv_w, gdn_a_log, gdn_dt_bias, gdn_norm_g, q_norm_g, kv_norm_g):
    return dict(
        norm_mix_g=norm_mix_g, norm_ffn_g=norm_ffn_g, final_g=final_norm_g.reshape(1, D),
        gdn_cw=[jnp.transpose(gdn_conv_w[j]) for j in range(2)],
        gdn_alog=[_pad_cols(gdn_a_log[j:j + 1], HD) for j in range(2)],
        gdn_dtb=[_pad_cols(gdn_dt_bias[j:j + 1], HD) for j in range(2)],
        gdn_ng=[gdn_norm_g[j:j + 1] for j in range(2)],
        mla_qg=[q_norm_g[j:j + 1] for j in range(2)],
        mla_kvg=[kv_norm_g[j:j + 1] for j in range(2)],
    )


_CHIP_FLIPS = ((1, 0), (0, 1), (1, 1))
_ANY = pl.BlockSpec(memory_space=pl.ANY)


def _me():
    return lax.axis_index("x"), lax.axis_index("y"), lax.axis_index("c")


def _chip_peer(dx, dy):
    x, y, c = _me()
    return ((1 - x) if dx else x, (1 - y) if dy else y, c)


def _rcopy(src, dst, send_sem, recv_sem, to):
    return pltpu.make_async_remote_copy(src_ref=src, dst_ref=dst, send_sem=send_sem, recv_sem=recv_sem,
                                        device_id=to, device_id_type=MESH)


def _allgather4(name, a, halves=False):
    R, C = a.shape
    rh = R // 2 if halves else R

    def body(a_ref, out_ref, send_sems, recv_sems, local_sem):
        x, y, c = _me()
        me = 2 * x + y
        src = a_ref.at[pl.ds(c * rh, rh)] if halves else a_ref
        local = pltpu.make_async_copy(src, out_ref.at[me], local_sem)
        local.start()
        sends = []
        for k, (dx, dy) in enumerate(_CHIP_FLIPS):
            cp = _rcopy(src, out_ref.at[me], send_sems.at[k], recv_sems.at[k], _chip_peer(dx, dy))
            cp.start()
            sends.append(cp)
        for k, (dx, dy) in enumerate(_CHIP_FLIPS):
            px, py, _ = _chip_peer(dx, dy)
            _rcopy(src, out_ref.at[2 * px + py], send_sems.at[k], recv_sems.at[k], _chip_peer(dx, dy)).wait_recv()
        for cp in sends:
            cp.wait_send()
        local.wait()

    return pl.pallas_call(
        body, name=name, in_specs=[_ANY], out_specs=_ANY, out_shape=jax.ShapeDtypeStruct((4, rh, C), a.dtype),
        scratch_shapes=[pltpu.SemaphoreType.DMA((3,)), pltpu.SemaphoreType.DMA((3,)), pltpu.SemaphoreType.DMA(())])(a)


_NCH = 4


def _dma_sems(*counts):
    return [pltpu.SemaphoreType.DMA((n,)) for n in counts]


def _slot_tile(rows, cap=512):
    best = rows
    for tr in range(16, min(rows, cap) + 1, 16):
        if rows % tr == 0:
            best = tr
    return best


def _cast_into_slot(name, a, chip, row0, rows):
    C = a.shape[1]
    tr = _slot_tile(rows)
    assert row0 % tr == 0
    first = row0 // tr

    def body(c_ref, a_ref, o_ref):
        o_ref[0] = a_ref[...].astype(o_ref.dtype)

    grid_spec = pltpu.PrefetchScalarGridSpec(
        num_scalar_prefetch=1, grid=(rows // tr,), in_specs=[pl.BlockSpec((tr, C), lambda i, c_ref: (first + i, 0))],
        out_specs=pl.BlockSpec((1, tr, C), lambda i, c_ref: (c_ref[0], i, 0)))
    return pl.pallas_call(body, name=name, grid_spec=grid_spec, out_shape=jax.ShapeDtypeStruct((4, rows, C), BF16),
                          compiler_params=_params(1))(chip, a)


def _chunks(rows, align):
    for nch in (_NCH, 2):
        if rows % (nch * align) == 0:
            return nch
    return 1


def _gather_exchange(out, ici_s, ici_r, d2d_s, d2d_r):
    n = len(out)
    x, y, c = _me()
    me = 2 * x + y
    sib = (x, y, 1 - c)
    peers = [_chip_peer(dx, dy) for dx, dy in _CHIP_FLIPS]
    for t in range(n):
        h = out[t].shape[1] // 2
        nch = _chunks(h, 16)
        ch = h // nch
        for k, peer in enumerate(peers):
            for i in range(nch):
                blk = out[t].at[me, pl.ds(c * h + i * ch, ch)]
                _rcopy(blk, blk, ici_s.at[3 * t + k], ici_r.at[3 * t + k], peer).start()
    for t in range(n):
        h = out[t].shape[1] // 2
        nch = _chunks(h, 16)
        ch = h // nch
        for k, peer in enumerate(peers):
            pchip = 2 * peer[0] + peer[1]
            got = out[t].at[pchip, pl.ds(c * h, h)]
            _rcopy(got, got, ici_s.at[3 * t + k], ici_r.at[3 * t + k], peer).wait_recv()
            for i in range(nch):
                blk = out[t].at[pchip, pl.ds(c * h + i * ch, ch)]
                _rcopy(blk, blk, d2d_s.at[3 * t + k], d2d_r.at[3 * t + k], sib).start()
    for t in range(n):
        h = out[t].shape[1] // 2
        for k, peer in enumerate(peers):
            pchip = 2 * peer[0] + peer[1]
            other = out[t].at[pchip, pl.ds((1 - c) * h, h)]
            _rcopy(other, other, d2d_s.at[3 * t + k], d2d_r.at[3 * t + k], sib).wait_recv()
            _rcopy(other, other, ici_s.at[3 * t + k], ici_r.at[3 * t + k], peer).wait_send()
            _rcopy(other, other, d2d_s.at[3 * t + k], d2d_r.at[3 * t + k], sib).wait_send()


def _gather_weights(name, bufs):
    n = len(bufs)

    def body(*refs):
        _gather_exchange(refs[n:2 * n], *refs[2 * n:])

    return pl.pallas_call(
        body, name=name, in_specs=[_ANY] * n, out_specs=[_ANY] * n,
        out_shape=[jax.ShapeDtypeStruct(s.shape, s.dtype) for s in bufs],
        input_output_aliases={t: t for t in range(n)},
        scratch_shapes=_dma_sems(3 * n, 3 * n, 3 * n, 3 * n))(*bufs)


def _gather_weights_async(name, collective_id, bufs):
    n = len(bufs)
    refs = [jax.new_ref(b, memory_space=pltpu.MemorySpace.HBM) for b in bufs]

    @pl.kernel(mesh=plsc.ScalarSubcoreMesh(axis_name="sequencer", num_cores=1), name=name,
               scratch_types=tuple(_dma_sems(3 * n, 3 * n, 3 * n, 3 * n)),
               compiler_params=pltpu.CompilerParams(collective_id=collective_id))
    def launch(ici_s, ici_r, d2d_s, d2d_r):
        x, y, c = _me()
        barrier = pltpu.get_barrier_semaphore()
        for peer in [_chip_peer(dx, dy) for dx, dy in _CHIP_FLIPS] + [(x, y, 1 - c)]:
            pl.semaphore_signal(barrier, inc=1, device_id=peer, device_id_type=MESH)
        pl.semaphore_wait(barrier, 4)
        _gather_exchange(refs, ici_s, ici_r, d2d_s, d2d_r)

    launch()
    return [r[...] for r in refs]


def _rs_split(name, grads):
    n = len(grads)

    def body(*refs):
        g, out = refs[:n], refs[n:2 * n]
        send, recv = refs[2 * n:]
        x, y, c = _me()
        sib = (x, y, 1 - c)
        for t in range(n):
            h = g[t].shape[1] // 2
            for d in range(4):
                _rcopy(g[t].at[d, pl.ds((1 - c) * h, h)], out[t].at[d], send.at[t], recv.at[t], sib).start()
        for t in range(n):
            _rcopy(out[t], out[t], send.at[t], recv.at[t], sib).wait()

    return pl.pallas_call(
        body, name=name, in_specs=[_ANY] * n, out_specs=[_ANY] * n,
        out_shape=[jax.ShapeDtypeStruct((4, s.shape[1] // 2, s.shape[2]), s.dtype) for s in grads],
        scratch_shapes=_dma_sems(n, n))(*grads)


def _pair_add(name, g, theirs, core_chip):
    _, R, C = g.shape
    h = R // 2
    tr = _slot_tile(h)
    nb = h // tr

    def body(s_ref, g_ref, t_ref, p_ref, o_ref):
        val = (g_ref[...].astype(F32) + t_ref[...].astype(F32)).astype(p_ref.dtype)
        p_ref[...] = val

        @pl.when(pl.program_id(1) == s_ref[1])
        def _():
            o_ref[...] = val

    spec = pl.BlockSpec((1, tr, C), lambda i, d, s_ref: (d, i, 0))
    grid_spec = pltpu.PrefetchScalarGridSpec(
        num_scalar_prefetch=1, grid=(nb, 4),
        in_specs=[pl.BlockSpec((1, tr, C), lambda i, d, s_ref: (d, s_ref[0] * nb + i, 0)), spec],
        out_specs=[spec, pl.BlockSpec((1, tr, C), lambda i, d, s_ref: (s_ref[1], i, 0))])
    half = jax.ShapeDtypeStruct((4, h, C), BF16)
    return pl.pallas_call(body, name=name, grid_spec=grid_spec, out_shape=[half, half],
                          compiler_params=_params(2))(core_chip, g, theirs)


def _rs_alltoall_async(name, collective_id, parts, bufs):
    n = len(parts)
    p = [jax.new_ref(a, memory_space=pltpu.MemorySpace.HBM) for a in parts]
    out = [jax.new_ref(b, memory_space=pltpu.MemorySpace.HBM) for b in bufs]

    @pl.kernel(mesh=plsc.ScalarSubcoreMesh(axis_name="sequencer", num_cores=1), name=name,
               scratch_types=tuple(_dma_sems(3 * n, 3 * n)),
               compiler_params=pltpu.CompilerParams(collective_id=collective_id))
    def launch(send, recv):
        barrier = pltpu.get_barrier_semaphore()
        for peer in [_chip_peer(dx, dy) for dx, dy in _CHIP_FLIPS]:
            pl.semaphore_signal(barrier, inc=1, device_id=peer, device_id_type=MESH)
        pl.semaphore_wait(barrier, 3)
        _alltoall_exchange(p, out, send, recv)

    launch()
    return [r[...] for r in out]


def _alltoall_exchange(p, out, send, recv):
    x, y, c = _me()
    me = 2 * x + y
    peers = [_chip_peer(dx, dy) for dx, dy in _CHIP_FLIPS]
    for t in range(len(p)):
        h = p[t].shape[1]
        nch = _chunks(h, 16)
        ch = h // nch
        for k, peer in enumerate(peers):
            pchip = 2 * peer[0] + peer[1]
            for i in range(nch):
                rows = pl.ds(i * ch, ch)
                _rcopy(p[t].at[pchip, rows], out[t].at[me, rows], send.at[3 * t + k], recv.at[3 * t + k], peer).start()
    for t in range(len(p)):
        for k, peer in enumerate(peers):
            pchip = 2 * peer[0] + peer[1]
            _rcopy(out[t].at[pchip], out[t].at[pchip], send.at[3 * t + k], recv.at[3 * t + k], peer).wait()


def _rs_swap(name, halves):
    n = len(halves)

    def body(*refs):
        a, out = refs[:n], refs[n:2 * n]
        send, recv = refs[2 * n:]
        x, y, c = _me()
        sib = (x, y, 1 - c)
        for t in range(n):
            ch = a[t].shape[0] // _NCH
            for i in range(_NCH):
                rows = pl.ds(i * ch, ch)
                _rcopy(a[t].at[rows], out[t].at[rows], send.at[t], recv.at[t], sib).start()
        for t in range(n):
            _rcopy(a[t], out[t], send.at[t], recv.at[t], sib).wait()

    return pl.pallas_call(
        body, name=name, in_specs=[_ANY] * n, out_specs=[_ANY] * n,
        out_shape=[jax.ShapeDtypeStruct(s.shape, s.dtype) for s in halves],
        scratch_shapes=_dma_sems(n, n))(*halves)


def _sibling_merge(name, a):
    P_, rh, C = a.shape

    def body(a_ref, out_ref, send_sem, recv_sem, local_sem):
        x, y, c = _me()
        local = pltpu.make_async_copy(a_ref, out_ref.at[:, pl.ds(c * rh, rh)], local_sem)
        local.start()
        cp = _rcopy(a_ref, out_ref.at[:, pl.ds(c * rh, rh)], send_sem, recv_sem, (x, y, 1 - c))
        cp.start()
        cp.wait_send()
        _rcopy(a_ref, out_ref.at[:, pl.ds((1 - c) * rh, rh)], send_sem, recv_sem, (x, y, 1 - c)).wait_recv()
        local.wait()

    return pl.pallas_call(
        body, name=name, in_specs=[_ANY], out_specs=_ANY, out_shape=jax.ShapeDtypeStruct((P_, 2 * rh, C), a.dtype),
        scratch_shapes=[pltpu.SemaphoreType.DMA(()), pltpu.SemaphoreType.DMA(()), pltpu.SemaphoreType.DMA(())])(a)


def _allgather8(name, a):
    g4 = _allgather4(name + "_chips", a)
    both = _sibling_merge(name + "_cores", g4.reshape(1, 4 * a.shape[0], a.shape[1]))
    return jnp.transpose(both.reshape(2, 4, *a.shape), (1, 0, 2, 3)).reshape(8, *a.shape)


def _sum_slots(name, a, out_dtype):
    def fn(a):
        acc = a[0].astype(F32)
        for k in range(1, a.shape[0]):
            acc = acc + a[k].astype(F32)
        return acc
    return _rowwise(name, fn, [a], [], [(a.shape[2], out_dtype)])[0]


def _adamw_math(w, g, m, v):
    m = ADAM_B1 * m + (1.0 - ADAM_B1) * g
    v = ADAM_B2 * v + (1.0 - ADAM_B2) * (g * g)
    m_hat = m / (1.0 - ADAM_B1 ** ADAM_STEP)
    v_hat = v / (1.0 - ADAM_B2 ** ADAM_STEP)
    return -ADAM_LR * (m_hat / (jnp.sqrt(v_hat) + ADAM_EPS) + ADAM_WD * w), m, v


def _adamw_piece(name, w2, m2, v2, mine, theirs, row0, prev, core):
    R, C = w2.shape
    h = mine.shape[0]
    tr = _slot_tile(h, 256)
    nb = h // tr
    assert row0 % tr == 0
    first = row0 // tr

    def body(c_ref, w_ref, m_ref, v_ref, a_ref, b_ref, *rest):
        g_ref, d_ref, nm_ref, nv_ref = rest[-4:]
        g = jnp.where(pl.program_id(0) == c_ref[0], a_ref[...], b_ref[...])
        g_ref[...] = g
        d_ref[...], nm_ref[...], nv_ref[...] = _adamw_math(w_ref[...], g, m_ref[...], v_ref[...])

    full = pl.BlockSpec((tr, C), lambda s, i, c_ref: (first + s * nb + i, 0))
    mine_spec = pl.BlockSpec((tr, C), lambda s, i, c_ref: (jnp.where(s == c_ref[0], i, 0), 0))
    theirs_spec = pl.BlockSpec((tr, C), lambda s, i, c_ref: (jnp.where(s == c_ref[0], 0, i), 0))
    extra = [] if prev is None else list(prev)
    grid_spec = pltpu.PrefetchScalarGridSpec(
        num_scalar_prefetch=1, grid=(2, nb), in_specs=[full, full, full, mine_spec, theirs_spec] + [_ANY] * len(extra),
        out_specs=[full] * 4)
    return pl.pallas_call(
        body, name=name, grid_spec=grid_spec, out_shape=[jax.ShapeDtypeStruct((R, C), F32)] * 4,
        input_output_aliases={6 + k: k for k in range(len(extra))}, compiler_params=_params(2))(core, w2, m2, v2, mine, theirs, *extra)


def _adamw(name, w, g, m, v):
    shape = w.shape
    two_d = (-1, shape[-1]) if w.ndim > 1 else (1, -1)
    w2, g2, m2, v2 = [t.reshape(two_d) for t in (w, g, m, v)]
    rows = w2.shape[0]
    tr = rows
    for cand in (256, 128, 64, 32, 16, 8):
        if rows % cand == 0:
            tr = cand
            break

    c = w2.shape[1]
    outs = _rowwise(name, _adamw_math, [w2, g2, m2, v2], [], [(c, F32)] * 3, tr=tr)
    return [o.reshape(shape) for o in outs]


_WEIGHT_ORDER = ("ada_w", "ada_b", "norm_mix_g", "norm_ffn_g", "gdn_w_in", "gdn_conv_w", "gdn_a_log", "gdn_dt_bias",
                 "gdn_norm_g", "gdn_w_out", "mla_w_in", "mla_q_norm_g", "mla_kv_norm_g", "mla_w_uq", "mla_w_ukv",
                 "mla_w_out", "ffn_w_gate", "ffn_w_up", "ffn_w_down", "final_norm_g")
_BIG = (("gdn_w_in", 2), ("gdn_w_out", 1), ("mla_w_in", 1), ("mla_w_uq", 2), ("mla_w_ukv", 2), ("mla_w_out", 1),
        ("ffn_w_gate", 2), ("ffn_w_up", 2), ("ffn_w_down", 1))
_SMALL_SHARDED = (("gdn_conv_w", 1), ("mla_q_norm_g", 1), ("mla_kv_norm_g", 1))
_STORED_TRANSPOSED = ("ffn_w_gate", "ffn_w_up")


def _size(shape):
    n = 1
    for s in shape:
        n *= s
    return n


def _pack_rows_each(tensors):
    parts, offs, off = [], [], 0
    for t in tensors:
        flat = t.reshape(-1).astype(F32)
        rows = -(-flat.shape[0] // PACK_W)
        parts.append(jnp.pad(flat, (0, rows * PACK_W - flat.shape[0])).reshape(rows, PACK_W))
        offs.append(off)
        off += rows
    total = -(-off // 16) * 16
    pack = jnp.pad(parts[0], ((offs[0], total - offs[0] - parts[0].shape[0]), (0, 0)))
    for p, o in zip(parts[1:], offs[1:]):
        pack = pack + jnp.pad(p, ((o, total - o - p.shape[0]), (0, 0)))
    return pack, offs


def _unpack_rows_each(pack, shapes):
    lead = pack.shape[:-2]
    out, off = [], 0
    for shp in shapes:
        n = _size(shp)
        rows = -(-n // PACK_W)
        out.append(pack[..., off:off + rows, :].reshape(*lead, -1)[..., :n].reshape(*lead, *shp))
        off += rows
    return out


def _merge_chips(stacked, axis):
    moved = jnp.moveaxis(stacked, 0, axis)
    shp = list(moved.shape)
    return moved.reshape(shp[:axis] + [shp[axis] * shp[axis + 1]] + shp[axis + 2:])


def _my_shard(full, axis, chip):
    n = full.shape[axis] // 4
    return lax.dynamic_slice_in_dim(full, chip * n, n, axis)


def kernel(x, c, positions, ada_w, ada_b, norm_mix_g, norm_ffn_g, gdn_w_in, gdn_conv_w, gdn_a_log, gdn_dt_bias, gdn_norm_g, gdn_w_out, mla_w_in, mla_q_norm_g, mla_kv_norm_g, mla_w_uq, mla_w_ukv, mla_w_out, ffn_w_gate, ffn_w_up, ffn_w_down, final_norm_g, loss_target, m_ada_w, m_ada_b, m_norm_mix_g, m_norm_ffn_g, m_gdn_w_in, m_gdn_conv_w, m_gdn_a_log, m_gdn_dt_bias, m_gdn_norm_g, m_gdn_w_out, m_mla_w_in, m_mla_q_norm_g, m_mla_kv_norm_g, m_mla_w_uq, m_mla_w_ukv, m_mla_w_out, m_ffn_w_gate, m_ffn_w_up, m_ffn_w_down, m_final_norm_g, v_ada_w, v_ada_b, v_norm_mix_g, v_norm_ffn_g, v_gdn_w_in, v_gdn_conv_w, v_gdn_a_log, v_gdn_dt_bias, v_gdn_norm_g, v_gdn_w_out, v_mla_w_in, v_mla_q_norm_g, v_mla_kv_norm_g, v_mla_w_uq, v_mla_w_ukv, v_mla_w_out, v_ffn_w_gate, v_ffn_w_up, v_ffn_w_down, v_final_norm_g):
    w = dict(ada_w=ada_w, ada_b=ada_b, norm_mix_g=norm_mix_g, norm_ffn_g=norm_ffn_g, gdn_w_in=gdn_w_in, gdn_conv_w=gdn_conv_w,
             gdn_a_log=gdn_a_log, gdn_dt_bias=gdn_dt_bias, gdn_norm_g=gdn_norm_g, gdn_w_out=gdn_w_out, mla_w_in=mla_w_in,
             mla_q_norm_g=mla_q_norm_g, mla_kv_norm_g=mla_kv_norm_g, mla_w_uq=mla_w_uq, mla_w_ukv=mla_w_ukv,
             mla_w_out=mla_w_out, ffn_w_gate=ffn_w_gate, ffn_w_up=ffn_w_up, ffn_w_down=ffn_w_down, final_norm_g=final_norm_g)
    m = dict(ada_w=m_ada_w, ada_b=m_ada_b, norm_mix_g=m_norm_mix_g, norm_ffn_g=m_norm_ffn_g, gdn_w_in=m_gdn_w_in,
             gdn_conv_w=m_gdn_conv_w, gdn_a_log=m_gdn_a_log, gdn_dt_bias=m_gdn_dt_bias, gdn_norm_g=m_gdn_norm_g,
             gdn_w_out=m_gdn_w_out, mla_w_in=m_mla_w_in, mla_q_norm_g=m_mla_q_norm_g, mla_kv_norm_g=m_mla_kv_norm_g,
             mla_w_uq=m_mla_w_uq, mla_w_ukv=m_mla_w_ukv, mla_w_out=m_mla_w_out, ffn_w_gate=m_ffn_w_gate,
             ffn_w_up=m_ffn_w_up, ffn_w_down=m_ffn_w_down, final_norm_g=m_final_norm_g)
    v = dict(ada_w=v_ada_w, ada_b=v_ada_b, norm_mix_g=v_norm_mix_g, norm_ffn_g=v_norm_ffn_g, gdn_w_in=v_gdn_w_in,
             gdn_conv_w=v_gdn_conv_w, gdn_a_log=v_gdn_a_log, gdn_dt_bias=v_gdn_dt_bias, gdn_norm_g=v_gdn_norm_g,
             gdn_w_out=v_gdn_w_out, mla_w_in=v_mla_w_in, mla_q_norm_g=v_mla_q_norm_g, mla_kv_norm_g=v_mla_kv_norm_g,
             mla_w_uq=v_mla_w_uq, mla_w_ukv=v_mla_w_ukv, mla_w_out=v_mla_w_out, ffn_w_gate=v_ffn_w_gate,
             ffn_w_up=v_ffn_w_up, ffn_w_down=v_ffn_w_down, final_norm_g=v_final_norm_g)
    T = x.shape[1]
    ix, iy, ic = _me()
    chip = 2 * ix + iy
    seq = 2 * chip + ic
    n_dev = 8

    small_shapes = [w[n].shape for n, _ in _SMALL_SHARDED] + [c.shape]
    pack0, _ = _pack_rows_each([w[n] for n, _ in _SMALL_SHARDED] + [c])
    got0 = _unpack_rows_each(_allgather8("gather_small", pack0), small_shapes)
    small_full = {n: _merge_chips(g[0::2], ax) for (n, ax), g in zip(_SMALL_SHARDED, got0)}
    c_all = got0[-1].reshape(n_dev, D)

    big = [n for n, _ in _BIG]
    chip_arr = chip.astype(jnp.int32).reshape(1)

    def stored(n, t):
        return jnp.swapaxes(t, 1, 2) if n in _STORED_TRANSPOSED else t

    ws, ms, vs = [{n: stored(n, d[n]) for n in big} for d in (w, m, v)]
    two_d = lambda t: t.reshape(-1, t.shape[-1])

    gathered = []
    for l in range(DEPTH):
        names = _layer_weights(l)
        bufs = [_cast_into_slot(f"to_bf16_{n}{l}", two_d(ws[n]), chip_arr, j * ws[n].shape[1], ws[n].shape[1]) for n, j in names]
        filled = _gather_weights("gather_weights0", bufs) if l == 0 else _gather_weights_async(f"gather_weights{l}", l, bufs)
        gathered.append({n: b for (n, _), b in zip(names, filled)})

    def weights_of(l, h):
        return _weights_to_kernel(l, gathered[l])

    P = _small_to_kernel(norm_mix_g, norm_ffn_g, final_norm_g, small_full["gdn_conv_w"], gdn_a_log, gdn_dt_bias,
                         gdn_norm_g, small_full["mla_q_norm_g"], small_full["mla_kv_norm_g"])

    c16 = jnp.pad(c_all, ((0, 16 - n_dev), (0, 0)))
    ca = _rowwise("cond_silu", lambda t: t * _sig(t), [c16], [], [(D, BF16)])[0]
    n_ada = ada_w.shape[2]
    mods = jnp.concatenate([_mm(f"ada_fwd{l}", ca, ada_w[l], "nn") for l in range(DEPTH)], axis=0)
    mods_all = _allgather4("gather_mod", mods).reshape(4, DEPTH, 16, n_ada)
    mod_mm = jnp.transpose(lax.dynamic_index_in_dim(mods_all, seq, axis=2, keepdims=False), (1, 0, 2)).reshape(DEPTH, 4 * n_ada)
    mod = _rowwise("mod_bias", lambda a, b: a + b, [mod_mm, ada_b], [], [(4 * n_ada, F32)])[0]

    core_chip = jnp.stack([ic, chip]).astype(jnp.int32)
    pending, in_flight = {}, []

    def reduce_group(layer, part, pieces):
        pending.update({(n, layer if n.startswith("ffn_") else layer // 2): g for n, g in pieces.items()})
        if part == "ffn" and layer > 0:
            return
        keys = list(pending)
        glist = [pending.pop(k) for k in keys]
        tag = f"{layer}{part}"
        theirs = _rs_split("grads_cores_" + tag, glist)
        both = [_pair_add(f"grads_pair_{n}{l}", g, t, core_chip) for (n, l), g, t in zip(keys, glist, theirs)]
        swapped = _rs_alltoall_async("grads_chips_" + tag, DEPTH + 1 + len(in_flight), [p for p, _ in both], [o for _, o in both])
        in_flight.append((tag, keys, swapped))

    dx, dmod, gP = _local_step(x.reshape(T, D), loss_target.reshape(T, D), positions.reshape(T, 1), mod, weights_of, P, reduce_group)

    partials = [dmod, jnp.concatenate(gP["norm_mix_g"]), jnp.concatenate(gP["norm_ffn_g"]), gP["final_g"],
                jnp.stack([jnp.transpose(g) for g in gP["gdn_cw"]]), jnp.concatenate(gP["gdn_alog"])[:, :NH],
                jnp.concatenate(gP["gdn_dtb"])[:, :NH], jnp.concatenate(gP["gdn_ng"]), jnp.concatenate(gP["mla_qg"]),
                jnp.concatenate(gP["mla_kvg"]), gP["loss"][:, :1]]
    part_shapes = [p.shape for p in partials]
    ppack, _ = _pack_rows_each(partials)
    pall = _allgather8("gather_partials", ppack)
    psum = _sum_slots("sum_partials", pall, F32)
    (g_ada_b, g_norm_mix, g_norm_ffn, g_final, g_conv_full, g_alog, g_dtb, g_gdn_ng, g_qg_full, g_kvg_full,
     loss_sum) = _unpack_rows_each(psum, part_shapes)
    dmod_all = _unpack_rows_each(pall, part_shapes[:1])[0]

    grads = dict(ada_b=g_ada_b, norm_mix_g=g_norm_mix, norm_ffn_g=g_norm_ffn, final_norm_g=g_final.reshape(D),
                 gdn_conv_w=_my_shard(g_conv_full, 1, chip), gdn_a_log=g_alog, gdn_dt_bias=g_dtb, gdn_norm_g=g_gdn_ng,
                 mla_q_norm_g=_my_shard(g_qg_full, 1, chip), mla_kv_norm_g=_my_shard(g_kvg_full, 1, chip))

    ca_t = jnp.zeros((D, LANES), BF16).at[:, :16].set(jnp.transpose(ca))
    dm_mine = lax.dynamic_slice_in_dim(dmod_all, chip * n_ada, n_ada, axis=2)
    grads["ada_w"] = jnp.stack([
        _mm(f"ada_bwd{l}", ca_t, jnp.pad(dm_mine[:, l], ((0, LANES - n_dev), (0, 0))), "nn") for l in range(DEPTH)])

    delta, new_m, new_v = {}, {}, {}
    results = {}
    keys = [k for _, ks, _ in in_flight for k in ks]
    halves = [_sum_slots(f"grads_sum_{n}{l}", s, F32) for _, ks, sw in in_flight for (n, l), s in zip(ks, sw)]
    others = _rs_swap("grads_swap", halves)
    for (n, l), mine, theirs in zip(keys, halves, others):
        results[n] = _adamw_piece(f"adamw_{n}{l}", two_d(ws[n]), two_d(ms[n]), two_d(vs[n]), mine, theirs,
                                  l * ws[n].shape[1], results.get(n), core_chip[:1])
    for n in big:
        grads[n], delta[n], new_m[n], new_v[n] = [stored(n, t.reshape(ws[n].shape)) for t in results[n]]
    delta["ada_w"], new_m["ada_w"], new_v["ada_w"] = _adamw("adamw_ada_w", ada_w, grads["ada_w"], m_ada_w, v_ada_w)
    for n in [n for n in _WEIGHT_ORDER if n not in delta]:
        delta[n], new_m[n], new_v[n] = _adamw("adamw_" + n, w[n], grads[n], m[n], v[n])

    loss = loss_sum.reshape(())
    return (loss, dx.reshape(1, T, D), *[grads[n] for n in _WEIGHT_ORDER], *[delta[n] for n in _WEIGHT_ORDER],
            *[new_m[n] for n in _WEIGHT_ORDER], *[new_v[n] for n in _WEIGHT_ORDER])
```

```python
import functools

import jax
import jax.numpy as jnp
from jax import lax
from jax.experimental import pallas as pl
from jax.experimental.pallas import tpu as pltpu
from jax.experimental.pallas import tpu_sc as plsc

F32 = jnp.float32
BF16 = jnp.bfloat16
HI = lax.Precision.HIGHEST
MESH = pl.DeviceIdType.MESH

D = 1024
DEPTH = 4
N_MOD = 6
NH = 8
HD = 128
CHUNK = 64
_GDN_HB = 8
GDN_QKV = 3 * NH * HD
GDN_INK = GDN_QKV + NH * HD + 2 * HD
Q_RANK, KV_RANK, ROPE = 384, 256, 64
MLA_INK = Q_RANK + KV_RANK + HD
DFF = 2816
EPS = 1e-6
ATT_SCALE = (HD + ROPE) ** -0.5
ROPE_THETA = 10000.0
LANES = 128
PACK_W = 1024

ADAM_LR, ADAM_B1, ADAM_B2, ADAM_EPS, ADAM_WD, ADAM_STEP = 0.001, 0.9, 0.999, 1e-08, 0.01, 10


H3 = "bf16x3"
B1 = "bf16"
HS = H3
HF = B1


def _dot(a, b, mode="nn", prec=None):
    dn = {"nn": (((1,), (0,)), ((), ())), "nt": (((1,), (1,)), ((), ())), "tn": (((0,), (0,)), ((), ()))}[mode]
    if prec == B1:
        return _dot(a.astype(BF16), b.astype(BF16), mode)
    if prec == H3:
        ah, bh = a.astype(BF16), b.astype(BF16)
        al, bl = (a - ah.astype(F32)).astype(BF16), (b - bh.astype(F32)).astype(BF16)
        return _dot(ah, bh, mode) + (_dot(ah, bl, mode) + _dot(al, bh, mode))
    return lax.dot_general(a, b, dn, precision=prec, preferred_element_type=F32)


def _sig(x):
    return 1.0 / (1.0 + jnp.exp(-x))


def _pick(n, cap):
    if n <= cap:
        return n
    best = None
    for d in range(LANES, cap + 1, LANES):
        if n % d == 0:
            best = d
    assert best is not None, (n, cap)
    return best


def _params(n_grid):
    return pltpu.CompilerParams(dimension_semantics=("arbitrary",) * n_grid, vmem_limit_bytes=56 * 1024 * 1024)


def _rowwise(name, fn, rows, consts, outs, sums=(), tr=256):
    first = rows[0][0] if isinstance(rows[0], tuple) else rows[0]
    T = first.shape[-2]
    tr = _slot_tile(T, tr)
    nr, nc, no, ns = len(rows), len(consts), len(outs), len(sums)

    windows = [c[1:] if isinstance(c, tuple) else None for c in consts]
    consts = [c[0] if isinstance(c, tuple) else c for c in consts]

    def body(*refs):
        vals = [r[...] for r in refs[:nr]]
        for r, win in zip(refs[nr:nr + nc], windows):
            vals.append(r[...] if win is None else r[win[0]:win[0] + 1, win[1] * win[2]:(win[1] + 1) * win[2]])
        res = fn(*vals)
        if not isinstance(res, (tuple, list)):
            res = (res,)
        o_refs = refs[nr + nc:nr + nc + no]
        s_refs = refs[nr + nc + no:]
        for r, val in zip(o_refs, res[:no]):
            r[...] = val.astype(r.dtype)
        if ns:
            @pl.when(pl.program_id(0) == 0)
            def _():
                for r in s_refs:
                    r[...] = jnp.zeros_like(r)
            for r, val in zip(s_refs, res[no:]):
                r[...] += val

    in_specs, args = [], []
    for a in rows:
        if isinstance(a, tuple):
            arr, width, cb = a
            in_specs.append(pl.BlockSpec((tr, width), lambda i, cb=cb: (i, cb)))
            args.append(arr)
        elif a.ndim == 3:
            in_specs.append(pl.BlockSpec((a.shape[0], tr, a.shape[2]), lambda i: (0, i, 0)))
            args.append(a)
        else:
            in_specs.append(pl.BlockSpec((tr, a.shape[1]), lambda i: (i, 0)))
            args.append(a)
    for a in consts:
        in_specs.append(pl.BlockSpec(a.shape, lambda i, nd=a.ndim: (0,) * nd))
        args.append(a)
    out_specs = [pl.BlockSpec((tr, w), lambda i: (i, 0)) for w, _ in outs]
    out_specs += [pl.BlockSpec((1, w), lambda i: (0, 0)) for w in sums]
    out_shape = [jax.ShapeDtypeStruct((T, w), dt) for w, dt in outs]
    out_shape += [jax.ShapeDtypeStruct((1, w), F32) for w in sums]
    res = pl.pallas_call(body, name=name, grid=(T // tr,), in_specs=in_specs, out_specs=out_specs,
                         out_shape=out_shape, compiler_params=_params(1))(*args)
    return res


def _mm(name, a, b, mode, out_dtype=F32, tm=512, tn=1024):
    if mode == "tn":
        K, M = a.shape
    else:
        M, K = a.shape
    N = b.shape[0] if mode == "nt" else b.shape[1]
    tm, tn = _pick(M, tm), _pick(N, tn)

    def body(a_ref, b_ref, o_ref):
        o_ref[...] = _dot(a_ref[...].astype(BF16), b_ref[...].astype(BF16), mode).astype(o_ref.dtype)

    a_spec = pl.BlockSpec((K, tm), lambda i, j: (0, i)) if mode == "tn" else pl.BlockSpec((tm, K), lambda i, j: (i, 0))
    b_spec = pl.BlockSpec((tn, K), lambda i, j: (j, 0)) if mode == "nt" else pl.BlockSpec((K, tn), lambda i, j: (0, j))
    return pl.pallas_call(body, name=name, grid=(M // tm, N // tn), in_specs=[a_spec, b_spec],
                          out_specs=pl.BlockSpec((tm, tn), lambda i, j: (i, j)),
                          out_shape=jax.ShapeDtypeStruct((M, N), out_dtype), compiler_params=_params(2))(a, b)


def _rms(x, eps=EPS):
    return lax.rsqrt(jnp.mean(x * x, axis=-1, keepdims=True) + eps)


def _norm_mod_fwd(name, x, g, scale, shift):
    def fn(x, g, scale, shift):
        return x * _rms(x) * g * (1.0 + scale) + shift
    return _rowwise(name, fn, [x], [g, scale, shift], [(D, BF16)])[0]


def _norm_mod_bwd(name, dh, x, dx_res, g, scale):
    def fn(dh, x, dx_res, g, scale):
        r = _rms(x)
        xh = x * r
        dxh = dh * (g * (1.0 + scale))
        dx = r * (dxh - xh * jnp.mean(dxh * xh, axis=-1, keepdims=True))
        dhx = dh * xh
        return (dx_res + dx, jnp.sum(dh, axis=0, keepdims=True), jnp.sum(dhx * g, axis=0, keepdims=True),
                jnp.sum(dhx * (1.0 + scale), axis=0, keepdims=True))
    return _rowwise(name, fn, [dh, x, dx_res], [g, scale], [(D, F32)], sums=[D, D, D])


def _residual_fwd(name, x, y, gate):
    def fn(x, y, gate):
        return x + gate * y
    return _rowwise(name, fn, [x, y], [gate], [(D, F32)])[0]


def _residual_norm_fwd(name, x, y, gate, g, scale, shift):
    def fn(x, y, gate, g, scale, shift):
        x = x + gate * y
        return x, x * _rms(x) * g * (1.0 + scale) + shift
    return _rowwise(name, fn, [x, y], [gate, g, scale, shift], [(D, F32), (D, BF16)])


def _norm_residual_bwd(name, dh, x, dx_res, g, scale, y, gate):
    def fn(dh, x, dx_res, y, g, scale, gate):
        r = _rms(x)
        xh = x * r
        dxh = dh * (g * (1.0 + scale))
        dx = dx_res + r * (dxh - xh * jnp.mean(dxh * xh, axis=-1, keepdims=True))
        dhx = dh * xh
        return (dx, dx * gate, jnp.sum(dh, axis=0, keepdims=True), jnp.sum(dhx * g, axis=0, keepdims=True),
                jnp.sum(dhx * (1.0 + scale), axis=0, keepdims=True), jnp.sum(dx * y, axis=0, keepdims=True))
    return _rowwise(name, fn, [dh, x, dx_res, y], [g, scale, gate], [(D, F32), (D, BF16)], sums=[D, D, D, D])


def _residual_bwd(name, dx, y, gate):
    def fn(dx, y, gate):
        return dx * gate, jnp.sum(dx * y, axis=0, keepdims=True)
    return _rowwise(name, fn, [dx, y], [gate], [(D, BF16)], sums=[D])


def _loss_head(x, target, g):
    def fn(x, t, g):
        r = _rms(x)
        xh = x * r
        err = xh * g - t
        loss = 0.5 * jnp.sum(jnp.mean(err * err, axis=-1, keepdims=True), axis=0, keepdims=True)
        dy = err * (1.0 / D)
        dxh = dy * g
        dx = r * (dxh - xh * jnp.mean(dxh * xh, axis=-1, keepdims=True))
        return dx, jnp.broadcast_to(loss, (1, LANES)), jnp.sum(dy * xh, axis=0, keepdims=True)
    return _rowwise("loss_head", fn, [x, target], [g], [(D, F32)], sums=[LANES, D])


def _ffn_up(name, h, wg, wu, layer, tm=1024):
    T, n = h.shape[0], wg.shape[1]
    tm = min(tm, T)

    def body(h_ref, wg_ref, wu_ref, a_ref, b_ref, s_ref):
        h = h_ref[...]
        a = _dot(h, wg_ref[0], "nt")
        b = _dot(h, wu_ref[0], "nt")
        a_ref[0] = a.astype(a_ref.dtype)
        b_ref[0] = b.astype(b_ref.dtype)
        s_ref[0] = (a * _sig(a) * b).astype(s_ref.dtype)

    wspec = pl.BlockSpec((1, n, D), lambda ch, i: (ch, layer, 0))
    ospec = pl.BlockSpec((1, tm, n), lambda ch, i: (ch, i, 0))
    return pl.pallas_call(
        body, name=name, grid=(4, T // tm), in_specs=[pl.BlockSpec((tm, D), lambda ch, i: (i, 0)), wspec, wspec],
        out_specs=[ospec, ospec, ospec],
        out_shape=[jax.ShapeDtypeStruct((4, T, n), BF16)] * 3, compiler_params=_params(2))(h, wg, wu)


def _ffn_down(name, s, wd, layer, tm=1024):
    _, T, n = s.shape
    tm = min(tm, T)

    def body(s_ref, w_ref, y_ref):
        @pl.when(pl.program_id(1) == 0)
        def _():
            y_ref[...] = jnp.zeros_like(y_ref)
        y_ref[...] += _dot(s_ref[0], w_ref[0], "nn")

    return pl.pallas_call(
        body, name=name, grid=(T // tm, 4),
        in_specs=[pl.BlockSpec((1, tm, n), lambda i, ch: (ch, i, 0)), pl.BlockSpec((1, n, D), lambda i, ch: (ch, layer, 0))],
        out_specs=pl.BlockSpec((tm, D), lambda i, ch: (i, 0)), out_shape=jax.ShapeDtypeStruct((T, D), F32),
        compiler_params=_params(2))(s, wd)


def _ffn_down_bwd(name, dy, wd, a, b, layer, tm=1024):
    _, T, n = a.shape
    tm = min(tm, T)

    def body(dy_ref, w_ref, a_ref, b_ref, da_ref, db_ref):
        ds = _dot(dy_ref[...], w_ref[0], "nt")
        a, b = a_ref[0].astype(F32), b_ref[0].astype(F32)
        sg = _sig(a)
        da_ref[0] = (ds * b * (sg * (1.0 + a * (1.0 - sg)))).astype(da_ref.dtype)
        db_ref[0] = (ds * (a * sg)).astype(db_ref.dtype)

    bspec = pl.BlockSpec((1, tm, n), lambda ch, i: (ch, i, 0))
    return pl.pallas_call(
        body, name=name, grid=(4, T // tm),
        in_specs=[pl.BlockSpec((tm, D), lambda ch, i: (i, 0)), pl.BlockSpec((1, n, D), lambda ch, i: (ch, layer, 0)), bspec, bspec],
        out_specs=[bspec, bspec], out_shape=[jax.ShapeDtypeStruct((4, T, n), BF16)] * 2,
        compiler_params=_params(2))(dy, wd, a, b)


def _ffn_down_dw(name, s, dy):
    _, T, n = s.shape

    def body(s_ref, dy_ref, o_ref):
        o_ref[0] = _dot(s_ref[0], dy_ref[...], "tn").astype(o_ref.dtype)

    return pl.pallas_call(
        body, name=name, grid=(4,),
        in_specs=[pl.BlockSpec((1, T, n), lambda ch: (ch, 0, 0)), pl.BlockSpec((T, D), lambda ch: (0, 0))],
        out_specs=pl.BlockSpec((1, n, D), lambda ch: (ch, 0, 0)), out_shape=jax.ShapeDtypeStruct((4, n, D), BF16),
        compiler_params=_params(1))(s, dy)


def _ffn_up_dw(name, h, da, db, tm=512):
    _, T, n = da.shape

    def body(h_ref, da_ref, db_ref, dg_ref, du_ref):
        h = h_ref[...]
        dg_ref[0] = _dot(da_ref[0], h, "tn").astype(dg_ref.dtype)
        du_ref[0] = _dot(db_ref[0], h, "tn").astype(du_ref.dtype)

    dspec = pl.BlockSpec((1, T, n), lambda ch, j: (ch, 0, 0))
    ospec = pl.BlockSpec((1, n, tm), lambda ch, j: (ch, 0, j))
    return pl.pallas_call(
        body, name=name, grid=(4, D // tm), in_specs=[pl.BlockSpec((T, tm), lambda ch, j: (0, j)), dspec, dspec],
        out_specs=[ospec, ospec], out_shape=[jax.ShapeDtypeStruct((4, n, D), BF16)] * 2,
        compiler_params=_params(2))(h, da, db)


def _ffn_up_dx(name, da, db, wg, wu, layer, tm=1024):
    _, T, n = da.shape
    tm = min(tm, T)

    def body(da_ref, db_ref, wg_ref, wu_ref, o_ref):
        @pl.when(pl.program_id(1) == 0)
        def _():
            o_ref[...] = jnp.zeros_like(o_ref)
        o_ref[...] += _dot(da_ref[0], wg_ref[0], "nn") + _dot(db_ref[0], wu_ref[0], "nn")

    dspec = pl.BlockSpec((1, tm, n), lambda i, ch: (ch, i, 0))
    wspec = pl.BlockSpec((1, n, D), lambda i, ch: (ch, layer, 0))
    return pl.pallas_call(
        body, name=name, grid=(T // tm, 4), in_specs=[dspec, dspec, wspec, wspec],
        out_specs=pl.BlockSpec((tm, D), lambda i, ch: (i, 0)), out_shape=jax.ShapeDtypeStruct((T, D), F32),
        compiler_params=_params(2))(da, db, wg, wu)


def _shift_down(x, k):
    if k == 0:
        return x
    rows = lax.broadcasted_iota(jnp.int32, x.shape, 0)
    return jnp.where(rows >= k, pltpu.roll(x, k, 0), 0.0)


def _shift_up(x, k):
    if k == 0:
        return x
    T = x.shape[0]
    rows = lax.broadcasted_iota(jnp.int32, x.shape, 0)
    return jnp.where(rows < T - k, pltpu.roll(x, T - k, 0), 0.0)


def _conv_silu(x, w):
    c = w[0:1, :] * _shift_down(x, 3) + w[1:2, :] * _shift_down(x, 2) + w[2:3, :] * _shift_down(x, 1) + w[3:4, :] * x
    sg = _sig(c)
    return c, sg, c * sg


def _gdn_conv_fwd(name, proj, cw):
    T = proj.shape[0]

    def body(x_ref, w_ref, o_ref):
        j = pl.program_id(0)
        _, _, y = _conv_silu(x_ref[...], w_ref[...])
        r = lax.rsqrt(jnp.sum(y * y, axis=1, keepdims=True) + EPS)
        mult = jnp.where(j < NH, HD ** -0.5, 1.0)
        o_ref[...] = jnp.where(j < 2 * NH, y * (r * mult), y)

    return pl.pallas_call(body, name=name, grid=(3 * NH,),
                          in_specs=[pl.BlockSpec((T, HD), lambda j: (0, j)), pl.BlockSpec((4, HD), lambda j: (0, j))],
                          out_specs=pl.BlockSpec((T, HD), lambda j: (0, j)),
                          out_shape=jax.ShapeDtypeStruct((T, GDN_QKV), F32), compiler_params=_params(1))(proj, cw)


def _gdn_conv_bwd(name, proj, cw, dz):
    T = proj.shape[0]

    def body(x_ref, w_ref, dz_ref, dx_ref, dw_ref):
        j = pl.program_id(0)
        x, w, dz = x_ref[...], w_ref[...], dz_ref[...]
        c, sg, y = _conv_silu(x, w)
        r = lax.rsqrt(jnp.sum(y * y, axis=1, keepdims=True) + EPS)
        mult = jnp.where(j < NH, HD ** -0.5, 1.0)
        dyn = mult * (r * dz - (r * r * r) * y * jnp.sum(dz * y, axis=1, keepdims=True))
        dy = jnp.where(j < 2 * NH, dyn, dz)
        dc = dy * (sg * (1.0 + c * (1.0 - sg)))
        dx = w[0:1, :] * _shift_up(dc, 3) + w[1:2, :] * _shift_up(dc, 2) + w[2:3, :] * _shift_up(dc, 1) + w[3:4, :] * dc
        dx_ref[...] = dx.astype(dx_ref.dtype)
        for k in range(4):
            dw_ref[pl.ds(k, 1), :] = jnp.sum(dc * _shift_down(x, 3 - k), axis=0, keepdims=True)

    return pl.pallas_call(body, name=name, grid=(3 * NH,),
                          in_specs=[pl.BlockSpec((T, HD), lambda j: (0, j)), pl.BlockSpec((4, HD), lambda j: (0, j)),
                                    pl.BlockSpec((T, HD), lambda j: (0, j))],
                          out_specs=[pl.BlockSpec((T, HD), lambda j: (0, j)), pl.BlockSpec((4, HD), lambda j: (0, j))],
                          out_shape=[jax.ShapeDtypeStruct((T, GDN_QKV), BF16), jax.ShapeDtypeStruct((4, GDN_QKV), F32)],
                          compiler_params=_params(1))(proj, cw, dz)


def _softplus(z):
    return jnp.maximum(z, 0.0) + jnp.log(1.0 + jnp.exp(-jnp.abs(z)))


_AB_CB = GDN_INK // (2 * HD) - 1


def _gdn_gates_fwd(name, proj, alog, dtb):
    def fn(ab, alog, dtb):
        a, b = ab[:, :HD], ab[:, HD:]
        return -jnp.exp(alog) * _softplus(a + dtb), _sig(b)
    return _rowwise(name, fn, [(proj, 2 * HD, _AB_CB)], [alog, dtb], [(HD, F32), (HD, F32)])


def _gdn_gates_bwd(name, proj, dg_h, db_h, alog, dtb):
    def fn(ab, dg_h, db_h, alog, dtb):
        lane = lax.broadcasted_iota(jnp.int32, (1, HD), 1)
        dg = jnp.zeros(dg_h.shape[1:], F32)
        dbeta = jnp.zeros(dg_h.shape[1:], F32)
        for h in range(NH):
            oh = (lane == h).astype(F32)
            dg = dg + dg_h[h] * oh
            dbeta = dbeta + db_h[h] * oh
        a, b = ab[:, :HD], ab[:, HD:]
        z = a + dtb
        ea = jnp.exp(alog)
        beta = _sig(b)
        da = dg * (-ea) * _sig(z)
        db = dbeta * beta * (1.0 - beta)
        return (jnp.concatenate([da, db], axis=1), jnp.sum(dg * (-ea * _softplus(z)), axis=0, keepdims=True),
                jnp.sum(da, axis=0, keepdims=True))
    return _rowwise(name, fn, [(proj, 2 * HD, _AB_CB), dg_h, db_h], [alog, dtb], [(2 * HD, BF16)], sums=[HD, HD])


def _interleave(gens):
    gens = list(gens)
    results = [None] * len(gens)
    active = list(range(len(gens)))
    while active:
        for i in list(active):
            try:
                next(gens[i])
            except StopIteration as stop:
                results[i] = stop.value
                active.remove(i)
    return results


def _chunk_common(q, k, v, gblk, bblk, h, prec):
    C = CHUNK
    lane = lax.broadcasted_iota(jnp.int32, (1, HD), 1)
    oh = (lane == h).astype(F32)
    g_col = jnp.sum(gblk * oh, axis=1, keepdims=True)
    beta = jnp.sum(bblk * oh, axis=1, keepdims=True)
    ri = lax.broadcasted_iota(jnp.int32, (C, C), 0)
    ci = lax.broadcasted_iota(jnp.int32, (C, C), 1)
    incl = ri >= ci
    strict = ri > ci
    eye = (ri == ci).astype(F32)
    gcb = _dot(incl.astype(F32), jnp.broadcast_to(g_col, (C, HD)), "nn", HI)
    yield
    gc = gcb[:, :C]
    gc_row = _dot(jnp.ones((C, C), F32), eye * gc, "nn", HI)
    yield
    decay = jnp.where(incl, jnp.exp(jnp.where(incl, gc - gc_row, 0.0)), 0.0)
    rows = lax.broadcasted_iota(jnp.int32, (C, HD), 0)
    gclb = jnp.sum(jnp.where(rows == C - 1, gcb, 0.0), axis=0, keepdims=True)
    eg = jnp.exp(gcb)
    egl = jnp.exp(gclb - gcb)
    gl = jnp.exp(gclb)
    kb = k * beta
    m1 = _dot(kb, k, "nt", prec)
    qk = _dot(q, k, "nt", prec)
    yield
    L = jnp.where(strict, m1 * decay, 0.0)
    nl = -L
    tinv = eye + nl
    p = nl
    for _ in range(5):
        p = _dot(p, p, "nn", H3)
        yield
        tinv = tinv + _dot(tinv, p, "nn", H3)
    vb = v * beta
    kbg = kb * eg
    yield
    u = _dot(tinv, vb, "nn", prec)
    w = _dot(tinv, kbg, "nn", prec)
    yield
    attn = jnp.where(incl, qk * decay, 0.0)
    return dict(beta=beta, incl=incl, strict=strict, decay=decay, eg=eg, egl=egl, gl=gl, kb=kb, m1=m1, tinv=tinv,
                kbg=kbg, u=u, w=w, qk=qk, attn=attn, q_dec=q * eg, k_dec=k * egl, rows=rows, oh=oh)


def _gdn_chunk_fwd(name, qkv, g, beta):
    T = qkv.shape[0]
    N = T // CHUNK

    hb = _GDN_HB
    w = hb * HD

    def body(q_ref, k_ref, v_ref, g_ref, b_ref, o_ref, st_ref, S):
        hg, n = pl.program_id(0), pl.program_id(1)

        @pl.when(n == 0)
        def _():
            S[...] = jnp.zeros_like(S)

        gblk, bblk = g_ref[...], b_ref[...]

        def one_head(i, q, k, v, s):
            c = yield from _chunk_common(q, k, v, gblk, bblk, hg * hb + i, HF)
            v_new = c["u"] - _dot(c["w"], s, "nn", HF)
            qs = _dot(c["q_dec"], s, "nn", HF)
            yield
            o = qs + _dot(c["attn"], v_new, "nn", HF)
            return o, s * c["gl"] + _dot(c["k_dec"], v_new, "tn", HF)

        sls = [slice(i * HD, (i + 1) * HD) for i in range(hb)]
        states = [S[i] for i in range(hb)]
        res = _interleave(one_head(i, q_ref[:, sls[i]], k_ref[:, sls[i]], v_ref[:, sls[i]], states[i]) for i in range(hb))
        for i, (o, s_new) in enumerate(res):
            st_ref[i, 0] = states[i]
            o_ref[:, sls[i]] = o
            S[i] = s_new

    blk = lambda off: pl.BlockSpec((CHUNK, w), lambda h, n, off=off: (n, off + h))
    gspec = pl.BlockSpec((CHUNK, HD), lambda h, n: (n, 0))
    return pl.pallas_call(
        body, name=name, grid=(NH // hb, N), in_specs=[blk(0), blk(NH // hb), blk(2 * NH // hb), gspec, gspec],
        out_specs=[pl.BlockSpec((CHUNK, w), lambda h, n: (n, h)), pl.BlockSpec((hb, 1, HD, HD), lambda h, n: (h, n, 0, 0))],
        out_shape=[jax.ShapeDtypeStruct((T, NH * HD), F32), jax.ShapeDtypeStruct((NH, N, HD, HD), F32)],
        scratch_shapes=[pltpu.VMEM((hb, HD, HD), F32)], compiler_params=_params(2))(qkv, qkv, qkv, g, beta)


def _gdn_chunk_bwd(name, qkv, g, beta, states, do):
    T = qkv.shape[0]
    N = T // CHUNK
    C = CHUNK

    hb = _GDN_HB
    w = hb * HD
    assert hb == NH

    def body(q_ref, k_ref, v_ref, g_ref, b_ref, st_ref, do_ref, dqkv_ref, dg_ref, db_ref, dS):
        hg, n = pl.program_id(0), pl.program_id(1)

        @pl.when(n == 0)
        def _():
            dS[...] = jnp.zeros_like(dS)

        gblk, bblk = g_ref[...], b_ref[...]
        sls = [slice(i * HD, (i + 1) * HD) for i in range(hb)]
        res = _interleave(one_head(hg * hb + i, gblk, bblk, q_ref[:, sls[i]], k_ref[:, sls[i]], v_ref[:, sls[i]],
                                   st_ref[i, 0], do_ref[:, sls[i]], dS[i]) for i in range(hb))
        for i, (dq, dk, dv, dg, db, ds_new) in enumerate(res):
            dqkv_ref[:, sls[i]] = dq
            dqkv_ref[:, slice(w + i * HD, w + (i + 1) * HD)] = dk
            dqkv_ref[:, slice(2 * w + i * HD, 2 * w + (i + 1) * HD)] = dv
            dg_ref[i] = dg
            db_ref[i] = db
            dS[i] = ds_new

    def one_head(h, gblk, bblk, q, k, v, s, do, ds):
        c = yield from _chunk_common(q, k, v, gblk, bblk, h, HF)
        eg, egl, gl, beta, decay, tinv = c["eg"], c["egl"], c["gl"], c["beta"], c["decay"], c["tinv"]
        v_new = c["u"] - _dot(c["w"], s, "nn", HF)
        dq_dec = _dot(do, s, "nt", HF)
        yield
        dv_new = _dot(c["attn"], do, "tn", HF) + _dot(c["k_dec"], ds, "nn", HF)
        dk_dec = _dot(v_new, ds, "nt", HF)
        dgl = jnp.sum(jnp.sum(s * ds, axis=1, keepdims=True), axis=0, keepdims=True)
        yield
        ds_new = ds * gl + _dot(c["q_dec"], do, "tn", HF) - _dot(c["w"], dv_new, "tn", HF)
        dattn = jnp.where(c["incl"], _dot(do, v_new, "nt", HF), 0.0)
        dw = -_dot(dv_new, s, "nt", HF)
        yield
        dvb = _dot(tinv, dv_new, "tn", HS)
        dkbg = _dot(tinv, dw, "tn", HS)
        yield
        dA = -(_dot(dvb, c["u"], "nt", HS) + _dot(dkbg, c["w"], "nt", HS))
        yield
        dL = jnp.where(c["strict"], dA, 0.0)
        dm1 = dL * decay
        dqk = dattn * decay
        xdec = (dL * c["m1"] + dattn * c["qk"]) * decay
        dkb = _dot(dm1, k, "nn", HS) + dkbg * eg
        dk = _dot(dm1, c["kb"], "tn", HF) + _dot(dqk, q, "tn", HF) + dk_dec * egl + dkb * beta
        dq = _dot(dqk, k, "nn", HF) + dq_dec * eg
        yield
        dkd_kd = jnp.sum(dk_dec * c["k_dec"], axis=1, keepdims=True)
        dgc = (jnp.sum(xdec, axis=1, keepdims=True) - _dot(xdec, jnp.ones((C, HD), F32), "tn", HS)
               + jnp.sum(dq_dec * c["q_dec"], axis=1, keepdims=True) - dkd_kd
               + jnp.sum(dkbg * c["kbg"], axis=1, keepdims=True))
        dgcl = jnp.sum(dkd_kd, axis=0, keepdims=True) + dgl * gl
        dgc = dgc + jnp.where(c["rows"] == C - 1, dgcl, 0.0)
        ri = lax.broadcasted_iota(jnp.int32, (C, C), 0)
        ci = lax.broadcasted_iota(jnp.int32, (C, C), 1)
        dg = _dot((ci >= ri).astype(F32), dgc, "nn", HI)
        db = jnp.broadcast_to(jnp.sum(dkb * k, axis=1, keepdims=True) + jnp.sum(dvb * v, axis=1, keepdims=True), (C, HD))
        return dq, dk, dvb * beta, dg, db, ds_new

    blk = lambda off: pl.BlockSpec((C, w), lambda h, n, off=off: (N - 1 - n, off + h))
    gspec = pl.BlockSpec((C, HD), lambda h, n: (N - 1 - n, 0))
    ospec = pl.BlockSpec((C, w), lambda h, n: (N - 1 - n, h))
    hspec = pl.BlockSpec((hb, C, HD), lambda h, n: (h, N - 1 - n, 0))
    return pl.pallas_call(
        body, name=name, grid=(NH // hb, N),
        in_specs=[blk(0), blk(NH // hb), blk(2 * NH // hb), gspec, gspec,
                  pl.BlockSpec((hb, 1, HD, HD), lambda h, n: (h, N - 1 - n, 0, 0)), ospec],
        out_specs=[pl.BlockSpec((C, 3 * w), lambda h, n: (N - 1 - n, 0)), hspec, hspec],
        out_shape=[jax.ShapeDtypeStruct((T, 3 * NH * HD), F32)] + [jax.ShapeDtypeStruct((NH, T, HD), F32)] * 2,
        scratch_shapes=[pltpu.VMEM((hb, HD, HD), F32)], compiler_params=_params(2))(qkv, qkv, qkv, g, beta, states, do)


_GATE_CB = GDN_QKV // (NH * HD)


def _gdn_gated_norm_fwd(name, o, proj, ng):
    def fn(o, gate, ng):
        outs = []
        for h in range(NH):
            sl = slice(h * HD, (h + 1) * HD)
            oh, gh = o[:, sl], gate[:, sl]
            outs.append(oh * _rms(oh) * ng * (gh * _sig(gh)))
        return jnp.concatenate(outs, axis=1)
    return _rowwise(name, fn, [o, (proj, NH * HD, _GATE_CB)], [ng], [(NH * HD, BF16)])[0]


def _gdn_gated_norm_bwd(name, don, o, proj, ng):
    def fn(don, o, gate, ng):
        dos, dgs = [], []
        dng = jnp.zeros((1, HD), F32)
        for h in range(NH):
            sl = slice(h * HD, (h + 1) * HD)
            oh, gh, dh = o[:, sl], gate[:, sl], don[:, sl]
            r = _rms(oh)
            xh = oh * r
            sg = _sig(gh)
            dn = dh * (gh * sg)
            dgs.append(dh * (xh * ng) * (sg * (1.0 + gh * (1.0 - sg))))
            dng = dng + jnp.sum(dn * xh, axis=0, keepdims=True)
            dxh = dn * ng
            dos.append(r * (dxh - xh * jnp.mean(dxh * xh, axis=-1, keepdims=True)))
        return jnp.concatenate(dos, axis=1), jnp.concatenate(dgs, axis=1), dng
    return _rowwise(name, fn, [don, o, (proj, NH * HD, _GATE_CB)], [ng], [(NH * HD, F32), (NH * HD, BF16)], sums=[HD])


def _rot(x):
    lane = lax.broadcasted_iota(jnp.int32, x.shape, 1)
    return jnp.where(lane < ROPE // 2, -pltpu.roll(x, HD - ROPE // 2, 1), pltpu.roll(x, ROPE // 2, 1))


def _rot_t(x):
    lane = lax.broadcasted_iota(jnp.int32, x.shape, 1)
    return jnp.where(lane < ROPE // 2, pltpu.roll(x, HD - ROPE // 2, 1), -pltpu.roll(x, ROPE // 2, 1))


def _rope_tables(pos_col):
    lane = jnp.arange(HD)
    inv_freq = ROPE_THETA ** (-(2.0 * (lane % (ROPE // 2)).astype(F32)) / ROPE)
    inv_freq = jnp.where(lane < ROPE, inv_freq, 0.0).astype(F32)[None, :]
    valid = (lane < ROPE).astype(F32)[None, :]

    def fn(pos, inv_freq, valid):
        ang = pos.astype(F32) * inv_freq
        return jnp.cos(ang) * valid, jnp.sin(ang) * valid
    return _rowwise("rope_tables", fn, [pos_col], [inv_freq, valid], [(HD, F32), (HD, F32)])


def _mla_pre_fwd(name, proj, cos, sin, qg, kvg):
    def fn(p, cos, sin, qg, kvg):
        cq, ckv, kr = p[:, :Q_RANK], p[:, Q_RANK:Q_RANK + KV_RANK], p[:, Q_RANK + KV_RANK:]
        return cq * _rms(cq) * qg, ckv * _rms(ckv) * kvg, kr * cos + _rot(kr) * sin
    return _rowwise(name, fn, [proj, cos, sin], [qg, kvg], [(Q_RANK, BF16), (KV_RANK, BF16), (HD, BF16)])


def _rms_bwd(dy, x, g):
    r = _rms(x)
    xh = x * r
    dxh = dy * g
    return r * (dxh - xh * jnp.mean(dxh * xh, axis=-1, keepdims=True)), jnp.sum(dy * xh, axis=0, keepdims=True)


def _mla_pre_bwd(name, proj, dcqn, dckvn, dkr, cos, sin, qg, kvg):
    def fn(p, dcqn, dckvn, dkr, cos, sin, qg, kvg):
        cq, ckv = p[:, :Q_RANK], p[:, Q_RANK:Q_RANK + KV_RANK]
        dcq, dqg = _rms_bwd(dcqn, cq, qg)
        dckv, dkvg = _rms_bwd(dckvn, ckv, kvg)
        dkr_pre = dkr * cos + _rot_t(dkr * sin)
        return jnp.concatenate([dcq, dckv, dkr_pre], axis=1), dqg, dkvg
    return _rowwise(name, fn, [proj, dcqn, dckvn, dkr, cos, sin], [qg, kvg], [(MLA_INK, BF16)], sums=[Q_RANK, KV_RANK])


def _mla_q_fwd(name, q, cos, sin):
    def fn(qn, qr, cos, sin):
        outs = []
        for h in range(NH):
            x = qr[:, h * HD:(h + 1) * HD]
            outs.append(x * cos + _rot(x) * sin)
        return qn, jnp.concatenate(outs, axis=1)
    return _rowwise(name, fn, [(q, NH * HD, 0), (q, NH * HD, 1), cos, sin], [], [(NH * HD, BF16), (NH * HD, BF16)])


def _mla_q_bwd(name, dqn, dqr, cos, sin):
    def fn(dqn, dqr, cos, sin):
        outs = [dqn]
        for h in range(NH):
            z = dqr[:, h * HD:(h + 1) * HD]
            outs.append(z * cos + _rot_t(z * sin))
        return jnp.concatenate(outs, axis=1)
    return _rowwise(name, fn, [dqn, dqr, cos, sin], [], [(2 * NH * HD, BF16)])[0]


def _att_probs(qn, qr, kn, kr, row0):
    s = (_dot(qn, kn, "nt") + _dot(qr, kr, "nt")) * ATT_SCALE
    qpos = row0 + lax.broadcasted_iota(jnp.int32, s.shape, 0)
    kpos = lax.broadcasted_iota(jnp.int32, s.shape, 1)
    s = jnp.where(kpos <= qpos, s, -1e30)
    e = jnp.exp(s - jnp.max(s, axis=1, keepdims=True))
    return e, 1.0 / jnp.sum(e, axis=1, keepdims=True)


def _mla_attn_fwd(name, qn, qr, kv, kr, tq=256):
    T = qn.shape[0]
    tq = min(tq, T)

    def body(qn_ref, qr_ref, kn_ref, v_ref, kr_ref, o_ref):
        i = pl.program_id(1)
        for blk in range(T // tq):
            @pl.when(i == blk)
            def _(blk=blk):
                keys = pl.ds(0, (blk + 1) * tq)
                e, inv_l = _att_probs(qn_ref[...], qr_ref[...], kn_ref[keys, :], kr_ref[keys, :], blk * tq)
                o_ref[...] = (_dot(e.astype(BF16), v_ref[keys, :], "nn") * inv_l).astype(o_ref.dtype)

    qspec = pl.BlockSpec((tq, HD), lambda h, i: (i, h))
    return pl.pallas_call(
        body, name=name, grid=(NH, T // tq),
        in_specs=[qspec, qspec, pl.BlockSpec((T, HD), lambda h, i: (0, h)), pl.BlockSpec((T, HD), lambda h, i: (0, NH + h)),
                  pl.BlockSpec((T, HD), lambda h, i: (0, 0))],
        out_specs=qspec, out_shape=jax.ShapeDtypeStruct((T, NH * HD), BF16), compiler_params=_params(2))(qn, qr, kv, kv, kr)


def _mla_attn_bwd(name, qn, qr, kv, kr, do, o, tq=256):
    T = qn.shape[0]
    tq = min(tq, T)

    def body(qn_ref, qr_ref, kn_ref, v_ref, kr_ref, do_ref, o_ref, dqn_ref, dqr_ref, dkn_ref, dv_ref, dkr_ref):
        h, i = pl.program_id(0), pl.program_id(1)

        @pl.when(i == 0)
        def _():
            dkn_ref[...] = jnp.zeros_like(dkn_ref)
            dv_ref[...] = jnp.zeros_like(dv_ref)

        @pl.when((i == 0) & (h == 0))
        def _():
            dkr_ref[...] = jnp.zeros_like(dkr_ref)

        for blk in range(T // tq):
            @pl.when(i == blk)
            def _(blk=blk):
                keys = pl.ds(0, (blk + 1) * tq)
                qn, qr, do = qn_ref[...], qr_ref[...], do_ref[...]
                kn, kr, v = kn_ref[keys, :], kr_ref[keys, :], v_ref[keys, :]
                e, inv_l = _att_probs(qn, qr, kn, kr, blk * tq)
                dp = _dot(do, v, "nt")
                delta = jnp.sum(do.astype(F32) * o_ref[...].astype(F32), axis=1, keepdims=True)
                ds = (e * ((dp - delta) * (inv_l * ATT_SCALE))).astype(BF16)
                dqn_ref[...] = _dot(ds, kn, "nn")
                dqr_ref[...] = _dot(ds, kr, "nn")
                dkn_ref[keys, :] += _dot(ds, qn, "tn")
                dkr_ref[keys, :] += _dot(ds, qr, "tn")
                dv_ref[keys, :] += _dot(e.astype(BF16), (do.astype(F32) * inv_l).astype(BF16), "tn")

    qspec = pl.BlockSpec((tq, HD), lambda h, i: (i, h))
    kspec = pl.BlockSpec((T, HD), lambda h, i: (0, h))
    return pl.pallas_call(
        body, name=name, grid=(NH, T // tq),
        in_specs=[qspec, qspec, kspec, pl.BlockSpec((T, HD), lambda h, i: (0, NH + h)),
                  pl.BlockSpec((T, HD), lambda h, i: (0, 0)), qspec, qspec],
        out_specs=[qspec, qspec, kspec, kspec, pl.BlockSpec((T, HD), lambda h, i: (0, 0))],
        out_shape=[jax.ShapeDtypeStruct((T, NH * HD), F32)] * 4 + [jax.ShapeDtypeStruct((T, HD), F32)],
        compiler_params=_params(2))(qn, qr, kv, kv, kr, do, o)


def _mod_rows(mod, layer):
    return [(mod, layer, i, D) for i in range(N_MOD)]


def _local_step(x, target, pos_col, mod, weights_of, P, on_grads):
    cos, sin = _rope_tables(pos_col)
    saved = []
    sh_m, sc_m = _mod_rows(mod, 0)[:2]
    h = _norm_mod_fwd("norm_mix0", x, (P["norm_mix_g"], 0, 0, D), sc_m, sh_m)
    for l in range(DEPTH):
        j = l // 2
        sh_m, sc_m, ga_m, sh_f, sc_f, ga_f = _mod_rows(mod, l)
        s = dict(x0=x)
        W = weights_of(l, h)
        s.update(h=h, W=W)
        if l % 2 == 0:
            proj = _mm(f"gdn_in{j}", h, W["gdn_in"], "nn", tn=GDN_INK // 2)
            qkv = _gdn_conv_fwd(f"gdn_conv{j}", proj, P["gdn_cw"][j])
            g, beta = _gdn_gates_fwd(f"gdn_gates{j}", proj, P["gdn_alog"][j], P["gdn_dtb"][j])
            o, states = _gdn_chunk_fwd(f"gdn_chunk{j}", qkv, g, beta)
            on = _gdn_gated_norm_fwd(f"gdn_gnorm{j}", o, proj, P["gdn_ng"][j])
            y = _mm(f"gdn_out{j}", on, W["gdn_out"], "nn")
            s.update(proj=proj, qkv=qkv, g=g, beta=beta, o=o, states=states, on=on)
        else:
            proj = _mm(f"mla_in{j}", h, W["mla_in"], "nn")
            cqn, ckvn, kr = _mla_pre_fwd(f"mla_pre{j}", proj, cos, sin, P["mla_qg"][j], P["mla_kvg"][j])
            q = _mm(f"mla_uq{j}", cqn, W["mla_uq"], "nn")
            kv = _mm(f"mla_ukv{j}", ckvn, W["mla_ukv"], "nn", out_dtype=BF16)
            qn, qr = _mla_q_fwd(f"mla_q{j}", q, cos, sin)
            o = _mla_attn_fwd(f"mla_attn{j}", qn, qr, kv, kr)
            y = _mm(f"mla_out{j}", o, W["mla_out"], "nn")
            s.update(proj=proj, cqn=cqn, ckvn=ckvn, kr=kr, kv=kv, qn=qn, qr=qr, o=o)
        s["y"] = y
        x, h2 = _residual_norm_fwd(f"res_mix{l}", x, y, ga_m, (P["norm_ffn_g"], l, 0, D), sc_f, sh_f)
        s["x1"] = x
        fa, fb, sw = _ffn_up(f"ffn_up{l}", h2, W["ffn_g"], W["ffn_u"], 0)
        yf = _ffn_down(f"ffn_down{l}", sw, W["ffn_d"], 0)
        if l + 1 < DEPTH:
            sh_n, sc_n = _mod_rows(mod, l + 1)[:2]
            x, h = _residual_norm_fwd(f"res_ffn{l}", x, yf, ga_f, (P["norm_mix_g"], l + 1, 0, D), sc_n, sh_n)
        else:
            x = _residual_fwd(f"res_ffn{l}", x, yf, ga_f)
        s.update(h2=h2, fa=fa, fb=fb, sw=sw, yf=yf)
        saved.append(s)

    dx, loss, d_final = _loss_head(x, target, P["final_g"])
    gP = dict(loss=loss, final_g=d_final, norm_mix_g=[None] * DEPTH, norm_ffn_g=[None] * DEPTH,
              gdn_cw=[None] * 2, gdn_alog=[None] * 2, gdn_dtb=[None] * 2, gdn_ng=[None] * 2,
              mla_qg=[None] * 2, mla_kvg=[None] * 2)
    dmod = [None] * DEPTH
    dyf, d_ga_f = _residual_bwd(f"res_ffn_b{DEPTH - 1}", dx, saved[-1]["yf"], _mod_rows(mod, DEPTH - 1)[5])
    for l in reversed(range(DEPTH)):
        j = l // 2
        s = saved[l]
        W = s["W"]
        sh_m, sc_m, ga_m, sh_f, sc_f, ga_f = _mod_rows(mod, l)
        da, db = _ffn_down_bwd(f"ffn_down_dx{l}", dyf, W["ffn_d"], s["fa"], s["fb"], 0)
        g_down = _ffn_down_dw(f"ffn_down_dw{l}", s["sw"], dyf)
        g_gate, g_up = _ffn_up_dw(f"ffn_up_dw{l}", s["h2"], da, db)
        on_grads(l, "ffn", dict(ffn_w_gate=g_gate, ffn_w_up=g_up, ffn_w_down=g_down))
        dh2 = _ffn_up_dx(f"ffn_up_dx{l}", da, db, W["ffn_g"], W["ffn_u"], 0)
        dx, dy, d_sh_f, d_sc_f, gP["norm_ffn_g"][l], d_ga_m = _norm_residual_bwd(
            f"norm_ffn_b{l}", dh2, s["x1"], dx, (P["norm_ffn_g"], l, 0, D), sc_f, s["y"], ga_m)
        if l % 2 == 0:
            don = _mm(f"gdn_out_dx{j}", dy, W["gdn_out"], "nt")
            g_out = _mm(f"gdn_out_dw{j}", s["on"], dy, "tn", out_dtype=BF16)
            do, dgate, gP["gdn_ng"][j] = _gdn_gated_norm_bwd(f"gdn_gnorm_b{j}", don, s["o"], s["proj"], P["gdn_ng"][j])
            dqkv, dg_h, db_h = _gdn_chunk_bwd(f"gdn_chunk_b{j}", s["qkv"], s["g"], s["beta"], s["states"], do)
            dab_, gP["gdn_alog"][j], gP["gdn_dtb"][j] = _gdn_gates_bwd(f"gdn_gates_b{j}", s["proj"], dg_h, db_h,
                                                                        P["gdn_alog"][j], P["gdn_dtb"][j])
            dpre, gP["gdn_cw"][j] = _gdn_conv_bwd(f"gdn_conv_b{j}", s["proj"], P["gdn_cw"][j], dqkv)
            dproj = jnp.concatenate([dpre, dgate, dab_], axis=1)
            g_in = _mm(f"gdn_in_dw{j}", s["h"], dproj, "tn", out_dtype=BF16, tn=GDN_INK // 2)
            on_grads(l, "mix", dict(gdn_w_in=_uncols(_gdn_in_from_kernel(g_in)), gdn_w_out=_unrows(g_out)))
            dh = _mm(f"gdn_in_dx{j}", dproj, W["gdn_in"], "nt")
        else:
            do = _mm(f"mla_out_dx{j}", dy, W["mla_out"], "nt", out_dtype=BF16)
            g_out = _mm(f"mla_out_dw{j}", s["o"], dy, "tn", out_dtype=BF16)
            dqn, dqr, dkn, dv, dkr = _mla_attn_bwd(f"mla_attn_b{j}", s["qn"], s["qr"], s["kv"], s["kr"], do, s["o"])
            dq = _mla_q_bwd(f"mla_q_b{j}", dqn, dqr, cos, sin)
            dkv = jnp.concatenate([dkn, dv], axis=1)
            g_uq = _mm(f"mla_uq_dw{j}", s["cqn"], dq, "tn", out_dtype=BF16)
            dcqn = _mm(f"mla_uq_dx{j}", dq, W["mla_uq"], "nt")
            g_ukv = _mm(f"mla_ukv_dw{j}", s["ckvn"], dkv, "tn", out_dtype=BF16)
            dckvn = _mm(f"mla_ukv_dx{j}", dkv, W["mla_ukv"], "nt")
            dproj, gP["mla_qg"][j], gP["mla_kvg"][j] = _mla_pre_bwd(f"mla_pre_b{j}", s["proj"], dcqn, dckvn, dkr, cos, sin,
                                                                     P["mla_qg"][j], P["mla_kvg"][j])
            g_in = _mm(f"mla_in_dw{j}", s["h"], dproj, "tn", out_dtype=BF16)
            on_grads(l, "mix", dict(mla_w_in=_unrows(g_in[:, :Q_RANK + KV_RANK + ROPE]), mla_w_uq=_uncols(_mla_uq_from_kernel(g_uq)),
                                    mla_w_ukv=_uncols(_mla_ukv_from_kernel(g_ukv)), mla_w_out=_unrows(g_out)))
            dh = _mm(f"mla_in_dx{j}", dproj, W["mla_in"], "nt")
        if l > 0:
            dx, dyf_prev, d_sh_m, d_sc_m, gP["norm_mix_g"][l], d_ga_f_prev = _norm_residual_bwd(
                f"norm_mix_b{l}", dh, s["x0"], dx, (P["norm_mix_g"], l, 0, D), sc_m, saved[l - 1]["yf"], _mod_rows(mod, l - 1)[5])
        else:
            dx, d_sh_m, d_sc_m, gP["norm_mix_g"][l] = _norm_mod_bwd(f"norm_mix_b{l}", dh, s["x0"], dx,
                                                                     (P["norm_mix_g"], l, 0, D), sc_m)
        dmod[l] = jnp.concatenate([d_sh_m, d_sc_m, d_ga_m, d_sh_f, d_sc_f, d_ga_f], axis=1)
        if l > 0:
            dyf, d_ga_f = dyf_prev, d_ga_f_prev
    return dx, jnp.concatenate(dmod, axis=0), gP


def _pad_cols(a, width):
    return jnp.pad(a, ((0, 0), (0, width - a.shape[1])))


def _gdn_in_to_kernel(w):
    m = GDN_QKV + NH * HD
    return jnp.concatenate([w[:, :m], _pad_cols(w[:, m:m + NH], HD), _pad_cols(w[:, m + NH:], HD)], axis=1)


def _gdn_in_from_kernel(g):
    m = GDN_QKV + NH * HD
    return jnp.concatenate([g[:, :m], g[:, m:m + NH], g[:, m + HD:m + HD + NH]], axis=1)


def _mla_uq_to_kernel(w):
    w3 = w.reshape(Q_RANK, NH, HD + ROPE)
    rope = jnp.pad(w3[:, :, HD:], ((0, 0), (0, 0), (0, HD - ROPE)))
    return jnp.concatenate([w3[:, :, :HD].reshape(Q_RANK, NH * HD), rope.reshape(Q_RANK, NH * HD)], axis=1)


def _mla_uq_from_kernel(g):
    gn = g[:, :NH * HD].reshape(Q_RANK, NH, HD)
    gr = g[:, NH * HD:].reshape(Q_RANK, NH, HD)[:, :, :ROPE]
    return jnp.concatenate([gn, gr], axis=2).reshape(Q_RANK, NH * (HD + ROPE))


def _mla_ukv_to_kernel(w):
    w3 = w.reshape(KV_RANK, NH, 2 * HD)
    return jnp.concatenate([w3[:, :, :HD].reshape(KV_RANK, NH * HD), w3[:, :, HD:].reshape(KV_RANK, NH * HD)], axis=1)


def _mla_ukv_from_kernel(g):
    gk = g[:, :NH * HD].reshape(KV_RANK, NH, HD)
    gv = g[:, NH * HD:].reshape(KV_RANK, NH, HD)
    return jnp.concatenate([gk, gv], axis=2).reshape(KV_RANK, NH * 2 * HD)


def _cols(t):
    return jnp.moveaxis(t, 0, 1).reshape(t.shape[1], -1)


def _uncols(g):
    return jnp.moveaxis(g.reshape(g.shape[0], 4, -1), 1, 0)


def _rows(t):
    return t.reshape(-1, t.shape[2])


def _unrows(g):
    return g.reshape(4, -1, g.shape[1])


def _layer_weights(layer):
    mixer = ("gdn_w_in", "gdn_w_out") if layer % 2 == 0 else ("mla_w_in", "mla_w_uq", "mla_w_ukv", "mla_w_out")
    return [(n, layer // 2) for n in mixer] + [(n, layer) for n in ("ffn_w_gate", "ffn_w_up", "ffn_w_down")]


def _weights_to_kernel(layer, g):
    out = dict(ffn_g=g["ffn_w_gate"], ffn_u=g["ffn_w_up"], ffn_d=g["ffn_w_down"])
    if layer % 2 == 0:
        out.update(gdn_in=_gdn_in_to_kernel(_cols(g["gdn_w_in"])), gdn_out=_rows(g["gdn_w_out"]))
    else:
        out.update(mla_in=_pad_cols(_rows(g["mla_w_in"]), MLA_INK), mla_uq=_mla_uq_to_kernel(_cols(g["mla_w_uq"])),
                   mla_ukv=_mla_ukv_to_kernel(_cols(g["mla_w_ukv"])), mla_out=_rows(g["mla_w_out"]))
    return out


def _small_to_kernel(norm_mix_g, norm_ffn_g, final_norm_g, gdn_conv_w, gdn_a_log, gdn_dt_bias, gdn_norm_g, q_norm_g, kv_norm_g):
    return dict(
        norm_mix_g=norm_mix_g, norm_ffn_g=norm_ffn_g, final_g=final_norm_g.reshape(1, D),
        gdn_cw=[jnp.transpose(gdn_conv_w[j]) for j in range(2)],
        gdn_alog=[_pad_cols(gdn_a_log[j:j + 1], HD) for j in range(2)],
        gdn_dtb=[_pad_cols(gdn_dt_bias[j:j + 1], HD) for j in range(2)],
        gdn_ng=[gdn_norm_g[j:j + 1] for j in range(2)],
        mla_qg=[q_norm_g[j:j + 1] for j in range(2)],
        mla_kvg=[kv_norm_g[j:j + 1] for j in range(2)],
    )


_CHIP_FLIPS = ((1, 0), (0, 1), (1, 1))
_ANY = pl.BlockSpec(memory_space=pl.ANY)


def _me():
    return lax.axis_index("x"), lax.axis_index("y"), lax.axis_index("c")


def _chip_peer(dx, dy):
    x, y, c = _me()
    return ((1 - x) if dx else x, (1 - y) if dy else y, c)


def _rcopy(src, dst, send_sem, recv_sem, to):
    return pltpu.make_async_remote_copy(src_ref=src, dst_ref=dst, send_sem=send_sem, recv_sem=recv_sem,
                                        device_id=to, device_id_type=MESH)


def _allgather4(name, a, halves=False):
    R, C = a.shape
    rh = R // 2 if halves else R

    def body(a_ref, out_ref, send_sems, recv_sems, local_sem):
        x, y, c = _me()
        me = 2 * x + y
        src = a_ref.at[pl.ds(c * rh, rh)] if halves else a_ref
        local = pltpu.make_async_copy(src, out_ref.at[me], local_sem)
        local.start()
        sends = []
        for k, (dx, dy) in enumerate(_CHIP_FLIPS):
            cp = _rcopy(src, out_ref.at[me], send_sems.at[k], recv_sems.at[k], _chip_peer(dx, dy))
            cp.start()
            sends.append(cp)
        for k, (dx, dy) in enumerate(_CHIP_FLIPS):
            px, py, _ = _chip_peer(dx, dy)
            _rcopy(src, out_ref.at[2 * px + py], send_sems.at[k], recv_sems.at[k], _chip_peer(dx, dy)).wait_recv()
        for cp in sends:
            cp.wait_send()
        local.wait()

    return pl.pallas_call(
        body, name=name, in_specs=[_ANY], out_specs=_ANY, out_shape=jax.ShapeDtypeStruct((4, rh, C), a.dtype),
        scratch_shapes=[pltpu.SemaphoreType.DMA((3,)), pltpu.SemaphoreType.DMA((3,)), pltpu.SemaphoreType.DMA(())])(a)


_NCH = 4


def _dma_sems(*counts):
    return [pltpu.SemaphoreType.DMA((n,)) for n in counts]


def _slot_tile(rows, cap=512):
    best = rows
    for tr in range(16, min(rows, cap) + 1, 16):
        if rows % tr == 0:
            best = tr
    return best


def _cast_into_slot(name, a, chip, row0, rows):
    C = a.shape[1]
    tr = _slot_tile(rows)
    assert row0 % tr == 0
    first = row0 // tr

    def body(c_ref, a_ref, o_ref):
        o_ref[0] = a_ref[...].astype(o_ref.dtype)

    grid_spec = pltpu.PrefetchScalarGridSpec(
        num_scalar_prefetch=1, grid=(rows // tr,), in_specs=[pl.BlockSpec((tr, C), lambda i, c_ref: (first + i, 0))],
        out_specs=pl.BlockSpec((1, tr, C), lambda i, c_ref: (c_ref[0], i, 0)))
    return pl.pallas_call(body, name=name, grid_spec=grid_spec, out_shape=jax.ShapeDtypeStruct((4, rows, C), BF16),
                          compiler_params=_params(1))(chip, a)


def _chunks(rows, align):
    for nch in (_NCH, 2):
        if rows % (nch * align) == 0:
            return nch
    return 1


def _gather_exchange(out, ici_s, ici_r, d2d_s, d2d_r):
    n = len(out)
    x, y, c = _me()
    me = 2 * x + y
    sib = (x, y, 1 - c)
    peers = [_chip_peer(dx, dy) for dx, dy in _CHIP_FLIPS]
    for t in range(n):
        h = out[t].shape[1] // 2
        nch = _chunks(h, 16)
        ch = h // nch
        for k, peer in enumerate(peers):
            for i in range(nch):
                blk = out[t].at[me, pl.ds(c * h + i * ch, ch)]
                _rcopy(blk, blk, ici_s.at[3 * t + k], ici_r.at[3 * t + k], peer).start()
    for t in range(n):
        h = out[t].shape[1] // 2
        nch = _chunks(h, 16)
        ch = h // nch
        for k, peer in enumerate(peers):
            pchip = 2 * peer[0] + peer[1]
            got = out[t].at[pchip, pl.ds(c * h, h)]
            _rcopy(got, got, ici_s.at[3 * t + k], ici_r.at[3 * t + k], peer).wait_recv()
            for i in range(nch):
                blk = out[t].at[pchip, pl.ds(c * h + i * ch, ch)]
                _rcopy(blk, blk, d2d_s.at[3 * t + k], d2d_r.at[3 * t + k], sib).start()
    for t in range(n):
        h = out[t].shape[1] // 2
        for k, peer in enumerate(peers):
            pchip = 2 * peer[0] + peer[1]
            other = out[t].at[pchip, pl.ds((1 - c) * h, h)]
            _rcopy(other, other, d2d_s.at[3 * t + k], d2d_r.at[3 * t + k], sib).wait_recv()
            _rcopy(other, other, ici_s.at[3 * t + k], ici_r.at[3 * t + k], peer).wait_send()
            _rcopy(other, other, d2d_s.at[3 * t + k], d2d_r.at[3 * t + k], sib).wait_send()


def _gather_weights(name, bufs):
    n = len(bufs)

    def body(*refs):
        _gather_exchange(refs[n:2 * n], *refs[2 * n:])

    return pl.pallas_call(
        body, name=name, in_specs=[_ANY] * n, out_specs=[_ANY] * n,
        out_shape=[jax.ShapeDtypeStruct(s.shape, s.dtype) for s in bufs],
        input_output_aliases={t: t for t in range(n)},
        scratch_shapes=_dma_sems(3 * n, 3 * n, 3 * n, 3 * n))(*bufs)


def _gather_weights_async(name, collective_id, bufs):
    n = len(bufs)
    refs = [jax.new_ref(b, memory_space=pltpu.MemorySpace.HBM) for b in bufs]

    @pl.kernel(mesh=plsc.ScalarSubcoreMesh(axis_name="sequencer", num_cores=1), name=name,
               scratch_types=tuple(_dma_sems(3 * n, 3 * n, 3 * n, 3 * n)),
               compiler_params=pltpu.CompilerParams(collective_id=collective_id))
    def launch(ici_s, ici_r, d2d_s, d2d_r):
        x, y, c = _me()
        barrier = pltpu.get_barrier_semaphore()
        for peer in [_chip_peer(dx, dy) for dx, dy in _CHIP_FLIPS] + [(x, y, 1 - c)]:
            pl.semaphore_signal(barrier, inc=1, device_id=peer, device_id_type=MESH)
        pl.semaphore_wait(barrier, 4)
        _gather_exchange(refs, ici_s, ici_r, d2d_s, d2d_r)

    launch()
    return [r[...] for r in refs]


def _rs_split(name, grads):
    n = len(grads)

    def body(*refs):
        g, out = refs[:n], refs[n:2 * n]
        send, recv = refs[2 * n:]
        x, y, c = _me()
        sib = (x, y, 1 - c)
        for t in range(n):
            h = g[t].shape[1] // 2
            for d in range(4):
                _rcopy(g[t].at[d, pl.ds((1 - c) * h, h)], out[t].at[d], send.at[t], recv.at[t], sib).start()
        for t in range(n):
            _rcopy(out[t], out[t], send.at[t], recv.at[t], sib).wait()

    return pl.pallas_call(
        body, name=name, in_specs=[_ANY] * n, out_specs=[_ANY] * n,
        out_shape=[jax.ShapeDtypeStruct((4, s.shape[1] // 2, s.shape[2]), s.dtype) for s in grads],
        scratch_shapes=_dma_sems(n, n))(*grads)


def _pair_add(name, g, theirs, core_chip):
    _, R, C = g.shape
    h = R // 2
    tr = _slot_tile(h)
    nb = h // tr

    def body(s_ref, g_ref, t_ref, p_ref, o_ref):
        val = (g_ref[...].astype(F32) + t_ref[...].astype(F32)).astype(p_ref.dtype)
        p_ref[...] = val

        @pl.when(pl.program_id(1) == s_ref[1])
        def _():
            o_ref[...] = val

    spec = pl.BlockSpec((1, tr, C), lambda i, d, s_ref: (d, i, 0))
    grid_spec = pltpu.PrefetchScalarGridSpec(
        num_scalar_prefetch=1, grid=(nb, 4),
        in_specs=[pl.BlockSpec((1, tr, C), lambda i, d, s_ref: (d, s_ref[0] * nb + i, 0)), spec],
        out_specs=[spec, pl.BlockSpec((1, tr, C), lambda i, d, s_ref: (s_ref[1], i, 0))])
    half = jax.ShapeDtypeStruct((4, h, C), BF16)
    return pl.pallas_call(body, name=name, grid_spec=grid_spec, out_shape=[half, half],
                          compiler_params=_params(2))(core_chip, g, theirs)


def _rs_alltoall_async(name, collective_id, parts, bufs):
    n = len(parts)
    p = [jax.new_ref(a, memory_space=pltpu.MemorySpace.HBM) for a in parts]
    out = [jax.new_ref(b, memory_space=pltpu.MemorySpace.HBM) for b in bufs]

    @pl.kernel(mesh=plsc.ScalarSubcoreMesh(axis_name="sequencer", num_cores=1), name=name,
               scratch_types=tuple(_dma_sems(3 * n, 3 * n)),
               compiler_params=pltpu.CompilerParams(collective_id=collective_id))
    def launch(send, recv):
        barrier = pltpu.get_barrier_semaphore()
        for peer in [_chip_peer(dx, dy) for dx, dy in _CHIP_FLIPS]:
            pl.semaphore_signal(barrier, inc=1, device_id=peer, device_id_type=MESH)
        pl.semaphore_wait(barrier, 3)
        _alltoall_exchange(p, out, send, recv)

    launch()
    return [r[...] for r in out]


def _alltoall_exchange(p, out, send, recv):
    x, y, c = _me()
    me = 2 * x + y
    peers = [_chip_peer(dx, dy) for dx, dy in _CHIP_FLIPS]
    for t in range(len(p)):
        h = p[t].shape[1]
        nch = _chunks(h, 16)
        ch = h // nch
        for k, peer in enumerate(peers):
            pchip = 2 * peer[0] + peer[1]
            for i in range(nch):
                rows = pl.ds(i * ch, ch)
                _rcopy(p[t].at[pchip, rows], out[t].at[me, rows], send.at[3 * t + k], recv.at[3 * t + k], peer).start()
    for t in range(len(p)):
        for k, peer in enumerate(peers):
            pchip = 2 * peer[0] + peer[1]
            _rcopy(out[t].at[pchip], out[t].at[pchip], send.at[3 * t + k], recv.at[3 * t + k], peer).wait()


def _rs_swap(name, halves):
    n = len(halves)

    def body(*refs):
        a, out = refs[:n], refs[n:2 * n]
        send, recv = refs[2 * n:]
        x, y, c = _me()
        sib = (x, y, 1 - c)
        for t in range(n):
            ch = a[t].shape[0] // _NCH
            for i in range(_NCH):
                rows = pl.ds(i * ch, ch)
                _rcopy(a[t].at[rows], out[t].at[rows], send.at[t], recv.at[t], sib).start()
        for t in range(n):
            _rcopy(a[t], out[t], send.at[t], recv.at[t], sib).wait()

    return pl.pallas_call(
        body, name=name, in_specs=[_ANY] * n, out_specs=[_ANY] * n,
        out_shape=[jax.ShapeDtypeStruct(s.shape, s.dtype) for s in halves],
        scratch_shapes=_dma_sems(n, n))(*halves)


def _sibling_merge(name, a):
    P_, rh, C = a.shape

    def body(a_ref, out_ref, send_sem, recv_sem, local_sem):
        x, y, c = _me()
        local = pltpu.make_async_copy(a_ref, out_ref.at[:, pl.ds(c * rh, rh)], local_sem)
        local.start()
        cp = _rcopy(a_ref, out_ref.at[:, pl.ds(c * rh, rh)], send_sem, recv_sem, (x, y, 1 - c))
        cp.start()
        cp.wait_send()
        _rcopy(a_ref, out_ref.at[:, pl.ds((1 - c) * rh, rh)], send_sem, recv_sem, (x, y, 1 - c)).wait_recv()
        local.wait()

    return pl.pallas_call(
        body, name=name, in_specs=[_ANY], out_specs=_ANY, out_shape=jax.ShapeDtypeStruct((P_, 2 * rh, C), a.dtype),
        scratch_shapes=[pltpu.SemaphoreType.DMA(()), pltpu.SemaphoreType.DMA(()), pltpu.SemaphoreType.DMA(())])(a)


def _allgather8(name, a):
    g4 = _allgather4(name + "_chips", a)
    both = _sibling_merge(name + "_cores", g4.reshape(1, 4 * a.shape[0], a.shape[1]))
    return jnp.transpose(both.reshape(2, 4, *a.shape), (1, 0, 2, 3)).reshape(8, *a.shape)


def _sum_slots(name, a, out_dtype):
    def fn(a):
        acc = a[0].astype(F32)
        for k in range(1, a.shape[0]):
            acc = acc + a[k].astype(F32)
        return acc
    return _rowwise(name, fn, [a], [], [(a.shape[2], out_dtype)])[0]


def _adamw_math(w, g, m, v):
    m = ADAM_B1 * m + (1.0 - ADAM_B1) * g
    v = ADAM_B2 * v + (1.0 - ADAM_B2) * (g * g)
    m_hat = m / (1.0 - ADAM_B1 ** ADAM_STEP)
    v_hat = v / (1.0 - ADAM_B2 ** ADAM_STEP)
    return -ADAM_LR * (m_hat / (jnp.sqrt(v_hat) + ADAM_EPS) + ADAM_WD * w), m, v


def _adamw_piece(name, w2, m2, v2, mine, theirs, row0, prev, core):
    R, C = w2.shape
    h = mine.shape[0]
    tr = _slot_tile(h, 256)
    nb = h // tr
    assert row0 % tr == 0
    first = row0 // tr

    def body(c_ref, w_ref, m_ref, v_ref, a_ref, b_ref, *rest):
        g_ref, d_ref, nm_ref, nv_ref = rest[-4:]
        g = jnp.where(pl.program_id(0) == c_ref[0], a_ref[...], b_ref[...])
        g_ref[...] = g
        d_ref[...], nm_ref[...], nv_ref[...] = _adamw_math(w_ref[...], g, m_ref[...], v_ref[...])

    full = pl.BlockSpec((tr, C), lambda s, i, c_ref: (first + s * nb + i, 0))
    mine_spec = pl.BlockSpec((tr, C), lambda s, i, c_ref: (jnp.where(s == c_ref[0], i, 0), 0))
    theirs_spec = pl.BlockSpec((tr, C), lambda s, i, c_ref: (jnp.where(s == c_ref[0], 0, i), 0))
    extra = [] if prev is None else list(prev)
    grid_spec = pltpu.PrefetchScalarGridSpec(
        num_scalar_prefetch=1, grid=(2, nb), in_specs=[full, full, full, mine_spec, theirs_spec] + [_ANY] * len(extra),
        out_specs=[full] * 4)
    return pl.pallas_call(
        body, name=name, grid_spec=grid_spec, out_shape=[jax.ShapeDtypeStruct((R, C), F32)] * 4,
        input_output_aliases={6 + k: k for k in range(len(extra))}, compiler_params=_params(2))(core, w2, m2, v2, mine, theirs, *extra)


def _adamw(name, w, g, m, v):
    shape = w.shape
    two_d = (-1, shape[-1]) if w.ndim > 1 else (1, -1)
    w2, g2, m2, v2 = [t.reshape(two_d) for t in (w, g, m, v)]
    rows = w2.shape[0]
    tr = rows
    for cand in (256, 128, 64, 32, 16, 8):
        if rows % cand == 0:
            tr = cand
            break

    c = w2.shape[1]
    outs = _rowwise(name, _adamw_math, [w2, g2, m2, v2], [], [(c, F32)] * 3, tr=tr)
    return [o.reshape(shape) for o in outs]


_WEIGHT_ORDER = ("ada_w", "ada_b", "norm_mix_g", "norm_ffn_g", "gdn_w_in", "gdn_conv_w", "gdn_a_log", "gdn_dt_bias",
                 "gdn_norm_g", "gdn_w_out", "mla_w_in", "mla_q_norm_g", "mla_kv_norm_g", "mla_w_uq", "mla_w_ukv",
                 "mla_w_out", "ffn_w_gate", "ffn_w_up", "ffn_w_down", "final_norm_g")
_BIG = (("gdn_w_in", 2), ("gdn_w_out", 1), ("mla_w_in", 1), ("mla_w_uq", 2), ("mla_w_ukv", 2), ("mla_w_out", 1),
        ("ffn_w_gate", 2), ("ffn_w_up", 2), ("ffn_w_down", 1))
_SMALL_SHARDED = (("gdn_conv_w", 1), ("mla_q_norm_g", 1), ("mla_kv_norm_g", 1))
_STORED_TRANSPOSED = ("ffn_w_gate", "ffn_w_up")


def _size(shape):
    n = 1
    for s in shape:
        n *= s
    return n


def _pack_rows_each(tensors):
    parts, offs, off = [], [], 0
    for t in tensors:
        flat = t.reshape(-1).astype(F32)
        rows = -(-flat.shape[0] // PACK_W)
        parts.append(jnp.pad(flat, (0, rows * PACK_W - flat.shape[0])).reshape(rows, PACK_W))
        offs.append(off)
        off += rows
    total = -(-off // 16) * 16
    pack = jnp.pad(parts[0], ((offs[0], total - offs[0] - parts[0].shape[0]), (0, 0)))
    for p, o in zip(parts[1:], offs[1:]):
        pack = pack + jnp.pad(p, ((o, total - o - p.shape[0]), (0, 0)))
    return pack, offs


def _unpack_rows_each(pack, shapes):
    lead = pack.shape[:-2]
    out, off = [], 0
    for shp in shapes:
        n = _size(shp)
        rows = -(-n // PACK_W)
        out.append(pack[..., off:off + rows, :].reshape(*lead, -1)[..., :n].reshape(*lead, *shp))
        off += rows
    return out


def _merge_chips(stacked, axis):
    moved = jnp.moveaxis(stacked, 0, axis)
    shp = list(moved.shape)
    return moved.reshape(shp[:axis] + [shp[axis] * shp[axis + 1]] + shp[axis + 2:])


def _my_shard(full, axis, chip):
    n = full.shape[axis] // 4
    return lax.dynamic_slice_in_dim(full, chip * n, n, axis)


def kernel(x, c, positions, ada_w, ada_b, norm_mix_g, norm_ffn_g, gdn_w_in, gdn_conv_w, gdn_a_log, gdn_dt_bias, gdn_norm_g, gdn_w_out, mla_w_in, mla_q_norm_g, mla_kv_norm_g, mla_w_uq, mla_w_ukv, mla_w_out, ffn_w_gate, ffn_w_up, ffn_w_down, final_norm_g, loss_target, m_ada_w, m_ada_b, m_norm_mix_g, m_norm_ffn_g, m_gdn_w_in, m_gdn_conv_w, m_gdn_a_log, m_gdn_dt_bias, m_gdn_norm_g, m_gdn_w_out, m_mla_w_in, m_mla_q_norm_g, m_mla_kv_norm_g, m_mla_w_uq, m_mla_w_ukv, m_mla_w_out, m_ffn_w_gate, m_ffn_w_up, m_ffn_w_down, m_final_norm_g, v_ada_w, v_ada_b, v_norm_mix_g, v_norm_ffn_g, v_gdn_w_in, v_gdn_conv_w, v_gdn_a_log, v_gdn_dt_bias, v_gdn_norm_g, v_gdn_w_out, v_mla_w_in, v_mla_q_norm_g, v_mla_kv_norm_g, v_mla_w_uq, v_mla_w_ukv, v_mla_w_out, v_ffn_w_gate, v_ffn_w_up, v_ffn_w_down, v_final_norm_g):
    w = dict(ada_w=ada_w, ada_b=ada_b, norm_mix_g=norm_mix_g, norm_ffn_g=norm_ffn_g, gdn_w_in=gdn_w_in, gdn_conv_w=gdn_conv_w,
             gdn_a_log=gdn_a_log, gdn_dt_bias=gdn_dt_bias, gdn_norm_g=gdn_norm_g, gdn_w_out=gdn_w_out, mla_w_in=mla_w_in,
             mla_q_norm_g=mla_q_norm_g, mla_kv_norm_g=mla_kv_norm_g, mla_w_uq=mla_w_uq, mla_w_ukv=mla_w_ukv,
             mla_w_out=mla_w_out, ffn_w_gate=ffn_w_gate, ffn_w_up=ffn_w_up, ffn_w_down=ffn_w_down, final_norm_g=final_norm_g)
    m = dict(ada_w=m_ada_w, ada_b=m_ada_b, norm_mix_g=m_norm_mix_g, norm_ffn_g=m_norm_ffn_g, gdn_w_in=m_gdn_w_in,
             gdn_conv_w=m_gdn_conv_w, gdn_a_log=m_gdn_a_log, gdn_dt_bias=m_gdn_dt_bias, gdn_norm_g=m_gdn_norm_g,
             gdn_w_out=m_gdn_w_out, mla_w_in=m_mla_w_in, mla_q_norm_g=m_mla_q_norm_g, mla_kv_norm_g=m_mla_kv_norm_g,
             mla_w_uq=m_mla_w_uq, mla_w_ukv=m_mla_w_ukv, mla_w_out=m_mla_w_out, ffn_w_gate=m_ffn_w_gate,
             ffn_w_up=m_ffn_w_up, ffn_w_down=m_ffn_w_down, final_norm_g=m_final_norm_g)
    v = dict(ada_w=v_ada_w, ada_b=v_ada_b, norm_mix_g=v_norm_mix_g, norm_ffn_g=v_norm_ffn_g, gdn_w_in=v_gdn_w_in,
             gdn_conv_w=v_gdn_conv_w, gdn_a_log=v_gdn_a_log, gdn_dt_bias=v_gdn_dt_bias, gdn_norm_g=v_gdn_norm_g,
             gdn_w_out=v_gdn_w_out, mla_w_in=v_mla_w_in, mla_q_norm_g=v_mla_q_norm_g, mla_kv_norm_g=v_mla_kv_norm_g,
             mla_w_uq=v_mla_w_uq, mla_w_ukv=v_mla_w_ukv, mla_w_out=v_mla_w_out, ffn_w_gate=v_ffn_w_gate,
             ffn_w_up=v_ffn_w_up, ffn_w_down=v_ffn_w_down, final_norm_g=v_final_norm_g)
    T = x.shape[1]
    ix, iy, ic = _me()
    chip = 2 * ix + iy
    seq = 2 * chip + ic
    n_dev = 8

    small_shapes = [w[n].shape for n, _ in _SMALL_SHARDED] + [c.shape]
    pack0, _ = _pack_rows_each([w[n] for n, _ in _SMALL_SHARDED] + [c])
    got0 = _unpack_rows_each(_allgather8("gather_small", pack0), small_shapes)
    small_full = {n: _merge_chips(g[0::2], ax) for (n, ax), g in zip(_SMALL_SHARDED, got0)}
    c_all = got0[-1].reshape(n_dev, D)

    big = [n for n, _ in _BIG]
    chip_arr = chip.astype(jnp.int32).reshape(1)

    def stored(n, t):
        return jnp.swapaxes(t, 1, 2) if n in _STORED_TRANSPOSED else t

    ws, ms, vs = [{n: stored(n, d[n]) for n in big} for d in (w, m, v)]
    two_d = lambda t: t.reshape(-1, t.shape[-1])

    gathered = []
    for l in range(DEPTH):
        names = _layer_weights(l)
        bufs = [_cast_into_slot(f"to_bf16_{n}{l}", two_d(ws[n]), chip_arr, j * ws[n].shape[1], ws[n].shape[1]) for n, j in names]
        filled = _gather_weights("gather_weights0", bufs) if l == 0 else _gather_weights_async(f"gather_weights{l}", l, bufs)
        gathered.append({n: b for (n, _), b in zip(names, filled)})

    def weights_of(l, h):
        return _weights_to_kernel(l, gathered[l])

    P = _small_to_kernel(norm_mix_g, norm_ffn_g, final_norm_g, small_full["gdn_conv_w"], gdn_a_log, gdn_dt_bias,
                         gdn_norm_g, small_full["mla_q_norm_g"], small_full["mla_kv_norm_g"])

    c16 = jnp.pad(c_all, ((0, 16 - n_dev), (0, 0)))
    ca = _rowwise("cond_silu", lambda t: t * _sig(t), [c16], [], [(D, BF16)])[0]
    n_ada = ada_w.shape[2]
    mods = jnp.concatenate([_mm(f"ada_fwd{l}", ca, ada_w[l], "nn") for l in range(DEPTH)], axis=0)
    mods_all = _allgather4("gather_mod", mods).reshape(4, DEPTH, 16, n_ada)
    mod_mm = jnp.transpose(lax.dynamic_index_in_dim(mods_all, seq, axis=2, keepdims=False), (1, 0, 2)).reshape(DEPTH, 4 * n_ada)
    mod = _rowwise("mod_bias", lambda a, b: a + b, [mod_mm, ada_b], [], [(4 * n_ada, F32)])[0]

    core_chip = jnp.stack([ic, chip]).astype(jnp.int32)
    pending, in_flight = {}, []

    def reduce_group(layer, part, pieces):
        pending.update({(n, layer if n.startswith("ffn_") else layer // 2): g for n, g in pieces.items()})
        if part == "ffn" and layer > 0:
            return
        keys = list(pending)
        glist = [pending.pop(k) for k in keys]
        tag = f"{layer}{part}"
        theirs = _rs_split("grads_cores_" + tag, glist)
        both = [_pair_add(f"grads_pair_{n}{l}", g, t, core_chip) for (n, l), g, t in zip(keys, glist, theirs)]
        swapped = _rs_alltoall_async("grads_chips_" + tag, DEPTH + 1 + len(in_flight), [p for p, _ in both], [o for _, o in both])
        in_flight.append((tag, keys, swapped))

    dx, dmod, gP = _local_step(x.reshape(T, D), loss_target.reshape(T, D), positions.reshape(T, 1), mod, weights_of, P, reduce_group)

    partials = [dmod, jnp.concatenate(gP["norm_mix_g"]), jnp.concatenate(gP["norm_ffn_g"]), gP["final_g"],
                jnp.stack([jnp.transpose(g) for g in gP["gdn_cw"]]), jnp.concatenate(gP["gdn_alog"])[:, :NH],
                jnp.concatenate(gP["gdn_dtb"])[:, :NH], jnp.concatenate(gP["gdn_ng"]), jnp.concatenate(gP["mla_qg"]),
                jnp.concatenate(gP["mla_kvg"]), gP["loss"][:, :1]]
    part_shapes = [p.shape for p in partials]
    ppack, _ = _pack_rows_each(partials)
    pall = _allgather8("gather_partials", ppack)
    psum = _sum_slots("sum_partials", pall, F32)
    (g_ada_b, g_norm_mix, g_norm_ffn, g_final, g_conv_full, g_alog, g_dtb, g_gdn_ng, g_qg_full, g_kvg_full,
     loss_sum) = _unpack_rows_each(psum, part_shapes)
    dmod_all = _unpack_rows_each(pall, part_shapes[:1])[0]

    grads = dict(ada_b=g_ada_b, norm_mix_g=g_norm_mix, norm_ffn_g=g_norm_ffn, final_norm_g=g_final.reshape(D),
                 gdn_conv_w=_my_shard(g_conv_full, 1, chip), gdn_a_log=g_alog, gdn_dt_bias=g_dtb, gdn_norm_g=g_gdn_ng,
                 mla_q_norm_g=_my_shard(g_qg_full, 1, chip), mla_kv_norm_g=_my_shard(g_kvg_full, 1, chip))

    ca_t = jnp.zeros((D, LANES), BF16).at[:, :16].set(jnp.transpose(ca))
    dm_mine = lax.dynamic_slice_in_dim(dmod_all, chip * n_ada, n_ada, axis=2)
    grads["ada_w"] = jnp.stack([
        _mm(f"ada_bwd{l}", ca_t, jnp.pad(dm_mine[:, l], ((0, LANES - n_dev), (0, 0))), "nn") for l in range(DEPTH)])

    delta, new_m, new_v = {}, {}, {}
    results = {}
    keys = [k for _, ks, _ in in_flight for k in ks]
    halves = [_sum_slots(f"grads_sum_{n}{l}", s, F32) for _, ks, sw in in_flight for (n, l), s in zip(ks, sw)]
    others = _rs_swap("grads_swap", halves)
    for (n, l), mine, theirs in zip(keys, halves, others):
        results[n] = _adamw_piece(f"adamw_{n}{l}", two_d(ws[n]), two_d(ms[n]), two_d(vs[n]), mine, theirs,
                                  l * ws[n].shape[1], results.get(n), core_chip[:1])
    for n in big:
        grads[n], delta[n], new_m[n], new_v[n] = [stored(n, t.reshape(ws[n].shape)) for t in results[n]]
    delta["ada_w"], new_m["ada_w"], new_v["ada_w"] = _adamw("adamw_ada_w", ada_w, grads["ada_w"], m_ada_w, v_ada_w)
    for n in [n for n in _WEIGHT_ORDER if n not in delta]:
        delta[n], new_m[n], new_v[n] = _adamw("adamw_" + n, w[n], grads[n], m[n], v[n])

    loss = loss_sum.reshape(())
    return (loss, dx.reshape(1, T, D), *[grads[n] for n in _WEIGHT_ORDER], *[delta[n] for n in _WEIGHT_ORDER],
            *[new_m[n] for n in _WEIGHT_ORDER], *[new_v[n] for n in _WEIGHT_ORDER])
```

```python
import functools

import jax
import jax.numpy as jnp
from jax import lax
from jax.experimental import pallas as pl
from jax.experimental.pallas import tpu as pltpu
from jax.experimental.pallas import tpu_sc as plsc

F32 = jnp.float32
BF16 = jnp.bfloat16
HI = lax.Precision.HIGHEST
MESH = pl.DeviceIdType.MESH

D = 1024
DEPTH = 4
N_MOD = 6
NH = 8
HD = 128
CHUNK = 64
_GDN_HB = 8
GDN_QKV = 3 * NH * HD
GDN_INK = GDN_QKV + NH * HD + 2 * HD
Q_RANK, KV_RANK, ROPE = 384, 256, 64
MLA_INK = Q_RANK + KV_RANK + HD
DFF = 2816
EPS = 1e-6
ATT_SCALE = (HD + ROPE) ** -0.5
ROPE_THETA = 10000.0
LANES = 128
PACK_W = 1024

ADAM_LR, ADAM_B1, ADAM_B2, ADAM_EPS, ADAM_WD, ADAM_STEP = 0.001, 0.9, 0.999, 1e-08, 0.01, 10


H3 = "bf16x3"
B1 = "bf16"
HS = H3
HF = B1


def _dot(a, b, mode="nn", prec=None):
    dn = {"nn": (((1,), (0,)), ((), ())), "nt": (((1,), (1,)), ((), ())), "tn": (((0,), (0,)), ((), ()))}[mode]
    if prec == B1:
        return _dot(a.astype(BF16), b.astype(BF16), mode)
    if prec == H3:
        ah, bh = a.astype(BF16), b.astype(BF16)
        al, bl = (a - ah.astype(F32)).astype(BF16), (b - bh.astype(F32)).astype(BF16)
        return _dot(ah, bh, mode) + (_dot(ah, bl, mode) + _dot(al, bh, mode))
    return lax.dot_general(a, b, dn, precision=prec, preferred_element_type=F32)


def _sig(x):
    return 1.0 / (1.0 + jnp.exp(-x))


def _pick(n, cap):
    if n <= cap:
        return n
    best = None
    for d in range(LANES, cap + 1, LANES):
        if n % d == 0:
            best = d
    assert best is not None, (n, cap)
    return best


def _params(n_grid):
    return pltpu.CompilerParams(dimension_semantics=("arbitrary",) * n_grid, vmem_limit_bytes=56 * 1024 * 1024)


def _rowwise(name, fn, rows, consts, outs, sums=(), tr=256):
    first = rows[0][0] if isinstance(rows[0], tuple) else rows[0]
    T = first.shape[-2]
    tr = _slot_tile(T, tr)
    nr, nc, no, ns = len(rows), len(consts), len(outs), len(sums)

    windows = [c[1:] if isinstance(c, tuple) else None for c in consts]
    consts = [c[0] if isinstance(c, tuple) else c for c in consts]

    def body(*refs):
        vals = [r[...] for r in refs[:nr]]
        for r, win in zip(refs[nr:nr + nc], windows):
            vals.append(r[...] if win is None else r[win[0]:win[0] + 1, win[1] * win[2]:(win[1] + 1) * win[2]])
        res = fn(*vals)
        if not isinstance(res, (tuple, list)):
            res = (res,)
        o_refs = refs[nr + nc:nr + nc + no]
        s_refs = refs[nr + nc + no:]
        for r, val in zip(o_refs, res[:no]):
            r[...] = val.astype(r.dtype)
        if ns:
            @pl.when(pl.program_id(0) == 0)
            def _():
                for r in s_refs:
                    r[...] = jnp.zeros_like(r)
            for r, val in zip(s_refs, res[no:]):
                r[...] += val

    in_specs, args = [], []
    for a in rows:
        if isinstance(a, tuple):
            arr, width, cb = a
            in_specs.append(pl.BlockSpec((tr, width), lambda i, cb=cb: (i, cb)))
            args.append(arr)
        elif a.ndim == 3:
            in_specs.append(pl.BlockSpec((a.shape[0], tr, a.shape[2]), lambda i: (0, i, 0)))
            args.append(a)
        else:
            in_specs.append(pl.BlockSpec((tr, a.shape[1]), lambda i: (i, 0)))
            args.append(a)
    for a in consts:
        in_specs.append(pl.BlockSpec(a.shape, lambda i, nd=a.ndim: (0,) * nd))
        args.append(a)
    out_specs = [pl.BlockSpec((tr, w), lambda i: (i, 0)) for w, _ in outs]
    out_specs += [pl.BlockSpec((1, w), lambda i: (0, 0)) for w in sums]
    out_shape = [jax.ShapeDtypeStruct((T, w), dt) for w, dt in outs]
    out_shape += [jax.ShapeDtypeStruct((1, w), F32) for w in sums]
    res = pl.pallas_call(body, name=name, grid=(T // tr,), in_specs=in_specs, out_specs=out_specs,
                         out_shape=out_shape, compiler_params=_params(1))(*args)
    return res


def _mm(name, a, b, mode, out_dtype=F32, tm=512, tn=1024):
    if mode == "tn":
        K, M = a.shape
    else:
        M, K = a.shape
    N = b.shape[0] if mode == "nt" else b.shape[1]
    tm, tn = _pick(M, tm), _pick(N, tn)

    def body(a_ref, b_ref, o_ref):
        o_ref[...] = _dot(a_ref[...].astype(BF16), b_ref[...].astype(BF16), mode).astype(o_ref.dtype)

    a_spec = pl.BlockSpec((K, tm), lambda i, j: (0, i)) if mode == "tn" else pl.BlockSpec((tm, K), lambda i, j: (i, 0))
    b_spec = pl.BlockSpec((tn, K), lambda i, j: (j, 0)) if mode == "nt" else pl.BlockSpec((K, tn), lambda i, j: (0, j))
    return pl.pallas_call(body, name=name, grid=(M // tm, N // tn), in_specs=[a_spec, b_spec],
                          out_specs=pl.BlockSpec((tm, tn), lambda i, j: (i, j)),
                          out_shape=jax.ShapeDtypeStruct((M, N), out_dtype), compiler_params=_params(2))(a, b)


def _rms(x, eps=EPS):
    return lax.rsqrt(jnp.mean(x * x, axis=-1, keepdims=True) + eps)


def _norm_mod_fwd(name, x, g, scale, shift):
    def fn(x, g, scale, shift):
        return x * _rms(x) * g * (1.0 + scale) + shift
    return _rowwise(name, fn, [x], [g, scale, shift], [(D, BF16)])[0]


def _norm_mod_bwd(name, dh, x, dx_res, g, scale):
    def fn(dh, x, dx_res, g, scale):
        r = _rms(x)
        xh = x * r
        dxh = dh * (g * (1.0 + scale))
        dx = r * (dxh - xh * jnp.mean(dxh * xh, axis=-1, keepdims=True))
        dhx = dh * xh
        return (dx_res + dx, jnp.sum(dh, axis=0, keepdims=True), jnp.sum(dhx * g, axis=0, keepdims=True),
                jnp.sum(dhx * (1.0 + scale), axis=0, keepdims=True))
    return _rowwise(name, fn, [dh, x, dx_res], [g, scale], [(D, F32)], sums=[D, D, D])


def _residual_fwd(name, x, y, gate):
    def fn(x, y, gate):
        return x + gate * y
    return _rowwise(name, fn, [x, y], [gate], [(D, F32)])[0]


def _residual_norm_fwd(name, x, y, gate, g, scale, shift):
    def fn(x, y, gate, g, scale, shift):
        x = x + gate * y
        return x, x * _rms(x) * g * (1.0 + scale) + shift
    return _rowwise(name, fn, [x, y], [gate, g, scale, shift], [(D, F32), (D, BF16)])


def _norm_residual_bwd(name, dh, x, dx_res, g, scale, y, gate):
    def fn(dh, x, dx_res, y, g, scale, gate):
        r = _rms(x)
        xh = x * r
        dxh = dh * (g * (1.0 + scale))
        dx = dx_res + r * (dxh - xh * jnp.mean(dxh * xh, axis=-1, keepdims=True))
        dhx = dh * xh
        return (dx, dx * gate, jnp.sum(dh, axis=0, keepdims=True), jnp.sum(dhx * g, axis=0, keepdims=True),
                jnp.sum(dhx * (1.0 + scale), axis=0, keepdims=True), jnp.sum(dx * y, axis=0, keepdims=True))
    return _rowwise(name, fn, [dh, x, dx_res, y], [g, scale, gate], [(D, F32), (D, BF16)], sums=[D, D, D, D])


def _residual_bwd(name, dx, y, gate):
    def fn(dx, y, gate):
        return dx * gate, jnp.sum(dx * y, axis=0, keepdims=True)
    return _rowwise(name, fn, [dx, y], [gate], [(D, BF16)], sums=[D])


def _loss_head(x, target, g):
    def fn(x, t, g):
        r = _rms(x)
        xh = x * r
        err = xh * g - t
        loss = 0.5 * jnp.sum(jnp.mean(err * err, axis=-1, keepdims=True), axis=0, keepdims=True)
        dy = err * (1.0 / D)
        dxh = dy * g
        dx = r * (dxh - xh * jnp.mean(dxh * xh, axis=-1, keepdims=True))
        return dx, jnp.broadcast_to(loss, (1, LANES)), jnp.sum(dy * xh, axis=0, keepdims=True)
    return _rowwise("loss_head", fn, [x, target], [g], [(D, F32)], sums=[LANES, D])


def _ffn_up(name, h, wg, wu, layer, tm=1024):
    T, n = h.shape[0], wg.shape[1]
    tm = min(tm, T)

    def body(h_ref, wg_ref, wu_ref, a_ref, b_ref, s_ref):
        h = h_ref[...]
        a = _dot(h, wg_ref[0], "nt")
        b = _dot(h, wu_ref[0], "nt")
        a_ref[0] = a.astype(a_ref.dtype)
        b_ref[0] = b.astype(b_ref.dtype)
        s_ref[0] = (a * _sig(a) * b).astype(s_ref.dtype)

    wspec = pl.BlockSpec((1, n, D), lambda ch, i: (ch, layer, 0))
    ospec = pl.BlockSpec((1, tm, n), lambda ch, i: (ch, i, 0))
    return pl.pallas_call(
        body, name=name, grid=(4, T // tm), in_specs=[pl.BlockSpec((tm, D), lambda ch, i: (i, 0)), wspec, wspec],
        out_specs=[ospec, ospec, ospec],
        out_shape=[jax.ShapeDtypeStruct((4, T, n), BF16)] * 3, compiler_params=_params(2))(h, wg, wu)


def _ffn_down(name, s, wd, layer, tm=1024):
    _, T, n = s.shape
    tm = min(tm, T)

    def body(s_ref, w_ref, y_ref):
        @pl.when(pl.program_id(1) == 0)
        def _():
            y_ref[...] = jnp.zeros_like(y_ref)
        y_ref[...] += _dot(s_ref[0], w_ref[0], "nn")

    return pl.pallas_call(
        body, name=name, grid=(T // tm, 4),
        in_specs=[pl.BlockSpec((1, tm, n), lambda i, ch: (ch, i, 0)), pl.BlockSpec((1, n, D), lambda i, ch: (ch, layer, 0))],
        out_specs=pl.BlockSpec((tm, D), lambda i, ch: (i, 0)), out_shape=jax.ShapeDtypeStruct((T, D), F32),
        compiler_params=_params(2))(s, wd)


def _ffn_down_bwd(name, dy, wd, a, b, layer, tm=1024):
    _, T, n = a.shape
    tm = min(tm, T)

    def body(dy_ref, w_ref, a_ref, b_ref, da_ref, db_ref):
        ds = _dot(dy_ref[...], w_ref[0], "nt")
        a, b = a_ref[0].astype(F32), b_ref[0].astype(F32)
        sg = _sig(a)
        da_ref[0] = (ds * b * (sg * (1.0 + a * (1.0 - sg)))).astype(da_ref.dtype)
        db_ref[0] = (ds * (a * sg)).astype(db_ref.dtype)

    bspec = pl.BlockSpec((1, tm, n), lambda ch, i: (ch, i, 0))
    return pl.pallas_call(
        body, name=name, grid=(4, T // tm),
        in_specs=[pl.BlockSpec((tm, D), lambda ch, i: (i, 0)), pl.BlockSpec((1, n, D), lambda ch, i: (ch, layer, 0)), bspec, bspec],
        out_specs=[bspec, bspec], out_shape=[jax.ShapeDtypeStruct((4, T, n), BF16)] * 2,
        compiler_params=_params(2))(dy, wd, a, b)


def _ffn_down_dw(name, s, dy):
    _, T, n = s.shape

    def body(s_ref, dy_ref, o_ref):
        o_ref[0] = _dot(s_ref[0], dy_ref[...], "tn").astype(o_ref.dtype)

    return pl.pallas_call(
        body, name=name, grid=(4,),
        in_specs=[pl.BlockSpec((1, T, n), lambda ch: (ch, 0, 0)), pl.BlockSpec((T, D), lambda ch: (0, 0))],
        out_specs=pl.BlockSpec((1, n, D), lambda ch: (ch, 0, 0)), out_shape=jax.ShapeDtypeStruct((4, n, D), BF16),
        compiler_params=_params(1))(s, dy)


def _ffn_up_dw(name, h, da, db, tm=512):
    _, T, n = da.shape

    def body(h_ref, da_ref, db_ref, dg_ref, du_ref):
        h = h_ref[...]
        dg_ref[0] = _dot(da_ref[0], h, "tn").astype(dg_ref.dtype)
        du_ref[0] = _dot(db_ref[0], h, "tn").astype(du_ref.dtype)

    dspec = pl.BlockSpec((1, T, n), lambda ch, j: (ch, 0, 0))
    ospec = pl.BlockSpec((1, n, tm), lambda ch, j: (ch, 0, j))
    return pl.pallas_call(
        body, name=name, grid=(4, D // tm), in_specs=[pl.BlockSpec((T, tm), lambda ch, j: (0, j)), dspec, dspec],
        out_specs=[ospec, ospec], out_shape=[jax.ShapeDtypeStruct((4, n, D), BF16)] * 2,
        compiler_params=_params(2))(h, da, db)


def _ffn_up_dx(name, da, db, wg, wu, layer, tm=1024):
    _, T, n = da.shape
    tm = min(tm, T)

    def body(da_ref, db_ref, wg_ref, wu_ref, o_ref):
        @pl.when(pl.program_id(1) == 0)
        def _():
            o_ref[...] = jnp.zeros_like(o_ref)
        o_ref[...] += _dot(da_ref[0], wg_ref[0], "nn") + _dot(db_ref[0], wu_ref[0], "nn")

    dspec = pl.BlockSpec((1, tm, n), lambda i, ch: (ch, i, 0))
    wspec = pl.BlockSpec((1, n, D), lambda i, ch: (ch, layer, 0))
    return pl.pallas_call(
        body, name=name, grid=(T // tm, 4), in_specs=[dspec, dspec, wspec, wspec],
        out_specs=pl.BlockSpec((tm, D), lambda i, ch: (i, 0)), out_shape=jax.ShapeDtypeStruct((T, D), F32),
        compiler_params=_params(2))(da, db, wg, wu)


def _shift_down(x, k):
    if k == 0:
        return x
    rows = lax.broadcasted_iota(jnp.int32, x.shape, 0)
    return jnp.where(rows >= k, pltpu.roll(x, k, 0), 0.0)


def _shift_up(x, k):
    if k == 0:
        return x
    T = x.shape[0]
    rows = lax.broadcasted_iota(jnp.int32, x.shape, 0)
    return jnp.where(rows < T - k, pltpu.roll(x, T - k, 0), 0.0)


def _conv_silu(x, w):
    c = w[0:1, :] * _shift_down(x, 3) + w[1:2, :] * _shift_down(x, 2) + w[2:3, :] * _shift_down(x, 1) + w[3:4, :] * x
    sg = _sig(c)
    return c, sg, c * sg


def _gdn_conv_fwd(name, proj, cw):
    T = proj.shape[0]

    def body(x_ref, w_ref, o_ref):
        j = pl.program_id(0)
        _, _, y = _conv_silu(x_ref[...], w_ref[...])
        r = lax.rsqrt(jnp.sum(y * y, axis=1, keepdims=True) + EPS)
        mult = jnp.where(j < NH, HD ** -0.5, 1.0)
        o_ref[...] = jnp.where(j < 2 * NH, y * (r * mult), y)

    return pl.pallas_call(body, name=name, grid=(3 * NH,),
                          in_specs=[pl.BlockSpec((T, HD), lambda j: (0, j)), pl.BlockSpec((4, HD), lambda j: (0, j))],
                          out_specs=pl.BlockSpec((T, HD), lambda j: (0, j)),
                          out_shape=jax.ShapeDtypeStruct((T, GDN_QKV), F32), compiler_params=_params(1))(proj, cw)


def _gdn_conv_bwd(name, proj, cw, dz):
    T = proj.shape[0]

    def body(x_ref, w_ref, dz_ref, dx_ref, dw_ref):
        j = pl.program_id(0)
        x, w, dz = x_ref[...], w_ref[...], dz_ref[...]
        c, sg, y = _conv_silu(x, w)
        r = lax.rsqrt(jnp.sum(y * y, axis=1, keepdims=True) + EPS)
        mult = jnp.where(j < NH, HD ** -0.5, 1.0)
        dyn = mult * (r * dz - (r * r * r) * y * jnp.sum(dz * y, axis=1, keepdims=True))
        dy = jnp.where(j < 2 * NH, dyn, dz)
        dc = dy * (sg * (1.0 + c * (1.0 - sg)))
        dx = w[0:1, :] * _shift_up(dc, 3) + w[1:2, :] * _shift_up(dc, 2) + w[2:3, :] * _shift_up(dc, 1) + w[3:4, :] * dc
        dx_ref[...] = dx.astype(dx_ref.dtype)
        for k in range(4):
            dw_ref[pl.ds(k, 1), :] = jnp.sum(dc * _shift_down(x, 3 - k), axis=0, keepdims=True)

    return pl.pallas_call(body, name=name, grid=(3 * NH,),
                          in_specs=[pl.BlockSpec((T, HD), lambda j: (0, j)), pl.BlockSpec((4, HD), lambda j: (0, j)),
                                    pl.BlockSpec((T, HD), lambda j: (0, j))],
                          out_specs=[pl.BlockSpec((T, HD), lambda j: (0, j)), pl.BlockSpec((4, HD), lambda j: (0, j))],
                          out_shape=[jax.ShapeDtypeStruct((T, GDN_QKV), BF16), jax.ShapeDtypeStruct((4, GDN_QKV), F32)],
                          compiler_params=_params(1))(proj, cw, dz)


def _softplus(z):
    return jnp.maximum(z, 0.0) + jnp.log(1.0 + jnp.exp(-jnp.abs(z)))


_AB_CB = GDN_INK // (2 * HD) - 1


def _gdn_gates_fwd(name, proj, alog, dtb):
    def fn(ab, alog, dtb):
        a, b = ab[:, :HD], ab[:, HD:]
        return -jnp.exp(alog) * _softplus(a + dtb), _sig(b)
    return _rowwise(name, fn, [(proj, 2 * HD, _AB_CB)], [alog, dtb], [(HD, F32), (HD, F32)])


def _gdn_gates_bwd(name, proj, dg_h, db_h, alog, dtb):
    def fn(ab, dg_h, db_h, alog, dtb):
        lane = lax.broadcasted_iota(jnp.int32, (1, HD), 1)
        dg = jnp.zeros(dg_h.shape[1:], F32)
        dbeta = jnp.zeros(dg_h.shape[1:], F32)
        for h in range(NH):
            oh = (lane == h).astype(F32)
            dg = dg + dg_h[h] * oh
            dbeta = dbeta + db_h[h] * oh
        a, b = ab[:, :HD], ab[:, HD:]
        z = a + dtb
        ea = jnp.exp(alog)
        beta = _sig(b)
        da = dg * (-ea) * _sig(z)
        db = dbeta * beta * (1.0 - beta)
        return (jnp.concatenate([da, db], axis=1), jnp.sum(dg * (-ea * _softplus(z)), axis=0, keepdims=True),
                jnp.sum(da, axis=0, keepdims=True))
    return _rowwise(name, fn, [(proj, 2 * HD, _AB_CB), dg_h, db_h], [alog, dtb], [(2 * HD, BF16)], sums=[HD, HD])


def _interleave(gens):
    gens = list(gens)
    results = [None] * len(gens)
    active = list(range(len(gens)))
    while active:
        for i in list(active):
            try:
                next(gens[i])
            except StopIteration as stop:
                results[i] = stop.value
                active.remove(i)
    return results


def _chunk_common(q, k, v, gblk, bblk, h, prec):
    C = CHUNK
    lane = lax.broadcasted_iota(jnp.int32, (1, HD), 1)
    oh = (lane == h).astype(F32)
    g_col = jnp.sum(gblk * oh, axis=1, keepdims=True)
    beta = jnp.sum(bblk * oh, axis=1, keepdims=True)
    ri = lax.broadcasted_iota(jnp.int32, (C, C), 0)
    ci = lax.broadcasted_iota(jnp.int32, (C, C), 1)
    incl = ri >= ci
    strict = ri > ci
    eye = (ri == ci).astype(F32)
    gcb = _dot(incl.astype(F32), jnp.broadcast_to(g_col, (C, HD)), "nn", HI)
    yield
    gc = gcb[:, :C]
    gc_row = _dot(jnp.ones((C, C), F32), eye * gc, "nn", HI)
    yield
    decay = jnp.where(incl, jnp.exp(jnp.where(incl, gc - gc_row, 0.0)), 0.0)
    rows = lax.broadcasted_iota(jnp.int32, (C, HD), 0)
    gclb = jnp.sum(jnp.where(rows == C - 1, gcb, 0.0), axis=0, keepdims=True)
    eg = jnp.exp(gcb)
    egl = jnp.exp(gclb - gcb)
    gl = jnp.exp(gclb)
    kb = k * beta
    m1 = _dot(kb, k, "nt", prec)
    qk = _dot(q, k, "nt", prec)
    yield
    L = jnp.where(strict, m1 * decay, 0.0)
    nl = -L
    tinv = eye + nl
    p = nl
    for _ in range(5):
        p = _dot(p, p, "nn", H3)
        yield
        tinv = tinv + _dot(tinv, p, "nn", H3)
    vb = v * beta
    kbg = kb * eg
    yield
    u = _dot(tinv, vb, "nn", prec)
    w = _dot(tinv, kbg, "nn", prec)
    yield
    attn = jnp.where(incl, qk * decay, 0.0)
    return dict(beta=beta, incl=incl, strict=strict, decay=decay, eg=eg, egl=egl, gl=gl, kb=kb, m1=m1, tinv=tinv,
                kbg=kbg, u=u, w=w, qk=qk, attn=attn, q_dec=q * eg, k_dec=k * egl, rows=rows, oh=oh)


def _gdn_chunk_fwd(name, qkv, g, beta):
    T = qkv.shape[0]
    N = T // CHUNK

    hb = _GDN_HB
    w = hb * HD

    def body(q_ref, k_ref, v_ref, g_ref, b_ref, o_ref, st_ref, S):
        hg, n = pl.program_id(0), pl.program_id(1)

        @pl.when(n == 0)
        def _():
            S[...] = jnp.zeros_like(S)

        gblk, bblk = g_ref[...], b_ref[...]

        def one_head(i, q, k, v, s):
            c = yield from _chunk_common(q, k, v, gblk, bblk, hg * hb + i, HF)
            v_new = c["u"] - _dot(c["w"], s, "nn", HF)
            qs = _dot(c["q_dec"], s, "nn", HF)
            yield
            o = qs + _dot(c["attn"], v_new, "nn", HF)
            return o, s * c["gl"] + _dot(c["k_dec"], v_new, "tn", HF)

        sls = [slice(i * HD, (i + 1) * HD) for i in range(hb)]
        states = [S[i] for i in range(hb)]
        res = _interleave(one_head(i, q_ref[:, sls[i]], k_ref[:, sls[i]], v_ref[:, sls[i]], states[i]) for i in range(hb))
        for i, (o, s_new) in enumerate(res):
            st_ref[i, 0] = states[i]
            o_ref[:, sls[i]] = o
            S[i] = s_new

    blk = lambda off: pl.BlockSpec((CHUNK, w), lambda h, n, off=off: (n, off + h))
    gspec = pl.BlockSpec((CHUNK, HD), lambda h, n: (n, 0))
    return pl.pallas_call(
        body, name=name, grid=(NH // hb, N), in_specs=[blk(0), blk(NH // hb), blk(2 * NH // hb), gspec, gspec],
        out_specs=[pl.BlockSpec((CHUNK, w), lambda h, n: (n, h)), pl.BlockSpec((hb, 1, HD, HD), lambda h, n: (h, n, 0, 0))],
        out_shape=[jax.ShapeDtypeStruct((T, NH * HD), F32), jax.ShapeDtypeStruct((NH, N, HD, HD), F32)],
        scratch_shapes=[pltpu.VMEM((hb, HD, HD), F32)], compiler_params=_params(2))(qkv, qkv, qkv, g, beta)


def _gdn_chunk_bwd(name, qkv, g, beta, states, do):
    T = qkv.shape[0]
    N = T // CHUNK
    C = CHUNK

    hb = _GDN_HB
    w = hb * HD
    assert hb == NH

    def body(q_ref, k_ref, v_ref, g_ref, b_ref, st_ref, do_ref, dqkv_ref, dg_ref, db_ref, dS):
        hg, n = pl.program_id(0), pl.program_id(1)

        @pl.when(n == 0)
        def _():
            dS[...] = jnp.zeros_like(dS)

        gblk, bblk = g_ref[...], b_ref[...]
        sls = [slice(i * HD, (i + 1) * HD) for i in range(hb)]
        res = _interleave(one_head(hg * hb + i, gblk, bblk, q_ref[:, sls[i]], k_ref[:, sls[i]], v_ref[:, sls[i]],
                                   st_ref[i, 0], do_ref[:, sls[i]], dS[i]) for i in range(hb))
        for i, (dq, dk, dv, dg, db, ds_new) in enumerate(res):
            dqkv_ref[:, sls[i]] = dq
            dqkv_ref[:, slice(w + i * HD, w + (i + 1) * HD)] = dk
            dqkv_ref[:, slice(2 * w + i * HD, 2 * w + (i + 1) * HD)] = dv
            dg_ref[i] = dg
            db_ref[i] = db
            dS[i] = ds_new

    def one_head(h, gblk, bblk, q, k, v, s, do, ds):
        c = yield from _chunk_common(q, k, v, gblk, bblk, h, HF)
        eg, egl, gl, beta, decay, tinv = c["eg"], c["egl"], c["gl"], c["beta"], c["decay"], c["tinv"]
        v_new = c["u"] - _dot(c["w"], s, "nn", HF)
        dq_dec = _dot(do, s, "nt", HF)
        yield
        dv_new = _dot(c["attn"], do, "tn", HF) + _dot(c["k_dec"], ds, "nn", HF)
        dk_dec = _dot(v_new, ds, "nt", HF)
        dgl = jnp.sum(jnp.sum(s * ds, axis=1, keepdims=True), axis=0, keepdims=True)
        yield
        ds_new = ds * gl + _dot(c["q_dec"], do, "tn", HF) - _dot(c["w"], dv_new, "tn", HF)
        dattn = jnp.where(c["incl"], _dot(do, v_new, "nt", HF), 0.0)
        dw = -_dot(dv_new, s, "nt", HF)
        yield
        dvb = _dot(tinv, dv_new, "tn", HS)
        dkbg = _dot(tinv, dw, "tn", HS)
        yield
        dA = -(_dot(dvb, c["u"], "nt", HS) + _dot(dkbg, c["w"], "nt", HS))
        yield
        dL = jnp.where(c["strict"], dA, 0.0)
        dm1 = dL * decay
        dqk = dattn * decay
        xdec = (dL * c["m1"] + dattn * c["qk"]) * decay
        dkb = _dot(dm1, k, "nn", HS) + dkbg * eg
        dk = _dot(dm1, c["kb"], "tn", HF) + _dot(dqk, q, "tn", HF) + dk_dec * egl + dkb * beta
        dq = _dot(dqk, k, "nn", HF) + dq_dec * eg
        yield
        dkd_kd = jnp.sum(dk_dec * c["k_dec"], axis=1, keepdims=True)
        dgc = (jnp.sum(xdec, axis=1, keepdims=True) - _dot(xdec, jnp.ones((C, HD), F32), "tn", HS)
               + jnp.sum(dq_dec * c["q_dec"], axis=1, keepdims=True) - dkd_kd
               + jnp.sum(dkbg * c["kbg"], axis=1, keepdims=True))
        dgcl = jnp.sum(dkd_kd, axis=0, keepdims=True) + dgl * gl
        dgc = dgc + jnp.where(c["rows"] == C - 1, dgcl, 0.0)
        ri = lax.broadcasted_iota(jnp.int32, (C, C), 0)
        ci = lax.broadcasted_iota(jnp.int32, (C, C), 1)
        dg = _dot((ci >= ri).astype(F32), dgc, "nn", HI)
        db = jnp.broadcast_to(jnp.sum(dkb * k, axis=1, keepdims=True) + jnp.sum(dvb * v, axis=1, keepdims=True), (C, HD))
        return dq, dk, dvb * beta, dg, db, ds_new

    blk = lambda off: pl.BlockSpec((C, w), lambda h, n, off=off: (N - 1 - n, off + h))
    gspec = pl.BlockSpec((C, HD), lambda h, n: (N - 1 - n, 0))
    ospec = pl.BlockSpec((C, w), lambda h, n: (N - 1 - n, h))
    hspec = pl.BlockSpec((hb, C, HD), lambda h, n: (h, N - 1 - n, 0))
    return pl.pallas_call(
        body, name=name, grid=(NH // hb, N),
        in_specs=[blk(0), blk(NH // hb), blk(2 * NH // hb), gspec, gspec,
                  pl.BlockSpec((hb, 1, HD, HD), lambda h, n: (h, N - 1 - n, 0, 0)), ospec],
        out_specs=[pl.BlockSpec((C, 3 * w), lambda h, n: (N - 1 - n, 0)), hspec, hspec],
        out_shape=[jax.ShapeDtypeStruct((T, 3 * NH * HD), F32)] + [jax.ShapeDtypeStruct((NH, T, HD), F32)] * 2,
        scratch_shapes=[pltpu.VMEM((hb, HD, HD), F32)], compiler_params=_params(2))(qkv, qkv, qkv, g, beta, states, do)


_GATE_CB = GDN_QKV // (NH * HD)


def _gdn_gated_norm_fwd(name, o, proj, ng):
    def fn(o, gate, ng):
        outs = []
        for h in range(NH):
            sl = slice(h * HD, (h + 1) * HD)
            oh, gh = o[:, sl], gate[:, sl]
            outs.append(oh * _rms(oh) * ng * (gh * _sig(gh)))
        return jnp.concatenate(outs, axis=1)
    return _rowwise(name, fn, [o, (proj, NH * HD, _GATE_CB)], [ng], [(NH * HD, BF16)])[0]


def _gdn_gated_norm_bwd(name, don, o, proj, ng):
    def fn(don, o, gate, ng):
        dos, dgs = [], []
        dng = jnp.zeros((1, HD), F32)
        for h in range(NH):
            sl = slice(h * HD, (h + 1) * HD)
            oh, gh, dh = o[:, sl], gate[:, sl], don[:, sl]
            r = _rms(oh)
            xh = oh * r
            sg = _sig(gh)
            dn = dh * (gh * sg)
            dgs.append(dh * (xh * ng) * (sg * (1.0 + gh * (1.0 - sg))))
            dng = dng + jnp.sum(dn * xh, axis=0, keepdims=True)
            dxh = dn * ng
            dos.append(r * (dxh - xh * jnp.mean(dxh * xh, axis=-1, keepdims=True)))
        return jnp.concatenate(dos, axis=1), jnp.concatenate(dgs, axis=1), dng
    return _rowwise(name, fn, [don, o, (proj, NH * HD, _GATE_CB)], [ng], [(NH * HD, F32), (NH * HD, BF16)], sums=[HD])


def _rot(x):
    lane = lax.broadcasted_iota(jnp.int32, x.shape, 1)
    return jnp.where(lane < ROPE // 2, -pltpu.roll(x, HD - ROPE // 2, 1), pltpu.roll(x, ROPE // 2, 1))


def _rot_t(x):
    lane = lax.broadcasted_iota(jnp.int32, x.shape, 1)
    return jnp.where(lane < ROPE // 2, pltpu.roll(x, HD - ROPE // 2, 1), -pltpu.roll(x, ROPE // 2, 1))


def _rope_tables(pos_col):
    lane = jnp.arange(HD)
    inv_freq = ROPE_THETA ** (-(2.0 * (lane % (ROPE // 2)).astype(F32)) / ROPE)
    inv_freq = jnp.where(lane < ROPE, inv_freq, 0.0).astype(F32)[None, :]
    valid = (lane < ROPE).astype(F32)[None, :]

    def fn(pos, inv_freq, valid):
        ang = pos.astype(F32) * inv_freq
        return jnp.cos(ang) * valid, jnp.sin(ang) * valid
    return _rowwise("rope_tables", fn, [pos_col], [inv_freq, valid], [(HD, F32), (HD, F32)])


def _mla_pre_fwd(name, proj, cos, sin, qg, kvg):
    def fn(p, cos, sin, qg, kvg):
        cq, ckv, kr = p[:, :Q_RANK], p[:, Q_RANK:Q_RANK + KV_RANK], p[:, Q_RANK + KV_RANK:]
        return cq * _rms(cq) * qg, ckv * _rms(ckv) * kvg, kr * cos + _rot(kr) * sin
    return _rowwise(name, fn, [proj, cos, sin], [qg, kvg], [(Q_RANK, BF16), (KV_RANK, BF16), (HD, BF16)])


def _rms_bwd(dy, x, g):
    r = _rms(x)
    xh = x * r
    dxh = dy * g
    return r * (dxh - xh * jnp.mean(dxh * xh, axis=-1, keepdims=True)), jnp.sum(dy * xh, axis=0, keepdims=True)


def _mla_pre_bwd(name, proj, dcqn, dckvn, dkr, cos, sin, qg, kvg):
    def fn(p, dcqn, dckvn, dkr, cos, sin, qg, kvg):
        cq, ckv = p[:, :Q_RANK], p[:, Q_RANK:Q_RANK + KV_RANK]
        dcq, dqg = _rms_bwd(dcqn, cq, qg)
        dckv, dkvg = _rms_bwd(dckvn, ckv, kvg)
        dkr_pre = dkr * cos + _rot_t(dkr * sin)
        return jnp.concatenate([dcq, dckv, dkr_pre], axis=1), dqg, dkvg
    return _rowwise(name, fn, [proj, dcqn, dckvn, dkr, cos, sin], [qg, kvg], [(MLA_INK, BF16)], sums=[Q_RANK, KV_RANK])


def _mla_q_fwd(name, q, cos, sin):
    def fn(qn, qr, cos, sin):
        outs = []
        for h in range(NH):
            x = qr[:, h * HD:(h + 1) * HD]
            outs.append(x * cos + _rot(x) * sin)
        return qn, jnp.concatenate(outs, axis=1)
    return _rowwise(name, fn, [(q, NH * HD, 0), (q, NH * HD, 1), cos, sin], [], [(NH * HD, BF16), (NH * HD, BF16)])


def _mla_q_bwd(name, dqn, dqr, cos, sin):
    def fn(dqn, dqr, cos, sin):
        outs = [dqn]
        for h in range(NH):
            z = dqr[:, h * HD:(h + 1) * HD]
            outs.append(z * cos + _rot_t(z * sin))
        return jnp.concatenate(outs, axis=1)
    return _rowwise(name, fn, [dqn, dqr, cos, sin], [], [(2 * NH * HD, BF16)])[0]


def _att_probs(q2, k2, row0):
    s = _dot(q2, k2, "nt") * ATT_SCALE
    qpos = row0 + lax.broadcasted_iota(jnp.int32, s.shape, 0)
    kpos = lax.broadcasted_iota(jnp.int32, s.shape, 1)
    s = jnp.where(kpos <= qpos, s, -1e30)
    e = jnp.exp(s - jnp.max(s, axis=1, keepdims=True))
    return e, 1.0 / jnp.sum(e, axis=1, keepdims=True)


def _mla_attn_fwd(name, qn, qr, kv, kr, tq=256):
    T = qn.shape[0]
    tq = min(tq, T)

    def body(qn_ref, qr_ref, kn_ref, v_ref, kr_ref, o_ref):
        i = pl.program_id(1)
        for blk in range(T // tq):
            @pl.when(i == blk)
            def _(blk=blk):
                keys = pl.ds(0, (blk + 1) * tq)
                q2 = jnp.concatenate([qn_ref[...], qr_ref[...]], axis=1)
                k2 = jnp.concatenate([kn_ref[keys, :], kr_ref[keys, :]], axis=1)
                e, inv_l = _att_probs(q2, k2, blk * tq)
                o_ref[...] = (_dot(e.astype(BF16), v_ref[keys, :], "nn") * inv_l).astype(o_ref.dtype)

    qspec = pl.BlockSpec((tq, HD), lambda h, i: (i, h))
    return pl.pallas_call(
        body, name=name, grid=(NH, T // tq),
        in_specs=[qspec, qspec, pl.BlockSpec((T, HD), lambda h, i: (0, h)), pl.BlockSpec((T, HD), lambda h, i: (0, NH + h)),
                  pl.BlockSpec((T, HD), lambda h, i: (0, 0))],
        out_specs=qspec, out_shape=jax.ShapeDtypeStruct((T, NH * HD), BF16), compiler_params=_params(2))(qn, qr, kv, kv, kr)


def _mla_attn_bwd(name, qn, qr, kv, kr, do, o, tq=256):
    T = qn.shape[0]
    tq = min(tq, T)

    def body(qn_ref, qr_ref, kn_ref, v_ref, kr_ref, do_ref, o_ref, dqn_ref, dqr_ref, dkn_ref, dv_ref, dkr_ref):
        h, i = pl.program_id(0), pl.program_id(1)

        @pl.when(i == 0)
        def _():
            dkn_ref[...] = jnp.zeros_like(dkn_ref)
            dv_ref[...] = jnp.zeros_like(dv_ref)

        @pl.when((i == 0) & (h == 0))
        def _():
            dkr_ref[...] = jnp.zeros_like(dkr_ref)

        for blk in range(T // tq):
            @pl.when(i == blk)
            def _(blk=blk):
                keys = pl.ds(0, (blk + 1) * tq)
                do, v = do_ref[...], v_ref[keys, :]
                q2 = jnp.concatenate([qn_ref[...], qr_ref[...]], axis=1)
                k2 = jnp.concatenate([kn_ref[keys, :], kr_ref[keys, :]], axis=1)
                e, inv_l = _att_probs(q2, k2, blk * tq)
                dp = _dot(do, v, "nt")
                delta = jnp.sum(do.astype(F32) * o_ref[...].astype(F32), axis=1, keepdims=True)
                ds = (e * ((dp - delta) * (inv_l * ATT_SCALE))).astype(BF16)
                dq2 = _dot(ds, k2, "nn")
                dqn_ref[...] = dq2[:, :HD]
                dqr_ref[...] = dq2[:, HD:]
                dk2 = _dot(ds, q2, "tn")
                dkn_ref[keys, :] += dk2[:, :HD]
                dkr_ref[keys, :] += dk2[:, HD:]
                dv_ref[keys, :] += _dot(e.astype(BF16), (do.astype(F32) * inv_l).astype(BF16), "tn")

    qspec = pl.BlockSpec((tq, HD), lambda h, i: (i, h))
    kspec = pl.BlockSpec((T, HD), lambda h, i: (0, h))
    return pl.pallas_call(
        body, name=name, grid=(NH, T // tq),
        in_specs=[qspec, qspec, kspec, pl.BlockSpec((T, HD), lambda h, i: (0, NH + h)),
                  pl.BlockSpec((T, HD), lambda h, i: (0, 0)), qspec, qspec],
        out_specs=[qspec, qspec, kspec, kspec, pl.BlockSpec((T, HD), lambda h, i: (0, 0))],
        out_shape=[jax.ShapeDtypeStruct((T, NH * HD), F32)] * 4 + [jax.ShapeDtypeStruct((T, HD), F32)],
        compiler_params=_params(2))(qn, qr, kv, kv, kr, do, o)


def _mod_rows(mod, layer):
    return [(mod, layer, i, D) for i in range(N_MOD)]


def _local_step(x, target, pos_col, mod, weights_of, P, on_grads):
    cos, sin = _rope_tables(pos_col)
    saved = []
    sh_m, sc_m = _mod_rows(mod, 0)[:2]
    h = _norm_mod_fwd("norm_mix0", x, (P["norm_mix_g"], 0, 0, D), sc_m, sh_m)
    for l in range(DEPTH):
        j = l // 2
        sh_m, sc_m, ga_m, sh_f, sc_f, ga_f = _mod_rows(mod, l)
        s = dict(x0=x)
        W = weights_of(l, h)
        s.update(h=h, W=W)
        if l % 2 == 0:
            proj = _mm(f"gdn_in{j}", h, W["gdn_in"], "nn", tn=GDN_INK // 2)
            qkv = _gdn_conv_fwd(f"gdn_conv{j}", proj, P["gdn_cw"][j])
            g, beta = _gdn_gates_fwd(f"gdn_gates{j}", proj, P["gdn_alog"][j], P["gdn_dtb"][j])
            o, states = _gdn_chunk_fwd(f"gdn_chunk{j}", qkv, g, beta)
            on = _gdn_gated_norm_fwd(f"gdn_gnorm{j}", o, proj, P["gdn_ng"][j])
            y = _mm(f"gdn_out{j}", on, W["gdn_out"], "nn")
            s.update(proj=proj, qkv=qkv, g=g, beta=beta, o=o, states=states, on=on)
        else:
            proj = _mm(f"mla_in{j}", h, W["mla_in"], "nn")
            cqn, ckvn, kr = _mla_pre_fwd(f"mla_pre{j}", proj, cos, sin, P["mla_qg"][j], P["mla_kvg"][j])
            q = _mm(f"mla_uq{j}", cqn, W["mla_uq"], "nn")
            kv = _mm(f"mla_ukv{j}", ckvn, W["mla_ukv"], "nn", out_dtype=BF16)
            qn, qr = _mla_q_fwd(f"mla_q{j}", q, cos, sin)
            o = _mla_attn_fwd(f"mla_attn{j}", qn, qr, kv, kr)
            y = _mm(f"mla_out{j}", o, W["mla_out"], "nn")
            s.update(proj=proj, cqn=cqn, ckvn=ckvn, kr=kr, kv=kv, qn=qn, qr=qr, o=o)
        s["y"] = y
        x, h2 = _residual_norm_fwd(f"res_mix{l}", x, y, ga_m, (P["norm_ffn_g"], l, 0, D), sc_f, sh_f)
        s["x1"] = x
        fa, fb, sw = _ffn_up(f"ffn_up{l}", h2, W["ffn_g"], W["ffn_u"], 0)
        yf = _ffn_down(f"ffn_down{l}", sw, W["ffn_d"], 0)
        if l + 1 < DEPTH:
            sh_n, sc_n = _mod_rows(mod, l + 1)[:2]
            x, h = _residual_norm_fwd(f"res_ffn{l}", x, yf, ga_f, (P["norm_mix_g"], l + 1, 0, D), sc_n, sh_n)
        else:
            x = _residual_fwd(f"res_ffn{l}", x, yf, ga_f)
        s.update(h2=h2, fa=fa, fb=fb, sw=sw, yf=yf)
        saved.append(s)

    dx, loss, d_final = _loss_head(x, target, P["final_g"])
    gP = dict(loss=loss, final_g=d_final, norm_mix_g=[None] * DEPTH, norm_ffn_g=[None] * DEPTH,
              gdn_cw=[None] * 2, gdn_alog=[None] * 2, gdn_dtb=[None] * 2, gdn_ng=[None] * 2,
              mla_qg=[None] * 2, mla_kvg=[None] * 2)
    dmod = [None] * DEPTH
    dyf, d_ga_f = _residual_bwd(f"res_ffn_b{DEPTH - 1}", dx, saved[-1]["yf"], _mod_rows(mod, DEPTH - 1)[5])
    for l in reversed(range(DEPTH)):
        j = l // 2
        s = saved[l]
        W = s["W"]
        sh_m, sc_m, ga_m, sh_f, sc_f, ga_f = _mod_rows(mod, l)
        da, db = _ffn_down_bwd(f"ffn_down_dx{l}", dyf, W["ffn_d"], s["fa"], s["fb"], 0)
        g_down = _ffn_down_dw(f"ffn_down_dw{l}", s["sw"], dyf)
        g_gate, g_up = _ffn_up_dw(f"ffn_up_dw{l}", s["h2"], da, db)
        on_grads(l, "ffn", dict(ffn_w_gate=g_gate, ffn_w_up=g_up, ffn_w_down=g_down))
        dh2 = _ffn_up_dx(f"ffn_up_dx{l}", da, db, W["ffn_g"], W["ffn_u"], 0)
        dx, dy, d_sh_f, d_sc_f, gP["norm_ffn_g"][l], d_ga_m = _norm_residual_bwd(
            f"norm_ffn_b{l}", dh2, s["x1"], dx, (P["norm_ffn_g"], l, 0, D), sc_f, s["y"], ga_m)
        if l % 2 == 0:
            don = _mm(f"gdn_out_dx{j}", dy, W["gdn_out"], "nt")
            g_out = _mm(f"gdn_out_dw{j}", s["on"], dy, "tn", out_dtype=BF16)
            do, dgate, gP["gdn_ng"][j] = _gdn_gated_norm_bwd(f"gdn_gnorm_b{j}", don, s["o"], s["proj"], P["gdn_ng"][j])
            dqkv, dg_h, db_h = _gdn_chunk_bwd(f"gdn_chunk_b{j}", s["qkv"], s["g"], s["beta"], s["states"], do)
            dab_, gP["gdn_alog"][j], gP["gdn_dtb"][j] = _gdn_gates_bwd(f"gdn_gates_b{j}", s["proj"], dg_h, db_h,
                                                                        P["gdn_alog"][j], P["gdn_dtb"][j])
            dpre, gP["gdn_cw"][j] = _gdn_conv_bwd(f"gdn_conv_b{j}", s["proj"], P["gdn_cw"][j], dqkv)
            dproj = jnp.concatenate([dpre, dgate, dab_], axis=1)
            g_in = _mm(f"gdn_in_dw{j}", s["h"], dproj, "tn", out_dtype=BF16, tn=GDN_INK // 2)
            on_grads(l, "mix", dict(gdn_w_in=_uncols(_gdn_in_from_kernel(g_in)), gdn_w_out=_unrows(g_out)))
            dh = _mm(f"gdn_in_dx{j}", dproj, W["gdn_in"], "nt")
        else:
            do = _mm(f"mla_out_dx{j}", dy, W["mla_out"], "nt", out_dtype=BF16)
            g_out = _mm(f"mla_out_dw{j}", s["o"], dy, "tn", out_dtype=BF16)
            dqn, dqr, dkn, dv, dkr = _mla_attn_bwd(f"mla_attn_b{j}", s["qn"], s["qr"], s["kv"], s["kr"], do, s["o"])
            dq = _mla_q_bwd(f"mla_q_b{j}", dqn, dqr, cos, sin)
            dkv = jnp.concatenate([dkn, dv], axis=1)
            g_uq = _mm(f"mla_uq_dw{j}", s["cqn"], dq, "tn", out_dtype=BF16)
            dcqn = _mm(f"mla_uq_dx{j}", dq, W["mla_uq"], "nt")
            g_ukv = _mm(f"mla_ukv_dw{j}", s["ckvn"], dkv, "tn", out_dtype=BF16)
            dckvn = _mm(f"mla_ukv_dx{j}", dkv, W["mla_ukv"], "nt")
            dproj, gP["mla_qg"][j], gP["mla_kvg"][j] = _mla_pre_bwd(f"mla_pre_b{j}", s["proj"], dcqn, dckvn, dkr, cos, sin,
                                                                     P["mla_qg"][j], P["mla_kvg"][j])
            g_in = _mm(f"mla_in_dw{j}", s["h"], dproj, "tn", out_dtype=BF16)
            on_grads(l, "mix", dict(mla_w_in=_unrows(g_in[:, :Q_RANK + KV_RANK + ROPE]), mla_w_uq=_uncols(_mla_uq_from_kernel(g_uq)),
                                    mla_w_ukv=_uncols(_mla_ukv_from_kernel(g_ukv)), mla_w_out=_unrows(g_out)))
            dh = _mm(f"mla_in_dx{j}", dproj, W["mla_in"], "nt")
        if l > 0:
            dx, dyf_prev, d_sh_m, d_sc_m, gP["norm_mix_g"][l], d_ga_f_prev = _norm_residual_bwd(
                f"norm_mix_b{l}", dh, s["x0"], dx, (P["norm_mix_g"], l, 0, D), sc_m, saved[l - 1]["yf"], _mod_rows(mod, l - 1)[5])
        else:
            dx, d_sh_m, d_sc_m, gP["norm_mix_g"][l] = _norm_mod_bwd(f"norm_mix_b{l}", dh, s["x0"], dx,
                                                                     (P["norm_mix_g"], l, 0, D), sc_m)
        dmod[l] = jnp.concatenate([d_sh_m, d_sc_m, d_ga_m, d_sh_f, d_sc_f, d_ga_f], axis=1)
        if l > 0:
            dyf, d_ga_f = dyf_prev, d_ga_f_prev
    return dx, jnp.concatenate(dmod, axis=0), gP


def _pad_cols(a, width):
    return jnp.pad(a, ((0, 0), (0, width - a.shape[1])))


def _gdn_in_to_kernel(w):
    m = GDN_QKV + NH * HD
    return jnp.concatenate([w[:, :m], _pad_cols(w[:, m:m + NH], HD), _pad_cols(w[:, m + NH:], HD)], axis=1)


def _gdn_in_from_kernel(g):
    m = GDN_QKV + NH * HD
    return jnp.concatenate([g[:, :m], g[:, m:m + NH], g[:, m + HD:m + HD + NH]], axis=1)


def _mla_uq_to_kernel(w):
    w3 = w.reshape(Q_RANK, NH, HD + ROPE)
    rope = jnp.pad(w3[:, :, HD:], ((0, 0), (0, 0), (0, HD - ROPE)))
    return jnp.concatenate([w3[:, :, :HD].reshape(Q_RANK, NH * HD), rope.reshape(Q_RANK, NH * HD)], axis=1)


def _mla_uq_from_kernel(g):
    gn = g[:, :NH * HD].reshape(Q_RANK, NH, HD)
    gr = g[:, NH * HD:].reshape(Q_RANK, NH, HD)[:, :, :ROPE]
    return jnp.concatenate([gn, gr], axis=2).reshape(Q_RANK, NH * (HD + ROPE))


def _mla_ukv_to_kernel(w):
    w3 = w.reshape(KV_RANK, NH, 2 * HD)
    return jnp.concatenate([w3[:, :, :HD].reshape(KV_RANK, NH * HD), w3[:, :, HD:].reshape(KV_RANK, NH * HD)], axis=1)


def _mla_ukv_from_kernel(g):
    gk = g[:, :NH * HD].reshape(KV_RANK, NH, HD)
    gv = g[:, NH * HD:].reshape(KV_RANK, NH, HD)
    return jnp.concatenate([gk, gv], axis=2).reshape(KV_RANK, NH * 2 * HD)


def _cols(t):
    return jnp.moveaxis(t, 0, 1).reshape(t.shape[1], -1)


def _uncols(g):
    return jnp.moveaxis(g.reshape(g.shape[0], 4, -1), 1, 0)


def _rows(t):
    return t.reshape(-1, t.shape[2])


def _unrows(g):
    return g.reshape(4, -1, g.shape[1])


def _layer_weights(layer):
    mixer = ("gdn_w_in", "gdn_w_out") if layer % 2 == 0 else ("mla_w_in", "mla_w_uq", "mla_w_ukv", "mla_w_out")
    return [(n, layer // 2) for n in mixer] + [(n, layer) for n in ("ffn_w_gate", "ffn_w_up", "ffn_w_down")]


def _weights_to_kernel(layer, g):
    out = dict(ffn_g=g["ffn_w_gate"], ffn_u=g["ffn_w_up"], ffn_d=g["ffn_w_down"])
    if layer % 2 == 0:
        out.update(gdn_in=_gdn_in_to_kernel(_cols(g["gdn_w_in"])), gdn_out=_rows(g["gdn_w_out"]))
    else:
        out.update(mla_in=_pad_cols(_rows(g["mla_w_in"]), MLA_INK), mla_uq=_mla_uq_to_kernel(_cols(g["mla_w_uq"])),
                   mla_ukv=_mla_ukv_to_kernel(_cols(g["mla_w_ukv"])), mla_out=_rows(g["mla_w_out"]))
    return out


def _small_to_kernel(norm_mix_g, norm_ffn_g, final_norm_g, gdn_conv_w, gdn_a_log, gdn_dt_bias, gdn_norm_g, q_norm_g, kv_norm_g):
    return dict(
        norm_mix_g=norm_mix_g, norm_ffn_g=norm_ffn_g, final_g=final_norm_g.reshape(1, D),
        gdn_cw=[jnp.transpose(gdn_conv_w[j]) for j in range(2)],
        gdn_alog=[_pad_cols(gdn_a_log[j:j + 1], HD) for j in range(2)],
        gdn_dtb=[_pad_cols(gdn_dt_bias[j:j + 1], HD) for j in range(2)],
        gdn_ng=[gdn_norm_g[j:j + 1] for j in range(2)],
        mla_qg=[q_norm_g[j:j + 1] for j in range(2)],
        mla_kvg=[kv_norm_g[j:j + 1] for j in range(2)],
    )


_CHIP_FLIPS = ((1, 0), (0, 1), (1, 1))
_ANY = pl.BlockSpec(memory_space=pl.ANY)


def _me():
    return lax.axis_index("x"), lax.axis_index("y"), lax.axis_index("c")


def _chip_peer(dx, dy):
    x, y, c = _me()
    return ((1 - x) if dx else x, (1 - y) if dy else y, c)


def _rcopy(src, dst, send_sem, recv_sem, to):
    return pltpu.make_async_remote_copy(src_ref=src, dst_ref=dst, send_sem=send_sem, recv_sem=recv_sem,
                                        device_id=to, device_id_type=MESH)


def _allgather4(name, a, halves=False):
    R, C = a.shape
    rh = R // 2 if halves else R

    def body(a_ref, out_ref, send_sems, recv_sems, local_sem):
        x, y, c = _me()
        me = 2 * x + y
        src = a_ref.at[pl.ds(c * rh, rh)] if halves else a_ref
        local = pltpu.make_async_copy(src, out_ref.at[me], local_sem)
        local.start()
        sends = []
        for k, (dx, dy) in enumerate(_CHIP_FLIPS):
            cp = _rcopy(src, out_ref.at[me], send_sems.at[k], recv_sems.at[k], _chip_peer(dx, dy))
            cp.start()
            sends.append(cp)
        for k, (dx, dy) in enumerate(_CHIP_FLIPS):
            px, py, _ = _chip_peer(dx, dy)
            _rcopy(src, out_ref.at[2 * px + py], send_sems.at[k], recv_sems.at[k], _chip_peer(dx, dy)).wait_recv()
        for cp in sends:
            cp.wait_send()
        local.wait()

    return pl.pallas_call(
        body, name=name, in_specs=[_ANY], out_specs=_ANY, out_shape=jax.ShapeDtypeStruct((4, rh, C), a.dtype),
        scratch_shapes=[pltpu.SemaphoreType.DMA((3,)), pltpu.SemaphoreType.DMA((3,)), pltpu.SemaphoreType.DMA(())])(a)


_NCH = 4


def _dma_sems(*counts):
    return [pltpu.SemaphoreType.DMA((n,)) for n in counts]


def _slot_tile(rows, cap=512):
    best = rows
    for tr in range(16, min(rows, cap) + 1, 16):
        if rows % tr == 0:
            best = tr
    return best


def _cast_into_slot(name, a, chip, row0, rows):
    C = a.shape[1]
    tr = _slot_tile(rows)
    assert row0 % tr == 0
    first = row0 // tr

    def body(c_ref, a_ref, o_ref):
        o_ref[0] = a_ref[...].astype(o_ref.dtype)

    grid_spec = pltpu.PrefetchScalarGridSpec(
        num_scalar_prefetch=1, grid=(rows // tr,), in_specs=[pl.BlockSpec((tr, C), lambda i, c_ref: (first + i, 0))],
        out_specs=pl.BlockSpec((1, tr, C), lambda i, c_ref: (c_ref[0], i, 0)))
    return pl.pallas_call(body, name=name, grid_spec=grid_spec, out_shape=jax.ShapeDtypeStruct((4, rows, C), BF16),
                          compiler_params=_params(1))(chip, a)


def _chunks(rows, align):
    for nch in (_NCH, 2):
        if rows % (nch * align) == 0:
            return nch
    return 1


def _gather_exchange(out, ici_s, ici_r, d2d_s, d2d_r):
    n = len(out)
    x, y, c = _me()
    me = 2 * x + y
    sib = (x, y, 1 - c)
    peers = [_chip_peer(dx, dy) for dx, dy in _CHIP_FLIPS]
    for t in range(n):
        h = out[t].shape[1] // 2
        nch = _chunks(h, 16)
        ch = h // nch
        for k, peer in enumerate(peers):
            for i in range(nch):
                blk = out[t].at[me, pl.ds(c * h + i * ch, ch)]
                _rcopy(blk, blk, ici_s.at[3 * t + k], ici_r.at[3 * t + k], peer).start()
    for t in range(n):
        h = out[t].shape[1] // 2
        nch = _chunks(h, 16)
        ch = h // nch
        for k, peer in enumerate(peers):
            pchip = 2 * peer[0] + peer[1]
            got = out[t].at[pchip, pl.ds(c * h, h)]
            _rcopy(got, got, ici_s.at[3 * t + k], ici_r.at[3 * t + k], peer).wait_recv()
            for i in range(nch):
                blk = out[t].at[pchip, pl.ds(c * h + i * ch, ch)]
                _rcopy(blk, blk, d2d_s.at[3 * t + k], d2d_r.at[3 * t + k], sib).start()
    for t in range(n):
        h = out[t].shape[1] // 2
        for k, peer in enumerate(peers):
            pchip = 2 * peer[0] + peer[1]
            other = out[t].at[pchip, pl.ds((1 - c) * h, h)]
            _rcopy(other, other, d2d_s.at[3 * t + k], d2d_r.at[3 * t + k], sib).wait_recv()
            _rcopy(other, other, ici_s.at[3 * t + k], ici_r.at[3 * t + k], peer).wait_send()
            _rcopy(other, other, d2d_s.at[3 * t + k], d2d_r.at[3 * t + k], sib).wait_send()


def _gather_weights(name, bufs):
    n = len(bufs)

    def body(*refs):
        _gather_exchange(refs[n:2 * n], *refs[2 * n:])

    return pl.pallas_call(
        body, name=name, in_specs=[_ANY] * n, out_specs=[_ANY] * n,
        out_shape=[jax.ShapeDtypeStruct(s.shape, s.dtype) for s in bufs],
        input_output_aliases={t: t for t in range(n)},
        scratch_shapes=_dma_sems(3 * n, 3 * n, 3 * n, 3 * n))(*bufs)


def _gather_weights_async(name, collective_id, bufs):
    n = len(bufs)
    refs = [jax.new_ref(b, memory_space=pltpu.MemorySpace.HBM) for b in bufs]

    @pl.kernel(mesh=plsc.ScalarSubcoreMesh(axis_name="sequencer", num_cores=1), name=name,
               scratch_types=tuple(_dma_sems(3 * n, 3 * n, 3 * n, 3 * n)),
               compiler_params=pltpu.CompilerParams(collective_id=collective_id))
    def launch(ici_s, ici_r, d2d_s, d2d_r):
        x, y, c = _me()
        barrier = pltpu.get_barrier_semaphore()
        for peer in [_chip_peer(dx, dy) for dx, dy in _CHIP_FLIPS] + [(x, y, 1 - c)]:
            pl.semaphore_signal(barrier, inc=1, device_id=peer, device_id_type=MESH)
        pl.semaphore_wait(barrier, 4)
        _gather_exchange(refs, ici_s, ici_r, d2d_s, d2d_r)

    launch()
    return [r[...] for r in refs]


def _rs_split(name, grads):
    n = len(grads)

    def body(*refs):
        g, out = refs[:n], refs[n:2 * n]
        send, recv = refs[2 * n:]
        x, y, c = _me()
        sib = (x, y, 1 - c)
        for t in range(n):
            h = g[t].shape[1] // 2
            for d in range(4):
                _rcopy(g[t].at[d, pl.ds((1 - c) * h, h)], out[t].at[d], send.at[t], recv.at[t], sib).start()
        for t in range(n):
            _rcopy(out[t], out[t], send.at[t], recv.at[t], sib).wait()

    return pl.pallas_call(
        body, name=name, in_specs=[_ANY] * n, out_specs=[_ANY] * n,
        out_shape=[jax.ShapeDtypeStruct((4, s.shape[1] // 2, s.shape[2]), s.dtype) for s in grads],
        scratch_shapes=_dma_sems(n, n))(*grads)


def _pair_add(name, g, theirs, core_chip):
    _, R, C = g.shape
    h = R // 2
    tr = _slot_tile(h)
    nb = h // tr

    def body(s_ref, g_ref, t_ref, p_ref, o_ref):
        val = (g_ref[...].astype(F32) + t_ref[...].astype(F32)).astype(p_ref.dtype)
        p_ref[...] = val

        @pl.when(pl.program_id(1) == s_ref[1])
        def _():
            o_ref[...] = val

    spec = pl.BlockSpec((1, tr, C), lambda i, d, s_ref: (d, i, 0))
    grid_spec = pltpu.PrefetchScalarGridSpec(
        num_scalar_prefetch=1, grid=(nb, 4),
        in_specs=[pl.BlockSpec((1, tr, C), lambda i, d, s_ref: (d, s_ref[0] * nb + i, 0)), spec],
        out_specs=[spec, pl.BlockSpec((1, tr, C), lambda i, d, s_ref: (s_ref[1], i, 0))])
    half = jax.ShapeDtypeStruct((4, h, C), BF16)
    return pl.pallas_call(body, name=name, grid_spec=grid_spec, out_shape=[half, half],
                          compiler_params=_params(2))(core_chip, g, theirs)


def _rs_alltoall_async(name, collective_id, parts, bufs):
    n = len(parts)
    p = [jax.new_ref(a, memory_space=pltpu.MemorySpace.HBM) for a in parts]
    out = [jax.new_ref(b, memory_space=pltpu.MemorySpace.HBM) for b in bufs]

    @pl.kernel(mesh=plsc.ScalarSubcoreMesh(axis_name="sequencer", num_cores=1), name=name,
               scratch_types=tuple(_dma_sems(3 * n, 3 * n)),
               compiler_params=pltpu.CompilerParams(collective_id=collective_id))
    def launch(send, recv):
        barrier = pltpu.get_barrier_semaphore()
        for peer in [_chip_peer(dx, dy) for dx, dy in _CHIP_FLIPS]:
            pl.semaphore_signal(barrier, inc=1, device_id=peer, device_id_type=MESH)
        pl.semaphore_wait(barrier, 3)
        _alltoall_exchange(p, out, send, recv)

    launch()
    return [r[...] for r in out]


def _alltoall_exchange(p, out, send, recv):
    x, y, c = _me()
    me = 2 * x + y
    peers = [_chip_peer(dx, dy) for dx, dy in _CHIP_FLIPS]
    for t in range(len(p)):
        h = p[t].shape[1]
        nch = _chunks(h, 16)
        ch = h // nch
        for k, peer in enumerate(peers):
            pchip = 2 * peer[0] + peer[1]
            for i in range(nch):
                rows = pl.ds(i * ch, ch)
                _rcopy(p[t].at[pchip, rows], out[t].at[me, rows], send.at[3 * t + k], recv.at[3 * t + k], peer).start()
    for t in range(len(p)):
        for k, peer in enumerate(peers):
            pchip = 2 * peer[0] + peer[1]
            _rcopy(out[t].at[pchip], out[t].at[pchip], send.at[3 * t + k], recv.at[3 * t + k], peer).wait()


def _rs_swap(name, halves):
    n = len(halves)

    def body(*refs):
        a, out = refs[:n], refs[n:2 * n]
        send, recv = refs[2 * n:]
        x, y, c = _me()
        sib = (x, y, 1 - c)
        for t in range(n):
            ch = a[t].shape[0] // _NCH
            for i in range(_NCH):
                rows = pl.ds(i * ch, ch)
                _rcopy(a[t].at[rows], out[t].at[rows], send.at[t], recv.at[t], sib).start()
        for t in range(n):
            _rcopy(a[t], out[t], send.at[t], recv.at[t], sib).wait()

    return pl.pallas_call(
        body, name=name, in_specs=[_ANY] * n, out_specs=[_ANY] * n,
        out_shape=[jax.ShapeDtypeStruct(s.shape, s.dtype) for s in halves],
        scratch_shapes=_dma_sems(n, n))(*halves)


def _sibling_merge(name, a):
    P_, rh, C = a.shape

    def body(a_ref, out_ref, send_sem, recv_sem, local_sem):
        x, y, c = _me()
        local = pltpu.make_async_copy(a_ref, out_ref.at[:, pl.ds(c * rh, rh)], local_sem)
        local.start()
        cp = _rcopy(a_ref, out_ref.at[:, pl.ds(c * rh, rh)], send_sem, recv_sem, (x, y, 1 - c))
        cp.start()
        cp.wait_send()
        _rcopy(a_ref, out_ref.at[:, pl.ds((1 - c) * rh, rh)], send_sem, recv_sem, (x, y, 1 - c)).wait_recv()
        local.wait()

    return pl.pallas_call(
        body, name=name, in_specs=[_ANY], out_specs=_ANY, out_shape=jax.ShapeDtypeStruct((P_, 2 * rh, C), a.dtype),
        scratch_shapes=[pltpu.SemaphoreType.DMA(()), pltpu.SemaphoreType.DMA(()), pltpu.SemaphoreType.DMA(())])(a)


def _allgather8(name, a):
    g4 = _allgather4(name + "_chips", a)
    both = _sibling_merge(name + "_cores", g4.reshape(1, 4 * a.shape[0], a.shape[1]))
    return jnp.transpose(both.reshape(2, 4, *a.shape), (1, 0, 2, 3)).reshape(8, *a.shape)


def _sum_slots(name, a, out_dtype):
    def fn(a):
        acc = a[0].astype(F32)
        for k in range(1, a.shape[0]):
            acc = acc + a[k].astype(F32)
        return acc
    return _rowwise(name, fn, [a], [], [(a.shape[2], out_dtype)])[0]


def _adamw_math(w, g, m, v):
    m = ADAM_B1 * m + (1.0 - ADAM_B1) * g
    v = ADAM_B2 * v + (1.0 - ADAM_B2) * (g * g)
    m_hat = m / (1.0 - ADAM_B1 ** ADAM_STEP)
    v_hat = v / (1.0 - ADAM_B2 ** ADAM_STEP)
    return -ADAM_LR * (m_hat / (jnp.sqrt(v_hat) + ADAM_EPS) + ADAM_WD * w), m, v


def _adamw_piece(name, w2, m2, v2, mine, theirs, row0, prev, core):
    R, C = w2.shape
    h = mine.shape[0]
    tr = _slot_tile(h, 256)
    nb = h // tr
    assert row0 % tr == 0
    first = row0 // tr

    def body(c_ref, w_ref, m_ref, v_ref, a_ref, b_ref, *rest):
        g_ref, d_ref, nm_ref, nv_ref = rest[-4:]
        g = jnp.where(pl.program_id(0) == c_ref[0], a_ref[...], b_ref[...])
        g_ref[...] = g
        d_ref[...], nm_ref[...], nv_ref[...] = _adamw_math(w_ref[...], g, m_ref[...], v_ref[...])

    full = pl.BlockSpec((tr, C), lambda s, i, c_ref: (first + s * nb + i, 0))
    mine_spec = pl.BlockSpec((tr, C), lambda s, i, c_ref: (jnp.where(s == c_ref[0], i, 0), 0))
    theirs_spec = pl.BlockSpec((tr, C), lambda s, i, c_ref: (jnp.where(s == c_ref[0], 0, i), 0))
    extra = [] if prev is None else list(prev)
    grid_spec = pltpu.PrefetchScalarGridSpec(
        num_scalar_prefetch=1, grid=(2, nb), in_specs=[full, full, full, mine_spec, theirs_spec] + [_ANY] * len(extra),
        out_specs=[full] * 4)
    return pl.pallas_call(
        body, name=name, grid_spec=grid_spec, out_shape=[jax.ShapeDtypeStruct((R, C), F32)] * 4,
        input_output_aliases={6 + k: k for k in range(len(extra))}, compiler_params=_params(2))(core, w2, m2, v2, mine, theirs, *extra)


def _adamw(name, w, g, m, v):
    shape = w.shape
    two_d = (-1, shape[-1]) if w.ndim > 1 else (1, -1)
    w2, g2, m2, v2 = [t.reshape(two_d) for t in (w, g, m, v)]
    rows = w2.shape[0]
    tr = rows
    for cand in (256, 128, 64, 32, 16, 8):
        if rows % cand == 0:
            tr = cand
            break

    c = w2.shape[1]
    outs = _rowwise(name, _adamw_math, [w2, g2, m2, v2], [], [(c, F32)] * 3, tr=tr)
    return [o.reshape(shape) for o in outs]


_WEIGHT_ORDER = ("ada_w", "ada_b", "norm_mix_g", "norm_ffn_g", "gdn_w_in", "gdn_conv_w", "gdn_a_log", "gdn_dt_bias",
                 "gdn_norm_g", "gdn_w_out", "mla_w_in", "mla_q_norm_g", "mla_kv_norm_g", "mla_w_uq", "mla_w_ukv",
                 "mla_w_out", "ffn_w_gate", "ffn_w_up", "ffn_w_down", "final_norm_g")
_BIG = (("gdn_w_in", 2), ("gdn_w_out", 1), ("mla_w_in", 1), ("mla_w_uq", 2), ("mla_w_ukv", 2), ("mla_w_out", 1),
        ("ffn_w_gate", 2), ("ffn_w_up", 2), ("ffn_w_down", 1))
_SMALL_SHARDED = (("gdn_conv_w", 1), ("mla_q_norm_g", 1), ("mla_kv_norm_g", 1))
_STORED_TRANSPOSED = ("ffn_w_gate", "ffn_w_up")


def _size(shape):
    n = 1
    for s in shape:
        n *= s
    return n


def _pack_rows_each(tensors):
    parts, offs, off = [], [], 0
    for t in tensors:
        flat = t.reshape(-1).astype(F32)
        rows = -(-flat.shape[0] // PACK_W)
        parts.append(jnp.pad(flat, (0, rows * PACK_W - flat.shape[0])).reshape(rows, PACK_W))
        offs.append(off)
        off += rows
    total = -(-off // 16) * 16
    pack = jnp.pad(parts[0], ((offs[0], total - offs[0] - parts[0].shape[0]), (0, 0)))
    for p, o in zip(parts[1:], offs[1:]):
        pack = pack + jnp.pad(p, ((o, total - o - p.shape[0]), (0, 0)))
    return pack, offs


def _unpack_rows_each(pack, shapes):
    lead = pack.shape[:-2]
    out, off = [], 0
    for shp in shapes:
        n = _size(shp)
        rows = -(-n // PACK_W)
        out.append(pack[..., off:off + rows, :].reshape(*lead, -1)[..., :n].reshape(*lead, *shp))
        off += rows
    return out


def _merge_chips(stacked, axis):
    moved = jnp.moveaxis(stacked, 0, axis)
    shp = list(moved.shape)
    return moved.reshape(shp[:axis] + [shp[axis] * shp[axis + 1]] + shp[axis + 2:])


def _my_shard(full, axis, chip):
    n = full.shape[axis] // 4
    return lax.dynamic_slice_in_dim(full, chip * n, n, axis)


def kernel(x, c, positions, ada_w, ada_b, norm_mix_g, norm_ffn_g, gdn_w_in, gdn_conv_w, gdn_a_log, gdn_dt_bias, gdn_norm_g, gdn_w_out, mla_w_in, mla_q_norm_g, mla_kv_norm_g, mla_w_uq, mla_w_ukv, mla_w_out, ffn_w_gate, ffn_w_up, ffn_w_down, final_norm_g, loss_target, m_ada_w, m_ada_b, m_norm_mix_g, m_norm_ffn_g, m_gdn_w_in, m_gdn_conv_w, m_gdn_a_log, m_gdn_dt_bias, m_gdn_norm_g, m_gdn_w_out, m_mla_w_in, m_mla_q_norm_g, m_mla_kv_norm_g, m_mla_w_uq, m_mla_w_ukv, m_mla_w_out, m_ffn_w_gate, m_ffn_w_up, m_ffn_w_down, m_final_norm_g, v_ada_w, v_ada_b, v_norm_mix_g, v_norm_ffn_g, v_gdn_w_in, v_gdn_conv_w, v_gdn_a_log, v_gdn_dt_bias, v_gdn_norm_g, v_gdn_w_out, v_mla_w_in, v_mla_q_norm_g, v_mla_kv_norm_g, v_mla_w_uq, v_mla_w_ukv, v_mla_w_out, v_ffn_w_gate, v_ffn_w_up, v_ffn_w_down, v_final_norm_g):
    w = dict(ada_w=ada_w, ada_b=ada_b, norm_mix_g=norm_mix_g, norm_ffn_g=norm_ffn_g, gdn_w_in=gdn_w_in, gdn_conv_w=gdn_conv_w,
             gdn_a_log=gdn_a_log, gdn_dt_bias=gdn_dt_bias, gdn_norm_g=gdn_norm_g, gdn_w_out=gdn_w_out, mla_w_in=mla_w_in,
             mla_q_norm_g=mla_q_norm_g, mla_kv_norm_g=mla_kv_norm_g, mla_w_uq=mla_w_uq, mla_w_ukv=mla_w_ukv,
             mla_w_out=mla_w_out, ffn_w_gate=ffn_w_gate, ffn_w_up=ffn_w_up, ffn_w_down=ffn_w_down, final_norm_g=final_norm_g)
    m = dict(ada_w=m_ada_w, ada_b=m_ada_b, norm_mix_g=m_norm_mix_g, norm_ffn_g=m_norm_ffn_g, gdn_w_in=m_gdn_w_in,
             gdn_conv_w=m_gdn_conv_w, gdn_a_log=m_gdn_a_log, gdn_dt_bias=m_gdn_dt_bias, gdn_norm_g=m_gdn_norm_g,
             gdn_w_out=m_gdn_w_out, mla_w_in=m_mla_w_in, mla_q_norm_g=m_mla_q_norm_g, mla_kv_norm_g=m_mla_kv_norm_g,
             mla_w_uq=m_mla_w_uq, mla_w_ukv=m_mla_w_ukv, mla_w_out=m_mla_w_out, ffn_w_gate=m_ffn_w_gate,
             ffn_w_up=m_ffn_w_up, ffn_w_down=m_ffn_w_down, final_norm_g=m_final_norm_g)
    v = dict(ada_w=v_ada_w, ada_b=v_ada_b, norm_mix_g=v_norm_mix_g, norm_ffn_g=v_norm_ffn_g, gdn_w_in=v_gdn_w_in,
             gdn_conv_w=v_gdn_conv_w, gdn_a_log=v_gdn_a_log, gdn_dt_bias=v_gdn_dt_bias, gdn_norm_g=v_gdn_norm_g,
             gdn_w_out=v_gdn_w_out, mla_w_in=v_mla_w_in, mla_q_norm_g=v_mla_q_norm_g, mla_kv_norm_g=v_mla_kv_norm_g,
             mla_w_uq=v_mla_w_uq, mla_w_ukv=v_mla_w_ukv, mla_w_out=v_mla_w_out, ffn_w_gate=v_ffn_w_gate,
             ffn_w_up=v_ffn_w_up, ffn_w_down=v_ffn_w_down, final_norm_g=v_final_norm_g)
    T = x.shape[1]
    ix, iy, ic = _me()
    chip = 2 * ix + iy
    seq = 2 * chip + ic
    n_dev = 8

    small_shapes = [w[n].shape for n, _ in _SMALL_SHARDED] + [c.shape]
    pack0, _ = _pack_rows_each([w[n] for n, _ in _SMALL_SHARDED] + [c])
    got0 = _unpack_rows_each(_allgather8("gather_small", pack0), small_shapes)
    small_full = {n: _merge_chips(g[0::2], ax) for (n, ax), g in zip(_SMALL_SHARDED, got0)}
    c_all = got0[-1].reshape(n_dev, D)

    big = [n for n, _ in _BIG]
    chip_arr = chip.astype(jnp.int32).reshape(1)

    def stored(n, t):
        return jnp.swapaxes(t, 1, 2) if n in _STORED_TRANSPOSED else t

    ws, ms, vs = [{n: stored(n, d[n]) for n in big} for d in (w, m, v)]
    two_d = lambda t: t.reshape(-1, t.shape[-1])

    gathered = []
    for l in range(DEPTH):
        names = _layer_weights(l)
        bufs = [_cast_into_slot(f"to_bf16_{n}{l}", two_d(ws[n]), chip_arr, j * ws[n].shape[1], ws[n].shape[1]) for n, j in names]
        filled = _gather_weights("gather_weights0", bufs) if l == 0 else _gather_weights_async(f"gather_weights{l}", l, bufs)
        gathered.append({n: b for (n, _), b in zip(names, filled)})

    def weights_of(l, h):
        return _weights_to_kernel(l, gathered[l])

    P = _small_to_kernel(norm_mix_g, norm_ffn_g, final_norm_g, small_full["gdn_conv_w"], gdn_a_log, gdn_dt_bias,
                         gdn_norm_g, small_full["mla_q_norm_g"], small_full["mla_kv_norm_g"])

    c16 = jnp.pad(c_all, ((0, 16 - n_dev), (0, 0)))
    ca = _rowwise("cond_silu", lambda t: t * _sig(t), [c16], [], [(D, BF16)])[0]
    n_ada = ada_w.shape[2]
    mods = jnp.concatenate([_mm(f"ada_fwd{l}", ca, ada_w[l], "nn") for l in range(DEPTH)], axis=0)
    mods_all = _allgather4("gather_mod", mods).reshape(4, DEPTH, 16, n_ada)
    mod_mm = jnp.transpose(lax.dynamic_index_in_dim(mods_all, seq, axis=2, keepdims=False), (1, 0, 2)).reshape(DEPTH, 4 * n_ada)
    mod = _rowwise("mod_bias", lambda a, b: a + b, [mod_mm, ada_b], [], [(4 * n_ada, F32)])[0]

    core_chip = jnp.stack([ic, chip]).astype(jnp.int32)
    pending, in_flight = {}, []

    def reduce_group(layer, part, pieces):
        pending.update({(n, layer if n.startswith("ffn_") else layer // 2): g for n, g in pieces.items()})
        if part == "ffn" and layer > 0:
            return
        keys = list(pending)
        glist = [pending.pop(k) for k in keys]
        tag = f"{layer}{part}"
        theirs = _rs_split("grads_cores_" + tag, glist)
        both = [_pair_add(f"grads_pair_{n}{l}", g, t, core_chip) for (n, l), g, t in zip(keys, glist, theirs)]
        swapped = _rs_alltoall_async("grads_chips_" + tag, DEPTH + 1 + len(in_flight), [p for p, _ in both], [o for _, o in both])
        in_flight.append((tag, keys, swapped))

    dx, dmod, gP = _local_step(x.reshape(T, D), loss_target.reshape(T, D), positions.reshape(T, 1), mod, weights_of, P, reduce_group)

    partials = [dmod, jnp.concatenate(gP["norm_mix_g"]), jnp.concatenate(gP["norm_ffn_g"]), gP["final_g"],
                jnp.stack([jnp.transpose(g) for g in gP["gdn_cw"]]), jnp.concatenate(gP["gdn_alog"])[:, :NH],
                jnp.concatenate(gP["gdn_dtb"])[:, :NH], jnp.concatenate(gP["gdn_ng"]), jnp.concatenate(gP["mla_qg"]),
                jnp.concatenate(gP["mla_kvg"]), gP["loss"][:, :1]]
    part_shapes = [p.shape for p in partials]
    ppack, _ = _pack_rows_each(partials)
    pall = _allgather8("gather_partials", ppack)
    psum = _sum_slots("sum_partials", pall, F32)
    (g_ada_b, g_norm_mix, g_norm_ffn, g_final, g_conv_full, g_alog, g_dtb, g_gdn_ng, g_qg_full, g_kvg_full,
     loss_sum) = _unpack_rows_each(psum, part_shapes)
    dmod_all = _unpack_rows_each(pall, part_shapes[:1])[0]

    grads = dict(ada_b=g_ada_b, norm_mix_g=g_norm_mix, norm_ffn_g=g_norm_ffn, final_norm_g=g_final.reshape(D),
                 gdn_conv_w=_my_shard(g_conv_full, 1, chip), gdn_a_log=g_alog, gdn_dt_bias=g_dtb, gdn_norm_g=g_gdn_ng,
                 mla_q_norm_g=_my_shard(g_qg_full, 1, chip), mla_kv_norm_g=_my_shard(g_kvg_full, 1, chip))

    ca_t = jnp.zeros((D, LANES), BF16).at[:, :16].set(jnp.transpose(ca))
    dm_mine = lax.dynamic_slice_in_dim(dmod_all, chip * n_ada, n_ada, axis=2)
    grads["ada_w"] = jnp.stack([
        _mm(f"ada_bwd{l}", ca_t, jnp.pad(dm_mine[:, l], ((0, LANES - n_dev), (0, 0))), "nn") for l in range(DEPTH)])

    delta, new_m, new_v = {}, {}, {}
    results = {}
    keys = [k for _, ks, _ in in_flight for k in ks]
    halves = [_sum_slots(f"grads_sum_{n}{l}", s, F32) for _, ks, sw in in_flight for (n, l), s in zip(ks, sw)]
    others = _rs_swap("grads_swap", halves)
    for (n, l), mine, theirs in zip(keys, halves, others):
        results[n] = _adamw_piece(f"adamw_{n}{l}", two_d(ws[n]), two_d(ms[n]), two_d(vs[n]), mine, theirs,
                                  l * ws[n].shape[1], results.get(n), core_chip[:1])
    for n in big:
        grads[n], delta[n], new_m[n], new_v[n] = [stored(n, t.reshape(ws[n].shape)) for t in results[n]]
    delta["ada_w"], new_m["ada_w"], new_v["ada_w"] = _adamw("adamw_ada_w", ada_w, grads["ada_w"], m_ada_w, v_ada_w)
    for n in [n for n in _WEIGHT_ORDER if n not in delta]:
        delta[n], new_m[n], new_v[n] = _adamw("adamw_" + n, w[n], grads[n], m[n], v[n])

    loss = loss_sum.reshape(())
    return (loss, dx.reshape(1, T, D), *[grads[n] for n in _WEIGHT_ORDER], *[delta[n] for n in _WEIGHT_ORDER],
            *[new_m[n] for n in _WEIGHT_ORDER], *[new_v[n] for n in _WEIGHT_ORDER])
```

```python
import functools

import jax
import jax.numpy as jnp
from jax import lax
from jax.experimental import pallas as pl
from jax.experimental.pallas import tpu as pltpu
from jax.experimental.pallas import tpu_sc as plsc

F32 = jnp.float32
BF16 = jnp.bfloat16
HI = lax.Precision.HIGHEST
MESH = pl.DeviceIdType.MESH

D = 1024
DEPTH = 4
N_MOD = 6
NH = 8
HD = 128
CHUNK = 64
_GDN_HB = 8
GDN_QKV = 3 * NH * HD
GDN_INK = GDN_QKV + NH * HD + 2 * HD
Q_RANK, KV_RANK, ROPE = 384, 256, 64
MLA_INK = Q_RANK + KV_RANK + HD
DFF = 2816
EPS = 1e-6
ATT_SCALE = (HD + ROPE) ** -0.5
ROPE_THETA = 10000.0
LANES = 128
PACK_W = 1024

ADAM_LR, ADAM_B1, ADAM_B2, ADAM_EPS, ADAM_WD, ADAM_STEP = 0.001, 0.9, 0.999, 1e-08, 0.01, 10


H3 = "bf16x3"
B1 = "bf16"
HS = H3
HF = B1


def _dot(a, b, mode="nn", prec=None):
    dn = {"nn": (((1,), (0,)), ((), ())), "nt": (((1,), (1,)), ((), ())), "tn": (((0,), (0,)), ((), ()))}[mode]
    if prec == B1:
        return _dot(a.astype(BF16), b.astype(BF16), mode)
    if prec == H3:
        ah, bh = a.astype(BF16), b.astype(BF16)
        al, bl = (a - ah.astype(F32)).astype(BF16), (b - bh.astype(F32)).astype(BF16)
        return _dot(ah, bh, mode) + (_dot(ah, bl, mode) + _dot(al, bh, mode))
    return lax.dot_general(a, b, dn, precision=prec, preferred_element_type=F32)


def _sig(x):
    return 1.0 / (1.0 + jnp.exp(-x))


def _pick(n, cap):
    if n <= cap:
        return n
    best = None
    for d in range(LANES, cap + 1, LANES):
        if n % d == 0:
            best = d
    assert best is not None, (n, cap)
    return best


def _params(n_grid):
    return pltpu.CompilerParams(dimension_semantics=("arbitrary",) * n_grid, vmem_limit_bytes=56 * 1024 * 1024)


def _rowwise(name, fn, rows, consts, outs, sums=(), tr=256):
    first = rows[0][0] if isinstance(rows[0], tuple) else rows[0]
    T = first.shape[-2]
    tr = _slot_tile(T, tr)
    nr, nc, no, ns = len(rows), len(consts), len(outs), len(sums)

    windows = [c[1:] if isinstance(c, tuple) else None for c in consts]
    consts = [c[0] if isinstance(c, tuple) else c for c in consts]

    def body(*refs):
        vals = [r[...] for r in refs[:nr]]
        for r, win in zip(refs[nr:nr + nc], windows):
            vals.append(r[...] if win is None else r[win[0]:win[0] + 1, win[1] * win[2]:(win[1] + 1) * win[2]])
        res = fn(*vals)
        if not isinstance(res, (tuple, list)):
            res = (res,)
        o_refs = refs[nr + nc:nr + nc + no]
        s_refs = refs[nr + nc + no:]
        for r, val in zip(o_refs, res[:no]):
            r[...] = val.astype(r.dtype)
        if ns:
            @pl.when(pl.program_id(0) == 0)
            def _():
                for r in s_refs:
                    r[...] = jnp.zeros_like(r)
            for r, val in zip(s_refs, res[no:]):
                r[...] += val

    in_specs, args = [], []
    for a in rows:
        if isinstance(a, tuple):
            arr, width, cb = a
            in_specs.append(pl.BlockSpec((tr, width), lambda i, cb=cb: (i, cb)))
            args.append(arr)
        elif a.ndim == 3:
            in_specs.append(pl.BlockSpec((a.shape[0], tr, a.shape[2]), lambda i: (0, i, 0)))
            args.append(a)
        else:
            in_specs.append(pl.BlockSpec((tr, a.shape[1]), lambda i: (i, 0)))
            args.append(a)
    for a in consts:
        in_specs.append(pl.BlockSpec(a.shape, lambda i, nd=a.ndim: (0,) * nd))
        args.append(a)
    out_specs = [pl.BlockSpec((tr, w), lambda i: (i, 0)) for w, _ in outs]
    out_specs += [pl.BlockSpec((1, w), lambda i: (0, 0)) for w in sums]
    out_shape = [jax.ShapeDtypeStruct((T, w), dt) for w, dt in outs]
    out_shape += [jax.ShapeDtypeStruct((1, w), F32) for w in sums]
    res = pl.pallas_call(body, name=name, grid=(T // tr,), in_specs=in_specs, out_specs=out_specs,
                         out_shape=out_shape, compiler_params=_params(1))(*args)
    return res


def _mm(name, a, b, mode, out_dtype=F32, tm=512, tn=1024):
    if mode == "tn":
        K, M = a.shape
    else:
        M, K = a.shape
    N = b.shape[0] if mode == "nt" else b.shape[1]
    tm, tn = _pick(M, tm), _pick(N, tn)

    def body(a_ref, b_ref, o_ref):
        o_ref[...] = _dot(a_ref[...].astype(BF16), b_ref[...].astype(BF16), mode).astype(o_ref.dtype)

    a_spec = pl.BlockSpec((K, tm), lambda i, j: (0, i)) if mode == "tn" else pl.BlockSpec((tm, K), lambda i, j: (i, 0))
    b_spec = pl.BlockSpec((tn, K), lambda i, j: (j, 0)) if mode == "nt" else pl.BlockSpec((K, tn), lambda i, j: (0, j))
    return pl.pallas_call(body, name=name, grid=(M // tm, N // tn), in_specs=[a_spec, b_spec],
                          out_specs=pl.BlockSpec((tm, tn), lambda i, j: (i, j)),
                          out_shape=jax.ShapeDtypeStruct((M, N), out_dtype), compiler_params=_params(2))(a, b)


def _rms(x, eps=EPS):
    return lax.rsqrt(jnp.mean(x * x, axis=-1, keepdims=True) + eps)


def _norm_mod_fwd(name, x, g, scale, shift):
    def fn(x, g, scale, shift):
        return x * _rms(x) * g * (1.0 + scale) + shift
    return _rowwise(name, fn, [x], [g, scale, shift], [(D, BF16)])[0]


def _norm_mod_bwd(name, dh, x, dx_res, g, scale):
    def fn(dh, x, dx_res, g, scale):
        r = _rms(x)
        xh = x * r
        dxh = dh * (g * (1.0 + scale))
        dx = r * (dxh - xh * jnp.mean(dxh * xh, axis=-1, keepdims=True))
        dhx = dh * xh
        return (dx_res + dx, jnp.sum(dh, axis=0, keepdims=True), jnp.sum(dhx * g, axis=0, keepdims=True),
                jnp.sum(dhx * (1.0 + scale), axis=0, keepdims=True))
    return _rowwise(name, fn, [dh, x, dx_res], [g, scale], [(D, F32)], sums=[D, D, D])


def _residual_fwd(name, x, y, gate):
    def fn(x, y, gate):
        return x + gate * y
    return _rowwise(name, fn, [x, y], [gate], [(D, F32)])[0]


def _residual_norm_fwd(name, x, y, gate, g, scale, shift):
    def fn(x, y, gate, g, scale, shift):
        x = x + gate * y
        return x, x * _rms(x) * g * (1.0 + scale) + shift
    return _rowwise(name, fn, [x, y], [gate, g, scale, shift], [(D, F32), (D, BF16)])


def _norm_residual_bwd(name, dh, x, dx_res, g, scale, y, gate):
    def fn(dh, x, dx_res, y, g, scale, gate):
        r = _rms(x)
        xh = x * r
        dxh = dh * (g * (1.0 + scale))
        dx = dx_res + r * (dxh - xh * jnp.mean(dxh * xh, axis=-1, keepdims=True))
        dhx = dh * xh
        return (dx, dx * gate, jnp.sum(dh, axis=0, keepdims=True), jnp.sum(dhx * g, axis=0, keepdims=True),
                jnp.sum(dhx * (1.0 + scale), axis=0, keepdims=True), jnp.sum(dx * y, axis=0, keepdims=True))
    return _rowwise(name, fn, [dh, x, dx_res, y], [g, scale, gate], [(D, F32), (D, BF16)], sums=[D, D, D, D])


def _residual_bwd(name, dx, y, gate):
    def fn(dx, y, gate):
        return dx * gate, jnp.sum(dx * y, axis=0, keepdims=True)
    return _rowwise(name, fn, [dx, y], [gate], [(D, BF16)], sums=[D])


def _loss_head(x, target, g):
    def fn(x, t, g):
        r = _rms(x)
        xh = x * r
        err = xh * g - t
        loss = 0.5 * jnp.sum(jnp.mean(err * err, axis=-1, keepdims=True), axis=0, keepdims=True)
        dy = err * (1.0 / D)
        dxh = dy * g
        dx = r * (dxh - xh * jnp.mean(dxh * xh, axis=-1, keepdims=True))
        return dx, jnp.broadcast_to(loss, (1, LANES)), jnp.sum(dy * xh, axis=0, keepdims=True)
    return _rowwise("loss_head", fn, [x, target], [g], [(D, F32)], sums=[LANES, D])


def _ffn_up(name, h, wg, wu, layer, tm=1024):
    T, n = h.shape[0], wg.shape[1]
    tm = min(tm, T)

    def body(h_ref, wg_ref, wu_ref, a_ref, b_ref, s_ref):
        h = h_ref[...]
        a = _dot(h, wg_ref[0], "nt")
        b = _dot(h, wu_ref[0], "nt")
        a_ref[0] = a.astype(a_ref.dtype)
        b_ref[0] = b.astype(b_ref.dtype)
        s_ref[0] = (a * _sig(a) * b).astype(s_ref.dtype)

    wspec = pl.BlockSpec((1, n, D), lambda ch, i: (ch, layer, 0))
    ospec = pl.BlockSpec((1, tm, n), lambda ch, i: (ch, i, 0))
    return pl.pallas_call(
        body, name=name, grid=(4, T // tm), in_specs=[pl.BlockSpec((tm, D), lambda ch, i: (i, 0)), wspec, wspec],
        out_specs=[ospec, ospec, ospec],
        out_shape=[jax.ShapeDtypeStruct((4, T, n), BF16)] * 3, compiler_params=_params(2))(h, wg, wu)


def _ffn_down(name, s, wd, layer, tm=1024):
    _, T, n = s.shape
    tm = min(tm, T)

    def body(s_ref, w_ref, y_ref):
        @pl.when(pl.program_id(1) == 0)
        def _():
            y_ref[...] = jnp.zeros_like(y_ref)
        y_ref[...] += _dot(s_ref[0], w_ref[0], "nn")

    return pl.pallas_call(
        body, name=name, grid=(T // tm, 4),
        in_specs=[pl.BlockSpec((1, tm, n), lambda i, ch: (ch, i, 0)), pl.BlockSpec((1, n, D), lambda i, ch: (ch, layer, 0))],
        out_specs=pl.BlockSpec((tm, D), lambda i, ch: (i, 0)), out_shape=jax.ShapeDtypeStruct((T, D), F32),
        compiler_params=_params(2))(s, wd)


def _ffn_down_bwd(name, dy, wd, a, b, layer, tm=1024):
    _, T, n = a.shape
    tm = min(tm, T)

    def body(dy_ref, w_ref, a_ref, b_ref, da_ref, db_ref):
        ds = _dot(dy_ref[...], w_ref[0], "nt")
        a, b = a_ref[0].astype(F32), b_ref[0].astype(F32)
        sg = _sig(a)
        da_ref[0] = (ds * b * (sg * (1.0 + a * (1.0 - sg)))).astype(da_ref.dtype)
        db_ref[0] = (ds * (a * sg)).astype(db_ref.dtype)

    bspec = pl.BlockSpec((1, tm, n), lambda ch, i: (ch, i, 0))
    return pl.pallas_call(
        body, name=name, grid=(4, T // tm),
        in_specs=[pl.BlockSpec((tm, D), lambda ch, i: (i, 0)), pl.BlockSpec((1, n, D), lambda ch, i: (ch, layer, 0)), bspec, bspec],
        out_specs=[bspec, bspec], out_shape=[jax.ShapeDtypeStruct((4, T, n), BF16)] * 2,
        compiler_params=_params(2))(dy, wd, a, b)


def _ffn_down_dw(name, s, dy):
    _, T, n = s.shape

    def body(s_ref, dy_ref, o_ref):
        o_ref[0] = _dot(s_ref[0], dy_ref[...], "tn").astype(o_ref.dtype)

    return pl.pallas_call(
        body, name=name, grid=(4,),
        in_specs=[pl.BlockSpec((1, T, n), lambda ch: (ch, 0, 0)), pl.BlockSpec((T, D), lambda ch: (0, 0))],
        out_specs=pl.BlockSpec((1, n, D), lambda ch: (ch, 0, 0)), out_shape=jax.ShapeDtypeStruct((4, n, D), BF16),
        compiler_params=_params(1))(s, dy)


def _ffn_up_dw(name, h, da, db, tm=512):
    _, T, n = da.shape

    def body(h_ref, da_ref, db_ref, dg_ref, du_ref):
        h = h_ref[...]
        dg_ref[0] = _dot(da_ref[0], h, "tn").astype(dg_ref.dtype)
        du_ref[0] = _dot(db_ref[0], h, "tn").astype(du_ref.dtype)

    dspec = pl.BlockSpec((1, T, n), lambda ch, j: (ch, 0, 0))
    ospec = pl.BlockSpec((1, n, tm), lambda ch, j: (ch, 0, j))
    return pl.pallas_call(
        body, name=name, grid=(4, D // tm), in_specs=[pl.BlockSpec((T, tm), lambda ch, j: (0, j)), dspec, dspec],
        out_specs=[ospec, ospec], out_shape=[jax.ShapeDtypeStruct((4, n, D), BF16)] * 2,
        compiler_params=_params(2))(h, da, db)


def _ffn_up_dx(name, da, db, wg, wu, layer, tm=1024):
    _, T, n = da.shape
    tm = min(tm, T)

    def body(da_ref, db_ref, wg_ref, wu_ref, o_ref):
        @pl.when(pl.program_id(1) == 0)
        def _():
            o_ref[...] = jnp.zeros_like(o_ref)
        o_ref[...] += _dot(da_ref[0], wg_ref[0], "nn") + _dot(db_ref[0], wu_ref[0], "nn")

    dspec = pl.BlockSpec((1, tm, n), lambda i, ch: (ch, i, 0))
    wspec = pl.BlockSpec((1, n, D), lambda i, ch: (ch, layer, 0))
    return pl.pallas_call(
        body, name=name, grid=(T // tm, 4), in_specs=[dspec, dspec, wspec, wspec],
        out_specs=pl.BlockSpec((tm, D), lambda i, ch: (i, 0)), out_shape=jax.ShapeDtypeStruct((T, D), F32),
        compiler_params=_params(2))(da, db, wg, wu)


def _shift_down(x, k):
    if k == 0:
        return x
    rows = lax.broadcasted_iota(jnp.int32, x.shape, 0)
    return jnp.where(rows >= k, pltpu.roll(x, k, 0), 0.0)


def _shift_up(x, k):
    if k == 0:
        return x
    T = x.shape[0]
    rows = lax.broadcasted_iota(jnp.int32, x.shape, 0)
    return jnp.where(rows < T - k, pltpu.roll(x, T - k, 0), 0.0)


def _conv_silu(x, w):
    c = w[0:1, :] * _shift_down(x, 3) + w[1:2, :] * _shift_down(x, 2) + w[2:3, :] * _shift_down(x, 1) + w[3:4, :] * x
    sg = _sig(c)
    return c, sg, c * sg


def _gdn_conv_fwd(name, proj, cw):
    T = proj.shape[0]

    def body(x_ref, w_ref, o_ref):
        j = pl.program_id(0)
        _, _, y = _conv_silu(x_ref[...], w_ref[...])
        r = lax.rsqrt(jnp.sum(y * y, axis=1, keepdims=True) + EPS)
        mult = jnp.where(j < NH, HD ** -0.5, 1.0)
        o_ref[...] = jnp.where(j < 2 * NH, y * (r * mult), y)

    return pl.pallas_call(body, name=name, grid=(3 * NH,),
                          in_specs=[pl.BlockSpec((T, HD), lambda j: (0, j)), pl.BlockSpec((4, HD), lambda j: (0, j))],
                          out_specs=pl.BlockSpec((T, HD), lambda j: (0, j)),
                          out_shape=jax.ShapeDtypeStruct((T, GDN_QKV), F32), compiler_params=_params(1))(proj, cw)


def _gdn_conv_bwd(name, proj, cw, dz):
    T = proj.shape[0]

    def body(x_ref, w_ref, dz_ref, dx_ref, dw_ref):
        j = pl.program_id(0)
        x, w, dz = x_ref[...], w_ref[...], dz_ref[...]
        c, sg, y = _conv_silu(x, w)
        r = lax.rsqrt(jnp.sum(y * y, axis=1, keepdims=True) + EPS)
        mult = jnp.where(j < NH, HD ** -0.5, 1.0)
        dyn = mult * (r * dz - (r * r * r) * y * jnp.sum(dz * y, axis=1, keepdims=True))
        dy = jnp.where(j < 2 * NH, dyn, dz)
        dc = dy * (sg * (1.0 + c * (1.0 - sg)))
        dx = w[0:1, :] * _shift_up(dc, 3) + w[1:2, :] * _shift_up(dc, 2) + w[2:3, :] * _shift_up(dc, 1) + w[3:4, :] * dc
        dx_ref[...] = dx.astype(dx_ref.dtype)
        for k in range(4):
            dw_ref[pl.ds(k, 1), :] = jnp.sum(dc * _shift_down(x, 3 - k), axis=0, keepdims=True)

    return pl.pallas_call(body, name=name, grid=(3 * NH,),
                          in_specs=[pl.BlockSpec((T, HD), lambda j: (0, j)), pl.BlockSpec((4, HD), lambda j: (0, j)),
                                    pl.BlockSpec((T, HD), lambda j: (0, j))],
                          out_specs=[pl.BlockSpec((T, HD), lambda j: (0, j)), pl.BlockSpec((4, HD), lambda j: (0, j))],
                          out_shape=[jax.ShapeDtypeStruct((T, GDN_QKV), BF16), jax.ShapeDtypeStruct((4, GDN_QKV), F32)],
                          compiler_params=_params(1))(proj, cw, dz)


def _softplus(z):
    return jnp.maximum(z, 0.0) + jnp.log(1.0 + jnp.exp(-jnp.abs(z)))


_AB_CB = GDN_INK // (2 * HD) - 1


def _gdn_gates_fwd(name, proj, alog, dtb):
    def fn(ab, alog, dtb):
        a, b = ab[:, :HD], ab[:, HD:]
        return -jnp.exp(alog) * _softplus(a + dtb), _sig(b)
    return _rowwise(name, fn, [(proj, 2 * HD, _AB_CB)], [alog, dtb], [(HD, F32), (HD, F32)])


def _gdn_gates_bwd(name, proj, dg_h, db_h, alog, dtb):
    def fn(ab, dg_h, db_h, alog, dtb):
        lane = lax.broadcasted_iota(jnp.int32, (1, HD), 1)
        dg = jnp.zeros(dg_h.shape[1:], F32)
        dbeta = jnp.zeros(dg_h.shape[1:], F32)
        for h in range(NH):
            oh = (lane == h).astype(F32)
            dg = dg + dg_h[h] * oh
            dbeta = dbeta + db_h[h] * oh
        a, b = ab[:, :HD], ab[:, HD:]
        z = a + dtb
        ea = jnp.exp(alog)
        beta = _sig(b)
        da = dg * (-ea) * _sig(z)
        db = dbeta * beta * (1.0 - beta)
        return (jnp.concatenate([da, db], axis=1), jnp.sum(dg * (-ea * _softplus(z)), axis=0, keepdims=True),
                jnp.sum(da, axis=0, keepdims=True))
    return _rowwise(name, fn, [(proj, 2 * HD, _AB_CB), dg_h, db_h], [alog, dtb], [(2 * HD, BF16)], sums=[HD, HD])


def _interleave(gens):
    gens = list(gens)
    results = [None] * len(gens)
    active = list(range(len(gens)))
    while active:
        for i in list(active):
            try:
                next(gens[i])
            except StopIteration as stop:
                results[i] = stop.value
                active.remove(i)
    return results


def _chunk_common(q, k, v, gblk, bblk, h, prec):
    C = CHUNK
    lane = lax.broadcasted_iota(jnp.int32, (1, HD), 1)
    oh = (lane == h).astype(F32)
    g_col = jnp.sum(gblk * oh, axis=1, keepdims=True)
    beta = jnp.sum(bblk * oh, axis=1, keepdims=True)
    ri = lax.broadcasted_iota(jnp.int32, (C, C), 0)
    ci = lax.broadcasted_iota(jnp.int32, (C, C), 1)
    incl = ri >= ci
    strict = ri > ci
    eye = (ri == ci).astype(F32)
    gcb = _dot(incl.astype(F32), jnp.broadcast_to(g_col, (C, HD)), "nn", HI)
    yield
    gc = gcb[:, :C]
    gc_row = _dot(jnp.ones((C, C), F32), eye * gc, "nn", HI)
    yield
    decay = jnp.where(incl, jnp.exp(jnp.where(incl, gc - gc_row, 0.0)), 0.0)
    rows = lax.broadcasted_iota(jnp.int32, (C, HD), 0)
    gclb = jnp.sum(jnp.where(rows == C - 1, gcb, 0.0), axis=0, keepdims=True)
    eg = jnp.exp(gcb)
    egl = jnp.exp(gclb - gcb)
    gl = jnp.exp(gclb)
    kb = k * beta
    m1 = _dot(kb, k, "nt", prec)
    qk = _dot(q, k, "nt", prec)
    yield
    L = jnp.where(strict, m1 * decay, 0.0)
    nl = -L
    tinv = eye + nl
    p = nl
    for _ in range(5):
        p = _dot(p, p, "nn", H3)
        yield
        tinv = tinv + _dot(tinv, p, "nn", H3)
    vb = v * beta
    kbg = kb * eg
    yield
    u = _dot(tinv, vb, "nn", prec)
    w = _dot(tinv, kbg, "nn", prec)
    yield
    attn = jnp.where(incl, qk * decay, 0.0)
    return dict(beta=beta, incl=incl, strict=strict, decay=decay, eg=eg, egl=egl, gl=gl, kb=kb, m1=m1, tinv=tinv,
                kbg=kbg, u=u, w=w, qk=qk, attn=attn, q_dec=q * eg, k_dec=k * egl, rows=rows, oh=oh)


def _gdn_chunk_fwd(name, qkv, g, beta):
    T = qkv.shape[0]
    N = T // CHUNK

    hb = _GDN_HB
    w = hb * HD

    def body(q_ref, k_ref, v_ref, g_ref, b_ref, o_ref, st_ref, S):
        hg, n = pl.program_id(0), pl.program_id(1)

        @pl.when(n == 0)
        def _():
            S[...] = jnp.zeros_like(S)

        gblk, bblk = g_ref[...], b_ref[...]

        def one_head(i, q, k, v, s):
            c = yield from _chunk_common(q, k, v, gblk, bblk, hg * hb + i, HF)
            v_new = c["u"] - _dot(c["w"], s, "nn", HF)
            qs = _dot(c["q_dec"], s, "nn", HF)
            yield
            o = qs + _dot(c["attn"], v_new, "nn", HF)
            return o, s * c["gl"] + _dot(c["k_dec"], v_new, "tn", HF)

        sls = [slice(i * HD, (i + 1) * HD) for i in range(hb)]
        states = [S[i] for i in range(hb)]
        res = _interleave(one_head(i, q_ref[:, sls[i]], k_ref[:, sls[i]], v_ref[:, sls[i]], states[i]) for i in range(hb))
        for i, (o, s_new) in enumerate(res):
            st_ref[i, 0] = states[i]
            o_ref[:, sls[i]] = o
            S[i] = s_new

    blk = lambda off: pl.BlockSpec((CHUNK, w), lambda h, n, off=off: (n, off + h))
    gspec = pl.BlockSpec((CHUNK, HD), lambda h, n: (n, 0))
    return pl.pallas_call(
        body, name=name, grid=(NH // hb, N), in_specs=[blk(0), blk(NH // hb), blk(2 * NH // hb), gspec, gspec],
        out_specs=[pl.BlockSpec((CHUNK, w), lambda h, n: (n, h)), pl.BlockSpec((hb, 1, HD, HD), lambda h, n: (h, n, 0, 0))],
        out_shape=[jax.ShapeDtypeStruct((T, NH * HD), F32), jax.ShapeDtypeStruct((NH, N, HD, HD), F32)],
        scratch_shapes=[pltpu.VMEM((hb, HD, HD), F32)], compiler_params=_params(2))(qkv, qkv, qkv, g, beta)


def _gdn_chunk_bwd(name, qkv, g, beta, states, do):
    T = qkv.shape[0]
    N = T // CHUNK
    C = CHUNK

    hb = _GDN_HB
    w = hb * HD
    assert hb == NH

    def body(q_ref, k_ref, v_ref, g_ref, b_ref, st_ref, do_ref, dqkv_ref, dg_ref, db_ref, dS):
        hg, n = pl.program_id(0), pl.program_id(1)

        @pl.when(n == 0)
        def _():
            dS[...] = jnp.zeros_like(dS)

        gblk, bblk = g_ref[...], b_ref[...]
        sls = [slice(i * HD, (i + 1) * HD) for i in range(hb)]
        res = _interleave(one_head(hg * hb + i, gblk, bblk, q_ref[:, sls[i]], k_ref[:, sls[i]], v_ref[:, sls[i]],
                                   st_ref[i, 0], do_ref[:, sls[i]], dS[i]) for i in range(hb))
        for i, (dq, dk, dv, dg, db, ds_new) in enumerate(res):
            dqkv_ref[:, sls[i]] = dq
            dqkv_ref[:, slice(w + i * HD, w + (i + 1) * HD)] = dk
            dqkv_ref[:, slice(2 * w + i * HD, 2 * w + (i + 1) * HD)] = dv
            dg_ref[i] = dg
            db_ref[i] = db
            dS[i] = ds_new

    def one_head(h, gblk, bblk, q, k, v, s, do, ds):
        c = yield from _chunk_common(q, k, v, gblk, bblk, h, HF)
        eg, egl, gl, beta, decay, tinv = c["eg"], c["egl"], c["gl"], c["beta"], c["decay"], c["tinv"]
        v_new = c["u"] - _dot(c["w"], s, "nn", HF)
        dq_dec = _dot(do, s, "nt", HF)
        yield
        dv_new = _dot(c["attn"], do, "tn", HF) + _dot(c["k_dec"], ds, "nn", HF)
        dk_dec = _dot(v_new, ds, "nt", HF)
        dgl = jnp.sum(jnp.sum(s * ds, axis=1, keepdims=True), axis=0, keepdims=True)
        yield
        ds_new = ds * gl + _dot(c["q_dec"], do, "tn", HF) - _dot(c["w"], dv_new, "tn", HF)
        dattn = jnp.where(c["incl"], _dot(do, v_new, "nt", HF), 0.0)
        dw = -_dot(dv_new, s, "nt", HF)
        yield
        dvb = _dot(tinv, dv_new, "tn", HS)
        dkbg = _dot(tinv, dw, "tn", HS)
        yield
        dA = -(_dot(dvb, c["u"], "nt", HS) + _dot(dkbg, c["w"], "nt", HS))
        yield
        dL = jnp.where(c["strict"], dA, 0.0)
        dm1 = dL * decay
        dqk = dattn * decay
        xdec = (dL * c["m1"] + dattn * c["qk"]) * decay
        dkb = _dot(dm1, k, "nn", HS) + dkbg * eg
        dk = _dot(dm1, c["kb"], "tn", HF) + _dot(dqk, q, "tn", HF) + dk_dec * egl + dkb * beta
        dq = _dot(dqk, k, "nn", HF) + dq_dec * eg
        yield
        dkd_kd = jnp.sum(dk_dec * c["k_dec"], axis=1, keepdims=True)
        dgc = (jnp.sum(xdec, axis=1, keepdims=True) - _dot(xdec, jnp.ones((C, HD), F32), "tn", HS)
               + jnp.sum(dq_dec * c["q_dec"], axis=1, keepdims=True) - dkd_kd
               + jnp.sum(dkbg * c["kbg"], axis=1, keepdims=True))
        dgcl = jnp.sum(dkd_kd, axis=0, keepdims=True) + dgl * gl
        dgc = dgc + jnp.where(c["rows"] == C - 1, dgcl, 0.0)
        ri = lax.broadcasted_iota(jnp.int32, (C, C), 0)
        ci = lax.broadcasted_iota(jnp.int32, (C, C), 1)
        dg = _dot((ci >= ri).astype(F32), dgc, "nn", HI)
        db = jnp.broadcast_to(jnp.sum(dkb * k, axis=1, keepdims=True) + jnp.sum(dvb * v, axis=1, keepdims=True), (C, HD))
        return dq, dk, dvb * beta, dg, db, ds_new

    blk = lambda off: pl.BlockSpec((C, w), lambda h, n, off=off: (N - 1 - n, off + h))
    gspec = pl.BlockSpec((C, HD), lambda h, n: (N - 1 - n, 0))
    ospec = pl.BlockSpec((C, w), lambda h, n: (N - 1 - n, h))
    hspec = pl.BlockSpec((hb, C, HD), lambda h, n: (h, N - 1 - n, 0))
    return pl.pallas_call(
        body, name=name, grid=(NH // hb, N),
        in_specs=[blk(0), blk(NH // hb), blk(2 * NH // hb), gspec, gspec,
                  pl.BlockSpec((hb, 1, HD, HD), lambda h, n: (h, N - 1 - n, 0, 0)), ospec],
        out_specs=[pl.BlockSpec((C, 3 * w), lambda h, n: (N - 1 - n, 0)), hspec, hspec],
        out_shape=[jax.ShapeDtypeStruct((T, 3 * NH * HD), F32)] + [jax.ShapeDtypeStruct((NH, T, HD), F32)] * 2,
        scratch_shapes=[pltpu.VMEM((hb, HD, HD), F32)], compiler_params=_params(2))(qkv, qkv, qkv, g, beta, states, do)


_GATE_CB = GDN_QKV // (NH * HD)


def _gdn_gated_norm_fwd(name, o, proj, ng):
    def fn(o, gate, ng):
        outs = []
        for h in range(NH):
            sl = slice(h * HD, (h + 1) * HD)
            oh, gh = o[:, sl], gate[:, sl]
            outs.append(oh * _rms(oh) * ng * (gh * _sig(gh)))
        return jnp.concatenate(outs, axis=1)
    return _rowwise(name, fn, [o, (proj, NH * HD, _GATE_CB)], [ng], [(NH * HD, BF16)])[0]


def _gdn_gated_norm_bwd(name, don, o, proj, ng):
    def fn(don, o, gate, ng):
        dos, dgs = [], []
        dng = jnp.zeros((1, HD), F32)
        for h in range(NH):
            sl = slice(h * HD, (h + 1) * HD)
            oh, gh, dh = o[:, sl], gate[:, sl], don[:, sl]
            r = _rms(oh)
            xh = oh * r
            sg = _sig(gh)
            dn = dh * (gh * sg)
            dgs.append(dh * (xh * ng) * (sg * (1.0 + gh * (1.0 - sg))))
            dng = dng + jnp.sum(dn * xh, axis=0, keepdims=True)
            dxh = dn * ng
            dos.append(r * (dxh - xh * jnp.mean(dxh * xh, axis=-1, keepdims=True)))
        return jnp.concatenate(dos, axis=1), jnp.concatenate(dgs, axis=1), dng
    return _rowwise(name, fn, [don, o, (proj, NH * HD, _GATE_CB)], [ng], [(NH * HD, F32), (NH * HD, BF16)], sums=[HD])


def _rot(x):
    lane = lax.broadcasted_iota(jnp.int32, x.shape, 1)
    return jnp.where(lane < ROPE // 2, -pltpu.roll(x, HD - ROPE // 2, 1), pltpu.roll(x, ROPE // 2, 1))


def _rot_t(x):
    lane = lax.broadcasted_iota(jnp.int32, x.shape, 1)
    return jnp.where(lane < ROPE // 2, pltpu.roll(x, HD - ROPE // 2, 1), -pltpu.roll(x, ROPE // 2, 1))


def _rope_tables(pos_col):
    lane = jnp.arange(HD)
    inv_freq = ROPE_THETA ** (-(2.0 * (lane % (ROPE // 2)).astype(F32)) / ROPE)
    inv_freq = jnp.where(lane < ROPE, inv_freq, 0.0).astype(F32)[None, :]
    valid = (lane < ROPE).astype(F32)[None, :]

    def fn(pos, inv_freq, valid):
        ang = pos.astype(F32) * inv_freq
        return jnp.cos(ang) * valid, jnp.sin(ang) * valid
    return _rowwise("rope_tables", fn, [pos_col], [inv_freq, valid], [(HD, F32), (HD, F32)])


def _mla_pre_fwd(name, proj, cos, sin, qg, kvg):
    def fn(p, cos, sin, qg, kvg):
        cq, ckv, kr = p[:, :Q_RANK], p[:, Q_RANK:Q_RANK + KV_RANK], p[:, Q_RANK + KV_RANK:]
        return cq * _rms(cq) * qg, ckv * _rms(ckv) * kvg, kr * cos + _rot(kr) * sin
    return _rowwise(name, fn, [proj, cos, sin], [qg, kvg], [(Q_RANK, BF16), (KV_RANK, BF16), (HD, BF16)])


def _rms_bwd(dy, x, g):
    r = _rms(x)
    xh = x * r
    dxh = dy * g
    return r * (dxh - xh * jnp.mean(dxh * xh, axis=-1, keepdims=True)), jnp.sum(dy * xh, axis=0, keepdims=True)


def _mla_pre_bwd(name, proj, dcqn, dckvn, dkr, cos, sin, qg, kvg):
    def fn(p, dcqn, dckvn, dkr, cos, sin, qg, kvg):
        cq, ckv = p[:, :Q_RANK], p[:, Q_RANK:Q_RANK + KV_RANK]
        dcq, dqg = _rms_bwd(dcqn, cq, qg)
        dckv, dkvg = _rms_bwd(dckvn, ckv, kvg)
        dkr_pre = dkr * cos + _rot_t(dkr * sin)
        return jnp.concatenate([dcq, dckv, dkr_pre], axis=1), dqg, dkvg
    return _rowwise(name, fn, [proj, dcqn, dckvn, dkr, cos, sin], [qg, kvg], [(MLA_INK, BF16)], sums=[Q_RANK, KV_RANK])


def _mla_q_fwd(name, q, cos, sin):
    def fn(qn, qr, cos, sin):
        outs = []
        for h in range(NH):
            x = qr[:, h * HD:(h + 1) * HD]
            outs.append(x * cos + _rot(x) * sin)
        return qn, jnp.concatenate(outs, axis=1)
    return _rowwise(name, fn, [(q, NH * HD, 0), (q, NH * HD, 1), cos, sin], [], [(NH * HD, BF16), (NH * HD, BF16)])


def _mla_q_bwd(name, dqn, dqr, cos, sin):
    def fn(dqn, dqr, cos, sin):
        outs = [dqn]
        for h in range(NH):
            z = dqr[:, h * HD:(h + 1) * HD]
            outs.append(z * cos + _rot_t(z * sin))
        return jnp.concatenate(outs, axis=1)
    return _rowwise(name, fn, [dqn, dqr, cos, sin], [], [(2 * NH * HD, BF16)])[0]


def _att_parts(q2, kn_ref, kr_ref, blk, tq):
    spans = ([pl.ds(0, blk * tq)] if blk else []) + [pl.ds(blk * tq, tq)]
    k2s = [jnp.concatenate([kn_ref[keys, :], kr_ref[keys, :]], axis=1) for keys in spans]
    ss = [_dot(q2, k2, "nt") * ATT_SCALE for k2 in k2s]
    row = lax.broadcasted_iota(jnp.int32, ss[-1].shape, 0)
    col = lax.broadcasted_iota(jnp.int32, ss[-1].shape, 1)
    ss[-1] = jnp.where(col <= row, ss[-1], -1e30)
    m = jnp.max(ss[-1], axis=1, keepdims=True)
    for s in ss[:-1]:
        m = jnp.maximum(m, jnp.max(s, axis=1, keepdims=True))
    es = [jnp.exp(s - m) for s in ss]
    l = jnp.sum(es[-1], axis=1, keepdims=True)
    for e in es[:-1]:
        l = l + jnp.sum(e, axis=1, keepdims=True)
    return list(zip(spans, k2s, es)), 1.0 / l


def _mla_attn_fwd(name, qn, qr, kv, kr, tq=256):
    T = qn.shape[0]
    tq = min(tq, T)

    def body(qn_ref, qr_ref, kn_ref, v_ref, kr_ref, o_ref):
        i = pl.program_id(1)
        for blk in range(T // tq):
            @pl.when(i == blk)
            def _(blk=blk):
                q2 = jnp.concatenate([qn_ref[...], qr_ref[...]], axis=1)
                parts, inv_l = _att_parts(q2, kn_ref, kr_ref, blk, tq)
                acc = None
                for keys, _, e in parts:
                    pv = _dot(e.astype(BF16), v_ref[keys, :], "nn")
                    acc = pv if acc is None else acc + pv
                o_ref[...] = (acc * inv_l).astype(o_ref.dtype)

    qspec = pl.BlockSpec((tq, HD), lambda h, i: (i, h))
    return pl.pallas_call(
        body, name=name, grid=(NH, T // tq),
        in_specs=[qspec, qspec, pl.BlockSpec((T, HD), lambda h, i: (0, h)), pl.BlockSpec((T, HD), lambda h, i: (0, NH + h)),
                  pl.BlockSpec((T, HD), lambda h, i: (0, 0))],
        out_specs=qspec, out_shape=jax.ShapeDtypeStruct((T, NH * HD), BF16), compiler_params=_params(2))(qn, qr, kv, kv, kr)


def _mla_attn_bwd(name, qn, qr, kv, kr, do, o, tq=256):
    T = qn.shape[0]
    tq = min(tq, T)

    def body(qn_ref, qr_ref, kn_ref, v_ref, kr_ref, do_ref, o_ref, dqn_ref, dqr_ref, dkn_ref, dv_ref, dkr_ref):
        h, i = pl.program_id(0), pl.program_id(1)

        @pl.when(i == 0)
        def _():
            dkn_ref[...] = jnp.zeros_like(dkn_ref)
            dv_ref[...] = jnp.zeros_like(dv_ref)

        @pl.when((i == 0) & (h == 0))
        def _():
            dkr_ref[...] = jnp.zeros_like(dkr_ref)

        for blk in range(T // tq):
            @pl.when(i == blk)
            def _(blk=blk):
                do = do_ref[...]
                q2 = jnp.concatenate([qn_ref[...], qr_ref[...]], axis=1)
                parts, inv_l = _att_parts(q2, kn_ref, kr_ref, blk, tq)
                delta = jnp.sum(do.astype(F32) * o_ref[...].astype(F32), axis=1, keepdims=True)
                do_l = (do.astype(F32) * inv_l).astype(BF16)
                dq2 = None
                for keys, k2, e in parts:
                    dp = _dot(do, v_ref[keys, :], "nt")
                    ds = (e * ((dp - delta) * (inv_l * ATT_SCALE))).astype(BF16)
                    dq_part = _dot(ds, k2, "nn")
                    dq2 = dq_part if dq2 is None else dq2 + dq_part
                    dk2 = _dot(ds, q2, "tn")
                    dkn_ref[keys, :] += dk2[:, :HD]
                    dkr_ref[keys, :] += dk2[:, HD:]
                    dv_ref[keys, :] += _dot(e.astype(BF16), do_l, "tn")
                dqn_ref[...] = dq2[:, :HD]
                dqr_ref[...] = dq2[:, HD:]

    qspec = pl.BlockSpec((tq, HD), lambda h, i: (i, h))
    kspec = pl.BlockSpec((T, HD), lambda h, i: (0, h))
    return pl.pallas_call(
        body, name=name, grid=(NH, T // tq),
        in_specs=[qspec, qspec, kspec, pl.BlockSpec((T, HD), lambda h, i: (0, NH + h)),
                  pl.BlockSpec((T, HD), lambda h, i: (0, 0)), qspec, qspec],
        out_specs=[qspec, qspec, kspec, kspec, pl.BlockSpec((T, HD), lambda h, i: (0, 0))],
        out_shape=[jax.ShapeDtypeStruct((T, NH * HD), F32)] * 4 + [jax.ShapeDtypeStruct((T, HD), F32)],
        compiler_params=_params(2))(qn, qr, kv, kv, kr, do, o)


def _mod_rows(mod, layer):
    return [(mod, layer, i, D) for i in range(N_MOD)]


def _local_step(x, target, pos_col, mod, weights_of, P, on_grads):
    cos, sin = _rope_tables(pos_col)
    saved = []
    sh_m, sc_m = _mod_rows(mod, 0)[:2]
    h = _norm_mod_fwd("norm_mix0", x, (P["norm_mix_g"], 0, 0, D), sc_m, sh_m)
    for l in range(DEPTH):
        j = l // 2
        sh_m, sc_m, ga_m, sh_f, sc_f, ga_f = _mod_rows(mod, l)
        s = dict(x0=x)
        W = weights_of(l, h)
        s.update(h=h, W=W)
        if l % 2 == 0:
            proj = _mm(f"gdn_in{j}", h, W["gdn_in"], "nn", tn=GDN_INK // 2)
            qkv = _gdn_conv_fwd(f"gdn_conv{j}", proj, P["gdn_cw"][j])
            g, beta = _gdn_gates_fwd(f"gdn_gates{j}", proj, P["gdn_alog"][j], P["gdn_dtb"][j])
            o, states = _gdn_chunk_fwd(f"gdn_chunk{j}", qkv, g, beta)
            on = _gdn_gated_norm_fwd(f"gdn_gnorm{j}", o, proj, P["gdn_ng"][j])
            y = _mm(f"gdn_out{j}", on, W["gdn_out"], "nn")
            s.update(proj=proj, qkv=qkv, g=g, beta=beta, o=o, states=states, on=on)
        else:
            proj = _mm(f"mla_in{j}", h, W["mla_in"], "nn")
            cqn, ckvn, kr = _mla_pre_fwd(f"mla_pre{j}", proj, cos, sin, P["mla_qg"][j], P["mla_kvg"][j])
            q = _mm(f"mla_uq{j}", cqn, W["mla_uq"], "nn")
            kv = _mm(f"mla_ukv{j}", ckvn, W["mla_ukv"], "nn", out_dtype=BF16)
            qn, qr = _mla_q_fwd(f"mla_q{j}", q, cos, sin)
            o = _mla_attn_fwd(f"mla_attn{j}", qn, qr, kv, kr)
            y = _mm(f"mla_out{j}", o, W["mla_out"], "nn")
            s.update(proj=proj, cqn=cqn, ckvn=ckvn, kr=kr, kv=kv, qn=qn, qr=qr, o=o)
        s["y"] = y
        x, h2 = _residual_norm_fwd(f"res_mix{l}", x, y, ga_m, (P["norm_ffn_g"], l, 0, D), sc_f, sh_f)
        s["x1"] = x
        fa, fb, sw = _ffn_up(f"ffn_up{l}", h2, W["ffn_g"], W["ffn_u"], 0)
        yf = _ffn_down(f"ffn_down{l}", sw, W["ffn_d"], 0)
        if l + 1 < DEPTH:
            sh_n, sc_n = _mod_rows(mod, l + 1)[:2]
            x, h = _residual_norm_fwd(f"res_ffn{l}", x, yf, ga_f, (P["norm_mix_g"], l + 1, 0, D), sc_n, sh_n)
        else:
            x = _residual_fwd(f"res_ffn{l}", x, yf, ga_f)
        s.update(h2=h2, fa=fa, fb=fb, sw=sw, yf=yf)
        saved.append(s)

    dx, loss, d_final = _loss_head(x, target, P["final_g"])
    gP = dict(loss=loss, final_g=d_final, norm_mix_g=[None] * DEPTH, norm_ffn_g=[None] * DEPTH,
              gdn_cw=[None] * 2, gdn_alog=[None] * 2, gdn_dtb=[None] * 2, gdn_ng=[None] * 2,
              mla_qg=[None] * 2, mla_kvg=[None] * 2)
    dmod = [None] * DEPTH
    dyf, d_ga_f = _residual_bwd(f"res_ffn_b{DEPTH - 1}", dx, saved[-1]["yf"], _mod_rows(mod, DEPTH - 1)[5])
    for l in reversed(range(DEPTH)):
        j = l // 2
        s = saved[l]
        W = s["W"]
        sh_m, sc_m, ga_m, sh_f, sc_f, ga_f = _mod_rows(mod, l)
        da, db = _ffn_down_bwd(f"ffn_down_dx{l}", dyf, W["ffn_d"], s["fa"], s["fb"], 0)
        g_down = _ffn_down_dw(f"ffn_down_dw{l}", s["sw"], dyf)
        g_gate, g_up = _ffn_up_dw(f"ffn_up_dw{l}", s["h2"], da, db)
        on_grads(l, "ffn", dict(ffn_w_gate=g_gate, ffn_w_up=g_up, ffn_w_down=g_down))
        dh2 = _ffn_up_dx(f"ffn_up_dx{l}", da, db, W["ffn_g"], W["ffn_u"], 0)
        dx, dy, d_sh_f, d_sc_f, gP["norm_ffn_g"][l], d_ga_m = _norm_residual_bwd(
            f"norm_ffn_b{l}", dh2, s["x1"], dx, (P["norm_ffn_g"], l, 0, D), sc_f, s["y"], ga_m)
        if l % 2 == 0:
            don = _mm(f"gdn_out_dx{j}", dy, W["gdn_out"], "nt")
            g_out = _mm(f"gdn_out_dw{j}", s["on"], dy, "tn", out_dtype=BF16)
            do, dgate, gP["gdn_ng"][j] = _gdn_gated_norm_bwd(f"gdn_gnorm_b{j}", don, s["o"], s["proj"], P["gdn_ng"][j])
            dqkv, dg_h, db_h = _gdn_chunk_bwd(f"gdn_chunk_b{j}", s["qkv"], s["g"], s["beta"], s["states"], do)
            dab_, gP["gdn_alog"][j], gP["gdn_dtb"][j] = _gdn_gates_bwd(f"gdn_gates_b{j}", s["proj"], dg_h, db_h,
                                                                        P["gdn_alog"][j], P["gdn_dtb"][j])
            dpre, gP["gdn_cw"][j] = _gdn_conv_bwd(f"gdn_conv_b{j}", s["proj"], P["gdn_cw"][j], dqkv)
            dproj = jnp.concatenate([dpre, dgate, dab_], axis=1)
            g_in = _mm(f"gdn_in_dw{j}", s["h"], dproj, "tn", out_dtype=BF16, tn=GDN_INK // 2)
            on_grads(l, "mix", dict(gdn_w_in=_uncols(_gdn_in_from_kernel(g_in)), gdn_w_out=_unrows(g_out)))
            dh = _mm(f"gdn_in_dx{j}", dproj, W["gdn_in"], "nt")
        else:
            do = _mm(f"mla_out_dx{j}", dy, W["mla_out"], "nt", out_dtype=BF16)
            g_out = _mm(f"mla_out_dw{j}", s["o"], dy, "tn", out_dtype=BF16)
            dqn, dqr, dkn, dv, dkr = _mla_attn_bwd(f"mla_attn_b{j}", s["qn"], s["qr"], s["kv"], s["kr"], do, s["o"])
            dq = _mla_q_bwd(f"mla_q_b{j}", dqn, dqr, cos, sin)
            dkv = jnp.concatenate([dkn, dv], axis=1)
            g_uq = _mm(f"mla_uq_dw{j}", s["cqn"], dq, "tn", out_dtype=BF16)
            dcqn = _mm(f"mla_uq_dx{j}", dq, W["mla_uq"], "nt")
            g_ukv = _mm(f"mla_ukv_dw{j}", s["ckvn"], dkv, "tn", out_dtype=BF16)
            dckvn = _mm(f"mla_ukv_dx{j}", dkv, W["mla_ukv"], "nt")
            dproj, gP["mla_qg"][j], gP["mla_kvg"][j] = _mla_pre_bwd(f"mla_pre_b{j}", s["proj"], dcqn, dckvn, dkr, cos, sin,
                                                                     P["mla_qg"][j], P["mla_kvg"][j])
            g_in = _mm(f"mla_in_dw{j}", s["h"], dproj, "tn", out_dtype=BF16)
            on_grads(l, "mix", dict(mla_w_in=_unrows(g_in[:, :Q_RANK + KV_RANK + ROPE]), mla_w_uq=_uncols(_mla_uq_from_kernel(g_uq)),
                                    mla_w_ukv=_uncols(_mla_ukv_from_kernel(g_ukv)), mla_w_out=_unrows(g_out)))
            dh = _mm(f"mla_in_dx{j}", dproj, W["mla_in"], "nt")
        if l > 0:
            dx, dyf_prev, d_sh_m, d_sc_m, gP["norm_mix_g"][l], d_ga_f_prev = _norm_residual_bwd(
                f"norm_mix_b{l}", dh, s["x0"], dx, (P["norm_mix_g"], l, 0, D), sc_m, saved[l - 1]["yf"], _mod_rows(mod, l - 1)[5])
        else:
            dx, d_sh_m, d_sc_m, gP["norm_mix_g"][l] = _norm_mod_bwd(f"norm_mix_b{l}", dh, s["x0"], dx,
                                                                     (P["norm_mix_g"], l, 0, D), sc_m)
        dmod[l] = jnp.concatenate([d_sh_m, d_sc_m, d_ga_m, d_sh_f, d_sc_f, d_ga_f], axis=1)
        if l > 0:
            dyf, d_ga_f = dyf_prev, d_ga_f_prev
    return dx, jnp.concatenate(dmod, axis=0), gP


def _pad_cols(a, width):
    return jnp.pad(a, ((0, 0), (0, width - a.shape[1])))


def _gdn_in_to_kernel(w):
    m = GDN_QKV + NH * HD
    return jnp.concatenate([w[:, :m], _pad_cols(w[:, m:m + NH], HD), _pad_cols(w[:, m + NH:], HD)], axis=1)


def _gdn_in_from_kernel(g):
    m = GDN_QKV + NH * HD
    return jnp.concatenate([g[:, :m], g[:, m:m + NH], g[:, m + HD:m + HD + NH]], axis=1)


def _mla_uq_to_kernel(w):
    w3 = w.reshape(Q_RANK, NH, HD + ROPE)
    rope = jnp.pad(w3[:, :, HD:], ((0, 0), (0, 0), (0, HD - ROPE)))
    return jnp.concatenate([w3[:, :, :HD].reshape(Q_RANK, NH * HD), rope.reshape(Q_RANK, NH * HD)], axis=1)


def _mla_uq_from_kernel(g):
    gn = g[:, :NH * HD].reshape(Q_RANK, NH, HD)
    gr = g[:, NH * HD:].reshape(Q_RANK, NH, HD)[:, :, :ROPE]
    return jnp.concatenate([gn, gr], axis=2).reshape(Q_RANK, NH * (HD + ROPE))


def _mla_ukv_to_kernel(w):
    w3 = w.reshape(KV_RANK, NH, 2 * HD)
    return jnp.concatenate([w3[:, :, :HD].reshape(KV_RANK, NH * HD), w3[:, :, HD:].reshape(KV_RANK, NH * HD)], axis=1)


def _mla_ukv_from_kernel(g):
    gk = g[:, :NH * HD].reshape(KV_RANK, NH, HD)
    gv = g[:, NH * HD:].reshape(KV_RANK, NH, HD)
    return jnp.concatenate([gk, gv], axis=2).reshape(KV_RANK, NH * 2 * HD)


def _cols(t):
    return jnp.moveaxis(t, 0, 1).reshape(t.shape[1], -1)


def _uncols(g):
    return jnp.moveaxis(g.reshape(g.shape[0], 4, -1), 1, 0)


def _rows(t):
    return t.reshape(-1, t.shape[2])


def _unrows(g):
    return g.reshape(4, -1, g.shape[1])


def _layer_weights(layer):
    mixer = ("gdn_w_in", "gdn_w_out") if layer % 2 == 0 else ("mla_w_in", "mla_w_uq", "mla_w_ukv", "mla_w_out")
    return [(n, layer // 2) for n in mixer] + [(n, layer) for n in ("ffn_w_gate", "ffn_w_up", "ffn_w_down")]


def _weights_to_kernel(layer, g):
    out = dict(ffn_g=g["ffn_w_gate"], ffn_u=g["ffn_w_up"], ffn_d=g["ffn_w_down"])
    if layer % 2 == 0:
        out.update(gdn_in=_gdn_in_to_kernel(_cols(g["gdn_w_in"])), gdn_out=_rows(g["gdn_w_out"]))
    else:
        out.update(mla_in=_pad_cols(_rows(g["mla_w_in"]), MLA_INK), mla_uq=_mla_uq_to_kernel(_cols(g["mla_w_uq"])),
                   mla_ukv=_mla_ukv_to_kernel(_cols(g["mla_w_ukv"])), mla_out=_rows(g["mla_w_out"]))
    return out


def _small_to_kernel(norm_mix_g, norm_ffn_g, final_norm_g, gdn_conv_w, gdn_a_log, gdn_dt_bias, gdn_norm_g, q_norm_g, kv_norm_g):
    return dict(
        norm_mix_g=norm_mix_g, norm_ffn_g=norm_ffn_g, final_g=final_norm_g.reshape(1, D),
        gdn_cw=[jnp.transpose(gdn_conv_w[j]) for j in range(2)],
        gdn_alog=[_pad_cols(gdn_a_log[j:j + 1], HD) for j in range(2)],
        gdn_dtb=[_pad_cols(gdn_dt_bias[j:j + 1], HD) for j in range(2)],
        gdn_ng=[gdn_norm_g[j:j + 1] for j in range(2)],
        mla_qg=[q_norm_g[j:j + 1] for j in range(2)],
        mla_kvg=[kv_norm_g[j:j + 1] for j in range(2)],
    )


_CHIP_FLIPS = ((1, 0), (0, 1), (1, 1))
_ANY = pl.BlockSpec(memory_space=pl.ANY)


def _me():
    return lax.axis_index("x"), lax.axis_index("y"), lax.axis_index("c")


def _chip_peer(dx, dy):
    x, y, c = _me()
    return ((1 - x) if dx else x, (1 - y) if dy else y, c)


def _rcopy(src, dst, send_sem, recv_sem, to):
    return pltpu.make_async_remote_copy(src_ref=src, dst_ref=dst, send_sem=send_sem, recv_sem=recv_sem,
                                        device_id=to, device_id_type=MESH)


def _allgather4(name, a, halves=False):
    R, C = a.shape
    rh = R // 2 if halves else R

    def body(a_ref, out_ref, send_sems, recv_sems, local_sem):
        x, y, c = _me()
        me = 2 * x + y
        src = a_ref.at[pl.ds(c * rh, rh)] if halves else a_ref
        local = pltpu.make_async_copy(src, out_ref.at[me], local_sem)
        local.start()
        sends = []
        for k, (dx, dy) in enumerate(_CHIP_FLIPS):
            cp = _rcopy(src, out_ref.at[me], send_sems.at[k], recv_sems.at[k], _chip_peer(dx, dy))
            cp.start()
            sends.append(cp)
        for k, (dx, dy) in enumerate(_CHIP_FLIPS):
            px, py, _ = _chip_peer(dx, dy)
            _rcopy(src, out_ref.at[2 * px + py], send_sems.at[k], recv_sems.at[k], _chip_peer(dx, dy)).wait_recv()
        for cp in sends:
            cp.wait_send()
        local.wait()

    return pl.pallas_call(
        body, name=name, in_specs=[_ANY], out_specs=_ANY, out_shape=jax.ShapeDtypeStruct((4, rh, C), a.dtype),
        scratch_shapes=[pltpu.SemaphoreType.DMA((3,)), pltpu.SemaphoreType.DMA((3,)), pltpu.SemaphoreType.DMA(())])(a)


_NCH = 4


def _dma_sems(*counts):
    return [pltpu.SemaphoreType.DMA((n,)) for n in counts]


def _slot_tile(rows, cap=512):
    best = rows
    for tr in range(16, min(rows, cap) + 1, 16):
        if rows % tr == 0:
            best = tr
    return best


def _cast_into_slot(name, a, chip, row0, rows):
    C = a.shape[1]
    tr = _slot_tile(rows)
    assert row0 % tr == 0
    first = row0 // tr

    def body(c_ref, a_ref, o_ref):
        o_ref[0] = a_ref[...].astype(o_ref.dtype)

    grid_spec = pltpu.PrefetchScalarGridSpec(
        num_scalar_prefetch=1, grid=(rows // tr,), in_specs=[pl.BlockSpec((tr, C), lambda i, c_ref: (first + i, 0))],
        out_specs=pl.BlockSpec((1, tr, C), lambda i, c_ref: (c_ref[0], i, 0)))
    return pl.pallas_call(body, name=name, grid_spec=grid_spec, out_shape=jax.ShapeDtypeStruct((4, rows, C), BF16),
                          compiler_params=_params(1))(chip, a)


def _chunks(rows, align):
    for nch in (_NCH, 2):
        if rows % (nch * align) == 0:
            return nch
    return 1


def _gather_exchange(out, ici_s, ici_r, d2d_s, d2d_r):
    n = len(out)
    x, y, c = _me()
    me = 2 * x + y
    sib = (x, y, 1 - c)
    peers = [_chip_peer(dx, dy) for dx, dy in _CHIP_FLIPS]
    for t in range(n):
        h = out[t].shape[1] // 2
        nch = _chunks(h, 16)
        ch = h // nch
        for k, peer in enumerate(peers):
            for i in range(nch):
                blk = out[t].at[me, pl.ds(c * h + i * ch, ch)]
                _rcopy(blk, blk, ici_s.at[3 * t + k], ici_r.at[3 * t + k], peer).start()
    for t in range(n):
        h = out[t].shape[1] // 2
        nch = _chunks(h, 16)
        ch = h // nch
        for k, peer in enumerate(peers):
            pchip = 2 * peer[0] + peer[1]
            got = out[t].at[pchip, pl.ds(c * h, h)]
            _rcopy(got, got, ici_s.at[3 * t + k], ici_r.at[3 * t + k], peer).wait_recv()
            for i in range(nch):
                blk = out[t].at[pchip, pl.ds(c * h + i * ch, ch)]
                _rcopy(blk, blk, d2d_s.at[3 * t + k], d2d_r.at[3 * t + k], sib).start()
    for t in range(n):
        h = out[t].shape[1] // 2
        for k, peer in enumerate(peers):
            pchip = 2 * peer[0] + peer[1]
            other = out[t].at[pchip, pl.ds((1 - c) * h, h)]
            _rcopy(other, other, d2d_s.at[3 * t + k], d2d_r.at[3 * t + k], sib).wait_recv()
            _rcopy(other, other, ici_s.at[3 * t + k], ici_r.at[3 * t + k], peer).wait_send()
            _rcopy(other, other, d2d_s.at[3 * t + k], d2d_r.at[3 * t + k], sib).wait_send()


def _gather_weights(name, bufs):
    n = len(bufs)

    def body(*refs):
        _gather_exchange(refs[n:2 * n], *refs[2 * n:])

    return pl.pallas_call(
        body, name=name, in_specs=[_ANY] * n, out_specs=[_ANY] * n,
        out_shape=[jax.ShapeDtypeStruct(s.shape, s.dtype) for s in bufs],
        input_output_aliases={t: t for t in range(n)},
        scratch_shapes=_dma_sems(3 * n, 3 * n, 3 * n, 3 * n))(*bufs)


def _gather_weights_async(name, collective_id, bufs):
    n = len(bufs)
    refs = [jax.new_ref(b, memory_space=pltpu.MemorySpace.HBM) for b in bufs]

    @pl.kernel(mesh=plsc.ScalarSubcoreMesh(axis_name="sequencer", num_cores=1), name=name,
               scratch_types=tuple(_dma_sems(3 * n, 3 * n, 3 * n, 3 * n)),
               compiler_params=pltpu.CompilerParams(collective_id=collective_id))
    def launch(ici_s, ici_r, d2d_s, d2d_r):
        x, y, c = _me()
        barrier = pltpu.get_barrier_semaphore()
        for peer in [_chip_peer(dx, dy) for dx, dy in _CHIP_FLIPS] + [(x, y, 1 - c)]:
            pl.semaphore_signal(barrier, inc=1, device_id=peer, device_id_type=MESH)
        pl.semaphore_wait(barrier, 4)
        _gather_exchange(refs, ici_s, ici_r, d2d_s, d2d_r)

    launch()
    return [r[...] for r in refs]


def _rs_split(name, grads):
    n = len(grads)

    def body(*refs):
        g, out = refs[:n], refs[n:2 * n]
        send, recv = refs[2 * n:]
        x, y, c = _me()
        sib = (x, y, 1 - c)
        for t in range(n):
            h = g[t].shape[1] // 2
            for d in range(4):
                _rcopy(g[t].at[d, pl.ds((1 - c) * h, h)], out[t].at[d], send.at[t], recv.at[t], sib).start()
        for t in range(n):
            _rcopy(out[t], out[t], send.at[t], recv.at[t], sib).wait()

    return pl.pallas_call(
        body, name=name, in_specs=[_ANY] * n, out_specs=[_ANY] * n,
        out_shape=[jax.ShapeDtypeStruct((4, s.shape[1] // 2, s.shape[2]), s.dtype) for s in grads],
        scratch_shapes=_dma_sems(n, n))(*grads)


def _pair_add(name, g, theirs, core_chip):
    _, R, C = g.shape
    h = R // 2
    tr = _slot_tile(h)
    nb = h // tr

    def body(s_ref, g_ref, t_ref, p_ref, o_ref):
        val = (g_ref[...].astype(F32) + t_ref[...].astype(F32)).astype(p_ref.dtype)
        p_ref[...] = val

        @pl.when(pl.program_id(1) == s_ref[1])
        def _():
            o_ref[...] = val

    spec = pl.BlockSpec((1, tr, C), lambda i, d, s_ref: (d, i, 0))
    grid_spec = pltpu.PrefetchScalarGridSpec(
        num_scalar_prefetch=1, grid=(nb, 4),
        in_specs=[pl.BlockSpec((1, tr, C), lambda i, d, s_ref: (d, s_ref[0] * nb + i, 0)), spec],
        out_specs=[spec, pl.BlockSpec((1, tr, C), lambda i, d, s_ref: (s_ref[1], i, 0))])
    half = jax.ShapeDtypeStruct((4, h, C), BF16)
    return pl.pallas_call(body, name=name, grid_spec=grid_spec, out_shape=[half, half],
                          compiler_params=_params(2))(core_chip, g, theirs)


def _rs_alltoall_async(name, collective_id, parts, bufs):
    n = len(parts)
    p = [jax.new_ref(a, memory_space=pltpu.MemorySpace.HBM) for a in parts]
    out = [jax.new_ref(b, memory_space=pltpu.MemorySpace.HBM) for b in bufs]

    @pl.kernel(mesh=plsc.ScalarSubcoreMesh(axis_name="sequencer", num_cores=1), name=name,
               scratch_types=tuple(_dma_sems(3 * n, 3 * n)),
               compiler_params=pltpu.CompilerParams(collective_id=collective_id))
    def launch(send, recv):
        barrier = pltpu.get_barrier_semaphore()
        for peer in [_chip_peer(dx, dy) for dx, dy in _CHIP_FLIPS]:
            pl.semaphore_signal(barrier, inc=1, device_id=peer, device_id_type=MESH)
        pl.semaphore_wait(barrier, 3)
        _alltoall_exchange(p, out, send, recv)

    launch()
    return [r[...] for r in out]


def _alltoall_exchange(p, out, send, recv):
    x, y, c = _me()
    me = 2 * x + y
    peers = [_chip_peer(dx, dy) for dx, dy in _CHIP_FLIPS]
    for t in range(len(p)):
        h = p[t].shape[1]
        nch = _chunks(h, 16)
        ch = h // nch
        for k, peer in enumerate(peers):
            pchip = 2 * peer[0] + peer[1]
            for i in range(nch):
                rows = pl.ds(i * ch, ch)
                _rcopy(p[t].at[pchip, rows], out[t].at[me, rows], send.at[3 * t + k], recv.at[3 * t + k], peer).start()
    for t in range(len(p)):
        for k, peer in enumerate(peers):
            pchip = 2 * peer[0] + peer[1]
            _rcopy(out[t].at[pchip], out[t].at[pchip], send.at[3 * t + k], recv.at[3 * t + k], peer).wait()


def _rs_swap(name, halves):
    n = len(halves)

    def body(*refs):
        a, out = refs[:n], refs[n:2 * n]
        send, recv = refs[2 * n:]
        x, y, c = _me()
        sib = (x, y, 1 - c)
        for t in range(n):
            ch = a[t].shape[0] // _NCH
            for i in range(_NCH):
                rows = pl.ds(i * ch, ch)
                _rcopy(a[t].at[rows], out[t].at[rows], send.at[t], recv.at[t], sib).start()
        for t in range(n):
            _rcopy(a[t], out[t], send.at[t], recv.at[t], sib).wait()

    return pl.pallas_call(
        body, name=name, in_specs=[_ANY] * n, out_specs=[_ANY] * n,
        out_shape=[jax.ShapeDtypeStruct(s.shape, s.dtype) for s in halves],
        scratch_shapes=_dma_sems(n, n))(*halves)


def _sibling_merge(name, a):
    P_, rh, C = a.shape

    def body(a_ref, out_ref, send_sem, recv_sem, local_sem):
        x, y, c = _me()
        local = pltpu.make_async_copy(a_ref, out_ref.at[:, pl.ds(c * rh, rh)], local_sem)
        local.start()
        cp = _rcopy(a_ref, out_ref.at[:, pl.ds(c * rh, rh)], send_sem, recv_sem, (x, y, 1 - c))
        cp.start()
        cp.wait_send()
        _rcopy(a_ref, out_ref.at[:, pl.ds((1 - c) * rh, rh)], send_sem, recv_sem, (x, y, 1 - c)).wait_recv()
        local.wait()

    return pl.pallas_call(
        body, name=name, in_specs=[_ANY], out_specs=_ANY, out_shape=jax.ShapeDtypeStruct((P_, 2 * rh, C), a.dtype),
        scratch_shapes=[pltpu.SemaphoreType.DMA(()), pltpu.SemaphoreType.DMA(()), pltpu.SemaphoreType.DMA(())])(a)


def _allgather8(name, a):
    g4 = _allgather4(name + "_chips", a)
    both = _sibling_merge(name + "_cores", g4.reshape(1, 4 * a.shape[0], a.shape[1]))
    return jnp.transpose(both.reshape(2, 4, *a.shape), (1, 0, 2, 3)).reshape(8, *a.shape)


def _sum_slots(name, a, out_dtype):
    def fn(a):
        acc = a[0].astype(F32)
        for k in range(1, a.shape[0]):
            acc = acc + a[k].astype(F32)
        return acc
    return _rowwise(name, fn, [a], [], [(a.shape[2], out_dtype)])[0]


def _adamw_math(w, g, m, v):
    m = ADAM_B1 * m + (1.0 - ADAM_B1) * g
    v = ADAM_B2 * v + (1.0 - ADAM_B2) * (g * g)
    m_hat = m / (1.0 - ADAM_B1 ** ADAM_STEP)
    v_hat = v / (1.0 - ADAM_B2 ** ADAM_STEP)
    return -ADAM_LR * (m_hat / (jnp.sqrt(v_hat) + ADAM_EPS) + ADAM_WD * w), m, v


def _adamw_piece(name, w2, m2, v2, mine, theirs, row0, prev, core):
    R, C = w2.shape
    h = mine.shape[0]
    tr = _slot_tile(h, 256)
    nb = h // tr
    assert row0 % tr == 0
    first = row0 // tr

    def body(c_ref, w_ref, m_ref, v_ref, a_ref, b_ref, *rest):
        g_ref, d_ref, nm_ref, nv_ref = rest[-4:]
        g = jnp.where(pl.program_id(0) == c_ref[0], a_ref[...], b_ref[...])
        g_ref[...] = g
        d_ref[...], nm_ref[...], nv_ref[...] = _adamw_math(w_ref[...], g, m_ref[...], v_ref[...])

    full = pl.BlockSpec((tr, C), lambda s, i, c_ref: (first + s * nb + i, 0))
    mine_spec = pl.BlockSpec((tr, C), lambda s, i, c_ref: (jnp.where(s == c_ref[0], i, 0), 0))
    theirs_spec = pl.BlockSpec((tr, C), lambda s, i, c_ref: (jnp.where(s == c_ref[0], 0, i), 0))
    extra = [] if prev is None else list(prev)
    grid_spec = pltpu.PrefetchScalarGridSpec(
        num_scalar_prefetch=1, grid=(2, nb), in_specs=[full, full, full, mine_spec, theirs_spec] + [_ANY] * len(extra),
        out_specs=[full] * 4)
    return pl.pallas_call(
        body, name=name, grid_spec=grid_spec, out_shape=[jax.ShapeDtypeStruct((R, C), F32)] * 4,
        input_output_aliases={6 + k: k for k in range(len(extra))}, compiler_params=_params(2))(core, w2, m2, v2, mine, theirs, *extra)


def _adamw(name, w, g, m, v):
    shape = w.shape
    two_d = (-1, shape[-1]) if w.ndim > 1 else (1, -1)
    w2, g2, m2, v2 = [t.reshape(two_d) for t in (w, g, m, v)]
    rows = w2.shape[0]
    tr = rows
    for cand in (256, 128, 64, 32, 16, 8):
        if rows % cand == 0:
            tr = cand
            break

    c = w2.shape[1]
    outs = _rowwise(name, _adamw_math, [w2, g2, m2, v2], [], [(c, F32)] * 3, tr=tr)
    return [o.reshape(shape) for o in outs]


_WEIGHT_ORDER = ("ada_w", "ada_b", "norm_mix_g", "norm_ffn_g", "gdn_w_in", "gdn_conv_w", "gdn_a_log", "gdn_dt_bias",
                 "gdn_norm_g", "gdn_w_out", "mla_w_in", "mla_q_norm_g", "mla_kv_norm_g", "mla_w_uq", "mla_w_ukv",
                 "mla_w_out", "ffn_w_gate", "ffn_w_up", "ffn_w_down", "final_norm_g")
_BIG = (("gdn_w_in", 2), ("gdn_w_out", 1), ("mla_w_in", 1), ("mla_w_uq", 2), ("mla_w_ukv", 2), ("mla_w_out", 1),
        ("ffn_w_gate", 2), ("ffn_w_up", 2), ("ffn_w_down", 1))
_SMALL_SHARDED = (("gdn_conv_w", 1), ("mla_q_norm_g", 1), ("mla_kv_norm_g", 1))
_STORED_TRANSPOSED = ("ffn_w_gate", "ffn_w_up")


def _size(shape):
    n = 1
    for s in shape:
        n *= s
    return n


def _pack_rows_each(tensors):
    parts, offs, off = [], [], 0
    for t in tensors:
        flat = t.reshape(-1).astype(F32)
        rows = -(-flat.shape[0] // PACK_W)
        parts.append(jnp.pad(flat, (0, rows * PACK_W - flat.shape[0])).reshape(rows, PACK_W))
        offs.append(off)
        off += rows
    total = -(-off // 16) * 16
    pack = jnp.pad(parts[0], ((offs[0], total - offs[0] - parts[0].shape[0]), (0, 0)))
    for p, o in zip(parts[1:], offs[1:]):
        pack = pack + jnp.pad(p, ((o, total - o - p.shape[0]), (0, 0)))
    return pack, offs


def _unpack_rows_each(pack, shapes):
    lead = pack.shape[:-2]
    out, off = [], 0
    for shp in shapes:
        n = _size(shp)
        rows = -(-n // PACK_W)
        out.append(pack[..., off:off + rows, :].reshape(*lead, -1)[..., :n].reshape(*lead, *shp))
        off += rows
    return out


def _merge_chips(stacked, axis):
    moved = jnp.moveaxis(stacked, 0, axis)
    shp = list(moved.shape)
    return moved.reshape(shp[:axis] + [shp[axis] * shp[axis + 1]] + shp[axis + 2:])


def _my_shard(full, axis, chip):
    n = full.shape[axis] // 4
    return lax.dynamic_slice_in_dim(full, chip * n, n, axis)


def kernel(x, c, positions, ada_w, ada_b, norm_mix_g, norm_ffn_g, gdn_w_in, gdn_conv_w, gdn_a_log, gdn_dt_bias, gdn_norm_g, gdn_w_out, mla_w_in, mla_q_norm_g, mla_kv_norm_g, mla_w_uq, mla_w_ukv, mla_w_out, ffn_w_gate, ffn_w_up, ffn_w_down, final_norm_g, loss_target, m_ada_w, m_ada_b, m_norm_mix_g, m_norm_ffn_g, m_gdn_w_in, m_gdn_conv_w, m_gdn_a_log, m_gdn_dt_bias, m_gdn_norm_g, m_gdn_w_out, m_mla_w_in, m_mla_q_norm_g, m_mla_kv_norm_g, m_mla_w_uq, m_mla_w_ukv, m_mla_w_out, m_ffn_w_gate, m_ffn_w_up, m_ffn_w_down, m_final_norm_g, v_ada_w, v_ada_b, v_norm_mix_g, v_norm_ffn_g, v_gdn_w_in, v_gdn_conv_w, v_gdn_a_log, v_gdn_dt_bias, v_gdn_norm_g, v_gdn_w_out, v_mla_w_in, v_mla_q_norm_g, v_mla_kv_norm_g, v_mla_w_uq, v_mla_w_ukv, v_mla_w_out, v_ffn_w_gate, v_ffn_w_up, v_ffn_w_down, v_final_norm_g):
    w = dict(ada_w=ada_w, ada_b=ada_b, norm_mix_g=norm_mix_g, norm_ffn_g=norm_ffn_g, gdn_w_in=gdn_w_in, gdn_conv_w=gdn_conv_w,
             gdn_a_log=gdn_a_log, gdn_dt_bias=gdn_dt_bias, gdn_norm_g=gdn_norm_g, gdn_w_out=gdn_w_out, mla_w_in=mla_w_in,
             mla_q_norm_g=mla_q_norm_g, mla_kv_norm_g=mla_kv_norm_g, mla_w_uq=mla_w_uq, mla_w_ukv=mla_w_ukv,
             mla_w_out=mla_w_out, ffn_w_gate=ffn_w_gate, ffn_w_up=ffn_w_up, ffn_w_down=ffn_w_down, final_norm_g=final_norm_g)
    m = dict(ada_w=m_ada_w, ada_b=m_ada_b, norm_mix_g=m_norm_mix_g, norm_ffn_g=m_norm_ffn_g, gdn_w_in=m_gdn_w_in,
             gdn_conv_w=m_gdn_conv_w, gdn_a_log=m_gdn_a_log, gdn_dt_bias=m_gdn_dt_bias, gdn_norm_g=m_gdn_norm_g,
             gdn_w_out=m_gdn_w_out, mla_w_in=m_mla_w_in, mla_q_norm_g=m_mla_q_norm_g, mla_kv_norm_g=m_mla_kv_norm_g,
             mla_w_uq=m_mla_w_uq, mla_w_ukv=m_mla_w_ukv, mla_w_out=m_mla_w_out, ffn_w_gate=m_ffn_w_gate,
             ffn_w_up=m_ffn_w_up, ffn_w_down=m_ffn_w_down, final_norm_g=m_final_norm_g)
    v = dict(ada_w=v_ada_w, ada_b=v_ada_b, norm_mix_g=v_norm_mix_g, norm_ffn_g=v_norm_ffn_g, gdn_w_in=v_gdn_w_in,
             gdn_conv_w=v_gdn_conv_w, gdn_a_log=v_gdn_a_log, gdn_dt_bias=v_gdn_dt_bias, gdn_norm_g=v_gdn_norm_g,
             gdn_w_out=v_gdn_w_out, mla_w_in=v_mla_w_in, mla_q_norm_g=v_mla_q_norm_g, mla_kv_norm_g=v_mla_kv_norm_g,
             mla_w_uq=v_mla_w_uq, mla_w_ukv=v_mla_w_ukv, mla_w_out=v_mla_w_out, ffn_w_gate=v_ffn_w_gate,
             ffn_w_up=v_ffn_w_up, ffn_w_down=v_ffn_w_down, final_norm_g=v_final_norm_g)
    T = x.shape[1]
    ix, iy, ic = _me()
    chip = 2 * ix + iy
    seq = 2 * chip + ic
    n_dev = 8

    small_shapes = [w[n].shape for n, _ in _SMALL_SHARDED] + [c.shape]
    pack0, _ = _pack_rows_each([w[n] for n, _ in _SMALL_SHARDED] + [c])
    got0 = _unpack_rows_each(_allgather8("gather_small", pack0), small_shapes)
    small_full = {n: _merge_chips(g[0::2], ax) for (n, ax), g in zip(_SMALL_SHARDED, got0)}
    c_all = got0[-1].reshape(n_dev, D)

    big = [n for n, _ in _BIG]
    chip_arr = chip.astype(jnp.int32).reshape(1)

    def stored(n, t):
        return jnp.swapaxes(t, 1, 2) if n in _STORED_TRANSPOSED else t

    ws, ms, vs = [{n: stored(n, d[n]) for n in big} for d in (w, m, v)]
    two_d = lambda t: t.reshape(-1, t.shape[-1])

    gathered = []
    for l in range(DEPTH):
        names = _layer_weights(l)
        bufs = [_cast_into_slot(f"to_bf16_{n}{l}", two_d(ws[n]), chip_arr, j * ws[n].shape[1], ws[n].shape[1]) for n, j in names]
        filled = _gather_weights("gather_weights0", bufs) if l == 0 else _gather_weights_async(f"gather_weights{l}", l, bufs)
        gathered.append({n: b for (n, _), b in zip(names, filled)})

    def weights_of(l, h):
        return _weights_to_kernel(l, gathered[l])

    P = _small_to_kernel(norm_mix_g, norm_ffn_g, final_norm_g, small_full["gdn_conv_w"], gdn_a_log, gdn_dt_bias,
                         gdn_norm_g, small_full["mla_q_norm_g"], small_full["mla_kv_norm_g"])

    c16 = jnp.pad(c_all, ((0, 16 - n_dev), (0, 0)))
    ca = _rowwise("cond_silu", lambda t: t * _sig(t), [c16], [], [(D, BF16)])[0]
    n_ada = ada_w.shape[2]
    mods = jnp.concatenate([_mm(f"ada_fwd{l}", ca, ada_w[l], "nn") for l in range(DEPTH)], axis=0)
    mods_all = _allgather4("gather_mod", mods).reshape(4, DEPTH, 16, n_ada)
    mod_mm = jnp.transpose(lax.dynamic_index_in_dim(mods_all, seq, axis=2, keepdims=False), (1, 0, 2)).reshape(DEPTH, 4 * n_ada)
    mod = _rowwise("mod_bias", lambda a, b: a + b, [mod_mm, ada_b], [], [(4 * n_ada, F32)])[0]

    core_chip = jnp.stack([ic, chip]).astype(jnp.int32)
    pending, in_flight = {}, []

    def reduce_group(layer, part, pieces):
        pending.update({(n, layer if n.startswith("ffn_") else layer // 2): g for n, g in pieces.items()})
        if part == "ffn" and layer > 0:
            return
        keys = list(pending)
        glist = [pending.pop(k) for k in keys]
        tag = f"{layer}{part}"
        theirs = _rs_split("grads_cores_" + tag, glist)
        both = [_pair_add(f"grads_pair_{n}{l}", g, t, core_chip) for (n, l), g, t in zip(keys, glist, theirs)]
        swapped = _rs_alltoall_async("grads_chips_" + tag, DEPTH + 1 + len(in_flight), [p for p, _ in both], [o for _, o in both])
        in_flight.append((tag, keys, swapped))

    dx, dmod, gP = _local_step(x.reshape(T, D), loss_target.reshape(T, D), positions.reshape(T, 1), mod, weights_of, P, reduce_group)

    partials = [dmod, jnp.concatenate(gP["norm_mix_g"]), jnp.concatenate(gP["norm_ffn_g"]), gP["final_g"],
                jnp.stack([jnp.transpose(g) for g in gP["gdn_cw"]]), jnp.concatenate(gP["gdn_alog"])[:, :NH],
                jnp.concatenate(gP["gdn_dtb"])[:, :NH], jnp.concatenate(gP["gdn_ng"]), jnp.concatenate(gP["mla_qg"]),
                jnp.concatenate(gP["mla_kvg"]), gP["loss"][:, :1]]
    part_shapes = [p.shape for p in partials]
    ppack, _ = _pack_rows_each(partials)
    pall = _allgather8("gather_partials", ppack)
    psum = _sum_slots("sum_partials", pall, F32)
    (g_ada_b, g_norm_mix, g_norm_ffn, g_final, g_conv_full, g_alog, g_dtb, g_gdn_ng, g_qg_full, g_kvg_full,
     loss_sum) = _unpack_rows_each(psum, part_shapes)
    dmod_all = _unpack_rows_each(pall, part_shapes[:1])[0]

    grads = dict(ada_b=g_ada_b, norm_mix_g=g_norm_mix, norm_ffn_g=g_norm_ffn, final_norm_g=g_final.reshape(D),
                 gdn_conv_w=_my_shard(g_conv_full, 1, chip), gdn_a_log=g_alog, gdn_dt_bias=g_dtb, gdn_norm_g=g_gdn_ng,
                 mla_q_norm_g=_my_shard(g_qg_full, 1, chip), mla_kv_norm_g=_my_shard(g_kvg_full, 1, chip))

    ca_t = jnp.zeros((D, LANES), BF16).at[:, :16].set(jnp.transpose(ca))
    dm_mine = lax.dynamic_slice_in_dim(dmod_all, chip * n_ada, n_ada, axis=2)
    grads["ada_w"] = jnp.stack([
        _mm(f"ada_bwd{l}", ca_t, jnp.pad(dm_mine[:, l], ((0, LANES - n_dev), (0, 0))), "nn") for l in range(DEPTH)])

    delta, new_m, new_v = {}, {}, {}
    results = {}
    keys = [k for _, ks, _ in in_flight for k in ks]
    halves = [_sum_slots(f"grads_sum_{n}{l}", s, F32) for _, ks, sw in in_flight for (n, l), s in zip(ks, sw)]
    others = _rs_swap("grads_swap", halves)
    for (n, l), mine, theirs in zip(keys, halves, others):
        results[n] = _adamw_piece(f"adamw_{n}{l}", two_d(ws[n]), two_d(ms[n]), two_d(vs[n]), mine, theirs,
                                  l * ws[n].shape[1], results.get(n), core_chip[:1])
    for n in big:
        grads[n], delta[n], new_m[n], new_v[n] = [stored(n, t.reshape(ws[n].shape)) for t in results[n]]
    delta["ada_w"], new_m["ada_w"], new_v["ada_w"] = _adamw("adamw_ada_w", ada_w, grads["ada_w"], m_ada_w, v_ada_w)
    for n in [n for n in _WEIGHT_ORDER if n not in delta]:
        delta[n], new_m[n], new_v[n] = _adamw("adamw_" + n, w[n], grads[n], m[n], v[n])

    loss = loss_sum.reshape(())
    return (loss, dx.reshape(1, T, D), *[grads[n] for n in _WEIGHT_ORDER], *[delta[n] for n in _WEIGHT_ORDER],
            *[new_m[n] for n in _WEIGHT_ORDER], *[new_v[n] for n in _WEIGHT_ORDER])
```
